```python
import jax, jax.numpy as jnp
from jax import lax
import numpy as np

D_MODEL = 1024
BATCH = 8
SEQ = 8192
DEPTH = 1

CHUNK = 64
Q_BLOCK = 128
EPS = 1e-6
D_FF = 2816
N_MOD = 9
CONV_WIDTH = 512
CONV_GROUPS = 8
CONV_K = 3
MLA_HEADS = 4
QK_NOPE = 128
QK_ROPE = 64
V_HEAD = 128
Q_LORA = 384
KV_LORA = 256
ROPE_THETA = 10000.0
MLA_WIDTH = MLA_HEADS * V_HEAD
MIX_WIDTH = CONV_WIDTH + MLA_WIDTH
IN_COLS = 3 * CONV_WIDTH + Q_LORA + KV_LORA + QK_ROPE

kernel_name = "hymba_conv_mla_macaron_adaln_block"


def rmsnorm(x, g):
    xf = x.astype(jnp.float32)
    y = xf * lax.rsqrt(jnp.mean(xf * xf, axis=-1, keepdims=True) + EPS)
    return (y * g.astype(jnp.float32)).astype(x.dtype)


def group_rmsnorm(y, g, n_groups):
    b, s, w = y.shape
    yf = y.astype(jnp.float32).reshape(b, s, n_groups, w // n_groups)
    yf = yf * lax.rsqrt(jnp.mean(yf * yf, axis=-1, keepdims=True) + EPS)
    return (yf.reshape(b, s, w) * g.astype(jnp.float32)).astype(y.dtype)


def modulate(h, shift, scale):
    return h * (1.0 + scale[:, None, :]) + shift[:, None, :]


def swiglu(h, w1, w3, w2):
    return (jax.nn.silu(h @ w1) * (h @ w3)) @ w2


def rope(x, cos, sin):
    half = x.shape[-1] // 2
    x1, x2 = x[..., :half], x[..., half:]
    return jnp.concatenate([x1 * cos - x2 * sin, x1 * sin + x2 * cos], axis=-1)


def short_conv_mixer(xb, xc, xu, conv_w):
    u = xc * xu
    s = u.shape[1]
    up = jnp.pad(u, ((0, 0), (CONV_K - 1, 0), (0, 0)))
    y = conv_w[0] * up[:, 0:s]
    for k in range(1, CONV_K):
        y = y + conv_w[k] * up[:, k:k + s]
    return xb * y


def mla(cq, ckv, kr, positions, q_norm_g, w_uq, kv_norm_g, w_ukv):
    b, s, _ = cq.shape
    q = (rmsnorm(cq, q_norm_g) @ w_uq).reshape(b, s, MLA_HEADS, QK_NOPE + QK_ROPE)
    q_nope, q_rope = q[..., :QK_NOPE], q[..., QK_NOPE:]
    kv = (rmsnorm(ckv, kv_norm_g) @ w_ukv).reshape(b, s, MLA_HEADS, QK_NOPE + V_HEAD)
    k_nope, v = kv[..., :QK_NOPE], kv[..., QK_NOPE:]

    inv_freq = ROPE_THETA ** (-jnp.arange(0, QK_ROPE, 2, dtype=jnp.float32) / QK_ROPE)
    ang = positions.astype(jnp.float32)[..., None] * inv_freq
    cos = jnp.cos(ang).astype(cq.dtype)
    sin = jnp.sin(ang).astype(cq.dtype)
    q_rope = rope(q_rope, cos[:, :, None, :], sin[:, :, None, :])
    k_rope = rope(kr, cos, sin)

    scale = (QK_NOPE + QK_ROPE) ** -0.5
    nblk = s // Q_BLOCK
    qn_b = q_nope.reshape(b, nblk, Q_BLOCK, MLA_HEADS, QK_NOPE).transpose(1, 0, 2, 3, 4)
    qr_b = q_rope.reshape(b, nblk, Q_BLOCK, MLA_HEADS, QK_ROPE).transpose(1, 0, 2, 3, 4)
    k_chunk = jnp.arange(s) // CHUNK

    def block(args):
        i, qn, qr = args
        sc = (jnp.einsum('bqhd,bkhd->bhqk', qn, k_nope)
              + jnp.einsum('bqhd,bkd->bhqk', qr, k_rope)).astype(jnp.float32) * scale
        q_chunk = (i * Q_BLOCK + jnp.arange(Q_BLOCK)) // CHUNK
        mask = k_chunk[None, :] <= q_chunk[:, None]
        sc = jnp.where(mask[None, None], sc, jnp.float32(-1e30))
        p = jax.nn.softmax(sc, axis=-1).astype(v.dtype)
        return jnp.einsum('bhqk,bkhd->bqhd', p, v)

    o = lax.map(block, (jnp.arange(nblk), qn_b, qr_b))
    return o.transpose(1, 0, 2, 3, 4).reshape(b, s, MLA_WIDTH)


def _fwd_setup_inputs(seed: int = 0) -> dict:
    key = jax.random.key(seed)
    ks = iter(jax.random.split(key, 32))
    f32 = jnp.float32
    L = DEPTH

    def nrm(shape, fan_in, scale=1.0):
        return jax.random.normal(next(ks), shape, f32) * (scale * fan_in ** -0.5)

    def gain(shape):
        return 1.0 + 0.05 * jax.random.normal(next(ks), shape, f32)

    x = jax.random.normal(next(ks), (BATCH, SEQ, D_MODEL), f32)
    c = jax.random.normal(next(ks), (BATCH, D_MODEL), f32)
    offset = jax.random.randint(next(ks), (BATCH, 1), 0, 4096)
    positions = (offset + jnp.arange(SEQ)[None, :]).astype(jnp.int32)
    return {
        "x": x,
        "c": c,
        "positions": positions,
        "ada_w": nrm((L, D_MODEL, N_MOD * D_MODEL), D_MODEL, 0.5),
        "ada_b": 0.02 * jax.random.normal(next(ks), (L, N_MOD * D_MODEL), f32),
        "norm_ffn1_g": gain((L, D_MODEL)),
        "ffn1_w1": nrm((L, D_MODEL, D_FF), D_MODEL),
        "ffn1_w3": nrm((L, D_MODEL, D_FF), D_MODEL),
        "ffn1_w2": nrm((L, D_FF, D_MODEL), D_FF),
        "norm_mix_g": gain((L, D_MODEL)),
        "w_in": nrm((L, D_MODEL, IN_COLS), D_MODEL),
        "conv_w": nrm((L, CONV_K, CONV_WIDTH), CONV_K),
        "q_norm_g": gain((L, Q_LORA)),
        "w_uq": nrm((L, Q_LORA, MLA_HEADS * (QK_NOPE + QK_ROPE)), Q_LORA),
        "kv_norm_g": gain((L, KV_LORA)),
        "w_ukv": nrm((L, KV_LORA, MLA_HEADS * (QK_NOPE + V_HEAD)), KV_LORA),
        "out_norm_g": gain((L, MIX_WIDTH)),
        "w_out": nrm((L, MIX_WIDTH, D_MODEL), MIX_WIDTH),
        "norm_ffn2_g": gain((L, D_MODEL)),
        "ffn2_w1": nrm((L, D_MODEL, D_FF), D_MODEL),
        "ffn2_w3": nrm((L, D_MODEL, D_FF), D_MODEL),
        "ffn2_w2": nrm((L, D_FF, D_MODEL), D_FF),
        "final_norm_g": gain((D_MODEL,)),
    }


def _fwd_reference(x, c, positions, ada_w, ada_b, norm_ffn1_g, ffn1_w1, ffn1_w3, ffn1_w2,
              norm_mix_g, w_in, conv_w, q_norm_g, w_uq, kv_norm_g, w_ukv, out_norm_g,
              w_out, norm_ffn2_g, ffn2_w1, ffn2_w3, ffn2_w2, final_norm_g):
    b = x.shape[0]
    cuts = [CONV_WIDTH, 2 * CONV_WIDTH, 3 * CONV_WIDTH,
            3 * CONV_WIDTH + Q_LORA, 3 * CONV_WIDTH + Q_LORA + KV_LORA]
    for l in range(DEPTH):
        mod = (jax.nn.silu(c) @ ada_w[l] + ada_b[l]).reshape(b, N_MOD, D_MODEL)
        sh1, sc1, g1 = mod[:, 0], mod[:, 1], mod[:, 2]
        sh2, sc2, g2 = mod[:, 3], mod[:, 4], mod[:, 5]
        sh3, sc3, g3 = mod[:, 6], mod[:, 7], mod[:, 8]

        h = modulate(rmsnorm(x, norm_ffn1_g[l]), sh1, sc1)
        x = x + 0.5 * g1[:, None, :] * swiglu(h, ffn1_w1[l], ffn1_w3[l], ffn1_w2[l])

        h = modulate(rmsnorm(x, norm_mix_g[l]), sh2, sc2)
        z = h @ w_in[l]
        xb, xc, xu, cq, ckv, kr = jnp.split(z, cuts, axis=-1)
        y_a = short_conv_mixer(xb, xc, xu, conv_w[l])
        y_b = mla(cq, ckv, kr, positions, q_norm_g[l], w_uq[l], kv_norm_g[l], w_ukv[l])
        y_a = group_rmsnorm(y_a, out_norm_g[l, :CONV_WIDTH], CONV_GROUPS)
        y_b = group_rmsnorm(y_b, out_norm_g[l, CONV_WIDTH:], MLA_HEADS)
        y = jnp.concatenate([y_a, y_b], axis=-1) @ w_out[l]
        x = x + g2[:, None, :] * y

        h = modulate(rmsnorm(x, norm_ffn2_g[l]), sh3, sc3)
        x = x + 0.5 * g3[:, None, :] * swiglu(h, ffn2_w1[l], ffn2_w3[l], ffn2_w2[l])
    return rmsnorm(x, final_norm_g)


import jax as _jax
import jax.numpy as _jnp

TWIN_FORMAT = 'train_step'
FWD_PARAMS = ['x', 'c', 'positions', 'ada_w', 'ada_b', 'norm_ffn1_g', 'ffn1_w1', 'ffn1_w3', 'ffn1_w2', 'norm_mix_g', 'w_in', 'conv_w', 'q_norm_g', 'w_uq', 'kv_norm_g', 'w_ukv', 'out_norm_g', 'w_out', 'norm_ffn2_g', 'ffn2_w1', 'ffn2_w3', 'ffn2_w2', 'final_norm_g']
TWIN_WEIGHTS = ['ada_w', 'ada_b', 'norm_ffn1_g', 'ffn1_w1', 'ffn1_w3', 'ffn1_w2', 'norm_mix_g', 'w_in', 'conv_w', 'q_norm_g', 'w_uq', 'kv_norm_g', 'w_ukv', 'out_norm_g', 'w_out', 'norm_ffn2_g', 'ffn2_w1', 'ffn2_w3', 'ffn2_w2', 'final_norm_g']
TWIN_DIFF_INPUT = 'x'
TWIN_INPUTS = ['x', 'c', 'positions', 'ada_w', 'ada_b', 'norm_ffn1_g', 'ffn1_w1', 'ffn1_w3', 'ffn1_w2', 'norm_mix_g', 'w_in', 'conv_w', 'q_norm_g', 'w_uq', 'kv_norm_g', 'w_ukv', 'out_norm_g', 'w_out', 'norm_ffn2_g', 'ffn2_w1', 'ffn2_w3', 'ffn2_w2', 'final_norm_g', 'loss_target', 'm_ada_w', 'm_ada_b', 'm_norm_ffn1_g', 'm_ffn1_w1', 'm_ffn1_w3', 'm_ffn1_w2', 'm_norm_mix_g', 'm_w_in', 'm_conv_w', 'm_q_norm_g', 'm_w_uq', 'm_kv_norm_g', 'm_w_ukv', 'm_out_norm_g', 'm_w_out', 'm_norm_ffn2_g', 'm_ffn2_w1', 'm_ffn2_w3', 'm_ffn2_w2', 'm_final_norm_g', 'v_ada_w', 'v_ada_b', 'v_norm_ffn1_g', 'v_ffn1_w1', 'v_ffn1_w3', 'v_ffn1_w2', 'v_norm_mix_g', 'v_w_in', 'v_conv_w', 'v_q_norm_g', 'v_w_uq', 'v_kv_norm_g', 'v_w_ukv', 'v_out_norm_g', 'v_w_out', 'v_norm_ffn2_g', 'v_ffn2_w1', 'v_ffn2_w3', 'v_ffn2_w2', 'v_final_norm_g']
TWIN_OUTPUTS = ['loss', 'grad_x', 'grad_ada_w', 'grad_ada_b', 'grad_norm_ffn1_g', 'grad_ffn1_w1', 'grad_ffn1_w3', 'grad_ffn1_w2', 'grad_norm_mix_g', 'grad_w_in', 'grad_conv_w', 'grad_q_norm_g', 'grad_w_uq', 'grad_kv_norm_g', 'grad_w_ukv', 'grad_out_norm_g', 'grad_w_out', 'grad_norm_ffn2_g', 'grad_ffn2_w1', 'grad_ffn2_w3', 'grad_ffn2_w2', 'grad_final_norm_g', 'delta_ada_w', 'delta_ada_b', 'delta_norm_ffn1_g', 'delta_ffn1_w1', 'delta_ffn1_w3', 'delta_ffn1_w2', 'delta_norm_mix_g', 'delta_w_in', 'delta_conv_w', 'delta_q_norm_g', 'delta_w_uq', 'delta_kv_norm_g', 'delta_w_ukv', 'delta_out_norm_g', 'delta_w_out', 'delta_norm_ffn2_g', 'delta_ffn2_w1', 'delta_ffn2_w3', 'delta_ffn2_w2', 'delta_final_norm_g', 'new_m_ada_w', 'new_m_ada_b', 'new_m_norm_ffn1_g', 'new_m_ffn1_w1', 'new_m_ffn1_w3', 'new_m_ffn1_w2', 'new_m_norm_mix_g', 'new_m_w_in', 'new_m_conv_w', 'new_m_q_norm_g', 'new_m_w_uq', 'new_m_kv_norm_g', 'new_m_w_ukv', 'new_m_out_norm_g', 'new_m_w_out', 'new_m_norm_ffn2_g', 'new_m_ffn2_w1', 'new_m_ffn2_w3', 'new_m_ffn2_w2', 'new_m_final_norm_g', 'new_v_ada_w', 'new_v_ada_b', 'new_v_norm_ffn1_g', 'new_v_ffn1_w1', 'new_v_ffn1_w3', 'new_v_ffn1_w2', 'new_v_norm_mix_g', 'new_v_w_in', 'new_v_conv_w', 'new_v_q_norm_g', 'new_v_w_uq', 'new_v_kv_norm_g', 'new_v_w_ukv', 'new_v_out_norm_g', 'new_v_w_out', 'new_v_norm_ffn2_g', 'new_v_ffn2_w1', 'new_v_ffn2_w3', 'new_v_ffn2_w2', 'new_v_final_norm_g']
TWIN_LEAF_KINDS = {'loss': 'loss', 'grad_x': 'grad_x', 'grad_ada_w': 'grad_w', 'grad_ada_b': 'grad_w', 'grad_norm_ffn1_g': 'grad_w', 'grad_ffn1_w1': 'grad_w', 'grad_ffn1_w3': 'grad_w', 'grad_ffn1_w2': 'grad_w', 'grad_norm_mix_g': 'grad_w', 'grad_w_in': 'grad_w', 'grad_conv_w': 'grad_w', 'grad_q_norm_g': 'grad_w', 'grad_w_uq': 'grad_w', 'grad_kv_norm_g': 'grad_w', 'grad_w_ukv': 'grad_w', 'grad_out_norm_g': 'grad_w', 'grad_w_out': 'grad_w', 'grad_norm_ffn2_g': 'grad_w', 'grad_ffn2_w1': 'grad_w', 'grad_ffn2_w3': 'grad_w', 'grad_ffn2_w2': 'grad_w', 'grad_final_norm_g': 'grad_w', 'delta_ada_w': 'delta_w', 'delta_ada_b': 'delta_w', 'delta_norm_ffn1_g': 'delta_w', 'delta_ffn1_w1': 'delta_w', 'delta_ffn1_w3': 'delta_w', 'delta_ffn1_w2': 'delta_w', 'delta_norm_mix_g': 'delta_w', 'delta_w_in': 'delta_w', 'delta_conv_w': 'delta_w', 'delta_q_norm_g': 'delta_w', 'delta_w_uq': 'delta_w', 'delta_kv_norm_g': 'delta_w', 'delta_w_ukv': 'delta_w', 'delta_out_norm_g': 'delta_w', 'delta_w_out': 'delta_w', 'delta_norm_ffn2_g': 'delta_w', 'delta_ffn2_w1': 'delta_w', 'delta_ffn2_w3': 'delta_w', 'delta_ffn2_w2': 'delta_w', 'delta_final_norm_g': 'delta_w', 'new_m_ada_w': 'new_m', 'new_m_ada_b': 'new_m', 'new_m_norm_ffn1_g': 'new_m', 'new_m_ffn1_w1': 'new_m', 'new_m_ffn1_w3': 'new_m', 'new_m_ffn1_w2': 'new_m', 'new_m_norm_mix_g': 'new_m', 'new_m_w_in': 'new_m', 'new_m_conv_w': 'new_m', 'new_m_q_norm_g': 'new_m', 'new_m_w_uq': 'new_m', 'new_m_kv_norm_g': 'new_m', 'new_m_w_ukv': 'new_m', 'new_m_out_norm_g': 'new_m', 'new_m_w_out': 'new_m', 'new_m_norm_ffn2_g': 'new_m', 'new_m_ffn2_w1': 'new_m', 'new_m_ffn2_w3': 'new_m', 'new_m_ffn2_w2': 'new_m', 'new_m_final_norm_g': 'new_m', 'new_v_ada_w': 'new_v', 'new_v_ada_b': 'new_v', 'new_v_norm_ffn1_g': 'new_v', 'new_v_ffn1_w1': 'new_v', 'new_v_ffn1_w3': 'new_v', 'new_v_ffn1_w2': 'new_v', 'new_v_norm_mix_g': 'new_v', 'new_v_w_in': 'new_v', 'new_v_conv_w': 'new_v', 'new_v_q_norm_g': 'new_v', 'new_v_w_uq': 'new_v', 'new_v_kv_norm_g': 'new_v', 'new_v_w_ukv': 'new_v', 'new_v_out_norm_g': 'new_v', 'new_v_w_out': 'new_v', 'new_v_norm_ffn2_g': 'new_v', 'new_v_ffn2_w1': 'new_v', 'new_v_ffn2_w3': 'new_v', 'new_v_ffn2_w2': 'new_v', 'new_v_final_norm_g': 'new_v'}


def _forward(args):
    return _fwd_reference(*[args[k] for k in FWD_PARAMS])


def _output_shape():
    def fwd():
        inp = _fwd_setup_inputs(0)
        return _fwd_reference(*[inp[k] for k in FWD_PARAMS])
    out = _jax.eval_shape(fwd)
    return out.shape, out.dtype

N_MICROBATCH = 1
ADAM_LR = 0.001
ADAM_B1 = 0.9
ADAM_B2 = 0.999
ADAM_EPS = 1e-08
ADAM_WD = 0.01
ADAM_STEP = 10
PER_EXAMPLE_BATCH_AXIS = {'x': 0, 'c': 0, 'positions': 0, 'loss_target': 0}
SHARED_INPUTS = []
_WEIGHT_DTYPES = {'ada_w': _jnp.float32, 'ada_b': _jnp.float32, 'norm_ffn1_g': _jnp.float32, 'ffn1_w1': _jnp.float32, 'ffn1_w3': _jnp.float32, 'ffn1_w2': _jnp.float32, 'norm_mix_g': _jnp.float32, 'w_in': _jnp.float32, 'conv_w': _jnp.float32, 'q_norm_g': _jnp.float32, 'w_uq': _jnp.float32, 'kv_norm_g': _jnp.float32, 'w_ukv': _jnp.float32, 'out_norm_g': _jnp.float32, 'w_out': _jnp.float32, 'norm_ffn2_g': _jnp.float32, 'ffn2_w1': _jnp.float32, 'ffn2_w3': _jnp.float32, 'ffn2_w2': _jnp.float32, 'final_norm_g': _jnp.float32}
MOMENT_SCALE = {'ada_w': 1.394759e-01, 'ada_b': 2.722790e-01, 'norm_ffn1_g': 4.155496e-02, 'ffn1_w1': 1.749669e-02, 'ffn1_w3': 1.708617e-02, 'ffn1_w2': 2.827460e-02, 'norm_mix_g': 9.681326e-02, 'w_in': 9.172623e-02, 'conv_w': 8.228571e-02, 'q_norm_g': 3.863973e-02, 'w_uq': 2.691604e-02, 'kv_norm_g': 2.477027e-01, 'w_ukv': 1.128716e-01, 'out_norm_g': 1.436872e-01, 'w_out': 1.191030e-01, 'norm_ffn2_g': 3.776150e-02, 'ffn2_w1': 1.620111e-02, 'ffn2_w3': 1.584418e-02, 'ffn2_w2': 2.634969e-02, 'final_norm_g': 6.420824e+01}


def _to_microbatches(a, axis):
    t = _jnp.moveaxis(a, axis, 0)
    t = t.reshape((N_MICROBATCH, t.shape[0] // N_MICROBATCH) + t.shape[1:])
    return _jnp.moveaxis(t, 1, axis + 1)


def setup_inputs(seed: int = 0) -> dict:
    inp = _fwd_setup_inputs(seed)
    key = _jax.random.fold_in(_jax.random.key(seed), 7919)
    shape, _ = _output_shape()
    out = dict(inp)
    out["loss_target"] = _jax.random.normal(_jax.random.fold_in(key, 0), shape, _jnp.float32)
    for i, name in enumerate(TWIN_WEIGHTS):
        w = inp[name].astype(_jnp.float32)
        if MOMENT_SCALE is None:
            s = _jnp.sqrt(_jnp.mean(_jnp.square(w)) + 1e-30)
        else:
            s = MOMENT_SCALE[name]
        km, kv = _jax.random.split(_jax.random.fold_in(key, i + 1))
        out[name] = w
        out["m_" + name] = s * _jax.random.normal(km, w.shape, _jnp.float32)
        out["v_" + name] = (s * s) * _jax.random.uniform(kv, w.shape, _jnp.float32, 0.5, 1.5)
    if N_MICROBATCH > 1:
        for name, axis in PER_EXAMPLE_BATCH_AXIS.items():
            out[name] = _to_microbatches(out[name], axis)
    return {'x': out['x'], 'c': out['c'], 'positions': out['positions'], 'ada_w': out['ada_w'], 'ada_b': out['ada_b'], 'norm_ffn1_g': out['norm_ffn1_g'], 'ffn1_w1': out['ffn1_w1'], 'ffn1_w3': out['ffn1_w3'], 'ffn1_w2': out['ffn1_w2'], 'norm_mix_g': out['norm_mix_g'], 'w_in': out['w_in'], 'conv_w': out['conv_w'], 'q_norm_g': out['q_norm_g'], 'w_uq': out['w_uq'], 'kv_norm_g': out['kv_norm_g'], 'w_ukv': out['w_ukv'], 'out_norm_g': out['out_norm_g'], 'w_out': out['w_out'], 'norm_ffn2_g': out['norm_ffn2_g'], 'ffn2_w1': out['ffn2_w1'], 'ffn2_w3': out['ffn2_w3'], 'ffn2_w2': out['ffn2_w2'], 'final_norm_g': out['final_norm_g'], 'loss_target': out['loss_target'], 'm_ada_w': out['m_ada_w'], 'm_ada_b': out['m_ada_b'], 'm_norm_ffn1_g': out['m_norm_ffn1_g'], 'm_ffn1_w1': out['m_ffn1_w1'], 'm_ffn1_w3': out['m_ffn1_w3'], 'm_ffn1_w2': out['m_ffn1_w2'], 'm_norm_mix_g': out['m_norm_mix_g'], 'm_w_in': out['m_w_in'], 'm_conv_w': out['m_conv_w'], 'm_q_norm_g': out['m_q_norm_g'], 'm_w_uq': out['m_w_uq'], 'm_kv_norm_g': out['m_kv_norm_g'], 'm_w_ukv': out['m_w_ukv'], 'm_out_norm_g': out['m_out_norm_g'], 'm_w_out': out['m_w_out'], 'm_norm_ffn2_g': out['m_norm_ffn2_g'], 'm_ffn2_w1': out['m_ffn2_w1'], 'm_ffn2_w3': out['m_ffn2_w3'], 'm_ffn2_w2': out['m_ffn2_w2'], 'm_final_norm_g': out['m_final_norm_g'], 'v_ada_w': out['v_ada_w'], 'v_ada_b': out['v_ada_b'], 'v_norm_ffn1_g': out['v_norm_ffn1_g'], 'v_ffn1_w1': out['v_ffn1_w1'], 'v_ffn1_w3': out['v_ffn1_w3'], 'v_ffn1_w2': out['v_ffn1_w2'], 'v_norm_mix_g': out['v_norm_mix_g'], 'v_w_in': out['v_w_in'], 'v_conv_w': out['v_conv_w'], 'v_q_norm_g': out['v_q_norm_g'], 'v_w_uq': out['v_w_uq'], 'v_kv_norm_g': out['v_kv_norm_g'], 'v_w_ukv': out['v_w_ukv'], 'v_out_norm_g': out['v_out_norm_g'], 'v_w_out': out['v_w_out'], 'v_norm_ffn2_g': out['v_norm_ffn2_g'], 'v_ffn2_w1': out['v_ffn2_w1'], 'v_ffn2_w3': out['v_ffn2_w3'], 'v_ffn2_w2': out['v_ffn2_w2'], 'v_final_norm_g': out['v_final_norm_g']}


def _loss(weights, diff, rest, loss_target):
    with _jax.named_scope("forward"):
        args = {**rest, TWIN_DIFF_INPUT: diff, **{k: w.astype(_WEIGHT_DTYPES[k]) for k, w in weights.items()}}
        y = _forward(args)
    with _jax.named_scope("loss_head"):
        err = _jnp.square(y.astype(_jnp.float32) - loss_target)
        return 0.5 * _jnp.sum(_jnp.mean(err, axis=-1)) if err.ndim else 0.5 * err


def _adamw(w, g, m, v):
    m = ADAM_B1 * m + (1.0 - ADAM_B1) * g
    v = ADAM_B2 * v + (1.0 - ADAM_B2) * _jnp.square(g)
    m_hat = m / (1.0 - ADAM_B1 ** ADAM_STEP)
    v_hat = v / (1.0 - ADAM_B2 ** ADAM_STEP)
    delta = -ADAM_LR * (m_hat / (_jnp.sqrt(v_hat) + ADAM_EPS) + ADAM_WD * w)
    return delta, m, v


def reference(x, c, positions, ada_w, ada_b, norm_ffn1_g, ffn1_w1, ffn1_w3, ffn1_w2, norm_mix_g, w_in, conv_w, q_norm_g, w_uq, kv_norm_g, w_ukv, out_norm_g, w_out, norm_ffn2_g, ffn2_w1, ffn2_w3, ffn2_w2, final_norm_g, loss_target, m_ada_w, m_ada_b, m_norm_ffn1_g, m_ffn1_w1, m_ffn1_w3, m_ffn1_w2, m_norm_mix_g, m_w_in, m_conv_w, m_q_norm_g, m_w_uq, m_kv_norm_g, m_w_ukv, m_out_norm_g, m_w_out, m_norm_ffn2_g, m_ffn2_w1, m_ffn2_w3, m_ffn2_w2, m_final_norm_g, v_ada_w, v_ada_b, v_norm_ffn1_g, v_ffn1_w1, v_ffn1_w3, v_ffn1_w2, v_norm_mix_g, v_w_in, v_conv_w, v_q_norm_g, v_w_uq, v_kv_norm_g, v_w_ukv, v_out_norm_g, v_w_out, v_norm_ffn2_g, v_ffn2_w1, v_ffn2_w3, v_ffn2_w2, v_final_norm_g):
    given = dict(x=x, c=c, positions=positions, ada_w=ada_w, ada_b=ada_b, norm_ffn1_g=norm_ffn1_g, ffn1_w1=ffn1_w1, ffn1_w3=ffn1_w3, ffn1_w2=ffn1_w2, norm_mix_g=norm_mix_g, w_in=w_in, conv_w=conv_w, q_norm_g=q_norm_g, w_uq=w_uq, kv_norm_g=kv_norm_g, w_ukv=w_ukv, out_norm_g=out_norm_g, w_out=w_out, norm_ffn2_g=norm_ffn2_g, ffn2_w1=ffn2_w1, ffn2_w3=ffn2_w3, ffn2_w2=ffn2_w2, final_norm_g=final_norm_g, loss_target=loss_target, m_ada_w=m_ada_w, m_ada_b=m_ada_b, m_norm_ffn1_g=m_norm_ffn1_g, m_ffn1_w1=m_ffn1_w1, m_ffn1_w3=m_ffn1_w3, m_ffn1_w2=m_ffn1_w2, m_norm_mix_g=m_norm_mix_g, m_w_in=m_w_in, m_conv_w=m_conv_w, m_q_norm_g=m_q_norm_g, m_w_uq=m_w_uq, m_kv_norm_g=m_kv_norm_g, m_w_ukv=m_w_ukv, m_out_norm_g=m_out_norm_g, m_w_out=m_w_out, m_norm_ffn2_g=m_norm_ffn2_g, m_ffn2_w1=m_ffn2_w1, m_ffn2_w3=m_ffn2_w3, m_ffn2_w2=m_ffn2_w2, m_final_norm_g=m_final_norm_g, v_ada_w=v_ada_w, v_ada_b=v_ada_b, v_norm_ffn1_g=v_norm_ffn1_g, v_ffn1_w1=v_ffn1_w1, v_ffn1_w3=v_ffn1_w3, v_ffn1_w2=v_ffn1_w2, v_norm_mix_g=v_norm_mix_g, v_w_in=v_w_in, v_conv_w=v_conv_w, v_q_norm_g=v_q_norm_g, v_w_uq=v_w_uq, v_kv_norm_g=v_kv_norm_g, v_w_ukv=v_w_ukv, v_out_norm_g=v_out_norm_g, v_w_out=v_w_out, v_norm_ffn2_g=v_norm_ffn2_g, v_ffn2_w1=v_ffn2_w1, v_ffn2_w3=v_ffn2_w3, v_ffn2_w2=v_ffn2_w2, v_final_norm_g=v_final_norm_g)
    weights = {n: given[n] for n in TWIN_WEIGHTS}
    shared = {n: given[n] for n in SHARED_INPUTS}
    per_example = {n: given[n] for n in ['x', 'c', 'positions']}
    grad_fn = _jax.value_and_grad(_loss, argnums=(0, 1))

    def one_microbatch(ex, loss_target):
        ex = dict(ex)
        diff = ex.pop(TWIN_DIFF_INPUT)
        return grad_fn(weights, diff, {**shared, **ex}, loss_target)

    if N_MICROBATCH == 1:
        loss, (grad_w, grad_x) = one_microbatch(per_example, given["loss_target"])
    else:
        def body(carry, xs):
            loss_sum, grad_sum = carry
            l_k, (gw_k, gx_k) = one_microbatch(xs[0], xs[1])
            with _jax.named_scope("update"):
                return (loss_sum + l_k, _jax.tree.map(_jnp.add, grad_sum, gw_k)), gx_k

        init = (_jnp.zeros((), _jnp.float32), _jax.tree.map(_jnp.zeros_like, weights))
        (loss, grad_w), grad_x = _jax.lax.scan(body, init, (per_example, given["loss_target"]))
    with _jax.named_scope("update"):
        delta_w, new_m, new_v = {}, {}, {}
        for n in TWIN_WEIGHTS:
            delta_w[n], new_m[n], new_v[n] = _adamw(weights[n], grad_w[n], given["m_" + n], given["v_" + n])
    return (loss, grad_x, *[grad_w[n] for n in TWIN_WEIGHTS], *[delta_w[n] for n in TWIN_WEIGHTS],
            *[new_m[n] for n in TWIN_WEIGHTS], *[new_v[n] for n in TWIN_WEIGHTS])
```

```python
import functools

import jax
import jax.numpy as jnp
from jax import lax
from jax.experimental import pallas as pl
from jax.experimental.pallas import tpu as pltpu

F32 = jnp.float32
BF16 = jnp.bfloat16
MESH_ID = pl.DeviceIdType.MESH
N_DEV = 8

EPS = 1e-6
CHUNK = 64
N_MOD = 9
CONV_WIDTH = 512
CONV_GROUPS = 8
CONV_K = 3
MLA_HEADS = 4
QK_NOPE = 128
QK_ROPE = 64
V_HEAD = 128
Q_LORA = 384
KV_LORA = 256
ROPE_THETA = 10000.0
MLA_WIDTH = MLA_HEADS * V_HEAD
MIX_WIDTH = CONV_WIDTH + MLA_WIDTH
IN_COLS = 3 * CONV_WIDTH + Q_LORA + KV_LORA + QK_ROPE
ZC_COLS = 3 * CONV_WIDTH
ZM_COLS = Q_LORA + KV_LORA + 128
HEAD_PAD = 256
QK_COLS = MLA_HEADS * HEAD_PAD
ATTN_SCALE = (QK_NOPE + QK_ROPE) ** -0.5
NEG_INF = -1e30

ADAM_LR = 0.001
ADAM_B1 = 0.9
ADAM_B2 = 0.999
ADAM_EPS = 1e-08
ADAM_WD = 0.01
ADAM_STEP = 10

LANES = 128
VMEM_LIMIT = 56 * 1024 * 1024
ROW_TILE = 512
FFN_FWD_ROWS = 1024
FFN_COLS = 256
GRAD_TILE = 1408
ATTN_TILE = 512

NN = (((1,), (0,)), ((), ()))
NT = (((1,), (1,)), ((), ()))
TN = (((0,), (0,)), ((), ()))


def _dot(a, b, dims=NN):
    return lax.dot_general(a, b, dims, preferred_element_type=F32)


def _tile(n, cap, mult=LANES):
    best = None
    for t in range(mult, min(n, cap) + 1, mult):
        if n % t == 0:
            best = t
    return n if best is None else best


def _params(sem=None):
    return pltpu.CompilerParams(dimension_semantics=sem, vmem_limit_bytes=VMEM_LIMIT)


def _row(v):
    return pl.BlockSpec(v.shape, lambda *_: (0,) * v.ndim)


def _rms(x):
    r = lax.rsqrt(jnp.mean(x * x, axis=-1, keepdims=True) + EPS)
    return x * r, r


def _norm_mod_bwd(dh, x, gn, sc):
    xhat, r = _rms(x)
    d_sh = jnp.sum(dh, axis=0, keepdims=True)
    d_sc = jnp.sum(dh * (xhat * gn), axis=0, keepdims=True)
    dxn = dh * (1.0 + sc)
    d_gn = jnp.sum(dxn * xhat, axis=0, keepdims=True)
    dxh = dxn * gn
    dx = r * (dxh - xhat * jnp.mean(dxh * xhat, axis=-1, keepdims=True))
    return dx, d_sh, d_sc, d_gn


def _group_mean(v, gmat):
    hi = v.astype(BF16)
    lo = (v - hi.astype(F32)).astype(BF16)
    return _dot(hi, gmat) + _dot(lo, gmat)


def _add_rows(ref, rows):
    for r, v in enumerate(rows):
        ref[r:r + 1, :] += v


def all_gather(xs, name, in_vmem):
    m_per, n = xs.shape

    def body(x_ref, out_ref, send_sems, recv_sems, local_sem):
        x, y, c = lax.axis_index("x"), lax.axis_index("y"), lax.axis_index("c")
        me, sibling = (x, y, c), (x, y, 1 - c)
        chips = [(1 - x, y), (x, 1 - y), (1 - x, 1 - y)]

        def rows(px, py, pc):
            return out_ref.at[pl.ds((4 * px + 2 * py + pc) * m_per, m_per), :]

        def copy(k, block, to, src=None):
            return pltpu.make_async_remote_copy(
                src_ref=rows(*block) if src is None else src, dst_ref=rows(*block),
                send_sem=send_sems.at[k], recv_sem=recv_sems.at[k], device_id=to, device_id_type=MESH_ID)

        mine = pltpu.make_async_copy(x_ref, rows(*me), local_sem)
        mine.start()
        first = [copy(0, me, sibling, src=x_ref)]
        first += [copy(1 + j, me, (*chip, c), src=x_ref) for j, chip in enumerate(chips)]
        for cp in first:
            cp.start()
        passed = [copy(4 + j, (*chip, c), sibling) for j, chip in enumerate(chips)]
        for j, chip in enumerate(chips):
            copy(1 + j, (*chip, c), me).wait_recv()
            passed[j].start()
        copy(0, sibling, me).wait_recv()
        for j, chip in enumerate(chips):
            copy(4 + j, (*chip, 1 - c), me).wait_recv()
        for cp in first + passed:
            cp.wait_send()
        mine.wait()

    space = pltpu.VMEM if in_vmem else pl.ANY
    return pl.pallas_call(
        body, name=name,
        out_shape=jax.ShapeDtypeStruct((N_DEV * m_per, n), xs.dtype),
        in_specs=[pl.BlockSpec(memory_space=space)],
        out_specs=pl.BlockSpec(memory_space=space),
        scratch_shapes=[pltpu.SemaphoreType.DMA((7,)), pltpu.SemaphoreType.DMA((7,)), pltpu.SemaphoreType.DMA],
        compiler_params=pltpu.CompilerParams(vmem_limit_bytes=VMEM_LIMIT),
    )(xs)


def exchange_sibling(g8):
    _, r, n = g8.shape

    def body(g_ref, out_ref, send_sems, recv_sems):
        x, y, c = lax.axis_index("x"), lax.axis_index("y"), lax.axis_index("c")
        copies = [pltpu.make_async_remote_copy(
            src_ref=g_ref.at[2 * k + (1 - c)], dst_ref=out_ref.at[k],
            send_sem=send_sems.at[k], recv_sem=recv_sems.at[k],
            device_id=(x, y, 1 - c), device_id_type=MESH_ID) for k in range(4)]
        for cp in copies:
            cp.start()
        for cp in copies:
            cp.wait_recv()
        for cp in copies:
            cp.wait_send()

    return pl.pallas_call(
        body, name="rs_sibling",
        out_shape=jax.ShapeDtypeStruct((4, r, n), g8.dtype),
        in_specs=[pl.BlockSpec(memory_space=pl.ANY)], out_specs=pl.BlockSpec(memory_space=pl.ANY),
        scratch_shapes=[pltpu.SemaphoreType.DMA((4,)), pltpu.SemaphoreType.DMA((4,))],
    )(g8)


def exchange_chips(ps):
    _, r, n = ps.shape

    def body(p_ref, out_ref, send_sems, recv_sems):
        x, y, c = lax.axis_index("x"), lax.axis_index("y"), lax.axis_index("c")
        chips = [(1 - x, y), (x, 1 - y), (1 - x, 1 - y)]
        copies = [pltpu.make_async_remote_copy(
            src_ref=p_ref.at[j], dst_ref=out_ref.at[j],
            send_sem=send_sems.at[j], recv_sem=recv_sems.at[j],
            device_id=(*chip, c), device_id_type=MESH_ID) for j, chip in enumerate(chips)]
        for cp in copies:
            cp.start()
        for cp in copies:
            cp.wait_recv()
        for cp in copies:
            cp.wait_send()

    return pl.pallas_call(
        body, name="rs_chips",
        out_shape=jax.ShapeDtypeStruct((3, r, n), ps.dtype),
        in_specs=[pl.BlockSpec(memory_space=pl.ANY)], out_specs=pl.BlockSpec(memory_space=pl.ANY),
        scratch_shapes=[pltpu.SemaphoreType.DMA((3,)), pltpu.SemaphoreType.DMA((3,))],
    )(ps)


def add_pairs(g8, got, src_idx, got_idx, out_dtype, name):
    nj = src_idx.shape[0]
    _, r, n = g8.shape
    tr = _tile(r, 2048, 16)

    def body(si_ref, gi_ref, a_ref, b_ref, o_ref):
        o_ref[...] = (a_ref[...] + b_ref[...]).astype(out_dtype)

    return pl.pallas_call(
        body, name=name,
        out_shape=jax.ShapeDtypeStruct((nj, r, n), out_dtype),
        grid_spec=pltpu.PrefetchScalarGridSpec(
            num_scalar_prefetch=2, grid=(nj, r // tr),
            in_specs=[pl.BlockSpec((1, tr, n), lambda j, i, si, gi: (si[j], i, 0)),
                      pl.BlockSpec((1, tr, n), lambda j, i, si, gi: (gi[j], i, 0))],
            out_specs=pl.BlockSpec((1, tr, n), lambda j, i, si, gi: (j, i, 0))),
        compiler_params=_params(("arbitrary", "arbitrary")),
    )(src_idx, got_idx, g8, got)


def add_received(own, got):
    _, r, n = own.shape
    tr = _tile(r, 2048, 16)

    def body(a_ref, b_ref, o_ref):
        acc = a_ref[0]
        for j in range(3):
            acc = acc + b_ref[j].astype(F32)
        o_ref[...] = acc

    return pl.pallas_call(
        body, name="rs_sum",
        out_shape=jax.ShapeDtypeStruct((r, n), F32),
        grid=(r // tr,),
        in_specs=[pl.BlockSpec((1, tr, n), lambda i: (0, i, 0)), pl.BlockSpec((3, tr, n), lambda i: (0, i, 0))],
        out_specs=pl.BlockSpec((tr, n), lambda i: (i, 0)),
        compiler_params=_params(("arbitrary",)),
    )(own, got)


def sum_devices(g):
    _, l = g.shape

    def body(g_ref, o_ref):
        acc = g_ref[0:1, :]
        for j in range(1, N_DEV):
            acc = acc + g_ref[j:j + 1, :]
        o_ref[...] = acc

    return pl.pallas_call(body, name="sum_devices", out_shape=jax.ShapeDtypeStruct((1, l), F32))(g)


def sum_lanes(v):
    def body(v_ref, o_ref):
        o_ref[...] = jnp.broadcast_to(jnp.sum(v_ref[...], axis=-1, keepdims=True), (1, LANES))

    return pl.pallas_call(body, name="sum_lanes", out_shape=jax.ShapeDtypeStruct((1, LANES), F32))(v)


def ada_forward(c_all, ada_w, ada_b_cols):
    nb, n = c_all.shape[0], ada_w.shape[1]

    def body(c_ref, w_ref, b_ref, o_ref):
        cv = c_ref[...]
        s = (cv * jax.nn.sigmoid(cv)).astype(BF16)
        o_ref[...] = _dot(s, w_ref[...].astype(BF16)) + b_ref[...]

    return pl.pallas_call(body, name="ada_fwd", out_shape=jax.ShapeDtypeStruct((nb, n), F32),
                          compiler_params=_params())(c_all, ada_w, ada_b_cols)


def ada_backward(c_all16, dmod16):
    d, n = c_all16.shape[1], dmod16.shape[1]

    def body(c_ref, g_ref, o_ref):
        cv = c_ref[...]
        s = (cv * jax.nn.sigmoid(cv)).astype(BF16)
        o_ref[...] = _dot(s, g_ref[...].astype(BF16), TN)

    return pl.pallas_call(body, name="ada_bwd", out_shape=jax.ShapeDtypeStruct((d, n), F32),
                          compiler_params=_params())(c_all16, dmod16)


def ffn_forward(x, gn, sc, sh, gate, w1, w3, w2, name):
    t, d = x.shape
    f = w1.shape[1]
    tm, tf = _tile(t, FFN_FWD_ROWS, 16), _tile(f, FFN_COLS)
    nf = f // tf

    def body(x_ref, gn_ref, sc_ref, sh_ref, gate_ref, w1_ref, w3_ref, w2_ref,
             xo_ref, h_ref, a_ref, b_ref, y_ref, hs, acc):
        j = pl.program_id(1)

        @pl.when(j == 0)
        def _():
            xhat, _ = _rms(x_ref[...])
            h = (xhat * gn_ref[...] * (1.0 + sc_ref[...]) + sh_ref[...]).astype(BF16)
            hs[...] = h
            h_ref[...] = h
            acc[...] = jnp.zeros_like(acc)

        h = hs[...]
        a = _dot(h, w1_ref[...])
        b = _dot(h, w3_ref[...])
        a_ref[...] = a.astype(BF16)
        b_ref[...] = b.astype(BF16)
        u = (a * jax.nn.sigmoid(a) * b).astype(BF16)
        acc[...] += _dot(u, w2_ref[...])

        @pl.when(j == nf - 1)
        def _():
            y = acc[...]
            y_ref[...] = y
            xo_ref[...] = x_ref[...] + 0.5 * gate_ref[...] * y

    row = pl.BlockSpec((tm, d), lambda i, j: (i, 0))
    vec = pl.BlockSpec((1, d), lambda i, j: (0, 0))
    wide = pl.BlockSpec((tm, tf), lambda i, j: (i, j))
    return pl.pallas_call(
        body, name=name, grid=(t // tm, nf),
        in_specs=[row, vec, vec, vec, vec,
                  pl.BlockSpec((d, tf), lambda i, j: (0, j)), pl.BlockSpec((d, tf), lambda i, j: (0, j)),
                  pl.BlockSpec((tf, d), lambda i, j: (j, 0))],
        out_specs=[row, row, wide, wide, row],
        out_shape=[jax.ShapeDtypeStruct((t, d), F32), jax.ShapeDtypeStruct((t, d), BF16),
                   jax.ShapeDtypeStruct((t, f), BF16), jax.ShapeDtypeStruct((t, f), BF16),
                   jax.ShapeDtypeStruct((t, d), F32)],
        scratch_shapes=[pltpu.VMEM((tm, d), BF16), pltpu.VMEM((tm, d), F32)],
        compiler_params=_params(("arbitrary", "arbitrary")),
    )(x, gn, sc, sh, gate, w1, w3, w2)


def ffn_backward(dxo, x, a, b, y, gn, sc, sh, gate, w1, w3, w2, name):
    t, d = x.shape
    f = w1.shape[1]
    tm, tf = _tile(t, ROW_TILE, 16), _tile(f, FFN_COLS)
    nf = f // tf

    def body(dxo_ref, x_ref, a_ref, b_ref, y_ref, gn_ref, sc_ref, sh_ref, gate_ref, w1_ref, w3_ref, w2_ref,
             dx_ref, da_ref, db_ref, u_ref, dy_ref, sums_ref, dys, acc):
        i, j = pl.program_id(0), pl.program_id(1)

        @pl.when(jnp.logical_and(i == 0, j == 0))
        def _():
            sums_ref[...] = jnp.zeros_like(sums_ref)

        @pl.when(j == 0)
        def _():
            dy = (0.5 * gate_ref[...] * dxo_ref[...]).astype(BF16)
            dys[...] = dy
            dy_ref[...] = dy
            acc[...] = jnp.zeros_like(acc)

        du = _dot(dys[...], w2_ref[...], NT)
        av = a_ref[...].astype(F32)
        bv = b_ref[...].astype(F32)
        s = jax.nn.sigmoid(av)
        sa = av * s
        da = (du * bv * (s * (1.0 + av * (1.0 - s)))).astype(BF16)
        db = (du * sa).astype(BF16)
        da_ref[...] = da
        db_ref[...] = db
        u_ref[...] = (sa * bv).astype(BF16)
        acc[...] += _dot(da, w1_ref[...], NT) + _dot(db, w3_ref[...], NT)

        @pl.when(j == nf - 1)
        def _():
            dxo_v = dxo_ref[...]
            dx, d_sh, d_sc, d_gn = _norm_mod_bwd(acc[...], x_ref[...], gn_ref[...], sc_ref[...])
            dx_ref[...] = dxo_v + dx
            d_gate = jnp.sum(dxo_v * (0.5 * y_ref[...]), axis=0, keepdims=True)
            _add_rows(sums_ref, [d_sh, d_sc, d_gate, d_gn])

    row = pl.BlockSpec((tm, d), lambda i, j: (i, 0))
    vec = pl.BlockSpec((1, d), lambda i, j: (0, 0))
    wide = pl.BlockSpec((tm, tf), lambda i, j: (i, j))
    return pl.pallas_call(
        body, name=name, grid=(t // tm, nf),
        in_specs=[row, row, wide, wide, row, vec, vec, vec, vec,
                  pl.BlockSpec((d, tf), lambda i, j: (0, j)), pl.BlockSpec((d, tf), lambda i, j: (0, j)),
                  pl.BlockSpec((tf, d), lambda i, j: (j, 0))],
        out_specs=[row, wide, wide, wide, row, pl.BlockSpec((8, d), lambda i, j: (0, 0))],
        out_shape=[jax.ShapeDtypeStruct((t, d), F32), jax.ShapeDtypeStruct((t, f), BF16),
                   jax.ShapeDtypeStruct((t, f), BF16), jax.ShapeDtypeStruct((t, f), BF16),
                   jax.ShapeDtypeStruct((t, d), BF16), jax.ShapeDtypeStruct((8, d), F32)],
        scratch_shapes=[pltpu.VMEM((tm, d), BF16), pltpu.VMEM((tm, d), F32)],
        compiler_params=_params(("arbitrary", "arbitrary")),
    )(dxo, x, a, b, y, gn, sc, sh, gate, w1, w3, w2)


def matmul_tn(a, b, name):
    t, m = a.shape
    n = b.shape[1]
    tm, tn, tk = _tile(m, GRAD_TILE), _tile(n, GRAD_TILE), _tile(t, 1024, 16)
    nk = t // tk

    def body(a_ref, b_ref, o_ref, acc):
        k = pl.program_id(2)

        @pl.when(k == 0)
        def _():
            acc[...] = jnp.zeros_like(acc)

        acc[...] += _dot(a_ref[...], b_ref[...], TN)

        @pl.when(k == nk - 1)
        def _():
            o_ref[...] = acc[...]

    return pl.pallas_call(
        body, name=name, grid=(m // tm, n // tn, nk),
        in_specs=[pl.BlockSpec((tk, tm), lambda i, j, k: (k, i)), pl.BlockSpec((tk, tn), lambda i, j, k: (k, j))],
        out_specs=pl.BlockSpec((tm, tn), lambda i, j, k: (i, j)),
        out_shape=jax.ShapeDtypeStruct((m, n), F32),
        scratch_shapes=[pltpu.VMEM((tm, tn), F32)],
        compiler_params=_params(("arbitrary", "arbitrary", "arbitrary")),
    )(a, b)


def mix_in_forward(x, gn, sc, sh, w_in):
    t, d = x.shape
    tm = _tile(t, ROW_TILE, 16)

    def body(x_ref, gn_ref, sc_ref, sh_ref, w_ref, h_ref, zc_ref, zm_ref):
        xhat, _ = _rms(x_ref[...])
        h = (xhat * gn_ref[...] * (1.0 + sc_ref[...]) + sh_ref[...]).astype(BF16)
        h_ref[...] = h
        z = _dot(h, w_ref[...])
        zc_ref[...] = z[:, :ZC_COLS]
        zm_ref[...] = z[:, ZC_COLS:]

    row = pl.BlockSpec((tm, d), lambda i: (i, 0))
    vec = pl.BlockSpec((1, d), lambda i: (0, 0))
    return pl.pallas_call(
        body, name="mix_in_fwd", grid=(t // tm,),
        in_specs=[row, vec, vec, vec, _row(w_in)],
        out_specs=[row, pl.BlockSpec((tm, ZC_COLS), lambda i: (i, 0)), pl.BlockSpec((tm, ZM_COLS), lambda i: (i, 0))],
        out_shape=[jax.ShapeDtypeStruct((t, d), BF16), jax.ShapeDtypeStruct((t, ZC_COLS), F32),
                   jax.ShapeDtypeStruct((t, ZM_COLS), F32)],
        compiler_params=_params(("arbitrary",)),
    )(x, gn, sc, sh, w_in)


def _rope_tables(pos, inv_freq):
    ang = pos * inv_freq
    lane = lax.broadcasted_iota(jnp.int32, ang.shape, 1)
    cos, sin = jnp.cos(ang), jnp.sin(ang)
    half = QK_ROPE // 2
    return cos, jnp.where(lane < half, -sin, 0.0), jnp.where(jnp.logical_and(lane >= half, lane < QK_ROPE), sin, 0.0)


def _rope(v, tables):
    cos, sin_a, sin_b = tables
    return v * cos + pltpu.roll(v, LANES - QK_ROPE // 2, 1) * sin_a + pltpu.roll(v, QK_ROPE // 2, 1) * sin_b


def _rope_transposed(dv, tables):
    cos, sin_a, sin_b = tables
    return dv * cos + pltpu.roll(dv * sin_a, QK_ROPE // 2, 1) + pltpu.roll(dv * sin_b, LANES - QK_ROPE // 2, 1)


def mla_project(zm, pos, inv_freq, qg, kvg, w_uq, w_ukv):
    t = zm.shape[0]
    tm = _tile(t, ROW_TILE, 16)

    def body(zm_ref, pos_ref, if_ref, qg_ref, kvg_ref, wq_ref, wkv_ref, qn_ref, kvn_ref, q_ref, k_ref, v_ref):
        zv = zm_ref[...]
        qn = (_rms(zv[:, :Q_LORA])[0] * qg_ref[...]).astype(BF16)
        kvn = (_rms(zv[:, Q_LORA:Q_LORA + KV_LORA])[0] * kvg_ref[...]).astype(BF16)
        qn_ref[...] = qn
        kvn_ref[...] = kvn
        qf = _dot(qn, wq_ref[...])
        kvf = _dot(kvn, wkv_ref[...])
        tables = _rope_tables(pos_ref[...], if_ref[...])
        kr = _rope(zv[:, Q_LORA + KV_LORA:], tables).astype(BF16)
        for h in range(MLA_HEADS):
            lo = h * HEAD_PAD
            q_ref[:, lo:lo + QK_NOPE] = qf[:, lo:lo + QK_NOPE].astype(BF16)
            q_ref[:, lo + QK_NOPE:lo + HEAD_PAD] = _rope(qf[:, lo + QK_NOPE:lo + HEAD_PAD], tables).astype(BF16)
            k_ref[:, lo:lo + QK_NOPE] = kvf[:, h * QK_NOPE:(h + 1) * QK_NOPE].astype(BF16)
            k_ref[:, lo + QK_NOPE:lo + HEAD_PAD] = kr
        v_ref[...] = kvf[:, MLA_HEADS * QK_NOPE:].astype(BF16)

    def rows(n):
        return pl.BlockSpec((tm, n), lambda i: (i, 0))

    return pl.pallas_call(
        body, name="mla_project", grid=(t // tm,),
        in_specs=[rows(ZM_COLS), rows(1), _row(inv_freq), _row(qg), _row(kvg), _row(w_uq), _row(w_ukv)],
        out_specs=[rows(Q_LORA), rows(KV_LORA), rows(QK_COLS), rows(QK_COLS), rows(MLA_WIDTH)],
        out_shape=[jax.ShapeDtypeStruct((t, Q_LORA), BF16), jax.ShapeDtypeStruct((t, KV_LORA), BF16),
                   jax.ShapeDtypeStruct((t, QK_COLS), BF16), jax.ShapeDtypeStruct((t, QK_COLS), BF16),
                   jax.ShapeDtypeStruct((t, MLA_WIDTH), BF16)],
        compiler_params=_params(("arbitrary",)),
    )(zm, pos, inv_freq, qg, kvg, w_uq, w_ukv)


def _chunk_mask(shape, q_axis):
    qi = lax.broadcasted_iota(jnp.int32, shape, q_axis) // CHUNK
    ki = lax.broadcasted_iota(jnp.int32, shape, 1 - q_axis) // CHUNK
    return ki <= qi


def attention_forward(q, k, v):
    t = q.shape[0]
    tq = _tile(t, ATTN_TILE, CHUNK)

    def body(q_ref, k_ref, v_ref, o_ref, lse_ref):
        i = pl.program_id(1)
        qv = q_ref[...]

        def step(kb, carry, masked):
            m, l, acc = carry
            start = pl.multiple_of(kb * tq, tq)
            s = _dot(qv, k_ref[pl.ds(start, tq), :], NT) * ATTN_SCALE
            if masked:
                s = jnp.where(_chunk_mask(s.shape, 0), s, NEG_INF)
            m_new = jnp.maximum(m, jnp.max(s, axis=-1, keepdims=True))
            alpha = jnp.exp(m - m_new)
            p = jnp.exp(s - m_new)
            l = alpha * l + jnp.sum(p, axis=-1, keepdims=True)
            acc = alpha * acc + _dot(p.astype(BF16), v_ref[pl.ds(start, tq), :])
            return m_new, l, acc

        init = (jnp.full((tq, 1), NEG_INF, F32), jnp.zeros((tq, 1), F32), jnp.zeros((tq, V_HEAD), F32))
        carry = lax.fori_loop(0, i, lambda kb, cr: step(kb, cr, False), init)
        m, l, acc = step(i, carry, True)
        o_ref[...] = acc / l
        lse_ref[0] = m + jnp.log(l)

    return pl.pallas_call(
        body, name="attn_fwd", grid=(MLA_HEADS, t // tq),
        in_specs=[pl.BlockSpec((tq, HEAD_PAD), lambda h, i: (i, h)),
                  pl.BlockSpec((t, HEAD_PAD), lambda h, i: (0, h)),
                  pl.BlockSpec((t, V_HEAD), lambda h, i: (0, h))],
        out_specs=[pl.BlockSpec((tq, V_HEAD), lambda h, i: (i, h)),
                   pl.BlockSpec((1, tq, 1), lambda h, i: (h, i, 0))],
        out_shape=[jax.ShapeDtypeStruct((t, MLA_WIDTH), F32), jax.ShapeDtypeStruct((MLA_HEADS, t, 1), F32)],
        compiler_params=_params(("arbitrary", "arbitrary")),
    )(q, k, v)


def attention_backward(q, k, v, do, lse, delta):
    t = q.shape[0]
    tq = _tile(t, ATTN_TILE, CHUNK)
    nq = t // tq

    def body(q_ref, k_ref, v_ref, do_ref, lse_ref, delta_ref, dq_ref, dk_ref, dv_ref):
        kb = pl.program_id(1)

        @pl.when(kb == 0)
        def _():
            dq_ref[...] = jnp.zeros_like(dq_ref)

        kv, vv = k_ref[...], v_ref[...]

        def step(qb, carry, masked):
            dk, dv = carry
            rows = pl.ds(pl.multiple_of(qb * tq, tq), tq)
            qv, dov = q_ref[rows, :], do_ref[rows, :]
            s = _dot(kv, qv, NT) * ATTN_SCALE
            if masked:
                s = jnp.where(_chunk_mask(s.shape, 1), s, NEG_INF)
            p = jnp.exp(s - lse_ref[0, qb])
            dv = dv + _dot(p.astype(BF16), dov)
            dp = _dot(vv, dov, NT)
            ds = (p * (dp - delta_ref[0, qb]) * ATTN_SCALE).astype(BF16)
            dk = dk + _dot(ds, qv)
            dq_ref[rows, :] += _dot(ds, kv, TN)
            return dk, dv

        carry = step(kb, (jnp.zeros((tq, HEAD_PAD), F32), jnp.zeros((tq, V_HEAD), F32)), True)
        dk, dv = lax.fori_loop(kb + 1, nq, lambda qb, cr: step(qb, cr, False), carry)
        dk_ref[...] = dk
        dv_ref[...] = dv

    stat = pl.BlockSpec((1, nq, 1, tq), lambda h, j: (h, 0, 0, 0))
    return pl.pallas_call(
        body, name="attn_bwd", grid=(MLA_HEADS, nq),
        in_specs=[pl.BlockSpec((t, HEAD_PAD), lambda h, j: (0, h)),
                  pl.BlockSpec((tq, HEAD_PAD), lambda h, j: (j, h)),
                  pl.BlockSpec((tq, V_HEAD), lambda h, j: (j, h)),
                  pl.BlockSpec((t, V_HEAD), lambda h, j: (0, h)), stat, stat],
        out_specs=[pl.BlockSpec((t, HEAD_PAD), lambda h, j: (0, h)),
                   pl.BlockSpec((tq, HEAD_PAD), lambda h, j: (j, h)),
                   pl.BlockSpec((tq, V_HEAD), lambda h, j: (j, h))],
        out_shape=[jax.ShapeDtypeStruct((t, QK_COLS), F32), jax.ShapeDtypeStruct((t, QK_COLS), F32),
                   jax.ShapeDtypeStruct((t, MLA_WIDTH), F32)],
        compiler_params=_params(("arbitrary", "arbitrary")),
    )(q, k, v, do, lse, delta)


def _shift_rows(v, prev, n):
    out = pltpu.roll(v, n, 0)
    row = lax.broadcasted_iota(jnp.int32, v.shape, 0)
    for r in range(n):
        out = jnp.where(row == r, prev[8 - n + r:8 - n + r + 1, :], out)
    return out


def _advance_rows(v, nxt, n):
    rows = v.shape[0]
    out = pltpu.roll(v, rows - n, 0)
    row = lax.broadcasted_iota(jnp.int32, v.shape, 0)
    for r in range(n):
        out = jnp.where(row == rows - n + r, nxt[r:r + 1, :], out)
    return out


def _conv_taps(zc, zc_prev, first):
    w = CONV_WIDTH
    u = zc[:, w:2 * w] * zc[:, 2 * w:]
    up = jnp.where(first, 0.0, zc_prev[:, w:2 * w] * zc_prev[:, 2 * w:])
    return u, _shift_rows(u, up, 1), _shift_rows(u, up, 2)


def mix_out_forward(zc, o, conv_w, og, gmat_a, gmat_b, w_out, x, gate):
    t, d = x.shape
    tm = _tile(t, ROW_TILE, 16)
    w = CONV_WIDTH

    def body(zc_ref, zp_ref, o_ref, cw_ref, og_ref, ga_ref, gb_ref, w_ref, x_ref, gate_ref,
             xo_ref, yn_ref, y_ref, ya_ref):
        zc_v = zc_ref[...]
        u, u1, u2 = _conv_taps(zc_v, zp_ref[...], pl.program_id(0) == 0)
        cw = cw_ref[...]
        ya = zc_v[:, :w] * (cw[0:1] * u2 + cw[1:2] * u1 + cw[2:3] * u)
        ya_ref[...] = ya
        ov = o_ref[...]
        ogv = og_ref[...]
        yn_ref[:, :w] = (ya * lax.rsqrt(_group_mean(ya * ya, ga_ref[...]) + EPS) * ogv[:, :w]).astype(BF16)
        yn_ref[:, w:] = (ov * lax.rsqrt(_group_mean(ov * ov, gb_ref[...]) + EPS) * ogv[:, w:]).astype(BF16)
        y = _dot(yn_ref[...], w_ref[...])
        y_ref[...] = y
        xo_ref[...] = x_ref[...] + gate_ref[...] * y

    def rows(n):
        return pl.BlockSpec((tm, n), lambda i: (i, 0))

    prev = pl.BlockSpec((8, ZC_COLS), lambda i: (jnp.maximum(i * (tm // 8) - 1, 0), 0))
    return pl.pallas_call(
        body, name="mix_out_fwd", grid=(t // tm,),
        in_specs=[rows(ZC_COLS), prev, rows(MLA_WIDTH), _row(conv_w), _row(og), _row(gmat_a), _row(gmat_b),
                  _row(w_out), rows(d), _row(gate)],
        out_specs=[rows(d), rows(MIX_WIDTH), rows(d), rows(w)],
        out_shape=[jax.ShapeDtypeStruct((t, d), F32), jax.ShapeDtypeStruct((t, MIX_WIDTH), BF16),
                   jax.ShapeDtypeStruct((t, d), F32), jax.ShapeDtypeStruct((t, w), F32)],
        compiler_params=_params(("arbitrary",)),
    )(zc, zc, o, conv_w, og, gmat_a, gmat_b, w_out, x, gate)


def _group_norm_bwd(dyn, y, og, gmat):
    rs = lax.rsqrt(_group_mean(y * y, gmat) + EPS)
    yhat = y * rs
    d_og = jnp.sum(dyn * yhat, axis=0, keepdims=True)
    dyh = dyn * og
    return rs * (dyh - yhat * _group_mean(dyh * yhat, gmat)), d_og


def mix_out_backward(dxo, y, gate, ya, o, og, gmat_a, gmat_b, w_out):
    t, d = dxo.shape
    tm = _tile(t, ROW_TILE, 16)
    w = CONV_WIDTH

    def body(dxo_ref, y_ref, gate_ref, ya_ref, o_ref, og_ref, ga_ref, gb_ref, w_ref,
             dy_ref, dya_ref, do_ref, delta_ref, sd_ref, so_ref):
        @pl.when(pl.program_id(0) == 0)
        def _():
            sd_ref[...] = jnp.zeros_like(sd_ref)
            so_ref[...] = jnp.zeros_like(so_ref)

        dxo_v = dxo_ref[...]
        dy = (gate_ref[...] * dxo_v).astype(BF16)
        dy_ref[...] = dy
        sd_ref[0:1, :] += jnp.sum(dxo_v * y_ref[...], axis=0, keepdims=True)
        dyn = _dot(dy, w_ref[...], NT)
        ogv = og_ref[...]
        ov = o_ref[...]
        dya, d_og_a = _group_norm_bwd(dyn[:, :w], ya_ref[...], ogv[:, :w], ga_ref[...])
        dov, d_og_b = _group_norm_bwd(dyn[:, w:], ov, ogv[:, w:], gb_ref[...])
        dya_ref[...] = dya
        do_ref[...] = dov.astype(BF16)
        so_ref[0:1, :w] += d_og_a
        so_ref[0:1, w:] += d_og_b
        prod = dov * ov
        for h in range(MLA_HEADS):
            delta_ref[h] = jnp.sum(prod[:, h * V_HEAD:(h + 1) * V_HEAD], axis=-1, keepdims=True)

    def rows(n):
        return pl.BlockSpec((tm, n), lambda i: (i, 0))

    return pl.pallas_call(
        body, name="mix_out_bwd", grid=(t // tm,),
        in_specs=[rows(d), rows(d), _row(gate), rows(w), rows(MLA_WIDTH), _row(og), _row(gmat_a), _row(gmat_b),
                  _row(w_out)],
        out_specs=[rows(d), rows(w), rows(MLA_WIDTH), pl.BlockSpec((MLA_HEADS, tm, 1), lambda i: (0, i, 0)),
                   pl.BlockSpec((8, d), lambda i: (0, 0)), pl.BlockSpec((8, MIX_WIDTH), lambda i: (0, 0))],
        out_shape=[jax.ShapeDtypeStruct((t, d), BF16), jax.ShapeDtypeStruct((t, w), F32),
                   jax.ShapeDtypeStruct((t, MLA_WIDTH), BF16), jax.ShapeDtypeStruct((MLA_HEADS, t, 1), F32),
                   jax.ShapeDtypeStruct((8, d), F32), jax.ShapeDtypeStruct((8, MIX_WIDTH), F32)],
        compiler_params=_params(("arbitrary",)),
    )(dxo, y, gate, ya, o, og, gmat_a, gmat_b, w_out)


def conv_backward(zc, dya, conv_w):
    t = zc.shape[0]
    tm = _tile(t, ROW_TILE, 16)
    nt = t // tm
    w = CONV_WIDTH

    def body(zc_ref, zp_ref, zn_ref, dya_ref, dn_ref, cw_ref, dzc_ref, sums_ref):
        i = pl.program_id(0)

        @pl.when(i == 0)
        def _():
            sums_ref[...] = jnp.zeros_like(sums_ref)

        zc_v = zc_ref[...]
        u, u1, u2 = _conv_taps(zc_v, zp_ref[...], i == 0)
        cw = cw_ref[...]
        dya_v = dya_ref[...]
        dyc = dya_v * zc_v[:, :w]
        dyc_next = jnp.where(i == nt - 1, 0.0, dn_ref[...] * zn_ref[...][:, :w])
        du = cw[2:3] * dyc + cw[1:2] * _advance_rows(dyc, dyc_next, 1) + cw[0:1] * _advance_rows(dyc, dyc_next, 2)
        dzc_ref[:, :w] = (dya_v * (cw[0:1] * u2 + cw[1:2] * u1 + cw[2:3] * u)).astype(BF16)
        dzc_ref[:, w:2 * w] = (du * zc_v[:, 2 * w:]).astype(BF16)
        dzc_ref[:, 2 * w:] = (du * zc_v[:, w:2 * w]).astype(BF16)
        _add_rows(sums_ref, [jnp.sum(dyc * tap, axis=0, keepdims=True) for tap in (u2, u1, u)])

    def rows(n):
        return pl.BlockSpec((tm, n), lambda i: (i, 0))

    def halo(n, step):
        last = t // 8 - 1
        return pl.BlockSpec((8, n), lambda i: (jnp.clip(i * (tm // 8) + step, 0, last), 0))

    return pl.pallas_call(
        body, name="conv_bwd", grid=(nt,),
        in_specs=[rows(ZC_COLS), halo(ZC_COLS, -1), halo(ZC_COLS, tm // 8), rows(w), halo(w, tm // 8), _row(conv_w)],
        out_specs=[rows(ZC_COLS), pl.BlockSpec((8, w), lambda i: (0, 0))],
        out_shape=[jax.ShapeDtypeStruct((t, ZC_COLS), BF16), jax.ShapeDtypeStruct((8, w), F32)],
        compiler_params=_params(("arbitrary",)),
    )(zc, zc, zc, dya, dya, conv_w)


def _rms_bwd(dy, x, g):
    xhat, r = _rms(x)
    d_g = jnp.sum(dy * xhat, axis=0, keepdims=True)
    dxh = dy * g
    return r * (dxh - xhat * jnp.mean(dxh * xhat, axis=-1, keepdims=True)), d_g


def mla_project_backward(dq, dk, dv, zm, pos, inv_freq, qg, kvg, w_uq, w_ukv):
    t = zm.shape[0]
    tm = _tile(t, ROW_TILE, 16)

    def body(dq_ref, dk_ref, dv_ref, zm_ref, pos_ref, if_ref, qg_ref, kvg_ref, wq_ref, wkv_ref,
             dql_ref, dkvl_ref, dzm_ref, sums_ref):
        @pl.when(pl.program_id(0) == 0)
        def _():
            sums_ref[...] = jnp.zeros_like(sums_ref)

        tables = _rope_tables(pos_ref[...], if_ref[...])
        dkr = jnp.zeros((tm, LANES), F32)
        for h in range(MLA_HEADS):
            lo = h * HEAD_PAD
            dql_ref[:, lo:lo + QK_NOPE] = dq_ref[:, lo:lo + QK_NOPE].astype(BF16)
            dql_ref[:, lo + QK_NOPE:lo + HEAD_PAD] = _rope_transposed(
                dq_ref[:, lo + QK_NOPE:lo + HEAD_PAD], tables).astype(BF16)
            dkvl_ref[:, h * QK_NOPE:(h + 1) * QK_NOPE] = dk_ref[:, lo:lo + QK_NOPE].astype(BF16)
            dkr = dkr + dk_ref[:, lo + QK_NOPE:lo + HEAD_PAD]
        dkvl_ref[:, MLA_HEADS * QK_NOPE:] = dv_ref[...].astype(BF16)
        zv = zm_ref[...]
        dqn = _dot(dql_ref[...], wq_ref[...], NT)
        dkvn = _dot(dkvl_ref[...], wkv_ref[...], NT)
        dcq, d_qg = _rms_bwd(dqn, zv[:, :Q_LORA], qg_ref[...])
        dckv, d_kvg = _rms_bwd(dkvn, zv[:, Q_LORA:Q_LORA + KV_LORA], kvg_ref[...])
        dzm_ref[:, :Q_LORA] = dcq.astype(BF16)
        dzm_ref[:, Q_LORA:Q_LORA + KV_LORA] = dckv.astype(BF16)
        dzm_ref[:, Q_LORA + KV_LORA:] = _rope_transposed(dkr, tables).astype(BF16)
        sums_ref[0:1, :Q_LORA] += d_qg
        sums_ref[0:1, Q_LORA:Q_LORA + KV_LORA] += d_kvg

    def rows(n):
        return pl.BlockSpec((tm, n), lambda i: (i, 0))

    return pl.pallas_call(
        body, name="mla_project_bwd", grid=(t // tm,),
        in_specs=[rows(QK_COLS), rows(QK_COLS), rows(MLA_WIDTH), rows(ZM_COLS), rows(1), _row(inv_freq),
                  _row(qg), _row(kvg), _row(w_uq), _row(w_ukv)],
        out_specs=[rows(QK_COLS), rows(QK_COLS), rows(ZM_COLS), pl.BlockSpec((8, ZM_COLS), lambda i: (0, 0))],
        out_shape=[jax.ShapeDtypeStruct((t, QK_COLS), BF16), jax.ShapeDtypeStruct((t, QK_COLS), BF16),
                   jax.ShapeDtypeStruct((t, ZM_COLS), BF16), jax.ShapeDtypeStruct((8, ZM_COLS), F32)],
        compiler_params=_params(("arbitrary",)),
    )(dq, dk, dv, zm, pos, inv_freq, qg, kvg, w_uq, w_ukv)


def mix_in_backward(dzc, dzm, w_in, x, dxo, gn, sc):
    t, d = x.shape
    tm = _tile(t, ROW_TILE, 16)

    def body(dzc_ref, dzm_ref, w_ref, x_ref, dxo_ref, gn_ref, sc_ref, dx_ref, sums_ref):
        @pl.when(pl.program_id(0) == 0)
        def _():
            sums_ref[...] = jnp.zeros_like(sums_ref)

        dh = _dot(dzc_ref[...], w_ref[:, :ZC_COLS], NT) + _dot(dzm_ref[...], w_ref[:, ZC_COLS:], NT)
        dx, d_sh, d_sc, d_gn = _norm_mod_bwd(dh, x_ref[...], gn_ref[...], sc_ref[...])
        dx_ref[...] = dxo_ref[...] + dx
        _add_rows(sums_ref, [d_sh, d_sc, d_gn])

    def rows(n):
        return pl.BlockSpec((tm, n), lambda i: (i, 0))

    return pl.pallas_call(
        body, name="mix_in_bwd", grid=(t // tm,),
        in_specs=[rows(ZC_COLS), rows(ZM_COLS), _row(w_in), rows(d), rows(d), _row(gn), _row(sc)],
        out_specs=[rows(d), pl.BlockSpec((8, d), lambda i: (0, 0))],
        out_shape=[jax.ShapeDtypeStruct((t, d), F32), jax.ShapeDtypeStruct((8, d), F32)],
        compiler_params=_params(("arbitrary",)),
    )(dzc, dzm, w_in, x, dxo, gn, sc)


def final_loss(x, target, g):
    t, d = x.shape
    tm = _tile(t, ROW_TILE, 16)

    def body(x_ref, t_ref, g_ref, dx_ref, sums_ref):
        @pl.when(pl.program_id(0) == 0)
        def _():
            sums_ref[...] = jnp.zeros_like(sums_ref)

        gv = g_ref[...]
        xhat, r = _rms(x_ref[...])
        err = xhat * gv - t_ref[...]
        dy = err * (1.0 / d)
        dxh = dy * gv
        dx_ref[...] = r * (dxh - xhat * jnp.mean(dxh * xhat, axis=-1, keepdims=True))
        _add_rows(sums_ref, [jnp.sum(dy * xhat, axis=0, keepdims=True),
                             jnp.sum(err * err, axis=0, keepdims=True) * (0.5 / d)])

    row = pl.BlockSpec((tm, d), lambda i: (i, 0))
    return pl.pallas_call(
        body, name="final_loss", grid=(t // tm,),
        in_specs=[row, row, _row(g)],
        out_specs=[row, pl.BlockSpec((8, d), lambda i: (0, 0))],
        out_shape=[jax.ShapeDtypeStruct((t, d), F32), jax.ShapeDtypeStruct((8, d), F32)],
        compiler_params=_params(("arbitrary",)),
    )(x, target, g)


def adamw(w, g, m, v, name):
    r, n = w.shape
    tr = _tile(r, max(8, (1 << 19) // n), 8)

    def body(w_ref, g_ref, m_ref, v_ref, d_ref, mo_ref, vo_ref):
        gv = g_ref[...]
        m_new = ADAM_B1 * m_ref[...] + (1.0 - ADAM_B1) * gv
        v_new = ADAM_B2 * v_ref[...] + (1.0 - ADAM_B2) * (gv * gv)
        m_hat = m_new / (1.0 - ADAM_B1 ** ADAM_STEP)
        v_hat = v_new / (1.0 - ADAM_B2 ** ADAM_STEP)
        d_ref[...] = -ADAM_LR * (m_hat / (jnp.sqrt(v_hat) + ADAM_EPS) + ADAM_WD * w_ref[...])
        mo_ref[...] = m_new
        vo_ref[...] = v_new

    blk = pl.BlockSpec((tr, n), lambda i: (i, 0))
    shape = jax.ShapeDtypeStruct((r, n), F32)
    return pl.pallas_call(
        body, name=name, grid=(r // tr,), in_specs=[blk] * 4, out_specs=[blk] * 3, out_shape=[shape] * 3,
        compiler_params=_params(("arbitrary",)),
    )(w, g, m, v)


def _pad_to(v, n):
    return jnp.pad(v, (0, n - v.shape[0]))


def _col_blocks(w):
    k, n8 = w.shape
    return w.reshape(k, N_DEV, n8 // N_DEV).transpose(1, 0, 2).reshape(N_DEV, -1)


def _from_col_blocks(b, k):
    return b.reshape(N_DEV, k, -1).transpose(1, 0, 2).reshape(k, -1)


def kernel(x, c, positions, ada_w, ada_b, norm_ffn1_g, ffn1_w1, ffn1_w3, ffn1_w2, norm_mix_g, w_in, conv_w, q_norm_g, w_uq, kv_norm_g, w_ukv, out_norm_g, w_out, norm_ffn2_g, ffn2_w1, ffn2_w3, ffn2_w2, final_norm_g, loss_target, m_ada_w, m_ada_b, m_norm_ffn1_g, m_ffn1_w1, m_ffn1_w3, m_ffn1_w2, m_norm_mix_g, m_w_in, m_conv_w, m_q_norm_g, m_w_uq, m_kv_norm_g, m_w_ukv, m_out_norm_g, m_w_out, m_norm_ffn2_g, m_ffn2_w1, m_ffn2_w3, m_ffn2_w2, m_final_norm_g, v_ada_w, v_ada_b, v_norm_ffn1_g, v_ffn1_w1, v_ffn1_w3, v_ffn1_w2, v_norm_mix_g, v_w_in, v_conv_w, v_q_norm_g, v_w_uq, v_kv_norm_g, v_w_ukv, v_out_norm_g, v_w_out, v_norm_ffn2_g, v_ffn2_w1, v_ffn2_w3, v_ffn2_w2, v_final_norm_g):
    t, d = x.shape[1], x.shape[2]
    f = ffn1_w2.shape[1] * N_DEV
    me = 4 * lax.axis_index("x") + 2 * lax.axis_index("y") + lax.axis_index("c")
    my_c = lax.axis_index("c")
    my_chip = 2 * lax.axis_index("x") + lax.axis_index("y")
    xs = x[0]
    n_ada = ada_w.shape[2]
    cw_n = conv_w.shape[2]

    small = jnp.concatenate([c[0], conv_w[0].reshape(-1)])
    small_len = -(-small.shape[0] // 1024) * 1024
    small_all = all_gather(_pad_to(small, small_len).reshape(-1, LANES), "gather_inputs", True).reshape(N_DEV, small_len)
    c_all = small_all[:, :d]
    conv_full = small_all[:, d:d + CONV_K * cw_n].reshape(N_DEV, CONV_K, cw_n).transpose(1, 0, 2).reshape(CONV_K, CONV_WIDTH)
    conv_full8 = jnp.pad(conv_full, ((0, 8 - CONV_K), (0, 0)))

    col_sharded = [(ffn1_w1, d), (ffn1_w3, d), (ffn2_w1, d), (ffn2_w3, d), (w_in, d), (w_uq, Q_LORA), (w_ukv, KV_LORA)]
    row_sharded = [ffn1_w2, ffn2_w2, w_out]
    shards = [w[0].reshape(-1) for w, _ in col_sharded] + [w[0].reshape(-1) for w in row_sharded]
    sizes = [s.shape[0] for s in shards]
    total = sum(sizes)
    packed_len = -(-total // (16 * LANES)) * (16 * LANES)
    packed = _pad_to(jnp.concatenate(shards), packed_len).astype(BF16).reshape(-1, LANES)
    gathered = all_gather(packed, "gather_weights", False).reshape(N_DEV, packed_len)
    offs = [sum(sizes[:i]) for i in range(len(sizes))]
    full = []
    for i, (w, k) in enumerate(col_sharded):
        full.append(_from_col_blocks(gathered[:, offs[i]:offs[i] + sizes[i]], k))
    for i, w in enumerate(row_sharded):
        o = offs[len(col_sharded) + i]
        full.append(gathered[:, o:o + sizes[len(col_sharded) + i]].reshape(-1, w.shape[2]))
    w1a, w3a, w1b, w3b, w_in_f, w_uq_f, w_ukv_f, w2a, w2b, w_out_f = full
    w_in_p = jnp.pad(w_in_f, ((0, 0), (0, ZC_COLS + ZM_COLS - IN_COLS)))
    w_uq_p = jnp.pad(w_uq_f.reshape(Q_LORA, MLA_HEADS, QK_NOPE + QK_ROPE),
                     ((0, 0), (0, 0), (0, HEAD_PAD - QK_NOPE - QK_ROPE))).reshape(Q_LORA, QK_COLS)
    w_ukv_p = w_ukv_f.reshape(KV_LORA, MLA_HEADS, 2, QK_NOPE).transpose(0, 2, 1, 3).reshape(KV_LORA, -1)

    ada_b_cols = lax.dynamic_slice_in_dim(ada_b, me * n_ada, n_ada, axis=1)
    mod_cols = ada_forward(c_all, ada_w[0], ada_b_cols)
    mod_len = -(-n_ada // LANES) * LANES
    mod_all = all_gather(jnp.pad(mod_cols, ((0, 0), (0, mod_len - n_ada))), "gather_mod", True)
    mod_mine = lax.dynamic_index_in_dim(mod_all.reshape(N_DEV, N_DEV, mod_len), me, axis=1, keepdims=False)
    mod = mod_mine[:, :n_ada].reshape(N_MOD, 1, d)
    sh1, sc1, g1, sh2, sc2, g2, sh3, sc3, g3 = [mod[i] for i in range(N_MOD)]

    gf = final_norm_g.reshape(1, d)
    x1, h1, a1, b1, y1 = ffn_forward(xs, norm_ffn1_g, sc1, sh1, g1, w1a, w3a, w2a, "ffn1_fwd")
    h2, zc, zm = mix_in_forward(x1, norm_mix_g, sc2, sh2, w_in_p)
    pos = positions[0].astype(F32).reshape(t, 1)
    inv_freq = ROPE_THETA ** (-jnp.arange(0, QK_ROPE, 2, dtype=F32) / QK_ROPE)
    inv_freq = jnp.concatenate([inv_freq, inv_freq, jnp.zeros((LANES - QK_ROPE,), F32)]).reshape(1, LANES)
    qn, kvn, q, k, v = mla_project(zm, pos, inv_freq, q_norm_g, kv_norm_g, w_uq_p, w_ukv_p)
    o, lse = attention_forward(q, k, v)
    lane = jnp.arange(CONV_WIDTH)
    gmat_a = (lane[:, None] // (CONV_WIDTH // CONV_GROUPS) == lane[None, :] // (CONV_WIDTH // CONV_GROUPS))
    gmat_a = (gmat_a / (CONV_WIDTH // CONV_GROUPS)).astype(BF16)
    gmat_b = ((lane[:, None] // V_HEAD == lane[None, :] // V_HEAD) / V_HEAD).astype(BF16)
    x2, yn, y2, ya = mix_out_forward(zc, o, conv_full8, out_norm_g, gmat_a, gmat_b, w_out_f, x1, g2)
    x3, h3, a3, b3, y3 = ffn_forward(x2, norm_ffn2_g, sc3, sh3, g3, w1b, w3b, w2b, "ffn2_fwd")
    dx3, sums_f = final_loss(x3, loss_target[0], gf)

    dx2, da3, db3, u3, dy3, sums_3 = ffn_backward(dx3, x2, a3, b3, y3, norm_ffn2_g, sc3, sh3, g3, w1b, w3b, w2b, "ffn2_bwd")
    g_w1b = matmul_tn(h3, da3, "ffn2_gw1")
    g_w3b = matmul_tn(h3, db3, "ffn2_gw3")
    g_w2b = matmul_tn(u3, dy3, "ffn2_gw2")
    dy2, dya, do, delta, sums_2d, sums_2o = mix_out_backward(dx2, y2, g2, ya, o, out_norm_g, gmat_a, gmat_b, w_out_f)
    g_w_out = matmul_tn(yn, dy2, "gw_out")
    nq = t // _tile(t, ATTN_TILE, CHUNK)
    stat_shape = (MLA_HEADS, nq, 1, t // nq)
    dq, dk, dv = attention_backward(q, k, v, do, lse.reshape(stat_shape), delta.reshape(stat_shape))
    dzc, sums_c = conv_backward(zc, dya, conv_full8)
    dql, dkvl, dzm, sums_m = mla_project_backward(dq, dk, dv, zm, pos, inv_freq, q_norm_g, kv_norm_g, w_uq_p, w_ukv_p)
    g_w_uq_p = matmul_tn(qn, dql, "gw_uq")
    g_w_ukv_p = matmul_tn(kvn, dkvl, "gw_ukv")
    g_w_in = jnp.concatenate([matmul_tn(h2, dzc, "gw_in_conv"), matmul_tn(h2, dzm, "gw_in_mla")], axis=1)[:, :IN_COLS]
    dx1, sums_1m = mix_in_backward(dzc, dzm, w_in_p, x1, dx2, norm_mix_g, sc2)
    dx0, da1, db1, u1, dy1, sums_1 = ffn_backward(dx1, xs, a1, b1, y1, norm_ffn1_g, sc1, sh1, g1, w1a, w3a, w2a, "ffn1_bwd")
    g_w1a = matmul_tn(h1, da1, "ffn1_gw1")
    g_w3a = matmul_tn(h1, db1, "ffn1_gw3")
    g_w2a = matmul_tn(u1, dy1, "ffn1_gw2")
    g_w_uq = g_w_uq_p.reshape(Q_LORA, MLA_HEADS, HEAD_PAD)[:, :, :QK_NOPE + QK_ROPE].reshape(Q_LORA, -1)
    g_w_ukv = g_w_ukv_p.reshape(KV_LORA, 2, MLA_HEADS, QK_NOPE).transpose(0, 2, 1, 3).reshape(KV_LORA, -1)

    dmod = jnp.concatenate([sums_1[0], sums_1[1], sums_1[2], sums_1m[0], sums_1m[1], sums_2d[0],
                            sums_3[0], sums_3[1], sums_3[2]])
    pieces = [dmod, sums_1[3], sums_1m[2], sums_m[0, :Q_LORA], sums_m[0, Q_LORA:Q_LORA + KV_LORA], sums_2o[0],
              sums_3[3], sums_f[0], sums_f[1], sums_c[:CONV_K].reshape(-1)]
    plens = [p.shape[0] for p in pieces]
    poffs = [sum(plens[:i]) for i in range(len(plens))]
    vec_len = -(-sum(plens) // 1024) * 1024
    vec = _pad_to(jnp.concatenate(pieces), vec_len)
    vec_all = all_gather(vec.reshape(-1, LANES), "gather_sums", True).reshape(N_DEV, vec_len)
    tot = sum_devices(vec_all)[0]
    g_ada_b, g_n1, g_nmix, g_qg, g_kvg, g_og, g_n3, g_gf, loss_lanes, g_conv_full = [
        tot[o:o + n] for o, n in zip(poffs, plens)]
    loss = sum_lanes(loss_lanes.reshape(1, d))[0, 0]
    g_conv = lax.dynamic_slice_in_dim(g_conv_full.reshape(CONV_K, CONV_WIDTH), me * cw_n, cw_n, axis=1)
    dmod_all = vec_all[:, :N_MOD * d]
    dmod_cols = lax.dynamic_slice_in_dim(dmod_all, me * n_ada, n_ada, axis=1)
    g_ada_w = ada_backward(jnp.pad(c_all, ((0, 8), (0, 0))), jnp.pad(dmod_cols, ((0, 8), (0, 0))))

    col_grads = [g_w1a, g_w3a, g_w1b, g_w3b, g_w_in, g_w_uq, g_w_ukv]
    row_grads = [g_w2a, g_w2b, g_w_out]
    blocks = [_col_blocks(g) for g in col_grads] + [g.reshape(N_DEV, -1) for g in row_grads]
    g8 = jnp.pad(jnp.concatenate(blocks, axis=1), ((0, 0), (0, packed_len - total))).reshape(N_DEV, -1, LANES)
    got_sib = exchange_sibling(g8)
    rel = jnp.array([2, 1, 3], jnp.int32)
    own_chip = jnp.reshape(my_chip, (1,)).astype(jnp.int32)
    p_own = add_pairs(g8, got_sib, 2 * own_chip + my_c, own_chip, F32, "rs_add_own")
    other = jnp.bitwise_xor(my_chip, rel).astype(jnp.int32)
    p_send = add_pairs(g8, got_sib, 2 * other + my_c, other, BF16, "rs_add_send")
    got_chips = exchange_chips(p_send)
    g_flat = add_received(p_own, got_chips).reshape(-1)
    g_sh = [g_flat[o:o + n] for o, n in zip(offs, sizes)]

    def update(name, w, g, m, v):
        shape = w.shape
        two_d = (-1, shape[-1])
        dlt, nm, nv = adamw(w.reshape(two_d), g.reshape(two_d), m.reshape(two_d), v.reshape(two_d), "adamw_" + name)
        return g.reshape(shape), dlt.reshape(shape), nm.reshape(shape), nv.reshape(shape)

    res = {}
    res["ada_w"] = update("ada_w", ada_w, g_ada_w, m_ada_w, v_ada_w)
    big = [("ffn1_w1", ffn1_w1, m_ffn1_w1, v_ffn1_w1), ("ffn1_w3", ffn1_w3, m_ffn1_w3, v_ffn1_w3),
           ("ffn2_w1", ffn2_w1, m_ffn2_w1, v_ffn2_w1), ("ffn2_w3", ffn2_w3, m_ffn2_w3, v_ffn2_w3),
           ("w_in", w_in, m_w_in, v_w_in), ("w_uq", w_uq, m_w_uq, v_w_uq), ("w_ukv", w_ukv, m_w_ukv, v_w_ukv),
           ("ffn1_w2", ffn1_w2, m_ffn1_w2, v_ffn1_w2), ("ffn2_w2", ffn2_w2, m_ffn2_w2, v_ffn2_w2),
           ("w_out", w_out, m_w_out, v_w_out)]
    for (name, w, m, v), g in zip(big, g_sh):
        res[name] = update(name, w, g, m, v)
    smalls = [("ada_b", ada_b, g_ada_b, m_ada_b, v_ada_b),
              ("norm_ffn1_g", norm_ffn1_g, g_n1, m_norm_ffn1_g, v_norm_ffn1_g),
              ("norm_mix_g", norm_mix_g, g_nmix, m_norm_mix_g, v_norm_mix_g),
              ("conv_w", conv_w, g_conv, m_conv_w, v_conv_w),
              ("q_norm_g", q_norm_g, g_qg, m_q_norm_g, v_q_norm_g),
              ("kv_norm_g", kv_norm_g, g_kvg, m_kv_norm_g, v_kv_norm_g),
              ("out_norm_g", out_norm_g, g_og, m_out_norm_g, v_out_norm_g),
              ("norm_ffn2_g", norm_ffn2_g, g_n3, m_norm_ffn2_g, v_norm_ffn2_g),
              ("final_norm_g", final_norm_g, g_gf, m_final_norm_g, v_final_norm_g)]
    slens = [w.size for _, w, _, _, _ in smalls]
    soffs = [sum(slens[:i]) for i in range(len(slens))]
    s_len = -(-sum(slens) // 1024) * 1024

    def pack_small(i):
        return _pad_to(jnp.concatenate([s[i].reshape(-1) for s in smalls]), s_len).reshape(8, -1)

    s_out = adamw(pack_small(1), pack_small(2), pack_small(3), pack_small(4), "adamw_small")
    for (name, w, g, _, _), o, n in zip(smalls, soffs, slens):
        res[name] = (g.reshape(w.shape),) + tuple(a.reshape(-1)[o:o + n].reshape(w.shape) for a in s_out)

    order = ["ada_w", "ada_b", "norm_ffn1_g", "ffn1_w1", "ffn1_w3", "ffn1_w2", "norm_mix_g", "w_in", "conv_w",
             "q_norm_g", "w_uq", "kv_norm_g", "w_ukv", "out_norm_g", "w_out", "norm_ffn2_g", "ffn2_w1", "ffn2_w3",
             "ffn2_w2", "final_norm_g"]
    return (loss, dx0.reshape(x.shape), *[res[n][0] for n in order], *[res[n][1] for n in order],
            *[res[n][2] for n in order], *[res[n][3] for n in order])
```

```python
import functools

import jax
import jax.numpy as jnp
from jax import lax
from jax.experimental import pallas as pl
from jax.experimental.pallas import tpu as pltpu

F32 = jnp.float32
BF16 = jnp.bfloat16
MESH_ID = pl.DeviceIdType.MESH
N_DEV = 8

EPS = 1e-6
CHUNK = 64
N_MOD = 9
CONV_WIDTH = 512
CONV_GROUPS = 8
CONV_K = 3
MLA_HEADS = 4
QK_NOPE = 128
QK_ROPE = 64
V_HEAD = 128
Q_LORA = 384
KV_LORA = 256
ROPE_THETA = 10000.0
MLA_WIDTH = MLA_HEADS * V_HEAD
MIX_WIDTH = CONV_WIDTH + MLA_WIDTH
IN_COLS = 3 * CONV_WIDTH + Q_LORA + KV_LORA + QK_ROPE
ZC_COLS = 3 * CONV_WIDTH
ZM_COLS = Q_LORA + KV_LORA + 128
HEAD_PAD = 256
QK_COLS = MLA_HEADS * HEAD_PAD
ATTN_SCALE = (QK_NOPE + QK_ROPE) ** -0.5
NEG_INF = -1e30

ADAM_LR = 0.001
ADAM_B1 = 0.9
ADAM_B2 = 0.999
ADAM_EPS = 1e-08
ADAM_WD = 0.01
ADAM_STEP = 10

LANES = 128
VMEM_LIMIT = 56 * 1024 * 1024
ROW_TILE = 512
FFN_FWD_ROWS = 1024
FFN_COLS = 256
GRAD_TILE = 1408
ATTN_TILE = 512

NN = (((1,), (0,)), ((), ()))
NT = (((1,), (1,)), ((), ()))
TN = (((0,), (0,)), ((), ()))


def _dot(a, b, dims=NN):
    return lax.dot_general(a, b, dims, preferred_element_type=F32)


def _tile(n, cap, mult=LANES):
    best = None
    for t in range(mult, min(n, cap) + 1, mult):
        if n % t == 0:
            best = t
    return n if best is None else best


def _params(sem=None):
    return pltpu.CompilerParams(dimension_semantics=sem, vmem_limit_bytes=VMEM_LIMIT)


def _row(v):
    return pl.BlockSpec(v.shape, lambda *_: (0,) * v.ndim)


def _rms(x):
    r = lax.rsqrt(jnp.mean(x * x, axis=-1, keepdims=True) + EPS)
    return x * r, r


def _norm_mod_bwd(dh, x, gn, sc):
    xhat, r = _rms(x)
    d_sh = jnp.sum(dh, axis=0, keepdims=True)
    d_sc = jnp.sum(dh * (xhat * gn), axis=0, keepdims=True)
    dxn = dh * (1.0 + sc)
    d_gn = jnp.sum(dxn * xhat, axis=0, keepdims=True)
    dxh = dxn * gn
    dx = r * (dxh - xhat * jnp.mean(dxh * xhat, axis=-1, keepdims=True))
    return dx, d_sh, d_sc, d_gn


def _group_mean(v, gmat):
    hi = v.astype(BF16)
    lo = (v - hi.astype(F32)).astype(BF16)
    return _dot(hi, gmat) + _dot(lo, gmat)


def _add_rows(ref, rows):
    for r, v in enumerate(rows):
        ref[r:r + 1, :] += v


def _window(ref, axis, j):
    return ref.at[(slice(None),) * axis + (j,)]


def _any_specs(n):
    return [pl.BlockSpec(memory_space=pl.ANY)] * n


def all_gather(blocks, axes, name):
    n_arr = len(blocks)

    def body(*refs):
        ins, outs = refs[:n_arr], refs[n_arr:2 * n_arr]
        send_sems, recv_sems, local_sems = refs[2 * n_arr:]
        x, y, c = lax.axis_index("x"), lax.axis_index("y"), lax.axis_index("c")
        me, sibling = (x, y, c), (x, y, 1 - c)
        chips = [(1 - x, y), (x, 1 - y), (1 - x, 1 - y)]

        def slot(a, px, py, pc):
            return _window(outs[a], axes[a], 4 * px + 2 * py + pc)

        def copy(a, k, block, to, src=None):
            return pltpu.make_async_remote_copy(
                src_ref=slot(a, *block) if src is None else src, dst_ref=slot(a, *block),
                send_sem=send_sems.at[k, a], recv_sem=recv_sems.at[k, a], device_id=to, device_id_type=MESH_ID)

        arrays = range(n_arr)
        mine = [pltpu.make_async_copy(ins[a], slot(a, *me), local_sems.at[a]) for a in arrays]
        for cp in mine:
            cp.start()
        first = [copy(a, 0, me, sibling, src=ins[a]) for a in arrays]
        first += [copy(a, 1 + j, me, (*chip, c), src=ins[a]) for j, chip in enumerate(chips) for a in arrays]
        for cp in first:
            cp.start()
        passed = []
        for j, chip in enumerate(chips):
            for a in arrays:
                copy(a, 1 + j, (*chip, c), me).wait_recv()
                passed.append(copy(a, 4 + j, (*chip, c), sibling))
                passed[-1].start()
        for a in arrays:
            copy(a, 0, sibling, me).wait_recv()
        for j, chip in enumerate(chips):
            for a in arrays:
                copy(a, 4 + j, (*chip, 1 - c), me).wait_recv()
        for cp in first + passed:
            cp.wait_send()
        for cp in mine:
            cp.wait()

    def gathered(b, axis):
        return jax.ShapeDtypeStruct(b.shape[:axis] + (N_DEV,) + b.shape[axis:], b.dtype)

    return pl.pallas_call(
        body, name=name,
        out_shape=[gathered(b, ax) for b, ax in zip(blocks, axes)],
        in_specs=_any_specs(n_arr), out_specs=_any_specs(n_arr),
        scratch_shapes=[pltpu.SemaphoreType.DMA((7, n_arr)), pltpu.SemaphoreType.DMA((7, n_arr)),
                        pltpu.SemaphoreType.DMA((n_arr,))],
    )(*blocks)


def exchange_sibling(grads):
    n_arr = len(grads)

    def body(*refs):
        ins, outs = refs[:n_arr], refs[n_arr:2 * n_arr]
        send_sems, recv_sems = refs[2 * n_arr:]
        x, y, c = lax.axis_index("x"), lax.axis_index("y"), lax.axis_index("c")

        def copy(a, src, dst):
            return pltpu.make_async_remote_copy(
                src_ref=src, dst_ref=dst, send_sem=send_sems.at[a], recv_sem=recv_sems.at[a],
                device_id=(x, y, 1 - c), device_id_type=MESH_ID)

        for a in range(n_arr):
            for k in range(4):
                copy(a, ins[a].at[2 * k + (1 - c)], outs[a].at[k]).start()
        whole = [copy(a, ins[a].at[pl.ds(0, 4)], outs[a]) for a in range(n_arr)]
        for cp in whole:
            cp.wait_recv()
        for cp in whole:
            cp.wait_send()

    return pl.pallas_call(
        body, name="rs_sibling",
        out_shape=[jax.ShapeDtypeStruct((4,) + g.shape[1:], g.dtype) for g in grads],
        in_specs=_any_specs(n_arr), out_specs=_any_specs(n_arr),
        scratch_shapes=[pltpu.SemaphoreType.DMA((n_arr,)), pltpu.SemaphoreType.DMA((n_arr,))],
    )(*grads)


def exchange_chips(parts):
    n_arr = len(parts)

    def body(*refs):
        ins, outs = refs[:n_arr], refs[n_arr:2 * n_arr]
        send_sems, recv_sems = refs[2 * n_arr:]
        x, y, c = lax.axis_index("x"), lax.axis_index("y"), lax.axis_index("c")
        chips = [(1 - x, y), (x, 1 - y), (1 - x, 1 - y)]

        def copy(a, src, dst, chip):
            return pltpu.make_async_remote_copy(
                src_ref=src, dst_ref=dst, send_sem=send_sems.at[a], recv_sem=recv_sems.at[a],
                device_id=(*chip, c), device_id_type=MESH_ID)

        for a in range(n_arr):
            for j, chip in enumerate(chips):
                copy(a, ins[a].at[j], outs[a].at[j], chip).start()
        whole = [copy(a, ins[a], outs[a], chips[0]) for a in range(n_arr)]
        for cp in whole:
            cp.wait_recv()
        for cp in whole:
            cp.wait_send()

    return pl.pallas_call(
        body, name="rs_chips",
        out_shape=[jax.ShapeDtypeStruct(p.shape, p.dtype) for p in parts],
        in_specs=_any_specs(n_arr), out_specs=_any_specs(n_arr),
        scratch_shapes=[pltpu.SemaphoreType.DMA((n_arr,)), pltpu.SemaphoreType.DMA((n_arr,))],
    )(*parts)


def add_sibling(g8, got, src_idx, chip_idx, name):
    _, r, n = g8.shape
    tr = _tile(r, 256, 16)

    def body(si_ref, ci_ref, g0_ref, g1_ref, g2_ref, g3_ref, got_ref, own_ref, send_ref):
        own_ref[...] = g0_ref[0] + got_ref[ci_ref[0]]
        for j, g_ref in enumerate((g1_ref, g2_ref, g3_ref)):
            send_ref[j] = (g_ref[0] + got_ref[ci_ref[j + 1]]).astype(BF16)

    def mine(j):
        return pl.BlockSpec((1, tr, n), lambda i, si, ci: (si[j], i, 0))

    return pl.pallas_call(
        body, name=name,
        out_shape=[jax.ShapeDtypeStruct((r, n), F32), jax.ShapeDtypeStruct((3, r, n), BF16)],
        grid_spec=pltpu.PrefetchScalarGridSpec(
            num_scalar_prefetch=2, grid=(r // tr,),
            in_specs=[mine(0), mine(1), mine(2), mine(3), pl.BlockSpec((4, tr, n), lambda i, si, ci: (0, i, 0))],
            out_specs=[pl.BlockSpec((tr, n), lambda i, si, ci: (i, 0)),
                       pl.BlockSpec((3, tr, n), lambda i, si, ci: (0, i, 0))]),
        compiler_params=_params(("arbitrary",)),
    )(src_idx, chip_idx, g8, g8, g8, g8, got)


def add_received(own, got, name):
    r, n = own.shape
    tr = _tile(r, 256, 16)

    def body(a_ref, b_ref, o_ref):
        acc = a_ref[...]
        for j in range(3):
            acc = acc + b_ref[j].astype(F32)
        o_ref[...] = acc

    return pl.pallas_call(
        body, name=name,
        out_shape=jax.ShapeDtypeStruct((r, n), F32),
        grid=(r // tr,),
        in_specs=[pl.BlockSpec((tr, n), lambda i: (i, 0)), pl.BlockSpec((3, tr, n), lambda i: (0, i, 0))],
        out_specs=pl.BlockSpec((tr, n), lambda i: (i, 0)),
        compiler_params=_params(("arbitrary",)),
    )(own, got)


def sum_devices(g):
    def body(g_ref, o_ref):
        acc = g_ref[0]
        for j in range(1, N_DEV):
            acc = acc + g_ref[j]
        o_ref[...] = acc

    return pl.pallas_call(body, name="sum_devices", out_shape=jax.ShapeDtypeStruct(g.shape[1:], F32))(g)


def sum_lanes(v):
    def body(v_ref, o_ref):
        o_ref[...] = jnp.broadcast_to(jnp.sum(v_ref[...], axis=-1, keepdims=True), (1, LANES))

    return pl.pallas_call(body, name="sum_lanes", out_shape=jax.ShapeDtypeStruct((1, LANES), F32))(v)


def ada_forward(c_all, ada_w, ada_b_cols):
    nb, n = c_all.shape[0], ada_w.shape[1]

    def body(c_ref, w_ref, b_ref, o_ref):
        cv = c_ref[...]
        s = (cv * jax.nn.sigmoid(cv)).astype(BF16)
        o_ref[...] = _dot(s, w_ref[...].astype(BF16)) + b_ref[...]

    return pl.pallas_call(body, name="ada_fwd", out_shape=jax.ShapeDtypeStruct((nb, n), F32),
                          compiler_params=_params())(c_all, ada_w, ada_b_cols)


def ada_backward(c_all16, dmod16):
    d, n = c_all16.shape[1], dmod16.shape[1]

    def body(c_ref, g_ref, o_ref):
        cv = c_ref[...]
        s = (cv * jax.nn.sigmoid(cv)).astype(BF16)
        o_ref[...] = _dot(s, g_ref[...].astype(BF16), TN)

    return pl.pallas_call(body, name="ada_bwd", out_shape=jax.ShapeDtypeStruct((d, n), F32),
                          compiler_params=_params())(c_all16, dmod16)


def ffn_forward(x, gn, sc, sh, gate, ws, first, name):
    t, d = x.shape
    f = ws.shape[1]
    tm, tf = _tile(t, FFN_FWD_ROWS, 16), _tile(f, FFN_COLS)
    nf = f // tf

    def body(x_ref, gn_ref, sc_ref, sh_ref, gate_ref, w1_ref, w3_ref, w2_ref,
             xo_ref, h_ref, a_ref, b_ref, y_ref, hs, acc):
        j = pl.program_id(1)

        @pl.when(j == 0)
        def _():
            xhat, _ = _rms(x_ref[...])
            h = (xhat * gn_ref[...] * (1.0 + sc_ref[...]) + sh_ref[...]).astype(BF16)
            hs[...] = h
            h_ref[...] = h
            acc[...] = jnp.zeros_like(acc)

        h = hs[...]
        a = _dot(h, w1_ref[...], NT)
        b = _dot(h, w3_ref[...], NT)
        a_ref[...] = a.astype(BF16)
        b_ref[...] = b.astype(BF16)
        u = (a * jax.nn.sigmoid(a) * b).astype(BF16)
        acc[...] += _dot(u, w2_ref[...])

        @pl.when(j == nf - 1)
        def _():
            y = acc[...]
            y_ref[...] = y
            xo_ref[...] = x_ref[...] + 0.5 * gate_ref[...] * y

    row = pl.BlockSpec((tm, d), lambda i, j: (i, 0))
    vec = pl.BlockSpec((1, d), lambda i, j: (0, 0))
    wide = pl.BlockSpec((tm, tf), lambda i, j: (i, j))
    return pl.pallas_call(
        body, name=name, grid=(t // tm, nf),
        in_specs=[row, vec, vec, vec, vec] + _ffn_weight_specs(first, tf, d),
        out_specs=[row, row, wide, wide, row],
        out_shape=[jax.ShapeDtypeStruct((t, d), F32), jax.ShapeDtypeStruct((t, d), BF16),
                   jax.ShapeDtypeStruct((t, f), BF16), jax.ShapeDtypeStruct((t, f), BF16),
                   jax.ShapeDtypeStruct((t, d), F32)],
        scratch_shapes=[pltpu.VMEM((tm, d), BF16), pltpu.VMEM((tm, d), F32)],
        compiler_params=_params(("arbitrary", "arbitrary")),
    )(x, gn, sc, sh, gate, ws, ws, ws)


def _ffn_weight_specs(first, tf, d):
    return [pl.BlockSpec((None, tf, d), lambda i, j, w=first + k: (w, j, 0)) for k in range(3)]


def ffn_backward(dxo, x, a, b, y, gn, sc, sh, gate, ws, first, name):
    t, d = x.shape
    f = ws.shape[1]
    tm, tf = _tile(t, ROW_TILE, 16), _tile(f, FFN_COLS)
    nf = f // tf

    def body(dxo_ref, x_ref, a_ref, b_ref, y_ref, gn_ref, sc_ref, sh_ref, gate_ref, w1_ref, w3_ref, w2_ref,
             dx_ref, da_ref, db_ref, u_ref, dy_ref, sums_ref, dys, acc):
        i, j = pl.program_id(0), pl.program_id(1)

        @pl.when(jnp.logical_and(i == 0, j == 0))
        def _():
            sums_ref[...] = jnp.zeros_like(sums_ref)

        @pl.when(j == 0)
        def _():
            dy = (0.5 * gate_ref[...] * dxo_ref[...]).astype(BF16)
            dys[...] = dy
            dy_ref[...] = dy
            acc[...] = jnp.zeros_like(acc)

        du = _dot(dys[...], w2_ref[...], NT)
        av = a_ref[...].astype(F32)
        bv = b_ref[...].astype(F32)
        s = jax.nn.sigmoid(av)
        sa = av * s
        da = (du * bv * (s * (1.0 + av * (1.0 - s)))).astype(BF16)
        db = (du * sa).astype(BF16)
        da_ref[...] = da
        db_ref[...] = db
        u_ref[...] = (sa * bv).astype(BF16)
        acc[...] += _dot(da, w1_ref[...]) + _dot(db, w3_ref[...])

        @pl.when(j == nf - 1)
        def _():
            dxo_v = dxo_ref[...]
            dx, d_sh, d_sc, d_gn = _norm_mod_bwd(acc[...], x_ref[...], gn_ref[...], sc_ref[...])
            dx_ref[...] = dxo_v + dx
            d_gate = jnp.sum(dxo_v * (0.5 * y_ref[...]), axis=0, keepdims=True)
            _add_rows(sums_ref, [d_sh, d_sc, d_gate, d_gn])

    row = pl.BlockSpec((tm, d), lambda i, j: (i, 0))
    vec = pl.BlockSpec((1, d), lambda i, j: (0, 0))
    wide = pl.BlockSpec((tm, tf), lambda i, j: (i, j))
    return pl.pallas_call(
        body, name=name, grid=(t // tm, nf),
        in_specs=[row, row, wide, wide, row, vec, vec, vec, vec] + _ffn_weight_specs(first, tf, d),
        out_specs=[row, wide, wide, wide, row, pl.BlockSpec((8, d), lambda i, j: (0, 0))],
        out_shape=[jax.ShapeDtypeStruct((t, d), F32), jax.ShapeDtypeStruct((t, f), BF16),
                   jax.ShapeDtypeStruct((t, f), BF16), jax.ShapeDtypeStruct((t, f), BF16),
                   jax.ShapeDtypeStruct((t, d), BF16), jax.ShapeDtypeStruct((8, d), F32)],
        scratch_shapes=[pltpu.VMEM((tm, d), BF16), pltpu.VMEM((tm, d), F32)],
        compiler_params=_params(("arbitrary", "arbitrary")),
    )(dxo, x, a, b, y, gn, sc, sh, gate, ws, ws, ws)


def matmul_tn(a, b, name):
    t, m = a.shape
    n = b.shape[1]
    tm, tn, tk = _tile(m, GRAD_TILE), _tile(n, GRAD_TILE), _tile(t, 1024, 16)
    nk = t // tk

    def body(a_ref, b_ref, o_ref, acc):
        k = pl.program_id(2)

        @pl.when(k == 0)
        def _():
            acc[...] = jnp.zeros_like(acc)

        acc[...] += _dot(a_ref[...], b_ref[...], TN)

        @pl.when(k == nk - 1)
        def _():
            o_ref[...] = acc[...]

    return pl.pallas_call(
        body, name=name, grid=(m // tm, n // tn, nk),
        in_specs=[pl.BlockSpec((tk, tm), lambda i, j, k: (k, i)), pl.BlockSpec((tk, tn), lambda i, j, k: (k, j))],
        out_specs=pl.BlockSpec((tm, tn), lambda i, j, k: (i, j)),
        out_shape=jax.ShapeDtypeStruct((m, n), F32),
        scratch_shapes=[pltpu.VMEM((tm, tn), F32)],
        compiler_params=_params(("arbitrary", "arbitrary", "arbitrary")),
    )(a, b)


def mix_in_forward(x, gn, sc, sh, w_in):
    t, d = x.shape
    tm = _tile(t, ROW_TILE, 16)

    def body(x_ref, gn_ref, sc_ref, sh_ref, w_ref, h_ref, zc_ref, zm_ref):
        xhat, _ = _rms(x_ref[...])
        h = (xhat * gn_ref[...] * (1.0 + sc_ref[...]) + sh_ref[...]).astype(BF16)
        h_ref[...] = h
        z = _dot(h, w_ref[...], NT)
        zc_ref[...] = z[:, :ZC_COLS]
        zm_ref[...] = z[:, ZC_COLS:]

    row = pl.BlockSpec((tm, d), lambda i: (i, 0))
    vec = pl.BlockSpec((1, d), lambda i: (0, 0))
    return pl.pallas_call(
        body, name="mix_in_fwd", grid=(t // tm,),
        in_specs=[row, vec, vec, vec, _row(w_in)],
        out_specs=[row, pl.BlockSpec((tm, ZC_COLS), lambda i: (i, 0)), pl.BlockSpec((tm, ZM_COLS), lambda i: (i, 0))],
        out_shape=[jax.ShapeDtypeStruct((t, d), BF16), jax.ShapeDtypeStruct((t, ZC_COLS), F32),
                   jax.ShapeDtypeStruct((t, ZM_COLS), F32)],
        compiler_params=_params(("arbitrary",)),
    )(x, gn, sc, sh, w_in)


def _rope_tables(pos, inv_freq):
    ang = pos * inv_freq
    lane = lax.broadcasted_iota(jnp.int32, ang.shape, 1)
    cos, sin = jnp.cos(ang), jnp.sin(ang)
    half = QK_ROPE // 2
    return cos, jnp.where(lane < half, -sin, 0.0), jnp.where(jnp.logical_and(lane >= half, lane < QK_ROPE), sin, 0.0)


def _rope(v, tables):
    cos, sin_a, sin_b = tables
    return v * cos + pltpu.roll(v, LANES - QK_ROPE // 2, 1) * sin_a + pltpu.roll(v, QK_ROPE // 2, 1) * sin_b


def _rope_transposed(dv, tables):
    cos, sin_a, sin_b = tables
    return dv * cos + pltpu.roll(dv * sin_a, QK_ROPE // 2, 1) + pltpu.roll(dv * sin_b, LANES - QK_ROPE // 2, 1)


def mla_project(zm, pos, inv_freq, qg, kvg, w_uq, w_ukv):
    t = zm.shape[0]
    tm = _tile(t, ROW_TILE, 16)

    def body(zm_ref, pos_ref, if_ref, qg_ref, kvg_ref, wq_ref, wkv_ref, qn_ref, kvn_ref, q_ref, k_ref, v_ref):
        zv = zm_ref[...]
        qn = (_rms(zv[:, :Q_LORA])[0] * qg_ref[...]).astype(BF16)
        kvn = (_rms(zv[:, Q_LORA:Q_LORA + KV_LORA])[0] * kvg_ref[...]).astype(BF16)
        qn_ref[...] = qn
        kvn_ref[...] = kvn
        qf = _dot(qn, wq_ref[...], NT)
        kvf = _dot(kvn, wkv_ref[...], NT)
        tables = _rope_tables(pos_ref[...], if_ref[...])
        kr = _rope(zv[:, Q_LORA + KV_LORA:], tables).astype(BF16)
        for h in range(MLA_HEADS):
            lo = h * HEAD_PAD
            q_ref[:, lo:lo + QK_NOPE] = qf[:, lo:lo + QK_NOPE].astype(BF16)
            q_ref[:, lo + QK_NOPE:lo + HEAD_PAD] = _rope(qf[:, lo + QK_NOPE:lo + HEAD_PAD], tables).astype(BF16)
            k_ref[:, lo:lo + QK_NOPE] = kvf[:, h * QK_NOPE:(h + 1) * QK_NOPE].astype(BF16)
            k_ref[:, lo + QK_NOPE:lo + HEAD_PAD] = kr
        v_ref[...] = kvf[:, MLA_HEADS * QK_NOPE:].astype(BF16)

    def rows(n):
        return pl.BlockSpec((tm, n), lambda i: (i, 0))

    return pl.pallas_call(
        body, name="mla_project", grid=(t // tm,),
        in_specs=[rows(ZM_COLS), rows(1), _row(inv_freq), _row(qg), _row(kvg), _row(w_uq), _row(w_ukv)],
        out_specs=[rows(Q_LORA), rows(KV_LORA), rows(QK_COLS), rows(QK_COLS), rows(MLA_WIDTH)],
        out_shape=[jax.ShapeDtypeStruct((t, Q_LORA), BF16), jax.ShapeDtypeStruct((t, KV_LORA), BF16),
                   jax.ShapeDtypeStruct((t, QK_COLS), BF16), jax.ShapeDtypeStruct((t, QK_COLS), BF16),
                   jax.ShapeDtypeStruct((t, MLA_WIDTH), BF16)],
        compiler_params=_params(("arbitrary",)),
    )(zm, pos, inv_freq, qg, kvg, w_uq, w_ukv)


def _chunk_mask(shape, q_axis):
    qi = lax.broadcasted_iota(jnp.int32, shape, q_axis) // CHUNK
    ki = lax.broadcasted_iota(jnp.int32, shape, 1 - q_axis) // CHUNK
    return ki <= qi


def attention_forward(q, k, v):
    t = q.shape[0]
    tq = _tile(t, ATTN_TILE, CHUNK)

    def body(q_ref, k_ref, v_ref, o_ref, lse_ref):
        i = pl.program_id(1)
        qv = q_ref[...]

        def step(kb, carry, masked):
            m, l, acc = carry
            start = pl.multiple_of(kb * tq, tq)
            s = _dot(qv, k_ref[pl.ds(start, tq), :], NT) * ATTN_SCALE
            if masked:
                s = jnp.where(_chunk_mask(s.shape, 0), s, NEG_INF)
            m_new = jnp.maximum(m, jnp.max(s, axis=-1, keepdims=True))
            alpha = jnp.exp(m - m_new)
            p = jnp.exp(s - m_new)
            l = alpha * l + jnp.sum(p, axis=-1, keepdims=True)
            acc = alpha * acc + _dot(p.astype(BF16), v_ref[pl.ds(start, tq), :])
            return m_new, l, acc

        init = (jnp.full((tq, 1), NEG_INF, F32), jnp.zeros((tq, 1), F32), jnp.zeros((tq, V_HEAD), F32))
        carry = lax.fori_loop(0, i, lambda kb, cr: step(kb, cr, False), init)
        m, l, acc = step(i, carry, True)
        o_ref[...] = acc / l
        lse_ref[0] = m + jnp.log(l)

    return pl.pallas_call(
        body, name="attn_fwd", grid=(MLA_HEADS, t // tq),
        in_specs=[pl.BlockSpec((tq, HEAD_PAD), lambda h, i: (i, h)),
                  pl.BlockSpec((t, HEAD_PAD), lambda h, i: (0, h)),
                  pl.BlockSpec((t, V_HEAD), lambda h, i: (0, h))],
        out_specs=[pl.BlockSpec((tq, V_HEAD), lambda h, i: (i, h)),
                   pl.BlockSpec((1, tq, 1), lambda h, i: (h, i, 0))],
        out_shape=[jax.ShapeDtypeStruct((t, MLA_WIDTH), F32), jax.ShapeDtypeStruct((MLA_HEADS, t, 1), F32)],
        compiler_params=_params(("arbitrary", "arbitrary")),
    )(q, k, v)


def attention_backward(q, k, v, do, lse, delta):
    t = q.shape[0]
    tq = _tile(t, ATTN_TILE, CHUNK)
    nq = t // tq

    def body(q_ref, k_ref, v_ref, do_ref, lse_ref, delta_ref, dq_ref, dk_ref, dv_ref):
        kb = pl.program_id(1)

        @pl.when(kb == 0)
        def _():
            dq_ref[...] = jnp.zeros_like(dq_ref)

        kv, vv = k_ref[...], v_ref[...]

        def step(qb, carry, masked):
            dk, dv = carry
            rows = pl.ds(pl.multiple_of(qb * tq, tq), tq)
            qv, dov = q_ref[rows, :], do_ref[rows, :]
            s = _dot(kv, qv, NT) * ATTN_SCALE
            if masked:
                s = jnp.where(_chunk_mask(s.shape, 1), s, NEG_INF)
            p = jnp.exp(s - lse_ref[0, qb])
            dv = dv + _dot(p.astype(BF16), dov)
            dp = _dot(vv, dov, NT)
            ds = (p * (dp - delta_ref[0, qb]) * ATTN_SCALE).astype(BF16)
            dk = dk + _dot(ds, qv)
            dq_ref[rows, :] += _dot(ds, kv, TN)
            return dk, dv

        carry = step(kb, (jnp.zeros((tq, HEAD_PAD), F32), jnp.zeros((tq, V_HEAD), F32)), True)
        dk, dv = lax.fori_loop(kb + 1, nq, lambda qb, cr: step(qb, cr, False), carry)
        dk_ref[...] = dk
        dv_ref[...] = dv

    stat = pl.BlockSpec((1, nq, 1, tq), lambda h, j: (h, 0, 0, 0))
    return pl.pallas_call(
        body, name="attn_bwd", grid=(MLA_HEADS, nq),
        in_specs=[pl.BlockSpec((t, HEAD_PAD), lambda h, j: (0, h)),
                  pl.BlockSpec((tq, HEAD_PAD), lambda h, j: (j, h)),
                  pl.BlockSpec((tq, V_HEAD), lambda h, j: (j, h)),
                  pl.BlockSpec((t, V_HEAD), lambda h, j: (0, h)), stat, stat],
        out_specs=[pl.BlockSpec((t, HEAD_PAD), lambda h, j: (0, h)),
                   pl.BlockSpec((tq, HEAD_PAD), lambda h, j: (j, h)),
                   pl.BlockSpec((tq, V_HEAD), lambda h, j: (j, h))],
        out_shape=[jax.ShapeDtypeStruct((t, QK_COLS), F32), jax.ShapeDtypeStruct((t, QK_COLS), F32),
                   jax.ShapeDtypeStruct((t, MLA_WIDTH), F32)],
        compiler_params=_params(("arbitrary", "arbitrary")),
    )(q, k, v, do, lse, delta)


def _shift_rows(v, prev, n):
    out = pltpu.roll(v, n, 0)
    row = lax.broadcasted_iota(jnp.int32, v.shape, 0)
    for r in range(n):
        out = jnp.where(row == r, prev[8 - n + r:8 - n + r + 1, :], out)
    return out


def _advance_rows(v, nxt, n):
    rows = v.shape[0]
    out = pltpu.roll(v, rows - n, 0)
    row = lax.broadcasted_iota(jnp.int32, v.shape, 0)
    for r in range(n):
        out = jnp.where(row == rows - n + r, nxt[r:r + 1, :], out)
    return out


def _conv_taps(zc, zc_prev, first):
    w = CONV_WIDTH
    u = zc[:, w:2 * w] * zc[:, 2 * w:]
    up = jnp.where(first, 0.0, zc_prev[:, w:2 * w] * zc_prev[:, 2 * w:])
    return u, _shift_rows(u, up, 1), _shift_rows(u, up, 2)


def mix_out_forward(zc, o, conv_w, og, gmat_a, gmat_b, w_out, x, gate):
    t, d = x.shape
    tm = _tile(t, ROW_TILE, 16)
    w = CONV_WIDTH

    def body(zc_ref, zp_ref, o_ref, cw_ref, og_ref, ga_ref, gb_ref, w_ref, x_ref, gate_ref,
             xo_ref, yn_ref, y_ref, ya_ref):
        zc_v = zc_ref[...]
        u, u1, u2 = _conv_taps(zc_v, zp_ref[...], pl.program_id(0) == 0)
        cw = cw_ref[...]
        ya = zc_v[:, :w] * (cw[0:1] * u2 + cw[1:2] * u1 + cw[2:3] * u)
        ya_ref[...] = ya
        ov = o_ref[...]
        ogv = og_ref[...]
        yn_ref[:, :w] = (ya * lax.rsqrt(_group_mean(ya * ya, ga_ref[...]) + EPS) * ogv[:, :w]).astype(BF16)
        yn_ref[:, w:] = (ov * lax.rsqrt(_group_mean(ov * ov, gb_ref[...]) + EPS) * ogv[:, w:]).astype(BF16)
        y = _dot(yn_ref[...], w_ref[...])
        y_ref[...] = y
        xo_ref[...] = x_ref[...] + gate_ref[...] * y

    def rows(n):
        return pl.BlockSpec((tm, n), lambda i: (i, 0))

    prev = pl.BlockSpec((8, ZC_COLS), lambda i: (jnp.maximum(i * (tm // 8) - 1, 0), 0))
    return pl.pallas_call(
        body, name="mix_out_fwd", grid=(t // tm,),
        in_specs=[rows(ZC_COLS), prev, rows(MLA_WIDTH), _row(conv_w), _row(og), _row(gmat_a), _row(gmat_b),
                  _row(w_out), rows(d), _row(gate)],
        out_specs=[rows(d), rows(MIX_WIDTH), rows(d), rows(w)],
        out_shape=[jax.ShapeDtypeStruct((t, d), F32), jax.ShapeDtypeStruct((t, MIX_WIDTH), BF16),
                   jax.ShapeDtypeStruct((t, d), F32), jax.ShapeDtypeStruct((t, w), F32)],
        compiler_params=_params(("arbitrary",)),
    )(zc, zc, o, conv_w, og, gmat_a, gmat_b, w_out, x, gate)


def _group_norm_bwd(dyn, y, og, gmat):
    rs = lax.rsqrt(_group_mean(y * y, gmat) + EPS)
    yhat = y * rs
    d_og = jnp.sum(dyn * yhat, axis=0, keepdims=True)
    dyh = dyn * og
    return rs * (dyh - yhat * _group_mean(dyh * yhat, gmat)), d_og


def mix_out_backward(dxo, y, gate, ya, o, og, gmat_a, gmat_b, w_out):
    t, d = dxo.shape
    tm = _tile(t, ROW_TILE, 16)
    w = CONV_WIDTH

    def body(dxo_ref, y_ref, gate_ref, ya_ref, o_ref, og_ref, ga_ref, gb_ref, w_ref,
             dy_ref, dya_ref, do_ref, delta_ref, sd_ref, so_ref):
        @pl.when(pl.program_id(0) == 0)
        def _():
            sd_ref[...] = jnp.zeros_like(sd_ref)
            so_ref[...] = jnp.zeros_like(so_ref)

        dxo_v = dxo_ref[...]
        dy = (gate_ref[...] * dxo_v).astype(BF16)
        dy_ref[...] = dy
        sd_ref[0:1, :] += jnp.sum(dxo_v * y_ref[...], axis=0, keepdims=True)
        dyn = _dot(dy, w_ref[...], NT)
        ogv = og_ref[...]
        ov = o_ref[...]
        dya, d_og_a = _group_norm_bwd(dyn[:, :w], ya_ref[...], ogv[:, :w], ga_ref[...])
        dov, d_og_b = _group_norm_bwd(dyn[:, w:], ov, ogv[:, w:], gb_ref[...])
        dya_ref[...] = dya
        do_ref[...] = dov.astype(BF16)
        so_ref[0:1, :w] += d_og_a
        so_ref[0:1, w:] += d_og_b
        prod = dov * ov
        for h in range(MLA_HEADS):
            delta_ref[h] = jnp.sum(prod[:, h * V_HEAD:(h + 1) * V_HEAD], axis=-1, keepdims=True)

    def rows(n):
        return pl.BlockSpec((tm, n), lambda i: (i, 0))

    return pl.pallas_call(
        body, name="mix_out_bwd", grid=(t // tm,),
        in_specs=[rows(d), rows(d), _row(gate), rows(w), rows(MLA_WIDTH), _row(og), _row(gmat_a), _row(gmat_b),
                  _row(w_out)],
        out_specs=[rows(d), rows(w), rows(MLA_WIDTH), pl.BlockSpec((MLA_HEADS, tm, 1), lambda i: (0, i, 0)),
                   pl.BlockSpec((8, d), lambda i: (0, 0)), pl.BlockSpec((8, MIX_WIDTH), lambda i: (0, 0))],
        out_shape=[jax.ShapeDtypeStruct((t, d), BF16), jax.ShapeDtypeStruct((t, w), F32),
                   jax.ShapeDtypeStruct((t, MLA_WIDTH), BF16), jax.ShapeDtypeStruct((MLA_HEADS, t, 1), F32),
                   jax.ShapeDtypeStruct((8, d), F32), jax.ShapeDtypeStruct((8, MIX_WIDTH), F32)],
        compiler_params=_params(("arbitrary",)),
    )(dxo, y, gate, ya, o, og, gmat_a, gmat_b, w_out)


def conv_backward(zc, dya, conv_w):
    t = zc.shape[0]
    tm = _tile(t, ROW_TILE, 16)
    nt = t // tm
    w = CONV_WIDTH

    def body(zc_ref, zp_ref, zn_ref, dya_ref, dn_ref, cw_ref, dzc_ref, sums_ref):
        i = pl.program_id(0)

        @pl.when(i == 0)
        def _():
            sums_ref[...] = jnp.zeros_like(sums_ref)

        zc_v = zc_ref[...]
        u, u1, u2 = _conv_taps(zc_v, zp_ref[...], i == 0)
        cw = cw_ref[...]
        dya_v = dya_ref[...]
        dyc = dya_v * zc_v[:, :w]
        dyc_next = jnp.where(i == nt - 1, 0.0, dn_ref[...] * zn_ref[...][:, :w])
        du = cw[2:3] * dyc + cw[1:2] * _advance_rows(dyc, dyc_next, 1) + cw[0:1] * _advance_rows(dyc, dyc_next, 2)
        dzc_ref[:, :w] = (dya_v * (cw[0:1] * u2 + cw[1:2] * u1 + cw[2:3] * u)).astype(BF16)
        dzc_ref[:, w:2 * w] = (du * zc_v[:, 2 * w:]).astype(BF16)
        dzc_ref[:, 2 * w:] = (du * zc_v[:, w:2 * w]).astype(BF16)
        _add_rows(sums_ref, [jnp.sum(dyc * tap, axis=0, keepdims=True) for tap in (u2, u1, u)])

    def rows(n):
        return pl.BlockSpec((tm, n), lambda i: (i, 0))

    def halo(n, step):
        last = t // 8 - 1
        return pl.BlockSpec((8, n), lambda i: (jnp.clip(i * (tm // 8) + step, 0, last), 0))

    return pl.pallas_call(
        body, name="conv_bwd", grid=(nt,),
        in_specs=[rows(ZC_COLS), halo(ZC_COLS, -1), halo(ZC_COLS, tm // 8), rows(w), halo(w, tm // 8), _row(conv_w)],
        out_specs=[rows(ZC_COLS), pl.BlockSpec((8, w), lambda i: (0, 0))],
        out_shape=[jax.ShapeDtypeStruct((t, ZC_COLS), BF16), jax.ShapeDtypeStruct((8, w), F32)],
        compiler_params=_params(("arbitrary",)),
    )(zc, zc, zc, dya, dya, conv_w)


def _rms_bwd(dy, x, g):
    xhat, r = _rms(x)
    d_g = jnp.sum(dy * xhat, axis=0, keepdims=True)
    dxh = dy * g
    return r * (dxh - xhat * jnp.mean(dxh * xhat, axis=-1, keepdims=True)), d_g


def mla_project_backward(dq, dk, dv, zm, pos, inv_freq, qg, kvg, w_uq, w_ukv):
    t = zm.shape[0]
    tm = _tile(t, ROW_TILE, 16)

    def body(dq_ref, dk_ref, dv_ref, zm_ref, pos_ref, if_ref, qg_ref, kvg_ref, wq_ref, wkv_ref,
             dql_ref, dkvl_ref, dzm_ref, sums_ref):
        @pl.when(pl.program_id(0) == 0)
        def _():
            sums_ref[...] = jnp.zeros_like(sums_ref)

        tables = _rope_tables(pos_ref[...], if_ref[...])
        dkr = jnp.zeros((tm, LANES), F32)
        for h in range(MLA_HEADS):
            lo = h * HEAD_PAD
            dql_ref[:, lo:lo + QK_NOPE] = dq_ref[:, lo:lo + QK_NOPE].astype(BF16)
            dql_ref[:, lo + QK_NOPE:lo + HEAD_PAD] = _rope_transposed(
                dq_ref[:, lo + QK_NOPE:lo + HEAD_PAD], tables).astype(BF16)
            dkvl_ref[:, h * QK_NOPE:(h + 1) * QK_NOPE] = dk_ref[:, lo:lo + QK_NOPE].astype(BF16)
            dkr = dkr + dk_ref[:, lo + QK_NOPE:lo + HEAD_PAD]
        dkvl_ref[:, MLA_HEADS * QK_NOPE:] = dv_ref[...].astype(BF16)
        zv = zm_ref[...]
        dqn = _dot(dql_ref[...], wq_ref[...])
        dkvn = _dot(dkvl_ref[...], wkv_ref[...])
        dcq, d_qg = _rms_bwd(dqn, zv[:, :Q_LORA], qg_ref[...])
        dckv, d_kvg = _rms_bwd(dkvn, zv[:, Q_LORA:Q_LORA + KV_LORA], kvg_ref[...])
        dzm_ref[:, :Q_LORA] = dcq.astype(BF16)
        dzm_ref[:, Q_LORA:Q_LORA + KV_LORA] = dckv.astype(BF16)
        dzm_ref[:, Q_LORA + KV_LORA:] = _rope_transposed(dkr, tables).astype(BF16)
        sums_ref[0:1, :Q_LORA] += d_qg
        sums_ref[0:1, Q_LORA:Q_LORA + KV_LORA] += d_kvg

    def rows(n):
        return pl.BlockSpec((tm, n), lambda i: (i, 0))

    return pl.pallas_call(
        body, name="mla_project_bwd", grid=(t // tm,),
        in_specs=[rows(QK_COLS), rows(QK_COLS), rows(MLA_WIDTH), rows(ZM_COLS), rows(1), _row(inv_freq),
                  _row(qg), _row(kvg), _row(w_uq), _row(w_ukv)],
        out_specs=[rows(QK_COLS), rows(QK_COLS), rows(ZM_COLS), pl.BlockSpec((8, ZM_COLS), lambda i: (0, 0))],
        out_shape=[jax.ShapeDtypeStruct((t, QK_COLS), BF16), jax.ShapeDtypeStruct((t, QK_COLS), BF16),
                   jax.ShapeDtypeStruct((t, ZM_COLS), BF16), jax.ShapeDtypeStruct((8, ZM_COLS), F32)],
        compiler_params=_params(("arbitrary",)),
    )(dq, dk, dv, zm, pos, inv_freq, qg, kvg, w_uq, w_ukv)


def mix_in_backward(dzc, dzm, w_in, x, dxo, gn, sc):
    t, d = x.shape
    tm = _tile(t, ROW_TILE, 16)

    def body(dzc_ref, dzm_ref, w_ref, x_ref, dxo_ref, gn_ref, sc_ref, dx_ref, sums_ref):
        @pl.when(pl.program_id(0) == 0)
        def _():
            sums_ref[...] = jnp.zeros_like(sums_ref)

        dh = _dot(dzc_ref[...], w_ref[:ZC_COLS, :]) + _dot(dzm_ref[...], w_ref[ZC_COLS:, :])
        dx, d_sh, d_sc, d_gn = _norm_mod_bwd(dh, x_ref[...], gn_ref[...], sc_ref[...])
        dx_ref[...] = dxo_ref[...] + dx
        _add_rows(sums_ref, [d_sh, d_sc, d_gn])

    def rows(n):
        return pl.BlockSpec((tm, n), lambda i: (i, 0))

    return pl.pallas_call(
        body, name="mix_in_bwd", grid=(t // tm,),
        in_specs=[rows(ZC_COLS), rows(ZM_COLS), _row(w_in), rows(d), rows(d), _row(gn), _row(sc)],
        out_specs=[rows(d), pl.BlockSpec((8, d), lambda i: (0, 0))],
        out_shape=[jax.ShapeDtypeStruct((t, d), F32), jax.ShapeDtypeStruct((8, d), F32)],
        compiler_params=_params(("arbitrary",)),
    )(dzc, dzm, w_in, x, dxo, gn, sc)


def final_loss(x, target, g):
    t, d = x.shape
    tm = _tile(t, ROW_TILE, 16)

    def body(x_ref, t_ref, g_ref, dx_ref, sums_ref):
        @pl.when(pl.program_id(0) == 0)
        def _():
            sums_ref[...] = jnp.zeros_like(sums_ref)

        gv = g_ref[...]
        xhat, r = _rms(x_ref[...])
        err = xhat * gv - t_ref[...]
        dy = err * (1.0 / d)
        dxh = dy * gv
        dx_ref[...] = r * (dxh - xhat * jnp.mean(dxh * xhat, axis=-1, keepdims=True))
        _add_rows(sums_ref, [jnp.sum(dy * xhat, axis=0, keepdims=True),
                             jnp.sum(err * err, axis=0, keepdims=True) * (0.5 / d)])

    row = pl.BlockSpec((tm, d), lambda i: (i, 0))
    return pl.pallas_call(
        body, name="final_loss", grid=(t // tm,),
        in_specs=[row, row, _row(g)],
        out_specs=[row, pl.BlockSpec((8, d), lambda i: (0, 0))],
        out_shape=[jax.ShapeDtypeStruct((t, d), F32), jax.ShapeDtypeStruct((8, d), F32)],
        compiler_params=_params(("arbitrary",)),
    )(x, target, g)


def adamw(w, g, m, v, name):
    r, n = w.shape
    tr = _tile(r, max(8, (1 << 19) // n), 8)

    def body(w_ref, g_ref, m_ref, v_ref, d_ref, mo_ref, vo_ref):
        gv = g_ref[...]
        m_new = ADAM_B1 * m_ref[...] + (1.0 - ADAM_B1) * gv
        v_new = ADAM_B2 * v_ref[...] + (1.0 - ADAM_B2) * (gv * gv)
        m_hat = m_new / (1.0 - ADAM_B1 ** ADAM_STEP)
        v_hat = v_new / (1.0 - ADAM_B2 ** ADAM_STEP)
        d_ref[...] = -ADAM_LR * (m_hat / (jnp.sqrt(v_hat) + ADAM_EPS) + ADAM_WD * w_ref[...])
        mo_ref[...] = m_new
        vo_ref[...] = v_new

    blk = pl.BlockSpec((tr, n), lambda i: (i, 0))
    shape = jax.ShapeDtypeStruct((r, n), F32)
    return pl.pallas_call(
        body, name=name, grid=(r // tr,), in_specs=[blk] * 4, out_specs=[blk] * 3, out_shape=[shape] * 3,
        compiler_params=_params(("arbitrary",)),
    )(w, g, m, v)


def _pad_to(v, n):
    return jnp.pad(v, (0, n - v.shape[0]))


def _pad_heads(w, axis_len):
    n = w.shape[1]
    return jnp.pad(w.reshape(MLA_HEADS, axis_len, n), ((0, 0), (0, HEAD_PAD - axis_len), (0, 0))).reshape(-1, n)


def _swap_head_parts(w, inner, outer):
    n = w.shape[1]
    return w.reshape(outer, inner, QK_NOPE, n).transpose(1, 0, 2, 3).reshape(-1, n)


def kernel(x, c, positions, ada_w, ada_b, norm_ffn1_g, ffn1_w1, ffn1_w3, ffn1_w2, norm_mix_g, w_in, conv_w, q_norm_g, w_uq, kv_norm_g, w_ukv, out_norm_g, w_out, norm_ffn2_g, ffn2_w1, ffn2_w3, ffn2_w2, final_norm_g, loss_target, m_ada_w, m_ada_b, m_norm_ffn1_g, m_ffn1_w1, m_ffn1_w3, m_ffn1_w2, m_norm_mix_g, m_w_in, m_conv_w, m_q_norm_g, m_w_uq, m_kv_norm_g, m_w_ukv, m_out_norm_g, m_w_out, m_norm_ffn2_g, m_ffn2_w1, m_ffn2_w3, m_ffn2_w2, m_final_norm_g, v_ada_w, v_ada_b, v_norm_ffn1_g, v_ffn1_w1, v_ffn1_w3, v_ffn1_w2, v_norm_mix_g, v_w_in, v_conv_w, v_q_norm_g, v_w_uq, v_kv_norm_g, v_w_ukv, v_out_norm_g, v_w_out, v_norm_ffn2_g, v_ffn2_w1, v_ffn2_w3, v_ffn2_w2, v_final_norm_g):
    t, d = x.shape[1], x.shape[2]
    f = ffn1_w2.shape[1] * N_DEV
    me = 4 * lax.axis_index("x") + 2 * lax.axis_index("y") + lax.axis_index("c")
    my_c = lax.axis_index("c")
    my_chip = 2 * lax.axis_index("x") + lax.axis_index("y")
    xs = x[0]
    n_ada = ada_w.shape[2]
    cw_n = conv_w.shape[2]

    c_rows = jnp.broadcast_to(c, (8, d))
    conv_rows = jnp.pad(conv_w[0], ((0, 8 - CONV_K), (0, LANES - cw_n)))
    c_all, conv_all = all_gather([c_rows, conv_rows], [0, 0], "gather_inputs")
    c_all = c_all[:, 0, :]
    conv_full8 = conv_all[:, :, :cw_n].transpose(1, 0, 2).reshape(8, CONV_WIDTH)

    ffn_blocks = jnp.stack([ffn1_w1[0].T, ffn1_w3[0].T, ffn1_w2[0], ffn2_w1[0].T, ffn2_w3[0].T, ffn2_w2[0]]).astype(BF16)
    gathered = all_gather(
        [ffn_blocks, w_in[0].T.astype(BF16), w_uq[0].T.astype(BF16), w_ukv[0].T.astype(BF16), w_out[0].astype(BF16)],
        [1, 0, 0, 0, 0], "gather_weights")
    ffn_ws = gathered[0].reshape(6, f, d)
    w_in_p = jnp.pad(gathered[1].reshape(IN_COLS, d), ((0, ZC_COLS + ZM_COLS - IN_COLS), (0, 0)))
    w_uq_p = _pad_heads(gathered[2].reshape(-1, Q_LORA), QK_NOPE + QK_ROPE)
    w_ukv_p = _swap_head_parts(gathered[3].reshape(-1, KV_LORA), 2, MLA_HEADS)
    w_out_f = gathered[4].reshape(MIX_WIDTH, d)

    ada_b_cols = lax.dynamic_slice_in_dim(ada_b, me * n_ada, n_ada, axis=1)
    mod_cols = ada_forward(c_all, ada_w[0], ada_b_cols)
    mod_all, = all_gather([mod_cols], [0], "gather_mod")
    mod = lax.dynamic_index_in_dim(mod_all, me, axis=1, keepdims=False).reshape(N_MOD, 1, d)
    sh1, sc1, g1, sh2, sc2, g2, sh3, sc3, g3 = [mod[i] for i in range(N_MOD)]

    gf = final_norm_g.reshape(1, d)
    x1, h1, a1, b1, y1 = ffn_forward(xs, norm_ffn1_g, sc1, sh1, g1, ffn_ws, 0, "ffn1_fwd")
    h2, zc, zm = mix_in_forward(x1, norm_mix_g, sc2, sh2, w_in_p)
    pos = positions[0].astype(F32).reshape(t, 1)
    inv_freq = ROPE_THETA ** (-jnp.arange(0, QK_ROPE, 2, dtype=F32) / QK_ROPE)
    inv_freq = jnp.concatenate([inv_freq, inv_freq, jnp.zeros((LANES - QK_ROPE,), F32)]).reshape(1, LANES)
    qn, kvn, q, k, v = mla_project(zm, pos, inv_freq, q_norm_g, kv_norm_g, w_uq_p, w_ukv_p)
    o, lse = attention_forward(q, k, v)
    lane = jnp.arange(CONV_WIDTH)
    gmat_a = (lane[:, None] // (CONV_WIDTH // CONV_GROUPS) == lane[None, :] // (CONV_WIDTH // CONV_GROUPS))
    gmat_a = (gmat_a / (CONV_WIDTH // CONV_GROUPS)).astype(BF16)
    gmat_b = ((lane[:, None] // V_HEAD == lane[None, :] // V_HEAD) / V_HEAD).astype(BF16)
    x2, yn, y2, ya = mix_out_forward(zc, o, conv_full8, out_norm_g, gmat_a, gmat_b, w_out_f, x1, g2)
    x3, h3, a3, b3, y3 = ffn_forward(x2, norm_ffn2_g, sc3, sh3, g3, ffn_ws, 3, "ffn2_fwd")
    dx3, sums_f = final_loss(x3, loss_target[0], gf)

    dx2, da3, db3, u3, dy3, sums_3 = ffn_backward(dx3, x2, a3, b3, y3, norm_ffn2_g, sc3, sh3, g3, ffn_ws, 3, "ffn2_bwd")
    g_w1b = matmul_tn(da3, h3, "ffn2_gw1")
    g_w3b = matmul_tn(db3, h3, "ffn2_gw3")
    g_w2b = matmul_tn(u3, dy3, "ffn2_gw2")
    dy2, dya, do, delta, sums_2d, sums_2o = mix_out_backward(dx2, y2, g2, ya, o, out_norm_g, gmat_a, gmat_b, w_out_f)
    g_w_out = matmul_tn(yn, dy2, "gw_out")
    nq = t // _tile(t, ATTN_TILE, CHUNK)
    stat_shape = (MLA_HEADS, nq, 1, t // nq)
    dq, dk, dv = attention_backward(q, k, v, do, lse.reshape(stat_shape), delta.reshape(stat_shape))
    dzc, sums_c = conv_backward(zc, dya, conv_full8)
    dql, dkvl, dzm, sums_m = mla_project_backward(dq, dk, dv, zm, pos, inv_freq, q_norm_g, kv_norm_g, w_uq_p, w_ukv_p)
    g_w_uq_p = matmul_tn(dql, qn, "gw_uq")
    g_w_ukv_p = matmul_tn(dkvl, kvn, "gw_ukv")
    g_w_in = jnp.concatenate([matmul_tn(dzc, h2, "gw_in_conv"), matmul_tn(dzm, h2, "gw_in_mla")])[:IN_COLS]
    dx1, sums_1m = mix_in_backward(dzc, dzm, w_in_p, x1, dx2, norm_mix_g, sc2)
    dx0, da1, db1, u1, dy1, sums_1 = ffn_backward(dx1, xs, a1, b1, y1, norm_ffn1_g, sc1, sh1, g1, ffn_ws, 0, "ffn1_bwd")
    g_w1a = matmul_tn(da1, h1, "ffn1_gw1")
    g_w3a = matmul_tn(db1, h1, "ffn1_gw3")
    g_w2a = matmul_tn(u1, dy1, "ffn1_gw2")
    g_w_uq = g_w_uq_p.reshape(MLA_HEADS, HEAD_PAD, Q_LORA)[:, :QK_NOPE + QK_ROPE].reshape(-1, Q_LORA)
    g_w_ukv = _swap_head_parts(g_w_ukv_p, MLA_HEADS, 2)

    dmod = jnp.concatenate([sums_1[0], sums_1[1], sums_1[2], sums_1m[0], sums_1m[1], sums_2d[0],
                            sums_3[0], sums_3[1], sums_3[2]])
    pieces = [dmod, sums_1[3], sums_1m[2], sums_m[0, :Q_LORA], sums_m[0, Q_LORA:Q_LORA + KV_LORA], sums_2o[0],
              sums_3[3], sums_f[0], sums_f[1], sums_c[:CONV_K].reshape(-1)]
    plens = [p.shape[0] for p in pieces]
    poffs = [sum(plens[:i]) for i in range(len(plens))]
    vec_len = -(-sum(plens) // 1024) * 1024
    vec = _pad_to(jnp.concatenate(pieces), vec_len).reshape(-1, LANES)
    vec_all, = all_gather([vec], [0], "gather_sums")
    tot = sum_devices(vec_all).reshape(-1)
    g_ada_b, g_n1, g_nmix, g_qg, g_kvg, g_og, g_n3, g_gf, loss_lanes, g_conv_full = [
        tot[o:o + n] for o, n in zip(poffs, plens)]
    loss = sum_lanes(loss_lanes.reshape(1, d))[0, 0]
    g_conv = lax.dynamic_slice_in_dim(g_conv_full.reshape(CONV_K, CONV_WIDTH), me * cw_n, cw_n, axis=1)
    dmod_all = vec_all.reshape(N_DEV, vec_len)[:, :N_MOD * d]
    dmod_cols = lax.dynamic_slice_in_dim(dmod_all, me * n_ada, n_ada, axis=1)
    g_ada_w = ada_backward(jnp.pad(c_all, ((0, 8), (0, 0))), jnp.pad(dmod_cols, ((0, 8), (0, 0))))

    names = ["ffn1_w1", "ffn1_w3", "ffn1_w2", "ffn2_w1", "ffn2_w3", "ffn2_w2", "w_in", "w_uq", "w_ukv", "w_out"]
    grads8 = [g.reshape(N_DEV, g.shape[0] // N_DEV, g.shape[1])
              for g in (g_w1a, g_w3a, g_w2a, g_w1b, g_w3b, g_w2b, g_w_in, g_w_uq, g_w_ukv, g_w_out)]
    got_sib = exchange_sibling(grads8)
    chip_idx = jnp.bitwise_xor(my_chip, jnp.array([0, 2, 1, 3], jnp.int32)).astype(jnp.int32)
    src_idx = (2 * chip_idx + my_c).astype(jnp.int32)
    sums = [add_sibling(g, got, src_idx, chip_idx, "rs_add_" + n) for g, got, n in zip(grads8, got_sib, names)]
    got_chips = exchange_chips([s[1] for s in sums])
    g_rows = [add_received(s[0], got, "rs_sum_" + n) for s, got, n in zip(sums, got_chips, names)]
    transposed = {"ffn1_w1", "ffn1_w3", "ffn2_w1", "ffn2_w3", "w_in", "w_uq", "w_ukv"}
    g_sh = {n: (g.T if n in transposed else g) for n, g in zip(names, g_rows)}

    def update(name, w, g, m, v):
        shape = w.shape
        two_d = (-1, shape[-1])
        dlt, nm, nv = adamw(w.reshape(two_d), g.reshape(two_d), m.reshape(two_d), v.reshape(two_d), "adamw_" + name)
        return g.reshape(shape), dlt.reshape(shape), nm.reshape(shape), nv.reshape(shape)

    res = {}
    res["ada_w"] = update("ada_w", ada_w, g_ada_w, m_ada_w, v_ada_w)
    big = [("ffn1_w1", ffn1_w1, m_ffn1_w1, v_ffn1_w1), ("ffn1_w3", ffn1_w3, m_ffn1_w3, v_ffn1_w3),
           ("ffn2_w1", ffn2_w1, m_ffn2_w1, v_ffn2_w1), ("ffn2_w3", ffn2_w3, m_ffn2_w3, v_ffn2_w3),
           ("w_in", w_in, m_w_in, v_w_in), ("w_uq", w_uq, m_w_uq, v_w_uq), ("w_ukv", w_ukv, m_w_ukv, v_w_ukv),
           ("ffn1_w2", ffn1_w2, m_ffn1_w2, v_ffn1_w2), ("ffn2_w2", ffn2_w2, m_ffn2_w2, v_ffn2_w2),
           ("w_out", w_out, m_w_out, v_w_out)]
    for name, w, m, v in big:
        res[name] = update(name, w, g_sh[name], m, v)
    smalls = [("ada_b", ada_b, g_ada_b, m_ada_b, v_ada_b),
              ("norm_ffn1_g", norm_ffn1_g, g_n1, m_norm_ffn1_g, v_norm_ffn1_g),
              ("norm_mix_g", norm_mix_g, g_nmix, m_norm_mix_g, v_norm_mix_g),
              ("conv_w", conv_w, g_conv, m_conv_w, v_conv_w),
              ("q_norm_g", q_norm_g, g_qg, m_q_norm_g, v_q_norm_g),
              ("kv_norm_g", kv_norm_g, g_kvg, m_kv_norm_g, v_kv_norm_g),
              ("out_norm_g", out_norm_g, g_og, m_out_norm_g, v_out_norm_g),
              ("norm_ffn2_g", norm_ffn2_g, g_n3, m_norm_ffn2_g, v_norm_ffn2_g),
              ("final_norm_g", final_norm_g, g_gf, m_final_norm_g, v_final_norm_g)]
    slens = [w.size for _, w, _, _, _ in smalls]
    soffs = [sum(slens[:i]) for i in range(len(slens))]
    s_len = -(-sum(slens) // 1024) * 1024

    def pack_small(i):
        return _pad_to(jnp.concatenate([s[i].reshape(-1) for s in smalls]), s_len).reshape(8, -1)

    s_out = adamw(pack_small(1), pack_small(2), pack_small(3), pack_small(4), "adamw_small")
    for (name, w, g, _, _), o, n in zip(smalls, soffs, slens):
        res[name] = (g.reshape(w.shape),) + tuple(a.reshape(-1)[o:o + n].reshape(w.shape) for a in s_out)

    order = ["ada_w", "ada_b", "norm_ffn1_g", "ffn1_w1", "ffn1_w3", "ffn1_w2", "norm_mix_g", "w_in", "conv_w",
             "q_norm_g", "w_uq", "kv_norm_g", "w_ukv", "out_norm_g", "w_out", "norm_ffn2_g", "ffn2_w1", "ffn2_w3",
             "ffn2_w2", "final_norm_g"]
    return (loss, dx0.reshape(x.shape), *[res[n][0] for n in order], *[res[n][1] for n in order],
            *[res[n][2] for n in order], *[res[n][3] for n in order])
```

```python
import functools

import jax
import jax.numpy as jnp
from jax import lax
from jax.experimental import pallas as pl
from jax.experimental.pallas import tpu as pltpu

F32 = jnp.float32
BF16 = jnp.bfloat16
MESH_ID = pl.DeviceIdType.MESH
N_DEV = 8

EPS = 1e-6
CHUNK = 64
N_MOD = 9
CONV_WIDTH = 512
CONV_GROUPS = 8
CONV_K = 3
MLA_HEADS = 4
QK_NOPE = 128
QK_ROPE = 64
V_HEAD = 128
Q_LORA = 384
KV_LORA = 256
ROPE_THETA = 10000.0
MLA_WIDTH = MLA_HEADS * V_HEAD
MIX_WIDTH = CONV_WIDTH + MLA_WIDTH
IN_COLS = 3 * CONV_WIDTH + Q_LORA + KV_LORA + QK_ROPE
ZC_COLS = 3 * CONV_WIDTH
ZM_COLS = Q_LORA + KV_LORA + 128
HEAD_PAD = 256
QK_COLS = MLA_HEADS * HEAD_PAD
ATTN_SCALE = (QK_NOPE + QK_ROPE) ** -0.5
NEG_INF = -1e30

ADAM_LR = 0.001
ADAM_B1 = 0.9
ADAM_B2 = 0.999
ADAM_EPS = 1e-08
ADAM_WD = 0.01
ADAM_STEP = 10

LANES = 128
VMEM_LIMIT = 56 * 1024 * 1024
ROW_TILE = 512
FFN_FWD_TILE = (1024, 256)
FFN_BWD_TILE = (256, 1408)
GRAD_TILE = 1408
ATTN_TILE = 512

NN = (((1,), (0,)), ((), ()))
NT = (((1,), (1,)), ((), ()))
TN = (((0,), (0,)), ((), ()))


def _dot(a, b, dims=NN):
    return lax.dot_general(a, b, dims, preferred_element_type=F32)


def _tile(n, cap, mult=LANES):
    best = None
    for t in range(mult, min(n, cap) + 1, mult):
        if n % t == 0:
            best = t
    return n if best is None else best


def _params(sem=None):
    return pltpu.CompilerParams(dimension_semantics=sem, vmem_limit_bytes=VMEM_LIMIT)


def _row(v):
    return pl.BlockSpec(v.shape, lambda *_: (0,) * v.ndim)


def _rms(x):
    r = lax.rsqrt(jnp.mean(x * x, axis=-1, keepdims=True) + EPS)
    return x * r, r


def _norm_mod_bwd(dh, x, gn, sc):
    xhat, r = _rms(x)
    d_sh = jnp.sum(dh, axis=0, keepdims=True)
    d_sc = jnp.sum(dh * (xhat * gn), axis=0, keepdims=True)
    dxn = dh * (1.0 + sc)
    d_gn = jnp.sum(dxn * xhat, axis=0, keepdims=True)
    dxh = dxn * gn
    dx = r * (dxh - xhat * jnp.mean(dxh * xhat, axis=-1, keepdims=True))
    return dx, d_sh, d_sc, d_gn


def _group_mean(v, gmat):
    hi = v.astype(BF16)
    lo = (v - hi.astype(F32)).astype(BF16)
    return _dot(hi, gmat) + _dot(lo, gmat)


def _add_rows(ref, rows):
    for r, v in enumerate(rows):
        ref[r:r + 1, :] += v


def _window(ref, axis, j):
    return ref.at[(slice(None),) * axis + (j,)]


def _any_specs(n):
    return [pl.BlockSpec(memory_space=pl.ANY)] * n


def all_gather(blocks, axes, name):
    n_arr = len(blocks)

    def body(*refs):
        ins, outs = refs[:n_arr], refs[n_arr:2 * n_arr]
        send_sems, recv_sems, local_sems = refs[2 * n_arr:]
        x, y, c = lax.axis_index("x"), lax.axis_index("y"), lax.axis_index("c")
        me, sibling = (x, y, c), (x, y, 1 - c)
        chips = [(1 - x, y), (x, 1 - y), (1 - x, 1 - y)]

        def slot(a, px, py, pc):
            return _window(outs[a], axes[a], 4 * px + 2 * py + pc)

        def copy(a, k, block, to, src=None):
            return pltpu.make_async_remote_copy(
                src_ref=slot(a, *block) if src is None else src, dst_ref=slot(a, *block),
                send_sem=send_sems.at[k, a], recv_sem=recv_sems.at[k, a], device_id=to, device_id_type=MESH_ID)

        arrays = range(n_arr)
        mine = [pltpu.make_async_copy(ins[a], slot(a, *me), local_sems.at[a]) for a in arrays]
        for cp in mine:
            cp.start()
        first = [copy(a, 0, me, sibling, src=ins[a]) for a in arrays]
        first += [copy(a, 1 + j, me, (*chip, c), src=ins[a]) for j, chip in enumerate(chips) for a in arrays]
        for cp in first:
            cp.start()
        passed = []
        for j, chip in enumerate(chips):
            for a in arrays:
                copy(a, 1 + j, (*chip, c), me).wait_recv()
                passed.append(copy(a, 4 + j, (*chip, c), sibling))
                passed[-1].start()
        for a in arrays:
            copy(a, 0, sibling, me).wait_recv()
        for j, chip in enumerate(chips):
            for a in arrays:
                copy(a, 4 + j, (*chip, 1 - c), me).wait_recv()
        for cp in first + passed:
            cp.wait_send()
        for cp in mine:
            cp.wait()

    def gathered(b, axis):
        return jax.ShapeDtypeStruct(b.shape[:axis] + (N_DEV,) + b.shape[axis:], b.dtype)

    return pl.pallas_call(
        body, name=name,
        out_shape=[gathered(b, ax) for b, ax in zip(blocks, axes)],
        in_specs=_any_specs(n_arr), out_specs=_any_specs(n_arr),
        scratch_shapes=[pltpu.SemaphoreType.DMA((7, n_arr)), pltpu.SemaphoreType.DMA((7, n_arr)),
                        pltpu.SemaphoreType.DMA((n_arr,))],
    )(*blocks)


def exchange_sibling(grads):
    n_arr = len(grads)

    def body(*refs):
        ins, outs = refs[:n_arr], refs[n_arr:2 * n_arr]
        send_sems, recv_sems = refs[2 * n_arr:]
        x, y, c = lax.axis_index("x"), lax.axis_index("y"), lax.axis_index("c")

        def copy(a, src, dst):
            return pltpu.make_async_remote_copy(
                src_ref=src, dst_ref=dst, send_sem=send_sems.at[a], recv_sem=recv_sems.at[a],
                device_id=(x, y, 1 - c), device_id_type=MESH_ID)

        for a in range(n_arr):
            for k in range(4):
                copy(a, ins[a].at[2 * k + (1 - c)], outs[a].at[k]).start()
        whole = [copy(a, ins[a].at[pl.ds(0, 4)], outs[a]) for a in range(n_arr)]
        for cp in whole:
            cp.wait_recv()
        for cp in whole:
            cp.wait_send()

    return pl.pallas_call(
        body, name="rs_sibling",
        out_shape=[jax.ShapeDtypeStruct((4,) + g.shape[1:], g.dtype) for g in grads],
        in_specs=_any_specs(n_arr), out_specs=_any_specs(n_arr),
        scratch_shapes=[pltpu.SemaphoreType.DMA((n_arr,)), pltpu.SemaphoreType.DMA((n_arr,))],
    )(*grads)


def exchange_chips(parts):
    n_arr = len(parts)

    def body(*refs):
        ins, outs = refs[:n_arr], refs[n_arr:2 * n_arr]
        send_sems, recv_sems = refs[2 * n_arr:]
        x, y, c = lax.axis_index("x"), lax.axis_index("y"), lax.axis_index("c")
        chips = [(1 - x, y), (x, 1 - y), (1 - x, 1 - y)]

        def copy(a, src, dst, chip):
            return pltpu.make_async_remote_copy(
                src_ref=src, dst_ref=dst, send_sem=send_sems.at[a], recv_sem=recv_sems.at[a],
                device_id=(*chip, c), device_id_type=MESH_ID)

        for a in range(n_arr):
            for j, chip in enumerate(chips):
                copy(a, ins[a].at[j], outs[a].at[j], chip).start()
        whole = [copy(a, ins[a], outs[a], chips[0]) for a in range(n_arr)]
        for cp in whole:
            cp.wait_recv()
        for cp in whole:
            cp.wait_send()

    return pl.pallas_call(
        body, name="rs_chips",
        out_shape=[jax.ShapeDtypeStruct(p.shape, p.dtype) for p in parts],
        in_specs=_any_specs(n_arr), out_specs=_any_specs(n_arr),
        scratch_shapes=[pltpu.SemaphoreType.DMA((n_arr,)), pltpu.SemaphoreType.DMA((n_arr,))],
    )(*parts)


def add_sibling(g8, got, src_idx, chip_idx, name):
    _, r, n = g8.shape
    tr = _tile(r, 256, 16)

    def body(si_ref, ci_ref, g0_ref, g1_ref, g2_ref, g3_ref, got_ref, own_ref, send_ref):
        own_ref[...] = g0_ref[0] + got_ref[ci_ref[0]]
        for j, g_ref in enumerate((g1_ref, g2_ref, g3_ref)):
            send_ref[j] = (g_ref[0] + got_ref[ci_ref[j + 1]]).astype(BF16)

    def mine(j):
        return pl.BlockSpec((1, tr, n), lambda i, si, ci: (si[j], i, 0))

    return pl.pallas_call(
        body, name=name,
        out_shape=[jax.ShapeDtypeStruct((r, n), F32), jax.ShapeDtypeStruct((3, r, n), BF16)],
        grid_spec=pltpu.PrefetchScalarGridSpec(
            num_scalar_prefetch=2, grid=(r // tr,),
            in_specs=[mine(0), mine(1), mine(2), mine(3), pl.BlockSpec((4, tr, n), lambda i, si, ci: (0, i, 0))],
            out_specs=[pl.BlockSpec((tr, n), lambda i, si, ci: (i, 0)),
                       pl.BlockSpec((3, tr, n), lambda i, si, ci: (0, i, 0))]),
        compiler_params=_params(("arbitrary",)),
    )(src_idx, chip_idx, g8, g8, g8, g8, got)


def add_received(own, got, name):
    r, n = own.shape
    tr = _tile(r, 256, 16)

    def body(a_ref, b_ref, o_ref):
        acc = a_ref[...]
        for j in range(3):
            acc = acc + b_ref[j].astype(F32)
        o_ref[...] = acc

    return pl.pallas_call(
        body, name=name,
        out_shape=jax.ShapeDtypeStruct((r, n), F32),
        grid=(r // tr,),
        in_specs=[pl.BlockSpec((tr, n), lambda i: (i, 0)), pl.BlockSpec((3, tr, n), lambda i: (0, i, 0))],
        out_specs=pl.BlockSpec((tr, n), lambda i: (i, 0)),
        compiler_params=_params(("arbitrary",)),
    )(own, got)


def sum_devices(g):
    def body(g_ref, o_ref):
        acc = g_ref[0]
        for j in range(1, N_DEV):
            acc = acc + g_ref[j]
        o_ref[...] = acc

    return pl.pallas_call(body, name="sum_devices", out_shape=jax.ShapeDtypeStruct(g.shape[1:], F32))(g)


def sum_lanes(v):
    def body(v_ref, o_ref):
        o_ref[...] = jnp.broadcast_to(jnp.sum(v_ref[...], axis=-1, keepdims=True), (1, LANES))

    return pl.pallas_call(body, name="sum_lanes", out_shape=jax.ShapeDtypeStruct((1, LANES), F32))(v)


def ada_forward(c_all, ada_w, ada_b_cols):
    nb, n = c_all.shape[0], ada_w.shape[1]

    def body(c_ref, w_ref, b_ref, o_ref):
        cv = c_ref[...]
        s = (cv * jax.nn.sigmoid(cv)).astype(BF16)
        o_ref[...] = _dot(s, w_ref[...].astype(BF16)) + b_ref[...]

    return pl.pallas_call(body, name="ada_fwd", out_shape=jax.ShapeDtypeStruct((nb, n), F32),
                          compiler_params=_params())(c_all, ada_w, ada_b_cols)


def ada_backward(c_all16, dmod16):
    d, n = c_all16.shape[1], dmod16.shape[1]

    def body(c_ref, g_ref, o_ref):
        cv = c_ref[...]
        s = (cv * jax.nn.sigmoid(cv)).astype(BF16)
        o_ref[...] = _dot(s, g_ref[...].astype(BF16), TN)

    return pl.pallas_call(body, name="ada_bwd", out_shape=jax.ShapeDtypeStruct((d, n), F32),
                          compiler_params=_params())(c_all16, dmod16)


def ffn_forward(x, gn, sc, sh, gate, ws, first, name):
    t, d = x.shape
    f = ws.shape[1]
    tm, tf = _tile(t, FFN_FWD_TILE[0], 16), _tile(f, FFN_FWD_TILE[1])
    nf = f // tf

    def body(x_ref, gn_ref, sc_ref, sh_ref, gate_ref, w1_ref, w3_ref, w2_ref,
             xo_ref, h_ref, a_ref, b_ref, y_ref, hs, acc):
        j = pl.program_id(1)

        @pl.when(j == 0)
        def _():
            xhat, _ = _rms(x_ref[...])
            h = (xhat * gn_ref[...] * (1.0 + sc_ref[...]) + sh_ref[...]).astype(BF16)
            hs[...] = h
            h_ref[...] = h
            acc[...] = jnp.zeros_like(acc)

        h = hs[...]
        a = _dot(h, w1_ref[...], NT)
        b = _dot(h, w3_ref[...], NT)
        a_ref[...] = a.astype(BF16)
        b_ref[...] = b.astype(BF16)
        u = (a * jax.nn.sigmoid(a) * b).astype(BF16)
        acc[...] += _dot(u, w2_ref[...])

        @pl.when(j == nf - 1)
        def _():
            y = acc[...]
            y_ref[...] = y
            xo_ref[...] = x_ref[...] + 0.5 * gate_ref[...] * y

    row = pl.BlockSpec((tm, d), lambda i, j: (i, 0))
    vec = pl.BlockSpec((1, d), lambda i, j: (0, 0))
    wide = pl.BlockSpec((tm, tf), lambda i, j: (i, j))
    return pl.pallas_call(
        body, name=name, grid=(t // tm, nf),
        in_specs=[row, vec, vec, vec, vec] + _ffn_weight_specs(first, tf, d),
        out_specs=[row, row, wide, wide, row],
        out_shape=[jax.ShapeDtypeStruct((t, d), F32), jax.ShapeDtypeStruct((t, d), BF16),
                   jax.ShapeDtypeStruct((t, f), BF16), jax.ShapeDtypeStruct((t, f), BF16),
                   jax.ShapeDtypeStruct((t, d), F32)],
        scratch_shapes=[pltpu.VMEM((tm, d), BF16), pltpu.VMEM((tm, d), F32)],
        compiler_params=_params(("arbitrary", "arbitrary")),
    )(x, gn, sc, sh, gate, ws, ws, ws)


def _ffn_weight_specs(first, tf, d):
    return [pl.BlockSpec((None, tf, d), lambda i, j, w=first + k: (w, j, 0)) for k in range(3)]


def ffn_backward(dxo, x, a, b, y, gn, sc, sh, gate, ws, first, name):
    t, d = x.shape
    f = ws.shape[1]
    tm, tf = _tile(t, FFN_BWD_TILE[0], 16), _tile(f, FFN_BWD_TILE[1])
    nf = f // tf

    def body(dxo_ref, x_ref, a_ref, b_ref, y_ref, gn_ref, sc_ref, sh_ref, gate_ref, w1_ref, w3_ref, w2_ref,
             dx_ref, da_ref, db_ref, u_ref, dy_ref, sums_ref, dys, acc):
        i, j = pl.program_id(0), pl.program_id(1)

        @pl.when(jnp.logical_and(i == 0, j == 0))
        def _():
            sums_ref[...] = jnp.zeros_like(sums_ref)

        @pl.when(j == 0)
        def _():
            dy = (0.5 * gate_ref[...] * dxo_ref[...]).astype(BF16)
            dys[...] = dy
            dy_ref[...] = dy
            acc[...] = jnp.zeros_like(acc)

        du = _dot(dys[...], w2_ref[...], NT)
        av = a_ref[...].astype(F32)
        bv = b_ref[...].astype(F32)
        s = jax.nn.sigmoid(av)
        sa = av * s
        da = (du * bv * (s * (1.0 + av * (1.0 - s)))).astype(BF16)
        db = (du * sa).astype(BF16)
        da_ref[...] = da
        db_ref[...] = db
        u_ref[...] = (sa * bv).astype(BF16)
        acc[...] += _dot(da, w1_ref[...]) + _dot(db, w3_ref[...])

        @pl.when(j == nf - 1)
        def _():
            dxo_v = dxo_ref[...]
            dx, d_sh, d_sc, d_gn = _norm_mod_bwd(acc[...], x_ref[...], gn_ref[...], sc_ref[...])
            dx_ref[...] = dxo_v + dx
            d_gate = jnp.sum(dxo_v * (0.5 * y_ref[...]), axis=0, keepdims=True)
            _add_rows(sums_ref, [d_sh, d_sc, d_gate, d_gn])

    row = pl.BlockSpec((tm, d), lambda i, j: (i, 0))
    vec = pl.BlockSpec((1, d), lambda i, j: (0, 0))
    wide = pl.BlockSpec((tm, tf), lambda i, j: (i, j))
    return pl.pallas_call(
        body, name=name, grid=(t // tm, nf),
        in_specs=[row, row, wide, wide, row, vec, vec, vec, vec] + _ffn_weight_specs(first, tf, d),
        out_specs=[row, wide, wide, wide, row, pl.BlockSpec((8, d), lambda i, j: (0, 0))],
        out_shape=[jax.ShapeDtypeStruct((t, d), F32), jax.ShapeDtypeStruct((t, f), BF16),
                   jax.ShapeDtypeStruct((t, f), BF16), jax.ShapeDtypeStruct((t, f), BF16),
                   jax.ShapeDtypeStruct((t, d), BF16), jax.ShapeDtypeStruct((8, d), F32)],
        scratch_shapes=[pltpu.VMEM((tm, d), BF16), pltpu.VMEM((tm, d), F32)],
        compiler_params=_params(("arbitrary", "arbitrary")),
    )(dxo, x, a, b, y, gn, sc, sh, gate, ws, ws, ws)


def matmul_tn(a, b, name):
    t, m = a.shape
    n = b.shape[1]
    tm, tn, tk = _tile(m, GRAD_TILE), _tile(n, GRAD_TILE), _tile(t, 1024, 16)
    nk = t // tk

    def body(a_ref, b_ref, o_ref, acc):
        k = pl.program_id(2)

        @pl.when(k == 0)
        def _():
            acc[...] = jnp.zeros_like(acc)

        acc[...] += _dot(a_ref[...], b_ref[...], TN)

        @pl.when(k == nk - 1)
        def _():
            o_ref[...] = acc[...]

    return pl.pallas_call(
        body, name=name, grid=(m // tm, n // tn, nk),
        in_specs=[pl.BlockSpec((tk, tm), lambda i, j, k: (k, i)), pl.BlockSpec((tk, tn), lambda i, j, k: (k, j))],
        out_specs=pl.BlockSpec((tm, tn), lambda i, j, k: (i, j)),
        out_shape=jax.ShapeDtypeStruct((m, n), F32),
        scratch_shapes=[pltpu.VMEM((tm, tn), F32)],
        compiler_params=_params(("arbitrary", "arbitrary", "arbitrary")),
    )(a, b)


def mix_in_forward(x, gn, sc, sh, w_in):
    t, d = x.shape
    tm = _tile(t, ROW_TILE, 16)

    def body(x_ref, gn_ref, sc_ref, sh_ref, w_ref, h_ref, zc_ref, zm_ref):
        xhat, _ = _rms(x_ref[...])
        h = (xhat * gn_ref[...] * (1.0 + sc_ref[...]) + sh_ref[...]).astype(BF16)
        h_ref[...] = h
        z = _dot(h, w_ref[...], NT)
        zc_ref[...] = z[:, :ZC_COLS]
        zm_ref[...] = z[:, ZC_COLS:]

    row = pl.BlockSpec((tm, d), lambda i: (i, 0))
    vec = pl.BlockSpec((1, d), lambda i: (0, 0))
    return pl.pallas_call(
        body, name="mix_in_fwd", grid=(t // tm,),
        in_specs=[row, vec, vec, vec, _row(w_in)],
        out_specs=[row, pl.BlockSpec((tm, ZC_COLS), lambda i: (i, 0)), pl.BlockSpec((tm, ZM_COLS), lambda i: (i, 0))],
        out_shape=[jax.ShapeDtypeStruct((t, d), BF16), jax.ShapeDtypeStruct((t, ZC_COLS), F32),
                   jax.ShapeDtypeStruct((t, ZM_COLS), F32)],
        compiler_params=_params(("arbitrary",)),
    )(x, gn, sc, sh, w_in)


def _rope_tables(pos, inv_freq):
    ang = pos * inv_freq
    lane = lax.broadcasted_iota(jnp.int32, ang.shape, 1)
    cos, sin = jnp.cos(ang), jnp.sin(ang)
    half = QK_ROPE // 2
    return cos, jnp.where(lane < half, -sin, 0.0), jnp.where(jnp.logical_and(lane >= half, lane < QK_ROPE), sin, 0.0)


def _rope(v, tables):
    cos, sin_a, sin_b = tables
    return v * cos + pltpu.roll(v, LANES - QK_ROPE // 2, 1) * sin_a + pltpu.roll(v, QK_ROPE // 2, 1) * sin_b


def _rope_transposed(dv, tables):
    cos, sin_a, sin_b = tables
    return dv * cos + pltpu.roll(dv * sin_a, QK_ROPE // 2, 1) + pltpu.roll(dv * sin_b, LANES - QK_ROPE // 2, 1)


def mla_project(zm, pos, inv_freq, qg, kvg, w_uq, w_ukv):
    t = zm.shape[0]
    tm = _tile(t, ROW_TILE, 16)

    def body(zm_ref, pos_ref, if_ref, qg_ref, kvg_ref, wq_ref, wkv_ref, qn_ref, kvn_ref, q_ref, k_ref, v_ref):
        zv = zm_ref[...]
        qn = (_rms(zv[:, :Q_LORA])[0] * qg_ref[...]).astype(BF16)
        kvn = (_rms(zv[:, Q_LORA:Q_LORA + KV_LORA])[0] * kvg_ref[...]).astype(BF16)
        qn_ref[...] = qn
        kvn_ref[...] = kvn
        qf = _dot(qn, wq_ref[...], NT)
        kvf = _dot(kvn, wkv_ref[...], NT)
        tables = _rope_tables(pos_ref[...], if_ref[...])
        kr = _rope(zv[:, Q_LORA + KV_LORA:], tables).astype(BF16)
        for h in range(MLA_HEADS):
            lo = h * HEAD_PAD
            q_ref[:, lo:lo + QK_NOPE] = qf[:, lo:lo + QK_NOPE].astype(BF16)
            q_ref[:, lo + QK_NOPE:lo + HEAD_PAD] = _rope(qf[:, lo + QK_NOPE:lo + HEAD_PAD], tables).astype(BF16)
            k_ref[:, lo:lo + QK_NOPE] = kvf[:, h * QK_NOPE:(h + 1) * QK_NOPE].astype(BF16)
            k_ref[:, lo + QK_NOPE:lo + HEAD_PAD] = kr
        v_ref[...] = kvf[:, MLA_HEADS * QK_NOPE:].astype(BF16)

    def rows(n):
        return pl.BlockSpec((tm, n), lambda i: (i, 0))

    return pl.pallas_call(
        body, name="mla_project", grid=(t // tm,),
        in_specs=[rows(ZM_COLS), rows(1), _row(inv_freq), _row(qg), _row(kvg), _row(w_uq), _row(w_ukv)],
        out_specs=[rows(Q_LORA), rows(KV_LORA), rows(QK_COLS), rows(QK_COLS), rows(MLA_WIDTH)],
        out_shape=[jax.ShapeDtypeStruct((t, Q_LORA), BF16), jax.ShapeDtypeStruct((t, KV_LORA), BF16),
                   jax.ShapeDtypeStruct((t, QK_COLS), BF16), jax.ShapeDtypeStruct((t, QK_COLS), BF16),
                   jax.ShapeDtypeStruct((t, MLA_WIDTH), BF16)],
        compiler_params=_params(("arbitrary",)),
    )(zm, pos, inv_freq, qg, kvg, w_uq, w_ukv)


def _chunk_mask(shape, q_axis):
    qi = lax.broadcasted_iota(jnp.int32, shape, q_axis) // CHUNK
    ki = lax.broadcasted_iota(jnp.int32, shape, 1 - q_axis) // CHUNK
    return ki <= qi


def attention_forward(q, k, v):
    t = q.shape[0]
    tq = _tile(t, ATTN_TILE, CHUNK)

    def body(q_ref, k_ref, v_ref, o_ref, lse_ref):
        i = pl.program_id(1)
        qv = q_ref[...]

        def step(kb, carry, masked):
            m, l, acc = carry
            start = pl.multiple_of(kb * tq, tq)
            s = _dot(qv, k_ref[pl.ds(start, tq), :], NT) * ATTN_SCALE
            if masked:
                s = jnp.where(_chunk_mask(s.shape, 0), s, NEG_INF)
            m_new = jnp.maximum(m, jnp.max(s, axis=-1, keepdims=True))
            alpha = jnp.exp(m - m_new)
            p = jnp.exp(s - m_new)
            l = alpha * l + jnp.sum(p, axis=-1, keepdims=True)
            acc = alpha * acc + _dot(p.astype(BF16), v_ref[pl.ds(start, tq), :])
            return m_new, l, acc

        init = (jnp.full((tq, 1), NEG_INF, F32), jnp.zeros((tq, 1), F32), jnp.zeros((tq, V_HEAD), F32))
        carry = lax.fori_loop(0, i // 2, lambda pb, cr: step(2 * pb + 1, step(2 * pb, cr, False), False), init)
        carry = lax.fori_loop(0, i % 2, lambda _, cr: step(i - 1, cr, False), carry)
        m, l, acc = step(i, carry, True)
        o_ref[...] = acc / l
        lse_ref[0] = m + jnp.log(l)

    return pl.pallas_call(
        body, name="attn_fwd", grid=(MLA_HEADS, t // tq),
        in_specs=[pl.BlockSpec((tq, HEAD_PAD), lambda h, i: (i, h)),
                  pl.BlockSpec((t, HEAD_PAD), lambda h, i: (0, h)),
                  pl.BlockSpec((t, V_HEAD), lambda h, i: (0, h))],
        out_specs=[pl.BlockSpec((tq, V_HEAD), lambda h, i: (i, h)),
                   pl.BlockSpec((1, tq, 1), lambda h, i: (h, i, 0))],
        out_shape=[jax.ShapeDtypeStruct((t, MLA_WIDTH), F32), jax.ShapeDtypeStruct((MLA_HEADS, t, 1), F32)],
        compiler_params=_params(("arbitrary", "arbitrary")),
    )(q, k, v)


def attention_backward(q, k, v, do, lse, delta):
    t = q.shape[0]
    tq = _tile(t, ATTN_TILE, CHUNK)
    nq = t // tq

    def body(q_ref, k_ref, v_ref, do_ref, lse_ref, delta_ref, dq_ref, dk_ref, dv_ref):
        kb = pl.program_id(1)

        @pl.when(kb == 0)
        def _():
            dq_ref[...] = jnp.zeros_like(dq_ref)

        kv, vv = k_ref[...], v_ref[...]

        def step(qb, carry, masked):
            dk, dv = carry
            rows = pl.ds(pl.multiple_of(qb * tq, tq), tq)
            qv, dov = q_ref[rows, :], do_ref[rows, :]
            s = _dot(kv, qv, NT) * ATTN_SCALE
            if masked:
                s = jnp.where(_chunk_mask(s.shape, 1), s, NEG_INF)
            p = jnp.exp(s - lse_ref[0, qb])
            dv = dv + _dot(p.astype(BF16), dov)
            dp = _dot(vv, dov, NT)
            ds = (p * (dp - delta_ref[0, qb]) * ATTN_SCALE).astype(BF16)
            dk = dk + _dot(ds, qv)
            dq_ref[rows, :] += _dot(ds, kv, TN)
            return dk, dv

        carry = step(kb, (jnp.zeros((tq, HEAD_PAD), F32), jnp.zeros((tq, V_HEAD), F32)), True)
        odd = (nq - 1 - kb) % 2
        carry = lax.fori_loop(0, odd, lambda _, cr: step(kb + 1, cr, False), carry)
        first = kb + 1 + odd
        dk, dv = lax.fori_loop(0, (nq - first) // 2,
                               lambda pb, cr: step(first + 2 * pb + 1, step(first + 2 * pb, cr, False), False), carry)
        dk_ref[...] = dk
        dv_ref[...] = dv

    stat = pl.BlockSpec((1, nq, 1, tq), lambda h, j: (h, 0, 0, 0))
    return pl.pallas_call(
        body, name="attn_bwd", grid=(MLA_HEADS, nq),
        in_specs=[pl.BlockSpec((t, HEAD_PAD), lambda h, j: (0, h)),
                  pl.BlockSpec((tq, HEAD_PAD), lambda h, j: (j, h)),
                  pl.BlockSpec((tq, V_HEAD), lambda h, j: (j, h)),
                  pl.BlockSpec((t, V_HEAD), lambda h, j: (0, h)), stat, stat],
        out_specs=[pl.BlockSpec((t, HEAD_PAD), lambda h, j: (0, h)),
                   pl.BlockSpec((tq, HEAD_PAD), lambda h, j: (j, h)),
                   pl.BlockSpec((tq, V_HEAD), lambda h, j: (j, h))],
        out_shape=[jax.ShapeDtypeStruct((t, QK_COLS), F32), jax.ShapeDtypeStruct((t, QK_COLS), F32),
                   jax.ShapeDtypeStruct((t, MLA_WIDTH), F32)],
        compiler_params=_params(("arbitrary", "arbitrary")),
    )(q, k, v, do, lse, delta)


def _shift_rows(v, prev, n):
    out = pltpu.roll(v, n, 0)
    row = lax.broadcasted_iota(jnp.int32, v.shape, 0)
    for r in range(n):
        out = jnp.where(row == r, prev[8 - n + r:8 - n + r + 1, :], out)
    return out


def _advance_rows(v, nxt, n):
    rows = v.shape[0]
    out = pltpu.roll(v, rows - n, 0)
    row = lax.broadcasted_iota(jnp.int32, v.shape, 0)
    for r in range(n):
        out = jnp.where(row == rows - n + r, nxt[r:r + 1, :], out)
    return out


def _conv_taps(zc, zc_prev, first):
    w = CONV_WIDTH
    u = zc[:, w:2 * w] * zc[:, 2 * w:]
    up = jnp.where(first, 0.0, zc_prev[:, w:2 * w] * zc_prev[:, 2 * w:])
    return u, _shift_rows(u, up, 1), _shift_rows(u, up, 2)


def mix_out_forward(zc, o, conv_w, og, gmat_a, gmat_b, w_out, x, gate):
    t, d = x.shape
    tm = _tile(t, ROW_TILE, 16)
    w = CONV_WIDTH

    def body(zc_ref, zp_ref, o_ref, cw_ref, og_ref, ga_ref, gb_ref, w_ref, x_ref, gate_ref,
             xo_ref, yn_ref, y_ref, ya_ref):
        zc_v = zc_ref[...]
        u, u1, u2 = _conv_taps(zc_v, zp_ref[...], pl.program_id(0) == 0)
        cw = cw_ref[...]
        ya = zc_v[:, :w] * (cw[0:1] * u2 + cw[1:2] * u1 + cw[2:3] * u)
        ya_ref[...] = ya
        ov = o_ref[...]
        ogv = og_ref[...]
        yn_ref[:, :w] = (ya * lax.rsqrt(_group_mean(ya * ya, ga_ref[...]) + EPS) * ogv[:, :w]).astype(BF16)
        yn_ref[:, w:] = (ov * lax.rsqrt(_group_mean(ov * ov, gb_ref[...]) + EPS) * ogv[:, w:]).astype(BF16)
        y = _dot(yn_ref[...], w_ref[...])
        y_ref[...] = y
        xo_ref[...] = x_ref[...] + gate_ref[...] * y

    def rows(n):
        return pl.BlockSpec((tm, n), lambda i: (i, 0))

    prev = pl.BlockSpec((8, ZC_COLS), lambda i: (jnp.maximum(i * (tm // 8) - 1, 0), 0))
    return pl.pallas_call(
        body, name="mix_out_fwd", grid=(t // tm,),
        in_specs=[rows(ZC_COLS), prev, rows(MLA_WIDTH), _row(conv_w), _row(og), _row(gmat_a), _row(gmat_b),
                  _row(w_out), rows(d), _row(gate)],
        out_specs=[rows(d), rows(MIX_WIDTH), rows(d), rows(w)],
        out_shape=[jax.ShapeDtypeStruct((t, d), F32), jax.ShapeDtypeStruct((t, MIX_WIDTH), BF16),
                   jax.ShapeDtypeStruct((t, d), F32), jax.ShapeDtypeStruct((t, w), F32)],
        compiler_params=_params(("arbitrary",)),
    )(zc, zc, o, conv_w, og, gmat_a, gmat_b, w_out, x, gate)


def _group_norm_bwd(dyn, y, og, gmat):
    rs = lax.rsqrt(_group_mean(y * y, gmat) + EPS)
    yhat = y * rs
    d_og = jnp.sum(dyn * yhat, axis=0, keepdims=True)
    dyh = dyn * og
    return rs * (dyh - yhat * _group_mean(dyh * yhat, gmat)), d_og


def mix_out_backward(dxo, y, gate, ya, o, og, gmat_a, gmat_b, w_out):
    t, d = dxo.shape
    tm = _tile(t, ROW_TILE, 16)
    w = CONV_WIDTH

    def body(dxo_ref, y_ref, gate_ref, ya_ref, o_ref, og_ref, ga_ref, gb_ref, w_ref,
             dy_ref, dya_ref, do_ref, delta_ref, sd_ref, so_ref):
        @pl.when(pl.program_id(0) == 0)
        def _():
            sd_ref[...] = jnp.zeros_like(sd_ref)
            so_ref[...] = jnp.zeros_like(so_ref)

        dxo_v = dxo_ref[...]
        dy = (gate_ref[...] * dxo_v).astype(BF16)
        dy_ref[...] = dy
        sd_ref[0:1, :] += jnp.sum(dxo_v * y_ref[...], axis=0, keepdims=True)
        dyn = _dot(dy, w_ref[...], NT)
        ogv = og_ref[...]
        ov = o_ref[...]
        dya, d_og_a = _group_norm_bwd(dyn[:, :w], ya_ref[...], ogv[:, :w], ga_ref[...])
        dov, d_og_b = _group_norm_bwd(dyn[:, w:], ov, ogv[:, w:], gb_ref[...])
        dya_ref[...] = dya
        do_ref[...] = dov.astype(BF16)
        so_ref[0:1, :w] += d_og_a
        so_ref[0:1, w:] += d_og_b
        prod = dov * ov
        for h in range(MLA_HEADS):
            delta_ref[h] = jnp.sum(prod[:, h * V_HEAD:(h + 1) * V_HEAD], axis=-1, keepdims=True)

    def rows(n):
        return pl.BlockSpec((tm, n), lambda i: (i, 0))

    return pl.pallas_call(
        body, name="mix_out_bwd", grid=(t // tm,),
        in_specs=[rows(d), rows(d), _row(gate), rows(w), rows(MLA_WIDTH), _row(og), _row(gmat_a), _row(gmat_b),
                  _row(w_out)],
        out_specs=[rows(d), rows(w), rows(MLA_WIDTH), pl.BlockSpec((MLA_HEADS, tm, 1), lambda i: (0, i, 0)),
                   pl.BlockSpec((8, d), lambda i: (0, 0)), pl.BlockSpec((8, MIX_WIDTH), lambda i: (0, 0))],
        out_shape=[jax.ShapeDtypeStruct((t, d), BF16), jax.ShapeDtypeStruct((t, w), F32),
                   jax.ShapeDtypeStruct((t, MLA_WIDTH), BF16), jax.ShapeDtypeStruct((MLA_HEADS, t, 1), F32),
                   jax.ShapeDtypeStruct((8, d), F32), jax.ShapeDtypeStruct((8, MIX_WIDTH), F32)],
        compiler_params=_params(("arbitrary",)),
    )(dxo, y, gate, ya, o, og, gmat_a, gmat_b, w_out)


def conv_backward(zc, dya, conv_w):
    t = zc.shape[0]
    tm = _tile(t, ROW_TILE, 16)
    nt = t // tm
    w = CONV_WIDTH

    def body(zc_ref, zp_ref, zn_ref, dya_ref, dn_ref, cw_ref, dzc_ref, sums_ref):
        i = pl.program_id(0)

        @pl.when(i == 0)
        def _():
            sums_ref[...] = jnp.zeros_like(sums_ref)

        zc_v = zc_ref[...]
        u, u1, u2 = _conv_taps(zc_v, zp_ref[...], i == 0)
        cw = cw_ref[...]
        dya_v = dya_ref[...]
        dyc = dya_v * zc_v[:, :w]
        dyc_next = jnp.where(i == nt - 1, 0.0, dn_ref[...] * zn_ref[...][:, :w])
        du = cw[2:3] * dyc + cw[1:2] * _advance_rows(dyc, dyc_next, 1) + cw[0:1] * _advance_rows(dyc, dyc_next, 2)
        dzc_ref[:, :w] = (dya_v * (cw[0:1] * u2 + cw[1:2] * u1 + cw[2:3] * u)).astype(BF16)
        dzc_ref[:, w:2 * w] = (du * zc_v[:, 2 * w:]).astype(BF16)
        dzc_ref[:, 2 * w:] = (du * zc_v[:, w:2 * w]).astype(BF16)
        _add_rows(sums_ref, [jnp.sum(dyc * tap, axis=0, keepdims=True) for tap in (u2, u1, u)])

    def rows(n):
        return pl.BlockSpec((tm, n), lambda i: (i, 0))

    def halo(n, step):
        last = t // 8 - 1
        return pl.BlockSpec((8, n), lambda i: (jnp.clip(i * (tm // 8) + step, 0, last), 0))

    return pl.pallas_call(
        body, name="conv_bwd", grid=(nt,),
        in_specs=[rows(ZC_COLS), halo(ZC_COLS, -1), halo(ZC_COLS, tm // 8), rows(w), halo(w, tm // 8), _row(conv_w)],
        out_specs=[rows(ZC_COLS), pl.BlockSpec((8, w), lambda i: (0, 0))],
        out_shape=[jax.ShapeDtypeStruct((t, ZC_COLS), BF16), jax.ShapeDtypeStruct((8, w), F32)],
        compiler_params=_params(("arbitrary",)),
    )(zc, zc, zc, dya, dya, conv_w)


def _rms_bwd(dy, x, g):
    xhat, r = _rms(x)
    d_g = jnp.sum(dy * xhat, axis=0, keepdims=True)
    dxh = dy * g
    return r * (dxh - xhat * jnp.mean(dxh * xhat, axis=-1, keepdims=True)), d_g


def mla_project_backward(dq, dk, dv, zm, pos, inv_freq, qg, kvg, w_uq, w_ukv):
    t = zm.shape[0]
    tm = _tile(t, ROW_TILE, 16)

    def body(dq_ref, dk_ref, dv_ref, zm_ref, pos_ref, if_ref, qg_ref, kvg_ref, wq_ref, wkv_ref,
             dql_ref, dkvl_ref, dzm_ref, sums_ref):
        @pl.when(pl.program_id(0) == 0)
        def _():
            sums_ref[...] = jnp.zeros_like(sums_ref)

        tables = _rope_tables(pos_ref[...], if_ref[...])
        dkr = jnp.zeros((tm, LANES), F32)
        for h in range(MLA_HEADS):
            lo = h * HEAD_PAD
            dql_ref[:, lo:lo + QK_NOPE] = dq_ref[:, lo:lo + QK_NOPE].astype(BF16)
            dql_ref[:, lo + QK_NOPE:lo + HEAD_PAD] = _rope_transposed(
                dq_ref[:, lo + QK_NOPE:lo + HEAD_PAD], tables).astype(BF16)
            dkvl_ref[:, h * QK_NOPE:(h + 1) * QK_NOPE] = dk_ref[:, lo:lo + QK_NOPE].astype(BF16)
            dkr = dkr + dk_ref[:, lo + QK_NOPE:lo + HEAD_PAD]
        dkvl_ref[:, MLA_HEADS * QK_NOPE:] = dv_ref[...].astype(BF16)
        zv = zm_ref[...]
        dqn = _dot(dql_ref[...], wq_ref[...])
        dkvn = _dot(dkvl_ref[...], wkv_ref[...])
        dcq, d_qg = _rms_bwd(dqn, zv[:, :Q_LORA], qg_ref[...])
        dckv, d_kvg = _rms_bwd(dkvn, zv[:, Q_LORA:Q_LORA + KV_LORA], kvg_ref[...])
        dzm_ref[:, :Q_LORA] = dcq.astype(BF16)
        dzm_ref[:, Q_LORA:Q_LORA + KV_LORA] = dckv.astype(BF16)
        dzm_ref[:, Q_LORA + KV_LORA:] = _rope_transposed(dkr, tables).astype(BF16)
        sums_ref[0:1, :Q_LORA] += d_qg
        sums_ref[0:1, Q_LORA:Q_LORA + KV_LORA] += d_kvg

    def rows(n):
        return pl.BlockSpec((tm, n), lambda i: (i, 0))

    return pl.pallas_call(
        body, name="mla_project_bwd", grid=(t // tm,),
        in_specs=[rows(QK_COLS), rows(QK_COLS), rows(MLA_WIDTH), rows(ZM_COLS), rows(1), _row(inv_freq),
                  _row(qg), _row(kvg), _row(w_uq), _row(w_ukv)],
        out_specs=[rows(QK_COLS), rows(QK_COLS), rows(ZM_COLS), pl.BlockSpec((8, ZM_COLS), lambda i: (0, 0))],
        out_shape=[jax.ShapeDtypeStruct((t, QK_COLS), BF16), jax.ShapeDtypeStruct((t, QK_COLS), BF16),
                   jax.ShapeDtypeStruct((t, ZM_COLS), BF16), jax.ShapeDtypeStruct((8, ZM_COLS), F32)],
        compiler_params=_params(("arbitrary",)),
    )(dq, dk, dv, zm, pos, inv_freq, qg, kvg, w_uq, w_ukv)


def mix_in_backward(dzc, dzm, w_in, x, dxo, gn, sc):
    t, d = x.shape
    tm = _tile(t, ROW_TILE, 16)

    def body(dzc_ref, dzm_ref, w_ref, x_ref, dxo_ref, gn_ref, sc_ref, dx_ref, sums_ref):
        @pl.when(pl.program_id(0) == 0)
        def _():
            sums_ref[...] = jnp.zeros_like(sums_ref)

        dh = _dot(dzc_ref[...], w_ref[:ZC_COLS, :]) + _dot(dzm_ref[...], w_ref[ZC_COLS:, :])
        dx, d_sh, d_sc, d_gn = _norm_mod_bwd(dh, x_ref[...], gn_ref[...], sc_ref[...])
        dx_ref[...] = dxo_ref[...] + dx
        _add_rows(sums_ref, [d_sh, d_sc, d_gn])

    def rows(n):
        return pl.BlockSpec((tm, n), lambda i: (i, 0))

    return pl.pallas_call(
        body, name="mix_in_bwd", grid=(t // tm,),
        in_specs=[rows(ZC_COLS), rows(ZM_COLS), _row(w_in), rows(d), rows(d), _row(gn), _row(sc)],
        out_specs=[rows(d), pl.BlockSpec((8, d), lambda i: (0, 0))],
        out_shape=[jax.ShapeDtypeStruct((t, d), F32), jax.ShapeDtypeStruct((8, d), F32)],
        compiler_params=_params(("arbitrary",)),
    )(dzc, dzm, w_in, x, dxo, gn, sc)


def final_loss(x, target, g):
    t, d = x.shape
    tm = _tile(t, ROW_TILE, 16)

    def body(x_ref, t_ref, g_ref, dx_ref, sums_ref):
        @pl.when(pl.program_id(0) == 0)
        def _():
            sums_ref[...] = jnp.zeros_like(sums_ref)

        gv = g_ref[...]
        xhat, r = _rms(x_ref[...])
        err = xhat * gv - t_ref[...]
        dy = err * (1.0 / d)
        dxh = dy * gv
        dx_ref[...] = r * (dxh - xhat * jnp.mean(dxh * xhat, axis=-1, keepdims=True))
        _add_rows(sums_ref, [jnp.sum(dy * xhat, axis=0, keepdims=True),
                             jnp.sum(err * err, axis=0, keepdims=True) * (0.5 / d)])

    row = pl.BlockSpec((tm, d), lambda i: (i, 0))
    return pl.pallas_call(
        body, name="final_loss", grid=(t // tm,),
        in_specs=[row, row, _row(g)],
        out_specs=[row, pl.BlockSpec((8, d), lambda i: (0, 0))],
        out_shape=[jax.ShapeDtypeStruct((t, d), F32), jax.ShapeDtypeStruct((8, d), F32)],
        compiler_params=_params(("arbitrary",)),
    )(x, target, g)


def adamw(w, g, m, v, name):
    r, n = w.shape
    tr = _tile(r, max(8, (1 << 19) // n), 8)

    def body(w_ref, g_ref, m_ref, v_ref, d_ref, mo_ref, vo_ref):
        gv = g_ref[...]
        m_new = ADAM_B1 * m_ref[...] + (1.0 - ADAM_B1) * gv
        v_new = ADAM_B2 * v_ref[...] + (1.0 - ADAM_B2) * (gv * gv)
        m_hat = m_new / (1.0 - ADAM_B1 ** ADAM_STEP)
        v_hat = v_new / (1.0 - ADAM_B2 ** ADAM_STEP)
        d_ref[...] = -ADAM_LR * (m_hat / (jnp.sqrt(v_hat) + ADAM_EPS) + ADAM_WD * w_ref[...])
        mo_ref[...] = m_new
        vo_ref[...] = v_new

    blk = pl.BlockSpec((tr, n), lambda i: (i, 0))
    shape = jax.ShapeDtypeStruct((r, n), F32)
    return pl.pallas_call(
        body, name=name, grid=(r // tr,), in_specs=[blk] * 4, out_specs=[blk] * 3, out_shape=[shape] * 3,
        compiler_params=_params(("arbitrary",)),
    )(w, g, m, v)


def _pad_to(v, n):
    return jnp.pad(v, (0, n - v.shape[0]))


def _pad_heads(w, axis_len):
    n = w.shape[1]
    return jnp.pad(w.reshape(MLA_HEADS, axis_len, n), ((0, 0), (0, HEAD_PAD - axis_len), (0, 0))).reshape(-1, n)


def _swap_head_parts(w, inner, outer):
    n = w.shape[1]
    return w.reshape(outer, inner, QK_NOPE, n).transpose(1, 0, 2, 3).reshape(-1, n)


def kernel(x, c, positions, ada_w, ada_b, norm_ffn1_g, ffn1_w1, ffn1_w3, ffn1_w2, norm_mix_g, w_in, conv_w, q_norm_g, w_uq, kv_norm_g, w_ukv, out_norm_g, w_out, norm_ffn2_g, ffn2_w1, ffn2_w3, ffn2_w2, final_norm_g, loss_target, m_ada_w, m_ada_b, m_norm_ffn1_g, m_ffn1_w1, m_ffn1_w3, m_ffn1_w2, m_norm_mix_g, m_w_in, m_conv_w, m_q_norm_g, m_w_uq, m_kv_norm_g, m_w_ukv, m_out_norm_g, m_w_out, m_norm_ffn2_g, m_ffn2_w1, m_ffn2_w3, m_ffn2_w2, m_final_norm_g, v_ada_w, v_ada_b, v_norm_ffn1_g, v_ffn1_w1, v_ffn1_w3, v_ffn1_w2, v_norm_mix_g, v_w_in, v_conv_w, v_q_norm_g, v_w_uq, v_kv_norm_g, v_w_ukv, v_out_norm_g, v_w_out, v_norm_ffn2_g, v_ffn2_w1, v_ffn2_w3, v_ffn2_w2, v_final_norm_g):
    t, d = x.shape[1], x.shape[2]
    f = ffn1_w2.shape[1] * N_DEV
    me = 4 * lax.axis_index("x") + 2 * lax.axis_index("y") + lax.axis_index("c")
    my_c = lax.axis_index("c")
    my_chip = 2 * lax.axis_index("x") + lax.axis_index("y")
    xs = x[0]
    n_ada = ada_w.shape[2]
    cw_n = conv_w.shape[2]

    c_rows = jnp.broadcast_to(c, (8, d))
    conv_rows = jnp.pad(conv_w[0], ((0, 8 - CONV_K), (0, LANES - cw_n)))
    c_all, conv_all = all_gather([c_rows, conv_rows], [0, 0], "gather_inputs")
    c_all = c_all[:, 0, :]
    conv_full8 = conv_all[:, :, :cw_n].transpose(1, 0, 2).reshape(8, CONV_WIDTH)

    ffn_blocks = jnp.stack([ffn1_w1[0].T, ffn1_w3[0].T, ffn1_w2[0], ffn2_w1[0].T, ffn2_w3[0].T, ffn2_w2[0]]).astype(BF16)
    gathered = all_gather(
        [ffn_blocks, w_in[0].T.astype(BF16), w_uq[0].T.astype(BF16), w_ukv[0].T.astype(BF16), w_out[0].astype(BF16)],
        [1, 0, 0, 0, 0], "gather_weights")
    ffn_ws = gathered[0].reshape(6, f, d)
    w_in_p = jnp.pad(gathered[1].reshape(IN_COLS, d), ((0, ZC_COLS + ZM_COLS - IN_COLS), (0, 0)))
    w_uq_p = _pad_heads(gathered[2].reshape(-1, Q_LORA), QK_NOPE + QK_ROPE)
    w_ukv_p = _swap_head_parts(gathered[3].reshape(-1, KV_LORA), 2, MLA_HEADS)
    w_out_f = gathered[4].reshape(MIX_WIDTH, d)

    ada_b_cols = lax.dynamic_slice_in_dim(ada_b, me * n_ada, n_ada, axis=1)
    mod_cols = ada_forward(c_all, ada_w[0], ada_b_cols)
    mod_all, = all_gather([mod_cols], [0], "gather_mod")
    mod = lax.dynamic_index_in_dim(mod_all, me, axis=1, keepdims=False).reshape(N_MOD, 1, d)
    sh1, sc1, g1, sh2, sc2, g2, sh3, sc3, g3 = [mod[i] for i in range(N_MOD)]

    gf = final_norm_g.reshape(1, d)
    x1, h1, a1, b1, y1 = ffn_forward(xs, norm_ffn1_g, sc1, sh1, g1, ffn_ws, 0, "ffn1_fwd")
    h2, zc, zm = mix_in_forward(x1, norm_mix_g, sc2, sh2, w_in_p)
    pos = positions[0].astype(F32).reshape(t, 1)
    inv_freq = ROPE_THETA ** (-jnp.arange(0, QK_ROPE, 2, dtype=F32) / QK_ROPE)
    inv_freq = jnp.concatenate([inv_freq, inv_freq, jnp.zeros((LANES - QK_ROPE,), F32)]).reshape(1, LANES)
    qn, kvn, q, k, v = mla_project(zm, pos, inv_freq, q_norm_g, kv_norm_g, w_uq_p, w_ukv_p)
    o, lse = attention_forward(q, k, v)
    lane = jnp.arange(CONV_WIDTH)
    gmat_a = (lane[:, None] // (CONV_WIDTH // CONV_GROUPS) == lane[None, :] // (CONV_WIDTH // CONV_GROUPS))
    gmat_a = (gmat_a / (CONV_WIDTH // CONV_GROUPS)).astype(BF16)
    gmat_b = ((lane[:, None] // V_HEAD == lane[None, :] // V_HEAD) / V_HEAD).astype(BF16)
    x2, yn, y2, ya = mix_out_forward(zc, o, conv_full8, out_norm_g, gmat_a, gmat_b, w_out_f, x1, g2)
    x3, h3, a3, b3, y3 = ffn_forward(x2, norm_ffn2_g, sc3, sh3, g3, ffn_ws, 3, "ffn2_fwd")
    dx3, sums_f = final_loss(x3, loss_target[0], gf)

    dx2, da3, db3, u3, dy3, sums_3 = ffn_backward(dx3, x2, a3, b3, y3, norm_ffn2_g, sc3, sh3, g3, ffn_ws, 3, "ffn2_bwd")
    g_w1b = matmul_tn(da3, h3, "ffn2_gw1")
    g_w3b = matmul_tn(db3, h3, "ffn2_gw3")
    g_w2b = matmul_tn(u3, dy3, "ffn2_gw2")
    dy2, dya, do, delta, sums_2d, sums_2o = mix_out_backward(dx2, y2, g2, ya, o, out_norm_g, gmat_a, gmat_b, w_out_f)
    g_w_out = matmul_tn(yn, dy2, "gw_out")
    nq = t // _tile(t, ATTN_TILE, CHUNK)
    stat_shape = (MLA_HEADS, nq, 1, t // nq)
    dq, dk, dv = attention_backward(q, k, v, do, lse.reshape(stat_shape), delta.reshape(stat_shape))
    dzc, sums_c = conv_backward(zc, dya, conv_full8)
    dql, dkvl, dzm, sums_m = mla_project_backward(dq, dk, dv, zm, pos, inv_freq, q_norm_g, kv_norm_g, w_uq_p, w_ukv_p)
    g_w_uq_p = matmul_tn(dql, qn, "gw_uq")
    g_w_ukv_p = matmul_tn(dkvl, kvn, "gw_ukv")
    g_w_in = jnp.concatenate([matmul_tn(dzc, h2, "gw_in_conv"), matmul_tn(dzm, h2, "gw_in_mla")])[:IN_COLS]
    dx1, sums_1m = mix_in_backward(dzc, dzm, w_in_p, x1, dx2, norm_mix_g, sc2)
    dx0, da1, db1, u1, dy1, sums_1 = ffn_backward(dx1, xs, a1, b1, y1, norm_ffn1_g, sc1, sh1, g1, ffn_ws, 0, "ffn1_bwd")
    g_w1a = matmul_tn(da1, h1, "ffn1_gw1")
    g_w3a = matmul_tn(db1, h1, "ffn1_gw3")
    g_w2a = matmul_tn(u1, dy1, "ffn1_gw2")
    g_w_uq = g_w_uq_p.reshape(MLA_HEADS, HEAD_PAD, Q_LORA)[:, :QK_NOPE + QK_ROPE].reshape(-1, Q_LORA)
    g_w_ukv = _swap_head_parts(g_w_ukv_p, MLA_HEADS, 2)

    dmod = jnp.concatenate([sums_1[0], sums_1[1], sums_1[2], sums_1m[0], sums_1m[1], sums_2d[0],
                            sums_3[0], sums_3[1], sums_3[2]])
    pieces = [dmod, sums_1[3], sums_1m[2], sums_m[0, :Q_LORA], sums_m[0, Q_LORA:Q_LORA + KV_LORA], sums_2o[0],
              sums_3[3], sums_f[0], sums_f[1], sums_c[:CONV_K].reshape(-1)]
    plens = [p.shape[0] for p in pieces]
    poffs = [sum(plens[:i]) for i in range(len(plens))]
    vec_len = -(-sum(plens) // 1024) * 1024
    vec = _pad_to(jnp.concatenate(pieces), vec_len).reshape(-1, LANES)
    vec_all, = all_gather([vec], [0], "gather_sums")
    tot = sum_devices(vec_all).reshape(-1)
    g_ada_b, g_n1, g_nmix, g_qg, g_kvg, g_og, g_n3, g_gf, loss_lanes, g_conv_full = [
        tot[o:o + n] for o, n in zip(poffs, plens)]
    loss = sum_lanes(loss_lanes.reshape(1, d))[0, 0]
    g_conv = lax.dynamic_slice_in_dim(g_conv_full.reshape(CONV_K, CONV_WIDTH), me * cw_n, cw_n, axis=1)
    dmod_all = vec_all.reshape(N_DEV, vec_len)[:, :N_MOD * d]
    dmod_cols = lax.dynamic_slice_in_dim(dmod_all, me * n_ada, n_ada, axis=1)
    g_ada_w = ada_backward(jnp.pad(c_all, ((0, 8), (0, 0))), jnp.pad(dmod_cols, ((0, 8), (0, 0))))

    names = ["ffn1_w1", "ffn1_w3", "ffn1_w2", "ffn2_w1", "ffn2_w3", "ffn2_w2", "w_in", "w_uq", "w_ukv", "w_out"]
    grads8 = [g.reshape(N_DEV, g.shape[0] // N_DEV, g.shape[1])
              for g in (g_w1a, g_w3a, g_w2a, g_w1b, g_w3b, g_w2b, g_w_in, g_w_uq, g_w_ukv, g_w_out)]
    got_sib = exchange_sibling(grads8)
    chip_idx = jnp.bitwise_xor(my_chip, jnp.array([0, 2, 1, 3], jnp.int32)).astype(jnp.int32)
    src_idx = (2 * chip_idx + my_c).astype(jnp.int32)
    sums = [add_sibling(g, got, src_idx, chip_idx, "rs_add_" + n) for g, got, n in zip(grads8, got_sib, names)]
    got_chips = exchange_chips([s[1] for s in sums])
    g_rows = [add_received(s[0], got, "rs_sum_" + n) for s, got, n in zip(sums, got_chips, names)]
    transposed = {"ffn1_w1", "ffn1_w3", "ffn2_w1", "ffn2_w3", "w_in", "w_uq", "w_ukv"}
    g_sh = {n: (g.T if n in transposed else g) for n, g in zip(names, g_rows)}

    def update(name, w, g, m, v):
        shape = w.shape
        two_d = (-1, shape[-1])
        dlt, nm, nv = adamw(w.reshape(two_d), g.reshape(two_d), m.reshape(two_d), v.reshape(two_d), "adamw_" + name)
        return g.reshape(shape), dlt.reshape(shape), nm.reshape(shape), nv.reshape(shape)

    res = {}
    res["ada_w"] = update("ada_w", ada_w, g_ada_w, m_ada_w, v_ada_w)
    big = [("ffn1_w1", ffn1_w1, m_ffn1_w1, v_ffn1_w1), ("ffn1_w3", ffn1_w3, m_ffn1_w3, v_ffn1_w3),
           ("ffn2_w1", ffn2_w1, m_ffn2_w1, v_ffn2_w1), ("ffn2_w3", ffn2_w3, m_ffn2_w3, v_ffn2_w3),
           ("w_in", w_in, m_w_in, v_w_in), ("w_uq", w_uq, m_w_uq, v_w_uq), ("w_ukv", w_ukv, m_w_ukv, v_w_ukv),
           ("ffn1_w2", ffn1_w2, m_ffn1_w2, v_ffn1_w2), ("ffn2_w2", ffn2_w2, m_ffn2_w2, v_ffn2_w2),
           ("w_out", w_out, m_w_out, v_w_out)]
    for name, w, m, v in big:
        res[name] = update(name, w, g_sh[name], m, v)
    smalls = [("ada_b", ada_b, g_ada_b, m_ada_b, v_ada_b),
              ("norm_ffn1_g", norm_ffn1_g, g_n1, m_norm_ffn1_g, v_norm_ffn1_g),
              ("norm_mix_g", norm_mix_g, g_nmix, m_norm_mix_g, v_norm_mix_g),
              ("conv_w", conv_w, g_conv, m_conv_w, v_conv_w),
              ("q_norm_g", q_norm_g, g_qg, m_q_norm_g, v_q_norm_g),
              ("kv_norm_g", kv_norm_g, g_kvg, m_kv_norm_g, v_kv_norm_g),
              ("out_norm_g", out_norm_g, g_og, m_out_norm_g, v_out_norm_g),
              ("norm_ffn2_g", norm_ffn2_g, g_n3, m_norm_ffn2_g, v_norm_ffn2_g),
              ("final_norm_g", final_norm_g, g_gf, m_final_norm_g, v_final_norm_g)]
    slens = [w.size for _, w, _, _, _ in smalls]
    soffs = [sum(slens[:i]) for i in range(len(slens))]
    s_len = -(-sum(slens) // 1024) * 1024

    def pack_small(i):
        return _pad_to(jnp.concatenate([s[i].reshape(-1) for s in smalls]), s_len).reshape(8, -1)

    s_out = adamw(pack_small(1), pack_small(2), pack_small(3), pack_small(4), "adamw_small")
    for (name, w, g, _, _), o, n in zip(smalls, soffs, slens):
        res[name] = (g.reshape(w.shape),) + tuple(a.reshape(-1)[o:o + n].reshape(w.shape) for a in s_out)

    order = ["ada_w", "ada_b", "norm_ffn1_g", "ffn1_w1", "ffn1_w3", "ffn1_w2", "norm_mix_g", "w_in", "conv_w",
             "q_norm_g", "w_uq", "kv_norm_g", "w_ukv", "out_norm_g", "w_out", "norm_ffn2_g", "ffn2_w1", "ffn2_w3",
             "ffn2_w2", "final_norm_g"]
    return (loss, dx0.reshape(x.shape), *[res[n][0] for n in order], *[res[n][1] for n in order],
            *[res[n][2] for n in order], *[res[n][3] for n in order])
```

```python
import functools

import jax
import jax.numpy as jnp
from jax import lax
from jax.experimental import pallas as pl
from jax.experimental.pallas import tpu as pltpu

F32 = jnp.float32
BF16 = jnp.bfloat16
MESH_ID = pl.DeviceIdType.MESH
N_DEV = 8

EPS = 1e-6
CHUNK = 64
N_MOD = 9
CONV_WIDTH = 512
CONV_GROUPS = 8
CONV_K = 3
MLA_HEADS = 4
QK_NOPE = 128
QK_ROPE = 64
V_HEAD = 128
Q_LORA = 384
KV_LORA = 256
ROPE_THETA = 10000.0
MLA_WIDTH = MLA_HEADS * V_HEAD
MIX_WIDTH = CONV_WIDTH + MLA_WIDTH
IN_COLS = 3 * CONV_WIDTH + Q_LORA + KV_LORA + QK_ROPE
ZC_COLS = 3 * CONV_WIDTH
ZM_COLS = Q_LORA + KV_LORA + 128
HEAD_PAD = 256
QK_COLS = MLA_HEADS * HEAD_PAD
ATTN_SCALE = (QK_NOPE + QK_ROPE) ** -0.5
NEG_INF = -1e30

ADAM_LR = 0.001
ADAM_B1 = 0.9
ADAM_B2 = 0.999
ADAM_EPS = 1e-08
ADAM_WD = 0.01
ADAM_STEP = 10

LANES = 128
VMEM_LIMIT = 56 * 1024 * 1024
ROW_TILE = 512
FFN_FWD_TILE = (1024, 256)
FFN_BWD_TILE = (256, 1408)
GRAD_TILE = 1408
ATTN_TILE = 512

NN = (((1,), (0,)), ((), ()))
NT = (((1,), (1,)), ((), ()))
TN = (((0,), (0,)), ((), ()))


def _dot(a, b, dims=NN):
    return lax.dot_general(a, b, dims, preferred_element_type=F32)


def _tile(n, cap, mult=LANES):
    best = None
    for t in range(mult, min(n, cap) + 1, mult):
        if n % t == 0:
            best = t
    return n if best is None else best


def _params(sem=None):
    return pltpu.CompilerParams(dimension_semantics=sem, vmem_limit_bytes=VMEM_LIMIT)


def _row(v):
    return pl.BlockSpec(v.shape, lambda *_: (0,) * v.ndim)


def _rms(x):
    r = lax.rsqrt(jnp.mean(x * x, axis=-1, keepdims=True) + EPS)
    return x * r, r


def _norm_mod_bwd(dh, x, gn, sc):
    xhat, r = _rms(x)
    d_sh = jnp.sum(dh, axis=0, keepdims=True)
    d_sc = jnp.sum(dh * (xhat * gn), axis=0, keepdims=True)
    dxn = dh * (1.0 + sc)
    d_gn = jnp.sum(dxn * xhat, axis=0, keepdims=True)
    dxh = dxn * gn
    dx = r * (dxh - xhat * jnp.mean(dxh * xhat, axis=-1, keepdims=True))
    return dx, d_sh, d_sc, d_gn


def _group_mean(v, gmat):
    hi = v.astype(BF16)
    lo = (v - hi.astype(F32)).astype(BF16)
    return _dot(hi, gmat) + _dot(lo, gmat)


def _add_rows(ref, rows):
    for r, v in enumerate(rows):
        ref[r:r + 1, :] += v


def _window(ref, axis, j):
    return ref.at[(slice(None),) * axis + (j,)]


def _any_specs(n):
    return [pl.BlockSpec(memory_space=pl.ANY)] * n


def all_gather(blocks, axes, name):
    n_arr = len(blocks)

    def body(*refs):
        start, forward, finish = _gather_steps(refs[:n_arr], refs[n_arr:2 * n_arr], axes, *refs[2 * n_arr:])
        start()
        for j in range(3):
            forward(j)
        finish()

    return pl.pallas_call(
        body, name=name, out_shape=_gathered_shapes(blocks, axes),
        in_specs=_any_specs(n_arr), out_specs=_any_specs(n_arr), scratch_shapes=_gather_sems(n_arr),
    )(*blocks)


def _gathered_shapes(blocks, axes):
    return [jax.ShapeDtypeStruct(b.shape[:ax] + (N_DEV,) + b.shape[ax:], b.dtype) for b, ax in zip(blocks, axes)]


def _gather_sems(n_arr):
    return [pltpu.SemaphoreType.DMA((7, n_arr)), pltpu.SemaphoreType.DMA((7, n_arr)), pltpu.SemaphoreType.DMA((n_arr,))]


def _gather_steps(ins, outs, axes, send_sems, recv_sems, local_sems):
    arrays = range(len(ins))
    x, y, c = lax.axis_index("x"), lax.axis_index("y"), lax.axis_index("c")
    me, sibling = (x, y, c), (x, y, 1 - c)
    chips = [(1 - x, y), (x, 1 - y), (1 - x, 1 - y)]

    def slot(a, px, py, pc):
        return _window(outs[a], axes[a], 4 * px + 2 * py + pc)

    def copy(a, k, block, to, src=None):
        return pltpu.make_async_remote_copy(
            src_ref=slot(a, *block) if src is None else src, dst_ref=slot(a, *block),
            send_sem=send_sems.at[k, a], recv_sem=recv_sems.at[k, a], device_id=to, device_id_type=MESH_ID)

    def mine(a):
        return pltpu.make_async_copy(ins[a], slot(a, *me), local_sems.at[a])

    def first():
        return ([copy(a, 0, me, sibling, src=ins[a]) for a in arrays]
                + [copy(a, 1 + j, me, (*chip, c), src=ins[a]) for j, chip in enumerate(chips) for a in arrays])

    def passed(j):
        return [copy(a, 4 + j, (*chips[j], c), sibling) for a in arrays]

    def start():
        for a in arrays:
            mine(a).start()
        for cp in first():
            cp.start()

    def forward(j):
        for a, cp in zip(arrays, passed(j)):
            copy(a, 1 + j, (*chips[j], c), me).wait_recv()
            cp.start()

    def finish():
        for a in arrays:
            copy(a, 0, sibling, me).wait_recv()
        for j, chip in enumerate(chips):
            for a in arrays:
                copy(a, 4 + j, (*chip, 1 - c), me).wait_recv()
        for cp in first() + passed(0) + passed(1) + passed(2):
            cp.wait_send()
        for a in arrays:
            mine(a).wait()

    return start, forward, finish


def exchange_sibling(grads, name):
    n_arr = len(grads)

    def body(*refs):
        ins, outs = refs[:n_arr], refs[n_arr:2 * n_arr]
        send_sems, recv_sems = refs[2 * n_arr:]
        x, y, c = lax.axis_index("x"), lax.axis_index("y"), lax.axis_index("c")

        def copy(a, src, dst):
            return pltpu.make_async_remote_copy(
                src_ref=src, dst_ref=dst, send_sem=send_sems.at[a], recv_sem=recv_sems.at[a],
                device_id=(x, y, 1 - c), device_id_type=MESH_ID)

        for a in range(n_arr):
            for k in range(4):
                copy(a, ins[a].at[2 * k + (1 - c)], outs[a].at[k]).start()
        whole = [copy(a, ins[a].at[pl.ds(0, 4)], outs[a]) for a in range(n_arr)]
        for cp in whole:
            cp.wait_recv()
        for cp in whole:
            cp.wait_send()

    return pl.pallas_call(
        body, name=name,
        out_shape=[jax.ShapeDtypeStruct((4,) + g.shape[1:], g.dtype) for g in grads],
        in_specs=_any_specs(n_arr), out_specs=_any_specs(n_arr),
        scratch_shapes=[pltpu.SemaphoreType.DMA((n_arr,)), pltpu.SemaphoreType.DMA((n_arr,))],
    )(*grads)


def exchange_chips(parts, name):
    n_arr = len(parts)

    def body(*refs):
        start, finish = _chip_exchange_steps(refs[:n_arr], refs[n_arr:2 * n_arr], *refs[2 * n_arr:])
        start()
        finish()

    return pl.pallas_call(
        body, name=name,
        out_shape=[jax.ShapeDtypeStruct(p.shape, p.dtype) for p in parts],
        in_specs=_any_specs(n_arr), out_specs=_any_specs(n_arr), scratch_shapes=_exchange_sems(n_arr),
    )(*parts)


def _exchange_sems(n_arr):
    return [pltpu.SemaphoreType.DMA((n_arr,)), pltpu.SemaphoreType.DMA((n_arr,))]


def _chip_exchange_steps(ins, outs, send_sems, recv_sems):
    x, y, c = lax.axis_index("x"), lax.axis_index("y"), lax.axis_index("c")
    chips = [(1 - x, y), (x, 1 - y), (1 - x, 1 - y)]

    def copy(a, src, dst, chip):
        return pltpu.make_async_remote_copy(
            src_ref=src, dst_ref=dst, send_sem=send_sems.at[a], recv_sem=recv_sems.at[a],
            device_id=(*chip, c), device_id_type=MESH_ID)

    def start():
        for a in range(len(ins)):
            for j, chip in enumerate(chips):
                copy(a, ins[a].at[j], outs[a].at[j], chip).start()

    def finish():
        whole = [copy(a, ins[a], outs[a], chips[0]) for a in range(len(ins))]
        for cp in whole:
            cp.wait_recv()
        for cp in whole:
            cp.wait_send()

    return start, finish


def riding_gather(blocks, axes):
    def phases(ins, outs, *sems):
        start, forward, finish = _gather_steps(ins, outs, axes, *sems)
        return [start] + [functools.partial(forward, j) for j in range(3)] + [finish]

    return dict(operands=blocks, out_shape=_gathered_shapes(blocks, axes), sems=_gather_sems(len(blocks)),
                phases=phases, when=("first", "late0", "late1", "late2", "last"))


def riding_exchange(parts):
    def phases(ins, outs, *sems):
        return list(_chip_exchange_steps(ins, outs, *sems))

    return dict(operands=parts, out_shape=[jax.ShapeDtypeStruct(p.shape, p.dtype) for p in parts],
                sems=_exchange_sems(len(parts)), phases=phases, when=("first", "last"))


def _call_with_rider(body, rider, *, name, grid, in_specs, out_specs, out_shape, scratch_shapes, operands):
    params = _params(("arbitrary",) * len(grid))
    if rider is None:
        return pl.pallas_call(body, name=name, grid=grid, in_specs=in_specs, out_specs=out_specs,
                              out_shape=out_shape, scratch_shapes=scratch_shapes, compiler_params=params)(*operands)
    n_in, n_out, n_scr, k = len(in_specs), len(out_specs), len(scratch_shapes), len(rider["operands"])
    rows, cols = grid
    assert cols >= 3 or "late0" not in rider["when"]
    late_row = max(rows - 2, 0)
    at = {"first": (0, 0), "last": (rows - 1, cols - 1),
          "late0": (late_row, 0), "late1": (late_row, 1), "late2": (late_row, 2)}

    def wrapped(*refs):
        ins, c_in = refs[:n_in], refs[n_in:n_in + k]
        outs, c_out = refs[n_in + k:n_in + k + n_out], refs[n_in + k + n_out:n_in + 2 * k + n_out]
        scratch, sems = refs[n_in + 2 * k + n_out:n_in + 2 * k + n_out + n_scr], refs[n_in + 2 * k + n_out + n_scr:]
        i, j = pl.program_id(0), pl.program_id(1)
        phases = rider["phases"](c_in, c_out, *sems)
        for fn, key in zip(phases, rider["when"]):
            if key != "last":
                pl.when(jnp.logical_and(i == at[key][0], j == at[key][1]))(fn)
        body(*ins, *outs, *scratch)
        pl.when(jnp.logical_and(i == at["last"][0], j == at["last"][1]))(phases[-1])

    return pl.pallas_call(
        wrapped, name=name, grid=grid,
        in_specs=list(in_specs) + _any_specs(k), out_specs=list(out_specs) + _any_specs(k),
        out_shape=list(out_shape) + rider["out_shape"], scratch_shapes=list(scratch_shapes) + rider["sems"],
        compiler_params=params)(*operands, *rider["operands"])


def add_sibling(g8, got, src_idx, chip_idx, name):
    _, r, n = g8.shape
    tr = _tile(r, 256, 16)

    def body(si_ref, ci_ref, g0_ref, g1_ref, g2_ref, g3_ref, got_ref, own_ref, send_ref):
        own_ref[...] = g0_ref[0] + got_ref[ci_ref[0]]
        for j, g_ref in enumerate((g1_ref, g2_ref, g3_ref)):
            send_ref[j] = (g_ref[0] + got_ref[ci_ref[j + 1]]).astype(BF16)

    def mine(j):
        return pl.BlockSpec((1, tr, n), lambda i, si, ci: (si[j], i, 0))

    return pl.pallas_call(
        body, name=name,
        out_shape=[jax.ShapeDtypeStruct((r, n), F32), jax.ShapeDtypeStruct((3, r, n), BF16)],
        grid_spec=pltpu.PrefetchScalarGridSpec(
            num_scalar_prefetch=2, grid=(r // tr,),
            in_specs=[mine(0), mine(1), mine(2), mine(3), pl.BlockSpec((4, tr, n), lambda i, si, ci: (0, i, 0))],
            out_specs=[pl.BlockSpec((tr, n), lambda i, si, ci: (i, 0)),
                       pl.BlockSpec((3, tr, n), lambda i, si, ci: (0, i, 0))]),
        compiler_params=_params(("arbitrary",)),
    )(src_idx, chip_idx, g8, g8, g8, g8, got)


def add_received(own, got, name):
    r, n = own.shape
    tr = _tile(r, 256, 16)

    def body(a_ref, b_ref, o_ref):
        acc = a_ref[...]
        for j in range(3):
            acc = acc + b_ref[j].astype(F32)
        o_ref[...] = acc

    return pl.pallas_call(
        body, name=name,
        out_shape=jax.ShapeDtypeStruct((r, n), F32),
        grid=(r // tr,),
        in_specs=[pl.BlockSpec((tr, n), lambda i: (i, 0)), pl.BlockSpec((3, tr, n), lambda i: (0, i, 0))],
        out_specs=pl.BlockSpec((tr, n), lambda i: (i, 0)),
        compiler_params=_params(("arbitrary",)),
    )(own, got)


def sum_devices(g):
    def body(g_ref, o_ref):
        acc = g_ref[0]
        for j in range(1, N_DEV):
            acc = acc + g_ref[j]
        o_ref[...] = acc

    return pl.pallas_call(body, name="sum_devices", out_shape=jax.ShapeDtypeStruct(g.shape[1:], F32))(g)


def sum_lanes(v):
    def body(v_ref, o_ref):
        o_ref[...] = jnp.broadcast_to(jnp.sum(v_ref[...], axis=-1, keepdims=True), (1, LANES))

    return pl.pallas_call(body, name="sum_lanes", out_shape=jax.ShapeDtypeStruct((1, LANES), F32))(v)


def ada_forward(c_all, ada_w, ada_b_cols):
    nb, n = c_all.shape[0], ada_w.shape[1]

    def body(c_ref, w_ref, b_ref, o_ref):
        cv = c_ref[...]
        s = (cv * jax.nn.sigmoid(cv)).astype(BF16)
        o_ref[...] = _dot(s, w_ref[...].astype(BF16)) + b_ref[...]

    return pl.pallas_call(body, name="ada_fwd", out_shape=jax.ShapeDtypeStruct((nb, n), F32),
                          compiler_params=_params())(c_all, ada_w, ada_b_cols)


def ada_backward(c_all16, dmod16):
    d, n = c_all16.shape[1], dmod16.shape[1]

    def body(c_ref, g_ref, o_ref):
        cv = c_ref[...]
        s = (cv * jax.nn.sigmoid(cv)).astype(BF16)
        o_ref[...] = _dot(s, g_ref[...].astype(BF16), TN)

    return pl.pallas_call(body, name="ada_bwd", out_shape=jax.ShapeDtypeStruct((d, n), F32),
                          compiler_params=_params())(c_all16, dmod16)


def ffn_forward(x, gn, sc, sh, gate, ws, first, name, rider=None):
    t, d = x.shape
    f = ws.shape[1]
    tm, tf = _tile(t, FFN_FWD_TILE[0], 16), _tile(f, FFN_FWD_TILE[1])
    nf = f // tf

    def body(x_ref, gn_ref, sc_ref, sh_ref, gate_ref, w1_ref, w3_ref, w2_ref,
             xo_ref, h_ref, a_ref, b_ref, y_ref, hs, acc):
        j = pl.program_id(1)

        @pl.when(j == 0)
        def _():
            xhat, _ = _rms(x_ref[...])
            h = (xhat * gn_ref[...] * (1.0 + sc_ref[...]) + sh_ref[...]).astype(BF16)
            hs[...] = h
            h_ref[...] = h
            acc[...] = jnp.zeros_like(acc)

        h = hs[...]
        a = _dot(h, w1_ref[...], NT)
        b = _dot(h, w3_ref[...], NT)
        a_ref[...] = a.astype(BF16)
        b_ref[...] = b.astype(BF16)
        u = (a * jax.nn.sigmoid(a) * b).astype(BF16)
        acc[...] += _dot(u, w2_ref[...])

        @pl.when(j == nf - 1)
        def _():
            y = acc[...]
            y_ref[...] = y
            xo_ref[...] = x_ref[...] + 0.5 * gate_ref[...] * y

    row = pl.BlockSpec((tm, d), lambda i, j: (i, 0))
    vec = pl.BlockSpec((1, d), lambda i, j: (0, 0))
    wide = pl.BlockSpec((tm, tf), lambda i, j: (i, j))
    return _call_with_rider(
        body, rider, name=name, grid=(t // tm, nf),
        in_specs=[row, vec, vec, vec, vec] + _ffn_weight_specs(first, tf, d),
        out_specs=[row, row, wide, wide, row],
        out_shape=[jax.ShapeDtypeStruct((t, d), F32), jax.ShapeDtypeStruct((t, d), BF16),
                   jax.ShapeDtypeStruct((t, f), BF16), jax.ShapeDtypeStruct((t, f), BF16),
                   jax.ShapeDtypeStruct((t, d), F32)],
        scratch_shapes=[pltpu.VMEM((tm, d), BF16), pltpu.VMEM((tm, d), F32)],
        operands=(x, gn, sc, sh, gate, ws, ws, ws))


def _ffn_weight_specs(first, tf, d):
    return [pl.BlockSpec((None, tf, d), lambda i, j, w=first + k: (w, j, 0)) for k in range(3)]


def ffn_backward(dxo, x, a, b, y, gn, sc, sh, gate, ws, first, name, rider=None):
    t, d = x.shape
    f = ws.shape[1]
    tm, tf = _tile(t, FFN_BWD_TILE[0], 16), _tile(f, FFN_BWD_TILE[1])
    nf = f // tf

    def body(dxo_ref, x_ref, a_ref, b_ref, y_ref, gn_ref, sc_ref, sh_ref, gate_ref, w1_ref, w3_ref, w2_ref,
             dx_ref, da_ref, db_ref, u_ref, dy_ref, sums_ref, dys, acc):
        i, j = pl.program_id(0), pl.program_id(1)

        @pl.when(jnp.logical_and(i == 0, j == 0))
        def _():
            sums_ref[...] = jnp.zeros_like(sums_ref)

        @pl.when(j == 0)
        def _():
            dy = (0.5 * gate_ref[...] * dxo_ref[...]).astype(BF16)
            dys[...] = dy
            dy_ref[...] = dy
            acc[...] = jnp.zeros_like(acc)

        du = _dot(dys[...], w2_ref[...], NT)
        av = a_ref[...].astype(F32)
        bv = b_ref[...].astype(F32)
        s = jax.nn.sigmoid(av)
        sa = av * s
        da = (du * bv * (s * (1.0 + av * (1.0 - s)))).astype(BF16)
        db = (du * sa).astype(BF16)
        da_ref[...] = da
        db_ref[...] = db
        u_ref[...] = (sa * bv).astype(BF16)
        acc[...] += _dot(da, w1_ref[...]) + _dot(db, w3_ref[...])

        @pl.when(j == nf - 1)
        def _():
            dxo_v = dxo_ref[...]
            dx, d_sh, d_sc, d_gn = _norm_mod_bwd(acc[...], x_ref[...], gn_ref[...], sc_ref[...])
            dx_ref[...] = dxo_v + dx
            d_gate = jnp.sum(dxo_v * (0.5 * y_ref[...]), axis=0, keepdims=True)
            _add_rows(sums_ref, [d_sh, d_sc, d_gate, d_gn])

    row = pl.BlockSpec((tm, d), lambda i, j: (i, 0))
    vec = pl.BlockSpec((1, d), lambda i, j: (0, 0))
    wide = pl.BlockSpec((tm, tf), lambda i, j: (i, j))
    return _call_with_rider(
        body, rider, name=name, grid=(t // tm, nf),
        in_specs=[row, row, wide, wide, row, vec, vec, vec, vec] + _ffn_weight_specs(first, tf, d),
        out_specs=[row, wide, wide, wide, row, pl.BlockSpec((8, d), lambda i, j: (0, 0))],
        out_shape=[jax.ShapeDtypeStruct((t, d), F32), jax.ShapeDtypeStruct((t, f), BF16),
                   jax.ShapeDtypeStruct((t, f), BF16), jax.ShapeDtypeStruct((t, f), BF16),
                   jax.ShapeDtypeStruct((t, d), BF16), jax.ShapeDtypeStruct((8, d), F32)],
        scratch_shapes=[pltpu.VMEM((tm, d), BF16), pltpu.VMEM((tm, d), F32)],
        operands=(dxo, x, a, b, y, gn, sc, sh, gate, ws, ws, ws))


def matmul_tn(a, b, name):
    t, m = a.shape
    n = b.shape[1]
    tm, tn, tk = _tile(m, GRAD_TILE), _tile(n, GRAD_TILE), _tile(t, 1024, 16)
    nk = t // tk

    def body(a_ref, b_ref, o_ref, acc):
        k = pl.program_id(2)

        @pl.when(k == 0)
        def _():
            acc[...] = jnp.zeros_like(acc)

        acc[...] += _dot(a_ref[...], b_ref[...], TN)

        @pl.when(k == nk - 1)
        def _():
            o_ref[...] = acc[...]

    return pl.pallas_call(
        body, name=name, grid=(m // tm, n // tn, nk),
        in_specs=[pl.BlockSpec((tk, tm), lambda i, j, k: (k, i)), pl.BlockSpec((tk, tn), lambda i, j, k: (k, j))],
        out_specs=pl.BlockSpec((tm, tn), lambda i, j, k: (i, j)),
        out_shape=jax.ShapeDtypeStruct((m, n), F32),
        scratch_shapes=[pltpu.VMEM((tm, tn), F32)],
        compiler_params=_params(("arbitrary", "arbitrary", "arbitrary")),
    )(a, b)


def mix_in_forward(x, gn, sc, sh, w_in):
    t, d = x.shape
    tm = _tile(t, ROW_TILE, 16)

    def body(x_ref, gn_ref, sc_ref, sh_ref, w_ref, h_ref, zc_ref, zm_ref):
        xhat, _ = _rms(x_ref[...])
        h = (xhat * gn_ref[...] * (1.0 + sc_ref[...]) + sh_ref[...]).astype(BF16)
        h_ref[...] = h
        z = _dot(h, w_ref[...], NT)
        zc_ref[...] = z[:, :ZC_COLS]
        zm_ref[...] = z[:, ZC_COLS:]

    row = pl.BlockSpec((tm, d), lambda i: (i, 0))
    vec = pl.BlockSpec((1, d), lambda i: (0, 0))
    return pl.pallas_call(
        body, name="mix_in_fwd", grid=(t // tm,),
        in_specs=[row, vec, vec, vec, _row(w_in)],
        out_specs=[row, pl.BlockSpec((tm, ZC_COLS), lambda i: (i, 0)), pl.BlockSpec((tm, ZM_COLS), lambda i: (i, 0))],
        out_shape=[jax.ShapeDtypeStruct((t, d), BF16), jax.ShapeDtypeStruct((t, ZC_COLS), F32),
                   jax.ShapeDtypeStruct((t, ZM_COLS), F32)],
        compiler_params=_params(("arbitrary",)),
    )(x, gn, sc, sh, w_in)


def _rope_tables(pos, inv_freq):
    ang = pos * inv_freq
    lane = lax.broadcasted_iota(jnp.int32, ang.shape, 1)
    cos, sin = jnp.cos(ang), jnp.sin(ang)
    half = QK_ROPE // 2
    return cos, jnp.where(lane < half, -sin, 0.0), jnp.where(jnp.logical_and(lane >= half, lane < QK_ROPE), sin, 0.0)


def _rope(v, tables):
    cos, sin_a, sin_b = tables
    return v * cos + pltpu.roll(v, LANES - QK_ROPE // 2, 1) * sin_a + pltpu.roll(v, QK_ROPE // 2, 1) * sin_b


def _rope_transposed(dv, tables):
    cos, sin_a, sin_b = tables
    return dv * cos + pltpu.roll(dv * sin_a, QK_ROPE // 2, 1) + pltpu.roll(dv * sin_b, LANES - QK_ROPE // 2, 1)


def mla_project(zm, pos, inv_freq, qg, kvg, w_uq, w_ukv):
    t = zm.shape[0]
    tm = _tile(t, ROW_TILE, 16)

    def body(zm_ref, pos_ref, if_ref, qg_ref, kvg_ref, wq_ref, wkv_ref, qn_ref, kvn_ref, q_ref, k_ref, v_ref):
        zv = zm_ref[...]
        qn = (_rms(zv[:, :Q_LORA])[0] * qg_ref[...]).astype(BF16)
        kvn = (_rms(zv[:, Q_LORA:Q_LORA + KV_LORA])[0] * kvg_ref[...]).astype(BF16)
        qn_ref[...] = qn
        kvn_ref[...] = kvn
        qf = _dot(qn, wq_ref[...], NT)
        kvf = _dot(kvn, wkv_ref[...], NT)
        tables = _rope_tables(pos_ref[...], if_ref[...])
        kr = _rope(zv[:, Q_LORA + KV_LORA:], tables).astype(BF16)
        for h in range(MLA_HEADS):
            lo = h * HEAD_PAD
            q_ref[:, lo:lo + QK_NOPE] = qf[:, lo:lo + QK_NOPE].astype(BF16)
            q_ref[:, lo + QK_NOPE:lo + HEAD_PAD] = _rope(qf[:, lo + QK_NOPE:lo + HEAD_PAD], tables).astype(BF16)
            k_ref[:, lo:lo + QK_NOPE] = kvf[:, h * QK_NOPE:(h + 1) * QK_NOPE].astype(BF16)
            k_ref[:, lo + QK_NOPE:lo + HEAD_PAD] = kr
        v_ref[...] = kvf[:, MLA_HEADS * QK_NOPE:].astype(BF16)

    def rows(n):
        return pl.BlockSpec((tm, n), lambda i: (i, 0))

    return pl.pallas_call(
        body, name="mla_project", grid=(t // tm,),
        in_specs=[rows(ZM_COLS), rows(1), _row(inv_freq), _row(qg), _row(kvg), _row(w_uq), _row(w_ukv)],
        out_specs=[rows(Q_LORA), rows(KV_LORA), rows(QK_COLS), rows(QK_COLS), rows(MLA_WIDTH)],
        out_shape=[jax.ShapeDtypeStruct((t, Q_LORA), BF16), jax.ShapeDtypeStruct((t, KV_LORA), BF16),
                   jax.ShapeDtypeStruct((t, QK_COLS), BF16), jax.ShapeDtypeStruct((t, QK_COLS), BF16),
                   jax.ShapeDtypeStruct((t, MLA_WIDTH), BF16)],
        compiler_params=_params(("arbitrary",)),
    )(zm, pos, inv_freq, qg, kvg, w_uq, w_ukv)


def _chunk_mask(shape, q_axis):
    qi = lax.broadcasted_iota(jnp.int32, shape, q_axis) // CHUNK
    ki = lax.broadcasted_iota(jnp.int32, shape, 1 - q_axis) // CHUNK
    return ki <= qi


def attention_forward(q, k, v):
    t = q.shape[0]
    tq = _tile(t, ATTN_TILE, CHUNK)

    def body(q_ref, k_ref, v_ref, o_ref, lse_ref):
        i = pl.program_id(1)
        qv = q_ref[...]

        def step(kb, carry, masked):
            m, l, acc = carry
            start = pl.multiple_of(kb * tq, tq)
            s = _dot(qv, k_ref[pl.ds(start, tq), :], NT) * ATTN_SCALE
            if masked:
                s = jnp.where(_chunk_mask(s.shape, 0), s, NEG_INF)
            m_new = jnp.maximum(m, jnp.max(s, axis=-1, keepdims=True))
            alpha = jnp.exp(m - m_new)
            p = jnp.exp(s - m_new)
            l = alpha * l + jnp.sum(p, axis=-1, keepdims=True)
            acc = alpha * acc + _dot(p.astype(BF16), v_ref[pl.ds(start, tq), :])
            return m_new, l, acc

        init = (jnp.full((tq, 1), NEG_INF, F32), jnp.zeros((tq, 1), F32), jnp.zeros((tq, V_HEAD), F32))
        carry = lax.fori_loop(0, i // 2, lambda pb, cr: step(2 * pb + 1, step(2 * pb, cr, False), False), init)
        carry = lax.fori_loop(0, i % 2, lambda _, cr: step(i - 1, cr, False), carry)
        m, l, acc = step(i, carry, True)
        o_ref[...] = acc / l
        lse_ref[0] = m + jnp.log(l)

    return pl.pallas_call(
        body, name="attn_fwd", grid=(MLA_HEADS, t // tq),
        in_specs=[pl.BlockSpec((tq, HEAD_PAD), lambda h, i: (i, h)),
                  pl.BlockSpec((t, HEAD_PAD), lambda h, i: (0, h)),
                  pl.BlockSpec((t, V_HEAD), lambda h, i: (0, h))],
        out_specs=[pl.BlockSpec((tq, V_HEAD), lambda h, i: (i, h)),
                   pl.BlockSpec((1, tq, 1), lambda h, i: (h, i, 0))],
        out_shape=[jax.ShapeDtypeStruct((t, MLA_WIDTH), F32), jax.ShapeDtypeStruct((MLA_HEADS, t, 1), F32)],
        compiler_params=_params(("arbitrary", "arbitrary")),
    )(q, k, v)


def attention_backward(q, k, v, do, lse, delta, rider=None):
    t = q.shape[0]
    tq = _tile(t, ATTN_TILE, CHUNK)
    nq = t // tq

    def body(q_ref, k_ref, v_ref, do_ref, lse_ref, delta_ref, dq_ref, dk_ref, dv_ref):
        kb = pl.program_id(1)

        @pl.when(kb == 0)
        def _():
            dq_ref[...] = jnp.zeros_like(dq_ref)

        kv, vv = k_ref[...], v_ref[...]

        def step(qb, carry, masked):
            dk, dv = carry
            rows = pl.ds(pl.multiple_of(qb * tq, tq), tq)
            qv, dov = q_ref[rows, :], do_ref[rows, :]
            s = _dot(kv, qv, NT) * ATTN_SCALE
            if masked:
                s = jnp.where(_chunk_mask(s.shape, 1), s, NEG_INF)
            p = jnp.exp(s - lse_ref[0, qb])
            dv = dv + _dot(p.astype(BF16), dov)
            dp = _dot(vv, dov, NT)
            ds = (p * (dp - delta_ref[0, qb]) * ATTN_SCALE).astype(BF16)
            dk = dk + _dot(ds, qv)
            dq_ref[rows, :] += _dot(ds, kv, TN)
            return dk, dv

        carry = step(kb, (jnp.zeros((tq, HEAD_PAD), F32), jnp.zeros((tq, V_HEAD), F32)), True)
        odd = (nq - 1 - kb) % 2
        carry = lax.fori_loop(0, odd, lambda _, cr: step(kb + 1, cr, False), carry)
        first = kb + 1 + odd
        dk, dv = lax.fori_loop(0, (nq - first) // 2,
                               lambda pb, cr: step(first + 2 * pb + 1, step(first + 2 * pb, cr, False), False), carry)
        dk_ref[...] = dk
        dv_ref[...] = dv

    stat = pl.BlockSpec((1, nq, 1, tq), lambda h, j: (h, 0, 0, 0))
    return _call_with_rider(
        body, rider, name="attn_bwd", grid=(MLA_HEADS, nq),
        in_specs=[pl.BlockSpec((t, HEAD_PAD), lambda h, j: (0, h)),
                  pl.BlockSpec((tq, HEAD_PAD), lambda h, j: (j, h)),
                  pl.BlockSpec((tq, V_HEAD), lambda h, j: (j, h)),
                  pl.BlockSpec((t, V_HEAD), lambda h, j: (0, h)), stat, stat],
        out_specs=[pl.BlockSpec((t, HEAD_PAD), lambda h, j: (0, h)),
                   pl.BlockSpec((tq, HEAD_PAD), lambda h, j: (j, h)),
                   pl.BlockSpec((tq, V_HEAD), lambda h, j: (j, h))],
        out_shape=[jax.ShapeDtypeStruct((t, QK_COLS), F32), jax.ShapeDtypeStruct((t, QK_COLS), F32),
                   jax.ShapeDtypeStruct((t, MLA_WIDTH), F32)],
        scratch_shapes=[], operands=(q, k, v, do, lse, delta))


def _shift_rows(v, prev, n):
    out = pltpu.roll(v, n, 0)
    row = lax.broadcasted_iota(jnp.int32, v.shape, 0)
    for r in range(n):
        out = jnp.where(row == r, prev[8 - n + r:8 - n + r + 1, :], out)
    return out


def _advance_rows(v, nxt, n):
    rows = v.shape[0]
    out = pltpu.roll(v, rows - n, 0)
    row = lax.broadcasted_iota(jnp.int32, v.shape, 0)
    for r in range(n):
        out = jnp.where(row == rows - n + r, nxt[r:r + 1, :], out)
    return out


def _conv_taps(zc, zc_prev, first):
    w = CONV_WIDTH
    u = zc[:, w:2 * w] * zc[:, 2 * w:]
    up = jnp.where(first, 0.0, zc_prev[:, w:2 * w] * zc_prev[:, 2 * w:])
    return u, _shift_rows(u, up, 1), _shift_rows(u, up, 2)


def mix_out_forward(zc, o, conv_w, og, gmat_a, gmat_b, w_out, x, gate):
    t, d = x.shape
    tm = _tile(t, ROW_TILE, 16)
    w = CONV_WIDTH

    def body(zc_ref, zp_ref, o_ref, cw_ref, og_ref, ga_ref, gb_ref, w_ref, x_ref, gate_ref,
             xo_ref, yn_ref, y_ref, ya_ref):
        zc_v = zc_ref[...]
        u, u1, u2 = _conv_taps(zc_v, zp_ref[...], pl.program_id(0) == 0)
        cw = cw_ref[...]
        ya = zc_v[:, :w] * (cw[0:1] * u2 + cw[1:2] * u1 + cw[2:3] * u)
        ya_ref[...] = ya
        ov = o_ref[...]
        ogv = og_ref[...]
        yn_ref[:, :w] = (ya * lax.rsqrt(_group_mean(ya * ya, ga_ref[...]) + EPS) * ogv[:, :w]).astype(BF16)
        yn_ref[:, w:] = (ov * lax.rsqrt(_group_mean(ov * ov, gb_ref[...]) + EPS) * ogv[:, w:]).astype(BF16)
        y = _dot(yn_ref[...], w_ref[...])
        y_ref[...] = y
        xo_ref[...] = x_ref[...] + gate_ref[...] * y

    def rows(n):
        return pl.BlockSpec((tm, n), lambda i: (i, 0))

    prev = pl.BlockSpec((8, ZC_COLS), lambda i: (jnp.maximum(i * (tm // 8) - 1, 0), 0))
    return pl.pallas_call(
        body, name="mix_out_fwd", grid=(t // tm,),
        in_specs=[rows(ZC_COLS), prev, rows(MLA_WIDTH), _row(conv_w), _row(og), _row(gmat_a), _row(gmat_b),
                  _row(w_out), rows(d), _row(gate)],
        out_specs=[rows(d), rows(MIX_WIDTH), rows(d), rows(w)],
        out_shape=[jax.ShapeDtypeStruct((t, d), F32), jax.ShapeDtypeStruct((t, MIX_WIDTH), BF16),
                   jax.ShapeDtypeStruct((t, d), F32), jax.ShapeDtypeStruct((t, w), F32)],
        compiler_params=_params(("arbitrary",)),
    )(zc, zc, o, conv_w, og, gmat_a, gmat_b, w_out, x, gate)


def _group_norm_bwd(dyn, y, og, gmat):
    rs = lax.rsqrt(_group_mean(y * y, gmat) + EPS)
    yhat = y * rs
    d_og = jnp.sum(dyn * yhat, axis=0, keepdims=True)
    dyh = dyn * og
    return rs * (dyh - yhat * _group_mean(dyh * yhat, gmat)), d_og


def mix_out_backward(dxo, y, gate, ya, o, og, gmat_a, gmat_b, w_out):
    t, d = dxo.shape
    tm = _tile(t, ROW_TILE, 16)
    w = CONV_WIDTH

    def body(dxo_ref, y_ref, gate_ref, ya_ref, o_ref, og_ref, ga_ref, gb_ref, w_ref,
             dy_ref, dya_ref, do_ref, delta_ref, sd_ref, so_ref):
        @pl.when(pl.program_id(0) == 0)
        def _():
            sd_ref[...] = jnp.zeros_like(sd_ref)
            so_ref[...] = jnp.zeros_like(so_ref)

        dxo_v = dxo_ref[...]
        dy = (gate_ref[...] * dxo_v).astype(BF16)
        dy_ref[...] = dy
        sd_ref[0:1, :] += jnp.sum(dxo_v * y_ref[...], axis=0, keepdims=True)
        dyn = _dot(dy, w_ref[...], NT)
        ogv = og_ref[...]
        ov = o_ref[...]
        dya, d_og_a = _group_norm_bwd(dyn[:, :w], ya_ref[...], ogv[:, :w], ga_ref[...])
        dov, d_og_b = _group_norm_bwd(dyn[:, w:], ov, ogv[:, w:], gb_ref[...])
        dya_ref[...] = dya
        do_ref[...] = dov.astype(BF16)
        so_ref[0:1, :w] += d_og_a
        so_ref[0:1, w:] += d_og_b
        prod = dov * ov
        for h in range(MLA_HEADS):
            delta_ref[h] = jnp.sum(prod[:, h * V_HEAD:(h + 1) * V_HEAD], axis=-1, keepdims=True)

    def rows(n):
        return pl.BlockSpec((tm, n), lambda i: (i, 0))

    return pl.pallas_call(
        body, name="mix_out_bwd", grid=(t // tm,),
        in_specs=[rows(d), rows(d), _row(gate), rows(w), rows(MLA_WIDTH), _row(og), _row(gmat_a), _row(gmat_b),
                  _row(w_out)],
        out_specs=[rows(d), rows(w), rows(MLA_WIDTH), pl.BlockSpec((MLA_HEADS, tm, 1), lambda i: (0, i, 0)),
                   pl.BlockSpec((8, d), lambda i: (0, 0)), pl.BlockSpec((8, MIX_WIDTH), lambda i: (0, 0))],
        out_shape=[jax.ShapeDtypeStruct((t, d), BF16), jax.ShapeDtypeStruct((t, w), F32),
                   jax.ShapeDtypeStruct((t, MLA_WIDTH), BF16), jax.ShapeDtypeStruct((MLA_HEADS, t, 1), F32),
                   jax.ShapeDtypeStruct((8, d), F32), jax.ShapeDtypeStruct((8, MIX_WIDTH), F32)],
        compiler_params=_params(("arbitrary",)),
    )(dxo, y, gate, ya, o, og, gmat_a, gmat_b, w_out)


def conv_backward(zc, dya, conv_w):
    t = zc.shape[0]
    tm = _tile(t, ROW_TILE, 16)
    nt = t // tm
    w = CONV_WIDTH

    def body(zc_ref, zp_ref, zn_ref, dya_ref, dn_ref, cw_ref, dzc_ref, sums_ref):
        i = pl.program_id(0)

        @pl.when(i == 0)
        def _():
            sums_ref[...] = jnp.zeros_like(sums_ref)

        zc_v = zc_ref[...]
        u, u1, u2 = _conv_taps(zc_v, zp_ref[...], i == 0)
        cw = cw_ref[...]
        dya_v = dya_ref[...]
        dyc = dya_v * zc_v[:, :w]
        dyc_next = jnp.where(i == nt - 1, 0.0, dn_ref[...] * zn_ref[...][:, :w])
        du = cw[2:3] * dyc + cw[1:2] * _advance_rows(dyc, dyc_next, 1) + cw[0:1] * _advance_rows(dyc, dyc_next, 2)
        dzc_ref[:, :w] = (dya_v * (cw[0:1] * u2 + cw[1:2] * u1 + cw[2:3] * u)).astype(BF16)
        dzc_ref[:, w:2 * w] = (du * zc_v[:, 2 * w:]).astype(BF16)
        dzc_ref[:, 2 * w:] = (du * zc_v[:, w:2 * w]).astype(BF16)
        _add_rows(sums_ref, [jnp.sum(dyc * tap, axis=0, keepdims=True) for tap in (u2, u1, u)])

    def rows(n):
        return pl.BlockSpec((tm, n), lambda i: (i, 0))

    def halo(n, step):
        last = t // 8 - 1
        return pl.BlockSpec((8, n), lambda i: (jnp.clip(i * (tm // 8) + step, 0, last), 0))

    return pl.pallas_call(
        body, name="conv_bwd", grid=(nt,),
        in_specs=[rows(ZC_COLS), halo(ZC_COLS, -1), halo(ZC_COLS, tm // 8), rows(w), halo(w, tm // 8), _row(conv_w)],
        out_specs=[rows(ZC_COLS), pl.BlockSpec((8, w), lambda i: (0, 0))],
        out_shape=[jax.ShapeDtypeStruct((t, ZC_COLS), BF16), jax.ShapeDtypeStruct((8, w), F32)],
        compiler_params=_params(("arbitrary",)),
    )(zc, zc, zc, dya, dya, conv_w)


def _rms_bwd(dy, x, g):
    xhat, r = _rms(x)
    d_g = jnp.sum(dy * xhat, axis=0, keepdims=True)
    dxh = dy * g
    return r * (dxh - xhat * jnp.mean(dxh * xhat, axis=-1, keepdims=True)), d_g


def mla_project_backward(dq, dk, dv, zm, pos, inv_freq, qg, kvg, w_uq, w_ukv):
    t = zm.shape[0]
    tm = _tile(t, ROW_TILE, 16)

    def body(dq_ref, dk_ref, dv_ref, zm_ref, pos_ref, if_ref, qg_ref, kvg_ref, wq_ref, wkv_ref,
             dql_ref, dkvl_ref, dzm_ref, sums_ref):
        @pl.when(pl.program_id(0) == 0)
        def _():
            sums_ref[...] = jnp.zeros_like(sums_ref)

        tables = _rope_tables(pos_ref[...], if_ref[...])
        dkr = jnp.zeros((tm, LANES), F32)
        for h in range(MLA_HEADS):
            lo = h * HEAD_PAD
            dql_ref[:, lo:lo + QK_NOPE] = dq_ref[:, lo:lo + QK_NOPE].astype(BF16)
            dql_ref[:, lo + QK_NOPE:lo + HEAD_PAD] = _rope_transposed(
                dq_ref[:, lo + QK_NOPE:lo + HEAD_PAD], tables).astype(BF16)
            dkvl_ref[:, h * QK_NOPE:(h + 1) * QK_NOPE] = dk_ref[:, lo:lo + QK_NOPE].astype(BF16)
            dkr = dkr + dk_ref[:, lo + QK_NOPE:lo + HEAD_PAD]
        dkvl_ref[:, MLA_HEADS * QK_NOPE:] = dv_ref[...].astype(BF16)
        zv = zm_ref[...]
        dqn = _dot(dql_ref[...], wq_ref[...])
        dkvn = _dot(dkvl_ref[...], wkv_ref[...])
        dcq, d_qg = _rms_bwd(dqn, zv[:, :Q_LORA], qg_ref[...])
        dckv, d_kvg = _rms_bwd(dkvn, zv[:, Q_LORA:Q_LORA + KV_LORA], kvg_ref[...])
        dzm_ref[:, :Q_LORA] = dcq.astype(BF16)
        dzm_ref[:, Q_LORA:Q_LORA + KV_LORA] = dckv.astype(BF16)
        dzm_ref[:, Q_LORA + KV_LORA:] = _rope_transposed(dkr, tables).astype(BF16)
        sums_ref[0:1, :Q_LORA] += d_qg
        sums_ref[0:1, Q_LORA:Q_LORA + KV_LORA] += d_kvg

    def rows(n):
        return pl.BlockSpec((tm, n), lambda i: (i, 0))

    return pl.pallas_call(
        body, name="mla_project_bwd", grid=(t // tm,),
        in_specs=[rows(QK_COLS), rows(QK_COLS), rows(MLA_WIDTH), rows(ZM_COLS), rows(1), _row(inv_freq),
                  _row(qg), _row(kvg), _row(w_uq), _row(w_ukv)],
        out_specs=[rows(QK_COLS), rows(QK_COLS), rows(ZM_COLS), pl.BlockSpec((8, ZM_COLS), lambda i: (0, 0))],
        out_shape=[jax.ShapeDtypeStruct((t, QK_COLS), BF16), jax.ShapeDtypeStruct((t, QK_COLS), BF16),
                   jax.ShapeDtypeStruct((t, ZM_COLS), BF16), jax.ShapeDtypeStruct((8, ZM_COLS), F32)],
        compiler_params=_params(("arbitrary",)),
    )(dq, dk, dv, zm, pos, inv_freq, qg, kvg, w_uq, w_ukv)


def mix_in_backward(dzc, dzm, w_in, x, dxo, gn, sc):
    t, d = x.shape
    tm = _tile(t, ROW_TILE, 16)

    def body(dzc_ref, dzm_ref, w_ref, x_ref, dxo_ref, gn_ref, sc_ref, dx_ref, sums_ref):
        @pl.when(pl.program_id(0) == 0)
        def _():
            sums_ref[...] = jnp.zeros_like(sums_ref)

        dh = _dot(dzc_ref[...], w_ref[:ZC_COLS, :]) + _dot(dzm_ref[...], w_ref[ZC_COLS:, :])
        dx, d_sh, d_sc, d_gn = _norm_mod_bwd(dh, x_ref[...], gn_ref[...], sc_ref[...])
        dx_ref[...] = dxo_ref[...] + dx
        _add_rows(sums_ref, [d_sh, d_sc, d_gn])

    def rows(n):
        return pl.BlockSpec((tm, n), lambda i: (i, 0))

    return pl.pallas_call(
        body, name="mix_in_bwd", grid=(t // tm,),
        in_specs=[rows(ZC_COLS), rows(ZM_COLS), _row(w_in), rows(d), rows(d), _row(gn), _row(sc)],
        out_specs=[rows(d), pl.BlockSpec((8, d), lambda i: (0, 0))],
        out_shape=[jax.ShapeDtypeStruct((t, d), F32), jax.ShapeDtypeStruct((8, d), F32)],
        compiler_params=_params(("arbitrary",)),
    )(dzc, dzm, w_in, x, dxo, gn, sc)


def final_loss(x, target, g):
    t, d = x.shape
    tm = _tile(t, ROW_TILE, 16)

    def body(x_ref, t_ref, g_ref, dx_ref, sums_ref):
        @pl.when(pl.program_id(0) == 0)
        def _():
            sums_ref[...] = jnp.zeros_like(sums_ref)

        gv = g_ref[...]
        xhat, r = _rms(x_ref[...])
        err = xhat * gv - t_ref[...]
        dy = err * (1.0 / d)
        dxh = dy * gv
        dx_ref[...] = r * (dxh - xhat * jnp.mean(dxh * xhat, axis=-1, keepdims=True))
        _add_rows(sums_ref, [jnp.sum(dy * xhat, axis=0, keepdims=True),
                             jnp.sum(err * err, axis=0, keepdims=True) * (0.5 / d)])

    row = pl.BlockSpec((tm, d), lambda i: (i, 0))
    return pl.pallas_call(
        body, name="final_loss", grid=(t // tm,),
        in_specs=[row, row, _row(g)],
        out_specs=[row, pl.BlockSpec((8, d), lambda i: (0, 0))],
        out_shape=[jax.ShapeDtypeStruct((t, d), F32), jax.ShapeDtypeStruct((8, d), F32)],
        compiler_params=_params(("arbitrary",)),
    )(x, target, g)


def adamw(w, g, m, v, name):
    r, n = w.shape
    tr = _tile(r, max(8, (1 << 19) // n), 8)

    def body(w_ref, g_ref, m_ref, v_ref, d_ref, mo_ref, vo_ref):
        gv = g_ref[...]
        m_new = ADAM_B1 * m_ref[...] + (1.0 - ADAM_B1) * gv
        v_new = ADAM_B2 * v_ref[...] + (1.0 - ADAM_B2) * (gv * gv)
        m_hat = m_new / (1.0 - ADAM_B1 ** ADAM_STEP)
        v_hat = v_new / (1.0 - ADAM_B2 ** ADAM_STEP)
        d_ref[...] = -ADAM_LR * (m_hat / (jnp.sqrt(v_hat) + ADAM_EPS) + ADAM_WD * w_ref[...])
        mo_ref[...] = m_new
        vo_ref[...] = v_new

    blk = pl.BlockSpec((tr, n), lambda i: (i, 0))
    shape = jax.ShapeDtypeStruct((r, n), F32)
    return pl.pallas_call(
        body, name=name, grid=(r // tr,), in_specs=[blk] * 4, out_specs=[blk] * 3, out_shape=[shape] * 3,
        compiler_params=_params(("arbitrary",)),
    )(w, g, m, v)


def _pad_to(v, n):
    return jnp.pad(v, (0, n - v.shape[0]))


def _pad_heads(w, axis_len):
    n = w.shape[1]
    return jnp.pad(w.reshape(MLA_HEADS, axis_len, n), ((0, 0), (0, HEAD_PAD - axis_len), (0, 0))).reshape(-1, n)


def _swap_head_parts(w, inner, outer):
    n = w.shape[1]
    return w.reshape(outer, inner, QK_NOPE, n).transpose(1, 0, 2, 3).reshape(-1, n)


def kernel(x, c, positions, ada_w, ada_b, norm_ffn1_g, ffn1_w1, ffn1_w3, ffn1_w2, norm_mix_g, w_in, conv_w, q_norm_g, w_uq, kv_norm_g, w_ukv, out_norm_g, w_out, norm_ffn2_g, ffn2_w1, ffn2_w3, ffn2_w2, final_norm_g, loss_target, m_ada_w, m_ada_b, m_norm_ffn1_g, m_ffn1_w1, m_ffn1_w3, m_ffn1_w2, m_norm_mix_g, m_w_in, m_conv_w, m_q_norm_g, m_w_uq, m_kv_norm_g, m_w_ukv, m_out_norm_g, m_w_out, m_norm_ffn2_g, m_ffn2_w1, m_ffn2_w3, m_ffn2_w2, m_final_norm_g, v_ada_w, v_ada_b, v_norm_ffn1_g, v_ffn1_w1, v_ffn1_w3, v_ffn1_w2, v_norm_mix_g, v_w_in, v_conv_w, v_q_norm_g, v_w_uq, v_kv_norm_g, v_w_ukv, v_out_norm_g, v_w_out, v_norm_ffn2_g, v_ffn2_w1, v_ffn2_w3, v_ffn2_w2, v_final_norm_g):
    t, d = x.shape[1], x.shape[2]
    f = ffn1_w2.shape[1] * N_DEV
    me = 4 * lax.axis_index("x") + 2 * lax.axis_index("y") + lax.axis_index("c")
    my_c = lax.axis_index("c")
    my_chip = 2 * lax.axis_index("x") + lax.axis_index("y")
    xs = x[0]
    n_ada = ada_w.shape[2]
    cw_n = conv_w.shape[2]

    c_rows = jnp.broadcast_to(c, (8, d))
    conv_rows = jnp.pad(conv_w[0], ((0, 8 - CONV_K), (0, LANES - cw_n)))
    ffn1_blocks = jnp.stack([ffn1_w1[0].T, ffn1_w3[0].T, ffn1_w2[0]]).astype(BF16)
    ffn2_blocks = jnp.stack([ffn2_w1[0].T, ffn2_w3[0].T, ffn2_w2[0]]).astype(BF16)
    c_all, conv_all, ffn1_all = all_gather([c_rows, conv_rows, ffn1_blocks], [0, 0, 1], "gather_first")
    c_all = c_all[:, 0, :]
    conv_full8 = conv_all[:, :, :cw_n].transpose(1, 0, 2).reshape(8, CONV_WIDTH)
    ffn1_ws = ffn1_all.reshape(3, f, d)
    gather_rest = riding_gather(
        [ffn2_blocks, w_in[0].T.astype(BF16), w_uq[0].T.astype(BF16), w_ukv[0].T.astype(BF16), w_out[0].astype(BF16)],
        [1, 0, 0, 0, 0])

    ada_b_cols = lax.dynamic_slice_in_dim(ada_b, me * n_ada, n_ada, axis=1)
    mod_cols = ada_forward(c_all, ada_w[0], ada_b_cols)
    mod_all, = all_gather([mod_cols], [0], "gather_mod")
    mod = lax.dynamic_index_in_dim(mod_all, me, axis=1, keepdims=False).reshape(N_MOD, 1, d)
    sh1, sc1, g1, sh2, sc2, g2, sh3, sc3, g3 = [mod[i] for i in range(N_MOD)]

    gf = final_norm_g.reshape(1, d)
    x1, h1, a1, b1, y1, *gathered = ffn_forward(xs, norm_ffn1_g, sc1, sh1, g1, ffn1_ws, 0, "ffn1_fwd", gather_rest)
    ffn2_ws = gathered[0].reshape(3, f, d)
    w_in_p = jnp.pad(gathered[1].reshape(IN_COLS, d), ((0, ZC_COLS + ZM_COLS - IN_COLS), (0, 0)))
    w_uq_p = _pad_heads(gathered[2].reshape(-1, Q_LORA), QK_NOPE + QK_ROPE)
    w_ukv_p = _swap_head_parts(gathered[3].reshape(-1, KV_LORA), 2, MLA_HEADS)
    w_out_f = gathered[4].reshape(MIX_WIDTH, d)
    h2, zc, zm = mix_in_forward(x1, norm_mix_g, sc2, sh2, w_in_p)
    pos = positions[0].astype(F32).reshape(t, 1)
    inv_freq = ROPE_THETA ** (-jnp.arange(0, QK_ROPE, 2, dtype=F32) / QK_ROPE)
    inv_freq = jnp.concatenate([inv_freq, inv_freq, jnp.zeros((LANES - QK_ROPE,), F32)]).reshape(1, LANES)
    qn, kvn, q, k, v = mla_project(zm, pos, inv_freq, q_norm_g, kv_norm_g, w_uq_p, w_ukv_p)
    o, lse = attention_forward(q, k, v)
    lane = jnp.arange(CONV_WIDTH)
    gmat_a = (lane[:, None] // (CONV_WIDTH // CONV_GROUPS) == lane[None, :] // (CONV_WIDTH // CONV_GROUPS))
    gmat_a = (gmat_a / (CONV_WIDTH // CONV_GROUPS)).astype(BF16)
    gmat_b = ((lane[:, None] // V_HEAD == lane[None, :] // V_HEAD) / V_HEAD).astype(BF16)
    x2, yn, y2, ya = mix_out_forward(zc, o, conv_full8, out_norm_g, gmat_a, gmat_b, w_out_f, x1, g2)
    x3, h3, a3, b3, y3 = ffn_forward(x2, norm_ffn2_g, sc3, sh3, g3, ffn2_ws, 0, "ffn2_fwd")
    dx3, sums_f = final_loss(x3, loss_target[0], gf)

    chip_idx = jnp.bitwise_xor(my_chip, jnp.array([0, 2, 1, 3], jnp.int32)).astype(jnp.int32)
    src_idx = (2 * chip_idx + my_c).astype(jnp.int32)

    def chip_sums(tag, named):
        g8 = [g.reshape(N_DEV, g.shape[0] // N_DEV, g.shape[1]) for _, g in named]
        got = exchange_sibling(g8, "rs_sibling_" + tag)
        return [add_sibling(g, r, src_idx, chip_idx, "rs_add_" + n) for g, r, (n, _) in zip(g8, got, named)]

    dx2, da3, db3, u3, dy3, sums_3 = ffn_backward(dx3, x2, a3, b3, y3, norm_ffn2_g, sc3, sh3, g3, ffn2_ws, 0, "ffn2_bwd")
    ffn2_named = [("ffn2_w1", matmul_tn(da3, h3, "ffn2_gw1")), ("ffn2_w3", matmul_tn(db3, h3, "ffn2_gw3")),
                  ("ffn2_w2", matmul_tn(u3, dy3, "ffn2_gw2"))]
    ffn2_sums = chip_sums("ffn2", ffn2_named)
    dy2, dya, do, delta, sums_2d, sums_2o = mix_out_backward(dx2, y2, g2, ya, o, out_norm_g, gmat_a, gmat_b, w_out_f)
    g_w_out = matmul_tn(yn, dy2, "gw_out")
    nq = t // _tile(t, ATTN_TILE, CHUNK)
    stat_shape = (MLA_HEADS, nq, 1, t // nq)
    dq, dk, dv, *ffn2_got = attention_backward(q, k, v, do, lse.reshape(stat_shape), delta.reshape(stat_shape),
                                               riding_exchange([s[1] for s in ffn2_sums]))
    dzc, sums_c = conv_backward(zc, dya, conv_full8)
    dql, dkvl, dzm, sums_m = mla_project_backward(dq, dk, dv, zm, pos, inv_freq, q_norm_g, kv_norm_g, w_uq_p, w_ukv_p)
    g_w_uq_p = matmul_tn(dql, qn, "gw_uq")
    g_w_ukv_p = matmul_tn(dkvl, kvn, "gw_ukv")
    g_w_in = jnp.concatenate([matmul_tn(dzc, h2, "gw_in_conv"), matmul_tn(dzm, h2, "gw_in_mla")])[:IN_COLS]
    g_w_uq = g_w_uq_p.reshape(MLA_HEADS, HEAD_PAD, Q_LORA)[:, :QK_NOPE + QK_ROPE].reshape(-1, Q_LORA)
    g_w_ukv = _swap_head_parts(g_w_ukv_p, MLA_HEADS, 2)
    mix_named = [("w_in", g_w_in), ("w_uq", g_w_uq), ("w_ukv", g_w_ukv), ("w_out", g_w_out)]
    mix_sums = chip_sums("mix", mix_named)
    dx1, sums_1m = mix_in_backward(dzc, dzm, w_in_p, x1, dx2, norm_mix_g, sc2)
    dx0, da1, db1, u1, dy1, sums_1, *mix_got = ffn_backward(
        dx1, xs, a1, b1, y1, norm_ffn1_g, sc1, sh1, g1, ffn1_ws, 0, "ffn1_bwd", riding_exchange([s[1] for s in mix_sums]))
    ffn1_named = [("ffn1_w1", matmul_tn(da1, h1, "ffn1_gw1")), ("ffn1_w3", matmul_tn(db1, h1, "ffn1_gw3")),
                  ("ffn1_w2", matmul_tn(u1, dy1, "ffn1_gw2"))]
    ffn1_sums = chip_sums("ffn1", ffn1_named)
    ffn1_got = exchange_chips([s[1] for s in ffn1_sums], "rs_chips_ffn1")
    transposed = {"ffn1_w1", "ffn1_w3", "ffn2_w1", "ffn2_w3", "w_in", "w_uq", "w_ukv"}
    g_sh = {}
    for named, group_sums, group_got in ((ffn2_named, ffn2_sums, ffn2_got), (mix_named, mix_sums, mix_got),
                                         (ffn1_named, ffn1_sums, ffn1_got)):
        for (n, _), (own, _), got in zip(named, group_sums, group_got):
            g_rows = add_received(own, got, "rs_sum_" + n)
            g_sh[n] = g_rows.T if n in transposed else g_rows

    dmod = jnp.concatenate([sums_1[0], sums_1[1], sums_1[2], sums_1m[0], sums_1m[1], sums_2d[0],
                            sums_3[0], sums_3[1], sums_3[2]])
    pieces = [dmod, sums_1[3], sums_1m[2], sums_m[0, :Q_LORA], sums_m[0, Q_LORA:Q_LORA + KV_LORA], sums_2o[0],
              sums_3[3], sums_f[0], sums_f[1], sums_c[:CONV_K].reshape(-1)]
    plens = [p.shape[0] for p in pieces]
    poffs = [sum(plens[:i]) for i in range(len(plens))]
    vec_len = -(-sum(plens) // 1024) * 1024
    vec = _pad_to(jnp.concatenate(pieces), vec_len).reshape(-1, LANES)
    vec_all, = all_gather([vec], [0], "gather_sums")
    tot = sum_devices(vec_all).reshape(-1)
    g_ada_b, g_n1, g_nmix, g_qg, g_kvg, g_og, g_n3, g_gf, loss_lanes, g_conv_full = [
        tot[o:o + n] for o, n in zip(poffs, plens)]
    loss = sum_lanes(loss_lanes.reshape(1, d))[0, 0]
    g_conv = lax.dynamic_slice_in_dim(g_conv_full.reshape(CONV_K, CONV_WIDTH), me * cw_n, cw_n, axis=1)
    dmod_all = vec_all.reshape(N_DEV, vec_len)[:, :N_MOD * d]
    dmod_cols = lax.dynamic_slice_in_dim(dmod_all, me * n_ada, n_ada, axis=1)
    g_ada_w = ada_backward(jnp.pad(c_all, ((0, 8), (0, 0))), jnp.pad(dmod_cols, ((0, 8), (0, 0))))

    def update(name, w, g, m, v):
        shape = w.shape
        two_d = (-1, shape[-1])
        dlt, nm, nv = adamw(w.reshape(two_d), g.reshape(two_d), m.reshape(two_d), v.reshape(two_d), "adamw_" + name)
        return g.reshape(shape), dlt.reshape(shape), nm.reshape(shape), nv.reshape(shape)

    res = {}
    res["ada_w"] = update("ada_w", ada_w, g_ada_w, m_ada_w, v_ada_w)
    big = [("ffn1_w1", ffn1_w1, m_ffn1_w1, v_ffn1_w1), ("ffn1_w3", ffn1_w3, m_ffn1_w3, v_ffn1_w3),
           ("ffn2_w1", ffn2_w1, m_ffn2_w1, v_ffn2_w1), ("ffn2_w3", ffn2_w3, m_ffn2_w3, v_ffn2_w3),
           ("w_in", w_in, m_w_in, v_w_in), ("w_uq", w_uq, m_w_uq, v_w_uq), ("w_ukv", w_ukv, m_w_ukv, v_w_ukv),
           ("ffn1_w2", ffn1_w2, m_ffn1_w2, v_ffn1_w2), ("ffn2_w2", ffn2_w2, m_ffn2_w2, v_ffn2_w2),
           ("w_out", w_out, m_w_out, v_w_out)]
    for name, w, m, v in big:
        res[name] = update(name, w, g_sh[name], m, v)
    smalls = [("ada_b", ada_b, g_ada_b, m_ada_b, v_ada_b),
              ("norm_ffn1_g", norm_ffn1_g, g_n1, m_norm_ffn1_g, v_norm_ffn1_g),
              ("norm_mix_g", norm_mix_g, g_nmix, m_norm_mix_g, v_norm_mix_g),
              ("conv_w", conv_w, g_conv, m_conv_w, v_conv_w),
              ("q_norm_g", q_norm_g, g_qg, m_q_norm_g, v_q_norm_g),
              ("kv_norm_g", kv_norm_g, g_kvg, m_kv_norm_g, v_kv_norm_g),
              ("out_norm_g", out_norm_g, g_og, m_out_norm_g, v_out_norm_g),
              ("norm_ffn2_g", norm_ffn2_g, g_n3, m_norm_ffn2_g, v_norm_ffn2_g),
              ("final_norm_g", final_norm_g, g_gf, m_final_norm_g, v_final_norm_g)]
    slens = [w.size for _, w, _, _, _ in smalls]
    soffs = [sum(slens[:i]) for i in range(len(slens))]
    s_len = -(-sum(slens) // 1024) * 1024

    def pack_small(i):
        return _pad_to(jnp.concatenate([s[i].reshape(-1) for s in smalls]), s_len).reshape(8, -1)

    s_out = adamw(pack_small(1), pack_small(2), pack_small(3), pack_small(4), "adamw_small")
    for (name, w, g, _, _), o, n in zip(smalls, soffs, slens):
        res[name] = (g.reshape(w.shape),) + tuple(a.reshape(-1)[o:o + n].reshape(w.shape) for a in s_out)

    order = ["ada_w", "ada_b", "norm_ffn1_g", "ffn1_w1", "ffn1_w3", "ffn1_w2", "norm_mix_g", "w_in", "conv_w",
             "q_norm_g", "w_uq", "kv_norm_g", "w_ukv", "out_norm_g", "w_out", "norm_ffn2_g", "ffn2_w1", "ffn2_w3",
             "ffn2_w2", "final_norm_g"]
    return (loss, dx0.reshape(x.shape), *[res[n][0] for n in order], *[res[n][1] for n in order],
            *[res[n][2] for n in order], *[res[n][3] for n in order])
```

```python
import functools

import jax
import jax.numpy as jnp
from jax import lax
from jax.experimental import pallas as pl
from jax.experimental.pallas import tpu as pltpu

F32 = jnp.float32
BF16 = jnp.bfloat16
MESH_ID = pl.DeviceIdType.MESH
N_DEV = 8

EPS = 1e-6
CHUNK = 64
N_MOD = 9
CONV_WIDTH = 512
CONV_GROUPS = 8
CONV_K = 3
MLA_HEADS = 4
QK_NOPE = 128
QK_ROPE = 64
V_HEAD = 128
Q_LORA = 384
KV_LORA = 256
ROPE_THETA = 10000.0
MLA_WIDTH = MLA_HEADS * V_HEAD
MIX_WIDTH = CONV_WIDTH + MLA_WIDTH
IN_COLS = 3 * CONV_WIDTH + Q_LORA + KV_LORA + QK_ROPE
ZC_COLS = 3 * CONV_WIDTH
ZM_COLS = Q_LORA + KV_LORA + 128
HEAD_PAD = 256
QK_COLS = MLA_HEADS * HEAD_PAD
ATTN_SCALE = (QK_NOPE + QK_ROPE) ** -0.5
NEG_INF = -1e30

ADAM_LR = 0.001
ADAM_B1 = 0.9
ADAM_B2 = 0.999
ADAM_EPS = 1e-08
ADAM_WD = 0.01
ADAM_STEP = 10

LANES = 128
MXU_COLS = 256
VMEM_LIMIT = 56 * 1024 * 1024
ROW_TILE = 512
FFN_FWD_TILE = (1024, 256)
FFN_BWD_TILE = (512, 1408)
GRAD_TILE = 1408
ATTN_TILE = 512

NN = (((1,), (0,)), ((), ()))
NT = (((1,), (1,)), ((), ()))
TN = (((0,), (0,)), ((), ()))


def _dot(a, b, dims=NN):
    return lax.dot_general(a, b, dims, preferred_element_type=F32)


def _tile(n, cap, mult=LANES):
    best = None
    for t in range(mult, min(n, cap) + 1, mult):
        if n % t == 0:
            best = t
    return n if best is None else best


def _params(sem=None):
    return pltpu.CompilerParams(dimension_semantics=sem, vmem_limit_bytes=VMEM_LIMIT)


def _row(v):
    return pl.BlockSpec(v.shape, lambda *_: (0,) * v.ndim)


def _sigmoid(x):
    return 0.5 * jnp.tanh(0.5 * x) + 0.5


def _rms(x):
    r = lax.rsqrt(jnp.mean(x * x, axis=-1, keepdims=True) + EPS)
    return x * r, r


def _norm_mod_bwd(dh, x, gn, sc):
    xhat, r = _rms(x)
    d_sh = jnp.sum(dh, axis=0, keepdims=True)
    d_sc = jnp.sum(dh * (xhat * gn), axis=0, keepdims=True)
    dxn = dh * (1.0 + sc)
    d_gn = jnp.sum(dxn * xhat, axis=0, keepdims=True)
    dxh = dxn * gn
    dx = r * (dxh - xhat * jnp.mean(dxh * xhat, axis=-1, keepdims=True))
    return dx, d_sh, d_sc, d_gn


def _group_mean(v, gmat):
    hi = v.astype(BF16)
    lo = (v - hi.astype(F32)).astype(BF16)
    return _dot(hi, gmat) + _dot(lo, gmat)


def _add_rows(ref, rows):
    for r, v in enumerate(rows):
        ref[r:r + 1, :] += v


def _window(ref, axis, j):
    return ref.at[(slice(None),) * axis + (j,)]


def _any_specs(n):
    return [pl.BlockSpec(memory_space=pl.ANY)] * n


def all_gather(blocks, axes, name):
    n_arr = len(blocks)

    def body(*refs):
        start, forward, finish = _gather_steps(refs[:n_arr], refs[n_arr:2 * n_arr], axes, *refs[2 * n_arr:])
        start()
        for j in range(3):
            forward(j)
        finish()

    return pl.pallas_call(
        body, name=name, out_shape=_gathered_shapes(blocks, axes),
        in_specs=_any_specs(n_arr), out_specs=_any_specs(n_arr), scratch_shapes=_gather_sems(n_arr),
    )(*blocks)


def _gathered_shapes(blocks, axes):
    return [jax.ShapeDtypeStruct(b.shape[:ax] + (N_DEV,) + b.shape[ax:], b.dtype) for b, ax in zip(blocks, axes)]


def _gather_sems(n_arr):
    return [pltpu.SemaphoreType.DMA((7, n_arr)), pltpu.SemaphoreType.DMA((7, n_arr)), pltpu.SemaphoreType.DMA((n_arr,))]


def _gather_steps(ins, outs, axes, send_sems, recv_sems, local_sems):
    arrays = range(len(ins))
    x, y, c = lax.axis_index("x"), lax.axis_index("y"), lax.axis_index("c")
    me, sibling = (x, y, c), (x, y, 1 - c)
    chips = [(1 - x, y), (x, 1 - y), (1 - x, 1 - y)]

    def slot(a, px, py, pc):
        return _window(outs[a], axes[a], 4 * px + 2 * py + pc)

    def copy(a, k, block, to, src=None):
        return pltpu.make_async_remote_copy(
            src_ref=slot(a, *block) if src is None else src, dst_ref=slot(a, *block),
            send_sem=send_sems.at[k, a], recv_sem=recv_sems.at[k, a], device_id=to, device_id_type=MESH_ID)

    def mine(a):
        return pltpu.make_async_copy(ins[a], slot(a, *me), local_sems.at[a])

    def first():
        return ([copy(a, 0, me, sibling, src=ins[a]) for a in arrays]
                + [copy(a, 1 + j, me, (*chip, c), src=ins[a]) for j, chip in enumerate(chips) for a in arrays])

    def passed(j):
        return [copy(a, 4 + j, (*chips[j], c), sibling) for a in arrays]

    def start():
        for a in arrays:
            mine(a).start()
        for cp in first():
            cp.start()

    def forward(j):
        for a, cp in zip(arrays, passed(j)):
            copy(a, 1 + j, (*chips[j], c), me).wait_recv()
            cp.start()

    def finish():
        for a in arrays:
            copy(a, 0, sibling, me).wait_recv()
        for j, chip in enumerate(chips):
            for a in arrays:
                copy(a, 4 + j, (*chip, 1 - c), me).wait_recv()
        for cp in first() + passed(0) + passed(1) + passed(2):
            cp.wait_send()
        for a in arrays:
            mine(a).wait()

    return start, forward, finish


def exchange_sibling(grads, name):
    n_arr = len(grads)

    def body(*refs):
        ins, outs = refs[:n_arr], refs[n_arr:2 * n_arr]
        send_sems, recv_sems = refs[2 * n_arr:]
        x, y, c = lax.axis_index("x"), lax.axis_index("y"), lax.axis_index("c")

        def copy(a, src, dst):
            return pltpu.make_async_remote_copy(
                src_ref=src, dst_ref=dst, send_sem=send_sems.at[a], recv_sem=recv_sems.at[a],
                device_id=(x, y, 1 - c), device_id_type=MESH_ID)

        for a in range(n_arr):
            for k in range(4):
                copy(a, ins[a].at[2 * k + (1 - c)], outs[a].at[k]).start()
        whole = [copy(a, ins[a].at[pl.ds(0, 4)], outs[a]) for a in range(n_arr)]
        for cp in whole:
            cp.wait_recv()
        for cp in whole:
            cp.wait_send()

    return pl.pallas_call(
        body, name=name,
        out_shape=[jax.ShapeDtypeStruct((4,) + g.shape[1:], g.dtype) for g in grads],
        in_specs=_any_specs(n_arr), out_specs=_any_specs(n_arr),
        scratch_shapes=[pltpu.SemaphoreType.DMA((n_arr,)), pltpu.SemaphoreType.DMA((n_arr,))],
    )(*grads)


def exchange_chips(parts, name):
    n_arr = len(parts)

    def body(*refs):
        start, finish = _chip_exchange_steps(refs[:n_arr], refs[n_arr:2 * n_arr], *refs[2 * n_arr:])
        start()
        finish()

    return pl.pallas_call(
        body, name=name,
        out_shape=[jax.ShapeDtypeStruct(p.shape, p.dtype) for p in parts],
        in_specs=_any_specs(n_arr), out_specs=_any_specs(n_arr), scratch_shapes=_exchange_sems(n_arr),
    )(*parts)


def _exchange_sems(n_arr):
    return [pltpu.SemaphoreType.DMA((n_arr,)), pltpu.SemaphoreType.DMA((n_arr,))]


def _chip_exchange_steps(ins, outs, send_sems, recv_sems):
    x, y, c = lax.axis_index("x"), lax.axis_index("y"), lax.axis_index("c")
    chips = [(1 - x, y), (x, 1 - y), (1 - x, 1 - y)]

    def copy(a, src, dst, chip):
        return pltpu.make_async_remote_copy(
            src_ref=src, dst_ref=dst, send_sem=send_sems.at[a], recv_sem=recv_sems.at[a],
            device_id=(*chip, c), device_id_type=MESH_ID)

    def start():
        for a in range(len(ins)):
            for j, chip in enumerate(chips):
                copy(a, ins[a].at[j], outs[a].at[j], chip).start()

    def finish():
        whole = [copy(a, ins[a], outs[a], chips[0]) for a in range(len(ins))]
        for cp in whole:
            cp.wait_recv()
        for cp in whole:
            cp.wait_send()

    return start, finish


def riding_gather(blocks, axes):
    def phases(ins, outs, *sems):
        start, forward, finish = _gather_steps(ins, outs, axes, *sems)
        return [start] + [functools.partial(forward, j) for j in range(3)] + [finish]

    return dict(operands=blocks, out_shape=_gathered_shapes(blocks, axes), sems=_gather_sems(len(blocks)),
                phases=phases, when=("first", "late0", "late1", "late2", "last"))


def riding_exchange(parts):
    def phases(ins, outs, *sems):
        return list(_chip_exchange_steps(ins, outs, *sems))

    return dict(operands=parts, out_shape=[jax.ShapeDtypeStruct(p.shape, p.dtype) for p in parts],
                sems=_exchange_sems(len(parts)), phases=phases, when=("first", "last"))


def _call_with_rider(body, rider, *, name, grid, in_specs, out_specs, out_shape, scratch_shapes, operands):
    params = _params(("arbitrary",) * len(grid))
    if rider is None:
        return pl.pallas_call(body, name=name, grid=grid, in_specs=in_specs, out_specs=out_specs,
                              out_shape=out_shape, scratch_shapes=scratch_shapes, compiler_params=params)(*operands)
    n_in, n_out, n_scr, k = len(in_specs), len(out_specs), len(scratch_shapes), len(rider["operands"])
    rows, cols = grid
    assert cols >= 3 or "late0" not in rider["when"]
    late_row = max(rows - 2, 0)
    at = {"first": (0, 0), "last": (rows - 1, cols - 1),
          "late0": (late_row, 0), "late1": (late_row, 1), "late2": (late_row, 2)}

    def wrapped(*refs):
        ins, c_in = refs[:n_in], refs[n_in:n_in + k]
        outs, c_out = refs[n_in + k:n_in + k + n_out], refs[n_in + k + n_out:n_in + 2 * k + n_out]
        scratch, sems = refs[n_in + 2 * k + n_out:n_in + 2 * k + n_out + n_scr], refs[n_in + 2 * k + n_out + n_scr:]
        i, j = pl.program_id(0), pl.program_id(1)
        phases = rider["phases"](c_in, c_out, *sems)
        for fn, key in zip(phases, rider["when"]):
            if key != "last":
                pl.when(jnp.logical_and(i == at[key][0], j == at[key][1]))(fn)
        body(*ins, *outs, *scratch)
        pl.when(jnp.logical_and(i == at["last"][0], j == at["last"][1]))(phases[-1])

    return pl.pallas_call(
        wrapped, name=name, grid=grid,
        in_specs=list(in_specs) + _any_specs(k), out_specs=list(out_specs) + _any_specs(k),
        out_shape=list(out_shape) + rider["out_shape"], scratch_shapes=list(scratch_shapes) + rider["sems"],
        compiler_params=params)(*operands, *rider["operands"])


def add_sibling(g8, got, src_idx, chip_idx, name):
    _, r, n = g8.shape
    tr = _tile(r, 256, 16)

    def body(si_ref, ci_ref, g0_ref, g1_ref, g2_ref, g3_ref, got_ref, own_ref, send_ref):
        own_ref[...] = g0_ref[0] + got_ref[ci_ref[0]]
        for j, g_ref in enumerate((g1_ref, g2_ref, g3_ref)):
            send_ref[j] = (g_ref[0] + got_ref[ci_ref[j + 1]]).astype(BF16)

    def mine(j):
        return pl.BlockSpec((1, tr, n), lambda i, si, ci: (si[j], i, 0))

    return pl.pallas_call(
        body, name=name,
        out_shape=[jax.ShapeDtypeStruct((r, n), F32), jax.ShapeDtypeStruct((3, r, n), BF16)],
        grid_spec=pltpu.PrefetchScalarGridSpec(
            num_scalar_prefetch=2, grid=(r // tr,),
            in_specs=[mine(0), mine(1), mine(2), mine(3), pl.BlockSpec((4, tr, n), lambda i, si, ci: (0, i, 0))],
            out_specs=[pl.BlockSpec((tr, n), lambda i, si, ci: (i, 0)),
                       pl.BlockSpec((3, tr, n), lambda i, si, ci: (0, i, 0))]),
        compiler_params=_params(("arbitrary",)),
    )(src_idx, chip_idx, g8, g8, g8, g8, got)


def add_received(own, got, name):
    r, n = own.shape
    tr = _tile(r, 256, 16)

    def body(a_ref, b_ref, o_ref):
        acc = a_ref[...]
        for j in range(3):
            acc = acc + b_ref[j].astype(F32)
        o_ref[...] = acc

    return pl.pallas_call(
        body, name=name,
        out_shape=jax.ShapeDtypeStruct((r, n), F32),
        grid=(r // tr,),
        in_specs=[pl.BlockSpec((tr, n), lambda i: (i, 0)), pl.BlockSpec((3, tr, n), lambda i: (0, i, 0))],
        out_specs=pl.BlockSpec((tr, n), lambda i: (i, 0)),
        compiler_params=_params(("arbitrary",)),
    )(own, got)


def sum_devices(g):
    def body(g_ref, o_ref):
        acc = g_ref[0]
        for j in range(1, N_DEV):
            acc = acc + g_ref[j]
        o_ref[...] = acc

    return pl.pallas_call(body, name="sum_devices", out_shape=jax.ShapeDtypeStruct(g.shape[1:], F32))(g)


def sum_lanes(v):
    def body(v_ref, o_ref):
        o_ref[...] = jnp.broadcast_to(jnp.sum(v_ref[...], axis=-1, keepdims=True), (1, LANES))

    return pl.pallas_call(body, name="sum_lanes", out_shape=jax.ShapeDtypeStruct((1, LANES), F32))(v)


def ada_forward(c_all, ada_w, ada_b_cols):
    nb, n = c_all.shape[0], ada_w.shape[1]

    def body(c_ref, w_ref, b_ref, o_ref):
        cv = c_ref[...]
        s = (cv * jax.nn.sigmoid(cv)).astype(BF16)
        o_ref[...] = _dot(s, w_ref[...].astype(BF16)) + b_ref[...]

    return pl.pallas_call(body, name="ada_fwd", out_shape=jax.ShapeDtypeStruct((nb, n), F32),
                          compiler_params=_params())(c_all, ada_w, ada_b_cols)


def ada_backward(c_all16, dmod16):
    d, n = c_all16.shape[1], dmod16.shape[1]

    def body(c_ref, g_ref, o_ref):
        cv = c_ref[...]
        s = (cv * jax.nn.sigmoid(cv)).astype(BF16)
        o_ref[...] = _dot(s, g_ref[...].astype(BF16), TN)

    return pl.pallas_call(body, name="ada_bwd", out_shape=jax.ShapeDtypeStruct((d, n), F32),
                          compiler_params=_params())(c_all16, dmod16)


def ffn_forward(x, gn, sc, sh, gate, ws, first, name, rider=None):
    t, d = x.shape
    f = ws.shape[1]
    tm, tf = _tile(t, FFN_FWD_TILE[0], 16), _tile(f, FFN_FWD_TILE[1])
    nf = f // tf

    def body(x_ref, gn_ref, sc_ref, sh_ref, gate_ref, w1_ref, w3_ref, w2_ref,
             xo_ref, h_ref, a_ref, b_ref, y_ref, hs, acc):
        j = pl.program_id(1)

        @pl.when(j == 0)
        def _():
            xhat, _ = _rms(x_ref[...])
            h = (xhat * gn_ref[...] * (1.0 + sc_ref[...]) + sh_ref[...]).astype(BF16)
            hs[...] = h
            h_ref[...] = h
            acc[...] = jnp.zeros_like(acc)

        h = hs[...]
        a = _dot(h, w1_ref[...], NT)
        b = _dot(h, w3_ref[...], NT)
        a_ref[...] = a.astype(BF16)
        b_ref[...] = b.astype(BF16)
        u = (a * _sigmoid(a) * b).astype(BF16)
        acc[...] += _dot(u, w2_ref[...])

        @pl.when(j == nf - 1)
        def _():
            y = acc[...]
            y_ref[...] = y.astype(BF16)
            xo_ref[...] = x_ref[...] + 0.5 * gate_ref[...] * y

    row = pl.BlockSpec((tm, d), lambda i, j: (i, 0))
    vec = pl.BlockSpec((1, d), lambda i, j: (0, 0))
    wide = pl.BlockSpec((tm, tf), lambda i, j: (i, j))
    return _call_with_rider(
        body, rider, name=name, grid=(t // tm, nf),
        in_specs=[row, vec, vec, vec, vec] + _ffn_weight_specs(first, tf, d),
        out_specs=[row, row, wide, wide, row],
        out_shape=[jax.ShapeDtypeStruct((t, d), F32), jax.ShapeDtypeStruct((t, d), BF16),
                   jax.ShapeDtypeStruct((t, f), BF16), jax.ShapeDtypeStruct((t, f), BF16),
                   jax.ShapeDtypeStruct((t, d), BF16)],
        scratch_shapes=[pltpu.VMEM((tm, d), BF16), pltpu.VMEM((tm, d), F32)],
        operands=(x, gn, sc, sh, gate, ws, ws, ws))


def _ffn_weight_specs(first, tf, d):
    return [pl.BlockSpec((None, tf, d), lambda i, j, w=first + k: (w, j, 0)) for k in range(3)]


def ffn_backward(dxo, dy, x, a, b, y, gn, sc, ws, first, name, rider=None):
    t, d = x.shape
    f = ws.shape[1]
    tm, tf = _tile(t, FFN_BWD_TILE[0], 16), _tile(f, FFN_BWD_TILE[1])
    nf = f // tf
    row = pl.BlockSpec((tm, d), lambda i, j: (i, 0))
    vec = pl.BlockSpec((1, d), lambda i, j: (0, 0))
    wide = pl.BlockSpec((tm, tf), lambda i, j: (i, j))

    def gate_body(dy_ref, a_ref, b_ref, w2_ref, da_ref, db_ref, u_ref):
        du = _dot(dy_ref[...], w2_ref[...], NT)
        av = a_ref[...].astype(F32)
        bv = b_ref[...].astype(F32)
        s = _sigmoid(av)
        sa = av * s
        da_ref[...] = (du * bv * (s + sa * (1.0 - s))).astype(BF16)
        db_ref[...] = (du * sa).astype(BF16)
        u_ref[...] = (sa * bv).astype(BF16)

    hidden = jax.ShapeDtypeStruct((t, f), BF16)
    wide_t = pl.BlockSpec((tm, tf), lambda j, i: (i, j))
    da, db, u = pl.pallas_call(
        gate_body, name=name + "_gate", grid=(nf, t // tm),
        in_specs=[pl.BlockSpec((tm, d), lambda j, i: (i, 0)), wide_t, wide_t,
                  pl.BlockSpec((None, tf, d), lambda j, i: (first + 2, j, 0))],
        out_specs=[wide_t, wide_t, wide_t], out_shape=[hidden, hidden, hidden],
        compiler_params=_params(("arbitrary", "arbitrary")),
    )(dy, a, b, ws)

    def norm_body(da_ref, db_ref, w1_ref, w3_ref, dxo_ref, x_ref, y_ref, gn_ref, sc_ref, dx_ref, sums_ref, acc):
        i, j = pl.program_id(0), pl.program_id(1)

        @pl.when(jnp.logical_and(i == 0, j == 0))
        def _():
            sums_ref[...] = jnp.zeros_like(sums_ref)

        part = _dot(da_ref[...], w1_ref[...]) + _dot(db_ref[...], w3_ref[...])

        @pl.when(j == 0)
        def _():
            acc[...] = part

        @pl.when(jnp.logical_and(j > 0, j < nf - 1))
        def _():
            acc[...] += part

        @pl.when(j == nf - 1)
        def _():
            dh = part if nf == 1 else acc[...] + part
            dxo_v = dxo_ref[...]
            dx, d_sh, d_sc, d_gn = _norm_mod_bwd(dh, x_ref[...], gn_ref[...], sc_ref[...])
            dx_ref[...] = dxo_v + dx
            d_gate = jnp.sum(dxo_v * (0.5 * y_ref[...].astype(F32)), axis=0, keepdims=True)
            _add_rows(sums_ref, [d_sh, d_sc, d_gate, d_gn])

    w1_spec, w3_spec, _ = _ffn_weight_specs(first, tf, d)
    dx, sums, *ridden = _call_with_rider(
        norm_body, rider, name=name + "_norm", grid=(t // tm, nf),
        in_specs=[wide, wide, w1_spec, w3_spec, row, row, row, vec, vec],
        out_specs=[row, pl.BlockSpec((8, d), lambda i, j: (0, 0))],
        out_shape=[jax.ShapeDtypeStruct((t, d), F32), jax.ShapeDtypeStruct((8, d), F32)],
        scratch_shapes=[pltpu.VMEM((tm, d), F32)],
        operands=(da, db, ws, ws, dxo, x, y, gn, sc))
    return (dx, da, db, u, sums, *ridden)


def matmul_tn(a, b, name):
    t, m = a.shape
    n = b.shape[1]
    tm, tn, tk = _tile(m, GRAD_TILE), _tile(n, GRAD_TILE), _tile(t, 1024, 16)
    nk = t // tk

    def body(a_ref, b_ref, o_ref, acc):
        k = pl.program_id(2)

        @pl.when(k == 0)
        def _():
            acc[...] = jnp.zeros_like(acc)

        acc[...] += _dot(a_ref[...], b_ref[...], TN)

        @pl.when(k == nk - 1)
        def _():
            o_ref[...] = acc[...]

    return pl.pallas_call(
        body, name=name, grid=(m // tm, n // tn, nk),
        in_specs=[pl.BlockSpec((tk, tm), lambda i, j, k: (k, i)), pl.BlockSpec((tk, tn), lambda i, j, k: (k, j))],
        out_specs=pl.BlockSpec((tm, tn), lambda i, j, k: (i, j)),
        out_shape=jax.ShapeDtypeStruct((m, n), F32),
        scratch_shapes=[pltpu.VMEM((tm, tn), F32)],
        compiler_params=_params(("arbitrary", "arbitrary", "arbitrary")),
    )(a, b)


def mix_in_forward(x, gn, sc, sh, w_in):
    t, d = x.shape
    tm = _tile(t, ROW_TILE, 16)

    def body(x_ref, gn_ref, sc_ref, sh_ref, w_ref, h_ref, zc_ref, zm_ref):
        xhat, _ = _rms(x_ref[...])
        h = (xhat * gn_ref[...] * (1.0 + sc_ref[...]) + sh_ref[...]).astype(BF16)
        h_ref[...] = h
        z = _dot(h, w_ref[...], NT)
        zc_ref[...] = z[:, :ZC_COLS]
        zm_ref[...] = z[:, ZC_COLS:]

    row = pl.BlockSpec((tm, d), lambda i: (i, 0))
    vec = pl.BlockSpec((1, d), lambda i: (0, 0))
    return pl.pallas_call(
        body, name="mix_in_fwd", grid=(t // tm,),
        in_specs=[row, vec, vec, vec, _row(w_in)],
        out_specs=[row, pl.BlockSpec((tm, ZC_COLS), lambda i: (i, 0)), pl.BlockSpec((tm, ZM_COLS), lambda i: (i, 0))],
        out_shape=[jax.ShapeDtypeStruct((t, d), BF16), jax.ShapeDtypeStruct((t, ZC_COLS), F32),
                   jax.ShapeDtypeStruct((t, ZM_COLS), F32)],
        compiler_params=_params(("arbitrary",)),
    )(x, gn, sc, sh, w_in)


def _rope_tables(pos, inv_freq):
    ang = pos * inv_freq
    lane = lax.broadcasted_iota(jnp.int32, ang.shape, 1)
    cos, sin = jnp.cos(ang), jnp.sin(ang)
    half = QK_ROPE // 2
    return cos, jnp.where(lane < half, -sin, 0.0), jnp.where(jnp.logical_and(lane >= half, lane < QK_ROPE), sin, 0.0)


def _rope(v, tables):
    cos, sin_a, sin_b = tables
    return v * cos + pltpu.roll(v, LANES - QK_ROPE // 2, 1) * sin_a + pltpu.roll(v, QK_ROPE // 2, 1) * sin_b


def _rope_transposed(dv, tables):
    cos, sin_a, sin_b = tables
    return dv * cos + pltpu.roll(dv * sin_a, QK_ROPE // 2, 1) + pltpu.roll(dv * sin_b, LANES - QK_ROPE // 2, 1)


def mla_project(zm, pos, inv_freq, qg, kvg, w_uq, w_ukv):
    t = zm.shape[0]
    tm = _tile(t, ROW_TILE, 16)

    def body(zm_ref, pos_ref, if_ref, qg_ref, kvg_ref, wq_ref, wkv_ref, qn_ref, kvn_ref, q_ref, k_ref, v_ref):
        zv = zm_ref[...]
        qn = (_rms(zv[:, :Q_LORA])[0] * qg_ref[...]).astype(BF16)
        kvn = (_rms(zv[:, Q_LORA:Q_LORA + KV_LORA])[0] * kvg_ref[...]).astype(BF16)
        qn_ref[...] = qn
        kvn_ref[...] = kvn
        qf = _dot(qn, wq_ref[...], NT)
        kvf = _dot(kvn, wkv_ref[...], NT)
        tables = _rope_tables(pos_ref[...], if_ref[...])
        kr = _rope(zv[:, Q_LORA + KV_LORA:], tables).astype(BF16)
        for h in range(MLA_HEADS):
            lo = h * HEAD_PAD
            q_ref[:, lo:lo + QK_NOPE] = qf[:, lo:lo + QK_NOPE].astype(BF16)
            q_ref[:, lo + QK_NOPE:lo + HEAD_PAD] = _rope(qf[:, lo + QK_NOPE:lo + HEAD_PAD], tables).astype(BF16)
            k_ref[:, lo:lo + QK_NOPE] = kvf[:, h * QK_NOPE:(h + 1) * QK_NOPE].astype(BF16)
            k_ref[:, lo + QK_NOPE:lo + HEAD_PAD] = kr
        v_ref[...] = kvf[:, MLA_HEADS * QK_NOPE:].astype(BF16)

    def rows(n):
        return pl.BlockSpec((tm, n), lambda i: (i, 0))

    return pl.pallas_call(
        body, name="mla_project", grid=(t // tm,),
        in_specs=[rows(ZM_COLS), rows(1), _row(inv_freq), _row(qg), _row(kvg), _row(w_uq), _row(w_ukv)],
        out_specs=[rows(Q_LORA), rows(KV_LORA), rows(QK_COLS), rows(QK_COLS), rows(MLA_WIDTH)],
        out_shape=[jax.ShapeDtypeStruct((t, Q_LORA), BF16), jax.ShapeDtypeStruct((t, KV_LORA), BF16),
                   jax.ShapeDtypeStruct((t, QK_COLS), BF16), jax.ShapeDtypeStruct((t, QK_COLS), BF16),
                   jax.ShapeDtypeStruct((t, MLA_WIDTH), BF16)],
        compiler_params=_params(("arbitrary",)),
    )(zm, pos, inv_freq, qg, kvg, w_uq, w_ukv)


def _chunk_mask(shape, q_axis):
    qi = lax.broadcasted_iota(jnp.int32, shape, q_axis) // CHUNK
    ki = lax.broadcasted_iota(jnp.int32, shape, 1 - q_axis) // CHUNK
    return ki <= qi


def attention_forward(q, k, v):
    t = q.shape[0]
    tq = _tile(t, ATTN_TILE, CHUNK)

    def body(q_ref, k_ref, v_ref, o_ref, lse_ref):
        i = pl.program_id(1)
        qv = q_ref[...]

        def step(kb, carry, masked):
            m, l, acc = carry
            start = pl.multiple_of(kb * tq, tq)
            s = _dot(qv, k_ref[pl.ds(start, tq), :], NT) * ATTN_SCALE
            if masked:
                s = jnp.where(_chunk_mask(s.shape, 0), s, NEG_INF)
            m_new = jnp.maximum(m, jnp.max(s, axis=-1, keepdims=True))
            alpha = jnp.exp(m - m_new)
            p = jnp.exp(s - m_new)
            l = alpha * l + jnp.sum(p, axis=-1, keepdims=True)
            acc = alpha * acc + _dot(p.astype(BF16), v_ref[pl.ds(start, tq), :])
            return m_new, l, acc

        init = (jnp.full((tq, 1), NEG_INF, F32), jnp.zeros((tq, 1), F32), jnp.zeros((tq, V_HEAD), F32))
        carry = lax.fori_loop(0, i // 2, lambda pb, cr: step(2 * pb + 1, step(2 * pb, cr, False), False), init)
        carry = lax.fori_loop(0, i % 2, lambda _, cr: step(i - 1, cr, False), carry)
        m, l, acc = step(i, carry, True)
        o_ref[...] = acc / l
        lse_ref[0] = m + jnp.log(l)

    return pl.pallas_call(
        body, name="attn_fwd", grid=(MLA_HEADS, t // tq),
        in_specs=[pl.BlockSpec((tq, HEAD_PAD), lambda h, i: (i, h)),
                  pl.BlockSpec((t, HEAD_PAD), lambda h, i: (0, h)),
                  pl.BlockSpec((t, V_HEAD), lambda h, i: (0, h))],
        out_specs=[pl.BlockSpec((tq, V_HEAD), lambda h, i: (i, h)),
                   pl.BlockSpec((1, tq, 1), lambda h, i: (h, i, 0))],
        out_shape=[jax.ShapeDtypeStruct((t, MLA_WIDTH), F32), jax.ShapeDtypeStruct((MLA_HEADS, t, 1), F32)],
        compiler_params=_params(("arbitrary", "arbitrary")),
    )(q, k, v)


def attention_backward(q, k, v, do, lse, delta, rider=None):
    t = q.shape[0]
    tq = _tile(t, ATTN_TILE, CHUNK)
    nq = t // tq

    def body(q_ref, k_ref, v_ref, do_ref, lse_ref, delta_ref, dq_ref, dk_ref, dv_ref):
        kb = pl.program_id(1)

        @pl.when(kb == 0)
        def _():
            dq_ref[...] = jnp.zeros_like(dq_ref)

        kv, vv = k_ref[...], v_ref[...]

        def step(qb, carry, masked):
            dk, dv = carry
            rows = pl.ds(pl.multiple_of(qb * tq, tq), tq)
            qv, dov = q_ref[rows, :], do_ref[rows, :]
            s = _dot(kv, qv, NT) * ATTN_SCALE
            if masked:
                s = jnp.where(_chunk_mask(s.shape, 1), s, NEG_INF)
            p = jnp.exp(s - lse_ref[0, qb])
            dv = dv + _dot(p.astype(BF16), dov)
            dp = _dot(vv, dov, NT)
            ds = (p * (dp - delta_ref[0, qb]) * ATTN_SCALE).astype(BF16)
            dk = dk + _dot(ds, qv)
            dq_ref[rows, :] += _dot(ds, kv, TN)
            return dk, dv

        carry = step(kb, (jnp.zeros((tq, HEAD_PAD), F32), jnp.zeros((tq, V_HEAD), F32)), True)
        odd = (nq - 1 - kb) % 2
        carry = lax.fori_loop(0, odd, lambda _, cr: step(kb + 1, cr, False), carry)
        first = kb + 1 + odd
        dk, dv = lax.fori_loop(0, (nq - first) // 2,
                               lambda pb, cr: step(first + 2 * pb + 1, step(first + 2 * pb, cr, False), False), carry)
        dk_ref[...] = dk
        dv_ref[...] = dv

    stat = pl.BlockSpec((1, nq, 1, tq), lambda h, j: (h, 0, 0, 0))
    return _call_with_rider(
        body, rider, name="attn_bwd", grid=(MLA_HEADS, nq),
        in_specs=[pl.BlockSpec((t, HEAD_PAD), lambda h, j: (0, h)),
                  pl.BlockSpec((tq, HEAD_PAD), lambda h, j: (j, h)),
                  pl.BlockSpec((tq, V_HEAD), lambda h, j: (j, h)),
                  pl.BlockSpec((t, V_HEAD), lambda h, j: (0, h)), stat, stat],
        out_specs=[pl.BlockSpec((t, HEAD_PAD), lambda h, j: (0, h)),
                   pl.BlockSpec((tq, HEAD_PAD), lambda h, j: (j, h)),
                   pl.BlockSpec((tq, V_HEAD), lambda h, j: (j, h))],
        out_shape=[jax.ShapeDtypeStruct((t, QK_COLS), F32), jax.ShapeDtypeStruct((t, QK_COLS), F32),
                   jax.ShapeDtypeStruct((t, MLA_WIDTH), F32)],
        scratch_shapes=[], operands=(q, k, v, do, lse, delta))


def _shift_rows(v, prev, n):
    out = pltpu.roll(v, n, 0)
    row = lax.broadcasted_iota(jnp.int32, v.shape, 0)
    for r in range(n):
        out = jnp.where(row == r, prev[8 - n + r:8 - n + r + 1, :], out)
    return out


def _advance_rows(v, nxt, n):
    rows = v.shape[0]
    out = pltpu.roll(v, rows - n, 0)
    row = lax.broadcasted_iota(jnp.int32, v.shape, 0)
    for r in range(n):
        out = jnp.where(row == rows - n + r, nxt[r:r + 1, :], out)
    return out


def _conv_taps(zc, zc_prev, first):
    w = CONV_WIDTH
    u = zc[:, w:2 * w] * zc[:, 2 * w:]
    up = jnp.where(first, 0.0, zc_prev[:, w:2 * w] * zc_prev[:, 2 * w:])
    return u, _shift_rows(u, up, 1), _shift_rows(u, up, 2)


def mix_out_forward(zc, o, conv_w, og, gmat_a, gmat_b, w_out, x, gate):
    t, d = x.shape
    tm = _tile(t, ROW_TILE, 16)
    w = CONV_WIDTH

    def body(zc_ref, zp_ref, o_ref, cw_ref, og_ref, ga_ref, gb_ref, w_ref, x_ref, gate_ref,
             xo_ref, yn_ref, y_ref, ya_ref):
        zc_v = zc_ref[...]
        u, u1, u2 = _conv_taps(zc_v, zp_ref[...], pl.program_id(0) == 0)
        cw = cw_ref[...]
        ya = zc_v[:, :w] * (cw[0:1] * u2 + cw[1:2] * u1 + cw[2:3] * u)
        ya_ref[...] = ya
        ov = o_ref[...]
        ogv = og_ref[...]
        yn_ref[:, :w] = (ya * lax.rsqrt(_group_mean(ya * ya, ga_ref[...]) + EPS) * ogv[:, :w]).astype(BF16)
        yn_ref[:, w:] = (ov * lax.rsqrt(_group_mean(ov * ov, gb_ref[...]) + EPS) * ogv[:, w:]).astype(BF16)
        y = _dot(yn_ref[...], w_ref[...])
        y_ref[...] = y.astype(BF16)
        xo_ref[...] = x_ref[...] + gate_ref[...] * y

    def rows(n):
        return pl.BlockSpec((tm, n), lambda i: (i, 0))

    prev = pl.BlockSpec((8, ZC_COLS), lambda i: (jnp.maximum(i * (tm // 8) - 1, 0), 0))
    return pl.pallas_call(
        body, name="mix_out_fwd", grid=(t // tm,),
        in_specs=[rows(ZC_COLS), prev, rows(MLA_WIDTH), _row(conv_w), _row(og), _row(gmat_a), _row(gmat_b),
                  _row(w_out), rows(d), _row(gate)],
        out_specs=[rows(d), rows(MIX_WIDTH), rows(d), rows(w)],
        out_shape=[jax.ShapeDtypeStruct((t, d), F32), jax.ShapeDtypeStruct((t, MIX_WIDTH), BF16),
                   jax.ShapeDtypeStruct((t, d), BF16), jax.ShapeDtypeStruct((t, w), F32)],
        compiler_params=_params(("arbitrary",)),
    )(zc, zc, o, conv_w, og, gmat_a, gmat_b, w_out, x, gate)


def _group_norm_bwd(dyn, y, og, gmat):
    rs = lax.rsqrt(_group_mean(y * y, gmat) + EPS)
    yhat = y * rs
    d_og = jnp.sum(dyn * yhat, axis=0, keepdims=True)
    dyh = dyn * og
    return rs * (dyh - yhat * _group_mean(dyh * yhat, gmat)), d_og


def mix_out_backward(dxo, y, gate, ya, o, og, gmat_a, gmat_b, w_out):
    t, d = dxo.shape
    tm = _tile(t, ROW_TILE, 16)
    w = CONV_WIDTH

    def body(dxo_ref, y_ref, gate_ref, ya_ref, o_ref, og_ref, ga_ref, gb_ref, w_ref,
             dy_ref, dya_ref, do_ref, delta_ref, sd_ref, so_ref):
        @pl.when(pl.program_id(0) == 0)
        def _():
            sd_ref[...] = jnp.zeros_like(sd_ref)
            so_ref[...] = jnp.zeros_like(so_ref)

        dxo_v = dxo_ref[...]
        dy = (gate_ref[...] * dxo_v).astype(BF16)
        dy_ref[...] = dy
        sd_ref[0:1, :] += jnp.sum(dxo_v * y_ref[...].astype(F32), axis=0, keepdims=True)
        dyn = _dot(dy, w_ref[...], NT)
        ogv = og_ref[...]
        ov = o_ref[...]
        dya, d_og_a = _group_norm_bwd(dyn[:, :w], ya_ref[...], ogv[:, :w], ga_ref[...])
        dov, d_og_b = _group_norm_bwd(dyn[:, w:], ov, ogv[:, w:], gb_ref[...])
        dya_ref[...] = dya
        do_ref[...] = dov.astype(BF16)
        so_ref[0:1, :w] += d_og_a
        so_ref[0:1, w:] += d_og_b
        prod = dov * ov
        for h in range(MLA_HEADS):
            delta_ref[h] = jnp.sum(prod[:, h * V_HEAD:(h + 1) * V_HEAD], axis=-1, keepdims=True)

    def rows(n):
        return pl.BlockSpec((tm, n), lambda i: (i, 0))

    return pl.pallas_call(
        body, name="mix_out_bwd", grid=(t // tm,),
        in_specs=[rows(d), rows(d), _row(gate), rows(w), rows(MLA_WIDTH), _row(og), _row(gmat_a), _row(gmat_b),
                  _row(w_out)],
        out_specs=[rows(d), rows(w), rows(MLA_WIDTH), pl.BlockSpec((MLA_HEADS, tm, 1), lambda i: (0, i, 0)),
                   pl.BlockSpec((8, d), lambda i: (0, 0)), pl.BlockSpec((8, MIX_WIDTH), lambda i: (0, 0))],
        out_shape=[jax.ShapeDtypeStruct((t, d), BF16), jax.ShapeDtypeStruct((t, w), F32),
                   jax.ShapeDtypeStruct((t, MLA_WIDTH), BF16), jax.ShapeDtypeStruct((MLA_HEADS, t, 1), F32),
                   jax.ShapeDtypeStruct((8, d), F32), jax.ShapeDtypeStruct((8, MIX_WIDTH), F32)],
        compiler_params=_params(("arbitrary",)),
    )(dxo, y, gate, ya, o, og, gmat_a, gmat_b, w_out)


def conv_backward(zc, dya, conv_w):
    t = zc.shape[0]
    tm = _tile(t, ROW_TILE, 16)
    nt = t // tm
    w = CONV_WIDTH

    def body(zc_ref, zp_ref, zn_ref, dya_ref, dn_ref, cw_ref, dzc_ref, sums_ref):
        i = pl.program_id(0)

        @pl.when(i == 0)
        def _():
            sums_ref[...] = jnp.zeros_like(sums_ref)

        zc_v = zc_ref[...]
        u, u1, u2 = _conv_taps(zc_v, zp_ref[...], i == 0)
        cw = cw_ref[...]
        dya_v = dya_ref[...]
        dyc = dya_v * zc_v[:, :w]
        dyc_next = jnp.where(i == nt - 1, 0.0, dn_ref[...] * zn_ref[...][:, :w])
        du = cw[2:3] * dyc + cw[1:2] * _advance_rows(dyc, dyc_next, 1) + cw[0:1] * _advance_rows(dyc, dyc_next, 2)
        dzc_ref[:, :w] = (dya_v * (cw[0:1] * u2 + cw[1:2] * u1 + cw[2:3] * u)).astype(BF16)
        dzc_ref[:, w:2 * w] = (du * zc_v[:, 2 * w:]).astype(BF16)
        dzc_ref[:, 2 * w:] = (du * zc_v[:, w:2 * w]).astype(BF16)
        _add_rows(sums_ref, [jnp.sum(dyc * tap, axis=0, keepdims=True) for tap in (u2, u1, u)])

    def rows(n):
        return pl.BlockSpec((tm, n), lambda i: (i, 0))

    def halo(n, step):
        last = t // 8 - 1
        return pl.BlockSpec((8, n), lambda i: (jnp.clip(i * (tm // 8) + step, 0, last), 0))

    return pl.pallas_call(
        body, name="conv_bwd", grid=(nt,),
        in_specs=[rows(ZC_COLS), halo(ZC_COLS, -1), halo(ZC_COLS, tm // 8), rows(w), halo(w, tm // 8), _row(conv_w)],
        out_specs=[rows(ZC_COLS), pl.BlockSpec((8, w), lambda i: (0, 0))],
        out_shape=[jax.ShapeDtypeStruct((t, ZC_COLS), BF16), jax.ShapeDtypeStruct((8, w), F32)],
        compiler_params=_params(("arbitrary",)),
    )(zc, zc, zc, dya, dya, conv_w)


def _rms_bwd(dy, x, g):
    xhat, r = _rms(x)
    d_g = jnp.sum(dy * xhat, axis=0, keepdims=True)
    dxh = dy * g
    return r * (dxh - xhat * jnp.mean(dxh * xhat, axis=-1, keepdims=True)), d_g


def mla_project_backward(dq, dk, dv, zm, pos, inv_freq, qg, kvg, w_uq, w_ukv):
    t = zm.shape[0]
    tm = _tile(t, ROW_TILE, 16)

    def body(dq_ref, dk_ref, dv_ref, zm_ref, pos_ref, if_ref, qg_ref, kvg_ref, wq_ref, wkv_ref,
             dql_ref, dkvl_ref, dzm_ref, sums_ref):
        @pl.when(pl.program_id(0) == 0)
        def _():
            sums_ref[...] = jnp.zeros_like(sums_ref)

        tables = _rope_tables(pos_ref[...], if_ref[...])
        dkr = jnp.zeros((tm, LANES), F32)
        for h in range(MLA_HEADS):
            lo = h * HEAD_PAD
            dql_ref[:, lo:lo + QK_NOPE] = dq_ref[:, lo:lo + QK_NOPE].astype(BF16)
            dql_ref[:, lo + QK_NOPE:lo + HEAD_PAD] = _rope_transposed(
                dq_ref[:, lo + QK_NOPE:lo + HEAD_PAD], tables).astype(BF16)
            dkvl_ref[:, h * QK_NOPE:(h + 1) * QK_NOPE] = dk_ref[:, lo:lo + QK_NOPE].astype(BF16)
            dkr = dkr + dk_ref[:, lo + QK_NOPE:lo + HEAD_PAD]
        dkvl_ref[:, MLA_HEADS * QK_NOPE:] = dv_ref[...].astype(BF16)
        zv = zm_ref[...]
        dqn = _dot(dql_ref[...], wq_ref[...])
        dkvn = _dot(dkvl_ref[...], wkv_ref[...])
        dcq, d_qg = _rms_bwd(dqn, zv[:, :Q_LORA], qg_ref[...])
        dckv, d_kvg = _rms_bwd(dkvn, zv[:, Q_LORA:Q_LORA + KV_LORA], kvg_ref[...])
        dzm_ref[:, :Q_LORA] = dcq.astype(BF16)
        dzm_ref[:, Q_LORA:Q_LORA + KV_LORA] = dckv.astype(BF16)
        dzm_ref[:, Q_LORA + KV_LORA:] = _rope_transposed(dkr, tables).astype(BF16)
        sums_ref[0:1, :Q_LORA] += d_qg
        sums_ref[0:1, Q_LORA:Q_LORA + KV_LORA] += d_kvg

    def rows(n):
        return pl.BlockSpec((tm, n), lambda i: (i, 0))

    return pl.pallas_call(
        body, name="mla_project_bwd", grid=(t // tm,),
        in_specs=[rows(QK_COLS), rows(QK_COLS), rows(MLA_WIDTH), rows(ZM_COLS), rows(1), _row(inv_freq),
                  _row(qg), _row(kvg), _row(w_uq), _row(w_ukv)],
        out_specs=[rows(QK_COLS), rows(QK_COLS), rows(ZM_COLS), pl.BlockSpec((8, ZM_COLS), lambda i: (0, 0))],
        out_shape=[jax.ShapeDtypeStruct((t, QK_COLS), BF16), jax.ShapeDtypeStruct((t, QK_COLS), BF16),
                   jax.ShapeDtypeStruct((t, ZM_COLS), BF16), jax.ShapeDtypeStruct((8, ZM_COLS), F32)],
        compiler_params=_params(("arbitrary",)),
    )(dq, dk, dv, zm, pos, inv_freq, qg, kvg, w_uq, w_ukv)


def mix_in_backward(dzc, dzm, w_in, x, dxo, gn, sc, gate):
    t, d = x.shape
    tm = _tile(t, ROW_TILE, 16)

    def body(dzc_ref, dzm_ref, w_ref, x_ref, dxo_ref, gn_ref, sc_ref, gate_ref, dx_ref, dy_ref, sums_ref):
        @pl.when(pl.program_id(0) == 0)
        def _():
            sums_ref[...] = jnp.zeros_like(sums_ref)

        dh = _dot(dzc_ref[...], w_ref[:ZC_COLS, :]) + _dot(dzm_ref[...], w_ref[ZC_COLS:, :])
        dx, d_sh, d_sc, d_gn = _norm_mod_bwd(dh, x_ref[...], gn_ref[...], sc_ref[...])
        dx = dxo_ref[...] + dx
        dx_ref[...] = dx
        dy_ref[...] = (0.5 * gate_ref[...] * dx).astype(BF16)
        _add_rows(sums_ref, [d_sh, d_sc, d_gn])

    def rows(n):
        return pl.BlockSpec((tm, n), lambda i: (i, 0))

    return pl.pallas_call(
        body, name="mix_in_bwd", grid=(t // tm,),
        in_specs=[rows(ZC_COLS), rows(ZM_COLS), _row(w_in), rows(d), rows(d), _row(gn), _row(sc), _row(gate)],
        out_specs=[rows(d), rows(d), pl.BlockSpec((8, d), lambda i: (0, 0))],
        out_shape=[jax.ShapeDtypeStruct((t, d), F32), jax.ShapeDtypeStruct((t, d), BF16),
                   jax.ShapeDtypeStruct((8, d), F32)],
        compiler_params=_params(("arbitrary",)),
    )(dzc, dzm, w_in, x, dxo, gn, sc, gate)


def final_loss(x, target, g, gate):
    t, d = x.shape
    tm = _tile(t, ROW_TILE, 16)

    def body(x_ref, t_ref, g_ref, gate_ref, dx_ref, dy_ref, sums_ref):
        @pl.when(pl.program_id(0) == 0)
        def _():
            sums_ref[...] = jnp.zeros_like(sums_ref)

        gv = g_ref[...]
        xhat, r = _rms(x_ref[...])
        err = xhat * gv - t_ref[...]
        dyf = err * (1.0 / d)
        dxh = dyf * gv
        dx = r * (dxh - xhat * jnp.mean(dxh * xhat, axis=-1, keepdims=True))
        dx_ref[...] = dx
        dy_ref[...] = (0.5 * gate_ref[...] * dx).astype(BF16)
        _add_rows(sums_ref, [jnp.sum(dyf * xhat, axis=0, keepdims=True),
                             jnp.sum(err * err, axis=0, keepdims=True) * (0.5 / d)])

    row = pl.BlockSpec((tm, d), lambda i: (i, 0))
    return pl.pallas_call(
        body, name="final_loss", grid=(t // tm,),
        in_specs=[row, row, _row(g), _row(gate)],
        out_specs=[row, row, pl.BlockSpec((8, d), lambda i: (0, 0))],
        out_shape=[jax.ShapeDtypeStruct((t, d), F32), jax.ShapeDtypeStruct((t, d), BF16),
                   jax.ShapeDtypeStruct((8, d), F32)],
        compiler_params=_params(("arbitrary",)),
    )(x, target, g, gate)


def adamw(w, g, m, v, name):
    r, n = w.shape
    tr = _tile(r, max(8, (1 << 19) // n), 8)

    def body(w_ref, g_ref, m_ref, v_ref, d_ref, mo_ref, vo_ref):
        gv = g_ref[...]
        m_new = ADAM_B1 * m_ref[...] + (1.0 - ADAM_B1) * gv
        v_new = ADAM_B2 * v_ref[...] + (1.0 - ADAM_B2) * (gv * gv)
        m_hat = m_new / (1.0 - ADAM_B1 ** ADAM_STEP)
        v_hat = v_new / (1.0 - ADAM_B2 ** ADAM_STEP)
        d_ref[...] = -ADAM_LR * (m_hat / (jnp.sqrt(v_hat) + ADAM_EPS) + ADAM_WD * w_ref[...])
        mo_ref[...] = m_new
        vo_ref[...] = v_new

    blk = pl.BlockSpec((tr, n), lambda i: (i, 0))
    shape = jax.ShapeDtypeStruct((r, n), F32)
    return pl.pallas_call(
        body, name=name, grid=(r // tr,), in_specs=[blk] * 4, out_specs=[blk] * 3, out_shape=[shape] * 3,
        compiler_params=_params(("arbitrary",)),
    )(w, g, m, v)


def _pad_to(v, n):
    return jnp.pad(v, (0, n - v.shape[0]))


def _pad_heads(w, axis_len):
    n = w.shape[1]
    return jnp.pad(w.reshape(MLA_HEADS, axis_len, n), ((0, 0), (0, HEAD_PAD - axis_len), (0, 0))).reshape(-1, n)


def _swap_head_parts(w, inner, outer):
    n = w.shape[1]
    return w.reshape(outer, inner, QK_NOPE, n).transpose(1, 0, 2, 3).reshape(-1, n)


def kernel(x, c, positions, ada_w, ada_b, norm_ffn1_g, ffn1_w1, ffn1_w3, ffn1_w2, norm_mix_g, w_in, conv_w, q_norm_g, w_uq, kv_norm_g, w_ukv, out_norm_g, w_out, norm_ffn2_g, ffn2_w1, ffn2_w3, ffn2_w2, final_norm_g, loss_target, m_ada_w, m_ada_b, m_norm_ffn1_g, m_ffn1_w1, m_ffn1_w3, m_ffn1_w2, m_norm_mix_g, m_w_in, m_conv_w, m_q_norm_g, m_w_uq, m_kv_norm_g, m_w_ukv, m_out_norm_g, m_w_out, m_norm_ffn2_g, m_ffn2_w1, m_ffn2_w3, m_ffn2_w2, m_final_norm_g, v_ada_w, v_ada_b, v_norm_ffn1_g, v_ffn1_w1, v_ffn1_w3, v_ffn1_w2, v_norm_mix_g, v_w_in, v_conv_w, v_q_norm_g, v_w_uq, v_kv_norm_g, v_w_ukv, v_out_norm_g, v_w_out, v_norm_ffn2_g, v_ffn2_w1, v_ffn2_w3, v_ffn2_w2, v_final_norm_g):
    t, d = x.shape[1], x.shape[2]
    f = ffn1_w2.shape[1] * N_DEV
    me = 4 * lax.axis_index("x") + 2 * lax.axis_index("y") + lax.axis_index("c")
    my_c = lax.axis_index("c")
    my_chip = 2 * lax.axis_index("x") + lax.axis_index("y")
    xs = x[0]
    n_ada = ada_w.shape[2]
    cw_n = conv_w.shape[2]

    c_rows = jnp.broadcast_to(c, (8, d))
    conv_rows = jnp.pad(conv_w[0], ((0, 8 - CONV_K), (0, LANES - cw_n)))
    ffn1_blocks = jnp.stack([ffn1_w1[0].T, ffn1_w3[0].T, ffn1_w2[0]]).astype(BF16)
    ffn2_blocks = jnp.stack([ffn2_w1[0].T, ffn2_w3[0].T, ffn2_w2[0]]).astype(BF16)
    c_all, conv_all, ffn1_all = all_gather([c_rows, conv_rows, ffn1_blocks], [0, 0, 1], "gather_first")
    c_all = c_all[:, 0, :]
    conv_full8 = conv_all[:, :, :cw_n].transpose(1, 0, 2).reshape(8, CONV_WIDTH)
    ffn1_ws = ffn1_all.reshape(3, f, d)
    gather_rest = riding_gather(
        [ffn2_blocks, w_in[0].T.astype(BF16), w_uq[0].T.astype(BF16), w_ukv[0].T.astype(BF16), w_out[0].astype(BF16)],
        [1, 0, 0, 0, 0])

    ada_b_cols = lax.dynamic_slice_in_dim(ada_b, me * n_ada, n_ada, axis=1)
    mod_cols = ada_forward(c_all, ada_w[0], ada_b_cols)
    mod_all, = all_gather([mod_cols], [0], "gather_mod")
    mod = lax.dynamic_index_in_dim(mod_all, me, axis=1, keepdims=False).reshape(N_MOD, 1, d)
    sh1, sc1, g1, sh2, sc2, g2, sh3, sc3, g3 = [mod[i] for i in range(N_MOD)]

    gf = final_norm_g.reshape(1, d)
    x1, h1, a1, b1, y1, *gathered = ffn_forward(xs, norm_ffn1_g, sc1, sh1, g1, ffn1_ws, 0, "ffn1_fwd", gather_rest)
    ffn2_ws = gathered[0].reshape(3, f, d)
    w_in_p = jnp.pad(gathered[1].reshape(IN_COLS, d), ((0, ZC_COLS + ZM_COLS - IN_COLS), (0, 0)))
    w_uq_p = _pad_heads(gathered[2].reshape(-1, Q_LORA), QK_NOPE + QK_ROPE)
    w_ukv_p = _swap_head_parts(gathered[3].reshape(-1, KV_LORA), 2, MLA_HEADS)
    w_out_f = gathered[4].reshape(MIX_WIDTH, d)
    h2, zc, zm = mix_in_forward(x1, norm_mix_g, sc2, sh2, w_in_p)
    pos = positions[0].astype(F32).reshape(t, 1)
    inv_freq = ROPE_THETA ** (-jnp.arange(0, QK_ROPE, 2, dtype=F32) / QK_ROPE)
    inv_freq = jnp.concatenate([inv_freq, inv_freq, jnp.zeros((LANES - QK_ROPE,), F32)]).reshape(1, LANES)
    qn, kvn, q, k, v = mla_project(zm, pos, inv_freq, q_norm_g, kv_norm_g, w_uq_p, w_ukv_p)
    o, lse = attention_forward(q, k, v)
    lane = jnp.arange(CONV_WIDTH)
    gmat_a = (lane[:, None] // (CONV_WIDTH // CONV_GROUPS) == lane[None, :] // (CONV_WIDTH // CONV_GROUPS))
    gmat_a = (gmat_a / (CONV_WIDTH // CONV_GROUPS)).astype(BF16)
    gmat_b = ((lane[:, None] // V_HEAD == lane[None, :] // V_HEAD) / V_HEAD).astype(BF16)
    x2, yn, y2, ya = mix_out_forward(zc, o, conv_full8, out_norm_g, gmat_a, gmat_b, w_out_f, x1, g2)
    x3, h3, a3, b3, y3 = ffn_forward(x2, norm_ffn2_g, sc3, sh3, g3, ffn2_ws, 0, "ffn2_fwd")
    dx3, dy3, sums_f = final_loss(x3, loss_target[0], gf, g3)

    chip_idx = jnp.bitwise_xor(my_chip, jnp.array([0, 2, 1, 3], jnp.int32)).astype(jnp.int32)
    src_idx = (2 * chip_idx + my_c).astype(jnp.int32)

    def chip_sums(tag, named):
        g8 = [g.reshape(N_DEV, g.shape[0] // N_DEV, g.shape[1]) for _, g in named]
        got = exchange_sibling(g8, "rs_sibling_" + tag)
        return [add_sibling(g, r, src_idx, chip_idx, "rs_add_" + n) for g, r, (n, _) in zip(g8, got, named)]

    dx2, da3, db3, u3, sums_3 = ffn_backward(dx3, dy3, x2, a3, b3, y3, norm_ffn2_g, sc3, ffn2_ws, 0, "ffn2_bwd")
    ffn2_named = [("ffn2_w1", matmul_tn(da3, h3, "ffn2_gw1")), ("ffn2_w3", matmul_tn(db3, h3, "ffn2_gw3")),
                  ("ffn2_w2", matmul_tn(u3, dy3, "ffn2_gw2"))]
    ffn2_sums = chip_sums("ffn2", ffn2_named)
    dy2, dya, do, delta, sums_2d, sums_2o = mix_out_backward(dx2, y2, g2, ya, o, out_norm_g, gmat_a, gmat_b, w_out_f)
    g_w_out = matmul_tn(yn, dy2, "gw_out")
    nq = t // _tile(t, ATTN_TILE, CHUNK)
    stat_shape = (MLA_HEADS, nq, 1, t // nq)
    dq, dk, dv, *ffn2_got = attention_backward(q, k, v, do, lse.reshape(stat_shape), delta.reshape(stat_shape),
                                               riding_exchange([s[1] for s in ffn2_sums]))
    dzc, sums_c = conv_backward(zc, dya, conv_full8)
    dql, dkvl, dzm, sums_m = mla_project_backward(dq, dk, dv, zm, pos, inv_freq, q_norm_g, kv_norm_g, w_uq_p, w_ukv_p)
    g_w_uq_p = matmul_tn(dql, qn, "gw_uq")
    g_w_ukv_p = matmul_tn(dkvl, kvn, "gw_ukv")
    g_w_in = jnp.concatenate([matmul_tn(dzc, h2, "gw_in_conv"), matmul_tn(dzm, h2, "gw_in_mla")])[:IN_COLS]
    g_w_uq = g_w_uq_p.reshape(MLA_HEADS, HEAD_PAD, Q_LORA)[:, :QK_NOPE + QK_ROPE].reshape(-1, Q_LORA)
    g_w_ukv = _swap_head_parts(g_w_ukv_p, MLA_HEADS, 2)
    mix_named = [("w_in", g_w_in), ("w_uq", g_w_uq), ("w_ukv", g_w_ukv), ("w_out", g_w_out)]
    mix_sums = chip_sums("mix", mix_named)
    dx1, dy1, sums_1m = mix_in_backward(dzc, dzm, w_in_p, x1, dx2, norm_mix_g, sc2, g1)
    dx0, da1, db1, u1, sums_1, *mix_got = ffn_backward(
        dx1, dy1, xs, a1, b1, y1, norm_ffn1_g, sc1, ffn1_ws, 0, "ffn1_bwd", riding_exchange([s[1] for s in mix_sums]))
    ffn1_named = [("ffn1_w1", matmul_tn(da1, h1, "ffn1_gw1")), ("ffn1_w3", matmul_tn(db1, h1, "ffn1_gw3")),
                  ("ffn1_w2", matmul_tn(u1, dy1, "ffn1_gw2"))]
    ffn1_sums = chip_sums("ffn1", ffn1_named)
    ffn1_got = exchange_chips([s[1] for s in ffn1_sums], "rs_chips_ffn1")
    transposed = {"ffn1_w1", "ffn1_w3", "ffn2_w1", "ffn2_w3", "w_in", "w_uq", "w_ukv"}
    g_sh = {}
    for named, group_sums, group_got in ((ffn2_named, ffn2_sums, ffn2_got), (mix_named, mix_sums, mix_got),
                                         (ffn1_named, ffn1_sums, ffn1_got)):
        for (n, _), (own, _), got in zip(named, group_sums, group_got):
            g_rows = add_received(own, got, "rs_sum_" + n)
            g_sh[n] = g_rows.T if n in transposed else g_rows

    dmod = jnp.concatenate([sums_1[0], sums_1[1], sums_1[2], sums_1m[0], sums_1m[1], sums_2d[0],
                            sums_3[0], sums_3[1], sums_3[2]])
    pieces = [dmod, sums_1[3], sums_1m[2], sums_m[0, :Q_LORA], sums_m[0, Q_LORA:Q_LORA + KV_LORA], sums_2o[0],
              sums_3[3], sums_f[0], sums_f[1], sums_c[:CONV_K].reshape(-1)]
    plens = [p.shape[0] for p in pieces]
    poffs = [sum(plens[:i]) for i in range(len(plens))]
    vec_len = -(-sum(plens) // 1024) * 1024
    vec = _pad_to(jnp.concatenate(pieces), vec_len).reshape(-1, LANES)
    vec_all, = all_gather([vec], [0], "gather_sums")
    tot = sum_devices(vec_all).reshape(-1)
    g_ada_b, g_n1, g_nmix, g_qg, g_kvg, g_og, g_n3, g_gf, loss_lanes, g_conv_full = [
        tot[o:o + n] for o, n in zip(poffs, plens)]
    loss = sum_lanes(loss_lanes.reshape(1, d))[0, 0]
    g_conv = lax.dynamic_slice_in_dim(g_conv_full.reshape(CONV_K, CONV_WIDTH), me * cw_n, cw_n, axis=1)
    dmod_all = vec_all.reshape(N_DEV, vec_len)[:, :N_MOD * d]
    dmod_cols = lax.dynamic_slice_in_dim(dmod_all, me * n_ada, n_ada, axis=1)
    g_ada_w = ada_backward(jnp.pad(c_all, ((0, 8), (0, 0))), jnp.pad(dmod_cols, ((0, 8), (0, 0))))

    def update(name, w, g, m, v):
        shape = w.shape
        two_d = (-1, shape[-1])
        dlt, nm, nv = adamw(w.reshape(two_d), g.reshape(two_d), m.reshape(two_d), v.reshape(two_d), "adamw_" + name)
        return g.reshape(shape), dlt.reshape(shape), nm.reshape(shape), nv.reshape(shape)

    res = {}
    res["ada_w"] = update("ada_w", ada_w, g_ada_w, m_ada_w, v_ada_w)
    big = [("ffn1_w1", ffn1_w1, m_ffn1_w1, v_ffn1_w1), ("ffn1_w3", ffn1_w3, m_ffn1_w3, v_ffn1_w3),
           ("ffn2_w1", ffn2_w1, m_ffn2_w1, v_ffn2_w1), ("ffn2_w3", ffn2_w3, m_ffn2_w3, v_ffn2_w3),
           ("w_in", w_in, m_w_in, v_w_in), ("w_uq", w_uq, m_w_uq, v_w_uq), ("w_ukv", w_ukv, m_w_ukv, v_w_ukv),
           ("ffn1_w2", ffn1_w2, m_ffn1_w2, v_ffn1_w2), ("ffn2_w2", ffn2_w2, m_ffn2_w2, v_ffn2_w2),
           ("w_out", w_out, m_w_out, v_w_out)]
    for name, w, m, v in big:
        res[name] = update(name, w, g_sh[name], m, v)
    smalls = [("ada_b", ada_b, g_ada_b, m_ada_b, v_ada_b),
              ("norm_ffn1_g", norm_ffn1_g, g_n1, m_norm_ffn1_g, v_norm_ffn1_g),
              ("norm_mix_g", norm_mix_g, g_nmix, m_norm_mix_g, v_norm_mix_g),
              ("conv_w", conv_w, g_conv, m_conv_w, v_conv_w),
              ("q_norm_g", q_norm_g, g_qg, m_q_norm_g, v_q_norm_g),
              ("kv_norm_g", kv_norm_g, g_kvg, m_kv_norm_g, v_kv_norm_g),
              ("out_norm_g", out_norm_g, g_og, m_out_norm_g, v_out_norm_g),
              ("norm_ffn2_g", norm_ffn2_g, g_n3, m_norm_ffn2_g, v_norm_ffn2_g),
              ("final_norm_g", final_norm_g, g_gf, m_final_norm_g, v_final_norm_g)]
    slens = [w.size for _, w, _, _, _ in smalls]
    soffs = [sum(slens[:i]) for i in range(len(slens))]
    s_len = -(-sum(slens) // 1024) * 1024

    def pack_small(i):
        return _pad_to(jnp.concatenate([s[i].reshape(-1) for s in smalls]), s_len).reshape(8, -1)

    s_out = adamw(pack_small(1), pack_small(2), pack_small(3), pack_small(4), "adamw_small")
    for (name, w, g, _, _), o, n in zip(smalls, soffs, slens):
        res[name] = (g.reshape(w.shape),) + tuple(a.reshape(-1)[o:o + n].reshape(w.shape) for a in s_out)

    order = ["ada_w", "ada_b", "norm_ffn1_g", "ffn1_w1", "ffn1_w3", "ffn1_w2", "norm_mix_g", "w_in", "conv_w",
             "q_norm_g", "w_uq", "kv_norm_g", "w_ukv", "out_norm_g", "w_out", "norm_ffn2_g", "ffn2_w1", "ffn2_w3",
             "ffn2_w2", "final_norm_g"]
    return (loss, dx0.reshape(x.shape), *[res[n][0] for n in order], *[res[n][1] for n in order],
            *[res[n][2] for n in order], *[res[n][3] for n in order])
```

```python
import functools

import jax
import jax.numpy as jnp
from jax import lax
from jax.experimental import pallas as pl
from jax.experimental.pallas import tpu as pltpu

F32 = jnp.float32
BF16 = jnp.bfloat16
MESH_ID = pl.DeviceIdType.MESH
N_DEV = 8

EPS = 1e-6
CHUNK = 64
N_MOD = 9
CONV_WIDTH = 512
CONV_GROUPS = 8
CONV_K = 3
MLA_HEADS = 4
QK_NOPE = 128
QK_ROPE = 64
V_HEAD = 128
Q_LORA = 384
KV_LORA = 256
ROPE_THETA = 10000.0
MLA_WIDTH = MLA_HEADS * V_HEAD
MIX_WIDTH = CONV_WIDTH + MLA_WIDTH
IN_COLS = 3 * CONV_WIDTH + Q_LORA + KV_LORA + QK_ROPE
ZC_COLS = 3 * CONV_WIDTH
ZM_COLS = Q_LORA + KV_LORA + 128
HEAD_PAD = 256
QK_COLS = MLA_HEADS * HEAD_PAD
ATTN_SCALE = (QK_NOPE + QK_ROPE) ** -0.5
NEG_INF = -1e30

ADAM_LR = 0.001
ADAM_B1 = 0.9
ADAM_B2 = 0.999
ADAM_EPS = 1e-08
ADAM_WD = 0.01
ADAM_STEP = 10

LANES = 128
MXU_COLS = 256
VMEM_LIMIT = 56 * 1024 * 1024
ROW_TILE = 512
FFN_FWD_TILE = (1024, 256)
FFN_BWD_TILE = (512, 1408)
GRAD_TILE = 1408
ATTN_TILE = 512

NN = (((1,), (0,)), ((), ()))
NT = (((1,), (1,)), ((), ()))
TN = (((0,), (0,)), ((), ()))


def _dot(a, b, dims=NN):
    return lax.dot_general(a, b, dims, preferred_element_type=F32)


def _tile(n, cap, mult=LANES):
    best = None
    for t in range(mult, min(n, cap) + 1, mult):
        if n % t == 0:
            best = t
    return n if best is None else best


def _params(sem=None):
    return pltpu.CompilerParams(dimension_semantics=sem, vmem_limit_bytes=VMEM_LIMIT)


def _row(v):
    return pl.BlockSpec(v.shape, lambda *_: (0,) * v.ndim)


def _sigmoid(x):
    return 0.5 * jnp.tanh(0.5 * x) + 0.5


def _rms(x):
    r = lax.rsqrt(jnp.mean(x * x, axis=-1, keepdims=True) + EPS)
    return x * r, r


def _norm_mod_bwd(dh, x, gn, sc):
    xhat, r = _rms(x)
    d_sh = jnp.sum(dh, axis=0, keepdims=True)
    d_sc = jnp.sum(dh * (xhat * gn), axis=0, keepdims=True)
    dxn = dh * (1.0 + sc)
    d_gn = jnp.sum(dxn * xhat, axis=0, keepdims=True)
    dxh = dxn * gn
    dx = r * (dxh - xhat * jnp.mean(dxh * xhat, axis=-1, keepdims=True))
    return dx, d_sh, d_sc, d_gn


def _group_mean(v, gmat):
    hi = v.astype(BF16)
    lo = (v - hi.astype(F32)).astype(BF16)
    return _dot(hi, gmat) + _dot(lo, gmat)


def _add_rows(ref, rows):
    for r, v in enumerate(rows):
        ref[r:r + 1, :] += v


def _window(ref, axis, j):
    return ref.at[(slice(None),) * axis + (j,)]


def _any_specs(n):
    return [pl.BlockSpec(memory_space=pl.ANY)] * n


def all_gather(blocks, axes, name):
    n_arr = len(blocks)

    def body(*refs):
        start, forward, finish = _gather_steps(refs[:n_arr], refs[n_arr:2 * n_arr], axes, *refs[2 * n_arr:])
        start()
        for j in range(3):
            forward(j)
        finish()

    return pl.pallas_call(
        body, name=name, out_shape=_gathered_shapes(blocks, axes),
        in_specs=_any_specs(n_arr), out_specs=_any_specs(n_arr), scratch_shapes=_gather_sems(n_arr),
    )(*blocks)


def _gathered_shapes(blocks, axes):
    return [jax.ShapeDtypeStruct(b.shape[:ax] + (N_DEV,) + b.shape[ax:], b.dtype) for b, ax in zip(blocks, axes)]


def _gather_sems(n_arr):
    return [pltpu.SemaphoreType.DMA((7, n_arr)), pltpu.SemaphoreType.DMA((7, n_arr)), pltpu.SemaphoreType.DMA((n_arr,))]


def _gather_steps(ins, outs, axes, send_sems, recv_sems, local_sems):
    arrays = range(len(ins))
    x, y, c = lax.axis_index("x"), lax.axis_index("y"), lax.axis_index("c")
    me, sibling = (x, y, c), (x, y, 1 - c)
    chips = [(1 - x, y), (x, 1 - y), (1 - x, 1 - y)]

    def slot(a, px, py, pc):
        return _window(outs[a], axes[a], 4 * px + 2 * py + pc)

    def copy(a, k, block, to, src=None):
        return pltpu.make_async_remote_copy(
            src_ref=slot(a, *block) if src is None else src, dst_ref=slot(a, *block),
            send_sem=send_sems.at[k, a], recv_sem=recv_sems.at[k, a], device_id=to, device_id_type=MESH_ID)

    def mine(a):
        return pltpu.make_async_copy(ins[a], slot(a, *me), local_sems.at[a])

    def first():
        return ([copy(a, 0, me, sibling, src=ins[a]) for a in arrays]
                + [copy(a, 1 + j, me, (*chip, c), src=ins[a]) for j, chip in enumerate(chips) for a in arrays])

    def passed(j):
        return [copy(a, 4 + j, (*chips[j], c), sibling) for a in arrays]

    def start():
        for a in arrays:
            mine(a).start()
        for cp in first():
            cp.start()

    def forward(j):
        for a, cp in zip(arrays, passed(j)):
            copy(a, 1 + j, (*chips[j], c), me).wait_recv()
            cp.start()

    def finish():
        for a in arrays:
            copy(a, 0, sibling, me).wait_recv()
        for j, chip in enumerate(chips):
            for a in arrays:
                copy(a, 4 + j, (*chip, 1 - c), me).wait_recv()
        for cp in first() + passed(0) + passed(1) + passed(2):
            cp.wait_send()
        for a in arrays:
            mine(a).wait()

    return start, forward, finish


def exchange_sibling(grads, name):
    n_arr = len(grads)

    def body(*refs):
        ins, outs = refs[:n_arr], refs[n_arr:2 * n_arr]
        send_sems, recv_sems = refs[2 * n_arr:]
        x, y, c = lax.axis_index("x"), lax.axis_index("y"), lax.axis_index("c")

        def copy(a, src, dst):
            return pltpu.make_async_remote_copy(
                src_ref=src, dst_ref=dst, send_sem=send_sems.at[a], recv_sem=recv_sems.at[a],
                device_id=(x, y, 1 - c), device_id_type=MESH_ID)

        for a in range(n_arr):
            for k in range(4):
                copy(a, ins[a].at[2 * k + (1 - c)], outs[a].at[k]).start()
        whole = [copy(a, ins[a].at[pl.ds(0, 4)], outs[a]) for a in range(n_arr)]
        for cp in whole:
            cp.wait_recv()
        for cp in whole:
            cp.wait_send()

    return pl.pallas_call(
        body, name=name,
        out_shape=[jax.ShapeDtypeStruct((4,) + g.shape[1:], g.dtype) for g in grads],
        in_specs=_any_specs(n_arr), out_specs=_any_specs(n_arr),
        scratch_shapes=[pltpu.SemaphoreType.DMA((n_arr,)), pltpu.SemaphoreType.DMA((n_arr,))],
    )(*grads)


def _exchange_sems(n_arr):
    return [pltpu.SemaphoreType.DMA((n_arr,)), pltpu.SemaphoreType.DMA((n_arr,))]


def _chip_exchange_steps(ins, outs, send_sems, recv_sems):
    x, y, c = lax.axis_index("x"), lax.axis_index("y"), lax.axis_index("c")
    chips = [(1 - x, y), (x, 1 - y), (1 - x, 1 - y)]

    def copy(a, src, dst, chip):
        return pltpu.make_async_remote_copy(
            src_ref=src, dst_ref=dst, send_sem=send_sems.at[a], recv_sem=recv_sems.at[a],
            device_id=(*chip, c), device_id_type=MESH_ID)

    def start():
        for a in range(len(ins)):
            for j, chip in enumerate(chips):
                copy(a, ins[a].at[j], outs[a].at[j], chip).start()

    def finish():
        whole = [copy(a, ins[a], outs[a], chips[0]) for a in range(len(ins))]
        for cp in whole:
            cp.wait_recv()
        for cp in whole:
            cp.wait_send()

    return start, finish


def riding_gather(blocks, axes):
    def phases(ins, outs, *sems):
        start, forward, finish = _gather_steps(ins, outs, axes, *sems)
        return [start] + [functools.partial(forward, j) for j in range(3)] + [finish]

    return dict(operands=blocks, out_shape=_gathered_shapes(blocks, axes), sems=_gather_sems(len(blocks)),
                phases=phases, when=("first", "late0", "late1", "late2", "last"))


def riding_exchange(parts):
    def phases(ins, outs, *sems):
        return list(_chip_exchange_steps(ins, outs, *sems))

    return dict(operands=parts, out_shape=[jax.ShapeDtypeStruct(p.shape, p.dtype) for p in parts],
                sems=_exchange_sems(len(parts)), phases=phases, when=("first", "last"))


def _call_with_rider(body, rider, *, name, grid, in_specs, out_specs, out_shape, scratch_shapes, operands):
    params = _params(("arbitrary",) * len(grid))
    if rider is None:
        return pl.pallas_call(body, name=name, grid=grid, in_specs=in_specs, out_specs=out_specs,
                              out_shape=out_shape, scratch_shapes=scratch_shapes, compiler_params=params)(*operands)
    n_in, n_out, n_scr, k = len(in_specs), len(out_specs), len(scratch_shapes), len(rider["operands"])
    rows, cols = grid
    assert cols >= 3 or "late0" not in rider["when"]
    late_row = max(rows - 2, 0)
    at = {"first": (0, 0), "last": (rows - 1, cols - 1),
          "late0": (late_row, 0), "late1": (late_row, 1), "late2": (late_row, 2)}

    def wrapped(*refs):
        ins, c_in = refs[:n_in], refs[n_in:n_in + k]
        outs, c_out = refs[n_in + k:n_in + k + n_out], refs[n_in + k + n_out:n_in + 2 * k + n_out]
        scratch, sems = refs[n_in + 2 * k + n_out:n_in + 2 * k + n_out + n_scr], refs[n_in + 2 * k + n_out + n_scr:]
        i, j = pl.program_id(0), pl.program_id(1)
        phases = rider["phases"](c_in, c_out, *sems)
        for fn, key in zip(phases, rider["when"]):
            if key != "last":
                pl.when(jnp.logical_and(i == at[key][0], j == at[key][1]))(fn)
        body(*ins, *outs, *scratch)
        pl.when(jnp.logical_and(i == at["last"][0], j == at["last"][1]))(phases[-1])

    return pl.pallas_call(
        wrapped, name=name, grid=grid,
        in_specs=list(in_specs) + _any_specs(k), out_specs=list(out_specs) + _any_specs(k),
        out_shape=list(out_shape) + rider["out_shape"], scratch_shapes=list(scratch_shapes) + rider["sems"],
        compiler_params=params)(*operands, *rider["operands"])


def add_sibling(g8, got, src_idx, chip_idx, name):
    _, r, n = g8.shape
    tr = _tile(r, 256, 16)

    def body(si_ref, ci_ref, g0_ref, g1_ref, g2_ref, g3_ref, got_ref, own_ref, send_ref):
        own_ref[...] = g0_ref[0] + got_ref[ci_ref[0]]
        for j, g_ref in enumerate((g1_ref, g2_ref, g3_ref)):
            send_ref[j] = (g_ref[0] + got_ref[ci_ref[j + 1]]).astype(BF16)

    def mine(j):
        return pl.BlockSpec((1, tr, n), lambda i, si, ci: (si[j], i, 0))

    return pl.pallas_call(
        body, name=name,
        out_shape=[jax.ShapeDtypeStruct((r, n), F32), jax.ShapeDtypeStruct((3, r, n), BF16)],
        grid_spec=pltpu.PrefetchScalarGridSpec(
            num_scalar_prefetch=2, grid=(r // tr,),
            in_specs=[mine(0), mine(1), mine(2), mine(3), pl.BlockSpec((4, tr, n), lambda i, si, ci: (0, i, 0))],
            out_specs=[pl.BlockSpec((tr, n), lambda i, si, ci: (i, 0)),
                       pl.BlockSpec((3, tr, n), lambda i, si, ci: (0, i, 0))]),
        compiler_params=_params(("arbitrary",)),
    )(src_idx, chip_idx, g8, g8, g8, g8, got)


def add_received(own, got, name):
    r, n = own.shape
    tr = _tile(r, 256, 16)

    def body(a_ref, b_ref, o_ref):
        acc = a_ref[...]
        for j in range(3):
            acc = acc + b_ref[j].astype(F32)
        o_ref[...] = acc

    return pl.pallas_call(
        body, name=name,
        out_shape=jax.ShapeDtypeStruct((r, n), F32),
        grid=(r // tr,),
        in_specs=[pl.BlockSpec((tr, n), lambda i: (i, 0)), pl.BlockSpec((3, tr, n), lambda i: (0, i, 0))],
        out_specs=pl.BlockSpec((tr, n), lambda i: (i, 0)),
        compiler_params=_params(("arbitrary",)),
    )(own, got)


def sum_devices(g):
    def body(g_ref, o_ref):
        acc = g_ref[0]
        for j in range(1, N_DEV):
            acc = acc + g_ref[j]
        o_ref[...] = acc

    return pl.pallas_call(body, name="sum_devices", out_shape=jax.ShapeDtypeStruct(g.shape[1:], F32))(g)


def sum_lanes(v):
    def body(v_ref, o_ref):
        o_ref[...] = jnp.broadcast_to(jnp.sum(v_ref[...], axis=-1, keepdims=True), (1, LANES))

    return pl.pallas_call(body, name="sum_lanes", out_shape=jax.ShapeDtypeStruct((1, LANES), F32))(v)


def ada_forward(c_all, ada_w, ada_b_cols):
    nb, n = c_all.shape[0], ada_w.shape[1]

    def body(c_ref, w_ref, b_ref, o_ref):
        cv = c_ref[...]
        s = (cv * jax.nn.sigmoid(cv)).astype(BF16)
        o_ref[...] = _dot(s, w_ref[...].astype(BF16)) + b_ref[...]

    return pl.pallas_call(body, name="ada_fwd", out_shape=jax.ShapeDtypeStruct((nb, n), F32),
                          compiler_params=_params())(c_all, ada_w, ada_b_cols)


def ada_backward(c_all16, dmod16):
    d, n = c_all16.shape[1], dmod16.shape[1]

    def body(c_ref, g_ref, o_ref):
        cv = c_ref[...]
        s = (cv * jax.nn.sigmoid(cv)).astype(BF16)
        o_ref[...] = _dot(s, g_ref[...].astype(BF16), TN)

    return pl.pallas_call(body, name="ada_bwd", out_shape=jax.ShapeDtypeStruct((d, n), F32),
                          compiler_params=_params())(c_all16, dmod16)


def ffn_forward(x, gn, sc, sh, gate, ws, first, name, rider=None):
    t, d = x.shape
    f = ws.shape[1]
    tm, tf = _tile(t, FFN_FWD_TILE[0], 16), _tile(f, FFN_FWD_TILE[1])
    nf = f // tf

    def body(x_ref, gn_ref, sc_ref, sh_ref, gate_ref, w1_ref, w3_ref, w2_ref,
             xo_ref, h_ref, a_ref, b_ref, y_ref, hs, acc):
        j = pl.program_id(1)

        @pl.when(j == 0)
        def _():
            xhat, _ = _rms(x_ref[...])
            h = (xhat * gn_ref[...] * (1.0 + sc_ref[...]) + sh_ref[...]).astype(BF16)
            hs[...] = h
            h_ref[...] = h
            acc[...] = jnp.zeros_like(acc)

        h = hs[...]
        a = _dot(h, w1_ref[...], NT)
        b = _dot(h, w3_ref[...], NT)
        a_ref[...] = a.astype(BF16)
        b_ref[...] = b.astype(BF16)
        u = (a * _sigmoid(a) * b).astype(BF16)
        acc[...] += _dot(u, w2_ref[...])

        @pl.when(j == nf - 1)
        def _():
            y = acc[...]
            y_ref[...] = y.astype(BF16)
            xo_ref[...] = x_ref[...] + 0.5 * gate_ref[...] * y

    row = pl.BlockSpec((tm, d), lambda i, j: (i, 0))
    vec = pl.BlockSpec((1, d), lambda i, j: (0, 0))
    wide = pl.BlockSpec((tm, tf), lambda i, j: (i, j))
    return _call_with_rider(
        body, rider, name=name, grid=(t // tm, nf),
        in_specs=[row, vec, vec, vec, vec] + _ffn_weight_specs(first, tf, d),
        out_specs=[row, row, wide, wide, row],
        out_shape=[jax.ShapeDtypeStruct((t, d), F32), jax.ShapeDtypeStruct((t, d), BF16),
                   jax.ShapeDtypeStruct((t, f), BF16), jax.ShapeDtypeStruct((t, f), BF16),
                   jax.ShapeDtypeStruct((t, d), BF16)],
        scratch_shapes=[pltpu.VMEM((tm, d), BF16), pltpu.VMEM((tm, d), F32)],
        operands=(x, gn, sc, sh, gate, ws, ws, ws))


def _ffn_weight_specs(first, tf, d):
    return [pl.BlockSpec((None, tf, d), lambda i, j, w=first + k: (w, j, 0)) for k in range(3)]


def ffn_backward_gate(dy, a, b, ws, first, name, rider=None):
    t, d = dy.shape
    f = ws.shape[1]
    tm, tf = _tile(t, FFN_BWD_TILE[0], 16), _tile(f, FFN_BWD_TILE[1])
    nf = f // tf

    def gate_body(dy_ref, a_ref, b_ref, w2_ref, da_ref, db_ref, u_ref):
        du = _dot(dy_ref[...], w2_ref[...], NT)
        av = a_ref[...].astype(F32)
        bv = b_ref[...].astype(F32)
        s = _sigmoid(av)
        sa = av * s
        da_ref[...] = (du * bv * (s + sa * (1.0 - s))).astype(BF16)
        db_ref[...] = (du * sa).astype(BF16)
        u_ref[...] = (sa * bv).astype(BF16)

    hidden = jax.ShapeDtypeStruct((t, f), BF16)
    wide_t = pl.BlockSpec((tm, tf), lambda j, i: (i, j))
    return _call_with_rider(
        gate_body, rider, name=name, grid=(nf, t // tm),
        in_specs=[pl.BlockSpec((tm, d), lambda j, i: (i, 0)), wide_t, wide_t,
                  pl.BlockSpec((None, tf, d), lambda j, i: (first + 2, j, 0))],
        out_specs=[wide_t, wide_t, wide_t], out_shape=[hidden, hidden, hidden],
        scratch_shapes=[], operands=(dy, a, b, ws))


def ffn_backward_norm(da, db, dxo, x, y, gn, sc, ws, first, name, rider=None):
    t, d = x.shape
    f = ws.shape[1]
    tm, tf = _tile(t, FFN_BWD_TILE[0], 16), _tile(f, FFN_BWD_TILE[1])
    nf = f // tf
    row = pl.BlockSpec((tm, d), lambda i, j: (i, 0))
    vec = pl.BlockSpec((1, d), lambda i, j: (0, 0))
    wide = pl.BlockSpec((tm, tf), lambda i, j: (i, j))

    def norm_body(da_ref, db_ref, w1_ref, w3_ref, dxo_ref, x_ref, y_ref, gn_ref, sc_ref, dx_ref, sums_ref, acc):
        i, j = pl.program_id(0), pl.program_id(1)

        @pl.when(jnp.logical_and(i == 0, j == 0))
        def _():
            sums_ref[...] = jnp.zeros_like(sums_ref)

        part = _dot(da_ref[...], w1_ref[...]) + _dot(db_ref[...], w3_ref[...])

        @pl.when(j == 0)
        def _():
            acc[...] = part

        @pl.when(jnp.logical_and(j > 0, j < nf - 1))
        def _():
            acc[...] += part

        @pl.when(j == nf - 1)
        def _():
            dh = part if nf == 1 else acc[...] + part
            dxo_v = dxo_ref[...]
            dx, d_sh, d_sc, d_gn = _norm_mod_bwd(dh, x_ref[...], gn_ref[...], sc_ref[...])
            dx_ref[...] = dxo_v + dx
            d_gate = jnp.sum(dxo_v * (0.5 * y_ref[...].astype(F32)), axis=0, keepdims=True)
            _add_rows(sums_ref, [d_sh, d_sc, d_gate, d_gn])

    w1_spec, w3_spec, _ = _ffn_weight_specs(first, tf, d)
    return _call_with_rider(
        norm_body, rider, name=name, grid=(t // tm, nf),
        in_specs=[wide, wide, w1_spec, w3_spec, row, row, row, vec, vec],
        out_specs=[row, pl.BlockSpec((8, d), lambda i, j: (0, 0))],
        out_shape=[jax.ShapeDtypeStruct((t, d), F32), jax.ShapeDtypeStruct((8, d), F32)],
        scratch_shapes=[pltpu.VMEM((tm, d), F32)],
        operands=(da, db, ws, ws, dxo, x, y, gn, sc))


def matmul_tn(a, b, name):
    t, m = a.shape
    n = b.shape[1]
    tm, tn, tk = _tile(m, GRAD_TILE), _tile(n, GRAD_TILE), _tile(t, 1024, 16)
    nk = t // tk

    def body(a_ref, b_ref, o_ref, acc):
        k = pl.program_id(2)

        @pl.when(k == 0)
        def _():
            acc[...] = jnp.zeros_like(acc)

        acc[...] += _dot(a_ref[...], b_ref[...], TN)

        @pl.when(k == nk - 1)
        def _():
            o_ref[...] = acc[...]

    return pl.pallas_call(
        body, name=name, grid=(m // tm, n // tn, nk),
        in_specs=[pl.BlockSpec((tk, tm), lambda i, j, k: (k, i)), pl.BlockSpec((tk, tn), lambda i, j, k: (k, j))],
        out_specs=pl.BlockSpec((tm, tn), lambda i, j, k: (i, j)),
        out_shape=jax.ShapeDtypeStruct((m, n), F32),
        scratch_shapes=[pltpu.VMEM((tm, tn), F32)],
        compiler_params=_params(("arbitrary", "arbitrary", "arbitrary")),
    )(a, b)


def mix_in_forward(x, gn, sc, sh, w_in):
    t, d = x.shape
    tm = _tile(t, ROW_TILE, 16)

    def body(x_ref, gn_ref, sc_ref, sh_ref, w_ref, h_ref, zc_ref, zm_ref):
        xhat, _ = _rms(x_ref[...])
        h = (xhat * gn_ref[...] * (1.0 + sc_ref[...]) + sh_ref[...]).astype(BF16)
        h_ref[...] = h
        z = _dot(h, w_ref[...], NT)
        zc_ref[...] = z[:, :ZC_COLS]
        zm_ref[...] = z[:, ZC_COLS:]

    row = pl.BlockSpec((tm, d), lambda i: (i, 0))
    vec = pl.BlockSpec((1, d), lambda i: (0, 0))
    return pl.pallas_call(
        body, name="mix_in_fwd", grid=(t // tm,),
        in_specs=[row, vec, vec, vec, _row(w_in)],
        out_specs=[row, pl.BlockSpec((tm, ZC_COLS), lambda i: (i, 0)), pl.BlockSpec((tm, ZM_COLS), lambda i: (i, 0))],
        out_shape=[jax.ShapeDtypeStruct((t, d), BF16), jax.ShapeDtypeStruct((t, ZC_COLS), F32),
                   jax.ShapeDtypeStruct((t, ZM_COLS), F32)],
        compiler_params=_params(("arbitrary",)),
    )(x, gn, sc, sh, w_in)


def _rope_tables(pos, inv_freq):
    ang = pos * inv_freq
    lane = lax.broadcasted_iota(jnp.int32, ang.shape, 1)
    cos, sin = jnp.cos(ang), jnp.sin(ang)
    half = QK_ROPE // 2
    return cos, jnp.where(lane < half, -sin, 0.0), jnp.where(jnp.logical_and(lane >= half, lane < QK_ROPE), sin, 0.0)


def _rope(v, tables):
    cos, sin_a, sin_b = tables
    return v * cos + pltpu.roll(v, LANES - QK_ROPE // 2, 1) * sin_a + pltpu.roll(v, QK_ROPE // 2, 1) * sin_b


def _rope_transposed(dv, tables):
    cos, sin_a, sin_b = tables
    return dv * cos + pltpu.roll(dv * sin_a, QK_ROPE // 2, 1) + pltpu.roll(dv * sin_b, LANES - QK_ROPE // 2, 1)


def mla_project(zm, pos, inv_freq, qg, kvg, w_uq, w_ukv):
    t = zm.shape[0]
    tm = _tile(t, ROW_TILE, 16)

    def body(zm_ref, pos_ref, if_ref, qg_ref, kvg_ref, wq_ref, wkv_ref, qn_ref, kvn_ref, q_ref, k_ref, v_ref):
        zv = zm_ref[...]
        qn = (_rms(zv[:, :Q_LORA])[0] * qg_ref[...]).astype(BF16)
        kvn = (_rms(zv[:, Q_LORA:Q_LORA + KV_LORA])[0] * kvg_ref[...]).astype(BF16)
        qn_ref[...] = qn
        kvn_ref[...] = kvn
        qf = _dot(qn, wq_ref[...], NT)
        kvf = _dot(kvn, wkv_ref[...], NT)
        tables = _rope_tables(pos_ref[...], if_ref[...])
        kr = _rope(zv[:, Q_LORA + KV_LORA:], tables).astype(BF16)
        for h in range(MLA_HEADS):
            lo = h * HEAD_PAD
            q_ref[:, lo:lo + QK_NOPE] = qf[:, lo:lo + QK_NOPE].astype(BF16)
            q_ref[:, lo + QK_NOPE:lo + HEAD_PAD] = _rope(qf[:, lo + QK_NOPE:lo + HEAD_PAD], tables).astype(BF16)
            k_ref[:, lo:lo + QK_NOPE] = kvf[:, h * QK_NOPE:(h + 1) * QK_NOPE].astype(BF16)
            k_ref[:, lo + QK_NOPE:lo + HEAD_PAD] = kr
        v_ref[...] = kvf[:, MLA_HEADS * QK_NOPE:].astype(BF16)

    def rows(n):
        return pl.BlockSpec((tm, n), lambda i: (i, 0))

    return pl.pallas_call(
        body, name="mla_project", grid=(t // tm,),
        in_specs=[rows(ZM_COLS), rows(1), _row(inv_freq), _row(qg), _row(kvg), _row(w_uq), _row(w_ukv)],
        out_specs=[rows(Q_LORA), rows(KV_LORA), rows(QK_COLS), rows(QK_COLS), rows(MLA_WIDTH)],
        out_shape=[jax.ShapeDtypeStruct((t, Q_LORA), BF16), jax.ShapeDtypeStruct((t, KV_LORA), BF16),
                   jax.ShapeDtypeStruct((t, QK_COLS), BF16), jax.ShapeDtypeStruct((t, QK_COLS), BF16),
                   jax.ShapeDtypeStruct((t, MLA_WIDTH), BF16)],
        compiler_params=_params(("arbitrary",)),
    )(zm, pos, inv_freq, qg, kvg, w_uq, w_ukv)


def _chunk_mask(shape, q_axis):
    qi = lax.broadcasted_iota(jnp.int32, shape, q_axis) // CHUNK
    ki = lax.broadcasted_iota(jnp.int32, shape, 1 - q_axis) // CHUNK
    return ki <= qi


def attention_forward(q, k, v):
    t = q.shape[0]
    tq = _tile(t, ATTN_TILE, CHUNK)

    def body(q_ref, k_ref, v_ref, o_ref, lse_ref):
        i = pl.program_id(1)
        qv = q_ref[...]

        def step(kb, carry, masked):
            m, l, acc = carry
            start = pl.multiple_of(kb * tq, tq)
            s = _dot(qv, k_ref[pl.ds(start, tq), :], NT) * ATTN_SCALE
            if masked:
                s = jnp.where(_chunk_mask(s.shape, 0), s, NEG_INF)
            m_new = jnp.maximum(m, jnp.max(s, axis=-1, keepdims=True))
            alpha = jnp.exp(m - m_new)
            p = jnp.exp(s - m_new)
            l = alpha * l + jnp.sum(p, axis=-1, keepdims=True)
            acc = alpha * acc + _dot(p.astype(BF16), v_ref[pl.ds(start, tq), :])
            return m_new, l, acc

        init = (jnp.full((tq, 1), NEG_INF, F32), jnp.zeros((tq, 1), F32), jnp.zeros((tq, V_HEAD), F32))
        carry = lax.fori_loop(0, i // 2, lambda pb, cr: step(2 * pb + 1, step(2 * pb, cr, False), False), init)
        carry = lax.fori_loop(0, i % 2, lambda _, cr: step(i - 1, cr, False), carry)
        m, l, acc = step(i, carry, True)
        o_ref[...] = acc / l
        lse_ref[0] = m + jnp.log(l)

    return pl.pallas_call(
        body, name="attn_fwd", grid=(MLA_HEADS, t // tq),
        in_specs=[pl.BlockSpec((tq, HEAD_PAD), lambda h, i: (i, h)),
                  pl.BlockSpec((t, HEAD_PAD), lambda h, i: (0, h)),
                  pl.BlockSpec((t, V_HEAD), lambda h, i: (0, h))],
        out_specs=[pl.BlockSpec((tq, V_HEAD), lambda h, i: (i, h)),
                   pl.BlockSpec((1, tq, 1), lambda h, i: (h, i, 0))],
        out_shape=[jax.ShapeDtypeStruct((t, MLA_WIDTH), F32), jax.ShapeDtypeStruct((MLA_HEADS, t, 1), F32)],
        compiler_params=_params(("arbitrary", "arbitrary")),
    )(q, k, v)


def attention_backward(q, k, v, do, lse, delta, rider=None):
    t = q.shape[0]
    tq = _tile(t, ATTN_TILE, CHUNK)
    nq = t // tq

    def body(q_ref, k_ref, v_ref, do_ref, lse_ref, delta_ref, dq_ref, dk_ref, dv_ref):
        kb = pl.program_id(1)

        @pl.when(kb == 0)
        def _():
            dq_ref[...] = jnp.zeros_like(dq_ref)

        kv, vv = k_ref[...], v_ref[...]

        def step(qb, carry, masked):
            dk, dv = carry
            rows = pl.ds(pl.multiple_of(qb * tq, tq), tq)
            qv, dov = q_ref[rows, :], do_ref[rows, :]
            s = _dot(kv, qv, NT) * ATTN_SCALE
            if masked:
                s = jnp.where(_chunk_mask(s.shape, 1), s, NEG_INF)
            p = jnp.exp(s - lse_ref[0, qb])
            dv = dv + _dot(p.astype(BF16), dov)
            dp = _dot(vv, dov, NT)
            ds = (p * (dp - delta_ref[0, qb]) * ATTN_SCALE).astype(BF16)
            dk = dk + _dot(ds, qv)
            dq_ref[rows, :] += _dot(ds, kv, TN)
            return dk, dv

        carry = step(kb, (jnp.zeros((tq, HEAD_PAD), F32), jnp.zeros((tq, V_HEAD), F32)), True)
        odd = (nq - 1 - kb) % 2
        carry = lax.fori_loop(0, odd, lambda _, cr: step(kb + 1, cr, False), carry)
        first = kb + 1 + odd
        dk, dv = lax.fori_loop(0, (nq - first) // 2,
                               lambda pb, cr: step(first + 2 * pb + 1, step(first + 2 * pb, cr, False), False), carry)
        dk_ref[...] = dk
        dv_ref[...] = dv

    stat = pl.BlockSpec((1, nq, 1, tq), lambda h, j: (h, 0, 0, 0))
    return _call_with_rider(
        body, rider, name="attn_bwd", grid=(MLA_HEADS, nq),
        in_specs=[pl.BlockSpec((t, HEAD_PAD), lambda h, j: (0, h)),
                  pl.BlockSpec((tq, HEAD_PAD), lambda h, j: (j, h)),
                  pl.BlockSpec((tq, V_HEAD), lambda h, j: (j, h)),
                  pl.BlockSpec((t, V_HEAD), lambda h, j: (0, h)), stat, stat],
        out_specs=[pl.BlockSpec((t, HEAD_PAD), lambda h, j: (0, h)),
                   pl.BlockSpec((tq, HEAD_PAD), lambda h, j: (j, h)),
                   pl.BlockSpec((tq, V_HEAD), lambda h, j: (j, h))],
        out_shape=[jax.ShapeDtypeStruct((t, QK_COLS), F32), jax.ShapeDtypeStruct((t, QK_COLS), F32),
                   jax.ShapeDtypeStruct((t, MLA_WIDTH), F32)],
        scratch_shapes=[], operands=(q, k, v, do, lse, delta))


def _shift_rows(v, prev, n):
    out = pltpu.roll(v, n, 0)
    row = lax.broadcasted_iota(jnp.int32, v.shape, 0)
    for r in range(n):
        out = jnp.where(row == r, prev[8 - n + r:8 - n + r + 1, :], out)
    return out


def _advance_rows(v, nxt, n):
    rows = v.shape[0]
    out = pltpu.roll(v, rows - n, 0)
    row = lax.broadcasted_iota(jnp.int32, v.shape, 0)
    for r in range(n):
        out = jnp.where(row == rows - n + r, nxt[r:r + 1, :], out)
    return out


def _conv_taps(zc, zc_prev, first):
    w = CONV_WIDTH
    u = zc[:, w:2 * w] * zc[:, 2 * w:]
    up = jnp.where(first, 0.0, zc_prev[:, w:2 * w] * zc_prev[:, 2 * w:])
    return u, _shift_rows(u, up, 1), _shift_rows(u, up, 2)


def mix_out_forward(zc, o, conv_w, og, gmat_a, gmat_b, w_out, x, gate):
    t, d = x.shape
    tm = _tile(t, ROW_TILE, 16)
    w = CONV_WIDTH

    def body(zc_ref, zp_ref, o_ref, cw_ref, og_ref, ga_ref, gb_ref, w_ref, x_ref, gate_ref,
             xo_ref, yn_ref, y_ref, ya_ref):
        zc_v = zc_ref[...]
        u, u1, u2 = _conv_taps(zc_v, zp_ref[...], pl.program_id(0) == 0)
        cw = cw_ref[...]
        ya = zc_v[:, :w] * (cw[0:1] * u2 + cw[1:2] * u1 + cw[2:3] * u)
        ya_ref[...] = ya
        ov = o_ref[...]
        ogv = og_ref[...]
        yn_ref[:, :w] = (ya * lax.rsqrt(_group_mean(ya * ya, ga_ref[...]) + EPS) * ogv[:, :w]).astype(BF16)
        yn_ref[:, w:] = (ov * lax.rsqrt(_group_mean(ov * ov, gb_ref[...]) + EPS) * ogv[:, w:]).astype(BF16)
        y = _dot(yn_ref[...], w_ref[...])
        y_ref[...] = y.astype(BF16)
        xo_ref[...] = x_ref[...] + gate_ref[...] * y

    def rows(n):
        return pl.BlockSpec((tm, n), lambda i: (i, 0))

    prev = pl.BlockSpec((8, ZC_COLS), lambda i: (jnp.maximum(i * (tm // 8) - 1, 0), 0))
    return pl.pallas_call(
        body, name="mix_out_fwd", grid=(t // tm,),
        in_specs=[rows(ZC_COLS), prev, rows(MLA_WIDTH), _row(conv_w), _row(og), _row(gmat_a), _row(gmat_b),
                  _row(w_out), rows(d), _row(gate)],
        out_specs=[rows(d), rows(MIX_WIDTH), rows(d), rows(w)],
        out_shape=[jax.ShapeDtypeStruct((t, d), F32), jax.ShapeDtypeStruct((t, MIX_WIDTH), BF16),
                   jax.ShapeDtypeStruct((t, d), BF16), jax.ShapeDtypeStruct((t, w), F32)],
        compiler_params=_params(("arbitrary",)),
    )(zc, zc, o, conv_w, og, gmat_a, gmat_b, w_out, x, gate)


def _group_norm_bwd(dyn, y, og, gmat):
    rs = lax.rsqrt(_group_mean(y * y, gmat) + EPS)
    yhat = y * rs
    d_og = jnp.sum(dyn * yhat, axis=0, keepdims=True)
    dyh = dyn * og
    return rs * (dyh - yhat * _group_mean(dyh * yhat, gmat)), d_og


def mix_out_backward(dxo, y, gate, ya, o, og, gmat_a, gmat_b, w_out):
    t, d = dxo.shape
    tm = _tile(t, ROW_TILE, 16)
    w = CONV_WIDTH

    def body(dxo_ref, y_ref, gate_ref, ya_ref, o_ref, og_ref, ga_ref, gb_ref, w_ref,
             dy_ref, dya_ref, do_ref, delta_ref, sd_ref, so_ref):
        @pl.when(pl.program_id(0) == 0)
        def _():
            sd_ref[...] = jnp.zeros_like(sd_ref)
            so_ref[...] = jnp.zeros_like(so_ref)

        dxo_v = dxo_ref[...]
        dy = (gate_ref[...] * dxo_v).astype(BF16)
        dy_ref[...] = dy
        sd_ref[0:1, :] += jnp.sum(dxo_v * y_ref[...].astype(F32), axis=0, keepdims=True)
        dyn = _dot(dy, w_ref[...], NT)
        ogv = og_ref[...]
        ov = o_ref[...]
        dya, d_og_a = _group_norm_bwd(dyn[:, :w], ya_ref[...], ogv[:, :w], ga_ref[...])
        dov, d_og_b = _group_norm_bwd(dyn[:, w:], ov, ogv[:, w:], gb_ref[...])
        dya_ref[...] = dya
        do_ref[...] = dov.astype(BF16)
        so_ref[0:1, :w] += d_og_a
        so_ref[0:1, w:] += d_og_b
        prod = dov * ov
        for h in range(MLA_HEADS):
            delta_ref[h] = jnp.sum(prod[:, h * V_HEAD:(h + 1) * V_HEAD], axis=-1, keepdims=True)

    def rows(n):
        return pl.BlockSpec((tm, n), lambda i: (i, 0))

    return pl.pallas_call(
        body, name="mix_out_bwd", grid=(t // tm,),
        in_specs=[rows(d), rows(d), _row(gate), rows(w), rows(MLA_WIDTH), _row(og), _row(gmat_a), _row(gmat_b),
                  _row(w_out)],
        out_specs=[rows(d), rows(w), rows(MLA_WIDTH), pl.BlockSpec((MLA_HEADS, tm, 1), lambda i: (0, i, 0)),
                   pl.BlockSpec((8, d), lambda i: (0, 0)), pl.BlockSpec((8, MIX_WIDTH), lambda i: (0, 0))],
        out_shape=[jax.ShapeDtypeStruct((t, d), BF16), jax.ShapeDtypeStruct((t, w), F32),
                   jax.ShapeDtypeStruct((t, MLA_WIDTH), BF16), jax.ShapeDtypeStruct((MLA_HEADS, t, 1), F32),
                   jax.ShapeDtypeStruct((8, d), F32), jax.ShapeDtypeStruct((8, MIX_WIDTH), F32)],
        compiler_params=_params(("arbitrary",)),
    )(dxo, y, gate, ya, o, og, gmat_a, gmat_b, w_out)


def conv_backward(zc, dya, conv_w):
    t = zc.shape[0]
    tm = _tile(t, ROW_TILE, 16)
    nt = t // tm
    w = CONV_WIDTH

    def body(zc_ref, zp_ref, zn_ref, dya_ref, dn_ref, cw_ref, dzc_ref, sums_ref):
        i = pl.program_id(0)

        @pl.when(i == 0)
        def _():
            sums_ref[...] = jnp.zeros_like(sums_ref)

        zc_v = zc_ref[...]
        u, u1, u2 = _conv_taps(zc_v, zp_ref[...], i == 0)
        cw = cw_ref[...]
        dya_v = dya_ref[...]
        dyc = dya_v * zc_v[:, :w]
        dyc_next = jnp.where(i == nt - 1, 0.0, dn_ref[...] * zn_ref[...][:, :w])
        du = cw[2:3] * dyc + cw[1:2] * _advance_rows(dyc, dyc_next, 1) + cw[0:1] * _advance_rows(dyc, dyc_next, 2)
        dzc_ref[:, :w] = (dya_v * (cw[0:1] * u2 + cw[1:2] * u1 + cw[2:3] * u)).astype(BF16)
        dzc_ref[:, w:2 * w] = (du * zc_v[:, 2 * w:]).astype(BF16)
        dzc_ref[:, 2 * w:] = (du * zc_v[:, w:2 * w]).astype(BF16)
        _add_rows(sums_ref, [jnp.sum(dyc * tap, axis=0, keepdims=True) for tap in (u2, u1, u)])

    def rows(n):
        return pl.BlockSpec((tm, n), lambda i: (i, 0))

    def halo(n, step):
        last = t // 8 - 1
        return pl.BlockSpec((8, n), lambda i: (jnp.clip(i * (tm // 8) + step, 0, last), 0))

    return pl.pallas_call(
        body, name="conv_bwd", grid=(nt,),
        in_specs=[rows(ZC_COLS), halo(ZC_COLS, -1), halo(ZC_COLS, tm // 8), rows(w), halo(w, tm // 8), _row(conv_w)],
        out_specs=[rows(ZC_COLS), pl.BlockSpec((8, w), lambda i: (0, 0))],
        out_shape=[jax.ShapeDtypeStruct((t, ZC_COLS), BF16), jax.ShapeDtypeStruct((8, w), F32)],
        compiler_params=_params(("arbitrary",)),
    )(zc, zc, zc, dya, dya, conv_w)


def _rms_bwd(dy, x, g):
    xhat, r = _rms(x)
    d_g = jnp.sum(dy * xhat, axis=0, keepdims=True)
    dxh = dy * g
    return r * (dxh - xhat * jnp.mean(dxh * xhat, axis=-1, keepdims=True)), d_g


def mla_project_backward(dq, dk, dv, zm, pos, inv_freq, qg, kvg, w_uq, w_ukv):
    t = zm.shape[0]
    tm = _tile(t, ROW_TILE, 16)

    def body(dq_ref, dk_ref, dv_ref, zm_ref, pos_ref, if_ref, qg_ref, kvg_ref, wq_ref, wkv_ref,
             dql_ref, dkvl_ref, dzm_ref, sums_ref):
        @pl.when(pl.program_id(0) == 0)
        def _():
            sums_ref[...] = jnp.zeros_like(sums_ref)

        tables = _rope_tables(pos_ref[...], if_ref[...])
        dkr = jnp.zeros((tm, LANES), F32)
        for h in range(MLA_HEADS):
            lo = h * HEAD_PAD
            dql_ref[:, lo:lo + QK_NOPE] = dq_ref[:, lo:lo + QK_NOPE].astype(BF16)
            dql_ref[:, lo + QK_NOPE:lo + HEAD_PAD] = _rope_transposed(
                dq_ref[:, lo + QK_NOPE:lo + HEAD_PAD], tables).astype(BF16)
            dkvl_ref[:, h * QK_NOPE:(h + 1) * QK_NOPE] = dk_ref[:, lo:lo + QK_NOPE].astype(BF16)
            dkr = dkr + dk_ref[:, lo + QK_NOPE:lo + HEAD_PAD]
        dkvl_ref[:, MLA_HEADS * QK_NOPE:] = dv_ref[...].astype(BF16)
        zv = zm_ref[...]
        dqn = _dot(dql_ref[...], wq_ref[...])
        dkvn = _dot(dkvl_ref[...], wkv_ref[...])
        dcq, d_qg = _rms_bwd(dqn, zv[:, :Q_LORA], qg_ref[...])
        dckv, d_kvg = _rms_bwd(dkvn, zv[:, Q_LORA:Q_LORA + KV_LORA], kvg_ref[...])
        dzm_ref[:, :Q_LORA] = dcq.astype(BF16)
        dzm_ref[:, Q_LORA:Q_LORA + KV_LORA] = dckv.astype(BF16)
        dzm_ref[:, Q_LORA + KV_LORA:] = _rope_transposed(dkr, tables).astype(BF16)
        sums_ref[0:1, :Q_LORA] += d_qg
        sums_ref[0:1, Q_LORA:Q_LORA + KV_LORA] += d_kvg

    def rows(n):
        return pl.BlockSpec((tm, n), lambda i: (i, 0))

    return pl.pallas_call(
        body, name="mla_project_bwd", grid=(t // tm,),
        in_specs=[rows(QK_COLS), rows(QK_COLS), rows(MLA_WIDTH), rows(ZM_COLS), rows(1), _row(inv_freq),
                  _row(qg), _row(kvg), _row(w_uq), _row(w_ukv)],
        out_specs=[rows(QK_COLS), rows(QK_COLS), rows(ZM_COLS), pl.BlockSpec((8, ZM_COLS), lambda i: (0, 0))],
        out_shape=[jax.ShapeDtypeStruct((t, QK_COLS), BF16), jax.ShapeDtypeStruct((t, QK_COLS), BF16),
                   jax.ShapeDtypeStruct((t, ZM_COLS), BF16), jax.ShapeDtypeStruct((8, ZM_COLS), F32)],
        compiler_params=_params(("arbitrary",)),
    )(dq, dk, dv, zm, pos, inv_freq, qg, kvg, w_uq, w_ukv)


def mix_in_backward(dzc, dzm, w_in, x, dxo, gn, sc, gate):
    t, d = x.shape
    tm = _tile(t, ROW_TILE, 16)

    def body(dzc_ref, dzm_ref, w_ref, x_ref, dxo_ref, gn_ref, sc_ref, gate_ref, dx_ref, dy_ref, sums_ref):
        @pl.when(pl.program_id(0) == 0)
        def _():
            sums_ref[...] = jnp.zeros_like(sums_ref)

        dh = _dot(dzc_ref[...], w_ref[:ZC_COLS, :]) + _dot(dzm_ref[...], w_ref[ZC_COLS:, :])
        dx, d_sh, d_sc, d_gn = _norm_mod_bwd(dh, x_ref[...], gn_ref[...], sc_ref[...])
        dx = dxo_ref[...] + dx
        dx_ref[...] = dx
        dy_ref[...] = (0.5 * gate_ref[...] * dx).astype(BF16)
        _add_rows(sums_ref, [d_sh, d_sc, d_gn])

    def rows(n):
        return pl.BlockSpec((tm, n), lambda i: (i, 0))

    return pl.pallas_call(
        body, name="mix_in_bwd", grid=(t // tm,),
        in_specs=[rows(ZC_COLS), rows(ZM_COLS), _row(w_in), rows(d), rows(d), _row(gn), _row(sc), _row(gate)],
        out_specs=[rows(d), rows(d), pl.BlockSpec((8, d), lambda i: (0, 0))],
        out_shape=[jax.ShapeDtypeStruct((t, d), F32), jax.ShapeDtypeStruct((t, d), BF16),
                   jax.ShapeDtypeStruct((8, d), F32)],
        compiler_params=_params(("arbitrary",)),
    )(dzc, dzm, w_in, x, dxo, gn, sc, gate)


def final_loss(x, target, g, gate):
    t, d = x.shape
    tm = _tile(t, ROW_TILE, 16)

    def body(x_ref, t_ref, g_ref, gate_ref, dx_ref, dy_ref, sums_ref):
        @pl.when(pl.program_id(0) == 0)
        def _():
            sums_ref[...] = jnp.zeros_like(sums_ref)

        gv = g_ref[...]
        xhat, r = _rms(x_ref[...])
        err = xhat * gv - t_ref[...]
        dyf = err * (1.0 / d)
        dxh = dyf * gv
        dx = r * (dxh - xhat * jnp.mean(dxh * xhat, axis=-1, keepdims=True))
        dx_ref[...] = dx
        dy_ref[...] = (0.5 * gate_ref[...] * dx).astype(BF16)
        _add_rows(sums_ref, [jnp.sum(dyf * xhat, axis=0, keepdims=True),
                             jnp.sum(err * err, axis=0, keepdims=True) * (0.5 / d)])

    row = pl.BlockSpec((tm, d), lambda i: (i, 0))
    return pl.pallas_call(
        body, name="final_loss", grid=(t // tm,),
        in_specs=[row, row, _row(g), _row(gate)],
        out_specs=[row, row, pl.BlockSpec((8, d), lambda i: (0, 0))],
        out_shape=[jax.ShapeDtypeStruct((t, d), F32), jax.ShapeDtypeStruct((t, d), BF16),
                   jax.ShapeDtypeStruct((8, d), F32)],
        compiler_params=_params(("arbitrary",)),
    )(x, target, g, gate)


def adamw(w, g, m, v, name):
    r, n = w.shape
    tr = _tile(r, max(8, (1 << 19) // n), 8)

    def body(w_ref, g_ref, m_ref, v_ref, d_ref, mo_ref, vo_ref):
        gv = g_ref[...]
        m_new = ADAM_B1 * m_ref[...] + (1.0 - ADAM_B1) * gv
        v_new = ADAM_B2 * v_ref[...] + (1.0 - ADAM_B2) * (gv * gv)
        m_hat = m_new / (1.0 - ADAM_B1 ** ADAM_STEP)
        v_hat = v_new / (1.0 - ADAM_B2 ** ADAM_STEP)
        d_ref[...] = -ADAM_LR * (m_hat / (jnp.sqrt(v_hat) + ADAM_EPS) + ADAM_WD * w_ref[...])
        mo_ref[...] = m_new
        vo_ref[...] = v_new

    blk = pl.BlockSpec((tr, n), lambda i: (i, 0))
    shape = jax.ShapeDtypeStruct((r, n), F32)
    return pl.pallas_call(
        body, name=name, grid=(r // tr,), in_specs=[blk] * 4, out_specs=[blk] * 3, out_shape=[shape] * 3,
        compiler_params=_params(("arbitrary",)),
    )(w, g, m, v)


def _pad_to(v, n):
    return jnp.pad(v, (0, n - v.shape[0]))


def _pad_heads(w, axis_len):
    n = w.shape[1]
    return jnp.pad(w.reshape(MLA_HEADS, axis_len, n), ((0, 0), (0, HEAD_PAD - axis_len), (0, 0))).reshape(-1, n)


def _swap_head_parts(w, inner, outer):
    n = w.shape[1]
    return w.reshape(outer, inner, QK_NOPE, n).transpose(1, 0, 2, 3).reshape(-1, n)


def kernel(x, c, positions, ada_w, ada_b, norm_ffn1_g, ffn1_w1, ffn1_w3, ffn1_w2, norm_mix_g, w_in, conv_w, q_norm_g, w_uq, kv_norm_g, w_ukv, out_norm_g, w_out, norm_ffn2_g, ffn2_w1, ffn2_w3, ffn2_w2, final_norm_g, loss_target, m_ada_w, m_ada_b, m_norm_ffn1_g, m_ffn1_w1, m_ffn1_w3, m_ffn1_w2, m_norm_mix_g, m_w_in, m_conv_w, m_q_norm_g, m_w_uq, m_kv_norm_g, m_w_ukv, m_out_norm_g, m_w_out, m_norm_ffn2_g, m_ffn2_w1, m_ffn2_w3, m_ffn2_w2, m_final_norm_g, v_ada_w, v_ada_b, v_norm_ffn1_g, v_ffn1_w1, v_ffn1_w3, v_ffn1_w2, v_norm_mix_g, v_w_in, v_conv_w, v_q_norm_g, v_w_uq, v_kv_norm_g, v_w_ukv, v_out_norm_g, v_w_out, v_norm_ffn2_g, v_ffn2_w1, v_ffn2_w3, v_ffn2_w2, v_final_norm_g):
    t, d = x.shape[1], x.shape[2]
    f = ffn1_w2.shape[1] * N_DEV
    me = 4 * lax.axis_index("x") + 2 * lax.axis_index("y") + lax.axis_index("c")
    my_c = lax.axis_index("c")
    my_chip = 2 * lax.axis_index("x") + lax.axis_index("y")
    xs = x[0]
    n_ada = ada_w.shape[2]
    cw_n = conv_w.shape[2]

    c_rows = jnp.broadcast_to(c, (8, d))
    conv_rows = jnp.pad(conv_w[0], ((0, 8 - CONV_K), (0, LANES - cw_n)))
    ffn1_blocks = jnp.stack([ffn1_w1[0].T, ffn1_w3[0].T, ffn1_w2[0]]).astype(BF16)
    ffn2_blocks = jnp.stack([ffn2_w1[0].T, ffn2_w3[0].T, ffn2_w2[0]]).astype(BF16)
    c_all, conv_all, ffn1_all = all_gather([c_rows, conv_rows, ffn1_blocks], [0, 0, 1], "gather_first")
    c_all = c_all[:, 0, :]
    conv_full8 = conv_all[:, :, :cw_n].transpose(1, 0, 2).reshape(8, CONV_WIDTH)
    ffn1_ws = ffn1_all.reshape(3, f, d)
    gather_rest = riding_gather(
        [ffn2_blocks, w_in[0].T.astype(BF16), w_uq[0].T.astype(BF16), w_ukv[0].T.astype(BF16), w_out[0].astype(BF16)],
        [1, 0, 0, 0, 0])

    ada_b_cols = lax.dynamic_slice_in_dim(ada_b, me * n_ada, n_ada, axis=1)
    mod_cols = ada_forward(c_all, ada_w[0], ada_b_cols)
    mod_all, = all_gather([mod_cols], [0], "gather_mod")
    mod = lax.dynamic_index_in_dim(mod_all, me, axis=1, keepdims=False).reshape(N_MOD, 1, d)
    sh1, sc1, g1, sh2, sc2, g2, sh3, sc3, g3 = [mod[i] for i in range(N_MOD)]

    gf = final_norm_g.reshape(1, d)
    x1, h1, a1, b1, y1, *gathered = ffn_forward(xs, norm_ffn1_g, sc1, sh1, g1, ffn1_ws, 0, "ffn1_fwd", gather_rest)
    ffn2_ws = gathered[0].reshape(3, f, d)
    w_in_p = jnp.pad(gathered[1].reshape(IN_COLS, d), ((0, ZC_COLS + ZM_COLS - IN_COLS), (0, 0)))
    w_uq_p = _pad_heads(gathered[2].reshape(-1, Q_LORA), QK_NOPE + QK_ROPE)
    w_ukv_p = _swap_head_parts(gathered[3].reshape(-1, KV_LORA), 2, MLA_HEADS)
    w_out_f = gathered[4].reshape(MIX_WIDTH, d)
    h2, zc, zm = mix_in_forward(x1, norm_mix_g, sc2, sh2, w_in_p)
    pos = positions[0].astype(F32).reshape(t, 1)
    inv_freq = ROPE_THETA ** (-jnp.arange(0, QK_ROPE, 2, dtype=F32) / QK_ROPE)
    inv_freq = jnp.concatenate([inv_freq, inv_freq, jnp.zeros((LANES - QK_ROPE,), F32)]).reshape(1, LANES)
    qn, kvn, q, k, v = mla_project(zm, pos, inv_freq, q_norm_g, kv_norm_g, w_uq_p, w_ukv_p)
    o, lse = attention_forward(q, k, v)
    lane = jnp.arange(CONV_WIDTH)
    gmat_a = (lane[:, None] // (CONV_WIDTH // CONV_GROUPS) == lane[None, :] // (CONV_WIDTH // CONV_GROUPS))
    gmat_a = (gmat_a / (CONV_WIDTH // CONV_GROUPS)).astype(BF16)
    gmat_b = ((lane[:, None] // V_HEAD == lane[None, :] // V_HEAD) / V_HEAD).astype(BF16)
    x2, yn, y2, ya = mix_out_forward(zc, o, conv_full8, out_norm_g, gmat_a, gmat_b, w_out_f, x1, g2)
    x3, h3, a3, b3, y3 = ffn_forward(x2, norm_ffn2_g, sc3, sh3, g3, ffn2_ws, 0, "ffn2_fwd")
    dx3, dy3, sums_f = final_loss(x3, loss_target[0], gf, g3)

    chip_idx = jnp.bitwise_xor(my_chip, jnp.array([0, 2, 1, 3], jnp.int32)).astype(jnp.int32)
    src_idx = (2 * chip_idx + my_c).astype(jnp.int32)

    def chip_sums(tag, named):
        g8 = [g.reshape(N_DEV, g.shape[0] // N_DEV, g.shape[1]) for _, g in named]
        got = exchange_sibling(g8, "rs_sibling_" + tag)
        return [add_sibling(g, r, src_idx, chip_idx, "rs_add_" + n) for g, r, (n, _) in zip(g8, got, named)]

    da3, db3, u3 = ffn_backward_gate(dy3, a3, b3, ffn2_ws, 0, "ffn2_bwd_gate")
    dx2, sums_3 = ffn_backward_norm(da3, db3, dx3, x2, y3, norm_ffn2_g, sc3, ffn2_ws, 0, "ffn2_bwd_norm")
    ffn2_named = [("ffn2_w1", matmul_tn(da3, h3, "ffn2_gw1")), ("ffn2_w3", matmul_tn(db3, h3, "ffn2_gw3")),
                  ("ffn2_w2", matmul_tn(u3, dy3, "ffn2_gw2"))]
    ffn2_sums = chip_sums("ffn2", ffn2_named)
    dy2, dya, do, delta, sums_2d, sums_2o = mix_out_backward(dx2, y2, g2, ya, o, out_norm_g, gmat_a, gmat_b, w_out_f)
    g_w_out = matmul_tn(yn, dy2, "gw_out")
    nq = t // _tile(t, ATTN_TILE, CHUNK)
    stat_shape = (MLA_HEADS, nq, 1, t // nq)
    dq, dk, dv, *ffn2_got = attention_backward(q, k, v, do, lse.reshape(stat_shape), delta.reshape(stat_shape),
                                               riding_exchange([s[1] for s in ffn2_sums]))
    dzc, sums_c = conv_backward(zc, dya, conv_full8)
    dql, dkvl, dzm, sums_m = mla_project_backward(dq, dk, dv, zm, pos, inv_freq, q_norm_g, kv_norm_g, w_uq_p, w_ukv_p)
    g_w_uq_p = matmul_tn(dql, qn, "gw_uq")
    g_w_ukv_p = matmul_tn(dkvl, kvn, "gw_ukv")
    g_w_in = jnp.concatenate([matmul_tn(dzc, h2, "gw_in_conv"), matmul_tn(dzm, h2, "gw_in_mla")])[:IN_COLS]
    g_w_uq = g_w_uq_p.reshape(MLA_HEADS, HEAD_PAD, Q_LORA)[:, :QK_NOPE + QK_ROPE].reshape(-1, Q_LORA)
    g_w_ukv = _swap_head_parts(g_w_ukv_p, MLA_HEADS, 2)
    mix_named = [("w_in", g_w_in), ("w_uq", g_w_uq), ("w_ukv", g_w_ukv), ("w_out", g_w_out)]
    mix_sums = chip_sums("mix", mix_named)
    dx1, dy1, sums_1m = mix_in_backward(dzc, dzm, w_in_p, x1, dx2, norm_mix_g, sc2, g1)
    da1, db1, u1, *mix_got = ffn_backward_gate(dy1, a1, b1, ffn1_ws, 0, "ffn1_bwd_gate",
                                               riding_exchange([s[1] for s in mix_sums]))
    ffn1_named = [("ffn1_w1", matmul_tn(da1, h1, "ffn1_gw1")), ("ffn1_w3", matmul_tn(db1, h1, "ffn1_gw3")),
                  ("ffn1_w2", matmul_tn(u1, dy1, "ffn1_gw2"))]
    ffn1_sums = chip_sums("ffn1", ffn1_named)
    dx0, sums_1, *ffn1_got = ffn_backward_norm(da1, db1, dx1, xs, y1, norm_ffn1_g, sc1, ffn1_ws, 0, "ffn1_bwd_norm",
                                               riding_exchange([s[1] for s in ffn1_sums]))
    transposed = {"ffn1_w1", "ffn1_w3", "ffn2_w1", "ffn2_w3", "w_in", "w_uq", "w_ukv"}
    g_sh = {}
    for named, group_sums, group_got in ((ffn2_named, ffn2_sums, ffn2_got), (mix_named, mix_sums, mix_got),
                                         (ffn1_named, ffn1_sums, ffn1_got)):
        for (n, _), (own, _), got in zip(named, group_sums, group_got):
            g_rows = add_received(own, got, "rs_sum_" + n)
            g_sh[n] = g_rows.T if n in transposed else g_rows

    dmod = jnp.concatenate([sums_1[0], sums_1[1], sums_1[2], sums_1m[0], sums_1m[1], sums_2d[0],
                            sums_3[0], sums_3[1], sums_3[2]])
    pieces = [dmod, sums_1[3], sums_1m[2], sums_m[0, :Q_LORA], sums_m[0, Q_LORA:Q_LORA + KV_LORA], sums_2o[0],
              sums_3[3], sums_f[0], sums_f[1], sums_c[:CONV_K].reshape(-1)]
    plens = [p.shape[0] for p in pieces]
    poffs = [sum(plens[:i]) for i in range(len(plens))]
    vec_len = -(-sum(plens) // 1024) * 1024
    vec = _pad_to(jnp.concatenate(pieces), vec_len).reshape(-1, LANES)
    vec_all, = all_gather([vec], [0], "gather_sums")
    tot = sum_devices(vec_all).reshape(-1)
    g_ada_b, g_n1, g_nmix, g_qg, g_kvg, g_og, g_n3, g_gf, loss_lanes, g_conv_full = [
        tot[o:o + n] for o, n in zip(poffs, plens)]
    loss = sum_lanes(loss_lanes.reshape(1, d))[0, 0]
    g_conv = lax.dynamic_slice_in_dim(g_conv_full.reshape(CONV_K, CONV_WIDTH), me * cw_n, cw_n, axis=1)
    dmod_all = vec_all.reshape(N_DEV, vec_len)[:, :N_MOD * d]
    dmod_cols = lax.dynamic_slice_in_dim(dmod_all, me * n_ada, n_ada, axis=1)
    g_ada_w = ada_backward(jnp.pad(c_all, ((0, 8), (0, 0))), jnp.pad(dmod_cols, ((0, 8), (0, 0))))

    def update(name, w, g, m, v):
        shape = w.shape
        two_d = (-1, shape[-1])
        dlt, nm, nv = adamw(w.reshape(two_d), g.reshape(two_d), m.reshape(two_d), v.reshape(two_d), "adamw_" + name)
        return g.reshape(shape), dlt.reshape(shape), nm.reshape(shape), nv.reshape(shape)

    res = {}
    res["ada_w"] = update("ada_w", ada_w, g_ada_w, m_ada_w, v_ada_w)
    big = [("ffn1_w1", ffn1_w1, m_ffn1_w1, v_ffn1_w1), ("ffn1_w3", ffn1_w3, m_ffn1_w3, v_ffn1_w3),
           ("ffn2_w1", ffn2_w1, m_ffn2_w1, v_ffn2_w1), ("ffn2_w3", ffn2_w3, m_ffn2_w3, v_ffn2_w3),
           ("w_in", w_in, m_w_in, v_w_in), ("w_uq", w_uq, m_w_uq, v_w_uq), ("w_ukv", w_ukv, m_w_ukv, v_w_ukv),
           ("ffn1_w2", ffn1_w2, m_ffn1_w2, v_ffn1_w2), ("ffn2_w2", ffn2_w2, m_ffn2_w2, v_ffn2_w2),
           ("w_out", w_out, m_w_out, v_w_out)]
    for name, w, m, v in big:
        res[name] = update(name, w, g_sh[name], m, v)
    smalls = [("ada_b", ada_b, g_ada_b, m_ada_b, v_ada_b),
              ("norm_ffn1_g", norm_ffn1_g, g_n1, m_norm_ffn1_g, v_norm_ffn1_g),
              ("norm_mix_g", norm_mix_g, g_nmix, m_norm_mix_g, v_norm_mix_g),
              ("conv_w", conv_w, g_conv, m_conv_w, v_conv_w),
              ("q_norm_g", q_norm_g, g_qg, m_q_norm_g, v_q_norm_g),
              ("kv_norm_g", kv_norm_g, g_kvg, m_kv_norm_g, v_kv_norm_g),
              ("out_norm_g", out_norm_g, g_og, m_out_norm_g, v_out_norm_g),
              ("norm_ffn2_g", norm_ffn2_g, g_n3, m_norm_ffn2_g, v_norm_ffn2_g),
              ("final_norm_g", final_norm_g, g_gf, m_final_norm_g, v_final_norm_g)]
    slens = [w.size for _, w, _, _, _ in smalls]
    soffs = [sum(slens[:i]) for i in range(len(slens))]
    s_len = -(-sum(slens) // 1024) * 1024

    def pack_small(i):
        return _pad_to(jnp.concatenate([s[i].reshape(-1) for s in smalls]), s_len).reshape(8, -1)

    s_out = adamw(pack_small(1), pack_small(2), pack_small(3), pack_small(4), "adamw_small")
    for (name, w, g, _, _), o, n in zip(smalls, soffs, slens):
        res[name] = (g.reshape(w.shape),) + tuple(a.reshape(-1)[o:o + n].reshape(w.shape) for a in s_out)

    order = ["ada_w", "ada_b", "norm_ffn1_g", "ffn1_w1", "ffn1_w3", "ffn1_w2", "norm_mix_g", "w_in", "conv_w",
             "q_norm_g", "w_uq", "kv_norm_g", "w_ukv", "out_norm_g", "w_out", "norm_ffn2_g", "ffn2_w1", "ffn2_w3",
             "ffn2_w2", "final_norm_g"]
    return (loss, dx0.reshape(x.shape), *[res[n][0] for n in order], *[res[n][1] for n in order],
            *[res[n][2] for n in order], *[res[n][3] for n in order])
```

```python
import functools

import jax
import jax.numpy as jnp
from jax import lax
from jax.experimental import pallas as pl
from jax.experimental.pallas import tpu as pltpu

F32 = jnp.float32
BF16 = jnp.bfloat16
MESH_ID = pl.DeviceIdType.MESH
N_DEV = 8

EPS = 1e-6
CHUNK = 64
N_MOD = 9
CONV_WIDTH = 512
CONV_GROUPS = 8
CONV_K = 3
MLA_HEADS = 4
QK_NOPE = 128
QK_ROPE = 64
V_HEAD = 128
Q_LORA = 384
KV_LORA = 256
ROPE_THETA = 10000.0
MLA_WIDTH = MLA_HEADS * V_HEAD
MIX_WIDTH = CONV_WIDTH + MLA_WIDTH
IN_COLS = 3 * CONV_WIDTH + Q_LORA + KV_LORA + QK_ROPE
ZC_COLS = 3 * CONV_WIDTH
ZM_COLS = Q_LORA + KV_LORA + 128
HEAD_PAD = 256
QK_COLS = MLA_HEADS * HEAD_PAD
ATTN_SCALE = (QK_NOPE + QK_ROPE) ** -0.5
LOG2_E = 1.4426950408889634
LN_2 = 0.6931471805599453
QK_FOLD = ATTN_SCALE * LOG2_E
NEG_INF = -1e30

ADAM_LR = 0.001
ADAM_B1 = 0.9
ADAM_B2 = 0.999
ADAM_EPS = 1e-08
ADAM_WD = 0.01
ADAM_STEP = 10

LANES = 128
MXU_COLS = 256
VMEM_LIMIT = 56 * 1024 * 1024
ROW_TILE = 512
FFN_FWD_TILE = (1024, 256)
FFN_BWD_TILE = (512, 1408)
GRAD_TILE = 1408
ATTN_TILE = 512

NN = (((1,), (0,)), ((), ()))
NT = (((1,), (1,)), ((), ()))
TN = (((0,), (0,)), ((), ()))


def _dot(a, b, dims=NN):
    return lax.dot_general(a, b, dims, preferred_element_type=F32)


def _tile(n, cap, mult=LANES):
    best = None
    for t in range(mult, min(n, cap) + 1, mult):
        if n % t == 0:
            best = t
    return n if best is None else best


def _params(sem=None):
    return pltpu.CompilerParams(dimension_semantics=sem, vmem_limit_bytes=VMEM_LIMIT)


def _row(v):
    return pl.BlockSpec(v.shape, lambda *_: (0,) * v.ndim)


def _sigmoid(x):
    return 0.5 * jnp.tanh(0.5 * x) + 0.5


def _rms(x):
    r = lax.rsqrt(jnp.mean(x * x, axis=-1, keepdims=True) + EPS)
    return x * r, r


def _norm_mod_bwd(dh, x, gn, sc):
    xhat, r = _rms(x)
    d_sh = jnp.sum(dh, axis=0, keepdims=True)
    d_sc = jnp.sum(dh * (xhat * gn), axis=0, keepdims=True)
    dxn = dh * (1.0 + sc)
    d_gn = jnp.sum(dxn * xhat, axis=0, keepdims=True)
    dxh = dxn * gn
    dx = r * (dxh - xhat * jnp.mean(dxh * xhat, axis=-1, keepdims=True))
    return dx, d_sh, d_sc, d_gn


def _group_mean(v, gmat):
    hi = v.astype(BF16)
    lo = (v - hi.astype(F32)).astype(BF16)
    return _dot(hi, gmat) + _dot(lo, gmat)


def _add_rows(ref, rows):
    for r, v in enumerate(rows):
        ref[r:r + 1, :] += v


def _window(ref, axis, j):
    return ref.at[(slice(None),) * axis + (j,)]


def _any_specs(n):
    return [pl.BlockSpec(memory_space=pl.ANY)] * n


def all_gather(blocks, axes, name):
    n_arr = len(blocks)

    def body(*refs):
        start, forward, finish = _gather_steps(refs[:n_arr], refs[n_arr:2 * n_arr], axes, *refs[2 * n_arr:])
        start()
        for j in range(3):
            forward(j)
        finish()

    return pl.pallas_call(
        body, name=name, out_shape=_gathered_shapes(blocks, axes),
        in_specs=_any_specs(n_arr), out_specs=_any_specs(n_arr), scratch_shapes=_gather_sems(n_arr),
    )(*blocks)


def _gathered_shapes(blocks, axes):
    return [jax.ShapeDtypeStruct(b.shape[:ax] + (N_DEV,) + b.shape[ax:], b.dtype) for b, ax in zip(blocks, axes)]


def _gather_sems(n_arr):
    return [pltpu.SemaphoreType.DMA((7, n_arr)), pltpu.SemaphoreType.DMA((7, n_arr)), pltpu.SemaphoreType.DMA((n_arr,))]


def _gather_steps(ins, outs, axes, send_sems, recv_sems, local_sems):
    arrays = range(len(ins))
    x, y, c = lax.axis_index("x"), lax.axis_index("y"), lax.axis_index("c")
    me, sibling = (x, y, c), (x, y, 1 - c)
    chips = [(1 - x, y), (x, 1 - y), (1 - x, 1 - y)]

    def slot(a, px, py, pc):
        return _window(outs[a], axes[a], 4 * px + 2 * py + pc)

    def copy(a, k, block, to, src=None):
        return pltpu.make_async_remote_copy(
            src_ref=slot(a, *block) if src is None else src, dst_ref=slot(a, *block),
            send_sem=send_sems.at[k, a], recv_sem=recv_sems.at[k, a], device_id=to, device_id_type=MESH_ID)

    def mine(a):
        return pltpu.make_async_copy(ins[a], slot(a, *me), local_sems.at[a])

    def first():
        return ([copy(a, 0, me, sibling, src=ins[a]) for a in arrays]
                + [copy(a, 1 + j, me, (*chip, c), src=ins[a]) for j, chip in enumerate(chips) for a in arrays])

    def passed(j):
        return [copy(a, 4 + j, (*chips[j], c), sibling) for a in arrays]

    def start():
        for a in arrays:
            mine(a).start()
        for cp in first():
            cp.start()

    def forward(j):
        for a, cp in zip(arrays, passed(j)):
            copy(a, 1 + j, (*chips[j], c), me).wait_recv()
            cp.start()

    def finish():
        for a in arrays:
            copy(a, 0, sibling, me).wait_recv()
        for j, chip in enumerate(chips):
            for a in arrays:
                copy(a, 4 + j, (*chip, 1 - c), me).wait_recv()
        for cp in first() + passed(0) + passed(1) + passed(2):
            cp.wait_send()
        for a in arrays:
            mine(a).wait()

    return start, forward, finish


def exchange_sibling(grads, name):
    n_arr = len(grads)

    def body(*refs):
        ins, outs = refs[:n_arr], refs[n_arr:2 * n_arr]
        send_sems, recv_sems = refs[2 * n_arr:]
        x, y, c = lax.axis_index("x"), lax.axis_index("y"), lax.axis_index("c")

        def copy(a, src, dst):
            return pltpu.make_async_remote_copy(
                src_ref=src, dst_ref=dst, send_sem=send_sems.at[a], recv_sem=recv_sems.at[a],
                device_id=(x, y, 1 - c), device_id_type=MESH_ID)

        for a in range(n_arr):
            for k in range(4):
                copy(a, ins[a].at[2 * k + (1 - c)], outs[a].at[k]).start()
        whole = [copy(a, ins[a].at[pl.ds(0, 4)], outs[a]) for a in range(n_arr)]
        for cp in whole:
            cp.wait_recv()
        for cp in whole:
            cp.wait_send()

    return pl.pallas_call(
        body, name=name,
        out_shape=[jax.ShapeDtypeStruct((4,) + g.shape[1:], g.dtype) for g in grads],
        in_specs=_any_specs(n_arr), out_specs=_any_specs(n_arr),
        scratch_shapes=[pltpu.SemaphoreType.DMA((n_arr,)), pltpu.SemaphoreType.DMA((n_arr,))],
    )(*grads)


def _exchange_sems(n_arr):
    return [pltpu.SemaphoreType.DMA((n_arr,)), pltpu.SemaphoreType.DMA((n_arr,))]


def _chip_exchange_steps(ins, outs, send_sems, recv_sems):
    x, y, c = lax.axis_index("x"), lax.axis_index("y"), lax.axis_index("c")
    chips = [(1 - x, y), (x, 1 - y), (1 - x, 1 - y)]

    def copy(a, src, dst, chip):
        return pltpu.make_async_remote_copy(
            src_ref=src, dst_ref=dst, send_sem=send_sems.at[a], recv_sem=recv_sems.at[a],
            device_id=(*chip, c), device_id_type=MESH_ID)

    def start():
        for a in range(len(ins)):
            for j, chip in enumerate(chips):
                copy(a, ins[a].at[j], outs[a].at[j], chip).start()

    def finish():
        whole = [copy(a, ins[a], outs[a], chips[0]) for a in range(len(ins))]
        for cp in whole:
            cp.wait_recv()
        for cp in whole:
            cp.wait_send()

    return start, finish


def riding_gather(blocks, axes):
    def phases(ins, outs, *sems):
        start, forward, finish = _gather_steps(ins, outs, axes, *sems)
        return [start] + [functools.partial(forward, j) for j in range(3)] + [finish]

    return dict(operands=blocks, out_shape=_gathered_shapes(blocks, axes), sems=_gather_sems(len(blocks)),
                phases=phases, when=("first", "late0", "late1", "late2", "last"))


def riding_exchange(parts):
    def phases(ins, outs, *sems):
        return list(_chip_exchange_steps(ins, outs, *sems))

    return dict(operands=parts, out_shape=[jax.ShapeDtypeStruct(p.shape, p.dtype) for p in parts],
                sems=_exchange_sems(len(parts)), phases=phases, when=("first", "last"))


def _call_with_rider(body, rider, *, name, grid, in_specs, out_specs, out_shape, scratch_shapes, operands):
    params = _params(("arbitrary",) * len(grid))
    if rider is None:
        return pl.pallas_call(body, name=name, grid=grid, in_specs=in_specs, out_specs=out_specs,
                              out_shape=out_shape, scratch_shapes=scratch_shapes, compiler_params=params)(*operands)
    n_in, n_out, n_scr, k = len(in_specs), len(out_specs), len(scratch_shapes), len(rider["operands"])
    rows, cols = grid
    assert cols >= 3 or "late0" not in rider["when"]
    late_row = max(rows - 2, 0)
    at = {"first": (0, 0), "last": (rows - 1, cols - 1),
          "late0": (late_row, 0), "late1": (late_row, 1), "late2": (late_row, 2)}

    def wrapped(*refs):
        ins, c_in = refs[:n_in], refs[n_in:n_in + k]
        outs, c_out = refs[n_in + k:n_in + k + n_out], refs[n_in + k + n_out:n_in + 2 * k + n_out]
        scratch, sems = refs[n_in + 2 * k + n_out:n_in + 2 * k + n_out + n_scr], refs[n_in + 2 * k + n_out + n_scr:]
        i, j = pl.program_id(0), pl.program_id(1)
        phases = rider["phases"](c_in, c_out, *sems)
        for fn, key in zip(phases, rider["when"]):
            if key != "last":
                pl.when(jnp.logical_and(i == at[key][0], j == at[key][1]))(fn)
        body(*ins, *outs, *scratch)
        pl.when(jnp.logical_and(i == at["last"][0], j == at["last"][1]))(phases[-1])

    return pl.pallas_call(
        wrapped, name=name, grid=grid,
        in_specs=list(in_specs) + _any_specs(k), out_specs=list(out_specs) + _any_specs(k),
        out_shape=list(out_shape) + rider["out_shape"], scratch_shapes=list(scratch_shapes) + rider["sems"],
        compiler_params=params)(*operands, *rider["operands"])


def add_sibling(g8, got, src_idx, chip_idx, name):
    _, r, n = g8.shape
    tr = _tile(r, 256, 16)

    def body(si_ref, ci_ref, g0_ref, g1_ref, g2_ref, g3_ref, got_ref, own_ref, send_ref):
        own_ref[...] = g0_ref[0] + got_ref[ci_ref[0]]
        for j, g_ref in enumerate((g1_ref, g2_ref, g3_ref)):
            send_ref[j] = (g_ref[0] + got_ref[ci_ref[j + 1]]).astype(BF16)

    def mine(j):
        return pl.BlockSpec((1, tr, n), lambda i, si, ci: (si[j], i, 0))

    return pl.pallas_call(
        body, name=name,
        out_shape=[jax.ShapeDtypeStruct((r, n), F32), jax.ShapeDtypeStruct((3, r, n), BF16)],
        grid_spec=pltpu.PrefetchScalarGridSpec(
            num_scalar_prefetch=2, grid=(r // tr,),
            in_specs=[mine(0), mine(1), mine(2), mine(3), pl.BlockSpec((4, tr, n), lambda i, si, ci: (0, i, 0))],
            out_specs=[pl.BlockSpec((tr, n), lambda i, si, ci: (i, 0)),
                       pl.BlockSpec((3, tr, n), lambda i, si, ci: (0, i, 0))]),
        compiler_params=_params(("arbitrary",)),
    )(src_idx, chip_idx, g8, g8, g8, g8, got)


def add_received(own, got, name):
    r, n = own.shape
    tr = _tile(r, 256, 16)

    def body(a_ref, b_ref, o_ref):
        acc = a_ref[...]
        for j in range(3):
            acc = acc + b_ref[j].astype(F32)
        o_ref[...] = acc

    return pl.pallas_call(
        body, name=name,
        out_shape=jax.ShapeDtypeStruct((r, n), F32),
        grid=(r // tr,),
        in_specs=[pl.BlockSpec((tr, n), lambda i: (i, 0)), pl.BlockSpec((3, tr, n), lambda i: (0, i, 0))],
        out_specs=pl.BlockSpec((tr, n), lambda i: (i, 0)),
        compiler_params=_params(("arbitrary",)),
    )(own, got)


def sum_devices(g):
    def body(g_ref, o_ref):
        acc = g_ref[0]
        for j in range(1, N_DEV):
            acc = acc + g_ref[j]
        o_ref[...] = acc

    return pl.pallas_call(body, name="sum_devices", out_shape=jax.ShapeDtypeStruct(g.shape[1:], F32))(g)


def sum_lanes(v):
    def body(v_ref, o_ref):
        o_ref[...] = jnp.broadcast_to(jnp.sum(v_ref[...], axis=-1, keepdims=True), (1, LANES))

    return pl.pallas_call(body, name="sum_lanes", out_shape=jax.ShapeDtypeStruct((1, LANES), F32))(v)


def ada_forward(c_all, ada_w, ada_b_cols):
    nb, n = c_all.shape[0], ada_w.shape[1]

    def body(c_ref, w_ref, b_ref, o_ref):
        cv = c_ref[...]
        s = (cv * jax.nn.sigmoid(cv)).astype(BF16)
        o_ref[...] = _dot(s, w_ref[...].astype(BF16)) + b_ref[...]

    return pl.pallas_call(body, name="ada_fwd", out_shape=jax.ShapeDtypeStruct((nb, n), F32),
                          compiler_params=_params())(c_all, ada_w, ada_b_cols)


def ada_backward(c_all16, dmod16):
    d, n = c_all16.shape[1], dmod16.shape[1]

    def body(c_ref, g_ref, o_ref):
        cv = c_ref[...]
        s = (cv * jax.nn.sigmoid(cv)).astype(BF16)
        o_ref[...] = _dot(s, g_ref[...].astype(BF16), TN)

    return pl.pallas_call(body, name="ada_bwd", out_shape=jax.ShapeDtypeStruct((d, n), F32),
                          compiler_params=_params())(c_all16, dmod16)


def ffn_forward(x, gn, sc, sh, gate, ws, first, name, rider=None):
    t, d = x.shape
    f = ws.shape[1]
    tm, tf = _tile(t, FFN_FWD_TILE[0], 16), _tile(f, FFN_FWD_TILE[1])
    nf = f // tf

    def body(x_ref, gn_ref, sc_ref, sh_ref, gate_ref, w1_ref, w3_ref, w2_ref,
             xo_ref, h_ref, a_ref, b_ref, y_ref, hs, acc):
        j = pl.program_id(1)

        @pl.when(j == 0)
        def _():
            xhat, _ = _rms(x_ref[...])
            h = (xhat * gn_ref[...] * (1.0 + sc_ref[...]) + sh_ref[...]).astype(BF16)
            hs[...] = h
            h_ref[...] = h
            acc[...] = jnp.zeros_like(acc)

        h = hs[...]
        a = _dot(h, w1_ref[...], NT)
        b = _dot(h, w3_ref[...], NT)
        a_ref[...] = a.astype(BF16)
        b_ref[...] = b.astype(BF16)
        u = (a * _sigmoid(a) * b).astype(BF16)
        acc[...] += _dot(u, w2_ref[...])

        @pl.when(j == nf - 1)
        def _():
            y = acc[...]
            y_ref[...] = y.astype(BF16)
            xo_ref[...] = x_ref[...] + 0.5 * gate_ref[...] * y

    row = pl.BlockSpec((tm, d), lambda i, j: (i, 0))
    vec = pl.BlockSpec((1, d), lambda i, j: (0, 0))
    wide = pl.BlockSpec((tm, tf), lambda i, j: (i, j))
    return _call_with_rider(
        body, rider, name=name, grid=(t // tm, nf),
        in_specs=[row, vec, vec, vec, vec] + _ffn_weight_specs(first, tf, d),
        out_specs=[row, row, wide, wide, row],
        out_shape=[jax.ShapeDtypeStruct((t, d), F32), jax.ShapeDtypeStruct((t, d), BF16),
                   jax.ShapeDtypeStruct((t, f), BF16), jax.ShapeDtypeStruct((t, f), BF16),
                   jax.ShapeDtypeStruct((t, d), BF16)],
        scratch_shapes=[pltpu.VMEM((tm, d), BF16), pltpu.VMEM((tm, d), F32)],
        operands=(x, gn, sc, sh, gate, ws, ws, ws))


def _ffn_weight_specs(first, tf, d):
    return [pl.BlockSpec((None, tf, d), lambda i, j, w=first + k: (w, j, 0)) for k in range(3)]


def ffn_backward_gate(dy, a, b, ws, first, name, rider=None):
    t, d = dy.shape
    f = ws.shape[1]
    tm, tf = _tile(t, FFN_BWD_TILE[0], 16), _tile(f, FFN_BWD_TILE[1])
    nf = f // tf

    def gate_body(dy_ref, a_ref, b_ref, w2_ref, da_ref, db_ref, u_ref):
        du = _dot(dy_ref[...], w2_ref[...], NT)
        av = a_ref[...].astype(F32)
        bv = b_ref[...].astype(F32)
        s = _sigmoid(av)
        sa = av * s
        da_ref[...] = (du * bv * (s + sa * (1.0 - s))).astype(BF16)
        db_ref[...] = (du * sa).astype(BF16)
        u_ref[...] = (sa * bv).astype(BF16)

    hidden = jax.ShapeDtypeStruct((t, f), BF16)
    wide_t = pl.BlockSpec((tm, tf), lambda j, i: (i, j))
    return _call_with_rider(
        gate_body, rider, name=name, grid=(nf, t // tm),
        in_specs=[pl.BlockSpec((tm, d), lambda j, i: (i, 0)), wide_t, wide_t,
                  pl.BlockSpec((None, tf, d), lambda j, i: (first + 2, j, 0))],
        out_specs=[wide_t, wide_t, wide_t], out_shape=[hidden, hidden, hidden],
        scratch_shapes=[], operands=(dy, a, b, ws))


def ffn_backward_norm(da, db, dxo, x, y, gn, sc, ws, first, name, rider=None):
    t, d = x.shape
    f = ws.shape[1]
    tm, tf = _tile(t, FFN_BWD_TILE[0], 16), _tile(f, FFN_BWD_TILE[1])
    nf = f // tf
    row = pl.BlockSpec((tm, d), lambda i, j: (i, 0))
    vec = pl.BlockSpec((1, d), lambda i, j: (0, 0))
    wide = pl.BlockSpec((tm, tf), lambda i, j: (i, j))

    def norm_body(da_ref, db_ref, w1_ref, w3_ref, dxo_ref, x_ref, y_ref, gn_ref, sc_ref, dx_ref, sums_ref, acc):
        i, j = pl.program_id(0), pl.program_id(1)

        @pl.when(jnp.logical_and(i == 0, j == 0))
        def _():
            sums_ref[...] = jnp.zeros_like(sums_ref)

        part = _dot(da_ref[...], w1_ref[...]) + _dot(db_ref[...], w3_ref[...])

        @pl.when(j == 0)
        def _():
            acc[...] = part

        @pl.when(jnp.logical_and(j > 0, j < nf - 1))
        def _():
            acc[...] += part

        @pl.when(j == nf - 1)
        def _():
            dh = part if nf == 1 else acc[...] + part
            dxo_v = dxo_ref[...]
            dx, d_sh, d_sc, d_gn = _norm_mod_bwd(dh, x_ref[...], gn_ref[...], sc_ref[...])
            dx_ref[...] = dxo_v + dx
            d_gate = jnp.sum(dxo_v * (0.5 * y_ref[...].astype(F32)), axis=0, keepdims=True)
            _add_rows(sums_ref, [d_sh, d_sc, d_gate, d_gn])

    w1_spec, w3_spec, _ = _ffn_weight_specs(first, tf, d)
    return _call_with_rider(
        norm_body, rider, name=name, grid=(t // tm, nf),
        in_specs=[wide, wide, w1_spec, w3_spec, row, row, row, vec, vec],
        out_specs=[row, pl.BlockSpec((8, d), lambda i, j: (0, 0))],
        out_shape=[jax.ShapeDtypeStruct((t, d), F32), jax.ShapeDtypeStruct((8, d), F32)],
        scratch_shapes=[pltpu.VMEM((tm, d), F32)],
        operands=(da, db, ws, ws, dxo, x, y, gn, sc))


def matmul_tn(a, b, name):
    t, m = a.shape
    n = b.shape[1]
    tm, tn, tk = _tile(m, GRAD_TILE), _tile(n, GRAD_TILE), _tile(t, 1024, 16)
    nk = t // tk

    def body(a_ref, b_ref, o_ref, acc):
        k = pl.program_id(2)

        @pl.when(k == 0)
        def _():
            acc[...] = jnp.zeros_like(acc)

        acc[...] += _dot(a_ref[...], b_ref[...], TN)

        @pl.when(k == nk - 1)
        def _():
            o_ref[...] = acc[...]

    return pl.pallas_call(
        body, name=name, grid=(m // tm, n // tn, nk),
        in_specs=[pl.BlockSpec((tk, tm), lambda i, j, k: (k, i)), pl.BlockSpec((tk, tn), lambda i, j, k: (k, j))],
        out_specs=pl.BlockSpec((tm, tn), lambda i, j, k: (i, j)),
        out_shape=jax.ShapeDtypeStruct((m, n), F32),
        scratch_shapes=[pltpu.VMEM((tm, tn), F32)],
        compiler_params=_params(("arbitrary", "arbitrary", "arbitrary")),
    )(a, b)


def mix_in_forward(x, gn, sc, sh, w_in):
    t, d = x.shape
    tm = _tile(t, ROW_TILE, 16)

    def body(x_ref, gn_ref, sc_ref, sh_ref, w_ref, h_ref, zc_ref, zm_ref):
        xhat, _ = _rms(x_ref[...])
        h = (xhat * gn_ref[...] * (1.0 + sc_ref[...]) + sh_ref[...]).astype(BF16)
        h_ref[...] = h
        z = _dot(h, w_ref[...], NT)
        zc_ref[...] = z[:, :ZC_COLS]
        zm_ref[...] = z[:, ZC_COLS:]

    row = pl.BlockSpec((tm, d), lambda i: (i, 0))
    vec = pl.BlockSpec((1, d), lambda i: (0, 0))
    return pl.pallas_call(
        body, name="mix_in_fwd", grid=(t // tm,),
        in_specs=[row, vec, vec, vec, _row(w_in)],
        out_specs=[row, pl.BlockSpec((tm, ZC_COLS), lambda i: (i, 0)), pl.BlockSpec((tm, ZM_COLS), lambda i: (i, 0))],
        out_shape=[jax.ShapeDtypeStruct((t, d), BF16), jax.ShapeDtypeStruct((t, ZC_COLS), F32),
                   jax.ShapeDtypeStruct((t, ZM_COLS), F32)],
        compiler_params=_params(("arbitrary",)),
    )(x, gn, sc, sh, w_in)


def _rope_tables(pos, inv_freq):
    ang = pos * inv_freq
    lane = lax.broadcasted_iota(jnp.int32, ang.shape, 1)
    cos, sin = jnp.cos(ang), jnp.sin(ang)
    half = QK_ROPE // 2
    return cos, jnp.where(lane < half, -sin, 0.0), jnp.where(jnp.logical_and(lane >= half, lane < QK_ROPE), sin, 0.0)


def _rope(v, tables):
    cos, sin_a, sin_b = tables
    return v * cos + pltpu.roll(v, LANES - QK_ROPE // 2, 1) * sin_a + pltpu.roll(v, QK_ROPE // 2, 1) * sin_b


def _rope_transposed(dv, tables):
    cos, sin_a, sin_b = tables
    return dv * cos + pltpu.roll(dv * sin_a, QK_ROPE // 2, 1) + pltpu.roll(dv * sin_b, LANES - QK_ROPE // 2, 1)


def mla_project(zm, pos, inv_freq, qg, kvg, w_uq, w_ukv):
    t = zm.shape[0]
    tm = _tile(t, ROW_TILE, 16)

    def body(zm_ref, pos_ref, if_ref, qg_ref, kvg_ref, wq_ref, wkv_ref, qn_ref, kvn_ref, q_ref, k_ref, v_ref):
        zv = zm_ref[...]
        qn = (_rms(zv[:, :Q_LORA])[0] * qg_ref[...]).astype(BF16)
        kvn = (_rms(zv[:, Q_LORA:Q_LORA + KV_LORA])[0] * kvg_ref[...]).astype(BF16)
        qn_ref[...] = qn
        kvn_ref[...] = kvn
        qf = _dot(qn, wq_ref[...], NT) * QK_FOLD
        kvf = _dot(kvn, wkv_ref[...], NT)
        tables = _rope_tables(pos_ref[...], if_ref[...])
        kr = _rope(zv[:, Q_LORA + KV_LORA:], tables).astype(BF16)
        for h in range(MLA_HEADS):
            lo = h * HEAD_PAD
            q_ref[:, lo:lo + QK_NOPE] = qf[:, lo:lo + QK_NOPE].astype(BF16)
            q_ref[:, lo + QK_NOPE:lo + HEAD_PAD] = _rope(qf[:, lo + QK_NOPE:lo + HEAD_PAD], tables).astype(BF16)
            k_ref[:, lo:lo + QK_NOPE] = kvf[:, h * QK_NOPE:(h + 1) * QK_NOPE].astype(BF16)
            k_ref[:, lo + QK_NOPE:lo + HEAD_PAD] = kr
        v_ref[...] = kvf[:, MLA_HEADS * QK_NOPE:].astype(BF16)

    def rows(n):
        return pl.BlockSpec((tm, n), lambda i: (i, 0))

    return pl.pallas_call(
        body, name="mla_project", grid=(t // tm,),
        in_specs=[rows(ZM_COLS), rows(1), _row(inv_freq), _row(qg), _row(kvg), _row(w_uq), _row(w_ukv)],
        out_specs=[rows(Q_LORA), rows(KV_LORA), rows(QK_COLS), rows(QK_COLS), rows(MLA_WIDTH)],
        out_shape=[jax.ShapeDtypeStruct((t, Q_LORA), BF16), jax.ShapeDtypeStruct((t, KV_LORA), BF16),
                   jax.ShapeDtypeStruct((t, QK_COLS), BF16), jax.ShapeDtypeStruct((t, QK_COLS), BF16),
                   jax.ShapeDtypeStruct((t, MLA_WIDTH), BF16)],
        compiler_params=_params(("arbitrary",)),
    )(zm, pos, inv_freq, qg, kvg, w_uq, w_ukv)


def _chunk_mask(shape, q_axis):
    qi = lax.broadcasted_iota(jnp.int32, shape, q_axis) // CHUNK
    ki = lax.broadcasted_iota(jnp.int32, shape, 1 - q_axis) // CHUNK
    return ki <= qi


def attention_forward(q, k, v):
    t = q.shape[0]
    tq = _tile(t, ATTN_TILE, CHUNK)

    def body(q_ref, k_ref, v_ref, o_ref, lse_ref):
        i = pl.program_id(1)
        qv = q_ref[...]

        def step(kb, carry, masked):
            m, l, acc = carry
            start = pl.multiple_of(kb * tq, tq)
            s = _dot(qv, k_ref[pl.ds(start, tq), :], NT)
            if masked:
                s = jnp.where(_chunk_mask(s.shape, 0), s, NEG_INF)
            m_new = jnp.maximum(m, jnp.max(s, axis=-1, keepdims=True))
            alpha = jnp.exp2(m - m_new)
            p = jnp.exp2(s - m_new)
            l = alpha * l + jnp.sum(p, axis=-1, keepdims=True)
            acc = alpha * acc + _dot(p.astype(BF16), v_ref[pl.ds(start, tq), :])
            return m_new, l, acc

        init = (jnp.full((tq, 1), NEG_INF, F32), jnp.zeros((tq, 1), F32), jnp.zeros((tq, V_HEAD), F32))
        carry = lax.fori_loop(0, i // 2, lambda pb, cr: step(2 * pb + 1, step(2 * pb, cr, False), False), init)
        carry = lax.fori_loop(0, i % 2, lambda _, cr: step(i - 1, cr, False), carry)
        m, l, acc = step(i, carry, True)
        o_ref[...] = acc / l
        lse_ref[0] = m + jnp.log2(l)

    return pl.pallas_call(
        body, name="attn_fwd", grid=(MLA_HEADS, t // tq),
        in_specs=[pl.BlockSpec((tq, HEAD_PAD), lambda h, i: (i, h)),
                  pl.BlockSpec((t, HEAD_PAD), lambda h, i: (0, h)),
                  pl.BlockSpec((t, V_HEAD), lambda h, i: (0, h))],
        out_specs=[pl.BlockSpec((tq, V_HEAD), lambda h, i: (i, h)),
                   pl.BlockSpec((1, tq, 1), lambda h, i: (h, i, 0))],
        out_shape=[jax.ShapeDtypeStruct((t, MLA_WIDTH), F32), jax.ShapeDtypeStruct((MLA_HEADS, t, 1), F32)],
        compiler_params=_params(("arbitrary", "arbitrary")),
    )(q, k, v)


def attention_backward(q, k, v, do, lse, delta, rider=None):
    t = q.shape[0]
    tq = _tile(t, ATTN_TILE, CHUNK)
    nq = t // tq

    def body(q_ref, k_ref, v_ref, do_ref, lse_ref, delta_ref, dq_ref, dk_ref, dv_ref):
        kb = pl.program_id(1)

        @pl.when(kb == 0)
        def _():
            dq_ref[...] = jnp.zeros_like(dq_ref)

        kv, vv = k_ref[...], v_ref[...]

        def step(qb, carry, masked):
            dk, dv = carry
            rows = pl.ds(pl.multiple_of(qb * tq, tq), tq)
            qv, dov = q_ref[rows, :], do_ref[rows, :]
            s = _dot(kv, qv, NT)
            if masked:
                s = jnp.where(_chunk_mask(s.shape, 1), s, NEG_INF)
            p = jnp.exp2(s - lse_ref[0, qb])
            dv = dv + _dot(p.astype(BF16), dov)
            dp = _dot(vv, dov, NT)
            ds = (p * (dp - delta_ref[0, qb]) * LN_2).astype(BF16)
            dk = dk + _dot(ds, qv)
            dq_ref[rows, :] += _dot(ds, kv, TN)
            return dk, dv

        carry = step(kb, (jnp.zeros((tq, HEAD_PAD), F32), jnp.zeros((tq, V_HEAD), F32)), True)
        odd = (nq - 1 - kb) % 2
        carry = lax.fori_loop(0, odd, lambda _, cr: step(kb + 1, cr, False), carry)
        first = kb + 1 + odd
        dk, dv = lax.fori_loop(0, (nq - first) // 2,
                               lambda pb, cr: step(first + 2 * pb + 1, step(first + 2 * pb, cr, False), False), carry)
        dk_ref[...] = dk
        dv_ref[...] = dv

    stat = pl.BlockSpec((1, nq, 1, tq), lambda h, j: (h, 0, 0, 0))
    return _call_with_rider(
        body, rider, name="attn_bwd", grid=(MLA_HEADS, nq),
        in_specs=[pl.BlockSpec((t, HEAD_PAD), lambda h, j: (0, h)),
                  pl.BlockSpec((tq, HEAD_PAD), lambda h, j: (j, h)),
                  pl.BlockSpec((tq, V_HEAD), lambda h, j: (j, h)),
                  pl.BlockSpec((t, V_HEAD), lambda h, j: (0, h)), stat, stat],
        out_specs=[pl.BlockSpec((t, HEAD_PAD), lambda h, j: (0, h)),
                   pl.BlockSpec((tq, HEAD_PAD), lambda h, j: (j, h)),
                   pl.BlockSpec((tq, V_HEAD), lambda h, j: (j, h))],
        out_shape=[jax.ShapeDtypeStruct((t, QK_COLS), F32), jax.ShapeDtypeStruct((t, QK_COLS), F32),
                   jax.ShapeDtypeStruct((t, MLA_WIDTH), F32)],
        scratch_shapes=[], operands=(q, k, v, do, lse, delta))


def _shift_rows(v, prev, n):
    out = pltpu.roll(v, n, 0)
    row = lax.broadcasted_iota(jnp.int32, v.shape, 0)
    for r in range(n):
        out = jnp.where(row == r, prev[8 - n + r:8 - n + r + 1, :], out)
    return out


def _advance_rows(v, nxt, n):
    rows = v.shape[0]
    out = pltpu.roll(v, rows - n, 0)
    row = lax.broadcasted_iota(jnp.int32, v.shape, 0)
    for r in range(n):
        out = jnp.where(row == rows - n + r, nxt[r:r + 1, :], out)
    return out


def _conv_taps(zc, zc_prev, first):
    w = CONV_WIDTH
    u = zc[:, w:2 * w] * zc[:, 2 * w:]
    up = jnp.where(first, 0.0, zc_prev[:, w:2 * w] * zc_prev[:, 2 * w:])
    return u, _shift_rows(u, up, 1), _shift_rows(u, up, 2)


def mix_out_forward(zc, o, conv_w, og, gmat_a, gmat_b, w_out, x, gate):
    t, d = x.shape
    tm = _tile(t, ROW_TILE, 16)
    w = CONV_WIDTH

    def body(zc_ref, zp_ref, o_ref, cw_ref, og_ref, ga_ref, gb_ref, w_ref, x_ref, gate_ref,
             xo_ref, yn_ref, y_ref, ya_ref):
        zc_v = zc_ref[...]
        u, u1, u2 = _conv_taps(zc_v, zp_ref[...], pl.program_id(0) == 0)
        cw = cw_ref[...]
        ya = zc_v[:, :w] * (cw[0:1] * u2 + cw[1:2] * u1 + cw[2:3] * u)
        ya_ref[...] = ya
        ov = o_ref[...]
        ogv = og_ref[...]
        yn_ref[:, :w] = (ya * lax.rsqrt(_group_mean(ya * ya, ga_ref[...]) + EPS) * ogv[:, :w]).astype(BF16)
        yn_ref[:, w:] = (ov * lax.rsqrt(_group_mean(ov * ov, gb_ref[...]) + EPS) * ogv[:, w:]).astype(BF16)
        y = _dot(yn_ref[...], w_ref[...])
        y_ref[...] = y.astype(BF16)
        xo_ref[...] = x_ref[...] + gate_ref[...] * y

    def rows(n):
        return pl.BlockSpec((tm, n), lambda i: (i, 0))

    prev = pl.BlockSpec((8, ZC_COLS), lambda i: (jnp.maximum(i * (tm // 8) - 1, 0), 0))
    return pl.pallas_call(
        body, name="mix_out_fwd", grid=(t // tm,),
        in_specs=[rows(ZC_COLS), prev, rows(MLA_WIDTH), _row(conv_w), _row(og), _row(gmat_a), _row(gmat_b),
                  _row(w_out), rows(d), _row(gate)],
        out_specs=[rows(d), rows(MIX_WIDTH), rows(d), rows(w)],
        out_shape=[jax.ShapeDtypeStruct((t, d), F32), jax.ShapeDtypeStruct((t, MIX_WIDTH), BF16),
                   jax.ShapeDtypeStruct((t, d), BF16), jax.ShapeDtypeStruct((t, w), F32)],
        compiler_params=_params(("arbitrary",)),
    )(zc, zc, o, conv_w, og, gmat_a, gmat_b, w_out, x, gate)


def _group_norm_bwd(dyn, y, og, gmat):
    rs = lax.rsqrt(_group_mean(y * y, gmat) + EPS)
    yhat = y * rs
    d_og = jnp.sum(dyn * yhat, axis=0, keepdims=True)
    dyh = dyn * og
    return rs * (dyh - yhat * _group_mean(dyh * yhat, gmat)), d_og


def mix_out_backward(dxo, y, gate, ya, o, og, gmat_a, gmat_b, w_out):
    t, d = dxo.shape
    tm = _tile(t, ROW_TILE, 16)
    w = CONV_WIDTH

    def body(dxo_ref, y_ref, gate_ref, ya_ref, o_ref, og_ref, ga_ref, gb_ref, w_ref,
             dy_ref, dya_ref, do_ref, delta_ref, sd_ref, so_ref):
        @pl.when(pl.program_id(0) == 0)
        def _():
            sd_ref[...] = jnp.zeros_like(sd_ref)
            so_ref[...] = jnp.zeros_like(so_ref)

        dxo_v = dxo_ref[...]
        dy = (gate_ref[...] * dxo_v).astype(BF16)
        dy_ref[...] = dy
        sd_ref[0:1, :] += jnp.sum(dxo_v * y_ref[...].astype(F32), axis=0, keepdims=True)
        dyn = _dot(dy, w_ref[...], NT)
        ogv = og_ref[...]
        ov = o_ref[...]
        dya, d_og_a = _group_norm_bwd(dyn[:, :w], ya_ref[...], ogv[:, :w], ga_ref[...])
        dov, d_og_b = _group_norm_bwd(dyn[:, w:], ov, ogv[:, w:], gb_ref[...])
        dya_ref[...] = dya
        do_ref[...] = dov.astype(BF16)
        so_ref[0:1, :w] += d_og_a
        so_ref[0:1, w:] += d_og_b
        prod = dov * ov
        for h in range(MLA_HEADS):
            delta_ref[h] = jnp.sum(prod[:, h * V_HEAD:(h + 1) * V_HEAD], axis=-1, keepdims=True)

    def rows(n):
        return pl.BlockSpec((tm, n), lambda i: (i, 0))

    return pl.pallas_call(
        body, name="mix_out_bwd", grid=(t // tm,),
        in_specs=[rows(d), rows(d), _row(gate), rows(w), rows(MLA_WIDTH), _row(og), _row(gmat_a), _row(gmat_b),
                  _row(w_out)],
        out_specs=[rows(d), rows(w), rows(MLA_WIDTH), pl.BlockSpec((MLA_HEADS, tm, 1), lambda i: (0, i, 0)),
                   pl.BlockSpec((8, d), lambda i: (0, 0)), pl.BlockSpec((8, MIX_WIDTH), lambda i: (0, 0))],
        out_shape=[jax.ShapeDtypeStruct((t, d), BF16), jax.ShapeDtypeStruct((t, w), F32),
                   jax.ShapeDtypeStruct((t, MLA_WIDTH), BF16), jax.ShapeDtypeStruct((MLA_HEADS, t, 1), F32),
                   jax.ShapeDtypeStruct((8, d), F32), jax.ShapeDtypeStruct((8, MIX_WIDTH), F32)],
        compiler_params=_params(("arbitrary",)),
    )(dxo, y, gate, ya, o, og, gmat_a, gmat_b, w_out)


def conv_backward(zc, dya, conv_w):
    t = zc.shape[0]
    tm = _tile(t, ROW_TILE, 16)
    nt = t // tm
    w = CONV_WIDTH

    def body(zc_ref, zp_ref, zn_ref, dya_ref, dn_ref, cw_ref, dzc_ref, sums_ref):
        i = pl.program_id(0)

        @pl.when(i == 0)
        def _():
            sums_ref[...] = jnp.zeros_like(sums_ref)

        zc_v = zc_ref[...]
        u, u1, u2 = _conv_taps(zc_v, zp_ref[...], i == 0)
        cw = cw_ref[...]
        dya_v = dya_ref[...]
        dyc = dya_v * zc_v[:, :w]
        dyc_next = jnp.where(i == nt - 1, 0.0, dn_ref[...] * zn_ref[...][:, :w])
        du = cw[2:3] * dyc + cw[1:2] * _advance_rows(dyc, dyc_next, 1) + cw[0:1] * _advance_rows(dyc, dyc_next, 2)
        dzc_ref[:, :w] = (dya_v * (cw[0:1] * u2 + cw[1:2] * u1 + cw[2:3] * u)).astype(BF16)
        dzc_ref[:, w:2 * w] = (du * zc_v[:, 2 * w:]).astype(BF16)
        dzc_ref[:, 2 * w:] = (du * zc_v[:, w:2 * w]).astype(BF16)
        _add_rows(sums_ref, [jnp.sum(dyc * tap, axis=0, keepdims=True) for tap in (u2, u1, u)])

    def rows(n):
        return pl.BlockSpec((tm, n), lambda i: (i, 0))

    def halo(n, step):
        last = t // 8 - 1
        return pl.BlockSpec((8, n), lambda i: (jnp.clip(i * (tm // 8) + step, 0, last), 0))

    return pl.pallas_call(
        body, name="conv_bwd", grid=(nt,),
        in_specs=[rows(ZC_COLS), halo(ZC_COLS, -1), halo(ZC_COLS, tm // 8), rows(w), halo(w, tm // 8), _row(conv_w)],
        out_specs=[rows(ZC_COLS), pl.BlockSpec((8, w), lambda i: (0, 0))],
        out_shape=[jax.ShapeDtypeStruct((t, ZC_COLS), BF16), jax.ShapeDtypeStruct((8, w), F32)],
        compiler_params=_params(("arbitrary",)),
    )(zc, zc, zc, dya, dya, conv_w)


def _rms_bwd(dy, x, g):
    xhat, r = _rms(x)
    d_g = jnp.sum(dy * xhat, axis=0, keepdims=True)
    dxh = dy * g
    return r * (dxh - xhat * jnp.mean(dxh * xhat, axis=-1, keepdims=True)), d_g


def mla_project_backward(dq, dk, dv, zm, pos, inv_freq, qg, kvg, w_uq, w_ukv):
    t = zm.shape[0]
    tm = _tile(t, ROW_TILE, 16)

    def body(dq_ref, dk_ref, dv_ref, zm_ref, pos_ref, if_ref, qg_ref, kvg_ref, wq_ref, wkv_ref,
             dql_ref, dkvl_ref, dzm_ref, sums_ref):
        @pl.when(pl.program_id(0) == 0)
        def _():
            sums_ref[...] = jnp.zeros_like(sums_ref)

        tables = _rope_tables(pos_ref[...], if_ref[...])
        dkr = jnp.zeros((tm, LANES), F32)
        for h in range(MLA_HEADS):
            lo = h * HEAD_PAD
            dql_ref[:, lo:lo + QK_NOPE] = (dq_ref[:, lo:lo + QK_NOPE] * QK_FOLD).astype(BF16)
            dql_ref[:, lo + QK_NOPE:lo + HEAD_PAD] = _rope_transposed(
                dq_ref[:, lo + QK_NOPE:lo + HEAD_PAD] * QK_FOLD, tables).astype(BF16)
            dkvl_ref[:, h * QK_NOPE:(h + 1) * QK_NOPE] = dk_ref[:, lo:lo + QK_NOPE].astype(BF16)
            dkr = dkr + dk_ref[:, lo + QK_NOPE:lo + HEAD_PAD]
        dkvl_ref[:, MLA_HEADS * QK_NOPE:] = dv_ref[...].astype(BF16)
        zv = zm_ref[...]
        dqn = _dot(dql_ref[...], wq_ref[...])
        dkvn = _dot(dkvl_ref[...], wkv_ref[...])
        dcq, d_qg = _rms_bwd(dqn, zv[:, :Q_LORA], qg_ref[...])
        dckv, d_kvg = _rms_bwd(dkvn, zv[:, Q_LORA:Q_LORA + KV_LORA], kvg_ref[...])
        dzm_ref[:, :Q_LORA] = dcq.astype(BF16)
        dzm_ref[:, Q_LORA:Q_LORA + KV_LORA] = dckv.astype(BF16)
        dzm_ref[:, Q_LORA + KV_LORA:] = _rope_transposed(dkr, tables).astype(BF16)
        sums_ref[0:1, :Q_LORA] += d_qg
        sums_ref[0:1, Q_LORA:Q_LORA + KV_LORA] += d_kvg

    def rows(n):
        return pl.BlockSpec((tm, n), lambda i: (i, 0))

    return pl.pallas_call(
        body, name="mla_project_bwd", grid=(t // tm,),
        in_specs=[rows(QK_COLS), rows(QK_COLS), rows(MLA_WIDTH), rows(ZM_COLS), rows(1), _row(inv_freq),
                  _row(qg), _row(kvg), _row(w_uq), _row(w_ukv)],
        out_specs=[rows(QK_COLS), rows(QK_COLS), rows(ZM_COLS), pl.BlockSpec((8, ZM_COLS), lambda i: (0, 0))],
        out_shape=[jax.ShapeDtypeStruct((t, QK_COLS), BF16), jax.ShapeDtypeStruct((t, QK_COLS), BF16),
                   jax.ShapeDtypeStruct((t, ZM_COLS), BF16), jax.ShapeDtypeStruct((8, ZM_COLS), F32)],
        compiler_params=_params(("arbitrary",)),
    )(dq, dk, dv, zm, pos, inv_freq, qg, kvg, w_uq, w_ukv)


def mix_in_backward(dzc, dzm, w_in, x, dxo, gn, sc, gate):
    t, d = x.shape
    tm = _tile(t, ROW_TILE, 16)

    def body(dzc_ref, dzm_ref, w_ref, x_ref, dxo_ref, gn_ref, sc_ref, gate_ref, dx_ref, dy_ref, sums_ref):
        @pl.when(pl.program_id(0) == 0)
        def _():
            sums_ref[...] = jnp.zeros_like(sums_ref)

        dh = _dot(dzc_ref[...], w_ref[:ZC_COLS, :]) + _dot(dzm_ref[...], w_ref[ZC_COLS:, :])
        dx, d_sh, d_sc, d_gn = _norm_mod_bwd(dh, x_ref[...], gn_ref[...], sc_ref[...])
        dx = dxo_ref[...] + dx
        dx_ref[...] = dx
        dy_ref[...] = (0.5 * gate_ref[...] * dx).astype(BF16)
        _add_rows(sums_ref, [d_sh, d_sc, d_gn])

    def rows(n):
        return pl.BlockSpec((tm, n), lambda i: (i, 0))

    return pl.pallas_call(
        body, name="mix_in_bwd", grid=(t // tm,),
        in_specs=[rows(ZC_COLS), rows(ZM_COLS), _row(w_in), rows(d), rows(d), _row(gn), _row(sc), _row(gate)],
        out_specs=[rows(d), rows(d), pl.BlockSpec((8, d), lambda i: (0, 0))],
        out_shape=[jax.ShapeDtypeStruct((t, d), F32), jax.ShapeDtypeStruct((t, d), BF16),
                   jax.ShapeDtypeStruct((8, d), F32)],
        compiler_params=_params(("arbitrary",)),
    )(dzc, dzm, w_in, x, dxo, gn, sc, gate)


def final_loss(x, target, g, gate):
    t, d = x.shape
    tm = _tile(t, ROW_TILE, 16)

    def body(x_ref, t_ref, g_ref, gate_ref, dx_ref, dy_ref, sums_ref):
        @pl.when(pl.program_id(0) == 0)
        def _():
            sums_ref[...] = jnp.zeros_like(sums_ref)

        gv = g_ref[...]
        xhat, r = _rms(x_ref[...])
        err = xhat * gv - t_ref[...]
        dyf = err * (1.0 / d)
        dxh = dyf * gv
        dx = r * (dxh - xhat * jnp.mean(dxh * xhat, axis=-1, keepdims=True))
        dx_ref[...] = dx
        dy_ref[...] = (0.5 * gate_ref[...] * dx).astype(BF16)
        _add_rows(sums_ref, [jnp.sum(dyf * xhat, axis=0, keepdims=True),
                             jnp.sum(err * err, axis=0, keepdims=True) * (0.5 / d)])

    row = pl.BlockSpec((tm, d), lambda i: (i, 0))
    return pl.pallas_call(
        body, name="final_loss", grid=(t // tm,),
        in_specs=[row, row, _row(g), _row(gate)],
        out_specs=[row, row, pl.BlockSpec((8, d), lambda i: (0, 0))],
        out_shape=[jax.ShapeDtypeStruct((t, d), F32), jax.ShapeDtypeStruct((t, d), BF16),
                   jax.ShapeDtypeStruct((8, d), F32)],
        compiler_params=_params(("arbitrary",)),
    )(x, target, g, gate)


def adamw(w, g, m, v, name):
    r, n = w.shape
    tr = _tile(r, max(8, (1 << 19) // n), 8)

    def body(w_ref, g_ref, m_ref, v_ref, d_ref, mo_ref, vo_ref):
        gv = g_ref[...]
        m_new = ADAM_B1 * m_ref[...] + (1.0 - ADAM_B1) * gv
        v_new = ADAM_B2 * v_ref[...] + (1.0 - ADAM_B2) * (gv * gv)
        m_hat = m_new / (1.0 - ADAM_B1 ** ADAM_STEP)
        v_hat = v_new / (1.0 - ADAM_B2 ** ADAM_STEP)
        d_ref[...] = -ADAM_LR * (m_hat / (jnp.sqrt(v_hat) + ADAM_EPS) + ADAM_WD * w_ref[...])
        mo_ref[...] = m_new
        vo_ref[...] = v_new

    blk = pl.BlockSpec((tr, n), lambda i: (i, 0))
    shape = jax.ShapeDtypeStruct((r, n), F32)
    return pl.pallas_call(
        body, name=name, grid=(r // tr,), in_specs=[blk] * 4, out_specs=[blk] * 3, out_shape=[shape] * 3,
        compiler_params=_params(("arbitrary",)),
    )(w, g, m, v)


def _pad_to(v, n):
    return jnp.pad(v, (0, n - v.shape[0]))


def _pad_heads(w, axis_len):
    n = w.shape[1]
    return jnp.pad(w.reshape(MLA_HEADS, axis_len, n), ((0, 0), (0, HEAD_PAD - axis_len), (0, 0))).reshape(-1, n)


def _swap_head_parts(w, inner, outer):
    n = w.shape[1]
    return w.reshape(outer, inner, QK_NOPE, n).transpose(1, 0, 2, 3).reshape(-1, n)


def kernel(x, c, positions, ada_w, ada_b, norm_ffn1_g, ffn1_w1, ffn1_w3, ffn1_w2, norm_mix_g, w_in, conv_w, q_norm_g, w_uq, kv_norm_g, w_ukv, out_norm_g, w_out, norm_ffn2_g, ffn2_w1, ffn2_w3, ffn2_w2, final_norm_g, loss_target, m_ada_w, m_ada_b, m_norm_ffn1_g, m_ffn1_w1, m_ffn1_w3, m_ffn1_w2, m_norm_mix_g, m_w_in, m_conv_w, m_q_norm_g, m_w_uq, m_kv_norm_g, m_w_ukv, m_out_norm_g, m_w_out, m_norm_ffn2_g, m_ffn2_w1, m_ffn2_w3, m_ffn2_w2, m_final_norm_g, v_ada_w, v_ada_b, v_norm_ffn1_g, v_ffn1_w1, v_ffn1_w3, v_ffn1_w2, v_norm_mix_g, v_w_in, v_conv_w, v_q_norm_g, v_w_uq, v_kv_norm_g, v_w_ukv, v_out_norm_g, v_w_out, v_norm_ffn2_g, v_ffn2_w1, v_ffn2_w3, v_ffn2_w2, v_final_norm_g):
    t, d = x.shape[1], x.shape[2]
    f = ffn1_w2.shape[1] * N_DEV
    me = 4 * lax.axis_index("x") + 2 * lax.axis_index("y") + lax.axis_index("c")
    my_c = lax.axis_index("c")
    my_chip = 2 * lax.axis_index("x") + lax.axis_index("y")
    xs = x[0]
    n_ada = ada_w.shape[2]
    cw_n = conv_w.shape[2]

    c_rows = jnp.broadcast_to(c, (8, d))
    conv_rows = jnp.pad(conv_w[0], ((0, 8 - CONV_K), (0, LANES - cw_n)))
    ffn1_blocks = jnp.stack([ffn1_w1[0].T, ffn1_w3[0].T, ffn1_w2[0]]).astype(BF16)
    ffn2_blocks = jnp.stack([ffn2_w1[0].T, ffn2_w3[0].T, ffn2_w2[0]]).astype(BF16)
    c_all, conv_all, ffn1_all = all_gather([c_rows, conv_rows, ffn1_blocks], [0, 0, 1], "gather_first")
    c_all = c_all[:, 0, :]
    conv_full8 = conv_all[:, :, :cw_n].transpose(1, 0, 2).reshape(8, CONV_WIDTH)
    ffn1_ws = ffn1_all.reshape(3, f, d)
    gather_rest = riding_gather(
        [ffn2_blocks, w_in[0].T.astype(BF16), w_uq[0].T.astype(BF16), w_ukv[0].T.astype(BF16), w_out[0].astype(BF16)],
        [1, 0, 0, 0, 0])

    ada_b_cols = lax.dynamic_slice_in_dim(ada_b, me * n_ada, n_ada, axis=1)
    mod_cols = ada_forward(c_all, ada_w[0], ada_b_cols)
    mod_all, = all_gather([mod_cols], [0], "gather_mod")
    mod = lax.dynamic_index_in_dim(mod_all, me, axis=1, keepdims=False).reshape(N_MOD, 1, d)
    sh1, sc1, g1, sh2, sc2, g2, sh3, sc3, g3 = [mod[i] for i in range(N_MOD)]

    gf = final_norm_g.reshape(1, d)
    x1, h1, a1, b1, y1, *gathered = ffn_forward(xs, norm_ffn1_g, sc1, sh1, g1, ffn1_ws, 0, "ffn1_fwd", gather_rest)
    ffn2_ws = gathered[0].reshape(3, f, d)
    w_in_p = jnp.pad(gathered[1].reshape(IN_COLS, d), ((0, ZC_COLS + ZM_COLS - IN_COLS), (0, 0)))
    w_uq_p = _pad_heads(gathered[2].reshape(-1, Q_LORA), QK_NOPE + QK_ROPE)
    w_ukv_p = _swap_head_parts(gathered[3].reshape(-1, KV_LORA), 2, MLA_HEADS)
    w_out_f = gathered[4].reshape(MIX_WIDTH, d)
    h2, zc, zm = mix_in_forward(x1, norm_mix_g, sc2, sh2, w_in_p)
    pos = positions[0].astype(F32).reshape(t, 1)
    inv_freq = ROPE_THETA ** (-jnp.arange(0, QK_ROPE, 2, dtype=F32) / QK_ROPE)
    inv_freq = jnp.concatenate([inv_freq, inv_freq, jnp.zeros((LANES - QK_ROPE,), F32)]).reshape(1, LANES)
    qn, kvn, q, k, v = mla_project(zm, pos, inv_freq, q_norm_g, kv_norm_g, w_uq_p, w_ukv_p)
    o, lse = attention_forward(q, k, v)
    lane = jnp.arange(CONV_WIDTH)
    gmat_a = (lane[:, None] // (CONV_WIDTH // CONV_GROUPS) == lane[None, :] // (CONV_WIDTH // CONV_GROUPS))
    gmat_a = (gmat_a / (CONV_WIDTH // CONV_GROUPS)).astype(BF16)
    gmat_b = ((lane[:, None] // V_HEAD == lane[None, :] // V_HEAD) / V_HEAD).astype(BF16)
    x2, yn, y2, ya = mix_out_forward(zc, o, conv_full8, out_norm_g, gmat_a, gmat_b, w_out_f, x1, g2)
    x3, h3, a3, b3, y3 = ffn_forward(x2, norm_ffn2_g, sc3, sh3, g3, ffn2_ws, 0, "ffn2_fwd")
    dx3, dy3, sums_f = final_loss(x3, loss_target[0], gf, g3)

    chip_idx = jnp.bitwise_xor(my_chip, jnp.array([0, 2, 1, 3], jnp.int32)).astype(jnp.int32)
    src_idx = (2 * chip_idx + my_c).astype(jnp.int32)

    def chip_sums(tag, named):
        g8 = [g.reshape(N_DEV, g.shape[0] // N_DEV, g.shape[1]) for _, g in named]
        got = exchange_sibling(g8, "rs_sibling_" + tag)
        return [add_sibling(g, r, src_idx, chip_idx, "rs_add_" + n) for g, r, (n, _) in zip(g8, got, named)]

    da3, db3, u3 = ffn_backward_gate(dy3, a3, b3, ffn2_ws, 0, "ffn2_bwd_gate")
    dx2, sums_3 = ffn_backward_norm(da3, db3, dx3, x2, y3, norm_ffn2_g, sc3, ffn2_ws, 0, "ffn2_bwd_norm")
    ffn2_named = [("ffn2_w1", matmul_tn(da3, h3, "ffn2_gw1")), ("ffn2_w3", matmul_tn(db3, h3, "ffn2_gw3")),
                  ("ffn2_w2", matmul_tn(u3, dy3, "ffn2_gw2"))]
    ffn2_sums = chip_sums("ffn2", ffn2_named)
    dy2, dya, do, delta, sums_2d, sums_2o = mix_out_backward(dx2, y2, g2, ya, o, out_norm_g, gmat_a, gmat_b, w_out_f)
    g_w_out = matmul_tn(yn, dy2, "gw_out")
    nq = t // _tile(t, ATTN_TILE, CHUNK)
    stat_shape = (MLA_HEADS, nq, 1, t // nq)
    dq, dk, dv, *ffn2_got = attention_backward(q, k, v, do, lse.reshape(stat_shape), delta.reshape(stat_shape),
                                               riding_exchange([s[1] for s in ffn2_sums]))
    dzc, sums_c = conv_backward(zc, dya, conv_full8)
    dql, dkvl, dzm, sums_m = mla_project_backward(dq, dk, dv, zm, pos, inv_freq, q_norm_g, kv_norm_g, w_uq_p, w_ukv_p)
    g_w_uq_p = matmul_tn(dql, qn, "gw_uq")
    g_w_ukv_p = matmul_tn(dkvl, kvn, "gw_ukv")
    g_w_in = jnp.concatenate([matmul_tn(dzc, h2, "gw_in_conv"), matmul_tn(dzm, h2, "gw_in_mla")])[:IN_COLS]
    g_w_uq = g_w_uq_p.reshape(MLA_HEADS, HEAD_PAD, Q_LORA)[:, :QK_NOPE + QK_ROPE].reshape(-1, Q_LORA)
    g_w_ukv = _swap_head_parts(g_w_ukv_p, MLA_HEADS, 2)
    mix_named = [("w_in", g_w_in), ("w_uq", g_w_uq), ("w_ukv", g_w_ukv), ("w_out", g_w_out)]
    mix_sums = chip_sums("mix", mix_named)
    dx1, dy1, sums_1m = mix_in_backward(dzc, dzm, w_in_p, x1, dx2, norm_mix_g, sc2, g1)
    da1, db1, u1, *mix_got = ffn_backward_gate(dy1, a1, b1, ffn1_ws, 0, "ffn1_bwd_gate",
                                               riding_exchange([s[1] for s in mix_sums]))
    ffn1_named = [("ffn1_w1", matmul_tn(da1, h1, "ffn1_gw1")), ("ffn1_w3", matmul_tn(db1, h1, "ffn1_gw3")),
                  ("ffn1_w2", matmul_tn(u1, dy1, "ffn1_gw2"))]
    ffn1_sums = chip_sums("ffn1", ffn1_named)
    dx0, sums_1, *ffn1_got = ffn_backward_norm(da1, db1, dx1, xs, y1, norm_ffn1_g, sc1, ffn1_ws, 0, "ffn1_bwd_norm",
                                               riding_exchange([s[1] for s in ffn1_sums]))
    transposed = {"ffn1_w1", "ffn1_w3", "ffn2_w1", "ffn2_w3", "w_in", "w_uq", "w_ukv"}
    g_sh = {}
    for named, group_sums, group_got in ((ffn2_named, ffn2_sums, ffn2_got), (mix_named, mix_sums, mix_got),
                                         (ffn1_named, ffn1_sums, ffn1_got)):
        for (n, _), (own, _), got in zip(named, group_sums, group_got):
            g_rows = add_received(own, got, "rs_sum_" + n)
            g_sh[n] = g_rows.T if n in transposed else g_rows

    dmod = jnp.concatenate([sums_1[0], sums_1[1], sums_1[2], sums_1m[0], sums_1m[1], sums_2d[0],
                            sums_3[0], sums_3[1], sums_3[2]])
    pieces = [dmod, sums_1[3], sums_1m[2], sums_m[0, :Q_LORA], sums_m[0, Q_LORA:Q_LORA + KV_LORA], sums_2o[0],
              sums_3[3], sums_f[0], sums_f[1], sums_c[:CONV_K].reshape(-1)]
    plens = [p.shape[0] for p in pieces]
    poffs = [sum(plens[:i]) for i in range(len(plens))]
    vec_len = -(-sum(plens) // 1024) * 1024
    vec = _pad_to(jnp.concatenate(pieces), vec_len).reshape(-1, LANES)
    vec_all, = all_gather([vec], [0], "gather_sums")
    tot = sum_devices(vec_all).reshape(-1)
    g_ada_b, g_n1, g_nmix, g_qg, g_kvg, g_og, g_n3, g_gf, loss_lanes, g_conv_full = [
        tot[o:o + n] for o, n in zip(poffs, plens)]
    loss = sum_lanes(loss_lanes.reshape(1, d))[0, 0]
    g_conv = lax.dynamic_slice_in_dim(g_conv_full.reshape(CONV_K, CONV_WIDTH), me * cw_n, cw_n, axis=1)
    dmod_all = vec_all.reshape(N_DEV, vec_len)[:, :N_MOD * d]
    dmod_cols = lax.dynamic_slice_in_dim(dmod_all, me * n_ada, n_ada, axis=1)
    g_ada_w = ada_backward(jnp.pad(c_all, ((0, 8), (0, 0))), jnp.pad(dmod_cols, ((0, 8), (0, 0))))

    def update(name, w, g, m, v):
        shape = w.shape
        two_d = (-1, shape[-1])
        dlt, nm, nv = adamw(w.reshape(two_d), g.reshape(two_d), m.reshape(two_d), v.reshape(two_d), "adamw_" + name)
        return g.reshape(shape), dlt.reshape(shape), nm.reshape(shape), nv.reshape(shape)

    res = {}
    res["ada_w"] = update("ada_w", ada_w, g_ada_w, m_ada_w, v_ada_w)
    big = [("ffn1_w1", ffn1_w1, m_ffn1_w1, v_ffn1_w1), ("ffn1_w3", ffn1_w3, m_ffn1_w3, v_ffn1_w3),
           ("ffn2_w1", ffn2_w1, m_ffn2_w1, v_ffn2_w1), ("ffn2_w3", ffn2_w3, m_ffn2_w3, v_ffn2_w3),
           ("w_in", w_in, m_w_in, v_w_in), ("w_uq", w_uq, m_w_uq, v_w_uq), ("w_ukv", w_ukv, m_w_ukv, v_w_ukv),
           ("ffn1_w2", ffn1_w2, m_ffn1_w2, v_ffn1_w2), ("ffn2_w2", ffn2_w2, m_ffn2_w2, v_ffn2_w2),
           ("w_out", w_out, m_w_out, v_w_out)]
    for name, w, m, v in big:
        res[name] = update(name, w, g_sh[name], m, v)
    smalls = [("ada_b", ada_b, g_ada_b, m_ada_b, v_ada_b),
              ("norm_ffn1_g", norm_ffn1_g, g_n1, m_norm_ffn1_g, v_norm_ffn1_g),
              ("norm_mix_g", norm_mix_g, g_nmix, m_norm_mix_g, v_norm_mix_g),
              ("conv_w", conv_w, g_conv, m_conv_w, v_conv_w),
              ("q_norm_g", q_norm_g, g_qg, m_q_norm_g, v_q_norm_g),
              ("kv_norm_g", kv_norm_g, g_kvg, m_kv_norm_g, v_kv_norm_g),
              ("out_norm_g", out_norm_g, g_og, m_out_norm_g, v_out_norm_g),
              ("norm_ffn2_g", norm_ffn2_g, g_n3, m_norm_ffn2_g, v_norm_ffn2_g),
              ("final_norm_g", final_norm_g, g_gf, m_final_norm_g, v_final_norm_g)]
    slens = [w.size for _, w, _, _, _ in smalls]
    soffs = [sum(slens[:i]) for i in range(len(slens))]
    s_len = -(-sum(slens) // 1024) * 1024

    def pack_small(i):
        return _pad_to(jnp.concatenate([s[i].reshape(-1) for s in smalls]), s_len).reshape(8, -1)

    s_out = adamw(pack_small(1), pack_small(2), pack_small(3), pack_small(4), "adamw_small")
    for (name, w, g, _, _), o, n in zip(smalls, soffs, slens):
        res[name] = (g.reshape(w.shape),) + tuple(a.reshape(-1)[o:o + n].reshape(w.shape) for a in s_out)

    order = ["ada_w", "ada_b", "norm_ffn1_g", "ffn1_w1", "ffn1_w3", "ffn1_w2", "norm_mix_g", "w_in", "conv_w",
             "q_norm_g", "w_uq", "kv_norm_g", "w_ukv", "out_norm_g", "w_out", "norm_ffn2_g", "ffn2_w1", "ffn2_w3",
             "ffn2_w2", "final_norm_g"]
    return (loss, dx0.reshape(x.shape), *[res[n][0] for n in order], *[res[n][1] for n in order],
            *[res[n][2] for n in order], *[res[n][3] for n in order])
```

```python
import functools

import jax
import jax.numpy as jnp
from jax import lax
from jax.experimental import pallas as pl
from jax.experimental.pallas import tpu as pltpu

F32 = jnp.float32
BF16 = jnp.bfloat16
MESH_ID = pl.DeviceIdType.MESH
N_DEV = 8

EPS = 1e-6
CHUNK = 64
N_MOD = 9
CONV_WIDTH = 512
CONV_GROUPS = 8
CONV_K = 3
MLA_HEADS = 4
QK_NOPE = 128
QK_ROPE = 64
V_HEAD = 128
Q_LORA = 384
KV_LORA = 256
ROPE_THETA = 10000.0
MLA_WIDTH = MLA_HEADS * V_HEAD
MIX_WIDTH = CONV_WIDTH + MLA_WIDTH
IN_COLS = 3 * CONV_WIDTH + Q_LORA + KV_LORA + QK_ROPE
ZC_COLS = 3 * CONV_WIDTH
ZM_COLS = Q_LORA + KV_LORA + 128
HEAD_PAD = 256
QK_COLS = MLA_HEADS * HEAD_PAD
ATTN_SCALE = (QK_NOPE + QK_ROPE) ** -0.5
LOG2_E = 1.4426950408889634
LN_2 = 0.6931471805599453
QK_FOLD = ATTN_SCALE * LOG2_E
NEG_INF = -1e30

ADAM_LR = 0.001
ADAM_B1 = 0.9
ADAM_B2 = 0.999
ADAM_EPS = 1e-08
ADAM_WD = 0.01
ADAM_STEP = 10

LANES = 128
MXU_COLS = 256
VMEM_LIMIT = 56 * 1024 * 1024
ROW_TILE = 512
FFN_FWD_TILE = (1024, 256)
FFN_BWD_TILE = (512, 1408)
GRAD_TILE = 1408
ATTN_TILE = 512

NN = (((1,), (0,)), ((), ()))
NT = (((1,), (1,)), ((), ()))
TN = (((0,), (0,)), ((), ()))


def _dot(a, b, dims=NN):
    return lax.dot_general(a, b, dims, preferred_element_type=F32)


def _tile(n, cap, mult=LANES):
    best = None
    for t in range(mult, min(n, cap) + 1, mult):
        if n % t == 0:
            best = t
    return n if best is None else best


def _params(sem=None):
    return pltpu.CompilerParams(dimension_semantics=sem, vmem_limit_bytes=VMEM_LIMIT)


def _row(v):
    return pl.BlockSpec(v.shape, lambda *_: (0,) * v.ndim)


def _sigmoid(x):
    return 0.5 * jnp.tanh(0.5 * x) + 0.5


def _rms(x):
    r = lax.rsqrt(jnp.mean(x * x, axis=-1, keepdims=True) + EPS)
    return x * r, r


def _norm_mod_bwd(dh, x, gn, sc):
    xhat, r = _rms(x)
    d_sh = jnp.sum(dh, axis=0, keepdims=True)
    d_sc = jnp.sum(dh * (xhat * gn), axis=0, keepdims=True)
    dxn = dh * (1.0 + sc)
    d_gn = jnp.sum(dxn * xhat, axis=0, keepdims=True)
    dxh = dxn * gn
    dx = r * (dxh - xhat * jnp.mean(dxh * xhat, axis=-1, keepdims=True))
    return dx, d_sh, d_sc, d_gn


def _group_mean(v, gmat):
    hi = v.astype(BF16)
    lo = (v - hi.astype(F32)).astype(BF16)
    return _dot(hi, gmat) + _dot(lo, gmat)


def _add_rows(ref, rows):
    for r, v in enumerate(rows):
        ref[r:r + 1, :] += v


def _window(ref, axis, j):
    return ref.at[(slice(None),) * axis + (j,)]


def _any_specs(n):
    return [pl.BlockSpec(memory_space=pl.ANY)] * n


def all_gather(blocks, axes, name):
    n_arr = len(blocks)

    def body(*refs):
        start, forward, finish = _gather_steps(refs[:n_arr], refs[n_arr:2 * n_arr], axes, *refs[2 * n_arr:])
        start()
        for j in range(3):
            forward(j)
        finish()

    return pl.pallas_call(
        body, name=name, out_shape=_gathered_shapes(blocks, axes),
        in_specs=_any_specs(n_arr), out_specs=_any_specs(n_arr), scratch_shapes=_gather_sems(n_arr),
    )(*blocks)


def _gathered_shapes(blocks, axes):
    return [jax.ShapeDtypeStruct(b.shape[:ax] + (N_DEV,) + b.shape[ax:], b.dtype) for b, ax in zip(blocks, axes)]


def _gather_sems(n_arr):
    return [pltpu.SemaphoreType.DMA((7, n_arr)), pltpu.SemaphoreType.DMA((7, n_arr)), pltpu.SemaphoreType.DMA((n_arr,))]


def _gather_steps(ins, outs, axes, send_sems, recv_sems, local_sems):
    arrays = range(len(ins))
    x, y, c = lax.axis_index("x"), lax.axis_index("y"), lax.axis_index("c")
    me, sibling = (x, y, c), (x, y, 1 - c)
    chips = [(1 - x, y), (x, 1 - y), (1 - x, 1 - y)]

    def slot(a, px, py, pc):
        return _window(outs[a], axes[a], 4 * px + 2 * py + pc)

    def copy(a, k, block, to, src=None):
        return pltpu.make_async_remote_copy(
            src_ref=slot(a, *block) if src is None else src, dst_ref=slot(a, *block),
            send_sem=send_sems.at[k, a], recv_sem=recv_sems.at[k, a], device_id=to, device_id_type=MESH_ID)

    def mine(a):
        return pltpu.make_async_copy(ins[a], slot(a, *me), local_sems.at[a])

    def first():
        return ([copy(a, 0, me, sibling, src=ins[a]) for a in arrays]
                + [copy(a, 1 + j, me, (*chip, c), src=ins[a]) for j, chip in enumerate(chips) for a in arrays])

    def passed(j):
        return [copy(a, 4 + j, (*chips[j], c), sibling) for a in arrays]

    def start():
        for a in arrays:
            mine(a).start()
        for cp in first():
            cp.start()

    def forward(j):
        for a, cp in zip(arrays, passed(j)):
            copy(a, 1 + j, (*chips[j], c), me).wait_recv()
            cp.start()

    def finish():
        for a in arrays:
            copy(a, 0, sibling, me).wait_recv()
        for j, chip in enumerate(chips):
            for a in arrays:
                copy(a, 4 + j, (*chip, 1 - c), me).wait_recv()
        for cp in first() + passed(0) + passed(1) + passed(2):
            cp.wait_send()
        for a in arrays:
            mine(a).wait()

    return start, forward, finish


def exchange_sibling(grads, name):
    n_arr = len(grads)

    def body(*refs):
        ins, outs = refs[:n_arr], refs[n_arr:2 * n_arr]
        send_sems, recv_sems = refs[2 * n_arr:]
        x, y, c = lax.axis_index("x"), lax.axis_index("y"), lax.axis_index("c")

        def copy(a, src, dst):
            return pltpu.make_async_remote_copy(
                src_ref=src, dst_ref=dst, send_sem=send_sems.at[a], recv_sem=recv_sems.at[a],
                device_id=(x, y, 1 - c), device_id_type=MESH_ID)

        for a in range(n_arr):
            for k in range(4):
                copy(a, ins[a].at[2 * k + (1 - c)], outs[a].at[k]).start()
        whole = [copy(a, ins[a].at[pl.ds(0, 4)], outs[a]) for a in range(n_arr)]
        for cp in whole:
            cp.wait_recv()
        for cp in whole:
            cp.wait_send()

    return pl.pallas_call(
        body, name=name,
        out_shape=[jax.ShapeDtypeStruct((4,) + g.shape[1:], g.dtype) for g in grads],
        in_specs=_any_specs(n_arr), out_specs=_any_specs(n_arr),
        scratch_shapes=[pltpu.SemaphoreType.DMA((n_arr,)), pltpu.SemaphoreType.DMA((n_arr,))],
    )(*grads)


def _exchange_sems(n_arr):
    return [pltpu.SemaphoreType.DMA((n_arr,)), pltpu.SemaphoreType.DMA((n_arr,))]


def _chip_exchange_steps(ins, outs, send_sems, recv_sems):
    x, y, c = lax.axis_index("x"), lax.axis_index("y"), lax.axis_index("c")
    chips = [(1 - x, y), (x, 1 - y), (1 - x, 1 - y)]

    def copy(a, src, dst, chip):
        return pltpu.make_async_remote_copy(
            src_ref=src, dst_ref=dst, send_sem=send_sems.at[a], recv_sem=recv_sems.at[a],
            device_id=(*chip, c), device_id_type=MESH_ID)

    def start():
        for a in range(len(ins)):
            for j, chip in enumerate(chips):
                copy(a, ins[a].at[j], outs[a].at[j], chip).start()

    def finish():
        whole = [copy(a, ins[a], outs[a], chips[0]) for a in range(len(ins))]
        for cp in whole:
            cp.wait_recv()
        for cp in whole:
            cp.wait_send()

    return start, finish


def riding_gather(blocks, axes):
    def phases(ins, outs, *sems):
        start, forward, finish = _gather_steps(ins, outs, axes, *sems)
        return [start] + [functools.partial(forward, j) for j in range(3)] + [finish]

    return dict(operands=blocks, out_shape=_gathered_shapes(blocks, axes), sems=_gather_sems(len(blocks)),
                phases=phases, when=("first", "late0", "late1", "late2", "last"))


def riding_exchange(parts):
    def phases(ins, outs, *sems):
        return list(_chip_exchange_steps(ins, outs, *sems))

    return dict(operands=parts, out_shape=[jax.ShapeDtypeStruct(p.shape, p.dtype) for p in parts],
                sems=_exchange_sems(len(parts)), phases=phases, when=("first", "last"))


def _call_with_rider(body, rider, *, name, grid, in_specs, out_specs, out_shape, scratch_shapes, operands):
    params = _params(("arbitrary",) * len(grid))
    if rider is None:
        return pl.pallas_call(body, name=name, grid=grid, in_specs=in_specs, out_specs=out_specs,
                              out_shape=out_shape, scratch_shapes=scratch_shapes, compiler_params=params)(*operands)
    n_in, n_out, n_scr, k = len(in_specs), len(out_specs), len(scratch_shapes), len(rider["operands"])
    rows, cols = grid
    assert cols >= 3 or "late0" not in rider["when"]
    late_row = max(rows - 2, 0)
    at = {"first": (0, 0), "last": (rows - 1, cols - 1),
          "late0": (late_row, 0), "late1": (late_row, 1), "late2": (late_row, 2)}

    def wrapped(*refs):
        ins, c_in = refs[:n_in], refs[n_in:n_in + k]
        outs, c_out = refs[n_in + k:n_in + k + n_out], refs[n_in + k + n_out:n_in + 2 * k + n_out]
        scratch, sems = refs[n_in + 2 * k + n_out:n_in + 2 * k + n_out + n_scr], refs[n_in + 2 * k + n_out + n_scr:]
        i, j = pl.program_id(0), pl.program_id(1)
        phases = rider["phases"](c_in, c_out, *sems)
        for fn, key in zip(phases, rider["when"]):
            if key != "last":
                pl.when(jnp.logical_and(i == at[key][0], j == at[key][1]))(fn)
        body(*ins, *outs, *scratch)
        pl.when(jnp.logical_and(i == at["last"][0], j == at["last"][1]))(phases[-1])

    return pl.pallas_call(
        wrapped, name=name, grid=grid,
        in_specs=list(in_specs) + _any_specs(k), out_specs=list(out_specs) + _any_specs(k),
        out_shape=list(out_shape) + rider["out_shape"], scratch_shapes=list(scratch_shapes) + rider["sems"],
        compiler_params=params)(*operands, *rider["operands"])


def add_sibling(g8, got, src_idx, chip_idx, name):
    _, r, n = g8.shape
    tr = _tile(r, 256, 16)

    def body(si_ref, ci_ref, g0_ref, g1_ref, g2_ref, g3_ref, got_ref, own_ref, send_ref):
        own_ref[...] = g0_ref[0] + got_ref[ci_ref[0]]
        for j, g_ref in enumerate((g1_ref, g2_ref, g3_ref)):
            send_ref[j] = (g_ref[0] + got_ref[ci_ref[j + 1]]).astype(BF16)

    def mine(j):
        return pl.BlockSpec((1, tr, n), lambda i, si, ci: (si[j], i, 0))

    return pl.pallas_call(
        body, name=name,
        out_shape=[jax.ShapeDtypeStruct((r, n), F32), jax.ShapeDtypeStruct((3, r, n), BF16)],
        grid_spec=pltpu.PrefetchScalarGridSpec(
            num_scalar_prefetch=2, grid=(r // tr,),
            in_specs=[mine(0), mine(1), mine(2), mine(3), pl.BlockSpec((4, tr, n), lambda i, si, ci: (0, i, 0))],
            out_specs=[pl.BlockSpec((tr, n), lambda i, si, ci: (i, 0)),
                       pl.BlockSpec((3, tr, n), lambda i, si, ci: (0, i, 0))]),
        compiler_params=_params(("arbitrary",)),
    )(src_idx, chip_idx, g8, g8, g8, g8, got)


def add_received(own, got, name):
    r, n = own.shape
    tr = _tile(r, 256, 16)

    def body(a_ref, b_ref, o_ref):
        acc = a_ref[...]
        for j in range(3):
            acc = acc + b_ref[j].astype(F32)
        o_ref[...] = acc

    return pl.pallas_call(
        body, name=name,
        out_shape=jax.ShapeDtypeStruct((r, n), F32),
        grid=(r // tr,),
        in_specs=[pl.BlockSpec((tr, n), lambda i: (i, 0)), pl.BlockSpec((3, tr, n), lambda i: (0, i, 0))],
        out_specs=pl.BlockSpec((tr, n), lambda i: (i, 0)),
        compiler_params=_params(("arbitrary",)),
    )(own, got)


def sum_devices(g):
    def body(g_ref, o_ref):
        acc = g_ref[0]
        for j in range(1, N_DEV):
            acc = acc + g_ref[j]
        o_ref[...] = acc

    return pl.pallas_call(body, name="sum_devices", out_shape=jax.ShapeDtypeStruct(g.shape[1:], F32))(g)


def sum_lanes(v):
    def body(v_ref, o_ref):
        o_ref[...] = jnp.broadcast_to(jnp.sum(v_ref[...], axis=-1, keepdims=True), (1, LANES))

    return pl.pallas_call(body, name="sum_lanes", out_shape=jax.ShapeDtypeStruct((1, LANES), F32))(v)


def ada_forward(c_all, ada_w, ada_b_cols):
    nb, n = c_all.shape[0], ada_w.shape[1]

    def body(c_ref, w_ref, b_ref, o_ref):
        cv = c_ref[...]
        s = (cv * jax.nn.sigmoid(cv)).astype(BF16)
        o_ref[...] = _dot(s, w_ref[...].astype(BF16)) + b_ref[...]

    return pl.pallas_call(body, name="ada_fwd", out_shape=jax.ShapeDtypeStruct((nb, n), F32),
                          compiler_params=_params())(c_all, ada_w, ada_b_cols)


def ada_backward(c_all16, dmod16):
    d, n = c_all16.shape[1], dmod16.shape[1]

    def body(c_ref, g_ref, o_ref):
        cv = c_ref[...]
        s = (cv * jax.nn.sigmoid(cv)).astype(BF16)
        o_ref[...] = _dot(s, g_ref[...].astype(BF16), TN)

    return pl.pallas_call(body, name="ada_bwd", out_shape=jax.ShapeDtypeStruct((d, n), F32),
                          compiler_params=_params())(c_all16, dmod16)


def ffn_forward(x, gn, sc, sh, gate, ws, first, name, rider=None):
    t, d = x.shape
    f = ws.shape[1]
    tm, tf = _tile(t, FFN_FWD_TILE[0], 16), _tile(f, FFN_FWD_TILE[1])
    nf = f // tf

    def body(x_ref, gn_ref, sc_ref, sh_ref, gate_ref, w1_ref, w3_ref, w2_ref,
             xo_ref, h_ref, a_ref, b_ref, y_ref, hs, acc):
        j = pl.program_id(1)

        @pl.when(j == 0)
        def _():
            xhat, _ = _rms(x_ref[...])
            h = (xhat * gn_ref[...] * (1.0 + sc_ref[...]) + sh_ref[...]).astype(BF16)
            hs[...] = h
            h_ref[...] = h
            acc[...] = jnp.zeros_like(acc)

        h = hs[...]
        a = _dot(h, w1_ref[...], NT)
        b = _dot(h, w3_ref[...], NT)
        a_ref[...] = a.astype(BF16)
        b_ref[...] = b.astype(BF16)
        u = (a * _sigmoid(a) * b).astype(BF16)
        acc[...] += _dot(u, w2_ref[...])

        @pl.when(j == nf - 1)
        def _():
            y = acc[...]
            y_ref[...] = y.astype(BF16)
            xo_ref[...] = x_ref[...] + 0.5 * gate_ref[...] * y

    row = pl.BlockSpec((tm, d), lambda i, j: (i, 0))
    vec = pl.BlockSpec((1, d), lambda i, j: (0, 0))
    wide = pl.BlockSpec((tm, tf), lambda i, j: (i, j))
    return _call_with_rider(
        body, rider, name=name, grid=(t // tm, nf),
        in_specs=[row, vec, vec, vec, vec] + _ffn_weight_specs(first, tf, d),
        out_specs=[row, row, wide, wide, row],
        out_shape=[jax.ShapeDtypeStruct((t, d), F32), jax.ShapeDtypeStruct((t, d), BF16),
                   jax.ShapeDtypeStruct((t, f), BF16), jax.ShapeDtypeStruct((t, f), BF16),
                   jax.ShapeDtypeStruct((t, d), BF16)],
        scratch_shapes=[pltpu.VMEM((tm, d), BF16), pltpu.VMEM((tm, d), F32)],
        operands=(x, gn, sc, sh, gate, ws, ws, ws))


def _ffn_weight_specs(first, tf, d):
    return [pl.BlockSpec((None, tf, d), lambda i, j, w=first + k: (w, j, 0)) for k in range(3)]


def ffn_backward_gate(dy, a, b, ws, first, name, rider=None):
    t, d = dy.shape
    f = ws.shape[1]
    tm, tf = _tile(t, FFN_BWD_TILE[0], 16), _tile(f, FFN_BWD_TILE[1])
    nf = f // tf

    def gate_body(dy_ref, a_ref, b_ref, w2_ref, da_ref, db_ref, u_ref):
        du = _dot(dy_ref[...], w2_ref[...], NT)
        av = a_ref[...].astype(F32)
        bv = b_ref[...].astype(F32)
        s = _sigmoid(av)
        sa = av * s
        da_ref[...] = (du * bv * (s + sa * (1.0 - s))).astype(BF16)
        db_ref[...] = (du * sa).astype(BF16)
        u_ref[...] = (sa * bv).astype(BF16)

    hidden = jax.ShapeDtypeStruct((t, f), BF16)
    wide_t = pl.BlockSpec((tm, tf), lambda j, i: (i, j))
    return _call_with_rider(
        gate_body, rider, name=name, grid=(nf, t // tm),
        in_specs=[pl.BlockSpec((tm, d), lambda j, i: (i, 0)), wide_t, wide_t,
                  pl.BlockSpec((None, tf, d), lambda j, i: (first + 2, j, 0))],
        out_specs=[wide_t, wide_t, wide_t], out_shape=[hidden, hidden, hidden],
        scratch_shapes=[], operands=(dy, a, b, ws))


def ffn_backward_norm(da, db, dxo, x, y, gn, sc, ws, first, name, rider=None):
    t, d = x.shape
    f = ws.shape[1]
    tm, tf = _tile(t, FFN_BWD_TILE[0], 16), _tile(f, FFN_BWD_TILE[1])
    nf = f // tf
    row = pl.BlockSpec((tm, d), lambda i, j: (i, 0))
    vec = pl.BlockSpec((1, d), lambda i, j: (0, 0))
    wide = pl.BlockSpec((tm, tf), lambda i, j: (i, j))

    def norm_body(da_ref, db_ref, w1_ref, w3_ref, dxo_ref, x_ref, y_ref, gn_ref, sc_ref, dx_ref, sums_ref, acc):
        i, j = pl.program_id(0), pl.program_id(1)

        @pl.when(jnp.logical_and(i == 0, j == 0))
        def _():
            sums_ref[...] = jnp.zeros_like(sums_ref)

        part = _dot(da_ref[...], w1_ref[...]) + _dot(db_ref[...], w3_ref[...])

        @pl.when(j == 0)
        def _():
            acc[...] = part

        @pl.when(jnp.logical_and(j > 0, j < nf - 1))
        def _():
            acc[...] += part

        @pl.when(j == nf - 1)
        def _():
            dh = part if nf == 1 else acc[...] + part
            dxo_v = dxo_ref[...]
            dx, d_sh, d_sc, d_gn = _norm_mod_bwd(dh, x_ref[...], gn_ref[...], sc_ref[...])
            dx_ref[...] = dxo_v + dx
            d_gate = jnp.sum(dxo_v * (0.5 * y_ref[...].astype(F32)), axis=0, keepdims=True)
            _add_rows(sums_ref, [d_sh, d_sc, d_gate, d_gn])

    w1_spec, w3_spec, _ = _ffn_weight_specs(first, tf, d)
    return _call_with_rider(
        norm_body, rider, name=name, grid=(t // tm, nf),
        in_specs=[wide, wide, w1_spec, w3_spec, row, row, row, vec, vec],
        out_specs=[row, pl.BlockSpec((8, d), lambda i, j: (0, 0))],
        out_shape=[jax.ShapeDtypeStruct((t, d), F32), jax.ShapeDtypeStruct((8, d), F32)],
        scratch_shapes=[pltpu.VMEM((tm, d), F32)],
        operands=(da, db, ws, ws, dxo, x, y, gn, sc))


def matmul_tn(a, b, name):
    t, m = a.shape
    n = b.shape[1]
    tm, tn, tk = _tile(m, GRAD_TILE), _tile(n, GRAD_TILE), _tile(t, 1024, 16)
    nk = t // tk

    def body(a_ref, b_ref, o_ref, acc):
        k = pl.program_id(2)

        @pl.when(k == 0)
        def _():
            acc[...] = jnp.zeros_like(acc)

        acc[...] += _dot(a_ref[...], b_ref[...], TN)

        @pl.when(k == nk - 1)
        def _():
            o_ref[...] = acc[...]

    return pl.pallas_call(
        body, name=name, grid=(m // tm, n // tn, nk),
        in_specs=[pl.BlockSpec((tk, tm), lambda i, j, k: (k, i)), pl.BlockSpec((tk, tn), lambda i, j, k: (k, j))],
        out_specs=pl.BlockSpec((tm, tn), lambda i, j, k: (i, j)),
        out_shape=jax.ShapeDtypeStruct((m, n), F32),
        scratch_shapes=[pltpu.VMEM((tm, tn), F32)],
        compiler_params=_params(("arbitrary", "arbitrary", "arbitrary")),
    )(a, b)


def mix_in_forward(x, gn, sc, sh, w_in):
    t, d = x.shape
    tm = _tile(t, ROW_TILE, 16)

    def body(x_ref, gn_ref, sc_ref, sh_ref, w_ref, h_ref, zc_ref, zm_ref):
        xhat, _ = _rms(x_ref[...])
        h = (xhat * gn_ref[...] * (1.0 + sc_ref[...]) + sh_ref[...]).astype(BF16)
        h_ref[...] = h
        z = _dot(h, w_ref[...], NT)
        zc_ref[...] = z[:, :ZC_COLS].astype(BF16)
        zm_ref[...] = z[:, ZC_COLS:].astype(BF16)

    row = pl.BlockSpec((tm, d), lambda i: (i, 0))
    vec = pl.BlockSpec((1, d), lambda i: (0, 0))
    return pl.pallas_call(
        body, name="mix_in_fwd", grid=(t // tm,),
        in_specs=[row, vec, vec, vec, _row(w_in)],
        out_specs=[row, pl.BlockSpec((tm, ZC_COLS), lambda i: (i, 0)), pl.BlockSpec((tm, ZM_COLS), lambda i: (i, 0))],
        out_shape=[jax.ShapeDtypeStruct((t, d), BF16), jax.ShapeDtypeStruct((t, ZC_COLS), BF16),
                   jax.ShapeDtypeStruct((t, ZM_COLS), BF16)],
        compiler_params=_params(("arbitrary",)),
    )(x, gn, sc, sh, w_in)


def _rope_tables(pos, inv_freq):
    ang = pos * inv_freq
    lane = lax.broadcasted_iota(jnp.int32, ang.shape, 1)
    cos, sin = jnp.cos(ang), jnp.sin(ang)
    half = QK_ROPE // 2
    return cos, jnp.where(lane < half, -sin, 0.0), jnp.where(jnp.logical_and(lane >= half, lane < QK_ROPE), sin, 0.0)


def _rope(v, tables):
    cos, sin_a, sin_b = tables
    return v * cos + pltpu.roll(v, LANES - QK_ROPE // 2, 1) * sin_a + pltpu.roll(v, QK_ROPE // 2, 1) * sin_b


def _rope_transposed(dv, tables):
    cos, sin_a, sin_b = tables
    return dv * cos + pltpu.roll(dv * sin_a, QK_ROPE // 2, 1) + pltpu.roll(dv * sin_b, LANES - QK_ROPE // 2, 1)


def mla_project(zm, pos, inv_freq, qg, kvg, w_uq, w_ukv):
    t = zm.shape[0]
    tm = _tile(t, ROW_TILE, 16)

    def body(zm_ref, pos_ref, if_ref, qg_ref, kvg_ref, wq_ref, wkv_ref, qn_ref, kvn_ref, q_ref, k_ref, v_ref):
        zv = zm_ref[...].astype(F32)
        qn = (_rms(zv[:, :Q_LORA])[0] * qg_ref[...]).astype(BF16)
        kvn = (_rms(zv[:, Q_LORA:Q_LORA + KV_LORA])[0] * kvg_ref[...]).astype(BF16)
        qn_ref[...] = qn
        kvn_ref[...] = kvn
        qf = _dot(qn, wq_ref[...], NT) * QK_FOLD
        kvf = _dot(kvn, wkv_ref[...], NT)
        tables = _rope_tables(pos_ref[...], if_ref[...])
        kr = _rope(zv[:, Q_LORA + KV_LORA:], tables).astype(BF16)
        for h in range(MLA_HEADS):
            lo = h * HEAD_PAD
            q_ref[:, lo:lo + QK_NOPE] = qf[:, lo:lo + QK_NOPE].astype(BF16)
            q_ref[:, lo + QK_NOPE:lo + HEAD_PAD] = _rope(qf[:, lo + QK_NOPE:lo + HEAD_PAD], tables).astype(BF16)
            k_ref[:, lo:lo + QK_NOPE] = kvf[:, h * QK_NOPE:(h + 1) * QK_NOPE].astype(BF16)
            k_ref[:, lo + QK_NOPE:lo + HEAD_PAD] = kr
        v_ref[...] = kvf[:, MLA_HEADS * QK_NOPE:].astype(BF16)

    def rows(n):
        return pl.BlockSpec((tm, n), lambda i: (i, 0))

    return pl.pallas_call(
        body, name="mla_project", grid=(t // tm,),
        in_specs=[rows(ZM_COLS), rows(1), _row(inv_freq), _row(qg), _row(kvg), _row(w_uq), _row(w_ukv)],
        out_specs=[rows(Q_LORA), rows(KV_LORA), rows(QK_COLS), rows(QK_COLS), rows(MLA_WIDTH)],
        out_shape=[jax.ShapeDtypeStruct((t, Q_LORA), BF16), jax.ShapeDtypeStruct((t, KV_LORA), BF16),
                   jax.ShapeDtypeStruct((t, QK_COLS), BF16), jax.ShapeDtypeStruct((t, QK_COLS), BF16),
                   jax.ShapeDtypeStruct((t, MLA_WIDTH), BF16)],
        compiler_params=_params(("arbitrary",)),
    )(zm, pos, inv_freq, qg, kvg, w_uq, w_ukv)


def _chunk_mask(shape, q_axis):
    qi = lax.broadcasted_iota(jnp.int32, shape, q_axis) // CHUNK
    ki = lax.broadcasted_iota(jnp.int32, shape, 1 - q_axis) // CHUNK
    return ki <= qi


def attention_forward(q, k, v):
    t = q.shape[0]
    tq = _tile(t, ATTN_TILE, CHUNK)

    def body(q_ref, k_ref, v_ref, o_ref, lse_ref):
        i = pl.program_id(1)
        qv = q_ref[...]

        def step(kb, carry, masked):
            m, l, acc = carry
            start = pl.multiple_of(kb * tq, tq)
            s = _dot(qv, k_ref[pl.ds(start, tq), :], NT)
            if masked:
                s = jnp.where(_chunk_mask(s.shape, 0), s, NEG_INF)
            m_new = jnp.maximum(m, jnp.max(s, axis=-1, keepdims=True))
            alpha = jnp.exp2(m - m_new)
            p = jnp.exp2(s - m_new)
            l = alpha * l + jnp.sum(p, axis=-1, keepdims=True)
            acc = alpha * acc + _dot(p.astype(BF16), v_ref[pl.ds(start, tq), :])
            return m_new, l, acc

        init = (jnp.full((tq, 1), NEG_INF, F32), jnp.zeros((tq, 1), F32), jnp.zeros((tq, V_HEAD), F32))
        carry = lax.fori_loop(0, i // 2, lambda pb, cr: step(2 * pb + 1, step(2 * pb, cr, False), False), init)
        carry = lax.fori_loop(0, i % 2, lambda _, cr: step(i - 1, cr, False), carry)
        m, l, acc = step(i, carry, True)
        o_ref[...] = (acc / l).astype(BF16)
        lse_ref[0] = m + jnp.log2(l)

    return pl.pallas_call(
        body, name="attn_fwd", grid=(MLA_HEADS, t // tq),
        in_specs=[pl.BlockSpec((tq, HEAD_PAD), lambda h, i: (i, h)),
                  pl.BlockSpec((t, HEAD_PAD), lambda h, i: (0, h)),
                  pl.BlockSpec((t, V_HEAD), lambda h, i: (0, h))],
        out_specs=[pl.BlockSpec((tq, V_HEAD), lambda h, i: (i, h)),
                   pl.BlockSpec((1, tq, 1), lambda h, i: (h, i, 0))],
        out_shape=[jax.ShapeDtypeStruct((t, MLA_WIDTH), BF16), jax.ShapeDtypeStruct((MLA_HEADS, t, 1), F32)],
        compiler_params=_params(("arbitrary", "arbitrary")),
    )(q, k, v)


def attention_backward(q, k, v, do, lse, delta, rider=None):
    t = q.shape[0]
    tq = _tile(t, ATTN_TILE, CHUNK)
    nq = t // tq

    def body(q_ref, k_ref, v_ref, do_ref, lse_ref, delta_ref, dq_ref, dk_ref, dv_ref, dq_acc):
        kb = pl.program_id(1)

        @pl.when(kb == 0)
        def _():
            dq_acc[...] = jnp.zeros_like(dq_acc)

        kv, vv = k_ref[...], v_ref[...]

        def step(qb, carry, masked):
            dk, dv = carry
            rows = pl.ds(pl.multiple_of(qb * tq, tq), tq)
            qv, dov = q_ref[rows, :], do_ref[rows, :]
            s = _dot(kv, qv, NT)
            if masked:
                s = jnp.where(_chunk_mask(s.shape, 1), s, NEG_INF)
            p = jnp.exp2(s - lse_ref[0, qb])
            dv = dv + _dot(p.astype(BF16), dov)
            dp = _dot(vv, dov, NT)
            ds = (p * (dp - delta_ref[0, qb]) * LN_2).astype(BF16)
            dk = dk + _dot(ds, qv)
            dq_acc[rows, :] += _dot(ds, kv, TN)
            return dk, dv

        carry = step(kb, (jnp.zeros((tq, HEAD_PAD), F32), jnp.zeros((tq, V_HEAD), F32)), True)
        odd = (nq - 1 - kb) % 2
        carry = lax.fori_loop(0, odd, lambda _, cr: step(kb + 1, cr, False), carry)
        first = kb + 1 + odd
        dk, dv = lax.fori_loop(0, (nq - first) // 2,
                               lambda pb, cr: step(first + 2 * pb + 1, step(first + 2 * pb, cr, False), False), carry)
        dk_ref[...] = dk.astype(BF16)
        dv_ref[...] = dv.astype(BF16)

        @pl.when(kb == nq - 1)
        def _():
            dq_ref[...] = dq_acc[...].astype(BF16)

    stat = pl.BlockSpec((1, nq, 1, tq), lambda h, j: (h, 0, 0, 0))
    return _call_with_rider(
        body, rider, name="attn_bwd", grid=(MLA_HEADS, nq),
        in_specs=[pl.BlockSpec((t, HEAD_PAD), lambda h, j: (0, h)),
                  pl.BlockSpec((tq, HEAD_PAD), lambda h, j: (j, h)),
                  pl.BlockSpec((tq, V_HEAD), lambda h, j: (j, h)),
                  pl.BlockSpec((t, V_HEAD), lambda h, j: (0, h)), stat, stat],
        out_specs=[pl.BlockSpec((t, HEAD_PAD), lambda h, j: (0, h)),
                   pl.BlockSpec((tq, HEAD_PAD), lambda h, j: (j, h)),
                   pl.BlockSpec((tq, V_HEAD), lambda h, j: (j, h))],
        out_shape=[jax.ShapeDtypeStruct((t, QK_COLS), BF16), jax.ShapeDtypeStruct((t, QK_COLS), BF16),
                   jax.ShapeDtypeStruct((t, MLA_WIDTH), BF16)],
        scratch_shapes=[pltpu.VMEM((t, HEAD_PAD), F32)], operands=(q, k, v, do, lse, delta))


HALO = 16


def _halo_spec(tm, n, step, last):
    return pl.BlockSpec((HALO, n), lambda i: (jnp.clip(i * (tm // HALO) + step, 0, last), 0))


def _shift_rows(v, prev, n):
    out = pltpu.roll(v, n, 0)
    row = lax.broadcasted_iota(jnp.int32, v.shape, 0)
    for r in range(n):
        out = jnp.where(row == r, prev[HALO - n + r:HALO - n + r + 1, :], out)
    return out


def _advance_rows(v, nxt, n):
    rows = v.shape[0]
    out = pltpu.roll(v, rows - n, 0)
    row = lax.broadcasted_iota(jnp.int32, v.shape, 0)
    for r in range(n):
        out = jnp.where(row == rows - n + r, nxt[r:r + 1, :], out)
    return out


def _conv_taps(zc, zc_prev, first):
    w = CONV_WIDTH
    u = zc[:, w:2 * w] * zc[:, 2 * w:]
    up = jnp.where(first, 0.0, zc_prev[:, w:2 * w] * zc_prev[:, 2 * w:])
    return u, _shift_rows(u, up, 1), _shift_rows(u, up, 2)


def mix_out_forward(zc, o, conv_w, og, gmat_a, gmat_b, w_out, x, gate):
    t, d = x.shape
    tm = _tile(t, ROW_TILE, 16)
    w = CONV_WIDTH

    def body(zc_ref, zp_ref, o_ref, cw_ref, og_ref, ga_ref, gb_ref, w_ref, x_ref, gate_ref,
             xo_ref, yn_ref, y_ref, ya_ref):
        zc_v = zc_ref[...].astype(F32)
        u, u1, u2 = _conv_taps(zc_v, zp_ref[...].astype(F32), pl.program_id(0) == 0)
        cw = cw_ref[...]
        ya = zc_v[:, :w] * (cw[0:1] * u2 + cw[1:2] * u1 + cw[2:3] * u)
        ya_ref[...] = ya.astype(BF16)
        ov = o_ref[...].astype(F32)
        ogv = og_ref[...]
        yn_ref[:, :w] = (ya * lax.rsqrt(_group_mean(ya * ya, ga_ref[...]) + EPS) * ogv[:, :w]).astype(BF16)
        yn_ref[:, w:] = (ov * lax.rsqrt(_group_mean(ov * ov, gb_ref[...]) + EPS) * ogv[:, w:]).astype(BF16)
        y = _dot(yn_ref[...], w_ref[...])
        y_ref[...] = y.astype(BF16)
        xo_ref[...] = x_ref[...] + gate_ref[...] * y

    def rows(n):
        return pl.BlockSpec((tm, n), lambda i: (i, 0))

    return pl.pallas_call(
        body, name="mix_out_fwd", grid=(t // tm,),
        in_specs=[rows(ZC_COLS), _halo_spec(tm, ZC_COLS, -1, t // HALO - 1), rows(MLA_WIDTH), _row(conv_w), _row(og),
                  _row(gmat_a), _row(gmat_b), _row(w_out), rows(d), _row(gate)],
        out_specs=[rows(d), rows(MIX_WIDTH), rows(d), rows(w)],
        out_shape=[jax.ShapeDtypeStruct((t, d), F32), jax.ShapeDtypeStruct((t, MIX_WIDTH), BF16),
                   jax.ShapeDtypeStruct((t, d), BF16), jax.ShapeDtypeStruct((t, w), BF16)],
        compiler_params=_params(("arbitrary",)),
    )(zc, zc, o, conv_w, og, gmat_a, gmat_b, w_out, x, gate)


def _group_norm_bwd(dyn, y, og, gmat):
    rs = lax.rsqrt(_group_mean(y * y, gmat) + EPS)
    yhat = y * rs
    d_og = jnp.sum(dyn * yhat, axis=0, keepdims=True)
    dyh = dyn * og
    return rs * (dyh - yhat * _group_mean(dyh * yhat, gmat)), d_og


def mix_out_backward(dxo, y, gate, ya, o, og, gmat_a, gmat_b, w_out):
    t, d = dxo.shape
    tm = _tile(t, ROW_TILE, 16)
    w = CONV_WIDTH

    def body(dxo_ref, y_ref, gate_ref, ya_ref, o_ref, og_ref, ga_ref, gb_ref, w_ref,
             dy_ref, dya_ref, do_ref, delta_ref, sd_ref, so_ref):
        @pl.when(pl.program_id(0) == 0)
        def _():
            sd_ref[...] = jnp.zeros_like(sd_ref)
            so_ref[...] = jnp.zeros_like(so_ref)

        dxo_v = dxo_ref[...]
        dy = (gate_ref[...] * dxo_v).astype(BF16)
        dy_ref[...] = dy
        sd_ref[0:1, :] += jnp.sum(dxo_v * y_ref[...].astype(F32), axis=0, keepdims=True)
        dyn = _dot(dy, w_ref[...], NT)
        ogv = og_ref[...]
        ov = o_ref[...].astype(F32)
        dya, d_og_a = _group_norm_bwd(dyn[:, :w], ya_ref[...].astype(F32), ogv[:, :w], ga_ref[...])
        dov, d_og_b = _group_norm_bwd(dyn[:, w:], ov, ogv[:, w:], gb_ref[...])
        dya_ref[...] = dya.astype(BF16)
        do_ref[...] = dov.astype(BF16)
        so_ref[0:1, :w] += d_og_a
        so_ref[0:1, w:] += d_og_b
        prod = dov * ov
        for h in range(MLA_HEADS):
            delta_ref[h] = jnp.sum(prod[:, h * V_HEAD:(h + 1) * V_HEAD], axis=-1, keepdims=True)

    def rows(n):
        return pl.BlockSpec((tm, n), lambda i: (i, 0))

    return pl.pallas_call(
        body, name="mix_out_bwd", grid=(t // tm,),
        in_specs=[rows(d), rows(d), _row(gate), rows(w), rows(MLA_WIDTH), _row(og), _row(gmat_a), _row(gmat_b),
                  _row(w_out)],
        out_specs=[rows(d), rows(w), rows(MLA_WIDTH), pl.BlockSpec((MLA_HEADS, tm, 1), lambda i: (0, i, 0)),
                   pl.BlockSpec((8, d), lambda i: (0, 0)), pl.BlockSpec((8, MIX_WIDTH), lambda i: (0, 0))],
        out_shape=[jax.ShapeDtypeStruct((t, d), BF16), jax.ShapeDtypeStruct((t, w), BF16),
                   jax.ShapeDtypeStruct((t, MLA_WIDTH), BF16), jax.ShapeDtypeStruct((MLA_HEADS, t, 1), F32),
                   jax.ShapeDtypeStruct((8, d), F32), jax.ShapeDtypeStruct((8, MIX_WIDTH), F32)],
        compiler_params=_params(("arbitrary",)),
    )(dxo, y, gate, ya, o, og, gmat_a, gmat_b, w_out)


def conv_backward(zc, dya, conv_w):
    t = zc.shape[0]
    tm = _tile(t, ROW_TILE, 16)
    nt = t // tm
    w = CONV_WIDTH

    def body(zc_ref, zp_ref, zn_ref, dya_ref, dn_ref, cw_ref, dzc_ref, sums_ref):
        i = pl.program_id(0)

        @pl.when(i == 0)
        def _():
            sums_ref[...] = jnp.zeros_like(sums_ref)

        zc_v = zc_ref[...].astype(F32)
        u, u1, u2 = _conv_taps(zc_v, zp_ref[...].astype(F32), i == 0)
        cw = cw_ref[...]
        dya_v = dya_ref[...].astype(F32)
        dyc = dya_v * zc_v[:, :w]
        dyc_next = jnp.where(i == nt - 1, 0.0, dn_ref[...].astype(F32) * zn_ref[:, :w].astype(F32))
        du = cw[2:3] * dyc + cw[1:2] * _advance_rows(dyc, dyc_next, 1) + cw[0:1] * _advance_rows(dyc, dyc_next, 2)
        dzc_ref[:, :w] = (dya_v * (cw[0:1] * u2 + cw[1:2] * u1 + cw[2:3] * u)).astype(BF16)
        dzc_ref[:, w:2 * w] = (du * zc_v[:, 2 * w:]).astype(BF16)
        dzc_ref[:, 2 * w:] = (du * zc_v[:, w:2 * w]).astype(BF16)
        _add_rows(sums_ref, [jnp.sum(dyc * tap, axis=0, keepdims=True) for tap in (u2, u1, u)])

    def rows(n):
        return pl.BlockSpec((tm, n), lambda i: (i, 0))

    def halo(n, step):
        return _halo_spec(tm, n, step, t // HALO - 1)

    return pl.pallas_call(
        body, name="conv_bwd", grid=(nt,),
        in_specs=[rows(ZC_COLS), halo(ZC_COLS, -1), halo(ZC_COLS, tm // HALO), rows(w), halo(w, tm // HALO),
                  _row(conv_w)],
        out_specs=[rows(ZC_COLS), pl.BlockSpec((8, w), lambda i: (0, 0))],
        out_shape=[jax.ShapeDtypeStruct((t, ZC_COLS), BF16), jax.ShapeDtypeStruct((8, w), F32)],
        compiler_params=_params(("arbitrary",)),
    )(zc, zc, zc, dya, dya, conv_w)


def _rms_bwd(dy, x, g):
    xhat, r = _rms(x)
    d_g = jnp.sum(dy * xhat, axis=0, keepdims=True)
    dxh = dy * g
    return r * (dxh - xhat * jnp.mean(dxh * xhat, axis=-1, keepdims=True)), d_g


def mla_project_backward(dq, dk, dv, zm, pos, inv_freq, qg, kvg, w_uq, w_ukv):
    t = zm.shape[0]
    tm = _tile(t, ROW_TILE, 16)

    def body(dq_ref, dk_ref, dv_ref, zm_ref, pos_ref, if_ref, qg_ref, kvg_ref, wq_ref, wkv_ref,
             dql_ref, dkvl_ref, dzm_ref, sums_ref):
        @pl.when(pl.program_id(0) == 0)
        def _():
            sums_ref[...] = jnp.zeros_like(sums_ref)

        tables = _rope_tables(pos_ref[...], if_ref[...])
        dkr = jnp.zeros((tm, LANES), F32)
        for h in range(MLA_HEADS):
            lo = h * HEAD_PAD
            dql_ref[:, lo:lo + QK_NOPE] = (dq_ref[:, lo:lo + QK_NOPE].astype(F32) * QK_FOLD).astype(BF16)
            dql_ref[:, lo + QK_NOPE:lo + HEAD_PAD] = _rope_transposed(
                dq_ref[:, lo + QK_NOPE:lo + HEAD_PAD].astype(F32) * QK_FOLD, tables).astype(BF16)
            dkvl_ref[:, h * QK_NOPE:(h + 1) * QK_NOPE] = dk_ref[:, lo:lo + QK_NOPE]
            dkr = dkr + dk_ref[:, lo + QK_NOPE:lo + HEAD_PAD].astype(F32)
        dkvl_ref[:, MLA_HEADS * QK_NOPE:] = dv_ref[...]
        zv = zm_ref[...].astype(F32)
        dqn = _dot(dql_ref[...], wq_ref[...])
        dkvn = _dot(dkvl_ref[...], wkv_ref[...])
        dcq, d_qg = _rms_bwd(dqn, zv[:, :Q_LORA], qg_ref[...])
        dckv, d_kvg = _rms_bwd(dkvn, zv[:, Q_LORA:Q_LORA + KV_LORA], kvg_ref[...])
        dzm_ref[:, :Q_LORA] = dcq.astype(BF16)
        dzm_ref[:, Q_LORA:Q_LORA + KV_LORA] = dckv.astype(BF16)
        dzm_ref[:, Q_LORA + KV_LORA:] = _rope_transposed(dkr, tables).astype(BF16)
        sums_ref[0:1, :Q_LORA] += d_qg
        sums_ref[0:1, Q_LORA:Q_LORA + KV_LORA] += d_kvg

    def rows(n):
        return pl.BlockSpec((tm, n), lambda i: (i, 0))

    return pl.pallas_call(
        body, name="mla_project_bwd", grid=(t // tm,),
        in_specs=[rows(QK_COLS), rows(QK_COLS), rows(MLA_WIDTH), rows(ZM_COLS), rows(1), _row(inv_freq),
                  _row(qg), _row(kvg), _row(w_uq), _row(w_ukv)],
        out_specs=[rows(QK_COLS), rows(QK_COLS), rows(ZM_COLS), pl.BlockSpec((8, ZM_COLS), lambda i: (0, 0))],
        out_shape=[jax.ShapeDtypeStruct((t, QK_COLS), BF16), jax.ShapeDtypeStruct((t, QK_COLS), BF16),
                   jax.ShapeDtypeStruct((t, ZM_COLS), BF16), jax.ShapeDtypeStruct((8, ZM_COLS), F32)],
        compiler_params=_params(("arbitrary",)),
    )(dq, dk, dv, zm, pos, inv_freq, qg, kvg, w_uq, w_ukv)


def mix_in_backward(dzc, dzm, w_in, x, dxo, gn, sc, gate):
    t, d = x.shape
    tm = _tile(t, ROW_TILE, 16)

    def body(dzc_ref, dzm_ref, w_ref, x_ref, dxo_ref, gn_ref, sc_ref, gate_ref, dx_ref, dy_ref, sums_ref):
        @pl.when(pl.program_id(0) == 0)
        def _():
            sums_ref[...] = jnp.zeros_like(sums_ref)

        dh = _dot(dzc_ref[...], w_ref[:ZC_COLS, :]) + _dot(dzm_ref[...], w_ref[ZC_COLS:, :])
        dx, d_sh, d_sc, d_gn = _norm_mod_bwd(dh, x_ref[...], gn_ref[...], sc_ref[...])
        dx = dxo_ref[...] + dx
        dx_ref[...] = dx
        dy_ref[...] = (0.5 * gate_ref[...] * dx).astype(BF16)
        _add_rows(sums_ref, [d_sh, d_sc, d_gn])

    def rows(n):
        return pl.BlockSpec((tm, n), lambda i: (i, 0))

    return pl.pallas_call(
        body, name="mix_in_bwd", grid=(t // tm,),
        in_specs=[rows(ZC_COLS), rows(ZM_COLS), _row(w_in), rows(d), rows(d), _row(gn), _row(sc), _row(gate)],
        out_specs=[rows(d), rows(d), pl.BlockSpec((8, d), lambda i: (0, 0))],
        out_shape=[jax.ShapeDtypeStruct((t, d), F32), jax.ShapeDtypeStruct((t, d), BF16),
                   jax.ShapeDtypeStruct((8, d), F32)],
        compiler_params=_params(("arbitrary",)),
    )(dzc, dzm, w_in, x, dxo, gn, sc, gate)


def final_loss(x, target, g, gate):
    t, d = x.shape
    tm = _tile(t, ROW_TILE, 16)

    def body(x_ref, t_ref, g_ref, gate_ref, dx_ref, dy_ref, sums_ref):
        @pl.when(pl.program_id(0) == 0)
        def _():
            sums_ref[...] = jnp.zeros_like(sums_ref)

        gv = g_ref[...]
        xhat, r = _rms(x_ref[...])
        err = xhat * gv - t_ref[...]
        dyf = err * (1.0 / d)
        dxh = dyf * gv
        dx = r * (dxh - xhat * jnp.mean(dxh * xhat, axis=-1, keepdims=True))
        dx_ref[...] = dx
        dy_ref[...] = (0.5 * gate_ref[...] * dx).astype(BF16)
        _add_rows(sums_ref, [jnp.sum(dyf * xhat, axis=0, keepdims=True),
                             jnp.sum(err * err, axis=0, keepdims=True) * (0.5 / d)])

    row = pl.BlockSpec((tm, d), lambda i: (i, 0))
    return pl.pallas_call(
        body, name="final_loss", grid=(t // tm,),
        in_specs=[row, row, _row(g), _row(gate)],
        out_specs=[row, row, pl.BlockSpec((8, d), lambda i: (0, 0))],
        out_shape=[jax.ShapeDtypeStruct((t, d), F32), jax.ShapeDtypeStruct((t, d), BF16),
                   jax.ShapeDtypeStruct((8, d), F32)],
        compiler_params=_params(("arbitrary",)),
    )(x, target, g, gate)


def adamw(w, g, m, v, name):
    r, n = w.shape
    tr = _tile(r, max(8, (1 << 19) // n), 8)

    def body(w_ref, g_ref, m_ref, v_ref, d_ref, mo_ref, vo_ref):
        gv = g_ref[...]
        m_new = ADAM_B1 * m_ref[...] + (1.0 - ADAM_B1) * gv
        v_new = ADAM_B2 * v_ref[...] + (1.0 - ADAM_B2) * (gv * gv)
        m_hat = m_new / (1.0 - ADAM_B1 ** ADAM_STEP)
        v_hat = v_new / (1.0 - ADAM_B2 ** ADAM_STEP)
        d_ref[...] = -ADAM_LR * (m_hat / (jnp.sqrt(v_hat) + ADAM_EPS) + ADAM_WD * w_ref[...])
        mo_ref[...] = m_new
        vo_ref[...] = v_new

    blk = pl.BlockSpec((tr, n), lambda i: (i, 0))
    shape = jax.ShapeDtypeStruct((r, n), F32)
    return pl.pallas_call(
        body, name=name, grid=(r // tr,), in_specs=[blk] * 4, out_specs=[blk] * 3, out_shape=[shape] * 3,
        compiler_params=_params(("arbitrary",)),
    )(w, g, m, v)


def _pad_to(v, n):
    return jnp.pad(v, (0, n - v.shape[0]))


def _pad_heads(w, axis_len):
    n = w.shape[1]
    return jnp.pad(w.reshape(MLA_HEADS, axis_len, n), ((0, 0), (0, HEAD_PAD - axis_len), (0, 0))).reshape(-1, n)


def _swap_head_parts(w, inner, outer):
    n = w.shape[1]
    return w.reshape(outer, inner, QK_NOPE, n).transpose(1, 0, 2, 3).reshape(-1, n)


def kernel(x, c, positions, ada_w, ada_b, norm_ffn1_g, ffn1_w1, ffn1_w3, ffn1_w2, norm_mix_g, w_in, conv_w, q_norm_g, w_uq, kv_norm_g, w_ukv, out_norm_g, w_out, norm_ffn2_g, ffn2_w1, ffn2_w3, ffn2_w2, final_norm_g, loss_target, m_ada_w, m_ada_b, m_norm_ffn1_g, m_ffn1_w1, m_ffn1_w3, m_ffn1_w2, m_norm_mix_g, m_w_in, m_conv_w, m_q_norm_g, m_w_uq, m_kv_norm_g, m_w_ukv, m_out_norm_g, m_w_out, m_norm_ffn2_g, m_ffn2_w1, m_ffn2_w3, m_ffn2_w2, m_final_norm_g, v_ada_w, v_ada_b, v_norm_ffn1_g, v_ffn1_w1, v_ffn1_w3, v_ffn1_w2, v_norm_mix_g, v_w_in, v_conv_w, v_q_norm_g, v_w_uq, v_kv_norm_g, v_w_ukv, v_out_norm_g, v_w_out, v_norm_ffn2_g, v_ffn2_w1, v_ffn2_w3, v_ffn2_w2, v_final_norm_g):
    t, d = x.shape[1], x.shape[2]
    f = ffn1_w2.shape[1] * N_DEV
    me = 4 * lax.axis_index("x") + 2 * lax.axis_index("y") + lax.axis_index("c")
    my_c = lax.axis_index("c")
    my_chip = 2 * lax.axis_index("x") + lax.axis_index("y")
    xs = x[0]
    n_ada = ada_w.shape[2]
    cw_n = conv_w.shape[2]

    c_rows = jnp.broadcast_to(c, (8, d))
    conv_rows = jnp.pad(conv_w[0], ((0, 8 - CONV_K), (0, LANES - cw_n)))
    ffn1_blocks = jnp.stack([ffn1_w1[0].T, ffn1_w3[0].T, ffn1_w2[0]]).astype(BF16)
    ffn2_blocks = jnp.stack([ffn2_w1[0].T, ffn2_w3[0].T, ffn2_w2[0]]).astype(BF16)
    c_all, conv_all, ffn1_all = all_gather([c_rows, conv_rows, ffn1_blocks], [0, 0, 1], "gather_first")
    c_all = c_all[:, 0, :]
    conv_full8 = conv_all[:, :, :cw_n].transpose(1, 0, 2).reshape(8, CONV_WIDTH)
    ffn1_ws = ffn1_all.reshape(3, f, d)
    gather_rest = riding_gather(
        [ffn2_blocks, w_in[0].T.astype(BF16), w_uq[0].T.astype(BF16), w_ukv[0].T.astype(BF16), w_out[0].astype(BF16)],
        [1, 0, 0, 0, 0])

    ada_b_cols = lax.dynamic_slice_in_dim(ada_b, me * n_ada, n_ada, axis=1)
    mod_cols = ada_forward(c_all, ada_w[0], ada_b_cols)
    mod_all, = all_gather([mod_cols], [0], "gather_mod")
    mod = lax.dynamic_index_in_dim(mod_all, me, axis=1, keepdims=False).reshape(N_MOD, 1, d)
    sh1, sc1, g1, sh2, sc2, g2, sh3, sc3, g3 = [mod[i] for i in range(N_MOD)]

    gf = final_norm_g.reshape(1, d)
    x1, h1, a1, b1, y1, *gathered = ffn_forward(xs, norm_ffn1_g, sc1, sh1, g1, ffn1_ws, 0, "ffn1_fwd", gather_rest)
    ffn2_ws = gathered[0].reshape(3, f, d)
    w_in_p = jnp.pad(gathered[1].reshape(IN_COLS, d), ((0, ZC_COLS + ZM_COLS - IN_COLS), (0, 0)))
    w_uq_p = _pad_heads(gathered[2].reshape(-1, Q_LORA), QK_NOPE + QK_ROPE)
    w_ukv_p = _swap_head_parts(gathered[3].reshape(-1, KV_LORA), 2, MLA_HEADS)
    w_out_f = gathered[4].reshape(MIX_WIDTH, d)
    h2, zc, zm = mix_in_forward(x1, norm_mix_g, sc2, sh2, w_in_p)
    pos = positions[0].astype(F32).reshape(t, 1)
    inv_freq = ROPE_THETA ** (-jnp.arange(0, QK_ROPE, 2, dtype=F32) / QK_ROPE)
    inv_freq = jnp.concatenate([inv_freq, inv_freq, jnp.zeros((LANES - QK_ROPE,), F32)]).reshape(1, LANES)
    qn, kvn, q, k, v = mla_project(zm, pos, inv_freq, q_norm_g, kv_norm_g, w_uq_p, w_ukv_p)
    o, lse = attention_forward(q, k, v)
    lane = jnp.arange(CONV_WIDTH)
    gmat_a = (lane[:, None] // (CONV_WIDTH // CONV_GROUPS) == lane[None, :] // (CONV_WIDTH // CONV_GROUPS))
    gmat_a = (gmat_a / (CONV_WIDTH // CONV_GROUPS)).astype(BF16)
    gmat_b = ((lane[:, None] // V_HEAD == lane[None, :] // V_HEAD) / V_HEAD).astype(BF16)
    x2, yn, y2, ya = mix_out_forward(zc, o, conv_full8, out_norm_g, gmat_a, gmat_b, w_out_f, x1, g2)
    x3, h3, a3, b3, y3 = ffn_forward(x2, norm_ffn2_g, sc3, sh3, g3, ffn2_ws, 0, "ffn2_fwd")
    dx3, dy3, sums_f = final_loss(x3, loss_target[0], gf, g3)

    chip_idx = jnp.bitwise_xor(my_chip, jnp.array([0, 2, 1, 3], jnp.int32)).astype(jnp.int32)
    src_idx = (2 * chip_idx + my_c).astype(jnp.int32)

    def chip_sums(tag, named):
        g8 = [g.reshape(N_DEV, g.shape[0] // N_DEV, g.shape[1]) for _, g in named]
        got = exchange_sibling(g8, "rs_sibling_" + tag)
        return [add_sibling(g, r, src_idx, chip_idx, "rs_add_" + n) for g, r, (n, _) in zip(g8, got, named)]

    da3, db3, u3 = ffn_backward_gate(dy3, a3, b3, ffn2_ws, 0, "ffn2_bwd_gate")
    dx2, sums_3 = ffn_backward_norm(da3, db3, dx3, x2, y3, norm_ffn2_g, sc3, ffn2_ws, 0, "ffn2_bwd_norm")
    ffn2_named = [("ffn2_w1", matmul_tn(da3, h3, "ffn2_gw1")), ("ffn2_w3", matmul_tn(db3, h3, "ffn2_gw3")),
                  ("ffn2_w2", matmul_tn(u3, dy3, "ffn2_gw2"))]
    ffn2_sums = chip_sums("ffn2", ffn2_named)
    dy2, dya, do, delta, sums_2d, sums_2o = mix_out_backward(dx2, y2, g2, ya, o, out_norm_g, gmat_a, gmat_b, w_out_f)
    g_w_out = matmul_tn(yn, dy2, "gw_out")
    nq = t // _tile(t, ATTN_TILE, CHUNK)
    stat_shape = (MLA_HEADS, nq, 1, t // nq)
    dq, dk, dv, *ffn2_got = attention_backward(q, k, v, do, lse.reshape(stat_shape), delta.reshape(stat_shape),
                                               riding_exchange([s[1] for s in ffn2_sums]))
    dzc, sums_c = conv_backward(zc, dya, conv_full8)
    dql, dkvl, dzm, sums_m = mla_project_backward(dq, dk, dv, zm, pos, inv_freq, q_norm_g, kv_norm_g, w_uq_p, w_ukv_p)
    g_w_uq_p = matmul_tn(dql, qn, "gw_uq")
    g_w_ukv_p = matmul_tn(dkvl, kvn, "gw_ukv")
    g_w_in = jnp.concatenate([matmul_tn(dzc, h2, "gw_in_conv"), matmul_tn(dzm, h2, "gw_in_mla")])[:IN_COLS]
    g_w_uq = g_w_uq_p.reshape(MLA_HEADS, HEAD_PAD, Q_LORA)[:, :QK_NOPE + QK_ROPE].reshape(-1, Q_LORA)
    g_w_ukv = _swap_head_parts(g_w_ukv_p, MLA_HEADS, 2)
    mix_named = [("w_in", g_w_in), ("w_uq", g_w_uq), ("w_ukv", g_w_ukv), ("w_out", g_w_out)]
    mix_sums = chip_sums("mix", mix_named)
    dx1, dy1, sums_1m = mix_in_backward(dzc, dzm, w_in_p, x1, dx2, norm_mix_g, sc2, g1)
    da1, db1, u1, *mix_got = ffn_backward_gate(dy1, a1, b1, ffn1_ws, 0, "ffn1_bwd_gate",
                                               riding_exchange([s[1] for s in mix_sums]))
    ffn1_named = [("ffn1_w1", matmul_tn(da1, h1, "ffn1_gw1")), ("ffn1_w3", matmul_tn(db1, h1, "ffn1_gw3")),
                  ("ffn1_w2", matmul_tn(u1, dy1, "ffn1_gw2"))]
    ffn1_sums = chip_sums("ffn1", ffn1_named)
    dx0, sums_1, *ffn1_got = ffn_backward_norm(da1, db1, dx1, xs, y1, norm_ffn1_g, sc1, ffn1_ws, 0, "ffn1_bwd_norm",
                                               riding_exchange([s[1] for s in ffn1_sums]))
    transposed = {"ffn1_w1", "ffn1_w3", "ffn2_w1", "ffn2_w3", "w_in", "w_uq", "w_ukv"}
    g_sh = {}
    for named, group_sums, group_got in ((ffn2_named, ffn2_sums, ffn2_got), (mix_named, mix_sums, mix_got),
                                         (ffn1_named, ffn1_sums, ffn1_got)):
        for (n, _), (own, _), got in zip(named, group_sums, group_got):
            g_rows = add_received(own, got, "rs_sum_" + n)
            g_sh[n] = g_rows.T if n in transposed else g_rows

    dmod = jnp.concatenate([sums_1[0], sums_1[1], sums_1[2], sums_1m[0], sums_1m[1], sums_2d[0],
                            sums_3[0], sums_3[1], sums_3[2]])
    pieces = [dmod, sums_1[3], sums_1m[2], sums_m[0, :Q_LORA], sums_m[0, Q_LORA:Q_LORA + KV_LORA], sums_2o[0],
              sums_3[3], sums_f[0], sums_f[1], sums_c[:CONV_K].reshape(-1)]
    plens = [p.shape[0] for p in pieces]
    poffs = [sum(plens[:i]) for i in range(len(plens))]
    vec_len = -(-sum(plens) // 1024) * 1024
    vec = _pad_to(jnp.concatenate(pieces), vec_len).reshape(-1, LANES)
    vec_all, = all_gather([vec], [0], "gather_sums")
    tot = sum_devices(vec_all).reshape(-1)
    g_ada_b, g_n1, g_nmix, g_qg, g_kvg, g_og, g_n3, g_gf, loss_lanes, g_conv_full = [
        tot[o:o + n] for o, n in zip(poffs, plens)]
    loss = sum_lanes(loss_lanes.reshape(1, d))[0, 0]
    g_conv = lax.dynamic_slice_in_dim(g_conv_full.reshape(CONV_K, CONV_WIDTH), me * cw_n, cw_n, axis=1)
    dmod_all = vec_all.reshape(N_DEV, vec_len)[:, :N_MOD * d]
    dmod_cols = lax.dynamic_slice_in_dim(dmod_all, me * n_ada, n_ada, axis=1)
    g_ada_w = ada_backward(jnp.pad(c_all, ((0, 8), (0, 0))), jnp.pad(dmod_cols, ((0, 8), (0, 0))))

    def update(name, w, g, m, v):
        shape = w.shape
        two_d = (-1, shape[-1])
        dlt, nm, nv = adamw(w.reshape(two_d), g.reshape(two_d), m.reshape(two_d), v.reshape(two_d), "adamw_" + name)
        return g.reshape(shape), dlt.reshape(shape), nm.reshape(shape), nv.reshape(shape)

    res = {}
    res["ada_w"] = update("ada_w", ada_w, g_ada_w, m_ada_w, v_ada_w)
    big = [("ffn1_w1", ffn1_w1, m_ffn1_w1, v_ffn1_w1), ("ffn1_w3", ffn1_w3, m_ffn1_w3, v_ffn1_w3),
           ("ffn2_w1", ffn2_w1, m_ffn2_w1, v_ffn2_w1), ("ffn2_w3", ffn2_w3, m_ffn2_w3, v_ffn2_w3),
           ("w_in", w_in, m_w_in, v_w_in), ("w_uq", w_uq, m_w_uq, v_w_uq), ("w_ukv", w_ukv, m_w_ukv, v_w_ukv),
           ("ffn1_w2", ffn1_w2, m_ffn1_w2, v_ffn1_w2), ("ffn2_w2", ffn2_w2, m_ffn2_w2, v_ffn2_w2),
           ("w_out", w_out, m_w_out, v_w_out)]
    for name, w, m, v in big:
        res[name] = update(name, w, g_sh[name], m, v)
    smalls = [("ada_b", ada_b, g_ada_b, m_ada_b, v_ada_b),
              ("norm_ffn1_g", norm_ffn1_g, g_n1, m_norm_ffn1_g, v_norm_ffn1_g),
              ("norm_mix_g", norm_mix_g, g_nmix, m_norm_mix_g, v_norm_mix_g),
              ("conv_w", conv_w, g_conv, m_conv_w, v_conv_w),
              ("q_norm_g", q_norm_g, g_qg, m_q_norm_g, v_q_norm_g),
              ("kv_norm_g", kv_norm_g, g_kvg, m_kv_norm_g, v_kv_norm_g),
              ("out_norm_g", out_norm_g, g_og, m_out_norm_g, v_out_norm_g),
              ("norm_ffn2_g", norm_ffn2_g, g_n3, m_norm_ffn2_g, v_norm_ffn2_g),
              ("final_norm_g", final_norm_g, g_gf, m_final_norm_g, v_final_norm_g)]
    slens = [w.size for _, w, _, _, _ in smalls]
    soffs = [sum(slens[:i]) for i in range(len(slens))]
    s_len = -(-sum(slens) // 1024) * 1024

    def pack_small(i):
        return _pad_to(jnp.concatenate([s[i].reshape(-1) for s in smalls]), s_len).reshape(8, -1)

    s_out = adamw(pack_small(1), pack_small(2), pack_small(3), pack_small(4), "adamw_small")
    for (name, w, g, _, _), o, n in zip(smalls, soffs, slens):
        res[name] = (g.reshape(w.shape),) + tuple(a.reshape(-1)[o:o + n].reshape(w.shape) for a in s_out)

    order = ["ada_w", "ada_b", "norm_ffn1_g", "ffn1_w1", "ffn1_w3", "ffn1_w2", "norm_mix_g", "w_in", "conv_w",
             "q_norm_g", "w_uq", "kv_norm_g", "w_ukv", "out_norm_g", "w_out", "norm_ffn2_g", "ffn2_w1", "ffn2_w3",
             "ffn2_w2", "final_norm_g"]
    return (loss, dx0.reshape(x.shape), *[res[n][0] for n in order], *[res[n][1] for n in order],
            *[res[n][2] for n in order], *[res[n][3] for n in order])
```

```python
import functools

import jax
import jax.numpy as jnp
from jax import lax
from jax.experimental import pallas as pl
from jax.experimental.pallas import tpu as pltpu

F32 = jnp.float32
BF16 = jnp.bfloat16
MESH_ID = pl.DeviceIdType.MESH
N_DEV = 8

EPS = 1e-6
CHUNK = 64
N_MOD = 9
CONV_WIDTH = 512
CONV_GROUPS = 8
CONV_K = 3
MLA_HEADS = 4
QK_NOPE = 128
QK_ROPE = 64
V_HEAD = 128
Q_LORA = 384
KV_LORA = 256
ROPE_THETA = 10000.0
MLA_WIDTH = MLA_HEADS * V_HEAD
MIX_WIDTH = CONV_WIDTH + MLA_WIDTH
IN_COLS = 3 * CONV_WIDTH + Q_LORA + KV_LORA + QK_ROPE
ZC_COLS = 3 * CONV_WIDTH
ZM_COLS = Q_LORA + KV_LORA + 128
HEAD_PAD = 256
QK_COLS = MLA_HEADS * HEAD_PAD
ATTN_SCALE = (QK_NOPE + QK_ROPE) ** -0.5
LOG2_E = 1.4426950408889634
LN_2 = 0.6931471805599453
QK_FOLD = ATTN_SCALE * LOG2_E
NEG_INF = -1e30

ADAM_LR = 0.001
ADAM_B1 = 0.9
ADAM_B2 = 0.999
ADAM_EPS = 1e-08
ADAM_WD = 0.01
ADAM_STEP = 10

LANES = 128
MXU_COLS = 256
VMEM_LIMIT = 56 * 1024 * 1024
ROW_TILE = 512
FFN_FWD_TILE = (1024, 256)
FFN_BWD_TILE = (512, 1408)
GRAD_TILE = 1408
GRAD_DEPTH = 1024
SUM_ROWS = 256
ATTN_TILE = 512

NN = (((1,), (0,)), ((), ()))
NT = (((1,), (1,)), ((), ()))
TN = (((0,), (0,)), ((), ()))


def _dot(a, b, dims=NN):
    return lax.dot_general(a, b, dims, preferred_element_type=F32)


def _tile(n, cap, mult=LANES):
    best = None
    for t in range(mult, min(n, cap) + 1, mult):
        if n % t == 0:
            best = t
    return n if best is None else best


def _params(sem=None):
    return pltpu.CompilerParams(dimension_semantics=sem, vmem_limit_bytes=VMEM_LIMIT)


def _row(v):
    return pl.BlockSpec(v.shape, lambda *_: (0,) * v.ndim)


def _sigmoid(x):
    return 0.5 * jnp.tanh(0.5 * x) + 0.5


def _rms(x):
    r = lax.rsqrt(jnp.mean(x * x, axis=-1, keepdims=True) + EPS)
    return x * r, r


def _norm_mod_bwd(dh, x, gn, sc):
    xhat, r = _rms(x)
    d_sh = jnp.sum(dh, axis=0, keepdims=True)
    d_sc = jnp.sum(dh * (xhat * gn), axis=0, keepdims=True)
    dxn = dh * (1.0 + sc)
    d_gn = jnp.sum(dxn * xhat, axis=0, keepdims=True)
    dxh = dxn * gn
    dx = r * (dxh - xhat * jnp.mean(dxh * xhat, axis=-1, keepdims=True))
    return dx, d_sh, d_sc, d_gn


def _group_mean(v, gmat):
    return _dot(v.astype(BF16), gmat)


def _add_rows(ref, rows):
    for r, v in enumerate(rows):
        ref[r:r + 1, :] += v


def _window(ref, axis, j):
    return ref.at[(slice(None),) * axis + (j,)]


def _any_specs(n):
    return [pl.BlockSpec(memory_space=pl.ANY)] * n


def all_gather(blocks, axes, name):
    n_arr = len(blocks)

    def body(*refs):
        start, forward, finish = _gather_steps(refs[:n_arr], refs[n_arr:2 * n_arr], axes, *refs[2 * n_arr:])
        start()
        for j in range(3):
            forward(j)
        finish()

    return pl.pallas_call(
        body, name=name, out_shape=_gathered_shapes(blocks, axes),
        in_specs=_any_specs(n_arr), out_specs=_any_specs(n_arr), scratch_shapes=_gather_sems(n_arr),
    )(*blocks)


def _gathered_shapes(blocks, axes):
    return [jax.ShapeDtypeStruct(b.shape[:ax] + (N_DEV,) + b.shape[ax:], b.dtype) for b, ax in zip(blocks, axes)]


def _gather_sems(n_arr):
    return [pltpu.SemaphoreType.DMA((7, n_arr)), pltpu.SemaphoreType.DMA((7, n_arr)), pltpu.SemaphoreType.DMA((n_arr,))]


def _gather_steps(ins, outs, axes, send_sems, recv_sems, local_sems):
    arrays = range(len(ins))
    x, y, c = lax.axis_index("x"), lax.axis_index("y"), lax.axis_index("c")
    me, sibling = (x, y, c), (x, y, 1 - c)
    chips = [(1 - x, y), (x, 1 - y), (1 - x, 1 - y)]

    def slot(a, px, py, pc):
        return _window(outs[a], axes[a], 4 * px + 2 * py + pc)

    def copy(a, k, block, to, src=None):
        return pltpu.make_async_remote_copy(
            src_ref=slot(a, *block) if src is None else src, dst_ref=slot(a, *block),
            send_sem=send_sems.at[k, a], recv_sem=recv_sems.at[k, a], device_id=to, device_id_type=MESH_ID)

    def mine(a):
        return pltpu.make_async_copy(ins[a], slot(a, *me), local_sems.at[a])

    def first():
        return ([copy(a, 0, me, sibling, src=ins[a]) for a in arrays]
                + [copy(a, 1 + j, me, (*chip, c), src=ins[a]) for j, chip in enumerate(chips) for a in arrays])

    def passed(j):
        return [copy(a, 4 + j, (*chips[j], c), sibling) for a in arrays]

    def start():
        for a in arrays:
            mine(a).start()
        for cp in first():
            cp.start()

    def forward(j):
        for a, cp in zip(arrays, passed(j)):
            copy(a, 1 + j, (*chips[j], c), me).wait_recv()
            cp.start()

    def finish():
        for a in arrays:
            copy(a, 0, sibling, me).wait_recv()
        for j, chip in enumerate(chips):
            for a in arrays:
                copy(a, 4 + j, (*chip, 1 - c), me).wait_recv()
        for cp in first() + passed(0) + passed(1) + passed(2):
            cp.wait_send()
        for a in arrays:
            mine(a).wait()

    return start, forward, finish


def exchange_sibling(grads, name):
    n_arr = len(grads)

    def body(*refs):
        start, finish = _sibling_exchange_steps(refs[:n_arr], refs[n_arr:2 * n_arr], *refs[2 * n_arr:])
        start()
        finish()

    return pl.pallas_call(
        body, name=name, out_shape=_sibling_shapes(grads),
        in_specs=_any_specs(n_arr), out_specs=_any_specs(n_arr), scratch_shapes=_exchange_sems(n_arr),
    )(*grads)


def _sibling_shapes(grads):
    return [jax.ShapeDtypeStruct((4,) + g.shape[1:], g.dtype) for g in grads]


def _exchange_sems(n_arr):
    return [pltpu.SemaphoreType.DMA((n_arr,)), pltpu.SemaphoreType.DMA((n_arr,))]


def _sibling_exchange_steps(ins, outs, send_sems, recv_sems):
    x, y, c = lax.axis_index("x"), lax.axis_index("y"), lax.axis_index("c")

    def copy(a, src, dst):
        return pltpu.make_async_remote_copy(
            src_ref=src, dst_ref=dst, send_sem=send_sems.at[a], recv_sem=recv_sems.at[a],
            device_id=(x, y, 1 - c), device_id_type=MESH_ID)

    def start():
        for a in range(len(ins)):
            for k in range(4):
                copy(a, ins[a].at[2 * k + (1 - c)], outs[a].at[k]).start()

    def finish():
        whole = [copy(a, ins[a].at[pl.ds(0, 4)], outs[a]) for a in range(len(ins))]
        for cp in whole:
            cp.wait_recv()
        for cp in whole:
            cp.wait_send()

    return start, finish


def _chip_exchange_steps(ins, outs, send_sems, recv_sems):
    x, y, c = lax.axis_index("x"), lax.axis_index("y"), lax.axis_index("c")
    chips = [(1 - x, y), (x, 1 - y), (1 - x, 1 - y)]

    def copy(a, src, dst, chip):
        return pltpu.make_async_remote_copy(
            src_ref=src, dst_ref=dst, send_sem=send_sems.at[a], recv_sem=recv_sems.at[a],
            device_id=(*chip, c), device_id_type=MESH_ID)

    def start():
        for a in range(len(ins)):
            for j, chip in enumerate(chips):
                copy(a, ins[a].at[j], outs[a].at[j], chip).start()

    def finish():
        whole = [copy(a, ins[a], outs[a], chips[0]) for a in range(len(ins))]
        for cp in whole:
            cp.wait_recv()
        for cp in whole:
            cp.wait_send()

    return start, finish


def riding_gather(blocks, axes):
    def phases(ins, outs, *sems):
        start, forward, finish = _gather_steps(ins, outs, axes, *sems)
        return [start] + [functools.partial(forward, j) for j in range(3)] + [finish]

    return dict(operands=blocks, out_shape=_gathered_shapes(blocks, axes), sems=_gather_sems(len(blocks)),
                phases=phases, when=("first", "late0", "late1", "late2", "last"))


def riding_exchange(parts):
    def phases(ins, outs, *sems):
        return list(_chip_exchange_steps(ins, outs, *sems))

    return dict(operands=parts, out_shape=[jax.ShapeDtypeStruct(p.shape, p.dtype) for p in parts],
                sems=_exchange_sems(len(parts)), phases=phases, when=("first", "last"))


def riding_sibling(grads):
    def phases(ins, outs, *sems):
        return list(_sibling_exchange_steps(ins, outs, *sems))

    return dict(operands=grads, out_shape=_sibling_shapes(grads), sems=_exchange_sems(len(grads)),
                phases=phases, when=("first", "last"))


def _call_with_rider(body, rider, *, name, grid, in_specs, out_specs, out_shape, scratch_shapes, operands):
    params = _params(("arbitrary",) * len(grid))
    if rider is None:
        return pl.pallas_call(body, name=name, grid=grid, in_specs=in_specs, out_specs=out_specs,
                              out_shape=out_shape, scratch_shapes=scratch_shapes, compiler_params=params)(*operands)
    n_in, n_out, n_scr, k = len(in_specs), len(out_specs), len(scratch_shapes), len(rider["operands"])
    at = {"first": (0,) * len(grid), "last": tuple(g - 1 for g in grid)}
    if "late0" in rider["when"]:
        rows, cols = grid
        assert cols >= 3
        at.update({"late%d" % j: (max(rows - 2, 0), j) for j in range(3)})

    def wrapped(*refs):
        ins, c_in = refs[:n_in], refs[n_in:n_in + k]
        outs, c_out = refs[n_in + k:n_in + k + n_out], refs[n_in + k + n_out:n_in + 2 * k + n_out]
        scratch, sems = refs[n_in + 2 * k + n_out:n_in + 2 * k + n_out + n_scr], refs[n_in + 2 * k + n_out + n_scr:]
        pos = [pl.program_id(axis) for axis in range(len(grid))]

        def here(key):
            return functools.reduce(jnp.logical_and, [p == v for p, v in zip(pos, at[key])])

        phases = rider["phases"](c_in, c_out, *sems)
        for fn, key in zip(phases, rider["when"]):
            if key != "last":
                pl.when(here(key))(fn)
        body(*ins, *outs, *scratch)
        pl.when(here("last"))(phases[-1])

    return pl.pallas_call(
        wrapped, name=name, grid=grid,
        in_specs=list(in_specs) + _any_specs(k), out_specs=list(out_specs) + _any_specs(k),
        out_shape=list(out_shape) + rider["out_shape"], scratch_shapes=list(scratch_shapes) + rider["sems"],
        compiler_params=params)(*operands, *rider["operands"])


def add_sibling(g8, got, src_idx, chip_idx, name):
    _, r, n = g8.shape
    tr = _tile(r, SUM_ROWS, 16)

    def body(si_ref, ci_ref, g0_ref, g1_ref, g2_ref, g3_ref, got_ref, own_ref, send_ref):
        own_ref[...] = g0_ref[0] + got_ref[ci_ref[0]]
        for j, g_ref in enumerate((g1_ref, g2_ref, g3_ref)):
            send_ref[j] = (g_ref[0] + got_ref[ci_ref[j + 1]]).astype(BF16)

    def mine(j):
        return pl.BlockSpec((1, tr, n), lambda i, si, ci: (si[j], i, 0))

    return pl.pallas_call(
        body, name=name,
        out_shape=[jax.ShapeDtypeStruct((r, n), F32), jax.ShapeDtypeStruct((3, r, n), BF16)],
        grid_spec=pltpu.PrefetchScalarGridSpec(
            num_scalar_prefetch=2, grid=(r // tr,),
            in_specs=[mine(0), mine(1), mine(2), mine(3), pl.BlockSpec((4, tr, n), lambda i, si, ci: (0, i, 0))],
            out_specs=[pl.BlockSpec((tr, n), lambda i, si, ci: (i, 0)),
                       pl.BlockSpec((3, tr, n), lambda i, si, ci: (0, i, 0))]),
        compiler_params=_params(("arbitrary",)),
    )(src_idx, chip_idx, g8, g8, g8, g8, got)


def add_received(own, got, name):
    r, n = own.shape
    tr = _tile(r, SUM_ROWS, 16)

    def body(a_ref, b_ref, o_ref):
        acc = a_ref[...]
        for j in range(3):
            acc = acc + b_ref[j].astype(F32)
        o_ref[...] = acc

    return pl.pallas_call(
        body, name=name,
        out_shape=jax.ShapeDtypeStruct((r, n), F32),
        grid=(r // tr,),
        in_specs=[pl.BlockSpec((tr, n), lambda i: (i, 0)), pl.BlockSpec((3, tr, n), lambda i: (0, i, 0))],
        out_specs=pl.BlockSpec((tr, n), lambda i: (i, 0)),
        compiler_params=_params(("arbitrary",)),
    )(own, got)


def sum_devices(g):
    def body(g_ref, o_ref):
        acc = g_ref[0]
        for j in range(1, N_DEV):
            acc = acc + g_ref[j]
        o_ref[...] = acc

    return pl.pallas_call(body, name="sum_devices", out_shape=jax.ShapeDtypeStruct(g.shape[1:], F32))(g)


def sum_lanes(v):
    def body(v_ref, o_ref):
        o_ref[...] = jnp.broadcast_to(jnp.sum(v_ref[...], axis=-1, keepdims=True), (1, LANES))

    return pl.pallas_call(body, name="sum_lanes", out_shape=jax.ShapeDtypeStruct((1, LANES), F32))(v)


def ada_forward(c_all, ada_w, ada_b_cols):
    nb, n = c_all.shape[0], ada_w.shape[1]

    def body(c_ref, w_ref, b_ref, o_ref):
        cv = c_ref[...]
        s = (cv * jax.nn.sigmoid(cv)).astype(BF16)
        o_ref[...] = _dot(s, w_ref[...].astype(BF16)) + b_ref[...]

    return pl.pallas_call(body, name="ada_fwd", out_shape=jax.ShapeDtypeStruct((nb, n), F32),
                          compiler_params=_params())(c_all, ada_w, ada_b_cols)


def ada_backward(c_all16, dmod16):
    d, n = c_all16.shape[1], dmod16.shape[1]

    def body(c_ref, g_ref, o_ref):
        cv = c_ref[...]
        s = (cv * jax.nn.sigmoid(cv)).astype(BF16)
        o_ref[...] = _dot(s, g_ref[...].astype(BF16), TN)

    return pl.pallas_call(body, name="ada_bwd", out_shape=jax.ShapeDtypeStruct((d, n), F32),
                          compiler_params=_params())(c_all16, dmod16)


def ffn_forward(x, gn, sc, sh, gate, ws, first, name, rider=None):
    t, d = x.shape
    f = ws.shape[1]
    tm, tf = _tile(t, FFN_FWD_TILE[0], 16), _tile(f, FFN_FWD_TILE[1])
    nf = f // tf

    def body(x_ref, gn_ref, sc_ref, sh_ref, gate_ref, w1_ref, w3_ref, w2_ref,
             xo_ref, h_ref, a_ref, b_ref, y_ref, hs, acc):
        j = pl.program_id(1)

        @pl.when(j == 0)
        def _():
            xhat, _ = _rms(x_ref[...])
            h = (xhat * gn_ref[...] * (1.0 + sc_ref[...]) + sh_ref[...]).astype(BF16)
            hs[...] = h
            h_ref[...] = h
            acc[...] = jnp.zeros_like(acc)

        h = hs[...]
        a = _dot(h, w1_ref[...], NT)
        b = _dot(h, w3_ref[...], NT)
        a_ref[...] = a.astype(BF16)
        b_ref[...] = b.astype(BF16)
        u = (a * _sigmoid(a) * b).astype(BF16)
        acc[...] += _dot(u, w2_ref[...])

        @pl.when(j == nf - 1)
        def _():
            y = acc[...]
            y_ref[...] = y.astype(BF16)
            xo_ref[...] = x_ref[...] + 0.5 * gate_ref[...] * y

    row = pl.BlockSpec((tm, d), lambda i, j: (i, 0))
    vec = pl.BlockSpec((1, d), lambda i, j: (0, 0))
    wide = pl.BlockSpec((tm, tf), lambda i, j: (i, j))
    return _call_with_rider(
        body, rider, name=name, grid=(t // tm, nf),
        in_specs=[row, vec, vec, vec, vec] + _ffn_weight_specs(first, tf, d),
        out_specs=[row, row, wide, wide, row],
        out_shape=[jax.ShapeDtypeStruct((t, d), F32), jax.ShapeDtypeStruct((t, d), BF16),
                   jax.ShapeDtypeStruct((t, f), BF16), jax.ShapeDtypeStruct((t, f), BF16),
                   jax.ShapeDtypeStruct((t, d), BF16)],
        scratch_shapes=[pltpu.VMEM((tm, d), BF16), pltpu.VMEM((tm, d), F32)],
        operands=(x, gn, sc, sh, gate, ws, ws, ws))


def _ffn_weight_specs(first, tf, d):
    return [pl.BlockSpec((None, tf, d), lambda i, j, w=first + k: (w, j, 0)) for k in range(3)]


def ffn_backward_gate(dy, a, b, ws, first, name, rider=None):
    t, d = dy.shape
    f = ws.shape[1]
    tm, tf = _tile(t, FFN_BWD_TILE[0], 16), _tile(f, FFN_BWD_TILE[1])
    nf = f // tf

    def gate_body(dy_ref, a_ref, b_ref, w2_ref, da_ref, db_ref, u_ref):
        du = _dot(dy_ref[...], w2_ref[...], NT)
        av = a_ref[...].astype(F32)
        bv = b_ref[...].astype(F32)
        s = _sigmoid(av)
        sa = av * s
        da_ref[...] = (du * bv * (s + sa * (1.0 - s))).astype(BF16)
        db_ref[...] = (du * sa).astype(BF16)
        u_ref[...] = (sa * bv).astype(BF16)

    hidden = jax.ShapeDtypeStruct((t, f), BF16)
    wide_t = pl.BlockSpec((tm, tf), lambda j, i: (i, j))
    return _call_with_rider(
        gate_body, rider, name=name, grid=(nf, t // tm),
        in_specs=[pl.BlockSpec((tm, d), lambda j, i: (i, 0)), wide_t, wide_t,
                  pl.BlockSpec((None, tf, d), lambda j, i: (first + 2, j, 0))],
        out_specs=[wide_t, wide_t, wide_t], out_shape=[hidden, hidden, hidden],
        scratch_shapes=[], operands=(dy, a, b, ws))


def ffn_backward_norm(da, db, dxo, x, y, gn, sc, ws, first, name, rider=None):
    t, d = x.shape
    f = ws.shape[1]
    tm, tf = _tile(t, FFN_BWD_TILE[0], 16), _tile(f, FFN_BWD_TILE[1])
    nf = f // tf
    row = pl.BlockSpec((tm, d), lambda i, j: (i, 0))
    vec = pl.BlockSpec((1, d), lambda i, j: (0, 0))
    wide = pl.BlockSpec((tm, tf), lambda i, j: (i, j))

    def norm_body(da_ref, db_ref, w1_ref, w3_ref, dxo_ref, x_ref, y_ref, gn_ref, sc_ref, dx_ref, sums_ref, acc):
        i, j = pl.program_id(0), pl.program_id(1)

        @pl.when(jnp.logical_and(i == 0, j == 0))
        def _():
            sums_ref[...] = jnp.zeros_like(sums_ref)

        part = _dot(da_ref[...], w1_ref[...]) + _dot(db_ref[...], w3_ref[...])

        @pl.when(j == 0)
        def _():
            acc[...] = part

        @pl.when(jnp.logical_and(j > 0, j < nf - 1))
        def _():
            acc[...] += part

        @pl.when(j == nf - 1)
        def _():
            dh = part if nf == 1 else acc[...] + part
            dxo_v = dxo_ref[...]
            dx, d_sh, d_sc, d_gn = _norm_mod_bwd(dh, x_ref[...], gn_ref[...], sc_ref[...])
            dx_ref[...] = dxo_v + dx
            d_gate = jnp.sum(dxo_v * (0.5 * y_ref[...].astype(F32)), axis=0, keepdims=True)
            _add_rows(sums_ref, [d_sh, d_sc, d_gate, d_gn])

    w1_spec, w3_spec, _ = _ffn_weight_specs(first, tf, d)
    return _call_with_rider(
        norm_body, rider, name=name, grid=(t // tm, nf),
        in_specs=[wide, wide, w1_spec, w3_spec, row, row, row, vec, vec],
        out_specs=[row, pl.BlockSpec((8, d), lambda i, j: (0, 0))],
        out_shape=[jax.ShapeDtypeStruct((t, d), F32), jax.ShapeDtypeStruct((8, d), F32)],
        scratch_shapes=[pltpu.VMEM((tm, d), F32)],
        operands=(da, db, ws, ws, dxo, x, y, gn, sc))


def matmul_tn(a, b, name, rider=None):
    t, m = a.shape
    n = b.shape[1]
    tm, tn, tk = _tile(m, GRAD_TILE), _tile(n, GRAD_TILE), _tile(t, GRAD_DEPTH, 16)
    nk = t // tk

    def body(a_ref, b_ref, o_ref, acc):
        k = pl.program_id(2)

        @pl.when(k == 0)
        def _():
            acc[...] = jnp.zeros_like(acc)

        acc[...] += _dot(a_ref[...], b_ref[...], TN)

        @pl.when(k == nk - 1)
        def _():
            o_ref[...] = acc[...]

    out = _call_with_rider(
        body, rider, name=name, grid=(m // tm, n // tn, nk),
        in_specs=[pl.BlockSpec((tk, tm), lambda i, j, k: (k, i)), pl.BlockSpec((tk, tn), lambda i, j, k: (k, j))],
        out_specs=[pl.BlockSpec((tm, tn), lambda i, j, k: (i, j))],
        out_shape=[jax.ShapeDtypeStruct((m, n), F32)],
        scratch_shapes=[pltpu.VMEM((tm, tn), F32)], operands=(a, b))
    return out[0] if rider is None else out


def mix_in_forward(x, gn, sc, sh, w_in):
    t, d = x.shape
    tm = _tile(t, ROW_TILE, 16)

    def body(x_ref, gn_ref, sc_ref, sh_ref, w_ref, h_ref, zc_ref, zm_ref):
        xhat, _ = _rms(x_ref[...])
        h = (xhat * gn_ref[...] * (1.0 + sc_ref[...]) + sh_ref[...]).astype(BF16)
        h_ref[...] = h
        z = _dot(h, w_ref[...], NT)
        zc_ref[...] = z[:, :ZC_COLS].astype(BF16)
        zm_ref[...] = z[:, ZC_COLS:].astype(BF16)

    row = pl.BlockSpec((tm, d), lambda i: (i, 0))
    vec = pl.BlockSpec((1, d), lambda i: (0, 0))
    return pl.pallas_call(
        body, name="mix_in_fwd", grid=(t // tm,),
        in_specs=[row, vec, vec, vec, _row(w_in)],
        out_specs=[row, pl.BlockSpec((tm, ZC_COLS), lambda i: (i, 0)), pl.BlockSpec((tm, ZM_COLS), lambda i: (i, 0))],
        out_shape=[jax.ShapeDtypeStruct((t, d), BF16), jax.ShapeDtypeStruct((t, ZC_COLS), BF16),
                   jax.ShapeDtypeStruct((t, ZM_COLS), BF16)],
        compiler_params=_params(("arbitrary",)),
    )(x, gn, sc, sh, w_in)


def _rope_tables(pos, inv_freq):
    ang = pos * inv_freq
    lane = lax.broadcasted_iota(jnp.int32, ang.shape, 1)
    cos, sin = jnp.cos(ang), jnp.sin(ang)
    half = QK_ROPE // 2
    return cos, jnp.where(lane < half, -sin, 0.0), jnp.where(jnp.logical_and(lane >= half, lane < QK_ROPE), sin, 0.0)


def _rope(v, tables):
    cos, sin_a, sin_b = tables
    return v * cos + pltpu.roll(v, LANES - QK_ROPE // 2, 1) * sin_a + pltpu.roll(v, QK_ROPE // 2, 1) * sin_b


def _rope_transposed(dv, tables):
    cos, sin_a, sin_b = tables
    return dv * cos + pltpu.roll(dv * sin_a, QK_ROPE // 2, 1) + pltpu.roll(dv * sin_b, LANES - QK_ROPE // 2, 1)


def mla_project(zm, pos, inv_freq, qg, kvg, w_uq, w_ukv):
    t = zm.shape[0]
    tm = _tile(t, ROW_TILE, 16)

    def body(zm_ref, pos_ref, if_ref, qg_ref, kvg_ref, wq_ref, wkv_ref, qn_ref, kvn_ref, q_ref, k_ref, v_ref):
        zv = zm_ref[...].astype(F32)
        qn = (_rms(zv[:, :Q_LORA])[0] * qg_ref[...]).astype(BF16)
        kvn = (_rms(zv[:, Q_LORA:Q_LORA + KV_LORA])[0] * kvg_ref[...]).astype(BF16)
        qn_ref[...] = qn
        kvn_ref[...] = kvn
        qf = _dot(qn, wq_ref[...], NT) * QK_FOLD
        kvf = _dot(kvn, wkv_ref[...], NT)
        tables = _rope_tables(pos_ref[...], if_ref[...])
        kr = _rope(zv[:, Q_LORA + KV_LORA:], tables).astype(BF16)
        for h in range(MLA_HEADS):
            lo = h * HEAD_PAD
            q_ref[:, lo:lo + QK_NOPE] = qf[:, lo:lo + QK_NOPE].astype(BF16)
            q_ref[:, lo + QK_NOPE:lo + HEAD_PAD] = _rope(qf[:, lo + QK_NOPE:lo + HEAD_PAD], tables).astype(BF16)
            k_ref[:, lo:lo + QK_NOPE] = kvf[:, h * QK_NOPE:(h + 1) * QK_NOPE].astype(BF16)
            k_ref[:, lo + QK_NOPE:lo + HEAD_PAD] = kr
        v_ref[...] = kvf[:, MLA_HEADS * QK_NOPE:].astype(BF16)

    def rows(n):
        return pl.BlockSpec((tm, n), lambda i: (i, 0))

    return pl.pallas_call(
        body, name="mla_project", grid=(t // tm,),
        in_specs=[rows(ZM_COLS), rows(1), _row(inv_freq), _row(qg), _row(kvg), _row(w_uq), _row(w_ukv)],
        out_specs=[rows(Q_LORA), rows(KV_LORA), rows(QK_COLS), rows(QK_COLS), rows(MLA_WIDTH)],
        out_shape=[jax.ShapeDtypeStruct((t, Q_LORA), BF16), jax.ShapeDtypeStruct((t, KV_LORA), BF16),
                   jax.ShapeDtypeStruct((t, QK_COLS), BF16), jax.ShapeDtypeStruct((t, QK_COLS), BF16),
                   jax.ShapeDtypeStruct((t, MLA_WIDTH), BF16)],
        compiler_params=_params(("arbitrary",)),
    )(zm, pos, inv_freq, qg, kvg, w_uq, w_ukv)


def _chunk_mask(shape, q_axis):
    qi = lax.broadcasted_iota(jnp.int32, shape, q_axis) // CHUNK
    ki = lax.broadcasted_iota(jnp.int32, shape, 1 - q_axis) // CHUNK
    return ki <= qi


def attention_forward(q, k, v):
    t = q.shape[0]
    tq = _tile(t, ATTN_TILE, CHUNK)

    def body(q_ref, k_ref, v_ref, o_ref, lse_ref):
        i = pl.program_id(1)
        qv = q_ref[...]

        def step(kb, carry, masked):
            m, l, acc = carry
            start = pl.multiple_of(kb * tq, tq)
            s = _dot(qv, k_ref[pl.ds(start, tq), :], NT)
            if masked:
                s = jnp.where(_chunk_mask(s.shape, 0), s, NEG_INF)
            m_new = jnp.maximum(m, jnp.max(s, axis=-1, keepdims=True))
            alpha = jnp.exp2(m - m_new)
            p = jnp.exp2(s - m_new)
            l = alpha * l + jnp.sum(p, axis=-1, keepdims=True)
            acc = alpha * acc + _dot(p.astype(BF16), v_ref[pl.ds(start, tq), :])
            return m_new, l, acc

        init = (jnp.full((tq, 1), NEG_INF, F32), jnp.zeros((tq, 1), F32), jnp.zeros((tq, V_HEAD), F32))
        carry = lax.fori_loop(0, i // 2, lambda pb, cr: step(2 * pb + 1, step(2 * pb, cr, False), False), init)
        carry = lax.fori_loop(0, i % 2, lambda _, cr: step(i - 1, cr, False), carry)
        m, l, acc = step(i, carry, True)
        o_ref[...] = (acc / l).astype(BF16)
        lse_ref[0] = m + jnp.log2(l)

    return pl.pallas_call(
        body, name="attn_fwd", grid=(MLA_HEADS, t // tq),
        in_specs=[pl.BlockSpec((tq, HEAD_PAD), lambda h, i: (i, h)),
                  pl.BlockSpec((t, HEAD_PAD), lambda h, i: (0, h)),
                  pl.BlockSpec((t, V_HEAD), lambda h, i: (0, h))],
        out_specs=[pl.BlockSpec((tq, V_HEAD), lambda h, i: (i, h)),
                   pl.BlockSpec((1, tq, 1), lambda h, i: (h, i, 0))],
        out_shape=[jax.ShapeDtypeStruct((t, MLA_WIDTH), BF16), jax.ShapeDtypeStruct((MLA_HEADS, t, 1), F32)],
        compiler_params=_params(("arbitrary", "arbitrary")),
    )(q, k, v)


def attention_backward(q, k, v, do, lse, delta, rider=None):
    t = q.shape[0]
    tq = _tile(t, ATTN_TILE, CHUNK)
    nq = t // tq

    def body(q_ref, k_ref, v_ref, do_ref, lse_ref, delta_ref, dq_ref, dk_ref, dv_ref, dq_acc):
        kb = pl.program_id(1)

        @pl.when(kb == 0)
        def _():
            dq_acc[...] = jnp.zeros_like(dq_acc)

        kv, vv = k_ref[...], v_ref[...]

        def step(qb, carry, masked):
            dk, dv = carry
            rows = pl.ds(pl.multiple_of(qb * tq, tq), tq)
            qv, dov = q_ref[rows, :], do_ref[rows, :]
            s = _dot(kv, qv, NT)
            if masked:
                s = jnp.where(_chunk_mask(s.shape, 1), s, NEG_INF)
            p = jnp.exp2(s - lse_ref[0, qb])
            dv = dv + _dot(p.astype(BF16), dov)
            dp = _dot(vv, dov, NT)
            ds = (p * (dp - delta_ref[0, qb]) * LN_2).astype(BF16)
            dk = dk + _dot(ds, qv)
            dq_acc[rows, :] += _dot(ds, kv, TN)
            return dk, dv

        carry = step(kb, (jnp.zeros((tq, HEAD_PAD), F32), jnp.zeros((tq, V_HEAD), F32)), True)
        odd = (nq - 1 - kb) % 2
        carry = lax.fori_loop(0, odd, lambda _, cr: step(kb + 1, cr, False), carry)
        first = kb + 1 + odd
        dk, dv = lax.fori_loop(0, (nq - first) // 2,
                               lambda pb, cr: step(first + 2 * pb + 1, step(first + 2 * pb, cr, False), False), carry)
        dk_ref[...] = dk.astype(BF16)
        dv_ref[...] = dv.astype(BF16)

        @pl.when(kb == nq - 1)
        def _():
            dq_ref[...] = dq_acc[...].astype(BF16)

    stat = pl.BlockSpec((1, nq, 1, tq), lambda h, j: (h, 0, 0, 0))
    return _call_with_rider(
        body, rider, name="attn_bwd", grid=(MLA_HEADS, nq),
        in_specs=[pl.BlockSpec((t, HEAD_PAD), lambda h, j: (0, h)),
                  pl.BlockSpec((tq, HEAD_PAD), lambda h, j: (j, h)),
                  pl.BlockSpec((tq, V_HEAD), lambda h, j: (j, h)),
                  pl.BlockSpec((t, V_HEAD), lambda h, j: (0, h)), stat, stat],
        out_specs=[pl.BlockSpec((t, HEAD_PAD), lambda h, j: (0, h)),
                   pl.BlockSpec((tq, HEAD_PAD), lambda h, j: (j, h)),
                   pl.BlockSpec((tq, V_HEAD), lambda h, j: (j, h))],
        out_shape=[jax.ShapeDtypeStruct((t, QK_COLS), BF16), jax.ShapeDtypeStruct((t, QK_COLS), BF16),
                   jax.ShapeDtypeStruct((t, MLA_WIDTH), BF16)],
        scratch_shapes=[pltpu.VMEM((t, HEAD_PAD), F32)], operands=(q, k, v, do, lse, delta))


HALO = 16


def _halo_spec(tm, n, step, last):
    return pl.BlockSpec((HALO, n), lambda i: (jnp.clip(i * (tm // HALO) + step, 0, last), 0))


def _shift_rows(v, prev, n):
    out = pltpu.roll(v, n, 0)
    row = lax.broadcasted_iota(jnp.int32, v.shape, 0)
    for r in range(n):
        out = jnp.where(row == r, prev[HALO - n + r:HALO - n + r + 1, :], out)
    return out


def _advance_rows(v, nxt, n):
    rows = v.shape[0]
    out = pltpu.roll(v, rows - n, 0)
    row = lax.broadcasted_iota(jnp.int32, v.shape, 0)
    for r in range(n):
        out = jnp.where(row == rows - n + r, nxt[r:r + 1, :], out)
    return out


def _conv_taps(zc, zc_prev, first):
    w = CONV_WIDTH
    u = zc[:, w:2 * w] * zc[:, 2 * w:]
    up = jnp.where(first, 0.0, zc_prev[:, w:2 * w] * zc_prev[:, 2 * w:])
    return u, _shift_rows(u, up, 1), _shift_rows(u, up, 2)


def mix_out_forward(zc, o, conv_w, og, gmat_a, gmat_b, w_out, x, gate):
    t, d = x.shape
    tm = _tile(t, ROW_TILE, 16)
    w = CONV_WIDTH

    def body(zc_ref, zp_ref, o_ref, cw_ref, og_ref, ga_ref, gb_ref, w_ref, x_ref, gate_ref,
             xo_ref, yn_ref, y_ref, ya_ref):
        zc_v = zc_ref[...].astype(F32)
        u, u1, u2 = _conv_taps(zc_v, zp_ref[...].astype(F32), pl.program_id(0) == 0)
        cw = cw_ref[...]
        ya = zc_v[:, :w] * (cw[0:1] * u2 + cw[1:2] * u1 + cw[2:3] * u)
        ya_ref[...] = ya.astype(BF16)
        ov = o_ref[...].astype(F32)
        ogv = og_ref[...]
        yn_ref[:, :w] = (ya * lax.rsqrt(_group_mean(ya * ya, ga_ref[...]) + EPS) * ogv[:, :w]).astype(BF16)
        yn_ref[:, w:] = (ov * lax.rsqrt(_group_mean(ov * ov, gb_ref[...]) + EPS) * ogv[:, w:]).astype(BF16)
        y = _dot(yn_ref[...], w_ref[...])
        y_ref[...] = y.astype(BF16)
        xo_ref[...] = x_ref[...] + gate_ref[...] * y

    def rows(n):
        return pl.BlockSpec((tm, n), lambda i: (i, 0))

    return pl.pallas_call(
        body, name="mix_out_fwd", grid=(t // tm,),
        in_specs=[rows(ZC_COLS), _halo_spec(tm, ZC_COLS, -1, t // HALO - 1), rows(MLA_WIDTH), _row(conv_w), _row(og),
                  _row(gmat_a), _row(gmat_b), _row(w_out), rows(d), _row(gate)],
        out_specs=[rows(d), rows(MIX_WIDTH), rows(d), rows(w)],
        out_shape=[jax.ShapeDtypeStruct((t, d), F32), jax.ShapeDtypeStruct((t, MIX_WIDTH), BF16),
                   jax.ShapeDtypeStruct((t, d), BF16), jax.ShapeDtypeStruct((t, w), BF16)],
        compiler_params=_params(("arbitrary",)),
    )(zc, zc, o, conv_w, og, gmat_a, gmat_b, w_out, x, gate)


def _group_norm_bwd(dyn, y, og, gmat):
    rs = lax.rsqrt(_group_mean(y * y, gmat) + EPS)
    yhat = y * rs
    d_og = jnp.sum(dyn * yhat, axis=0, keepdims=True)
    dyh = dyn * og
    return rs * (dyh - yhat * _group_mean(dyh * yhat, gmat)), d_og


def mix_out_backward(dxo, y, gate, ya, o, og, gmat_a, gmat_b, w_out, rider=None):
    t, d = dxo.shape
    tm = _tile(t, ROW_TILE, 16)
    w = CONV_WIDTH

    def body(dxo_ref, y_ref, gate_ref, ya_ref, o_ref, og_ref, ga_ref, gb_ref, w_ref,
             dy_ref, dya_ref, do_ref, delta_ref, sd_ref, so_ref):
        @pl.when(pl.program_id(0) == 0)
        def _():
            sd_ref[...] = jnp.zeros_like(sd_ref)
            so_ref[...] = jnp.zeros_like(so_ref)

        dxo_v = dxo_ref[...]
        dy = (gate_ref[...] * dxo_v).astype(BF16)
        dy_ref[...] = dy
        sd_ref[0:1, :] += jnp.sum(dxo_v * y_ref[...].astype(F32), axis=0, keepdims=True)
        dyn = _dot(dy, w_ref[...], NT)
        ogv = og_ref[...]
        ov = o_ref[...].astype(F32)
        dya, d_og_a = _group_norm_bwd(dyn[:, :w], ya_ref[...].astype(F32), ogv[:, :w], ga_ref[...])
        dov, d_og_b = _group_norm_bwd(dyn[:, w:], ov, ogv[:, w:], gb_ref[...])
        dya_ref[...] = dya.astype(BF16)
        do_ref[...] = dov.astype(BF16)
        so_ref[0:1, :w] += d_og_a
        so_ref[0:1, w:] += d_og_b
        prod = dov * ov
        for h in range(MLA_HEADS):
            delta_ref[h] = jnp.sum(prod[:, h * V_HEAD:(h + 1) * V_HEAD], axis=-1, keepdims=True)

    def rows(n):
        return pl.BlockSpec((tm, n), lambda i: (i, 0))

    return _call_with_rider(
        body, rider, name="mix_out_bwd", grid=(t // tm,),
        in_specs=[rows(d), rows(d), _row(gate), rows(w), rows(MLA_WIDTH), _row(og), _row(gmat_a), _row(gmat_b),
                  _row(w_out)],
        out_specs=[rows(d), rows(w), rows(MLA_WIDTH), pl.BlockSpec((MLA_HEADS, tm, 1), lambda i: (0, i, 0)),
                   pl.BlockSpec((8, d), lambda i: (0, 0)), pl.BlockSpec((8, MIX_WIDTH), lambda i: (0, 0))],
        out_shape=[jax.ShapeDtypeStruct((t, d), BF16), jax.ShapeDtypeStruct((t, w), BF16),
                   jax.ShapeDtypeStruct((t, MLA_WIDTH), BF16), jax.ShapeDtypeStruct((MLA_HEADS, t, 1), F32),
                   jax.ShapeDtypeStruct((8, d), F32), jax.ShapeDtypeStruct((8, MIX_WIDTH), F32)],
        scratch_shapes=[], operands=(dxo, y, gate, ya, o, og, gmat_a, gmat_b, w_out))


def conv_backward(zc, dya, conv_w):
    t = zc.shape[0]
    tm = _tile(t, ROW_TILE, 16)
    nt = t // tm
    w = CONV_WIDTH

    def body(zc_ref, zp_ref, zn_ref, dya_ref, dn_ref, cw_ref, dzc_ref, sums_ref):
        i = pl.program_id(0)

        @pl.when(i == 0)
        def _():
            sums_ref[...] = jnp.zeros_like(sums_ref)

        zc_v = zc_ref[...].astype(F32)
        u, u1, u2 = _conv_taps(zc_v, zp_ref[...].astype(F32), i == 0)
        cw = cw_ref[...]
        dya_v = dya_ref[...].astype(F32)
        dyc = dya_v * zc_v[:, :w]
        dyc_next = jnp.where(i == nt - 1, 0.0, dn_ref[...].astype(F32) * zn_ref[:, :w].astype(F32))
        du = cw[2:3] * dyc + cw[1:2] * _advance_rows(dyc, dyc_next, 1) + cw[0:1] * _advance_rows(dyc, dyc_next, 2)
        dzc_ref[:, :w] = (dya_v * (cw[0:1] * u2 + cw[1:2] * u1 + cw[2:3] * u)).astype(BF16)
        dzc_ref[:, w:2 * w] = (du * zc_v[:, 2 * w:]).astype(BF16)
        dzc_ref[:, 2 * w:] = (du * zc_v[:, w:2 * w]).astype(BF16)
        _add_rows(sums_ref, [jnp.sum(dyc * tap, axis=0, keepdims=True) for tap in (u2, u1, u)])

    def rows(n):
        return pl.BlockSpec((tm, n), lambda i: (i, 0))

    def halo(n, step):
        return _halo_spec(tm, n, step, t // HALO - 1)

    return pl.pallas_call(
        body, name="conv_bwd", grid=(nt,),
        in_specs=[rows(ZC_COLS), halo(ZC_COLS, -1), halo(ZC_COLS, tm // HALO), rows(w), halo(w, tm // HALO),
                  _row(conv_w)],
        out_specs=[rows(ZC_COLS), pl.BlockSpec((8, w), lambda i: (0, 0))],
        out_shape=[jax.ShapeDtypeStruct((t, ZC_COLS), BF16), jax.ShapeDtypeStruct((8, w), F32)],
        compiler_params=_params(("arbitrary",)),
    )(zc, zc, zc, dya, dya, conv_w)


def _rms_bwd(dy, x, g):
    xhat, r = _rms(x)
    d_g = jnp.sum(dy * xhat, axis=0, keepdims=True)
    dxh = dy * g
    return r * (dxh - xhat * jnp.mean(dxh * xhat, axis=-1, keepdims=True)), d_g


def mla_project_backward(dq, dk, dv, zm, pos, inv_freq, qg, kvg, w_uq, w_ukv):
    t = zm.shape[0]
    tm = _tile(t, ROW_TILE, 16)

    def body(dq_ref, dk_ref, dv_ref, zm_ref, pos_ref, if_ref, qg_ref, kvg_ref, wq_ref, wkv_ref,
             dql_ref, dkvl_ref, dzm_ref, sums_ref):
        @pl.when(pl.program_id(0) == 0)
        def _():
            sums_ref[...] = jnp.zeros_like(sums_ref)

        tables = _rope_tables(pos_ref[...], if_ref[...])
        dkr = jnp.zeros((tm, LANES), F32)
        for h in range(MLA_HEADS):
            lo = h * HEAD_PAD
            dql_ref[:, lo:lo + QK_NOPE] = (dq_ref[:, lo:lo + QK_NOPE].astype(F32) * QK_FOLD).astype(BF16)
            dql_ref[:, lo + QK_NOPE:lo + HEAD_PAD] = _rope_transposed(
                dq_ref[:, lo + QK_NOPE:lo + HEAD_PAD].astype(F32) * QK_FOLD, tables).astype(BF16)
            dkvl_ref[:, h * QK_NOPE:(h + 1) * QK_NOPE] = dk_ref[:, lo:lo + QK_NOPE]
            dkr = dkr + dk_ref[:, lo + QK_NOPE:lo + HEAD_PAD].astype(F32)
        dkvl_ref[:, MLA_HEADS * QK_NOPE:] = dv_ref[...]
        zv = zm_ref[...].astype(F32)
        dqn = _dot(dql_ref[...], wq_ref[...])
        dkvn = _dot(dkvl_ref[...], wkv_ref[...])
        dcq, d_qg = _rms_bwd(dqn, zv[:, :Q_LORA], qg_ref[...])
        dckv, d_kvg = _rms_bwd(dkvn, zv[:, Q_LORA:Q_LORA + KV_LORA], kvg_ref[...])
        dzm_ref[:, :Q_LORA] = dcq.astype(BF16)
        dzm_ref[:, Q_LORA:Q_LORA + KV_LORA] = dckv.astype(BF16)
        dzm_ref[:, Q_LORA + KV_LORA:] = _rope_transposed(dkr, tables).astype(BF16)
        sums_ref[0:1, :Q_LORA] += d_qg
        sums_ref[0:1, Q_LORA:Q_LORA + KV_LORA] += d_kvg

    def rows(n):
        return pl.BlockSpec((tm, n), lambda i: (i, 0))

    return pl.pallas_call(
        body, name="mla_project_bwd", grid=(t // tm,),
        in_specs=[rows(QK_COLS), rows(QK_COLS), rows(MLA_WIDTH), rows(ZM_COLS), rows(1), _row(inv_freq),
                  _row(qg), _row(kvg), _row(w_uq), _row(w_ukv)],
        out_specs=[rows(QK_COLS), rows(QK_COLS), rows(ZM_COLS), pl.BlockSpec((8, ZM_COLS), lambda i: (0, 0))],
        out_shape=[jax.ShapeDtypeStruct((t, QK_COLS), BF16), jax.ShapeDtypeStruct((t, QK_COLS), BF16),
                   jax.ShapeDtypeStruct((t, ZM_COLS), BF16), jax.ShapeDtypeStruct((8, ZM_COLS), F32)],
        compiler_params=_params(("arbitrary",)),
    )(dq, dk, dv, zm, pos, inv_freq, qg, kvg, w_uq, w_ukv)


def mix_in_backward(dzc, dzm, w_in, x, dxo, gn, sc, gate, rider=None):
    t, d = x.shape
    tm = _tile(t, ROW_TILE, 16)

    def body(dzc_ref, dzm_ref, w_ref, x_ref, dxo_ref, gn_ref, sc_ref, gate_ref, dx_ref, dy_ref, sums_ref):
        @pl.when(pl.program_id(0) == 0)
        def _():
            sums_ref[...] = jnp.zeros_like(sums_ref)

        dh = _dot(dzc_ref[...], w_ref[:ZC_COLS, :]) + _dot(dzm_ref[...], w_ref[ZC_COLS:, :])
        dx, d_sh, d_sc, d_gn = _norm_mod_bwd(dh, x_ref[...], gn_ref[...], sc_ref[...])
        dx = dxo_ref[...] + dx
        dx_ref[...] = dx
        dy_ref[...] = (0.5 * gate_ref[...] * dx).astype(BF16)
        _add_rows(sums_ref, [d_sh, d_sc, d_gn])

    def rows(n):
        return pl.BlockSpec((tm, n), lambda i: (i, 0))

    return _call_with_rider(
        body, rider, name="mix_in_bwd", grid=(t // tm,),
        in_specs=[rows(ZC_COLS), rows(ZM_COLS), _row(w_in), rows(d), rows(d), _row(gn), _row(sc), _row(gate)],
        out_specs=[rows(d), rows(d), pl.BlockSpec((8, d), lambda i: (0, 0))],
        out_shape=[jax.ShapeDtypeStruct((t, d), F32), jax.ShapeDtypeStruct((t, d), BF16),
                   jax.ShapeDtypeStruct((8, d), F32)],
        scratch_shapes=[], operands=(dzc, dzm, w_in, x, dxo, gn, sc, gate))


def final_loss(x, target, g, gate):
    t, d = x.shape
    tm = _tile(t, ROW_TILE, 16)

    def body(x_ref, t_ref, g_ref, gate_ref, dx_ref, dy_ref, sums_ref):
        @pl.when(pl.program_id(0) == 0)
        def _():
            sums_ref[...] = jnp.zeros_like(sums_ref)

        gv = g_ref[...]
        xhat, r = _rms(x_ref[...])
        err = xhat * gv - t_ref[...]
        dyf = err * (1.0 / d)
        dxh = dyf * gv
        dx = r * (dxh - xhat * jnp.mean(dxh * xhat, axis=-1, keepdims=True))
        dx_ref[...] = dx
        dy_ref[...] = (0.5 * gate_ref[...] * dx).astype(BF16)
        _add_rows(sums_ref, [jnp.sum(dyf * xhat, axis=0, keepdims=True),
                             jnp.sum(err * err, axis=0, keepdims=True) * (0.5 / d)])

    row = pl.BlockSpec((tm, d), lambda i: (i, 0))
    return pl.pallas_call(
        body, name="final_loss", grid=(t // tm,),
        in_specs=[row, row, _row(g), _row(gate)],
        out_specs=[row, row, pl.BlockSpec((8, d), lambda i: (0, 0))],
        out_shape=[jax.ShapeDtypeStruct((t, d), F32), jax.ShapeDtypeStruct((t, d), BF16),
                   jax.ShapeDtypeStruct((8, d), F32)],
        compiler_params=_params(("arbitrary",)),
    )(x, target, g, gate)


def adamw(w, g, m, v, name):
    r, n = w.shape
    tr = _tile(r, max(8, (1 << 19) // n), 8)

    def body(w_ref, g_ref, m_ref, v_ref, d_ref, mo_ref, vo_ref):
        gv = g_ref[...]
        m_new = ADAM_B1 * m_ref[...] + (1.0 - ADAM_B1) * gv
        v_new = ADAM_B2 * v_ref[...] + (1.0 - ADAM_B2) * (gv * gv)
        m_hat = m_new / (1.0 - ADAM_B1 ** ADAM_STEP)
        v_hat = v_new / (1.0 - ADAM_B2 ** ADAM_STEP)
        d_ref[...] = -ADAM_LR * (m_hat / (jnp.sqrt(v_hat) + ADAM_EPS) + ADAM_WD * w_ref[...])
        mo_ref[...] = m_new
        vo_ref[...] = v_new

    blk = pl.BlockSpec((tr, n), lambda i: (i, 0))
    shape = jax.ShapeDtypeStruct((r, n), F32)
    return pl.pallas_call(
        body, name=name, grid=(r // tr,), in_specs=[blk] * 4, out_specs=[blk] * 3, out_shape=[shape] * 3,
        compiler_params=_params(("arbitrary",)),
    )(w, g, m, v)


def _pad_to(v, n):
    return jnp.pad(v, (0, n - v.shape[0]))


def _pad_heads(w, axis_len):
    n = w.shape[1]
    return jnp.pad(w.reshape(MLA_HEADS, axis_len, n), ((0, 0), (0, HEAD_PAD - axis_len), (0, 0))).reshape(-1, n)


def _swap_head_parts(w, inner, outer):
    n = w.shape[1]
    return w.reshape(outer, inner, QK_NOPE, n).transpose(1, 0, 2, 3).reshape(-1, n)


def kernel(x, c, positions, ada_w, ada_b, norm_ffn1_g, ffn1_w1, ffn1_w3, ffn1_w2, norm_mix_g, w_in, conv_w, q_norm_g, w_uq, kv_norm_g, w_ukv, out_norm_g, w_out, norm_ffn2_g, ffn2_w1, ffn2_w3, ffn2_w2, final_norm_g, loss_target, m_ada_w, m_ada_b, m_norm_ffn1_g, m_ffn1_w1, m_ffn1_w3, m_ffn1_w2, m_norm_mix_g, m_w_in, m_conv_w, m_q_norm_g, m_w_uq, m_kv_norm_g, m_w_ukv, m_out_norm_g, m_w_out, m_norm_ffn2_g, m_ffn2_w1, m_ffn2_w3, m_ffn2_w2, m_final_norm_g, v_ada_w, v_ada_b, v_norm_ffn1_g, v_ffn1_w1, v_ffn1_w3, v_ffn1_w2, v_norm_mix_g, v_w_in, v_conv_w, v_q_norm_g, v_w_uq, v_kv_norm_g, v_w_ukv, v_out_norm_g, v_w_out, v_norm_ffn2_g, v_ffn2_w1, v_ffn2_w3, v_ffn2_w2, v_final_norm_g):
    t, d = x.shape[1], x.shape[2]
    f = ffn1_w2.shape[1] * N_DEV
    me = 4 * lax.axis_index("x") + 2 * lax.axis_index("y") + lax.axis_index("c")
    my_c = lax.axis_index("c")
    my_chip = 2 * lax.axis_index("x") + lax.axis_index("y")
    xs = x[0]
    n_ada = ada_w.shape[2]
    cw_n = conv_w.shape[2]

    c_rows = jnp.broadcast_to(c, (8, d))
    conv_rows = jnp.pad(conv_w[0], ((0, 8 - CONV_K), (0, LANES - cw_n)))
    ffn1_blocks = jnp.stack([ffn1_w1[0].T, ffn1_w3[0].T, ffn1_w2[0]]).astype(BF16)
    ffn2_blocks = jnp.stack([ffn2_w1[0].T, ffn2_w3[0].T, ffn2_w2[0]]).astype(BF16)
    c_all, conv_all, ffn1_all = all_gather([c_rows, conv_rows, ffn1_blocks], [0, 0, 1], "gather_first")
    c_all = c_all[:, 0, :]
    conv_full8 = conv_all[:, :, :cw_n].transpose(1, 0, 2).reshape(8, CONV_WIDTH)
    ffn1_ws = ffn1_all.reshape(3, f, d)
    gather_rest = riding_gather(
        [ffn2_blocks, w_in[0].T.astype(BF16), w_uq[0].T.astype(BF16), w_ukv[0].T.astype(BF16), w_out[0].astype(BF16)],
        [1, 0, 0, 0, 0])

    ada_b_cols = lax.dynamic_slice_in_dim(ada_b, me * n_ada, n_ada, axis=1)
    mod_cols = ada_forward(c_all, ada_w[0], ada_b_cols)
    mod_all, = all_gather([mod_cols], [0], "gather_mod")
    mod = lax.dynamic_index_in_dim(mod_all, me, axis=1, keepdims=False).reshape(N_MOD, 1, d)
    sh1, sc1, g1, sh2, sc2, g2, sh3, sc3, g3 = [mod[i] for i in range(N_MOD)]

    gf = final_norm_g.reshape(1, d)
    x1, h1, a1, b1, y1, *gathered = ffn_forward(xs, norm_ffn1_g, sc1, sh1, g1, ffn1_ws, 0, "ffn1_fwd", gather_rest)
    ffn2_ws = gathered[0].reshape(3, f, d)
    w_in_p = jnp.pad(gathered[1].reshape(IN_COLS, d), ((0, ZC_COLS + ZM_COLS - IN_COLS), (0, 0)))
    w_uq_p = _pad_heads(gathered[2].reshape(-1, Q_LORA), QK_NOPE + QK_ROPE)
    w_ukv_p = _swap_head_parts(gathered[3].reshape(-1, KV_LORA), 2, MLA_HEADS)
    w_out_f = gathered[4].reshape(MIX_WIDTH, d)
    h2, zc, zm = mix_in_forward(x1, norm_mix_g, sc2, sh2, w_in_p)
    pos = positions[0].astype(F32).reshape(t, 1)
    inv_freq = ROPE_THETA ** (-jnp.arange(0, QK_ROPE, 2, dtype=F32) / QK_ROPE)
    inv_freq = jnp.concatenate([inv_freq, inv_freq, jnp.zeros((LANES - QK_ROPE,), F32)]).reshape(1, LANES)
    qn, kvn, q, k, v = mla_project(zm, pos, inv_freq, q_norm_g, kv_norm_g, w_uq_p, w_ukv_p)
    o, lse = attention_forward(q, k, v)
    lane = jnp.arange(CONV_WIDTH)
    gmat_a = (lane[:, None] // (CONV_WIDTH // CONV_GROUPS) == lane[None, :] // (CONV_WIDTH // CONV_GROUPS))
    gmat_a = (gmat_a / (CONV_WIDTH // CONV_GROUPS)).astype(BF16)
    gmat_b = ((lane[:, None] // V_HEAD == lane[None, :] // V_HEAD) / V_HEAD).astype(BF16)
    x2, yn, y2, ya = mix_out_forward(zc, o, conv_full8, out_norm_g, gmat_a, gmat_b, w_out_f, x1, g2)
    x3, h3, a3, b3, y3 = ffn_forward(x2, norm_ffn2_g, sc3, sh3, g3, ffn2_ws, 0, "ffn2_fwd")
    dx3, dy3, sums_f = final_loss(x3, loss_target[0], gf, g3)

    chip_idx = jnp.bitwise_xor(my_chip, jnp.array([0, 2, 1, 3], jnp.int32)).astype(jnp.int32)
    src_idx = (2 * chip_idx + my_c).astype(jnp.int32)

    def row_blocks(named):
        return [g.reshape(N_DEV, g.shape[0] // N_DEV, g.shape[1]) for _, g in named]

    def chip_sums(named, g8, got):
        return [add_sibling(g, r, src_idx, chip_idx, "rs_add_" + n) for g, r, (n, _) in zip(g8, got, named)]

    da3, db3, u3 = ffn_backward_gate(dy3, a3, b3, ffn2_ws, 0, "ffn2_bwd_gate")
    dx2, sums_3 = ffn_backward_norm(da3, db3, dx3, x2, y3, norm_ffn2_g, sc3, ffn2_ws, 0, "ffn2_bwd_norm")
    ffn2_named = [("ffn2_w1", matmul_tn(da3, h3, "ffn2_gw1")), ("ffn2_w3", matmul_tn(db3, h3, "ffn2_gw3")),
                  ("ffn2_w2", matmul_tn(u3, dy3, "ffn2_gw2"))]
    ffn2_g8 = row_blocks(ffn2_named)
    dy2, dya, do, delta, sums_2d, sums_2o, *ffn2_sib = mix_out_backward(
        dx2, y2, g2, ya, o, out_norm_g, gmat_a, gmat_b, w_out_f, riding_sibling(ffn2_g8))
    ffn2_sums = chip_sums(ffn2_named, ffn2_g8, ffn2_sib)
    g_w_out = matmul_tn(yn, dy2, "gw_out")
    nq = t // _tile(t, ATTN_TILE, CHUNK)
    stat_shape = (MLA_HEADS, nq, 1, t // nq)
    dq, dk, dv, *ffn2_got = attention_backward(q, k, v, do, lse.reshape(stat_shape), delta.reshape(stat_shape),
                                               riding_exchange([s[1] for s in ffn2_sums]))
    dzc, sums_c = conv_backward(zc, dya, conv_full8)
    dql, dkvl, dzm, sums_m = mla_project_backward(dq, dk, dv, zm, pos, inv_freq, q_norm_g, kv_norm_g, w_uq_p, w_ukv_p)
    g_w_uq_p = matmul_tn(dql, qn, "gw_uq")
    g_w_ukv_p = matmul_tn(dkvl, kvn, "gw_ukv")
    g_w_in = jnp.concatenate([matmul_tn(dzc, h2, "gw_in_conv"), matmul_tn(dzm, h2, "gw_in_mla")])[:IN_COLS]
    g_w_uq = g_w_uq_p.reshape(MLA_HEADS, HEAD_PAD, Q_LORA)[:, :QK_NOPE + QK_ROPE].reshape(-1, Q_LORA)
    g_w_ukv = _swap_head_parts(g_w_ukv_p, MLA_HEADS, 2)
    mix_named = [("w_in", g_w_in), ("w_uq", g_w_uq), ("w_ukv", g_w_ukv), ("w_out", g_w_out)]
    mix_g8 = row_blocks(mix_named)
    dx1, dy1, sums_1m, *mix_sib = mix_in_backward(dzc, dzm, w_in_p, x1, dx2, norm_mix_g, sc2, g1, riding_sibling(mix_g8))
    mix_sums = chip_sums(mix_named, mix_g8, mix_sib)
    da1, db1, u1, *mix_got = ffn_backward_gate(dy1, a1, b1, ffn1_ws, 0, "ffn1_bwd_gate",
                                               riding_exchange([s[1] for s in mix_sums]))
    ffn1_pair = [("ffn1_w1", matmul_tn(da1, h1, "ffn1_gw1")), ("ffn1_w3", matmul_tn(db1, h1, "ffn1_gw3"))]
    pair_g8 = row_blocks(ffn1_pair)
    g_w2a, *pair_sib = matmul_tn(u1, dy1, "ffn1_gw2", riding_sibling(pair_g8))
    ffn1_last = [("ffn1_w2", g_w2a)]
    last_g8 = row_blocks(ffn1_last)
    ffn1_named = ffn1_pair + ffn1_last
    ffn1_sums = chip_sums(ffn1_pair, pair_g8, pair_sib) + chip_sums(
        ffn1_last, last_g8, exchange_sibling(last_g8, "rs_sibling_ffn1_w2"))
    dx0, sums_1, *ffn1_got = ffn_backward_norm(da1, db1, dx1, xs, y1, norm_ffn1_g, sc1, ffn1_ws, 0, "ffn1_bwd_norm",
                                               riding_exchange([s[1] for s in ffn1_sums]))
    transposed = {"ffn1_w1", "ffn1_w3", "ffn2_w1", "ffn2_w3", "w_in", "w_uq", "w_ukv"}
    g_sh = {}
    for named, group_sums, group_got in ((ffn2_named, ffn2_sums, ffn2_got), (mix_named, mix_sums, mix_got),
                                         (ffn1_named, ffn1_sums, ffn1_got)):
        for (n, _), (own, _), got in zip(named, group_sums, group_got):
            g_rows = add_received(own, got, "rs_sum_" + n)
            g_sh[n] = g_rows.T if n in transposed else g_rows

    dmod = jnp.concatenate([sums_1[0], sums_1[1], sums_1[2], sums_1m[0], sums_1m[1], sums_2d[0],
                            sums_3[0], sums_3[1], sums_3[2]])
    pieces = [dmod, sums_1[3], sums_1m[2], sums_m[0, :Q_LORA], sums_m[0, Q_LORA:Q_LORA + KV_LORA], sums_2o[0],
              sums_3[3], sums_f[0], sums_f[1], sums_c[:CONV_K].reshape(-1)]
    plens = [p.shape[0] for p in pieces]
    poffs = [sum(plens[:i]) for i in range(len(plens))]
    vec_len = -(-sum(plens) // 1024) * 1024
    vec = _pad_to(jnp.concatenate(pieces), vec_len).reshape(-1, LANES)
    vec_all, = all_gather([vec], [0], "gather_sums")
    tot = sum_devices(vec_all).reshape(-1)
    g_ada_b, g_n1, g_nmix, g_qg, g_kvg, g_og, g_n3, g_gf, loss_lanes, g_conv_full = [
        tot[o:o + n] for o, n in zip(poffs, plens)]
    loss = sum_lanes(loss_lanes.reshape(1, d))[0, 0]
    g_conv = lax.dynamic_slice_in_dim(g_conv_full.reshape(CONV_K, CONV_WIDTH), me * cw_n, cw_n, axis=1)
    dmod_all = vec_all.reshape(N_DEV, vec_len)[:, :N_MOD * d]
    dmod_cols = lax.dynamic_slice_in_dim(dmod_all, me * n_ada, n_ada, axis=1)
    g_ada_w = ada_backward(jnp.pad(c_all, ((0, 8), (0, 0))), jnp.pad(dmod_cols, ((0, 8), (0, 0))))

    def update(name, w, g, m, v):
        shape = w.shape
        two_d = (-1, shape[-1])
        dlt, nm, nv = adamw(w.reshape(two_d), g.reshape(two_d), m.reshape(two_d), v.reshape(two_d), "adamw_" + name)
        return g.reshape(shape), dlt.reshape(shape), nm.reshape(shape), nv.reshape(shape)

    res = {}
    res["ada_w"] = update("ada_w", ada_w, g_ada_w, m_ada_w, v_ada_w)
    big = [("ffn1_w1", ffn1_w1, m_ffn1_w1, v_ffn1_w1), ("ffn1_w3", ffn1_w3, m_ffn1_w3, v_ffn1_w3),
           ("ffn2_w1", ffn2_w1, m_ffn2_w1, v_ffn2_w1), ("ffn2_w3", ffn2_w3, m_ffn2_w3, v_ffn2_w3),
           ("w_in", w_in, m_w_in, v_w_in), ("w_uq", w_uq, m_w_uq, v_w_uq), ("w_ukv", w_ukv, m_w_ukv, v_w_ukv),
           ("ffn1_w2", ffn1_w2, m_ffn1_w2, v_ffn1_w2), ("ffn2_w2", ffn2_w2, m_ffn2_w2, v_ffn2_w2),
           ("w_out", w_out, m_w_out, v_w_out)]
    for name, w, m, v in big:
        res[name] = update(name, w, g_sh[name], m, v)
    smalls = [("ada_b", ada_b, g_ada_b, m_ada_b, v_ada_b),
              ("norm_ffn1_g", norm_ffn1_g, g_n1, m_norm_ffn1_g, v_norm_ffn1_g),
              ("norm_mix_g", norm_mix_g, g_nmix, m_norm_mix_g, v_norm_mix_g),
              ("conv_w", conv_w, g_conv, m_conv_w, v_conv_w),
              ("q_norm_g", q_norm_g, g_qg, m_q_norm_g, v_q_norm_g),
              ("kv_norm_g", kv_norm_g, g_kvg, m_kv_norm_g, v_kv_norm_g),
              ("out_norm_g", out_norm_g, g_og, m_out_norm_g, v_out_norm_g),
              ("norm_ffn2_g", norm_ffn2_g, g_n3, m_norm_ffn2_g, v_norm_ffn2_g),
              ("final_norm_g", final_norm_g, g_gf, m_final_norm_g, v_final_norm_g)]
    slens = [w.size for _, w, _, _, _ in smalls]
    soffs = [sum(slens[:i]) for i in range(len(slens))]
    s_len = -(-sum(slens) // 1024) * 1024

    def pack_small(i):
        return _pad_to(jnp.concatenate([s[i].reshape(-1) for s in smalls]), s_len).reshape(8, -1)

    s_out = adamw(pack_small(1), pack_small(2), pack_small(3), pack_small(4), "adamw_small")
    for (name, w, g, _, _), o, n in zip(smalls, soffs, slens):
        res[name] = (g.reshape(w.shape),) + tuple(a.reshape(-1)[o:o + n].reshape(w.shape) for a in s_out)

    order = ["ada_w", "ada_b", "norm_ffn1_g", "ffn1_w1", "ffn1_w3", "ffn1_w2", "norm_mix_g", "w_in", "conv_w",
             "q_norm_g", "w_uq", "kv_norm_g", "w_ukv", "out_norm_g", "w_out", "norm_ffn2_g", "ffn2_w1", "ffn2_w3",
             "ffn2_w2", "final_norm_g"]
    return (loss, dx0.reshape(x.shape), *[res[n][0] for n in order], *[res[n][1] for n in order],
            *[res[n][2] for n in order], *[res[n][3] for n in order])
```

```python
import functools
import math

import jax
import jax.numpy as jnp
from jax import lax
from jax.experimental import pallas as pl
from jax.experimental.pallas import tpu as pltpu

F32 = jnp.float32
BF16 = jnp.bfloat16
MESH_ID = pl.DeviceIdType.MESH
N_DEV = 8

EPS = 1e-6
CHUNK = 64
N_MOD = 9
CONV_WIDTH = 512
CONV_GROUPS = 8
CONV_K = 3
MLA_HEADS = 4
QK_NOPE = 128
QK_ROPE = 64
V_HEAD = 128
Q_LORA = 384
KV_LORA = 256
ROPE_THETA = 10000.0
MLA_WIDTH = MLA_HEADS * V_HEAD
MIX_WIDTH = CONV_WIDTH + MLA_WIDTH
IN_COLS = 3 * CONV_WIDTH + Q_LORA + KV_LORA + QK_ROPE
ZC_COLS = 3 * CONV_WIDTH
ZM_COLS = Q_LORA + KV_LORA + 128
HEAD_PAD = 256
QK_COLS = MLA_HEADS * HEAD_PAD
ATTN_SCALE = (QK_NOPE + QK_ROPE) ** -0.5
LOG2_E = 1.4426950408889634
LN_2 = 0.6931471805599453
QK_FOLD = ATTN_SCALE * LOG2_E
NEG_INF = -1e30

ADAM_LR = 0.001
ADAM_B1 = 0.9
ADAM_B2 = 0.999
ADAM_EPS = 1e-08
ADAM_WD = 0.01
ADAM_STEP = 10

LANES = 128
MXU_COLS = 256
VMEM_LIMIT = 56 * 1024 * 1024
ROW_TILE = 512
FFN_FWD_TILE = (1024, 256)
FFN_BWD_TILE = (512, 1408)
GRAD_TILE = 1408
GRAD_DEPTH = 1024
SUM_ROWS = 256
ATTN_TILE = 512

NN = (((1,), (0,)), ((), ()))
NT = (((1,), (1,)), ((), ()))
TN = (((0,), (0,)), ((), ()))


def _dot(a, b, dims=NN):
    return lax.dot_general(a, b, dims, preferred_element_type=F32)


def _tile(n, cap, mult=LANES):
    best = None
    for t in range(mult, min(n, cap) + 1, mult):
        if n % t == 0:
            best = t
    return n if best is None else best


def _params(sem=None):
    return pltpu.CompilerParams(dimension_semantics=sem, vmem_limit_bytes=VMEM_LIMIT)


def _row(v):
    return pl.BlockSpec(v.shape, lambda *_: (0,) * v.ndim)


def _sigmoid(x):
    return 0.5 * jnp.tanh(0.5 * x) + 0.5


def _rms(x):
    r = lax.rsqrt(jnp.mean(x * x, axis=-1, keepdims=True) + EPS)
    return x * r, r


def _norm_mod_bwd(dh, x, gn, sc):
    xhat, r = _rms(x)
    d_sh = jnp.sum(dh, axis=0, keepdims=True)
    d_sc = jnp.sum(dh * (xhat * gn), axis=0, keepdims=True)
    dxn = dh * (1.0 + sc)
    d_gn = jnp.sum(dxn * xhat, axis=0, keepdims=True)
    dxh = dxn * gn
    dx = r * (dxh - xhat * jnp.mean(dxh * xhat, axis=-1, keepdims=True))
    return dx, d_sh, d_sc, d_gn


def _group_mean(v, gmat):
    return _dot(v.astype(BF16), gmat)


def _add_rows(ref, rows):
    for r, v in enumerate(rows):
        ref[r:r + 1, :] += v


def _window(ref, axis, j):
    return ref.at[(slice(None),) * axis + (j,)]


def _any_specs(n):
    return [pl.BlockSpec(memory_space=pl.ANY)] * n


def all_gather(blocks, axes, name):
    n_arr = len(blocks)

    def body(*refs):
        start, forward, finish = _gather_steps(refs[:n_arr], refs[n_arr:2 * n_arr], axes, *refs[2 * n_arr:])
        start()
        for j in range(3):
            forward(j)
        finish()

    return pl.pallas_call(
        body, name=name, out_shape=_gathered_shapes(blocks, axes),
        in_specs=_any_specs(n_arr), out_specs=_any_specs(n_arr), scratch_shapes=_gather_sems(n_arr),
    )(*blocks)


def _gathered_shapes(blocks, axes):
    return [jax.ShapeDtypeStruct(b.shape[:ax] + (N_DEV,) + b.shape[ax:], b.dtype) for b, ax in zip(blocks, axes)]


def _gather_sems(n_arr):
    return [pltpu.SemaphoreType.DMA((7, n_arr)), pltpu.SemaphoreType.DMA((7, n_arr)), pltpu.SemaphoreType.DMA((n_arr,))]


def _gather_steps(ins, outs, axes, send_sems, recv_sems, local_sems):
    arrays = range(len(ins))
    x, y, c = lax.axis_index("x"), lax.axis_index("y"), lax.axis_index("c")
    me, sibling = (x, y, c), (x, y, 1 - c)
    chips = [(1 - x, y), (x, 1 - y), (1 - x, 1 - y)]

    def slot(a, px, py, pc):
        return _window(outs[a], axes[a], 4 * px + 2 * py + pc)

    def copy(a, k, block, to, src=None):
        return pltpu.make_async_remote_copy(
            src_ref=slot(a, *block) if src is None else src, dst_ref=slot(a, *block),
            send_sem=send_sems.at[k, a], recv_sem=recv_sems.at[k, a], device_id=to, device_id_type=MESH_ID)

    def mine(a):
        return pltpu.make_async_copy(ins[a], slot(a, *me), local_sems.at[a])

    def first():
        return ([copy(a, 0, me, sibling, src=ins[a]) for a in arrays]
                + [copy(a, 1 + j, me, (*chip, c), src=ins[a]) for j, chip in enumerate(chips) for a in arrays])

    def passed(j):
        return [copy(a, 4 + j, (*chips[j], c), sibling) for a in arrays]

    def start():
        for a in arrays:
            mine(a).start()
        for cp in first():
            cp.start()

    def forward(j):
        for a, cp in zip(arrays, passed(j)):
            copy(a, 1 + j, (*chips[j], c), me).wait_recv()
            cp.start()

    def finish():
        for a in arrays:
            copy(a, 0, sibling, me).wait_recv()
        for j, chip in enumerate(chips):
            for a in arrays:
                copy(a, 4 + j, (*chip, 1 - c), me).wait_recv()
        for cp in first() + passed(0) + passed(1) + passed(2):
            cp.wait_send()
        for a in arrays:
            mine(a).wait()

    return start, forward, finish


def exchange_sibling(grads, name):
    n_arr = len(grads)

    def body(*refs):
        start, finish = _sibling_exchange_steps(refs[:n_arr], refs[n_arr:2 * n_arr], *refs[2 * n_arr:])
        start()
        finish()

    return pl.pallas_call(
        body, name=name, out_shape=_sibling_shapes(grads),
        in_specs=_any_specs(n_arr), out_specs=_any_specs(n_arr), scratch_shapes=_exchange_sems(n_arr),
    )(*grads)


def _sibling_shapes(grads):
    return [jax.ShapeDtypeStruct((4,) + g.shape[1:], g.dtype) for g in grads]


def _exchange_sems(n_arr):
    return [pltpu.SemaphoreType.DMA((n_arr,)), pltpu.SemaphoreType.DMA((n_arr,))]


def _sibling_exchange_steps(ins, outs, send_sems, recv_sems):
    x, y, c = lax.axis_index("x"), lax.axis_index("y"), lax.axis_index("c")

    def copy(a, src, dst):
        return pltpu.make_async_remote_copy(
            src_ref=src, dst_ref=dst, send_sem=send_sems.at[a], recv_sem=recv_sems.at[a],
            device_id=(x, y, 1 - c), device_id_type=MESH_ID)

    def start():
        for a in range(len(ins)):
            for k in range(4):
                copy(a, ins[a].at[2 * k + (1 - c)], outs[a].at[k]).start()

    def finish():
        whole = [copy(a, ins[a].at[pl.ds(0, 4)], outs[a]) for a in range(len(ins))]
        for cp in whole:
            cp.wait_recv()
        for cp in whole:
            cp.wait_send()

    return start, finish


def _chip_exchange_steps(ins, outs, send_sems, recv_sems):
    x, y, c = lax.axis_index("x"), lax.axis_index("y"), lax.axis_index("c")
    chips = [(1 - x, y), (x, 1 - y), (1 - x, 1 - y)]

    def copy(a, src, dst, chip):
        return pltpu.make_async_remote_copy(
            src_ref=src, dst_ref=dst, send_sem=send_sems.at[a], recv_sem=recv_sems.at[a],
            device_id=(*chip, c), device_id_type=MESH_ID)

    def start():
        for a in range(len(ins)):
            for j, chip in enumerate(chips):
                copy(a, ins[a].at[j], outs[a].at[j], chip).start()

    def finish():
        whole = [copy(a, ins[a], outs[a], chips[0]) for a in range(len(ins))]
        for cp in whole:
            cp.wait_recv()
        for cp in whole:
            cp.wait_send()

    return start, finish


def riding_gather(blocks, axes):
    def phases(ins, outs, *sems):
        start, forward, finish = _gather_steps(ins, outs, axes, *sems)
        return [start] + [functools.partial(forward, j) for j in range(3)] + [finish]

    return dict(operands=blocks, out_shape=_gathered_shapes(blocks, axes), sems=_gather_sems(len(blocks)),
                phases=phases, when=("first", "late0", "late1", "late2", "last"))


def riding_exchange(parts):
    def phases(ins, outs, *sems):
        return list(_chip_exchange_steps(ins, outs, *sems))

    return dict(operands=parts, out_shape=[jax.ShapeDtypeStruct(p.shape, p.dtype) for p in parts],
                sems=_exchange_sems(len(parts)), phases=phases, when=("first", "last"))


def riding_sibling(grads):
    def phases(ins, outs, *sems):
        return list(_sibling_exchange_steps(ins, outs, *sems))

    return dict(operands=grads, out_shape=_sibling_shapes(grads), sems=_exchange_sems(len(grads)),
                phases=phases, when=("first", "last"))


def _call_with_rider(body, rider, *, name, grid, in_specs, out_specs, out_shape, scratch_shapes, operands):
    params = _params(("arbitrary",) * len(grid))
    if rider is None:
        return pl.pallas_call(body, name=name, grid=grid, in_specs=in_specs, out_specs=out_specs,
                              out_shape=out_shape, scratch_shapes=scratch_shapes, compiler_params=params)(*operands)
    n_in, n_out, n_scr, k = len(in_specs), len(out_specs), len(scratch_shapes), len(rider["operands"])
    at = {"first": (0,) * len(grid), "last": tuple(g - 1 for g in grid)}
    if "late0" in rider["when"]:
        rows, cols = grid
        assert cols >= 3
        at.update({"late%d" % j: (max(rows - 2, 0), j) for j in range(3)})

    def wrapped(*refs):
        ins, c_in = refs[:n_in], refs[n_in:n_in + k]
        outs, c_out = refs[n_in + k:n_in + k + n_out], refs[n_in + k + n_out:n_in + 2 * k + n_out]
        scratch, sems = refs[n_in + 2 * k + n_out:n_in + 2 * k + n_out + n_scr], refs[n_in + 2 * k + n_out + n_scr:]
        pos = [pl.program_id(axis) for axis in range(len(grid))]

        def here(key):
            return functools.reduce(jnp.logical_and, [p == v for p, v in zip(pos, at[key])])

        phases = rider["phases"](c_in, c_out, *sems)
        for fn, key in zip(phases, rider["when"]):
            if key != "last":
                pl.when(here(key))(fn)
        body(*ins, *outs, *scratch)
        pl.when(here("last"))(phases[-1])

    return pl.pallas_call(
        wrapped, name=name, grid=grid,
        in_specs=list(in_specs) + _any_specs(k), out_specs=list(out_specs) + _any_specs(k),
        out_shape=list(out_shape) + rider["out_shape"], scratch_shapes=list(scratch_shapes) + rider["sems"],
        compiler_params=params)(*operands, *rider["operands"])


def add_sibling(g8, got, src_idx, chip_idx, name):
    _, r, n = g8.shape
    tr = _tile(r, SUM_ROWS, 16)

    def body(si_ref, ci_ref, g0_ref, g1_ref, g2_ref, g3_ref, got_ref, own_ref, send_ref):
        own_ref[...] = g0_ref[0] + got_ref[ci_ref[0]]
        for j, g_ref in enumerate((g1_ref, g2_ref, g3_ref)):
            send_ref[j] = (g_ref[0] + got_ref[ci_ref[j + 1]]).astype(BF16)

    def mine(j):
        return pl.BlockSpec((1, tr, n), lambda i, si, ci: (si[j], i, 0))

    return pl.pallas_call(
        body, name=name,
        out_shape=[jax.ShapeDtypeStruct((r, n), F32), jax.ShapeDtypeStruct((3, r, n), BF16)],
        grid_spec=pltpu.PrefetchScalarGridSpec(
            num_scalar_prefetch=2, grid=(r // tr,),
            in_specs=[mine(0), mine(1), mine(2), mine(3), pl.BlockSpec((4, tr, n), lambda i, si, ci: (0, i, 0))],
            out_specs=[pl.BlockSpec((tr, n), lambda i, si, ci: (i, 0)),
                       pl.BlockSpec((3, tr, n), lambda i, si, ci: (0, i, 0))]),
        compiler_params=_params(("arbitrary",)),
    )(src_idx, chip_idx, g8, g8, g8, g8, got)


def add_received(own, got, name):
    r, n = own.shape
    tr = _tile(r, SUM_ROWS, 16)

    def body(a_ref, b_ref, o_ref):
        acc = a_ref[...]
        for j in range(3):
            acc = acc + b_ref[j].astype(F32)
        o_ref[...] = acc

    return pl.pallas_call(
        body, name=name,
        out_shape=jax.ShapeDtypeStruct((r, n), F32),
        grid=(r // tr,),
        in_specs=[pl.BlockSpec((tr, n), lambda i: (i, 0)), pl.BlockSpec((3, tr, n), lambda i: (0, i, 0))],
        out_specs=pl.BlockSpec((tr, n), lambda i: (i, 0)),
        compiler_params=_params(("arbitrary",)),
    )(own, got)


def sum_devices(g):
    def body(g_ref, o_ref):
        acc = g_ref[0]
        for j in range(1, N_DEV):
            acc = acc + g_ref[j]
        o_ref[...] = acc

    return pl.pallas_call(body, name="sum_devices", out_shape=jax.ShapeDtypeStruct(g.shape[1:], F32))(g)


def sum_lanes(v):
    def body(v_ref, o_ref):
        o_ref[...] = jnp.broadcast_to(jnp.sum(v_ref[...], axis=-1, keepdims=True), (1, LANES))

    return pl.pallas_call(body, name="sum_lanes", out_shape=jax.ShapeDtypeStruct((1, LANES), F32))(v)


def ada_forward(c_all, ada_w, ada_b_cols):
    nb, n = c_all.shape[0], ada_w.shape[1]

    def body(c_ref, w_ref, b_ref, o_ref):
        cv = c_ref[...]
        s = (cv * jax.nn.sigmoid(cv)).astype(BF16)
        o_ref[...] = _dot(s, w_ref[...].astype(BF16)) + b_ref[...]

    return pl.pallas_call(body, name="ada_fwd", out_shape=jax.ShapeDtypeStruct((nb, n), F32),
                          compiler_params=_params())(c_all, ada_w, ada_b_cols)


def ada_backward(c_all16, dmod16):
    d, n = c_all16.shape[1], dmod16.shape[1]

    def body(c_ref, g_ref, o_ref):
        cv = c_ref[...]
        s = (cv * jax.nn.sigmoid(cv)).astype(BF16)
        o_ref[...] = _dot(s, g_ref[...].astype(BF16), TN)

    return pl.pallas_call(body, name="ada_bwd", out_shape=jax.ShapeDtypeStruct((d, n), F32),
                          compiler_params=_params())(c_all16, dmod16)


def ffn_forward(x, gn, sc, sh, gate, ws, first, name, rider=None):
    t, d = x.shape
    f = ws.shape[1]
    tm, tf = _tile(t, FFN_FWD_TILE[0], 16), _tile(f, FFN_FWD_TILE[1])
    nf = f // tf

    def body(x_ref, gn_ref, sc_ref, sh_ref, gate_ref, w1_ref, w3_ref, w2_ref,
             xo_ref, h_ref, a_ref, b_ref, y_ref, hs, acc):
        j = pl.program_id(1)

        @pl.when(j == 0)
        def _():
            xhat, _ = _rms(x_ref[...])
            h = (xhat * gn_ref[...] * (1.0 + sc_ref[...]) + sh_ref[...]).astype(BF16)
            hs[...] = h
            h_ref[...] = h
            acc[...] = jnp.zeros_like(acc)

        h = hs[...]
        a = _dot(h, w1_ref[...], NT)
        b = _dot(h, w3_ref[...], NT)
        a_ref[...] = a.astype(BF16)
        b_ref[...] = b.astype(BF16)
        u = (a * _sigmoid(a) * b).astype(BF16)
        acc[...] += _dot(u, w2_ref[...])

        @pl.when(j == nf - 1)
        def _():
            y = acc[...]
            y_ref[...] = y.astype(BF16)
            xo_ref[...] = x_ref[...] + 0.5 * gate_ref[...] * y

    row = pl.BlockSpec((tm, d), lambda i, j: (i, 0))
    vec = pl.BlockSpec((1, d), lambda i, j: (0, 0))
    wide = pl.BlockSpec((tm, tf), lambda i, j: (i, j))
    return _call_with_rider(
        body, rider, name=name, grid=(t // tm, nf),
        in_specs=[row, vec, vec, vec, vec] + _ffn_weight_specs(first, tf, d),
        out_specs=[row, row, wide, wide, row],
        out_shape=[jax.ShapeDtypeStruct((t, d), F32), jax.ShapeDtypeStruct((t, d), BF16),
                   jax.ShapeDtypeStruct((t, f), BF16), jax.ShapeDtypeStruct((t, f), BF16),
                   jax.ShapeDtypeStruct((t, d), BF16)],
        scratch_shapes=[pltpu.VMEM((tm, d), BF16), pltpu.VMEM((tm, d), F32)],
        operands=(x, gn, sc, sh, gate, ws, ws, ws))


def _ffn_weight_specs(first, tf, d):
    return [pl.BlockSpec((None, tf, d), lambda i, j, w=first + k: (w, j, 0)) for k in range(3)]


def ffn_backward_gate(dy, a, b, ws, first, name, rider=None):
    t, d = dy.shape
    f = ws.shape[1]
    tm, tf = _tile(t, FFN_BWD_TILE[0], 16), _tile(f, FFN_BWD_TILE[1])
    nf = f // tf

    def gate_body(dy_ref, a_ref, b_ref, w2_ref, da_ref, db_ref, u_ref):
        du = _dot(dy_ref[...], w2_ref[...], NT)
        av = a_ref[...].astype(F32)
        bv = b_ref[...].astype(F32)
        s = _sigmoid(av)
        sa = av * s
        da_ref[...] = (du * bv * (s + sa * (1.0 - s))).astype(BF16)
        db_ref[...] = (du * sa).astype(BF16)
        u_ref[...] = (sa * bv).astype(BF16)

    hidden = jax.ShapeDtypeStruct((t, f), BF16)
    wide_t = pl.BlockSpec((tm, tf), lambda j, i: (i, j))
    return _call_with_rider(
        gate_body, rider, name=name, grid=(nf, t // tm),
        in_specs=[pl.BlockSpec((tm, d), lambda j, i: (i, 0)), wide_t, wide_t,
                  pl.BlockSpec((None, tf, d), lambda j, i: (first + 2, j, 0))],
        out_specs=[wide_t, wide_t, wide_t], out_shape=[hidden, hidden, hidden],
        scratch_shapes=[], operands=(dy, a, b, ws))


def ffn_backward_norm(da, db, dxo, x, y, gn, sc, ws, first, name, rider=None):
    t, d = x.shape
    f = ws.shape[1]
    tm, tf = _tile(t, FFN_BWD_TILE[0], 16), _tile(f, FFN_BWD_TILE[1])
    nf = f // tf
    row = pl.BlockSpec((tm, d), lambda i, j: (i, 0))
    vec = pl.BlockSpec((1, d), lambda i, j: (0, 0))
    wide = pl.BlockSpec((tm, tf), lambda i, j: (i, j))

    def norm_body(da_ref, db_ref, w1_ref, w3_ref, dxo_ref, x_ref, y_ref, gn_ref, sc_ref, dx_ref, sums_ref, acc):
        i, j = pl.program_id(0), pl.program_id(1)

        @pl.when(jnp.logical_and(i == 0, j == 0))
        def _():
            sums_ref[...] = jnp.zeros_like(sums_ref)

        part = _dot(da_ref[...], w1_ref[...]) + _dot(db_ref[...], w3_ref[...])

        @pl.when(j == 0)
        def _():
            acc[...] = part

        @pl.when(jnp.logical_and(j > 0, j < nf - 1))
        def _():
            acc[...] += part

        @pl.when(j == nf - 1)
        def _():
            dh = part if nf == 1 else acc[...] + part
            dxo_v = dxo_ref[...]
            dx, d_sh, d_sc, d_gn = _norm_mod_bwd(dh, x_ref[...], gn_ref[...], sc_ref[...])
            dx_ref[...] = dxo_v + dx
            d_gate = jnp.sum(dxo_v * (0.5 * y_ref[...].astype(F32)), axis=0, keepdims=True)
            _add_rows(sums_ref, [d_sh, d_sc, d_gate, d_gn])

    w1_spec, w3_spec, _ = _ffn_weight_specs(first, tf, d)
    return _call_with_rider(
        norm_body, rider, name=name, grid=(t // tm, nf),
        in_specs=[wide, wide, w1_spec, w3_spec, row, row, row, vec, vec],
        out_specs=[row, pl.BlockSpec((8, d), lambda i, j: (0, 0))],
        out_shape=[jax.ShapeDtypeStruct((t, d), F32), jax.ShapeDtypeStruct((8, d), F32)],
        scratch_shapes=[pltpu.VMEM((tm, d), F32)],
        operands=(da, db, ws, ws, dxo, x, y, gn, sc))


def matmul_tn(a, b, name, rider=None):
    parts = list(a) if isinstance(a, (list, tuple)) else [a]
    t, n = b.shape
    widths = [p.shape[1] for p in parts]
    tm = _tile(functools.reduce(math.gcd, widths), GRAD_TILE)
    tn, tk = _tile(n, GRAD_TILE), _tile(t, GRAD_DEPTH, 16)
    nk = t // tk
    counts = [w // tm for w in widths]
    firsts = [sum(counts[:p]) for p in range(len(parts))]

    def body(*refs):
        a_refs, (b_ref, o_ref, acc) = refs[:len(parts)], refs[len(parts):]
        i, k = pl.program_id(0), pl.program_id(2)

        @pl.when(k == 0)
        def _():
            acc[...] = jnp.zeros_like(acc)

        for a_ref, lo, cnt in zip(a_refs, firsts, counts):
            def accumulate(a_ref=a_ref):
                acc[...] += _dot(a_ref[...], b_ref[...], TN)

            if len(parts) == 1:
                accumulate()
            else:
                pl.when(jnp.logical_and(i >= lo, i < lo + cnt))(accumulate)

        @pl.when(k == nk - 1)
        def _():
            o_ref[...] = acc[...]

    def part_spec(lo, cnt):
        if len(parts) == 1:
            return pl.BlockSpec((tk, tm), lambda i, j, k: (k, i))

        def index(i, j, k):
            mine = jnp.logical_and(i >= lo, i < lo + cnt)
            return jnp.where(mine, k, 0), jnp.clip(i - lo, 0, cnt - 1)
        return pl.BlockSpec((tk, tm), index)

    out = _call_with_rider(
        body, rider, name=name, grid=(sum(counts), n // tn, nk),
        in_specs=[part_spec(lo, cnt) for lo, cnt in zip(firsts, counts)]
        + [pl.BlockSpec((tk, tn), lambda i, j, k: (k, j))],
        out_specs=[pl.BlockSpec((tm, tn), lambda i, j, k: (i, j))],
        out_shape=[jax.ShapeDtypeStruct((sum(widths), n), F32)],
        scratch_shapes=[pltpu.VMEM((tm, tn), F32)], operands=(*parts, b))
    return out[0] if rider is None else out


def mix_in_forward(x, gn, sc, sh, w_in):
    t, d = x.shape
    tm = _tile(t, ROW_TILE, 16)

    def body(x_ref, gn_ref, sc_ref, sh_ref, w_ref, h_ref, zc_ref, zm_ref):
        xhat, _ = _rms(x_ref[...])
        h = (xhat * gn_ref[...] * (1.0 + sc_ref[...]) + sh_ref[...]).astype(BF16)
        h_ref[...] = h
        z = _dot(h, w_ref[...], NT)
        zc_ref[...] = z[:, :ZC_COLS].astype(BF16)
        zm_ref[...] = z[:, ZC_COLS:].astype(BF16)

    row = pl.BlockSpec((tm, d), lambda i: (i, 0))
    vec = pl.BlockSpec((1, d), lambda i: (0, 0))
    return pl.pallas_call(
        body, name="mix_in_fwd", grid=(t // tm,),
        in_specs=[row, vec, vec, vec, _row(w_in)],
        out_specs=[row, pl.BlockSpec((tm, ZC_COLS), lambda i: (i, 0)), pl.BlockSpec((tm, ZM_COLS), lambda i: (i, 0))],
        out_shape=[jax.ShapeDtypeStruct((t, d), BF16), jax.ShapeDtypeStruct((t, ZC_COLS), BF16),
                   jax.ShapeDtypeStruct((t, ZM_COLS), BF16)],
        compiler_params=_params(("arbitrary",)),
    )(x, gn, sc, sh, w_in)


def rope_angles(pos, inv_freq):
    t = pos.shape[0]
    tm = _tile(t, ROW_TILE, 16)

    def body(pos_ref, if_ref, cos_ref, sin_ref):
        ang = pos_ref[...] * if_ref[...]
        cos_ref[...] = jnp.cos(ang)
        sin_ref[...] = jnp.sin(ang)

    table = jax.ShapeDtypeStruct((t, LANES), F32)
    rows = pl.BlockSpec((tm, LANES), lambda i: (i, 0))
    return pl.pallas_call(
        body, name="rope_angles", grid=(t // tm,),
        in_specs=[pl.BlockSpec((tm, 1), lambda i: (i, 0)), _row(inv_freq)], out_specs=[rows, rows],
        out_shape=[table, table], compiler_params=_params(("arbitrary",)),
    )(pos, inv_freq)


def _rope_tables(cos, sin):
    lane = lax.broadcasted_iota(jnp.int32, cos.shape, 1)
    half = QK_ROPE // 2
    return cos, jnp.where(lane < half, -sin, 0.0), jnp.where(jnp.logical_and(lane >= half, lane < QK_ROPE), sin, 0.0)


def _rope(v, tables):
    cos, sin_a, sin_b = tables
    return v * cos + pltpu.roll(v, LANES - QK_ROPE // 2, 1) * sin_a + pltpu.roll(v, QK_ROPE // 2, 1) * sin_b


def _rope_transposed(dv, tables):
    cos, sin_a, sin_b = tables
    return dv * cos + pltpu.roll(dv * sin_a, QK_ROPE // 2, 1) + pltpu.roll(dv * sin_b, LANES - QK_ROPE // 2, 1)


def mla_project(zm, cos, sin, qg, kvg, w_uq, w_ukv):
    t = zm.shape[0]
    tm = _tile(t, ROW_TILE, 16)

    def body(zm_ref, cos_ref, sin_ref, qg_ref, kvg_ref, wq_ref, wkv_ref, qn_ref, kvn_ref, q_ref, k_ref, v_ref):
        zv = zm_ref[...].astype(F32)
        qn = (_rms(zv[:, :Q_LORA])[0] * qg_ref[...]).astype(BF16)
        kvn = (_rms(zv[:, Q_LORA:Q_LORA + KV_LORA])[0] * kvg_ref[...]).astype(BF16)
        qn_ref[...] = qn
        kvn_ref[...] = kvn
        qf = _dot(qn, wq_ref[...], NT) * QK_FOLD
        kvf = _dot(kvn, wkv_ref[...], NT)
        tables = _rope_tables(cos_ref[...], sin_ref[...])
        kr = _rope(zv[:, Q_LORA + KV_LORA:], tables).astype(BF16)
        for h in range(MLA_HEADS):
            lo = h * HEAD_PAD
            q_ref[:, lo:lo + QK_NOPE] = qf[:, lo:lo + QK_NOPE].astype(BF16)
            q_ref[:, lo + QK_NOPE:lo + HEAD_PAD] = _rope(qf[:, lo + QK_NOPE:lo + HEAD_PAD], tables).astype(BF16)
            k_ref[:, lo:lo + QK_NOPE] = kvf[:, h * QK_NOPE:(h + 1) * QK_NOPE].astype(BF16)
            k_ref[:, lo + QK_NOPE:lo + HEAD_PAD] = kr
        v_ref[...] = kvf[:, MLA_HEADS * QK_NOPE:].astype(BF16)

    def rows(n):
        return pl.BlockSpec((tm, n), lambda i: (i, 0))

    return pl.pallas_call(
        body, name="mla_project", grid=(t // tm,),
        in_specs=[rows(ZM_COLS), rows(LANES), rows(LANES), _row(qg), _row(kvg), _row(w_uq), _row(w_ukv)],
        out_specs=[rows(Q_LORA), rows(KV_LORA), rows(QK_COLS), rows(QK_COLS), rows(MLA_WIDTH)],
        out_shape=[jax.ShapeDtypeStruct((t, Q_LORA), BF16), jax.ShapeDtypeStruct((t, KV_LORA), BF16),
                   jax.ShapeDtypeStruct((t, QK_COLS), BF16), jax.ShapeDtypeStruct((t, QK_COLS), BF16),
                   jax.ShapeDtypeStruct((t, MLA_WIDTH), BF16)],
        compiler_params=_params(("arbitrary",)),
    )(zm, cos, sin, qg, kvg, w_uq, w_ukv)


def _chunk_mask(shape, q_axis):
    qi = lax.broadcasted_iota(jnp.int32, shape, q_axis) // CHUNK
    ki = lax.broadcasted_iota(jnp.int32, shape, 1 - q_axis) // CHUNK
    return ki <= qi


def attention_forward(q, k, v):
    t = q.shape[0]
    tq = _tile(t, ATTN_TILE, CHUNK)

    def body(q_ref, k_ref, v_ref, o_ref, lse_ref):
        i = pl.program_id(1)
        qv = q_ref[...]

        def step(kb, carry, masked, tiles=1):
            m, l, acc = carry
            keys = pl.ds(pl.multiple_of(kb * tq, tq), tiles * tq)
            s = _dot(qv, k_ref[keys, :], NT)
            if masked:
                s = jnp.where(_chunk_mask(s.shape, 0), s, NEG_INF)
            m_new = jnp.maximum(m, jnp.max(s, axis=-1, keepdims=True))
            alpha = jnp.exp2(m - m_new)
            p = jnp.exp2(s - m_new)
            l = alpha * l + jnp.sum(p, axis=-1, keepdims=True)
            acc = alpha * acc + _dot(p.astype(BF16), v_ref[keys, :])
            return m_new, l, acc

        init = (jnp.full((tq, 1), NEG_INF, F32), jnp.zeros((tq, 1), F32), jnp.zeros((tq, V_HEAD), F32))
        carry = lax.fori_loop(0, i // 2, lambda pb, cr: step(2 * pb, cr, False, 2), init)
        carry = lax.fori_loop(0, i % 2, lambda _, cr: step(i - 1, cr, False), carry)
        m, l, acc = step(i, carry, True)
        o_ref[...] = (acc / l).astype(BF16)
        lse_ref[0] = m + jnp.log2(l)

    return pl.pallas_call(
        body, name="attn_fwd", grid=(MLA_HEADS, t // tq),
        in_specs=[pl.BlockSpec((tq, HEAD_PAD), lambda h, i: (i, h)),
                  pl.BlockSpec((t, HEAD_PAD), lambda h, i: (0, h)),
                  pl.BlockSpec((t, V_HEAD), lambda h, i: (0, h))],
        out_specs=[pl.BlockSpec((tq, V_HEAD), lambda h, i: (i, h)),
                   pl.BlockSpec((1, tq, 1), lambda h, i: (h, i, 0))],
        out_shape=[jax.ShapeDtypeStruct((t, MLA_WIDTH), BF16), jax.ShapeDtypeStruct((MLA_HEADS, t, 1), F32)],
        compiler_params=_params(("arbitrary", "arbitrary")),
    )(q, k, v)


def attention_backward(q, k, v, do, lse, delta, rider=None):
    t = q.shape[0]
    tq = _tile(t, ATTN_TILE, CHUNK)
    nq = t // tq

    def body(q_ref, k_ref, v_ref, do_ref, lse_ref, delta_ref, dq_ref, dk_ref, dv_ref, dq_acc):
        kb = pl.program_id(1)

        @pl.when(kb == 0)
        def _():
            dq_acc[...] = jnp.zeros_like(dq_acc)

        kv, vv = k_ref[...], v_ref[...]

        def step(qb, carry, masked):
            dk, dv = carry
            rows = pl.ds(pl.multiple_of(qb * tq, tq), tq)
            qv, dov = q_ref[rows, :], do_ref[rows, :]
            s = _dot(kv, qv, NT)
            if masked:
                s = jnp.where(_chunk_mask(s.shape, 1), s, NEG_INF)
            p = jnp.exp2(s - lse_ref[0, qb])
            dv = dv + _dot(p.astype(BF16), dov)
            dp = _dot(vv, dov, NT)
            ds = (p * (dp - delta_ref[0, qb]) * LN_2).astype(BF16)
            dk = dk + _dot(ds, qv)
            dq_acc[rows, :] += _dot(ds, kv, TN)
            return dk, dv

        carry = step(kb, (jnp.zeros((tq, HEAD_PAD), F32), jnp.zeros((tq, V_HEAD), F32)), True)
        odd = (nq - 1 - kb) % 2
        carry = lax.fori_loop(0, odd, lambda _, cr: step(kb + 1, cr, False), carry)
        first = kb + 1 + odd
        dk, dv = lax.fori_loop(0, (nq - first) // 2,
                               lambda pb, cr: step(first + 2 * pb + 1, step(first + 2 * pb, cr, False), False), carry)
        dk_ref[...] = dk.astype(BF16)
        dv_ref[...] = dv.astype(BF16)

        @pl.when(kb == nq - 1)
        def _():
            dq_ref[...] = dq_acc[...].astype(BF16)

    stat = pl.BlockSpec((1, nq, 1, tq), lambda h, j: (h, 0, 0, 0))
    return _call_with_rider(
        body, rider, name="attn_bwd", grid=(MLA_HEADS, nq),
        in_specs=[pl.BlockSpec((t, HEAD_PAD), lambda h, j: (0, h)),
                  pl.BlockSpec((tq, HEAD_PAD), lambda h, j: (j, h)),
                  pl.BlockSpec((tq, V_HEAD), lambda h, j: (j, h)),
                  pl.BlockSpec((t, V_HEAD), lambda h, j: (0, h)), stat, stat],
        out_specs=[pl.BlockSpec((t, HEAD_PAD), lambda h, j: (0, h)),
                   pl.BlockSpec((tq, HEAD_PAD), lambda h, j: (j, h)),
                   pl.BlockSpec((tq, V_HEAD), lambda h, j: (j, h))],
        out_shape=[jax.ShapeDtypeStruct((t, QK_COLS), BF16), jax.ShapeDtypeStruct((t, QK_COLS), BF16),
                   jax.ShapeDtypeStruct((t, MLA_WIDTH), BF16)],
        scratch_shapes=[pltpu.VMEM((t, HEAD_PAD), F32)], operands=(q, k, v, do, lse, delta))


HALO = 16


def _halo_spec(tm, n, step, last):
    return pl.BlockSpec((HALO, n), lambda i: (jnp.clip(i * (tm // HALO) + step, 0, last), 0))


def _shift_rows(v, prev, n):
    out = pltpu.roll(v, n, 0)
    row = lax.broadcasted_iota(jnp.int32, v.shape, 0)
    for r in range(n):
        out = jnp.where(row == r, prev[HALO - n + r:HALO - n + r + 1, :], out)
    return out


def _advance_rows(v, nxt, n):
    rows = v.shape[0]
    out = pltpu.roll(v, rows - n, 0)
    row = lax.broadcasted_iota(jnp.int32, v.shape, 0)
    for r in range(n):
        out = jnp.where(row == rows - n + r, nxt[r:r + 1, :], out)
    return out


def _conv_taps(zc, zc_prev, first):
    w = CONV_WIDTH
    u = zc[:, w:2 * w] * zc[:, 2 * w:]
    up = jnp.where(first, 0.0, zc_prev[:, w:2 * w] * zc_prev[:, 2 * w:])
    return u, _shift_rows(u, up, 1), _shift_rows(u, up, 2)


def mix_out_forward(zc, o, conv_w, og, gmat_a, gmat_b, w_out, x, gate):
    t, d = x.shape
    tm = _tile(t, ROW_TILE, 16)
    w = CONV_WIDTH

    def body(zc_ref, zp_ref, o_ref, cw_ref, og_ref, ga_ref, gb_ref, w_ref, x_ref, gate_ref,
             xo_ref, yn_ref, y_ref, ya_ref):
        zc_v = zc_ref[...].astype(F32)
        u, u1, u2 = _conv_taps(zc_v, zp_ref[...].astype(F32), pl.program_id(0) == 0)
        cw = cw_ref[...]
        ya = zc_v[:, :w] * (cw[0:1] * u2 + cw[1:2] * u1 + cw[2:3] * u)
        ya_ref[...] = ya.astype(BF16)
        ov = o_ref[...].astype(F32)
        ogv = og_ref[...]
        yn_ref[:, :w] = (ya * lax.rsqrt(_group_mean(ya * ya, ga_ref[...]) + EPS) * ogv[:, :w]).astype(BF16)
        yn_ref[:, w:] = (ov * lax.rsqrt(_group_mean(ov * ov, gb_ref[...]) + EPS) * ogv[:, w:]).astype(BF16)
        y = _dot(yn_ref[...], w_ref[...])
        y_ref[...] = y.astype(BF16)
        xo_ref[...] = x_ref[...] + gate_ref[...] * y

    def rows(n):
        return pl.BlockSpec((tm, n), lambda i: (i, 0))

    return pl.pallas_call(
        body, name="mix_out_fwd", grid=(t // tm,),
        in_specs=[rows(ZC_COLS), _halo_spec(tm, ZC_COLS, -1, t // HALO - 1), rows(MLA_WIDTH), _row(conv_w), _row(og),
                  _row(gmat_a), _row(gmat_b), _row(w_out), rows(d), _row(gate)],
        out_specs=[rows(d), rows(MIX_WIDTH), rows(d), rows(w)],
        out_shape=[jax.ShapeDtypeStruct((t, d), F32), jax.ShapeDtypeStruct((t, MIX_WIDTH), BF16),
                   jax.ShapeDtypeStruct((t, d), BF16), jax.ShapeDtypeStruct((t, w), BF16)],
        compiler_params=_params(("arbitrary",)),
    )(zc, zc, o, conv_w, og, gmat_a, gmat_b, w_out, x, gate)


def _group_norm_bwd(dyn, y, og, gmat):
    rs = lax.rsqrt(_group_mean(y * y, gmat) + EPS)
    yhat = y * rs
    d_og = jnp.sum(dyn * yhat, axis=0, keepdims=True)
    dyh = dyn * og
    return rs * (dyh - yhat * _group_mean(dyh * yhat, gmat)), d_og


def mix_out_backward(dxo, y, gate, ya, o, og, gmat_a, gmat_b, w_out, rider=None):
    t, d = dxo.shape
    tm = _tile(t, ROW_TILE, 16)
    w = CONV_WIDTH

    def body(dxo_ref, y_ref, gate_ref, ya_ref, o_ref, og_ref, ga_ref, gb_ref, w_ref,
             dy_ref, dya_ref, do_ref, delta_ref, sd_ref, so_ref):
        @pl.when(pl.program_id(0) == 0)
        def _():
            sd_ref[...] = jnp.zeros_like(sd_ref)
            so_ref[...] = jnp.zeros_like(so_ref)

        dxo_v = dxo_ref[...]
        dy = (gate_ref[...] * dxo_v).astype(BF16)
        dy_ref[...] = dy
        sd_ref[0:1, :] += jnp.sum(dxo_v * y_ref[...].astype(F32), axis=0, keepdims=True)
        dyn = _dot(dy, w_ref[...], NT)
        ogv = og_ref[...]
        ov = o_ref[...].astype(F32)
        dya, d_og_a = _group_norm_bwd(dyn[:, :w], ya_ref[...].astype(F32), ogv[:, :w], ga_ref[...])
        dov, d_og_b = _group_norm_bwd(dyn[:, w:], ov, ogv[:, w:], gb_ref[...])
        dya_ref[...] = dya.astype(BF16)
        do_ref[...] = dov.astype(BF16)
        so_ref[0:1, :w] += d_og_a
        so_ref[0:1, w:] += d_og_b
        prod = dov * ov
        for h in range(MLA_HEADS):
            delta_ref[h] = jnp.sum(prod[:, h * V_HEAD:(h + 1) * V_HEAD], axis=-1, keepdims=True)

    def rows(n):
        return pl.BlockSpec((tm, n), lambda i: (i, 0))

    return _call_with_rider(
        body, rider, name="mix_out_bwd", grid=(t // tm,),
        in_specs=[rows(d), rows(d), _row(gate), rows(w), rows(MLA_WIDTH), _row(og), _row(gmat_a), _row(gmat_b),
                  _row(w_out)],
        out_specs=[rows(d), rows(w), rows(MLA_WIDTH), pl.BlockSpec((MLA_HEADS, tm, 1), lambda i: (0, i, 0)),
                   pl.BlockSpec((8, d), lambda i: (0, 0)), pl.BlockSpec((8, MIX_WIDTH), lambda i: (0, 0))],
        out_shape=[jax.ShapeDtypeStruct((t, d), BF16), jax.ShapeDtypeStruct((t, w), BF16),
                   jax.ShapeDtypeStruct((t, MLA_WIDTH), BF16), jax.ShapeDtypeStruct((MLA_HEADS, t, 1), F32),
                   jax.ShapeDtypeStruct((8, d), F32), jax.ShapeDtypeStruct((8, MIX_WIDTH), F32)],
        scratch_shapes=[], operands=(dxo, y, gate, ya, o, og, gmat_a, gmat_b, w_out))


def conv_backward(zc, dya, conv_w):
    t = zc.shape[0]
    tm = _tile(t, ROW_TILE, 16)
    nt = t // tm
    w = CONV_WIDTH

    def body(zc_ref, zp_ref, zn_ref, dya_ref, dn_ref, cw_ref, dzc_ref, sums_ref):
        i = pl.program_id(0)

        @pl.when(i == 0)
        def _():
            sums_ref[...] = jnp.zeros_like(sums_ref)

        zc_v = zc_ref[...].astype(F32)
        u, u1, u2 = _conv_taps(zc_v, zp_ref[...].astype(F32), i == 0)
        cw = cw_ref[...]
        dya_v = dya_ref[...].astype(F32)
        dyc = dya_v * zc_v[:, :w]
        dyc_next = jnp.where(i == nt - 1, 0.0, dn_ref[...].astype(F32) * zn_ref[:, :w].astype(F32))
        du = cw[2:3] * dyc + cw[1:2] * _advance_rows(dyc, dyc_next, 1) + cw[0:1] * _advance_rows(dyc, dyc_next, 2)
        dzc_ref[:, :w] = (dya_v * (cw[0:1] * u2 + cw[1:2] * u1 + cw[2:3] * u)).astype(BF16)
        dzc_ref[:, w:2 * w] = (du * zc_v[:, 2 * w:]).astype(BF16)
        dzc_ref[:, 2 * w:] = (du * zc_v[:, w:2 * w]).astype(BF16)
        _add_rows(sums_ref, [jnp.sum(dyc * tap, axis=0, keepdims=True) for tap in (u2, u1, u)])

    def rows(n):
        return pl.BlockSpec((tm, n), lambda i: (i, 0))

    def halo(n, step):
        return _halo_spec(tm, n, step, t // HALO - 1)

    return pl.pallas_call(
        body, name="conv_bwd", grid=(nt,),
        in_specs=[rows(ZC_COLS), halo(ZC_COLS, -1), halo(ZC_COLS, tm // HALO), rows(w), halo(w, tm // HALO),
                  _row(conv_w)],
        out_specs=[rows(ZC_COLS), pl.BlockSpec((8, w), lambda i: (0, 0))],
        out_shape=[jax.ShapeDtypeStruct((t, ZC_COLS), BF16), jax.ShapeDtypeStruct((8, w), F32)],
        compiler_params=_params(("arbitrary",)),
    )(zc, zc, zc, dya, dya, conv_w)


def _rms_bwd(dy, x, g):
    xhat, r = _rms(x)
    d_g = jnp.sum(dy * xhat, axis=0, keepdims=True)
    dxh = dy * g
    return r * (dxh - xhat * jnp.mean(dxh * xhat, axis=-1, keepdims=True)), d_g


def mla_project_backward(dq, dk, dv, zm, cos, sin, qg, kvg, w_uq, w_ukv):
    t = zm.shape[0]
    tm = _tile(t, ROW_TILE, 16)

    def body(dq_ref, dk_ref, dv_ref, zm_ref, cos_ref, sin_ref, qg_ref, kvg_ref, wq_ref, wkv_ref,
             dql_ref, dkvl_ref, dzm_ref, sums_ref):
        @pl.when(pl.program_id(0) == 0)
        def _():
            sums_ref[...] = jnp.zeros_like(sums_ref)

        tables = _rope_tables(cos_ref[...], sin_ref[...])
        dkr = jnp.zeros((tm, LANES), F32)
        for h in range(MLA_HEADS):
            lo = h * HEAD_PAD
            dql_ref[:, lo:lo + QK_NOPE] = (dq_ref[:, lo:lo + QK_NOPE].astype(F32) * QK_FOLD).astype(BF16)
            dql_ref[:, lo + QK_NOPE:lo + HEAD_PAD] = _rope_transposed(
                dq_ref[:, lo + QK_NOPE:lo + HEAD_PAD].astype(F32) * QK_FOLD, tables).astype(BF16)
            dkvl_ref[:, h * QK_NOPE:(h + 1) * QK_NOPE] = dk_ref[:, lo:lo + QK_NOPE]
            dkr = dkr + dk_ref[:, lo + QK_NOPE:lo + HEAD_PAD].astype(F32)
        dkvl_ref[:, MLA_HEADS * QK_NOPE:] = dv_ref[...]
        zv = zm_ref[...].astype(F32)
        dqn = _dot(dql_ref[...], wq_ref[...])
        dkvn = _dot(dkvl_ref[...], wkv_ref[...])
        dcq, d_qg = _rms_bwd(dqn, zv[:, :Q_LORA], qg_ref[...])
        dckv, d_kvg = _rms_bwd(dkvn, zv[:, Q_LORA:Q_LORA + KV_LORA], kvg_ref[...])
        dzm_ref[:, :Q_LORA] = dcq.astype(BF16)
        dzm_ref[:, Q_LORA:Q_LORA + KV_LORA] = dckv.astype(BF16)
        dzm_ref[:, Q_LORA + KV_LORA:] = _rope_transposed(dkr, tables).astype(BF16)
        sums_ref[0:1, :Q_LORA] += d_qg
        sums_ref[0:1, Q_LORA:Q_LORA + KV_LORA] += d_kvg

    def rows(n):
        return pl.BlockSpec((tm, n), lambda i: (i, 0))

    return pl.pallas_call(
        body, name="mla_project_bwd", grid=(t // tm,),
        in_specs=[rows(QK_COLS), rows(QK_COLS), rows(MLA_WIDTH), rows(ZM_COLS), rows(LANES), rows(LANES),
                  _row(qg), _row(kvg), _row(w_uq), _row(w_ukv)],
        out_specs=[rows(QK_COLS), rows(QK_COLS), rows(ZM_COLS), pl.BlockSpec((8, ZM_COLS), lambda i: (0, 0))],
        out_shape=[jax.ShapeDtypeStruct((t, QK_COLS), BF16), jax.ShapeDtypeStruct((t, QK_COLS), BF16),
                   jax.ShapeDtypeStruct((t, ZM_COLS), BF16), jax.ShapeDtypeStruct((8, ZM_COLS), F32)],
        compiler_params=_params(("arbitrary",)),
    )(dq, dk, dv, zm, cos, sin, qg, kvg, w_uq, w_ukv)


def mix_in_backward(dzc, dzm, w_in, x, dxo, gn, sc, gate, rider=None):
    t, d = x.shape
    tm = _tile(t, ROW_TILE, 16)

    def body(dzc_ref, dzm_ref, w_ref, x_ref, dxo_ref, gn_ref, sc_ref, gate_ref, dx_ref, dy_ref, sums_ref):
        @pl.when(pl.program_id(0) == 0)
        def _():
            sums_ref[...] = jnp.zeros_like(sums_ref)

        dh = _dot(dzc_ref[...], w_ref[:ZC_COLS, :]) + _dot(dzm_ref[...], w_ref[ZC_COLS:, :])
        dx, d_sh, d_sc, d_gn = _norm_mod_bwd(dh, x_ref[...], gn_ref[...], sc_ref[...])
        dx = dxo_ref[...] + dx
        dx_ref[...] = dx
        dy_ref[...] = (0.5 * gate_ref[...] * dx).astype(BF16)
        _add_rows(sums_ref, [d_sh, d_sc, d_gn])

    def rows(n):
        return pl.BlockSpec((tm, n), lambda i: (i, 0))

    return _call_with_rider(
        body, rider, name="mix_in_bwd", grid=(t // tm,),
        in_specs=[rows(ZC_COLS), rows(ZM_COLS), _row(w_in), rows(d), rows(d), _row(gn), _row(sc), _row(gate)],
        out_specs=[rows(d), rows(d), pl.BlockSpec((8, d), lambda i: (0, 0))],
        out_shape=[jax.ShapeDtypeStruct((t, d), F32), jax.ShapeDtypeStruct((t, d), BF16),
                   jax.ShapeDtypeStruct((8, d), F32)],
        scratch_shapes=[], operands=(dzc, dzm, w_in, x, dxo, gn, sc, gate))


def final_loss(x, target, g, gate):
    t, d = x.shape
    tm = _tile(t, ROW_TILE, 16)

    def body(x_ref, t_ref, g_ref, gate_ref, dx_ref, dy_ref, sums_ref):
        @pl.when(pl.program_id(0) == 0)
        def _():
            sums_ref[...] = jnp.zeros_like(sums_ref)

        gv = g_ref[...]
        xhat, r = _rms(x_ref[...])
        err = xhat * gv - t_ref[...]
        dyf = err * (1.0 / d)
        dxh = dyf * gv
        dx = r * (dxh - xhat * jnp.mean(dxh * xhat, axis=-1, keepdims=True))
        dx_ref[...] = dx
        dy_ref[...] = (0.5 * gate_ref[...] * dx).astype(BF16)
        _add_rows(sums_ref, [jnp.sum(dyf * xhat, axis=0, keepdims=True),
                             jnp.sum(err * err, axis=0, keepdims=True) * (0.5 / d)])

    row = pl.BlockSpec((tm, d), lambda i: (i, 0))
    return pl.pallas_call(
        body, name="final_loss", grid=(t // tm,),
        in_specs=[row, row, _row(g), _row(gate)],
        out_specs=[row, row, pl.BlockSpec((8, d), lambda i: (0, 0))],
        out_shape=[jax.ShapeDtypeStruct((t, d), F32), jax.ShapeDtypeStruct((t, d), BF16),
                   jax.ShapeDtypeStruct((8, d), F32)],
        compiler_params=_params(("arbitrary",)),
    )(x, target, g, gate)


def adamw(w, g, m, v, name):
    r, n = w.shape
    tr = _tile(r, max(8, (1 << 19) // n), 8)

    def body(w_ref, g_ref, m_ref, v_ref, d_ref, mo_ref, vo_ref):
        gv = g_ref[...]
        m_new = ADAM_B1 * m_ref[...] + (1.0 - ADAM_B1) * gv
        v_new = ADAM_B2 * v_ref[...] + (1.0 - ADAM_B2) * (gv * gv)
        m_hat = m_new / (1.0 - ADAM_B1 ** ADAM_STEP)
        v_hat = v_new / (1.0 - ADAM_B2 ** ADAM_STEP)
        d_ref[...] = -ADAM_LR * (m_hat / (jnp.sqrt(v_hat) + ADAM_EPS) + ADAM_WD * w_ref[...])
        mo_ref[...] = m_new
        vo_ref[...] = v_new

    blk = pl.BlockSpec((tr, n), lambda i: (i, 0))
    shape = jax.ShapeDtypeStruct((r, n), F32)
    return pl.pallas_call(
        body, name=name, grid=(r // tr,), in_specs=[blk] * 4, out_specs=[blk] * 3, out_shape=[shape] * 3,
        compiler_params=_params(("arbitrary",)),
    )(w, g, m, v)


def _pad_to(v, n):
    return jnp.pad(v, (0, n - v.shape[0]))


def _pad_heads(w, axis_len):
    n = w.shape[1]
    return jnp.pad(w.reshape(MLA_HEADS, axis_len, n), ((0, 0), (0, HEAD_PAD - axis_len), (0, 0))).reshape(-1, n)


def _swap_head_parts(w, inner, outer):
    n = w.shape[1]
    return w.reshape(outer, inner, QK_NOPE, n).transpose(1, 0, 2, 3).reshape(-1, n)


def kernel(x, c, positions, ada_w, ada_b, norm_ffn1_g, ffn1_w1, ffn1_w3, ffn1_w2, norm_mix_g, w_in, conv_w, q_norm_g, w_uq, kv_norm_g, w_ukv, out_norm_g, w_out, norm_ffn2_g, ffn2_w1, ffn2_w3, ffn2_w2, final_norm_g, loss_target, m_ada_w, m_ada_b, m_norm_ffn1_g, m_ffn1_w1, m_ffn1_w3, m_ffn1_w2, m_norm_mix_g, m_w_in, m_conv_w, m_q_norm_g, m_w_uq, m_kv_norm_g, m_w_ukv, m_out_norm_g, m_w_out, m_norm_ffn2_g, m_ffn2_w1, m_ffn2_w3, m_ffn2_w2, m_final_norm_g, v_ada_w, v_ada_b, v_norm_ffn1_g, v_ffn1_w1, v_ffn1_w3, v_ffn1_w2, v_norm_mix_g, v_w_in, v_conv_w, v_q_norm_g, v_w_uq, v_kv_norm_g, v_w_ukv, v_out_norm_g, v_w_out, v_norm_ffn2_g, v_ffn2_w1, v_ffn2_w3, v_ffn2_w2, v_final_norm_g):
    t, d = x.shape[1], x.shape[2]
    f = ffn1_w2.shape[1] * N_DEV
    me = 4 * lax.axis_index("x") + 2 * lax.axis_index("y") + lax.axis_index("c")
    my_c = lax.axis_index("c")
    my_chip = 2 * lax.axis_index("x") + lax.axis_index("y")
    xs = x[0]
    n_ada = ada_w.shape[2]
    cw_n = conv_w.shape[2]

    c_rows = jnp.broadcast_to(c, (8, d))
    conv_rows = jnp.pad(conv_w[0], ((0, 8 - CONV_K), (0, LANES - cw_n)))
    ffn1_blocks = jnp.stack([ffn1_w1[0].T, ffn1_w3[0].T, ffn1_w2[0]]).astype(BF16)
    ffn2_blocks = jnp.stack([ffn2_w1[0].T, ffn2_w3[0].T, ffn2_w2[0]]).astype(BF16)
    c_all, conv_all, ffn1_all = all_gather([c_rows, conv_rows, ffn1_blocks], [0, 0, 1], "gather_first")
    c_all = c_all[:, 0, :]
    conv_full8 = conv_all[:, :, :cw_n].transpose(1, 0, 2).reshape(8, CONV_WIDTH)
    ffn1_ws = ffn1_all.reshape(3, f, d)
    gather_rest = riding_gather(
        [ffn2_blocks, w_in[0].T.astype(BF16), w_uq[0].T.astype(BF16), w_ukv[0].T.astype(BF16), w_out[0].astype(BF16)],
        [1, 0, 0, 0, 0])

    ada_b_cols = lax.dynamic_slice_in_dim(ada_b, me * n_ada, n_ada, axis=1)
    mod_cols = ada_forward(c_all, ada_w[0], ada_b_cols)
    mod_all, = all_gather([mod_cols], [0], "gather_mod")
    mod = lax.dynamic_index_in_dim(mod_all, me, axis=1, keepdims=False).reshape(N_MOD, 1, d)
    sh1, sc1, g1, sh2, sc2, g2, sh3, sc3, g3 = [mod[i] for i in range(N_MOD)]

    gf = final_norm_g.reshape(1, d)
    x1, h1, a1, b1, y1, *gathered = ffn_forward(xs, norm_ffn1_g, sc1, sh1, g1, ffn1_ws, 0, "ffn1_fwd", gather_rest)
    ffn2_ws = gathered[0].reshape(3, f, d)
    w_in_p = jnp.pad(gathered[1].reshape(IN_COLS, d), ((0, ZC_COLS + ZM_COLS - IN_COLS), (0, 0)))
    w_uq_p = _pad_heads(gathered[2].reshape(-1, Q_LORA), QK_NOPE + QK_ROPE)
    w_ukv_p = _swap_head_parts(gathered[3].reshape(-1, KV_LORA), 2, MLA_HEADS)
    w_out_f = gathered[4].reshape(MIX_WIDTH, d)
    h2, zc, zm = mix_in_forward(x1, norm_mix_g, sc2, sh2, w_in_p)
    pos = positions[0].astype(F32).reshape(t, 1)
    inv_freq = ROPE_THETA ** (-jnp.arange(0, QK_ROPE, 2, dtype=F32) / QK_ROPE)
    inv_freq = jnp.concatenate([inv_freq, inv_freq, jnp.zeros((LANES - QK_ROPE,), F32)]).reshape(1, LANES)
    cos, sin = rope_angles(pos, inv_freq)
    qn, kvn, q, k, v = mla_project(zm, cos, sin, q_norm_g, kv_norm_g, w_uq_p, w_ukv_p)
    o, lse = attention_forward(q, k, v)
    lane = jnp.arange(CONV_WIDTH)
    gmat_a = (lane[:, None] // (CONV_WIDTH // CONV_GROUPS) == lane[None, :] // (CONV_WIDTH // CONV_GROUPS))
    gmat_a = (gmat_a / (CONV_WIDTH // CONV_GROUPS)).astype(BF16)
    gmat_b = ((lane[:, None] // V_HEAD == lane[None, :] // V_HEAD) / V_HEAD).astype(BF16)
    x2, yn, y2, ya = mix_out_forward(zc, o, conv_full8, out_norm_g, gmat_a, gmat_b, w_out_f, x1, g2)
    x3, h3, a3, b3, y3 = ffn_forward(x2, norm_ffn2_g, sc3, sh3, g3, ffn2_ws, 0, "ffn2_fwd")
    dx3, dy3, sums_f = final_loss(x3, loss_target[0], gf, g3)

    chip_idx = jnp.bitwise_xor(my_chip, jnp.array([0, 2, 1, 3], jnp.int32)).astype(jnp.int32)
    src_idx = (2 * chip_idx + my_c).astype(jnp.int32)

    def row_blocks(named):
        return [g.reshape(N_DEV, g.shape[0] // N_DEV, g.shape[1]) for _, g in named]

    def chip_sums(named, g8, got):
        return [add_sibling(g, r, src_idx, chip_idx, "rs_add_" + n) for g, r, (n, _) in zip(g8, got, named)]

    da3, db3, u3 = ffn_backward_gate(dy3, a3, b3, ffn2_ws, 0, "ffn2_bwd_gate")
    dx2, sums_3 = ffn_backward_norm(da3, db3, dx3, x2, y3, norm_ffn2_g, sc3, ffn2_ws, 0, "ffn2_bwd_norm")
    ffn2_named = [("ffn2_w1", matmul_tn(da3, h3, "ffn2_gw1")), ("ffn2_w3", matmul_tn(db3, h3, "ffn2_gw3")),
                  ("ffn2_w2", matmul_tn(u3, dy3, "ffn2_gw2"))]
    ffn2_g8 = row_blocks(ffn2_named)
    dy2, dya, do, delta, sums_2d, sums_2o, *ffn2_sib = mix_out_backward(
        dx2, y2, g2, ya, o, out_norm_g, gmat_a, gmat_b, w_out_f, riding_sibling(ffn2_g8))
    ffn2_sums = chip_sums(ffn2_named, ffn2_g8, ffn2_sib)
    g_w_out = matmul_tn(yn, dy2, "gw_out")
    nq = t // _tile(t, ATTN_TILE, CHUNK)
    stat_shape = (MLA_HEADS, nq, 1, t // nq)
    dq, dk, dv, *ffn2_got = attention_backward(q, k, v, do, lse.reshape(stat_shape), delta.reshape(stat_shape),
                                               riding_exchange([s[1] for s in ffn2_sums]))
    dzc, sums_c = conv_backward(zc, dya, conv_full8)
    dql, dkvl, dzm, sums_m = mla_project_backward(dq, dk, dv, zm, cos, sin, q_norm_g, kv_norm_g, w_uq_p, w_ukv_p)
    g_w_uq_p = matmul_tn(dql, qn, "gw_uq")
    g_w_ukv_p = matmul_tn(dkvl, kvn, "gw_ukv")
    g_w_in = matmul_tn([dzc, dzm], h2, "gw_in")[:IN_COLS]
    g_w_uq = g_w_uq_p.reshape(MLA_HEADS, HEAD_PAD, Q_LORA)[:, :QK_NOPE + QK_ROPE].reshape(-1, Q_LORA)
    g_w_ukv = _swap_head_parts(g_w_ukv_p, MLA_HEADS, 2)
    mix_named = [("w_in", g_w_in), ("w_uq", g_w_uq), ("w_ukv", g_w_ukv), ("w_out", g_w_out)]
    mix_g8 = row_blocks(mix_named)
    dx1, dy1, sums_1m, *mix_sib = mix_in_backward(dzc, dzm, w_in_p, x1, dx2, norm_mix_g, sc2, g1, riding_sibling(mix_g8))
    mix_sums = chip_sums(mix_named, mix_g8, mix_sib)
    da1, db1, u1, *mix_got = ffn_backward_gate(dy1, a1, b1, ffn1_ws, 0, "ffn1_bwd_gate",
                                               riding_exchange([s[1] for s in mix_sums]))
    ffn1_pair = [("ffn1_w1", matmul_tn(da1, h1, "ffn1_gw1")), ("ffn1_w3", matmul_tn(db1, h1, "ffn1_gw3"))]
    pair_g8 = row_blocks(ffn1_pair)
    g_w2a, *pair_sib = matmul_tn(u1, dy1, "ffn1_gw2", riding_sibling(pair_g8))
    ffn1_last = [("ffn1_w2", g_w2a)]
    last_g8 = row_blocks(ffn1_last)
    ffn1_named = ffn1_pair + ffn1_last
    ffn1_sums = chip_sums(ffn1_pair, pair_g8, pair_sib) + chip_sums(
        ffn1_last, last_g8, exchange_sibling(last_g8, "rs_sibling_ffn1_w2"))
    dx0, sums_1, *ffn1_got = ffn_backward_norm(da1, db1, dx1, xs, y1, norm_ffn1_g, sc1, ffn1_ws, 0, "ffn1_bwd_norm",
                                               riding_exchange([s[1] for s in ffn1_sums]))
    transposed = {"ffn1_w1", "ffn1_w3", "ffn2_w1", "ffn2_w3", "w_in", "w_uq", "w_ukv"}
    g_sh = {}
    for named, group_sums, group_got in ((ffn2_named, ffn2_sums, ffn2_got), (mix_named, mix_sums, mix_got),
                                         (ffn1_named, ffn1_sums, ffn1_got)):
        for (n, _), (own, _), got in zip(named, group_sums, group_got):
            g_rows = add_received(own, got, "rs_sum_" + n)
            g_sh[n] = g_rows.T if n in transposed else g_rows

    dmod = jnp.concatenate([sums_1[0], sums_1[1], sums_1[2], sums_1m[0], sums_1m[1], sums_2d[0],
                            sums_3[0], sums_3[1], sums_3[2]])
    pieces = [dmod, sums_1[3], sums_1m[2], sums_m[0, :Q_LORA], sums_m[0, Q_LORA:Q_LORA + KV_LORA], sums_2o[0],
              sums_3[3], sums_f[0], sums_f[1], sums_c[:CONV_K].reshape(-1)]
    plens = [p.shape[0] for p in pieces]
    poffs = [sum(plens[:i]) for i in range(len(plens))]
    vec_len = -(-sum(plens) // 1024) * 1024
    vec = _pad_to(jnp.concatenate(pieces), vec_len).reshape(-1, LANES)
    vec_all, = all_gather([vec], [0], "gather_sums")
    tot = sum_devices(vec_all).reshape(-1)
    g_ada_b, g_n1, g_nmix, g_qg, g_kvg, g_og, g_n3, g_gf, loss_lanes, g_conv_full = [
        tot[o:o + n] for o, n in zip(poffs, plens)]
    loss = sum_lanes(loss_lanes.reshape(1, d))[0, 0]
    g_conv = lax.dynamic_slice_in_dim(g_conv_full.reshape(CONV_K, CONV_WIDTH), me * cw_n, cw_n, axis=1)
    dmod_all = vec_all.reshape(N_DEV, vec_len)[:, :N_MOD * d]
    dmod_cols = lax.dynamic_slice_in_dim(dmod_all, me * n_ada, n_ada, axis=1)
    g_ada_w = ada_backward(jnp.pad(c_all, ((0, 8), (0, 0))), jnp.pad(dmod_cols, ((0, 8), (0, 0))))

    def update(name, w, g, m, v):
        shape = w.shape
        two_d = (-1, shape[-1])
        dlt, nm, nv = adamw(w.reshape(two_d), g.reshape(two_d), m.reshape(two_d), v.reshape(two_d), "adamw_" + name)
        return g.reshape(shape), dlt.reshape(shape), nm.reshape(shape), nv.reshape(shape)

    res = {}
    res["ada_w"] = update("ada_w", ada_w, g_ada_w, m_ada_w, v_ada_w)
    big = [("ffn1_w1", ffn1_w1, m_ffn1_w1, v_ffn1_w1), ("ffn1_w3", ffn1_w3, m_ffn1_w3, v_ffn1_w3),
           ("ffn2_w1", ffn2_w1, m_ffn2_w1, v_ffn2_w1), ("ffn2_w3", ffn2_w3, m_ffn2_w3, v_ffn2_w3),
           ("w_in", w_in, m_w_in, v_w_in), ("w_uq", w_uq, m_w_uq, v_w_uq), ("w_ukv", w_ukv, m_w_ukv, v_w_ukv),
           ("ffn1_w2", ffn1_w2, m_ffn1_w2, v_ffn1_w2), ("ffn2_w2", ffn2_w2, m_ffn2_w2, v_ffn2_w2),
           ("w_out", w_out, m_w_out, v_w_out)]
    for name, w, m, v in big:
        res[name] = update(name, w, g_sh[name], m, v)
    smalls = [("ada_b", ada_b, g_ada_b, m_ada_b, v_ada_b),
              ("norm_ffn1_g", norm_ffn1_g, g_n1, m_norm_ffn1_g, v_norm_ffn1_g),
              ("norm_mix_g", norm_mix_g, g_nmix, m_norm_mix_g, v_norm_mix_g),
              ("conv_w", conv_w, g_conv, m_conv_w, v_conv_w),
              ("q_norm_g", q_norm_g, g_qg, m_q_norm_g, v_q_norm_g),
              ("kv_norm_g", kv_norm_g, g_kvg, m_kv_norm_g, v_kv_norm_g),
              ("out_norm_g", out_norm_g, g_og, m_out_norm_g, v_out_norm_g),
              ("norm_ffn2_g", norm_ffn2_g, g_n3, m_norm_ffn2_g, v_norm_ffn2_g),
              ("final_norm_g", final_norm_g, g_gf, m_final_norm_g, v_final_norm_g)]
    slens = [w.size for _, w, _, _, _ in smalls]
    soffs = [sum(slens[:i]) for i in range(len(slens))]
    s_len = -(-sum(slens) // 1024) * 1024

    def pack_small(i):
        return _pad_to(jnp.concatenate([s[i].reshape(-1) for s in smalls]), s_len).reshape(8, -1)

    s_out = adamw(pack_small(1), pack_small(2), pack_small(3), pack_small(4), "adamw_small")
    for (name, w, g, _, _), o, n in zip(smalls, soffs, slens):
        res[name] = (g.reshape(w.shape),) + tuple(a.reshape(-1)[o:o + n].reshape(w.shape) for a in s_out)

    order = ["ada_w", "ada_b", "norm_ffn1_g", "ffn1_w1", "ffn1_w3", "ffn1_w2", "norm_mix_g", "w_in", "conv_w",
             "q_norm_g", "w_uq", "kv_norm_g", "w_ukv", "out_norm_g", "w_out", "norm_ffn2_g", "ffn2_w1", "ffn2_w3",
             "ffn2_w2", "final_norm_g"]
    return (loss, dx0.reshape(x.shape), *[res[n][0] for n in order], *[res[n][1] for n in order],
            *[res[n][2] for n in order], *[res[n][3] for n in order])
```

```python
import functools
import math

import jax
import jax.numpy as jnp
from jax import lax
from jax.experimental import pallas as pl
from jax.experimental.pallas import tpu as pltpu

F32 = jnp.float32
BF16 = jnp.bfloat16
MESH_ID = pl.DeviceIdType.MESH
N_DEV = 8

EPS = 1e-6
CHUNK = 64
N_MOD = 9
CONV_WIDTH = 512
CONV_GROUPS = 8
CONV_K = 3
MLA_HEADS = 4
QK_NOPE = 128
QK_ROPE = 64
V_HEAD = 128
Q_LORA = 384
KV_LORA = 256
ROPE_THETA = 10000.0
MLA_WIDTH = MLA_HEADS * V_HEAD
MIX_WIDTH = CONV_WIDTH + MLA_WIDTH
IN_COLS = 3 * CONV_WIDTH + Q_LORA + KV_LORA + QK_ROPE
ZC_COLS = 3 * CONV_WIDTH
ZM_COLS = Q_LORA + KV_LORA + 128
HEAD_PAD = 256
QK_COLS = MLA_HEADS * HEAD_PAD
ATTN_SCALE = (QK_NOPE + QK_ROPE) ** -0.5
LOG2_E = 1.4426950408889634
LN_2 = 0.6931471805599453
QK_FOLD = ATTN_SCALE * LOG2_E
NEG_INF = -1e30

ADAM_LR = 0.001
ADAM_B1 = 0.9
ADAM_B2 = 0.999
ADAM_EPS = 1e-08
ADAM_WD = 0.01
ADAM_STEP = 10

LANES = 128
MXU_COLS = 256
VMEM_LIMIT = 56 * 1024 * 1024
ROW_TILE = 1024
FFN_FWD_TILE = (1024, 256)
FFN_BWD_TILE = (512, 1408)
GRAD_TILE = 1408
GRAD_DEPTH = 2048
SUM_ROWS = 256
ATTN_TILE = 512

NN = (((1,), (0,)), ((), ()))
NT = (((1,), (1,)), ((), ()))
TN = (((0,), (0,)), ((), ()))


def _dot(a, b, dims=NN):
    return lax.dot_general(a, b, dims, preferred_element_type=F32)


def _tile(n, cap, mult=LANES):
    best = None
    for t in range(mult, min(n, cap) + 1, mult):
        if n % t == 0:
            best = t
    return n if best is None else best


def _params(sem=None):
    return pltpu.CompilerParams(dimension_semantics=sem, vmem_limit_bytes=VMEM_LIMIT)


def _row(v):
    return pl.BlockSpec(v.shape, lambda *_: (0,) * v.ndim)


def _sigmoid(x):
    return 0.5 * jnp.tanh(0.5 * x) + 0.5


def _rms(x):
    r = lax.rsqrt(jnp.mean(x * x, axis=-1, keepdims=True) + EPS)
    return x * r, r


def _norm_mod_bwd(dh, x, gn, sc):
    xhat, r = _rms(x)
    d_sh = jnp.sum(dh, axis=0, keepdims=True)
    d_sc = jnp.sum(dh * (xhat * gn), axis=0, keepdims=True)
    dxn = dh * (1.0 + sc)
    d_gn = jnp.sum(dxn * xhat, axis=0, keepdims=True)
    dxh = dxn * gn
    dx = r * (dxh - xhat * jnp.mean(dxh * xhat, axis=-1, keepdims=True))
    return dx, d_sh, d_sc, d_gn


def _group_mean(v, gmat):
    return _dot(v.astype(BF16), gmat)


def _add_rows(ref, rows):
    for r, v in enumerate(rows):
        ref[r:r + 1, :] += v


def _window(ref, axis, j):
    return ref.at[(slice(None),) * axis + (j,)]


def _any_specs(n):
    return [pl.BlockSpec(memory_space=pl.ANY)] * n


def all_gather(blocks, axes, name):
    n_arr = len(blocks)

    def body(*refs):
        start, forward, finish = _gather_steps(refs[:n_arr], refs[n_arr:2 * n_arr], axes, *refs[2 * n_arr:])
        start()
        for j in range(3):
            forward(j)
        finish()

    return pl.pallas_call(
        body, name=name, out_shape=_gathered_shapes(blocks, axes),
        in_specs=_any_specs(n_arr), out_specs=_any_specs(n_arr), scratch_shapes=_gather_sems(n_arr),
    )(*blocks)


def _gathered_shapes(blocks, axes):
    return [jax.ShapeDtypeStruct(b.shape[:ax] + (N_DEV,) + b.shape[ax:], b.dtype) for b, ax in zip(blocks, axes)]


def _gather_sems(n_arr):
    return [pltpu.SemaphoreType.DMA((7, n_arr)), pltpu.SemaphoreType.DMA((7, n_arr)), pltpu.SemaphoreType.DMA((n_arr,))]


def _gather_steps(ins, outs, axes, send_sems, recv_sems, local_sems):
    arrays = range(len(ins))
    x, y, c = lax.axis_index("x"), lax.axis_index("y"), lax.axis_index("c")
    me, sibling = (x, y, c), (x, y, 1 - c)
    chips = [(1 - x, y), (x, 1 - y), (1 - x, 1 - y)]

    def slot(a, px, py, pc):
        return _window(outs[a], axes[a], 4 * px + 2 * py + pc)

    def copy(a, k, block, to, src=None):
        return pltpu.make_async_remote_copy(
            src_ref=slot(a, *block) if src is None else src, dst_ref=slot(a, *block),
            send_sem=send_sems.at[k, a], recv_sem=recv_sems.at[k, a], device_id=to, device_id_type=MESH_ID)

    def mine(a):
        return pltpu.make_async_copy(ins[a], slot(a, *me), local_sems.at[a])

    def first():
        return ([copy(a, 0, me, sibling, src=ins[a]) for a in arrays]
                + [copy(a, 1 + j, me, (*chip, c), src=ins[a]) for j, chip in enumerate(chips) for a in arrays])

    def passed(j):
        return [copy(a, 4 + j, (*chips[j], c), sibling) for a in arrays]

    def start():
        for a in arrays:
            mine(a).start()
        for cp in first():
            cp.start()

    def forward(j):
        for a, cp in zip(arrays, passed(j)):
            copy(a, 1 + j, (*chips[j], c), me).wait_recv()
            cp.start()

    def finish():
        for a in arrays:
            copy(a, 0, sibling, me).wait_recv()
        for j, chip in enumerate(chips):
            for a in arrays:
                copy(a, 4 + j, (*chip, 1 - c), me).wait_recv()
        for cp in first() + passed(0) + passed(1) + passed(2):
            cp.wait_send()
        for a in arrays:
            mine(a).wait()

    return start, forward, finish


def exchange_sibling(grads, name):
    n_arr = len(grads)

    def body(*refs):
        start, finish = _sibling_exchange_steps(refs[:n_arr], refs[n_arr:2 * n_arr], *refs[2 * n_arr:])
        start()
        finish()

    return pl.pallas_call(
        body, name=name, out_shape=_sibling_shapes(grads),
        in_specs=_any_specs(n_arr), out_specs=_any_specs(n_arr), scratch_shapes=_exchange_sems(n_arr),
    )(*grads)


def _sibling_shapes(grads):
    return [jax.ShapeDtypeStruct((4,) + g.shape[1:], g.dtype) for g in grads]


def _exchange_sems(n_arr):
    return [pltpu.SemaphoreType.DMA((n_arr,)), pltpu.SemaphoreType.DMA((n_arr,))]


def _sibling_exchange_steps(ins, outs, send_sems, recv_sems):
    x, y, c = lax.axis_index("x"), lax.axis_index("y"), lax.axis_index("c")

    def copy(a, src, dst):
        return pltpu.make_async_remote_copy(
            src_ref=src, dst_ref=dst, send_sem=send_sems.at[a], recv_sem=recv_sems.at[a],
            device_id=(x, y, 1 - c), device_id_type=MESH_ID)

    def start():
        for a in range(len(ins)):
            for k in range(4):
                copy(a, ins[a].at[2 * k + (1 - c)], outs[a].at[k]).start()

    def finish():
        whole = [copy(a, ins[a].at[pl.ds(0, 4)], outs[a]) for a in range(len(ins))]
        for cp in whole:
            cp.wait_recv()
        for cp in whole:
            cp.wait_send()

    return start, finish


def _chip_exchange_steps(ins, outs, send_sems, recv_sems):
    x, y, c = lax.axis_index("x"), lax.axis_index("y"), lax.axis_index("c")
    chips = [(1 - x, y), (x, 1 - y), (1 - x, 1 - y)]

    def copy(a, src, dst, chip):
        return pltpu.make_async_remote_copy(
            src_ref=src, dst_ref=dst, send_sem=send_sems.at[a], recv_sem=recv_sems.at[a],
            device_id=(*chip, c), device_id_type=MESH_ID)

    def start():
        for a in range(len(ins)):
            for j, chip in enumerate(chips):
                copy(a, ins[a].at[j], outs[a].at[j], chip).start()

    def finish():
        whole = [copy(a, ins[a], outs[a], chips[0]) for a in range(len(ins))]
        for cp in whole:
            cp.wait_recv()
        for cp in whole:
            cp.wait_send()

    return start, finish


def riding_gather(blocks, axes):
    def phases(ins, outs, *sems):
        start, forward, finish = _gather_steps(ins, outs, axes, *sems)
        return [start] + [functools.partial(forward, j) for j in range(3)] + [finish]

    return dict(operands=blocks, out_shape=_gathered_shapes(blocks, axes), sems=_gather_sems(len(blocks)),
                phases=phases, when=("first", "late0", "late1", "late2", "last"))


def riding_exchange(parts):
    def phases(ins, outs, *sems):
        return list(_chip_exchange_steps(ins, outs, *sems))

    return dict(operands=parts, out_shape=[jax.ShapeDtypeStruct(p.shape, p.dtype) for p in parts],
                sems=_exchange_sems(len(parts)), phases=phases, when=("first", "last"))


def riding_sibling(grads):
    def phases(ins, outs, *sems):
        return list(_sibling_exchange_steps(ins, outs, *sems))

    return dict(operands=grads, out_shape=_sibling_shapes(grads), sems=_exchange_sems(len(grads)),
                phases=phases, when=("first", "last"))


def _call_with_rider(body, rider, *, name, grid, in_specs, out_specs, out_shape, scratch_shapes, operands):
    params = _params(("arbitrary",) * len(grid))
    if rider is None:
        return pl.pallas_call(body, name=name, grid=grid, in_specs=in_specs, out_specs=out_specs,
                              out_shape=out_shape, scratch_shapes=scratch_shapes, compiler_params=params)(*operands)
    n_in, n_out, n_scr, k = len(in_specs), len(out_specs), len(scratch_shapes), len(rider["operands"])
    at = {"first": (0,) * len(grid), "last": tuple(g - 1 for g in grid)}
    if "late0" in rider["when"]:
        rows, cols = grid
        assert cols >= 3
        at.update({"late%d" % j: (max(rows - 2, 0), j) for j in range(3)})

    def wrapped(*refs):
        ins, c_in = refs[:n_in], refs[n_in:n_in + k]
        outs, c_out = refs[n_in + k:n_in + k + n_out], refs[n_in + k + n_out:n_in + 2 * k + n_out]
        scratch, sems = refs[n_in + 2 * k + n_out:n_in + 2 * k + n_out + n_scr], refs[n_in + 2 * k + n_out + n_scr:]
        pos = [pl.program_id(axis) for axis in range(len(grid))]

        def here(key):
            return functools.reduce(jnp.logical_and, [p == v for p, v in zip(pos, at[key])])

        phases = rider["phases"](c_in, c_out, *sems)
        for fn, key in zip(phases, rider["when"]):
            if key != "last":
                pl.when(here(key))(fn)
        body(*ins, *outs, *scratch)
        pl.when(here("last"))(phases[-1])

    return pl.pallas_call(
        wrapped, name=name, grid=grid,
        in_specs=list(in_specs) + _any_specs(k), out_specs=list(out_specs) + _any_specs(k),
        out_shape=list(out_shape) + rider["out_shape"], scratch_shapes=list(scratch_shapes) + rider["sems"],
        compiler_params=params)(*operands, *rider["operands"])


def add_sibling(g8, got, src_idx, chip_idx, name):
    _, r, n = g8.shape
    tr = _tile(r, SUM_ROWS, 16)

    def body(si_ref, ci_ref, g0_ref, g1_ref, g2_ref, g3_ref, got_ref, own_ref, send_ref):
        own_ref[...] = g0_ref[0] + got_ref[ci_ref[0]]
        for j, g_ref in enumerate((g1_ref, g2_ref, g3_ref)):
            send_ref[j] = (g_ref[0] + got_ref[ci_ref[j + 1]]).astype(BF16)

    def mine(j):
        return pl.BlockSpec((1, tr, n), lambda i, si, ci: (si[j], i, 0))

    return pl.pallas_call(
        body, name=name,
        out_shape=[jax.ShapeDtypeStruct((r, n), F32), jax.ShapeDtypeStruct((3, r, n), BF16)],
        grid_spec=pltpu.PrefetchScalarGridSpec(
            num_scalar_prefetch=2, grid=(r // tr,),
            in_specs=[mine(0), mine(1), mine(2), mine(3), pl.BlockSpec((4, tr, n), lambda i, si, ci: (0, i, 0))],
            out_specs=[pl.BlockSpec((tr, n), lambda i, si, ci: (i, 0)),
                       pl.BlockSpec((3, tr, n), lambda i, si, ci: (0, i, 0))]),
        compiler_params=_params(("arbitrary",)),
    )(src_idx, chip_idx, g8, g8, g8, g8, got)


def add_received(own, got, name):
    r, n = own.shape
    tr = _tile(r, SUM_ROWS, 16)

    def body(a_ref, b_ref, o_ref):
        acc = a_ref[...]
        for j in range(3):
            acc = acc + b_ref[j].astype(F32)
        o_ref[...] = acc

    return pl.pallas_call(
        body, name=name,
        out_shape=jax.ShapeDtypeStruct((r, n), F32),
        grid=(r // tr,),
        in_specs=[pl.BlockSpec((tr, n), lambda i: (i, 0)), pl.BlockSpec((3, tr, n), lambda i: (0, i, 0))],
        out_specs=pl.BlockSpec((tr, n), lambda i: (i, 0)),
        compiler_params=_params(("arbitrary",)),
    )(own, got)


def sum_devices(g):
    def body(g_ref, o_ref):
        acc = g_ref[0]
        for j in range(1, N_DEV):
            acc = acc + g_ref[j]
        o_ref[...] = acc

    return pl.pallas_call(body, name="sum_devices", out_shape=jax.ShapeDtypeStruct(g.shape[1:], F32))(g)


def sum_lanes(v):
    def body(v_ref, o_ref):
        o_ref[...] = jnp.broadcast_to(jnp.sum(v_ref[...], axis=-1, keepdims=True), (1, LANES))

    return pl.pallas_call(body, name="sum_lanes", out_shape=jax.ShapeDtypeStruct((1, LANES), F32))(v)


def ada_forward(c_all, ada_w, ada_b_cols):
    nb, n = c_all.shape[0], ada_w.shape[1]

    def body(c_ref, w_ref, b_ref, o_ref):
        cv = c_ref[...]
        s = (cv * jax.nn.sigmoid(cv)).astype(BF16)
        o_ref[...] = _dot(s, w_ref[...].astype(BF16)) + b_ref[...]

    return pl.pallas_call(body, name="ada_fwd", out_shape=jax.ShapeDtypeStruct((nb, n), F32),
                          compiler_params=_params())(c_all, ada_w, ada_b_cols)


def ada_backward(c_all16, dmod16):
    d, n = c_all16.shape[1], dmod16.shape[1]

    def body(c_ref, g_ref, o_ref):
        cv = c_ref[...]
        s = (cv * jax.nn.sigmoid(cv)).astype(BF16)
        o_ref[...] = _dot(s, g_ref[...].astype(BF16), TN)

    return pl.pallas_call(body, name="ada_bwd", out_shape=jax.ShapeDtypeStruct((d, n), F32),
                          compiler_params=_params())(c_all16, dmod16)


def ffn_forward(x, gn, sc, sh, gate, ws, first, name, rider=None):
    t, d = x.shape
    f = ws.shape[1]
    tm, tf = _tile(t, FFN_FWD_TILE[0], 16), _tile(f, FFN_FWD_TILE[1])
    nf = f // tf

    def body(x_ref, gn_ref, sc_ref, sh_ref, gate_ref, w1_ref, w3_ref, w2_ref,
             xo_ref, h_ref, a_ref, b_ref, y_ref, hs, acc):
        j = pl.program_id(1)

        @pl.when(j == 0)
        def _():
            xhat, _ = _rms(x_ref[...])
            h = (xhat * gn_ref[...] * (1.0 + sc_ref[...]) + sh_ref[...]).astype(BF16)
            hs[...] = h
            h_ref[...] = h
            acc[...] = jnp.zeros_like(acc)

        h = hs[...]
        a = _dot(h, w1_ref[...], NT)
        b = _dot(h, w3_ref[...], NT)
        a_ref[...] = a.astype(BF16)
        b_ref[...] = b.astype(BF16)
        u = (a * _sigmoid(a) * b).astype(BF16)
        acc[...] += _dot(u, w2_ref[...])

        @pl.when(j == nf - 1)
        def _():
            y = acc[...]
            y_ref[...] = y.astype(BF16)
            xo_ref[...] = x_ref[...] + 0.5 * gate_ref[...] * y

    row = pl.BlockSpec((tm, d), lambda i, j: (i, 0))
    vec = pl.BlockSpec((1, d), lambda i, j: (0, 0))
    wide = pl.BlockSpec((tm, tf), lambda i, j: (i, j))
    return _call_with_rider(
        body, rider, name=name, grid=(t // tm, nf),
        in_specs=[row, vec, vec, vec, vec] + _ffn_weight_specs(first, tf, d),
        out_specs=[row, row, wide, wide, row],
        out_shape=[jax.ShapeDtypeStruct((t, d), F32), jax.ShapeDtypeStruct((t, d), BF16),
                   jax.ShapeDtypeStruct((t, f), BF16), jax.ShapeDtypeStruct((t, f), BF16),
                   jax.ShapeDtypeStruct((t, d), BF16)],
        scratch_shapes=[pltpu.VMEM((tm, d), BF16), pltpu.VMEM((tm, d), F32)],
        operands=(x, gn, sc, sh, gate, ws, ws, ws))


def _ffn_weight_specs(first, tf, d):
    return [pl.BlockSpec((None, tf, d), lambda i, j, w=first + k: (w, j, 0)) for k in range(3)]


def ffn_backward_gate(dy, a, b, ws, first, name, rider=None):
    t, d = dy.shape
    f = ws.shape[1]
    tm, tf = _tile(t, FFN_BWD_TILE[0], 16), _tile(f, FFN_BWD_TILE[1])
    nf = f // tf

    def gate_body(dy_ref, a_ref, b_ref, w2_ref, da_ref, db_ref, u_ref):
        du = _dot(dy_ref[...], w2_ref[...], NT)
        av = a_ref[...].astype(F32)
        bv = b_ref[...].astype(F32)
        s = _sigmoid(av)
        sa = av * s
        da_ref[...] = (du * bv * (s + sa * (1.0 - s))).astype(BF16)
        db_ref[...] = (du * sa).astype(BF16)
        u_ref[...] = (sa * bv).astype(BF16)

    hidden = jax.ShapeDtypeStruct((t, f), BF16)
    wide_t = pl.BlockSpec((tm, tf), lambda j, i: (i, j))
    return _call_with_rider(
        gate_body, rider, name=name, grid=(nf, t // tm),
        in_specs=[pl.BlockSpec((tm, d), lambda j, i: (i, 0)), wide_t, wide_t,
                  pl.BlockSpec((None, tf, d), lambda j, i: (first + 2, j, 0))],
        out_specs=[wide_t, wide_t, wide_t], out_shape=[hidden, hidden, hidden],
        scratch_shapes=[], operands=(dy, a, b, ws))


def ffn_backward_norm(da, db, dxo, x, y, gn, sc, ws, first, name, rider=None):
    t, d = x.shape
    f = ws.shape[1]
    tm, tf = _tile(t, FFN_BWD_TILE[0], 16), _tile(f, FFN_BWD_TILE[1])
    nf = f // tf
    row = pl.BlockSpec((tm, d), lambda i, j: (i, 0))
    vec = pl.BlockSpec((1, d), lambda i, j: (0, 0))
    wide = pl.BlockSpec((tm, tf), lambda i, j: (i, j))

    def norm_body(da_ref, db_ref, w1_ref, w3_ref, dxo_ref, x_ref, y_ref, gn_ref, sc_ref, dx_ref, sums_ref, acc):
        i, j = pl.program_id(0), pl.program_id(1)

        @pl.when(jnp.logical_and(i == 0, j == 0))
        def _():
            sums_ref[...] = jnp.zeros_like(sums_ref)

        part = _dot(da_ref[...], w1_ref[...]) + _dot(db_ref[...], w3_ref[...])

        @pl.when(j == 0)
        def _():
            acc[...] = part

        @pl.when(jnp.logical_and(j > 0, j < nf - 1))
        def _():
            acc[...] += part

        @pl.when(j == nf - 1)
        def _():
            dh = part if nf == 1 else acc[...] + part
            dxo_v = dxo_ref[...]
            dx, d_sh, d_sc, d_gn = _norm_mod_bwd(dh, x_ref[...], gn_ref[...], sc_ref[...])
            dx_ref[...] = dxo_v + dx
            d_gate = jnp.sum(dxo_v * (0.5 * y_ref[...].astype(F32)), axis=0, keepdims=True)
            _add_rows(sums_ref, [d_sh, d_sc, d_gate, d_gn])

    w1_spec, w3_spec, _ = _ffn_weight_specs(first, tf, d)
    return _call_with_rider(
        norm_body, rider, name=name, grid=(t // tm, nf),
        in_specs=[wide, wide, w1_spec, w3_spec, row, row, row, vec, vec],
        out_specs=[row, pl.BlockSpec((8, d), lambda i, j: (0, 0))],
        out_shape=[jax.ShapeDtypeStruct((t, d), F32), jax.ShapeDtypeStruct((8, d), F32)],
        scratch_shapes=[pltpu.VMEM((tm, d), F32)],
        operands=(da, db, ws, ws, dxo, x, y, gn, sc))


def matmul_tn(a, b, name, rider=None):
    parts = list(a) if isinstance(a, (list, tuple)) else [a]
    t, n = b.shape
    widths = [p.shape[1] for p in parts]
    tm = _tile(functools.reduce(math.gcd, widths), GRAD_TILE)
    tn, tk = _tile(n, GRAD_TILE), _tile(t, GRAD_DEPTH, 16)
    nk = t // tk
    counts = [w // tm for w in widths]
    firsts = [sum(counts[:p]) for p in range(len(parts))]

    def body(*refs):
        a_refs, (b_ref, o_ref, acc) = refs[:len(parts)], refs[len(parts):]
        i, k = pl.program_id(0), pl.program_id(2)

        @pl.when(k == 0)
        def _():
            acc[...] = jnp.zeros_like(acc)

        for a_ref, lo, cnt in zip(a_refs, firsts, counts):
            def accumulate(a_ref=a_ref):
                acc[...] += _dot(a_ref[...], b_ref[...], TN)

            if len(parts) == 1:
                accumulate()
            else:
                pl.when(jnp.logical_and(i >= lo, i < lo + cnt))(accumulate)

        @pl.when(k == nk - 1)
        def _():
            o_ref[...] = acc[...]

    def part_spec(lo, cnt):
        if len(parts) == 1:
            return pl.BlockSpec((tk, tm), lambda i, j, k: (k, i))

        def index(i, j, k):
            mine = jnp.logical_and(i >= lo, i < lo + cnt)
            return jnp.where(mine, k, 0), jnp.clip(i - lo, 0, cnt - 1)
        return pl.BlockSpec((tk, tm), index)

    out = _call_with_rider(
        body, rider, name=name, grid=(sum(counts), n // tn, nk),
        in_specs=[part_spec(lo, cnt) for lo, cnt in zip(firsts, counts)]
        + [pl.BlockSpec((tk, tn), lambda i, j, k: (k, j))],
        out_specs=[pl.BlockSpec((tm, tn), lambda i, j, k: (i, j))],
        out_shape=[jax.ShapeDtypeStruct((sum(widths), n), F32)],
        scratch_shapes=[pltpu.VMEM((tm, tn), F32)], operands=(*parts, b))
    return out[0] if rider is None else out


def mix_in_forward(x, gn, sc, sh, w_in):
    t, d = x.shape
    tm = _tile(t, ROW_TILE, 16)

    def body(x_ref, gn_ref, sc_ref, sh_ref, w_ref, h_ref, zc_ref, zm_ref):
        xhat, _ = _rms(x_ref[...])
        h = (xhat * gn_ref[...] * (1.0 + sc_ref[...]) + sh_ref[...]).astype(BF16)
        h_ref[...] = h
        z = _dot(h, w_ref[...], NT)
        zc_ref[...] = z[:, :ZC_COLS].astype(BF16)
        zm_ref[...] = z[:, ZC_COLS:].astype(BF16)

    row = pl.BlockSpec((tm, d), lambda i: (i, 0))
    vec = pl.BlockSpec((1, d), lambda i: (0, 0))
    return pl.pallas_call(
        body, name="mix_in_fwd", grid=(t // tm,),
        in_specs=[row, vec, vec, vec, _row(w_in)],
        out_specs=[row, pl.BlockSpec((tm, ZC_COLS), lambda i: (i, 0)), pl.BlockSpec((tm, ZM_COLS), lambda i: (i, 0))],
        out_shape=[jax.ShapeDtypeStruct((t, d), BF16), jax.ShapeDtypeStruct((t, ZC_COLS), BF16),
                   jax.ShapeDtypeStruct((t, ZM_COLS), BF16)],
        compiler_params=_params(("arbitrary",)),
    )(x, gn, sc, sh, w_in)


def _rope_tables(pos, inv_freq):
    ang = pos * inv_freq
    lane = lax.broadcasted_iota(jnp.int32, ang.shape, 1)
    cos, sin = jnp.cos(ang), jnp.sin(ang)
    half = QK_ROPE // 2
    return cos, jnp.where(lane < half, -sin, 0.0), jnp.where(jnp.logical_and(lane >= half, lane < QK_ROPE), sin, 0.0)


def _rope(v, tables):
    cos, sin_a, sin_b = tables
    return v * cos + pltpu.roll(v, LANES - QK_ROPE // 2, 1) * sin_a + pltpu.roll(v, QK_ROPE // 2, 1) * sin_b


def _rope_transposed(dv, tables):
    cos, sin_a, sin_b = tables
    return dv * cos + pltpu.roll(dv * sin_a, QK_ROPE // 2, 1) + pltpu.roll(dv * sin_b, LANES - QK_ROPE // 2, 1)


def mla_project(zm, pos, inv_freq, qg, kvg, w_uq, w_ukv):
    t = zm.shape[0]
    tm = _tile(t, ROW_TILE, 16)

    def body(zm_ref, pos_ref, if_ref, qg_ref, kvg_ref, wq_ref, wkv_ref, qn_ref, kvn_ref, q_ref, k_ref, v_ref):
        zv = zm_ref[...].astype(F32)
        qn = (_rms(zv[:, :Q_LORA])[0] * qg_ref[...]).astype(BF16)
        kvn = (_rms(zv[:, Q_LORA:Q_LORA + KV_LORA])[0] * kvg_ref[...]).astype(BF16)
        qn_ref[...] = qn
        kvn_ref[...] = kvn
        qf = _dot(qn, wq_ref[...], NT) * QK_FOLD
        kvf = _dot(kvn, wkv_ref[...], NT)
        tables = _rope_tables(pos_ref[...], if_ref[...])
        kr = _rope(zv[:, Q_LORA + KV_LORA:], tables).astype(BF16)
        for h in range(MLA_HEADS):
            lo = h * HEAD_PAD
            q_ref[:, lo:lo + QK_NOPE] = qf[:, lo:lo + QK_NOPE].astype(BF16)
            q_ref[:, lo + QK_NOPE:lo + HEAD_PAD] = _rope(qf[:, lo + QK_NOPE:lo + HEAD_PAD], tables).astype(BF16)
            k_ref[:, lo:lo + QK_NOPE] = kvf[:, h * QK_NOPE:(h + 1) * QK_NOPE].astype(BF16)
            k_ref[:, lo + QK_NOPE:lo + HEAD_PAD] = kr
        v_ref[...] = kvf[:, MLA_HEADS * QK_NOPE:].astype(BF16)

    def rows(n):
        return pl.BlockSpec((tm, n), lambda i: (i, 0))

    return pl.pallas_call(
        body, name="mla_project", grid=(t // tm,),
        in_specs=[rows(ZM_COLS), rows(1), _row(inv_freq), _row(qg), _row(kvg), _row(w_uq), _row(w_ukv)],
        out_specs=[rows(Q_LORA), rows(KV_LORA), rows(QK_COLS), rows(QK_COLS), rows(MLA_WIDTH)],
        out_shape=[jax.ShapeDtypeStruct((t, Q_LORA), BF16), jax.ShapeDtypeStruct((t, KV_LORA), BF16),
                   jax.ShapeDtypeStruct((t, QK_COLS), BF16), jax.ShapeDtypeStruct((t, QK_COLS), BF16),
                   jax.ShapeDtypeStruct((t, MLA_WIDTH), BF16)],
        compiler_params=_params(("arbitrary",)),
    )(zm, pos, inv_freq, qg, kvg, w_uq, w_ukv)


def _chunk_mask(shape, q_axis):
    qi = lax.broadcasted_iota(jnp.int32, shape, q_axis) // CHUNK
    ki = lax.broadcasted_iota(jnp.int32, shape, 1 - q_axis) // CHUNK
    return ki <= qi


def attention_forward(q, k, v):
    t = q.shape[0]
    tq = _tile(t, ATTN_TILE, CHUNK)

    def body(q_ref, k_ref, v_ref, o_ref, lse_ref):
        i = pl.program_id(1)
        qv = q_ref[...]

        def step(kb, carry, masked, tiles=1):
            m, l, acc = carry
            keys = pl.ds(pl.multiple_of(kb * tq, tq), tiles * tq)
            s = _dot(qv, k_ref[keys, :], NT)
            if masked:
                s = jnp.where(_chunk_mask(s.shape, 0), s, NEG_INF)
            m_new = jnp.maximum(m, jnp.max(s, axis=-1, keepdims=True))
            alpha = jnp.exp2(m - m_new)
            p = jnp.exp2(s - m_new)
            l = alpha * l + jnp.sum(p, axis=-1, keepdims=True)
            acc = alpha * acc + _dot(p.astype(BF16), v_ref[keys, :])
            return m_new, l, acc

        init = (jnp.full((tq, 1), NEG_INF, F32), jnp.zeros((tq, 1), F32), jnp.zeros((tq, V_HEAD), F32))
        carry = lax.fori_loop(0, i // 2, lambda pb, cr: step(2 * pb, cr, False, 2), init)
        carry = lax.fori_loop(0, i % 2, lambda _, cr: step(i - 1, cr, False), carry)
        m, l, acc = step(i, carry, True)
        o_ref[...] = (acc / l).astype(BF16)
        lse_ref[0] = m + jnp.log2(l)

    return pl.pallas_call(
        body, name="attn_fwd", grid=(MLA_HEADS, t // tq),
        in_specs=[pl.BlockSpec((tq, HEAD_PAD), lambda h, i: (i, h)),
                  pl.BlockSpec((t, HEAD_PAD), lambda h, i: (0, h)),
                  pl.BlockSpec((t, V_HEAD), lambda h, i: (0, h))],
        out_specs=[pl.BlockSpec((tq, V_HEAD), lambda h, i: (i, h)),
                   pl.BlockSpec((1, tq, 1), lambda h, i: (h, i, 0))],
        out_shape=[jax.ShapeDtypeStruct((t, MLA_WIDTH), BF16), jax.ShapeDtypeStruct((MLA_HEADS, t, 1), F32)],
        compiler_params=_params(("arbitrary", "arbitrary")),
    )(q, k, v)


def attention_backward(q, k, v, do, lse, delta, rider=None):
    t = q.shape[0]
    tq = _tile(t, ATTN_TILE, CHUNK)
    nq = t // tq

    def body(q_ref, k_ref, v_ref, do_ref, lse_ref, delta_ref, dq_ref, dk_ref, dv_ref, dq_acc):
        kb = pl.program_id(1)

        @pl.when(kb == 0)
        def _():
            dq_acc[...] = jnp.zeros_like(dq_acc)

        kv, vv = k_ref[...], v_ref[...]

        def step(qb, carry, masked):
            dk, dv = carry
            rows = pl.ds(pl.multiple_of(qb * tq, tq), tq)
            qv, dov = q_ref[rows, :], do_ref[rows, :]
            s = _dot(kv, qv, NT)
            if masked:
                s = jnp.where(_chunk_mask(s.shape, 1), s, NEG_INF)
            p = jnp.exp2(s - lse_ref[0, qb])
            dv = dv + _dot(p.astype(BF16), dov)
            dp = _dot(vv, dov, NT)
            ds = (p * (dp - delta_ref[0, qb]) * LN_2).astype(BF16)
            dk = dk + _dot(ds, qv)
            dq_acc[rows, :] += _dot(ds, kv, TN)
            return dk, dv

        carry = step(kb, (jnp.zeros((tq, HEAD_PAD), F32), jnp.zeros((tq, V_HEAD), F32)), True)
        odd = (nq - 1 - kb) % 2
        carry = lax.fori_loop(0, odd, lambda _, cr: step(kb + 1, cr, False), carry)
        first = kb + 1 + odd
        dk, dv = lax.fori_loop(0, (nq - first) // 2,
                               lambda pb, cr: step(first + 2 * pb + 1, step(first + 2 * pb, cr, False), False), carry)
        dk_ref[...] = dk.astype(BF16)
        dv_ref[...] = dv.astype(BF16)

        @pl.when(kb == nq - 1)
        def _():
            dq_ref[...] = dq_acc[...].astype(BF16)

    stat = pl.BlockSpec((1, nq, 1, tq), lambda h, j: (h, 0, 0, 0))
    return _call_with_rider(
        body, rider, name="attn_bwd", grid=(MLA_HEADS, nq),
        in_specs=[pl.BlockSpec((t, HEAD_PAD), lambda h, j: (0, h)),
                  pl.BlockSpec((tq, HEAD_PAD), lambda h, j: (j, h)),
                  pl.BlockSpec((tq, V_HEAD), lambda h, j: (j, h)),
                  pl.BlockSpec((t, V_HEAD), lambda h, j: (0, h)), stat, stat],
        out_specs=[pl.BlockSpec((t, HEAD_PAD), lambda h, j: (0, h)),
                   pl.BlockSpec((tq, HEAD_PAD), lambda h, j: (j, h)),
                   pl.BlockSpec((tq, V_HEAD), lambda h, j: (j, h))],
        out_shape=[jax.ShapeDtypeStruct((t, QK_COLS), BF16), jax.ShapeDtypeStruct((t, QK_COLS), BF16),
                   jax.ShapeDtypeStruct((t, MLA_WIDTH), BF16)],
        scratch_shapes=[pltpu.VMEM((t, HEAD_PAD), F32)], operands=(q, k, v, do, lse, delta))


HALO = 16


def _halo_spec(tm, n, step, last):
    return pl.BlockSpec((HALO, n), lambda i: (jnp.clip(i * (tm // HALO) + step, 0, last), 0))


def _shift_rows(v, prev, n):
    out = pltpu.roll(v, n, 0)
    row = lax.broadcasted_iota(jnp.int32, v.shape, 0)
    for r in range(n):
        out = jnp.where(row == r, prev[HALO - n + r:HALO - n + r + 1, :], out)
    return out


def _advance_rows(v, nxt, n):
    rows = v.shape[0]
    out = pltpu.roll(v, rows - n, 0)
    row = lax.broadcasted_iota(jnp.int32, v.shape, 0)
    for r in range(n):
        out = jnp.where(row == rows - n + r, nxt[r:r + 1, :], out)
    return out


def _conv_taps(zc, zc_prev, first):
    w = CONV_WIDTH
    u = zc[:, w:2 * w] * zc[:, 2 * w:]
    up = jnp.where(first, 0.0, zc_prev[:, w:2 * w] * zc_prev[:, 2 * w:])
    return u, _shift_rows(u, up, 1), _shift_rows(u, up, 2)


def mix_out_forward(zc, o, conv_w, og, gmat_a, gmat_b, w_out, x, gate):
    t, d = x.shape
    tm = _tile(t, ROW_TILE, 16)
    w = CONV_WIDTH

    def body(zc_ref, zp_ref, o_ref, cw_ref, og_ref, ga_ref, gb_ref, w_ref, x_ref, gate_ref,
             xo_ref, yn_ref, y_ref, ya_ref):
        zc_v = zc_ref[...].astype(F32)
        u, u1, u2 = _conv_taps(zc_v, zp_ref[...].astype(F32), pl.program_id(0) == 0)
        cw = cw_ref[...]
        ya = zc_v[:, :w] * (cw[0:1] * u2 + cw[1:2] * u1 + cw[2:3] * u)
        ya_ref[...] = ya.astype(BF16)
        ov = o_ref[...].astype(F32)
        ogv = og_ref[...]
        yn_ref[:, :w] = (ya * lax.rsqrt(_group_mean(ya * ya, ga_ref[...]) + EPS) * ogv[:, :w]).astype(BF16)
        yn_ref[:, w:] = (ov * lax.rsqrt(_group_mean(ov * ov, gb_ref[...]) + EPS) * ogv[:, w:]).astype(BF16)
        y = _dot(yn_ref[...], w_ref[...])
        y_ref[...] = y.astype(BF16)
        xo_ref[...] = x_ref[...] + gate_ref[...] * y

    def rows(n):
        return pl.BlockSpec((tm, n), lambda i: (i, 0))

    return pl.pallas_call(
        body, name="mix_out_fwd", grid=(t // tm,),
        in_specs=[rows(ZC_COLS), _halo_spec(tm, ZC_COLS, -1, t // HALO - 1), rows(MLA_WIDTH), _row(conv_w), _row(og),
                  _row(gmat_a), _row(gmat_b), _row(w_out), rows(d), _row(gate)],
        out_specs=[rows(d), rows(MIX_WIDTH), rows(d), rows(w)],
        out_shape=[jax.ShapeDtypeStruct((t, d), F32), jax.ShapeDtypeStruct((t, MIX_WIDTH), BF16),
                   jax.ShapeDtypeStruct((t, d), BF16), jax.ShapeDtypeStruct((t, w), BF16)],
        compiler_params=_params(("arbitrary",)),
    )(zc, zc, o, conv_w, og, gmat_a, gmat_b, w_out, x, gate)


def _group_norm_bwd(dyn, y, og, gmat):
    rs = lax.rsqrt(_group_mean(y * y, gmat) + EPS)
    yhat = y * rs
    d_og = jnp.sum(dyn * yhat, axis=0, keepdims=True)
    dyh = dyn * og
    return rs * (dyh - yhat * _group_mean(dyh * yhat, gmat)), d_og


def mix_out_backward(dxo, y, gate, ya, o, og, gmat_a, gmat_b, w_out, rider=None):
    t, d = dxo.shape
    tm = _tile(t, ROW_TILE, 16)
    w = CONV_WIDTH

    def body(dxo_ref, y_ref, gate_ref, ya_ref, o_ref, og_ref, ga_ref, gb_ref, w_ref,
             dy_ref, dya_ref, do_ref, delta_ref, sd_ref, so_ref):
        @pl.when(pl.program_id(0) == 0)
        def _():
            sd_ref[...] = jnp.zeros_like(sd_ref)
            so_ref[...] = jnp.zeros_like(so_ref)

        dxo_v = dxo_ref[...]
        dy = (gate_ref[...] * dxo_v).astype(BF16)
        dy_ref[...] = dy
        sd_ref[0:1, :] += jnp.sum(dxo_v * y_ref[...].astype(F32), axis=0, keepdims=True)
        dyn = _dot(dy, w_ref[...], NT)
        ogv = og_ref[...]
        ov = o_ref[...].astype(F32)
        dya, d_og_a = _group_norm_bwd(dyn[:, :w], ya_ref[...].astype(F32), ogv[:, :w], ga_ref[...])
        dov, d_og_b = _group_norm_bwd(dyn[:, w:], ov, ogv[:, w:], gb_ref[...])
        dya_ref[...] = dya.astype(BF16)
        do_ref[...] = dov.astype(BF16)
        so_ref[0:1, :w] += d_og_a
        so_ref[0:1, w:] += d_og_b
        prod = dov * ov
        for h in range(MLA_HEADS):
            delta_ref[h] = jnp.sum(prod[:, h * V_HEAD:(h + 1) * V_HEAD], axis=-1, keepdims=True)

    def rows(n):
        return pl.BlockSpec((tm, n), lambda i: (i, 0))

    return _call_with_rider(
        body, rider, name="mix_out_bwd", grid=(t // tm,),
        in_specs=[rows(d), rows(d), _row(gate), rows(w), rows(MLA_WIDTH), _row(og), _row(gmat_a), _row(gmat_b),
                  _row(w_out)],
        out_specs=[rows(d), rows(w), rows(MLA_WIDTH), pl.BlockSpec((MLA_HEADS, tm, 1), lambda i: (0, i, 0)),
                   pl.BlockSpec((8, d), lambda i: (0, 0)), pl.BlockSpec((8, MIX_WIDTH), lambda i: (0, 0))],
        out_shape=[jax.ShapeDtypeStruct((t, d), BF16), jax.ShapeDtypeStruct((t, w), BF16),
                   jax.ShapeDtypeStruct((t, MLA_WIDTH), BF16), jax.ShapeDtypeStruct((MLA_HEADS, t, 1), F32),
                   jax.ShapeDtypeStruct((8, d), F32), jax.ShapeDtypeStruct((8, MIX_WIDTH), F32)],
        scratch_shapes=[], operands=(dxo, y, gate, ya, o, og, gmat_a, gmat_b, w_out))


def conv_backward(zc, dya, conv_w):
    t = zc.shape[0]
    tm = _tile(t, ROW_TILE, 16)
    nt = t // tm
    w = CONV_WIDTH

    def body(zc_ref, zp_ref, zn_ref, dya_ref, dn_ref, cw_ref, dzc_ref, sums_ref):
        i = pl.program_id(0)

        @pl.when(i == 0)
        def _():
            sums_ref[...] = jnp.zeros_like(sums_ref)

        zc_v = zc_ref[...].astype(F32)
        u, u1, u2 = _conv_taps(zc_v, zp_ref[...].astype(F32), i == 0)
        cw = cw_ref[...]
        dya_v = dya_ref[...].astype(F32)
        dyc = dya_v * zc_v[:, :w]
        dyc_next = jnp.where(i == nt - 1, 0.0, dn_ref[...].astype(F32) * zn_ref[:, :w].astype(F32))
        du = cw[2:3] * dyc + cw[1:2] * _advance_rows(dyc, dyc_next, 1) + cw[0:1] * _advance_rows(dyc, dyc_next, 2)
        dzc_ref[:, :w] = (dya_v * (cw[0:1] * u2 + cw[1:2] * u1 + cw[2:3] * u)).astype(BF16)
        dzc_ref[:, w:2 * w] = (du * zc_v[:, 2 * w:]).astype(BF16)
        dzc_ref[:, 2 * w:] = (du * zc_v[:, w:2 * w]).astype(BF16)
        _add_rows(sums_ref, [jnp.sum(dyc * tap, axis=0, keepdims=True) for tap in (u2, u1, u)])

    def rows(n):
        return pl.BlockSpec((tm, n), lambda i: (i, 0))

    def halo(n, step):
        return _halo_spec(tm, n, step, t // HALO - 1)

    return pl.pallas_call(
        body, name="conv_bwd", grid=(nt,),
        in_specs=[rows(ZC_COLS), halo(ZC_COLS, -1), halo(ZC_COLS, tm // HALO), rows(w), halo(w, tm // HALO),
                  _row(conv_w)],
        out_specs=[rows(ZC_COLS), pl.BlockSpec((8, w), lambda i: (0, 0))],
        out_shape=[jax.ShapeDtypeStruct((t, ZC_COLS), BF16), jax.ShapeDtypeStruct((8, w), F32)],
        compiler_params=_params(("arbitrary",)),
    )(zc, zc, zc, dya, dya, conv_w)


def _rms_bwd(dy, x, g):
    xhat, r = _rms(x)
    d_g = jnp.sum(dy * xhat, axis=0, keepdims=True)
    dxh = dy * g
    return r * (dxh - xhat * jnp.mean(dxh * xhat, axis=-1, keepdims=True)), d_g


def mla_project_backward(dq, dk, dv, zm, pos, inv_freq, qg, kvg, w_uq, w_ukv):
    t = zm.shape[0]
    tm = _tile(t, ROW_TILE, 16)

    def body(dq_ref, dk_ref, dv_ref, zm_ref, pos_ref, if_ref, qg_ref, kvg_ref, wq_ref, wkv_ref,
             dql_ref, dkvl_ref, dzm_ref, sums_ref):
        @pl.when(pl.program_id(0) == 0)
        def _():
            sums_ref[...] = jnp.zeros_like(sums_ref)

        tables = _rope_tables(pos_ref[...], if_ref[...])
        dkr = jnp.zeros((tm, LANES), F32)
        for h in range(MLA_HEADS):
            lo = h * HEAD_PAD
            dql_ref[:, lo:lo + QK_NOPE] = (dq_ref[:, lo:lo + QK_NOPE].astype(F32) * QK_FOLD).astype(BF16)
            dql_ref[:, lo + QK_NOPE:lo + HEAD_PAD] = _rope_transposed(
                dq_ref[:, lo + QK_NOPE:lo + HEAD_PAD].astype(F32) * QK_FOLD, tables).astype(BF16)
            dkvl_ref[:, h * QK_NOPE:(h + 1) * QK_NOPE] = dk_ref[:, lo:lo + QK_NOPE]
            dkr = dkr + dk_ref[:, lo + QK_NOPE:lo + HEAD_PAD].astype(F32)
        dkvl_ref[:, MLA_HEADS * QK_NOPE:] = dv_ref[...]
        zv = zm_ref[...].astype(F32)
        dqn = _dot(dql_ref[...], wq_ref[...])
        dkvn = _dot(dkvl_ref[...], wkv_ref[...])
        dcq, d_qg = _rms_bwd(dqn, zv[:, :Q_LORA], qg_ref[...])
        dckv, d_kvg = _rms_bwd(dkvn, zv[:, Q_LORA:Q_LORA + KV_LORA], kvg_ref[...])
        dzm_ref[:, :Q_LORA] = dcq.astype(BF16)
        dzm_ref[:, Q_LORA:Q_LORA + KV_LORA] = dckv.astype(BF16)
        dzm_ref[:, Q_LORA + KV_LORA:] = _rope_transposed(dkr, tables).astype(BF16)
        sums_ref[0:1, :Q_LORA] += d_qg
        sums_ref[0:1, Q_LORA:Q_LORA + KV_LORA] += d_kvg

    def rows(n):
        return pl.BlockSpec((tm, n), lambda i: (i, 0))

    return pl.pallas_call(
        body, name="mla_project_bwd", grid=(t // tm,),
        in_specs=[rows(QK_COLS), rows(QK_COLS), rows(MLA_WIDTH), rows(ZM_COLS), rows(1), _row(inv_freq),
                  _row(qg), _row(kvg), _row(w_uq), _row(w_ukv)],
        out_specs=[rows(QK_COLS), rows(QK_COLS), rows(ZM_COLS), pl.BlockSpec((8, ZM_COLS), lambda i: (0, 0))],
        out_shape=[jax.ShapeDtypeStruct((t, QK_COLS), BF16), jax.ShapeDtypeStruct((t, QK_COLS), BF16),
                   jax.ShapeDtypeStruct((t, ZM_COLS), BF16), jax.ShapeDtypeStruct((8, ZM_COLS), F32)],
        compiler_params=_params(("arbitrary",)),
    )(dq, dk, dv, zm, pos, inv_freq, qg, kvg, w_uq, w_ukv)


def mix_in_backward(dzc, dzm, w_in, x, dxo, gn, sc, gate, rider=None):
    t, d = x.shape
    tm = _tile(t, ROW_TILE, 16)

    def body(dzc_ref, dzm_ref, w_ref, x_ref, dxo_ref, gn_ref, sc_ref, gate_ref, dx_ref, dy_ref, sums_ref):
        @pl.when(pl.program_id(0) == 0)
        def _():
            sums_ref[...] = jnp.zeros_like(sums_ref)

        dh = _dot(dzc_ref[...], w_ref[:ZC_COLS, :]) + _dot(dzm_ref[...], w_ref[ZC_COLS:, :])
        dx, d_sh, d_sc, d_gn = _norm_mod_bwd(dh, x_ref[...], gn_ref[...], sc_ref[...])
        dx = dxo_ref[...] + dx
        dx_ref[...] = dx
        dy_ref[...] = (0.5 * gate_ref[...] * dx).astype(BF16)
        _add_rows(sums_ref, [d_sh, d_sc, d_gn])

    def rows(n):
        return pl.BlockSpec((tm, n), lambda i: (i, 0))

    return _call_with_rider(
        body, rider, name="mix_in_bwd", grid=(t // tm,),
        in_specs=[rows(ZC_COLS), rows(ZM_COLS), _row(w_in), rows(d), rows(d), _row(gn), _row(sc), _row(gate)],
        out_specs=[rows(d), rows(d), pl.BlockSpec((8, d), lambda i: (0, 0))],
        out_shape=[jax.ShapeDtypeStruct((t, d), F32), jax.ShapeDtypeStruct((t, d), BF16),
                   jax.ShapeDtypeStruct((8, d), F32)],
        scratch_shapes=[], operands=(dzc, dzm, w_in, x, dxo, gn, sc, gate))


def final_loss(x, target, g, gate):
    t, d = x.shape
    tm = _tile(t, ROW_TILE, 16)

    def body(x_ref, t_ref, g_ref, gate_ref, dx_ref, dy_ref, sums_ref):
        @pl.when(pl.program_id(0) == 0)
        def _():
            sums_ref[...] = jnp.zeros_like(sums_ref)

        gv = g_ref[...]
        xhat, r = _rms(x_ref[...])
        err = xhat * gv - t_ref[...]
        dyf = err * (1.0 / d)
        dxh = dyf * gv
        dx = r * (dxh - xhat * jnp.mean(dxh * xhat, axis=-1, keepdims=True))
        dx_ref[...] = dx
        dy_ref[...] = (0.5 * gate_ref[...] * dx).astype(BF16)
        _add_rows(sums_ref, [jnp.sum(dyf * xhat, axis=0, keepdims=True),
                             jnp.sum(err * err, axis=0, keepdims=True) * (0.5 / d)])

    row = pl.BlockSpec((tm, d), lambda i: (i, 0))
    return pl.pallas_call(
        body, name="final_loss", grid=(t // tm,),
        in_specs=[row, row, _row(g), _row(gate)],
        out_specs=[row, row, pl.BlockSpec((8, d), lambda i: (0, 0))],
        out_shape=[jax.ShapeDtypeStruct((t, d), F32), jax.ShapeDtypeStruct((t, d), BF16),
                   jax.ShapeDtypeStruct((8, d), F32)],
        compiler_params=_params(("arbitrary",)),
    )(x, target, g, gate)


def adamw(w, g, m, v, name):
    r, n = w.shape
    tr = _tile(r, max(8, (1 << 19) // n), 8)

    def body(w_ref, g_ref, m_ref, v_ref, d_ref, mo_ref, vo_ref):
        gv = g_ref[...]
        m_new = ADAM_B1 * m_ref[...] + (1.0 - ADAM_B1) * gv
        v_new = ADAM_B2 * v_ref[...] + (1.0 - ADAM_B2) * (gv * gv)
        m_hat = m_new / (1.0 - ADAM_B1 ** ADAM_STEP)
        v_hat = v_new / (1.0 - ADAM_B2 ** ADAM_STEP)
        d_ref[...] = -ADAM_LR * (m_hat / (jnp.sqrt(v_hat) + ADAM_EPS) + ADAM_WD * w_ref[...])
        mo_ref[...] = m_new
        vo_ref[...] = v_new

    blk = pl.BlockSpec((tr, n), lambda i: (i, 0))
    shape = jax.ShapeDtypeStruct((r, n), F32)
    return pl.pallas_call(
        body, name=name, grid=(r // tr,), in_specs=[blk] * 4, out_specs=[blk] * 3, out_shape=[shape] * 3,
        compiler_params=_params(("arbitrary",)),
    )(w, g, m, v)


def _pad_to(v, n):
    return jnp.pad(v, (0, n - v.shape[0]))


def _pad_heads(w, axis_len):
    n = w.shape[1]
    return jnp.pad(w.reshape(MLA_HEADS, axis_len, n), ((0, 0), (0, HEAD_PAD - axis_len), (0, 0))).reshape(-1, n)


def _swap_head_parts(w, inner, outer):
    n = w.shape[1]
    return w.reshape(outer, inner, QK_NOPE, n).transpose(1, 0, 2, 3).reshape(-1, n)


def kernel(x, c, positions, ada_w, ada_b, norm_ffn1_g, ffn1_w1, ffn1_w3, ffn1_w2, norm_mix_g, w_in, conv_w, q_norm_g, w_uq, kv_norm_g, w_ukv, out_norm_g, w_out, norm_ffn2_g, ffn2_w1, ffn2_w3, ffn2_w2, final_norm_g, loss_target, m_ada_w, m_ada_b, m_norm_ffn1_g, m_ffn1_w1, m_ffn1_w3, m_ffn1_w2, m_norm_mix_g, m_w_in, m_conv_w, m_q_norm_g, m_w_uq, m_kv_norm_g, m_w_ukv, m_out_norm_g, m_w_out, m_norm_ffn2_g, m_ffn2_w1, m_ffn2_w3, m_ffn2_w2, m_final_norm_g, v_ada_w, v_ada_b, v_norm_ffn1_g, v_ffn1_w1, v_ffn1_w3, v_ffn1_w2, v_norm_mix_g, v_w_in, v_conv_w, v_q_norm_g, v_w_uq, v_kv_norm_g, v_w_ukv, v_out_norm_g, v_w_out, v_norm_ffn2_g, v_ffn2_w1, v_ffn2_w3, v_ffn2_w2, v_final_norm_g):
    t, d = x.shape[1], x.shape[2]
    f = ffn1_w2.shape[1] * N_DEV
    me = 4 * lax.axis_index("x") + 2 * lax.axis_index("y") + lax.axis_index("c")
    my_c = lax.axis_index("c")
    my_chip = 2 * lax.axis_index("x") + lax.axis_index("y")
    xs = x[0]
    n_ada = ada_w.shape[2]
    cw_n = conv_w.shape[2]

    c_rows = jnp.broadcast_to(c, (8, d))
    conv_rows = jnp.pad(conv_w[0], ((0, 8 - CONV_K), (0, LANES - cw_n)))
    ffn1_blocks = jnp.stack([ffn1_w1[0].T, ffn1_w3[0].T, ffn1_w2[0]]).astype(BF16)
    ffn2_blocks = jnp.stack([ffn2_w1[0].T, ffn2_w3[0].T, ffn2_w2[0]]).astype(BF16)
    c_all, conv_all, ffn1_all = all_gather([c_rows, conv_rows, ffn1_blocks], [0, 0, 1], "gather_first")
    c_all = c_all[:, 0, :]
    conv_full8 = conv_all[:, :, :cw_n].transpose(1, 0, 2).reshape(8, CONV_WIDTH)
    ffn1_ws = ffn1_all.reshape(3, f, d)
    gather_rest = riding_gather(
        [ffn2_blocks, w_in[0].T.astype(BF16), w_uq[0].T.astype(BF16), w_ukv[0].T.astype(BF16), w_out[0].astype(BF16)],
        [1, 0, 0, 0, 0])

    ada_b_cols = lax.dynamic_slice_in_dim(ada_b, me * n_ada, n_ada, axis=1)
    mod_cols = ada_forward(c_all, ada_w[0], ada_b_cols)
    mod_all, = all_gather([mod_cols], [0], "gather_mod")
    mod = lax.dynamic_index_in_dim(mod_all, me, axis=1, keepdims=False).reshape(N_MOD, 1, d)
    sh1, sc1, g1, sh2, sc2, g2, sh3, sc3, g3 = [mod[i] for i in range(N_MOD)]

    gf = final_norm_g.reshape(1, d)
    x1, h1, a1, b1, y1, *gathered = ffn_forward(xs, norm_ffn1_g, sc1, sh1, g1, ffn1_ws, 0, "ffn1_fwd", gather_rest)
    ffn2_ws = gathered[0].reshape(3, f, d)
    w_in_p = jnp.pad(gathered[1].reshape(IN_COLS, d), ((0, ZC_COLS + ZM_COLS - IN_COLS), (0, 0)))
    w_uq_p = _pad_heads(gathered[2].reshape(-1, Q_LORA), QK_NOPE + QK_ROPE)
    w_ukv_p = _swap_head_parts(gathered[3].reshape(-1, KV_LORA), 2, MLA_HEADS)
    w_out_f = gathered[4].reshape(MIX_WIDTH, d)
    h2, zc, zm = mix_in_forward(x1, norm_mix_g, sc2, sh2, w_in_p)
    pos = positions[0].astype(F32).reshape(t, 1)
    inv_freq = ROPE_THETA ** (-jnp.arange(0, QK_ROPE, 2, dtype=F32) / QK_ROPE)
    inv_freq = jnp.concatenate([inv_freq, inv_freq, jnp.zeros((LANES - QK_ROPE,), F32)]).reshape(1, LANES)
    qn, kvn, q, k, v = mla_project(zm, pos, inv_freq, q_norm_g, kv_norm_g, w_uq_p, w_ukv_p)
    o, lse = attention_forward(q, k, v)
    lane = jnp.arange(CONV_WIDTH)
    gmat_a = (lane[:, None] // (CONV_WIDTH // CONV_GROUPS) == lane[None, :] // (CONV_WIDTH // CONV_GROUPS))
    gmat_a = (gmat_a / (CONV_WIDTH // CONV_GROUPS)).astype(BF16)
    gmat_b = ((lane[:, None] // V_HEAD == lane[None, :] // V_HEAD) / V_HEAD).astype(BF16)
    x2, yn, y2, ya = mix_out_forward(zc, o, conv_full8, out_norm_g, gmat_a, gmat_b, w_out_f, x1, g2)
    x3, h3, a3, b3, y3 = ffn_forward(x2, norm_ffn2_g, sc3, sh3, g3, ffn2_ws, 0, "ffn2_fwd")
    dx3, dy3, sums_f = final_loss(x3, loss_target[0], gf, g3)

    chip_idx = jnp.bitwise_xor(my_chip, jnp.array([0, 2, 1, 3], jnp.int32)).astype(jnp.int32)
    src_idx = (2 * chip_idx + my_c).astype(jnp.int32)

    def row_blocks(named):
        return [g.reshape(N_DEV, g.shape[0] // N_DEV, g.shape[1]) for _, g in named]

    def chip_sums(named, g8, got):
        return [add_sibling(g, r, src_idx, chip_idx, "rs_add_" + n) for g, r, (n, _) in zip(g8, got, named)]

    da3, db3, u3 = ffn_backward_gate(dy3, a3, b3, ffn2_ws, 0, "ffn2_bwd_gate")
    dx2, sums_3 = ffn_backward_norm(da3, db3, dx3, x2, y3, norm_ffn2_g, sc3, ffn2_ws, 0, "ffn2_bwd_norm")
    ffn2_named = [("ffn2_w1", matmul_tn(da3, h3, "ffn2_gw1")), ("ffn2_w3", matmul_tn(db3, h3, "ffn2_gw3")),
                  ("ffn2_w2", matmul_tn(u3, dy3, "ffn2_gw2"))]
    ffn2_g8 = row_blocks(ffn2_named)
    dy2, dya, do, delta, sums_2d, sums_2o, *ffn2_sib = mix_out_backward(
        dx2, y2, g2, ya, o, out_norm_g, gmat_a, gmat_b, w_out_f, riding_sibling(ffn2_g8))
    ffn2_sums = chip_sums(ffn2_named, ffn2_g8, ffn2_sib)
    g_w_out = matmul_tn(yn, dy2, "gw_out")
    nq = t // _tile(t, ATTN_TILE, CHUNK)
    stat_shape = (MLA_HEADS, nq, 1, t // nq)
    dq, dk, dv, *ffn2_got = attention_backward(q, k, v, do, lse.reshape(stat_shape), delta.reshape(stat_shape),
                                               riding_exchange([s[1] for s in ffn2_sums]))
    dzc, sums_c = conv_backward(zc, dya, conv_full8)
    dql, dkvl, dzm, sums_m = mla_project_backward(dq, dk, dv, zm, pos, inv_freq, q_norm_g, kv_norm_g, w_uq_p, w_ukv_p)
    g_w_uq_p = matmul_tn(dql, qn, "gw_uq")
    g_w_ukv_p = matmul_tn(dkvl, kvn, "gw_ukv")
    g_w_in = matmul_tn([dzc, dzm], h2, "gw_in")[:IN_COLS]
    g_w_uq = g_w_uq_p.reshape(MLA_HEADS, HEAD_PAD, Q_LORA)[:, :QK_NOPE + QK_ROPE].reshape(-1, Q_LORA)
    g_w_ukv = _swap_head_parts(g_w_ukv_p, MLA_HEADS, 2)
    mix_named = [("w_in", g_w_in), ("w_uq", g_w_uq), ("w_ukv", g_w_ukv), ("w_out", g_w_out)]
    mix_g8 = row_blocks(mix_named)
    dx1, dy1, sums_1m, *mix_sib = mix_in_backward(dzc, dzm, w_in_p, x1, dx2, norm_mix_g, sc2, g1, riding_sibling(mix_g8))
    mix_sums = chip_sums(mix_named, mix_g8, mix_sib)
    da1, db1, u1, *mix_got = ffn_backward_gate(dy1, a1, b1, ffn1_ws, 0, "ffn1_bwd_gate",
                                               riding_exchange([s[1] for s in mix_sums]))
    ffn1_pair = [("ffn1_w1", matmul_tn(da1, h1, "ffn1_gw1")), ("ffn1_w3", matmul_tn(db1, h1, "ffn1_gw3"))]
    pair_g8 = row_blocks(ffn1_pair)
    g_w2a, *pair_sib = matmul_tn(u1, dy1, "ffn1_gw2", riding_sibling(pair_g8))
    ffn1_last = [("ffn1_w2", g_w2a)]
    last_g8 = row_blocks(ffn1_last)
    ffn1_named = ffn1_pair + ffn1_last
    ffn1_sums = chip_sums(ffn1_pair, pair_g8, pair_sib) + chip_sums(
        ffn1_last, last_g8, exchange_sibling(last_g8, "rs_sibling_ffn1_w2"))
    dx0, sums_1, *ffn1_got = ffn_backward_norm(da1, db1, dx1, xs, y1, norm_ffn1_g, sc1, ffn1_ws, 0, "ffn1_bwd_norm",
                                               riding_exchange([s[1] for s in ffn1_sums]))
    transposed = {"ffn1_w1", "ffn1_w3", "ffn2_w1", "ffn2_w3", "w_in", "w_uq", "w_ukv"}
    g_sh = {}
    for named, group_sums, group_got in ((ffn2_named, ffn2_sums, ffn2_got), (mix_named, mix_sums, mix_got),
                                         (ffn1_named, ffn1_sums, ffn1_got)):
        for (n, _), (own, _), got in zip(named, group_sums, group_got):
            g_rows = add_received(own, got, "rs_sum_" + n)
            g_sh[n] = g_rows.T if n in transposed else g_rows

    dmod = jnp.concatenate([sums_1[0], sums_1[1], sums_1[2], sums_1m[0], sums_1m[1], sums_2d[0],
                            sums_3[0], sums_3[1], sums_3[2]])
    pieces = [dmod, sums_1[3], sums_1m[2], sums_m[0, :Q_LORA], sums_m[0, Q_LORA:Q_LORA + KV_LORA], sums_2o[0],
              sums_3[3], sums_f[0], sums_f[1], sums_c[:CONV_K].reshape(-1)]
    plens = [p.shape[0] for p in pieces]
    poffs = [sum(plens[:i]) for i in range(len(plens))]
    vec_len = -(-sum(plens) // 1024) * 1024
    vec = _pad_to(jnp.concatenate(pieces), vec_len).reshape(-1, LANES)
    vec_all, = all_gather([vec], [0], "gather_sums")
    tot = sum_devices(vec_all).reshape(-1)
    g_ada_b, g_n1, g_nmix, g_qg, g_kvg, g_og, g_n3, g_gf, loss_lanes, g_conv_full = [
        tot[o:o + n] for o, n in zip(poffs, plens)]
    loss = sum_lanes(loss_lanes.reshape(1, d))[0, 0]
    g_conv = lax.dynamic_slice_in_dim(g_conv_full.reshape(CONV_K, CONV_WIDTH), me * cw_n, cw_n, axis=1)
    dmod_all = vec_all.reshape(N_DEV, vec_len)[:, :N_MOD * d]
    dmod_cols = lax.dynamic_slice_in_dim(dmod_all, me * n_ada, n_ada, axis=1)
    g_ada_w = ada_backward(jnp.pad(c_all, ((0, 8), (0, 0))), jnp.pad(dmod_cols, ((0, 8), (0, 0))))

    def update(name, w, g, m, v):
        shape = w.shape
        two_d = (-1, shape[-1])
        dlt, nm, nv = adamw(w.reshape(two_d), g.reshape(two_d), m.reshape(two_d), v.reshape(two_d), "adamw_" + name)
        return g.reshape(shape), dlt.reshape(shape), nm.reshape(shape), nv.reshape(shape)

    res = {}
    res["ada_w"] = update("ada_w", ada_w, g_ada_w, m_ada_w, v_ada_w)
    big = [("ffn1_w1", ffn1_w1, m_ffn1_w1, v_ffn1_w1), ("ffn1_w3", ffn1_w3, m_ffn1_w3, v_ffn1_w3),
           ("ffn2_w1", ffn2_w1, m_ffn2_w1, v_ffn2_w1), ("ffn2_w3", ffn2_w3, m_ffn2_w3, v_ffn2_w3),
           ("w_in", w_in, m_w_in, v_w_in), ("w_uq", w_uq, m_w_uq, v_w_uq), ("w_ukv", w_ukv, m_w_ukv, v_w_ukv),
           ("ffn1_w2", ffn1_w2, m_ffn1_w2, v_ffn1_w2), ("ffn2_w2", ffn2_w2, m_ffn2_w2, v_ffn2_w2),
           ("w_out", w_out, m_w_out, v_w_out)]
    for name, w, m, v in big:
        res[name] = update(name, w, g_sh[name], m, v)
    smalls = [("ada_b", ada_b, g_ada_b, m_ada_b, v_ada_b),
              ("norm_ffn1_g", norm_ffn1_g, g_n1, m_norm_ffn1_g, v_norm_ffn1_g),
              ("norm_mix_g", norm_mix_g, g_nmix, m_norm_mix_g, v_norm_mix_g),
              ("conv_w", conv_w, g_conv, m_conv_w, v_conv_w),
              ("q_norm_g", q_norm_g, g_qg, m_q_norm_g, v_q_norm_g),
              ("kv_norm_g", kv_norm_g, g_kvg, m_kv_norm_g, v_kv_norm_g),
              ("out_norm_g", out_norm_g, g_og, m_out_norm_g, v_out_norm_g),
              ("norm_ffn2_g", norm_ffn2_g, g_n3, m_norm_ffn2_g, v_norm_ffn2_g),
              ("final_norm_g", final_norm_g, g_gf, m_final_norm_g, v_final_norm_g)]
    slens = [w.size for _, w, _, _, _ in smalls]
    soffs = [sum(slens[:i]) for i in range(len(slens))]
    s_len = -(-sum(slens) // 1024) * 1024

    def pack_small(i):
        return _pad_to(jnp.concatenate([s[i].reshape(-1) for s in smalls]), s_len).reshape(8, -1)

    s_out = adamw(pack_small(1), pack_small(2), pack_small(3), pack_small(4), "adamw_small")
    for (name, w, g, _, _), o, n in zip(smalls, soffs, slens):
        res[name] = (g.reshape(w.shape),) + tuple(a.reshape(-1)[o:o + n].reshape(w.shape) for a in s_out)

    order = ["ada_w", "ada_b", "norm_ffn1_g", "ffn1_w1", "ffn1_w3", "ffn1_w2", "norm_mix_g", "w_in", "conv_w",
             "q_norm_g", "w_uq", "kv_norm_g", "w_ukv", "out_norm_g", "w_out", "norm_ffn2_g", "ffn2_w1", "ffn2_w3",
             "ffn2_w2", "final_norm_g"]
    return (loss, dx0.reshape(x.shape), *[res[n][0] for n in order], *[res[n][1] for n in order],
            *[res[n][2] for n in order], *[res[n][3] for n in order])
```

```python
import functools
import math

import jax
import jax.numpy as jnp
from jax import lax
from jax.experimental import pallas as pl
from jax.experimental.pallas import tpu as pltpu

F32 = jnp.float32
BF16 = jnp.bfloat16
MESH_ID = pl.DeviceIdType.MESH
N_DEV = 8

EPS = 1e-6
CHUNK = 64
N_MOD = 9
CONV_WIDTH = 512
CONV_GROUPS = 8
CONV_K = 3
MLA_HEADS = 4
QK_NOPE = 128
QK_ROPE = 64
V_HEAD = 128
Q_LORA = 384
KV_LORA = 256
ROPE_THETA = 10000.0
MLA_WIDTH = MLA_HEADS * V_HEAD
MIX_WIDTH = CONV_WIDTH + MLA_WIDTH
IN_COLS = 3 * CONV_WIDTH + Q_LORA + KV_LORA + QK_ROPE
ZC_COLS = 3 * CONV_WIDTH
ZM_COLS = Q_LORA + KV_LORA + 128
HEAD_PAD = 256
QK_COLS = MLA_HEADS * HEAD_PAD
ATTN_SCALE = (QK_NOPE + QK_ROPE) ** -0.5
LOG2_E = 1.4426950408889634
LN_2 = 0.6931471805599453
QK_FOLD = ATTN_SCALE * LOG2_E
NEG_INF = -1e30

ADAM_LR = 0.001
ADAM_B1 = 0.9
ADAM_B2 = 0.999
ADAM_EPS = 1e-08
ADAM_WD = 0.01
ADAM_STEP = 10

LANES = 128
MXU_COLS = 256
VMEM_LIMIT = 56 * 1024 * 1024
ROW_TILE = 1024
FFN_FWD_TILE = (1024, 256)
FFN_BWD_TILE = (512, 1408)
GRAD_TILE = 1408
GRAD_DEPTH = 2048
SUM_ROWS = 256
ATTN_TILE = 1024

NN = (((1,), (0,)), ((), ()))
NT = (((1,), (1,)), ((), ()))
TN = (((0,), (0,)), ((), ()))


def _dot(a, b, dims=NN):
    return lax.dot_general(a, b, dims, preferred_element_type=F32)


def _tile(n, cap, mult=LANES):
    best = None
    for t in range(mult, min(n, cap) + 1, mult):
        if n % t == 0:
            best = t
    return n if best is None else best


def _params(sem=None):
    return pltpu.CompilerParams(dimension_semantics=sem, vmem_limit_bytes=VMEM_LIMIT)


def _row(v):
    return pl.BlockSpec(v.shape, lambda *_: (0,) * v.ndim)


def _sigmoid(x):
    return 0.5 * jnp.tanh(0.5 * x) + 0.5


def _rms(x):
    r = lax.rsqrt(jnp.mean(x * x, axis=-1, keepdims=True) + EPS)
    return x * r, r


def _norm_mod_bwd(dh, x, gn, sc):
    xhat, r = _rms(x)
    d_sh = jnp.sum(dh, axis=0, keepdims=True)
    d_sc = jnp.sum(dh * (xhat * gn), axis=0, keepdims=True)
    dxn = dh * (1.0 + sc)
    d_gn = jnp.sum(dxn * xhat, axis=0, keepdims=True)
    dxh = dxn * gn
    dx = r * (dxh - xhat * jnp.mean(dxh * xhat, axis=-1, keepdims=True))
    return dx, d_sh, d_sc, d_gn


def _group_mean(v, gmat):
    return _dot(v.astype(BF16), gmat)


def _add_rows(ref, rows):
    for r, v in enumerate(rows):
        ref[r:r + 1, :] += v


def _window(ref, axis, j):
    return ref.at[(slice(None),) * axis + (j,)]


def _any_specs(n):
    return [pl.BlockSpec(memory_space=pl.ANY)] * n


def all_gather(blocks, axes, name):
    n_arr = len(blocks)

    def body(*refs):
        start, forward, finish = _gather_steps(refs[:n_arr], refs[n_arr:2 * n_arr], axes, *refs[2 * n_arr:])
        start()
        for j in range(3):
            forward(j)
        finish()

    return pl.pallas_call(
        body, name=name, out_shape=_gathered_shapes(blocks, axes),
        in_specs=_any_specs(n_arr), out_specs=_any_specs(n_arr), scratch_shapes=_gather_sems(n_arr),
    )(*blocks)


def _gathered_shapes(blocks, axes):
    return [jax.ShapeDtypeStruct(b.shape[:ax] + (N_DEV,) + b.shape[ax:], b.dtype) for b, ax in zip(blocks, axes)]


def _gather_sems(n_arr):
    return [pltpu.SemaphoreType.DMA((7, n_arr)), pltpu.SemaphoreType.DMA((7, n_arr)), pltpu.SemaphoreType.DMA((n_arr,))]


def _gather_steps(ins, outs, axes, send_sems, recv_sems, local_sems):
    arrays = range(len(ins))
    x, y, c = lax.axis_index("x"), lax.axis_index("y"), lax.axis_index("c")
    me, sibling = (x, y, c), (x, y, 1 - c)
    chips = [(1 - x, y), (x, 1 - y), (1 - x, 1 - y)]

    def slot(a, px, py, pc):
        return _window(outs[a], axes[a], 4 * px + 2 * py + pc)

    def copy(a, k, block, to, src=None):
        return pltpu.make_async_remote_copy(
            src_ref=slot(a, *block) if src is None else src, dst_ref=slot(a, *block),
            send_sem=send_sems.at[k, a], recv_sem=recv_sems.at[k, a], device_id=to, device_id_type=MESH_ID)

    def mine(a):
        return pltpu.make_async_copy(ins[a], slot(a, *me), local_sems.at[a])

    def first():
        return ([copy(a, 0, me, sibling, src=ins[a]) for a in arrays]
                + [copy(a, 1 + j, me, (*chip, c), src=ins[a]) for j, chip in enumerate(chips) for a in arrays])

    def passed(j):
        return [copy(a, 4 + j, (*chips[j], c), sibling) for a in arrays]

    def start():
        for a in arrays:
            mine(a).start()
        for cp in first():
            cp.start()

    def forward(j):
        for a, cp in zip(arrays, passed(j)):
            copy(a, 1 + j, (*chips[j], c), me).wait_recv()
            cp.start()

    def finish():
        for a in arrays:
            copy(a, 0, sibling, me).wait_recv()
        for j, chip in enumerate(chips):
            for a in arrays:
                copy(a, 4 + j, (*chip, 1 - c), me).wait_recv()
        for cp in first() + passed(0) + passed(1) + passed(2):
            cp.wait_send()
        for a in arrays:
            mine(a).wait()

    return start, forward, finish


def exchange_sibling(grads, name):
    n_arr = len(grads)

    def body(*refs):
        start, finish = _sibling_exchange_steps(refs[:n_arr], refs[n_arr:2 * n_arr], *refs[2 * n_arr:])
        start()
        finish()

    return pl.pallas_call(
        body, name=name, out_shape=_sibling_shapes(grads),
        in_specs=_any_specs(n_arr), out_specs=_any_specs(n_arr), scratch_shapes=_exchange_sems(n_arr),
    )(*grads)


def _sibling_shapes(grads):
    return [jax.ShapeDtypeStruct((4,) + g.shape[1:], g.dtype) for g in grads]


def _exchange_sems(n_arr):
    return [pltpu.SemaphoreType.DMA((n_arr,)), pltpu.SemaphoreType.DMA((n_arr,))]


def _sibling_exchange_steps(ins, outs, send_sems, recv_sems):
    x, y, c = lax.axis_index("x"), lax.axis_index("y"), lax.axis_index("c")

    def copy(a, src, dst):
        return pltpu.make_async_remote_copy(
            src_ref=src, dst_ref=dst, send_sem=send_sems.at[a], recv_sem=recv_sems.at[a],
            device_id=(x, y, 1 - c), device_id_type=MESH_ID)

    def start():
        for a in range(len(ins)):
            for k in range(4):
                copy(a, ins[a].at[2 * k + (1 - c)], outs[a].at[k]).start()

    def finish():
        whole = [copy(a, ins[a].at[pl.ds(0, 4)], outs[a]) for a in range(len(ins))]
        for cp in whole:
            cp.wait_recv()
        for cp in whole:
            cp.wait_send()

    return start, finish


def _chip_exchange_steps(ins, outs, send_sems, recv_sems):
    x, y, c = lax.axis_index("x"), lax.axis_index("y"), lax.axis_index("c")
    chips = [(1 - x, y), (x, 1 - y), (1 - x, 1 - y)]

    def copy(a, src, dst, chip):
        return pltpu.make_async_remote_copy(
            src_ref=src, dst_ref=dst, send_sem=send_sems.at[a], recv_sem=recv_sems.at[a],
            device_id=(*chip, c), device_id_type=MESH_ID)

    def start():
        for a in range(len(ins)):
            for j, chip in enumerate(chips):
                copy(a, ins[a].at[j], outs[a].at[j], chip).start()

    def finish():
        whole = [copy(a, ins[a], outs[a], chips[0]) for a in range(len(ins))]
        for cp in whole:
            cp.wait_recv()
        for cp in whole:
            cp.wait_send()

    return start, finish


def riding_gather(blocks, axes):
    def phases(ins, outs, *sems):
        start, forward, finish = _gather_steps(ins, outs, axes, *sems)
        return [start] + [functools.partial(forward, j) for j in range(3)] + [finish]

    return dict(operands=blocks, out_shape=_gathered_shapes(blocks, axes), sems=_gather_sems(len(blocks)),
                phases=phases, when=("first", "late0", "late1", "late2", "last"))


def riding_exchange(parts):
    def phases(ins, outs, *sems):
        return list(_chip_exchange_steps(ins, outs, *sems))

    return dict(operands=parts, out_shape=[jax.ShapeDtypeStruct(p.shape, p.dtype) for p in parts],
                sems=_exchange_sems(len(parts)), phases=phases, when=("first", "last"))


def riding_sibling(grads):
    def phases(ins, outs, *sems):
        return list(_sibling_exchange_steps(ins, outs, *sems))

    return dict(operands=grads, out_shape=_sibling_shapes(grads), sems=_exchange_sems(len(grads)),
                phases=phases, when=("first", "last"))


def _call_with_rider(body, rider, *, name, grid, in_specs, out_specs, out_shape, scratch_shapes, operands):
    params = _params(("arbitrary",) * len(grid))
    if rider is None:
        return pl.pallas_call(body, name=name, grid=grid, in_specs=in_specs, out_specs=out_specs,
                              out_shape=out_shape, scratch_shapes=scratch_shapes, compiler_params=params)(*operands)
    n_in, n_out, n_scr, k = len(in_specs), len(out_specs), len(scratch_shapes), len(rider["operands"])
    at = {"first": (0,) * len(grid), "last": tuple(g - 1 for g in grid)}
    if "late0" in rider["when"]:
        rows, cols = grid
        assert cols >= 3
        at.update({"late%d" % j: (max(rows - 2, 0), j) for j in range(3)})

    def wrapped(*refs):
        ins, c_in = refs[:n_in], refs[n_in:n_in + k]
        outs, c_out = refs[n_in + k:n_in + k + n_out], refs[n_in + k + n_out:n_in + 2 * k + n_out]
        scratch, sems = refs[n_in + 2 * k + n_out:n_in + 2 * k + n_out + n_scr], refs[n_in + 2 * k + n_out + n_scr:]
        pos = [pl.program_id(axis) for axis in range(len(grid))]

        def here(key):
            return functools.reduce(jnp.logical_and, [p == v for p, v in zip(pos, at[key])])

        phases = rider["phases"](c_in, c_out, *sems)
        for fn, key in zip(phases, rider["when"]):
            if key != "last":
                pl.when(here(key))(fn)
        body(*ins, *outs, *scratch)
        pl.when(here("last"))(phases[-1])

    return pl.pallas_call(
        wrapped, name=name, grid=grid,
        in_specs=list(in_specs) + _any_specs(k), out_specs=list(out_specs) + _any_specs(k),
        out_shape=list(out_shape) + rider["out_shape"], scratch_shapes=list(scratch_shapes) + rider["sems"],
        compiler_params=params)(*operands, *rider["operands"])


def add_sibling(g8, got, src_idx, chip_idx, name):
    _, r, n = g8.shape
    tr = _tile(r, SUM_ROWS, 16)

    def body(si_ref, ci_ref, g0_ref, g1_ref, g2_ref, g3_ref, got_ref, own_ref, send_ref):
        own_ref[...] = g0_ref[0] + got_ref[ci_ref[0]]
        for j, g_ref in enumerate((g1_ref, g2_ref, g3_ref)):
            send_ref[j] = (g_ref[0] + got_ref[ci_ref[j + 1]]).astype(BF16)

    def mine(j):
        return pl.BlockSpec((1, tr, n), lambda i, si, ci: (si[j], i, 0))

    return pl.pallas_call(
        body, name=name,
        out_shape=[jax.ShapeDtypeStruct((r, n), F32), jax.ShapeDtypeStruct((3, r, n), BF16)],
        grid_spec=pltpu.PrefetchScalarGridSpec(
            num_scalar_prefetch=2, grid=(r // tr,),
            in_specs=[mine(0), mine(1), mine(2), mine(3), pl.BlockSpec((4, tr, n), lambda i, si, ci: (0, i, 0))],
            out_specs=[pl.BlockSpec((tr, n), lambda i, si, ci: (i, 0)),
                       pl.BlockSpec((3, tr, n), lambda i, si, ci: (0, i, 0))]),
        compiler_params=_params(("arbitrary",)),
    )(src_idx, chip_idx, g8, g8, g8, g8, got)


def add_received(own, got, name):
    r, n = own.shape
    tr = _tile(r, SUM_ROWS, 16)

    def body(a_ref, b_ref, o_ref):
        acc = a_ref[...]
        for j in range(3):
            acc = acc + b_ref[j].astype(F32)
        o_ref[...] = acc

    return pl.pallas_call(
        body, name=name,
        out_shape=jax.ShapeDtypeStruct((r, n), F32),
        grid=(r // tr,),
        in_specs=[pl.BlockSpec((tr, n), lambda i: (i, 0)), pl.BlockSpec((3, tr, n), lambda i: (0, i, 0))],
        out_specs=pl.BlockSpec((tr, n), lambda i: (i, 0)),
        compiler_params=_params(("arbitrary",)),
    )(own, got)


def sum_devices(g):
    def body(g_ref, o_ref):
        acc = g_ref[0]
        for j in range(1, N_DEV):
            acc = acc + g_ref[j]
        o_ref[...] = acc

    return pl.pallas_call(body, name="sum_devices", out_shape=jax.ShapeDtypeStruct(g.shape[1:], F32))(g)


def sum_lanes(v):
    def body(v_ref, o_ref):
        o_ref[...] = jnp.broadcast_to(jnp.sum(v_ref[...], axis=-1, keepdims=True), (1, LANES))

    return pl.pallas_call(body, name="sum_lanes", out_shape=jax.ShapeDtypeStruct((1, LANES), F32))(v)


def ada_forward(c_all, ada_w, ada_b_cols):
    nb, n = c_all.shape[0], ada_w.shape[1]

    def body(c_ref, w_ref, b_ref, o_ref):
        cv = c_ref[...]
        s = (cv * jax.nn.sigmoid(cv)).astype(BF16)
        o_ref[...] = _dot(s, w_ref[...].astype(BF16)) + b_ref[...]

    return pl.pallas_call(body, name="ada_fwd", out_shape=jax.ShapeDtypeStruct((nb, n), F32),
                          compiler_params=_params())(c_all, ada_w, ada_b_cols)


def ada_backward(c_all16, dmod16):
    d, n = c_all16.shape[1], dmod16.shape[1]

    def body(c_ref, g_ref, o_ref):
        cv = c_ref[...]
        s = (cv * jax.nn.sigmoid(cv)).astype(BF16)
        o_ref[...] = _dot(s, g_ref[...].astype(BF16), TN)

    return pl.pallas_call(body, name="ada_bwd", out_shape=jax.ShapeDtypeStruct((d, n), F32),
                          compiler_params=_params())(c_all16, dmod16)


def ffn_forward(x, gn, sc, sh, gate, ws, first, name, rider=None):
    t, d = x.shape
    f = ws.shape[1]
    tm, tf = _tile(t, FFN_FWD_TILE[0], 16), _tile(f, FFN_FWD_TILE[1])
    nf = f // tf

    def body(x_ref, gn_ref, sc_ref, sh_ref, gate_ref, w1_ref, w3_ref, w2_ref,
             xo_ref, h_ref, a_ref, b_ref, y_ref, hs, acc):
        j = pl.program_id(1)

        @pl.when(j == 0)
        def _():
            xhat, _ = _rms(x_ref[...])
            h = (xhat * gn_ref[...] * (1.0 + sc_ref[...]) + sh_ref[...]).astype(BF16)
            hs[...] = h
            h_ref[...] = h
            acc[...] = jnp.zeros_like(acc)

        h = hs[...]
        a = _dot(h, w1_ref[...], NT)
        b = _dot(h, w3_ref[...], NT)
        a_ref[...] = a.astype(BF16)
        b_ref[...] = b.astype(BF16)
        u = (a * _sigmoid(a) * b).astype(BF16)
        acc[...] += _dot(u, w2_ref[...])

        @pl.when(j == nf - 1)
        def _():
            y = acc[...]
            y_ref[...] = y.astype(BF16)
            xo_ref[...] = x_ref[...] + 0.5 * gate_ref[...] * y

    row = pl.BlockSpec((tm, d), lambda i, j: (i, 0))
    vec = pl.BlockSpec((1, d), lambda i, j: (0, 0))
    wide = pl.BlockSpec((tm, tf), lambda i, j: (i, j))
    return _call_with_rider(
        body, rider, name=name, grid=(t // tm, nf),
        in_specs=[row, vec, vec, vec, vec] + _ffn_weight_specs(first, tf, d),
        out_specs=[row, row, wide, wide, row],
        out_shape=[jax.ShapeDtypeStruct((t, d), F32), jax.ShapeDtypeStruct((t, d), BF16),
                   jax.ShapeDtypeStruct((t, f), BF16), jax.ShapeDtypeStruct((t, f), BF16),
                   jax.ShapeDtypeStruct((t, d), BF16)],
        scratch_shapes=[pltpu.VMEM((tm, d), BF16), pltpu.VMEM((tm, d), F32)],
        operands=(x, gn, sc, sh, gate, ws, ws, ws))


def _ffn_weight_specs(first, tf, d):
    return [pl.BlockSpec((None, tf, d), lambda i, j, w=first + k: (w, j, 0)) for k in range(3)]


def ffn_backward_gate(dy, a, b, ws, first, name, rider=None):
    t, d = dy.shape
    f = ws.shape[1]
    tm, tf = _tile(t, FFN_BWD_TILE[0], 16), _tile(f, FFN_BWD_TILE[1])
    nf = f // tf

    def gate_body(dy_ref, a_ref, b_ref, w2_ref, da_ref, db_ref, u_ref):
        du = _dot(dy_ref[...], w2_ref[...], NT)
        av = a_ref[...].astype(F32)
        bv = b_ref[...].astype(F32)
        s = _sigmoid(av)
        sa = av * s
        da_ref[...] = (du * bv * (s + sa * (1.0 - s))).astype(BF16)
        db_ref[...] = (du * sa).astype(BF16)
        u_ref[...] = (sa * bv).astype(BF16)

    hidden = jax.ShapeDtypeStruct((t, f), BF16)
    wide_t = pl.BlockSpec((tm, tf), lambda j, i: (i, j))
    return _call_with_rider(
        gate_body, rider, name=name, grid=(nf, t // tm),
        in_specs=[pl.BlockSpec((tm, d), lambda j, i: (i, 0)), wide_t, wide_t,
                  pl.BlockSpec((None, tf, d), lambda j, i: (first + 2, j, 0))],
        out_specs=[wide_t, wide_t, wide_t], out_shape=[hidden, hidden, hidden],
        scratch_shapes=[], operands=(dy, a, b, ws))


def ffn_backward_norm(da, db, dxo, x, y, gn, sc, ws, first, name, rider=None):
    t, d = x.shape
    f = ws.shape[1]
    tm, tf = _tile(t, FFN_BWD_TILE[0], 16), _tile(f, FFN_BWD_TILE[1])
    nf = f // tf
    row = pl.BlockSpec((tm, d), lambda i, j: (i, 0))
    vec = pl.BlockSpec((1, d), lambda i, j: (0, 0))
    wide = pl.BlockSpec((tm, tf), lambda i, j: (i, j))

    def norm_body(da_ref, db_ref, w1_ref, w3_ref, dxo_ref, x_ref, y_ref, gn_ref, sc_ref, dx_ref, sums_ref, acc):
        i, j = pl.program_id(0), pl.program_id(1)

        @pl.when(jnp.logical_and(i == 0, j == 0))
        def _():
            sums_ref[...] = jnp.zeros_like(sums_ref)

        part = _dot(da_ref[...], w1_ref[...]) + _dot(db_ref[...], w3_ref[...])

        @pl.when(j == 0)
        def _():
            acc[...] = part

        @pl.when(jnp.logical_and(j > 0, j < nf - 1))
        def _():
            acc[...] += part

        @pl.when(j == nf - 1)
        def _():
            dh = part if nf == 1 else acc[...] + part
            dxo_v = dxo_ref[...]
            dx, d_sh, d_sc, d_gn = _norm_mod_bwd(dh, x_ref[...], gn_ref[...], sc_ref[...])
            dx_ref[...] = dxo_v + dx
            d_gate = jnp.sum(dxo_v * (0.5 * y_ref[...].astype(F32)), axis=0, keepdims=True)
            _add_rows(sums_ref, [d_sh, d_sc, d_gate, d_gn])

    w1_spec, w3_spec, _ = _ffn_weight_specs(first, tf, d)
    return _call_with_rider(
        norm_body, rider, name=name, grid=(t // tm, nf),
        in_specs=[wide, wide, w1_spec, w3_spec, row, row, row, vec, vec],
        out_specs=[row, pl.BlockSpec((8, d), lambda i, j: (0, 0))],
        out_shape=[jax.ShapeDtypeStruct((t, d), F32), jax.ShapeDtypeStruct((8, d), F32)],
        scratch_shapes=[pltpu.VMEM((tm, d), F32)],
        operands=(da, db, ws, ws, dxo, x, y, gn, sc))


def matmul_tn(a, b, name, rider=None):
    parts = list(a) if isinstance(a, (list, tuple)) else [a]
    t, n = b.shape
    widths = [p.shape[1] for p in parts]
    tm = _tile(functools.reduce(math.gcd, widths), GRAD_TILE)
    tn, tk = _tile(n, GRAD_TILE), _tile(t, GRAD_DEPTH, 16)
    nk = t // tk
    counts = [w // tm for w in widths]
    firsts = [sum(counts[:p]) for p in range(len(parts))]

    def body(*refs):
        a_refs, (b_ref, o_ref, acc) = refs[:len(parts)], refs[len(parts):]
        i, k = pl.program_id(0), pl.program_id(2)

        @pl.when(k == 0)
        def _():
            acc[...] = jnp.zeros_like(acc)

        for a_ref, lo, cnt in zip(a_refs, firsts, counts):
            def accumulate(a_ref=a_ref):
                acc[...] += _dot(a_ref[...], b_ref[...], TN)

            if len(parts) == 1:
                accumulate()
            else:
                pl.when(jnp.logical_and(i >= lo, i < lo + cnt))(accumulate)

        @pl.when(k == nk - 1)
        def _():
            o_ref[...] = acc[...]

    def part_spec(lo, cnt):
        if len(parts) == 1:
            return pl.BlockSpec((tk, tm), lambda i, j, k: (k, i))

        def index(i, j, k):
            mine = jnp.logical_and(i >= lo, i < lo + cnt)
            return jnp.where(mine, k, 0), jnp.clip(i - lo, 0, cnt - 1)
        return pl.BlockSpec((tk, tm), index)

    out = _call_with_rider(
        body, rider, name=name, grid=(sum(counts), n // tn, nk),
        in_specs=[part_spec(lo, cnt) for lo, cnt in zip(firsts, counts)]
        + [pl.BlockSpec((tk, tn), lambda i, j, k: (k, j))],
        out_specs=[pl.BlockSpec((tm, tn), lambda i, j, k: (i, j))],
        out_shape=[jax.ShapeDtypeStruct((sum(widths), n), F32)],
        scratch_shapes=[pltpu.VMEM((tm, tn), F32)], operands=(*parts, b))
    return out[0] if rider is None else out


def mix_in_forward(x, gn, sc, sh, w_in):
    t, d = x.shape
    tm = _tile(t, ROW_TILE, 16)

    def body(x_ref, gn_ref, sc_ref, sh_ref, w_ref, h_ref, zc_ref, zm_ref):
        xhat, _ = _rms(x_ref[...])
        h = (xhat * gn_ref[...] * (1.0 + sc_ref[...]) + sh_ref[...]).astype(BF16)
        h_ref[...] = h
        z = _dot(h, w_ref[...], NT)
        zc_ref[...] = z[:, :ZC_COLS].astype(BF16)
        zm_ref[...] = z[:, ZC_COLS:].astype(BF16)

    row = pl.BlockSpec((tm, d), lambda i: (i, 0))
    vec = pl.BlockSpec((1, d), lambda i: (0, 0))
    return pl.pallas_call(
        body, name="mix_in_fwd", grid=(t // tm,),
        in_specs=[row, vec, vec, vec, _row(w_in)],
        out_specs=[row, pl.BlockSpec((tm, ZC_COLS), lambda i: (i, 0)), pl.BlockSpec((tm, ZM_COLS), lambda i: (i, 0))],
        out_shape=[jax.ShapeDtypeStruct((t, d), BF16), jax.ShapeDtypeStruct((t, ZC_COLS), BF16),
                   jax.ShapeDtypeStruct((t, ZM_COLS), BF16)],
        compiler_params=_params(("arbitrary",)),
    )(x, gn, sc, sh, w_in)


def _rope_tables(pos, inv_freq):
    ang = pos * inv_freq
    lane = lax.broadcasted_iota(jnp.int32, ang.shape, 1)
    cos, sin = jnp.cos(ang), jnp.sin(ang)
    half = QK_ROPE // 2
    return cos, jnp.where(lane < half, -sin, 0.0), jnp.where(jnp.logical_and(lane >= half, lane < QK_ROPE), sin, 0.0)


def _rope(v, tables):
    cos, sin_a, sin_b = tables
    return v * cos + pltpu.roll(v, LANES - QK_ROPE // 2, 1) * sin_a + pltpu.roll(v, QK_ROPE // 2, 1) * sin_b


def _rope_transposed(dv, tables):
    cos, sin_a, sin_b = tables
    return dv * cos + pltpu.roll(dv * sin_a, QK_ROPE // 2, 1) + pltpu.roll(dv * sin_b, LANES - QK_ROPE // 2, 1)


def mla_project(zm, pos, inv_freq, qg, kvg, w_uq, w_ukv):
    t = zm.shape[0]
    tm = _tile(t, ROW_TILE, 16)

    def body(zm_ref, pos_ref, if_ref, qg_ref, kvg_ref, wq_ref, wkv_ref, qn_ref, kvn_ref, q_ref, k_ref, v_ref):
        zv = zm_ref[...].astype(F32)
        qn = (_rms(zv[:, :Q_LORA])[0] * qg_ref[...]).astype(BF16)
        kvn = (_rms(zv[:, Q_LORA:Q_LORA + KV_LORA])[0] * kvg_ref[...]).astype(BF16)
        qn_ref[...] = qn
        kvn_ref[...] = kvn
        qf = _dot(qn, wq_ref[...], NT) * QK_FOLD
        kvf = _dot(kvn, wkv_ref[...], NT)
        tables = _rope_tables(pos_ref[...], if_ref[...])
        kr = _rope(zv[:, Q_LORA + KV_LORA:], tables).astype(BF16)
        for h in range(MLA_HEADS):
            lo = h * HEAD_PAD
            q_ref[:, lo:lo + QK_NOPE] = qf[:, lo:lo + QK_NOPE].astype(BF16)
            q_ref[:, lo + QK_NOPE:lo + HEAD_PAD] = _rope(qf[:, lo + QK_NOPE:lo + HEAD_PAD], tables).astype(BF16)
            k_ref[:, lo:lo + QK_NOPE] = kvf[:, h * QK_NOPE:(h + 1) * QK_NOPE].astype(BF16)
            k_ref[:, lo + QK_NOPE:lo + HEAD_PAD] = kr
        v_ref[...] = kvf[:, MLA_HEADS * QK_NOPE:].astype(BF16)

    def rows(n):
        return pl.BlockSpec((tm, n), lambda i: (i, 0))

    return pl.pallas_call(
        body, name="mla_project", grid=(t // tm,),
        in_specs=[rows(ZM_COLS), rows(1), _row(inv_freq), _row(qg), _row(kvg), _row(w_uq), _row(w_ukv)],
        out_specs=[rows(Q_LORA), rows(KV_LORA), rows(QK_COLS), rows(QK_COLS), rows(MLA_WIDTH)],
        out_shape=[jax.ShapeDtypeStruct((t, Q_LORA), BF16), jax.ShapeDtypeStruct((t, KV_LORA), BF16),
                   jax.ShapeDtypeStruct((t, QK_COLS), BF16), jax.ShapeDtypeStruct((t, QK_COLS), BF16),
                   jax.ShapeDtypeStruct((t, MLA_WIDTH), BF16)],
        compiler_params=_params(("arbitrary",)),
    )(zm, pos, inv_freq, qg, kvg, w_uq, w_ukv)


def _chunk_mask(shape, q_axis):
    qi = lax.broadcasted_iota(jnp.int32, shape, q_axis) // CHUNK
    ki = lax.broadcasted_iota(jnp.int32, shape, 1 - q_axis) // CHUNK
    return ki <= qi


def attention_forward(q, k, v):
    t = q.shape[0]
    tq = _tile(t, ATTN_TILE, CHUNK)

    def body(q_ref, k_ref, v_ref, o_ref, lse_ref):
        i = pl.program_id(1)
        qv = q_ref[...]

        def step(kb, carry, masked, tiles=1):
            m, l, acc = carry
            keys = pl.ds(pl.multiple_of(kb * tq, tq), tiles * tq)
            s = _dot(qv, k_ref[keys, :], NT)
            if masked:
                s = jnp.where(_chunk_mask(s.shape, 0), s, NEG_INF)
            m_new = jnp.maximum(m, jnp.max(s, axis=-1, keepdims=True))
            alpha = jnp.exp2(m - m_new)
            p = jnp.exp2(s - m_new)
            l = alpha * l + jnp.sum(p, axis=-1, keepdims=True)
            acc = alpha * acc + _dot(p.astype(BF16), v_ref[keys, :])
            return m_new, l, acc

        init = (jnp.full((tq, 1), NEG_INF, F32), jnp.zeros((tq, 1), F32), jnp.zeros((tq, V_HEAD), F32))
        carry = lax.fori_loop(0, i // 2, lambda pb, cr: step(2 * pb, cr, False, 2), init)
        carry = lax.fori_loop(0, i % 2, lambda _, cr: step(i - 1, cr, False), carry)
        m, l, acc = step(i, carry, True)
        o_ref[...] = (acc / l).astype(BF16)
        lse_ref[0] = m + jnp.log2(l)

    return pl.pallas_call(
        body, name="attn_fwd", grid=(MLA_HEADS, t // tq),
        in_specs=[pl.BlockSpec((tq, HEAD_PAD), lambda h, i: (i, h)),
                  pl.BlockSpec((t, HEAD_PAD), lambda h, i: (0, h)),
                  pl.BlockSpec((t, V_HEAD), lambda h, i: (0, h))],
        out_specs=[pl.BlockSpec((tq, V_HEAD), lambda h, i: (i, h)),
                   pl.BlockSpec((1, tq, 1), lambda h, i: (h, i, 0))],
        out_shape=[jax.ShapeDtypeStruct((t, MLA_WIDTH), BF16), jax.ShapeDtypeStruct((MLA_HEADS, t, 1), F32)],
        compiler_params=_params(("arbitrary", "arbitrary")),
    )(q, k, v)


def attention_backward(q, k, v, do, lse, delta, rider=None):
    t = q.shape[0]
    tq = _tile(t, ATTN_TILE, CHUNK)
    nq = t // tq

    def body(q_ref, k_ref, v_ref, do_ref, lse_ref, delta_ref, dq_ref, dk_ref, dv_ref, dq_acc):
        kb = pl.program_id(1)

        @pl.when(kb == 0)
        def _():
            dq_acc[...] = jnp.zeros_like(dq_acc)

        kv, vv = k_ref[...], v_ref[...]

        def step(qb, carry, masked):
            dk, dv = carry
            rows = pl.ds(pl.multiple_of(qb * tq, tq), tq)
            qv, dov = q_ref[rows, :], do_ref[rows, :]
            s = _dot(kv, qv, NT)
            if masked:
                s = jnp.where(_chunk_mask(s.shape, 1), s, NEG_INF)
            p = jnp.exp2(s - lse_ref[0, qb])
            dv = dv + _dot(p.astype(BF16), dov)
            dp = _dot(vv, dov, NT)
            ds = (p * (dp - delta_ref[0, qb]) * LN_2).astype(BF16)
            dk = dk + _dot(ds, qv)
            dq_acc[rows, :] += _dot(ds, kv, TN)
            return dk, dv

        carry = step(kb, (jnp.zeros((tq, HEAD_PAD), F32), jnp.zeros((tq, V_HEAD), F32)), True)
        odd = (nq - 1 - kb) % 2
        carry = lax.fori_loop(0, odd, lambda _, cr: step(kb + 1, cr, False), carry)
        first = kb + 1 + odd
        dk, dv = lax.fori_loop(0, (nq - first) // 2,
                               lambda pb, cr: step(first + 2 * pb + 1, step(first + 2 * pb, cr, False), False), carry)
        dk_ref[...] = dk.astype(BF16)
        dv_ref[...] = dv.astype(BF16)

        @pl.when(kb == nq - 1)
        def _():
            dq_ref[...] = dq_acc[...].astype(BF16)

    stat = pl.BlockSpec((1, nq, 1, tq), lambda h, j: (h, 0, 0, 0))
    return _call_with_rider(
        body, rider, name="attn_bwd", grid=(MLA_HEADS, nq),
        in_specs=[pl.BlockSpec((t, HEAD_PAD), lambda h, j: (0, h)),
                  pl.BlockSpec((tq, HEAD_PAD), lambda h, j: (j, h)),
                  pl.BlockSpec((tq, V_HEAD), lambda h, j: (j, h)),
                  pl.BlockSpec((t, V_HEAD), lambda h, j: (0, h)), stat, stat],
        out_specs=[pl.BlockSpec((t, HEAD_PAD), lambda h, j: (0, h)),
                   pl.BlockSpec((tq, HEAD_PAD), lambda h, j: (j, h)),
                   pl.BlockSpec((tq, V_HEAD), lambda h, j: (j, h))],
        out_shape=[jax.ShapeDtypeStruct((t, QK_COLS), BF16), jax.ShapeDtypeStruct((t, QK_COLS), BF16),
                   jax.ShapeDtypeStruct((t, MLA_WIDTH), BF16)],
        scratch_shapes=[pltpu.VMEM((t, HEAD_PAD), F32)], operands=(q, k, v, do, lse, delta))


HALO = 16


def _halo_spec(tm, n, step, last):
    return pl.BlockSpec((HALO, n), lambda i: (jnp.clip(i * (tm // HALO) + step, 0, last), 0))


def _shift_rows(v, prev, n):
    out = pltpu.roll(v, n, 0)
    row = lax.broadcasted_iota(jnp.int32, v.shape, 0)
    for r in range(n):
        out = jnp.where(row == r, prev[HALO - n + r:HALO - n + r + 1, :], out)
    return out


def _advance_rows(v, nxt, n):
    rows = v.shape[0]
    out = pltpu.roll(v, rows - n, 0)
    row = lax.broadcasted_iota(jnp.int32, v.shape, 0)
    for r in range(n):
        out = jnp.where(row == rows - n + r, nxt[r:r + 1, :], out)
    return out


def _conv_taps(zc, zc_prev, first):
    w = CONV_WIDTH
    u = zc[:, w:2 * w] * zc[:, 2 * w:]
    up = jnp.where(first, 0.0, zc_prev[:, w:2 * w] * zc_prev[:, 2 * w:])
    return u, _shift_rows(u, up, 1), _shift_rows(u, up, 2)


def mix_out_forward(zc, o, conv_w, og, gmat_a, gmat_b, w_out, x, gate):
    t, d = x.shape
    tm = _tile(t, ROW_TILE, 16)
    w = CONV_WIDTH

    def body(zc_ref, zp_ref, o_ref, cw_ref, og_ref, ga_ref, gb_ref, w_ref, x_ref, gate_ref,
             xo_ref, yn_ref, y_ref, ya_ref):
        zc_v = zc_ref[...].astype(F32)
        u, u1, u2 = _conv_taps(zc_v, zp_ref[...].astype(F32), pl.program_id(0) == 0)
        cw = cw_ref[...]
        ya = zc_v[:, :w] * (cw[0:1] * u2 + cw[1:2] * u1 + cw[2:3] * u)
        ya_ref[...] = ya.astype(BF16)
        ov = o_ref[...].astype(F32)
        ogv = og_ref[...]
        yn_ref[:, :w] = (ya * lax.rsqrt(_group_mean(ya * ya, ga_ref[...]) + EPS) * ogv[:, :w]).astype(BF16)
        yn_ref[:, w:] = (ov * lax.rsqrt(_group_mean(ov * ov, gb_ref[...]) + EPS) * ogv[:, w:]).astype(BF16)
        y = _dot(yn_ref[...], w_ref[...])
        y_ref[...] = y.astype(BF16)
        xo_ref[...] = x_ref[...] + gate_ref[...] * y

    def rows(n):
        return pl.BlockSpec((tm, n), lambda i: (i, 0))

    return pl.pallas_call(
        body, name="mix_out_fwd", grid=(t // tm,),
        in_specs=[rows(ZC_COLS), _halo_spec(tm, ZC_COLS, -1, t // HALO - 1), rows(MLA_WIDTH), _row(conv_w), _row(og),
                  _row(gmat_a), _row(gmat_b), _row(w_out), rows(d), _row(gate)],
        out_specs=[rows(d), rows(MIX_WIDTH), rows(d), rows(w)],
        out_shape=[jax.ShapeDtypeStruct((t, d), F32), jax.ShapeDtypeStruct((t, MIX_WIDTH), BF16),
                   jax.ShapeDtypeStruct((t, d), BF16), jax.ShapeDtypeStruct((t, w), BF16)],
        compiler_params=_params(("arbitrary",)),
    )(zc, zc, o, conv_w, og, gmat_a, gmat_b, w_out, x, gate)


def _group_norm_bwd(dyn, y, og, gmat):
    rs = lax.rsqrt(_group_mean(y * y, gmat) + EPS)
    yhat = y * rs
    d_og = jnp.sum(dyn * yhat, axis=0, keepdims=True)
    dyh = dyn * og
    return rs * (dyh - yhat * _group_mean(dyh * yhat, gmat)), d_og


def mix_out_backward(dxo, y, gate, ya, o, og, gmat_a, gmat_b, w_out, rider=None):
    t, d = dxo.shape
    tm = _tile(t, ROW_TILE, 16)
    w = CONV_WIDTH

    def body(dxo_ref, y_ref, gate_ref, ya_ref, o_ref, og_ref, ga_ref, gb_ref, w_ref,
             dy_ref, dya_ref, do_ref, delta_ref, sd_ref, so_ref):
        @pl.when(pl.program_id(0) == 0)
        def _():
            sd_ref[...] = jnp.zeros_like(sd_ref)
            so_ref[...] = jnp.zeros_like(so_ref)

        dxo_v = dxo_ref[...]
        dy = (gate_ref[...] * dxo_v).astype(BF16)
        dy_ref[...] = dy
        sd_ref[0:1, :] += jnp.sum(dxo_v * y_ref[...].astype(F32), axis=0, keepdims=True)
        dyn = _dot(dy, w_ref[...], NT)
        ogv = og_ref[...]
        ov = o_ref[...].astype(F32)
        dya, d_og_a = _group_norm_bwd(dyn[:, :w], ya_ref[...].astype(F32), ogv[:, :w], ga_ref[...])
        dov, d_og_b = _group_norm_bwd(dyn[:, w:], ov, ogv[:, w:], gb_ref[...])
        dya_ref[...] = dya.astype(BF16)
        do_ref[...] = dov.astype(BF16)
        so_ref[0:1, :w] += d_og_a
        so_ref[0:1, w:] += d_og_b
        prod = dov * ov
        for h in range(MLA_HEADS):
            delta_ref[h] = jnp.sum(prod[:, h * V_HEAD:(h + 1) * V_HEAD], axis=-1, keepdims=True)

    def rows(n):
        return pl.BlockSpec((tm, n), lambda i: (i, 0))

    return _call_with_rider(
        body, rider, name="mix_out_bwd", grid=(t // tm,),
        in_specs=[rows(d), rows(d), _row(gate), rows(w), rows(MLA_WIDTH), _row(og), _row(gmat_a), _row(gmat_b),
                  _row(w_out)],
        out_specs=[rows(d), rows(w), rows(MLA_WIDTH), pl.BlockSpec((MLA_HEADS, tm, 1), lambda i: (0, i, 0)),
                   pl.BlockSpec((8, d), lambda i: (0, 0)), pl.BlockSpec((8, MIX_WIDTH), lambda i: (0, 0))],
        out_shape=[jax.ShapeDtypeStruct((t, d), BF16), jax.ShapeDtypeStruct((t, w), BF16),
                   jax.ShapeDtypeStruct((t, MLA_WIDTH), BF16), jax.ShapeDtypeStruct((MLA_HEADS, t, 1), F32),
                   jax.ShapeDtypeStruct((8, d), F32), jax.ShapeDtypeStruct((8, MIX_WIDTH), F32)],
        scratch_shapes=[], operands=(dxo, y, gate, ya, o, og, gmat_a, gmat_b, w_out))


def conv_backward(zc, dya, conv_w):
    t = zc.shape[0]
    tm = _tile(t, ROW_TILE, 16)
    nt = t // tm
    w = CONV_WIDTH

    def body(zc_ref, zp_ref, zn_ref, dya_ref, dn_ref, cw_ref, dzc_ref, sums_ref):
        i = pl.program_id(0)

        @pl.when(i == 0)
        def _():
            sums_ref[...] = jnp.zeros_like(sums_ref)

        zc_v = zc_ref[...].astype(F32)
        u, u1, u2 = _conv_taps(zc_v, zp_ref[...].astype(F32), i == 0)
        cw = cw_ref[...]
        dya_v = dya_ref[...].astype(F32)
        dyc = dya_v * zc_v[:, :w]
        dyc_next = jnp.where(i == nt - 1, 0.0, dn_ref[...].astype(F32) * zn_ref[:, :w].astype(F32))
        du = cw[2:3] * dyc + cw[1:2] * _advance_rows(dyc, dyc_next, 1) + cw[0:1] * _advance_rows(dyc, dyc_next, 2)
        dzc_ref[:, :w] = (dya_v * (cw[0:1] * u2 + cw[1:2] * u1 + cw[2:3] * u)).astype(BF16)
        dzc_ref[:, w:2 * w] = (du * zc_v[:, 2 * w:]).astype(BF16)
        dzc_ref[:, 2 * w:] = (du * zc_v[:, w:2 * w]).astype(BF16)
        _add_rows(sums_ref, [jnp.sum(dyc * tap, axis=0, keepdims=True) for tap in (u2, u1, u)])

    def rows(n):
        return pl.BlockSpec((tm, n), lambda i: (i, 0))

    def halo(n, step):
        return _halo_spec(tm, n, step, t // HALO - 1)

    return pl.pallas_call(
        body, name="conv_bwd", grid=(nt,),
        in_specs=[rows(ZC_COLS), halo(ZC_COLS, -1), halo(ZC_COLS, tm // HALO), rows(w), halo(w, tm // HALO),
                  _row(conv_w)],
        out_specs=[rows(ZC_COLS), pl.BlockSpec((8, w), lambda i: (0, 0))],
        out_shape=[jax.ShapeDtypeStruct((t, ZC_COLS), BF16), jax.ShapeDtypeStruct((8, w), F32)],
        compiler_params=_params(("arbitrary",)),
    )(zc, zc, zc, dya, dya, conv_w)


def _rms_bwd(dy, x, g):
    xhat, r = _rms(x)
    d_g = jnp.sum(dy * xhat, axis=0, keepdims=True)
    dxh = dy * g
    return r * (dxh - xhat * jnp.mean(dxh * xhat, axis=-1, keepdims=True)), d_g


def mla_project_backward(dq, dk, dv, zm, pos, inv_freq, qg, kvg, w_uq, w_ukv):
    t = zm.shape[0]
    tm = _tile(t, ROW_TILE, 16)

    def body(dq_ref, dk_ref, dv_ref, zm_ref, pos_ref, if_ref, qg_ref, kvg_ref, wq_ref, wkv_ref,
             dql_ref, dkvl_ref, dzm_ref, sums_ref):
        @pl.when(pl.program_id(0) == 0)
        def _():
            sums_ref[...] = jnp.zeros_like(sums_ref)

        tables = _rope_tables(pos_ref[...], if_ref[...])
        dkr = jnp.zeros((tm, LANES), F32)
        for h in range(MLA_HEADS):
            lo = h * HEAD_PAD
            dql_ref[:, lo:lo + QK_NOPE] = (dq_ref[:, lo:lo + QK_NOPE].astype(F32) * QK_FOLD).astype(BF16)
            dql_ref[:, lo + QK_NOPE:lo + HEAD_PAD] = _rope_transposed(
                dq_ref[:, lo + QK_NOPE:lo + HEAD_PAD].astype(F32) * QK_FOLD, tables).astype(BF16)
            dkvl_ref[:, h * QK_NOPE:(h + 1) * QK_NOPE] = dk_ref[:, lo:lo + QK_NOPE]
            dkr = dkr + dk_ref[:, lo + QK_NOPE:lo + HEAD_PAD].astype(F32)
        dkvl_ref[:, MLA_HEADS * QK_NOPE:] = dv_ref[...]
        zv = zm_ref[...].astype(F32)
        dqn = _dot(dql_ref[...], wq_ref[...])
        dkvn = _dot(dkvl_ref[...], wkv_ref[...])
        dcq, d_qg = _rms_bwd(dqn, zv[:, :Q_LORA], qg_ref[...])
        dckv, d_kvg = _rms_bwd(dkvn, zv[:, Q_LORA:Q_LORA + KV_LORA], kvg_ref[...])
        dzm_ref[:, :Q_LORA] = dcq.astype(BF16)
        dzm_ref[:, Q_LORA:Q_LORA + KV_LORA] = dckv.astype(BF16)
        dzm_ref[:, Q_LORA + KV_LORA:] = _rope_transposed(dkr, tables).astype(BF16)
        sums_ref[0:1, :Q_LORA] += d_qg
        sums_ref[0:1, Q_LORA:Q_LORA + KV_LORA] += d_kvg

    def rows(n):
        return pl.BlockSpec((tm, n), lambda i: (i, 0))

    return pl.pallas_call(
        body, name="mla_project_bwd", grid=(t // tm,),
        in_specs=[rows(QK_COLS), rows(QK_COLS), rows(MLA_WIDTH), rows(ZM_COLS), rows(1), _row(inv_freq),
                  _row(qg), _row(kvg), _row(w_uq), _row(w_ukv)],
        out_specs=[rows(QK_COLS), rows(QK_COLS), rows(ZM_COLS), pl.BlockSpec((8, ZM_COLS), lambda i: (0, 0))],
        out_shape=[jax.ShapeDtypeStruct((t, QK_COLS), BF16), jax.ShapeDtypeStruct((t, QK_COLS), BF16),
                   jax.ShapeDtypeStruct((t, ZM_COLS), BF16), jax.ShapeDtypeStruct((8, ZM_COLS), F32)],
        compiler_params=_params(("arbitrary",)),
    )(dq, dk, dv, zm, pos, inv_freq, qg, kvg, w_uq, w_ukv)


def mix_in_backward(dzc, dzm, w_in, x, dxo, gn, sc, gate, rider=None):
    t, d = x.shape
    tm = _tile(t, ROW_TILE, 16)

    def body(dzc_ref, dzm_ref, w_ref, x_ref, dxo_ref, gn_ref, sc_ref, gate_ref, dx_ref, dy_ref, sums_ref):
        @pl.when(pl.program_id(0) == 0)
        def _():
            sums_ref[...] = jnp.zeros_like(sums_ref)

        dh = _dot(dzc_ref[...], w_ref[:ZC_COLS, :]) + _dot(dzm_ref[...], w_ref[ZC_COLS:, :])
        dx, d_sh, d_sc, d_gn = _norm_mod_bwd(dh, x_ref[...], gn_ref[...], sc_ref[...])
        dx = dxo_ref[...] + dx
        dx_ref[...] = dx
        dy_ref[...] = (0.5 * gate_ref[...] * dx).astype(BF16)
        _add_rows(sums_ref, [d_sh, d_sc, d_gn])

    def rows(n):
        return pl.BlockSpec((tm, n), lambda i: (i, 0))

    return _call_with_rider(
        body, rider, name="mix_in_bwd", grid=(t // tm,),
        in_specs=[rows(ZC_COLS), rows(ZM_COLS), _row(w_in), rows(d), rows(d), _row(gn), _row(sc), _row(gate)],
        out_specs=[rows(d), rows(d), pl.BlockSpec((8, d), lambda i: (0, 0))],
        out_shape=[jax.ShapeDtypeStruct((t, d), F32), jax.ShapeDtypeStruct((t, d), BF16),
                   jax.ShapeDtypeStruct((8, d), F32)],
        scratch_shapes=[], operands=(dzc, dzm, w_in, x, dxo, gn, sc, gate))


def final_loss(x, target, g, gate):
    t, d = x.shape
    tm = _tile(t, ROW_TILE, 16)

    def body(x_ref, t_ref, g_ref, gate_ref, dx_ref, dy_ref, sums_ref):
        @pl.when(pl.program_id(0) == 0)
        def _():
            sums_ref[...] = jnp.zeros_like(sums_ref)

        gv = g_ref[...]
        xhat, r = _rms(x_ref[...])
        err = xhat * gv - t_ref[...]
        dyf = err * (1.0 / d)
        dxh = dyf * gv
        dx = r * (dxh - xhat * jnp.mean(dxh * xhat, axis=-1, keepdims=True))
        dx_ref[...] = dx
        dy_ref[...] = (0.5 * gate_ref[...] * dx).astype(BF16)
        _add_rows(sums_ref, [jnp.sum(dyf * xhat, axis=0, keepdims=True),
                             jnp.sum(err * err, axis=0, keepdims=True) * (0.5 / d)])

    row = pl.BlockSpec((tm, d), lambda i: (i, 0))
    return pl.pallas_call(
        body, name="final_loss", grid=(t // tm,),
        in_specs=[row, row, _row(g), _row(gate)],
        out_specs=[row, row, pl.BlockSpec((8, d), lambda i: (0, 0))],
        out_shape=[jax.ShapeDtypeStruct((t, d), F32), jax.ShapeDtypeStruct((t, d), BF16),
                   jax.ShapeDtypeStruct((8, d), F32)],
        compiler_params=_params(("arbitrary",)),
    )(x, target, g, gate)


def adamw(w, g, m, v, name):
    r, n = w.shape
    tr = _tile(r, max(8, (1 << 19) // n), 8)

    def body(w_ref, g_ref, m_ref, v_ref, d_ref, mo_ref, vo_ref):
        gv = g_ref[...]
        m_new = ADAM_B1 * m_ref[...] + (1.0 - ADAM_B1) * gv
        v_new = ADAM_B2 * v_ref[...] + (1.0 - ADAM_B2) * (gv * gv)
        m_hat = m_new / (1.0 - ADAM_B1 ** ADAM_STEP)
        v_hat = v_new / (1.0 - ADAM_B2 ** ADAM_STEP)
        d_ref[...] = -ADAM_LR * (m_hat / (jnp.sqrt(v_hat) + ADAM_EPS) + ADAM_WD * w_ref[...])
        mo_ref[...] = m_new
        vo_ref[...] = v_new

    blk = pl.BlockSpec((tr, n), lambda i: (i, 0))
    shape = jax.ShapeDtypeStruct((r, n), F32)
    return pl.pallas_call(
        body, name=name, grid=(r // tr,), in_specs=[blk] * 4, out_specs=[blk] * 3, out_shape=[shape] * 3,
        compiler_params=_params(("arbitrary",)),
    )(w, g, m, v)


def _pad_to(v, n):
    return jnp.pad(v, (0, n - v.shape[0]))


def _pad_heads(w, axis_len):
    n = w.shape[1]
    return jnp.pad(w.reshape(MLA_HEADS, axis_len, n), ((0, 0), (0, HEAD_PAD - axis_len), (0, 0))).reshape(-1, n)


def _swap_head_parts(w, inner, outer):
    n = w.shape[1]
    return w.reshape(outer, inner, QK_NOPE, n).transpose(1, 0, 2, 3).reshape(-1, n)


def kernel(x, c, positions, ada_w, ada_b, norm_ffn1_g, ffn1_w1, ffn1_w3, ffn1_w2, norm_mix_g, w_in, conv_w, q_norm_g, w_uq, kv_norm_g, w_ukv, out_norm_g, w_out, norm_ffn2_g, ffn2_w1, ffn2_w3, ffn2_w2, final_norm_g, loss_target, m_ada_w, m_ada_b, m_norm_ffn1_g, m_ffn1_w1, m_ffn1_w3, m_ffn1_w2, m_norm_mix_g, m_w_in, m_conv_w, m_q_norm_g, m_w_uq, m_kv_norm_g, m_w_ukv, m_out_norm_g, m_w_out, m_norm_ffn2_g, m_ffn2_w1, m_ffn2_w3, m_ffn2_w2, m_final_norm_g, v_ada_w, v_ada_b, v_norm_ffn1_g, v_ffn1_w1, v_ffn1_w3, v_ffn1_w2, v_norm_mix_g, v_w_in, v_conv_w, v_q_norm_g, v_w_uq, v_kv_norm_g, v_w_ukv, v_out_norm_g, v_w_out, v_norm_ffn2_g, v_ffn2_w1, v_ffn2_w3, v_ffn2_w2, v_final_norm_g):
    t, d = x.shape[1], x.shape[2]
    f = ffn1_w2.shape[1] * N_DEV
    me = 4 * lax.axis_index("x") + 2 * lax.axis_index("y") + lax.axis_index("c")
    my_c = lax.axis_index("c")
    my_chip = 2 * lax.axis_index("x") + lax.axis_index("y")
    xs = x[0]
    n_ada = ada_w.shape[2]
    cw_n = conv_w.shape[2]

    c_rows = jnp.broadcast_to(c, (8, d))
    conv_rows = jnp.pad(conv_w[0], ((0, 8 - CONV_K), (0, LANES - cw_n)))
    ffn1_blocks = jnp.stack([ffn1_w1[0].T, ffn1_w3[0].T, ffn1_w2[0]]).astype(BF16)
    ffn2_blocks = jnp.stack([ffn2_w1[0].T, ffn2_w3[0].T, ffn2_w2[0]]).astype(BF16)
    c_all, conv_all, ffn1_all = all_gather([c_rows, conv_rows, ffn1_blocks], [0, 0, 1], "gather_first")
    c_all = c_all[:, 0, :]
    conv_full8 = conv_all[:, :, :cw_n].transpose(1, 0, 2).reshape(8, CONV_WIDTH)
    ffn1_ws = ffn1_all.reshape(3, f, d)
    gather_rest = riding_gather(
        [ffn2_blocks, w_in[0].T.astype(BF16), w_uq[0].T.astype(BF16), w_ukv[0].T.astype(BF16), w_out[0].astype(BF16)],
        [1, 0, 0, 0, 0])

    ada_b_cols = lax.dynamic_slice_in_dim(ada_b, me * n_ada, n_ada, axis=1)
    mod_cols = ada_forward(c_all, ada_w[0], ada_b_cols)
    mod_all, = all_gather([mod_cols], [0], "gather_mod")
    mod = lax.dynamic_index_in_dim(mod_all, me, axis=1, keepdims=False).reshape(N_MOD, 1, d)
    sh1, sc1, g1, sh2, sc2, g2, sh3, sc3, g3 = [mod[i] for i in range(N_MOD)]

    gf = final_norm_g.reshape(1, d)
    x1, h1, a1, b1, y1, *gathered = ffn_forward(xs, norm_ffn1_g, sc1, sh1, g1, ffn1_ws, 0, "ffn1_fwd", gather_rest)
    ffn2_ws = gathered[0].reshape(3, f, d)
    w_in_p = jnp.pad(gathered[1].reshape(IN_COLS, d), ((0, ZC_COLS + ZM_COLS - IN_COLS), (0, 0)))
    w_uq_p = _pad_heads(gathered[2].reshape(-1, Q_LORA), QK_NOPE + QK_ROPE)
    w_ukv_p = _swap_head_parts(gathered[3].reshape(-1, KV_LORA), 2, MLA_HEADS)
    w_out_f = gathered[4].reshape(MIX_WIDTH, d)
    h2, zc, zm = mix_in_forward(x1, norm_mix_g, sc2, sh2, w_in_p)
    pos = positions[0].astype(F32).reshape(t, 1)
    inv_freq = ROPE_THETA ** (-jnp.arange(0, QK_ROPE, 2, dtype=F32) / QK_ROPE)
    inv_freq = jnp.concatenate([inv_freq, inv_freq, jnp.zeros((LANES - QK_ROPE,), F32)]).reshape(1, LANES)
    qn, kvn, q, k, v = mla_project(zm, pos, inv_freq, q_norm_g, kv_norm_g, w_uq_p, w_ukv_p)
    o, lse = attention_forward(q, k, v)
    lane = jnp.arange(CONV_WIDTH)
    gmat_a = (lane[:, None] // (CONV_WIDTH // CONV_GROUPS) == lane[None, :] // (CONV_WIDTH // CONV_GROUPS))
    gmat_a = (gmat_a / (CONV_WIDTH // CONV_GROUPS)).astype(BF16)
    gmat_b = ((lane[:, None] // V_HEAD == lane[None, :] // V_HEAD) / V_HEAD).astype(BF16)
    x2, yn, y2, ya = mix_out_forward(zc, o, conv_full8, out_norm_g, gmat_a, gmat_b, w_out_f, x1, g2)
    x3, h3, a3, b3, y3 = ffn_forward(x2, norm_ffn2_g, sc3, sh3, g3, ffn2_ws, 0, "ffn2_fwd")
    dx3, dy3, sums_f = final_loss(x3, loss_target[0], gf, g3)

    chip_idx = jnp.bitwise_xor(my_chip, jnp.array([0, 2, 1, 3], jnp.int32)).astype(jnp.int32)
    src_idx = (2 * chip_idx + my_c).astype(jnp.int32)

    def row_blocks(named):
        return [g.reshape(N_DEV, g.shape[0] // N_DEV, g.shape[1]) for _, g in named]

    def chip_sums(named, g8, got):
        return [add_sibling(g, r, src_idx, chip_idx, "rs_add_" + n) for g, r, (n, _) in zip(g8, got, named)]

    da3, db3, u3 = ffn_backward_gate(dy3, a3, b3, ffn2_ws, 0, "ffn2_bwd_gate")
    dx2, sums_3 = ffn_backward_norm(da3, db3, dx3, x2, y3, norm_ffn2_g, sc3, ffn2_ws, 0, "ffn2_bwd_norm")
    ffn2_named = [("ffn2_w1", matmul_tn(da3, h3, "ffn2_gw1")), ("ffn2_w3", matmul_tn(db3, h3, "ffn2_gw3")),
                  ("ffn2_w2", matmul_tn(u3, dy3, "ffn2_gw2"))]
    ffn2_g8 = row_blocks(ffn2_named)
    dy2, dya, do, delta, sums_2d, sums_2o, *ffn2_sib = mix_out_backward(
        dx2, y2, g2, ya, o, out_norm_g, gmat_a, gmat_b, w_out_f, riding_sibling(ffn2_g8))
    ffn2_sums = chip_sums(ffn2_named, ffn2_g8, ffn2_sib)
    g_w_out = matmul_tn(yn, dy2, "gw_out")
    nq = t // _tile(t, ATTN_TILE, CHUNK)
    stat_shape = (MLA_HEADS, nq, 1, t // nq)
    dq, dk, dv, *ffn2_got = attention_backward(q, k, v, do, lse.reshape(stat_shape), delta.reshape(stat_shape),
                                               riding_exchange([s[1] for s in ffn2_sums]))
    dzc, sums_c = conv_backward(zc, dya, conv_full8)
    dql, dkvl, dzm, sums_m = mla_project_backward(dq, dk, dv, zm, pos, inv_freq, q_norm_g, kv_norm_g, w_uq_p, w_ukv_p)
    g_w_uq_p = matmul_tn(dql, qn, "gw_uq")
    g_w_ukv_p = matmul_tn(dkvl, kvn, "gw_ukv")
    g_w_in = matmul_tn([dzc, dzm], h2, "gw_in")[:IN_COLS]
    g_w_uq = g_w_uq_p.reshape(MLA_HEADS, HEAD_PAD, Q_LORA)[:, :QK_NOPE + QK_ROPE].reshape(-1, Q_LORA)
    g_w_ukv = _swap_head_parts(g_w_ukv_p, MLA_HEADS, 2)
    mix_named = [("w_in", g_w_in), ("w_uq", g_w_uq), ("w_ukv", g_w_ukv), ("w_out", g_w_out)]
    mix_g8 = row_blocks(mix_named)
    dx1, dy1, sums_1m, *mix_sib = mix_in_backward(dzc, dzm, w_in_p, x1, dx2, norm_mix_g, sc2, g1, riding_sibling(mix_g8))
    mix_sums = chip_sums(mix_named, mix_g8, mix_sib)
    da1, db1, u1, *mix_got = ffn_backward_gate(dy1, a1, b1, ffn1_ws, 0, "ffn1_bwd_gate",
                                               riding_exchange([s[1] for s in mix_sums]))
    ffn1_pair = [("ffn1_w1", matmul_tn(da1, h1, "ffn1_gw1")), ("ffn1_w3", matmul_tn(db1, h1, "ffn1_gw3"))]
    pair_g8 = row_blocks(ffn1_pair)
    g_w2a, *pair_sib = matmul_tn(u1, dy1, "ffn1_gw2", riding_sibling(pair_g8))
    ffn1_last = [("ffn1_w2", g_w2a)]
    last_g8 = row_blocks(ffn1_last)
    ffn1_named = ffn1_pair + ffn1_last
    ffn1_sums = chip_sums(ffn1_pair, pair_g8, pair_sib) + chip_sums(
        ffn1_last, last_g8, exchange_sibling(last_g8, "rs_sibling_ffn1_w2"))
    dx0, sums_1, *ffn1_got = ffn_backward_norm(da1, db1, dx1, xs, y1, norm_ffn1_g, sc1, ffn1_ws, 0, "ffn1_bwd_norm",
                                               riding_exchange([s[1] for s in ffn1_sums]))
    transposed = {"ffn1_w1", "ffn1_w3", "ffn2_w1", "ffn2_w3", "w_in", "w_uq", "w_ukv"}
    g_sh = {}
    for named, group_sums, group_got in ((ffn2_named, ffn2_sums, ffn2_got), (mix_named, mix_sums, mix_got),
                                         (ffn1_named, ffn1_sums, ffn1_got)):
        for (n, _), (own, _), got in zip(named, group_sums, group_got):
            g_rows = add_received(own, got, "rs_sum_" + n)
            g_sh[n] = g_rows.T if n in transposed else g_rows

    dmod = jnp.concatenate([sums_1[0], sums_1[1], sums_1[2], sums_1m[0], sums_1m[1], sums_2d[0],
                            sums_3[0], sums_3[1], sums_3[2]])
    pieces = [dmod, sums_1[3], sums_1m[2], sums_m[0, :Q_LORA], sums_m[0, Q_LORA:Q_LORA + KV_LORA], sums_2o[0],
              sums_3[3], sums_f[0], sums_f[1], sums_c[:CONV_K].reshape(-1)]
    plens = [p.shape[0] for p in pieces]
    poffs = [sum(plens[:i]) for i in range(len(plens))]
    vec_len = -(-sum(plens) // 1024) * 1024
    vec = _pad_to(jnp.concatenate(pieces), vec_len).reshape(-1, LANES)
    vec_all, = all_gather([vec], [0], "gather_sums")
    tot = sum_devices(vec_all).reshape(-1)
    g_ada_b, g_n1, g_nmix, g_qg, g_kvg, g_og, g_n3, g_gf, loss_lanes, g_conv_full = [
        tot[o:o + n] for o, n in zip(poffs, plens)]
    loss = sum_lanes(loss_lanes.reshape(1, d))[0, 0]
    g_conv = lax.dynamic_slice_in_dim(g_conv_full.reshape(CONV_K, CONV_WIDTH), me * cw_n, cw_n, axis=1)
    dmod_all = vec_all.reshape(N_DEV, vec_len)[:, :N_MOD * d]
    dmod_cols = lax.dynamic_slice_in_dim(dmod_all, me * n_ada, n_ada, axis=1)
    g_ada_w = ada_backward(jnp.pad(c_all, ((0, 8), (0, 0))), jnp.pad(dmod_cols, ((0, 8), (0, 0))))

    def update(name, w, g, m, v):
        shape = w.shape
        two_d = (-1, shape[-1])
        dlt, nm, nv = adamw(w.reshape(two_d), g.reshape(two_d), m.reshape(two_d), v.reshape(two_d), "adamw_" + name)
        return g.reshape(shape), dlt.reshape(shape), nm.reshape(shape), nv.reshape(shape)

    res = {}
    res["ada_w"] = update("ada_w", ada_w, g_ada_w, m_ada_w, v_ada_w)
    big = [("ffn1_w1", ffn1_w1, m_ffn1_w1, v_ffn1_w1), ("ffn1_w3", ffn1_w3, m_ffn1_w3, v_ffn1_w3),
           ("ffn2_w1", ffn2_w1, m_ffn2_w1, v_ffn2_w1), ("ffn2_w3", ffn2_w3, m_ffn2_w3, v_ffn2_w3),
           ("w_in", w_in, m_w_in, v_w_in), ("w_uq", w_uq, m_w_uq, v_w_uq), ("w_ukv", w_ukv, m_w_ukv, v_w_ukv),
           ("ffn1_w2", ffn1_w2, m_ffn1_w2, v_ffn1_w2), ("ffn2_w2", ffn2_w2, m_ffn2_w2, v_ffn2_w2),
           ("w_out", w_out, m_w_out, v_w_out)]
    for name, w, m, v in big:
        res[name] = update(name, w, g_sh[name], m, v)
    smalls = [("ada_b", ada_b, g_ada_b, m_ada_b, v_ada_b),
              ("norm_ffn1_g", norm_ffn1_g, g_n1, m_norm_ffn1_g, v_norm_ffn1_g),
              ("norm_mix_g", norm_mix_g, g_nmix, m_norm_mix_g, v_norm_mix_g),
              ("conv_w", conv_w, g_conv, m_conv_w, v_conv_w),
              ("q_norm_g", q_norm_g, g_qg, m_q_norm_g, v_q_norm_g),
              ("kv_norm_g", kv_norm_g, g_kvg, m_kv_norm_g, v_kv_norm_g),
              ("out_norm_g", out_norm_g, g_og, m_out_norm_g, v_out_norm_g),
              ("norm_ffn2_g", norm_ffn2_g, g_n3, m_norm_ffn2_g, v_norm_ffn2_g),
              ("final_norm_g", final_norm_g, g_gf, m_final_norm_g, v_final_norm_g)]
    slens = [w.size for _, w, _, _, _ in smalls]
    soffs = [sum(slens[:i]) for i in range(len(slens))]
    s_len = -(-sum(slens) // 1024) * 1024

    def pack_small(i):
        return _pad_to(jnp.concatenate([s[i].reshape(-1) for s in smalls]), s_len).reshape(8, -1)

    s_out = adamw(pack_small(1), pack_small(2), pack_small(3), pack_small(4), "adamw_small")
    for (name, w, g, _, _), o, n in zip(smalls, soffs, slens):
        res[name] = (g.reshape(w.shape),) + tuple(a.reshape(-1)[o:o + n].reshape(w.shape) for a in s_out)

    order = ["ada_w", "ada_b", "norm_ffn1_g", "ffn1_w1", "ffn1_w3", "ffn1_w2", "norm_mix_g", "w_in", "conv_w",
             "q_norm_g", "w_uq", "kv_norm_g", "w_ukv", "out_norm_g", "w_out", "norm_ffn2_g", "ffn2_w1", "ffn2_w3",
             "ffn2_w2", "final_norm_g"]
    return (loss, dx0.reshape(x.shape), *[res[n][0] for n in order], *[res[n][1] for n in order],
            *[res[n][2] for n in order], *[res[n][3] for n in order])
```

```python
import functools
import math

import jax
import jax.numpy as jnp
from jax import lax
from jax.experimental import pallas as pl
from jax.experimental.pallas import tpu as pltpu

F32 = jnp.float32
BF16 = jnp.bfloat16
MESH_ID = pl.DeviceIdType.MESH
N_DEV = 8

EPS = 1e-6
CHUNK = 64
N_MOD = 9
CONV_WIDTH = 512
CONV_GROUPS = 8
CONV_K = 3
MLA_HEADS = 4
QK_NOPE = 128
QK_ROPE = 64
V_HEAD = 128
Q_LORA = 384
KV_LORA = 256
ROPE_THETA = 10000.0
MLA_WIDTH = MLA_HEADS * V_HEAD
MIX_WIDTH = CONV_WIDTH + MLA_WIDTH
IN_COLS = 3 * CONV_WIDTH + Q_LORA + KV_LORA + QK_ROPE
ZC_COLS = 3 * CONV_WIDTH
ZM_COLS = Q_LORA + KV_LORA + 128
HEAD_PAD = 256
QK_COLS = MLA_HEADS * HEAD_PAD
ATTN_SCALE = (QK_NOPE + QK_ROPE) ** -0.5
LOG2_E = 1.4426950408889634
LN_2 = 0.6931471805599453
QK_FOLD = ATTN_SCALE * LOG2_E
NEG_INF = -1e30

ADAM_LR = 0.001
ADAM_B1 = 0.9
ADAM_B2 = 0.999
ADAM_EPS = 1e-08
ADAM_WD = 0.01
ADAM_STEP = 10

LANES = 128
MXU_COLS = 256
VMEM_LIMIT = 56 * 1024 * 1024
ROW_TILE = 1024
FFN_FWD_TILE = (1024, 256)
FFN_BWD_TILE = (512, 1408)
GRAD_TILE = 1408
GRAD_DEPTH = 2048
SUM_ROWS = 256
ATTN_TILE = 1024

NN = (((1,), (0,)), ((), ()))
NT = (((1,), (1,)), ((), ()))
TN = (((0,), (0,)), ((), ()))


def _dot(a, b, dims=NN):
    return lax.dot_general(a, b, dims, preferred_element_type=F32)


def _tile(n, cap, mult=LANES):
    best = None
    for t in range(mult, min(n, cap) + 1, mult):
        if n % t == 0:
            best = t
    return n if best is None else best


def _params(sem=None):
    return pltpu.CompilerParams(dimension_semantics=sem, vmem_limit_bytes=VMEM_LIMIT)


def _row(v):
    return pl.BlockSpec(v.shape, lambda *_: (0,) * v.ndim)


def _sigmoid(x):
    return 0.5 * jnp.tanh(0.5 * x) + 0.5


def _rms(x):
    r = lax.rsqrt(jnp.mean(x * x, axis=-1, keepdims=True) + EPS)
    return x * r, r


def _norm_mod_bwd(dh, x, gn, sc):
    xhat, r = _rms(x)
    d_sh = jnp.sum(dh, axis=0, keepdims=True)
    d_sc = jnp.sum(dh * (xhat * gn), axis=0, keepdims=True)
    dxn = dh * (1.0 + sc)
    d_gn = jnp.sum(dxn * xhat, axis=0, keepdims=True)
    dxh = dxn * gn
    dx = r * (dxh - xhat * jnp.mean(dxh * xhat, axis=-1, keepdims=True))
    return dx, d_sh, d_sc, d_gn


def _group_mean(v, gmat):
    return _dot(v.astype(BF16), gmat)


def _add_rows(ref, rows):
    for r, v in enumerate(rows):
        ref[r:r + 1, :] += v


def _window(ref, axis, j):
    return ref.at[(slice(None),) * axis + (j,)]


def _any_specs(n):
    return [pl.BlockSpec(memory_space=pl.ANY)] * n


def all_gather(blocks, axes, name):
    n_arr = len(blocks)

    def body(*refs):
        start, forward, finish = _gather_steps(refs[:n_arr], refs[n_arr:2 * n_arr], axes, *refs[2 * n_arr:])
        start()
        for j in range(3):
            forward(j)
        finish()

    return pl.pallas_call(
        body, name=name, out_shape=_gathered_shapes(blocks, axes),
        in_specs=_any_specs(n_arr), out_specs=_any_specs(n_arr), scratch_shapes=_gather_sems(n_arr),
    )(*blocks)


def _gathered_shapes(blocks, axes):
    return [jax.ShapeDtypeStruct(b.shape[:ax] + (N_DEV,) + b.shape[ax:], b.dtype) for b, ax in zip(blocks, axes)]


def _gather_sems(n_arr):
    return [pltpu.SemaphoreType.DMA((7, n_arr)), pltpu.SemaphoreType.DMA((7, n_arr)), pltpu.SemaphoreType.DMA((n_arr,))]


def _gather_steps(ins, outs, axes, send_sems, recv_sems, local_sems):
    arrays = range(len(ins))
    x, y, c = lax.axis_index("x"), lax.axis_index("y"), lax.axis_index("c")
    me, sibling = (x, y, c), (x, y, 1 - c)
    chips = [(1 - x, y), (x, 1 - y), (1 - x, 1 - y)]

    def slot(a, px, py, pc):
        return _window(outs[a], axes[a], 4 * px + 2 * py + pc)

    def copy(a, k, block, to, src=None):
        return pltpu.make_async_remote_copy(
            src_ref=slot(a, *block) if src is None else src, dst_ref=slot(a, *block),
            send_sem=send_sems.at[k, a], recv_sem=recv_sems.at[k, a], device_id=to, device_id_type=MESH_ID)

    def mine(a):
        return pltpu.make_async_copy(ins[a], slot(a, *me), local_sems.at[a])

    def first():
        return ([copy(a, 0, me, sibling, src=ins[a]) for a in arrays]
                + [copy(a, 1 + j, me, (*chip, c), src=ins[a]) for j, chip in enumerate(chips) for a in arrays])

    def passed(j):
        return [copy(a, 4 + j, (*chips[j], c), sibling) for a in arrays]

    def start():
        for a in arrays:
            mine(a).start()
        for cp in first():
            cp.start()

    def forward(j):
        for a, cp in zip(arrays, passed(j)):
            copy(a, 1 + j, (*chips[j], c), me).wait_recv()
            cp.start()

    def finish():
        for a in arrays:
            copy(a, 0, sibling, me).wait_recv()
        for j, chip in enumerate(chips):
            for a in arrays:
                copy(a, 4 + j, (*chip, 1 - c), me).wait_recv()
        for cp in first() + passed(0) + passed(1) + passed(2):
            cp.wait_send()
        for a in arrays:
            mine(a).wait()

    return start, forward, finish


def exchange_sibling(grads, name):
    n_arr = len(grads)

    def body(*refs):
        start, finish = _sibling_exchange_steps(refs[:n_arr], refs[n_arr:2 * n_arr], *refs[2 * n_arr:])
        start()
        finish()

    return pl.pallas_call(
        body, name=name, out_shape=_sibling_shapes(grads),
        in_specs=_any_specs(n_arr), out_specs=_any_specs(n_arr), scratch_shapes=_exchange_sems(n_arr),
    )(*grads)


def _sibling_shapes(grads):
    return [jax.ShapeDtypeStruct((4,) + g.shape[1:], g.dtype) for g in grads]


def _exchange_sems(n_arr):
    return [pltpu.SemaphoreType.DMA((n_arr,)), pltpu.SemaphoreType.DMA((n_arr,))]


def _sibling_exchange_steps(ins, outs, send_sems, recv_sems):
    x, y, c = lax.axis_index("x"), lax.axis_index("y"), lax.axis_index("c")

    def copy(a, src, dst):
        return pltpu.make_async_remote_copy(
            src_ref=src, dst_ref=dst, send_sem=send_sems.at[a], recv_sem=recv_sems.at[a],
            device_id=(x, y, 1 - c), device_id_type=MESH_ID)

    def start():
        for a in range(len(ins)):
            for k in range(4):
                copy(a, ins[a].at[2 * k + (1 - c)], outs[a].at[k]).start()

    def finish():
        whole = [copy(a, ins[a].at[pl.ds(0, 4)], outs[a]) for a in range(len(ins))]
        for cp in whole:
            cp.wait_recv()
        for cp in whole:
            cp.wait_send()

    return start, finish


def _chip_exchange_steps(ins, outs, send_sems, recv_sems):
    x, y, c = lax.axis_index("x"), lax.axis_index("y"), lax.axis_index("c")
    chips = [(1 - x, y), (x, 1 - y), (1 - x, 1 - y)]

    def copy(a, src, dst, chip):
        return pltpu.make_async_remote_copy(
            src_ref=src, dst_ref=dst, send_sem=send_sems.at[a], recv_sem=recv_sems.at[a],
            device_id=(*chip, c), device_id_type=MESH_ID)

    def start():
        for a in range(len(ins)):
            for j, chip in enumerate(chips):
                copy(a, ins[a].at[j], outs[a].at[j], chip).start()

    def finish():
        whole = [copy(a, ins[a], outs[a], chips[0]) for a in range(len(ins))]
        for cp in whole:
            cp.wait_recv()
        for cp in whole:
            cp.wait_send()

    return start, finish


def riding_gather(blocks, axes):
    def phases(ins, outs, *sems):
        start, forward, finish = _gather_steps(ins, outs, axes, *sems)
        return [start] + [functools.partial(forward, j) for j in range(3)] + [finish]

    return dict(operands=blocks, out_shape=_gathered_shapes(blocks, axes), sems=_gather_sems(len(blocks)),
                phases=phases, when=("first", "late0", "late1", "late2", "last"))


def riding_exchange(parts):
    def phases(ins, outs, *sems):
        return list(_chip_exchange_steps(ins, outs, *sems))

    return dict(operands=parts, out_shape=[jax.ShapeDtypeStruct(p.shape, p.dtype) for p in parts],
                sems=_exchange_sems(len(parts)), phases=phases, when=("first", "last"))


def riding_sibling(grads):
    def phases(ins, outs, *sems):
        return list(_sibling_exchange_steps(ins, outs, *sems))

    return dict(operands=grads, out_shape=_sibling_shapes(grads), sems=_exchange_sems(len(grads)),
                phases=phases, when=("first", "last"))


def _call_with_rider(body, rider, *, name, grid, in_specs, out_specs, out_shape, scratch_shapes, operands):
    params = _params(("arbitrary",) * len(grid))
    if rider is None:
        return pl.pallas_call(body, name=name, grid=grid, in_specs=in_specs, out_specs=out_specs,
                              out_shape=out_shape, scratch_shapes=scratch_shapes, compiler_params=params)(*operands)
    n_in, n_out, n_scr, k = len(in_specs), len(out_specs), len(scratch_shapes), len(rider["operands"])
    at = {"first": (0,) * len(grid), "last": tuple(g - 1 for g in grid)}
    if "late0" in rider["when"]:
        rows, cols = grid
        assert cols >= 3
        at.update({"late%d" % j: (max(rows - 2, 0), j) for j in range(3)})

    def wrapped(*refs):
        ins, c_in = refs[:n_in], refs[n_in:n_in + k]
        outs, c_out = refs[n_in + k:n_in + k + n_out], refs[n_in + k + n_out:n_in + 2 * k + n_out]
        scratch, sems = refs[n_in + 2 * k + n_out:n_in + 2 * k + n_out + n_scr], refs[n_in + 2 * k + n_out + n_scr:]
        pos = [pl.program_id(axis) for axis in range(len(grid))]

        def here(key):
            return functools.reduce(jnp.logical_and, [p == v for p, v in zip(pos, at[key])])

        phases = rider["phases"](c_in, c_out, *sems)
        for fn, key in zip(phases, rider["when"]):
            if key != "last":
                pl.when(here(key))(fn)
        body(*ins, *outs, *scratch)
        pl.when(here("last"))(phases[-1])

    return pl.pallas_call(
        wrapped, name=name, grid=grid,
        in_specs=list(in_specs) + _any_specs(k), out_specs=list(out_specs) + _any_specs(k),
        out_shape=list(out_shape) + rider["out_shape"], scratch_shapes=list(scratch_shapes) + rider["sems"],
        compiler_params=params)(*operands, *rider["operands"])


def add_sibling(g8, got, src_idx, chip_idx, name):
    _, r, n = g8.shape
    tr = _tile(r, SUM_ROWS, 16)

    def body(si_ref, ci_ref, g0_ref, g1_ref, g2_ref, g3_ref, got_ref, own_ref, send_ref):
        own_ref[...] = g0_ref[0] + got_ref[ci_ref[0]]
        for j, g_ref in enumerate((g1_ref, g2_ref, g3_ref)):
            send_ref[j] = (g_ref[0] + got_ref[ci_ref[j + 1]]).astype(BF16)

    def mine(j):
        return pl.BlockSpec((1, tr, n), lambda i, si, ci: (si[j], i, 0))

    return pl.pallas_call(
        body, name=name,
        out_shape=[jax.ShapeDtypeStruct((r, n), F32), jax.ShapeDtypeStruct((3, r, n), BF16)],
        grid_spec=pltpu.PrefetchScalarGridSpec(
            num_scalar_prefetch=2, grid=(r // tr,),
            in_specs=[mine(0), mine(1), mine(2), mine(3), pl.BlockSpec((4, tr, n), lambda i, si, ci: (0, i, 0))],
            out_specs=[pl.BlockSpec((tr, n), lambda i, si, ci: (i, 0)),
                       pl.BlockSpec((3, tr, n), lambda i, si, ci: (0, i, 0))]),
        compiler_params=_params(("arbitrary",)),
    )(src_idx, chip_idx, g8, g8, g8, g8, got)


def add_received(own, got, name):
    r, n = own.shape
    tr = _tile(r, SUM_ROWS, 16)

    def body(a_ref, b_ref, o_ref):
        acc = a_ref[...]
        for j in range(3):
            acc = acc + b_ref[j].astype(F32)
        o_ref[...] = acc

    return pl.pallas_call(
        body, name=name,
        out_shape=jax.ShapeDtypeStruct((r, n), F32),
        grid=(r // tr,),
        in_specs=[pl.BlockSpec((tr, n), lambda i: (i, 0)), pl.BlockSpec((3, tr, n), lambda i: (0, i, 0))],
        out_specs=pl.BlockSpec((tr, n), lambda i: (i, 0)),
        compiler_params=_params(("arbitrary",)),
    )(own, got)


def sum_devices(g):
    def body(g_ref, o_ref):
        acc = g_ref[0]
        for j in range(1, N_DEV):
            acc = acc + g_ref[j]
        o_ref[...] = acc

    return pl.pallas_call(body, name="sum_devices", out_shape=jax.ShapeDtypeStruct(g.shape[1:], F32))(g)


def sum_lanes(v):
    def body(v_ref, o_ref):
        o_ref[...] = jnp.broadcast_to(jnp.sum(v_ref[...], axis=-1, keepdims=True), (1, LANES))

    return pl.pallas_call(body, name="sum_lanes", out_shape=jax.ShapeDtypeStruct((1, LANES), F32))(v)


def ada_forward(c_all, ada_w, ada_b_cols):
    nb, n = c_all.shape[0], ada_w.shape[1]

    def body(c_ref, w_ref, b_ref, o_ref):
        cv = c_ref[...]
        s = (cv * jax.nn.sigmoid(cv)).astype(BF16)
        o_ref[...] = _dot(s, w_ref[...].astype(BF16)) + b_ref[...]

    return pl.pallas_call(body, name="ada_fwd", out_shape=jax.ShapeDtypeStruct((nb, n), F32),
                          compiler_params=_params())(c_all, ada_w, ada_b_cols)


def ada_backward(c_all16, dmod16):
    d, n = c_all16.shape[1], dmod16.shape[1]

    def body(c_ref, g_ref, o_ref):
        cv = c_ref[...]
        s = (cv * jax.nn.sigmoid(cv)).astype(BF16)
        o_ref[...] = _dot(s, g_ref[...].astype(BF16), TN)

    return pl.pallas_call(body, name="ada_bwd", out_shape=jax.ShapeDtypeStruct((d, n), F32),
                          compiler_params=_params())(c_all16, dmod16)


def ffn_forward(x, gn, sc, sh, gate, ws, first, name, rider=None):
    t, d = x.shape
    f = ws.shape[1]
    tm, tf = _tile(t, FFN_FWD_TILE[0], 16), _tile(f, FFN_FWD_TILE[1])
    nf = f // tf

    def body(x_ref, gn_ref, sc_ref, sh_ref, gate_ref, w1_ref, w3_ref, w2_ref,
             xo_ref, h_ref, a_ref, b_ref, y_ref, hs, acc):
        j = pl.program_id(1)

        @pl.when(j == 0)
        def _():
            xhat, _ = _rms(x_ref[...])
            h = (xhat * gn_ref[...] * (1.0 + sc_ref[...]) + sh_ref[...]).astype(BF16)
            hs[...] = h
            h_ref[...] = h
            acc[...] = jnp.zeros_like(acc)

        h = hs[...]
        a = _dot(h, w1_ref[...], NT)
        b = _dot(h, w3_ref[...], NT)
        a_ref[...] = a.astype(BF16)
        b_ref[...] = b.astype(BF16)
        u = (a * _sigmoid(a) * b).astype(BF16)
        acc[...] += _dot(u, w2_ref[...])

        @pl.when(j == nf - 1)
        def _():
            y = acc[...]
            y_ref[...] = y.astype(BF16)
            xo_ref[...] = x_ref[...] + 0.5 * gate_ref[...] * y

    row = pl.BlockSpec((tm, d), lambda i, j: (i, 0))
    vec = pl.BlockSpec((1, d), lambda i, j: (0, 0))
    wide = pl.BlockSpec((tm, tf), lambda i, j: (i, j))
    return _call_with_rider(
        body, rider, name=name, grid=(t // tm, nf),
        in_specs=[row, vec, vec, vec, vec] + _ffn_weight_specs(first, tf, d),
        out_specs=[row, row, wide, wide, row],
        out_shape=[jax.ShapeDtypeStruct((t, d), F32), jax.ShapeDtypeStruct((t, d), BF16),
                   jax.ShapeDtypeStruct((t, f), BF16), jax.ShapeDtypeStruct((t, f), BF16),
                   jax.ShapeDtypeStruct((t, d), BF16)],
        scratch_shapes=[pltpu.VMEM((tm, d), BF16), pltpu.VMEM((tm, d), F32)],
        operands=(x, gn, sc, sh, gate, ws, ws, ws))


def _ffn_weight_specs(first, tf, d):
    return [pl.BlockSpec((None, tf, d), lambda i, j, w=first + k: (w, j, 0)) for k in range(3)]


def ffn_backward_gate(dy, a, b, ws, first, name, rider=None):
    t, d = dy.shape
    f = ws.shape[1]
    tm, tf = _tile(t, FFN_BWD_TILE[0], 16), _tile(f, FFN_BWD_TILE[1])
    nf = f // tf

    def gate_body(dy_ref, a_ref, b_ref, w2_ref, da_ref, db_ref, u_ref):
        du = _dot(dy_ref[...], w2_ref[...], NT)
        av = a_ref[...].astype(F32)
        bv = b_ref[...].astype(F32)
        s = _sigmoid(av)
        sa = av * s
        da_ref[...] = (du * bv * (s + sa * (1.0 - s))).astype(BF16)
        db_ref[...] = (du * sa).astype(BF16)
        u_ref[...] = (sa * bv).astype(BF16)

    hidden = jax.ShapeDtypeStruct((t, f), BF16)
    wide_t = pl.BlockSpec((tm, tf), lambda j, i: (i, j))
    return _call_with_rider(
        gate_body, rider, name=name, grid=(nf, t // tm),
        in_specs=[pl.BlockSpec((tm, d), lambda j, i: (i, 0)), wide_t, wide_t,
                  pl.BlockSpec((None, tf, d), lambda j, i: (first + 2, j, 0))],
        out_specs=[wide_t, wide_t, wide_t], out_shape=[hidden, hidden, hidden],
        scratch_shapes=[], operands=(dy, a, b, ws))


def ffn_backward_norm(da, db, dxo, x, y, gn, sc, ws, first, name, rider=None):
    t, d = x.shape
    f = ws.shape[1]
    tm, tf = _tile(t, FFN_BWD_TILE[0], 16), _tile(f, FFN_BWD_TILE[1])
    nf = f // tf
    row = pl.BlockSpec((tm, d), lambda i, j: (i, 0))
    vec = pl.BlockSpec((1, d), lambda i, j: (0, 0))
    wide = pl.BlockSpec((tm, tf), lambda i, j: (i, j))

    def norm_body(da_ref, db_ref, w1_ref, w3_ref, dxo_ref, x_ref, y_ref, gn_ref, sc_ref, dx_ref, sums_ref, acc):
        i, j = pl.program_id(0), pl.program_id(1)

        @pl.when(jnp.logical_and(i == 0, j == 0))
        def _():
            sums_ref[...] = jnp.zeros_like(sums_ref)

        part = _dot(da_ref[...], w1_ref[...]) + _dot(db_ref[...], w3_ref[...])

        @pl.when(j == 0)
        def _():
            acc[...] = part

        @pl.when(jnp.logical_and(j > 0, j < nf - 1))
        def _():
            acc[...] += part

        @pl.when(j == nf - 1)
        def _():
            dh = part if nf == 1 else acc[...] + part
            dxo_v = dxo_ref[...]
            dx, d_sh, d_sc, d_gn = _norm_mod_bwd(dh, x_ref[...], gn_ref[...], sc_ref[...])
            dx_ref[...] = dxo_v + dx
            d_gate = jnp.sum(dxo_v * (0.5 * y_ref[...].astype(F32)), axis=0, keepdims=True)
            _add_rows(sums_ref, [d_sh, d_sc, d_gate, d_gn])

    w1_spec, w3_spec, _ = _ffn_weight_specs(first, tf, d)
    return _call_with_rider(
        norm_body, rider, name=name, grid=(t // tm, nf),
        in_specs=[wide, wide, w1_spec, w3_spec, row, row, row, vec, vec],
        out_specs=[row, pl.BlockSpec((8, d), lambda i, j: (0, 0))],
        out_shape=[jax.ShapeDtypeStruct((t, d), F32), jax.ShapeDtypeStruct((8, d), F32)],
        scratch_shapes=[pltpu.VMEM((tm, d), F32)],
        operands=(da, db, ws, ws, dxo, x, y, gn, sc))


def matmul_tn(a, b, name, rider=None):
    parts = list(a) if isinstance(a, (list, tuple)) else [a]
    t, n = b.shape
    widths = [p.shape[1] for p in parts]
    tm = _tile(functools.reduce(math.gcd, widths), GRAD_TILE)
    tn, tk = _tile(n, GRAD_TILE), _tile(t, GRAD_DEPTH, 16)
    nk = t // tk
    counts = [w // tm for w in widths]
    firsts = [sum(counts[:p]) for p in range(len(parts))]

    def body(*refs):
        a_refs, (b_ref, o_ref, acc) = refs[:len(parts)], refs[len(parts):]
        i, k = pl.program_id(0), pl.program_id(2)

        @pl.when(k == 0)
        def _():
            acc[...] = jnp.zeros_like(acc)

        for a_ref, lo, cnt in zip(a_refs, firsts, counts):
            def accumulate(a_ref=a_ref):
                acc[...] += _dot(a_ref[...], b_ref[...], TN)

            if len(parts) == 1:
                accumulate()
            else:
                pl.when(jnp.logical_and(i >= lo, i < lo + cnt))(accumulate)

        @pl.when(k == nk - 1)
        def _():
            o_ref[...] = acc[...]

    def part_spec(lo, cnt):
        if len(parts) == 1:
            return pl.BlockSpec((tk, tm), lambda i, j, k: (k, i))

        def index(i, j, k):
            mine = jnp.logical_and(i >= lo, i < lo + cnt)
            return jnp.where(mine, k, 0), jnp.clip(i - lo, 0, cnt - 1)
        return pl.BlockSpec((tk, tm), index)

    out = _call_with_rider(
        body, rider, name=name, grid=(sum(counts), n // tn, nk),
        in_specs=[part_spec(lo, cnt) for lo, cnt in zip(firsts, counts)]
        + [pl.BlockSpec((tk, tn), lambda i, j, k: (k, j))],
        out_specs=[pl.BlockSpec((tm, tn), lambda i, j, k: (i, j))],
        out_shape=[jax.ShapeDtypeStruct((sum(widths), n), F32)],
        scratch_shapes=[pltpu.VMEM((tm, tn), F32)], operands=(*parts, b))
    return out[0] if rider is None else out


def mix_in_forward(x, gn, sc, sh, w_in):
    t, d = x.shape
    tm = _tile(t, ROW_TILE, 16)

    def body(x_ref, gn_ref, sc_ref, sh_ref, w_ref, h_ref, zc_ref, zm_ref):
        xhat, _ = _rms(x_ref[...])
        h = (xhat * gn_ref[...] * (1.0 + sc_ref[...]) + sh_ref[...]).astype(BF16)
        h_ref[...] = h
        z = _dot(h, w_ref[...], NT)
        zc_ref[...] = z[:, :ZC_COLS].astype(BF16)
        zm_ref[...] = z[:, ZC_COLS:].astype(BF16)

    row = pl.BlockSpec((tm, d), lambda i: (i, 0))
    vec = pl.BlockSpec((1, d), lambda i: (0, 0))
    return pl.pallas_call(
        body, name="mix_in_fwd", grid=(t // tm,),
        in_specs=[row, vec, vec, vec, _row(w_in)],
        out_specs=[row, pl.BlockSpec((tm, ZC_COLS), lambda i: (i, 0)), pl.BlockSpec((tm, ZM_COLS), lambda i: (i, 0))],
        out_shape=[jax.ShapeDtypeStruct((t, d), BF16), jax.ShapeDtypeStruct((t, ZC_COLS), BF16),
                   jax.ShapeDtypeStruct((t, ZM_COLS), BF16)],
        compiler_params=_params(("arbitrary",)),
    )(x, gn, sc, sh, w_in)


def _rope_tables(pos, inv_freq):
    ang = pos * inv_freq
    lane = lax.broadcasted_iota(jnp.int32, ang.shape, 1)
    cos, sin = jnp.cos(ang), jnp.sin(ang)
    half = QK_ROPE // 2
    return cos, jnp.where(lane < half, -sin, 0.0), jnp.where(jnp.logical_and(lane >= half, lane < QK_ROPE), sin, 0.0)


def _rope(v, tables):
    cos, sin_a, sin_b = tables
    return v * cos + pltpu.roll(v, LANES - QK_ROPE // 2, 1) * sin_a + pltpu.roll(v, QK_ROPE // 2, 1) * sin_b


def _rope_transposed(dv, tables):
    cos, sin_a, sin_b = tables
    return dv * cos + pltpu.roll(dv * sin_a, QK_ROPE // 2, 1) + pltpu.roll(dv * sin_b, LANES - QK_ROPE // 2, 1)


def mla_project(zm, pos, inv_freq, qg, kvg, w_uq, w_ukv):
    t = zm.shape[0]
    tm = _tile(t, ROW_TILE, 16)

    def body(zm_ref, pos_ref, if_ref, qg_ref, kvg_ref, wq_ref, wkv_ref, qn_ref, kvn_ref, q_ref, k_ref, v_ref):
        zv = zm_ref[...].astype(F32)
        qn = (_rms(zv[:, :Q_LORA])[0] * qg_ref[...]).astype(BF16)
        kvn = (_rms(zv[:, Q_LORA:Q_LORA + KV_LORA])[0] * kvg_ref[...]).astype(BF16)
        qn_ref[...] = qn
        kvn_ref[...] = kvn
        qf = _dot(qn, wq_ref[...], NT) * QK_FOLD
        kvf = _dot(kvn, wkv_ref[...], NT)
        tables = _rope_tables(pos_ref[...], if_ref[...])
        kr = _rope(zv[:, Q_LORA + KV_LORA:], tables).astype(BF16)
        for h in range(MLA_HEADS):
            lo = h * HEAD_PAD
            q_ref[:, lo:lo + QK_NOPE] = qf[:, lo:lo + QK_NOPE].astype(BF16)
            q_ref[:, lo + QK_NOPE:lo + HEAD_PAD] = _rope(qf[:, lo + QK_NOPE:lo + HEAD_PAD], tables).astype(BF16)
            k_ref[:, lo:lo + QK_NOPE] = kvf[:, h * QK_NOPE:(h + 1) * QK_NOPE].astype(BF16)
            k_ref[:, lo + QK_NOPE:lo + HEAD_PAD] = kr
        v_ref[...] = kvf[:, MLA_HEADS * QK_NOPE:].astype(BF16)

    def rows(n):
        return pl.BlockSpec((tm, n), lambda i: (i, 0))

    return pl.pallas_call(
        body, name="mla_project", grid=(t // tm,),
        in_specs=[rows(ZM_COLS), rows(1), _row(inv_freq), _row(qg), _row(kvg), _row(w_uq), _row(w_ukv)],
        out_specs=[rows(Q_LORA), rows(KV_LORA), rows(QK_COLS), rows(QK_COLS), rows(MLA_WIDTH)],
        out_shape=[jax.ShapeDtypeStruct((t, Q_LORA), BF16), jax.ShapeDtypeStruct((t, KV_LORA), BF16),
                   jax.ShapeDtypeStruct((t, QK_COLS), BF16), jax.ShapeDtypeStruct((t, QK_COLS), BF16),
                   jax.ShapeDtypeStruct((t, MLA_WIDTH), BF16)],
        compiler_params=_params(("arbitrary",)),
    )(zm, pos, inv_freq, qg, kvg, w_uq, w_ukv)


def _chunk_mask(shape, q_axis):
    qi = lax.broadcasted_iota(jnp.int32, shape, q_axis) // CHUNK
    ki = lax.broadcasted_iota(jnp.int32, shape, 1 - q_axis) // CHUNK
    return ki <= qi


def attention_forward(q, k, v, rider=None):
    t = q.shape[0]
    tq = _tile(t, ATTN_TILE, CHUNK)

    def body(q_ref, k_ref, v_ref, o_ref, lse_ref):
        i = pl.program_id(1)
        qv = q_ref[...]

        def step(kb, carry, masked, tiles=1):
            m, l, acc = carry
            keys = pl.ds(pl.multiple_of(kb * tq, tq), tiles * tq)
            s = _dot(qv, k_ref[keys, :], NT)
            if masked:
                s = jnp.where(_chunk_mask(s.shape, 0), s, NEG_INF)
            m_new = jnp.maximum(m, jnp.max(s, axis=-1, keepdims=True))
            alpha = jnp.exp2(m - m_new)
            p = jnp.exp2(s - m_new)
            l = alpha * l + jnp.sum(p, axis=-1, keepdims=True)
            acc = alpha * acc + _dot(p.astype(BF16), v_ref[keys, :])
            return m_new, l, acc

        init = (jnp.full((tq, 1), NEG_INF, F32), jnp.zeros((tq, 1), F32), jnp.zeros((tq, V_HEAD), F32))
        carry = lax.fori_loop(0, i // 2, lambda pb, cr: step(2 * pb, cr, False, 2), init)
        carry = lax.fori_loop(0, i % 2, lambda _, cr: step(i - 1, cr, False), carry)
        m, l, acc = step(i, carry, True)
        o_ref[...] = (acc / l).astype(BF16)
        lse_ref[0] = m + jnp.log2(l)

    return _call_with_rider(
        body, rider, name="attn_fwd", grid=(MLA_HEADS, t // tq),
        in_specs=[pl.BlockSpec((tq, HEAD_PAD), lambda h, i: (i, h)),
                  pl.BlockSpec((t, HEAD_PAD), lambda h, i: (0, h)),
                  pl.BlockSpec((t, V_HEAD), lambda h, i: (0, h))],
        out_specs=[pl.BlockSpec((tq, V_HEAD), lambda h, i: (i, h)),
                   pl.BlockSpec((1, tq, 1), lambda h, i: (h, i, 0))],
        out_shape=[jax.ShapeDtypeStruct((t, MLA_WIDTH), BF16), jax.ShapeDtypeStruct((MLA_HEADS, t, 1), F32)],
        scratch_shapes=[], operands=(q, k, v))


def attention_backward(q, k, v, do, lse, delta, rider=None):
    t = q.shape[0]
    tq = _tile(t, ATTN_TILE, CHUNK)
    nq = t // tq

    def body(q_ref, k_ref, v_ref, do_ref, lse_ref, delta_ref, dq_ref, dk_ref, dv_ref, dq_acc):
        kb = pl.program_id(1)

        @pl.when(kb == 0)
        def _():
            dq_acc[...] = jnp.zeros_like(dq_acc)

        kv, vv = k_ref[...], v_ref[...]

        def step(qb, carry, masked):
            dk, dv = carry
            rows = pl.ds(pl.multiple_of(qb * tq, tq), tq)
            qv, dov = q_ref[rows, :], do_ref[rows, :]
            s = _dot(kv, qv, NT)
            if masked:
                s = jnp.where(_chunk_mask(s.shape, 1), s, NEG_INF)
            p = jnp.exp2(s - lse_ref[0, qb])
            dv = dv + _dot(p.astype(BF16), dov)
            dp = _dot(vv, dov, NT)
            ds = (p * (dp - delta_ref[0, qb]) * LN_2).astype(BF16)
            dk = dk + _dot(ds, qv)
            dq_acc[rows, :] += _dot(ds, kv, TN)
            return dk, dv

        carry = step(kb, (jnp.zeros((tq, HEAD_PAD), F32), jnp.zeros((tq, V_HEAD), F32)), True)
        odd = (nq - 1 - kb) % 2
        carry = lax.fori_loop(0, odd, lambda _, cr: step(kb + 1, cr, False), carry)
        first = kb + 1 + odd
        dk, dv = lax.fori_loop(0, (nq - first) // 2,
                               lambda pb, cr: step(first + 2 * pb + 1, step(first + 2 * pb, cr, False), False), carry)
        dk_ref[...] = dk.astype(BF16)
        dv_ref[...] = dv.astype(BF16)

        @pl.when(kb == nq - 1)
        def _():
            dq_ref[...] = dq_acc[...].astype(BF16)

    stat = pl.BlockSpec((1, nq, 1, tq), lambda h, j: (h, 0, 0, 0))
    return _call_with_rider(
        body, rider, name="attn_bwd", grid=(MLA_HEADS, nq),
        in_specs=[pl.BlockSpec((t, HEAD_PAD), lambda h, j: (0, h)),
                  pl.BlockSpec((tq, HEAD_PAD), lambda h, j: (j, h)),
                  pl.BlockSpec((tq, V_HEAD), lambda h, j: (j, h)),
                  pl.BlockSpec((t, V_HEAD), lambda h, j: (0, h)), stat, stat],
        out_specs=[pl.BlockSpec((t, HEAD_PAD), lambda h, j: (0, h)),
                   pl.BlockSpec((tq, HEAD_PAD), lambda h, j: (j, h)),
                   pl.BlockSpec((tq, V_HEAD), lambda h, j: (j, h))],
        out_shape=[jax.ShapeDtypeStruct((t, QK_COLS), BF16), jax.ShapeDtypeStruct((t, QK_COLS), BF16),
                   jax.ShapeDtypeStruct((t, MLA_WIDTH), BF16)],
        scratch_shapes=[pltpu.VMEM((t, HEAD_PAD), F32)], operands=(q, k, v, do, lse, delta))


HALO = 16


def _halo_spec(tm, n, step, last):
    return pl.BlockSpec((HALO, n), lambda i: (jnp.clip(i * (tm // HALO) + step, 0, last), 0))


def _shift_rows(v, prev, n):
    out = pltpu.roll(v, n, 0)
    row = lax.broadcasted_iota(jnp.int32, v.shape, 0)
    for r in range(n):
        out = jnp.where(row == r, prev[HALO - n + r:HALO - n + r + 1, :], out)
    return out


def _advance_rows(v, nxt, n):
    rows = v.shape[0]
    out = pltpu.roll(v, rows - n, 0)
    row = lax.broadcasted_iota(jnp.int32, v.shape, 0)
    for r in range(n):
        out = jnp.where(row == rows - n + r, nxt[r:r + 1, :], out)
    return out


def _conv_taps(zc, zc_prev, first):
    w = CONV_WIDTH
    u = zc[:, w:2 * w] * zc[:, 2 * w:]
    up = jnp.where(first, 0.0, zc_prev[:, w:2 * w] * zc_prev[:, 2 * w:])
    return u, _shift_rows(u, up, 1), _shift_rows(u, up, 2)


def mix_out_forward(zc, o, conv_w, og, gmat_a, gmat_b, w_out, x, gate):
    t, d = x.shape
    tm = _tile(t, ROW_TILE, 16)
    w = CONV_WIDTH

    def body(zc_ref, zp_ref, o_ref, cw_ref, og_ref, ga_ref, gb_ref, w_ref, x_ref, gate_ref,
             xo_ref, yn_ref, y_ref, ya_ref):
        zc_v = zc_ref[...].astype(F32)
        u, u1, u2 = _conv_taps(zc_v, zp_ref[...].astype(F32), pl.program_id(0) == 0)
        cw = cw_ref[...]
        ya = zc_v[:, :w] * (cw[0:1] * u2 + cw[1:2] * u1 + cw[2:3] * u)
        ya_ref[...] = ya.astype(BF16)
        ov = o_ref[...].astype(F32)
        ogv = og_ref[...]
        yn_ref[:, :w] = (ya * lax.rsqrt(_group_mean(ya * ya, ga_ref[...]) + EPS) * ogv[:, :w]).astype(BF16)
        yn_ref[:, w:] = (ov * lax.rsqrt(_group_mean(ov * ov, gb_ref[...]) + EPS) * ogv[:, w:]).astype(BF16)
        y = _dot(yn_ref[...], w_ref[...])
        y_ref[...] = y.astype(BF16)
        xo_ref[...] = x_ref[...] + gate_ref[...] * y

    def rows(n):
        return pl.BlockSpec((tm, n), lambda i: (i, 0))

    return pl.pallas_call(
        body, name="mix_out_fwd", grid=(t // tm,),
        in_specs=[rows(ZC_COLS), _halo_spec(tm, ZC_COLS, -1, t // HALO - 1), rows(MLA_WIDTH), _row(conv_w), _row(og),
                  _row(gmat_a), _row(gmat_b), _row(w_out), rows(d), _row(gate)],
        out_specs=[rows(d), rows(MIX_WIDTH), rows(d), rows(w)],
        out_shape=[jax.ShapeDtypeStruct((t, d), F32), jax.ShapeDtypeStruct((t, MIX_WIDTH), BF16),
                   jax.ShapeDtypeStruct((t, d), BF16), jax.ShapeDtypeStruct((t, w), BF16)],
        compiler_params=_params(("arbitrary",)),
    )(zc, zc, o, conv_w, og, gmat_a, gmat_b, w_out, x, gate)


def _group_norm_bwd(dyn, y, og, gmat):
    rs = lax.rsqrt(_group_mean(y * y, gmat) + EPS)
    yhat = y * rs
    d_og = jnp.sum(dyn * yhat, axis=0, keepdims=True)
    dyh = dyn * og
    return rs * (dyh - yhat * _group_mean(dyh * yhat, gmat)), d_og


def mix_out_backward(dxo, y, gate, ya, o, og, gmat_a, gmat_b, w_out, rider=None):
    t, d = dxo.shape
    tm = _tile(t, ROW_TILE, 16)
    w = CONV_WIDTH

    def body(dxo_ref, y_ref, gate_ref, ya_ref, o_ref, og_ref, ga_ref, gb_ref, w_ref,
             dy_ref, dya_ref, do_ref, delta_ref, sd_ref, so_ref):
        @pl.when(pl.program_id(0) == 0)
        def _():
            sd_ref[...] = jnp.zeros_like(sd_ref)
            so_ref[...] = jnp.zeros_like(so_ref)

        dxo_v = dxo_ref[...]
        dy = (gate_ref[...] * dxo_v).astype(BF16)
        dy_ref[...] = dy
        sd_ref[0:1, :] += jnp.sum(dxo_v * y_ref[...].astype(F32), axis=0, keepdims=True)
        dyn = _dot(dy, w_ref[...], NT)
        ogv = og_ref[...]
        ov = o_ref[...].astype(F32)
        dya, d_og_a = _group_norm_bwd(dyn[:, :w], ya_ref[...].astype(F32), ogv[:, :w], ga_ref[...])
        dov, d_og_b = _group_norm_bwd(dyn[:, w:], ov, ogv[:, w:], gb_ref[...])
        dya_ref[...] = dya.astype(BF16)
        do_ref[...] = dov.astype(BF16)
        so_ref[0:1, :w] += d_og_a
        so_ref[0:1, w:] += d_og_b
        prod = dov * ov
        for h in range(MLA_HEADS):
            delta_ref[h] = jnp.sum(prod[:, h * V_HEAD:(h + 1) * V_HEAD], axis=-1, keepdims=True)

    def rows(n):
        return pl.BlockSpec((tm, n), lambda i: (i, 0))

    return _call_with_rider(
        body, rider, name="mix_out_bwd", grid=(t // tm,),
        in_specs=[rows(d), rows(d), _row(gate), rows(w), rows(MLA_WIDTH), _row(og), _row(gmat_a), _row(gmat_b),
                  _row(w_out)],
        out_specs=[rows(d), rows(w), rows(MLA_WIDTH), pl.BlockSpec((MLA_HEADS, tm, 1), lambda i: (0, i, 0)),
                   pl.BlockSpec((8, d), lambda i: (0, 0)), pl.BlockSpec((8, MIX_WIDTH), lambda i: (0, 0))],
        out_shape=[jax.ShapeDtypeStruct((t, d), BF16), jax.ShapeDtypeStruct((t, w), BF16),
                   jax.ShapeDtypeStruct((t, MLA_WIDTH), BF16), jax.ShapeDtypeStruct((MLA_HEADS, t, 1), F32),
                   jax.ShapeDtypeStruct((8, d), F32), jax.ShapeDtypeStruct((8, MIX_WIDTH), F32)],
        scratch_shapes=[], operands=(dxo, y, gate, ya, o, og, gmat_a, gmat_b, w_out))


def conv_backward(zc, dya, conv_w):
    t = zc.shape[0]
    tm = _tile(t, ROW_TILE, 16)
    nt = t // tm
    w = CONV_WIDTH

    def body(zc_ref, zp_ref, zn_ref, dya_ref, dn_ref, cw_ref, dzc_ref, sums_ref):
        i = pl.program_id(0)

        @pl.when(i == 0)
        def _():
            sums_ref[...] = jnp.zeros_like(sums_ref)

        zc_v = zc_ref[...].astype(F32)
        u, u1, u2 = _conv_taps(zc_v, zp_ref[...].astype(F32), i == 0)
        cw = cw_ref[...]
        dya_v = dya_ref[...].astype(F32)
        dyc = dya_v * zc_v[:, :w]
        dyc_next = jnp.where(i == nt - 1, 0.0, dn_ref[...].astype(F32) * zn_ref[:, :w].astype(F32))
        du = cw[2:3] * dyc + cw[1:2] * _advance_rows(dyc, dyc_next, 1) + cw[0:1] * _advance_rows(dyc, dyc_next, 2)
        dzc_ref[:, :w] = (dya_v * (cw[0:1] * u2 + cw[1:2] * u1 + cw[2:3] * u)).astype(BF16)
        dzc_ref[:, w:2 * w] = (du * zc_v[:, 2 * w:]).astype(BF16)
        dzc_ref[:, 2 * w:] = (du * zc_v[:, w:2 * w]).astype(BF16)
        _add_rows(sums_ref, [jnp.sum(dyc * tap, axis=0, keepdims=True) for tap in (u2, u1, u)])

    def rows(n):
        return pl.BlockSpec((tm, n), lambda i: (i, 0))

    def halo(n, step):
        return _halo_spec(tm, n, step, t // HALO - 1)

    return pl.pallas_call(
        body, name="conv_bwd", grid=(nt,),
        in_specs=[rows(ZC_COLS), halo(ZC_COLS, -1), halo(ZC_COLS, tm // HALO), rows(w), halo(w, tm // HALO),
                  _row(conv_w)],
        out_specs=[rows(ZC_COLS), pl.BlockSpec((8, w), lambda i: (0, 0))],
        out_shape=[jax.ShapeDtypeStruct((t, ZC_COLS), BF16), jax.ShapeDtypeStruct((8, w), F32)],
        compiler_params=_params(("arbitrary",)),
    )(zc, zc, zc, dya, dya, conv_w)


def _rms_bwd(dy, x, g):
    xhat, r = _rms(x)
    d_g = jnp.sum(dy * xhat, axis=0, keepdims=True)
    dxh = dy * g
    return r * (dxh - xhat * jnp.mean(dxh * xhat, axis=-1, keepdims=True)), d_g


def mla_project_backward(dq, dk, dv, zm, pos, inv_freq, qg, kvg, w_uq, w_ukv):
    t = zm.shape[0]
    tm = _tile(t, ROW_TILE, 16)

    def body(dq_ref, dk_ref, dv_ref, zm_ref, pos_ref, if_ref, qg_ref, kvg_ref, wq_ref, wkv_ref,
             dql_ref, dkvl_ref, dzm_ref, sums_ref):
        @pl.when(pl.program_id(0) == 0)
        def _():
            sums_ref[...] = jnp.zeros_like(sums_ref)

        tables = _rope_tables(pos_ref[...], if_ref[...])
        dkr = jnp.zeros((tm, LANES), F32)
        for h in range(MLA_HEADS):
            lo = h * HEAD_PAD
            dql_ref[:, lo:lo + QK_NOPE] = (dq_ref[:, lo:lo + QK_NOPE].astype(F32) * QK_FOLD).astype(BF16)
            dql_ref[:, lo + QK_NOPE:lo + HEAD_PAD] = _rope_transposed(
                dq_ref[:, lo + QK_NOPE:lo + HEAD_PAD].astype(F32) * QK_FOLD, tables).astype(BF16)
            dkvl_ref[:, h * QK_NOPE:(h + 1) * QK_NOPE] = dk_ref[:, lo:lo + QK_NOPE]
            dkr = dkr + dk_ref[:, lo + QK_NOPE:lo + HEAD_PAD].astype(F32)
        dkvl_ref[:, MLA_HEADS * QK_NOPE:] = dv_ref[...]
        zv = zm_ref[...].astype(F32)
        dqn = _dot(dql_ref[...], wq_ref[...])
        dkvn = _dot(dkvl_ref[...], wkv_ref[...])
        dcq, d_qg = _rms_bwd(dqn, zv[:, :Q_LORA], qg_ref[...])
        dckv, d_kvg = _rms_bwd(dkvn, zv[:, Q_LORA:Q_LORA + KV_LORA], kvg_ref[...])
        dzm_ref[:, :Q_LORA] = dcq.astype(BF16)
        dzm_ref[:, Q_LORA:Q_LORA + KV_LORA] = dckv.astype(BF16)
        dzm_ref[:, Q_LORA + KV_LORA:] = _rope_transposed(dkr, tables).astype(BF16)
        sums_ref[0:1, :Q_LORA] += d_qg
        sums_ref[0:1, Q_LORA:Q_LORA + KV_LORA] += d_kvg

    def rows(n):
        return pl.BlockSpec((tm, n), lambda i: (i, 0))

    return pl.pallas_call(
        body, name="mla_project_bwd", grid=(t // tm,),
        in_specs=[rows(QK_COLS), rows(QK_COLS), rows(MLA_WIDTH), rows(ZM_COLS), rows(1), _row(inv_freq),
                  _row(qg), _row(kvg), _row(w_uq), _row(w_ukv)],
        out_specs=[rows(QK_COLS), rows(QK_COLS), rows(ZM_COLS), pl.BlockSpec((8, ZM_COLS), lambda i: (0, 0))],
        out_shape=[jax.ShapeDtypeStruct((t, QK_COLS), BF16), jax.ShapeDtypeStruct((t, QK_COLS), BF16),
                   jax.ShapeDtypeStruct((t, ZM_COLS), BF16), jax.ShapeDtypeStruct((8, ZM_COLS), F32)],
        compiler_params=_params(("arbitrary",)),
    )(dq, dk, dv, zm, pos, inv_freq, qg, kvg, w_uq, w_ukv)


def mix_in_backward(dzc, dzm, w_in, x, dxo, gn, sc, gate, rider=None):
    t, d = x.shape
    tm = _tile(t, ROW_TILE, 16)

    def body(dzc_ref, dzm_ref, w_ref, x_ref, dxo_ref, gn_ref, sc_ref, gate_ref, dx_ref, dy_ref, sums_ref):
        @pl.when(pl.program_id(0) == 0)
        def _():
            sums_ref[...] = jnp.zeros_like(sums_ref)

        dh = _dot(dzc_ref[...], w_ref[:ZC_COLS, :]) + _dot(dzm_ref[...], w_ref[ZC_COLS:, :])
        dx, d_sh, d_sc, d_gn = _norm_mod_bwd(dh, x_ref[...], gn_ref[...], sc_ref[...])
        dx = dxo_ref[...] + dx
        dx_ref[...] = dx
        dy_ref[...] = (0.5 * gate_ref[...] * dx).astype(BF16)
        _add_rows(sums_ref, [d_sh, d_sc, d_gn])

    def rows(n):
        return pl.BlockSpec((tm, n), lambda i: (i, 0))

    return _call_with_rider(
        body, rider, name="mix_in_bwd", grid=(t // tm,),
        in_specs=[rows(ZC_COLS), rows(ZM_COLS), _row(w_in), rows(d), rows(d), _row(gn), _row(sc), _row(gate)],
        out_specs=[rows(d), rows(d), pl.BlockSpec((8, d), lambda i: (0, 0))],
        out_shape=[jax.ShapeDtypeStruct((t, d), F32), jax.ShapeDtypeStruct((t, d), BF16),
                   jax.ShapeDtypeStruct((8, d), F32)],
        scratch_shapes=[], operands=(dzc, dzm, w_in, x, dxo, gn, sc, gate))


def final_loss(x, target, g, gate):
    t, d = x.shape
    tm = _tile(t, ROW_TILE, 16)

    def body(x_ref, t_ref, g_ref, gate_ref, dx_ref, dy_ref, sums_ref):
        @pl.when(pl.program_id(0) == 0)
        def _():
            sums_ref[...] = jnp.zeros_like(sums_ref)

        gv = g_ref[...]
        xhat, r = _rms(x_ref[...])
        err = xhat * gv - t_ref[...]
        dyf = err * (1.0 / d)
        dxh = dyf * gv
        dx = r * (dxh - xhat * jnp.mean(dxh * xhat, axis=-1, keepdims=True))
        dx_ref[...] = dx
        dy_ref[...] = (0.5 * gate_ref[...] * dx).astype(BF16)
        _add_rows(sums_ref, [jnp.sum(dyf * xhat, axis=0, keepdims=True),
                             jnp.sum(err * err, axis=0, keepdims=True) * (0.5 / d)])

    row = pl.BlockSpec((tm, d), lambda i: (i, 0))
    return pl.pallas_call(
        body, name="final_loss", grid=(t // tm,),
        in_specs=[row, row, _row(g), _row(gate)],
        out_specs=[row, row, pl.BlockSpec((8, d), lambda i: (0, 0))],
        out_shape=[jax.ShapeDtypeStruct((t, d), F32), jax.ShapeDtypeStruct((t, d), BF16),
                   jax.ShapeDtypeStruct((8, d), F32)],
        compiler_params=_params(("arbitrary",)),
    )(x, target, g, gate)


def adamw(w, g, m, v, name):
    r, n = w.shape
    tr = _tile(r, max(8, (1 << 19) // n), 8)

    def body(w_ref, g_ref, m_ref, v_ref, d_ref, mo_ref, vo_ref):
        gv = g_ref[...]
        m_new = ADAM_B1 * m_ref[...] + (1.0 - ADAM_B1) * gv
        v_new = ADAM_B2 * v_ref[...] + (1.0 - ADAM_B2) * (gv * gv)
        m_hat = m_new / (1.0 - ADAM_B1 ** ADAM_STEP)
        v_hat = v_new / (1.0 - ADAM_B2 ** ADAM_STEP)
        d_ref[...] = -ADAM_LR * (m_hat / (jnp.sqrt(v_hat) + ADAM_EPS) + ADAM_WD * w_ref[...])
        mo_ref[...] = m_new
        vo_ref[...] = v_new

    blk = pl.BlockSpec((tr, n), lambda i: (i, 0))
    shape = jax.ShapeDtypeStruct((r, n), F32)
    return pl.pallas_call(
        body, name=name, grid=(r // tr,), in_specs=[blk] * 4, out_specs=[blk] * 3, out_shape=[shape] * 3,
        compiler_params=_params(("arbitrary",)),
    )(w, g, m, v)


def _pad_to(v, n):
    return jnp.pad(v, (0, n - v.shape[0]))


def _pad_heads(w, axis_len):
    n = w.shape[1]
    return jnp.pad(w.reshape(MLA_HEADS, axis_len, n), ((0, 0), (0, HEAD_PAD - axis_len), (0, 0))).reshape(-1, n)


def _swap_head_parts(w, inner, outer):
    n = w.shape[1]
    return w.reshape(outer, inner, QK_NOPE, n).transpose(1, 0, 2, 3).reshape(-1, n)


def kernel(x, c, positions, ada_w, ada_b, norm_ffn1_g, ffn1_w1, ffn1_w3, ffn1_w2, norm_mix_g, w_in, conv_w, q_norm_g, w_uq, kv_norm_g, w_ukv, out_norm_g, w_out, norm_ffn2_g, ffn2_w1, ffn2_w3, ffn2_w2, final_norm_g, loss_target, m_ada_w, m_ada_b, m_norm_ffn1_g, m_ffn1_w1, m_ffn1_w3, m_ffn1_w2, m_norm_mix_g, m_w_in, m_conv_w, m_q_norm_g, m_w_uq, m_kv_norm_g, m_w_ukv, m_out_norm_g, m_w_out, m_norm_ffn2_g, m_ffn2_w1, m_ffn2_w3, m_ffn2_w2, m_final_norm_g, v_ada_w, v_ada_b, v_norm_ffn1_g, v_ffn1_w1, v_ffn1_w3, v_ffn1_w2, v_norm_mix_g, v_w_in, v_conv_w, v_q_norm_g, v_w_uq, v_kv_norm_g, v_w_ukv, v_out_norm_g, v_w_out, v_norm_ffn2_g, v_ffn2_w1, v_ffn2_w3, v_ffn2_w2, v_final_norm_g):
    t, d = x.shape[1], x.shape[2]
    f = ffn1_w2.shape[1] * N_DEV
    me = 4 * lax.axis_index("x") + 2 * lax.axis_index("y") + lax.axis_index("c")
    my_c = lax.axis_index("c")
    my_chip = 2 * lax.axis_index("x") + lax.axis_index("y")
    xs = x[0]
    n_ada = ada_w.shape[2]
    cw_n = conv_w.shape[2]

    c_rows = jnp.broadcast_to(c, (8, d))
    conv_rows = jnp.pad(conv_w[0], ((0, 8 - CONV_K), (0, LANES - cw_n)))
    ffn1_blocks = jnp.stack([ffn1_w1[0].T, ffn1_w3[0].T, ffn1_w2[0]]).astype(BF16)
    ffn2_blocks = jnp.stack([ffn2_w1[0].T, ffn2_w3[0].T, ffn2_w2[0]]).astype(BF16)
    c_all, conv_all, ffn1_all = all_gather([c_rows, conv_rows, ffn1_blocks], [0, 0, 1], "gather_first")
    c_all = c_all[:, 0, :]
    conv_full8 = conv_all[:, :, :cw_n].transpose(1, 0, 2).reshape(8, CONV_WIDTH)
    ffn1_ws = ffn1_all.reshape(3, f, d)
    gather_mix = riding_gather(
        [w_in[0].T.astype(BF16), w_uq[0].T.astype(BF16), w_ukv[0].T.astype(BF16), w_out[0].astype(BF16)], [0, 0, 0, 0])

    ada_b_cols = lax.dynamic_slice_in_dim(ada_b, me * n_ada, n_ada, axis=1)
    mod_cols = ada_forward(c_all, ada_w[0], ada_b_cols)
    mod_all, = all_gather([mod_cols], [0], "gather_mod")
    mod = lax.dynamic_index_in_dim(mod_all, me, axis=1, keepdims=False).reshape(N_MOD, 1, d)
    sh1, sc1, g1, sh2, sc2, g2, sh3, sc3, g3 = [mod[i] for i in range(N_MOD)]

    gf = final_norm_g.reshape(1, d)
    x1, h1, a1, b1, y1, *gathered = ffn_forward(xs, norm_ffn1_g, sc1, sh1, g1, ffn1_ws, 0, "ffn1_fwd", gather_mix)
    w_in_p = jnp.pad(gathered[0].reshape(IN_COLS, d), ((0, ZC_COLS + ZM_COLS - IN_COLS), (0, 0)))
    w_uq_p = _pad_heads(gathered[1].reshape(-1, Q_LORA), QK_NOPE + QK_ROPE)
    w_ukv_p = _swap_head_parts(gathered[2].reshape(-1, KV_LORA), 2, MLA_HEADS)
    w_out_f = gathered[3].reshape(MIX_WIDTH, d)
    h2, zc, zm = mix_in_forward(x1, norm_mix_g, sc2, sh2, w_in_p)
    pos = positions[0].astype(F32).reshape(t, 1)
    inv_freq = ROPE_THETA ** (-jnp.arange(0, QK_ROPE, 2, dtype=F32) / QK_ROPE)
    inv_freq = jnp.concatenate([inv_freq, inv_freq, jnp.zeros((LANES - QK_ROPE,), F32)]).reshape(1, LANES)
    qn, kvn, q, k, v = mla_project(zm, pos, inv_freq, q_norm_g, kv_norm_g, w_uq_p, w_ukv_p)
    o, lse, ffn2_all = attention_forward(q, k, v, riding_gather([ffn2_blocks], [1]))
    ffn2_ws = ffn2_all.reshape(3, f, d)
    lane = jnp.arange(CONV_WIDTH)
    gmat_a = (lane[:, None] // (CONV_WIDTH // CONV_GROUPS) == lane[None, :] // (CONV_WIDTH // CONV_GROUPS))
    gmat_a = (gmat_a / (CONV_WIDTH // CONV_GROUPS)).astype(BF16)
    gmat_b = ((lane[:, None] // V_HEAD == lane[None, :] // V_HEAD) / V_HEAD).astype(BF16)
    x2, yn, y2, ya = mix_out_forward(zc, o, conv_full8, out_norm_g, gmat_a, gmat_b, w_out_f, x1, g2)
    x3, h3, a3, b3, y3 = ffn_forward(x2, norm_ffn2_g, sc3, sh3, g3, ffn2_ws, 0, "ffn2_fwd")
    dx3, dy3, sums_f = final_loss(x3, loss_target[0], gf, g3)

    chip_idx = jnp.bitwise_xor(my_chip, jnp.array([0, 2, 1, 3], jnp.int32)).astype(jnp.int32)
    src_idx = (2 * chip_idx + my_c).astype(jnp.int32)

    def row_blocks(named):
        return [g.reshape(N_DEV, g.shape[0] // N_DEV, g.shape[1]) for _, g in named]

    def chip_sums(named, g8, got):
        return [add_sibling(g, r, src_idx, chip_idx, "rs_add_" + n) for g, r, (n, _) in zip(g8, got, named)]

    da3, db3, u3 = ffn_backward_gate(dy3, a3, b3, ffn2_ws, 0, "ffn2_bwd_gate")
    dx2, sums_3 = ffn_backward_norm(da3, db3, dx3, x2, y3, norm_ffn2_g, sc3, ffn2_ws, 0, "ffn2_bwd_norm")
    ffn2_named = [("ffn2_w1", matmul_tn(da3, h3, "ffn2_gw1")), ("ffn2_w3", matmul_tn(db3, h3, "ffn2_gw3")),
                  ("ffn2_w2", matmul_tn(u3, dy3, "ffn2_gw2"))]
    ffn2_g8 = row_blocks(ffn2_named)
    dy2, dya, do, delta, sums_2d, sums_2o, *ffn2_sib = mix_out_backward(
        dx2, y2, g2, ya, o, out_norm_g, gmat_a, gmat_b, w_out_f, riding_sibling(ffn2_g8))
    ffn2_sums = chip_sums(ffn2_named, ffn2_g8, ffn2_sib)
    g_w_out = matmul_tn(yn, dy2, "gw_out")
    nq = t // _tile(t, ATTN_TILE, CHUNK)
    stat_shape = (MLA_HEADS, nq, 1, t // nq)
    dq, dk, dv, *ffn2_got = attention_backward(q, k, v, do, lse.reshape(stat_shape), delta.reshape(stat_shape),
                                               riding_exchange([s[1] for s in ffn2_sums]))
    dzc, sums_c = conv_backward(zc, dya, conv_full8)
    dql, dkvl, dzm, sums_m = mla_project_backward(dq, dk, dv, zm, pos, inv_freq, q_norm_g, kv_norm_g, w_uq_p, w_ukv_p)
    g_w_uq_p = matmul_tn(dql, qn, "gw_uq")
    g_w_ukv_p = matmul_tn(dkvl, kvn, "gw_ukv")
    g_w_in = matmul_tn([dzc, dzm], h2, "gw_in")[:IN_COLS]
    g_w_uq = g_w_uq_p.reshape(MLA_HEADS, HEAD_PAD, Q_LORA)[:, :QK_NOPE + QK_ROPE].reshape(-1, Q_LORA)
    g_w_ukv = _swap_head_parts(g_w_ukv_p, MLA_HEADS, 2)
    mix_named = [("w_in", g_w_in), ("w_uq", g_w_uq), ("w_ukv", g_w_ukv), ("w_out", g_w_out)]
    mix_g8 = row_blocks(mix_named)
    dx1, dy1, sums_1m, *mix_sib = mix_in_backward(dzc, dzm, w_in_p, x1, dx2, norm_mix_g, sc2, g1, riding_sibling(mix_g8))
    mix_sums = chip_sums(mix_named, mix_g8, mix_sib)
    da1, db1, u1, *mix_got = ffn_backward_gate(dy1, a1, b1, ffn1_ws, 0, "ffn1_bwd_gate",
                                               riding_exchange([s[1] for s in mix_sums]))
    ffn1_pair = [("ffn1_w1", matmul_tn(da1, h1, "ffn1_gw1")), ("ffn1_w3", matmul_tn(db1, h1, "ffn1_gw3"))]
    pair_g8 = row_blocks(ffn1_pair)
    g_w2a, *pair_sib = matmul_tn(u1, dy1, "ffn1_gw2", riding_sibling(pair_g8))
    ffn1_last = [("ffn1_w2", g_w2a)]
    last_g8 = row_blocks(ffn1_last)
    ffn1_named = ffn1_pair + ffn1_last
    ffn1_sums = chip_sums(ffn1_pair, pair_g8, pair_sib) + chip_sums(
        ffn1_last, last_g8, exchange_sibling(last_g8, "rs_sibling_ffn1_w2"))
    dx0, sums_1, *ffn1_got = ffn_backward_norm(da1, db1, dx1, xs, y1, norm_ffn1_g, sc1, ffn1_ws, 0, "ffn1_bwd_norm",
                                               riding_exchange([s[1] for s in ffn1_sums]))
    g_sh = {}
    for named, group_sums, group_got in ((ffn2_named, ffn2_sums, ffn2_got), (mix_named, mix_sums, mix_got),
                                         (ffn1_named, ffn1_sums, ffn1_got)):
        for (n, _), (own, _), got in zip(named, group_sums, group_got):
            g_sh[n] = add_received(own, got, "rs_sum_" + n)

    dmod = jnp.concatenate([sums_1[0], sums_1[1], sums_1[2], sums_1m[0], sums_1m[1], sums_2d[0],
                            sums_3[0], sums_3[1], sums_3[2]])
    pieces = [dmod, sums_1[3], sums_1m[2], sums_m[0, :Q_LORA], sums_m[0, Q_LORA:Q_LORA + KV_LORA], sums_2o[0],
              sums_3[3], sums_f[0], sums_f[1], sums_c[:CONV_K].reshape(-1)]
    plens = [p.shape[0] for p in pieces]
    poffs = [sum(plens[:i]) for i in range(len(plens))]
    vec_len = -(-sum(plens) // 1024) * 1024
    vec = _pad_to(jnp.concatenate(pieces), vec_len).reshape(-1, LANES)
    vec_all, = all_gather([vec], [0], "gather_sums")
    tot = sum_devices(vec_all).reshape(-1)
    g_ada_b, g_n1, g_nmix, g_qg, g_kvg, g_og, g_n3, g_gf, loss_lanes, g_conv_full = [
        tot[o:o + n] for o, n in zip(poffs, plens)]
    loss = sum_lanes(loss_lanes.reshape(1, d))[0, 0]
    g_conv = lax.dynamic_slice_in_dim(g_conv_full.reshape(CONV_K, CONV_WIDTH), me * cw_n, cw_n, axis=1)
    dmod_all = vec_all.reshape(N_DEV, vec_len)[:, :N_MOD * d]
    dmod_cols = lax.dynamic_slice_in_dim(dmod_all, me * n_ada, n_ada, axis=1)
    g_ada_w = ada_backward(jnp.pad(c_all, ((0, 8), (0, 0))), jnp.pad(dmod_cols, ((0, 8), (0, 0))))

    def update(name, w, g, m, v):
        k, n = w.shape[-2:]
        if g.shape == (k, n):
            flat, back = (lambda a: a.reshape(k, n)), (lambda a: a.reshape(w.shape))
        else:
            flat, back = (lambda a: a.reshape(k, n).T), (lambda a: a.T.reshape(w.shape))
        dlt, nm, nv = adamw(flat(w), g, flat(m), flat(v), "adamw_" + name)
        return back(g), back(dlt), back(nm), back(nv)

    res = {}
    res["ada_w"] = update("ada_w", ada_w, g_ada_w, m_ada_w, v_ada_w)
    big = [("ffn1_w1", ffn1_w1, m_ffn1_w1, v_ffn1_w1), ("ffn1_w3", ffn1_w3, m_ffn1_w3, v_ffn1_w3),
           ("ffn2_w1", ffn2_w1, m_ffn2_w1, v_ffn2_w1), ("ffn2_w3", ffn2_w3, m_ffn2_w3, v_ffn2_w3),
           ("w_in", w_in, m_w_in, v_w_in), ("w_uq", w_uq, m_w_uq, v_w_uq), ("w_ukv", w_ukv, m_w_ukv, v_w_ukv),
           ("ffn1_w2", ffn1_w2, m_ffn1_w2, v_ffn1_w2), ("ffn2_w2", ffn2_w2, m_ffn2_w2, v_ffn2_w2),
           ("w_out", w_out, m_w_out, v_w_out)]
    for name, w, m, v in big:
        res[name] = update(name, w, g_sh[name], m, v)
    smalls = [("ada_b", ada_b, g_ada_b, m_ada_b, v_ada_b),
              ("norm_ffn1_g", norm_ffn1_g, g_n1, m_norm_ffn1_g, v_norm_ffn1_g),
              ("norm_mix_g", norm_mix_g, g_nmix, m_norm_mix_g, v_norm_mix_g),
              ("conv_w", conv_w, g_conv, m_conv_w, v_conv_w),
              ("q_norm_g", q_norm_g, g_qg, m_q_norm_g, v_q_norm_g),
              ("kv_norm_g", kv_norm_g, g_kvg, m_kv_norm_g, v_kv_norm_g),
              ("out_norm_g", out_norm_g, g_og, m_out_norm_g, v_out_norm_g),
              ("norm_ffn2_g", norm_ffn2_g, g_n3, m_norm_ffn2_g, v_norm_ffn2_g),
              ("final_norm_g", final_norm_g, g_gf, m_final_norm_g, v_final_norm_g)]
    slens = [w.size for _, w, _, _, _ in smalls]
    soffs = [sum(slens[:i]) for i in range(len(slens))]
    s_len = -(-sum(slens) // 1024) * 1024

    def pack_small(i):
        return _pad_to(jnp.concatenate([s[i].reshape(-1) for s in smalls]), s_len).reshape(8, -1)

    s_out = adamw(pack_small(1), pack_small(2), pack_small(3), pack_small(4), "adamw_small")
    for (name, w, g, _, _), o, n in zip(smalls, soffs, slens):
        res[name] = (g.reshape(w.shape),) + tuple(a.reshape(-1)[o:o + n].reshape(w.shape) for a in s_out)

    order = ["ada_w", "ada_b", "norm_ffn1_g", "ffn1_w1", "ffn1_w3", "ffn1_w2", "norm_mix_g", "w_in", "conv_w",
             "q_norm_g", "w_uq", "kv_norm_g", "w_ukv", "out_norm_g", "w_out", "norm_ffn2_g", "ffn2_w1", "ffn2_w3",
             "ffn2_w2", "final_norm_g"]
    return (loss, dx0.reshape(x.shape), *[res[n][0] for n in order], *[res[n][1] for n in order],
            *[res[n][2] for n in order], *[res[n][3] for n in order])
```

```python
import functools
import math

import jax
import jax.numpy as jnp
from jax import lax
from jax.experimental import pallas as pl
from jax.experimental.pallas import tpu as pltpu

F32 = jnp.float32
BF16 = jnp.bfloat16
MESH_ID = pl.DeviceIdType.MESH
N_DEV = 8

EPS = 1e-6
CHUNK = 64
N_MOD = 9
CONV_WIDTH = 512
CONV_GROUPS = 8
CONV_K = 3
MLA_HEADS = 4
QK_NOPE = 128
QK_ROPE = 64
V_HEAD = 128
Q_LORA = 384
KV_LORA = 256
ROPE_THETA = 10000.0
MLA_WIDTH = MLA_HEADS * V_HEAD
MIX_WIDTH = CONV_WIDTH + MLA_WIDTH
IN_COLS = 3 * CONV_WIDTH + Q_LORA + KV_LORA + QK_ROPE
ZC_COLS = 3 * CONV_WIDTH
ZM_COLS = Q_LORA + KV_LORA + 128
HEAD_PAD = 256
QK_COLS = MLA_HEADS * HEAD_PAD
ATTN_SCALE = (QK_NOPE + QK_ROPE) ** -0.5
LOG2_E = 1.4426950408889634
LN_2 = 0.6931471805599453
QK_FOLD = ATTN_SCALE * LOG2_E
NEG_INF = -1e30

ADAM_LR = 0.001
ADAM_B1 = 0.9
ADAM_B2 = 0.999
ADAM_EPS = 1e-08
ADAM_WD = 0.01
ADAM_STEP = 10

LANES = 128
MXU_COLS = 256
VMEM_LIMIT = 56 * 1024 * 1024
ROW_TILE = 1024
FFN_FWD_TILE = (1024, 256)
FFN_BWD_TILE = (512, 1408)
GRAD_TILE = 1408
GRAD_DEPTH = 2048
SUM_ROWS = 256
ATTN_TILE = 1024

NN = (((1,), (0,)), ((), ()))
NT = (((1,), (1,)), ((), ()))
TN = (((0,), (0,)), ((), ()))


def _dot(a, b, dims=NN):
    return lax.dot_general(a, b, dims, preferred_element_type=F32)


def _tile(n, cap, mult=LANES):
    best = None
    for t in range(mult, min(n, cap) + 1, mult):
        if n % t == 0:
            best = t
    return n if best is None else best


def _params(sem=None):
    return pltpu.CompilerParams(dimension_semantics=sem, vmem_limit_bytes=VMEM_LIMIT)


def _row(v):
    return pl.BlockSpec(v.shape, lambda *_: (0,) * v.ndim)


def _sigmoid(x):
    return 0.5 * jnp.tanh(0.5 * x) + 0.5


def _rms(x):
    r = lax.rsqrt(jnp.mean(x * x, axis=-1, keepdims=True) + EPS)
    return x * r, r


def _norm_mod_bwd(dh, x, gn, sc):
    xhat, r = _rms(x)
    d_sh = jnp.sum(dh, axis=0, keepdims=True)
    d_sc = jnp.sum(dh * (xhat * gn), axis=0, keepdims=True)
    dxn = dh * (1.0 + sc)
    d_gn = jnp.sum(dxn * xhat, axis=0, keepdims=True)
    dxh = dxn * gn
    dx = r * (dxh - xhat * jnp.mean(dxh * xhat, axis=-1, keepdims=True))
    return dx, d_sh, d_sc, d_gn


def _group_mean(v, gmat):
    return _dot(v.astype(BF16), gmat)


def _add_rows(ref, rows):
    for r, v in enumerate(rows):
        ref[r:r + 1, :] += v


def _window(ref, axis, j):
    return ref.at[(slice(None),) * axis + (j,)]


def _any_specs(n):
    return [pl.BlockSpec(memory_space=pl.ANY)] * n


def all_gather(blocks, axes, name):
    n_arr = len(blocks)

    def body(*refs):
        start, forward, finish = _gather_steps(refs[:n_arr], refs[n_arr:2 * n_arr], axes, *refs[2 * n_arr:])
        start()
        for j in range(3):
            forward(j)
        finish()

    return pl.pallas_call(
        body, name=name, out_shape=_gathered_shapes(blocks, axes),
        in_specs=_any_specs(n_arr), out_specs=_any_specs(n_arr), scratch_shapes=_gather_sems(n_arr),
    )(*blocks)


def all_gather_relayed(blocks, axes, name):
    n_arr = len(blocks)
    arrays = range(n_arr)

    def body(*refs):
        ins, outs = refs[:n_arr], refs[n_arr:2 * n_arr]
        send_sems, recv_sems, local_sems = refs[2 * n_arr:]
        x, y, c = lax.axis_index("x"), lax.axis_index("y"), lax.axis_index("c")
        sibling, x_nbr, y_nbr, diagonal = (x, y, 1 - c), (1 - x, y, c), (x, 1 - y, c), (1 - x, 1 - y, c)
        north = c == 1
        relay_slot = jnp.where(north, 1, 2)
        relay_from = tuple(jnp.where(north, a, b) for a, b in zip(x_nbr, y_nbr))
        relay_to = tuple(jnp.where(north, a, b) for a, b in zip(y_nbr, x_nbr))
        other_from = relay_to

        def slot(a, px, py, pc):
            return _window(outs[a], axes[a], 4 * px + 2 * py + pc)

        def copy(a, k, block, to, src=None):
            return pltpu.make_async_remote_copy(
                src_ref=slot(a, *block) if src is None else src, dst_ref=slot(a, *block),
                send_sem=send_sems.at[k, a], recv_sem=recv_sems.at[k, a], device_id=to, device_id_type=MESH_ID)

        mine = [pltpu.make_async_copy(ins[a], slot(a, x, y, c), local_sems.at[a]) for a in arrays]
        for cp in mine:
            cp.start()
        first = [copy(a, k, (x, y, c), to, src=ins[a])
                 for k, to in enumerate((sibling, x_nbr, y_nbr)) for a in arrays]
        for cp in first:
            cp.start()
        later = []
        for a in arrays:
            copy(a, relay_slot, relay_from, (x, y, c)).wait_recv()
            later += [copy(a, 3, relay_from, relay_to), copy(a, 3 + relay_slot, relay_from, sibling)]
            later[-2].start()
            later[-1].start()
        for a in arrays:
            copy(a, 3 - relay_slot, other_from, (x, y, c)).wait_recv()
            later.append(copy(a, 6 - relay_slot, other_from, sibling))
            later[-1].start()
        for a in arrays:
            copy(a, 3, diagonal, (x, y, c)).wait_recv()
            later.append(copy(a, 6, diagonal, sibling))
            later[-1].start()
        for a in arrays:
            for k, block in ((0, sibling), (4, (1 - x, y, 1 - c)), (5, (x, 1 - y, 1 - c)), (6, (1 - x, 1 - y, 1 - c))):
                copy(a, k, block, (x, y, c)).wait_recv()
        for cp in first + later:
            cp.wait_send()
        for cp in mine:
            cp.wait()

    return pl.pallas_call(
        body, name=name, out_shape=_gathered_shapes(blocks, axes),
        in_specs=_any_specs(n_arr), out_specs=_any_specs(n_arr), scratch_shapes=_gather_sems(n_arr),
    )(*blocks)


def _gathered_shapes(blocks, axes):
    return [jax.ShapeDtypeStruct(b.shape[:ax] + (N_DEV,) + b.shape[ax:], b.dtype) for b, ax in zip(blocks, axes)]


def _gather_sems(n_arr):
    return [pltpu.SemaphoreType.DMA((7, n_arr)), pltpu.SemaphoreType.DMA((7, n_arr)), pltpu.SemaphoreType.DMA((n_arr,))]


def _gather_steps(ins, outs, axes, send_sems, recv_sems, local_sems):
    arrays = range(len(ins))
    x, y, c = lax.axis_index("x"), lax.axis_index("y"), lax.axis_index("c")
    me, sibling = (x, y, c), (x, y, 1 - c)
    chips = [(1 - x, y), (x, 1 - y), (1 - x, 1 - y)]

    def slot(a, px, py, pc):
        return _window(outs[a], axes[a], 4 * px + 2 * py + pc)

    def copy(a, k, block, to, src=None):
        return pltpu.make_async_remote_copy(
            src_ref=slot(a, *block) if src is None else src, dst_ref=slot(a, *block),
            send_sem=send_sems.at[k, a], recv_sem=recv_sems.at[k, a], device_id=to, device_id_type=MESH_ID)

    def mine(a):
        return pltpu.make_async_copy(ins[a], slot(a, *me), local_sems.at[a])

    def first():
        return ([copy(a, 0, me, sibling, src=ins[a]) for a in arrays]
                + [copy(a, 1 + j, me, (*chip, c), src=ins[a]) for j, chip in enumerate(chips) for a in arrays])

    def passed(j):
        return [copy(a, 4 + j, (*chips[j], c), sibling) for a in arrays]

    def start():
        for a in arrays:
            mine(a).start()
        for cp in first():
            cp.start()

    def forward(j):
        for a, cp in zip(arrays, passed(j)):
            copy(a, 1 + j, (*chips[j], c), me).wait_recv()
            cp.start()

    def finish():
        for a in arrays:
            copy(a, 0, sibling, me).wait_recv()
        for j, chip in enumerate(chips):
            for a in arrays:
                copy(a, 4 + j, (*chip, 1 - c), me).wait_recv()
        for cp in first() + passed(0) + passed(1) + passed(2):
            cp.wait_send()
        for a in arrays:
            mine(a).wait()

    return start, forward, finish


def exchange_sibling(grads, name):
    n_arr = len(grads)

    def body(*refs):
        start, finish = _sibling_exchange_steps(refs[:n_arr], refs[n_arr:2 * n_arr], *refs[2 * n_arr:])
        start()
        finish()

    return pl.pallas_call(
        body, name=name, out_shape=_sibling_shapes(grads),
        in_specs=_any_specs(n_arr), out_specs=_any_specs(n_arr), scratch_shapes=_exchange_sems(n_arr),
    )(*grads)


def _sibling_shapes(grads):
    return [jax.ShapeDtypeStruct((4,) + g.shape[1:], g.dtype) for g in grads]


def _exchange_sems(n_arr):
    return [pltpu.SemaphoreType.DMA((n_arr,)), pltpu.SemaphoreType.DMA((n_arr,))]


def _sibling_exchange_steps(ins, outs, send_sems, recv_sems):
    x, y, c = lax.axis_index("x"), lax.axis_index("y"), lax.axis_index("c")

    def copy(a, src, dst):
        return pltpu.make_async_remote_copy(
            src_ref=src, dst_ref=dst, send_sem=send_sems.at[a], recv_sem=recv_sems.at[a],
            device_id=(x, y, 1 - c), device_id_type=MESH_ID)

    def start():
        for a in range(len(ins)):
            for k in range(4):
                copy(a, ins[a].at[2 * k + (1 - c)], outs[a].at[k]).start()

    def finish():
        whole = [copy(a, ins[a].at[pl.ds(0, 4)], outs[a]) for a in range(len(ins))]
        for cp in whole:
            cp.wait_recv()
        for cp in whole:
            cp.wait_send()

    return start, finish


def _chip_exchange_steps(ins, outs, send_sems, recv_sems):
    x, y, c = lax.axis_index("x"), lax.axis_index("y"), lax.axis_index("c")
    chips = [(1 - x, y), (x, 1 - y), (1 - x, 1 - y)]

    def copy(a, src, dst, chip):
        return pltpu.make_async_remote_copy(
            src_ref=src, dst_ref=dst, send_sem=send_sems.at[a], recv_sem=recv_sems.at[a],
            device_id=(*chip, c), device_id_type=MESH_ID)

    def start():
        for a in range(len(ins)):
            for j, chip in enumerate(chips):
                copy(a, ins[a].at[j], outs[a].at[j], chip).start()

    def finish():
        whole = [copy(a, ins[a], outs[a], chips[0]) for a in range(len(ins))]
        for cp in whole:
            cp.wait_recv()
        for cp in whole:
            cp.wait_send()

    return start, finish


def riding_gather(blocks, axes):
    def phases(ins, outs, *sems):
        start, forward, finish = _gather_steps(ins, outs, axes, *sems)
        return [start] + [functools.partial(forward, j) for j in range(3)] + [finish]

    return dict(operands=blocks, out_shape=_gathered_shapes(blocks, axes), sems=_gather_sems(len(blocks)),
                phases=phases, when=("first", "late0", "late1", "late2", "last"))


def riding_exchange(parts):
    def phases(ins, outs, *sems):
        return list(_chip_exchange_steps(ins, outs, *sems))

    return dict(operands=parts, out_shape=[jax.ShapeDtypeStruct(p.shape, p.dtype) for p in parts],
                sems=_exchange_sems(len(parts)), phases=phases, when=("first", "last"))


def riding_sibling(grads):
    def phases(ins, outs, *sems):
        return list(_sibling_exchange_steps(ins, outs, *sems))

    return dict(operands=grads, out_shape=_sibling_shapes(grads), sems=_exchange_sems(len(grads)),
                phases=phases, when=("first", "last"))


def _call_with_rider(body, rider, *, name, grid, in_specs, out_specs, out_shape, scratch_shapes, operands):
    params = _params(("arbitrary",) * len(grid))
    if rider is None:
        return pl.pallas_call(body, name=name, grid=grid, in_specs=in_specs, out_specs=out_specs,
                              out_shape=out_shape, scratch_shapes=scratch_shapes, compiler_params=params)(*operands)
    n_in, n_out, n_scr, k = len(in_specs), len(out_specs), len(scratch_shapes), len(rider["operands"])
    at = {"first": (0,) * len(grid), "last": tuple(g - 1 for g in grid)}
    if "late0" in rider["when"]:
        rows, cols = grid
        assert cols >= 3
        at.update({"late%d" % j: (max(rows - 2, 0), j) for j in range(3)})

    def wrapped(*refs):
        ins, c_in = refs[:n_in], refs[n_in:n_in + k]
        outs, c_out = refs[n_in + k:n_in + k + n_out], refs[n_in + k + n_out:n_in + 2 * k + n_out]
        scratch, sems = refs[n_in + 2 * k + n_out:n_in + 2 * k + n_out + n_scr], refs[n_in + 2 * k + n_out + n_scr:]
        pos = [pl.program_id(axis) for axis in range(len(grid))]

        def here(key):
            return functools.reduce(jnp.logical_and, [p == v for p, v in zip(pos, at[key])])

        phases = rider["phases"](c_in, c_out, *sems)
        for fn, key in zip(phases, rider["when"]):
            if key != "last":
                pl.when(here(key))(fn)
        body(*ins, *outs, *scratch)
        pl.when(here("last"))(phases[-1])

    return pl.pallas_call(
        wrapped, name=name, grid=grid,
        in_specs=list(in_specs) + _any_specs(k), out_specs=list(out_specs) + _any_specs(k),
        out_shape=list(out_shape) + rider["out_shape"], scratch_shapes=list(scratch_shapes) + rider["sems"],
        compiler_params=params)(*operands, *rider["operands"])


def add_sibling(g8, got, src_idx, chip_idx, name):
    _, r, n = g8.shape
    tr = _tile(r, SUM_ROWS, 16)

    def body(si_ref, ci_ref, g0_ref, g1_ref, g2_ref, g3_ref, got_ref, own_ref, send_ref):
        own_ref[...] = g0_ref[0] + got_ref[ci_ref[0]]
        for j, g_ref in enumerate((g1_ref, g2_ref, g3_ref)):
            send_ref[j] = (g_ref[0] + got_ref[ci_ref[j + 1]]).astype(BF16)

    def mine(j):
        return pl.BlockSpec((1, tr, n), lambda i, si, ci: (si[j], i, 0))

    return pl.pallas_call(
        body, name=name,
        out_shape=[jax.ShapeDtypeStruct((r, n), F32), jax.ShapeDtypeStruct((3, r, n), BF16)],
        grid_spec=pltpu.PrefetchScalarGridSpec(
            num_scalar_prefetch=2, grid=(r // tr,),
            in_specs=[mine(0), mine(1), mine(2), mine(3), pl.BlockSpec((4, tr, n), lambda i, si, ci: (0, i, 0))],
            out_specs=[pl.BlockSpec((tr, n), lambda i, si, ci: (i, 0)),
                       pl.BlockSpec((3, tr, n), lambda i, si, ci: (0, i, 0))]),
        compiler_params=_params(("arbitrary",)),
    )(src_idx, chip_idx, g8, g8, g8, g8, got)


def add_received(own, got, name):
    r, n = own.shape
    tr = _tile(r, SUM_ROWS, 16)

    def body(a_ref, b_ref, o_ref):
        acc = a_ref[...]
        for j in range(3):
            acc = acc + b_ref[j].astype(F32)
        o_ref[...] = acc

    return pl.pallas_call(
        body, name=name,
        out_shape=jax.ShapeDtypeStruct((r, n), F32),
        grid=(r // tr,),
        in_specs=[pl.BlockSpec((tr, n), lambda i: (i, 0)), pl.BlockSpec((3, tr, n), lambda i: (0, i, 0))],
        out_specs=pl.BlockSpec((tr, n), lambda i: (i, 0)),
        compiler_params=_params(("arbitrary",)),
    )(own, got)


def sum_devices(g):
    def body(g_ref, o_ref):
        acc = g_ref[0]
        for j in range(1, N_DEV):
            acc = acc + g_ref[j]
        o_ref[...] = acc

    return pl.pallas_call(body, name="sum_devices", out_shape=jax.ShapeDtypeStruct(g.shape[1:], F32))(g)


def sum_lanes(v):
    def body(v_ref, o_ref):
        o_ref[...] = jnp.broadcast_to(jnp.sum(v_ref[...], axis=-1, keepdims=True), (1, LANES))

    return pl.pallas_call(body, name="sum_lanes", out_shape=jax.ShapeDtypeStruct((1, LANES), F32))(v)


def ada_forward(c_all, ada_w, ada_b_cols):
    nb, n = c_all.shape[0], ada_w.shape[1]

    def body(c_ref, w_ref, b_ref, o_ref):
        cv = c_ref[...]
        s = (cv * jax.nn.sigmoid(cv)).astype(BF16)
        o_ref[...] = _dot(s, w_ref[...].astype(BF16)) + b_ref[...]

    return pl.pallas_call(body, name="ada_fwd", out_shape=jax.ShapeDtypeStruct((nb, n), F32),
                          compiler_params=_params())(c_all, ada_w, ada_b_cols)


def ada_backward(c_all16, dmod16):
    d, n = c_all16.shape[1], dmod16.shape[1]

    def body(c_ref, g_ref, o_ref):
        cv = c_ref[...]
        s = (cv * jax.nn.sigmoid(cv)).astype(BF16)
        o_ref[...] = _dot(s, g_ref[...].astype(BF16), TN)

    return pl.pallas_call(body, name="ada_bwd", out_shape=jax.ShapeDtypeStruct((d, n), F32),
                          compiler_params=_params())(c_all16, dmod16)


def ffn_forward(x, gn, sc, sh, gate, ws, first, name, rider=None):
    t, d = x.shape
    f = ws.shape[1]
    tm, tf = _tile(t, FFN_FWD_TILE[0], 16), _tile(f, FFN_FWD_TILE[1])
    nf = f // tf

    def body(x_ref, gn_ref, sc_ref, sh_ref, gate_ref, w1_ref, w3_ref, w2_ref,
             xo_ref, h_ref, a_ref, b_ref, y_ref, hs, acc):
        j = pl.program_id(1)

        @pl.when(j == 0)
        def _():
            xhat, _ = _rms(x_ref[...])
            h = (xhat * gn_ref[...] * (1.0 + sc_ref[...]) + sh_ref[...]).astype(BF16)
            hs[...] = h
            h_ref[...] = h
            acc[...] = jnp.zeros_like(acc)

        h = hs[...]
        a = _dot(h, w1_ref[...], NT)
        b = _dot(h, w3_ref[...], NT)
        a_ref[...] = a.astype(BF16)
        b_ref[...] = b.astype(BF16)
        u = (a * _sigmoid(a) * b).astype(BF16)
        acc[...] += _dot(u, w2_ref[...])

        @pl.when(j == nf - 1)
        def _():
            y = acc[...]
            y_ref[...] = y.astype(BF16)
            xo_ref[...] = x_ref[...] + 0.5 * gate_ref[...] * y

    row = pl.BlockSpec((tm, d), lambda i, j: (i, 0))
    vec = pl.BlockSpec((1, d), lambda i, j: (0, 0))
    wide = pl.BlockSpec((tm, tf), lambda i, j: (i, j))
    return _call_with_rider(
        body, rider, name=name, grid=(t // tm, nf),
        in_specs=[row, vec, vec, vec, vec] + _ffn_weight_specs(first, tf, d),
        out_specs=[row, row, wide, wide, row],
        out_shape=[jax.ShapeDtypeStruct((t, d), F32), jax.ShapeDtypeStruct((t, d), BF16),
                   jax.ShapeDtypeStruct((t, f), BF16), jax.ShapeDtypeStruct((t, f), BF16),
                   jax.ShapeDtypeStruct((t, d), BF16)],
        scratch_shapes=[pltpu.VMEM((tm, d), BF16), pltpu.VMEM((tm, d), F32)],
        operands=(x, gn, sc, sh, gate, ws, ws, ws))


def _ffn_weight_specs(first, tf, d):
    return [pl.BlockSpec((None, tf, d), lambda i, j, w=first + k: (w, j, 0)) for k in range(3)]


def ffn_backward_gate(dy, a, b, ws, first, name, rider=None):
    t, d = dy.shape
    f = ws.shape[1]
    tm, tf = _tile(t, FFN_BWD_TILE[0], 16), _tile(f, FFN_BWD_TILE[1])
    nf = f // tf

    def gate_body(dy_ref, a_ref, b_ref, w2_ref, da_ref, db_ref, u_ref):
        du = _dot(dy_ref[...], w2_ref[...], NT)
        av = a_ref[...].astype(F32)
        bv = b_ref[...].astype(F32)
        s = _sigmoid(av)
        sa = av * s
        da_ref[...] = (du * bv * (s + sa * (1.0 - s))).astype(BF16)
        db_ref[...] = (du * sa).astype(BF16)
        u_ref[...] = (sa * bv).astype(BF16)

    hidden = jax.ShapeDtypeStruct((t, f), BF16)
    wide_t = pl.BlockSpec((tm, tf), lambda j, i: (i, j))
    return _call_with_rider(
        gate_body, rider, name=name, grid=(nf, t // tm),
        in_specs=[pl.BlockSpec((tm, d), lambda j, i: (i, 0)), wide_t, wide_t,
                  pl.BlockSpec((None, tf, d), lambda j, i: (first + 2, j, 0))],
        out_specs=[wide_t, wide_t, wide_t], out_shape=[hidden, hidden, hidden],
        scratch_shapes=[], operands=(dy, a, b, ws))


def ffn_backward_norm(da, db, dxo, x, y, gn, sc, ws, first, name, rider=None):
    t, d = x.shape
    f = ws.shape[1]
    tm, tf = _tile(t, FFN_BWD_TILE[0], 16), _tile(f, FFN_BWD_TILE[1])
    nf = f // tf
    row = pl.BlockSpec((tm, d), lambda i, j: (i, 0))
    vec = pl.BlockSpec((1, d), lambda i, j: (0, 0))
    wide = pl.BlockSpec((tm, tf), lambda i, j: (i, j))

    def norm_body(da_ref, db_ref, w1_ref, w3_ref, dxo_ref, x_ref, y_ref, gn_ref, sc_ref, dx_ref, sums_ref, acc):
        i, j = pl.program_id(0), pl.program_id(1)

        @pl.when(jnp.logical_and(i == 0, j == 0))
        def _():
            sums_ref[...] = jnp.zeros_like(sums_ref)

        part = _dot(da_ref[...], w1_ref[...]) + _dot(db_ref[...], w3_ref[...])

        @pl.when(j == 0)
        def _():
            acc[...] = part

        @pl.when(jnp.logical_and(j > 0, j < nf - 1))
        def _():
            acc[...] += part

        @pl.when(j == nf - 1)
        def _():
            dh = part if nf == 1 else acc[...] + part
            dxo_v = dxo_ref[...]
            dx, d_sh, d_sc, d_gn = _norm_mod_bwd(dh, x_ref[...], gn_ref[...], sc_ref[...])
            dx_ref[...] = dxo_v + dx
            d_gate = jnp.sum(dxo_v * (0.5 * y_ref[...].astype(F32)), axis=0, keepdims=True)
            _add_rows(sums_ref, [d_sh, d_sc, d_gate, d_gn])

    w1_spec, w3_spec, _ = _ffn_weight_specs(first, tf, d)
    return _call_with_rider(
        norm_body, rider, name=name, grid=(t // tm, nf),
        in_specs=[wide, wide, w1_spec, w3_spec, row, row, row, vec, vec],
        out_specs=[row, pl.BlockSpec((8, d), lambda i, j: (0, 0))],
        out_shape=[jax.ShapeDtypeStruct((t, d), F32), jax.ShapeDtypeStruct((8, d), F32)],
        scratch_shapes=[pltpu.VMEM((tm, d), F32)],
        operands=(da, db, ws, ws, dxo, x, y, gn, sc))


def matmul_tn(a, b, name, rider=None):
    parts = list(a) if isinstance(a, (list, tuple)) else [a]
    t, n = b.shape
    widths = [p.shape[1] for p in parts]
    tm = _tile(functools.reduce(math.gcd, widths), GRAD_TILE)
    tn, tk = _tile(n, GRAD_TILE), _tile(t, GRAD_DEPTH, 16)
    nk = t // tk
    counts = [w // tm for w in widths]
    firsts = [sum(counts[:p]) for p in range(len(parts))]

    def body(*refs):
        a_refs, (b_ref, o_ref, acc) = refs[:len(parts)], refs[len(parts):]
        i, k = pl.program_id(0), pl.program_id(2)

        @pl.when(k == 0)
        def _():
            acc[...] = jnp.zeros_like(acc)

        for a_ref, lo, cnt in zip(a_refs, firsts, counts):
            def accumulate(a_ref=a_ref):
                acc[...] += _dot(a_ref[...], b_ref[...], TN)

            if len(parts) == 1:
                accumulate()
            else:
                pl.when(jnp.logical_and(i >= lo, i < lo + cnt))(accumulate)

        @pl.when(k == nk - 1)
        def _():
            o_ref[...] = acc[...]

    def part_spec(lo, cnt):
        if len(parts) == 1:
            return pl.BlockSpec((tk, tm), lambda i, j, k: (k, i))

        def index(i, j, k):
            mine = jnp.logical_and(i >= lo, i < lo + cnt)
            return jnp.where(mine, k, 0), jnp.clip(i - lo, 0, cnt - 1)
        return pl.BlockSpec((tk, tm), index)

    out = _call_with_rider(
        body, rider, name=name, grid=(sum(counts), n // tn, nk),
        in_specs=[part_spec(lo, cnt) for lo, cnt in zip(firsts, counts)]
        + [pl.BlockSpec((tk, tn), lambda i, j, k: (k, j))],
        out_specs=[pl.BlockSpec((tm, tn), lambda i, j, k: (i, j))],
        out_shape=[jax.ShapeDtypeStruct((sum(widths), n), F32)],
        scratch_shapes=[pltpu.VMEM((tm, tn), F32)], operands=(*parts, b))
    return out[0] if rider is None else out


def mix_in_forward(x, gn, sc, sh, w_in):
    t, d = x.shape
    tm = _tile(t, ROW_TILE, 16)

    def body(x_ref, gn_ref, sc_ref, sh_ref, w_ref, h_ref, zc_ref, zm_ref):
        xhat, _ = _rms(x_ref[...])
        h = (xhat * gn_ref[...] * (1.0 + sc_ref[...]) + sh_ref[...]).astype(BF16)
        h_ref[...] = h
        z = _dot(h, w_ref[...], NT)
        zc_ref[...] = z[:, :ZC_COLS].astype(BF16)
        zm_ref[...] = z[:, ZC_COLS:].astype(BF16)

    row = pl.BlockSpec((tm, d), lambda i: (i, 0))
    vec = pl.BlockSpec((1, d), lambda i: (0, 0))
    return pl.pallas_call(
        body, name="mix_in_fwd", grid=(t // tm,),
        in_specs=[row, vec, vec, vec, _row(w_in)],
        out_specs=[row, pl.BlockSpec((tm, ZC_COLS), lambda i: (i, 0)), pl.BlockSpec((tm, ZM_COLS), lambda i: (i, 0))],
        out_shape=[jax.ShapeDtypeStruct((t, d), BF16), jax.ShapeDtypeStruct((t, ZC_COLS), BF16),
                   jax.ShapeDtypeStruct((t, ZM_COLS), BF16)],
        compiler_params=_params(("arbitrary",)),
    )(x, gn, sc, sh, w_in)


def _rope_tables(pos, inv_freq):
    ang = pos * inv_freq
    lane = lax.broadcasted_iota(jnp.int32, ang.shape, 1)
    cos, sin = jnp.cos(ang), jnp.sin(ang)
    half = QK_ROPE // 2
    return cos, jnp.where(lane < half, -sin, 0.0), jnp.where(jnp.logical_and(lane >= half, lane < QK_ROPE), sin, 0.0)


def _rope(v, tables):
    cos, sin_a, sin_b = tables
    return v * cos + pltpu.roll(v, LANES - QK_ROPE // 2, 1) * sin_a + pltpu.roll(v, QK_ROPE // 2, 1) * sin_b


def _rope_transposed(dv, tables):
    cos, sin_a, sin_b = tables
    return dv * cos + pltpu.roll(dv * sin_a, QK_ROPE // 2, 1) + pltpu.roll(dv * sin_b, LANES - QK_ROPE // 2, 1)


def mla_project(zm, pos, inv_freq, qg, kvg, w_uq, w_ukv):
    t = zm.shape[0]
    tm = _tile(t, ROW_TILE, 16)

    def body(zm_ref, pos_ref, if_ref, qg_ref, kvg_ref, wq_ref, wkv_ref, qn_ref, kvn_ref, q_ref, k_ref, v_ref):
        zv = zm_ref[...].astype(F32)
        qn = (_rms(zv[:, :Q_LORA])[0] * qg_ref[...]).astype(BF16)
        kvn = (_rms(zv[:, Q_LORA:Q_LORA + KV_LORA])[0] * kvg_ref[...]).astype(BF16)
        qn_ref[...] = qn
        kvn_ref[...] = kvn
        qf = _dot(qn, wq_ref[...], NT) * QK_FOLD
        kvf = _dot(kvn, wkv_ref[...], NT)
        tables = _rope_tables(pos_ref[...], if_ref[...])
        kr = _rope(zv[:, Q_LORA + KV_LORA:], tables).astype(BF16)
        for h in range(MLA_HEADS):
            lo = h * HEAD_PAD
            q_ref[:, lo:lo + QK_NOPE] = qf[:, lo:lo + QK_NOPE].astype(BF16)
            q_ref[:, lo + QK_NOPE:lo + HEAD_PAD] = _rope(qf[:, lo + QK_NOPE:lo + HEAD_PAD], tables).astype(BF16)
            k_ref[:, lo:lo + QK_NOPE] = kvf[:, h * QK_NOPE:(h + 1) * QK_NOPE].astype(BF16)
            k_ref[:, lo + QK_NOPE:lo + HEAD_PAD] = kr
        v_ref[...] = kvf[:, MLA_HEADS * QK_NOPE:].astype(BF16)

    def rows(n):
        return pl.BlockSpec((tm, n), lambda i: (i, 0))

    return pl.pallas_call(
        body, name="mla_project", grid=(t // tm,),
        in_specs=[rows(ZM_COLS), rows(1), _row(inv_freq), _row(qg), _row(kvg), _row(w_uq), _row(w_ukv)],
        out_specs=[rows(Q_LORA), rows(KV_LORA), rows(QK_COLS), rows(QK_COLS), rows(MLA_WIDTH)],
        out_shape=[jax.ShapeDtypeStruct((t, Q_LORA), BF16), jax.ShapeDtypeStruct((t, KV_LORA), BF16),
                   jax.ShapeDtypeStruct((t, QK_COLS), BF16), jax.ShapeDtypeStruct((t, QK_COLS), BF16),
                   jax.ShapeDtypeStruct((t, MLA_WIDTH), BF16)],
        compiler_params=_params(("arbitrary",)),
    )(zm, pos, inv_freq, qg, kvg, w_uq, w_ukv)


def _chunk_mask(shape, q_axis):
    qi = lax.broadcasted_iota(jnp.int32, shape, q_axis) // CHUNK
    ki = lax.broadcasted_iota(jnp.int32, shape, 1 - q_axis) // CHUNK
    return ki <= qi


def attention_forward(q, k, v, rider=None):
    t = q.shape[0]
    tq = _tile(t, ATTN_TILE, CHUNK)

    def body(q_ref, k_ref, v_ref, o_ref, lse_ref):
        i = pl.program_id(1)
        qv = q_ref[...]

        def step(kb, carry, masked, tiles=1):
            m, l, acc = carry
            keys = pl.ds(pl.multiple_of(kb * tq, tq), tiles * tq)
            s = _dot(qv, k_ref[keys, :], NT)
            if masked:
                s = jnp.where(_chunk_mask(s.shape, 0), s, NEG_INF)
            m_new = jnp.maximum(m, jnp.max(s, axis=-1, keepdims=True))
            alpha = jnp.exp2(m - m_new)
            p = jnp.exp2(s - m_new)
            l = alpha * l + jnp.sum(p, axis=-1, keepdims=True)
            acc = alpha * acc + _dot(p.astype(BF16), v_ref[keys, :])
            return m_new, l, acc

        init = (jnp.full((tq, 1), NEG_INF, F32), jnp.zeros((tq, 1), F32), jnp.zeros((tq, V_HEAD), F32))
        carry = lax.fori_loop(0, i // 2, lambda pb, cr: step(2 * pb, cr, False, 2), init)
        carry = lax.fori_loop(0, i % 2, lambda _, cr: step(i - 1, cr, False), carry)
        m, l, acc = step(i, carry, True)
        o_ref[...] = (acc / l).astype(BF16)
        lse_ref[0] = m + jnp.log2(l)

    return _call_with_rider(
        body, rider, name="attn_fwd", grid=(MLA_HEADS, t // tq),
        in_specs=[pl.BlockSpec((tq, HEAD_PAD), lambda h, i: (i, h)),
                  pl.BlockSpec((t, HEAD_PAD), lambda h, i: (0, h)),
                  pl.BlockSpec((t, V_HEAD), lambda h, i: (0, h))],
        out_specs=[pl.BlockSpec((tq, V_HEAD), lambda h, i: (i, h)),
                   pl.BlockSpec((1, tq, 1), lambda h, i: (h, i, 0))],
        out_shape=[jax.ShapeDtypeStruct((t, MLA_WIDTH), BF16), jax.ShapeDtypeStruct((MLA_HEADS, t, 1), F32)],
        scratch_shapes=[], operands=(q, k, v))


def attention_backward(q, k, v, do, lse, delta, rider=None):
    t = q.shape[0]
    tq = _tile(t, ATTN_TILE, CHUNK)
    nq = t // tq

    def body(q_ref, k_ref, v_ref, do_ref, lse_ref, delta_ref, dq_ref, dk_ref, dv_ref, dq_acc):
        kb = pl.program_id(1)

        @pl.when(kb == 0)
        def _():
            dq_acc[...] = jnp.zeros_like(dq_acc)

        kv, vv = k_ref[...], v_ref[...]

        def step(qb, carry, masked):
            dk, dv = carry
            rows = pl.ds(pl.multiple_of(qb * tq, tq), tq)
            qv, dov = q_ref[rows, :], do_ref[rows, :]
            s = _dot(kv, qv, NT)
            if masked:
                s = jnp.where(_chunk_mask(s.shape, 1), s, NEG_INF)
            p = jnp.exp2(s - lse_ref[0, qb])
            dv = dv + _dot(p.astype(BF16), dov)
            dp = _dot(vv, dov, NT)
            ds = (p * (dp - delta_ref[0, qb]) * LN_2).astype(BF16)
            dk = dk + _dot(ds, qv)
            dq_acc[rows, :] += _dot(ds, kv, TN)
            return dk, dv

        carry = step(kb, (jnp.zeros((tq, HEAD_PAD), F32), jnp.zeros((tq, V_HEAD), F32)), True)
        odd = (nq - 1 - kb) % 2
        carry = lax.fori_loop(0, odd, lambda _, cr: step(kb + 1, cr, False), carry)
        first = kb + 1 + odd
        dk, dv = lax.fori_loop(0, (nq - first) // 2,
                               lambda pb, cr: step(first + 2 * pb + 1, step(first + 2 * pb, cr, False), False), carry)
        dk_ref[...] = dk.astype(BF16)
        dv_ref[...] = dv.astype(BF16)

        @pl.when(kb == nq - 1)
        def _():
            dq_ref[...] = dq_acc[...].astype(BF16)

    stat = pl.BlockSpec((1, nq, 1, tq), lambda h, j: (h, 0, 0, 0))
    return _call_with_rider(
        body, rider, name="attn_bwd", grid=(MLA_HEADS, nq),
        in_specs=[pl.BlockSpec((t, HEAD_PAD), lambda h, j: (0, h)),
                  pl.BlockSpec((tq, HEAD_PAD), lambda h, j: (j, h)),
                  pl.BlockSpec((tq, V_HEAD), lambda h, j: (j, h)),
                  pl.BlockSpec((t, V_HEAD), lambda h, j: (0, h)), stat, stat],
        out_specs=[pl.BlockSpec((t, HEAD_PAD), lambda h, j: (0, h)),
                   pl.BlockSpec((tq, HEAD_PAD), lambda h, j: (j, h)),
                   pl.BlockSpec((tq, V_HEAD), lambda h, j: (j, h))],
        out_shape=[jax.ShapeDtypeStruct((t, QK_COLS), BF16), jax.ShapeDtypeStruct((t, QK_COLS), BF16),
                   jax.ShapeDtypeStruct((t, MLA_WIDTH), BF16)],
        scratch_shapes=[pltpu.VMEM((t, HEAD_PAD), F32)], operands=(q, k, v, do, lse, delta))


HALO = 16


def _halo_spec(tm, n, step, last):
    return pl.BlockSpec((HALO, n), lambda i: (jnp.clip(i * (tm // HALO) + step, 0, last), 0))


def _shift_rows(v, prev, n):
    out = pltpu.roll(v, n, 0)
    row = lax.broadcasted_iota(jnp.int32, v.shape, 0)
    for r in range(n):
        out = jnp.where(row == r, prev[HALO - n + r:HALO - n + r + 1, :], out)
    return out


def _advance_rows(v, nxt, n):
    rows = v.shape[0]
    out = pltpu.roll(v, rows - n, 0)
    row = lax.broadcasted_iota(jnp.int32, v.shape, 0)
    for r in range(n):
        out = jnp.where(row == rows - n + r, nxt[r:r + 1, :], out)
    return out


def _conv_taps(zc, zc_prev, first):
    w = CONV_WIDTH
    u = zc[:, w:2 * w] * zc[:, 2 * w:]
    up = jnp.where(first, 0.0, zc_prev[:, w:2 * w] * zc_prev[:, 2 * w:])
    return u, _shift_rows(u, up, 1), _shift_rows(u, up, 2)


def mix_out_forward(zc, o, conv_w, og, gmat_a, gmat_b, w_out, x, gate):
    t, d = x.shape
    tm = _tile(t, ROW_TILE, 16)
    w = CONV_WIDTH

    def body(zc_ref, zp_ref, o_ref, cw_ref, og_ref, ga_ref, gb_ref, w_ref, x_ref, gate_ref,
             xo_ref, yn_ref, y_ref, ya_ref):
        zc_v = zc_ref[...].astype(F32)
        u, u1, u2 = _conv_taps(zc_v, zp_ref[...].astype(F32), pl.program_id(0) == 0)
        cw = cw_ref[...]
        ya = zc_v[:, :w] * (cw[0:1] * u2 + cw[1:2] * u1 + cw[2:3] * u)
        ya_ref[...] = ya.astype(BF16)
        ov = o_ref[...].astype(F32)
        ogv = og_ref[...]
        yn_ref[:, :w] = (ya * lax.rsqrt(_group_mean(ya * ya, ga_ref[...]) + EPS) * ogv[:, :w]).astype(BF16)
        yn_ref[:, w:] = (ov * lax.rsqrt(_group_mean(ov * ov, gb_ref[...]) + EPS) * ogv[:, w:]).astype(BF16)
        y = _dot(yn_ref[...], w_ref[...])
        y_ref[...] = y.astype(BF16)
        xo_ref[...] = x_ref[...] + gate_ref[...] * y

    def rows(n):
        return pl.BlockSpec((tm, n), lambda i: (i, 0))

    return pl.pallas_call(
        body, name="mix_out_fwd", grid=(t // tm,),
        in_specs=[rows(ZC_COLS), _halo_spec(tm, ZC_COLS, -1, t // HALO - 1), rows(MLA_WIDTH), _row(conv_w), _row(og),
                  _row(gmat_a), _row(gmat_b), _row(w_out), rows(d), _row(gate)],
        out_specs=[rows(d), rows(MIX_WIDTH), rows(d), rows(w)],
        out_shape=[jax.ShapeDtypeStruct((t, d), F32), jax.ShapeDtypeStruct((t, MIX_WIDTH), BF16),
                   jax.ShapeDtypeStruct((t, d), BF16), jax.ShapeDtypeStruct((t, w), BF16)],
        compiler_params=_params(("arbitrary",)),
    )(zc, zc, o, conv_w, og, gmat_a, gmat_b, w_out, x, gate)


def _group_norm_bwd(dyn, y, og, gmat):
    rs = lax.rsqrt(_group_mean(y * y, gmat) + EPS)
    yhat = y * rs
    d_og = jnp.sum(dyn * yhat, axis=0, keepdims=True)
    dyh = dyn * og
    return rs * (dyh - yhat * _group_mean(dyh * yhat, gmat)), d_og


def mix_out_backward(dxo, y, gate, ya, o, og, gmat_a, gmat_b, w_out, rider=None):
    t, d = dxo.shape
    tm = _tile(t, ROW_TILE, 16)
    w = CONV_WIDTH

    def body(dxo_ref, y_ref, gate_ref, ya_ref, o_ref, og_ref, ga_ref, gb_ref, w_ref,
             dy_ref, dya_ref, do_ref, delta_ref, sd_ref, so_ref):
        @pl.when(pl.program_id(0) == 0)
        def _():
            sd_ref[...] = jnp.zeros_like(sd_ref)
            so_ref[...] = jnp.zeros_like(so_ref)

        dxo_v = dxo_ref[...]
        dy = (gate_ref[...] * dxo_v).astype(BF16)
        dy_ref[...] = dy
        sd_ref[0:1, :] += jnp.sum(dxo_v * y_ref[...].astype(F32), axis=0, keepdims=True)
        dyn = _dot(dy, w_ref[...], NT)
        ogv = og_ref[...]
        ov = o_ref[...].astype(F32)
        dya, d_og_a = _group_norm_bwd(dyn[:, :w], ya_ref[...].astype(F32), ogv[:, :w], ga_ref[...])
        dov, d_og_b = _group_norm_bwd(dyn[:, w:], ov, ogv[:, w:], gb_ref[...])
        dya_ref[...] = dya.astype(BF16)
        do_ref[...] = dov.astype(BF16)
        so_ref[0:1, :w] += d_og_a
        so_ref[0:1, w:] += d_og_b
        prod = dov * ov
        for h in range(MLA_HEADS):
            delta_ref[h] = jnp.sum(prod[:, h * V_HEAD:(h + 1) * V_HEAD], axis=-1, keepdims=True)

    def rows(n):
        return pl.BlockSpec((tm, n), lambda i: (i, 0))

    return _call_with_rider(
        body, rider, name="mix_out_bwd", grid=(t // tm,),
        in_specs=[rows(d), rows(d), _row(gate), rows(w), rows(MLA_WIDTH), _row(og), _row(gmat_a), _row(gmat_b),
                  _row(w_out)],
        out_specs=[rows(d), rows(w), rows(MLA_WIDTH), pl.BlockSpec((MLA_HEADS, tm, 1), lambda i: (0, i, 0)),
                   pl.BlockSpec((8, d), lambda i: (0, 0)), pl.BlockSpec((8, MIX_WIDTH), lambda i: (0, 0))],
        out_shape=[jax.ShapeDtypeStruct((t, d), BF16), jax.ShapeDtypeStruct((t, w), BF16),
                   jax.ShapeDtypeStruct((t, MLA_WIDTH), BF16), jax.ShapeDtypeStruct((MLA_HEADS, t, 1), F32),
                   jax.ShapeDtypeStruct((8, d), F32), jax.ShapeDtypeStruct((8, MIX_WIDTH), F32)],
        scratch_shapes=[], operands=(dxo, y, gate, ya, o, og, gmat_a, gmat_b, w_out))


def conv_backward(zc, dya, conv_w):
    t = zc.shape[0]
    tm = _tile(t, ROW_TILE, 16)
    nt = t // tm
    w = CONV_WIDTH

    def body(zc_ref, zp_ref, zn_ref, dya_ref, dn_ref, cw_ref, dzc_ref, sums_ref):
        i = pl.program_id(0)

        @pl.when(i == 0)
        def _():
            sums_ref[...] = jnp.zeros_like(sums_ref)

        zc_v = zc_ref[...].astype(F32)
        u, u1, u2 = _conv_taps(zc_v, zp_ref[...].astype(F32), i == 0)
        cw = cw_ref[...]
        dya_v = dya_ref[...].astype(F32)
        dyc = dya_v * zc_v[:, :w]
        dyc_next = jnp.where(i == nt - 1, 0.0, dn_ref[...].astype(F32) * zn_ref[:, :w].astype(F32))
        du = cw[2:3] * dyc + cw[1:2] * _advance_rows(dyc, dyc_next, 1) + cw[0:1] * _advance_rows(dyc, dyc_next, 2)
        dzc_ref[:, :w] = (dya_v * (cw[0:1] * u2 + cw[1:2] * u1 + cw[2:3] * u)).astype(BF16)
        dzc_ref[:, w:2 * w] = (du * zc_v[:, 2 * w:]).astype(BF16)
        dzc_ref[:, 2 * w:] = (du * zc_v[:, w:2 * w]).astype(BF16)
        _add_rows(sums_ref, [jnp.sum(dyc * tap, axis=0, keepdims=True) for tap in (u2, u1, u)])

    def rows(n):
        return pl.BlockSpec((tm, n), lambda i: (i, 0))

    def halo(n, step):
        return _halo_spec(tm, n, step, t // HALO - 1)

    return pl.pallas_call(
        body, name="conv_bwd", grid=(nt,),
        in_specs=[rows(ZC_COLS), halo(ZC_COLS, -1), halo(ZC_COLS, tm // HALO), rows(w), halo(w, tm // HALO),
                  _row(conv_w)],
        out_specs=[rows(ZC_COLS), pl.BlockSpec((8, w), lambda i: (0, 0))],
        out_shape=[jax.ShapeDtypeStruct((t, ZC_COLS), BF16), jax.ShapeDtypeStruct((8, w), F32)],
        compiler_params=_params(("arbitrary",)),
    )(zc, zc, zc, dya, dya, conv_w)


def _rms_bwd(dy, x, g):
    xhat, r = _rms(x)
    d_g = jnp.sum(dy * xhat, axis=0, keepdims=True)
    dxh = dy * g
    return r * (dxh - xhat * jnp.mean(dxh * xhat, axis=-1, keepdims=True)), d_g


def mla_project_backward(dq, dk, dv, zm, pos, inv_freq, qg, kvg, w_uq, w_ukv):
    t = zm.shape[0]
    tm = _tile(t, ROW_TILE, 16)

    def body(dq_ref, dk_ref, dv_ref, zm_ref, pos_ref, if_ref, qg_ref, kvg_ref, wq_ref, wkv_ref,
             dql_ref, dkvl_ref, dzm_ref, sums_ref):
        @pl.when(pl.program_id(0) == 0)
        def _():
            sums_ref[...] = jnp.zeros_like(sums_ref)

        tables = _rope_tables(pos_ref[...], if_ref[...])
        dkr = jnp.zeros((tm, LANES), F32)
        for h in range(MLA_HEADS):
            lo = h * HEAD_PAD
            dql_ref[:, lo:lo + QK_NOPE] = (dq_ref[:, lo:lo + QK_NOPE].astype(F32) * QK_FOLD).astype(BF16)
            dql_ref[:, lo + QK_NOPE:lo + HEAD_PAD] = _rope_transposed(
                dq_ref[:, lo + QK_NOPE:lo + HEAD_PAD].astype(F32) * QK_FOLD, tables).astype(BF16)
            dkvl_ref[:, h * QK_NOPE:(h + 1) * QK_NOPE] = dk_ref[:, lo:lo + QK_NOPE]
            dkr = dkr + dk_ref[:, lo + QK_NOPE:lo + HEAD_PAD].astype(F32)
        dkvl_ref[:, MLA_HEADS * QK_NOPE:] = dv_ref[...]
        zv = zm_ref[...].astype(F32)
        dqn = _dot(dql_ref[...], wq_ref[...])
        dkvn = _dot(dkvl_ref[...], wkv_ref[...])
        dcq, d_qg = _rms_bwd(dqn, zv[:, :Q_LORA], qg_ref[...])
        dckv, d_kvg = _rms_bwd(dkvn, zv[:, Q_LORA:Q_LORA + KV_LORA], kvg_ref[...])
        dzm_ref[:, :Q_LORA] = dcq.astype(BF16)
        dzm_ref[:, Q_LORA:Q_LORA + KV_LORA] = dckv.astype(BF16)
        dzm_ref[:, Q_LORA + KV_LORA:] = _rope_transposed(dkr, tables).astype(BF16)
        sums_ref[0:1, :Q_LORA] += d_qg
        sums_ref[0:1, Q_LORA:Q_LORA + KV_LORA] += d_kvg

    def rows(n):
        return pl.BlockSpec((tm, n), lambda i: (i, 0))

    return pl.pallas_call(
        body, name="mla_project_bwd", grid=(t // tm,),
        in_specs=[rows(QK_COLS), rows(QK_COLS), rows(MLA_WIDTH), rows(ZM_COLS), rows(1), _row(inv_freq),
                  _row(qg), _row(kvg), _row(w_uq), _row(w_ukv)],
        out_specs=[rows(QK_COLS), rows(QK_COLS), rows(ZM_COLS), pl.BlockSpec((8, ZM_COLS), lambda i: (0, 0))],
        out_shape=[jax.ShapeDtypeStruct((t, QK_COLS), BF16), jax.ShapeDtypeStruct((t, QK_COLS), BF16),
                   jax.ShapeDtypeStruct((t, ZM_COLS), BF16), jax.ShapeDtypeStruct((8, ZM_COLS), F32)],
        compiler_params=_params(("arbitrary",)),
    )(dq, dk, dv, zm, pos, inv_freq, qg, kvg, w_uq, w_ukv)


def mix_in_backward(dzc, dzm, w_in, x, dxo, gn, sc, gate, rider=None):
    t, d = x.shape
    tm = _tile(t, ROW_TILE, 16)

    def body(dzc_ref, dzm_ref, w_ref, x_ref, dxo_ref, gn_ref, sc_ref, gate_ref, dx_ref, dy_ref, sums_ref):
        @pl.when(pl.program_id(0) == 0)
        def _():
            sums_ref[...] = jnp.zeros_like(sums_ref)

        dh = _dot(dzc_ref[...], w_ref[:ZC_COLS, :]) + _dot(dzm_ref[...], w_ref[ZC_COLS:, :])
        dx, d_sh, d_sc, d_gn = _norm_mod_bwd(dh, x_ref[...], gn_ref[...], sc_ref[...])
        dx = dxo_ref[...] + dx
        dx_ref[...] = dx
        dy_ref[...] = (0.5 * gate_ref[...] * dx).astype(BF16)
        _add_rows(sums_ref, [d_sh, d_sc, d_gn])

    def rows(n):
        return pl.BlockSpec((tm, n), lambda i: (i, 0))

    return _call_with_rider(
        body, rider, name="mix_in_bwd", grid=(t // tm,),
        in_specs=[rows(ZC_COLS), rows(ZM_COLS), _row(w_in), rows(d), rows(d), _row(gn), _row(sc), _row(gate)],
        out_specs=[rows(d), rows(d), pl.BlockSpec((8, d), lambda i: (0, 0))],
        out_shape=[jax.ShapeDtypeStruct((t, d), F32), jax.ShapeDtypeStruct((t, d), BF16),
                   jax.ShapeDtypeStruct((8, d), F32)],
        scratch_shapes=[], operands=(dzc, dzm, w_in, x, dxo, gn, sc, gate))


def final_loss(x, target, g, gate):
    t, d = x.shape
    tm = _tile(t, ROW_TILE, 16)

    def body(x_ref, t_ref, g_ref, gate_ref, dx_ref, dy_ref, sums_ref):
        @pl.when(pl.program_id(0) == 0)
        def _():
            sums_ref[...] = jnp.zeros_like(sums_ref)

        gv = g_ref[...]
        xhat, r = _rms(x_ref[...])
        err = xhat * gv - t_ref[...]
        dyf = err * (1.0 / d)
        dxh = dyf * gv
        dx = r * (dxh - xhat * jnp.mean(dxh * xhat, axis=-1, keepdims=True))
        dx_ref[...] = dx
        dy_ref[...] = (0.5 * gate_ref[...] * dx).astype(BF16)
        _add_rows(sums_ref, [jnp.sum(dyf * xhat, axis=0, keepdims=True),
                             jnp.sum(err * err, axis=0, keepdims=True) * (0.5 / d)])

    row = pl.BlockSpec((tm, d), lambda i: (i, 0))
    return pl.pallas_call(
        body, name="final_loss", grid=(t // tm,),
        in_specs=[row, row, _row(g), _row(gate)],
        out_specs=[row, row, pl.BlockSpec((8, d), lambda i: (0, 0))],
        out_shape=[jax.ShapeDtypeStruct((t, d), F32), jax.ShapeDtypeStruct((t, d), BF16),
                   jax.ShapeDtypeStruct((8, d), F32)],
        compiler_params=_params(("arbitrary",)),
    )(x, target, g, gate)


def adamw(w, g, m, v, name):
    r, n = w.shape
    tr = _tile(r, max(8, (1 << 19) // n), 8)

    def body(w_ref, g_ref, m_ref, v_ref, d_ref, mo_ref, vo_ref):
        gv = g_ref[...]
        m_new = ADAM_B1 * m_ref[...] + (1.0 - ADAM_B1) * gv
        v_new = ADAM_B2 * v_ref[...] + (1.0 - ADAM_B2) * (gv * gv)
        m_hat = m_new / (1.0 - ADAM_B1 ** ADAM_STEP)
        v_hat = v_new / (1.0 - ADAM_B2 ** ADAM_STEP)
        d_ref[...] = -ADAM_LR * (m_hat / (jnp.sqrt(v_hat) + ADAM_EPS) + ADAM_WD * w_ref[...])
        mo_ref[...] = m_new
        vo_ref[...] = v_new

    blk = pl.BlockSpec((tr, n), lambda i: (i, 0))
    shape = jax.ShapeDtypeStruct((r, n), F32)
    return pl.pallas_call(
        body, name=name, grid=(r // tr,), in_specs=[blk] * 4, out_specs=[blk] * 3, out_shape=[shape] * 3,
        compiler_params=_params(("arbitrary",)),
    )(w, g, m, v)


def _pad_to(v, n):
    return jnp.pad(v, (0, n - v.shape[0]))


def _pad_heads(w, axis_len):
    n = w.shape[1]
    return jnp.pad(w.reshape(MLA_HEADS, axis_len, n), ((0, 0), (0, HEAD_PAD - axis_len), (0, 0))).reshape(-1, n)


def _swap_head_parts(w, inner, outer):
    n = w.shape[1]
    return w.reshape(outer, inner, QK_NOPE, n).transpose(1, 0, 2, 3).reshape(-1, n)


def kernel(x, c, positions, ada_w, ada_b, norm_ffn1_g, ffn1_w1, ffn1_w3, ffn1_w2, norm_mix_g, w_in, conv_w, q_norm_g, w_uq, kv_norm_g, w_ukv, out_norm_g, w_out, norm_ffn2_g, ffn2_w1, ffn2_w3, ffn2_w2, final_norm_g, loss_target, m_ada_w, m_ada_b, m_norm_ffn1_g, m_ffn1_w1, m_ffn1_w3, m_ffn1_w2, m_norm_mix_g, m_w_in, m_conv_w, m_q_norm_g, m_w_uq, m_kv_norm_g, m_w_ukv, m_out_norm_g, m_w_out, m_norm_ffn2_g, m_ffn2_w1, m_ffn2_w3, m_ffn2_w2, m_final_norm_g, v_ada_w, v_ada_b, v_norm_ffn1_g, v_ffn1_w1, v_ffn1_w3, v_ffn1_w2, v_norm_mix_g, v_w_in, v_conv_w, v_q_norm_g, v_w_uq, v_kv_norm_g, v_w_ukv, v_out_norm_g, v_w_out, v_norm_ffn2_g, v_ffn2_w1, v_ffn2_w3, v_ffn2_w2, v_final_norm_g):
    t, d = x.shape[1], x.shape[2]
    f = ffn1_w2.shape[1] * N_DEV
    me = 4 * lax.axis_index("x") + 2 * lax.axis_index("y") + lax.axis_index("c")
    my_c = lax.axis_index("c")
    my_chip = 2 * lax.axis_index("x") + lax.axis_index("y")
    xs = x[0]
    n_ada = ada_w.shape[2]
    cw_n = conv_w.shape[2]

    c_rows = jnp.broadcast_to(c, (8, d))
    conv_rows = jnp.pad(conv_w[0], ((0, 8 - CONV_K), (0, LANES - cw_n)))
    ffn1_blocks = jnp.stack([ffn1_w1[0].T, ffn1_w3[0].T, ffn1_w2[0]]).astype(BF16)
    ffn2_blocks = jnp.stack([ffn2_w1[0].T, ffn2_w3[0].T, ffn2_w2[0]]).astype(BF16)
    c_all, conv_all, ffn1_all = all_gather_relayed([c_rows, conv_rows, ffn1_blocks], [0, 0, 1], "gather_first")
    c_all = c_all[:, 0, :]
    conv_full8 = conv_all[:, :, :cw_n].transpose(1, 0, 2).reshape(8, CONV_WIDTH)
    ffn1_ws = ffn1_all.reshape(3, f, d)
    gather_mix = riding_gather(
        [w_in[0].T.astype(BF16), w_uq[0].T.astype(BF16), w_ukv[0].T.astype(BF16), w_out[0].astype(BF16)], [0, 0, 0, 0])

    ada_b_cols = lax.dynamic_slice_in_dim(ada_b, me * n_ada, n_ada, axis=1)
    mod_cols = ada_forward(c_all, ada_w[0], ada_b_cols)
    mod_all, = all_gather([mod_cols], [0], "gather_mod")
    mod = lax.dynamic_index_in_dim(mod_all, me, axis=1, keepdims=False).reshape(N_MOD, 1, d)
    sh1, sc1, g1, sh2, sc2, g2, sh3, sc3, g3 = [mod[i] for i in range(N_MOD)]

    gf = final_norm_g.reshape(1, d)
    x1, h1, a1, b1, y1, *gathered = ffn_forward(xs, norm_ffn1_g, sc1, sh1, g1, ffn1_ws, 0, "ffn1_fwd", gather_mix)
    w_in_p = jnp.pad(gathered[0].reshape(IN_COLS, d), ((0, ZC_COLS + ZM_COLS - IN_COLS), (0, 0)))
    w_uq_p = _pad_heads(gathered[1].reshape(-1, Q_LORA), QK_NOPE + QK_ROPE)
    w_ukv_p = _swap_head_parts(gathered[2].reshape(-1, KV_LORA), 2, MLA_HEADS)
    w_out_f = gathered[3].reshape(MIX_WIDTH, d)
    h2, zc, zm = mix_in_forward(x1, norm_mix_g, sc2, sh2, w_in_p)
    pos = positions[0].astype(F32).reshape(t, 1)
    inv_freq = ROPE_THETA ** (-jnp.arange(0, QK_ROPE, 2, dtype=F32) / QK_ROPE)
    inv_freq = jnp.concatenate([inv_freq, inv_freq, jnp.zeros((LANES - QK_ROPE,), F32)]).reshape(1, LANES)
    qn, kvn, q, k, v = mla_project(zm, pos, inv_freq, q_norm_g, kv_norm_g, w_uq_p, w_ukv_p)
    o, lse, ffn2_all = attention_forward(q, k, v, riding_gather([ffn2_blocks], [1]))
    ffn2_ws = ffn2_all.reshape(3, f, d)
    lane = jnp.arange(CONV_WIDTH)
    gmat_a = (lane[:, None] // (CONV_WIDTH // CONV_GROUPS) == lane[None, :] // (CONV_WIDTH // CONV_GROUPS))
    gmat_a = (gmat_a / (CONV_WIDTH // CONV_GROUPS)).astype(BF16)
    gmat_b = ((lane[:, None] // V_HEAD == lane[None, :] // V_HEAD) / V_HEAD).astype(BF16)
    x2, yn, y2, ya = mix_out_forward(zc, o, conv_full8, out_norm_g, gmat_a, gmat_b, w_out_f, x1, g2)
    x3, h3, a3, b3, y3 = ffn_forward(x2, norm_ffn2_g, sc3, sh3, g3, ffn2_ws, 0, "ffn2_fwd")
    dx3, dy3, sums_f = final_loss(x3, loss_target[0], gf, g3)

    chip_idx = jnp.bitwise_xor(my_chip, jnp.array([0, 2, 1, 3], jnp.int32)).astype(jnp.int32)
    src_idx = (2 * chip_idx + my_c).astype(jnp.int32)

    def row_blocks(named):
        return [g.reshape(N_DEV, g.shape[0] // N_DEV, g.shape[1]) for _, g in named]

    def chip_sums(named, g8, got):
        return [add_sibling(g, r, src_idx, chip_idx, "rs_add_" + n) for g, r, (n, _) in zip(g8, got, named)]

    da3, db3, u3 = ffn_backward_gate(dy3, a3, b3, ffn2_ws, 0, "ffn2_bwd_gate")
    dx2, sums_3 = ffn_backward_norm(da3, db3, dx3, x2, y3, norm_ffn2_g, sc3, ffn2_ws, 0, "ffn2_bwd_norm")
    ffn2_named = [("ffn2_w1", matmul_tn(da3, h3, "ffn2_gw1")), ("ffn2_w3", matmul_tn(db3, h3, "ffn2_gw3")),
                  ("ffn2_w2", matmul_tn(u3, dy3, "ffn2_gw2"))]
    ffn2_g8 = row_blocks(ffn2_named)
    dy2, dya, do, delta, sums_2d, sums_2o, *ffn2_sib = mix_out_backward(
        dx2, y2, g2, ya, o, out_norm_g, gmat_a, gmat_b, w_out_f, riding_sibling(ffn2_g8))
    ffn2_sums = chip_sums(ffn2_named, ffn2_g8, ffn2_sib)
    g_w_out = matmul_tn(yn, dy2, "gw_out")
    nq = t // _tile(t, ATTN_TILE, CHUNK)
    stat_shape = (MLA_HEADS, nq, 1, t // nq)
    dq, dk, dv, *ffn2_got = attention_backward(q, k, v, do, lse.reshape(stat_shape), delta.reshape(stat_shape),
                                               riding_exchange([s[1] for s in ffn2_sums]))
    dzc, sums_c = conv_backward(zc, dya, conv_full8)
    dql, dkvl, dzm, sums_m = mla_project_backward(dq, dk, dv, zm, pos, inv_freq, q_norm_g, kv_norm_g, w_uq_p, w_ukv_p)
    g_w_uq_p = matmul_tn(dql, qn, "gw_uq")
    g_w_ukv_p = matmul_tn(dkvl, kvn, "gw_ukv")
    g_w_in = matmul_tn([dzc, dzm], h2, "gw_in")[:IN_COLS]
    g_w_uq = g_w_uq_p.reshape(MLA_HEADS, HEAD_PAD, Q_LORA)[:, :QK_NOPE + QK_ROPE].reshape(-1, Q_LORA)
    g_w_ukv = _swap_head_parts(g_w_ukv_p, MLA_HEADS, 2)
    mix_named = [("w_in", g_w_in), ("w_uq", g_w_uq), ("w_ukv", g_w_ukv), ("w_out", g_w_out)]
    mix_g8 = row_blocks(mix_named)
    dx1, dy1, sums_1m, *mix_sib = mix_in_backward(dzc, dzm, w_in_p, x1, dx2, norm_mix_g, sc2, g1, riding_sibling(mix_g8))
    mix_sums = chip_sums(mix_named, mix_g8, mix_sib)
    da1, db1, u1, *mix_got = ffn_backward_gate(dy1, a1, b1, ffn1_ws, 0, "ffn1_bwd_gate",
                                               riding_exchange([s[1] for s in mix_sums]))
    ffn1_pair = [("ffn1_w1", matmul_tn(da1, h1, "ffn1_gw1")), ("ffn1_w3", matmul_tn(db1, h1, "ffn1_gw3"))]
    pair_g8 = row_blocks(ffn1_pair)
    g_w2a, *pair_sib = matmul_tn(u1, dy1, "ffn1_gw2", riding_sibling(pair_g8))
    ffn1_last = [("ffn1_w2", g_w2a)]
    last_g8 = row_blocks(ffn1_last)
    ffn1_named = ffn1_pair + ffn1_last
    ffn1_sums = chip_sums(ffn1_pair, pair_g8, pair_sib) + chip_sums(
        ffn1_last, last_g8, exchange_sibling(last_g8, "rs_sibling_ffn1_w2"))
    dx0, sums_1, *ffn1_got = ffn_backward_norm(da1, db1, dx1, xs, y1, norm_ffn1_g, sc1, ffn1_ws, 0, "ffn1_bwd_norm",
                                               riding_exchange([s[1] for s in ffn1_sums]))
    g_sh = {}
    for named, group_sums, group_got in ((ffn2_named, ffn2_sums, ffn2_got), (mix_named, mix_sums, mix_got),
                                         (ffn1_named, ffn1_sums, ffn1_got)):
        for (n, _), (own, _), got in zip(named, group_sums, group_got):
            g_sh[n] = add_received(own, got, "rs_sum_" + n)

    dmod = jnp.concatenate([sums_1[0], sums_1[1], sums_1[2], sums_1m[0], sums_1m[1], sums_2d[0],
                            sums_3[0], sums_3[1], sums_3[2]])
    pieces = [dmod, sums_1[3], sums_1m[2], sums_m[0, :Q_LORA], sums_m[0, Q_LORA:Q_LORA + KV_LORA], sums_2o[0],
              sums_3[3], sums_f[0], sums_f[1], sums_c[:CONV_K].reshape(-1)]
    plens = [p.shape[0] for p in pieces]
    poffs = [sum(plens[:i]) for i in range(len(plens))]
    vec_len = -(-sum(plens) // 1024) * 1024
    vec = _pad_to(jnp.concatenate(pieces), vec_len).reshape(-1, LANES)
    vec_all, = all_gather([vec], [0], "gather_sums")
    tot = sum_devices(vec_all).reshape(-1)
    g_ada_b, g_n1, g_nmix, g_qg, g_kvg, g_og, g_n3, g_gf, loss_lanes, g_conv_full = [
        tot[o:o + n] for o, n in zip(poffs, plens)]
    loss = sum_lanes(loss_lanes.reshape(1, d))[0, 0]
    g_conv = lax.dynamic_slice_in_dim(g_conv_full.reshape(CONV_K, CONV_WIDTH), me * cw_n, cw_n, axis=1)
    dmod_all = vec_all.reshape(N_DEV, vec_len)[:, :N_MOD * d]
    dmod_cols = lax.dynamic_slice_in_dim(dmod_all, me * n_ada, n_ada, axis=1)
    g_ada_w = ada_backward(jnp.pad(c_all, ((0, 8), (0, 0))), jnp.pad(dmod_cols, ((0, 8), (0, 0))))

    def update(name, w, g, m, v):
        k, n = w.shape[-2:]
        if g.shape == (k, n):
            flat, back = (lambda a: a.reshape(k, n)), (lambda a: a.reshape(w.shape))
        else:
            flat, back = (lambda a: a.reshape(k, n).T), (lambda a: a.T.reshape(w.shape))
        dlt, nm, nv = adamw(flat(w), g, flat(m), flat(v), "adamw_" + name)
        return back(g), back(dlt), back(nm), back(nv)

    res = {}
    res["ada_w"] = update("ada_w", ada_w, g_ada_w, m_ada_w, v_ada_w)
    big = [("ffn1_w1", ffn1_w1, m_ffn1_w1, v_ffn1_w1), ("ffn1_w3", ffn1_w3, m_ffn1_w3, v_ffn1_w3),
           ("ffn2_w1", ffn2_w1, m_ffn2_w1, v_ffn2_w1), ("ffn2_w3", ffn2_w3, m_ffn2_w3, v_ffn2_w3),
           ("w_in", w_in, m_w_in, v_w_in), ("w_uq", w_uq, m_w_uq, v_w_uq), ("w_ukv", w_ukv, m_w_ukv, v_w_ukv),
           ("ffn1_w2", ffn1_w2, m_ffn1_w2, v_ffn1_w2), ("ffn2_w2", ffn2_w2, m_ffn2_w2, v_ffn2_w2),
           ("w_out", w_out, m_w_out, v_w_out)]
    for name, w, m, v in big:
        res[name] = update(name, w, g_sh[name], m, v)
    smalls = [("ada_b", ada_b, g_ada_b, m_ada_b, v_ada_b),
              ("norm_ffn1_g", norm_ffn1_g, g_n1, m_norm_ffn1_g, v_norm_ffn1_g),
              ("norm_mix_g", norm_mix_g, g_nmix, m_norm_mix_g, v_norm_mix_g),
              ("conv_w", conv_w, g_conv, m_conv_w, v_conv_w),
              ("q_norm_g", q_norm_g, g_qg, m_q_norm_g, v_q_norm_g),
              ("kv_norm_g", kv_norm_g, g_kvg, m_kv_norm_g, v_kv_norm_g),
              ("out_norm_g", out_norm_g, g_og, m_out_norm_g, v_out_norm_g),
              ("norm_ffn2_g", norm_ffn2_g, g_n3, m_norm_ffn2_g, v_norm_ffn2_g),
              ("final_norm_g", final_norm_g, g_gf, m_final_norm_g, v_final_norm_g)]
    slens = [w.size for _, w, _, _, _ in smalls]
    soffs = [sum(slens[:i]) for i in range(len(slens))]
    s_len = -(-sum(slens) // 1024) * 1024

    def pack_small(i):
        return _pad_to(jnp.concatenate([s[i].reshape(-1) for s in smalls]), s_len).reshape(8, -1)

    s_out = adamw(pack_small(1), pack_small(2), pack_small(3), pack_small(4), "adamw_small")
    for (name, w, g, _, _), o, n in zip(smalls, soffs, slens):
        res[name] = (g.reshape(w.shape),) + tuple(a.reshape(-1)[o:o + n].reshape(w.shape) for a in s_out)

    order = ["ada_w", "ada_b", "norm_ffn1_g", "ffn1_w1", "ffn1_w3", "ffn1_w2", "norm_mix_g", "w_in", "conv_w",
             "q_norm_g", "w_uq", "kv_norm_g", "w_ukv", "out_norm_g", "w_out", "norm_ffn2_g", "ffn2_w1", "ffn2_w3",
             "ffn2_w2", "final_norm_g"]
    return (loss, dx0.reshape(x.shape), *[res[n][0] for n in order], *[res[n][1] for n in order],
            *[res[n][2] for n in order], *[res[n][3] for n in order])
```

```python
import functools
import math

import jax
import jax.numpy as jnp
from jax import lax
from jax.experimental import pallas as pl
from jax.experimental.pallas import tpu as pltpu

F32 = jnp.float32
BF16 = jnp.bfloat16
MESH_ID = pl.DeviceIdType.MESH
N_DEV = 8

EPS = 1e-6
CHUNK = 64
N_MOD = 9
CONV_WIDTH = 512
CONV_GROUPS = 8
CONV_K = 3
MLA_HEADS = 4
QK_NOPE = 128
QK_ROPE = 64
V_HEAD = 128
Q_LORA = 384
KV_LORA = 256
ROPE_THETA = 10000.0
MLA_WIDTH = MLA_HEADS * V_HEAD
MIX_WIDTH = CONV_WIDTH + MLA_WIDTH
IN_COLS = 3 * CONV_WIDTH + Q_LORA + KV_LORA + QK_ROPE
ZC_COLS = 3 * CONV_WIDTH
ZM_COLS = Q_LORA + KV_LORA + 128
HEAD_PAD = 256
QK_COLS = MLA_HEADS * HEAD_PAD
ATTN_SCALE = (QK_NOPE + QK_ROPE) ** -0.5
LOG2_E = 1.4426950408889634
LN_2 = 0.6931471805599453
QK_FOLD = ATTN_SCALE * LOG2_E
NEG_INF = -1e30

ADAM_LR = 0.001
ADAM_B1 = 0.9
ADAM_B2 = 0.999
ADAM_EPS = 1e-08
ADAM_WD = 0.01
ADAM_STEP = 10

LANES = 128
MXU_COLS = 256
VMEM_LIMIT = 56 * 1024 * 1024
ROW_TILE = 1024
FFN_FWD_TILE = (1024, 256)
FFN_BWD_TILE = (512, 1408)
GRAD_TILE = 1408
GRAD_DEPTH = 2048
SUM_ROWS = 256
ATTN_TILE = 1024

NN = (((1,), (0,)), ((), ()))
NT = (((1,), (1,)), ((), ()))
TN = (((0,), (0,)), ((), ()))


def _dot(a, b, dims=NN):
    return lax.dot_general(a, b, dims, preferred_element_type=F32)


def _tile(n, cap, mult=LANES):
    best = None
    for t in range(mult, min(n, cap) + 1, mult):
        if n % t == 0:
            best = t
    return n if best is None else best


def _params(sem=None):
    return pltpu.CompilerParams(dimension_semantics=sem, vmem_limit_bytes=VMEM_LIMIT)


def _row(v):
    return pl.BlockSpec(v.shape, lambda *_: (0,) * v.ndim)


def _sigmoid(x):
    return 0.5 * jnp.tanh(0.5 * x) + 0.5


def _rms(x):
    r = lax.rsqrt(jnp.mean(x * x, axis=-1, keepdims=True) + EPS)
    return x * r, r


def _norm_mod_bwd(dh, x, gn, sc):
    xhat, r = _rms(x)
    d_sh = jnp.sum(dh, axis=0, keepdims=True)
    d_sc = jnp.sum(dh * (xhat * gn), axis=0, keepdims=True)
    dxn = dh * (1.0 + sc)
    d_gn = jnp.sum(dxn * xhat, axis=0, keepdims=True)
    dxh = dxn * gn
    dx = r * (dxh - xhat * jnp.mean(dxh * xhat, axis=-1, keepdims=True))
    return dx, d_sh, d_sc, d_gn


def _group_mean(v, gmat):
    return _dot(v.astype(BF16), gmat)


def _add_rows(ref, rows):
    for r, v in enumerate(rows):
        ref[r:r + 1, :] += v


def _window(ref, axis, j):
    return ref.at[(slice(None),) * axis + (j,)]


def _any_specs(n):
    return [pl.BlockSpec(memory_space=pl.ANY)] * n


def all_gather(blocks, axes, name):
    n_arr = len(blocks)

    def body(*refs):
        start, forward, finish = _gather_steps(refs[:n_arr], refs[n_arr:2 * n_arr], axes, *refs[2 * n_arr:])
        start()
        for j in range(3):
            forward(j)
        finish()

    return pl.pallas_call(
        body, name=name, out_shape=_gathered_shapes(blocks, axes),
        in_specs=_any_specs(n_arr), out_specs=_any_specs(n_arr), scratch_shapes=_gather_sems(n_arr),
    )(*blocks)


def all_gather_relayed(blocks, axes, name):
    n_arr = len(blocks)
    arrays = range(n_arr)

    def body(*refs):
        ins, outs = refs[:n_arr], refs[n_arr:2 * n_arr]
        send_sems, recv_sems, local_sems = refs[2 * n_arr:]
        x, y, c = lax.axis_index("x"), lax.axis_index("y"), lax.axis_index("c")
        sibling, x_nbr, y_nbr, diagonal = (x, y, 1 - c), (1 - x, y, c), (x, 1 - y, c), (1 - x, 1 - y, c)
        north = c == 1
        relay_slot = jnp.where(north, 1, 2)
        relay_from = tuple(jnp.where(north, a, b) for a, b in zip(x_nbr, y_nbr))
        relay_to = tuple(jnp.where(north, a, b) for a, b in zip(y_nbr, x_nbr))
        other_from = relay_to

        def slot(a, px, py, pc):
            return _window(outs[a], axes[a], 4 * px + 2 * py + pc)

        def copy(a, k, block, to, src=None):
            return pltpu.make_async_remote_copy(
                src_ref=slot(a, *block) if src is None else src, dst_ref=slot(a, *block),
                send_sem=send_sems.at[k, a], recv_sem=recv_sems.at[k, a], device_id=to, device_id_type=MESH_ID)

        mine = [pltpu.make_async_copy(ins[a], slot(a, x, y, c), local_sems.at[a]) for a in arrays]
        for cp in mine:
            cp.start()
        first = [copy(a, k, (x, y, c), to, src=ins[a])
                 for k, to in enumerate((sibling, x_nbr, y_nbr)) for a in arrays]
        for cp in first:
            cp.start()
        later = []
        for a in arrays:
            copy(a, relay_slot, relay_from, (x, y, c)).wait_recv()
            later += [copy(a, 3, relay_from, relay_to), copy(a, 3 + relay_slot, relay_from, sibling)]
            later[-2].start()
            later[-1].start()
        for a in arrays:
            copy(a, 3 - relay_slot, other_from, (x, y, c)).wait_recv()
            later.append(copy(a, 6 - relay_slot, other_from, sibling))
            later[-1].start()
        for a in arrays:
            copy(a, 3, diagonal, (x, y, c)).wait_recv()
            later.append(copy(a, 6, diagonal, sibling))
            later[-1].start()
        for a in arrays:
            for k, block in ((0, sibling), (4, (1 - x, y, 1 - c)), (5, (x, 1 - y, 1 - c)), (6, (1 - x, 1 - y, 1 - c))):
                copy(a, k, block, (x, y, c)).wait_recv()
        for cp in first + later:
            cp.wait_send()
        for cp in mine:
            cp.wait()

    return pl.pallas_call(
        body, name=name, out_shape=_gathered_shapes(blocks, axes),
        in_specs=_any_specs(n_arr), out_specs=_any_specs(n_arr), scratch_shapes=_gather_sems(n_arr),
    )(*blocks)


def _gathered_shapes(blocks, axes):
    return [jax.ShapeDtypeStruct(b.shape[:ax] + (N_DEV,) + b.shape[ax:], b.dtype) for b, ax in zip(blocks, axes)]


def _gather_sems(n_arr):
    return [pltpu.SemaphoreType.DMA((7, n_arr)), pltpu.SemaphoreType.DMA((7, n_arr)), pltpu.SemaphoreType.DMA((n_arr,))]


def _gather_steps(ins, outs, axes, send_sems, recv_sems, local_sems):
    arrays = range(len(ins))
    x, y, c = lax.axis_index("x"), lax.axis_index("y"), lax.axis_index("c")
    me, sibling = (x, y, c), (x, y, 1 - c)
    chips = [(1 - x, y), (x, 1 - y), (1 - x, 1 - y)]

    def slot(a, px, py, pc):
        return _window(outs[a], axes[a], 4 * px + 2 * py + pc)

    def copy(a, k, block, to, src=None):
        return pltpu.make_async_remote_copy(
            src_ref=slot(a, *block) if src is None else src, dst_ref=slot(a, *block),
            send_sem=send_sems.at[k, a], recv_sem=recv_sems.at[k, a], device_id=to, device_id_type=MESH_ID)

    def mine(a):
        return pltpu.make_async_copy(ins[a], slot(a, *me), local_sems.at[a])

    def first():
        return ([copy(a, 0, me, sibling, src=ins[a]) for a in arrays]
                + [copy(a, 1 + j, me, (*chip, c), src=ins[a]) for j, chip in enumerate(chips) for a in arrays])

    def passed(j):
        return [copy(a, 4 + j, (*chips[j], c), sibling) for a in arrays]

    def start():
        for a in arrays:
            mine(a).start()
        for cp in first():
            cp.start()

    def forward(j):
        for a, cp in zip(arrays, passed(j)):
            copy(a, 1 + j, (*chips[j], c), me).wait_recv()
            cp.start()

    def finish():
        for a in arrays:
            copy(a, 0, sibling, me).wait_recv()
        for j, chip in enumerate(chips):
            for a in arrays:
                copy(a, 4 + j, (*chip, 1 - c), me).wait_recv()
        for cp in first() + passed(0) + passed(1) + passed(2):
            cp.wait_send()
        for a in arrays:
            mine(a).wait()

    return start, forward, finish


def exchange_sibling(grads, name):
    n_arr = len(grads)

    def body(*refs):
        start, finish = _sibling_exchange_steps(refs[:n_arr], refs[n_arr:2 * n_arr], *refs[2 * n_arr:])
        start()
        finish()

    return pl.pallas_call(
        body, name=name, out_shape=_sibling_shapes(grads),
        in_specs=_any_specs(n_arr), out_specs=_any_specs(n_arr), scratch_shapes=_exchange_sems(n_arr),
    )(*grads)


def _sibling_shapes(grads):
    return [jax.ShapeDtypeStruct((4,) + g.shape[1:], g.dtype) for g in grads]


def _exchange_sems(n_arr):
    return [pltpu.SemaphoreType.DMA((n_arr,)), pltpu.SemaphoreType.DMA((n_arr,))]


def _sibling_exchange_steps(ins, outs, send_sems, recv_sems):
    x, y, c = lax.axis_index("x"), lax.axis_index("y"), lax.axis_index("c")

    def copy(a, src, dst):
        return pltpu.make_async_remote_copy(
            src_ref=src, dst_ref=dst, send_sem=send_sems.at[a], recv_sem=recv_sems.at[a],
            device_id=(x, y, 1 - c), device_id_type=MESH_ID)

    def start():
        for a in range(len(ins)):
            for k in range(4):
                copy(a, ins[a].at[2 * k + (1 - c)], outs[a].at[k]).start()

    def finish():
        whole = [copy(a, ins[a].at[pl.ds(0, 4)], outs[a]) for a in range(len(ins))]
        for cp in whole:
            cp.wait_recv()
        for cp in whole:
            cp.wait_send()

    return start, finish


def _chip_exchange_steps(ins, outs, send_sems, recv_sems):
    x, y, c = lax.axis_index("x"), lax.axis_index("y"), lax.axis_index("c")
    chips = [(1 - x, y), (x, 1 - y), (1 - x, 1 - y)]

    def copy(a, src, dst, chip):
        return pltpu.make_async_remote_copy(
            src_ref=src, dst_ref=dst, send_sem=send_sems.at[a], recv_sem=recv_sems.at[a],
            device_id=(*chip, c), device_id_type=MESH_ID)

    def start():
        for a in range(len(ins)):
            for j, chip in enumerate(chips):
                copy(a, ins[a].at[j], outs[a].at[j], chip).start()

    def finish():
        whole = [copy(a, ins[a], outs[a], chips[0]) for a in range(len(ins))]
        for cp in whole:
            cp.wait_recv()
        for cp in whole:
            cp.wait_send()

    return start, finish


def riding_gather(blocks, axes):
    def phases(ins, outs, *sems):
        start, forward, finish = _gather_steps(ins, outs, axes, *sems)
        return [start] + [functools.partial(forward, j) for j in range(3)] + [finish]

    return dict(operands=blocks, out_shape=_gathered_shapes(blocks, axes), sems=_gather_sems(len(blocks)),
                phases=phases, when=("first", "late0", "late1", "late2", "last"))


def riding_exchange(parts):
    def phases(ins, outs, *sems):
        return list(_chip_exchange_steps(ins, outs, *sems))

    return dict(operands=parts, out_shape=[jax.ShapeDtypeStruct(p.shape, p.dtype) for p in parts],
                sems=_exchange_sems(len(parts)), phases=phases, when=("first", "last"))


def riding_sibling(grads):
    def phases(ins, outs, *sems):
        return list(_sibling_exchange_steps(ins, outs, *sems))

    return dict(operands=grads, out_shape=_sibling_shapes(grads), sems=_exchange_sems(len(grads)),
                phases=phases, when=("first", "last"))


def _call_with_rider(body, rider, *, name, grid, in_specs, out_specs, out_shape, scratch_shapes, operands):
    params = _params(("arbitrary",) * len(grid))
    if rider is None:
        return pl.pallas_call(body, name=name, grid=grid, in_specs=in_specs, out_specs=out_specs,
                              out_shape=out_shape, scratch_shapes=scratch_shapes, compiler_params=params)(*operands)
    n_in, n_out, n_scr, k = len(in_specs), len(out_specs), len(scratch_shapes), len(rider["operands"])
    at = {"first": (0,) * len(grid), "last": tuple(g - 1 for g in grid)}
    if "late0" in rider["when"]:
        rows, cols = grid
        assert cols >= 3
        at.update({"late%d" % j: (max(rows - 2, 0), j) for j in range(3)})

    def wrapped(*refs):
        ins, c_in = refs[:n_in], refs[n_in:n_in + k]
        outs, c_out = refs[n_in + k:n_in + k + n_out], refs[n_in + k + n_out:n_in + 2 * k + n_out]
        scratch, sems = refs[n_in + 2 * k + n_out:n_in + 2 * k + n_out + n_scr], refs[n_in + 2 * k + n_out + n_scr:]
        pos = [pl.program_id(axis) for axis in range(len(grid))]

        def here(key):
            return functools.reduce(jnp.logical_and, [p == v for p, v in zip(pos, at[key])])

        phases = rider["phases"](c_in, c_out, *sems)
        for fn, key in zip(phases, rider["when"]):
            if key != "last":
                pl.when(here(key))(fn)
        body(*ins, *outs, *scratch)
        pl.when(here("last"))(phases[-1])

    return pl.pallas_call(
        wrapped, name=name, grid=grid,
        in_specs=list(in_specs) + _any_specs(k), out_specs=list(out_specs) + _any_specs(k),
        out_shape=list(out_shape) + rider["out_shape"], scratch_shapes=list(scratch_shapes) + rider["sems"],
        compiler_params=params)(*operands, *rider["operands"])


def add_sibling(g8, got, src_idx, chip_idx, name):
    _, r, n = g8.shape
    tr = _tile(r, SUM_ROWS, 16)

    def body(si_ref, ci_ref, g0_ref, g1_ref, g2_ref, g3_ref, got_ref, own_ref, send_ref):
        own_ref[...] = g0_ref[0] + got_ref[ci_ref[0]]
        for j, g_ref in enumerate((g1_ref, g2_ref, g3_ref)):
            send_ref[j] = (g_ref[0] + got_ref[ci_ref[j + 1]]).astype(BF16)

    def mine(j):
        return pl.BlockSpec((1, tr, n), lambda i, si, ci: (si[j], i, 0))

    return pl.pallas_call(
        body, name=name,
        out_shape=[jax.ShapeDtypeStruct((r, n), F32), jax.ShapeDtypeStruct((3, r, n), BF16)],
        grid_spec=pltpu.PrefetchScalarGridSpec(
            num_scalar_prefetch=2, grid=(r // tr,),
            in_specs=[mine(0), mine(1), mine(2), mine(3), pl.BlockSpec((4, tr, n), lambda i, si, ci: (0, i, 0))],
            out_specs=[pl.BlockSpec((tr, n), lambda i, si, ci: (i, 0)),
                       pl.BlockSpec((3, tr, n), lambda i, si, ci: (0, i, 0))]),
        compiler_params=_params(("arbitrary",)),
    )(src_idx, chip_idx, g8, g8, g8, g8, got)


def add_received(own, got, name):
    r, n = own.shape
    tr = _tile(r, SUM_ROWS, 16)

    def body(a_ref, b_ref, o_ref):
        acc = a_ref[...]
        for j in range(3):
            acc = acc + b_ref[j].astype(F32)
        o_ref[...] = acc

    return pl.pallas_call(
        body, name=name,
        out_shape=jax.ShapeDtypeStruct((r, n), F32),
        grid=(r // tr,),
        in_specs=[pl.BlockSpec((tr, n), lambda i: (i, 0)), pl.BlockSpec((3, tr, n), lambda i: (0, i, 0))],
        out_specs=pl.BlockSpec((tr, n), lambda i: (i, 0)),
        compiler_params=_params(("arbitrary",)),
    )(own, got)


def sum_devices(g):
    def body(g_ref, o_ref):
        acc = g_ref[0]
        for j in range(1, N_DEV):
            acc = acc + g_ref[j]
        o_ref[...] = acc

    return pl.pallas_call(body, name="sum_devices", out_shape=jax.ShapeDtypeStruct(g.shape[1:], F32))(g)


def sum_lanes(v):
    def body(v_ref, o_ref):
        o_ref[...] = jnp.broadcast_to(jnp.sum(v_ref[...], axis=-1, keepdims=True), (1, LANES))

    return pl.pallas_call(body, name="sum_lanes", out_shape=jax.ShapeDtypeStruct((1, LANES), F32))(v)


def ada_forward(c_all, ada_w, ada_b_cols):
    nb, n = c_all.shape[0], ada_w.shape[1]

    def body(c_ref, w_ref, b_ref, o_ref):
        cv = c_ref[...]
        s = (cv * jax.nn.sigmoid(cv)).astype(BF16)
        o_ref[...] = _dot(s, w_ref[...].astype(BF16)) + b_ref[...]

    return pl.pallas_call(body, name="ada_fwd", out_shape=jax.ShapeDtypeStruct((nb, n), F32),
                          compiler_params=_params())(c_all, ada_w, ada_b_cols)


def ada_backward(c_all16, dmod16):
    d, n = c_all16.shape[1], dmod16.shape[1]

    def body(c_ref, g_ref, o_ref):
        cv = c_ref[...]
        s = (cv * jax.nn.sigmoid(cv)).astype(BF16)
        o_ref[...] = _dot(s, g_ref[...].astype(BF16), TN)

    return pl.pallas_call(body, name="ada_bwd", out_shape=jax.ShapeDtypeStruct((d, n), F32),
                          compiler_params=_params())(c_all16, dmod16)


def ffn_forward(x, gn, sc, sh, gate, ws, first, name, rider=None):
    t, d = x.shape
    f = ws.shape[1]
    tm, tf = _tile(t, FFN_FWD_TILE[0], 16), _tile(f, FFN_FWD_TILE[1])
    nf = f // tf

    def body(x_ref, gn_ref, sc_ref, sh_ref, gate_ref, w1_ref, w3_ref, w2_ref,
             xo_ref, h_ref, a_ref, b_ref, y_ref, hs, acc):
        j = pl.program_id(1)

        @pl.when(j == 0)
        def _():
            xhat, _ = _rms(x_ref[...])
            h = (xhat * gn_ref[...] * (1.0 + sc_ref[...]) + sh_ref[...]).astype(BF16)
            hs[...] = h
            h_ref[...] = h
            acc[...] = jnp.zeros_like(acc)

        h = hs[...]
        a = _dot(h, w1_ref[...], NT)
        b = _dot(h, w3_ref[...], NT)
        a_ref[...] = a.astype(BF16)
        b_ref[...] = b.astype(BF16)
        u = (a * _sigmoid(a) * b).astype(BF16)
        acc[...] += _dot(u, w2_ref[...])

        @pl.when(j == nf - 1)
        def _():
            y = acc[...]
            y_ref[...] = y.astype(BF16)
            xo_ref[...] = x_ref[...] + 0.5 * gate_ref[...] * y

    row = pl.BlockSpec((tm, d), lambda i, j: (i, 0))
    vec = pl.BlockSpec((1, d), lambda i, j: (0, 0))
    wide = pl.BlockSpec((tm, tf), lambda i, j: (i, j))
    return _call_with_rider(
        body, rider, name=name, grid=(t // tm, nf),
        in_specs=[row, vec, vec, vec, vec] + _ffn_weight_specs(first, tf, d),
        out_specs=[row, row, wide, wide, row],
        out_shape=[jax.ShapeDtypeStruct((t, d), F32), jax.ShapeDtypeStruct((t, d), BF16),
                   jax.ShapeDtypeStruct((t, f), BF16), jax.ShapeDtypeStruct((t, f), BF16),
                   jax.ShapeDtypeStruct((t, d), BF16)],
        scratch_shapes=[pltpu.VMEM((tm, d), BF16), pltpu.VMEM((tm, d), F32)],
        operands=(x, gn, sc, sh, gate, ws, ws, ws))


def _ffn_weight_specs(first, tf, d):
    return [pl.BlockSpec((None, tf, d), lambda i, j, w=first + k: (w, j, 0)) for k in range(3)]


def ffn_backward_gate(dy, a, b, ws, first, name, rider=None):
    t, d = dy.shape
    f = ws.shape[1]
    tm, tf = _tile(t, FFN_BWD_TILE[0], 16), _tile(f, FFN_BWD_TILE[1])
    nf = f // tf

    def gate_body(dy_ref, a_ref, b_ref, w2_ref, da_ref, db_ref, gw2_ref):
        dy_v = dy_ref[...]
        du = _dot(dy_v, w2_ref[...], NT)
        av = a_ref[...].astype(F32)
        bv = b_ref[...].astype(F32)
        s = _sigmoid(av)
        sa = av * s
        da_ref[...] = (du * bv * (s + sa * (1.0 - s))).astype(BF16)
        db_ref[...] = (du * sa).astype(BF16)
        part = _dot((sa * bv).astype(BF16), dy_v, TN)

        @pl.when(pl.program_id(1) == 0)
        def _():
            gw2_ref[...] = part

        @pl.when(pl.program_id(1) > 0)
        def _():
            gw2_ref[...] += part

    hidden = jax.ShapeDtypeStruct((t, f), BF16)
    wide_t = pl.BlockSpec((tm, tf), lambda j, i: (i, j))
    return _call_with_rider(
        gate_body, rider, name=name, grid=(nf, t // tm),
        in_specs=[pl.BlockSpec((tm, d), lambda j, i: (i, 0)), wide_t, wide_t,
                  pl.BlockSpec((None, tf, d), lambda j, i: (first + 2, j, 0))],
        out_specs=[wide_t, wide_t, pl.BlockSpec((tf, d), lambda j, i: (j, 0))],
        out_shape=[hidden, hidden, jax.ShapeDtypeStruct((f, d), F32)],
        scratch_shapes=[], operands=(dy, a, b, ws))


def ffn_backward_norm(da, db, dxo, x, y, gn, sc, ws, first, name, rider=None):
    t, d = x.shape
    f = ws.shape[1]
    tm, tf = _tile(t, FFN_BWD_TILE[0], 16), _tile(f, FFN_BWD_TILE[1])
    nf = f // tf
    row = pl.BlockSpec((tm, d), lambda i, j: (i, 0))
    vec = pl.BlockSpec((1, d), lambda i, j: (0, 0))
    wide = pl.BlockSpec((tm, tf), lambda i, j: (i, j))

    def norm_body(da_ref, db_ref, w1_ref, w3_ref, dxo_ref, x_ref, y_ref, gn_ref, sc_ref, dx_ref, sums_ref, acc):
        i, j = pl.program_id(0), pl.program_id(1)

        @pl.when(jnp.logical_and(i == 0, j == 0))
        def _():
            sums_ref[...] = jnp.zeros_like(sums_ref)

        part = _dot(da_ref[...], w1_ref[...]) + _dot(db_ref[...], w3_ref[...])

        @pl.when(j == 0)
        def _():
            acc[...] = part

        @pl.when(jnp.logical_and(j > 0, j < nf - 1))
        def _():
            acc[...] += part

        @pl.when(j == nf - 1)
        def _():
            dh = part if nf == 1 else acc[...] + part
            dxo_v = dxo_ref[...]
            dx, d_sh, d_sc, d_gn = _norm_mod_bwd(dh, x_ref[...], gn_ref[...], sc_ref[...])
            dx_ref[...] = dxo_v + dx
            d_gate = jnp.sum(dxo_v * (0.5 * y_ref[...].astype(F32)), axis=0, keepdims=True)
            _add_rows(sums_ref, [d_sh, d_sc, d_gate, d_gn])

    w1_spec, w3_spec, _ = _ffn_weight_specs(first, tf, d)
    return _call_with_rider(
        norm_body, rider, name=name, grid=(t // tm, nf),
        in_specs=[wide, wide, w1_spec, w3_spec, row, row, row, vec, vec],
        out_specs=[row, pl.BlockSpec((8, d), lambda i, j: (0, 0))],
        out_shape=[jax.ShapeDtypeStruct((t, d), F32), jax.ShapeDtypeStruct((8, d), F32)],
        scratch_shapes=[pltpu.VMEM((tm, d), F32)],
        operands=(da, db, ws, ws, dxo, x, y, gn, sc))


def matmul_tn(a, b, name, rider=None):
    parts = list(a) if isinstance(a, (list, tuple)) else [a]
    t, n = b.shape
    widths = [p.shape[1] for p in parts]
    tm = _tile(functools.reduce(math.gcd, widths), GRAD_TILE)
    tn, tk = _tile(n, GRAD_TILE), _tile(t, GRAD_DEPTH, 16)
    nk = t // tk
    counts = [w // tm for w in widths]
    firsts = [sum(counts[:p]) for p in range(len(parts))]

    def body(*refs):
        a_refs, (b_ref, o_ref, acc) = refs[:len(parts)], refs[len(parts):]
        i, k = pl.program_id(0), pl.program_id(2)

        @pl.when(k == 0)
        def _():
            acc[...] = jnp.zeros_like(acc)

        for a_ref, lo, cnt in zip(a_refs, firsts, counts):
            def accumulate(a_ref=a_ref):
                acc[...] += _dot(a_ref[...], b_ref[...], TN)

            if len(parts) == 1:
                accumulate()
            else:
                pl.when(jnp.logical_and(i >= lo, i < lo + cnt))(accumulate)

        @pl.when(k == nk - 1)
        def _():
            o_ref[...] = acc[...]

    def part_spec(lo, cnt):
        if len(parts) == 1:
            return pl.BlockSpec((tk, tm), lambda i, j, k: (k, i))

        def index(i, j, k):
            mine = jnp.logical_and(i >= lo, i < lo + cnt)
            return jnp.where(mine, k, 0), jnp.clip(i - lo, 0, cnt - 1)
        return pl.BlockSpec((tk, tm), index)

    out = _call_with_rider(
        body, rider, name=name, grid=(sum(counts), n // tn, nk),
        in_specs=[part_spec(lo, cnt) for lo, cnt in zip(firsts, counts)]
        + [pl.BlockSpec((tk, tn), lambda i, j, k: (k, j))],
        out_specs=[pl.BlockSpec((tm, tn), lambda i, j, k: (i, j))],
        out_shape=[jax.ShapeDtypeStruct((sum(widths), n), F32)],
        scratch_shapes=[pltpu.VMEM((tm, tn), F32)], operands=(*parts, b))
    return out[0] if rider is None else out


def mix_in_forward(x, gn, sc, sh, w_in):
    t, d = x.shape
    tm = _tile(t, ROW_TILE, 16)

    def body(x_ref, gn_ref, sc_ref, sh_ref, w_ref, h_ref, zc_ref, zm_ref):
        xhat, _ = _rms(x_ref[...])
        h = (xhat * gn_ref[...] * (1.0 + sc_ref[...]) + sh_ref[...]).astype(BF16)
        h_ref[...] = h
        z = _dot(h, w_ref[...], NT)
        zc_ref[...] = z[:, :ZC_COLS].astype(BF16)
        zm_ref[...] = z[:, ZC_COLS:].astype(BF16)

    row = pl.BlockSpec((tm, d), lambda i: (i, 0))
    vec = pl.BlockSpec((1, d), lambda i: (0, 0))
    return pl.pallas_call(
        body, name="mix_in_fwd", grid=(t // tm,),
        in_specs=[row, vec, vec, vec, _row(w_in)],
        out_specs=[row, pl.BlockSpec((tm, ZC_COLS), lambda i: (i, 0)), pl.BlockSpec((tm, ZM_COLS), lambda i: (i, 0))],
        out_shape=[jax.ShapeDtypeStruct((t, d), BF16), jax.ShapeDtypeStruct((t, ZC_COLS), BF16),
                   jax.ShapeDtypeStruct((t, ZM_COLS), BF16)],
        compiler_params=_params(("arbitrary",)),
    )(x, gn, sc, sh, w_in)


def _rope_tables(pos, inv_freq):
    ang = pos * inv_freq
    lane = lax.broadcasted_iota(jnp.int32, ang.shape, 1)
    cos, sin = jnp.cos(ang), jnp.sin(ang)
    half = QK_ROPE // 2
    return cos, jnp.where(lane < half, -sin, 0.0), jnp.where(jnp.logical_and(lane >= half, lane < QK_ROPE), sin, 0.0)


def _rope(v, tables):
    cos, sin_a, sin_b = tables
    return v * cos + pltpu.roll(v, LANES - QK_ROPE // 2, 1) * sin_a + pltpu.roll(v, QK_ROPE // 2, 1) * sin_b


def _rope_transposed(dv, tables):
    cos, sin_a, sin_b = tables
    return dv * cos + pltpu.roll(dv * sin_a, QK_ROPE // 2, 1) + pltpu.roll(dv * sin_b, LANES - QK_ROPE // 2, 1)


def mla_project(zm, pos, inv_freq, qg, kvg, w_uq, w_ukv):
    t = zm.shape[0]
    tm = _tile(t, ROW_TILE, 16)

    def body(zm_ref, pos_ref, if_ref, qg_ref, kvg_ref, wq_ref, wkv_ref, qn_ref, kvn_ref, q_ref, k_ref, v_ref):
        zv = zm_ref[...].astype(F32)
        qn = (_rms(zv[:, :Q_LORA])[0] * qg_ref[...]).astype(BF16)
        kvn = (_rms(zv[:, Q_LORA:Q_LORA + KV_LORA])[0] * kvg_ref[...]).astype(BF16)
        qn_ref[...] = qn
        kvn_ref[...] = kvn
        qf = _dot(qn, wq_ref[...], NT) * QK_FOLD
        kvf = _dot(kvn, wkv_ref[...], NT)
        tables = _rope_tables(pos_ref[...], if_ref[...])
        kr = _rope(zv[:, Q_LORA + KV_LORA:], tables).astype(BF16)
        for h in range(MLA_HEADS):
            lo = h * HEAD_PAD
            q_ref[:, lo:lo + QK_NOPE] = qf[:, lo:lo + QK_NOPE].astype(BF16)
            q_ref[:, lo + QK_NOPE:lo + HEAD_PAD] = _rope(qf[:, lo + QK_NOPE:lo + HEAD_PAD], tables).astype(BF16)
            k_ref[:, lo:lo + QK_NOPE] = kvf[:, h * QK_NOPE:(h + 1) * QK_NOPE].astype(BF16)
            k_ref[:, lo + QK_NOPE:lo + HEAD_PAD] = kr
        v_ref[...] = kvf[:, MLA_HEADS * QK_NOPE:].astype(BF16)

    def rows(n):
        return pl.BlockSpec((tm, n), lambda i: (i, 0))

    return pl.pallas_call(
        body, name="mla_project", grid=(t // tm,),
        in_specs=[rows(ZM_COLS), rows(1), _row(inv_freq), _row(qg), _row(kvg), _row(w_uq), _row(w_ukv)],
        out_specs=[rows(Q_LORA), rows(KV_LORA), rows(QK_COLS), rows(QK_COLS), rows(MLA_WIDTH)],
        out_shape=[jax.ShapeDtypeStruct((t, Q_LORA), BF16), jax.ShapeDtypeStruct((t, KV_LORA), BF16),
                   jax.ShapeDtypeStruct((t, QK_COLS), BF16), jax.ShapeDtypeStruct((t, QK_COLS), BF16),
                   jax.ShapeDtypeStruct((t, MLA_WIDTH), BF16)],
        compiler_params=_params(("arbitrary",)),
    )(zm, pos, inv_freq, qg, kvg, w_uq, w_ukv)


def _chunk_mask(shape, q_axis):
    qi = lax.broadcasted_iota(jnp.int32, shape, q_axis) // CHUNK
    ki = lax.broadcasted_iota(jnp.int32, shape, 1 - q_axis) // CHUNK
    return ki <= qi


def attention_forward(q, k, v, rider=None):
    t = q.shape[0]
    tq = _tile(t, ATTN_TILE, CHUNK)

    def body(q_ref, k_ref, v_ref, o_ref, lse_ref):
        i = pl.program_id(1)
        qv = q_ref[...]

        def step(kb, carry, masked, tiles=1):
            m, l, acc = carry
            keys = pl.ds(pl.multiple_of(kb * tq, tq), tiles * tq)
            s = _dot(qv, k_ref[keys, :], NT)
            if masked:
                s = jnp.where(_chunk_mask(s.shape, 0), s, NEG_INF)
            m_new = jnp.maximum(m, jnp.max(s, axis=-1, keepdims=True))
            alpha = jnp.exp2(m - m_new)
            p = jnp.exp2(s - m_new)
            l = alpha * l + jnp.sum(p, axis=-1, keepdims=True)
            acc = alpha * acc + _dot(p.astype(BF16), v_ref[keys, :])
            return m_new, l, acc

        init = (jnp.full((tq, 1), NEG_INF, F32), jnp.zeros((tq, 1), F32), jnp.zeros((tq, V_HEAD), F32))
        carry = lax.fori_loop(0, i // 2, lambda pb, cr: step(2 * pb, cr, False, 2), init)
        carry = lax.fori_loop(0, i % 2, lambda _, cr: step(i - 1, cr, False), carry)
        m, l, acc = step(i, carry, True)
        o_ref[...] = (acc / l).astype(BF16)
        lse_ref[0] = m + jnp.log2(l)

    return _call_with_rider(
        body, rider, name="attn_fwd", grid=(MLA_HEADS, t // tq),
        in_specs=[pl.BlockSpec((tq, HEAD_PAD), lambda h, i: (i, h)),
                  pl.BlockSpec((t, HEAD_PAD), lambda h, i: (0, h)),
                  pl.BlockSpec((t, V_HEAD), lambda h, i: (0, h))],
        out_specs=[pl.BlockSpec((tq, V_HEAD), lambda h, i: (i, h)),
                   pl.BlockSpec((1, tq, 1), lambda h, i: (h, i, 0))],
        out_shape=[jax.ShapeDtypeStruct((t, MLA_WIDTH), BF16), jax.ShapeDtypeStruct((MLA_HEADS, t, 1), F32)],
        scratch_shapes=[], operands=(q, k, v))


def attention_backward(q, k, v, do, lse, delta, rider=None):
    t = q.shape[0]
    tq = _tile(t, ATTN_TILE, CHUNK)
    nq = t // tq

    def body(q_ref, k_ref, v_ref, do_ref, lse_ref, delta_ref, dq_ref, dk_ref, dv_ref, dq_acc):
        kb = pl.program_id(1)

        @pl.when(kb == 0)
        def _():
            dq_acc[...] = jnp.zeros_like(dq_acc)

        kv, vv = k_ref[...], v_ref[...]

        def step(qb, carry, masked):
            dk, dv = carry
            rows = pl.ds(pl.multiple_of(qb * tq, tq), tq)
            qv, dov = q_ref[rows, :], do_ref[rows, :]
            s = _dot(kv, qv, NT)
            if masked:
                s = jnp.where(_chunk_mask(s.shape, 1), s, NEG_INF)
            p = jnp.exp2(s - lse_ref[0, qb])
            dv = dv + _dot(p.astype(BF16), dov)
            dp = _dot(vv, dov, NT)
            ds = (p * (dp - delta_ref[0, qb]) * LN_2).astype(BF16)
            dk = dk + _dot(ds, qv)
            dq_acc[rows, :] += _dot(ds, kv, TN)
            return dk, dv

        carry = step(kb, (jnp.zeros((tq, HEAD_PAD), F32), jnp.zeros((tq, V_HEAD), F32)), True)
        odd = (nq - 1 - kb) % 2
        carry = lax.fori_loop(0, odd, lambda _, cr: step(kb + 1, cr, False), carry)
        first = kb + 1 + odd
        dk, dv = lax.fori_loop(0, (nq - first) // 2,
                               lambda pb, cr: step(first + 2 * pb + 1, step(first + 2 * pb, cr, False), False), carry)
        dk_ref[...] = dk.astype(BF16)
        dv_ref[...] = dv.astype(BF16)

        @pl.when(kb == nq - 1)
        def _():
            dq_ref[...] = dq_acc[...].astype(BF16)

    stat = pl.BlockSpec((1, nq, 1, tq), lambda h, j: (h, 0, 0, 0))
    return _call_with_rider(
        body, rider, name="attn_bwd", grid=(MLA_HEADS, nq),
        in_specs=[pl.BlockSpec((t, HEAD_PAD), lambda h, j: (0, h)),
                  pl.BlockSpec((tq, HEAD_PAD), lambda h, j: (j, h)),
                  pl.BlockSpec((tq, V_HEAD), lambda h, j: (j, h)),
                  pl.BlockSpec((t, V_HEAD), lambda h, j: (0, h)), stat, stat],
        out_specs=[pl.BlockSpec((t, HEAD_PAD), lambda h, j: (0, h)),
                   pl.BlockSpec((tq, HEAD_PAD), lambda h, j: (j, h)),
                   pl.BlockSpec((tq, V_HEAD), lambda h, j: (j, h))],
        out_shape=[jax.ShapeDtypeStruct((t, QK_COLS), BF16), jax.ShapeDtypeStruct((t, QK_COLS), BF16),
                   jax.ShapeDtypeStruct((t, MLA_WIDTH), BF16)],
        scratch_shapes=[pltpu.VMEM((t, HEAD_PAD), F32)], operands=(q, k, v, do, lse, delta))


HALO = 16


def _halo_spec(tm, n, step, last):
    return pl.BlockSpec((HALO, n), lambda i: (jnp.clip(i * (tm // HALO) + step, 0, last), 0))


def _shift_rows(v, prev, n):
    out = pltpu.roll(v, n, 0)
    row = lax.broadcasted_iota(jnp.int32, v.shape, 0)
    for r in range(n):
        out = jnp.where(row == r, prev[HALO - n + r:HALO - n + r + 1, :], out)
    return out


def _advance_rows(v, nxt, n):
    rows = v.shape[0]
    out = pltpu.roll(v, rows - n, 0)
    row = lax.broadcasted_iota(jnp.int32, v.shape, 0)
    for r in range(n):
        out = jnp.where(row == rows - n + r, nxt[r:r + 1, :], out)
    return out


def _conv_taps(zc, zc_prev, first):
    w = CONV_WIDTH
    u = zc[:, w:2 * w] * zc[:, 2 * w:]
    up = jnp.where(first, 0.0, zc_prev[:, w:2 * w] * zc_prev[:, 2 * w:])
    return u, _shift_rows(u, up, 1), _shift_rows(u, up, 2)


def mix_out_forward(zc, o, conv_w, og, gmat_a, gmat_b, w_out, x, gate):
    t, d = x.shape
    tm = _tile(t, ROW_TILE, 16)
    w = CONV_WIDTH

    def body(zc_ref, zp_ref, o_ref, cw_ref, og_ref, ga_ref, gb_ref, w_ref, x_ref, gate_ref,
             xo_ref, yn_ref, y_ref, ya_ref):
        zc_v = zc_ref[...].astype(F32)
        u, u1, u2 = _conv_taps(zc_v, zp_ref[...].astype(F32), pl.program_id(0) == 0)
        cw = cw_ref[...]
        ya = zc_v[:, :w] * (cw[0:1] * u2 + cw[1:2] * u1 + cw[2:3] * u)
        ya_ref[...] = ya.astype(BF16)
        ov = o_ref[...].astype(F32)
        ogv = og_ref[...]
        yn_ref[:, :w] = (ya * lax.rsqrt(_group_mean(ya * ya, ga_ref[...]) + EPS) * ogv[:, :w]).astype(BF16)
        yn_ref[:, w:] = (ov * lax.rsqrt(_group_mean(ov * ov, gb_ref[...]) + EPS) * ogv[:, w:]).astype(BF16)
        y = _dot(yn_ref[...], w_ref[...])
        y_ref[...] = y.astype(BF16)
        xo_ref[...] = x_ref[...] + gate_ref[...] * y

    def rows(n):
        return pl.BlockSpec((tm, n), lambda i: (i, 0))

    return pl.pallas_call(
        body, name="mix_out_fwd", grid=(t // tm,),
        in_specs=[rows(ZC_COLS), _halo_spec(tm, ZC_COLS, -1, t // HALO - 1), rows(MLA_WIDTH), _row(conv_w), _row(og),
                  _row(gmat_a), _row(gmat_b), _row(w_out), rows(d), _row(gate)],
        out_specs=[rows(d), rows(MIX_WIDTH), rows(d), rows(w)],
        out_shape=[jax.ShapeDtypeStruct((t, d), F32), jax.ShapeDtypeStruct((t, MIX_WIDTH), BF16),
                   jax.ShapeDtypeStruct((t, d), BF16), jax.ShapeDtypeStruct((t, w), BF16)],
        compiler_params=_params(("arbitrary",)),
    )(zc, zc, o, conv_w, og, gmat_a, gmat_b, w_out, x, gate)


def _group_norm_bwd(dyn, y, og, gmat):
    rs = lax.rsqrt(_group_mean(y * y, gmat) + EPS)
    yhat = y * rs
    d_og = jnp.sum(dyn * yhat, axis=0, keepdims=True)
    dyh = dyn * og
    return rs * (dyh - yhat * _group_mean(dyh * yhat, gmat)), d_og


def mix_out_backward(dxo, y, gate, ya, o, og, gmat_a, gmat_b, w_out, rider=None):
    t, d = dxo.shape
    tm = _tile(t, ROW_TILE, 16)
    w = CONV_WIDTH

    def body(dxo_ref, y_ref, gate_ref, ya_ref, o_ref, og_ref, ga_ref, gb_ref, w_ref,
             dy_ref, dya_ref, do_ref, delta_ref, sd_ref, so_ref):
        @pl.when(pl.program_id(0) == 0)
        def _():
            sd_ref[...] = jnp.zeros_like(sd_ref)
            so_ref[...] = jnp.zeros_like(so_ref)

        dxo_v = dxo_ref[...]
        dy = (gate_ref[...] * dxo_v).astype(BF16)
        dy_ref[...] = dy
        sd_ref[0:1, :] += jnp.sum(dxo_v * y_ref[...].astype(F32), axis=0, keepdims=True)
        dyn = _dot(dy, w_ref[...], NT)
        ogv = og_ref[...]
        ov = o_ref[...].astype(F32)
        dya, d_og_a = _group_norm_bwd(dyn[:, :w], ya_ref[...].astype(F32), ogv[:, :w], ga_ref[...])
        dov, d_og_b = _group_norm_bwd(dyn[:, w:], ov, ogv[:, w:], gb_ref[...])
        dya_ref[...] = dya.astype(BF16)
        do_ref[...] = dov.astype(BF16)
        so_ref[0:1, :w] += d_og_a
        so_ref[0:1, w:] += d_og_b
        prod = dov * ov
        for h in range(MLA_HEADS):
            delta_ref[h] = jnp.sum(prod[:, h * V_HEAD:(h + 1) * V_HEAD], axis=-1, keepdims=True)

    def rows(n):
        return pl.BlockSpec((tm, n), lambda i: (i, 0))

    return _call_with_rider(
        body, rider, name="mix_out_bwd", grid=(t // tm,),
        in_specs=[rows(d), rows(d), _row(gate), rows(w), rows(MLA_WIDTH), _row(og), _row(gmat_a), _row(gmat_b),
                  _row(w_out)],
        out_specs=[rows(d), rows(w), rows(MLA_WIDTH), pl.BlockSpec((MLA_HEADS, tm, 1), lambda i: (0, i, 0)),
                   pl.BlockSpec((8, d), lambda i: (0, 0)), pl.BlockSpec((8, MIX_WIDTH), lambda i: (0, 0))],
        out_shape=[jax.ShapeDtypeStruct((t, d), BF16), jax.ShapeDtypeStruct((t, w), BF16),
                   jax.ShapeDtypeStruct((t, MLA_WIDTH), BF16), jax.ShapeDtypeStruct((MLA_HEADS, t, 1), F32),
                   jax.ShapeDtypeStruct((8, d), F32), jax.ShapeDtypeStruct((8, MIX_WIDTH), F32)],
        scratch_shapes=[], operands=(dxo, y, gate, ya, o, og, gmat_a, gmat_b, w_out))


def conv_backward(zc, dya, conv_w):
    t = zc.shape[0]
    tm = _tile(t, ROW_TILE, 16)
    nt = t // tm
    w = CONV_WIDTH

    def body(zc_ref, zp_ref, zn_ref, dya_ref, dn_ref, cw_ref, dzc_ref, sums_ref):
        i = pl.program_id(0)

        @pl.when(i == 0)
        def _():
            sums_ref[...] = jnp.zeros_like(sums_ref)

        zc_v = zc_ref[...].astype(F32)
        u, u1, u2 = _conv_taps(zc_v, zp_ref[...].astype(F32), i == 0)
        cw = cw_ref[...]
        dya_v = dya_ref[...].astype(F32)
        dyc = dya_v * zc_v[:, :w]
        dyc_next = jnp.where(i == nt - 1, 0.0, dn_ref[...].astype(F32) * zn_ref[:, :w].astype(F32))
        du = cw[2:3] * dyc + cw[1:2] * _advance_rows(dyc, dyc_next, 1) + cw[0:1] * _advance_rows(dyc, dyc_next, 2)
        dzc_ref[:, :w] = (dya_v * (cw[0:1] * u2 + cw[1:2] * u1 + cw[2:3] * u)).astype(BF16)
        dzc_ref[:, w:2 * w] = (du * zc_v[:, 2 * w:]).astype(BF16)
        dzc_ref[:, 2 * w:] = (du * zc_v[:, w:2 * w]).astype(BF16)
        _add_rows(sums_ref, [jnp.sum(dyc * tap, axis=0, keepdims=True) for tap in (u2, u1, u)])

    def rows(n):
        return pl.BlockSpec((tm, n), lambda i: (i, 0))

    def halo(n, step):
        return _halo_spec(tm, n, step, t // HALO - 1)

    return pl.pallas_call(
        body, name="conv_bwd", grid=(nt,),
        in_specs=[rows(ZC_COLS), halo(ZC_COLS, -1), halo(ZC_COLS, tm // HALO), rows(w), halo(w, tm // HALO),
                  _row(conv_w)],
        out_specs=[rows(ZC_COLS), pl.BlockSpec((8, w), lambda i: (0, 0))],
        out_shape=[jax.ShapeDtypeStruct((t, ZC_COLS), BF16), jax.ShapeDtypeStruct((8, w), F32)],
        compiler_params=_params(("arbitrary",)),
    )(zc, zc, zc, dya, dya, conv_w)


def _rms_bwd(dy, x, g):
    xhat, r = _rms(x)
    d_g = jnp.sum(dy * xhat, axis=0, keepdims=True)
    dxh = dy * g
    return r * (dxh - xhat * jnp.mean(dxh * xhat, axis=-1, keepdims=True)), d_g


def mla_project_backward(dq, dk, dv, zm, pos, inv_freq, qg, kvg, w_uq, w_ukv):
    t = zm.shape[0]
    tm = _tile(t, ROW_TILE, 16)

    def body(dq_ref, dk_ref, dv_ref, zm_ref, pos_ref, if_ref, qg_ref, kvg_ref, wq_ref, wkv_ref,
             dql_ref, dkvl_ref, dzm_ref, sums_ref):
        @pl.when(pl.program_id(0) == 0)
        def _():
            sums_ref[...] = jnp.zeros_like(sums_ref)

        tables = _rope_tables(pos_ref[...], if_ref[...])
        dkr = jnp.zeros((tm, LANES), F32)
        for h in range(MLA_HEADS):
            lo = h * HEAD_PAD
            dql_ref[:, lo:lo + QK_NOPE] = (dq_ref[:, lo:lo + QK_NOPE].astype(F32) * QK_FOLD).astype(BF16)
            dql_ref[:, lo + QK_NOPE:lo + HEAD_PAD] = _rope_transposed(
                dq_ref[:, lo + QK_NOPE:lo + HEAD_PAD].astype(F32) * QK_FOLD, tables).astype(BF16)
            dkvl_ref[:, h * QK_NOPE:(h + 1) * QK_NOPE] = dk_ref[:, lo:lo + QK_NOPE]
            dkr = dkr + dk_ref[:, lo + QK_NOPE:lo + HEAD_PAD].astype(F32)
        dkvl_ref[:, MLA_HEADS * QK_NOPE:] = dv_ref[...]
        zv = zm_ref[...].astype(F32)
        dqn = _dot(dql_ref[...], wq_ref[...])
        dkvn = _dot(dkvl_ref[...], wkv_ref[...])
        dcq, d_qg = _rms_bwd(dqn, zv[:, :Q_LORA], qg_ref[...])
        dckv, d_kvg = _rms_bwd(dkvn, zv[:, Q_LORA:Q_LORA + KV_LORA], kvg_ref[...])
        dzm_ref[:, :Q_LORA] = dcq.astype(BF16)
        dzm_ref[:, Q_LORA:Q_LORA + KV_LORA] = dckv.astype(BF16)
        dzm_ref[:, Q_LORA + KV_LORA:] = _rope_transposed(dkr, tables).astype(BF16)
        sums_ref[0:1, :Q_LORA] += d_qg
        sums_ref[0:1, Q_LORA:Q_LORA + KV_LORA] += d_kvg

    def rows(n):
        return pl.BlockSpec((tm, n), lambda i: (i, 0))

    return pl.pallas_call(
        body, name="mla_project_bwd", grid=(t // tm,),
        in_specs=[rows(QK_COLS), rows(QK_COLS), rows(MLA_WIDTH), rows(ZM_COLS), rows(1), _row(inv_freq),
                  _row(qg), _row(kvg), _row(w_uq), _row(w_ukv)],
        out_specs=[rows(QK_COLS), rows(QK_COLS), rows(ZM_COLS), pl.BlockSpec((8, ZM_COLS), lambda i: (0, 0))],
        out_shape=[jax.ShapeDtypeStruct((t, QK_COLS), BF16), jax.ShapeDtypeStruct((t, QK_COLS), BF16),
                   jax.ShapeDtypeStruct((t, ZM_COLS), BF16), jax.ShapeDtypeStruct((8, ZM_COLS), F32)],
        compiler_params=_params(("arbitrary",)),
    )(dq, dk, dv, zm, pos, inv_freq, qg, kvg, w_uq, w_ukv)


def mix_in_backward(dzc, dzm, w_in, x, dxo, gn, sc, gate, rider=None):
    t, d = x.shape
    tm = _tile(t, ROW_TILE, 16)

    def body(dzc_ref, dzm_ref, w_ref, x_ref, dxo_ref, gn_ref, sc_ref, gate_ref, dx_ref, dy_ref, sums_ref):
        @pl.when(pl.program_id(0) == 0)
        def _():
            sums_ref[...] = jnp.zeros_like(sums_ref)

        dh = _dot(dzc_ref[...], w_ref[:ZC_COLS, :]) + _dot(dzm_ref[...], w_ref[ZC_COLS:, :])
        dx, d_sh, d_sc, d_gn = _norm_mod_bwd(dh, x_ref[...], gn_ref[...], sc_ref[...])
        dx = dxo_ref[...] + dx
        dx_ref[...] = dx
        dy_ref[...] = (0.5 * gate_ref[...] * dx).astype(BF16)
        _add_rows(sums_ref, [d_sh, d_sc, d_gn])

    def rows(n):
        return pl.BlockSpec((tm, n), lambda i: (i, 0))

    return _call_with_rider(
        body, rider, name="mix_in_bwd", grid=(t // tm,),
        in_specs=[rows(ZC_COLS), rows(ZM_COLS), _row(w_in), rows(d), rows(d), _row(gn), _row(sc), _row(gate)],
        out_specs=[rows(d), rows(d), pl.BlockSpec((8, d), lambda i: (0, 0))],
        out_shape=[jax.ShapeDtypeStruct((t, d), F32), jax.ShapeDtypeStruct((t, d), BF16),
                   jax.ShapeDtypeStruct((8, d), F32)],
        scratch_shapes=[], operands=(dzc, dzm, w_in, x, dxo, gn, sc, gate))


def final_loss(x, target, g, gate):
    t, d = x.shape
    tm = _tile(t, ROW_TILE, 16)

    def body(x_ref, t_ref, g_ref, gate_ref, dx_ref, dy_ref, sums_ref):
        @pl.when(pl.program_id(0) == 0)
        def _():
            sums_ref[...] = jnp.zeros_like(sums_ref)

        gv = g_ref[...]
        xhat, r = _rms(x_ref[...])
        err = xhat * gv - t_ref[...]
        dyf = err * (1.0 / d)
        dxh = dyf * gv
        dx = r * (dxh - xhat * jnp.mean(dxh * xhat, axis=-1, keepdims=True))
        dx_ref[...] = dx
        dy_ref[...] = (0.5 * gate_ref[...] * dx).astype(BF16)
        _add_rows(sums_ref, [jnp.sum(dyf * xhat, axis=0, keepdims=True),
                             jnp.sum(err * err, axis=0, keepdims=True) * (0.5 / d)])

    row = pl.BlockSpec((tm, d), lambda i: (i, 0))
    return pl.pallas_call(
        body, name="final_loss", grid=(t // tm,),
        in_specs=[row, row, _row(g), _row(gate)],
        out_specs=[row, row, pl.BlockSpec((8, d), lambda i: (0, 0))],
        out_shape=[jax.ShapeDtypeStruct((t, d), F32), jax.ShapeDtypeStruct((t, d), BF16),
                   jax.ShapeDtypeStruct((8, d), F32)],
        compiler_params=_params(("arbitrary",)),
    )(x, target, g, gate)


def adamw(w, g, m, v, name):
    r, n = w.shape
    tr = _tile(r, max(8, (1 << 19) // n), 8)

    def body(w_ref, g_ref, m_ref, v_ref, d_ref, mo_ref, vo_ref):
        gv = g_ref[...]
        m_new = ADAM_B1 * m_ref[...] + (1.0 - ADAM_B1) * gv
        v_new = ADAM_B2 * v_ref[...] + (1.0 - ADAM_B2) * (gv * gv)
        m_hat = m_new / (1.0 - ADAM_B1 ** ADAM_STEP)
        v_hat = v_new / (1.0 - ADAM_B2 ** ADAM_STEP)
        d_ref[...] = -ADAM_LR * (m_hat / (jnp.sqrt(v_hat) + ADAM_EPS) + ADAM_WD * w_ref[...])
        mo_ref[...] = m_new
        vo_ref[...] = v_new

    blk = pl.BlockSpec((tr, n), lambda i: (i, 0))
    shape = jax.ShapeDtypeStruct((r, n), F32)
    return pl.pallas_call(
        body, name=name, grid=(r // tr,), in_specs=[blk] * 4, out_specs=[blk] * 3, out_shape=[shape] * 3,
        compiler_params=_params(("arbitrary",)),
    )(w, g, m, v)


def _pad_to(v, n):
    return jnp.pad(v, (0, n - v.shape[0]))


def _pad_heads(w, axis_len):
    n = w.shape[1]
    return jnp.pad(w.reshape(MLA_HEADS, axis_len, n), ((0, 0), (0, HEAD_PAD - axis_len), (0, 0))).reshape(-1, n)


def _swap_head_parts(w, inner, outer):
    n = w.shape[1]
    return w.reshape(outer, inner, QK_NOPE, n).transpose(1, 0, 2, 3).reshape(-1, n)


def kernel(x, c, positions, ada_w, ada_b, norm_ffn1_g, ffn1_w1, ffn1_w3, ffn1_w2, norm_mix_g, w_in, conv_w, q_norm_g, w_uq, kv_norm_g, w_ukv, out_norm_g, w_out, norm_ffn2_g, ffn2_w1, ffn2_w3, ffn2_w2, final_norm_g, loss_target, m_ada_w, m_ada_b, m_norm_ffn1_g, m_ffn1_w1, m_ffn1_w3, m_ffn1_w2, m_norm_mix_g, m_w_in, m_conv_w, m_q_norm_g, m_w_uq, m_kv_norm_g, m_w_ukv, m_out_norm_g, m_w_out, m_norm_ffn2_g, m_ffn2_w1, m_ffn2_w3, m_ffn2_w2, m_final_norm_g, v_ada_w, v_ada_b, v_norm_ffn1_g, v_ffn1_w1, v_ffn1_w3, v_ffn1_w2, v_norm_mix_g, v_w_in, v_conv_w, v_q_norm_g, v_w_uq, v_kv_norm_g, v_w_ukv, v_out_norm_g, v_w_out, v_norm_ffn2_g, v_ffn2_w1, v_ffn2_w3, v_ffn2_w2, v_final_norm_g):
    t, d = x.shape[1], x.shape[2]
    f = ffn1_w2.shape[1] * N_DEV
    me = 4 * lax.axis_index("x") + 2 * lax.axis_index("y") + lax.axis_index("c")
    my_c = lax.axis_index("c")
    my_chip = 2 * lax.axis_index("x") + lax.axis_index("y")
    xs = x[0]
    n_ada = ada_w.shape[2]
    cw_n = conv_w.shape[2]

    c_rows = jnp.broadcast_to(c, (8, d))
    conv_rows = jnp.pad(conv_w[0], ((0, 8 - CONV_K), (0, LANES - cw_n)))
    ffn1_blocks = jnp.stack([ffn1_w1[0].T, ffn1_w3[0].T, ffn1_w2[0]]).astype(BF16)
    ffn2_blocks = jnp.stack([ffn2_w1[0].T, ffn2_w3[0].T, ffn2_w2[0]]).astype(BF16)
    c_all, conv_all, ffn1_all = all_gather_relayed([c_rows, conv_rows, ffn1_blocks], [0, 0, 1], "gather_first")
    c_all = c_all[:, 0, :]
    conv_full8 = conv_all[:, :, :cw_n].transpose(1, 0, 2).reshape(8, CONV_WIDTH)
    ffn1_ws = ffn1_all.reshape(3, f, d)
    gather_mix = riding_gather(
        [w_in[0].T.astype(BF16), w_uq[0].T.astype(BF16), w_ukv[0].T.astype(BF16), w_out[0].astype(BF16)], [0, 0, 0, 0])

    ada_b_cols = lax.dynamic_slice_in_dim(ada_b, me * n_ada, n_ada, axis=1)
    mod_cols = ada_forward(c_all, ada_w[0], ada_b_cols)
    mod_all, = all_gather([mod_cols], [0], "gather_mod")
    mod = lax.dynamic_index_in_dim(mod_all, me, axis=1, keepdims=False).reshape(N_MOD, 1, d)
    sh1, sc1, g1, sh2, sc2, g2, sh3, sc3, g3 = [mod[i] for i in range(N_MOD)]

    gf = final_norm_g.reshape(1, d)
    x1, h1, a1, b1, y1, *gathered = ffn_forward(xs, norm_ffn1_g, sc1, sh1, g1, ffn1_ws, 0, "ffn1_fwd", gather_mix)
    w_in_p = jnp.pad(gathered[0].reshape(IN_COLS, d), ((0, ZC_COLS + ZM_COLS - IN_COLS), (0, 0)))
    w_uq_p = _pad_heads(gathered[1].reshape(-1, Q_LORA), QK_NOPE + QK_ROPE)
    w_ukv_p = _swap_head_parts(gathered[2].reshape(-1, KV_LORA), 2, MLA_HEADS)
    w_out_f = gathered[3].reshape(MIX_WIDTH, d)
    h2, zc, zm = mix_in_forward(x1, norm_mix_g, sc2, sh2, w_in_p)
    pos = positions[0].astype(F32).reshape(t, 1)
    inv_freq = ROPE_THETA ** (-jnp.arange(0, QK_ROPE, 2, dtype=F32) / QK_ROPE)
    inv_freq = jnp.concatenate([inv_freq, inv_freq, jnp.zeros((LANES - QK_ROPE,), F32)]).reshape(1, LANES)
    qn, kvn, q, k, v = mla_project(zm, pos, inv_freq, q_norm_g, kv_norm_g, w_uq_p, w_ukv_p)
    o, lse, ffn2_all = attention_forward(q, k, v, riding_gather([ffn2_blocks], [1]))
    ffn2_ws = ffn2_all.reshape(3, f, d)
    lane = jnp.arange(CONV_WIDTH)
    gmat_a = (lane[:, None] // (CONV_WIDTH // CONV_GROUPS) == lane[None, :] // (CONV_WIDTH // CONV_GROUPS))
    gmat_a = (gmat_a / (CONV_WIDTH // CONV_GROUPS)).astype(BF16)
    gmat_b = ((lane[:, None] // V_HEAD == lane[None, :] // V_HEAD) / V_HEAD).astype(BF16)
    x2, yn, y2, ya = mix_out_forward(zc, o, conv_full8, out_norm_g, gmat_a, gmat_b, w_out_f, x1, g2)
    x3, h3, a3, b3, y3 = ffn_forward(x2, norm_ffn2_g, sc3, sh3, g3, ffn2_ws, 0, "ffn2_fwd")
    dx3, dy3, sums_f = final_loss(x3, loss_target[0], gf, g3)

    chip_idx = jnp.bitwise_xor(my_chip, jnp.array([0, 2, 1, 3], jnp.int32)).astype(jnp.int32)
    src_idx = (2 * chip_idx + my_c).astype(jnp.int32)

    def row_blocks(named):
        return [g.reshape(N_DEV, g.shape[0] // N_DEV, g.shape[1]) for _, g in named]

    def chip_sums(named, g8, got):
        return [add_sibling(g, r, src_idx, chip_idx, "rs_add_" + n) for g, r, (n, _) in zip(g8, got, named)]

    da3, db3, g_w2b = ffn_backward_gate(dy3, a3, b3, ffn2_ws, 0, "ffn2_bwd_gate")
    dx2, sums_3 = ffn_backward_norm(da3, db3, dx3, x2, y3, norm_ffn2_g, sc3, ffn2_ws, 0, "ffn2_bwd_norm")
    ffn2_named = [("ffn2_w1", matmul_tn(da3, h3, "ffn2_gw1")), ("ffn2_w3", matmul_tn(db3, h3, "ffn2_gw3")),
                  ("ffn2_w2", g_w2b)]
    ffn2_g8 = row_blocks(ffn2_named)
    dy2, dya, do, delta, sums_2d, sums_2o, *ffn2_sib = mix_out_backward(
        dx2, y2, g2, ya, o, out_norm_g, gmat_a, gmat_b, w_out_f, riding_sibling(ffn2_g8))
    ffn2_sums = chip_sums(ffn2_named, ffn2_g8, ffn2_sib)
    g_w_out = matmul_tn(yn, dy2, "gw_out")
    nq = t // _tile(t, ATTN_TILE, CHUNK)
    stat_shape = (MLA_HEADS, nq, 1, t // nq)
    dq, dk, dv, *ffn2_got = attention_backward(q, k, v, do, lse.reshape(stat_shape), delta.reshape(stat_shape),
                                               riding_exchange([s[1] for s in ffn2_sums]))
    dzc, sums_c = conv_backward(zc, dya, conv_full8)
    dql, dkvl, dzm, sums_m = mla_project_backward(dq, dk, dv, zm, pos, inv_freq, q_norm_g, kv_norm_g, w_uq_p, w_ukv_p)
    g_w_uq_p = matmul_tn(dql, qn, "gw_uq")
    g_w_ukv_p = matmul_tn(dkvl, kvn, "gw_ukv")
    g_w_in = matmul_tn([dzc, dzm], h2, "gw_in")[:IN_COLS]
    g_w_uq = g_w_uq_p.reshape(MLA_HEADS, HEAD_PAD, Q_LORA)[:, :QK_NOPE + QK_ROPE].reshape(-1, Q_LORA)
    g_w_ukv = _swap_head_parts(g_w_ukv_p, MLA_HEADS, 2)
    mix_named = [("w_in", g_w_in), ("w_uq", g_w_uq), ("w_ukv", g_w_ukv), ("w_out", g_w_out)]
    mix_g8 = row_blocks(mix_named)
    dx1, dy1, sums_1m, *mix_sib = mix_in_backward(dzc, dzm, w_in_p, x1, dx2, norm_mix_g, sc2, g1, riding_sibling(mix_g8))
    mix_sums = chip_sums(mix_named, mix_g8, mix_sib)
    da1, db1, g_w2a, *mix_got = ffn_backward_gate(dy1, a1, b1, ffn1_ws, 0, "ffn1_bwd_gate",
                                                  riding_exchange([s[1] for s in mix_sums]))
    ffn1_pair = [("ffn1_w2", g_w2a), ("ffn1_w1", matmul_tn(da1, h1, "ffn1_gw1"))]
    pair_g8 = row_blocks(ffn1_pair)
    g_w3a, *pair_sib = matmul_tn(db1, h1, "ffn1_gw3", riding_sibling(pair_g8))
    ffn1_last = [("ffn1_w3", g_w3a)]
    last_g8 = row_blocks(ffn1_last)
    ffn1_named = ffn1_pair + ffn1_last
    ffn1_sums = chip_sums(ffn1_pair, pair_g8, pair_sib) + chip_sums(
        ffn1_last, last_g8, exchange_sibling(last_g8, "rs_sibling_ffn1_w3"))
    dx0, sums_1, *ffn1_got = ffn_backward_norm(da1, db1, dx1, xs, y1, norm_ffn1_g, sc1, ffn1_ws, 0, "ffn1_bwd_norm",
                                               riding_exchange([s[1] for s in ffn1_sums]))
    g_sh = {}
    for named, group_sums, group_got in ((ffn2_named, ffn2_sums, ffn2_got), (mix_named, mix_sums, mix_got),
                                         (ffn1_named, ffn1_sums, ffn1_got)):
        for (n, _), (own, _), got in zip(named, group_sums, group_got):
            g_sh[n] = add_received(own, got, "rs_sum_" + n)

    dmod = jnp.concatenate([sums_1[0], sums_1[1], sums_1[2], sums_1m[0], sums_1m[1], sums_2d[0],
                            sums_3[0], sums_3[1], sums_3[2]])
    pieces = [dmod, sums_1[3], sums_1m[2], sums_m[0, :Q_LORA], sums_m[0, Q_LORA:Q_LORA + KV_LORA], sums_2o[0],
              sums_3[3], sums_f[0], sums_f[1], sums_c[:CONV_K].reshape(-1)]
    plens = [p.shape[0] for p in pieces]
    poffs = [sum(plens[:i]) for i in range(len(plens))]
    vec_len = -(-sum(plens) // 1024) * 1024
    vec = _pad_to(jnp.concatenate(pieces), vec_len).reshape(-1, LANES)
    vec_all, = all_gather([vec], [0], "gather_sums")
    tot = sum_devices(vec_all).reshape(-1)
    g_ada_b, g_n1, g_nmix, g_qg, g_kvg, g_og, g_n3, g_gf, loss_lanes, g_conv_full = [
        tot[o:o + n] for o, n in zip(poffs, plens)]
    loss = sum_lanes(loss_lanes.reshape(1, d))[0, 0]
    g_conv = lax.dynamic_slice_in_dim(g_conv_full.reshape(CONV_K, CONV_WIDTH), me * cw_n, cw_n, axis=1)
    dmod_all = vec_all.reshape(N_DEV, vec_len)[:, :N_MOD * d]
    dmod_cols = lax.dynamic_slice_in_dim(dmod_all, me * n_ada, n_ada, axis=1)
    g_ada_w = ada_backward(jnp.pad(c_all, ((0, 8), (0, 0))), jnp.pad(dmod_cols, ((0, 8), (0, 0))))

    def update(name, w, g, m, v):
        k, n = w.shape[-2:]
        if g.shape == (k, n):
            flat, back = (lambda a: a.reshape(k, n)), (lambda a: a.reshape(w.shape))
        else:
            flat, back = (lambda a: a.reshape(k, n).T), (lambda a: a.T.reshape(w.shape))
        dlt, nm, nv = adamw(flat(w), g, flat(m), flat(v), "adamw_" + name)
        return back(g), back(dlt), back(nm), back(nv)

    res = {}
    res["ada_w"] = update("ada_w", ada_w, g_ada_w, m_ada_w, v_ada_w)
    big = [("ffn1_w1", ffn1_w1, m_ffn1_w1, v_ffn1_w1), ("ffn1_w3", ffn1_w3, m_ffn1_w3, v_ffn1_w3),
           ("ffn2_w1", ffn2_w1, m_ffn2_w1, v_ffn2_w1), ("ffn2_w3", ffn2_w3, m_ffn2_w3, v_ffn2_w3),
           ("w_in", w_in, m_w_in, v_w_in), ("w_uq", w_uq, m_w_uq, v_w_uq), ("w_ukv", w_ukv, m_w_ukv, v_w_ukv),
           ("ffn1_w2", ffn1_w2, m_ffn1_w2, v_ffn1_w2), ("ffn2_w2", ffn2_w2, m_ffn2_w2, v_ffn2_w2),
           ("w_out", w_out, m_w_out, v_w_out)]
    for name, w, m, v in big:
        res[name] = update(name, w, g_sh[name], m, v)
    smalls = [("ada_b", ada_b, g_ada_b, m_ada_b, v_ada_b),
              ("norm_ffn1_g", norm_ffn1_g, g_n1, m_norm_ffn1_g, v_norm_ffn1_g),
              ("norm_mix_g", norm_mix_g, g_nmix, m_norm_mix_g, v_norm_mix_g),
              ("conv_w", conv_w, g_conv, m_conv_w, v_conv_w),
              ("q_norm_g", q_norm_g, g_qg, m_q_norm_g, v_q_norm_g),
              ("kv_norm_g", kv_norm_g, g_kvg, m_kv_norm_g, v_kv_norm_g),
              ("out_norm_g", out_norm_g, g_og, m_out_norm_g, v_out_norm_g),
              ("norm_ffn2_g", norm_ffn2_g, g_n3, m_norm_ffn2_g, v_norm_ffn2_g),
              ("final_norm_g", final_norm_g, g_gf, m_final_norm_g, v_final_norm_g)]
    slens = [w.size for _, w, _, _, _ in smalls]
    soffs = [sum(slens[:i]) for i in range(len(slens))]
    s_len = -(-sum(slens) // 1024) * 1024

    def pack_small(i):
        return _pad_to(jnp.concatenate([s[i].reshape(-1) for s in smalls]), s_len).reshape(8, -1)

    s_out = adamw(pack_small(1), pack_small(2), pack_small(3), pack_small(4), "adamw_small")
    for (name, w, g, _, _), o, n in zip(smalls, soffs, slens):
        res[name] = (g.reshape(w.shape),) + tuple(a.reshape(-1)[o:o + n].reshape(w.shape) for a in s_out)

    order = ["ada_w", "ada_b", "norm_ffn1_g", "ffn1_w1", "ffn1_w3", "ffn1_w2", "norm_mix_g", "w_in", "conv_w",
             "q_norm_g", "w_uq", "kv_norm_g", "w_ukv", "out_norm_g", "w_out", "norm_ffn2_g", "ffn2_w1", "ffn2_w3",
             "ffn2_w2", "final_norm_g"]
    return (loss, dx0.reshape(x.shape), *[res[n][0] for n in order], *[res[n][1] for n in order],
            *[res[n][2] for n in order], *[res[n][3] for n in order])
```

```python
import functools
import math

import jax
import jax.numpy as jnp
from jax import lax
from jax.experimental import pallas as pl
from jax.experimental.pallas import tpu as pltpu

F32 = jnp.float32
BF16 = jnp.bfloat16
MESH_ID = pl.DeviceIdType.MESH
N_DEV = 8

EPS = 1e-6
CHUNK = 64
N_MOD = 9
CONV_WIDTH = 512
CONV_GROUPS = 8
CONV_K = 3
MLA_HEADS = 4
QK_NOPE = 128
QK_ROPE = 64
V_HEAD = 128
Q_LORA = 384
KV_LORA = 256
ROPE_THETA = 10000.0
MLA_WIDTH = MLA_HEADS * V_HEAD
MIX_WIDTH = CONV_WIDTH + MLA_WIDTH
IN_COLS = 3 * CONV_WIDTH + Q_LORA + KV_LORA + QK_ROPE
ZC_COLS = 3 * CONV_WIDTH
ZM_COLS = Q_LORA + KV_LORA + 128
HEAD_PAD = 256
QK_COLS = MLA_HEADS * HEAD_PAD
ATTN_SCALE = (QK_NOPE + QK_ROPE) ** -0.5
LOG2_E = 1.4426950408889634
LN_2 = 0.6931471805599453
QK_FOLD = ATTN_SCALE * LOG2_E
NEG_INF = -1e30

ADAM_LR = 0.001
ADAM_B1 = 0.9
ADAM_B2 = 0.999
ADAM_EPS = 1e-08
ADAM_WD = 0.01
ADAM_STEP = 10

LANES = 128
MXU_COLS = 256
VMEM_LIMIT = 56 * 1024 * 1024
ROW_TILE = 1024
FFN_FWD_TILE = (1024, 256)
FFN_BWD_TILE = (512, 1408)
GRAD_TILE = 1408
GRAD_DEPTH = 2048
SUM_ROWS = 256
ADAM_TILE_ELEMS = 1 << 19
ATTN_TILE = 1024

NN = (((1,), (0,)), ((), ()))
NT = (((1,), (1,)), ((), ()))
TN = (((0,), (0,)), ((), ()))


def _dot(a, b, dims=NN):
    return lax.dot_general(a, b, dims, preferred_element_type=F32)


def _tile(n, cap, mult=LANES):
    best = None
    for t in range(mult, min(n, cap) + 1, mult):
        if n % t == 0:
            best = t
    return n if best is None else best


def _params(sem=None):
    return pltpu.CompilerParams(dimension_semantics=sem, vmem_limit_bytes=VMEM_LIMIT)


def _row(v):
    return pl.BlockSpec(v.shape, lambda *_: (0,) * v.ndim)


def _sigmoid(x):
    return 0.5 * jnp.tanh(0.5 * x) + 0.5


def _rms(x):
    r = lax.rsqrt(jnp.mean(x * x, axis=-1, keepdims=True) + EPS)
    return x * r, r


def _norm_mod_bwd(dh, x, gn, sc):
    xhat, r = _rms(x)
    d_sh = jnp.sum(dh, axis=0, keepdims=True)
    d_sc = jnp.sum(dh * (xhat * gn), axis=0, keepdims=True)
    dxn = dh * (1.0 + sc)
    d_gn = jnp.sum(dxn * xhat, axis=0, keepdims=True)
    dxh = dxn * gn
    dx = r * (dxh - xhat * jnp.mean(dxh * xhat, axis=-1, keepdims=True))
    return dx, d_sh, d_sc, d_gn


def _group_mean(v, gmat):
    return _dot(v.astype(BF16), gmat)


def _add_rows(ref, rows):
    for r, v in enumerate(rows):
        ref[r:r + 1, :] += v


def _window(ref, axis, j):
    return ref.at[(slice(None),) * axis + (j,)]


def _any_specs(n):
    return [pl.BlockSpec(memory_space=pl.ANY)] * n


def all_gather(blocks, axes, name):
    n_arr = len(blocks)

    def body(*refs):
        start, forward, finish = _gather_steps(refs[:n_arr], refs[n_arr:2 * n_arr], axes, *refs[2 * n_arr:])
        start()
        for j in range(3):
            forward(j)
        finish()

    return pl.pallas_call(
        body, name=name, out_shape=_gathered_shapes(blocks, axes),
        in_specs=_any_specs(n_arr), out_specs=_any_specs(n_arr), scratch_shapes=_gather_sems(n_arr),
    )(*blocks)


def all_gather_relayed(blocks, axes, name):
    n_arr = len(blocks)
    arrays = range(n_arr)

    def body(*refs):
        ins, outs = refs[:n_arr], refs[n_arr:2 * n_arr]
        send_sems, recv_sems, local_sems = refs[2 * n_arr:]
        x, y, c = lax.axis_index("x"), lax.axis_index("y"), lax.axis_index("c")
        sibling, x_nbr, y_nbr, diagonal = (x, y, 1 - c), (1 - x, y, c), (x, 1 - y, c), (1 - x, 1 - y, c)
        north = c == 1
        relay_slot = jnp.where(north, 1, 2)
        relay_from = tuple(jnp.where(north, a, b) for a, b in zip(x_nbr, y_nbr))
        relay_to = tuple(jnp.where(north, a, b) for a, b in zip(y_nbr, x_nbr))
        other_from = relay_to

        def slot(a, px, py, pc):
            return _window(outs[a], axes[a], 4 * px + 2 * py + pc)

        def copy(a, k, block, to, src=None):
            return pltpu.make_async_remote_copy(
                src_ref=slot(a, *block) if src is None else src, dst_ref=slot(a, *block),
                send_sem=send_sems.at[k, a], recv_sem=recv_sems.at[k, a], device_id=to, device_id_type=MESH_ID)

        mine = [pltpu.make_async_copy(ins[a], slot(a, x, y, c), local_sems.at[a]) for a in arrays]
        for cp in mine:
            cp.start()
        first = [copy(a, k, (x, y, c), to, src=ins[a])
                 for k, to in enumerate((sibling, x_nbr, y_nbr)) for a in arrays]
        for cp in first:
            cp.start()
        later = []
        for a in arrays:
            copy(a, relay_slot, relay_from, (x, y, c)).wait_recv()
            later += [copy(a, 3, relay_from, relay_to), copy(a, 3 + relay_slot, relay_from, sibling)]
            later[-2].start()
            later[-1].start()
        for a in arrays:
            copy(a, 3 - relay_slot, other_from, (x, y, c)).wait_recv()
            later.append(copy(a, 6 - relay_slot, other_from, sibling))
            later[-1].start()
        for a in arrays:
            copy(a, 3, diagonal, (x, y, c)).wait_recv()
            later.append(copy(a, 6, diagonal, sibling))
            later[-1].start()
        for a in arrays:
            for k, block in ((0, sibling), (4, (1 - x, y, 1 - c)), (5, (x, 1 - y, 1 - c)), (6, (1 - x, 1 - y, 1 - c))):
                copy(a, k, block, (x, y, c)).wait_recv()
        for cp in first + later:
            cp.wait_send()
        for cp in mine:
            cp.wait()

    return pl.pallas_call(
        body, name=name, out_shape=_gathered_shapes(blocks, axes),
        in_specs=_any_specs(n_arr), out_specs=_any_specs(n_arr), scratch_shapes=_gather_sems(n_arr),
    )(*blocks)


def _gathered_shapes(blocks, axes):
    return [jax.ShapeDtypeStruct(b.shape[:ax] + (N_DEV,) + b.shape[ax:], b.dtype) for b, ax in zip(blocks, axes)]


def _gather_sems(n_arr):
    return [pltpu.SemaphoreType.DMA((7, n_arr)), pltpu.SemaphoreType.DMA((7, n_arr)), pltpu.SemaphoreType.DMA((n_arr,))]


def _gather_steps(ins, outs, axes, send_sems, recv_sems, local_sems):
    arrays = range(len(ins))
    x, y, c = lax.axis_index("x"), lax.axis_index("y"), lax.axis_index("c")
    me, sibling = (x, y, c), (x, y, 1 - c)
    chips = [(1 - x, y), (x, 1 - y), (1 - x, 1 - y)]

    def slot(a, px, py, pc):
        return _window(outs[a], axes[a], 4 * px + 2 * py + pc)

    def copy(a, k, block, to, src=None):
        return pltpu.make_async_remote_copy(
            src_ref=slot(a, *block) if src is None else src, dst_ref=slot(a, *block),
            send_sem=send_sems.at[k, a], recv_sem=recv_sems.at[k, a], device_id=to, device_id_type=MESH_ID)

    def mine(a):
        return pltpu.make_async_copy(ins[a], slot(a, *me), local_sems.at[a])

    def first():
        return ([copy(a, 0, me, sibling, src=ins[a]) for a in arrays]
                + [copy(a, 1 + j, me, (*chip, c), src=ins[a]) for j, chip in enumerate(chips) for a in arrays])

    def passed(j):
        return [copy(a, 4 + j, (*chips[j], c), sibling) for a in arrays]

    def start():
        for a in arrays:
            mine(a).start()
        for cp in first():
            cp.start()

    def forward(j):
        for a, cp in zip(arrays, passed(j)):
            copy(a, 1 + j, (*chips[j], c), me).wait_recv()
            cp.start()

    def finish():
        for a in arrays:
            copy(a, 0, sibling, me).wait_recv()
        for j, chip in enumerate(chips):
            for a in arrays:
                copy(a, 4 + j, (*chip, 1 - c), me).wait_recv()
        for cp in first() + passed(0) + passed(1) + passed(2):
            cp.wait_send()
        for a in arrays:
            mine(a).wait()

    return start, forward, finish


def exchange_sibling(grads, name):
    n_arr = len(grads)

    def body(*refs):
        start, finish = _sibling_exchange_steps(refs[:n_arr], refs[n_arr:2 * n_arr], *refs[2 * n_arr:])
        start()
        finish()

    return pl.pallas_call(
        body, name=name, out_shape=_sibling_shapes(grads),
        in_specs=_any_specs(n_arr), out_specs=_any_specs(n_arr), scratch_shapes=_exchange_sems(n_arr),
    )(*grads)


def _sibling_shapes(grads):
    return [jax.ShapeDtypeStruct((4,) + g.shape[1:], g.dtype) for g in grads]


def _exchange_sems(n_arr):
    return [pltpu.SemaphoreType.DMA((n_arr,)), pltpu.SemaphoreType.DMA((n_arr,))]


def _sibling_exchange_steps(ins, outs, send_sems, recv_sems):
    x, y, c = lax.axis_index("x"), lax.axis_index("y"), lax.axis_index("c")

    def copy(a, src, dst):
        return pltpu.make_async_remote_copy(
            src_ref=src, dst_ref=dst, send_sem=send_sems.at[a], recv_sem=recv_sems.at[a],
            device_id=(x, y, 1 - c), device_id_type=MESH_ID)

    def start():
        for a in range(len(ins)):
            for k in range(4):
                copy(a, ins[a].at[2 * k + (1 - c)], outs[a].at[k]).start()

    def finish():
        whole = [copy(a, ins[a].at[pl.ds(0, 4)], outs[a]) for a in range(len(ins))]
        for cp in whole:
            cp.wait_recv()
        for cp in whole:
            cp.wait_send()

    return start, finish


def _chip_exchange_steps(ins, outs, send_sems, recv_sems):
    x, y, c = lax.axis_index("x"), lax.axis_index("y"), lax.axis_index("c")
    chips = [(1 - x, y), (x, 1 - y), (1 - x, 1 - y)]

    def copy(a, src, dst, chip):
        return pltpu.make_async_remote_copy(
            src_ref=src, dst_ref=dst, send_sem=send_sems.at[a], recv_sem=recv_sems.at[a],
            device_id=(*chip, c), device_id_type=MESH_ID)

    def start():
        for a in range(len(ins)):
            for j, chip in enumerate(chips):
                copy(a, ins[a].at[j], outs[a].at[j], chip).start()

    def finish():
        whole = [copy(a, ins[a], outs[a], chips[0]) for a in range(len(ins))]
        for cp in whole:
            cp.wait_recv()
        for cp in whole:
            cp.wait_send()

    return start, finish


def riding_gather(blocks, axes):
    def phases(ins, outs, *sems):
        start, forward, finish = _gather_steps(ins, outs, axes, *sems)
        return [start] + [functools.partial(forward, j) for j in range(3)] + [finish]

    return dict(operands=blocks, out_shape=_gathered_shapes(blocks, axes), sems=_gather_sems(len(blocks)),
                phases=phases, when=("first", "late0", "late1", "late2", "last"))


def riding_exchange(parts):
    def phases(ins, outs, *sems):
        return list(_chip_exchange_steps(ins, outs, *sems))

    return dict(operands=parts, out_shape=[jax.ShapeDtypeStruct(p.shape, p.dtype) for p in parts],
                sems=_exchange_sems(len(parts)), phases=phases, when=("first", "last"))


def riding_sibling(grads):
    def phases(ins, outs, *sems):
        return list(_sibling_exchange_steps(ins, outs, *sems))

    return dict(operands=grads, out_shape=_sibling_shapes(grads), sems=_exchange_sems(len(grads)),
                phases=phases, when=("first", "last"))


def _call_with_rider(body, rider, *, name, grid, in_specs, out_specs, out_shape, scratch_shapes, operands):
    params = _params(("arbitrary",) * len(grid))
    if rider is None:
        return pl.pallas_call(body, name=name, grid=grid, in_specs=in_specs, out_specs=out_specs,
                              out_shape=out_shape, scratch_shapes=scratch_shapes, compiler_params=params)(*operands)
    n_in, n_out, n_scr, k = len(in_specs), len(out_specs), len(scratch_shapes), len(rider["operands"])
    at = {"first": (0,) * len(grid), "last": tuple(g - 1 for g in grid)}
    if "late0" in rider["when"]:
        rows, cols = grid
        assert cols >= 3
        at.update({"late%d" % j: (max(rows - 2, 0), j) for j in range(3)})

    def wrapped(*refs):
        ins, c_in = refs[:n_in], refs[n_in:n_in + k]
        outs, c_out = refs[n_in + k:n_in + k + n_out], refs[n_in + k + n_out:n_in + 2 * k + n_out]
        scratch, sems = refs[n_in + 2 * k + n_out:n_in + 2 * k + n_out + n_scr], refs[n_in + 2 * k + n_out + n_scr:]
        pos = [pl.program_id(axis) for axis in range(len(grid))]

        def here(key):
            return functools.reduce(jnp.logical_and, [p == v for p, v in zip(pos, at[key])])

        phases = rider["phases"](c_in, c_out, *sems)
        for fn, key in zip(phases, rider["when"]):
            if key != "last":
                pl.when(here(key))(fn)
        body(*ins, *outs, *scratch)
        pl.when(here("last"))(phases[-1])

    return pl.pallas_call(
        wrapped, name=name, grid=grid,
        in_specs=list(in_specs) + _any_specs(k), out_specs=list(out_specs) + _any_specs(k),
        out_shape=list(out_shape) + rider["out_shape"], scratch_shapes=list(scratch_shapes) + rider["sems"],
        compiler_params=params)(*operands, *rider["operands"])


def add_sibling(g8, got, src_idx, chip_idx, name):
    _, r, n = g8.shape
    tr = _tile(r, SUM_ROWS, 16)

    def body(si_ref, ci_ref, g0_ref, g1_ref, g2_ref, g3_ref, got_ref, own_ref, send_ref):
        own_ref[...] = g0_ref[0] + got_ref[ci_ref[0]]
        for j, g_ref in enumerate((g1_ref, g2_ref, g3_ref)):
            send_ref[j] = (g_ref[0] + got_ref[ci_ref[j + 1]]).astype(BF16)

    def mine(j):
        return pl.BlockSpec((1, tr, n), lambda i, si, ci: (si[j], i, 0))

    return pl.pallas_call(
        body, name=name,
        out_shape=[jax.ShapeDtypeStruct((r, n), F32), jax.ShapeDtypeStruct((3, r, n), BF16)],
        grid_spec=pltpu.PrefetchScalarGridSpec(
            num_scalar_prefetch=2, grid=(r // tr,),
            in_specs=[mine(0), mine(1), mine(2), mine(3), pl.BlockSpec((4, tr, n), lambda i, si, ci: (0, i, 0))],
            out_specs=[pl.BlockSpec((tr, n), lambda i, si, ci: (i, 0)),
                       pl.BlockSpec((3, tr, n), lambda i, si, ci: (0, i, 0))]),
        compiler_params=_params(("arbitrary",)),
    )(src_idx, chip_idx, g8, g8, g8, g8, got)


def sum_devices(g):
    def body(g_ref, o_ref):
        acc = g_ref[0]
        for j in range(1, N_DEV):
            acc = acc + g_ref[j]
        o_ref[...] = acc

    return pl.pallas_call(body, name="sum_devices", out_shape=jax.ShapeDtypeStruct(g.shape[1:], F32))(g)


def sum_lanes(v):
    def body(v_ref, o_ref):
        o_ref[...] = jnp.broadcast_to(jnp.sum(v_ref[...], axis=-1, keepdims=True), (1, LANES))

    return pl.pallas_call(body, name="sum_lanes", out_shape=jax.ShapeDtypeStruct((1, LANES), F32))(v)


def ada_forward(c_all, ada_w, ada_b_cols):
    nb, n = c_all.shape[0], ada_w.shape[1]

    def body(c_ref, w_ref, b_ref, o_ref):
        cv = c_ref[...]
        s = (cv * jax.nn.sigmoid(cv)).astype(BF16)
        o_ref[...] = _dot(s, w_ref[...].astype(BF16)) + b_ref[...]

    return pl.pallas_call(body, name="ada_fwd", out_shape=jax.ShapeDtypeStruct((nb, n), F32),
                          compiler_params=_params())(c_all, ada_w, ada_b_cols)


def ada_backward(c_all16, dmod16):
    d, n = c_all16.shape[1], dmod16.shape[1]

    def body(c_ref, g_ref, o_ref):
        cv = c_ref[...]
        s = (cv * jax.nn.sigmoid(cv)).astype(BF16)
        o_ref[...] = _dot(s, g_ref[...].astype(BF16), TN)

    return pl.pallas_call(body, name="ada_bwd", out_shape=jax.ShapeDtypeStruct((d, n), F32),
                          compiler_params=_params())(c_all16, dmod16)


def ffn_forward(x, gn, sc, sh, gate, ws, first, name, rider=None):
    t, d = x.shape
    f = ws.shape[1]
    tm, tf = _tile(t, FFN_FWD_TILE[0], 16), _tile(f, FFN_FWD_TILE[1])
    nf = f // tf

    def body(x_ref, gn_ref, sc_ref, sh_ref, gate_ref, w1_ref, w3_ref, w2_ref,
             xo_ref, h_ref, a_ref, b_ref, y_ref, hs, acc):
        j = pl.program_id(1)

        @pl.when(j == 0)
        def _():
            xhat, _ = _rms(x_ref[...])
            h = (xhat * gn_ref[...] * (1.0 + sc_ref[...]) + sh_ref[...]).astype(BF16)
            hs[...] = h
            h_ref[...] = h
            acc[...] = jnp.zeros_like(acc)

        h = hs[...]
        a = _dot(h, w1_ref[...], NT)
        b = _dot(h, w3_ref[...], NT)
        a_ref[...] = a.astype(BF16)
        b_ref[...] = b.astype(BF16)
        u = (a * _sigmoid(a) * b).astype(BF16)
        acc[...] += _dot(u, w2_ref[...])

        @pl.when(j == nf - 1)
        def _():
            y = acc[...]
            y_ref[...] = y.astype(BF16)
            xo_ref[...] = x_ref[...] + 0.5 * gate_ref[...] * y

    row = pl.BlockSpec((tm, d), lambda i, j: (i, 0))
    vec = pl.BlockSpec((1, d), lambda i, j: (0, 0))
    wide = pl.BlockSpec((tm, tf), lambda i, j: (i, j))
    return _call_with_rider(
        body, rider, name=name, grid=(t // tm, nf),
        in_specs=[row, vec, vec, vec, vec] + _ffn_weight_specs(first, tf, d),
        out_specs=[row, row, wide, wide, row],
        out_shape=[jax.ShapeDtypeStruct((t, d), F32), jax.ShapeDtypeStruct((t, d), BF16),
                   jax.ShapeDtypeStruct((t, f), BF16), jax.ShapeDtypeStruct((t, f), BF16),
                   jax.ShapeDtypeStruct((t, d), BF16)],
        scratch_shapes=[pltpu.VMEM((tm, d), BF16), pltpu.VMEM((tm, d), F32)],
        operands=(x, gn, sc, sh, gate, ws, ws, ws))


def _ffn_weight_specs(first, tf, d):
    return [pl.BlockSpec((None, tf, d), lambda i, j, w=first + k: (w, j, 0)) for k in range(3)]


def ffn_backward_gate(dy, a, b, ws, first, name, rider=None):
    t, d = dy.shape
    f = ws.shape[1]
    tm, tf = _tile(t, FFN_BWD_TILE[0], 16), _tile(f, FFN_BWD_TILE[1])
    nf = f // tf

    def gate_body(dy_ref, a_ref, b_ref, w2_ref, da_ref, db_ref, gw2_ref):
        dy_v = dy_ref[...]
        du = _dot(dy_v, w2_ref[...], NT)
        av = a_ref[...].astype(F32)
        bv = b_ref[...].astype(F32)
        s = _sigmoid(av)
        sa = av * s
        da_ref[...] = (du * bv * (s + sa * (1.0 - s))).astype(BF16)
        db_ref[...] = (du * sa).astype(BF16)
        part = _dot((sa * bv).astype(BF16), dy_v, TN)

        @pl.when(pl.program_id(1) == 0)
        def _():
            gw2_ref[...] = part

        @pl.when(pl.program_id(1) > 0)
        def _():
            gw2_ref[...] += part

    hidden = jax.ShapeDtypeStruct((t, f), BF16)
    wide_t = pl.BlockSpec((tm, tf), lambda j, i: (i, j))
    return _call_with_rider(
        gate_body, rider, name=name, grid=(nf, t // tm),
        in_specs=[pl.BlockSpec((tm, d), lambda j, i: (i, 0)), wide_t, wide_t,
                  pl.BlockSpec((None, tf, d), lambda j, i: (first + 2, j, 0))],
        out_specs=[wide_t, wide_t, pl.BlockSpec((tf, d), lambda j, i: (j, 0))],
        out_shape=[hidden, hidden, jax.ShapeDtypeStruct((f, d), F32)],
        scratch_shapes=[], operands=(dy, a, b, ws))


def ffn_backward_norm(da, db, dxo, x, y, gn, sc, ws, first, name, rider=None):
    t, d = x.shape
    f = ws.shape[1]
    tm, tf = _tile(t, FFN_BWD_TILE[0], 16), _tile(f, FFN_BWD_TILE[1])
    nf = f // tf
    row = pl.BlockSpec((tm, d), lambda i, j: (i, 0))
    vec = pl.BlockSpec((1, d), lambda i, j: (0, 0))
    wide = pl.BlockSpec((tm, tf), lambda i, j: (i, j))

    def norm_body(da_ref, db_ref, w1_ref, w3_ref, dxo_ref, x_ref, y_ref, gn_ref, sc_ref, dx_ref, sums_ref, acc):
        i, j = pl.program_id(0), pl.program_id(1)

        @pl.when(jnp.logical_and(i == 0, j == 0))
        def _():
            sums_ref[...] = jnp.zeros_like(sums_ref)

        part = _dot(da_ref[...], w1_ref[...]) + _dot(db_ref[...], w3_ref[...])

        @pl.when(j == 0)
        def _():
            acc[...] = part

        @pl.when(jnp.logical_and(j > 0, j < nf - 1))
        def _():
            acc[...] += part

        @pl.when(j == nf - 1)
        def _():
            dh = part if nf == 1 else acc[...] + part
            dxo_v = dxo_ref[...]
            dx, d_sh, d_sc, d_gn = _norm_mod_bwd(dh, x_ref[...], gn_ref[...], sc_ref[...])
            dx_ref[...] = dxo_v + dx
            d_gate = jnp.sum(dxo_v * (0.5 * y_ref[...].astype(F32)), axis=0, keepdims=True)
            _add_rows(sums_ref, [d_sh, d_sc, d_gate, d_gn])

    w1_spec, w3_spec, _ = _ffn_weight_specs(first, tf, d)
    return _call_with_rider(
        norm_body, rider, name=name, grid=(t // tm, nf),
        in_specs=[wide, wide, w1_spec, w3_spec, row, row, row, vec, vec],
        out_specs=[row, pl.BlockSpec((8, d), lambda i, j: (0, 0))],
        out_shape=[jax.ShapeDtypeStruct((t, d), F32), jax.ShapeDtypeStruct((8, d), F32)],
        scratch_shapes=[pltpu.VMEM((tm, d), F32)],
        operands=(da, db, ws, ws, dxo, x, y, gn, sc))


def matmul_tn(a, b, name, rider=None):
    parts = list(a) if isinstance(a, (list, tuple)) else [a]
    t, n = b.shape
    widths = [p.shape[1] for p in parts]
    tm = _tile(functools.reduce(math.gcd, widths), GRAD_TILE)
    tn, tk = _tile(n, GRAD_TILE), _tile(t, GRAD_DEPTH, 16)
    nk = t // tk
    counts = [w // tm for w in widths]
    firsts = [sum(counts[:p]) for p in range(len(parts))]

    def body(*refs):
        a_refs, (b_ref, o_ref, acc) = refs[:len(parts)], refs[len(parts):]
        i, k = pl.program_id(0), pl.program_id(2)

        @pl.when(k == 0)
        def _():
            acc[...] = jnp.zeros_like(acc)

        for a_ref, lo, cnt in zip(a_refs, firsts, counts):
            def accumulate(a_ref=a_ref):
                acc[...] += _dot(a_ref[...], b_ref[...], TN)

            if len(parts) == 1:
                accumulate()
            else:
                pl.when(jnp.logical_and(i >= lo, i < lo + cnt))(accumulate)

        @pl.when(k == nk - 1)
        def _():
            o_ref[...] = acc[...]

    def part_spec(lo, cnt):
        if len(parts) == 1:
            return pl.BlockSpec((tk, tm), lambda i, j, k: (k, i))

        def index(i, j, k):
            mine = jnp.logical_and(i >= lo, i < lo + cnt)
            return jnp.where(mine, k, 0), jnp.clip(i - lo, 0, cnt - 1)
        return pl.BlockSpec((tk, tm), index)

    out = _call_with_rider(
        body, rider, name=name, grid=(sum(counts), n // tn, nk),
        in_specs=[part_spec(lo, cnt) for lo, cnt in zip(firsts, counts)]
        + [pl.BlockSpec((tk, tn), lambda i, j, k: (k, j))],
        out_specs=[pl.BlockSpec((tm, tn), lambda i, j, k: (i, j))],
        out_shape=[jax.ShapeDtypeStruct((sum(widths), n), F32)],
        scratch_shapes=[pltpu.VMEM((tm, tn), F32)], operands=(*parts, b))
    return out[0] if rider is None else out


def mix_in_forward(x, gn, sc, sh, w_in):
    t, d = x.shape
    tm = _tile(t, ROW_TILE, 16)

    def body(x_ref, gn_ref, sc_ref, sh_ref, w_ref, h_ref, zc_ref, zm_ref):
        xhat, _ = _rms(x_ref[...])
        h = (xhat * gn_ref[...] * (1.0 + sc_ref[...]) + sh_ref[...]).astype(BF16)
        h_ref[...] = h
        z = _dot(h, w_ref[...], NT)
        zc_ref[...] = z[:, :ZC_COLS].astype(BF16)
        zm_ref[...] = z[:, ZC_COLS:].astype(BF16)

    row = pl.BlockSpec((tm, d), lambda i: (i, 0))
    vec = pl.BlockSpec((1, d), lambda i: (0, 0))
    return pl.pallas_call(
        body, name="mix_in_fwd", grid=(t // tm,),
        in_specs=[row, vec, vec, vec, _row(w_in)],
        out_specs=[row, pl.BlockSpec((tm, ZC_COLS), lambda i: (i, 0)), pl.BlockSpec((tm, ZM_COLS), lambda i: (i, 0))],
        out_shape=[jax.ShapeDtypeStruct((t, d), BF16), jax.ShapeDtypeStruct((t, ZC_COLS), BF16),
                   jax.ShapeDtypeStruct((t, ZM_COLS), BF16)],
        compiler_params=_params(("arbitrary",)),
    )(x, gn, sc, sh, w_in)


def _rope_tables(pos, inv_freq):
    ang = pos * inv_freq
    lane = lax.broadcasted_iota(jnp.int32, ang.shape, 1)
    cos, sin = jnp.cos(ang), jnp.sin(ang)
    half = QK_ROPE // 2
    return cos, jnp.where(lane < half, -sin, 0.0), jnp.where(jnp.logical_and(lane >= half, lane < QK_ROPE), sin, 0.0)


def _rope(v, tables):
    cos, sin_a, sin_b = tables
    return v * cos + pltpu.roll(v, LANES - QK_ROPE // 2, 1) * sin_a + pltpu.roll(v, QK_ROPE // 2, 1) * sin_b


def _rope_transposed(dv, tables):
    cos, sin_a, sin_b = tables
    return dv * cos + pltpu.roll(dv * sin_a, QK_ROPE // 2, 1) + pltpu.roll(dv * sin_b, LANES - QK_ROPE // 2, 1)


def mla_project(zm, pos, inv_freq, qg, kvg, w_uq, w_ukv):
    t = zm.shape[0]
    tm = _tile(t, ROW_TILE, 16)

    def body(zm_ref, pos_ref, if_ref, qg_ref, kvg_ref, wq_ref, wkv_ref, qn_ref, kvn_ref, q_ref, k_ref, v_ref):
        zv = zm_ref[...].astype(F32)
        qn = (_rms(zv[:, :Q_LORA])[0] * qg_ref[...]).astype(BF16)
        kvn = (_rms(zv[:, Q_LORA:Q_LORA + KV_LORA])[0] * kvg_ref[...]).astype(BF16)
        qn_ref[...] = qn
        kvn_ref[...] = kvn
        qf = _dot(qn, wq_ref[...], NT) * QK_FOLD
        kvf = _dot(kvn, wkv_ref[...], NT)
        tables = _rope_tables(pos_ref[...], if_ref[...])
        kr = _rope(zv[:, Q_LORA + KV_LORA:], tables).astype(BF16)
        for h in range(MLA_HEADS):
            lo = h * HEAD_PAD
            q_ref[:, lo:lo + QK_NOPE] = qf[:, lo:lo + QK_NOPE].astype(BF16)
            q_ref[:, lo + QK_NOPE:lo + HEAD_PAD] = _rope(qf[:, lo + QK_NOPE:lo + HEAD_PAD], tables).astype(BF16)
            k_ref[:, lo:lo + QK_NOPE] = kvf[:, h * QK_NOPE:(h + 1) * QK_NOPE].astype(BF16)
            k_ref[:, lo + QK_NOPE:lo + HEAD_PAD] = kr
        v_ref[...] = kvf[:, MLA_HEADS * QK_NOPE:].astype(BF16)

    def rows(n):
        return pl.BlockSpec((tm, n), lambda i: (i, 0))

    return pl.pallas_call(
        body, name="mla_project", grid=(t // tm,),
        in_specs=[rows(ZM_COLS), rows(1), _row(inv_freq), _row(qg), _row(kvg), _row(w_uq), _row(w_ukv)],
        out_specs=[rows(Q_LORA), rows(KV_LORA), rows(QK_COLS), rows(QK_COLS), rows(MLA_WIDTH)],
        out_shape=[jax.ShapeDtypeStruct((t, Q_LORA), BF16), jax.ShapeDtypeStruct((t, KV_LORA), BF16),
                   jax.ShapeDtypeStruct((t, QK_COLS), BF16), jax.ShapeDtypeStruct((t, QK_COLS), BF16),
                   jax.ShapeDtypeStruct((t, MLA_WIDTH), BF16)],
        compiler_params=_params(("arbitrary",)),
    )(zm, pos, inv_freq, qg, kvg, w_uq, w_ukv)


def _chunk_mask(shape, q_axis):
    qi = lax.broadcasted_iota(jnp.int32, shape, q_axis) // CHUNK
    ki = lax.broadcasted_iota(jnp.int32, shape, 1 - q_axis) // CHUNK
    return ki <= qi


def attention_forward(q, k, v, rider=None):
    t = q.shape[0]
    tq = _tile(t, ATTN_TILE, CHUNK)

    def body(q_ref, k_ref, v_ref, o_ref, lse_ref):
        i = pl.program_id(1)
        qv = q_ref[...]

        def step(kb, carry, masked, tiles=1):
            m, l, acc = carry
            keys = pl.ds(pl.multiple_of(kb * tq, tq), tiles * tq)
            s = _dot(qv, k_ref[keys, :], NT)
            if masked:
                s = jnp.where(_chunk_mask(s.shape, 0), s, NEG_INF)
            m_new = jnp.maximum(m, jnp.max(s, axis=-1, keepdims=True))
            alpha = jnp.exp2(m - m_new)
            p = jnp.exp2(s - m_new)
            l = alpha * l + jnp.sum(p, axis=-1, keepdims=True)
            acc = alpha * acc + _dot(p.astype(BF16), v_ref[keys, :])
            return m_new, l, acc

        init = (jnp.full((tq, 1), NEG_INF, F32), jnp.zeros((tq, 1), F32), jnp.zeros((tq, V_HEAD), F32))
        carry = lax.fori_loop(0, i // 2, lambda pb, cr: step(2 * pb, cr, False, 2), init)
        carry = lax.fori_loop(0, i % 2, lambda _, cr: step(i - 1, cr, False), carry)
        m, l, acc = step(i, carry, True)
        o_ref[...] = (acc / l).astype(BF16)
        lse_ref[0] = m + jnp.log2(l)

    return _call_with_rider(
        body, rider, name="attn_fwd", grid=(MLA_HEADS, t // tq),
        in_specs=[pl.BlockSpec((tq, HEAD_PAD), lambda h, i: (i, h)),
                  pl.BlockSpec((t, HEAD_PAD), lambda h, i: (0, h)),
                  pl.BlockSpec((t, V_HEAD), lambda h, i: (0, h))],
        out_specs=[pl.BlockSpec((tq, V_HEAD), lambda h, i: (i, h)),
                   pl.BlockSpec((1, tq, 1), lambda h, i: (h, i, 0))],
        out_shape=[jax.ShapeDtypeStruct((t, MLA_WIDTH), BF16), jax.ShapeDtypeStruct((MLA_HEADS, t, 1), F32)],
        scratch_shapes=[], operands=(q, k, v))


def attention_backward(q, k, v, do, lse, delta, rider=None):
    t = q.shape[0]
    tq = _tile(t, ATTN_TILE, CHUNK)
    nq = t // tq

    def body(q_ref, k_ref, v_ref, do_ref, lse_ref, delta_ref, dq_ref, dk_ref, dv_ref, dq_acc):
        kb = pl.program_id(1)

        @pl.when(kb == 0)
        def _():
            dq_acc[...] = jnp.zeros_like(dq_acc)

        kv, vv = k_ref[...], v_ref[...]

        def step(qb, carry, masked):
            dk, dv = carry
            rows = pl.ds(pl.multiple_of(qb * tq, tq), tq)
            qv, dov = q_ref[rows, :], do_ref[rows, :]
            s = _dot(kv, qv, NT)
            if masked:
                s = jnp.where(_chunk_mask(s.shape, 1), s, NEG_INF)
            p = jnp.exp2(s - lse_ref[0, qb])
            dv = dv + _dot(p.astype(BF16), dov)
            dp = _dot(vv, dov, NT)
            ds = (p * (dp - delta_ref[0, qb]) * LN_2).astype(BF16)
            dk = dk + _dot(ds, qv)
            dq_acc[rows, :] += _dot(ds, kv, TN)
            return dk, dv

        carry = step(kb, (jnp.zeros((tq, HEAD_PAD), F32), jnp.zeros((tq, V_HEAD), F32)), True)
        odd = (nq - 1 - kb) % 2
        carry = lax.fori_loop(0, odd, lambda _, cr: step(kb + 1, cr, False), carry)
        first = kb + 1 + odd
        dk, dv = lax.fori_loop(0, (nq - first) // 2,
                               lambda pb, cr: step(first + 2 * pb + 1, step(first + 2 * pb, cr, False), False), carry)
        dk_ref[...] = dk.astype(BF16)
        dv_ref[...] = dv.astype(BF16)

        @pl.when(kb == nq - 1)
        def _():
            dq_ref[...] = dq_acc[...].astype(BF16)

    stat = pl.BlockSpec((1, nq, 1, tq), lambda h, j: (h, 0, 0, 0))
    return _call_with_rider(
        body, rider, name="attn_bwd", grid=(MLA_HEADS, nq),
        in_specs=[pl.BlockSpec((t, HEAD_PAD), lambda h, j: (0, h)),
                  pl.BlockSpec((tq, HEAD_PAD), lambda h, j: (j, h)),
                  pl.BlockSpec((tq, V_HEAD), lambda h, j: (j, h)),
                  pl.BlockSpec((t, V_HEAD), lambda h, j: (0, h)), stat, stat],
        out_specs=[pl.BlockSpec((t, HEAD_PAD), lambda h, j: (0, h)),
                   pl.BlockSpec((tq, HEAD_PAD), lambda h, j: (j, h)),
                   pl.BlockSpec((tq, V_HEAD), lambda h, j: (j, h))],
        out_shape=[jax.ShapeDtypeStruct((t, QK_COLS), BF16), jax.ShapeDtypeStruct((t, QK_COLS), BF16),
                   jax.ShapeDtypeStruct((t, MLA_WIDTH), BF16)],
        scratch_shapes=[pltpu.VMEM((t, HEAD_PAD), F32)], operands=(q, k, v, do, lse, delta))


HALO = 16


def _halo_spec(tm, n, step, last):
    return pl.BlockSpec((HALO, n), lambda i: (jnp.clip(i * (tm // HALO) + step, 0, last), 0))


def _shift_rows(v, prev, n):
    out = pltpu.roll(v, n, 0)
    row = lax.broadcasted_iota(jnp.int32, v.shape, 0)
    for r in range(n):
        out = jnp.where(row == r, prev[HALO - n + r:HALO - n + r + 1, :], out)
    return out


def _advance_rows(v, nxt, n):
    rows = v.shape[0]
    out = pltpu.roll(v, rows - n, 0)
    row = lax.broadcasted_iota(jnp.int32, v.shape, 0)
    for r in range(n):
        out = jnp.where(row == rows - n + r, nxt[r:r + 1, :], out)
    return out


def _conv_taps(zc, zc_prev, first):
    w = CONV_WIDTH
    u = zc[:, w:2 * w] * zc[:, 2 * w:]
    up = jnp.where(first, 0.0, zc_prev[:, w:2 * w] * zc_prev[:, 2 * w:])
    return u, _shift_rows(u, up, 1), _shift_rows(u, up, 2)


def mix_out_forward(zc, o, conv_w, og, gmat_a, gmat_b, w_out, x, gate):
    t, d = x.shape
    tm = _tile(t, ROW_TILE, 16)
    w = CONV_WIDTH

    def body(zc_ref, zp_ref, o_ref, cw_ref, og_ref, ga_ref, gb_ref, w_ref, x_ref, gate_ref,
             xo_ref, yn_ref, y_ref, ya_ref):
        zc_v = zc_ref[...].astype(F32)
        u, u1, u2 = _conv_taps(zc_v, zp_ref[...].astype(F32), pl.program_id(0) == 0)
        cw = cw_ref[...]
        ya = zc_v[:, :w] * (cw[0:1] * u2 + cw[1:2] * u1 + cw[2:3] * u)
        ya_ref[...] = ya.astype(BF16)
        ov = o_ref[...].astype(F32)
        ogv = og_ref[...]
        yn_ref[:, :w] = (ya * lax.rsqrt(_group_mean(ya * ya, ga_ref[...]) + EPS) * ogv[:, :w]).astype(BF16)
        yn_ref[:, w:] = (ov * lax.rsqrt(_group_mean(ov * ov, gb_ref[...]) + EPS) * ogv[:, w:]).astype(BF16)
        y = _dot(yn_ref[...], w_ref[...])
        y_ref[...] = y.astype(BF16)
        xo_ref[...] = x_ref[...] + gate_ref[...] * y

    def rows(n):
        return pl.BlockSpec((tm, n), lambda i: (i, 0))

    return pl.pallas_call(
        body, name="mix_out_fwd", grid=(t // tm,),
        in_specs=[rows(ZC_COLS), _halo_spec(tm, ZC_COLS, -1, t // HALO - 1), rows(MLA_WIDTH), _row(conv_w), _row(og),
                  _row(gmat_a), _row(gmat_b), _row(w_out), rows(d), _row(gate)],
        out_specs=[rows(d), rows(MIX_WIDTH), rows(d), rows(w)],
        out_shape=[jax.ShapeDtypeStruct((t, d), F32), jax.ShapeDtypeStruct((t, MIX_WIDTH), BF16),
                   jax.ShapeDtypeStruct((t, d), BF16), jax.ShapeDtypeStruct((t, w), BF16)],
        compiler_params=_params(("arbitrary",)),
    )(zc, zc, o, conv_w, og, gmat_a, gmat_b, w_out, x, gate)


def _group_norm_bwd(dyn, y, og, gmat):
    rs = lax.rsqrt(_group_mean(y * y, gmat) + EPS)
    yhat = y * rs
    d_og = jnp.sum(dyn * yhat, axis=0, keepdims=True)
    dyh = dyn * og
    return rs * (dyh - yhat * _group_mean(dyh * yhat, gmat)), d_og


def mix_out_backward(dxo, y, gate, ya, o, og, gmat_a, gmat_b, w_out, rider=None):
    t, d = dxo.shape
    tm = _tile(t, ROW_TILE, 16)
    w = CONV_WIDTH

    def body(dxo_ref, y_ref, gate_ref, ya_ref, o_ref, og_ref, ga_ref, gb_ref, w_ref,
             dy_ref, dya_ref, do_ref, delta_ref, sd_ref, so_ref):
        @pl.when(pl.program_id(0) == 0)
        def _():
            sd_ref[...] = jnp.zeros_like(sd_ref)
            so_ref[...] = jnp.zeros_like(so_ref)

        dxo_v = dxo_ref[...]
        dy = (gate_ref[...] * dxo_v).astype(BF16)
        dy_ref[...] = dy
        sd_ref[0:1, :] += jnp.sum(dxo_v * y_ref[...].astype(F32), axis=0, keepdims=True)
        dyn = _dot(dy, w_ref[...], NT)
        ogv = og_ref[...]
        ov = o_ref[...].astype(F32)
        dya, d_og_a = _group_norm_bwd(dyn[:, :w], ya_ref[...].astype(F32), ogv[:, :w], ga_ref[...])
        dov, d_og_b = _group_norm_bwd(dyn[:, w:], ov, ogv[:, w:], gb_ref[...])
        dya_ref[...] = dya.astype(BF16)
        do_ref[...] = dov.astype(BF16)
        so_ref[0:1, :w] += d_og_a
        so_ref[0:1, w:] += d_og_b
        prod = dov * ov
        for h in range(MLA_HEADS):
            delta_ref[h] = jnp.sum(prod[:, h * V_HEAD:(h + 1) * V_HEAD], axis=-1, keepdims=True)

    def rows(n):
        return pl.BlockSpec((tm, n), lambda i: (i, 0))

    return _call_with_rider(
        body, rider, name="mix_out_bwd", grid=(t // tm,),
        in_specs=[rows(d), rows(d), _row(gate), rows(w), rows(MLA_WIDTH), _row(og), _row(gmat_a), _row(gmat_b),
                  _row(w_out)],
        out_specs=[rows(d), rows(w), rows(MLA_WIDTH), pl.BlockSpec((MLA_HEADS, tm, 1), lambda i: (0, i, 0)),
                   pl.BlockSpec((8, d), lambda i: (0, 0)), pl.BlockSpec((8, MIX_WIDTH), lambda i: (0, 0))],
        out_shape=[jax.ShapeDtypeStruct((t, d), BF16), jax.ShapeDtypeStruct((t, w), BF16),
                   jax.ShapeDtypeStruct((t, MLA_WIDTH), BF16), jax.ShapeDtypeStruct((MLA_HEADS, t, 1), F32),
                   jax.ShapeDtypeStruct((8, d), F32), jax.ShapeDtypeStruct((8, MIX_WIDTH), F32)],
        scratch_shapes=[], operands=(dxo, y, gate, ya, o, og, gmat_a, gmat_b, w_out))


def conv_backward(zc, dya, conv_w):
    t = zc.shape[0]
    tm = _tile(t, ROW_TILE, 16)
    nt = t // tm
    w = CONV_WIDTH

    def body(zc_ref, zp_ref, zn_ref, dya_ref, dn_ref, cw_ref, dzc_ref, sums_ref):
        i = pl.program_id(0)

        @pl.when(i == 0)
        def _():
            sums_ref[...] = jnp.zeros_like(sums_ref)

        zc_v = zc_ref[...].astype(F32)
        u, u1, u2 = _conv_taps(zc_v, zp_ref[...].astype(F32), i == 0)
        cw = cw_ref[...]
        dya_v = dya_ref[...].astype(F32)
        dyc = dya_v * zc_v[:, :w]
        dyc_next = jnp.where(i == nt - 1, 0.0, dn_ref[...].astype(F32) * zn_ref[:, :w].astype(F32))
        du = cw[2:3] * dyc + cw[1:2] * _advance_rows(dyc, dyc_next, 1) + cw[0:1] * _advance_rows(dyc, dyc_next, 2)
        dzc_ref[:, :w] = (dya_v * (cw[0:1] * u2 + cw[1:2] * u1 + cw[2:3] * u)).astype(BF16)
        dzc_ref[:, w:2 * w] = (du * zc_v[:, 2 * w:]).astype(BF16)
        dzc_ref[:, 2 * w:] = (du * zc_v[:, w:2 * w]).astype(BF16)
        _add_rows(sums_ref, [jnp.sum(dyc * tap, axis=0, keepdims=True) for tap in (u2, u1, u)])

    def rows(n):
        return pl.BlockSpec((tm, n), lambda i: (i, 0))

    def halo(n, step):
        return _halo_spec(tm, n, step, t // HALO - 1)

    return pl.pallas_call(
        body, name="conv_bwd", grid=(nt,),
        in_specs=[rows(ZC_COLS), halo(ZC_COLS, -1), halo(ZC_COLS, tm // HALO), rows(w), halo(w, tm // HALO),
                  _row(conv_w)],
        out_specs=[rows(ZC_COLS), pl.BlockSpec((8, w), lambda i: (0, 0))],
        out_shape=[jax.ShapeDtypeStruct((t, ZC_COLS), BF16), jax.ShapeDtypeStruct((8, w), F32)],
        compiler_params=_params(("arbitrary",)),
    )(zc, zc, zc, dya, dya, conv_w)


def _rms_bwd(dy, x, g):
    xhat, r = _rms(x)
    d_g = jnp.sum(dy * xhat, axis=0, keepdims=True)
    dxh = dy * g
    return r * (dxh - xhat * jnp.mean(dxh * xhat, axis=-1, keepdims=True)), d_g


def mla_project_backward(dq, dk, dv, zm, pos, inv_freq, qg, kvg, w_uq, w_ukv):
    t = zm.shape[0]
    tm = _tile(t, ROW_TILE, 16)

    def body(dq_ref, dk_ref, dv_ref, zm_ref, pos_ref, if_ref, qg_ref, kvg_ref, wq_ref, wkv_ref,
             dql_ref, dkvl_ref, dzm_ref, sums_ref):
        @pl.when(pl.program_id(0) == 0)
        def _():
            sums_ref[...] = jnp.zeros_like(sums_ref)

        tables = _rope_tables(pos_ref[...], if_ref[...])
        dkr = jnp.zeros((tm, LANES), F32)
        for h in range(MLA_HEADS):
            lo = h * HEAD_PAD
            dql_ref[:, lo:lo + QK_NOPE] = (dq_ref[:, lo:lo + QK_NOPE].astype(F32) * QK_FOLD).astype(BF16)
            dql_ref[:, lo + QK_NOPE:lo + HEAD_PAD] = _rope_transposed(
                dq_ref[:, lo + QK_NOPE:lo + HEAD_PAD].astype(F32) * QK_FOLD, tables).astype(BF16)
            dkvl_ref[:, h * QK_NOPE:(h + 1) * QK_NOPE] = dk_ref[:, lo:lo + QK_NOPE]
            dkr = dkr + dk_ref[:, lo + QK_NOPE:lo + HEAD_PAD].astype(F32)
        dkvl_ref[:, MLA_HEADS * QK_NOPE:] = dv_ref[...]
        zv = zm_ref[...].astype(F32)
        dqn = _dot(dql_ref[...], wq_ref[...])
        dkvn = _dot(dkvl_ref[...], wkv_ref[...])
        dcq, d_qg = _rms_bwd(dqn, zv[:, :Q_LORA], qg_ref[...])
        dckv, d_kvg = _rms_bwd(dkvn, zv[:, Q_LORA:Q_LORA + KV_LORA], kvg_ref[...])
        dzm_ref[:, :Q_LORA] = dcq.astype(BF16)
        dzm_ref[:, Q_LORA:Q_LORA + KV_LORA] = dckv.astype(BF16)
        dzm_ref[:, Q_LORA + KV_LORA:] = _rope_transposed(dkr, tables).astype(BF16)
        sums_ref[0:1, :Q_LORA] += d_qg
        sums_ref[0:1, Q_LORA:Q_LORA + KV_LORA] += d_kvg

    def rows(n):
        return pl.BlockSpec((tm, n), lambda i: (i, 0))

    return pl.pallas_call(
        body, name="mla_project_bwd", grid=(t // tm,),
        in_specs=[rows(QK_COLS), rows(QK_COLS), rows(MLA_WIDTH), rows(ZM_COLS), rows(1), _row(inv_freq),
                  _row(qg), _row(kvg), _row(w_uq), _row(w_ukv)],
        out_specs=[rows(QK_COLS), rows(QK_COLS), rows(ZM_COLS), pl.BlockSpec((8, ZM_COLS), lambda i: (0, 0))],
        out_shape=[jax.ShapeDtypeStruct((t, QK_COLS), BF16), jax.ShapeDtypeStruct((t, QK_COLS), BF16),
                   jax.ShapeDtypeStruct((t, ZM_COLS), BF16), jax.ShapeDtypeStruct((8, ZM_COLS), F32)],
        compiler_params=_params(("arbitrary",)),
    )(dq, dk, dv, zm, pos, inv_freq, qg, kvg, w_uq, w_ukv)


def mix_in_backward(dzc, dzm, w_in, x, dxo, gn, sc, gate, rider=None):
    t, d = x.shape
    tm = _tile(t, ROW_TILE, 16)

    def body(dzc_ref, dzm_ref, w_ref, x_ref, dxo_ref, gn_ref, sc_ref, gate_ref, dx_ref, dy_ref, sums_ref):
        @pl.when(pl.program_id(0) == 0)
        def _():
            sums_ref[...] = jnp.zeros_like(sums_ref)

        dh = _dot(dzc_ref[...], w_ref[:ZC_COLS, :]) + _dot(dzm_ref[...], w_ref[ZC_COLS:, :])
        dx, d_sh, d_sc, d_gn = _norm_mod_bwd(dh, x_ref[...], gn_ref[...], sc_ref[...])
        dx = dxo_ref[...] + dx
        dx_ref[...] = dx
        dy_ref[...] = (0.5 * gate_ref[...] * dx).astype(BF16)
        _add_rows(sums_ref, [d_sh, d_sc, d_gn])

    def rows(n):
        return pl.BlockSpec((tm, n), lambda i: (i, 0))

    return _call_with_rider(
        body, rider, name="mix_in_bwd", grid=(t // tm,),
        in_specs=[rows(ZC_COLS), rows(ZM_COLS), _row(w_in), rows(d), rows(d), _row(gn), _row(sc), _row(gate)],
        out_specs=[rows(d), rows(d), pl.BlockSpec((8, d), lambda i: (0, 0))],
        out_shape=[jax.ShapeDtypeStruct((t, d), F32), jax.ShapeDtypeStruct((t, d), BF16),
                   jax.ShapeDtypeStruct((8, d), F32)],
        scratch_shapes=[], operands=(dzc, dzm, w_in, x, dxo, gn, sc, gate))


def final_loss(x, target, g, gate):
    t, d = x.shape
    tm = _tile(t, ROW_TILE, 16)

    def body(x_ref, t_ref, g_ref, gate_ref, dx_ref, dy_ref, sums_ref):
        @pl.when(pl.program_id(0) == 0)
        def _():
            sums_ref[...] = jnp.zeros_like(sums_ref)

        gv = g_ref[...]
        xhat, r = _rms(x_ref[...])
        err = xhat * gv - t_ref[...]
        dyf = err * (1.0 / d)
        dxh = dyf * gv
        dx = r * (dxh - xhat * jnp.mean(dxh * xhat, axis=-1, keepdims=True))
        dx_ref[...] = dx
        dy_ref[...] = (0.5 * gate_ref[...] * dx).astype(BF16)
        _add_rows(sums_ref, [jnp.sum(dyf * xhat, axis=0, keepdims=True),
                             jnp.sum(err * err, axis=0, keepdims=True) * (0.5 / d)])

    row = pl.BlockSpec((tm, d), lambda i: (i, 0))
    return pl.pallas_call(
        body, name="final_loss", grid=(t // tm,),
        in_specs=[row, row, _row(g), _row(gate)],
        out_specs=[row, row, pl.BlockSpec((8, d), lambda i: (0, 0))],
        out_shape=[jax.ShapeDtypeStruct((t, d), F32), jax.ShapeDtypeStruct((t, d), BF16),
                   jax.ShapeDtypeStruct((8, d), F32)],
        compiler_params=_params(("arbitrary",)),
    )(x, target, g, gate)


def _adamw_step(w, g, m, v):
    m_new = ADAM_B1 * m + (1.0 - ADAM_B1) * g
    v_new = ADAM_B2 * v + (1.0 - ADAM_B2) * (g * g)
    m_hat = m_new / (1.0 - ADAM_B1 ** ADAM_STEP)
    v_hat = v_new / (1.0 - ADAM_B2 ** ADAM_STEP)
    return -ADAM_LR * (m_hat / (jnp.sqrt(v_hat) + ADAM_EPS) + ADAM_WD * w), m_new, v_new


def adamw(w, g, m, v, name):
    r, n = w.shape
    tr = _tile(r, max(8, ADAM_TILE_ELEMS // n), 8)

    def body(w_ref, g_ref, m_ref, v_ref, d_ref, mo_ref, vo_ref):
        d_ref[...], mo_ref[...], vo_ref[...] = _adamw_step(w_ref[...], g_ref[...], m_ref[...], v_ref[...])

    blk = pl.BlockSpec((tr, n), lambda i: (i, 0))
    shape = jax.ShapeDtypeStruct((r, n), F32)
    return pl.pallas_call(
        body, name=name, grid=(r // tr,), in_specs=[blk] * 4, out_specs=[blk] * 3, out_shape=[shape] * 3,
        compiler_params=_params(("arbitrary",)),
    )(w, g, m, v)


def adamw_received(w, own, got, m, v, name):
    r, n = w.shape
    tr = _tile(r, SUM_ROWS, 16)

    def body(w_ref, own_ref, got_ref, m_ref, v_ref, g_ref, d_ref, mo_ref, vo_ref):
        g = own_ref[...]
        for j in range(3):
            g = g + got_ref[j].astype(F32)
        g_ref[...] = g
        d_ref[...], mo_ref[...], vo_ref[...] = _adamw_step(w_ref[...], g, m_ref[...], v_ref[...])

    blk = pl.BlockSpec((tr, n), lambda i: (i, 0))
    shape = jax.ShapeDtypeStruct((r, n), F32)
    return pl.pallas_call(
        body, name=name, grid=(r // tr,),
        in_specs=[blk, blk, pl.BlockSpec((3, tr, n), lambda i: (0, i, 0)), blk, blk],
        out_specs=[blk] * 4, out_shape=[shape] * 4, compiler_params=_params(("arbitrary",)),
    )(w, own, got, m, v)


def _pad_to(v, n):
    return jnp.pad(v, (0, n - v.shape[0]))


def _pad_heads(w, axis_len):
    n = w.shape[1]
    return jnp.pad(w.reshape(MLA_HEADS, axis_len, n), ((0, 0), (0, HEAD_PAD - axis_len), (0, 0))).reshape(-1, n)


def _swap_head_parts(w, inner, outer):
    n = w.shape[1]
    return w.reshape(outer, inner, QK_NOPE, n).transpose(1, 0, 2, 3).reshape(-1, n)


def kernel(x, c, positions, ada_w, ada_b, norm_ffn1_g, ffn1_w1, ffn1_w3, ffn1_w2, norm_mix_g, w_in, conv_w, q_norm_g, w_uq, kv_norm_g, w_ukv, out_norm_g, w_out, norm_ffn2_g, ffn2_w1, ffn2_w3, ffn2_w2, final_norm_g, loss_target, m_ada_w, m_ada_b, m_norm_ffn1_g, m_ffn1_w1, m_ffn1_w3, m_ffn1_w2, m_norm_mix_g, m_w_in, m_conv_w, m_q_norm_g, m_w_uq, m_kv_norm_g, m_w_ukv, m_out_norm_g, m_w_out, m_norm_ffn2_g, m_ffn2_w1, m_ffn2_w3, m_ffn2_w2, m_final_norm_g, v_ada_w, v_ada_b, v_norm_ffn1_g, v_ffn1_w1, v_ffn1_w3, v_ffn1_w2, v_norm_mix_g, v_w_in, v_conv_w, v_q_norm_g, v_w_uq, v_kv_norm_g, v_w_ukv, v_out_norm_g, v_w_out, v_norm_ffn2_g, v_ffn2_w1, v_ffn2_w3, v_ffn2_w2, v_final_norm_g):
    t, d = x.shape[1], x.shape[2]
    f = ffn1_w2.shape[1] * N_DEV
    me = 4 * lax.axis_index("x") + 2 * lax.axis_index("y") + lax.axis_index("c")
    my_c = lax.axis_index("c")
    my_chip = 2 * lax.axis_index("x") + lax.axis_index("y")
    xs = x[0]
    n_ada = ada_w.shape[2]
    cw_n = conv_w.shape[2]

    c_rows = jnp.broadcast_to(c, (8, d))
    conv_rows = jnp.pad(conv_w[0], ((0, 8 - CONV_K), (0, LANES - cw_n)))
    ffn1_blocks = jnp.stack([ffn1_w1[0].T, ffn1_w3[0].T, ffn1_w2[0]]).astype(BF16)
    ffn2_blocks = jnp.stack([ffn2_w1[0].T, ffn2_w3[0].T, ffn2_w2[0]]).astype(BF16)
    c_all, conv_all, ffn1_all = all_gather_relayed([c_rows, conv_rows, ffn1_blocks], [0, 0, 1], "gather_first")
    c_all = c_all[:, 0, :]
    conv_full8 = conv_all[:, :, :cw_n].transpose(1, 0, 2).reshape(8, CONV_WIDTH)
    ffn1_ws = ffn1_all.reshape(3, f, d)
    gather_mix = riding_gather(
        [w_in[0].T.astype(BF16), w_uq[0].T.astype(BF16), w_ukv[0].T.astype(BF16), w_out[0].astype(BF16)], [0, 0, 0, 0])

    ada_b_cols = lax.dynamic_slice_in_dim(ada_b, me * n_ada, n_ada, axis=1)
    mod_cols = ada_forward(c_all, ada_w[0], ada_b_cols)
    mod_all, = all_gather([mod_cols], [0], "gather_mod")
    mod = lax.dynamic_index_in_dim(mod_all, me, axis=1, keepdims=False).reshape(N_MOD, 1, d)
    sh1, sc1, g1, sh2, sc2, g2, sh3, sc3, g3 = [mod[i] for i in range(N_MOD)]

    gf = final_norm_g.reshape(1, d)
    x1, h1, a1, b1, y1, *gathered = ffn_forward(xs, norm_ffn1_g, sc1, sh1, g1, ffn1_ws, 0, "ffn1_fwd", gather_mix)
    w_in_p = jnp.pad(gathered[0].reshape(IN_COLS, d), ((0, ZC_COLS + ZM_COLS - IN_COLS), (0, 0)))
    w_uq_p = _pad_heads(gathered[1].reshape(-1, Q_LORA), QK_NOPE + QK_ROPE)
    w_ukv_p = _swap_head_parts(gathered[2].reshape(-1, KV_LORA), 2, MLA_HEADS)
    w_out_f = gathered[3].reshape(MIX_WIDTH, d)
    h2, zc, zm = mix_in_forward(x1, norm_mix_g, sc2, sh2, w_in_p)
    pos = positions[0].astype(F32).reshape(t, 1)
    inv_freq = ROPE_THETA ** (-jnp.arange(0, QK_ROPE, 2, dtype=F32) / QK_ROPE)
    inv_freq = jnp.concatenate([inv_freq, inv_freq, jnp.zeros((LANES - QK_ROPE,), F32)]).reshape(1, LANES)
    qn, kvn, q, k, v = mla_project(zm, pos, inv_freq, q_norm_g, kv_norm_g, w_uq_p, w_ukv_p)
    o, lse, ffn2_all = attention_forward(q, k, v, riding_gather([ffn2_blocks], [1]))
    ffn2_ws = ffn2_all.reshape(3, f, d)
    lane = jnp.arange(CONV_WIDTH)
    gmat_a = (lane[:, None] // (CONV_WIDTH // CONV_GROUPS) == lane[None, :] // (CONV_WIDTH // CONV_GROUPS))
    gmat_a = (gmat_a / (CONV_WIDTH // CONV_GROUPS)).astype(BF16)
    gmat_b = ((lane[:, None] // V_HEAD == lane[None, :] // V_HEAD) / V_HEAD).astype(BF16)
    x2, yn, y2, ya = mix_out_forward(zc, o, conv_full8, out_norm_g, gmat_a, gmat_b, w_out_f, x1, g2)
    x3, h3, a3, b3, y3 = ffn_forward(x2, norm_ffn2_g, sc3, sh3, g3, ffn2_ws, 0, "ffn2_fwd")
    dx3, dy3, sums_f = final_loss(x3, loss_target[0], gf, g3)

    chip_idx = jnp.bitwise_xor(my_chip, jnp.array([0, 2, 1, 3], jnp.int32)).astype(jnp.int32)
    src_idx = (2 * chip_idx + my_c).astype(jnp.int32)

    def row_blocks(named):
        return [g.reshape(N_DEV, g.shape[0] // N_DEV, g.shape[1]) for _, g in named]

    def chip_sums(named, g8, got):
        return [add_sibling(g, r, src_idx, chip_idx, "rs_add_" + n) for g, r, (n, _) in zip(g8, got, named)]

    da3, db3, g_w2b = ffn_backward_gate(dy3, a3, b3, ffn2_ws, 0, "ffn2_bwd_gate")
    dx2, sums_3 = ffn_backward_norm(da3, db3, dx3, x2, y3, norm_ffn2_g, sc3, ffn2_ws, 0, "ffn2_bwd_norm")
    ffn2_named = [("ffn2_w1", matmul_tn(da3, h3, "ffn2_gw1")), ("ffn2_w3", matmul_tn(db3, h3, "ffn2_gw3")),
                  ("ffn2_w2", g_w2b)]
    ffn2_g8 = row_blocks(ffn2_named)
    dy2, dya, do, delta, sums_2d, sums_2o, *ffn2_sib = mix_out_backward(
        dx2, y2, g2, ya, o, out_norm_g, gmat_a, gmat_b, w_out_f, riding_sibling(ffn2_g8))
    ffn2_sums = chip_sums(ffn2_named, ffn2_g8, ffn2_sib)
    g_w_out = matmul_tn(yn, dy2, "gw_out")
    nq = t // _tile(t, ATTN_TILE, CHUNK)
    stat_shape = (MLA_HEADS, nq, 1, t // nq)
    dq, dk, dv, *ffn2_got = attention_backward(q, k, v, do, lse.reshape(stat_shape), delta.reshape(stat_shape),
                                               riding_exchange([s[1] for s in ffn2_sums]))
    dzc, sums_c = conv_backward(zc, dya, conv_full8)
    dql, dkvl, dzm, sums_m = mla_project_backward(dq, dk, dv, zm, pos, inv_freq, q_norm_g, kv_norm_g, w_uq_p, w_ukv_p)
    g_w_uq_p = matmul_tn(dql, qn, "gw_uq")
    g_w_ukv_p = matmul_tn(dkvl, kvn, "gw_ukv")
    g_w_in = matmul_tn([dzc, dzm], h2, "gw_in")[:IN_COLS]
    g_w_uq = g_w_uq_p.reshape(MLA_HEADS, HEAD_PAD, Q_LORA)[:, :QK_NOPE + QK_ROPE].reshape(-1, Q_LORA)
    g_w_ukv = _swap_head_parts(g_w_ukv_p, MLA_HEADS, 2)
    mix_named = [("w_in", g_w_in), ("w_uq", g_w_uq), ("w_ukv", g_w_ukv), ("w_out", g_w_out)]
    mix_g8 = row_blocks(mix_named)
    dx1, dy1, sums_1m, *mix_sib = mix_in_backward(dzc, dzm, w_in_p, x1, dx2, norm_mix_g, sc2, g1, riding_sibling(mix_g8))
    mix_sums = chip_sums(mix_named, mix_g8, mix_sib)
    da1, db1, g_w2a, *mix_got = ffn_backward_gate(dy1, a1, b1, ffn1_ws, 0, "ffn1_bwd_gate",
                                                  riding_exchange([s[1] for s in mix_sums]))
    ffn1_pair = [("ffn1_w2", g_w2a), ("ffn1_w1", matmul_tn(da1, h1, "ffn1_gw1"))]
    pair_g8 = row_blocks(ffn1_pair)
    g_w3a, *pair_sib = matmul_tn(db1, h1, "ffn1_gw3", riding_sibling(pair_g8))
    ffn1_last = [("ffn1_w3", g_w3a)]
    last_g8 = row_blocks(ffn1_last)
    ffn1_named = ffn1_pair + ffn1_last
    ffn1_sums = chip_sums(ffn1_pair, pair_g8, pair_sib) + chip_sums(
        ffn1_last, last_g8, exchange_sibling(last_g8, "rs_sibling_ffn1_w3"))
    dx0, sums_1, *ffn1_got = ffn_backward_norm(da1, db1, dx1, xs, y1, norm_ffn1_g, sc1, ffn1_ws, 0, "ffn1_bwd_norm",
                                               riding_exchange([s[1] for s in ffn1_sums]))
    reduced = {}
    for named, group_sums, group_got in ((ffn2_named, ffn2_sums, ffn2_got), (mix_named, mix_sums, mix_got),
                                         (ffn1_named, ffn1_sums, ffn1_got)):
        for (n, _), (own, _), got in zip(named, group_sums, group_got):
            reduced[n] = (own, got)

    dmod = jnp.concatenate([sums_1[0], sums_1[1], sums_1[2], sums_1m[0], sums_1m[1], sums_2d[0],
                            sums_3[0], sums_3[1], sums_3[2]])
    pieces = [dmod, sums_1[3], sums_1m[2], sums_m[0, :Q_LORA], sums_m[0, Q_LORA:Q_LORA + KV_LORA], sums_2o[0],
              sums_3[3], sums_f[0], sums_f[1], sums_c[:CONV_K].reshape(-1)]
    plens = [p.shape[0] for p in pieces]
    poffs = [sum(plens[:i]) for i in range(len(plens))]
    vec_len = -(-sum(plens) // 1024) * 1024
    vec = _pad_to(jnp.concatenate(pieces), vec_len).reshape(-1, LANES)
    vec_all, = all_gather([vec], [0], "gather_sums")
    tot = sum_devices(vec_all).reshape(-1)
    g_ada_b, g_n1, g_nmix, g_qg, g_kvg, g_og, g_n3, g_gf, loss_lanes, g_conv_full = [
        tot[o:o + n] for o, n in zip(poffs, plens)]
    loss = sum_lanes(loss_lanes.reshape(1, d))[0, 0]
    g_conv = lax.dynamic_slice_in_dim(g_conv_full.reshape(CONV_K, CONV_WIDTH), me * cw_n, cw_n, axis=1)
    dmod_all = vec_all.reshape(N_DEV, vec_len)[:, :N_MOD * d]
    dmod_cols = lax.dynamic_slice_in_dim(dmod_all, me * n_ada, n_ada, axis=1)
    g_ada_w = ada_backward(jnp.pad(c_all, ((0, 8), (0, 0))), jnp.pad(dmod_cols, ((0, 8), (0, 0))))

    def update(name, w, g, m, v, received=None):
        k, n = w.shape[-2:]
        if g.shape == (k, n):
            flat, back = (lambda a: a.reshape(k, n)), (lambda a: a.reshape(w.shape))
        else:
            flat, back = (lambda a: a.reshape(k, n).T), (lambda a: a.T.reshape(w.shape))
        if received is None:
            out = (g,) + tuple(adamw(flat(w), g, flat(m), flat(v), "adamw_" + name))
        else:
            out = adamw_received(flat(w), g, received, flat(m), flat(v), "adamw_" + name)
        return tuple(back(a) for a in out)

    res = {}
    res["ada_w"] = update("ada_w", ada_w, g_ada_w, m_ada_w, v_ada_w)
    big = [("ffn1_w1", ffn1_w1, m_ffn1_w1, v_ffn1_w1), ("ffn1_w3", ffn1_w3, m_ffn1_w3, v_ffn1_w3),
           ("ffn2_w1", ffn2_w1, m_ffn2_w1, v_ffn2_w1), ("ffn2_w3", ffn2_w3, m_ffn2_w3, v_ffn2_w3),
           ("w_in", w_in, m_w_in, v_w_in), ("w_uq", w_uq, m_w_uq, v_w_uq), ("w_ukv", w_ukv, m_w_ukv, v_w_ukv),
           ("ffn1_w2", ffn1_w2, m_ffn1_w2, v_ffn1_w2), ("ffn2_w2", ffn2_w2, m_ffn2_w2, v_ffn2_w2),
           ("w_out", w_out, m_w_out, v_w_out)]
    for name, w, m, v in big:
        res[name] = update(name, w, reduced[name][0], m, v, reduced[name][1])
    smalls = [("ada_b", ada_b, g_ada_b, m_ada_b, v_ada_b),
              ("norm_ffn1_g", norm_ffn1_g, g_n1, m_norm_ffn1_g, v_norm_ffn1_g),
              ("norm_mix_g", norm_mix_g, g_nmix, m_norm_mix_g, v_norm_mix_g),
              ("conv_w", conv_w, g_conv, m_conv_w, v_conv_w),
              ("q_norm_g", q_norm_g, g_qg, m_q_norm_g, v_q_norm_g),
              ("kv_norm_g", kv_norm_g, g_kvg, m_kv_norm_g, v_kv_norm_g),
              ("out_norm_g", out_norm_g, g_og, m_out_norm_g, v_out_norm_g),
              ("norm_ffn2_g", norm_ffn2_g, g_n3, m_norm_ffn2_g, v_norm_ffn2_g),
              ("final_norm_g", final_norm_g, g_gf, m_final_norm_g, v_final_norm_g)]
    slens = [w.size for _, w, _, _, _ in smalls]
    soffs = [sum(slens[:i]) for i in range(len(slens))]
    s_len = -(-sum(slens) // 1024) * 1024

    def pack_small(i):
        return _pad_to(jnp.concatenate([s[i].reshape(-1) for s in smalls]), s_len).reshape(8, -1)

    s_out = adamw(pack_small(1), pack_small(2), pack_small(3), pack_small(4), "adamw_small")
    for (name, w, g, _, _), o, n in zip(smalls, soffs, slens):
        res[name] = (g.reshape(w.shape),) + tuple(a.reshape(-1)[o:o + n].reshape(w.shape) for a in s_out)

    order = ["ada_w", "ada_b", "norm_ffn1_g", "ffn1_w1", "ffn1_w3", "ffn1_w2", "norm_mix_g", "w_in", "conv_w",
             "q_norm_g", "w_uq", "kv_norm_g", "w_ukv", "out_norm_g", "w_out", "norm_ffn2_g", "ffn2_w1", "ffn2_w3",
             "ffn2_w2", "final_norm_g"]
    return (loss, dx0.reshape(x.shape), *[res[n][0] for n in order], *[res[n][1] for n in order],
            *[res[n][2] for n in order], *[res[n][3] for n in order])
```

```python
import functools
import math

import jax
import jax.numpy as jnp
from jax import lax
from jax.experimental import pallas as pl
from jax.experimental.pallas import tpu as pltpu

F32 = jnp.float32
BF16 = jnp.bfloat16
MESH_ID = pl.DeviceIdType.MESH
N_DEV = 8

EPS = 1e-6
CHUNK = 64
N_MOD = 9
CONV_WIDTH = 512
CONV_GROUPS = 8
CONV_K = 3
MLA_HEADS = 4
QK_NOPE = 128
QK_ROPE = 64
V_HEAD = 128
Q_LORA = 384
KV_LORA = 256
ROPE_THETA = 10000.0
MLA_WIDTH = MLA_HEADS * V_HEAD
MIX_WIDTH = CONV_WIDTH + MLA_WIDTH
IN_COLS = 3 * CONV_WIDTH + Q_LORA + KV_LORA + QK_ROPE
ZC_COLS = 3 * CONV_WIDTH
ZM_COLS = Q_LORA + KV_LORA + 128
HEAD_PAD = 256
QK_COLS = MLA_HEADS * HEAD_PAD
ATTN_SCALE = (QK_NOPE + QK_ROPE) ** -0.5
LOG2_E = 1.4426950408889634
LN_2 = 0.6931471805599453
QK_FOLD = ATTN_SCALE * LOG2_E
NEG_INF = -1e30

ADAM_LR = 0.001
ADAM_B1 = 0.9
ADAM_B2 = 0.999
ADAM_EPS = 1e-08
ADAM_WD = 0.01
ADAM_STEP = 10

LANES = 128
MXU_COLS = 256
VMEM_LIMIT = 56 * 1024 * 1024
ROW_TILE = 1024
FFN_FWD_TILE = (1024, 256)
FFN_BWD_TILE = (512, 1408)
GRAD_TILE = 1408
GRAD_DEPTH = 2048
SUM_ROWS = 256
ADAM_TILE_ELEMS = 1 << 19
ATTN_TILE = 1024

NN = (((1,), (0,)), ((), ()))
NT = (((1,), (1,)), ((), ()))
TN = (((0,), (0,)), ((), ()))


def _dot(a, b, dims=NN):
    return lax.dot_general(a, b, dims, preferred_element_type=F32)


def _tile(n, cap, mult=LANES):
    best = None
    for t in range(mult, min(n, cap) + 1, mult):
        if n % t == 0:
            best = t
    return n if best is None else best


def _params(sem=None):
    return pltpu.CompilerParams(dimension_semantics=sem, vmem_limit_bytes=VMEM_LIMIT)


def _row(v):
    return pl.BlockSpec(v.shape, lambda *_: (0,) * v.ndim)


def _sigmoid(x):
    return 0.5 * jnp.tanh(0.5 * x) + 0.5


def _rms(x):
    r = lax.rsqrt(jnp.mean(x * x, axis=-1, keepdims=True) + EPS)
    return x * r, r


def _norm_mod_bwd(dh, x, gn, sc):
    xhat, r = _rms(x)
    d_sh = jnp.sum(dh, axis=0, keepdims=True)
    d_sc = jnp.sum(dh * (xhat * gn), axis=0, keepdims=True)
    dxn = dh * (1.0 + sc)
    d_gn = jnp.sum(dxn * xhat, axis=0, keepdims=True)
    dxh = dxn * gn
    dx = r * (dxh - xhat * jnp.mean(dxh * xhat, axis=-1, keepdims=True))
    return dx, d_sh, d_sc, d_gn


def _group_mean(v, gmat):
    return _dot(v.astype(BF16), gmat)


def _add_rows(ref, rows):
    for r, v in enumerate(rows):
        ref[r:r + 1, :] += v


def _window(ref, axis, j):
    return ref.at[(slice(None),) * axis + (j,)]


def _any_specs(n):
    return [pl.BlockSpec(memory_space=pl.ANY)] * n


def all_gather(blocks, axes, name):
    n_arr = len(blocks)

    def body(*refs):
        start, forward, finish = _gather_steps(refs[:n_arr], refs[n_arr:2 * n_arr], axes, *refs[2 * n_arr:])
        start()
        for j in range(3):
            forward(j)
        finish()

    return pl.pallas_call(
        body, name=name, out_shape=_gathered_shapes(blocks, axes),
        in_specs=_any_specs(n_arr), out_specs=_any_specs(n_arr), scratch_shapes=_gather_sems(n_arr),
    )(*blocks)


def all_gather_relayed(blocks, axes, name):
    n_arr = len(blocks)
    arrays = range(n_arr)

    def body(*refs):
        ins, outs = refs[:n_arr], refs[n_arr:2 * n_arr]
        send_sems, recv_sems, local_sems = refs[2 * n_arr:]
        x, y, c = lax.axis_index("x"), lax.axis_index("y"), lax.axis_index("c")
        sibling, x_nbr, y_nbr, diagonal = (x, y, 1 - c), (1 - x, y, c), (x, 1 - y, c), (1 - x, 1 - y, c)
        north = c == 1
        relay_slot = jnp.where(north, 1, 2)
        relay_from = tuple(jnp.where(north, a, b) for a, b in zip(x_nbr, y_nbr))
        relay_to = tuple(jnp.where(north, a, b) for a, b in zip(y_nbr, x_nbr))
        other_from = relay_to

        def slot(a, px, py, pc):
            return _window(outs[a], axes[a], 4 * px + 2 * py + pc)

        def copy(a, k, block, to, src=None):
            return pltpu.make_async_remote_copy(
                src_ref=slot(a, *block) if src is None else src, dst_ref=slot(a, *block),
                send_sem=send_sems.at[k, a], recv_sem=recv_sems.at[k, a], device_id=to, device_id_type=MESH_ID)

        mine = [pltpu.make_async_copy(ins[a], slot(a, x, y, c), local_sems.at[a]) for a in arrays]
        for cp in mine:
            cp.start()
        first = [copy(a, k, (x, y, c), to, src=ins[a])
                 for k, to in enumerate((sibling, x_nbr, y_nbr)) for a in arrays]
        for cp in first:
            cp.start()
        later = []
        for a in arrays:
            copy(a, relay_slot, relay_from, (x, y, c)).wait_recv()
            later += [copy(a, 3, relay_from, relay_to), copy(a, 3 + relay_slot, relay_from, sibling)]
            later[-2].start()
            later[-1].start()
        for a in arrays:
            copy(a, 3 - relay_slot, other_from, (x, y, c)).wait_recv()
            later.append(copy(a, 6 - relay_slot, other_from, sibling))
            later[-1].start()
        for a in arrays:
            copy(a, 3, diagonal, (x, y, c)).wait_recv()
            later.append(copy(a, 6, diagonal, sibling))
            later[-1].start()
        for a in arrays:
            for k, block in ((0, sibling), (4, (1 - x, y, 1 - c)), (5, (x, 1 - y, 1 - c)), (6, (1 - x, 1 - y, 1 - c))):
                copy(a, k, block, (x, y, c)).wait_recv()
        for cp in first + later:
            cp.wait_send()
        for cp in mine:
            cp.wait()

    return pl.pallas_call(
        body, name=name, out_shape=_gathered_shapes(blocks, axes),
        in_specs=_any_specs(n_arr), out_specs=_any_specs(n_arr), scratch_shapes=_gather_sems(n_arr),
    )(*blocks)


def _gathered_shapes(blocks, axes):
    return [jax.ShapeDtypeStruct(b.shape[:ax] + (N_DEV,) + b.shape[ax:], b.dtype) for b, ax in zip(blocks, axes)]


def _gather_sems(n_arr):
    return [pltpu.SemaphoreType.DMA((7, n_arr)), pltpu.SemaphoreType.DMA((7, n_arr)), pltpu.SemaphoreType.DMA((n_arr,))]


def _gather_steps(ins, outs, axes, send_sems, recv_sems, local_sems):
    arrays = range(len(ins))
    x, y, c = lax.axis_index("x"), lax.axis_index("y"), lax.axis_index("c")
    me, sibling = (x, y, c), (x, y, 1 - c)
    chips = [(1 - x, y), (x, 1 - y), (1 - x, 1 - y)]

    def slot(a, px, py, pc):
        return _window(outs[a], axes[a], 4 * px + 2 * py + pc)

    def copy(a, k, block, to, src=None):
        return pltpu.make_async_remote_copy(
            src_ref=slot(a, *block) if src is None else src, dst_ref=slot(a, *block),
            send_sem=send_sems.at[k, a], recv_sem=recv_sems.at[k, a], device_id=to, device_id_type=MESH_ID)

    def mine(a):
        return pltpu.make_async_copy(ins[a], slot(a, *me), local_sems.at[a])

    def first():
        return ([copy(a, 0, me, sibling, src=ins[a]) for a in arrays]
                + [copy(a, 1 + j, me, (*chip, c), src=ins[a]) for j, chip in enumerate(chips) for a in arrays])

    def passed(j):
        return [copy(a, 4 + j, (*chips[j], c), sibling) for a in arrays]

    def start():
        for a in arrays:
            mine(a).start()
        for cp in first():
            cp.start()

    def forward(j):
        for a, cp in zip(arrays, passed(j)):
            copy(a, 1 + j, (*chips[j], c), me).wait_recv()
            cp.start()

    def finish():
        for a in arrays:
            copy(a, 0, sibling, me).wait_recv()
        for j, chip in enumerate(chips):
            for a in arrays:
                copy(a, 4 + j, (*chip, 1 - c), me).wait_recv()
        for cp in first() + passed(0) + passed(1) + passed(2):
            cp.wait_send()
        for a in arrays:
            mine(a).wait()

    return start, forward, finish


def exchange_sibling(grads, name):
    n_arr = len(grads)

    def body(*refs):
        start, finish = _sibling_exchange_steps(refs[:n_arr], refs[n_arr:2 * n_arr], *refs[2 * n_arr:])
        start()
        finish()

    return pl.pallas_call(
        body, name=name, out_shape=_sibling_shapes(grads),
        in_specs=_any_specs(n_arr), out_specs=_any_specs(n_arr), scratch_shapes=_exchange_sems(n_arr),
    )(*grads)


def _sibling_shapes(grads):
    return [jax.ShapeDtypeStruct((4,) + g.shape[1:], g.dtype) for g in grads]


def _exchange_sems(n_arr):
    return [pltpu.SemaphoreType.DMA((n_arr,)), pltpu.SemaphoreType.DMA((n_arr,))]


def _sibling_exchange_steps(ins, outs, send_sems, recv_sems):
    x, y, c = lax.axis_index("x"), lax.axis_index("y"), lax.axis_index("c")

    def copy(a, src, dst):
        return pltpu.make_async_remote_copy(
            src_ref=src, dst_ref=dst, send_sem=send_sems.at[a], recv_sem=recv_sems.at[a],
            device_id=(x, y, 1 - c), device_id_type=MESH_ID)

    def start():
        for a in range(len(ins)):
            for k in range(4):
                copy(a, ins[a].at[2 * k + (1 - c)], outs[a].at[k]).start()

    def finish():
        whole = [copy(a, ins[a].at[pl.ds(0, 4)], outs[a]) for a in range(len(ins))]
        for cp in whole:
            cp.wait_recv()
        for cp in whole:
            cp.wait_send()

    return start, finish


def _chip_exchange_steps(ins, outs, send_sems, recv_sems):
    x, y, c = lax.axis_index("x"), lax.axis_index("y"), lax.axis_index("c")
    chips = [(1 - x, y), (x, 1 - y), (1 - x, 1 - y)]

    def copy(a, src, dst, chip):
        return pltpu.make_async_remote_copy(
            src_ref=src, dst_ref=dst, send_sem=send_sems.at[a], recv_sem=recv_sems.at[a],
            device_id=(*chip, c), device_id_type=MESH_ID)

    def start():
        for a in range(len(ins)):
            for j, chip in enumerate(chips):
                copy(a, ins[a].at[j], outs[a].at[j], chip).start()

    def finish():
        whole = [copy(a, ins[a], outs[a], chips[0]) for a in range(len(ins))]
        for cp in whole:
            cp.wait_recv()
        for cp in whole:
            cp.wait_send()

    return start, finish


def riding_gather(blocks, axes):
    def phases(ins, outs, *sems):
        start, forward, finish = _gather_steps(ins, outs, axes, *sems)
        return [start] + [functools.partial(forward, j) for j in range(3)] + [finish]

    return dict(operands=blocks, out_shape=_gathered_shapes(blocks, axes), sems=_gather_sems(len(blocks)),
                phases=phases, when=("first", "late0", "late1", "late2", "last"))


def riding_exchange(parts):
    def phases(ins, outs, *sems):
        return list(_chip_exchange_steps(ins, outs, *sems))

    return dict(operands=parts, out_shape=[jax.ShapeDtypeStruct(p.shape, p.dtype) for p in parts],
                sems=_exchange_sems(len(parts)), phases=phases, when=("first", "last"))


def riding_sibling(grads):
    def phases(ins, outs, *sems):
        return list(_sibling_exchange_steps(ins, outs, *sems))

    return dict(operands=grads, out_shape=_sibling_shapes(grads), sems=_exchange_sems(len(grads)),
                phases=phases, when=("first", "last"))


def _call_with_rider(body, rider, *, name, grid, in_specs, out_specs, out_shape, scratch_shapes, operands):
    params = _params(("arbitrary",) * len(grid))
    if rider is None:
        return pl.pallas_call(body, name=name, grid=grid, in_specs=in_specs, out_specs=out_specs,
                              out_shape=out_shape, scratch_shapes=scratch_shapes, compiler_params=params)(*operands)
    n_in, n_out, n_scr, k = len(in_specs), len(out_specs), len(scratch_shapes), len(rider["operands"])
    at = {"first": (0,) * len(grid), "last": tuple(g - 1 for g in grid)}
    if "late0" in rider["when"]:
        rows, cols = grid
        assert cols >= 3
        at.update({"late%d" % j: (max(rows - 2, 0), j) for j in range(3)})

    def wrapped(*refs):
        ins, c_in = refs[:n_in], refs[n_in:n_in + k]
        outs, c_out = refs[n_in + k:n_in + k + n_out], refs[n_in + k + n_out:n_in + 2 * k + n_out]
        scratch, sems = refs[n_in + 2 * k + n_out:n_in + 2 * k + n_out + n_scr], refs[n_in + 2 * k + n_out + n_scr:]
        pos = [pl.program_id(axis) for axis in range(len(grid))]

        def here(key):
            return functools.reduce(jnp.logical_and, [p == v for p, v in zip(pos, at[key])])

        phases = rider["phases"](c_in, c_out, *sems)
        for fn, key in zip(phases, rider["when"]):
            if key != "last":
                pl.when(here(key))(fn)
        body(*ins, *outs, *scratch)
        pl.when(here("last"))(phases[-1])

    return pl.pallas_call(
        wrapped, name=name, grid=grid,
        in_specs=list(in_specs) + _any_specs(k), out_specs=list(out_specs) + _any_specs(k),
        out_shape=list(out_shape) + rider["out_shape"], scratch_shapes=list(scratch_shapes) + rider["sems"],
        compiler_params=params)(*operands, *rider["operands"])


def add_sibling(g8, got, src_idx, chip_idx, name):
    _, r, n = g8.shape
    tr = _tile(r, SUM_ROWS, 16)

    def body(si_ref, ci_ref, g0_ref, g1_ref, g2_ref, g3_ref, got_ref, own_ref, send_ref):
        own_ref[...] = g0_ref[0] + got_ref[ci_ref[0]]
        for j, g_ref in enumerate((g1_ref, g2_ref, g3_ref)):
            send_ref[j] = (g_ref[0] + got_ref[ci_ref[j + 1]]).astype(BF16)

    def mine(j):
        return pl.BlockSpec((1, tr, n), lambda i, si, ci: (si[j], i, 0))

    return pl.pallas_call(
        body, name=name,
        out_shape=[jax.ShapeDtypeStruct((r, n), F32), jax.ShapeDtypeStruct((3, r, n), BF16)],
        grid_spec=pltpu.PrefetchScalarGridSpec(
            num_scalar_prefetch=2, grid=(r // tr,),
            in_specs=[mine(0), mine(1), mine(2), mine(3), pl.BlockSpec((4, tr, n), lambda i, si, ci: (0, i, 0))],
            out_specs=[pl.BlockSpec((tr, n), lambda i, si, ci: (i, 0)),
                       pl.BlockSpec((3, tr, n), lambda i, si, ci: (0, i, 0))]),
        compiler_params=_params(("arbitrary",)),
    )(src_idx, chip_idx, g8, g8, g8, g8, got)


def sum_devices(g):
    def body(g_ref, o_ref):
        acc = g_ref[0]
        for j in range(1, N_DEV):
            acc = acc + g_ref[j]
        o_ref[...] = acc

    return pl.pallas_call(body, name="sum_devices", out_shape=jax.ShapeDtypeStruct(g.shape[1:], F32))(g)


def sum_lanes(v):
    def body(v_ref, o_ref):
        o_ref[...] = jnp.broadcast_to(jnp.sum(v_ref[...], axis=-1, keepdims=True), (1, LANES))

    return pl.pallas_call(body, name="sum_lanes", out_shape=jax.ShapeDtypeStruct((1, LANES), F32))(v)


def ada_forward(c_all, ada_w, ada_b_cols):
    nb, n = c_all.shape[0], ada_w.shape[1]

    def body(c_ref, w_ref, b_ref, o_ref):
        cv = c_ref[...]
        s = (cv * jax.nn.sigmoid(cv)).astype(BF16)
        o_ref[...] = _dot(s, w_ref[...].astype(BF16)) + b_ref[...]

    return pl.pallas_call(body, name="ada_fwd", out_shape=jax.ShapeDtypeStruct((nb, n), F32),
                          compiler_params=_params())(c_all, ada_w, ada_b_cols)


def ada_backward(c_all16, dmod16):
    d, n = c_all16.shape[1], dmod16.shape[1]

    def body(c_ref, g_ref, o_ref):
        cv = c_ref[...]
        s = (cv * jax.nn.sigmoid(cv)).astype(BF16)
        o_ref[...] = _dot(s, g_ref[...].astype(BF16), TN)

    return pl.pallas_call(body, name="ada_bwd", out_shape=jax.ShapeDtypeStruct((d, n), F32),
                          compiler_params=_params())(c_all16, dmod16)


def ffn_forward(x, gn, sc, sh, gate, ws, first, name, rider=None, loss_head=None):
    t, d = x.shape
    f = ws.shape[1]
    tm, tf = _tile(t, FFN_FWD_TILE[0], 16), _tile(f, FFN_FWD_TILE[1])
    nf = f // tf
    n_in = 5 if loss_head is None else 7

    def body(*refs):
        x_ref, gn_ref, sc_ref, sh_ref, gate_ref = refs[:5]
        w1_ref, w3_ref, w2_ref, xo_ref, h_ref, a_ref, b_ref, y_ref = refs[n_in:n_in + 8]
        hs, acc = refs[-2:]
        i, j = pl.program_id(0), pl.program_id(1)

        if loss_head is not None:
            @pl.when(jnp.logical_and(i == 0, j == 0))
            def _():
                refs[n_in + 9][...] = jnp.zeros_like(refs[n_in + 9])

        @pl.when(j == 0)
        def _():
            xhat, _ = _rms(x_ref[...])
            h = (xhat * gn_ref[...] * (1.0 + sc_ref[...]) + sh_ref[...]).astype(BF16)
            hs[...] = h
            h_ref[...] = h
            acc[...] = jnp.zeros_like(acc)

        h = hs[...]
        a = _dot(h, w1_ref[...], NT)
        b = _dot(h, w3_ref[...], NT)
        a_ref[...] = a.astype(BF16)
        b_ref[...] = b.astype(BF16)
        u = (a * _sigmoid(a) * b).astype(BF16)
        acc[...] += _dot(u, w2_ref[...])

        @pl.when(j == nf - 1)
        def _():
            y = acc[...]
            y_ref[...] = y.astype(BF16)
            x_out = x_ref[...] + 0.5 * gate_ref[...] * y
            if loss_head is None:
                xo_ref[...] = x_out
            else:
                dx, d_g, loss = _loss_head(x_out, refs[5][...], refs[6][...])
                xo_ref[...] = dx
                refs[n_in + 8][...] = (0.5 * gate_ref[...] * dx).astype(BF16)
                _add_rows(refs[n_in + 9], [d_g, loss])

    row = pl.BlockSpec((tm, d), lambda i, j: (i, 0))
    vec = pl.BlockSpec((1, d), lambda i, j: (0, 0))
    wide = pl.BlockSpec((tm, tf), lambda i, j: (i, j))
    head = loss_head is not None
    return _call_with_rider(
        body, rider, name=name, grid=(t // tm, nf),
        in_specs=[row, vec, vec, vec, vec] + ([row, vec] if head else []) + _ffn_weight_specs(first, tf, d),
        out_specs=[row, row, wide, wide, row] + ([row, pl.BlockSpec((8, d), lambda i, j: (0, 0))] if head else []),
        out_shape=[jax.ShapeDtypeStruct((t, d), F32), jax.ShapeDtypeStruct((t, d), BF16),
                   jax.ShapeDtypeStruct((t, f), BF16), jax.ShapeDtypeStruct((t, f), BF16),
                   jax.ShapeDtypeStruct((t, d), BF16)]
        + ([jax.ShapeDtypeStruct((t, d), BF16), jax.ShapeDtypeStruct((8, d), F32)] if head else []),
        scratch_shapes=[pltpu.VMEM((tm, d), BF16), pltpu.VMEM((tm, d), F32)],
        operands=(x, gn, sc, sh, gate) + (tuple(loss_head) if head else ()) + (ws, ws, ws))


def _loss_head(x, target, g):
    d = x.shape[-1]
    xhat, r = _rms(x)
    err = xhat * g - target
    dyf = err * (1.0 / d)
    dxh = dyf * g
    dx = r * (dxh - xhat * jnp.mean(dxh * xhat, axis=-1, keepdims=True))
    return dx, jnp.sum(dyf * xhat, axis=0, keepdims=True), jnp.sum(err * err, axis=0, keepdims=True) * (0.5 / d)


def _ffn_weight_specs(first, tf, d):
    return [pl.BlockSpec((None, tf, d), lambda i, j, w=first + k: (w, j, 0)) for k in range(3)]


def ffn_backward_gate(dy, a, b, ws, first, name, rider=None):
    t, d = dy.shape
    f = ws.shape[1]
    tm, tf = _tile(t, FFN_BWD_TILE[0], 16), _tile(f, FFN_BWD_TILE[1])
    nf = f // tf

    def gate_body(dy_ref, a_ref, b_ref, w2_ref, da_ref, db_ref, gw2_ref):
        dy_v = dy_ref[...]
        du = _dot(dy_v, w2_ref[...], NT)
        av = a_ref[...].astype(F32)
        bv = b_ref[...].astype(F32)
        s = _sigmoid(av)
        sa = av * s
        da_ref[...] = (du * bv * (s + sa * (1.0 - s))).astype(BF16)
        db_ref[...] = (du * sa).astype(BF16)
        part = _dot((sa * bv).astype(BF16), dy_v, TN)

        @pl.when(pl.program_id(1) == 0)
        def _():
            gw2_ref[...] = part

        @pl.when(pl.program_id(1) > 0)
        def _():
            gw2_ref[...] += part

    hidden = jax.ShapeDtypeStruct((t, f), BF16)
    wide_t = pl.BlockSpec((tm, tf), lambda j, i: (i, j))
    return _call_with_rider(
        gate_body, rider, name=name, grid=(nf, t // tm),
        in_specs=[pl.BlockSpec((tm, d), lambda j, i: (i, 0)), wide_t, wide_t,
                  pl.BlockSpec((None, tf, d), lambda j, i: (first + 2, j, 0))],
        out_specs=[wide_t, wide_t, pl.BlockSpec((tf, d), lambda j, i: (j, 0))],
        out_shape=[hidden, hidden, jax.ShapeDtypeStruct((f, d), F32)],
        scratch_shapes=[], operands=(dy, a, b, ws))


def ffn_backward_norm(da, db, dxo, x, y, gn, sc, ws, first, name, rider=None):
    t, d = x.shape
    f = ws.shape[1]
    tm, tf = _tile(t, FFN_BWD_TILE[0], 16), _tile(f, FFN_BWD_TILE[1])
    nf = f // tf
    row = pl.BlockSpec((tm, d), lambda i, j: (i, 0))
    vec = pl.BlockSpec((1, d), lambda i, j: (0, 0))
    wide = pl.BlockSpec((tm, tf), lambda i, j: (i, j))

    def norm_body(da_ref, db_ref, w1_ref, w3_ref, dxo_ref, x_ref, y_ref, gn_ref, sc_ref, dx_ref, sums_ref, acc):
        i, j = pl.program_id(0), pl.program_id(1)

        @pl.when(jnp.logical_and(i == 0, j == 0))
        def _():
            sums_ref[...] = jnp.zeros_like(sums_ref)

        part = _dot(da_ref[...], w1_ref[...]) + _dot(db_ref[...], w3_ref[...])

        @pl.when(j == 0)
        def _():
            acc[...] = part

        @pl.when(jnp.logical_and(j > 0, j < nf - 1))
        def _():
            acc[...] += part

        @pl.when(j == nf - 1)
        def _():
            dh = part if nf == 1 else acc[...] + part
            dxo_v = dxo_ref[...]
            dx, d_sh, d_sc, d_gn = _norm_mod_bwd(dh, x_ref[...], gn_ref[...], sc_ref[...])
            dx_ref[...] = dxo_v + dx
            d_gate = jnp.sum(dxo_v * (0.5 * y_ref[...].astype(F32)), axis=0, keepdims=True)
            _add_rows(sums_ref, [d_sh, d_sc, d_gate, d_gn])

    w1_spec, w3_spec, _ = _ffn_weight_specs(first, tf, d)
    return _call_with_rider(
        norm_body, rider, name=name, grid=(t // tm, nf),
        in_specs=[wide, wide, w1_spec, w3_spec, row, row, row, vec, vec],
        out_specs=[row, pl.BlockSpec((8, d), lambda i, j: (0, 0))],
        out_shape=[jax.ShapeDtypeStruct((t, d), F32), jax.ShapeDtypeStruct((8, d), F32)],
        scratch_shapes=[pltpu.VMEM((tm, d), F32)],
        operands=(da, db, ws, ws, dxo, x, y, gn, sc))


def matmul_tn(a, b, name, rider=None):
    parts = list(a) if isinstance(a, (list, tuple)) else [a]
    t, n = b.shape
    widths = [p.shape[1] for p in parts]
    tm = _tile(functools.reduce(math.gcd, widths), GRAD_TILE)
    tn, tk = _tile(n, GRAD_TILE), _tile(t, GRAD_DEPTH, 16)
    nk = t // tk
    counts = [w // tm for w in widths]
    firsts = [sum(counts[:p]) for p in range(len(parts))]

    def body(*refs):
        a_refs, (b_ref, o_ref, acc) = refs[:len(parts)], refs[len(parts):]
        i, k = pl.program_id(0), pl.program_id(2)

        @pl.when(k == 0)
        def _():
            acc[...] = jnp.zeros_like(acc)

        for a_ref, lo, cnt in zip(a_refs, firsts, counts):
            def accumulate(a_ref=a_ref):
                acc[...] += _dot(a_ref[...], b_ref[...], TN)

            if len(parts) == 1:
                accumulate()
            else:
                pl.when(jnp.logical_and(i >= lo, i < lo + cnt))(accumulate)

        @pl.when(k == nk - 1)
        def _():
            o_ref[...] = acc[...]

    def part_spec(lo, cnt):
        if len(parts) == 1:
            return pl.BlockSpec((tk, tm), lambda i, j, k: (k, i))

        def index(i, j, k):
            mine = jnp.logical_and(i >= lo, i < lo + cnt)
            return jnp.where(mine, k, 0), jnp.clip(i - lo, 0, cnt - 1)
        return pl.BlockSpec((tk, tm), index)

    out = _call_with_rider(
        body, rider, name=name, grid=(sum(counts), n // tn, nk),
        in_specs=[part_spec(lo, cnt) for lo, cnt in zip(firsts, counts)]
        + [pl.BlockSpec((tk, tn), lambda i, j, k: (k, j))],
        out_specs=[pl.BlockSpec((tm, tn), lambda i, j, k: (i, j))],
        out_shape=[jax.ShapeDtypeStruct((sum(widths), n), F32)],
        scratch_shapes=[pltpu.VMEM((tm, tn), F32)], operands=(*parts, b))
    return out[0] if rider is None else out


def mix_in_forward(x, gn, sc, sh, w_in):
    t, d = x.shape
    tm = _tile(t, ROW_TILE, 16)

    def body(x_ref, gn_ref, sc_ref, sh_ref, w_ref, h_ref, zc_ref, zm_ref):
        xhat, _ = _rms(x_ref[...])
        h = (xhat * gn_ref[...] * (1.0 + sc_ref[...]) + sh_ref[...]).astype(BF16)
        h_ref[...] = h
        z = _dot(h, w_ref[...], NT)
        zc_ref[...] = z[:, :ZC_COLS].astype(BF16)
        zm_ref[...] = z[:, ZC_COLS:].astype(BF16)

    row = pl.BlockSpec((tm, d), lambda i: (i, 0))
    vec = pl.BlockSpec((1, d), lambda i: (0, 0))
    return pl.pallas_call(
        body, name="mix_in_fwd", grid=(t // tm,),
        in_specs=[row, vec, vec, vec, _row(w_in)],
        out_specs=[row, pl.BlockSpec((tm, ZC_COLS), lambda i: (i, 0)), pl.BlockSpec((tm, ZM_COLS), lambda i: (i, 0))],
        out_shape=[jax.ShapeDtypeStruct((t, d), BF16), jax.ShapeDtypeStruct((t, ZC_COLS), BF16),
                   jax.ShapeDtypeStruct((t, ZM_COLS), BF16)],
        compiler_params=_params(("arbitrary",)),
    )(x, gn, sc, sh, w_in)


def _rope_tables(pos, inv_freq):
    ang = pos * inv_freq
    lane = lax.broadcasted_iota(jnp.int32, ang.shape, 1)
    cos, sin = jnp.cos(ang), jnp.sin(ang)
    half = QK_ROPE // 2
    return cos, jnp.where(lane < half, -sin, 0.0), jnp.where(jnp.logical_and(lane >= half, lane < QK_ROPE), sin, 0.0)


def _rope(v, tables):
    cos, sin_a, sin_b = tables
    return v * cos + pltpu.roll(v, LANES - QK_ROPE // 2, 1) * sin_a + pltpu.roll(v, QK_ROPE // 2, 1) * sin_b


def _rope_transposed(dv, tables):
    cos, sin_a, sin_b = tables
    return dv * cos + pltpu.roll(dv * sin_a, QK_ROPE // 2, 1) + pltpu.roll(dv * sin_b, LANES - QK_ROPE // 2, 1)


def mla_project(zm, pos, inv_freq, qg, kvg, w_uq, w_ukv):
    t = zm.shape[0]
    tm = _tile(t, ROW_TILE, 16)

    def body(zm_ref, pos_ref, if_ref, qg_ref, kvg_ref, wq_ref, wkv_ref, qn_ref, kvn_ref, q_ref, k_ref, v_ref):
        zv = zm_ref[...].astype(F32)
        qn = (_rms(zv[:, :Q_LORA])[0] * qg_ref[...]).astype(BF16)
        kvn = (_rms(zv[:, Q_LORA:Q_LORA + KV_LORA])[0] * kvg_ref[...]).astype(BF16)
        qn_ref[...] = qn
        kvn_ref[...] = kvn
        qf = _dot(qn, wq_ref[...], NT) * QK_FOLD
        kvf = _dot(kvn, wkv_ref[...], NT)
        tables = _rope_tables(pos_ref[...], if_ref[...])
        kr = _rope(zv[:, Q_LORA + KV_LORA:], tables).astype(BF16)
        for h in range(MLA_HEADS):
            lo = h * HEAD_PAD
            q_ref[:, lo:lo + QK_NOPE] = qf[:, lo:lo + QK_NOPE].astype(BF16)
            q_ref[:, lo + QK_NOPE:lo + HEAD_PAD] = _rope(qf[:, lo + QK_NOPE:lo + HEAD_PAD], tables).astype(BF16)
            k_ref[:, lo:lo + QK_NOPE] = kvf[:, h * QK_NOPE:(h + 1) * QK_NOPE].astype(BF16)
            k_ref[:, lo + QK_NOPE:lo + HEAD_PAD] = kr
        v_ref[...] = kvf[:, MLA_HEADS * QK_NOPE:].astype(BF16)

    def rows(n):
        return pl.BlockSpec((tm, n), lambda i: (i, 0))

    return pl.pallas_call(
        body, name="mla_project", grid=(t // tm,),
        in_specs=[rows(ZM_COLS), rows(1), _row(inv_freq), _row(qg), _row(kvg), _row(w_uq), _row(w_ukv)],
        out_specs=[rows(Q_LORA), rows(KV_LORA), rows(QK_COLS), rows(QK_COLS), rows(MLA_WIDTH)],
        out_shape=[jax.ShapeDtypeStruct((t, Q_LORA), BF16), jax.ShapeDtypeStruct((t, KV_LORA), BF16),
                   jax.ShapeDtypeStruct((t, QK_COLS), BF16), jax.ShapeDtypeStruct((t, QK_COLS), BF16),
                   jax.ShapeDtypeStruct((t, MLA_WIDTH), BF16)],
        compiler_params=_params(("arbitrary",)),
    )(zm, pos, inv_freq, qg, kvg, w_uq, w_ukv)


def _chunk_mask(shape, q_axis):
    qi = lax.broadcasted_iota(jnp.int32, shape, q_axis) // CHUNK
    ki = lax.broadcasted_iota(jnp.int32, shape, 1 - q_axis) // CHUNK
    return ki <= qi


def attention_forward(q, k, v, rider=None):
    t = q.shape[0]
    tq = _tile(t, ATTN_TILE, CHUNK)

    def body(q_ref, k_ref, v_ref, o_ref, lse_ref):
        i = pl.program_id(1)
        qv = q_ref[...]

        def step(kb, carry, masked, tiles=1):
            m, l, acc = carry
            keys = pl.ds(pl.multiple_of(kb * tq, tq), tiles * tq)
            s = _dot(qv, k_ref[keys, :], NT)
            if masked:
                s = jnp.where(_chunk_mask(s.shape, 0), s, NEG_INF)
            m_new = jnp.maximum(m, jnp.max(s, axis=-1, keepdims=True))
            alpha = jnp.exp2(m - m_new)
            p = jnp.exp2(s - m_new)
            l = alpha * l + jnp.sum(p, axis=-1, keepdims=True)
            acc = alpha * acc + _dot(p.astype(BF16), v_ref[keys, :])
            return m_new, l, acc

        init = (jnp.full((tq, 1), NEG_INF, F32), jnp.zeros((tq, 1), F32), jnp.zeros((tq, V_HEAD), F32))
        carry = lax.fori_loop(0, i // 2, lambda pb, cr: step(2 * pb, cr, False, 2), init)
        carry = lax.fori_loop(0, i % 2, lambda _, cr: step(i - 1, cr, False), carry)
        m, l, acc = step(i, carry, True)
        o_ref[...] = (acc / l).astype(BF16)
        lse_ref[0] = m + jnp.log2(l)

    return _call_with_rider(
        body, rider, name="attn_fwd", grid=(MLA_HEADS, t // tq),
        in_specs=[pl.BlockSpec((tq, HEAD_PAD), lambda h, i: (i, h)),
                  pl.BlockSpec((t, HEAD_PAD), lambda h, i: (0, h)),
                  pl.BlockSpec((t, V_HEAD), lambda h, i: (0, h))],
        out_specs=[pl.BlockSpec((tq, V_HEAD), lambda h, i: (i, h)),
                   pl.BlockSpec((1, tq, 1), lambda h, i: (h, i, 0))],
        out_shape=[jax.ShapeDtypeStruct((t, MLA_WIDTH), BF16), jax.ShapeDtypeStruct((MLA_HEADS, t, 1), F32)],
        scratch_shapes=[], operands=(q, k, v))


def attention_backward(q, k, v, do, lse, delta, rider=None):
    t = q.shape[0]
    tq = _tile(t, ATTN_TILE, CHUNK)
    nq = t // tq

    def body(q_ref, k_ref, v_ref, do_ref, lse_ref, delta_ref, dq_ref, dk_ref, dv_ref, dq_acc):
        kb = pl.program_id(1)

        @pl.when(kb == 0)
        def _():
            dq_acc[...] = jnp.zeros_like(dq_acc)

        kv, vv = k_ref[...], v_ref[...]

        def step(qb, carry, masked):
            dk, dv = carry
            rows = pl.ds(pl.multiple_of(qb * tq, tq), tq)
            qv, dov = q_ref[rows, :], do_ref[rows, :]
            s = _dot(kv, qv, NT)
            if masked:
                s = jnp.where(_chunk_mask(s.shape, 1), s, NEG_INF)
            p = jnp.exp2(s - lse_ref[0, qb])
            dv = dv + _dot(p.astype(BF16), dov)
            dp = _dot(vv, dov, NT)
            ds = (p * (dp - delta_ref[0, qb]) * LN_2).astype(BF16)
            dk = dk + _dot(ds, qv)
            dq_acc[rows, :] += _dot(ds, kv, TN)
            return dk, dv

        carry = step(kb, (jnp.zeros((tq, HEAD_PAD), F32), jnp.zeros((tq, V_HEAD), F32)), True)
        odd = (nq - 1 - kb) % 2
        carry = lax.fori_loop(0, odd, lambda _, cr: step(kb + 1, cr, False), carry)
        first = kb + 1 + odd
        dk, dv = lax.fori_loop(0, (nq - first) // 2,
                               lambda pb, cr: step(first + 2 * pb + 1, step(first + 2 * pb, cr, False), False), carry)
        dk_ref[...] = dk.astype(BF16)
        dv_ref[...] = dv.astype(BF16)

        @pl.when(kb == nq - 1)
        def _():
            dq_ref[...] = dq_acc[...].astype(BF16)

    stat = pl.BlockSpec((1, nq, 1, tq), lambda h, j: (h, 0, 0, 0))
    return _call_with_rider(
        body, rider, name="attn_bwd", grid=(MLA_HEADS, nq),
        in_specs=[pl.BlockSpec((t, HEAD_PAD), lambda h, j: (0, h)),
                  pl.BlockSpec((tq, HEAD_PAD), lambda h, j: (j, h)),
                  pl.BlockSpec((tq, V_HEAD), lambda h, j: (j, h)),
                  pl.BlockSpec((t, V_HEAD), lambda h, j: (0, h)), stat, stat],
        out_specs=[pl.BlockSpec((t, HEAD_PAD), lambda h, j: (0, h)),
                   pl.BlockSpec((tq, HEAD_PAD), lambda h, j: (j, h)),
                   pl.BlockSpec((tq, V_HEAD), lambda h, j: (j, h))],
        out_shape=[jax.ShapeDtypeStruct((t, QK_COLS), BF16), jax.ShapeDtypeStruct((t, QK_COLS), BF16),
                   jax.ShapeDtypeStruct((t, MLA_WIDTH), BF16)],
        scratch_shapes=[pltpu.VMEM((t, HEAD_PAD), F32)], operands=(q, k, v, do, lse, delta))


HALO = 16


def _halo_spec(tm, n, step, last):
    return pl.BlockSpec((HALO, n), lambda i: (jnp.clip(i * (tm // HALO) + step, 0, last), 0))


def _shift_rows(v, prev, n):
    out = pltpu.roll(v, n, 0)
    row = lax.broadcasted_iota(jnp.int32, v.shape, 0)
    for r in range(n):
        out = jnp.where(row == r, prev[HALO - n + r:HALO - n + r + 1, :], out)
    return out


def _advance_rows(v, nxt, n):
    rows = v.shape[0]
    out = pltpu.roll(v, rows - n, 0)
    row = lax.broadcasted_iota(jnp.int32, v.shape, 0)
    for r in range(n):
        out = jnp.where(row == rows - n + r, nxt[r:r + 1, :], out)
    return out


def _conv_taps(zc, zc_prev, first):
    w = CONV_WIDTH
    u = zc[:, w:2 * w] * zc[:, 2 * w:]
    up = jnp.where(first, 0.0, zc_prev[:, w:2 * w] * zc_prev[:, 2 * w:])
    return u, _shift_rows(u, up, 1), _shift_rows(u, up, 2)


def mix_out_forward(zc, o, conv_w, og, gmat_a, gmat_b, w_out, x, gate):
    t, d = x.shape
    tm = _tile(t, ROW_TILE, 16)
    w = CONV_WIDTH

    def body(zc_ref, zp_ref, o_ref, cw_ref, og_ref, ga_ref, gb_ref, w_ref, x_ref, gate_ref,
             xo_ref, yn_ref, y_ref, ya_ref):
        zc_v = zc_ref[...].astype(F32)
        u, u1, u2 = _conv_taps(zc_v, zp_ref[...].astype(F32), pl.program_id(0) == 0)
        cw = cw_ref[...]
        ya = zc_v[:, :w] * (cw[0:1] * u2 + cw[1:2] * u1 + cw[2:3] * u)
        ya_ref[...] = ya.astype(BF16)
        ov = o_ref[...].astype(F32)
        ogv = og_ref[...]
        yn_ref[:, :w] = (ya * lax.rsqrt(_group_mean(ya * ya, ga_ref[...]) + EPS) * ogv[:, :w]).astype(BF16)
        yn_ref[:, w:] = (ov * lax.rsqrt(_group_mean(ov * ov, gb_ref[...]) + EPS) * ogv[:, w:]).astype(BF16)
        y = _dot(yn_ref[...], w_ref[...])
        y_ref[...] = y.astype(BF16)
        xo_ref[...] = x_ref[...] + gate_ref[...] * y

    def rows(n):
        return pl.BlockSpec((tm, n), lambda i: (i, 0))

    return pl.pallas_call(
        body, name="mix_out_fwd", grid=(t // tm,),
        in_specs=[rows(ZC_COLS), _halo_spec(tm, ZC_COLS, -1, t // HALO - 1), rows(MLA_WIDTH), _row(conv_w), _row(og),
                  _row(gmat_a), _row(gmat_b), _row(w_out), rows(d), _row(gate)],
        out_specs=[rows(d), rows(MIX_WIDTH), rows(d), rows(w)],
        out_shape=[jax.ShapeDtypeStruct((t, d), F32), jax.ShapeDtypeStruct((t, MIX_WIDTH), BF16),
                   jax.ShapeDtypeStruct((t, d), BF16), jax.ShapeDtypeStruct((t, w), BF16)],
        compiler_params=_params(("arbitrary",)),
    )(zc, zc, o, conv_w, og, gmat_a, gmat_b, w_out, x, gate)


def _group_norm_bwd(dyn, y, og, gmat):
    rs = lax.rsqrt(_group_mean(y * y, gmat) + EPS)
    yhat = y * rs
    d_og = jnp.sum(dyn * yhat, axis=0, keepdims=True)
    dyh = dyn * og
    return rs * (dyh - yhat * _group_mean(dyh * yhat, gmat)), d_og


def mix_out_backward(dxo, y, gate, ya, o, og, gmat_a, gmat_b, w_out, rider=None):
    t, d = dxo.shape
    tm = _tile(t, ROW_TILE, 16)
    w = CONV_WIDTH

    def body(dxo_ref, y_ref, gate_ref, ya_ref, o_ref, og_ref, ga_ref, gb_ref, w_ref,
             dy_ref, dya_ref, do_ref, delta_ref, sd_ref, so_ref):
        @pl.when(pl.program_id(0) == 0)
        def _():
            sd_ref[...] = jnp.zeros_like(sd_ref)
            so_ref[...] = jnp.zeros_like(so_ref)

        dxo_v = dxo_ref[...]
        dy = (gate_ref[...] * dxo_v).astype(BF16)
        dy_ref[...] = dy
        sd_ref[0:1, :] += jnp.sum(dxo_v * y_ref[...].astype(F32), axis=0, keepdims=True)
        dyn = _dot(dy, w_ref[...], NT)
        ogv = og_ref[...]
        ov = o_ref[...].astype(F32)
        dya, d_og_a = _group_norm_bwd(dyn[:, :w], ya_ref[...].astype(F32), ogv[:, :w], ga_ref[...])
        dov, d_og_b = _group_norm_bwd(dyn[:, w:], ov, ogv[:, w:], gb_ref[...])
        dya_ref[...] = dya.astype(BF16)
        do_ref[...] = dov.astype(BF16)
        so_ref[0:1, :w] += d_og_a
        so_ref[0:1, w:] += d_og_b
        prod = dov * ov
        for h in range(MLA_HEADS):
            delta_ref[h] = jnp.sum(prod[:, h * V_HEAD:(h + 1) * V_HEAD], axis=-1, keepdims=True)

    def rows(n):
        return pl.BlockSpec((tm, n), lambda i: (i, 0))

    return _call_with_rider(
        body, rider, name="mix_out_bwd", grid=(t // tm,),
        in_specs=[rows(d), rows(d), _row(gate), rows(w), rows(MLA_WIDTH), _row(og), _row(gmat_a), _row(gmat_b),
                  _row(w_out)],
        out_specs=[rows(d), rows(w), rows(MLA_WIDTH), pl.BlockSpec((MLA_HEADS, tm, 1), lambda i: (0, i, 0)),
                   pl.BlockSpec((8, d), lambda i: (0, 0)), pl.BlockSpec((8, MIX_WIDTH), lambda i: (0, 0))],
        out_shape=[jax.ShapeDtypeStruct((t, d), BF16), jax.ShapeDtypeStruct((t, w), BF16),
                   jax.ShapeDtypeStruct((t, MLA_WIDTH), BF16), jax.ShapeDtypeStruct((MLA_HEADS, t, 1), F32),
                   jax.ShapeDtypeStruct((8, d), F32), jax.ShapeDtypeStruct((8, MIX_WIDTH), F32)],
        scratch_shapes=[], operands=(dxo, y, gate, ya, o, og, gmat_a, gmat_b, w_out))


def conv_backward(zc, dya, conv_w):
    t = zc.shape[0]
    tm = _tile(t, ROW_TILE, 16)
    nt = t // tm
    w = CONV_WIDTH

    def body(zc_ref, zp_ref, zn_ref, dya_ref, dn_ref, cw_ref, dzc_ref, sums_ref):
        i = pl.program_id(0)

        @pl.when(i == 0)
        def _():
            sums_ref[...] = jnp.zeros_like(sums_ref)

        zc_v = zc_ref[...].astype(F32)
        u, u1, u2 = _conv_taps(zc_v, zp_ref[...].astype(F32), i == 0)
        cw = cw_ref[...]
        dya_v = dya_ref[...].astype(F32)
        dyc = dya_v * zc_v[:, :w]
        dyc_next = jnp.where(i == nt - 1, 0.0, dn_ref[...].astype(F32) * zn_ref[:, :w].astype(F32))
        du = cw[2:3] * dyc + cw[1:2] * _advance_rows(dyc, dyc_next, 1) + cw[0:1] * _advance_rows(dyc, dyc_next, 2)
        dzc_ref[:, :w] = (dya_v * (cw[0:1] * u2 + cw[1:2] * u1 + cw[2:3] * u)).astype(BF16)
        dzc_ref[:, w:2 * w] = (du * zc_v[:, 2 * w:]).astype(BF16)
        dzc_ref[:, 2 * w:] = (du * zc_v[:, w:2 * w]).astype(BF16)
        _add_rows(sums_ref, [jnp.sum(dyc * tap, axis=0, keepdims=True) for tap in (u2, u1, u)])

    def rows(n):
        return pl.BlockSpec((tm, n), lambda i: (i, 0))

    def halo(n, step):
        return _halo_spec(tm, n, step, t // HALO - 1)

    return pl.pallas_call(
        body, name="conv_bwd", grid=(nt,),
        in_specs=[rows(ZC_COLS), halo(ZC_COLS, -1), halo(ZC_COLS, tm // HALO), rows(w), halo(w, tm // HALO),
                  _row(conv_w)],
        out_specs=[rows(ZC_COLS), pl.BlockSpec((8, w), lambda i: (0, 0))],
        out_shape=[jax.ShapeDtypeStruct((t, ZC_COLS), BF16), jax.ShapeDtypeStruct((8, w), F32)],
        compiler_params=_params(("arbitrary",)),
    )(zc, zc, zc, dya, dya, conv_w)


def _rms_bwd(dy, x, g):
    xhat, r = _rms(x)
    d_g = jnp.sum(dy * xhat, axis=0, keepdims=True)
    dxh = dy * g
    return r * (dxh - xhat * jnp.mean(dxh * xhat, axis=-1, keepdims=True)), d_g


def mla_project_backward(dq, dk, dv, zm, pos, inv_freq, qg, kvg, w_uq, w_ukv):
    t = zm.shape[0]
    tm = _tile(t, ROW_TILE, 16)

    def body(dq_ref, dk_ref, dv_ref, zm_ref, pos_ref, if_ref, qg_ref, kvg_ref, wq_ref, wkv_ref,
             dql_ref, dkvl_ref, dzm_ref, sums_ref):
        @pl.when(pl.program_id(0) == 0)
        def _():
            sums_ref[...] = jnp.zeros_like(sums_ref)

        tables = _rope_tables(pos_ref[...], if_ref[...])
        dkr = jnp.zeros((tm, LANES), F32)
        for h in range(MLA_HEADS):
            lo = h * HEAD_PAD
            dql_ref[:, lo:lo + QK_NOPE] = (dq_ref[:, lo:lo + QK_NOPE].astype(F32) * QK_FOLD).astype(BF16)
            dql_ref[:, lo + QK_NOPE:lo + HEAD_PAD] = _rope_transposed(
                dq_ref[:, lo + QK_NOPE:lo + HEAD_PAD].astype(F32) * QK_FOLD, tables).astype(BF16)
            dkvl_ref[:, h * QK_NOPE:(h + 1) * QK_NOPE] = dk_ref[:, lo:lo + QK_NOPE]
            dkr = dkr + dk_ref[:, lo + QK_NOPE:lo + HEAD_PAD].astype(F32)
        dkvl_ref[:, MLA_HEADS * QK_NOPE:] = dv_ref[...]
        zv = zm_ref[...].astype(F32)
        dqn = _dot(dql_ref[...], wq_ref[...])
        dkvn = _dot(dkvl_ref[...], wkv_ref[...])
        dcq, d_qg = _rms_bwd(dqn, zv[:, :Q_LORA], qg_ref[...])
        dckv, d_kvg = _rms_bwd(dkvn, zv[:, Q_LORA:Q_LORA + KV_LORA], kvg_ref[...])
        dzm_ref[:, :Q_LORA] = dcq.astype(BF16)
        dzm_ref[:, Q_LORA:Q_LORA + KV_LORA] = dckv.astype(BF16)
        dzm_ref[:, Q_LORA + KV_LORA:] = _rope_transposed(dkr, tables).astype(BF16)
        sums_ref[0:1, :Q_LORA] += d_qg
        sums_ref[0:1, Q_LORA:Q_LORA + KV_LORA] += d_kvg

    def rows(n):
        return pl.BlockSpec((tm, n), lambda i: (i, 0))

    return pl.pallas_call(
        body, name="mla_project_bwd", grid=(t // tm,),
        in_specs=[rows(QK_COLS), rows(QK_COLS), rows(MLA_WIDTH), rows(ZM_COLS), rows(1), _row(inv_freq),
                  _row(qg), _row(kvg), _row(w_uq), _row(w_ukv)],
        out_specs=[rows(QK_COLS), rows(QK_COLS), rows(ZM_COLS), pl.BlockSpec((8, ZM_COLS), lambda i: (0, 0))],
        out_shape=[jax.ShapeDtypeStruct((t, QK_COLS), BF16), jax.ShapeDtypeStruct((t, QK_COLS), BF16),
                   jax.ShapeDtypeStruct((t, ZM_COLS), BF16), jax.ShapeDtypeStruct((8, ZM_COLS), F32)],
        compiler_params=_params(("arbitrary",)),
    )(dq, dk, dv, zm, pos, inv_freq, qg, kvg, w_uq, w_ukv)


def mix_in_backward(dzc, dzm, w_in, x, dxo, gn, sc, gate, rider=None):
    t, d = x.shape
    tm = _tile(t, ROW_TILE, 16)

    def body(dzc_ref, dzm_ref, w_ref, x_ref, dxo_ref, gn_ref, sc_ref, gate_ref, dx_ref, dy_ref, sums_ref):
        @pl.when(pl.program_id(0) == 0)
        def _():
            sums_ref[...] = jnp.zeros_like(sums_ref)

        dh = _dot(dzc_ref[...], w_ref[:ZC_COLS, :]) + _dot(dzm_ref[...], w_ref[ZC_COLS:, :])
        dx, d_sh, d_sc, d_gn = _norm_mod_bwd(dh, x_ref[...], gn_ref[...], sc_ref[...])
        dx = dxo_ref[...] + dx
        dx_ref[...] = dx
        dy_ref[...] = (0.5 * gate_ref[...] * dx).astype(BF16)
        _add_rows(sums_ref, [d_sh, d_sc, d_gn])

    def rows(n):
        return pl.BlockSpec((tm, n), lambda i: (i, 0))

    return _call_with_rider(
        body, rider, name="mix_in_bwd", grid=(t // tm,),
        in_specs=[rows(ZC_COLS), rows(ZM_COLS), _row(w_in), rows(d), rows(d), _row(gn), _row(sc), _row(gate)],
        out_specs=[rows(d), rows(d), pl.BlockSpec((8, d), lambda i: (0, 0))],
        out_shape=[jax.ShapeDtypeStruct((t, d), F32), jax.ShapeDtypeStruct((t, d), BF16),
                   jax.ShapeDtypeStruct((8, d), F32)],
        scratch_shapes=[], operands=(dzc, dzm, w_in, x, dxo, gn, sc, gate))


def _adamw_step(w, g, m, v):
    m_new = ADAM_B1 * m + (1.0 - ADAM_B1) * g
    v_new = ADAM_B2 * v + (1.0 - ADAM_B2) * (g * g)
    m_hat = m_new / (1.0 - ADAM_B1 ** ADAM_STEP)
    v_hat = v_new / (1.0 - ADAM_B2 ** ADAM_STEP)
    return -ADAM_LR * (m_hat / (jnp.sqrt(v_hat) + ADAM_EPS) + ADAM_WD * w), m_new, v_new


def adamw(w, g, m, v, name):
    r, n = w.shape
    tr = _tile(r, max(8, ADAM_TILE_ELEMS // n), 8)

    def body(w_ref, g_ref, m_ref, v_ref, d_ref, mo_ref, vo_ref):
        d_ref[...], mo_ref[...], vo_ref[...] = _adamw_step(w_ref[...], g_ref[...], m_ref[...], v_ref[...])

    blk = pl.BlockSpec((tr, n), lambda i: (i, 0))
    shape = jax.ShapeDtypeStruct((r, n), F32)
    return pl.pallas_call(
        body, name=name, grid=(r // tr,), in_specs=[blk] * 4, out_specs=[blk] * 3, out_shape=[shape] * 3,
        compiler_params=_params(("arbitrary",)),
    )(w, g, m, v)


def adamw_received(w, own, got, m, v, name):
    r, n = w.shape
    tr = _tile(r, SUM_ROWS, 16)

    def body(w_ref, own_ref, got_ref, m_ref, v_ref, g_ref, d_ref, mo_ref, vo_ref):
        g = own_ref[...]
        for j in range(3):
            g = g + got_ref[j].astype(F32)
        g_ref[...] = g
        d_ref[...], mo_ref[...], vo_ref[...] = _adamw_step(w_ref[...], g, m_ref[...], v_ref[...])

    blk = pl.BlockSpec((tr, n), lambda i: (i, 0))
    shape = jax.ShapeDtypeStruct((r, n), F32)
    return pl.pallas_call(
        body, name=name, grid=(r // tr,),
        in_specs=[blk, blk, pl.BlockSpec((3, tr, n), lambda i: (0, i, 0)), blk, blk],
        out_specs=[blk] * 4, out_shape=[shape] * 4, compiler_params=_params(("arbitrary",)),
    )(w, own, got, m, v)


def _pad_to(v, n):
    return jnp.pad(v, (0, n - v.shape[0]))


def _pad_heads(w, axis_len):
    n = w.shape[1]
    return jnp.pad(w.reshape(MLA_HEADS, axis_len, n), ((0, 0), (0, HEAD_PAD - axis_len), (0, 0))).reshape(-1, n)


def _swap_head_parts(w, inner, outer):
    n = w.shape[1]
    return w.reshape(outer, inner, QK_NOPE, n).transpose(1, 0, 2, 3).reshape(-1, n)


def kernel(x, c, positions, ada_w, ada_b, norm_ffn1_g, ffn1_w1, ffn1_w3, ffn1_w2, norm_mix_g, w_in, conv_w, q_norm_g, w_uq, kv_norm_g, w_ukv, out_norm_g, w_out, norm_ffn2_g, ffn2_w1, ffn2_w3, ffn2_w2, final_norm_g, loss_target, m_ada_w, m_ada_b, m_norm_ffn1_g, m_ffn1_w1, m_ffn1_w3, m_ffn1_w2, m_norm_mix_g, m_w_in, m_conv_w, m_q_norm_g, m_w_uq, m_kv_norm_g, m_w_ukv, m_out_norm_g, m_w_out, m_norm_ffn2_g, m_ffn2_w1, m_ffn2_w3, m_ffn2_w2, m_final_norm_g, v_ada_w, v_ada_b, v_norm_ffn1_g, v_ffn1_w1, v_ffn1_w3, v_ffn1_w2, v_norm_mix_g, v_w_in, v_conv_w, v_q_norm_g, v_w_uq, v_kv_norm_g, v_w_ukv, v_out_norm_g, v_w_out, v_norm_ffn2_g, v_ffn2_w1, v_ffn2_w3, v_ffn2_w2, v_final_norm_g):
    t, d = x.shape[1], x.shape[2]
    f = ffn1_w2.shape[1] * N_DEV
    me = 4 * lax.axis_index("x") + 2 * lax.axis_index("y") + lax.axis_index("c")
    my_c = lax.axis_index("c")
    my_chip = 2 * lax.axis_index("x") + lax.axis_index("y")
    xs = x[0]
    n_ada = ada_w.shape[2]
    cw_n = conv_w.shape[2]

    c_rows = jnp.broadcast_to(c, (8, d))
    conv_rows = jnp.pad(conv_w[0], ((0, 8 - CONV_K), (0, LANES - cw_n)))
    ffn1_blocks = jnp.stack([ffn1_w1[0].T, ffn1_w3[0].T, ffn1_w2[0]]).astype(BF16)
    ffn2_blocks = jnp.stack([ffn2_w1[0].T, ffn2_w3[0].T, ffn2_w2[0]]).astype(BF16)
    c_all, conv_all, ffn1_all = all_gather_relayed([c_rows, conv_rows, ffn1_blocks], [0, 0, 1], "gather_first")
    c_all = c_all[:, 0, :]
    conv_full8 = conv_all[:, :, :cw_n].transpose(1, 0, 2).reshape(8, CONV_WIDTH)
    ffn1_ws = ffn1_all.reshape(3, f, d)
    gather_mix = riding_gather(
        [w_in[0].T.astype(BF16), w_uq[0].T.astype(BF16), w_ukv[0].T.astype(BF16), w_out[0].astype(BF16)], [0, 0, 0, 0])

    ada_b_cols = lax.dynamic_slice_in_dim(ada_b, me * n_ada, n_ada, axis=1)
    mod_cols = ada_forward(c_all, ada_w[0], ada_b_cols)
    mod_all, = all_gather([mod_cols], [0], "gather_mod")
    mod = lax.dynamic_index_in_dim(mod_all, me, axis=1, keepdims=False).reshape(N_MOD, 1, d)
    sh1, sc1, g1, sh2, sc2, g2, sh3, sc3, g3 = [mod[i] for i in range(N_MOD)]

    gf = final_norm_g.reshape(1, d)
    x1, h1, a1, b1, y1, *gathered = ffn_forward(xs, norm_ffn1_g, sc1, sh1, g1, ffn1_ws, 0, "ffn1_fwd", gather_mix)
    w_in_p = jnp.pad(gathered[0].reshape(IN_COLS, d), ((0, ZC_COLS + ZM_COLS - IN_COLS), (0, 0)))
    w_uq_p = _pad_heads(gathered[1].reshape(-1, Q_LORA), QK_NOPE + QK_ROPE)
    w_ukv_p = _swap_head_parts(gathered[2].reshape(-1, KV_LORA), 2, MLA_HEADS)
    w_out_f = gathered[3].reshape(MIX_WIDTH, d)
    h2, zc, zm = mix_in_forward(x1, norm_mix_g, sc2, sh2, w_in_p)
    pos = positions[0].astype(F32).reshape(t, 1)
    inv_freq = ROPE_THETA ** (-jnp.arange(0, QK_ROPE, 2, dtype=F32) / QK_ROPE)
    inv_freq = jnp.concatenate([inv_freq, inv_freq, jnp.zeros((LANES - QK_ROPE,), F32)]).reshape(1, LANES)
    qn, kvn, q, k, v = mla_project(zm, pos, inv_freq, q_norm_g, kv_norm_g, w_uq_p, w_ukv_p)
    o, lse, ffn2_all = attention_forward(q, k, v, riding_gather([ffn2_blocks], [1]))
    ffn2_ws = ffn2_all.reshape(3, f, d)
    lane = jnp.arange(CONV_WIDTH)
    gmat_a = (lane[:, None] // (CONV_WIDTH // CONV_GROUPS) == lane[None, :] // (CONV_WIDTH // CONV_GROUPS))
    gmat_a = (gmat_a / (CONV_WIDTH // CONV_GROUPS)).astype(BF16)
    gmat_b = ((lane[:, None] // V_HEAD == lane[None, :] // V_HEAD) / V_HEAD).astype(BF16)
    x2, yn, y2, ya = mix_out_forward(zc, o, conv_full8, out_norm_g, gmat_a, gmat_b, w_out_f, x1, g2)
    dx3, h3, a3, b3, y3, dy3, sums_f = ffn_forward(x2, norm_ffn2_g, sc3, sh3, g3, ffn2_ws, 0, "ffn2_fwd",
                                                   loss_head=(loss_target[0], gf))

    chip_idx = jnp.bitwise_xor(my_chip, jnp.array([0, 2, 1, 3], jnp.int32)).astype(jnp.int32)
    src_idx = (2 * chip_idx + my_c).astype(jnp.int32)

    def row_blocks(named):
        return [g.reshape(N_DEV, g.shape[0] // N_DEV, g.shape[1]) for _, g in named]

    def chip_sums(named, g8, got):
        return [add_sibling(g, r, src_idx, chip_idx, "rs_add_" + n) for g, r, (n, _) in zip(g8, got, named)]

    da3, db3, g_w2b = ffn_backward_gate(dy3, a3, b3, ffn2_ws, 0, "ffn2_bwd_gate")
    dx2, sums_3 = ffn_backward_norm(da3, db3, dx3, x2, y3, norm_ffn2_g, sc3, ffn2_ws, 0, "ffn2_bwd_norm")
    ffn2_named = [("ffn2_w1", matmul_tn(da3, h3, "ffn2_gw1")), ("ffn2_w3", matmul_tn(db3, h3, "ffn2_gw3")),
                  ("ffn2_w2", g_w2b)]
    ffn2_g8 = row_blocks(ffn2_named)
    dy2, dya, do, delta, sums_2d, sums_2o, *ffn2_sib = mix_out_backward(
        dx2, y2, g2, ya, o, out_norm_g, gmat_a, gmat_b, w_out_f, riding_sibling(ffn2_g8))
    ffn2_sums = chip_sums(ffn2_named, ffn2_g8, ffn2_sib)
    g_w_out = matmul_tn(yn, dy2, "gw_out")
    nq = t // _tile(t, ATTN_TILE, CHUNK)
    stat_shape = (MLA_HEADS, nq, 1, t // nq)
    dq, dk, dv, *ffn2_got = attention_backward(q, k, v, do, lse.reshape(stat_shape), delta.reshape(stat_shape),
                                               riding_exchange([s[1] for s in ffn2_sums]))
    dzc, sums_c = conv_backward(zc, dya, conv_full8)
    dql, dkvl, dzm, sums_m = mla_project_backward(dq, dk, dv, zm, pos, inv_freq, q_norm_g, kv_norm_g, w_uq_p, w_ukv_p)
    g_w_uq_p = matmul_tn(dql, qn, "gw_uq")
    g_w_ukv_p = matmul_tn(dkvl, kvn, "gw_ukv")
    g_w_in = matmul_tn([dzc, dzm], h2, "gw_in")[:IN_COLS]
    g_w_uq = g_w_uq_p.reshape(MLA_HEADS, HEAD_PAD, Q_LORA)[:, :QK_NOPE + QK_ROPE].reshape(-1, Q_LORA)
    g_w_ukv = _swap_head_parts(g_w_ukv_p, MLA_HEADS, 2)
    mix_named = [("w_in", g_w_in), ("w_uq", g_w_uq), ("w_ukv", g_w_ukv), ("w_out", g_w_out)]
    mix_g8 = row_blocks(mix_named)
    dx1, dy1, sums_1m, *mix_sib = mix_in_backward(dzc, dzm, w_in_p, x1, dx2, norm_mix_g, sc2, g1, riding_sibling(mix_g8))
    mix_sums = chip_sums(mix_named, mix_g8, mix_sib)
    da1, db1, g_w2a, *mix_got = ffn_backward_gate(dy1, a1, b1, ffn1_ws, 0, "ffn1_bwd_gate",
                                                  riding_exchange([s[1] for s in mix_sums]))
    ffn1_pair = [("ffn1_w2", g_w2a), ("ffn1_w1", matmul_tn(da1, h1, "ffn1_gw1"))]
    pair_g8 = row_blocks(ffn1_pair)
    g_w3a, *pair_sib = matmul_tn(db1, h1, "ffn1_gw3", riding_sibling(pair_g8))
    ffn1_last = [("ffn1_w3", g_w3a)]
    last_g8 = row_blocks(ffn1_last)
    ffn1_named = ffn1_pair + ffn1_last
    ffn1_sums = chip_sums(ffn1_pair, pair_g8, pair_sib) + chip_sums(
        ffn1_last, last_g8, exchange_sibling(last_g8, "rs_sibling_ffn1_w3"))
    dx0, sums_1, *ffn1_got = ffn_backward_norm(da1, db1, dx1, xs, y1, norm_ffn1_g, sc1, ffn1_ws, 0, "ffn1_bwd_norm",
                                               riding_exchange([s[1] for s in ffn1_sums]))
    reduced = {}
    for named, group_sums, group_got in ((ffn2_named, ffn2_sums, ffn2_got), (mix_named, mix_sums, mix_got),
                                         (ffn1_named, ffn1_sums, ffn1_got)):
        for (n, _), (own, _), got in zip(named, group_sums, group_got):
            reduced[n] = (own, got)

    dmod = jnp.concatenate([sums_1[0], sums_1[1], sums_1[2], sums_1m[0], sums_1m[1], sums_2d[0],
                            sums_3[0], sums_3[1], sums_3[2]])
    pieces = [dmod, sums_1[3], sums_1m[2], sums_m[0, :Q_LORA], sums_m[0, Q_LORA:Q_LORA + KV_LORA], sums_2o[0],
              sums_3[3], sums_f[0], sums_f[1], sums_c[:CONV_K].reshape(-1)]
    plens = [p.shape[0] for p in pieces]
    poffs = [sum(plens[:i]) for i in range(len(plens))]
    vec_len = -(-sum(plens) // 1024) * 1024
    vec = _pad_to(jnp.concatenate(pieces), vec_len).reshape(-1, LANES)
    vec_all, = all_gather([vec], [0], "gather_sums")
    tot = sum_devices(vec_all).reshape(-1)
    g_ada_b, g_n1, g_nmix, g_qg, g_kvg, g_og, g_n3, g_gf, loss_lanes, g_conv_full = [
        tot[o:o + n] for o, n in zip(poffs, plens)]
    loss = sum_lanes(loss_lanes.reshape(1, d))[0, 0]
    g_conv = lax.dynamic_slice_in_dim(g_conv_full.reshape(CONV_K, CONV_WIDTH), me * cw_n, cw_n, axis=1)
    dmod_all = vec_all.reshape(N_DEV, vec_len)[:, :N_MOD * d]
    dmod_cols = lax.dynamic_slice_in_dim(dmod_all, me * n_ada, n_ada, axis=1)
    g_ada_w = ada_backward(jnp.pad(c_all, ((0, 8), (0, 0))), jnp.pad(dmod_cols, ((0, 8), (0, 0))))

    def update(name, w, g, m, v, received=None):
        k, n = w.shape[-2:]
        if g.shape == (k, n):
            flat, back = (lambda a: a.reshape(k, n)), (lambda a: a.reshape(w.shape))
        else:
            flat, back = (lambda a: a.reshape(k, n).T), (lambda a: a.T.reshape(w.shape))
        if received is None:
            out = (g,) + tuple(adamw(flat(w), g, flat(m), flat(v), "adamw_" + name))
        else:
            out = adamw_received(flat(w), g, received, flat(m), flat(v), "adamw_" + name)
        return tuple(back(a) for a in out)

    res = {}
    res["ada_w"] = update("ada_w", ada_w, g_ada_w, m_ada_w, v_ada_w)
    big = [("ffn1_w1", ffn1_w1, m_ffn1_w1, v_ffn1_w1), ("ffn1_w3", ffn1_w3, m_ffn1_w3, v_ffn1_w3),
           ("ffn2_w1", ffn2_w1, m_ffn2_w1, v_ffn2_w1), ("ffn2_w3", ffn2_w3, m_ffn2_w3, v_ffn2_w3),
           ("w_in", w_in, m_w_in, v_w_in), ("w_uq", w_uq, m_w_uq, v_w_uq), ("w_ukv", w_ukv, m_w_ukv, v_w_ukv),
           ("ffn1_w2", ffn1_w2, m_ffn1_w2, v_ffn1_w2), ("ffn2_w2", ffn2_w2, m_ffn2_w2, v_ffn2_w2),
           ("w_out", w_out, m_w_out, v_w_out)]
    for name, w, m, v in big:
        res[name] = update(name, w, reduced[name][0], m, v, reduced[name][1])
    smalls = [("ada_b", ada_b, g_ada_b, m_ada_b, v_ada_b),
              ("norm_ffn1_g", norm_ffn1_g, g_n1, m_norm_ffn1_g, v_norm_ffn1_g),
              ("norm_mix_g", norm_mix_g, g_nmix, m_norm_mix_g, v_norm_mix_g),
              ("conv_w", conv_w, g_conv, m_conv_w, v_conv_w),
              ("q_norm_g", q_norm_g, g_qg, m_q_norm_g, v_q_norm_g),
              ("kv_norm_g", kv_norm_g, g_kvg, m_kv_norm_g, v_kv_norm_g),
              ("out_norm_g", out_norm_g, g_og, m_out_norm_g, v_out_norm_g),
              ("norm_ffn2_g", norm_ffn2_g, g_n3, m_norm_ffn2_g, v_norm_ffn2_g),
              ("final_norm_g", final_norm_g, g_gf, m_final_norm_g, v_final_norm_g)]
    slens = [w.size for _, w, _, _, _ in smalls]
    soffs = [sum(slens[:i]) for i in range(len(slens))]
    s_len = -(-sum(slens) // 1024) * 1024

    def pack_small(i):
        return _pad_to(jnp.concatenate([s[i].reshape(-1) for s in smalls]), s_len).reshape(8, -1)

    s_out = adamw(pack_small(1), pack_small(2), pack_small(3), pack_small(4), "adamw_small")
    for (name, w, g, _, _), o, n in zip(smalls, soffs, slens):
        res[name] = (g.reshape(w.shape),) + tuple(a.reshape(-1)[o:o + n].reshape(w.shape) for a in s_out)

    order = ["ada_w", "ada_b", "norm_ffn1_g", "ffn1_w1", "ffn1_w3", "ffn1_w2", "norm_mix_g", "w_in", "conv_w",
             "q_norm_g", "w_uq", "kv_norm_g", "w_ukv", "out_norm_g", "w_out", "norm_ffn2_g", "ffn2_w1", "ffn2_w3",
             "ffn2_w2", "final_norm_g"]
    return (loss, dx0.reshape(x.shape), *[res[n][0] for n in order], *[res[n][1] for n in order],
            *[res[n][2] for n in order], *[res[n][3] for n in order])
```

```python
import functools
import math

import jax
import jax.numpy as jnp
from jax import lax
from jax.experimental import pallas as pl
from jax.experimental.pallas import tpu as pltpu

F32 = jnp.float32
BF16 = jnp.bfloat16
MESH_ID = pl.DeviceIdType.MESH
N_DEV = 8

EPS = 1e-6
CHUNK = 64
N_MOD = 9
CONV_WIDTH = 512
CONV_GROUPS = 8
CONV_K = 3
MLA_HEADS = 4
QK_NOPE = 128
QK_ROPE = 64
V_HEAD = 128
Q_LORA = 384
KV_LORA = 256
ROPE_THETA = 10000.0
MLA_WIDTH = MLA_HEADS * V_HEAD
MIX_WIDTH = CONV_WIDTH + MLA_WIDTH
IN_COLS = 3 * CONV_WIDTH + Q_LORA + KV_LORA + QK_ROPE
ZC_COLS = 3 * CONV_WIDTH
ZM_COLS = Q_LORA + KV_LORA + 128
HEAD_PAD = 256
QK_COLS = MLA_HEADS * HEAD_PAD
ATTN_SCALE = (QK_NOPE + QK_ROPE) ** -0.5
LOG2_E = 1.4426950408889634
LN_2 = 0.6931471805599453
QK_FOLD = ATTN_SCALE * LOG2_E
NEG_INF = -1e30

ADAM_LR = 0.001
ADAM_B1 = 0.9
ADAM_B2 = 0.999
ADAM_EPS = 1e-08
ADAM_WD = 0.01
ADAM_STEP = 10

LANES = 128
MXU_COLS = 256
VMEM_LIMIT = 56 * 1024 * 1024
ROW_TILE = 1024
FFN_FWD_TILE = (1024, 256)
FFN_BWD_TILE = (512, 1408)
GRAD_TILE = 1408
GRAD_DEPTH = 2048
SUM_ROWS = 256
ADAM_TILE_ELEMS = 1 << 19
ATTN_TILE = 1024

NN = (((1,), (0,)), ((), ()))
NT = (((1,), (1,)), ((), ()))
TN = (((0,), (0,)), ((), ()))


def _dot(a, b, dims=NN):
    return lax.dot_general(a, b, dims, preferred_element_type=F32)


def _tile(n, cap, mult=LANES):
    best = None
    for t in range(mult, min(n, cap) + 1, mult):
        if n % t == 0:
            best = t
    return n if best is None else best


def _params(sem=None):
    return pltpu.CompilerParams(dimension_semantics=sem, vmem_limit_bytes=VMEM_LIMIT)


def _row(v):
    return pl.BlockSpec(v.shape, lambda *_: (0,) * v.ndim)


def _sigmoid(x):
    return 0.5 * jnp.tanh(0.5 * x) + 0.5


def _rms(x):
    r = lax.rsqrt(jnp.mean(x * x, axis=-1, keepdims=True) + EPS)
    return x * r, r


def _norm_mod_bwd(dh, x, gn, sc):
    xhat, r = _rms(x)
    d_sh = jnp.sum(dh, axis=0, keepdims=True)
    d_sc = jnp.sum(dh * (xhat * gn), axis=0, keepdims=True)
    dxn = dh * (1.0 + sc)
    d_gn = jnp.sum(dxn * xhat, axis=0, keepdims=True)
    dxh = dxn * gn
    dx = r * (dxh - xhat * jnp.mean(dxh * xhat, axis=-1, keepdims=True))
    return dx, d_sh, d_sc, d_gn


def _group_mean(v, gmat):
    return _dot(v.astype(BF16), gmat)


def _add_rows(ref, rows):
    for r, v in enumerate(rows):
        ref[r:r + 1, :] += v


def _window(ref, axis, j):
    return ref.at[(slice(None),) * axis + (j,)]


def _any_specs(n):
    return [pl.BlockSpec(memory_space=pl.ANY)] * n


def all_gather(blocks, axes, name):
    n_arr = len(blocks)

    def body(*refs):
        start, forward, finish = _gather_steps(refs[:n_arr], refs[n_arr:2 * n_arr], axes, *refs[2 * n_arr:])
        start()
        for j in range(3):
            forward(j)
        finish()

    return pl.pallas_call(
        body, name=name, out_shape=_gathered_shapes(blocks, axes),
        in_specs=_any_specs(n_arr), out_specs=_any_specs(n_arr), scratch_shapes=_gather_sems(n_arr),
    )(*blocks)


def all_gather_relayed(blocks, axes, name):
    n_arr = len(blocks)
    arrays = range(n_arr)

    def body(*refs):
        ins, outs = refs[:n_arr], refs[n_arr:2 * n_arr]
        send_sems, recv_sems, local_sems = refs[2 * n_arr:]
        x, y, c = lax.axis_index("x"), lax.axis_index("y"), lax.axis_index("c")
        sibling, x_nbr, y_nbr, diagonal = (x, y, 1 - c), (1 - x, y, c), (x, 1 - y, c), (1 - x, 1 - y, c)
        north = c == 1
        relay_slot = jnp.where(north, 1, 2)
        relay_from = tuple(jnp.where(north, a, b) for a, b in zip(x_nbr, y_nbr))
        relay_to = tuple(jnp.where(north, a, b) for a, b in zip(y_nbr, x_nbr))
        other_from = relay_to

        def slot(a, px, py, pc):
            return _window(outs[a], axes[a], 4 * px + 2 * py + pc)

        def copy(a, k, block, to, src=None):
            return pltpu.make_async_remote_copy(
                src_ref=slot(a, *block) if src is None else src, dst_ref=slot(a, *block),
                send_sem=send_sems.at[k, a], recv_sem=recv_sems.at[k, a], device_id=to, device_id_type=MESH_ID)

        mine = [pltpu.make_async_copy(ins[a], slot(a, x, y, c), local_sems.at[a]) for a in arrays]
        for cp in mine:
            cp.start()
        first = [copy(a, k, (x, y, c), to, src=ins[a])
                 for k, to in enumerate((sibling, x_nbr, y_nbr)) for a in arrays]
        for cp in first:
            cp.start()
        later = []
        for a in arrays:
            copy(a, relay_slot, relay_from, (x, y, c)).wait_recv()
            later += [copy(a, 3, relay_from, relay_to), copy(a, 3 + relay_slot, relay_from, sibling)]
            later[-2].start()
            later[-1].start()
        for a in arrays:
            copy(a, 3 - relay_slot, other_from, (x, y, c)).wait_recv()
            later.append(copy(a, 6 - relay_slot, other_from, sibling))
            later[-1].start()
        for a in arrays:
            copy(a, 3, diagonal, (x, y, c)).wait_recv()
            later.append(copy(a, 6, diagonal, sibling))
            later[-1].start()
        for a in arrays:
            for k, block in ((0, sibling), (4, (1 - x, y, 1 - c)), (5, (x, 1 - y, 1 - c)), (6, (1 - x, 1 - y, 1 - c))):
                copy(a, k, block, (x, y, c)).wait_recv()
        for cp in first + later:
            cp.wait_send()
        for cp in mine:
            cp.wait()

    return pl.pallas_call(
        body, name=name, out_shape=_gathered_shapes(blocks, axes),
        in_specs=_any_specs(n_arr), out_specs=_any_specs(n_arr), scratch_shapes=_gather_sems(n_arr),
    )(*blocks)


def _gathered_shapes(blocks, axes):
    return [jax.ShapeDtypeStruct(b.shape[:ax] + (N_DEV,) + b.shape[ax:], b.dtype) for b, ax in zip(blocks, axes)]


def _gather_sems(n_arr):
    return [pltpu.SemaphoreType.DMA((7, n_arr)), pltpu.SemaphoreType.DMA((7, n_arr)), pltpu.SemaphoreType.DMA((n_arr,))]


def _gather_steps(ins, outs, axes, send_sems, recv_sems, local_sems):
    arrays = range(len(ins))
    x, y, c = lax.axis_index("x"), lax.axis_index("y"), lax.axis_index("c")
    me, sibling = (x, y, c), (x, y, 1 - c)
    chips = [(1 - x, y), (x, 1 - y), (1 - x, 1 - y)]

    def slot(a, px, py, pc):
        return _window(outs[a], axes[a], 4 * px + 2 * py + pc)

    def copy(a, k, block, to, src=None):
        return pltpu.make_async_remote_copy(
            src_ref=slot(a, *block) if src is None else src, dst_ref=slot(a, *block),
            send_sem=send_sems.at[k, a], recv_sem=recv_sems.at[k, a], device_id=to, device_id_type=MESH_ID)

    def mine(a):
        return pltpu.make_async_copy(ins[a], slot(a, *me), local_sems.at[a])

    def first():
        return ([copy(a, 0, me, sibling, src=ins[a]) for a in arrays]
                + [copy(a, 1 + j, me, (*chip, c), src=ins[a]) for j, chip in enumerate(chips) for a in arrays])

    def passed(j):
        return [copy(a, 4 + j, (*chips[j], c), sibling) for a in arrays]

    def start():
        for a in arrays:
            mine(a).start()
        for cp in first():
            cp.start()

    def forward(j):
        for a, cp in zip(arrays, passed(j)):
            copy(a, 1 + j, (*chips[j], c), me).wait_recv()
            cp.start()

    def finish():
        for a in arrays:
            copy(a, 0, sibling, me).wait_recv()
        for j, chip in enumerate(chips):
            for a in arrays:
                copy(a, 4 + j, (*chip, 1 - c), me).wait_recv()
        for cp in first() + passed(0) + passed(1) + passed(2):
            cp.wait_send()
        for a in arrays:
            mine(a).wait()

    return start, forward, finish


def exchange_sibling(grads, name):
    n_arr = len(grads)

    def body(*refs):
        start, finish = _sibling_exchange_steps(refs[:n_arr], refs[n_arr:2 * n_arr], *refs[2 * n_arr:])
        start()
        finish()

    return pl.pallas_call(
        body, name=name, out_shape=_sibling_shapes(grads),
        in_specs=_any_specs(n_arr), out_specs=_any_specs(n_arr), scratch_shapes=_exchange_sems(n_arr),
    )(*grads)


def _sibling_shapes(grads):
    return [jax.ShapeDtypeStruct((4,) + g.shape[1:], g.dtype) for g in grads]


def _exchange_sems(n_arr):
    return [pltpu.SemaphoreType.DMA((n_arr,)), pltpu.SemaphoreType.DMA((n_arr,))]


def _sibling_exchange_steps(ins, outs, send_sems, recv_sems):
    x, y, c = lax.axis_index("x"), lax.axis_index("y"), lax.axis_index("c")

    def copy(a, src, dst):
        return pltpu.make_async_remote_copy(
            src_ref=src, dst_ref=dst, send_sem=send_sems.at[a], recv_sem=recv_sems.at[a],
            device_id=(x, y, 1 - c), device_id_type=MESH_ID)

    def start():
        for a in range(len(ins)):
            for k in range(4):
                copy(a, ins[a].at[2 * k + (1 - c)], outs[a].at[k]).start()

    def finish():
        whole = [copy(a, ins[a].at[pl.ds(0, 4)], outs[a]) for a in range(len(ins))]
        for cp in whole:
            cp.wait_recv()
        for cp in whole:
            cp.wait_send()

    return start, finish


def _chip_exchange_steps(ins, outs, send_sems, recv_sems):
    x, y, c = lax.axis_index("x"), lax.axis_index("y"), lax.axis_index("c")
    chips = [(1 - x, y), (x, 1 - y), (1 - x, 1 - y)]

    def copy(a, src, dst, chip):
        return pltpu.make_async_remote_copy(
            src_ref=src, dst_ref=dst, send_sem=send_sems.at[a], recv_sem=recv_sems.at[a],
            device_id=(*chip, c), device_id_type=MESH_ID)

    def start():
        for a in range(len(ins)):
            for j, chip in enumerate(chips):
                copy(a, ins[a].at[j], outs[a].at[j], chip).start()

    def finish():
        whole = [copy(a, ins[a], outs[a], chips[0]) for a in range(len(ins))]
        for cp in whole:
            cp.wait_recv()
        for cp in whole:
            cp.wait_send()

    return start, finish


def riding_gather(blocks, axes):
    def phases(ins, outs, *sems):
        start, forward, finish = _gather_steps(ins, outs, axes, *sems)
        return [start] + [functools.partial(forward, j) for j in range(3)] + [finish]

    return dict(operands=blocks, out_shape=_gathered_shapes(blocks, axes), sems=_gather_sems(len(blocks)),
                phases=phases, when=("first", "late0", "late1", "late2", "last"))


def riding_exchange(parts):
    def phases(ins, outs, *sems):
        return list(_chip_exchange_steps(ins, outs, *sems))

    return dict(operands=parts, out_shape=[jax.ShapeDtypeStruct(p.shape, p.dtype) for p in parts],
                sems=_exchange_sems(len(parts)), phases=phases, when=("first", "last"))


def riding_sibling(grads):
    def phases(ins, outs, *sems):
        return list(_sibling_exchange_steps(ins, outs, *sems))

    return dict(operands=grads, out_shape=_sibling_shapes(grads), sems=_exchange_sems(len(grads)),
                phases=phases, when=("first", "last"))


def _call_with_rider(body, rider, *, name, grid, in_specs, out_specs, out_shape, scratch_shapes, operands):
    params = _params(("arbitrary",) * len(grid))
    if rider is None:
        return pl.pallas_call(body, name=name, grid=grid, in_specs=in_specs, out_specs=out_specs,
                              out_shape=out_shape, scratch_shapes=scratch_shapes, compiler_params=params)(*operands)
    n_in, n_out, n_scr, k = len(in_specs), len(out_specs), len(scratch_shapes), len(rider["operands"])
    at = {"first": (0,) * len(grid), "last": tuple(g - 1 for g in grid)}
    if "late0" in rider["when"]:
        rows, cols = grid
        assert cols >= 3
        at.update({"late%d" % j: (max(rows - 2, 0), j) for j in range(3)})

    def wrapped(*refs):
        ins, c_in = refs[:n_in], refs[n_in:n_in + k]
        outs, c_out = refs[n_in + k:n_in + k + n_out], refs[n_in + k + n_out:n_in + 2 * k + n_out]
        scratch, sems = refs[n_in + 2 * k + n_out:n_in + 2 * k + n_out + n_scr], refs[n_in + 2 * k + n_out + n_scr:]
        pos = [pl.program_id(axis) for axis in range(len(grid))]

        def here(key):
            return functools.reduce(jnp.logical_and, [p == v for p, v in zip(pos, at[key])])

        phases = rider["phases"](c_in, c_out, *sems)
        for fn, key in zip(phases, rider["when"]):
            if key != "last":
                pl.when(here(key))(fn)
        body(*ins, *outs, *scratch)
        pl.when(here("last"))(phases[-1])

    return pl.pallas_call(
        wrapped, name=name, grid=grid,
        in_specs=list(in_specs) + _any_specs(k), out_specs=list(out_specs) + _any_specs(k),
        out_shape=list(out_shape) + rider["out_shape"], scratch_shapes=list(scratch_shapes) + rider["sems"],
        compiler_params=params)(*operands, *rider["operands"])


def add_sibling(g8, got, src_idx, chip_idx, name):
    _, r, n = g8.shape
    tr = _tile(r, SUM_ROWS, 16)

    def body(si_ref, ci_ref, g0_ref, g1_ref, g2_ref, g3_ref, got_ref, own_ref, send_ref):
        own_ref[...] = g0_ref[0] + got_ref[ci_ref[0]]
        for j, g_ref in enumerate((g1_ref, g2_ref, g3_ref)):
            send_ref[j] = (g_ref[0] + got_ref[ci_ref[j + 1]]).astype(BF16)

    def mine(j):
        return pl.BlockSpec((1, tr, n), lambda i, si, ci: (si[j], i, 0))

    return pl.pallas_call(
        body, name=name,
        out_shape=[jax.ShapeDtypeStruct((r, n), F32), jax.ShapeDtypeStruct((3, r, n), BF16)],
        grid_spec=pltpu.PrefetchScalarGridSpec(
            num_scalar_prefetch=2, grid=(r // tr,),
            in_specs=[mine(0), mine(1), mine(2), mine(3), pl.BlockSpec((4, tr, n), lambda i, si, ci: (0, i, 0))],
            out_specs=[pl.BlockSpec((tr, n), lambda i, si, ci: (i, 0)),
                       pl.BlockSpec((3, tr, n), lambda i, si, ci: (0, i, 0))]),
        compiler_params=_params(("arbitrary",)),
    )(src_idx, chip_idx, g8, g8, g8, g8, got)


def sum_devices(g):
    def body(g_ref, o_ref):
        acc = g_ref[0]
        for j in range(1, N_DEV):
            acc = acc + g_ref[j]
        o_ref[...] = acc

    return pl.pallas_call(body, name="sum_devices", out_shape=jax.ShapeDtypeStruct(g.shape[1:], F32))(g)


def sum_lanes(v):
    def body(v_ref, o_ref):
        o_ref[...] = jnp.broadcast_to(jnp.sum(v_ref[...], axis=-1, keepdims=True), (1, LANES))

    return pl.pallas_call(body, name="sum_lanes", out_shape=jax.ShapeDtypeStruct((1, LANES), F32))(v)


def ada_forward(c_all, ada_w, ada_b_cols):
    nb, n = c_all.shape[0], ada_w.shape[1]

    def body(c_ref, w_ref, b_ref, o_ref):
        cv = c_ref[...]
        s = (cv * jax.nn.sigmoid(cv)).astype(BF16)
        o_ref[...] = _dot(s, w_ref[...].astype(BF16)) + b_ref[...]

    return pl.pallas_call(body, name="ada_fwd", out_shape=jax.ShapeDtypeStruct((nb, n), F32),
                          compiler_params=_params())(c_all, ada_w, ada_b_cols)


def ada_backward(c_all16, dmod16):
    d, n = c_all16.shape[1], dmod16.shape[1]

    def body(c_ref, g_ref, o_ref):
        cv = c_ref[...]
        s = (cv * jax.nn.sigmoid(cv)).astype(BF16)
        o_ref[...] = _dot(s, g_ref[...].astype(BF16), TN)

    return pl.pallas_call(body, name="ada_bwd", out_shape=jax.ShapeDtypeStruct((d, n), F32),
                          compiler_params=_params())(c_all16, dmod16)


def ffn_forward(x, gn, sc, sh, gate, ws, name, rider=None, loss_head=None):
    t, d = x.shape
    f = ws[0].shape[0]
    tm, tf = _tile(t, FFN_FWD_TILE[0], 16), _tile(f, FFN_FWD_TILE[1])
    nf = f // tf
    n_in = 5 if loss_head is None else 7

    def body(*refs):
        x_ref, gn_ref, sc_ref, sh_ref, gate_ref = refs[:5]
        w1_ref, w3_ref, w2_ref, xo_ref, h_ref, a_ref, b_ref, y_ref = refs[n_in:n_in + 8]
        hs, acc = refs[-2:]
        i, j = pl.program_id(0), pl.program_id(1)

        if loss_head is not None:
            @pl.when(jnp.logical_and(i == 0, j == 0))
            def _():
                refs[n_in + 9][...] = jnp.zeros_like(refs[n_in + 9])

        @pl.when(j == 0)
        def _():
            xhat, _ = _rms(x_ref[...])
            h = (xhat * gn_ref[...] * (1.0 + sc_ref[...]) + sh_ref[...]).astype(BF16)
            hs[...] = h
            h_ref[...] = h
            acc[...] = jnp.zeros_like(acc)

        h = hs[...]
        a = _dot(h, w1_ref[...], NT)
        b = _dot(h, w3_ref[...], NT)
        a_ref[...] = a.astype(BF16)
        b_ref[...] = b.astype(BF16)
        u = (a * _sigmoid(a) * b).astype(BF16)
        acc[...] += _dot(u, w2_ref[...])

        @pl.when(j == nf - 1)
        def _():
            y = acc[...]
            y_ref[...] = y.astype(BF16)
            x_out = x_ref[...] + 0.5 * gate_ref[...] * y
            if loss_head is None:
                xo_ref[...] = x_out
            else:
                dx, d_g, loss = _loss_head(x_out, refs[5][...], refs[6][...])
                xo_ref[...] = dx
                refs[n_in + 8][...] = (0.5 * gate_ref[...] * dx).astype(BF16)
                _add_rows(refs[n_in + 9], [d_g, loss])

    row = pl.BlockSpec((tm, d), lambda i, j: (i, 0))
    vec = pl.BlockSpec((1, d), lambda i, j: (0, 0))
    wide = pl.BlockSpec((tm, tf), lambda i, j: (i, j))
    head = loss_head is not None
    return _call_with_rider(
        body, rider, name=name, grid=(t // tm, nf),
        in_specs=[row, vec, vec, vec, vec] + ([row, vec] if head else [])
        + [pl.BlockSpec((tf, d), lambda i, j: (j, 0))] * 3,
        out_specs=[row, row, wide, wide, row] + ([row, pl.BlockSpec((8, d), lambda i, j: (0, 0))] if head else []),
        out_shape=[jax.ShapeDtypeStruct((t, d), F32), jax.ShapeDtypeStruct((t, d), BF16),
                   jax.ShapeDtypeStruct((t, f), BF16), jax.ShapeDtypeStruct((t, f), BF16),
                   jax.ShapeDtypeStruct((t, d), BF16)]
        + ([jax.ShapeDtypeStruct((t, d), BF16), jax.ShapeDtypeStruct((8, d), F32)] if head else []),
        scratch_shapes=[pltpu.VMEM((tm, d), BF16), pltpu.VMEM((tm, d), F32)],
        operands=(x, gn, sc, sh, gate) + (tuple(loss_head) if head else ()) + tuple(ws))


def _loss_head(x, target, g):
    d = x.shape[-1]
    xhat, r = _rms(x)
    err = xhat * g - target
    dyf = err * (1.0 / d)
    dxh = dyf * g
    dx = r * (dxh - xhat * jnp.mean(dxh * xhat, axis=-1, keepdims=True))
    return dx, jnp.sum(dyf * xhat, axis=0, keepdims=True), jnp.sum(err * err, axis=0, keepdims=True) * (0.5 / d)


def ffn_backward_gate(dy, a, b, w2, name, rider=None):
    t, d = dy.shape
    f = w2.shape[0]
    tm, tf = _tile(t, FFN_BWD_TILE[0], 16), _tile(f, FFN_BWD_TILE[1])
    nf = f // tf

    def gate_body(dy_ref, a_ref, b_ref, w2_ref, da_ref, db_ref, gw2_ref):
        dy_v = dy_ref[...]
        du = _dot(dy_v, w2_ref[...], NT)
        av = a_ref[...].astype(F32)
        bv = b_ref[...].astype(F32)
        s = _sigmoid(av)
        sa = av * s
        da_ref[...] = (du * bv * (s + sa * (1.0 - s))).astype(BF16)
        db_ref[...] = (du * sa).astype(BF16)
        part = _dot((sa * bv).astype(BF16), dy_v, TN)

        @pl.when(pl.program_id(1) == 0)
        def _():
            gw2_ref[...] = part

        @pl.when(pl.program_id(1) > 0)
        def _():
            gw2_ref[...] += part

    hidden = jax.ShapeDtypeStruct((t, f), BF16)
    wide_t = pl.BlockSpec((tm, tf), lambda j, i: (i, j))
    return _call_with_rider(
        gate_body, rider, name=name, grid=(nf, t // tm),
        in_specs=[pl.BlockSpec((tm, d), lambda j, i: (i, 0)), wide_t, wide_t,
                  pl.BlockSpec((tf, d), lambda j, i: (j, 0))],
        out_specs=[wide_t, wide_t, pl.BlockSpec((tf, d), lambda j, i: (j, 0))],
        out_shape=[hidden, hidden, jax.ShapeDtypeStruct((f, d), F32)],
        scratch_shapes=[], operands=(dy, a, b, w2))


def ffn_backward_norm(da, db, dxo, x, y, gn, sc, w1t, w3t, name, rider=None):
    t, d = x.shape
    f = w1t.shape[0]
    tm, tf = _tile(t, FFN_BWD_TILE[0], 16), _tile(f, FFN_BWD_TILE[1])
    nf = f // tf
    row = pl.BlockSpec((tm, d), lambda i, j: (i, 0))
    vec = pl.BlockSpec((1, d), lambda i, j: (0, 0))
    wide = pl.BlockSpec((tm, tf), lambda i, j: (i, j))

    def norm_body(da_ref, db_ref, w1_ref, w3_ref, dxo_ref, x_ref, y_ref, gn_ref, sc_ref, dx_ref, sums_ref, acc):
        i, j = pl.program_id(0), pl.program_id(1)

        @pl.when(jnp.logical_and(i == 0, j == 0))
        def _():
            sums_ref[...] = jnp.zeros_like(sums_ref)

        part = _dot(da_ref[...], w1_ref[...]) + _dot(db_ref[...], w3_ref[...])

        @pl.when(j == 0)
        def _():
            acc[...] = part

        @pl.when(jnp.logical_and(j > 0, j < nf - 1))
        def _():
            acc[...] += part

        @pl.when(j == nf - 1)
        def _():
            dh = part if nf == 1 else acc[...] + part
            dxo_v = dxo_ref[...]
            dx, d_sh, d_sc, d_gn = _norm_mod_bwd(dh, x_ref[...], gn_ref[...], sc_ref[...])
            dx_ref[...] = dxo_v + dx
            d_gate = jnp.sum(dxo_v * (0.5 * y_ref[...].astype(F32)), axis=0, keepdims=True)
            _add_rows(sums_ref, [d_sh, d_sc, d_gate, d_gn])

    w_spec = pl.BlockSpec((tf, d), lambda i, j: (j, 0))
    return _call_with_rider(
        norm_body, rider, name=name, grid=(t // tm, nf),
        in_specs=[wide, wide, w_spec, w_spec, row, row, row, vec, vec],
        out_specs=[row, pl.BlockSpec((8, d), lambda i, j: (0, 0))],
        out_shape=[jax.ShapeDtypeStruct((t, d), F32), jax.ShapeDtypeStruct((8, d), F32)],
        scratch_shapes=[pltpu.VMEM((tm, d), F32)],
        operands=(da, db, w1t, w3t, dxo, x, y, gn, sc))


def matmul_tn(a, b, name, rider=None):
    parts = list(a) if isinstance(a, (list, tuple)) else [a]
    t, n = b.shape
    widths = [p.shape[1] for p in parts]
    tm = _tile(functools.reduce(math.gcd, widths), GRAD_TILE)
    tn, tk = _tile(n, GRAD_TILE), _tile(t, GRAD_DEPTH, 16)
    nk = t // tk
    counts = [w // tm for w in widths]
    firsts = [sum(counts[:p]) for p in range(len(parts))]

    def body(*refs):
        a_refs, (b_ref, o_ref, acc) = refs[:len(parts)], refs[len(parts):]
        i, k = pl.program_id(0), pl.program_id(2)

        @pl.when(k == 0)
        def _():
            acc[...] = jnp.zeros_like(acc)

        for a_ref, lo, cnt in zip(a_refs, firsts, counts):
            def accumulate(a_ref=a_ref):
                acc[...] += _dot(a_ref[...], b_ref[...], TN)

            if len(parts) == 1:
                accumulate()
            else:
                pl.when(jnp.logical_and(i >= lo, i < lo + cnt))(accumulate)

        @pl.when(k == nk - 1)
        def _():
            o_ref[...] = acc[...]

    def part_spec(lo, cnt):
        if len(parts) == 1:
            return pl.BlockSpec((tk, tm), lambda i, j, k: (k, i))

        def index(i, j, k):
            mine = jnp.logical_and(i >= lo, i < lo + cnt)
            return jnp.where(mine, k, 0), jnp.clip(i - lo, 0, cnt - 1)
        return pl.BlockSpec((tk, tm), index)

    out = _call_with_rider(
        body, rider, name=name, grid=(sum(counts), n // tn, nk),
        in_specs=[part_spec(lo, cnt) for lo, cnt in zip(firsts, counts)]
        + [pl.BlockSpec((tk, tn), lambda i, j, k: (k, j))],
        out_specs=[pl.BlockSpec((tm, tn), lambda i, j, k: (i, j))],
        out_shape=[jax.ShapeDtypeStruct((sum(widths), n), F32)],
        scratch_shapes=[pltpu.VMEM((tm, tn), F32)], operands=(*parts, b))
    return out[0] if rider is None else out


def mix_in_forward(x, gn, sc, sh, w_in):
    t, d = x.shape
    tm = _tile(t, ROW_TILE, 16)

    def body(x_ref, gn_ref, sc_ref, sh_ref, w_ref, h_ref, zc_ref, zm_ref):
        xhat, _ = _rms(x_ref[...])
        h = (xhat * gn_ref[...] * (1.0 + sc_ref[...]) + sh_ref[...]).astype(BF16)
        h_ref[...] = h
        z = _dot(h, w_ref[...], NT)
        zc_ref[...] = z[:, :ZC_COLS].astype(BF16)
        zm_ref[...] = z[:, ZC_COLS:].astype(BF16)

    row = pl.BlockSpec((tm, d), lambda i: (i, 0))
    vec = pl.BlockSpec((1, d), lambda i: (0, 0))
    return pl.pallas_call(
        body, name="mix_in_fwd", grid=(t // tm,),
        in_specs=[row, vec, vec, vec, _row(w_in)],
        out_specs=[row, pl.BlockSpec((tm, ZC_COLS), lambda i: (i, 0)), pl.BlockSpec((tm, ZM_COLS), lambda i: (i, 0))],
        out_shape=[jax.ShapeDtypeStruct((t, d), BF16), jax.ShapeDtypeStruct((t, ZC_COLS), BF16),
                   jax.ShapeDtypeStruct((t, ZM_COLS), BF16)],
        compiler_params=_params(("arbitrary",)),
    )(x, gn, sc, sh, w_in)


def _rope_tables(pos, inv_freq):
    ang = pos * inv_freq
    lane = lax.broadcasted_iota(jnp.int32, ang.shape, 1)
    cos, sin = jnp.cos(ang), jnp.sin(ang)
    half = QK_ROPE // 2
    return cos, jnp.where(lane < half, -sin, 0.0), jnp.where(jnp.logical_and(lane >= half, lane < QK_ROPE), sin, 0.0)


def _rope(v, tables):
    cos, sin_a, sin_b = tables
    return v * cos + pltpu.roll(v, LANES - QK_ROPE // 2, 1) * sin_a + pltpu.roll(v, QK_ROPE // 2, 1) * sin_b


def _rope_transposed(dv, tables):
    cos, sin_a, sin_b = tables
    return dv * cos + pltpu.roll(dv * sin_a, QK_ROPE // 2, 1) + pltpu.roll(dv * sin_b, LANES - QK_ROPE // 2, 1)


def mla_project(zm, pos, inv_freq, qg, kvg, w_uq, w_ukv):
    t = zm.shape[0]
    tm = _tile(t, ROW_TILE, 16)

    def body(zm_ref, pos_ref, if_ref, qg_ref, kvg_ref, wq_ref, wkv_ref, qn_ref, kvn_ref, q_ref, k_ref, v_ref):
        zv = zm_ref[...].astype(F32)
        qn = (_rms(zv[:, :Q_LORA])[0] * qg_ref[...]).astype(BF16)
        kvn = (_rms(zv[:, Q_LORA:Q_LORA + KV_LORA])[0] * kvg_ref[...]).astype(BF16)
        qn_ref[...] = qn
        kvn_ref[...] = kvn
        qf = _dot(qn, wq_ref[...], NT) * QK_FOLD
        kvf = _dot(kvn, wkv_ref[...], NT)
        tables = _rope_tables(pos_ref[...], if_ref[...])
        kr = _rope(zv[:, Q_LORA + KV_LORA:], tables).astype(BF16)
        for h in range(MLA_HEADS):
            lo = h * HEAD_PAD
            q_ref[:, lo:lo + QK_NOPE] = qf[:, lo:lo + QK_NOPE].astype(BF16)
            q_ref[:, lo + QK_NOPE:lo + HEAD_PAD] = _rope(qf[:, lo + QK_NOPE:lo + HEAD_PAD], tables).astype(BF16)
            k_ref[:, lo:lo + QK_NOPE] = kvf[:, h * QK_NOPE:(h + 1) * QK_NOPE].astype(BF16)
            k_ref[:, lo + QK_NOPE:lo + HEAD_PAD] = kr
        v_ref[...] = kvf[:, MLA_HEADS * QK_NOPE:].astype(BF16)

    def rows(n):
        return pl.BlockSpec((tm, n), lambda i: (i, 0))

    return pl.pallas_call(
        body, name="mla_project", grid=(t // tm,),
        in_specs=[rows(ZM_COLS), rows(1), _row(inv_freq), _row(qg), _row(kvg), _row(w_uq), _row(w_ukv)],
        out_specs=[rows(Q_LORA), rows(KV_LORA), rows(QK_COLS), rows(QK_COLS), rows(MLA_WIDTH)],
        out_shape=[jax.ShapeDtypeStruct((t, Q_LORA), BF16), jax.ShapeDtypeStruct((t, KV_LORA), BF16),
                   jax.ShapeDtypeStruct((t, QK_COLS), BF16), jax.ShapeDtypeStruct((t, QK_COLS), BF16),
                   jax.ShapeDtypeStruct((t, MLA_WIDTH), BF16)],
        compiler_params=_params(("arbitrary",)),
    )(zm, pos, inv_freq, qg, kvg, w_uq, w_ukv)


def _chunk_mask(shape, q_axis):
    qi = lax.broadcasted_iota(jnp.int32, shape, q_axis) // CHUNK
    ki = lax.broadcasted_iota(jnp.int32, shape, 1 - q_axis) // CHUNK
    return ki <= qi


def attention_forward(q, k, v, rider=None):
    t = q.shape[0]
    tq = _tile(t, ATTN_TILE, CHUNK)

    def body(q_ref, k_ref, v_ref, o_ref, lse_ref):
        i = pl.program_id(1)
        qv = q_ref[...]

        def step(kb, carry, masked, tiles=1):
            m, l, acc = carry
            keys = pl.ds(pl.multiple_of(kb * tq, tq), tiles * tq)
            s = _dot(qv, k_ref[keys, :], NT)
            if masked:
                s = jnp.where(_chunk_mask(s.shape, 0), s, NEG_INF)
            m_new = jnp.maximum(m, jnp.max(s, axis=-1, keepdims=True))
            alpha = jnp.exp2(m - m_new)
            p = jnp.exp2(s - m_new)
            l = alpha * l + jnp.sum(p, axis=-1, keepdims=True)
            acc = alpha * acc + _dot(p.astype(BF16), v_ref[keys, :])
            return m_new, l, acc

        init = (jnp.full((tq, 1), NEG_INF, F32), jnp.zeros((tq, 1), F32), jnp.zeros((tq, V_HEAD), F32))
        carry = lax.fori_loop(0, i // 2, lambda pb, cr: step(2 * pb, cr, False, 2), init)
        carry = lax.fori_loop(0, i % 2, lambda _, cr: step(i - 1, cr, False), carry)
        m, l, acc = step(i, carry, True)
        o_ref[...] = (acc / l).astype(BF16)
        lse_ref[0] = m + jnp.log2(l)

    return _call_with_rider(
        body, rider, name="attn_fwd", grid=(MLA_HEADS, t // tq),
        in_specs=[pl.BlockSpec((tq, HEAD_PAD), lambda h, i: (i, h)),
                  pl.BlockSpec((t, HEAD_PAD), lambda h, i: (0, h)),
                  pl.BlockSpec((t, V_HEAD), lambda h, i: (0, h))],
        out_specs=[pl.BlockSpec((tq, V_HEAD), lambda h, i: (i, h)),
                   pl.BlockSpec((1, tq, 1), lambda h, i: (h, i, 0))],
        out_shape=[jax.ShapeDtypeStruct((t, MLA_WIDTH), BF16), jax.ShapeDtypeStruct((MLA_HEADS, t, 1), F32)],
        scratch_shapes=[], operands=(q, k, v))


def attention_backward(q, k, v, do, lse, delta, rider=None):
    t = q.shape[0]
    tq = _tile(t, ATTN_TILE, CHUNK)
    nq = t // tq

    def body(q_ref, k_ref, v_ref, do_ref, lse_ref, delta_ref, dq_ref, dk_ref, dv_ref, dq_acc):
        kb = pl.program_id(1)

        @pl.when(kb == 0)
        def _():
            dq_acc[...] = jnp.zeros_like(dq_acc)

        kv, vv = k_ref[...], v_ref[...]

        def step(qb, carry, masked):
            dk, dv = carry
            rows = pl.ds(pl.multiple_of(qb * tq, tq), tq)
            qv, dov = q_ref[rows, :], do_ref[rows, :]
            s = _dot(kv, qv, NT)
            if masked:
                s = jnp.where(_chunk_mask(s.shape, 1), s, NEG_INF)
            p = jnp.exp2(s - lse_ref[0, qb])
            dv = dv + _dot(p.astype(BF16), dov)
            dp = _dot(vv, dov, NT)
            ds = (p * (dp - delta_ref[0, qb]) * LN_2).astype(BF16)
            dk = dk + _dot(ds, qv)
            dq_acc[rows, :] += _dot(ds, kv, TN)
            return dk, dv

        carry = step(kb, (jnp.zeros((tq, HEAD_PAD), F32), jnp.zeros((tq, V_HEAD), F32)), True)
        odd = (nq - 1 - kb) % 2
        carry = lax.fori_loop(0, odd, lambda _, cr: step(kb + 1, cr, False), carry)
        first = kb + 1 + odd
        dk, dv = lax.fori_loop(0, (nq - first) // 2,
                               lambda pb, cr: step(first + 2 * pb + 1, step(first + 2 * pb, cr, False), False), carry)
        dk_ref[...] = dk.astype(BF16)
        dv_ref[...] = dv.astype(BF16)

        @pl.when(kb == nq - 1)
        def _():
            dq_ref[...] = dq_acc[...].astype(BF16)

    stat = pl.BlockSpec((1, nq, 1, tq), lambda h, j: (h, 0, 0, 0))
    return _call_with_rider(
        body, rider, name="attn_bwd", grid=(MLA_HEADS, nq),
        in_specs=[pl.BlockSpec((t, HEAD_PAD), lambda h, j: (0, h)),
                  pl.BlockSpec((tq, HEAD_PAD), lambda h, j: (j, h)),
                  pl.BlockSpec((tq, V_HEAD), lambda h, j: (j, h)),
                  pl.BlockSpec((t, V_HEAD), lambda h, j: (0, h)), stat, stat],
        out_specs=[pl.BlockSpec((t, HEAD_PAD), lambda h, j: (0, h)),
                   pl.BlockSpec((tq, HEAD_PAD), lambda h, j: (j, h)),
                   pl.BlockSpec((tq, V_HEAD), lambda h, j: (j, h))],
        out_shape=[jax.ShapeDtypeStruct((t, QK_COLS), BF16), jax.ShapeDtypeStruct((t, QK_COLS), BF16),
                   jax.ShapeDtypeStruct((t, MLA_WIDTH), BF16)],
        scratch_shapes=[pltpu.VMEM((t, HEAD_PAD), F32)], operands=(q, k, v, do, lse, delta))


HALO = 16


def _halo_spec(tm, n, step, last):
    return pl.BlockSpec((HALO, n), lambda i: (jnp.clip(i * (tm // HALO) + step, 0, last), 0))


def _shift_rows(v, prev, n):
    out = pltpu.roll(v, n, 0)
    row = lax.broadcasted_iota(jnp.int32, v.shape, 0)
    for r in range(n):
        out = jnp.where(row == r, prev[HALO - n + r:HALO - n + r + 1, :], out)
    return out


def _advance_rows(v, nxt, n):
    rows = v.shape[0]
    out = pltpu.roll(v, rows - n, 0)
    row = lax.broadcasted_iota(jnp.int32, v.shape, 0)
    for r in range(n):
        out = jnp.where(row == rows - n + r, nxt[r:r + 1, :], out)
    return out


def _conv_taps(zc, zc_prev, first):
    w = CONV_WIDTH
    u = zc[:, w:2 * w] * zc[:, 2 * w:]
    up = jnp.where(first, 0.0, zc_prev[:, w:2 * w] * zc_prev[:, 2 * w:])
    return u, _shift_rows(u, up, 1), _shift_rows(u, up, 2)


def mix_out_forward(zc, o, conv_w, og, gmat_a, gmat_b, w_out, x, gate):
    t, d = x.shape
    tm = _tile(t, ROW_TILE, 16)
    w = CONV_WIDTH

    def body(zc_ref, zp_ref, o_ref, cw_ref, og_ref, ga_ref, gb_ref, w_ref, x_ref, gate_ref,
             xo_ref, yn_ref, y_ref, ya_ref):
        zc_v = zc_ref[...].astype(F32)
        u, u1, u2 = _conv_taps(zc_v, zp_ref[...].astype(F32), pl.program_id(0) == 0)
        cw = cw_ref[...]
        ya = zc_v[:, :w] * (cw[0:1] * u2 + cw[1:2] * u1 + cw[2:3] * u)
        ya_ref[...] = ya.astype(BF16)
        ov = o_ref[...].astype(F32)
        ogv = og_ref[...]
        yn_ref[:, :w] = (ya * lax.rsqrt(_group_mean(ya * ya, ga_ref[...]) + EPS) * ogv[:, :w]).astype(BF16)
        yn_ref[:, w:] = (ov * lax.rsqrt(_group_mean(ov * ov, gb_ref[...]) + EPS) * ogv[:, w:]).astype(BF16)
        y = _dot(yn_ref[...], w_ref[...])
        y_ref[...] = y.astype(BF16)
        xo_ref[...] = x_ref[...] + gate_ref[...] * y

    def rows(n):
        return pl.BlockSpec((tm, n), lambda i: (i, 0))

    return pl.pallas_call(
        body, name="mix_out_fwd", grid=(t // tm,),
        in_specs=[rows(ZC_COLS), _halo_spec(tm, ZC_COLS, -1, t // HALO - 1), rows(MLA_WIDTH), _row(conv_w), _row(og),
                  _row(gmat_a), _row(gmat_b), _row(w_out), rows(d), _row(gate)],
        out_specs=[rows(d), rows(MIX_WIDTH), rows(d), rows(w)],
        out_shape=[jax.ShapeDtypeStruct((t, d), F32), jax.ShapeDtypeStruct((t, MIX_WIDTH), BF16),
                   jax.ShapeDtypeStruct((t, d), BF16), jax.ShapeDtypeStruct((t, w), BF16)],
        compiler_params=_params(("arbitrary",)),
    )(zc, zc, o, conv_w, og, gmat_a, gmat_b, w_out, x, gate)


def _group_norm_bwd(dyn, y, og, gmat):
    rs = lax.rsqrt(_group_mean(y * y, gmat) + EPS)
    yhat = y * rs
    d_og = jnp.sum(dyn * yhat, axis=0, keepdims=True)
    dyh = dyn * og
    return rs * (dyh - yhat * _group_mean(dyh * yhat, gmat)), d_og


def mix_out_backward(dxo, y, gate, ya, o, og, gmat_a, gmat_b, w_out, rider=None):
    t, d = dxo.shape
    tm = _tile(t, ROW_TILE, 16)
    w = CONV_WIDTH

    def body(dxo_ref, y_ref, gate_ref, ya_ref, o_ref, og_ref, ga_ref, gb_ref, w_ref,
             dy_ref, dya_ref, do_ref, delta_ref, sd_ref, so_ref):
        @pl.when(pl.program_id(0) == 0)
        def _():
            sd_ref[...] = jnp.zeros_like(sd_ref)
            so_ref[...] = jnp.zeros_like(so_ref)

        dxo_v = dxo_ref[...]
        dy = (gate_ref[...] * dxo_v).astype(BF16)
        dy_ref[...] = dy
        sd_ref[0:1, :] += jnp.sum(dxo_v * y_ref[...].astype(F32), axis=0, keepdims=True)
        dyn = _dot(dy, w_ref[...], NT)
        ogv = og_ref[...]
        ov = o_ref[...].astype(F32)
        dya, d_og_a = _group_norm_bwd(dyn[:, :w], ya_ref[...].astype(F32), ogv[:, :w], ga_ref[...])
        dov, d_og_b = _group_norm_bwd(dyn[:, w:], ov, ogv[:, w:], gb_ref[...])
        dya_ref[...] = dya.astype(BF16)
        do_ref[...] = dov.astype(BF16)
        so_ref[0:1, :w] += d_og_a
        so_ref[0:1, w:] += d_og_b
        prod = dov * ov
        for h in range(MLA_HEADS):
            delta_ref[h] = jnp.sum(prod[:, h * V_HEAD:(h + 1) * V_HEAD], axis=-1, keepdims=True)

    def rows(n):
        return pl.BlockSpec((tm, n), lambda i: (i, 0))

    return _call_with_rider(
        body, rider, name="mix_out_bwd", grid=(t // tm,),
        in_specs=[rows(d), rows(d), _row(gate), rows(w), rows(MLA_WIDTH), _row(og), _row(gmat_a), _row(gmat_b),
                  _row(w_out)],
        out_specs=[rows(d), rows(w), rows(MLA_WIDTH), pl.BlockSpec((MLA_HEADS, tm, 1), lambda i: (0, i, 0)),
                   pl.BlockSpec((8, d), lambda i: (0, 0)), pl.BlockSpec((8, MIX_WIDTH), lambda i: (0, 0))],
        out_shape=[jax.ShapeDtypeStruct((t, d), BF16), jax.ShapeDtypeStruct((t, w), BF16),
                   jax.ShapeDtypeStruct((t, MLA_WIDTH), BF16), jax.ShapeDtypeStruct((MLA_HEADS, t, 1), F32),
                   jax.ShapeDtypeStruct((8, d), F32), jax.ShapeDtypeStruct((8, MIX_WIDTH), F32)],
        scratch_shapes=[], operands=(dxo, y, gate, ya, o, og, gmat_a, gmat_b, w_out))


def conv_backward(zc, dya, conv_w):
    t = zc.shape[0]
    tm = _tile(t, ROW_TILE, 16)
    nt = t // tm
    w = CONV_WIDTH

    def body(zc_ref, zp_ref, zn_ref, dya_ref, dn_ref, cw_ref, dzc_ref, sums_ref):
        i = pl.program_id(0)

        @pl.when(i == 0)
        def _():
            sums_ref[...] = jnp.zeros_like(sums_ref)

        zc_v = zc_ref[...].astype(F32)
        u, u1, u2 = _conv_taps(zc_v, zp_ref[...].astype(F32), i == 0)
        cw = cw_ref[...]
        dya_v = dya_ref[...].astype(F32)
        dyc = dya_v * zc_v[:, :w]
        dyc_next = jnp.where(i == nt - 1, 0.0, dn_ref[...].astype(F32) * zn_ref[:, :w].astype(F32))
        du = cw[2:3] * dyc + cw[1:2] * _advance_rows(dyc, dyc_next, 1) + cw[0:1] * _advance_rows(dyc, dyc_next, 2)
        dzc_ref[:, :w] = (dya_v * (cw[0:1] * u2 + cw[1:2] * u1 + cw[2:3] * u)).astype(BF16)
        dzc_ref[:, w:2 * w] = (du * zc_v[:, 2 * w:]).astype(BF16)
        dzc_ref[:, 2 * w:] = (du * zc_v[:, w:2 * w]).astype(BF16)
        _add_rows(sums_ref, [jnp.sum(dyc * tap, axis=0, keepdims=True) for tap in (u2, u1, u)])

    def rows(n):
        return pl.BlockSpec((tm, n), lambda i: (i, 0))

    def halo(n, step):
        return _halo_spec(tm, n, step, t // HALO - 1)

    return pl.pallas_call(
        body, name="conv_bwd", grid=(nt,),
        in_specs=[rows(ZC_COLS), halo(ZC_COLS, -1), halo(ZC_COLS, tm // HALO), rows(w), halo(w, tm // HALO),
                  _row(conv_w)],
        out_specs=[rows(ZC_COLS), pl.BlockSpec((8, w), lambda i: (0, 0))],
        out_shape=[jax.ShapeDtypeStruct((t, ZC_COLS), BF16), jax.ShapeDtypeStruct((8, w), F32)],
        compiler_params=_params(("arbitrary",)),
    )(zc, zc, zc, dya, dya, conv_w)


def _rms_bwd(dy, x, g):
    xhat, r = _rms(x)
    d_g = jnp.sum(dy * xhat, axis=0, keepdims=True)
    dxh = dy * g
    return r * (dxh - xhat * jnp.mean(dxh * xhat, axis=-1, keepdims=True)), d_g


def mla_project_backward(dq, dk, dv, zm, pos, inv_freq, qg, kvg, w_uq, w_ukv):
    t = zm.shape[0]
    tm = _tile(t, ROW_TILE, 16)

    def body(dq_ref, dk_ref, dv_ref, zm_ref, pos_ref, if_ref, qg_ref, kvg_ref, wq_ref, wkv_ref,
             dql_ref, dkvl_ref, dzm_ref, sums_ref):
        @pl.when(pl.program_id(0) == 0)
        def _():
            sums_ref[...] = jnp.zeros_like(sums_ref)

        tables = _rope_tables(pos_ref[...], if_ref[...])
        dkr = jnp.zeros((tm, LANES), F32)
        for h in range(MLA_HEADS):
            lo = h * HEAD_PAD
            dql_ref[:, lo:lo + QK_NOPE] = (dq_ref[:, lo:lo + QK_NOPE].astype(F32) * QK_FOLD).astype(BF16)
            dql_ref[:, lo + QK_NOPE:lo + HEAD_PAD] = _rope_transposed(
                dq_ref[:, lo + QK_NOPE:lo + HEAD_PAD].astype(F32) * QK_FOLD, tables).astype(BF16)
            dkvl_ref[:, h * QK_NOPE:(h + 1) * QK_NOPE] = dk_ref[:, lo:lo + QK_NOPE]
            dkr = dkr + dk_ref[:, lo + QK_NOPE:lo + HEAD_PAD].astype(F32)
        dkvl_ref[:, MLA_HEADS * QK_NOPE:] = dv_ref[...]
        zv = zm_ref[...].astype(F32)
        dqn = _dot(dql_ref[...], wq_ref[...])
        dkvn = _dot(dkvl_ref[...], wkv_ref[...])
        dcq, d_qg = _rms_bwd(dqn, zv[:, :Q_LORA], qg_ref[...])
        dckv, d_kvg = _rms_bwd(dkvn, zv[:, Q_LORA:Q_LORA + KV_LORA], kvg_ref[...])
        dzm_ref[:, :Q_LORA] = dcq.astype(BF16)
        dzm_ref[:, Q_LORA:Q_LORA + KV_LORA] = dckv.astype(BF16)
        dzm_ref[:, Q_LORA + KV_LORA:] = _rope_transposed(dkr, tables).astype(BF16)
        sums_ref[0:1, :Q_LORA] += d_qg
        sums_ref[0:1, Q_LORA:Q_LORA + KV_LORA] += d_kvg

    def rows(n):
        return pl.BlockSpec((tm, n), lambda i: (i, 0))

    return pl.pallas_call(
        body, name="mla_project_bwd", grid=(t // tm,),
        in_specs=[rows(QK_COLS), rows(QK_COLS), rows(MLA_WIDTH), rows(ZM_COLS), rows(1), _row(inv_freq),
                  _row(qg), _row(kvg), _row(w_uq), _row(w_ukv)],
        out_specs=[rows(QK_COLS), rows(QK_COLS), rows(ZM_COLS), pl.BlockSpec((8, ZM_COLS), lambda i: (0, 0))],
        out_shape=[jax.ShapeDtypeStruct((t, QK_COLS), BF16), jax.ShapeDtypeStruct((t, QK_COLS), BF16),
                   jax.ShapeDtypeStruct((t, ZM_COLS), BF16), jax.ShapeDtypeStruct((8, ZM_COLS), F32)],
        compiler_params=_params(("arbitrary",)),
    )(dq, dk, dv, zm, pos, inv_freq, qg, kvg, w_uq, w_ukv)


def mix_in_backward(dzc, dzm, w_in, x, dxo, gn, sc, gate, rider=None):
    t, d = x.shape
    tm = _tile(t, ROW_TILE, 16)

    def body(dzc_ref, dzm_ref, w_ref, x_ref, dxo_ref, gn_ref, sc_ref, gate_ref, dx_ref, dy_ref, sums_ref):
        @pl.when(pl.program_id(0) == 0)
        def _():
            sums_ref[...] = jnp.zeros_like(sums_ref)

        dh = _dot(dzc_ref[...], w_ref[:ZC_COLS, :]) + _dot(dzm_ref[...], w_ref[ZC_COLS:, :])
        dx, d_sh, d_sc, d_gn = _norm_mod_bwd(dh, x_ref[...], gn_ref[...], sc_ref[...])
        dx = dxo_ref[...] + dx
        dx_ref[...] = dx
        dy_ref[...] = (0.5 * gate_ref[...] * dx).astype(BF16)
        _add_rows(sums_ref, [d_sh, d_sc, d_gn])

    def rows(n):
        return pl.BlockSpec((tm, n), lambda i: (i, 0))

    return _call_with_rider(
        body, rider, name="mix_in_bwd", grid=(t // tm,),
        in_specs=[rows(ZC_COLS), rows(ZM_COLS), _row(w_in), rows(d), rows(d), _row(gn), _row(sc), _row(gate)],
        out_specs=[rows(d), rows(d), pl.BlockSpec((8, d), lambda i: (0, 0))],
        out_shape=[jax.ShapeDtypeStruct((t, d), F32), jax.ShapeDtypeStruct((t, d), BF16),
                   jax.ShapeDtypeStruct((8, d), F32)],
        scratch_shapes=[], operands=(dzc, dzm, w_in, x, dxo, gn, sc, gate))


def _adamw_step(w, g, m, v):
    m_new = ADAM_B1 * m + (1.0 - ADAM_B1) * g
    v_new = ADAM_B2 * v + (1.0 - ADAM_B2) * (g * g)
    m_hat = m_new / (1.0 - ADAM_B1 ** ADAM_STEP)
    v_hat = v_new / (1.0 - ADAM_B2 ** ADAM_STEP)
    return -ADAM_LR * (m_hat / (jnp.sqrt(v_hat) + ADAM_EPS) + ADAM_WD * w), m_new, v_new


def adamw(w, g, m, v, name):
    r, n = w.shape
    tr = _tile(r, max(8, ADAM_TILE_ELEMS // n), 8)

    def body(w_ref, g_ref, m_ref, v_ref, d_ref, mo_ref, vo_ref):
        d_ref[...], mo_ref[...], vo_ref[...] = _adamw_step(w_ref[...], g_ref[...], m_ref[...], v_ref[...])

    blk = pl.BlockSpec((tr, n), lambda i: (i, 0))
    shape = jax.ShapeDtypeStruct((r, n), F32)
    return pl.pallas_call(
        body, name=name, grid=(r // tr,), in_specs=[blk] * 4, out_specs=[blk] * 3, out_shape=[shape] * 3,
        compiler_params=_params(("arbitrary",)),
    )(w, g, m, v)


def adamw_received(w, own, got, m, v, name):
    r, n = w.shape
    tr = _tile(r, SUM_ROWS, 16)

    def body(w_ref, own_ref, got_ref, m_ref, v_ref, g_ref, d_ref, mo_ref, vo_ref):
        g = own_ref[...]
        for j in range(3):
            g = g + got_ref[j].astype(F32)
        g_ref[...] = g
        d_ref[...], mo_ref[...], vo_ref[...] = _adamw_step(w_ref[...], g, m_ref[...], v_ref[...])

    blk = pl.BlockSpec((tr, n), lambda i: (i, 0))
    shape = jax.ShapeDtypeStruct((r, n), F32)
    return pl.pallas_call(
        body, name=name, grid=(r // tr,),
        in_specs=[blk, blk, pl.BlockSpec((3, tr, n), lambda i: (0, i, 0)), blk, blk],
        out_specs=[blk] * 4, out_shape=[shape] * 4, compiler_params=_params(("arbitrary",)),
    )(w, own, got, m, v)


def _pad_to(v, n):
    return jnp.pad(v, (0, n - v.shape[0]))


def _pad_heads(w, axis_len):
    n = w.shape[1]
    return jnp.pad(w.reshape(MLA_HEADS, axis_len, n), ((0, 0), (0, HEAD_PAD - axis_len), (0, 0))).reshape(-1, n)


def _swap_head_parts(w, inner, outer):
    n = w.shape[1]
    return w.reshape(outer, inner, QK_NOPE, n).transpose(1, 0, 2, 3).reshape(-1, n)


def kernel(x, c, positions, ada_w, ada_b, norm_ffn1_g, ffn1_w1, ffn1_w3, ffn1_w2, norm_mix_g, w_in, conv_w, q_norm_g, w_uq, kv_norm_g, w_ukv, out_norm_g, w_out, norm_ffn2_g, ffn2_w1, ffn2_w3, ffn2_w2, final_norm_g, loss_target, m_ada_w, m_ada_b, m_norm_ffn1_g, m_ffn1_w1, m_ffn1_w3, m_ffn1_w2, m_norm_mix_g, m_w_in, m_conv_w, m_q_norm_g, m_w_uq, m_kv_norm_g, m_w_ukv, m_out_norm_g, m_w_out, m_norm_ffn2_g, m_ffn2_w1, m_ffn2_w3, m_ffn2_w2, m_final_norm_g, v_ada_w, v_ada_b, v_norm_ffn1_g, v_ffn1_w1, v_ffn1_w3, v_ffn1_w2, v_norm_mix_g, v_w_in, v_conv_w, v_q_norm_g, v_w_uq, v_kv_norm_g, v_w_ukv, v_out_norm_g, v_w_out, v_norm_ffn2_g, v_ffn2_w1, v_ffn2_w3, v_ffn2_w2, v_final_norm_g):
    t, d = x.shape[1], x.shape[2]
    f = ffn1_w2.shape[1] * N_DEV
    me = 4 * lax.axis_index("x") + 2 * lax.axis_index("y") + lax.axis_index("c")
    my_c = lax.axis_index("c")
    my_chip = 2 * lax.axis_index("x") + lax.axis_index("y")
    xs = x[0]
    n_ada = ada_w.shape[2]
    cw_n = conv_w.shape[2]

    c_rows = jnp.broadcast_to(c, (8, d))
    conv_rows = jnp.pad(conv_w[0], ((0, 8 - CONV_K), (0, LANES - cw_n)))
    ffn1_blocks = [ffn1_w1[0].T.astype(BF16), ffn1_w3[0].T.astype(BF16), ffn1_w2[0].astype(BF16)]
    ffn2_blocks = [ffn2_w1[0].T.astype(BF16), ffn2_w3[0].T.astype(BF16), ffn2_w2[0].astype(BF16)]
    c_all, conv_all, *ffn1_all = all_gather_relayed([c_rows, conv_rows] + ffn1_blocks, [0] * 5, "gather_first")
    c_all = c_all[:, 0, :]
    conv_full8 = conv_all[:, :, :cw_n].transpose(1, 0, 2).reshape(8, CONV_WIDTH)
    ffn1_ws = [w.reshape(f, d) for w in ffn1_all]
    gather_mix = riding_gather(
        [w_in[0].T.astype(BF16), w_uq[0].T.astype(BF16), w_ukv[0].T.astype(BF16), w_out[0].astype(BF16)], [0, 0, 0, 0])

    ada_b_cols = lax.dynamic_slice_in_dim(ada_b, me * n_ada, n_ada, axis=1)
    mod_cols = ada_forward(c_all, ada_w[0], ada_b_cols)
    mod_all, = all_gather([mod_cols], [0], "gather_mod")
    mod = lax.dynamic_index_in_dim(mod_all, me, axis=1, keepdims=False).reshape(N_MOD, 1, d)
    sh1, sc1, g1, sh2, sc2, g2, sh3, sc3, g3 = [mod[i] for i in range(N_MOD)]

    gf = final_norm_g.reshape(1, d)
    x1, h1, a1, b1, y1, *gathered = ffn_forward(xs, norm_ffn1_g, sc1, sh1, g1, ffn1_ws, "ffn1_fwd", gather_mix)
    w_in_p = jnp.pad(gathered[0].reshape(IN_COLS, d), ((0, ZC_COLS + ZM_COLS - IN_COLS), (0, 0)))
    w_uq_p = _pad_heads(gathered[1].reshape(-1, Q_LORA), QK_NOPE + QK_ROPE)
    w_ukv_p = _swap_head_parts(gathered[2].reshape(-1, KV_LORA), 2, MLA_HEADS)
    w_out_f = gathered[3].reshape(MIX_WIDTH, d)
    h2, zc, zm = mix_in_forward(x1, norm_mix_g, sc2, sh2, w_in_p)
    pos = positions[0].astype(F32).reshape(t, 1)
    inv_freq = ROPE_THETA ** (-jnp.arange(0, QK_ROPE, 2, dtype=F32) / QK_ROPE)
    inv_freq = jnp.concatenate([inv_freq, inv_freq, jnp.zeros((LANES - QK_ROPE,), F32)]).reshape(1, LANES)
    qn, kvn, q, k, v = mla_project(zm, pos, inv_freq, q_norm_g, kv_norm_g, w_uq_p, w_ukv_p)
    o, lse, *ffn2_all = attention_forward(q, k, v, riding_gather(ffn2_blocks, [0] * 3))
    ffn2_ws = [w.reshape(f, d) for w in ffn2_all]
    lane = jnp.arange(CONV_WIDTH)
    gmat_a = (lane[:, None] // (CONV_WIDTH // CONV_GROUPS) == lane[None, :] // (CONV_WIDTH // CONV_GROUPS))
    gmat_a = (gmat_a / (CONV_WIDTH // CONV_GROUPS)).astype(BF16)
    gmat_b = ((lane[:, None] // V_HEAD == lane[None, :] // V_HEAD) / V_HEAD).astype(BF16)
    x2, yn, y2, ya = mix_out_forward(zc, o, conv_full8, out_norm_g, gmat_a, gmat_b, w_out_f, x1, g2)
    dx3, h3, a3, b3, y3, dy3, sums_f = ffn_forward(x2, norm_ffn2_g, sc3, sh3, g3, ffn2_ws, "ffn2_fwd",
                                                   loss_head=(loss_target[0], gf))

    chip_idx = jnp.bitwise_xor(my_chip, jnp.array([0, 2, 1, 3], jnp.int32)).astype(jnp.int32)
    src_idx = (2 * chip_idx + my_c).astype(jnp.int32)

    def row_blocks(named):
        return [g.reshape(N_DEV, g.shape[0] // N_DEV, g.shape[1]) for _, g in named]

    def chip_sums(named, g8, got):
        return [add_sibling(g, r, src_idx, chip_idx, "rs_add_" + n) for g, r, (n, _) in zip(g8, got, named)]

    da3, db3, g_w2b = ffn_backward_gate(dy3, a3, b3, ffn2_ws[2], "ffn2_bwd_gate")
    dx2, sums_3 = ffn_backward_norm(da3, db3, dx3, x2, y3, norm_ffn2_g, sc3, ffn2_ws[0], ffn2_ws[1], "ffn2_bwd_norm")
    ffn2_named = [("ffn2_w1", matmul_tn(da3, h3, "ffn2_gw1")), ("ffn2_w3", matmul_tn(db3, h3, "ffn2_gw3")),
                  ("ffn2_w2", g_w2b)]
    ffn2_g8 = row_blocks(ffn2_named)
    dy2, dya, do, delta, sums_2d, sums_2o, *ffn2_sib = mix_out_backward(
        dx2, y2, g2, ya, o, out_norm_g, gmat_a, gmat_b, w_out_f, riding_sibling(ffn2_g8))
    ffn2_sums = chip_sums(ffn2_named, ffn2_g8, ffn2_sib)
    g_w_out = matmul_tn(yn, dy2, "gw_out")
    nq = t // _tile(t, ATTN_TILE, CHUNK)
    stat_shape = (MLA_HEADS, nq, 1, t // nq)
    dq, dk, dv, *ffn2_got = attention_backward(q, k, v, do, lse.reshape(stat_shape), delta.reshape(stat_shape),
                                               riding_exchange([s[1] for s in ffn2_sums]))
    dzc, sums_c = conv_backward(zc, dya, conv_full8)
    dql, dkvl, dzm, sums_m = mla_project_backward(dq, dk, dv, zm, pos, inv_freq, q_norm_g, kv_norm_g, w_uq_p, w_ukv_p)
    g_w_uq_p = matmul_tn(dql, qn, "gw_uq")
    g_w_ukv_p = matmul_tn(dkvl, kvn, "gw_ukv")
    g_w_in = matmul_tn([dzc, dzm], h2, "gw_in")[:IN_COLS]
    g_w_uq = g_w_uq_p.reshape(MLA_HEADS, HEAD_PAD, Q_LORA)[:, :QK_NOPE + QK_ROPE].reshape(-1, Q_LORA)
    g_w_ukv = _swap_head_parts(g_w_ukv_p, MLA_HEADS, 2)
    mix_named = [("w_in", g_w_in), ("w_uq", g_w_uq), ("w_ukv", g_w_ukv), ("w_out", g_w_out)]
    mix_g8 = row_blocks(mix_named)
    dx1, dy1, sums_1m, *mix_sib = mix_in_backward(dzc, dzm, w_in_p, x1, dx2, norm_mix_g, sc2, g1, riding_sibling(mix_g8))
    mix_sums = chip_sums(mix_named, mix_g8, mix_sib)
    da1, db1, g_w2a, *mix_got = ffn_backward_gate(dy1, a1, b1, ffn1_ws[2], "ffn1_bwd_gate",
                                                  riding_exchange([s[1] for s in mix_sums]))
    ffn1_pair = [("ffn1_w2", g_w2a), ("ffn1_w1", matmul_tn(da1, h1, "ffn1_gw1"))]
    pair_g8 = row_blocks(ffn1_pair)
    g_w3a, *pair_sib = matmul_tn(db1, h1, "ffn1_gw3", riding_sibling(pair_g8))
    ffn1_last = [("ffn1_w3", g_w3a)]
    last_g8 = row_blocks(ffn1_last)
    ffn1_named = ffn1_pair + ffn1_last
    ffn1_sums = chip_sums(ffn1_pair, pair_g8, pair_sib) + chip_sums(
        ffn1_last, last_g8, exchange_sibling(last_g8, "rs_sibling_ffn1_w3"))
    dx0, sums_1, *ffn1_got = ffn_backward_norm(da1, db1, dx1, xs, y1, norm_ffn1_g, sc1, ffn1_ws[0], ffn1_ws[1], "ffn1_bwd_norm",
                                               riding_exchange([s[1] for s in ffn1_sums]))
    reduced = {}
    for named, group_sums, group_got in ((ffn2_named, ffn2_sums, ffn2_got), (mix_named, mix_sums, mix_got),
                                         (ffn1_named, ffn1_sums, ffn1_got)):
        for (n, _), (own, _), got in zip(named, group_sums, group_got):
            reduced[n] = (own, got)

    dmod = jnp.concatenate([sums_1[0], sums_1[1], sums_1[2], sums_1m[0], sums_1m[1], sums_2d[0],
                            sums_3[0], sums_3[1], sums_3[2]])
    pieces = [dmod, sums_1[3], sums_1m[2], sums_m[0, :Q_LORA], sums_m[0, Q_LORA:Q_LORA + KV_LORA], sums_2o[0],
              sums_3[3], sums_f[0], sums_f[1], sums_c[:CONV_K].reshape(-1)]
    plens = [p.shape[0] for p in pieces]
    poffs = [sum(plens[:i]) for i in range(len(plens))]
    vec_len = -(-sum(plens) // 1024) * 1024
    vec = _pad_to(jnp.concatenate(pieces), vec_len).reshape(-1, LANES)
    vec_all, = all_gather([vec], [0], "gather_sums")
    tot = sum_devices(vec_all).reshape(-1)
    g_ada_b, g_n1, g_nmix, g_qg, g_kvg, g_og, g_n3, g_gf, loss_lanes, g_conv_full = [
        tot[o:o + n] for o, n in zip(poffs, plens)]
    loss = sum_lanes(loss_lanes.reshape(1, d))[0, 0]
    g_conv = lax.dynamic_slice_in_dim(g_conv_full.reshape(CONV_K, CONV_WIDTH), me * cw_n, cw_n, axis=1)
    dmod_all = vec_all.reshape(N_DEV, vec_len)[:, :N_MOD * d]
    dmod_cols = lax.dynamic_slice_in_dim(dmod_all, me * n_ada, n_ada, axis=1)
    g_ada_w = ada_backward(jnp.pad(c_all, ((0, 8), (0, 0))), jnp.pad(dmod_cols, ((0, 8), (0, 0))))

    def update(name, w, g, m, v, received=None):
        k, n = w.shape[-2:]
        if g.shape == (k, n):
            flat, back = (lambda a: a.reshape(k, n)), (lambda a: a.reshape(w.shape))
        else:
            flat, back = (lambda a: a.reshape(k, n).T), (lambda a: a.T.reshape(w.shape))
        if received is None:
            out = (g,) + tuple(adamw(flat(w), g, flat(m), flat(v), "adamw_" + name))
        else:
            out = adamw_received(flat(w), g, received, flat(m), flat(v), "adamw_" + name)
        return tuple(back(a) for a in out)

    res = {}
    res["ada_w"] = update("ada_w", ada_w, g_ada_w, m_ada_w, v_ada_w)
    big = [("ffn1_w1", ffn1_w1, m_ffn1_w1, v_ffn1_w1), ("ffn1_w3", ffn1_w3, m_ffn1_w3, v_ffn1_w3),
           ("ffn2_w1", ffn2_w1, m_ffn2_w1, v_ffn2_w1), ("ffn2_w3", ffn2_w3, m_ffn2_w3, v_ffn2_w3),
           ("w_in", w_in, m_w_in, v_w_in), ("w_uq", w_uq, m_w_uq, v_w_uq), ("w_ukv", w_ukv, m_w_ukv, v_w_ukv),
           ("ffn1_w2", ffn1_w2, m_ffn1_w2, v_ffn1_w2), ("ffn2_w2", ffn2_w2, m_ffn2_w2, v_ffn2_w2),
           ("w_out", w_out, m_w_out, v_w_out)]
    for name, w, m, v in big:
        res[name] = update(name, w, reduced[name][0], m, v, reduced[name][1])
    smalls = [("ada_b", ada_b, g_ada_b, m_ada_b, v_ada_b),
              ("norm_ffn1_g", norm_ffn1_g, g_n1, m_norm_ffn1_g, v_norm_ffn1_g),
              ("norm_mix_g", norm_mix_g, g_nmix, m_norm_mix_g, v_norm_mix_g),
              ("conv_w", conv_w, g_conv, m_conv_w, v_conv_w),
              ("q_norm_g", q_norm_g, g_qg, m_q_norm_g, v_q_norm_g),
              ("kv_norm_g", kv_norm_g, g_kvg, m_kv_norm_g, v_kv_norm_g),
              ("out_norm_g", out_norm_g, g_og, m_out_norm_g, v_out_norm_g),
              ("norm_ffn2_g", norm_ffn2_g, g_n3, m_norm_ffn2_g, v_norm_ffn2_g),
              ("final_norm_g", final_norm_g, g_gf, m_final_norm_g, v_final_norm_g)]
    slens = [w.size for _, w, _, _, _ in smalls]
    soffs = [sum(slens[:i]) for i in range(len(slens))]
    s_len = -(-sum(slens) // 1024) * 1024

    def pack_small(i):
        return _pad_to(jnp.concatenate([s[i].reshape(-1) for s in smalls]), s_len).reshape(8, -1)

    s_out = adamw(pack_small(1), pack_small(2), pack_small(3), pack_small(4), "adamw_small")
    for (name, w, g, _, _), o, n in zip(smalls, soffs, slens):
        res[name] = (g.reshape(w.shape),) + tuple(a.reshape(-1)[o:o + n].reshape(w.shape) for a in s_out)

    order = ["ada_w", "ada_b", "norm_ffn1_g", "ffn1_w1", "ffn1_w3", "ffn1_w2", "norm_mix_g", "w_in", "conv_w",
             "q_norm_g", "w_uq", "kv_norm_g", "w_ukv", "out_norm_g", "w_out", "norm_ffn2_g", "ffn2_w1", "ffn2_w3",
             "ffn2_w2", "final_norm_g"]
    return (loss, dx0.reshape(x.shape), *[res[n][0] for n in order], *[res[n][1] for n in order],
            *[res[n][2] for n in order], *[res[n][3] for n in order])
```

```python
import functools
import math

import jax
import jax.numpy as jnp
from jax import lax
from jax.experimental import pallas as pl
from jax.experimental.pallas import tpu as pltpu

F32 = jnp.float32
BF16 = jnp.bfloat16
MESH_ID = pl.DeviceIdType.MESH
N_DEV = 8

EPS = 1e-6
CHUNK = 64
N_MOD = 9
CONV_WIDTH = 512
CONV_GROUPS = 8
CONV_K = 3
MLA_HEADS = 4
QK_NOPE = 128
QK_ROPE = 64
V_HEAD = 128
Q_LORA = 384
KV_LORA = 256
ROPE_THETA = 10000.0
MLA_WIDTH = MLA_HEADS * V_HEAD
MIX_WIDTH = CONV_WIDTH + MLA_WIDTH
IN_COLS = 3 * CONV_WIDTH + Q_LORA + KV_LORA + QK_ROPE
ZC_COLS = 3 * CONV_WIDTH
ZM_COLS = Q_LORA + KV_LORA + 128
HEAD_PAD = 256
QK_COLS = MLA_HEADS * HEAD_PAD
ATTN_SCALE = (QK_NOPE + QK_ROPE) ** -0.5
LOG2_E = 1.4426950408889634
LN_2 = 0.6931471805599453
QK_FOLD = ATTN_SCALE * LOG2_E
NEG_INF = -1e30

ADAM_LR = 0.001
ADAM_B1 = 0.9
ADAM_B2 = 0.999
ADAM_EPS = 1e-08
ADAM_WD = 0.01
ADAM_STEP = 10

LANES = 128
MXU_COLS = 256
VMEM_LIMIT = 56 * 1024 * 1024
ROW_TILE = 1024
FFN_FWD_TILE = (1024, 256)
FFN_BWD_TILE = (512, 1408)
GRAD_TILE = 1408
GRAD_DEPTH = 2048
SUM_ROWS = 256
ADAM_TILE_ELEMS = 1 << 19
ATTN_TILE = 1024

NN = (((1,), (0,)), ((), ()))
NT = (((1,), (1,)), ((), ()))
TN = (((0,), (0,)), ((), ()))


def _dot(a, b, dims=NN):
    return lax.dot_general(a, b, dims, preferred_element_type=F32)


def _tile(n, cap, mult=LANES):
    best = None
    for t in range(mult, min(n, cap) + 1, mult):
        if n % t == 0:
            best = t
    return n if best is None else best


def _params(sem=None):
    return pltpu.CompilerParams(dimension_semantics=sem, vmem_limit_bytes=VMEM_LIMIT)


def _row(v):
    return pl.BlockSpec(v.shape, lambda *_: (0,) * v.ndim)


def _sigmoid(x):
    return 0.5 * jnp.tanh(0.5 * x) + 0.5


def _rms(x):
    r = lax.rsqrt(jnp.mean(x * x, axis=-1, keepdims=True) + EPS)
    return x * r, r


def _norm_mod_bwd(dh, x, gn, sc):
    xhat, r = _rms(x)
    d_sh = jnp.sum(dh, axis=0, keepdims=True)
    d_sc = jnp.sum(dh * (xhat * gn), axis=0, keepdims=True)
    dxn = dh * (1.0 + sc)
    d_gn = jnp.sum(dxn * xhat, axis=0, keepdims=True)
    dxh = dxn * gn
    dx = r * (dxh - xhat * jnp.mean(dxh * xhat, axis=-1, keepdims=True))
    return dx, d_sh, d_sc, d_gn


def _group_mean(v, gmat):
    return _dot(v.astype(BF16), gmat)


def _add_rows(ref, rows):
    for r, v in enumerate(rows):
        ref[r:r + 1, :] += v


def _window(ref, axis, j):
    return ref.at[(slice(None),) * axis + (j,)]


def _any_specs(n):
    return [pl.BlockSpec(memory_space=pl.ANY)] * n


def all_gather(blocks, axes, name):
    n_arr = len(blocks)

    def body(*refs):
        start, forward, finish = _gather_steps(refs[:n_arr], refs[n_arr:2 * n_arr], axes, *refs[2 * n_arr:])
        start()
        for j in range(3):
            forward(j)
        finish()

    return pl.pallas_call(
        body, name=name, out_shape=_gathered_shapes(blocks, axes),
        in_specs=_any_specs(n_arr), out_specs=_any_specs(n_arr), scratch_shapes=_gather_sems(n_arr),
    )(*blocks)


def all_gather_relayed(blocks, axes, name):
    n_arr = len(blocks)
    arrays = range(n_arr)

    def body(*refs):
        ins, outs = refs[:n_arr], refs[n_arr:2 * n_arr]
        send_sems, recv_sems, local_sems = refs[2 * n_arr:]
        x, y, c = lax.axis_index("x"), lax.axis_index("y"), lax.axis_index("c")
        sibling, x_nbr, y_nbr, diagonal = (x, y, 1 - c), (1 - x, y, c), (x, 1 - y, c), (1 - x, 1 - y, c)
        north = c == 1
        relay_slot = jnp.where(north, 1, 2)
        relay_from = tuple(jnp.where(north, a, b) for a, b in zip(x_nbr, y_nbr))
        relay_to = tuple(jnp.where(north, a, b) for a, b in zip(y_nbr, x_nbr))
        other_from = relay_to

        def slot(a, px, py, pc):
            return _window(outs[a], axes[a], 4 * px + 2 * py + pc)

        def copy(a, k, block, to, src=None):
            return pltpu.make_async_remote_copy(
                src_ref=slot(a, *block) if src is None else src, dst_ref=slot(a, *block),
                send_sem=send_sems.at[k, a], recv_sem=recv_sems.at[k, a], device_id=to, device_id_type=MESH_ID)

        mine = [pltpu.make_async_copy(ins[a], slot(a, x, y, c), local_sems.at[a]) for a in arrays]
        for cp in mine:
            cp.start()
        first = [copy(a, k, (x, y, c), to, src=ins[a])
                 for k, to in enumerate((sibling, x_nbr, y_nbr)) for a in arrays]
        for cp in first:
            cp.start()
        later = []
        for a in arrays:
            copy(a, relay_slot, relay_from, (x, y, c)).wait_recv()
            later += [copy(a, 3, relay_from, relay_to), copy(a, 3 + relay_slot, relay_from, sibling)]
            later[-2].start()
            later[-1].start()
        for a in arrays:
            copy(a, 3 - relay_slot, other_from, (x, y, c)).wait_recv()
            later.append(copy(a, 6 - relay_slot, other_from, sibling))
            later[-1].start()
        for a in arrays:
            copy(a, 3, diagonal, (x, y, c)).wait_recv()
            later.append(copy(a, 6, diagonal, sibling))
            later[-1].start()
        for a in arrays:
            for k, block in ((0, sibling), (4, (1 - x, y, 1 - c)), (5, (x, 1 - y, 1 - c)), (6, (1 - x, 1 - y, 1 - c))):
                copy(a, k, block, (x, y, c)).wait_recv()
        for cp in first + later:
            cp.wait_send()
        for cp in mine:
            cp.wait()

    return pl.pallas_call(
        body, name=name, out_shape=_gathered_shapes(blocks, axes),
        in_specs=_any_specs(n_arr), out_specs=_any_specs(n_arr), scratch_shapes=_gather_sems(n_arr),
    )(*blocks)


def _gathered_shapes(blocks, axes):
    return [jax.ShapeDtypeStruct(b.shape[:ax] + (N_DEV,) + b.shape[ax:], b.dtype) for b, ax in zip(blocks, axes)]


def _gather_sems(n_arr):
    return [pltpu.SemaphoreType.DMA((7, n_arr)), pltpu.SemaphoreType.DMA((7, n_arr)), pltpu.SemaphoreType.DMA((n_arr,))]


def _gather_steps(ins, outs, axes, send_sems, recv_sems, local_sems):
    arrays = range(len(ins))
    x, y, c = lax.axis_index("x"), lax.axis_index("y"), lax.axis_index("c")
    me, sibling = (x, y, c), (x, y, 1 - c)
    chips = [(1 - x, y), (x, 1 - y), (1 - x, 1 - y)]

    def slot(a, px, py, pc):
        return _window(outs[a], axes[a], 4 * px + 2 * py + pc)

    def copy(a, k, block, to, src=None):
        return pltpu.make_async_remote_copy(
            src_ref=slot(a, *block) if src is None else src, dst_ref=slot(a, *block),
            send_sem=send_sems.at[k, a], recv_sem=recv_sems.at[k, a], device_id=to, device_id_type=MESH_ID)

    def mine(a):
        return pltpu.make_async_copy(ins[a], slot(a, *me), local_sems.at[a])

    def first():
        return ([copy(a, 0, me, sibling, src=ins[a]) for a in arrays]
                + [copy(a, 1 + j, me, (*chip, c), src=ins[a]) for j, chip in enumerate(chips) for a in arrays])

    def passed(j):
        return [copy(a, 4 + j, (*chips[j], c), sibling) for a in arrays]

    def start():
        for a in arrays:
            mine(a).start()
        for cp in first():
            cp.start()

    def forward(j):
        for a, cp in zip(arrays, passed(j)):
            copy(a, 1 + j, (*chips[j], c), me).wait_recv()
            cp.start()

    def finish():
        for a in arrays:
            copy(a, 0, sibling, me).wait_recv()
        for j, chip in enumerate(chips):
            for a in arrays:
                copy(a, 4 + j, (*chip, 1 - c), me).wait_recv()
        for cp in first() + passed(0) + passed(1) + passed(2):
            cp.wait_send()
        for a in arrays:
            mine(a).wait()

    return start, forward, finish


def exchange_sibling(grads, name):
    n_arr = len(grads)

    def body(*refs):
        start, finish = _sibling_exchange_steps(refs[:n_arr], refs[n_arr:2 * n_arr], *refs[2 * n_arr:])
        start()
        finish()

    return pl.pallas_call(
        body, name=name, out_shape=_sibling_shapes(grads),
        in_specs=_any_specs(n_arr), out_specs=_any_specs(n_arr), scratch_shapes=_exchange_sems(n_arr),
    )(*grads)


def _sibling_shapes(grads):
    return [jax.ShapeDtypeStruct((4,) + g.shape[1:], g.dtype) for g in grads]


def _exchange_sems(n_arr):
    return [pltpu.SemaphoreType.DMA((n_arr,)), pltpu.SemaphoreType.DMA((n_arr,))]


def _sibling_exchange_steps(ins, outs, send_sems, recv_sems):
    x, y, c = lax.axis_index("x"), lax.axis_index("y"), lax.axis_index("c")

    def copy(a, src, dst):
        return pltpu.make_async_remote_copy(
            src_ref=src, dst_ref=dst, send_sem=send_sems.at[a], recv_sem=recv_sems.at[a],
            device_id=(x, y, 1 - c), device_id_type=MESH_ID)

    def start():
        for a in range(len(ins)):
            for k in range(4):
                copy(a, ins[a].at[2 * k + (1 - c)], outs[a].at[k]).start()

    def finish():
        whole = [copy(a, ins[a].at[pl.ds(0, 4)], outs[a]) for a in range(len(ins))]
        for cp in whole:
            cp.wait_recv()
        for cp in whole:
            cp.wait_send()

    return start, finish


def _chip_exchange_steps(ins, outs, send_sems, recv_sems):
    x, y, c = lax.axis_index("x"), lax.axis_index("y"), lax.axis_index("c")
    chips = [(1 - x, y), (x, 1 - y), (1 - x, 1 - y)]

    def copy(a, src, dst, chip):
        return pltpu.make_async_remote_copy(
            src_ref=src, dst_ref=dst, send_sem=send_sems.at[a], recv_sem=recv_sems.at[a],
            device_id=(*chip, c), device_id_type=MESH_ID)

    def start():
        for a in range(len(ins)):
            for j, chip in enumerate(chips):
                copy(a, ins[a].at[j], outs[a].at[j], chip).start()

    def finish():
        whole = [copy(a, ins[a], outs[a], chips[0]) for a in range(len(ins))]
        for cp in whole:
            cp.wait_recv()
        for cp in whole:
            cp.wait_send()

    return start, finish


def riding_gather(blocks, axes):
    def phases(ins, outs, *sems):
        start, forward, finish = _gather_steps(ins, outs, axes, *sems)
        return [start] + [functools.partial(forward, j) for j in range(3)] + [finish]

    return dict(operands=blocks, out_shape=_gathered_shapes(blocks, axes), sems=_gather_sems(len(blocks)),
                phases=phases, when=("first", "late0", "late1", "late2", "last"))


def riding_exchange(parts):
    def phases(ins, outs, *sems):
        return list(_chip_exchange_steps(ins, outs, *sems))

    return dict(operands=parts, out_shape=[jax.ShapeDtypeStruct(p.shape, p.dtype) for p in parts],
                sems=_exchange_sems(len(parts)), phases=phases, when=("first", "last"))


def riding_sibling(grads):
    def phases(ins, outs, *sems):
        return list(_sibling_exchange_steps(ins, outs, *sems))

    return dict(operands=grads, out_shape=_sibling_shapes(grads), sems=_exchange_sems(len(grads)),
                phases=phases, when=("first", "last"))


def _call_with_rider(body, rider, *, name, grid, in_specs, out_specs, out_shape, scratch_shapes, operands):
    params = _params(("arbitrary",) * len(grid))
    if rider is None:
        return pl.pallas_call(body, name=name, grid=grid, in_specs=in_specs, out_specs=out_specs,
                              out_shape=out_shape, scratch_shapes=scratch_shapes, compiler_params=params)(*operands)
    n_in, n_out, n_scr, k = len(in_specs), len(out_specs), len(scratch_shapes), len(rider["operands"])
    at = {"first": (0,) * len(grid), "last": tuple(g - 1 for g in grid)}
    if "late0" in rider["when"]:
        rows, cols = grid
        assert cols >= 3
        at.update({"late%d" % j: (max(rows - 2, 0), j) for j in range(3)})

    def wrapped(*refs):
        ins, c_in = refs[:n_in], refs[n_in:n_in + k]
        outs, c_out = refs[n_in + k:n_in + k + n_out], refs[n_in + k + n_out:n_in + 2 * k + n_out]
        scratch, sems = refs[n_in + 2 * k + n_out:n_in + 2 * k + n_out + n_scr], refs[n_in + 2 * k + n_out + n_scr:]
        pos = [pl.program_id(axis) for axis in range(len(grid))]

        def here(key):
            return functools.reduce(jnp.logical_and, [p == v for p, v in zip(pos, at[key])])

        phases = rider["phases"](c_in, c_out, *sems)
        for fn, key in zip(phases, rider["when"]):
            if key != "last":
                pl.when(here(key))(fn)
        body(*ins, *outs, *scratch)
        pl.when(here("last"))(phases[-1])

    return pl.pallas_call(
        wrapped, name=name, grid=grid,
        in_specs=list(in_specs) + _any_specs(k), out_specs=list(out_specs) + _any_specs(k),
        out_shape=list(out_shape) + rider["out_shape"], scratch_shapes=list(scratch_shapes) + rider["sems"],
        compiler_params=params)(*operands, *rider["operands"])


def add_sibling(g8, got, src_idx, chip_idx, name):
    _, r, n = g8.shape
    tr = _tile(r, SUM_ROWS, 16)

    def body(si_ref, ci_ref, g0_ref, g1_ref, g2_ref, g3_ref, got_ref, own_ref, send_ref):
        own_ref[...] = g0_ref[0] + got_ref[ci_ref[0]]
        for j, g_ref in enumerate((g1_ref, g2_ref, g3_ref)):
            send_ref[j] = (g_ref[0] + got_ref[ci_ref[j + 1]]).astype(BF16)

    def mine(j):
        return pl.BlockSpec((1, tr, n), lambda i, si, ci: (si[j], i, 0))

    return pl.pallas_call(
        body, name=name,
        out_shape=[jax.ShapeDtypeStruct((r, n), F32), jax.ShapeDtypeStruct((3, r, n), BF16)],
        grid_spec=pltpu.PrefetchScalarGridSpec(
            num_scalar_prefetch=2, grid=(r // tr,),
            in_specs=[mine(0), mine(1), mine(2), mine(3), pl.BlockSpec((4, tr, n), lambda i, si, ci: (0, i, 0))],
            out_specs=[pl.BlockSpec((tr, n), lambda i, si, ci: (i, 0)),
                       pl.BlockSpec((3, tr, n), lambda i, si, ci: (0, i, 0))]),
        compiler_params=_params(("arbitrary",)),
    )(src_idx, chip_idx, g8, g8, g8, g8, got)


def sum_devices(g):
    def body(g_ref, o_ref):
        acc = g_ref[0]
        for j in range(1, N_DEV):
            acc = acc + g_ref[j]
        o_ref[...] = acc

    return pl.pallas_call(body, name="sum_devices", out_shape=jax.ShapeDtypeStruct(g.shape[1:], F32))(g)


def sum_lanes(v):
    def body(v_ref, o_ref):
        o_ref[...] = jnp.broadcast_to(jnp.sum(v_ref[...], axis=-1, keepdims=True), (1, LANES))

    return pl.pallas_call(body, name="sum_lanes", out_shape=jax.ShapeDtypeStruct((1, LANES), F32))(v)


def ada_forward(c_all, ada_w, ada_b_cols):
    nb, n = c_all.shape[0], ada_w.shape[1]

    def body(c_ref, w_ref, b_ref, o_ref):
        cv = c_ref[...]
        s = (cv * jax.nn.sigmoid(cv)).astype(BF16)
        o_ref[...] = _dot(s, w_ref[...].astype(BF16)) + b_ref[...]

    return pl.pallas_call(body, name="ada_fwd", out_shape=jax.ShapeDtypeStruct((nb, n), F32),
                          compiler_params=_params())(c_all, ada_w, ada_b_cols)


def ada_backward(c_all16, dmod16):
    d, n = c_all16.shape[1], dmod16.shape[1]

    def body(c_ref, g_ref, o_ref):
        cv = c_ref[...]
        s = (cv * jax.nn.sigmoid(cv)).astype(BF16)
        o_ref[...] = _dot(s, g_ref[...].astype(BF16), TN)

    return pl.pallas_call(body, name="ada_bwd", out_shape=jax.ShapeDtypeStruct((d, n), F32),
                          compiler_params=_params())(c_all16, dmod16)


def ffn_forward(x, gn, sc, sh, gate, ws, name, rider=None, loss_head=None):
    t, d = x.shape
    f = ws[0].shape[0]
    tm, tf = _tile(t, FFN_FWD_TILE[0], 16), _tile(f, FFN_FWD_TILE[1])
    nf = f // tf
    n_in = 5 if loss_head is None else 7

    def body(*refs):
        x_ref, gn_ref, sc_ref, sh_ref, gate_ref = refs[:5]
        w1_ref, w3_ref, w2_ref, xo_ref, h_ref, a_ref, b_ref, y_ref = refs[n_in:n_in + 8]
        hs, acc = refs[-2:]
        i, j = pl.program_id(0), pl.program_id(1)

        if loss_head is not None:
            @pl.when(jnp.logical_and(i == 0, j == 0))
            def _():
                refs[n_in + 9][...] = jnp.zeros_like(refs[n_in + 9])

        @pl.when(j == 0)
        def _():
            xhat, _ = _rms(x_ref[...])
            h = (xhat * gn_ref[...] * (1.0 + sc_ref[...]) + sh_ref[...]).astype(BF16)
            hs[...] = h
            h_ref[...] = h
            acc[...] = jnp.zeros_like(acc)

        h = hs[...]
        a = _dot(h, w1_ref[...], NT)
        b = _dot(h, w3_ref[...], NT)
        a_ref[...] = a.astype(BF16)
        b_ref[...] = b.astype(BF16)
        u = (a * _sigmoid(a) * b).astype(BF16)
        acc[...] += _dot(u, w2_ref[...])

        @pl.when(j == nf - 1)
        def _():
            y = acc[...]
            y_ref[...] = y.astype(BF16)
            x_out = x_ref[...] + 0.5 * gate_ref[...] * y
            if loss_head is None:
                xo_ref[...] = x_out
            else:
                dx, d_g, loss = _loss_head(x_out, refs[5][...], refs[6][...])
                xo_ref[...] = dx
                refs[n_in + 8][...] = (0.5 * gate_ref[...] * dx).astype(BF16)
                _add_rows(refs[n_in + 9], [d_g, loss])

    row = pl.BlockSpec((tm, d), lambda i, j: (i, 0))
    vec = pl.BlockSpec((1, d), lambda i, j: (0, 0))
    wide = pl.BlockSpec((tm, tf), lambda i, j: (i, j))
    head = loss_head is not None
    return _call_with_rider(
        body, rider, name=name, grid=(t // tm, nf),
        in_specs=[row, vec, vec, vec, vec] + ([row, vec] if head else [])
        + [pl.BlockSpec((tf, d), lambda i, j: (j, 0))] * 3,
        out_specs=[row, row, wide, wide, row] + ([row, pl.BlockSpec((8, d), lambda i, j: (0, 0))] if head else []),
        out_shape=[jax.ShapeDtypeStruct((t, d), F32), jax.ShapeDtypeStruct((t, d), BF16),
                   jax.ShapeDtypeStruct((t, f), BF16), jax.ShapeDtypeStruct((t, f), BF16),
                   jax.ShapeDtypeStruct((t, d), BF16)]
        + ([jax.ShapeDtypeStruct((t, d), BF16), jax.ShapeDtypeStruct((8, d), F32)] if head else []),
        scratch_shapes=[pltpu.VMEM((tm, d), BF16), pltpu.VMEM((tm, d), F32)],
        operands=(x, gn, sc, sh, gate) + (tuple(loss_head) if head else ()) + tuple(ws))


def _loss_head(x, target, g):
    d = x.shape[-1]
    xhat, r = _rms(x)
    err = xhat * g - target
    dyf = err * (1.0 / d)
    dxh = dyf * g
    dx = r * (dxh - xhat * jnp.mean(dxh * xhat, axis=-1, keepdims=True))
    return dx, jnp.sum(dyf * xhat, axis=0, keepdims=True), jnp.sum(err * err, axis=0, keepdims=True) * (0.5 / d)


def ffn_backward_gate(dy, a, b, w2, name, rider=None):
    t, d = dy.shape
    f = w2.shape[0]
    tm, tf = _tile(t, FFN_BWD_TILE[0], 16), _tile(f, FFN_BWD_TILE[1])
    nf = f // tf

    def gate_body(dy_ref, a_ref, b_ref, w2_ref, da_ref, db_ref, gw2_ref):
        dy_v = dy_ref[...]
        du = _dot(dy_v, w2_ref[...], NT)
        av = a_ref[...].astype(F32)
        bv = b_ref[...].astype(F32)
        s = _sigmoid(av)
        sa = av * s
        da_ref[...] = (du * bv * (s + sa * (1.0 - s))).astype(BF16)
        db_ref[...] = (du * sa).astype(BF16)
        part = _dot((sa * bv).astype(BF16), dy_v, TN)

        @pl.when(pl.program_id(1) == 0)
        def _():
            gw2_ref[...] = part

        @pl.when(pl.program_id(1) > 0)
        def _():
            gw2_ref[...] += part

    hidden = jax.ShapeDtypeStruct((t, f), BF16)
    wide_t = pl.BlockSpec((tm, tf), lambda j, i: (i, j))
    return _call_with_rider(
        gate_body, rider, name=name, grid=(nf, t // tm),
        in_specs=[pl.BlockSpec((tm, d), lambda j, i: (i, 0)), wide_t, wide_t,
                  pl.BlockSpec((tf, d), lambda j, i: (j, 0))],
        out_specs=[wide_t, wide_t, pl.BlockSpec((tf, d), lambda j, i: (j, 0))],
        out_shape=[hidden, hidden, jax.ShapeDtypeStruct((f, d), F32)],
        scratch_shapes=[], operands=(dy, a, b, w2))


def ffn_backward_norm(da, db, dxo, x, y, gn, sc, w1t, w3t, name, rider=None):
    t, d = x.shape
    f = w1t.shape[0]
    tm, tf = _tile(t, FFN_BWD_TILE[0], 16), _tile(f, FFN_BWD_TILE[1])
    nf = f // tf
    row = pl.BlockSpec((tm, d), lambda i, j: (i, 0))
    vec = pl.BlockSpec((1, d), lambda i, j: (0, 0))
    wide = pl.BlockSpec((tm, tf), lambda i, j: (i, j))

    def norm_body(da_ref, db_ref, w1_ref, w3_ref, dxo_ref, x_ref, y_ref, gn_ref, sc_ref, dx_ref, sums_ref, acc):
        i, j = pl.program_id(0), pl.program_id(1)

        @pl.when(jnp.logical_and(i == 0, j == 0))
        def _():
            sums_ref[...] = jnp.zeros_like(sums_ref)

        part = _dot(da_ref[...], w1_ref[...]) + _dot(db_ref[...], w3_ref[...])

        @pl.when(j == 0)
        def _():
            acc[...] = part

        @pl.when(jnp.logical_and(j > 0, j < nf - 1))
        def _():
            acc[...] += part

        @pl.when(j == nf - 1)
        def _():
            dh = part if nf == 1 else acc[...] + part
            dxo_v = dxo_ref[...]
            dx, d_sh, d_sc, d_gn = _norm_mod_bwd(dh, x_ref[...], gn_ref[...], sc_ref[...])
            dx_ref[...] = dxo_v + dx
            d_gate = jnp.sum(dxo_v * (0.5 * y_ref[...].astype(F32)), axis=0, keepdims=True)
            _add_rows(sums_ref, [d_sh, d_sc, d_gate, d_gn])

    w_spec = pl.BlockSpec((tf, d), lambda i, j: (j, 0))
    return _call_with_rider(
        norm_body, rider, name=name, grid=(t // tm, nf),
        in_specs=[wide, wide, w_spec, w_spec, row, row, row, vec, vec],
        out_specs=[row, pl.BlockSpec((8, d), lambda i, j: (0, 0))],
        out_shape=[jax.ShapeDtypeStruct((t, d), F32), jax.ShapeDtypeStruct((8, d), F32)],
        scratch_shapes=[pltpu.VMEM((tm, d), F32)],
        operands=(da, db, w1t, w3t, dxo, x, y, gn, sc))


def matmul_tn(a, b, name, rider=None):
    parts = list(a) if isinstance(a, (list, tuple)) else [a]
    t, n = b.shape
    widths = [p.shape[1] for p in parts]
    tm = _tile(functools.reduce(math.gcd, widths), GRAD_TILE)
    tn, tk = _tile(n, GRAD_TILE), _tile(t, GRAD_DEPTH, 16)
    nk = t // tk
    counts = [w // tm for w in widths]
    firsts = [sum(counts[:p]) for p in range(len(parts))]

    def body(*refs):
        a_refs, (b_ref, o_ref, acc) = refs[:len(parts)], refs[len(parts):]
        i, k = pl.program_id(0), pl.program_id(2)

        @pl.when(k == 0)
        def _():
            acc[...] = jnp.zeros_like(acc)

        for a_ref, lo, cnt in zip(a_refs, firsts, counts):
            def accumulate(a_ref=a_ref):
                acc[...] += _dot(a_ref[...], b_ref[...], TN)

            if len(parts) == 1:
                accumulate()
            else:
                pl.when(jnp.logical_and(i >= lo, i < lo + cnt))(accumulate)

        @pl.when(k == nk - 1)
        def _():
            o_ref[...] = acc[...]

    def part_spec(lo, cnt):
        if len(parts) == 1:
            return pl.BlockSpec((tk, tm), lambda i, j, k: (k, i))

        def index(i, j, k):
            mine = jnp.logical_and(i >= lo, i < lo + cnt)
            return jnp.where(mine, k, 0), jnp.clip(i - lo, 0, cnt - 1)
        return pl.BlockSpec((tk, tm), index)

    out = _call_with_rider(
        body, rider, name=name, grid=(sum(counts), n // tn, nk),
        in_specs=[part_spec(lo, cnt) for lo, cnt in zip(firsts, counts)]
        + [pl.BlockSpec((tk, tn), lambda i, j, k: (k, j))],
        out_specs=[pl.BlockSpec((tm, tn), lambda i, j, k: (i, j))],
        out_shape=[jax.ShapeDtypeStruct((sum(widths), n), F32)],
        scratch_shapes=[pltpu.VMEM((tm, tn), F32)], operands=(*parts, b))
    return out[0] if rider is None else out


def mix_in_forward(x, gn, sc, sh, w_in):
    t, d = x.shape
    tm = _tile(t, ROW_TILE, 16)

    def body(x_ref, gn_ref, sc_ref, sh_ref, w_ref, h_ref, zc_ref, zm_ref):
        xhat, _ = _rms(x_ref[...])
        h = (xhat * gn_ref[...] * (1.0 + sc_ref[...]) + sh_ref[...]).astype(BF16)
        h_ref[...] = h
        z = _dot(h, w_ref[...], NT)
        zc_ref[...] = z[:, :ZC_COLS].astype(BF16)
        zm_ref[...] = z[:, ZC_COLS:].astype(BF16)

    row = pl.BlockSpec((tm, d), lambda i: (i, 0))
    vec = pl.BlockSpec((1, d), lambda i: (0, 0))
    return pl.pallas_call(
        body, name="mix_in_fwd", grid=(t // tm,),
        in_specs=[row, vec, vec, vec, _row(w_in)],
        out_specs=[row, pl.BlockSpec((tm, ZC_COLS), lambda i: (i, 0)), pl.BlockSpec((tm, ZM_COLS), lambda i: (i, 0))],
        out_shape=[jax.ShapeDtypeStruct((t, d), BF16), jax.ShapeDtypeStruct((t, ZC_COLS), BF16),
                   jax.ShapeDtypeStruct((t, ZM_COLS), BF16)],
        compiler_params=_params(("arbitrary",)),
    )(x, gn, sc, sh, w_in)


def _rope_tables(pos, inv_freq):
    ang = pos * inv_freq
    lane = lax.broadcasted_iota(jnp.int32, ang.shape, 1)
    cos, sin = jnp.cos(ang), jnp.sin(ang)
    half = QK_ROPE // 2
    return cos, jnp.where(lane < half, -sin, 0.0), jnp.where(jnp.logical_and(lane >= half, lane < QK_ROPE), sin, 0.0)


def _rope(v, tables):
    cos, sin_a, sin_b = tables
    return v * cos + pltpu.roll(v, LANES - QK_ROPE // 2, 1) * sin_a + pltpu.roll(v, QK_ROPE // 2, 1) * sin_b


def _rope_transposed(dv, tables):
    cos, sin_a, sin_b = tables
    return dv * cos + pltpu.roll(dv * sin_a, QK_ROPE // 2, 1) + pltpu.roll(dv * sin_b, LANES - QK_ROPE // 2, 1)


def mla_project(zm, pos, inv_freq, qg, kvg, w_uq, w_ukv):
    t = zm.shape[0]
    tm = _tile(t, ROW_TILE, 16)

    def body(zm_ref, pos_ref, if_ref, qg_ref, kvg_ref, wq_ref, wkv_ref, qn_ref, kvn_ref, q_ref, k_ref, v_ref):
        zv = zm_ref[...].astype(F32)
        qn = (_rms(zv[:, :Q_LORA])[0] * qg_ref[...]).astype(BF16)
        kvn = (_rms(zv[:, Q_LORA:Q_LORA + KV_LORA])[0] * kvg_ref[...]).astype(BF16)
        qn_ref[...] = qn
        kvn_ref[...] = kvn
        qf = _dot(qn, wq_ref[...], NT) * QK_FOLD
        kvf = _dot(kvn, wkv_ref[...], NT)
        tables = _rope_tables(pos_ref[...], if_ref[...])
        kr = _rope(zv[:, Q_LORA + KV_LORA:], tables).astype(BF16)
        for h in range(MLA_HEADS):
            lo = h * HEAD_PAD
            q_ref[:, lo:lo + QK_NOPE] = qf[:, lo:lo + QK_NOPE].astype(BF16)
            q_ref[:, lo + QK_NOPE:lo + HEAD_PAD] = _rope(qf[:, lo + QK_NOPE:lo + HEAD_PAD], tables).astype(BF16)
            k_ref[:, lo:lo + QK_NOPE] = kvf[:, h * QK_NOPE:(h + 1) * QK_NOPE].astype(BF16)
            k_ref[:, lo + QK_NOPE:lo + HEAD_PAD] = kr
        v_ref[...] = kvf[:, MLA_HEADS * QK_NOPE:].astype(BF16)

    def rows(n):
        return pl.BlockSpec((tm, n), lambda i: (i, 0))

    return pl.pallas_call(
        body, name="mla_project", grid=(t // tm,),
        in_specs=[rows(ZM_COLS), rows(1), _row(inv_freq), _row(qg), _row(kvg), _row(w_uq), _row(w_ukv)],
        out_specs=[rows(Q_LORA), rows(KV_LORA), rows(QK_COLS), rows(QK_COLS), rows(MLA_WIDTH)],
        out_shape=[jax.ShapeDtypeStruct((t, Q_LORA), BF16), jax.ShapeDtypeStruct((t, KV_LORA), BF16),
                   jax.ShapeDtypeStruct((t, QK_COLS), BF16), jax.ShapeDtypeStruct((t, QK_COLS), BF16),
                   jax.ShapeDtypeStruct((t, MLA_WIDTH), BF16)],
        compiler_params=_params(("arbitrary",)),
    )(zm, pos, inv_freq, qg, kvg, w_uq, w_ukv)


def _chunk_mask(shape, q_axis):
    qi = lax.broadcasted_iota(jnp.int32, shape, q_axis) // CHUNK
    ki = lax.broadcasted_iota(jnp.int32, shape, 1 - q_axis) // CHUNK
    return ki <= qi


def attention_forward(q, k, v, rider=None):
    t = q.shape[0]
    tq = _tile(t, ATTN_TILE, CHUNK)

    def body(q_ref, k_ref, v_ref, o_ref, lse_ref):
        i = pl.program_id(1)
        qv = q_ref[...]

        def step(kb, carry, masked, tiles=1):
            m, l, acc = carry
            keys = pl.ds(pl.multiple_of(kb * tq, tq), tiles * tq)
            s = _dot(qv, k_ref[keys, :], NT)
            if masked:
                s = jnp.where(_chunk_mask(s.shape, 0), s, NEG_INF)
            m_new = jnp.maximum(m, jnp.max(s, axis=-1, keepdims=True))
            alpha = jnp.exp2(m - m_new)
            p = jnp.exp2(s - m_new)
            l = alpha * l + jnp.sum(p, axis=-1, keepdims=True)
            acc = alpha * acc + _dot(p.astype(BF16), v_ref[keys, :])
            return m_new, l, acc

        init = (jnp.full((tq, 1), NEG_INF, F32), jnp.zeros((tq, 1), F32), jnp.zeros((tq, V_HEAD), F32))
        carry = lax.fori_loop(0, i // 2, lambda pb, cr: step(2 * pb, cr, False, 2), init)
        carry = lax.fori_loop(0, i % 2, lambda _, cr: step(i - 1, cr, False), carry)
        m, l, acc = step(i, carry, True)
        o_ref[...] = (acc / l).astype(BF16)
        lse_ref[0] = m + jnp.log2(l)

    return _call_with_rider(
        body, rider, name="attn_fwd", grid=(MLA_HEADS, t // tq),
        in_specs=[pl.BlockSpec((tq, HEAD_PAD), lambda h, i: (i, h)),
                  pl.BlockSpec((t, HEAD_PAD), lambda h, i: (0, h)),
                  pl.BlockSpec((t, V_HEAD), lambda h, i: (0, h))],
        out_specs=[pl.BlockSpec((tq, V_HEAD), lambda h, i: (i, h)),
                   pl.BlockSpec((1, tq, 1), lambda h, i: (h, i, 0))],
        out_shape=[jax.ShapeDtypeStruct((t, MLA_WIDTH), BF16), jax.ShapeDtypeStruct((MLA_HEADS, t, 1), F32)],
        scratch_shapes=[], operands=(q, k, v))


def attention_backward(q, k, v, do, lse, delta, rider=None):
    t = q.shape[0]
    tq = _tile(t, ATTN_TILE, CHUNK)
    nq = t // tq

    def body(q_ref, k_ref, v_ref, do_ref, lse_ref, delta_ref, dq_ref, dk_ref, dv_ref, dq_acc):
        kb = pl.program_id(1)

        @pl.when(kb == 0)
        def _():
            dq_acc[...] = jnp.zeros_like(dq_acc)

        kv, vv = k_ref[...], v_ref[...]

        def step(qb, carry, masked):
            dk, dv = carry
            rows = pl.ds(pl.multiple_of(qb * tq, tq), tq)
            qv, dov = q_ref[rows, :], do_ref[rows, :]
            s = _dot(kv, qv, NT)
            if masked:
                s = jnp.where(_chunk_mask(s.shape, 1), s, NEG_INF)
            p = jnp.exp2(s - lse_ref[0, qb])
            dv = dv + _dot(p.astype(BF16), dov)
            dp = _dot(vv, dov, NT)
            ds = (p * (dp - delta_ref[0, qb]) * LN_2).astype(BF16)
            dk = dk + _dot(ds, qv)
            dq_acc[rows, :] += _dot(ds, kv, TN)
            return dk, dv

        carry = step(kb, (jnp.zeros((tq, HEAD_PAD), F32), jnp.zeros((tq, V_HEAD), F32)), True)
        odd = (nq - 1 - kb) % 2
        carry = lax.fori_loop(0, odd, lambda _, cr: step(kb + 1, cr, False), carry)
        first = kb + 1 + odd
        dk, dv = lax.fori_loop(0, (nq - first) // 2,
                               lambda pb, cr: step(first + 2 * pb + 1, step(first + 2 * pb, cr, False), False), carry)
        dk_ref[...] = dk.astype(BF16)
        dv_ref[...] = dv.astype(BF16)

        @pl.when(kb == nq - 1)
        def _():
            dq_ref[...] = dq_acc[...].astype(BF16)

    stat = pl.BlockSpec((1, nq, 1, tq), lambda h, j: (h, 0, 0, 0))
    return _call_with_rider(
        body, rider, name="attn_bwd", grid=(MLA_HEADS, nq),
        in_specs=[pl.BlockSpec((t, HEAD_PAD), lambda h, j: (0, h)),
                  pl.BlockSpec((tq, HEAD_PAD), lambda h, j: (j, h)),
                  pl.BlockSpec((tq, V_HEAD), lambda h, j: (j, h)),
                  pl.BlockSpec((t, V_HEAD), lambda h, j: (0, h)), stat, stat],
        out_specs=[pl.BlockSpec((t, HEAD_PAD), lambda h, j: (0, h)),
                   pl.BlockSpec((tq, HEAD_PAD), lambda h, j: (j, h)),
                   pl.BlockSpec((tq, V_HEAD), lambda h, j: (j, h))],
        out_shape=[jax.ShapeDtypeStruct((t, QK_COLS), BF16), jax.ShapeDtypeStruct((t, QK_COLS), BF16),
                   jax.ShapeDtypeStruct((t, MLA_WIDTH), BF16)],
        scratch_shapes=[pltpu.VMEM((t, HEAD_PAD), F32)], operands=(q, k, v, do, lse, delta))


HALO = 16


def _halo_spec(tm, n, step, last):
    return pl.BlockSpec((HALO, n), lambda i: (jnp.clip(i * (tm // HALO) + step, 0, last), 0))


def _shift_rows(v, prev, n):
    out = pltpu.roll(v, n, 0)
    row = lax.broadcasted_iota(jnp.int32, v.shape, 0)
    for r in range(n):
        out = jnp.where(row == r, prev[HALO - n + r:HALO - n + r + 1, :], out)
    return out


def _advance_rows(v, nxt, n):
    rows = v.shape[0]
    out = pltpu.roll(v, rows - n, 0)
    row = lax.broadcasted_iota(jnp.int32, v.shape, 0)
    for r in range(n):
        out = jnp.where(row == rows - n + r, nxt[r:r + 1, :], out)
    return out


def _conv_taps(zc, zc_prev, first):
    w = CONV_WIDTH
    u = zc[:, w:2 * w] * zc[:, 2 * w:]
    up = jnp.where(first, 0.0, zc_prev[:, w:2 * w] * zc_prev[:, 2 * w:])
    return u, _shift_rows(u, up, 1), _shift_rows(u, up, 2)


def mix_out_forward(zc, o, conv_w, og, gmat_a, gmat_b, w_out, x, gate):
    t, d = x.shape
    tm = _tile(t, ROW_TILE, 16)
    w = CONV_WIDTH

    def body(zc_ref, zp_ref, o_ref, cw_ref, og_ref, ga_ref, gb_ref, w_ref, x_ref, gate_ref,
             xo_ref, yn_ref, y_ref, ya_ref):
        zc_v = zc_ref[...].astype(F32)
        u, u1, u2 = _conv_taps(zc_v, zp_ref[...].astype(F32), pl.program_id(0) == 0)
        cw = cw_ref[...]
        ya = zc_v[:, :w] * (cw[0:1] * u2 + cw[1:2] * u1 + cw[2:3] * u)
        ya_ref[...] = ya.astype(BF16)
        ov = o_ref[...].astype(F32)
        ogv = og_ref[...]
        yn_ref[:, :w] = (ya * lax.rsqrt(_group_mean(ya * ya, ga_ref[...]) + EPS) * ogv[:, :w]).astype(BF16)
        yn_ref[:, w:] = (ov * lax.rsqrt(_group_mean(ov * ov, gb_ref[...]) + EPS) * ogv[:, w:]).astype(BF16)
        y = _dot(yn_ref[...], w_ref[...])
        y_ref[...] = y.astype(BF16)
        xo_ref[...] = x_ref[...] + gate_ref[...] * y

    def rows(n):
        return pl.BlockSpec((tm, n), lambda i: (i, 0))

    return pl.pallas_call(
        body, name="mix_out_fwd", grid=(t // tm,),
        in_specs=[rows(ZC_COLS), _halo_spec(tm, ZC_COLS, -1, t // HALO - 1), rows(MLA_WIDTH), _row(conv_w), _row(og),
                  _row(gmat_a), _row(gmat_b), _row(w_out), rows(d), _row(gate)],
        out_specs=[rows(d), rows(MIX_WIDTH), rows(d), rows(w)],
        out_shape=[jax.ShapeDtypeStruct((t, d), F32), jax.ShapeDtypeStruct((t, MIX_WIDTH), BF16),
                   jax.ShapeDtypeStruct((t, d), BF16), jax.ShapeDtypeStruct((t, w), BF16)],
        compiler_params=_params(("arbitrary",)),
    )(zc, zc, o, conv_w, og, gmat_a, gmat_b, w_out, x, gate)


def _group_norm_bwd(dyn, y, og, gmat):
    rs = lax.rsqrt(_group_mean(y * y, gmat) + EPS)
    yhat = y * rs
    d_og = jnp.sum(dyn * yhat, axis=0, keepdims=True)
    dyh = dyn * og
    return rs * (dyh - yhat * _group_mean(dyh * yhat, gmat)), d_og


def mix_out_backward(dxo, y, gate, ya, o, og, gmat_a, gmat_b, w_out, rider=None):
    t, d = dxo.shape
    tm = _tile(t, ROW_TILE, 16)
    w = CONV_WIDTH

    def body(dxo_ref, y_ref, gate_ref, ya_ref, o_ref, og_ref, ga_ref, gb_ref, w_ref,
             dy_ref, dya_ref, do_ref, delta_ref, sd_ref, so_ref):
        @pl.when(pl.program_id(0) == 0)
        def _():
            sd_ref[...] = jnp.zeros_like(sd_ref)
            so_ref[...] = jnp.zeros_like(so_ref)

        dxo_v = dxo_ref[...]
        dy = (gate_ref[...] * dxo_v).astype(BF16)
        dy_ref[...] = dy
        sd_ref[0:1, :] += jnp.sum(dxo_v * y_ref[...].astype(F32), axis=0, keepdims=True)
        dyn = _dot(dy, w_ref[...], NT)
        ogv = og_ref[...]
        ov = o_ref[...].astype(F32)
        dya, d_og_a = _group_norm_bwd(dyn[:, :w], ya_ref[...].astype(F32), ogv[:, :w], ga_ref[...])
        dov, d_og_b = _group_norm_bwd(dyn[:, w:], ov, ogv[:, w:], gb_ref[...])
        dya_ref[...] = dya.astype(BF16)
        do_ref[...] = dov.astype(BF16)
        so_ref[0:1, :w] += d_og_a
        so_ref[0:1, w:] += d_og_b
        prod = dov * ov
        for h in range(MLA_HEADS):
            delta_ref[h] = jnp.sum(prod[:, h * V_HEAD:(h + 1) * V_HEAD], axis=-1, keepdims=True)

    def rows(n):
        return pl.BlockSpec((tm, n), lambda i: (i, 0))

    return _call_with_rider(
        body, rider, name="mix_out_bwd", grid=(t // tm,),
        in_specs=[rows(d), rows(d), _row(gate), rows(w), rows(MLA_WIDTH), _row(og), _row(gmat_a), _row(gmat_b),
                  _row(w_out)],
        out_specs=[rows(d), rows(w), rows(MLA_WIDTH), pl.BlockSpec((MLA_HEADS, tm, 1), lambda i: (0, i, 0)),
                   pl.BlockSpec((8, d), lambda i: (0, 0)), pl.BlockSpec((8, MIX_WIDTH), lambda i: (0, 0))],
        out_shape=[jax.ShapeDtypeStruct((t, d), BF16), jax.ShapeDtypeStruct((t, w), BF16),
                   jax.ShapeDtypeStruct((t, MLA_WIDTH), BF16), jax.ShapeDtypeStruct((MLA_HEADS, t, 1), F32),
                   jax.ShapeDtypeStruct((8, d), F32), jax.ShapeDtypeStruct((8, MIX_WIDTH), F32)],
        scratch_shapes=[], operands=(dxo, y, gate, ya, o, og, gmat_a, gmat_b, w_out))


def conv_backward(zc, dya, conv_w):
    t = zc.shape[0]
    tm = _tile(t, ROW_TILE, 16)
    nt = t // tm
    w = CONV_WIDTH

    def body(zc_ref, zp_ref, zn_ref, dya_ref, dn_ref, cw_ref, dzc_ref, sums_ref):
        i = pl.program_id(0)

        @pl.when(i == 0)
        def _():
            sums_ref[...] = jnp.zeros_like(sums_ref)

        zc_v = zc_ref[...].astype(F32)
        u, u1, u2 = _conv_taps(zc_v, zp_ref[...].astype(F32), i == 0)
        cw = cw_ref[...]
        dya_v = dya_ref[...].astype(F32)
        dyc = dya_v * zc_v[:, :w]
        dyc_next = jnp.where(i == nt - 1, 0.0, dn_ref[...].astype(F32) * zn_ref[:, :w].astype(F32))
        du = cw[2:3] * dyc + cw[1:2] * _advance_rows(dyc, dyc_next, 1) + cw[0:1] * _advance_rows(dyc, dyc_next, 2)
        dzc_ref[:, :w] = (dya_v * (cw[0:1] * u2 + cw[1:2] * u1 + cw[2:3] * u)).astype(BF16)
        dzc_ref[:, w:2 * w] = (du * zc_v[:, 2 * w:]).astype(BF16)
        dzc_ref[:, 2 * w:] = (du * zc_v[:, w:2 * w]).astype(BF16)
        _add_rows(sums_ref, [jnp.sum(dyc * tap, axis=0, keepdims=True) for tap in (u2, u1, u)])

    def rows(n):
        return pl.BlockSpec((tm, n), lambda i: (i, 0))

    def halo(n, step):
        return _halo_spec(tm, n, step, t // HALO - 1)

    return pl.pallas_call(
        body, name="conv_bwd", grid=(nt,),
        in_specs=[rows(ZC_COLS), halo(ZC_COLS, -1), halo(ZC_COLS, tm // HALO), rows(w), halo(w, tm // HALO),
                  _row(conv_w)],
        out_specs=[rows(ZC_COLS), pl.BlockSpec((8, w), lambda i: (0, 0))],
        out_shape=[jax.ShapeDtypeStruct((t, ZC_COLS), BF16), jax.ShapeDtypeStruct((8, w), F32)],
        compiler_params=_params(("arbitrary",)),
    )(zc, zc, zc, dya, dya, conv_w)


def _rms_bwd(dy, x, g):
    xhat, r = _rms(x)
    d_g = jnp.sum(dy * xhat, axis=0, keepdims=True)
    dxh = dy * g
    return r * (dxh - xhat * jnp.mean(dxh * xhat, axis=-1, keepdims=True)), d_g


def mla_project_backward(dq, dk, dv, zm, qn, kvn, pos, inv_freq, qg, kvg, w_uq, w_ukv):
    t = zm.shape[0]
    tm = _tile(t, ROW_TILE, 16)

    def body(dq_ref, dk_ref, dv_ref, zm_ref, qn_ref, kvn_ref, pos_ref, if_ref, qg_ref, kvg_ref, wq_ref, wkv_ref,
             dzm_ref, sums_ref, guq_ref, gukv_ref, dql_ref, dkvl_ref):
        @pl.when(pl.program_id(0) == 0)
        def _():
            sums_ref[...] = jnp.zeros_like(sums_ref)
            guq_ref[...] = jnp.zeros_like(guq_ref)
            gukv_ref[...] = jnp.zeros_like(gukv_ref)

        tables = _rope_tables(pos_ref[...], if_ref[...])
        dkr = jnp.zeros((tm, LANES), F32)
        for h in range(MLA_HEADS):
            lo = h * HEAD_PAD
            dql_ref[:, lo:lo + QK_NOPE] = (dq_ref[:, lo:lo + QK_NOPE].astype(F32) * QK_FOLD).astype(BF16)
            dql_ref[:, lo + QK_NOPE:lo + HEAD_PAD] = _rope_transposed(
                dq_ref[:, lo + QK_NOPE:lo + HEAD_PAD].astype(F32) * QK_FOLD, tables).astype(BF16)
            dkvl_ref[:, h * QK_NOPE:(h + 1) * QK_NOPE] = dk_ref[:, lo:lo + QK_NOPE]
            dkr = dkr + dk_ref[:, lo + QK_NOPE:lo + HEAD_PAD].astype(F32)
        dkvl_ref[:, MLA_HEADS * QK_NOPE:] = dv_ref[...]
        zv = zm_ref[...].astype(F32)
        dqn = _dot(dql_ref[...], wq_ref[...])
        dkvn = _dot(dkvl_ref[...], wkv_ref[...])
        dcq, d_qg = _rms_bwd(dqn, zv[:, :Q_LORA], qg_ref[...])
        dckv, d_kvg = _rms_bwd(dkvn, zv[:, Q_LORA:Q_LORA + KV_LORA], kvg_ref[...])
        dzm_ref[:, :Q_LORA] = dcq.astype(BF16)
        dzm_ref[:, Q_LORA:Q_LORA + KV_LORA] = dckv.astype(BF16)
        dzm_ref[:, Q_LORA + KV_LORA:] = _rope_transposed(dkr, tables).astype(BF16)
        sums_ref[0:1, :Q_LORA] += d_qg
        sums_ref[0:1, Q_LORA:Q_LORA + KV_LORA] += d_kvg
        guq_ref[...] += _dot(dql_ref[...], qn_ref[...], TN)
        gukv_ref[...] += _dot(dkvl_ref[...], kvn_ref[...], TN)

    def rows(n):
        return pl.BlockSpec((tm, n), lambda i: (i, 0))

    def whole(r, n):
        return pl.BlockSpec((r, n), lambda i: (0, 0))

    return pl.pallas_call(
        body, name="mla_project_bwd", grid=(t // tm,),
        in_specs=[rows(QK_COLS), rows(QK_COLS), rows(MLA_WIDTH), rows(ZM_COLS), rows(Q_LORA), rows(KV_LORA),
                  rows(1), _row(inv_freq), _row(qg), _row(kvg), _row(w_uq), _row(w_ukv)],
        out_specs=[rows(ZM_COLS), whole(8, ZM_COLS), whole(QK_COLS, Q_LORA), whole(QK_COLS, KV_LORA)],
        out_shape=[jax.ShapeDtypeStruct((t, ZM_COLS), BF16), jax.ShapeDtypeStruct((8, ZM_COLS), F32),
                   jax.ShapeDtypeStruct((QK_COLS, Q_LORA), F32), jax.ShapeDtypeStruct((QK_COLS, KV_LORA), F32)],
        scratch_shapes=[pltpu.VMEM((tm, QK_COLS), BF16), pltpu.VMEM((tm, QK_COLS), BF16)],
        compiler_params=_params(("arbitrary",)),
    )(dq, dk, dv, zm, qn, kvn, pos, inv_freq, qg, kvg, w_uq, w_ukv)


def mix_in_backward(dzc, dzm, w_in, x, dxo, gn, sc, gate, rider=None):
    t, d = x.shape
    tm = _tile(t, ROW_TILE, 16)

    def body(dzc_ref, dzm_ref, w_ref, x_ref, dxo_ref, gn_ref, sc_ref, gate_ref, dx_ref, dy_ref, sums_ref):
        @pl.when(pl.program_id(0) == 0)
        def _():
            sums_ref[...] = jnp.zeros_like(sums_ref)

        dh = _dot(dzc_ref[...], w_ref[:ZC_COLS, :]) + _dot(dzm_ref[...], w_ref[ZC_COLS:, :])
        dx, d_sh, d_sc, d_gn = _norm_mod_bwd(dh, x_ref[...], gn_ref[...], sc_ref[...])
        dx = dxo_ref[...] + dx
        dx_ref[...] = dx
        dy_ref[...] = (0.5 * gate_ref[...] * dx).astype(BF16)
        _add_rows(sums_ref, [d_sh, d_sc, d_gn])

    def rows(n):
        return pl.BlockSpec((tm, n), lambda i: (i, 0))

    return _call_with_rider(
        body, rider, name="mix_in_bwd", grid=(t // tm,),
        in_specs=[rows(ZC_COLS), rows(ZM_COLS), _row(w_in), rows(d), rows(d), _row(gn), _row(sc), _row(gate)],
        out_specs=[rows(d), rows(d), pl.BlockSpec((8, d), lambda i: (0, 0))],
        out_shape=[jax.ShapeDtypeStruct((t, d), F32), jax.ShapeDtypeStruct((t, d), BF16),
                   jax.ShapeDtypeStruct((8, d), F32)],
        scratch_shapes=[], operands=(dzc, dzm, w_in, x, dxo, gn, sc, gate))


def _adamw_step(w, g, m, v):
    m_new = ADAM_B1 * m + (1.0 - ADAM_B1) * g
    v_new = ADAM_B2 * v + (1.0 - ADAM_B2) * (g * g)
    m_hat = m_new / (1.0 - ADAM_B1 ** ADAM_STEP)
    v_hat = v_new / (1.0 - ADAM_B2 ** ADAM_STEP)
    return -ADAM_LR * (m_hat / (jnp.sqrt(v_hat) + ADAM_EPS) + ADAM_WD * w), m_new, v_new


def adamw(w, g, m, v, name):
    r, n = w.shape
    tr = _tile(r, max(8, ADAM_TILE_ELEMS // n), 8)

    def body(w_ref, g_ref, m_ref, v_ref, d_ref, mo_ref, vo_ref):
        d_ref[...], mo_ref[...], vo_ref[...] = _adamw_step(w_ref[...], g_ref[...], m_ref[...], v_ref[...])

    blk = pl.BlockSpec((tr, n), lambda i: (i, 0))
    shape = jax.ShapeDtypeStruct((r, n), F32)
    return pl.pallas_call(
        body, name=name, grid=(r // tr,), in_specs=[blk] * 4, out_specs=[blk] * 3, out_shape=[shape] * 3,
        compiler_params=_params(("arbitrary",)),
    )(w, g, m, v)


def adamw_received(w, own, got, m, v, name):
    r, n = w.shape
    tr = _tile(r, SUM_ROWS, 16)

    def body(w_ref, own_ref, got_ref, m_ref, v_ref, g_ref, d_ref, mo_ref, vo_ref):
        g = own_ref[...]
        for j in range(3):
            g = g + got_ref[j].astype(F32)
        g_ref[...] = g
        d_ref[...], mo_ref[...], vo_ref[...] = _adamw_step(w_ref[...], g, m_ref[...], v_ref[...])

    blk = pl.BlockSpec((tr, n), lambda i: (i, 0))
    shape = jax.ShapeDtypeStruct((r, n), F32)
    return pl.pallas_call(
        body, name=name, grid=(r // tr,),
        in_specs=[blk, blk, pl.BlockSpec((3, tr, n), lambda i: (0, i, 0)), blk, blk],
        out_specs=[blk] * 4, out_shape=[shape] * 4, compiler_params=_params(("arbitrary",)),
    )(w, own, got, m, v)


def _pad_to(v, n):
    return jnp.pad(v, (0, n - v.shape[0]))


def _pad_heads(w, axis_len):
    n = w.shape[1]
    return jnp.pad(w.reshape(MLA_HEADS, axis_len, n), ((0, 0), (0, HEAD_PAD - axis_len), (0, 0))).reshape(-1, n)


def _swap_head_parts(w, inner, outer):
    n = w.shape[1]
    return w.reshape(outer, inner, QK_NOPE, n).transpose(1, 0, 2, 3).reshape(-1, n)


def kernel(x, c, positions, ada_w, ada_b, norm_ffn1_g, ffn1_w1, ffn1_w3, ffn1_w2, norm_mix_g, w_in, conv_w, q_norm_g, w_uq, kv_norm_g, w_ukv, out_norm_g, w_out, norm_ffn2_g, ffn2_w1, ffn2_w3, ffn2_w2, final_norm_g, loss_target, m_ada_w, m_ada_b, m_norm_ffn1_g, m_ffn1_w1, m_ffn1_w3, m_ffn1_w2, m_norm_mix_g, m_w_in, m_conv_w, m_q_norm_g, m_w_uq, m_kv_norm_g, m_w_ukv, m_out_norm_g, m_w_out, m_norm_ffn2_g, m_ffn2_w1, m_ffn2_w3, m_ffn2_w2, m_final_norm_g, v_ada_w, v_ada_b, v_norm_ffn1_g, v_ffn1_w1, v_ffn1_w3, v_ffn1_w2, v_norm_mix_g, v_w_in, v_conv_w, v_q_norm_g, v_w_uq, v_kv_norm_g, v_w_ukv, v_out_norm_g, v_w_out, v_norm_ffn2_g, v_ffn2_w1, v_ffn2_w3, v_ffn2_w2, v_final_norm_g):
    t, d = x.shape[1], x.shape[2]
    f = ffn1_w2.shape[1] * N_DEV
    me = 4 * lax.axis_index("x") + 2 * lax.axis_index("y") + lax.axis_index("c")
    my_c = lax.axis_index("c")
    my_chip = 2 * lax.axis_index("x") + lax.axis_index("y")
    xs = x[0]
    n_ada = ada_w.shape[2]
    cw_n = conv_w.shape[2]

    c_rows = jnp.broadcast_to(c, (8, d))
    conv_rows = jnp.pad(conv_w[0], ((0, 8 - CONV_K), (0, LANES - cw_n)))
    ffn1_blocks = [ffn1_w1[0].T.astype(BF16), ffn1_w3[0].T.astype(BF16), ffn1_w2[0].astype(BF16)]
    ffn2_blocks = [ffn2_w1[0].T.astype(BF16), ffn2_w3[0].T.astype(BF16), ffn2_w2[0].astype(BF16)]
    c_all, conv_all, *ffn1_all = all_gather_relayed([c_rows, conv_rows] + ffn1_blocks, [0] * 5, "gather_first")
    c_all = c_all[:, 0, :]
    conv_full8 = conv_all[:, :, :cw_n].transpose(1, 0, 2).reshape(8, CONV_WIDTH)
    ffn1_ws = [w.reshape(f, d) for w in ffn1_all]
    gather_mix = riding_gather(
        [w_in[0].T.astype(BF16), w_uq[0].T.astype(BF16), w_ukv[0].T.astype(BF16), w_out[0].astype(BF16)], [0, 0, 0, 0])

    ada_b_cols = lax.dynamic_slice_in_dim(ada_b, me * n_ada, n_ada, axis=1)
    mod_cols = ada_forward(c_all, ada_w[0], ada_b_cols)
    mod_all, = all_gather([mod_cols], [0], "gather_mod")
    mod = lax.dynamic_index_in_dim(mod_all, me, axis=1, keepdims=False).reshape(N_MOD, 1, d)
    sh1, sc1, g1, sh2, sc2, g2, sh3, sc3, g3 = [mod[i] for i in range(N_MOD)]

    gf = final_norm_g.reshape(1, d)
    x1, h1, a1, b1, y1, *gathered = ffn_forward(xs, norm_ffn1_g, sc1, sh1, g1, ffn1_ws, "ffn1_fwd", gather_mix)
    w_in_p = jnp.pad(gathered[0].reshape(IN_COLS, d), ((0, ZC_COLS + ZM_COLS - IN_COLS), (0, 0)))
    w_uq_p = _pad_heads(gathered[1].reshape(-1, Q_LORA), QK_NOPE + QK_ROPE)
    w_ukv_p = _swap_head_parts(gathered[2].reshape(-1, KV_LORA), 2, MLA_HEADS)
    w_out_f = gathered[3].reshape(MIX_WIDTH, d)
    h2, zc, zm = mix_in_forward(x1, norm_mix_g, sc2, sh2, w_in_p)
    pos = positions[0].astype(F32).reshape(t, 1)
    inv_freq = ROPE_THETA ** (-jnp.arange(0, QK_ROPE, 2, dtype=F32) / QK_ROPE)
    inv_freq = jnp.concatenate([inv_freq, inv_freq, jnp.zeros((LANES - QK_ROPE,), F32)]).reshape(1, LANES)
    qn, kvn, q, k, v = mla_project(zm, pos, inv_freq, q_norm_g, kv_norm_g, w_uq_p, w_ukv_p)
    o, lse, *ffn2_all = attention_forward(q, k, v, riding_gather(ffn2_blocks, [0] * 3))
    ffn2_ws = [w.reshape(f, d) for w in ffn2_all]
    lane = jnp.arange(CONV_WIDTH)
    gmat_a = (lane[:, None] // (CONV_WIDTH // CONV_GROUPS) == lane[None, :] // (CONV_WIDTH // CONV_GROUPS))
    gmat_a = (gmat_a / (CONV_WIDTH // CONV_GROUPS)).astype(BF16)
    gmat_b = ((lane[:, None] // V_HEAD == lane[None, :] // V_HEAD) / V_HEAD).astype(BF16)
    x2, yn, y2, ya = mix_out_forward(zc, o, conv_full8, out_norm_g, gmat_a, gmat_b, w_out_f, x1, g2)
    dx3, h3, a3, b3, y3, dy3, sums_f = ffn_forward(x2, norm_ffn2_g, sc3, sh3, g3, ffn2_ws, "ffn2_fwd",
                                                   loss_head=(loss_target[0], gf))

    chip_idx = jnp.bitwise_xor(my_chip, jnp.array([0, 2, 1, 3], jnp.int32)).astype(jnp.int32)
    src_idx = (2 * chip_idx + my_c).astype(jnp.int32)

    def row_blocks(named):
        return [g.reshape(N_DEV, g.shape[0] // N_DEV, g.shape[1]) for _, g in named]

    def chip_sums(named, g8, got):
        return [add_sibling(g, r, src_idx, chip_idx, "rs_add_" + n) for g, r, (n, _) in zip(g8, got, named)]

    da3, db3, g_w2b = ffn_backward_gate(dy3, a3, b3, ffn2_ws[2], "ffn2_bwd_gate")
    dx2, sums_3 = ffn_backward_norm(da3, db3, dx3, x2, y3, norm_ffn2_g, sc3, ffn2_ws[0], ffn2_ws[1], "ffn2_bwd_norm")
    ffn2_named = [("ffn2_w1", matmul_tn(da3, h3, "ffn2_gw1")), ("ffn2_w3", matmul_tn(db3, h3, "ffn2_gw3")),
                  ("ffn2_w2", g_w2b)]
    ffn2_g8 = row_blocks(ffn2_named)
    dy2, dya, do, delta, sums_2d, sums_2o, *ffn2_sib = mix_out_backward(
        dx2, y2, g2, ya, o, out_norm_g, gmat_a, gmat_b, w_out_f, riding_sibling(ffn2_g8))
    ffn2_sums = chip_sums(ffn2_named, ffn2_g8, ffn2_sib)
    g_w_out = matmul_tn(yn, dy2, "gw_out")
    nq = t // _tile(t, ATTN_TILE, CHUNK)
    stat_shape = (MLA_HEADS, nq, 1, t // nq)
    dq, dk, dv, *ffn2_got = attention_backward(q, k, v, do, lse.reshape(stat_shape), delta.reshape(stat_shape),
                                               riding_exchange([s[1] for s in ffn2_sums]))
    dzc, sums_c = conv_backward(zc, dya, conv_full8)
    dzm, sums_m, g_w_uq_p, g_w_ukv_p = mla_project_backward(
        dq, dk, dv, zm, qn, kvn, pos, inv_freq, q_norm_g, kv_norm_g, w_uq_p, w_ukv_p)
    g_w_in = matmul_tn([dzc, dzm], h2, "gw_in")[:IN_COLS]
    g_w_uq = g_w_uq_p.reshape(MLA_HEADS, HEAD_PAD, Q_LORA)[:, :QK_NOPE + QK_ROPE].reshape(-1, Q_LORA)
    g_w_ukv = _swap_head_parts(g_w_ukv_p, MLA_HEADS, 2)
    mix_named = [("w_in", g_w_in), ("w_uq", g_w_uq), ("w_ukv", g_w_ukv), ("w_out", g_w_out)]
    mix_g8 = row_blocks(mix_named)
    dx1, dy1, sums_1m, *mix_sib = mix_in_backward(dzc, dzm, w_in_p, x1, dx2, norm_mix_g, sc2, g1, riding_sibling(mix_g8))
    mix_sums = chip_sums(mix_named, mix_g8, mix_sib)
    da1, db1, g_w2a, *mix_got = ffn_backward_gate(dy1, a1, b1, ffn1_ws[2], "ffn1_bwd_gate",
                                                  riding_exchange([s[1] for s in mix_sums]))
    ffn1_pair = [("ffn1_w2", g_w2a), ("ffn1_w1", matmul_tn(da1, h1, "ffn1_gw1"))]
    pair_g8 = row_blocks(ffn1_pair)
    g_w3a, *pair_sib = matmul_tn(db1, h1, "ffn1_gw3", riding_sibling(pair_g8))
    ffn1_last = [("ffn1_w3", g_w3a)]
    last_g8 = row_blocks(ffn1_last)
    ffn1_named = ffn1_pair + ffn1_last
    ffn1_sums = chip_sums(ffn1_pair, pair_g8, pair_sib) + chip_sums(
        ffn1_last, last_g8, exchange_sibling(last_g8, "rs_sibling_ffn1_w3"))
    dx0, sums_1, *ffn1_got = ffn_backward_norm(da1, db1, dx1, xs, y1, norm_ffn1_g, sc1, ffn1_ws[0], ffn1_ws[1], "ffn1_bwd_norm",
                                               riding_exchange([s[1] for s in ffn1_sums]))
    reduced = {}
    for named, group_sums, group_got in ((ffn2_named, ffn2_sums, ffn2_got), (mix_named, mix_sums, mix_got),
                                         (ffn1_named, ffn1_sums, ffn1_got)):
        for (n, _), (own, _), got in zip(named, group_sums, group_got):
            reduced[n] = (own, got)

    dmod = jnp.concatenate([sums_1[0], sums_1[1], sums_1[2], sums_1m[0], sums_1m[1], sums_2d[0],
                            sums_3[0], sums_3[1], sums_3[2]])
    pieces = [dmod, sums_1[3], sums_1m[2], sums_m[0, :Q_LORA], sums_m[0, Q_LORA:Q_LORA + KV_LORA], sums_2o[0],
              sums_3[3], sums_f[0], sums_f[1], sums_c[:CONV_K].reshape(-1)]
    plens = [p.shape[0] for p in pieces]
    poffs = [sum(plens[:i]) for i in range(len(plens))]
    vec_len = -(-sum(plens) // 1024) * 1024
    vec = _pad_to(jnp.concatenate(pieces), vec_len).reshape(-1, LANES)
    vec_all, = all_gather([vec], [0], "gather_sums")
    tot = sum_devices(vec_all).reshape(-1)
    g_ada_b, g_n1, g_nmix, g_qg, g_kvg, g_og, g_n3, g_gf, loss_lanes, g_conv_full = [
        tot[o:o + n] for o, n in zip(poffs, plens)]
    loss = sum_lanes(loss_lanes.reshape(1, d))[0, 0]
    g_conv = lax.dynamic_slice_in_dim(g_conv_full.reshape(CONV_K, CONV_WIDTH), me * cw_n, cw_n, axis=1)
    dmod_all = vec_all.reshape(N_DEV, vec_len)[:, :N_MOD * d]
    dmod_cols = lax.dynamic_slice_in_dim(dmod_all, me * n_ada, n_ada, axis=1)
    g_ada_w = ada_backward(jnp.pad(c_all, ((0, 8), (0, 0))), jnp.pad(dmod_cols, ((0, 8), (0, 0))))

    def update(name, w, g, m, v, received=None):
        k, n = w.shape[-2:]
        if g.shape == (k, n):
            flat, back = (lambda a: a.reshape(k, n)), (lambda a: a.reshape(w.shape))
        else:
            flat, back = (lambda a: a.reshape(k, n).T), (lambda a: a.T.reshape(w.shape))
        if received is None:
            out = (g,) + tuple(adamw(flat(w), g, flat(m), flat(v), "adamw_" + name))
        else:
            out = adamw_received(flat(w), g, received, flat(m), flat(v), "adamw_" + name)
        return tuple(back(a) for a in out)

    res = {}
    res["ada_w"] = update("ada_w", ada_w, g_ada_w, m_ada_w, v_ada_w)
    big = [("ffn1_w1", ffn1_w1, m_ffn1_w1, v_ffn1_w1), ("ffn1_w3", ffn1_w3, m_ffn1_w3, v_ffn1_w3),
           ("ffn2_w1", ffn2_w1, m_ffn2_w1, v_ffn2_w1), ("ffn2_w3", ffn2_w3, m_ffn2_w3, v_ffn2_w3),
           ("w_in", w_in, m_w_in, v_w_in), ("w_uq", w_uq, m_w_uq, v_w_uq), ("w_ukv", w_ukv, m_w_ukv, v_w_ukv),
           ("ffn1_w2", ffn1_w2, m_ffn1_w2, v_ffn1_w2), ("ffn2_w2", ffn2_w2, m_ffn2_w2, v_ffn2_w2),
           ("w_out", w_out, m_w_out, v_w_out)]
    for name, w, m, v in big:
        res[name] = update(name, w, reduced[name][0], m, v, reduced[name][1])
    smalls = [("ada_b", ada_b, g_ada_b, m_ada_b, v_ada_b),
              ("norm_ffn1_g", norm_ffn1_g, g_n1, m_norm_ffn1_g, v_norm_ffn1_g),
              ("norm_mix_g", norm_mix_g, g_nmix, m_norm_mix_g, v_norm_mix_g),
              ("conv_w", conv_w, g_conv, m_conv_w, v_conv_w),
              ("q_norm_g", q_norm_g, g_qg, m_q_norm_g, v_q_norm_g),
              ("kv_norm_g", kv_norm_g, g_kvg, m_kv_norm_g, v_kv_norm_g),
              ("out_norm_g", out_norm_g, g_og, m_out_norm_g, v_out_norm_g),
              ("norm_ffn2_g", norm_ffn2_g, g_n3, m_norm_ffn2_g, v_norm_ffn2_g),
              ("final_norm_g", final_norm_g, g_gf, m_final_norm_g, v_final_norm_g)]
    slens = [w.size for _, w, _, _, _ in smalls]
    soffs = [sum(slens[:i]) for i in range(len(slens))]
    s_len = -(-sum(slens) // 1024) * 1024

    def pack_small(i):
        return _pad_to(jnp.concatenate([s[i].reshape(-1) for s in smalls]), s_len).reshape(8, -1)

    s_out = adamw(pack_small(1), pack_small(2), pack_small(3), pack_small(4), "adamw_small")
    for (name, w, g, _, _), o, n in zip(smalls, soffs, slens):
        res[name] = (g.reshape(w.shape),) + tuple(a.reshape(-1)[o:o + n].reshape(w.shape) for a in s_out)

    order = ["ada_w", "ada_b", "norm_ffn1_g", "ffn1_w1", "ffn1_w3", "ffn1_w2", "norm_mix_g", "w_in", "conv_w",
             "q_norm_g", "w_uq", "kv_norm_g", "w_ukv", "out_norm_g", "w_out", "norm_ffn2_g", "ffn2_w1", "ffn2_w3",
             "ffn2_w2", "final_norm_g"]
    return (loss, dx0.reshape(x.shape), *[res[n][0] for n in order], *[res[n][1] for n in order],
            *[res[n][2] for n in order], *[res[n][3] for n in order])
```

```python
import functools
import math

import jax
import jax.numpy as jnp
from jax import lax
from jax.experimental import pallas as pl
from jax.experimental.pallas import tpu as pltpu

F32 = jnp.float32
BF16 = jnp.bfloat16
MESH_ID = pl.DeviceIdType.MESH
N_DEV = 8

EPS = 1e-6
CHUNK = 64
N_MOD = 9
CONV_WIDTH = 512
CONV_GROUPS = 8
CONV_K = 3
MLA_HEADS = 4
QK_NOPE = 128
QK_ROPE = 64
V_HEAD = 128
Q_LORA = 384
KV_LORA = 256
ROPE_THETA = 10000.0
MLA_WIDTH = MLA_HEADS * V_HEAD
MIX_WIDTH = CONV_WIDTH + MLA_WIDTH
IN_COLS = 3 * CONV_WIDTH + Q_LORA + KV_LORA + QK_ROPE
ZC_COLS = 3 * CONV_WIDTH
ZM_COLS = Q_LORA + KV_LORA + 128
HEAD_PAD = 256
QK_COLS = MLA_HEADS * HEAD_PAD
ATTN_SCALE = (QK_NOPE + QK_ROPE) ** -0.5
LOG2_E = 1.4426950408889634
LN_2 = 0.6931471805599453
QK_FOLD = ATTN_SCALE * LOG2_E
NEG_INF = -1e30

ADAM_LR = 0.001
ADAM_B1 = 0.9
ADAM_B2 = 0.999
ADAM_EPS = 1e-08
ADAM_WD = 0.01
ADAM_STEP = 10

LANES = 128
MXU_COLS = 256
VMEM_LIMIT = 56 * 1024 * 1024
ROW_TILE = 1024
FFN_FWD_TILE = (1024, 256)
FFN_BWD_TILE = (512, 1408)
GRAD_TILE = 1408
GRAD_DEPTH = 2048
SUM_ROWS = 256
ADAM_TILE_ELEMS = 1 << 19
ATTN_TILE = 1024

NN = (((1,), (0,)), ((), ()))
NT = (((1,), (1,)), ((), ()))
TN = (((0,), (0,)), ((), ()))


def _dot(a, b, dims=NN):
    return lax.dot_general(a, b, dims, preferred_element_type=F32)


def _tile(n, cap, mult=LANES):
    best = None
    for t in range(mult, min(n, cap) + 1, mult):
        if n % t == 0:
            best = t
    return n if best is None else best


def _params(sem=None):
    return pltpu.CompilerParams(dimension_semantics=sem, vmem_limit_bytes=VMEM_LIMIT)


def _row(v):
    return pl.BlockSpec(v.shape, lambda *_: (0,) * v.ndim)


def _sigmoid(x):
    return 0.5 * jnp.tanh(0.5 * x) + 0.5


def _rms(x):
    r = lax.rsqrt(jnp.mean(x * x, axis=-1, keepdims=True) + EPS)
    return x * r, r


def _norm_mod_bwd(dh, x, gn, sc):
    xhat, r = _rms(x)
    d_sh = jnp.sum(dh, axis=0, keepdims=True)
    d_sc = jnp.sum(dh * (xhat * gn), axis=0, keepdims=True)
    dxn = dh * (1.0 + sc)
    d_gn = jnp.sum(dxn * xhat, axis=0, keepdims=True)
    dxh = dxn * gn
    dx = r * (dxh - xhat * jnp.mean(dxh * xhat, axis=-1, keepdims=True))
    return dx, d_sh, d_sc, d_gn


def _group_mean(v, gmat):
    return _dot(v.astype(BF16), gmat)


def _add_rows(ref, rows):
    for r, v in enumerate(rows):
        ref[r:r + 1, :] += v


def _window(ref, axis, j):
    return ref.at[(slice(None),) * axis + (j,)]


def _any_specs(n):
    return [pl.BlockSpec(memory_space=pl.ANY)] * n


def all_gather(blocks, axes, name):
    n_arr = len(blocks)

    def body(*refs):
        start, forward, finish = _gather_steps(refs[:n_arr], refs[n_arr:2 * n_arr], axes, *refs[2 * n_arr:])
        start()
        for j in range(3):
            forward(j)
        finish()

    return pl.pallas_call(
        body, name=name, out_shape=_gathered_shapes(blocks, axes),
        in_specs=_any_specs(n_arr), out_specs=_any_specs(n_arr), scratch_shapes=_gather_sems(n_arr),
    )(*blocks)


def all_gather_relayed(blocks, axes, name):
    n_arr = len(blocks)
    arrays = range(n_arr)

    def body(*refs):
        ins, outs = refs[:n_arr], refs[n_arr:2 * n_arr]
        send_sems, recv_sems, local_sems = refs[2 * n_arr:]
        x, y, c = lax.axis_index("x"), lax.axis_index("y"), lax.axis_index("c")
        sibling, x_nbr, y_nbr, diagonal = (x, y, 1 - c), (1 - x, y, c), (x, 1 - y, c), (1 - x, 1 - y, c)
        north = c == 1
        relay_slot = jnp.where(north, 1, 2)
        relay_from = tuple(jnp.where(north, a, b) for a, b in zip(x_nbr, y_nbr))
        relay_to = tuple(jnp.where(north, a, b) for a, b in zip(y_nbr, x_nbr))
        other_from = relay_to

        def slot(a, px, py, pc):
            return _window(outs[a], axes[a], 4 * px + 2 * py + pc)

        def copy(a, k, block, to, src=None):
            return pltpu.make_async_remote_copy(
                src_ref=slot(a, *block) if src is None else src, dst_ref=slot(a, *block),
                send_sem=send_sems.at[k, a], recv_sem=recv_sems.at[k, a], device_id=to, device_id_type=MESH_ID)

        mine = [pltpu.make_async_copy(ins[a], slot(a, x, y, c), local_sems.at[a]) for a in arrays]
        for cp in mine:
            cp.start()
        first = [copy(a, k, (x, y, c), to, src=ins[a])
                 for k, to in enumerate((sibling, x_nbr, y_nbr)) for a in arrays]
        for cp in first:
            cp.start()
        later = []
        for a in arrays:
            copy(a, relay_slot, relay_from, (x, y, c)).wait_recv()
            later += [copy(a, 3, relay_from, relay_to), copy(a, 3 + relay_slot, relay_from, sibling)]
            later[-2].start()
            later[-1].start()
        for a in arrays:
            copy(a, 3 - relay_slot, other_from, (x, y, c)).wait_recv()
            later.append(copy(a, 6 - relay_slot, other_from, sibling))
            later[-1].start()
        for a in arrays:
            copy(a, 3, diagonal, (x, y, c)).wait_recv()
            later.append(copy(a, 6, diagonal, sibling))
            later[-1].start()
        for a in arrays:
            for k, block in ((0, sibling), (4, (1 - x, y, 1 - c)), (5, (x, 1 - y, 1 - c)), (6, (1 - x, 1 - y, 1 - c))):
                copy(a, k, block, (x, y, c)).wait_recv()
        for cp in first + later:
            cp.wait_send()
        for cp in mine:
            cp.wait()

    return pl.pallas_call(
        body, name=name, out_shape=_gathered_shapes(blocks, axes),
        in_specs=_any_specs(n_arr), out_specs=_any_specs(n_arr), scratch_shapes=_gather_sems(n_arr),
    )(*blocks)


def _gathered_shapes(blocks, axes):
    return [jax.ShapeDtypeStruct(b.shape[:ax] + (N_DEV,) + b.shape[ax:], b.dtype) for b, ax in zip(blocks, axes)]


def _gather_sems(n_arr):
    return [pltpu.SemaphoreType.DMA((7, n_arr)), pltpu.SemaphoreType.DMA((7, n_arr)), pltpu.SemaphoreType.DMA((n_arr,))]


def _gather_steps(ins, outs, axes, send_sems, recv_sems, local_sems):
    arrays = range(len(ins))
    x, y, c = lax.axis_index("x"), lax.axis_index("y"), lax.axis_index("c")
    me, sibling = (x, y, c), (x, y, 1 - c)
    chips = [(1 - x, y), (x, 1 - y), (1 - x, 1 - y)]

    def slot(a, px, py, pc):
        return _window(outs[a], axes[a], 4 * px + 2 * py + pc)

    def copy(a, k, block, to, src=None):
        return pltpu.make_async_remote_copy(
            src_ref=slot(a, *block) if src is None else src, dst_ref=slot(a, *block),
            send_sem=send_sems.at[k, a], recv_sem=recv_sems.at[k, a], device_id=to, device_id_type=MESH_ID)

    def mine(a):
        return pltpu.make_async_copy(ins[a], slot(a, *me), local_sems.at[a])

    def first():
        return ([copy(a, 0, me, sibling, src=ins[a]) for a in arrays]
                + [copy(a, 1 + j, me, (*chip, c), src=ins[a]) for j, chip in enumerate(chips) for a in arrays])

    def passed(j):
        return [copy(a, 4 + j, (*chips[j], c), sibling) for a in arrays]

    def start():
        for a in arrays:
            mine(a).start()
        for cp in first():
            cp.start()

    def forward(j):
        for a, cp in zip(arrays, passed(j)):
            copy(a, 1 + j, (*chips[j], c), me).wait_recv()
            cp.start()

    def finish():
        for a in arrays:
            copy(a, 0, sibling, me).wait_recv()
        for j, chip in enumerate(chips):
            for a in arrays:
                copy(a, 4 + j, (*chip, 1 - c), me).wait_recv()
        for cp in first() + passed(0) + passed(1) + passed(2):
            cp.wait_send()
        for a in arrays:
            mine(a).wait()

    return start, forward, finish


def exchange_sibling(grads, name):
    n_arr = len(grads)

    def body(*refs):
        start, finish = _sibling_exchange_steps(refs[:n_arr], refs[n_arr:2 * n_arr], *refs[2 * n_arr:])
        start()
        finish()

    return pl.pallas_call(
        body, name=name, out_shape=_sibling_shapes(grads),
        in_specs=_any_specs(n_arr), out_specs=_any_specs(n_arr), scratch_shapes=_exchange_sems(n_arr),
    )(*grads)


def _sibling_shapes(grads):
    return [jax.ShapeDtypeStruct((4,) + g.shape[1:], g.dtype) for g in grads]


def _exchange_sems(n_arr):
    return [pltpu.SemaphoreType.DMA((n_arr,)), pltpu.SemaphoreType.DMA((n_arr,))]


def _sibling_exchange_steps(ins, outs, send_sems, recv_sems):
    x, y, c = lax.axis_index("x"), lax.axis_index("y"), lax.axis_index("c")

    def copy(a, src, dst):
        return pltpu.make_async_remote_copy(
            src_ref=src, dst_ref=dst, send_sem=send_sems.at[a], recv_sem=recv_sems.at[a],
            device_id=(x, y, 1 - c), device_id_type=MESH_ID)

    def start():
        for a in range(len(ins)):
            for k in range(4):
                copy(a, ins[a].at[2 * k + (1 - c)], outs[a].at[k]).start()

    def finish():
        whole = [copy(a, ins[a].at[pl.ds(0, 4)], outs[a]) for a in range(len(ins))]
        for cp in whole:
            cp.wait_recv()
        for cp in whole:
            cp.wait_send()

    return start, finish


def _chip_exchange_steps(ins, outs, send_sems, recv_sems):
    x, y, c = lax.axis_index("x"), lax.axis_index("y"), lax.axis_index("c")
    chips = [(1 - x, y), (x, 1 - y), (1 - x, 1 - y)]

    def copy(a, src, dst, chip):
        return pltpu.make_async_remote_copy(
            src_ref=src, dst_ref=dst, send_sem=send_sems.at[a], recv_sem=recv_sems.at[a],
            device_id=(*chip, c), device_id_type=MESH_ID)

    def start():
        for a in range(len(ins)):
            for j, chip in enumerate(chips):
                copy(a, ins[a].at[j], outs[a].at[j], chip).start()

    def finish():
        whole = [copy(a, ins[a], outs[a], chips[0]) for a in range(len(ins))]
        for cp in whole:
            cp.wait_recv()
        for cp in whole:
            cp.wait_send()

    return start, finish


def riding_gather(blocks, axes):
    def phases(ins, outs, *sems):
        start, forward, finish = _gather_steps(ins, outs, axes, *sems)
        return [start] + [functools.partial(forward, j) for j in range(3)] + [finish]

    return dict(operands=blocks, out_shape=_gathered_shapes(blocks, axes), sems=_gather_sems(len(blocks)),
                phases=phases, when=("first", "late0", "late1", "late2", "last"))


def riding_exchange(parts):
    def phases(ins, outs, *sems):
        return list(_chip_exchange_steps(ins, outs, *sems))

    return dict(operands=parts, out_shape=[jax.ShapeDtypeStruct(p.shape, p.dtype) for p in parts],
                sems=_exchange_sems(len(parts)), phases=phases, when=("first", "last"))


def riding_sibling(grads):
    def phases(ins, outs, *sems):
        return list(_sibling_exchange_steps(ins, outs, *sems))

    return dict(operands=grads, out_shape=_sibling_shapes(grads), sems=_exchange_sems(len(grads)),
                phases=phases, when=("first", "last"))


def _call_with_rider(body, rider, *, name, grid, in_specs, out_specs, out_shape, scratch_shapes, operands):
    params = _params(("arbitrary",) * len(grid))
    if rider is None:
        return pl.pallas_call(body, name=name, grid=grid, in_specs=in_specs, out_specs=out_specs,
                              out_shape=out_shape, scratch_shapes=scratch_shapes, compiler_params=params)(*operands)
    n_in, n_out, n_scr, k = len(in_specs), len(out_specs), len(scratch_shapes), len(rider["operands"])
    at = {"first": (0,) * len(grid), "last": tuple(g - 1 for g in grid)}
    if "late0" in rider["when"]:
        rows, cols = grid
        assert cols >= 3
        at.update({"late%d" % j: (max(rows - 2, 0), j) for j in range(3)})

    def wrapped(*refs):
        ins, c_in = refs[:n_in], refs[n_in:n_in + k]
        outs, c_out = refs[n_in + k:n_in + k + n_out], refs[n_in + k + n_out:n_in + 2 * k + n_out]
        scratch, sems = refs[n_in + 2 * k + n_out:n_in + 2 * k + n_out + n_scr], refs[n_in + 2 * k + n_out + n_scr:]
        pos = [pl.program_id(axis) for axis in range(len(grid))]

        def here(key):
            return functools.reduce(jnp.logical_and, [p == v for p, v in zip(pos, at[key])])

        phases = rider["phases"](c_in, c_out, *sems)
        for fn, key in zip(phases, rider["when"]):
            if key != "last":
                pl.when(here(key))(fn)
        body(*ins, *outs, *scratch)
        pl.when(here("last"))(phases[-1])

    return pl.pallas_call(
        wrapped, name=name, grid=grid,
        in_specs=list(in_specs) + _any_specs(k), out_specs=list(out_specs) + _any_specs(k),
        out_shape=list(out_shape) + rider["out_shape"], scratch_shapes=list(scratch_shapes) + rider["sems"],
        compiler_params=params)(*operands, *rider["operands"])


def add_sibling(g8, got, src_idx, chip_idx, name):
    _, r, n = g8.shape
    tr = _tile(r, SUM_ROWS, 16)

    def body(si_ref, ci_ref, g0_ref, g1_ref, g2_ref, g3_ref, got_ref, own_ref, send_ref):
        own_ref[...] = g0_ref[0] + got_ref[ci_ref[0]]
        for j, g_ref in enumerate((g1_ref, g2_ref, g3_ref)):
            send_ref[j] = (g_ref[0] + got_ref[ci_ref[j + 1]]).astype(BF16)

    def mine(j):
        return pl.BlockSpec((1, tr, n), lambda i, si, ci: (si[j], i, 0))

    return pl.pallas_call(
        body, name=name,
        out_shape=[jax.ShapeDtypeStruct((r, n), F32), jax.ShapeDtypeStruct((3, r, n), BF16)],
        grid_spec=pltpu.PrefetchScalarGridSpec(
            num_scalar_prefetch=2, grid=(r // tr,),
            in_specs=[mine(0), mine(1), mine(2), mine(3), pl.BlockSpec((4, tr, n), lambda i, si, ci: (0, i, 0))],
            out_specs=[pl.BlockSpec((tr, n), lambda i, si, ci: (i, 0)),
                       pl.BlockSpec((3, tr, n), lambda i, si, ci: (0, i, 0))]),
        compiler_params=_params(("arbitrary",)),
    )(src_idx, chip_idx, g8, g8, g8, g8, got)


def sum_devices(g):
    def body(g_ref, o_ref):
        acc = g_ref[0]
        for j in range(1, N_DEV):
            acc = acc + g_ref[j]
        o_ref[...] = acc

    return pl.pallas_call(body, name="sum_devices", out_shape=jax.ShapeDtypeStruct(g.shape[1:], F32))(g)


def sum_lanes(v):
    def body(v_ref, o_ref):
        o_ref[...] = jnp.broadcast_to(jnp.sum(v_ref[...], axis=-1, keepdims=True), (1, LANES))

    return pl.pallas_call(body, name="sum_lanes", out_shape=jax.ShapeDtypeStruct((1, LANES), F32))(v)


def ada_forward(c_all, ada_w, ada_b_cols):
    nb, n = c_all.shape[0], ada_w.shape[1]

    def body(c_ref, w_ref, b_ref, o_ref):
        cv = c_ref[...]
        s = (cv * jax.nn.sigmoid(cv)).astype(BF16)
        o_ref[...] = _dot(s, w_ref[...].astype(BF16)) + b_ref[...]

    return pl.pallas_call(body, name="ada_fwd", out_shape=jax.ShapeDtypeStruct((nb, n), F32),
                          compiler_params=_params())(c_all, ada_w, ada_b_cols)


def ada_backward(c_all16, dmod16):
    d, n = c_all16.shape[1], dmod16.shape[1]

    def body(c_ref, g_ref, o_ref):
        cv = c_ref[...]
        s = (cv * jax.nn.sigmoid(cv)).astype(BF16)
        o_ref[...] = _dot(s, g_ref[...].astype(BF16), TN)

    return pl.pallas_call(body, name="ada_bwd", out_shape=jax.ShapeDtypeStruct((d, n), F32),
                          compiler_params=_params())(c_all16, dmod16)


def ffn_forward(x, gn, sc, sh, gate, ws, name, rider=None, loss_head=None):
    t, d = x.shape
    f = ws[0].shape[0]
    tm, tf = _tile(t, FFN_FWD_TILE[0], 16), _tile(f, FFN_FWD_TILE[1])
    nf = f // tf
    n_in = 5 if loss_head is None else 7

    def body(*refs):
        x_ref, gn_ref, sc_ref, sh_ref, gate_ref = refs[:5]
        w1_ref, w3_ref, w2_ref, xo_ref, h_ref, a_ref, b_ref, y_ref = refs[n_in:n_in + 8]
        hs, acc = refs[-2:]
        i, j = pl.program_id(0), pl.program_id(1)

        if loss_head is not None:
            @pl.when(jnp.logical_and(i == 0, j == 0))
            def _():
                refs[n_in + 9][...] = jnp.zeros_like(refs[n_in + 9])

        @pl.when(j == 0)
        def _():
            xhat, _ = _rms(x_ref[...])
            h = (xhat * gn_ref[...] * (1.0 + sc_ref[...]) + sh_ref[...]).astype(BF16)
            hs[...] = h
            h_ref[...] = h
            acc[...] = jnp.zeros_like(acc)

        h = hs[...]
        a = _dot(h, w1_ref[...], NT)
        b = _dot(h, w3_ref[...], NT)
        a_ref[...] = a.astype(BF16)
        b_ref[...] = b.astype(BF16)
        u = (a * _sigmoid(a) * b).astype(BF16)
        acc[...] += _dot(u, w2_ref[...])

        @pl.when(j == nf - 1)
        def _():
            y = acc[...]
            y_ref[...] = y.astype(BF16)
            x_out = x_ref[...] + 0.5 * gate_ref[...] * y
            if loss_head is None:
                xo_ref[...] = x_out
            else:
                dx, d_g, loss = _loss_head(x_out, refs[5][...], refs[6][...])
                xo_ref[...] = dx
                refs[n_in + 8][...] = (0.5 * gate_ref[...] * dx).astype(BF16)
                _add_rows(refs[n_in + 9], [d_g, loss])

    row = pl.BlockSpec((tm, d), lambda i, j: (i, 0))
    vec = pl.BlockSpec((1, d), lambda i, j: (0, 0))
    wide = pl.BlockSpec((tm, tf), lambda i, j: (i, j))
    head = loss_head is not None
    return _call_with_rider(
        body, rider, name=name, grid=(t // tm, nf),
        in_specs=[row, vec, vec, vec, vec] + ([row, vec] if head else [])
        + [pl.BlockSpec((tf, d), lambda i, j: (j, 0))] * 3,
        out_specs=[row, row, wide, wide, row] + ([row, pl.BlockSpec((8, d), lambda i, j: (0, 0))] if head else []),
        out_shape=[jax.ShapeDtypeStruct((t, d), F32), jax.ShapeDtypeStruct((t, d), BF16),
                   jax.ShapeDtypeStruct((t, f), BF16), jax.ShapeDtypeStruct((t, f), BF16),
                   jax.ShapeDtypeStruct((t, d), BF16)]
        + ([jax.ShapeDtypeStruct((t, d), BF16), jax.ShapeDtypeStruct((8, d), F32)] if head else []),
        scratch_shapes=[pltpu.VMEM((tm, d), BF16), pltpu.VMEM((tm, d), F32)],
        operands=(x, gn, sc, sh, gate) + (tuple(loss_head) if head else ()) + tuple(ws))


def _loss_head(x, target, g):
    d = x.shape[-1]
    xhat, r = _rms(x)
    err = xhat * g - target
    dyf = err * (1.0 / d)
    dxh = dyf * g
    dx = r * (dxh - xhat * jnp.mean(dxh * xhat, axis=-1, keepdims=True))
    return dx, jnp.sum(dyf * xhat, axis=0, keepdims=True), jnp.sum(err * err, axis=0, keepdims=True) * (0.5 / d)


def ffn_backward_gate(dy, a, b, w2, name, rider=None):
    t, d = dy.shape
    f = w2.shape[0]
    tm, tf = _tile(t, FFN_BWD_TILE[0], 16), _tile(f, FFN_BWD_TILE[1])
    nf = f // tf

    def gate_body(dy_ref, a_ref, b_ref, w2_ref, da_ref, db_ref, gw2_ref):
        dy_v = dy_ref[...]
        du = _dot(dy_v, w2_ref[...], NT)
        av = a_ref[...].astype(F32)
        bv = b_ref[...].astype(F32)
        s = _sigmoid(av)
        sa = av * s
        da_ref[...] = (du * bv * (s + sa * (1.0 - s))).astype(BF16)
        db_ref[...] = (du * sa).astype(BF16)
        part = _dot((sa * bv).astype(BF16), dy_v, TN)

        @pl.when(pl.program_id(1) == 0)
        def _():
            gw2_ref[...] = part

        @pl.when(pl.program_id(1) > 0)
        def _():
            gw2_ref[...] += part

    hidden = jax.ShapeDtypeStruct((t, f), BF16)
    wide_t = pl.BlockSpec((tm, tf), lambda j, i: (i, j))
    return _call_with_rider(
        gate_body, rider, name=name, grid=(nf, t // tm),
        in_specs=[pl.BlockSpec((tm, d), lambda j, i: (i, 0)), wide_t, wide_t,
                  pl.BlockSpec((tf, d), lambda j, i: (j, 0))],
        out_specs=[wide_t, wide_t, pl.BlockSpec((tf, d), lambda j, i: (j, 0))],
        out_shape=[hidden, hidden, jax.ShapeDtypeStruct((f, d), F32)],
        scratch_shapes=[], operands=(dy, a, b, w2))


def ffn_backward_norm(da, db, dxo, x, y, gn, sc, w1t, w3t, name, rider=None):
    t, d = x.shape
    f = w1t.shape[0]
    tm, tf = _tile(t, FFN_BWD_TILE[0], 16), _tile(f, FFN_BWD_TILE[1])
    nf = f // tf
    row = pl.BlockSpec((tm, d), lambda i, j: (i, 0))
    vec = pl.BlockSpec((1, d), lambda i, j: (0, 0))
    wide = pl.BlockSpec((tm, tf), lambda i, j: (i, j))

    def norm_body(da_ref, db_ref, w1_ref, w3_ref, dxo_ref, x_ref, y_ref, gn_ref, sc_ref, dx_ref, sums_ref, acc):
        i, j = pl.program_id(0), pl.program_id(1)

        @pl.when(jnp.logical_and(i == 0, j == 0))
        def _():
            sums_ref[...] = jnp.zeros_like(sums_ref)

        part = _dot(da_ref[...], w1_ref[...]) + _dot(db_ref[...], w3_ref[...])

        @pl.when(j == 0)
        def _():
            acc[...] = part

        @pl.when(jnp.logical_and(j > 0, j < nf - 1))
        def _():
            acc[...] += part

        @pl.when(j == nf - 1)
        def _():
            dh = part if nf == 1 else acc[...] + part
            dxo_v = dxo_ref[...]
            dx, d_sh, d_sc, d_gn = _norm_mod_bwd(dh, x_ref[...], gn_ref[...], sc_ref[...])
            dx_ref[...] = dxo_v + dx
            d_gate = jnp.sum(dxo_v * (0.5 * y_ref[...].astype(F32)), axis=0, keepdims=True)
            _add_rows(sums_ref, [d_sh, d_sc, d_gate, d_gn])

    w_spec = pl.BlockSpec((tf, d), lambda i, j: (j, 0))
    return _call_with_rider(
        norm_body, rider, name=name, grid=(t // tm, nf),
        in_specs=[wide, wide, w_spec, w_spec, row, row, row, vec, vec],
        out_specs=[row, pl.BlockSpec((8, d), lambda i, j: (0, 0))],
        out_shape=[jax.ShapeDtypeStruct((t, d), F32), jax.ShapeDtypeStruct((8, d), F32)],
        scratch_shapes=[pltpu.VMEM((tm, d), F32)],
        operands=(da, db, w1t, w3t, dxo, x, y, gn, sc))


def matmul_tn(a, b, name, rider=None):
    parts = list(a) if isinstance(a, (list, tuple)) else [a]
    t, n = b.shape
    widths = [p.shape[1] for p in parts]
    tm = _tile(functools.reduce(math.gcd, widths), GRAD_TILE)
    tn, tk = _tile(n, GRAD_TILE), _tile(t, GRAD_DEPTH, 16)
    nk = t // tk
    counts = [w // tm for w in widths]
    firsts = [sum(counts[:p]) for p in range(len(parts))]

    def body(*refs):
        a_refs, (b_ref, o_ref, acc) = refs[:len(parts)], refs[len(parts):]
        i, k = pl.program_id(0), pl.program_id(2)

        @pl.when(k == 0)
        def _():
            acc[...] = jnp.zeros_like(acc)

        for a_ref, lo, cnt in zip(a_refs, firsts, counts):
            def accumulate(a_ref=a_ref):
                acc[...] += _dot(a_ref[...], b_ref[...], TN)

            if len(parts) == 1:
                accumulate()
            else:
                pl.when(jnp.logical_and(i >= lo, i < lo + cnt))(accumulate)

        @pl.when(k == nk - 1)
        def _():
            o_ref[...] = acc[...]

    def part_spec(lo, cnt):
        if len(parts) == 1:
            return pl.BlockSpec((tk, tm), lambda i, j, k: (k, i))

        def index(i, j, k):
            mine = jnp.logical_and(i >= lo, i < lo + cnt)
            return jnp.where(mine, k, 0), jnp.clip(i - lo, 0, cnt - 1)
        return pl.BlockSpec((tk, tm), index)

    out = _call_with_rider(
        body, rider, name=name, grid=(sum(counts), n // tn, nk),
        in_specs=[part_spec(lo, cnt) for lo, cnt in zip(firsts, counts)]
        + [pl.BlockSpec((tk, tn), lambda i, j, k: (k, j))],
        out_specs=[pl.BlockSpec((tm, tn), lambda i, j, k: (i, j))],
        out_shape=[jax.ShapeDtypeStruct((sum(widths), n), F32)],
        scratch_shapes=[pltpu.VMEM((tm, tn), F32)], operands=(*parts, b))
    return out[0] if rider is None else out


def mix_in_forward(x, gn, sc, sh, w_in):
    t, d = x.shape
    tm = _tile(t, ROW_TILE, 16)

    def body(x_ref, gn_ref, sc_ref, sh_ref, w_ref, h_ref, zc_ref, zm_ref):
        xhat, _ = _rms(x_ref[...])
        h = (xhat * gn_ref[...] * (1.0 + sc_ref[...]) + sh_ref[...]).astype(BF16)
        h_ref[...] = h
        z = _dot(h, w_ref[...], NT)
        zc_ref[...] = z[:, :ZC_COLS].astype(BF16)
        zm_ref[...] = z[:, ZC_COLS:].astype(BF16)

    row = pl.BlockSpec((tm, d), lambda i: (i, 0))
    vec = pl.BlockSpec((1, d), lambda i: (0, 0))
    return pl.pallas_call(
        body, name="mix_in_fwd", grid=(t // tm,),
        in_specs=[row, vec, vec, vec, _row(w_in)],
        out_specs=[row, pl.BlockSpec((tm, ZC_COLS), lambda i: (i, 0)), pl.BlockSpec((tm, ZM_COLS), lambda i: (i, 0))],
        out_shape=[jax.ShapeDtypeStruct((t, d), BF16), jax.ShapeDtypeStruct((t, ZC_COLS), BF16),
                   jax.ShapeDtypeStruct((t, ZM_COLS), BF16)],
        compiler_params=_params(("arbitrary",)),
    )(x, gn, sc, sh, w_in)


def _rope_tables(pos, inv_freq):
    ang = pos * inv_freq
    lane = lax.broadcasted_iota(jnp.int32, ang.shape, 1)
    cos, sin = jnp.cos(ang), jnp.sin(ang)
    half = QK_ROPE // 2
    return cos, jnp.where(lane < half, -sin, 0.0), jnp.where(jnp.logical_and(lane >= half, lane < QK_ROPE), sin, 0.0)


def _rope(v, tables):
    cos, sin_a, sin_b = tables
    return v * cos + pltpu.roll(v, LANES - QK_ROPE // 2, 1) * sin_a + pltpu.roll(v, QK_ROPE // 2, 1) * sin_b


def _rope_transposed(dv, tables):
    cos, sin_a, sin_b = tables
    return dv * cos + pltpu.roll(dv * sin_a, QK_ROPE // 2, 1) + pltpu.roll(dv * sin_b, LANES - QK_ROPE // 2, 1)


def mla_project(zm, pos, inv_freq, qg, kvg, w_uq, w_ukv):
    t = zm.shape[0]
    tm = _tile(t, ROW_TILE, 16)

    def body(zm_ref, pos_ref, if_ref, qg_ref, kvg_ref, wq_ref, wkv_ref, qn_ref, kvn_ref, q_ref, k_ref, v_ref):
        zv = zm_ref[...].astype(F32)
        qn = (_rms(zv[:, :Q_LORA])[0] * qg_ref[...]).astype(BF16)
        kvn = (_rms(zv[:, Q_LORA:Q_LORA + KV_LORA])[0] * kvg_ref[...]).astype(BF16)
        qn_ref[...] = qn
        kvn_ref[...] = kvn
        qf = _dot(qn, wq_ref[...], NT) * QK_FOLD
        kvf = _dot(kvn, wkv_ref[...], NT)
        tables = _rope_tables(pos_ref[...], if_ref[...])
        kr = _rope(zv[:, Q_LORA + KV_LORA:], tables).astype(BF16)
        for h in range(MLA_HEADS):
            lo = h * HEAD_PAD
            q_ref[:, lo:lo + QK_NOPE] = qf[:, lo:lo + QK_NOPE].astype(BF16)
            q_ref[:, lo + QK_NOPE:lo + HEAD_PAD] = _rope(qf[:, lo + QK_NOPE:lo + HEAD_PAD], tables).astype(BF16)
            k_ref[:, lo:lo + QK_NOPE] = kvf[:, h * QK_NOPE:(h + 1) * QK_NOPE].astype(BF16)
            k_ref[:, lo + QK_NOPE:lo + HEAD_PAD] = kr
        v_ref[...] = kvf[:, MLA_HEADS * QK_NOPE:].astype(BF16)

    def rows(n):
        return pl.BlockSpec((tm, n), lambda i: (i, 0))

    return pl.pallas_call(
        body, name="mla_project", grid=(t // tm,),
        in_specs=[rows(ZM_COLS), rows(1), _row(inv_freq), _row(qg), _row(kvg), _row(w_uq), _row(w_ukv)],
        out_specs=[rows(Q_LORA), rows(KV_LORA), rows(QK_COLS), rows(QK_COLS), rows(MLA_WIDTH)],
        out_shape=[jax.ShapeDtypeStruct((t, Q_LORA), BF16), jax.ShapeDtypeStruct((t, KV_LORA), BF16),
                   jax.ShapeDtypeStruct((t, QK_COLS), BF16), jax.ShapeDtypeStruct((t, QK_COLS), BF16),
                   jax.ShapeDtypeStruct((t, MLA_WIDTH), BF16)],
        compiler_params=_params(("arbitrary",)),
    )(zm, pos, inv_freq, qg, kvg, w_uq, w_ukv)


def _chunk_mask(shape, q_axis):
    qi = lax.broadcasted_iota(jnp.int32, shape, q_axis) // CHUNK
    ki = lax.broadcasted_iota(jnp.int32, shape, 1 - q_axis) // CHUNK
    return ki <= qi


def attention_forward(q, k, v, rider=None):
    t = q.shape[0]
    tq = _tile(t, ATTN_TILE, CHUNK)

    def body(q_ref, k_ref, v_ref, o_ref, lse_ref):
        i = pl.program_id(1)
        qv = q_ref[...]

        def step(kb, carry, masked, tiles=1):
            m, l, acc = carry
            keys = pl.ds(pl.multiple_of(kb * tq, tq), tiles * tq)
            s = _dot(qv, k_ref[keys, :], NT)
            if masked:
                s = jnp.where(_chunk_mask(s.shape, 0), s, NEG_INF)
            m_new = jnp.maximum(m, jnp.max(s, axis=-1, keepdims=True))
            alpha = jnp.exp2(m - m_new)
            p = jnp.exp2(s - m_new)
            l = alpha * l + jnp.sum(p, axis=-1, keepdims=True)
            acc = alpha * acc + _dot(p.astype(BF16), v_ref[keys, :])
            return m_new, l, acc

        init = (jnp.full((tq, 1), NEG_INF, F32), jnp.zeros((tq, 1), F32), jnp.zeros((tq, V_HEAD), F32))
        carry = lax.fori_loop(0, i // 2, lambda pb, cr: step(2 * pb, cr, False, 2), init)
        carry = lax.fori_loop(0, i % 2, lambda _, cr: step(i - 1, cr, False), carry)
        m, l, acc = step(i, carry, True)
        o_ref[...] = (acc / l).astype(BF16)
        lse_ref[0] = m + jnp.log2(l)

    return _call_with_rider(
        body, rider, name="attn_fwd", grid=(MLA_HEADS, t // tq),
        in_specs=[pl.BlockSpec((tq, HEAD_PAD), lambda h, i: (i, h)),
                  pl.BlockSpec((t, HEAD_PAD), lambda h, i: (0, h)),
                  pl.BlockSpec((t, V_HEAD), lambda h, i: (0, h))],
        out_specs=[pl.BlockSpec((tq, V_HEAD), lambda h, i: (i, h)),
                   pl.BlockSpec((1, tq, 1), lambda h, i: (h, i, 0))],
        out_shape=[jax.ShapeDtypeStruct((t, MLA_WIDTH), BF16), jax.ShapeDtypeStruct((MLA_HEADS, t, 1), F32)],
        scratch_shapes=[], operands=(q, k, v))


def attention_backward(q, k, v, do, lse, delta, rider=None):
    t = q.shape[0]
    tq = _tile(t, ATTN_TILE, CHUNK)
    nq = t // tq

    def body(q_ref, k_ref, v_ref, do_ref, lse_ref, delta_ref, dq_ref, dk_ref, dv_ref, dq_acc):
        kb = pl.program_id(1)

        @pl.when(kb == 0)
        def _():
            dq_acc[...] = jnp.zeros_like(dq_acc)

        kv, vv = k_ref[...], v_ref[...]

        def step(qb, carry, masked):
            dk, dv = carry
            rows = pl.ds(pl.multiple_of(qb * tq, tq), tq)
            qv, dov = q_ref[rows, :], do_ref[rows, :]
            s = _dot(kv, qv, NT)
            if masked:
                s = jnp.where(_chunk_mask(s.shape, 1), s, NEG_INF)
            p = jnp.exp2(s - lse_ref[0, qb])
            dv = dv + _dot(p.astype(BF16), dov)
            dp = _dot(vv, dov, NT)
            ds = (p * (dp - delta_ref[0, qb]) * LN_2).astype(BF16)
            dk = dk + _dot(ds, qv)
            dq_acc[rows, :] += _dot(ds, kv, TN)
            return dk, dv

        carry = step(kb, (jnp.zeros((tq, HEAD_PAD), F32), jnp.zeros((tq, V_HEAD), F32)), True)
        odd = (nq - 1 - kb) % 2
        carry = lax.fori_loop(0, odd, lambda _, cr: step(kb + 1, cr, False), carry)
        first = kb + 1 + odd
        dk, dv = lax.fori_loop(0, (nq - first) // 2,
                               lambda pb, cr: step(first + 2 * pb + 1, step(first + 2 * pb, cr, False), False), carry)
        dk_ref[...] = dk.astype(BF16)
        dv_ref[...] = dv.astype(BF16)

        @pl.when(kb == nq - 1)
        def _():
            dq_ref[...] = dq_acc[...].astype(BF16)

    stat = pl.BlockSpec((1, nq, 1, tq), lambda h, j: (h, 0, 0, 0))
    return _call_with_rider(
        body, rider, name="attn_bwd", grid=(MLA_HEADS, nq),
        in_specs=[pl.BlockSpec((t, HEAD_PAD), lambda h, j: (0, h)),
                  pl.BlockSpec((tq, HEAD_PAD), lambda h, j: (j, h)),
                  pl.BlockSpec((tq, V_HEAD), lambda h, j: (j, h)),
                  pl.BlockSpec((t, V_HEAD), lambda h, j: (0, h)), stat, stat],
        out_specs=[pl.BlockSpec((t, HEAD_PAD), lambda h, j: (0, h)),
                   pl.BlockSpec((tq, HEAD_PAD), lambda h, j: (j, h)),
                   pl.BlockSpec((tq, V_HEAD), lambda h, j: (j, h))],
        out_shape=[jax.ShapeDtypeStruct((t, QK_COLS), BF16), jax.ShapeDtypeStruct((t, QK_COLS), BF16),
                   jax.ShapeDtypeStruct((t, MLA_WIDTH), BF16)],
        scratch_shapes=[pltpu.VMEM((t, HEAD_PAD), F32)], operands=(q, k, v, do, lse, delta))


HALO = 16


def _halo_spec(tm, n, step, last):
    return pl.BlockSpec((HALO, n), lambda i: (jnp.clip(i * (tm // HALO) + step, 0, last), 0))


def _shift_rows(v, prev, n):
    out = pltpu.roll(v, n, 0)
    row = lax.broadcasted_iota(jnp.int32, v.shape, 0)
    for r in range(n):
        out = jnp.where(row == r, prev[HALO - n + r:HALO - n + r + 1, :], out)
    return out


def _advance_rows(v, nxt, n):
    rows = v.shape[0]
    out = pltpu.roll(v, rows - n, 0)
    row = lax.broadcasted_iota(jnp.int32, v.shape, 0)
    for r in range(n):
        out = jnp.where(row == rows - n + r, nxt[r:r + 1, :], out)
    return out


def _conv_taps(zc, zc_prev, first):
    w = CONV_WIDTH
    u = zc[:, w:2 * w] * zc[:, 2 * w:]
    up = jnp.where(first, 0.0, zc_prev[:, w:2 * w] * zc_prev[:, 2 * w:])
    return u, _shift_rows(u, up, 1), _shift_rows(u, up, 2)


def mix_out_forward(zc, o, conv_w, og, gmat_a, gmat_b, w_out, x, gate):
    t, d = x.shape
    tm = _tile(t, ROW_TILE, 16)
    w = CONV_WIDTH

    def body(zc_ref, zp_ref, o_ref, cw_ref, og_ref, ga_ref, gb_ref, w_ref, x_ref, gate_ref,
             xo_ref, yn_ref, y_ref, ya_ref):
        zc_v = zc_ref[...].astype(F32)
        u, u1, u2 = _conv_taps(zc_v, zp_ref[...].astype(F32), pl.program_id(0) == 0)
        cw = cw_ref[...]
        ya = zc_v[:, :w] * (cw[0:1] * u2 + cw[1:2] * u1 + cw[2:3] * u)
        ya_ref[...] = ya.astype(BF16)
        ov = o_ref[...].astype(F32)
        ogv = og_ref[...]
        yn_ref[:, :w] = (ya * lax.rsqrt(_group_mean(ya * ya, ga_ref[...]) + EPS) * ogv[:, :w]).astype(BF16)
        yn_ref[:, w:] = (ov * lax.rsqrt(_group_mean(ov * ov, gb_ref[...]) + EPS) * ogv[:, w:]).astype(BF16)
        y = _dot(yn_ref[...], w_ref[...])
        y_ref[...] = y.astype(BF16)
        xo_ref[...] = x_ref[...] + gate_ref[...] * y

    def rows(n):
        return pl.BlockSpec((tm, n), lambda i: (i, 0))

    return pl.pallas_call(
        body, name="mix_out_fwd", grid=(t // tm,),
        in_specs=[rows(ZC_COLS), _halo_spec(tm, ZC_COLS, -1, t // HALO - 1), rows(MLA_WIDTH), _row(conv_w), _row(og),
                  _row(gmat_a), _row(gmat_b), _row(w_out), rows(d), _row(gate)],
        out_specs=[rows(d), rows(MIX_WIDTH), rows(d), rows(w)],
        out_shape=[jax.ShapeDtypeStruct((t, d), F32), jax.ShapeDtypeStruct((t, MIX_WIDTH), BF16),
                   jax.ShapeDtypeStruct((t, d), BF16), jax.ShapeDtypeStruct((t, w), BF16)],
        compiler_params=_params(("arbitrary",)),
    )(zc, zc, o, conv_w, og, gmat_a, gmat_b, w_out, x, gate)


def _group_norm_bwd(dyn, y, og, gmat):
    rs = lax.rsqrt(_group_mean(y * y, gmat) + EPS)
    yhat = y * rs
    d_og = jnp.sum(dyn * yhat, axis=0, keepdims=True)
    dyh = dyn * og
    return rs * (dyh - yhat * _group_mean(dyh * yhat, gmat)), d_og


def mix_out_backward(dxo, y, gate, ya, o, yn, og, gmat_a, gmat_b, w_out, rider=None):
    t, d = dxo.shape
    tm = _tile(t, ROW_TILE, 16)
    w = CONV_WIDTH

    def body(dxo_ref, y_ref, gate_ref, ya_ref, o_ref, yn_ref, og_ref, ga_ref, gb_ref, w_ref,
             dya_ref, do_ref, delta_ref, sd_ref, so_ref, gw_ref):
        @pl.when(pl.program_id(0) == 0)
        def _():
            sd_ref[...] = jnp.zeros_like(sd_ref)
            so_ref[...] = jnp.zeros_like(so_ref)
            gw_ref[...] = jnp.zeros_like(gw_ref)

        dxo_v = dxo_ref[...]
        dy = (gate_ref[...] * dxo_v).astype(BF16)
        gw_ref[...] += _dot(yn_ref[...], dy, TN)
        sd_ref[0:1, :] += jnp.sum(dxo_v * y_ref[...].astype(F32), axis=0, keepdims=True)
        dyn = _dot(dy, w_ref[...], NT)
        ogv = og_ref[...]
        ov = o_ref[...].astype(F32)
        dya, d_og_a = _group_norm_bwd(dyn[:, :w], ya_ref[...].astype(F32), ogv[:, :w], ga_ref[...])
        dov, d_og_b = _group_norm_bwd(dyn[:, w:], ov, ogv[:, w:], gb_ref[...])
        dya_ref[...] = dya.astype(BF16)
        do_ref[...] = dov.astype(BF16)
        so_ref[0:1, :w] += d_og_a
        so_ref[0:1, w:] += d_og_b
        prod = dov * ov
        for h in range(MLA_HEADS):
            delta_ref[h] = jnp.sum(prod[:, h * V_HEAD:(h + 1) * V_HEAD], axis=-1, keepdims=True)

    def rows(n):
        return pl.BlockSpec((tm, n), lambda i: (i, 0))

    return _call_with_rider(
        body, rider, name="mix_out_bwd", grid=(t // tm,),
        in_specs=[rows(d), rows(d), _row(gate), rows(w), rows(MLA_WIDTH), rows(MIX_WIDTH), _row(og), _row(gmat_a),
                  _row(gmat_b), _row(w_out)],
        out_specs=[rows(w), rows(MLA_WIDTH), pl.BlockSpec((MLA_HEADS, tm, 1), lambda i: (0, i, 0)),
                   pl.BlockSpec((8, d), lambda i: (0, 0)), pl.BlockSpec((8, MIX_WIDTH), lambda i: (0, 0)),
                   pl.BlockSpec((MIX_WIDTH, d), lambda i: (0, 0))],
        out_shape=[jax.ShapeDtypeStruct((t, w), BF16),
                   jax.ShapeDtypeStruct((t, MLA_WIDTH), BF16), jax.ShapeDtypeStruct((MLA_HEADS, t, 1), F32),
                   jax.ShapeDtypeStruct((8, d), F32), jax.ShapeDtypeStruct((8, MIX_WIDTH), F32),
                   jax.ShapeDtypeStruct((MIX_WIDTH, d), F32)],
        scratch_shapes=[], operands=(dxo, y, gate, ya, o, yn, og, gmat_a, gmat_b, w_out))


def conv_backward(zc, dya, conv_w):
    t = zc.shape[0]
    tm = _tile(t, ROW_TILE, 16)
    nt = t // tm
    w = CONV_WIDTH

    def body(zc_ref, zp_ref, zn_ref, dya_ref, dn_ref, cw_ref, dzc_ref, sums_ref):
        i = pl.program_id(0)

        @pl.when(i == 0)
        def _():
            sums_ref[...] = jnp.zeros_like(sums_ref)

        zc_v = zc_ref[...].astype(F32)
        u, u1, u2 = _conv_taps(zc_v, zp_ref[...].astype(F32), i == 0)
        cw = cw_ref[...]
        dya_v = dya_ref[...].astype(F32)
        dyc = dya_v * zc_v[:, :w]
        dyc_next = jnp.where(i == nt - 1, 0.0, dn_ref[...].astype(F32) * zn_ref[:, :w].astype(F32))
        du = cw[2:3] * dyc + cw[1:2] * _advance_rows(dyc, dyc_next, 1) + cw[0:1] * _advance_rows(dyc, dyc_next, 2)
        dzc_ref[:, :w] = (dya_v * (cw[0:1] * u2 + cw[1:2] * u1 + cw[2:3] * u)).astype(BF16)
        dzc_ref[:, w:2 * w] = (du * zc_v[:, 2 * w:]).astype(BF16)
        dzc_ref[:, 2 * w:] = (du * zc_v[:, w:2 * w]).astype(BF16)
        _add_rows(sums_ref, [jnp.sum(dyc * tap, axis=0, keepdims=True) for tap in (u2, u1, u)])

    def rows(n):
        return pl.BlockSpec((tm, n), lambda i: (i, 0))

    def halo(n, step):
        return _halo_spec(tm, n, step, t // HALO - 1)

    return pl.pallas_call(
        body, name="conv_bwd", grid=(nt,),
        in_specs=[rows(ZC_COLS), halo(ZC_COLS, -1), halo(ZC_COLS, tm // HALO), rows(w), halo(w, tm // HALO),
                  _row(conv_w)],
        out_specs=[rows(ZC_COLS), pl.BlockSpec((8, w), lambda i: (0, 0))],
        out_shape=[jax.ShapeDtypeStruct((t, ZC_COLS), BF16), jax.ShapeDtypeStruct((8, w), F32)],
        compiler_params=_params(("arbitrary",)),
    )(zc, zc, zc, dya, dya, conv_w)


def _rms_bwd(dy, x, g):
    xhat, r = _rms(x)
    d_g = jnp.sum(dy * xhat, axis=0, keepdims=True)
    dxh = dy * g
    return r * (dxh - xhat * jnp.mean(dxh * xhat, axis=-1, keepdims=True)), d_g


def mla_project_backward(dq, dk, dv, zm, qn, kvn, pos, inv_freq, qg, kvg, w_uq, w_ukv):
    t = zm.shape[0]
    tm = _tile(t, ROW_TILE, 16)

    def body(dq_ref, dk_ref, dv_ref, zm_ref, qn_ref, kvn_ref, pos_ref, if_ref, qg_ref, kvg_ref, wq_ref, wkv_ref,
             dzm_ref, sums_ref, guq_ref, gukv_ref, dql_ref, dkvl_ref):
        @pl.when(pl.program_id(0) == 0)
        def _():
            sums_ref[...] = jnp.zeros_like(sums_ref)
            guq_ref[...] = jnp.zeros_like(guq_ref)
            gukv_ref[...] = jnp.zeros_like(gukv_ref)

        tables = _rope_tables(pos_ref[...], if_ref[...])
        dkr = jnp.zeros((tm, LANES), F32)
        for h in range(MLA_HEADS):
            lo = h * HEAD_PAD
            dql_ref[:, lo:lo + QK_NOPE] = (dq_ref[:, lo:lo + QK_NOPE].astype(F32) * QK_FOLD).astype(BF16)
            dql_ref[:, lo + QK_NOPE:lo + HEAD_PAD] = _rope_transposed(
                dq_ref[:, lo + QK_NOPE:lo + HEAD_PAD].astype(F32) * QK_FOLD, tables).astype(BF16)
            dkvl_ref[:, h * QK_NOPE:(h + 1) * QK_NOPE] = dk_ref[:, lo:lo + QK_NOPE]
            dkr = dkr + dk_ref[:, lo + QK_NOPE:lo + HEAD_PAD].astype(F32)
        dkvl_ref[:, MLA_HEADS * QK_NOPE:] = dv_ref[...]
        zv = zm_ref[...].astype(F32)
        dqn = _dot(dql_ref[...], wq_ref[...])
        dkvn = _dot(dkvl_ref[...], wkv_ref[...])
        dcq, d_qg = _rms_bwd(dqn, zv[:, :Q_LORA], qg_ref[...])
        dckv, d_kvg = _rms_bwd(dkvn, zv[:, Q_LORA:Q_LORA + KV_LORA], kvg_ref[...])
        dzm_ref[:, :Q_LORA] = dcq.astype(BF16)
        dzm_ref[:, Q_LORA:Q_LORA + KV_LORA] = dckv.astype(BF16)
        dzm_ref[:, Q_LORA + KV_LORA:] = _rope_transposed(dkr, tables).astype(BF16)
        sums_ref[0:1, :Q_LORA] += d_qg
        sums_ref[0:1, Q_LORA:Q_LORA + KV_LORA] += d_kvg
        guq_ref[...] += _dot(dql_ref[...], qn_ref[...], TN)
        gukv_ref[...] += _dot(dkvl_ref[...], kvn_ref[...], TN)

    def rows(n):
        return pl.BlockSpec((tm, n), lambda i: (i, 0))

    def whole(r, n):
        return pl.BlockSpec((r, n), lambda i: (0, 0))

    return pl.pallas_call(
        body, name="mla_project_bwd", grid=(t // tm,),
        in_specs=[rows(QK_COLS), rows(QK_COLS), rows(MLA_WIDTH), rows(ZM_COLS), rows(Q_LORA), rows(KV_LORA),
                  rows(1), _row(inv_freq), _row(qg), _row(kvg), _row(w_uq), _row(w_ukv)],
        out_specs=[rows(ZM_COLS), whole(8, ZM_COLS), whole(QK_COLS, Q_LORA), whole(QK_COLS, KV_LORA)],
        out_shape=[jax.ShapeDtypeStruct((t, ZM_COLS), BF16), jax.ShapeDtypeStruct((8, ZM_COLS), F32),
                   jax.ShapeDtypeStruct((QK_COLS, Q_LORA), F32), jax.ShapeDtypeStruct((QK_COLS, KV_LORA), F32)],
        scratch_shapes=[pltpu.VMEM((tm, QK_COLS), BF16), pltpu.VMEM((tm, QK_COLS), BF16)],
        compiler_params=_params(("arbitrary",)),
    )(dq, dk, dv, zm, qn, kvn, pos, inv_freq, qg, kvg, w_uq, w_ukv)


def mix_in_backward(dzc, dzm, w_in, x, dxo, gn, sc, gate, rider=None):
    t, d = x.shape
    tm = _tile(t, ROW_TILE, 16)

    def body(dzc_ref, dzm_ref, w_ref, x_ref, dxo_ref, gn_ref, sc_ref, gate_ref, dx_ref, dy_ref, sums_ref):
        @pl.when(pl.program_id(0) == 0)
        def _():
            sums_ref[...] = jnp.zeros_like(sums_ref)

        dh = _dot(dzc_ref[...], w_ref[:ZC_COLS, :]) + _dot(dzm_ref[...], w_ref[ZC_COLS:, :])
        dx, d_sh, d_sc, d_gn = _norm_mod_bwd(dh, x_ref[...], gn_ref[...], sc_ref[...])
        dx = dxo_ref[...] + dx
        dx_ref[...] = dx
        dy_ref[...] = (0.5 * gate_ref[...] * dx).astype(BF16)
        _add_rows(sums_ref, [d_sh, d_sc, d_gn])

    def rows(n):
        return pl.BlockSpec((tm, n), lambda i: (i, 0))

    return _call_with_rider(
        body, rider, name="mix_in_bwd", grid=(t // tm,),
        in_specs=[rows(ZC_COLS), rows(ZM_COLS), _row(w_in), rows(d), rows(d), _row(gn), _row(sc), _row(gate)],
        out_specs=[rows(d), rows(d), pl.BlockSpec((8, d), lambda i: (0, 0))],
        out_shape=[jax.ShapeDtypeStruct((t, d), F32), jax.ShapeDtypeStruct((t, d), BF16),
                   jax.ShapeDtypeStruct((8, d), F32)],
        scratch_shapes=[], operands=(dzc, dzm, w_in, x, dxo, gn, sc, gate))


def _adamw_step(w, g, m, v):
    m_new = ADAM_B1 * m + (1.0 - ADAM_B1) * g
    v_new = ADAM_B2 * v + (1.0 - ADAM_B2) * (g * g)
    m_hat = m_new / (1.0 - ADAM_B1 ** ADAM_STEP)
    v_hat = v_new / (1.0 - ADAM_B2 ** ADAM_STEP)
    return -ADAM_LR * (m_hat / (jnp.sqrt(v_hat) + ADAM_EPS) + ADAM_WD * w), m_new, v_new


def adamw(w, g, m, v, name):
    r, n = w.shape
    tr = _tile(r, max(8, ADAM_TILE_ELEMS // n), 8)

    def body(w_ref, g_ref, m_ref, v_ref, d_ref, mo_ref, vo_ref):
        d_ref[...], mo_ref[...], vo_ref[...] = _adamw_step(w_ref[...], g_ref[...], m_ref[...], v_ref[...])

    blk = pl.BlockSpec((tr, n), lambda i: (i, 0))
    shape = jax.ShapeDtypeStruct((r, n), F32)
    return pl.pallas_call(
        body, name=name, grid=(r // tr,), in_specs=[blk] * 4, out_specs=[blk] * 3, out_shape=[shape] * 3,
        compiler_params=_params(("arbitrary",)),
    )(w, g, m, v)


def adamw_received(w, own, got, m, v, name):
    r, n = w.shape
    tr = _tile(r, SUM_ROWS, 16)

    def body(w_ref, own_ref, got_ref, m_ref, v_ref, g_ref, d_ref, mo_ref, vo_ref):
        g = own_ref[...]
        for j in range(3):
            g = g + got_ref[j].astype(F32)
        g_ref[...] = g
        d_ref[...], mo_ref[...], vo_ref[...] = _adamw_step(w_ref[...], g, m_ref[...], v_ref[...])

    blk = pl.BlockSpec((tr, n), lambda i: (i, 0))
    shape = jax.ShapeDtypeStruct((r, n), F32)
    return pl.pallas_call(
        body, name=name, grid=(r // tr,),
        in_specs=[blk, blk, pl.BlockSpec((3, tr, n), lambda i: (0, i, 0)), blk, blk],
        out_specs=[blk] * 4, out_shape=[shape] * 4, compiler_params=_params(("arbitrary",)),
    )(w, own, got, m, v)


def _pad_to(v, n):
    return jnp.pad(v, (0, n - v.shape[0]))


def _pad_heads(w, axis_len):
    n = w.shape[1]
    return jnp.pad(w.reshape(MLA_HEADS, axis_len, n), ((0, 0), (0, HEAD_PAD - axis_len), (0, 0))).reshape(-1, n)


def _swap_head_parts(w, inner, outer):
    n = w.shape[1]
    return w.reshape(outer, inner, QK_NOPE, n).transpose(1, 0, 2, 3).reshape(-1, n)


def kernel(x, c, positions, ada_w, ada_b, norm_ffn1_g, ffn1_w1, ffn1_w3, ffn1_w2, norm_mix_g, w_in, conv_w, q_norm_g, w_uq, kv_norm_g, w_ukv, out_norm_g, w_out, norm_ffn2_g, ffn2_w1, ffn2_w3, ffn2_w2, final_norm_g, loss_target, m_ada_w, m_ada_b, m_norm_ffn1_g, m_ffn1_w1, m_ffn1_w3, m_ffn1_w2, m_norm_mix_g, m_w_in, m_conv_w, m_q_norm_g, m_w_uq, m_kv_norm_g, m_w_ukv, m_out_norm_g, m_w_out, m_norm_ffn2_g, m_ffn2_w1, m_ffn2_w3, m_ffn2_w2, m_final_norm_g, v_ada_w, v_ada_b, v_norm_ffn1_g, v_ffn1_w1, v_ffn1_w3, v_ffn1_w2, v_norm_mix_g, v_w_in, v_conv_w, v_q_norm_g, v_w_uq, v_kv_norm_g, v_w_ukv, v_out_norm_g, v_w_out, v_norm_ffn2_g, v_ffn2_w1, v_ffn2_w3, v_ffn2_w2, v_final_norm_g):
    t, d = x.shape[1], x.shape[2]
    f = ffn1_w2.shape[1] * N_DEV
    me = 4 * lax.axis_index("x") + 2 * lax.axis_index("y") + lax.axis_index("c")
    my_c = lax.axis_index("c")
    my_chip = 2 * lax.axis_index("x") + lax.axis_index("y")
    xs = x[0]
    n_ada = ada_w.shape[2]
    cw_n = conv_w.shape[2]

    c_rows = jnp.broadcast_to(c, (8, d))
    conv_rows = jnp.pad(conv_w[0], ((0, 8 - CONV_K), (0, LANES - cw_n)))
    ffn1_blocks = [ffn1_w1[0].T.astype(BF16), ffn1_w3[0].T.astype(BF16), ffn1_w2[0].astype(BF16)]
    ffn2_blocks = [ffn2_w1[0].T.astype(BF16), ffn2_w3[0].T.astype(BF16), ffn2_w2[0].astype(BF16)]
    c_all, conv_all, *ffn1_all = all_gather_relayed([c_rows, conv_rows] + ffn1_blocks, [0] * 5, "gather_first")
    c_all = c_all[:, 0, :]
    conv_full8 = conv_all[:, :, :cw_n].transpose(1, 0, 2).reshape(8, CONV_WIDTH)
    ffn1_ws = [w.reshape(f, d) for w in ffn1_all]
    gather_mix = riding_gather(
        [w_in[0].T.astype(BF16), w_uq[0].T.astype(BF16), w_ukv[0].T.astype(BF16), w_out[0].astype(BF16)], [0, 0, 0, 0])

    ada_b_cols = lax.dynamic_slice_in_dim(ada_b, me * n_ada, n_ada, axis=1)
    mod_cols = ada_forward(c_all, ada_w[0], ada_b_cols)
    mod_all, = all_gather([mod_cols], [0], "gather_mod")
    mod = lax.dynamic_index_in_dim(mod_all, me, axis=1, keepdims=False).reshape(N_MOD, 1, d)
    sh1, sc1, g1, sh2, sc2, g2, sh3, sc3, g3 = [mod[i] for i in range(N_MOD)]

    gf = final_norm_g.reshape(1, d)
    x1, h1, a1, b1, y1, *gathered = ffn_forward(xs, norm_ffn1_g, sc1, sh1, g1, ffn1_ws, "ffn1_fwd", gather_mix)
    w_in_p = jnp.pad(gathered[0].reshape(IN_COLS, d), ((0, ZC_COLS + ZM_COLS - IN_COLS), (0, 0)))
    w_uq_p = _pad_heads(gathered[1].reshape(-1, Q_LORA), QK_NOPE + QK_ROPE)
    w_ukv_p = _swap_head_parts(gathered[2].reshape(-1, KV_LORA), 2, MLA_HEADS)
    w_out_f = gathered[3].reshape(MIX_WIDTH, d)
    h2, zc, zm = mix_in_forward(x1, norm_mix_g, sc2, sh2, w_in_p)
    pos = positions[0].astype(F32).reshape(t, 1)
    inv_freq = ROPE_THETA ** (-jnp.arange(0, QK_ROPE, 2, dtype=F32) / QK_ROPE)
    inv_freq = jnp.concatenate([inv_freq, inv_freq, jnp.zeros((LANES - QK_ROPE,), F32)]).reshape(1, LANES)
    qn, kvn, q, k, v = mla_project(zm, pos, inv_freq, q_norm_g, kv_norm_g, w_uq_p, w_ukv_p)
    o, lse, *ffn2_all = attention_forward(q, k, v, riding_gather(ffn2_blocks, [0] * 3))
    ffn2_ws = [w.reshape(f, d) for w in ffn2_all]
    lane = jnp.arange(CONV_WIDTH)
    gmat_a = (lane[:, None] // (CONV_WIDTH // CONV_GROUPS) == lane[None, :] // (CONV_WIDTH // CONV_GROUPS))
    gmat_a = (gmat_a / (CONV_WIDTH // CONV_GROUPS)).astype(BF16)
    gmat_b = ((lane[:, None] // V_HEAD == lane[None, :] // V_HEAD) / V_HEAD).astype(BF16)
    x2, yn, y2, ya = mix_out_forward(zc, o, conv_full8, out_norm_g, gmat_a, gmat_b, w_out_f, x1, g2)
    dx3, h3, a3, b3, y3, dy3, sums_f = ffn_forward(x2, norm_ffn2_g, sc3, sh3, g3, ffn2_ws, "ffn2_fwd",
                                                   loss_head=(loss_target[0], gf))

    chip_idx = jnp.bitwise_xor(my_chip, jnp.array([0, 2, 1, 3], jnp.int32)).astype(jnp.int32)
    src_idx = (2 * chip_idx + my_c).astype(jnp.int32)

    def row_blocks(named):
        return [g.reshape(N_DEV, g.shape[0] // N_DEV, g.shape[1]) for _, g in named]

    def chip_sums(named, g8, got):
        return [add_sibling(g, r, src_idx, chip_idx, "rs_add_" + n) for g, r, (n, _) in zip(g8, got, named)]

    da3, db3, g_w2b = ffn_backward_gate(dy3, a3, b3, ffn2_ws[2], "ffn2_bwd_gate")
    dx2, sums_3 = ffn_backward_norm(da3, db3, dx3, x2, y3, norm_ffn2_g, sc3, ffn2_ws[0], ffn2_ws[1], "ffn2_bwd_norm")
    ffn2_named = [("ffn2_w1", matmul_tn(da3, h3, "ffn2_gw1")), ("ffn2_w3", matmul_tn(db3, h3, "ffn2_gw3")),
                  ("ffn2_w2", g_w2b)]
    ffn2_g8 = row_blocks(ffn2_named)
    dya, do, delta, sums_2d, sums_2o, g_w_out, *ffn2_sib = mix_out_backward(
        dx2, y2, g2, ya, o, yn, out_norm_g, gmat_a, gmat_b, w_out_f, riding_sibling(ffn2_g8))
    ffn2_sums = chip_sums(ffn2_named, ffn2_g8, ffn2_sib)
    nq = t // _tile(t, ATTN_TILE, CHUNK)
    stat_shape = (MLA_HEADS, nq, 1, t // nq)
    dq, dk, dv, *ffn2_got = attention_backward(q, k, v, do, lse.reshape(stat_shape), delta.reshape(stat_shape),
                                               riding_exchange([s[1] for s in ffn2_sums]))
    dzc, sums_c = conv_backward(zc, dya, conv_full8)
    dzm, sums_m, g_w_uq_p, g_w_ukv_p = mla_project_backward(
        dq, dk, dv, zm, qn, kvn, pos, inv_freq, q_norm_g, kv_norm_g, w_uq_p, w_ukv_p)
    g_w_in = matmul_tn([dzc, dzm], h2, "gw_in")[:IN_COLS]
    g_w_uq = g_w_uq_p.reshape(MLA_HEADS, HEAD_PAD, Q_LORA)[:, :QK_NOPE + QK_ROPE].reshape(-1, Q_LORA)
    g_w_ukv = _swap_head_parts(g_w_ukv_p, MLA_HEADS, 2)
    mix_named = [("w_in", g_w_in), ("w_uq", g_w_uq), ("w_ukv", g_w_ukv), ("w_out", g_w_out)]
    mix_g8 = row_blocks(mix_named)
    dx1, dy1, sums_1m, *mix_sib = mix_in_backward(dzc, dzm, w_in_p, x1, dx2, norm_mix_g, sc2, g1, riding_sibling(mix_g8))
    mix_sums = chip_sums(mix_named, mix_g8, mix_sib)
    da1, db1, g_w2a, *mix_got = ffn_backward_gate(dy1, a1, b1, ffn1_ws[2], "ffn1_bwd_gate",
                                                  riding_exchange([s[1] for s in mix_sums]))
    ffn1_pair = [("ffn1_w2", g_w2a), ("ffn1_w1", matmul_tn(da1, h1, "ffn1_gw1"))]
    pair_g8 = row_blocks(ffn1_pair)
    g_w3a, *pair_sib = matmul_tn(db1, h1, "ffn1_gw3", riding_sibling(pair_g8))
    ffn1_last = [("ffn1_w3", g_w3a)]
    last_g8 = row_blocks(ffn1_last)
    ffn1_named = ffn1_pair + ffn1_last
    ffn1_sums = chip_sums(ffn1_pair, pair_g8, pair_sib) + chip_sums(
        ffn1_last, last_g8, exchange_sibling(last_g8, "rs_sibling_ffn1_w3"))
    dx0, sums_1, *ffn1_got = ffn_backward_norm(da1, db1, dx1, xs, y1, norm_ffn1_g, sc1, ffn1_ws[0], ffn1_ws[1], "ffn1_bwd_norm",
                                               riding_exchange([s[1] for s in ffn1_sums]))
    reduced = {}
    for named, group_sums, group_got in ((ffn2_named, ffn2_sums, ffn2_got), (mix_named, mix_sums, mix_got),
                                         (ffn1_named, ffn1_sums, ffn1_got)):
        for (n, _), (own, _), got in zip(named, group_sums, group_got):
            reduced[n] = (own, got)

    dmod = jnp.concatenate([sums_1[0], sums_1[1], sums_1[2], sums_1m[0], sums_1m[1], sums_2d[0],
                            sums_3[0], sums_3[1], sums_3[2]])
    pieces = [dmod, sums_1[3], sums_1m[2], sums_m[0, :Q_LORA], sums_m[0, Q_LORA:Q_LORA + KV_LORA], sums_2o[0],
              sums_3[3], sums_f[0], sums_f[1], sums_c[:CONV_K].reshape(-1)]
    plens = [p.shape[0] for p in pieces]
    poffs = [sum(plens[:i]) for i in range(len(plens))]
    vec_len = -(-sum(plens) // 1024) * 1024
    vec = _pad_to(jnp.concatenate(pieces), vec_len).reshape(-1, LANES)
    vec_all, = all_gather([vec], [0], "gather_sums")
    tot = sum_devices(vec_all).reshape(-1)
    g_ada_b, g_n1, g_nmix, g_qg, g_kvg, g_og, g_n3, g_gf, loss_lanes, g_conv_full = [
        tot[o:o + n] for o, n in zip(poffs, plens)]
    loss = sum_lanes(loss_lanes.reshape(1, d))[0, 0]
    g_conv = lax.dynamic_slice_in_dim(g_conv_full.reshape(CONV_K, CONV_WIDTH), me * cw_n, cw_n, axis=1)
    dmod_all = vec_all.reshape(N_DEV, vec_len)[:, :N_MOD * d]
    dmod_cols = lax.dynamic_slice_in_dim(dmod_all, me * n_ada, n_ada, axis=1)
    g_ada_w = ada_backward(jnp.pad(c_all, ((0, 8), (0, 0))), jnp.pad(dmod_cols, ((0, 8), (0, 0))))

    def update(name, w, g, m, v, received=None):
        k, n = w.shape[-2:]
        if g.shape == (k, n):
            flat, back = (lambda a: a.reshape(k, n)), (lambda a: a.reshape(w.shape))
        else:
            flat, back = (lambda a: a.reshape(k, n).T), (lambda a: a.T.reshape(w.shape))
        if received is None:
            out = (g,) + tuple(adamw(flat(w), g, flat(m), flat(v), "adamw_" + name))
        else:
            out = adamw_received(flat(w), g, received, flat(m), flat(v), "adamw_" + name)
        return tuple(back(a) for a in out)

    res = {}
    res["ada_w"] = update("ada_w", ada_w, g_ada_w, m_ada_w, v_ada_w)
    big = [("ffn1_w1", ffn1_w1, m_ffn1_w1, v_ffn1_w1), ("ffn1_w3", ffn1_w3, m_ffn1_w3, v_ffn1_w3),
           ("ffn2_w1", ffn2_w1, m_ffn2_w1, v_ffn2_w1), ("ffn2_w3", ffn2_w3, m_ffn2_w3, v_ffn2_w3),
           ("w_in", w_in, m_w_in, v_w_in), ("w_uq", w_uq, m_w_uq, v_w_uq), ("w_ukv", w_ukv, m_w_ukv, v_w_ukv),
           ("ffn1_w2", ffn1_w2, m_ffn1_w2, v_ffn1_w2), ("ffn2_w2", ffn2_w2, m_ffn2_w2, v_ffn2_w2),
           ("w_out", w_out, m_w_out, v_w_out)]
    for name, w, m, v in big:
        res[name] = update(name, w, reduced[name][0], m, v, reduced[name][1])
    smalls = [("ada_b", ada_b, g_ada_b, m_ada_b, v_ada_b),
              ("norm_ffn1_g", norm_ffn1_g, g_n1, m_norm_ffn1_g, v_norm_ffn1_g),
              ("norm_mix_g", norm_mix_g, g_nmix, m_norm_mix_g, v_norm_mix_g),
              ("conv_w", conv_w, g_conv, m_conv_w, v_conv_w),
              ("q_norm_g", q_norm_g, g_qg, m_q_norm_g, v_q_norm_g),
              ("kv_norm_g", kv_norm_g, g_kvg, m_kv_norm_g, v_kv_norm_g),
              ("out_norm_g", out_norm_g, g_og, m_out_norm_g, v_out_norm_g),
              ("norm_ffn2_g", norm_ffn2_g, g_n3, m_norm_ffn2_g, v_norm_ffn2_g),
              ("final_norm_g", final_norm_g, g_gf, m_final_norm_g, v_final_norm_g)]
    slens = [w.size for _, w, _, _, _ in smalls]
    soffs = [sum(slens[:i]) for i in range(len(slens))]
    s_len = -(-sum(slens) // 1024) * 1024

    def pack_small(i):
        return _pad_to(jnp.concatenate([s[i].reshape(-1) for s in smalls]), s_len).reshape(8, -1)

    s_out = adamw(pack_small(1), pack_small(2), pack_small(3), pack_small(4), "adamw_small")
    for (name, w, g, _, _), o, n in zip(smalls, soffs, slens):
        res[name] = (g.reshape(w.shape),) + tuple(a.reshape(-1)[o:o + n].reshape(w.shape) for a in s_out)

    order = ["ada_w", "ada_b", "norm_ffn1_g", "ffn1_w1", "ffn1_w3", "ffn1_w2", "norm_mix_g", "w_in", "conv_w",
             "q_norm_g", "w_uq", "kv_norm_g", "w_ukv", "out_norm_g", "w_out", "norm_ffn2_g", "ffn2_w1", "ffn2_w3",
             "ffn2_w2", "final_norm_g"]
    return (loss, dx0.reshape(x.shape), *[res[n][0] for n in order], *[res[n][1] for n in order],
            *[res[n][2] for n in order], *[res[n][3] for n in order])
```

```python
import functools

import jax
import jax.numpy as jnp
from jax import lax
from jax.experimental import pallas as pl
from jax.experimental.pallas import tpu as pltpu

F32 = jnp.float32
BF16 = jnp.bfloat16
MESH_ID = pl.DeviceIdType.MESH
N_DEV = 8

EPS = 1e-6
CHUNK = 64
N_MOD = 9
CONV_WIDTH = 512
CONV_GROUPS = 8
CONV_K = 3
MLA_HEADS = 4
QK_NOPE = 128
QK_ROPE = 64
V_HEAD = 128
Q_LORA = 384
KV_LORA = 256
ROPE_THETA = 10000.0
MLA_WIDTH = MLA_HEADS * V_HEAD
MIX_WIDTH = CONV_WIDTH + MLA_WIDTH
IN_COLS = 3 * CONV_WIDTH + Q_LORA + KV_LORA + QK_ROPE
ZC_COLS = 3 * CONV_WIDTH
ZM_COLS = Q_LORA + KV_LORA + 128
HEAD_PAD = 256
QK_COLS = MLA_HEADS * HEAD_PAD
ATTN_SCALE = (QK_NOPE + QK_ROPE) ** -0.5
LOG2_E = 1.4426950408889634
LN_2 = 0.6931471805599453
QK_FOLD = ATTN_SCALE * LOG2_E
NEG_INF = -1e30

ADAM_LR = 0.001
ADAM_B1 = 0.9
ADAM_B2 = 0.999
ADAM_EPS = 1e-08
ADAM_WD = 0.01
ADAM_STEP = 10

LANES = 128
VMEM_LIMIT = 56 * 1024 * 1024
ROW_TILE = 1024
FFN_FWD_TILE = (1024, 256)
FFN_BWD_TILE = (512, 1408)
GRAD_TILE = 1408
GRAD_DEPTH = 2048
SUM_ROWS = 256
ADAM_TILE_ELEMS = 1 << 19
ATTN_TILE = 1024

NN = (((1,), (0,)), ((), ()))
NT = (((1,), (1,)), ((), ()))
TN = (((0,), (0,)), ((), ()))


def _dot(a, b, dims=NN):
    return lax.dot_general(a, b, dims, preferred_element_type=F32)


def _tile(n, cap, mult=LANES):
    best = None
    for t in range(mult, min(n, cap) + 1, mult):
        if n % t == 0:
            best = t
    return n if best is None else best


def _params(sem=None):
    return pltpu.CompilerParams(dimension_semantics=sem, vmem_limit_bytes=VMEM_LIMIT)


def _row(v):
    return pl.BlockSpec(v.shape, lambda *_: (0,) * v.ndim)


def _sigmoid(x):
    return 0.5 * jnp.tanh(0.5 * x) + 0.5


def _rms(x):
    r = lax.rsqrt(jnp.mean(x * x, axis=-1, keepdims=True) + EPS)
    return x * r, r


def _norm_mod_bwd(dh, x, gn, sc):
    xhat, r = _rms(x)
    d_sh = jnp.sum(dh, axis=0, keepdims=True)
    d_sc = jnp.sum(dh * (xhat * gn), axis=0, keepdims=True)
    dxn = dh * (1.0 + sc)
    d_gn = jnp.sum(dxn * xhat, axis=0, keepdims=True)
    dxh = dxn * gn
    dx = r * (dxh - xhat * jnp.mean(dxh * xhat, axis=-1, keepdims=True))
    return dx, d_sh, d_sc, d_gn


def _group_mean(v, gmat):
    return _dot(v.astype(BF16), gmat)


def _add_rows(ref, rows):
    for r, v in enumerate(rows):
        ref[r:r + 1, :] += v


def _window(ref, axis, j):
    return ref.at[(slice(None),) * axis + (j,)]


def _any_specs(n):
    return [pl.BlockSpec(memory_space=pl.ANY)] * n


def all_gather(blocks, axes, name):
    n_arr = len(blocks)

    def body(*refs):
        start, forward, finish = _gather_steps(refs[:n_arr], refs[n_arr:2 * n_arr], axes, *refs[2 * n_arr:])
        start()
        for j in range(3):
            forward(j)
        finish()

    return pl.pallas_call(
        body, name=name, out_shape=_gathered_shapes(blocks, axes),
        in_specs=_any_specs(n_arr), out_specs=_any_specs(n_arr), scratch_shapes=_gather_sems(n_arr),
    )(*blocks)


def all_gather_relayed(blocks, axes, name):
    n_arr = len(blocks)
    arrays = range(n_arr)

    def body(*refs):
        ins, outs = refs[:n_arr], refs[n_arr:2 * n_arr]
        send_sems, recv_sems, local_sems = refs[2 * n_arr:]
        x, y, c = lax.axis_index("x"), lax.axis_index("y"), lax.axis_index("c")
        sibling, x_nbr, y_nbr, diagonal = (x, y, 1 - c), (1 - x, y, c), (x, 1 - y, c), (1 - x, 1 - y, c)
        north = c == 1
        relay_slot = jnp.where(north, 1, 2)
        relay_from = tuple(jnp.where(north, a, b) for a, b in zip(x_nbr, y_nbr))
        relay_to = tuple(jnp.where(north, a, b) for a, b in zip(y_nbr, x_nbr))
        other_from = relay_to

        def slot(a, px, py, pc):
            return _window(outs[a], axes[a], 4 * px + 2 * py + pc)

        def copy(a, k, block, to, src=None):
            return pltpu.make_async_remote_copy(
                src_ref=slot(a, *block) if src is None else src, dst_ref=slot(a, *block),
                send_sem=send_sems.at[k, a], recv_sem=recv_sems.at[k, a], device_id=to, device_id_type=MESH_ID)

        mine = [pltpu.make_async_copy(ins[a], slot(a, x, y, c), local_sems.at[a]) for a in arrays]
        for cp in mine:
            cp.start()
        first = [copy(a, k, (x, y, c), to, src=ins[a])
                 for k, to in enumerate((sibling, x_nbr, y_nbr)) for a in arrays]
        for cp in first:
            cp.start()
        later = []
        for a in arrays:
            copy(a, relay_slot, relay_from, (x, y, c)).wait_recv()
            later += [copy(a, 3, relay_from, relay_to), copy(a, 3 + relay_slot, relay_from, sibling)]
            later[-2].start()
            later[-1].start()
        for a in arrays:
            copy(a, 3 - relay_slot, other_from, (x, y, c)).wait_recv()
            later.append(copy(a, 6 - relay_slot, other_from, sibling))
            later[-1].start()
        for a in arrays:
            copy(a, 3, diagonal, (x, y, c)).wait_recv()
            later.append(copy(a, 6, diagonal, sibling))
            later[-1].start()
        for a in arrays:
            for k, block in ((0, sibling), (4, (1 - x, y, 1 - c)), (5, (x, 1 - y, 1 - c)), (6, (1 - x, 1 - y, 1 - c))):
                copy(a, k, block, (x, y, c)).wait_recv()
        for cp in first + later:
            cp.wait_send()
        for cp in mine:
            cp.wait()

    return pl.pallas_call(
        body, name=name, out_shape=_gathered_shapes(blocks, axes),
        in_specs=_any_specs(n_arr), out_specs=_any_specs(n_arr), scratch_shapes=_gather_sems(n_arr),
    )(*blocks)


def _gathered_shapes(blocks, axes):
    return [jax.ShapeDtypeStruct(b.shape[:ax] + (N_DEV,) + b.shape[ax:], b.dtype) for b, ax in zip(blocks, axes)]


def _gather_sems(n_arr):
    return [pltpu.SemaphoreType.DMA((7, n_arr)), pltpu.SemaphoreType.DMA((7, n_arr)), pltpu.SemaphoreType.DMA((n_arr,))]


def _gather_steps(ins, outs, axes, send_sems, recv_sems, local_sems):
    arrays = range(len(ins))
    x, y, c = lax.axis_index("x"), lax.axis_index("y"), lax.axis_index("c")
    me, sibling = (x, y, c), (x, y, 1 - c)
    chips = [(1 - x, y), (x, 1 - y), (1 - x, 1 - y)]

    def slot(a, px, py, pc):
        return _window(outs[a], axes[a], 4 * px + 2 * py + pc)

    def copy(a, k, block, to, src=None):
        return pltpu.make_async_remote_copy(
            src_ref=slot(a, *block) if src is None else src, dst_ref=slot(a, *block),
            send_sem=send_sems.at[k, a], recv_sem=recv_sems.at[k, a], device_id=to, device_id_type=MESH_ID)

    def mine(a):
        return pltpu.make_async_copy(ins[a], slot(a, *me), local_sems.at[a])

    def first():
        return ([copy(a, 0, me, sibling, src=ins[a]) for a in arrays]
                + [copy(a, 1 + j, me, (*chip, c), src=ins[a]) for j, chip in enumerate(chips) for a in arrays])

    def passed(j):
        return [copy(a, 4 + j, (*chips[j], c), sibling) for a in arrays]

    def start():
        for a in arrays:
            mine(a).start()
        for cp in first():
            cp.start()

    def forward(j):
        for a, cp in zip(arrays, passed(j)):
            copy(a, 1 + j, (*chips[j], c), me).wait_recv()
            cp.start()

    def finish():
        for a in arrays:
            copy(a, 0, sibling, me).wait_recv()
        for j, chip in enumerate(chips):
            for a in arrays:
                copy(a, 4 + j, (*chip, 1 - c), me).wait_recv()
        for cp in first() + passed(0) + passed(1) + passed(2):
            cp.wait_send()
        for a in arrays:
            mine(a).wait()

    return start, forward, finish


def exchange_sibling(grads, name):
    n_arr = len(grads)

    def body(*refs):
        start, finish = _sibling_exchange_steps(refs[:n_arr], refs[n_arr:2 * n_arr], *refs[2 * n_arr:])
        start()
        finish()

    return pl.pallas_call(
        body, name=name, out_shape=_sibling_shapes(grads),
        in_specs=_any_specs(n_arr), out_specs=_any_specs(n_arr), scratch_shapes=_exchange_sems(n_arr),
    )(*grads)


def _sibling_shapes(grads):
    return [jax.ShapeDtypeStruct((4,) + g.shape[1:], g.dtype) for g in grads]


def _exchange_sems(n_arr):
    return [pltpu.SemaphoreType.DMA((n_arr,)), pltpu.SemaphoreType.DMA((n_arr,))]


def _sibling_exchange_steps(ins, outs, send_sems, recv_sems):
    x, y, c = lax.axis_index("x"), lax.axis_index("y"), lax.axis_index("c")

    def copy(a, src, dst):
        return pltpu.make_async_remote_copy(
            src_ref=src, dst_ref=dst, send_sem=send_sems.at[a], recv_sem=recv_sems.at[a],
            device_id=(x, y, 1 - c), device_id_type=MESH_ID)

    def start():
        for a in range(len(ins)):
            for k in range(4):
                copy(a, ins[a].at[2 * k + (1 - c)], outs[a].at[k]).start()

    def finish():
        whole = [copy(a, ins[a].at[pl.ds(0, 4)], outs[a]) for a in range(len(ins))]
        for cp in whole:
            cp.wait_recv()
        for cp in whole:
            cp.wait_send()

    return start, finish


def _chip_exchange_steps(ins, outs, send_sems, recv_sems):
    x, y, c = lax.axis_index("x"), lax.axis_index("y"), lax.axis_index("c")
    chips = [(1 - x, y), (x, 1 - y), (1 - x, 1 - y)]

    def copy(a, src, dst, chip):
        return pltpu.make_async_remote_copy(
            src_ref=src, dst_ref=dst, send_sem=send_sems.at[a], recv_sem=recv_sems.at[a],
            device_id=(*chip, c), device_id_type=MESH_ID)

    def start():
        for a in range(len(ins)):
            for j, chip in enumerate(chips):
                copy(a, ins[a].at[j], outs[a].at[j], chip).start()

    def finish():
        whole = [copy(a, ins[a], outs[a], chips[0]) for a in range(len(ins))]
        for cp in whole:
            cp.wait_recv()
        for cp in whole:
            cp.wait_send()

    return start, finish


def riding_gather(blocks, axes):
    def phases(ins, outs, *sems):
        start, forward, finish = _gather_steps(ins, outs, axes, *sems)
        return [start] + [functools.partial(forward, j) for j in range(3)] + [finish]

    return dict(operands=blocks, out_shape=_gathered_shapes(blocks, axes), sems=_gather_sems(len(blocks)),
                phases=phases, when=("first", "late0", "late1", "late2", "last"))


def riding_exchange(parts):
    def phases(ins, outs, *sems):
        return list(_chip_exchange_steps(ins, outs, *sems))

    return dict(operands=parts, out_shape=[jax.ShapeDtypeStruct(p.shape, p.dtype) for p in parts],
                sems=_exchange_sems(len(parts)), phases=phases, when=("first", "last"))


def riding_sibling(grads):
    def phases(ins, outs, *sems):
        return list(_sibling_exchange_steps(ins, outs, *sems))

    return dict(operands=grads, out_shape=_sibling_shapes(grads), sems=_exchange_sems(len(grads)),
                phases=phases, when=("first", "last"))


def _call_with_rider(body, rider, *, name, grid, in_specs, out_specs, out_shape, scratch_shapes, operands):
    params = _params(("arbitrary",) * len(grid))
    if rider is None:
        return pl.pallas_call(body, name=name, grid=grid, in_specs=in_specs, out_specs=out_specs,
                              out_shape=out_shape, scratch_shapes=scratch_shapes, compiler_params=params)(*operands)
    n_in, n_out, n_scr, k = len(in_specs), len(out_specs), len(scratch_shapes), len(rider["operands"])
    at = {"first": (0,) * len(grid), "last": tuple(g - 1 for g in grid)}
    if "late0" in rider["when"]:
        rows, cols = grid
        assert cols >= 3
        at.update({"late%d" % j: (max(rows - 2, 0), j) for j in range(3)})

    def wrapped(*refs):
        ins, c_in = refs[:n_in], refs[n_in:n_in + k]
        outs, c_out = refs[n_in + k:n_in + k + n_out], refs[n_in + k + n_out:n_in + 2 * k + n_out]
        scratch, sems = refs[n_in + 2 * k + n_out:n_in + 2 * k + n_out + n_scr], refs[n_in + 2 * k + n_out + n_scr:]
        pos = [pl.program_id(axis) for axis in range(len(grid))]

        def here(key):
            return functools.reduce(jnp.logical_and, [p == v for p, v in zip(pos, at[key])])

        phases = rider["phases"](c_in, c_out, *sems)
        for fn, key in zip(phases, rider["when"]):
            if key != "last":
                pl.when(here(key))(fn)
        body(*ins, *outs, *scratch)
        pl.when(here("last"))(phases[-1])

    return pl.pallas_call(
        wrapped, name=name, grid=grid,
        in_specs=list(in_specs) + _any_specs(k), out_specs=list(out_specs) + _any_specs(k),
        out_shape=list(out_shape) + rider["out_shape"], scratch_shapes=list(scratch_shapes) + rider["sems"],
        compiler_params=params)(*operands, *rider["operands"])


def add_sibling(g8, got, src_idx, chip_idx, name):
    _, r, n = g8.shape
    tr = _tile(r, SUM_ROWS, 16)

    def body(si_ref, ci_ref, g0_ref, g1_ref, g2_ref, g3_ref, got_ref, own_ref, send_ref):
        own_ref[...] = g0_ref[0] + got_ref[ci_ref[0]]
        for j, g_ref in enumerate((g1_ref, g2_ref, g3_ref)):
            send_ref[j] = (g_ref[0] + got_ref[ci_ref[j + 1]]).astype(BF16)

    def mine(j):
        return pl.BlockSpec((1, tr, n), lambda i, si, ci: (si[j], i, 0))

    return pl.pallas_call(
        body, name=name,
        out_shape=[jax.ShapeDtypeStruct((r, n), F32), jax.ShapeDtypeStruct((3, r, n), BF16)],
        grid_spec=pltpu.PrefetchScalarGridSpec(
            num_scalar_prefetch=2, grid=(r // tr,),
            in_specs=[mine(0), mine(1), mine(2), mine(3), pl.BlockSpec((4, tr, n), lambda i, si, ci: (0, i, 0))],
            out_specs=[pl.BlockSpec((tr, n), lambda i, si, ci: (i, 0)),
                       pl.BlockSpec((3, tr, n), lambda i, si, ci: (0, i, 0))]),
        compiler_params=_params(("arbitrary",)),
    )(src_idx, chip_idx, g8, g8, g8, g8, got)


def sum_devices(g):
    def body(g_ref, o_ref):
        acc = g_ref[0]
        for j in range(1, N_DEV):
            acc = acc + g_ref[j]
        o_ref[...] = acc

    return pl.pallas_call(body, name="sum_devices", out_shape=jax.ShapeDtypeStruct(g.shape[1:], F32))(g)


def sum_lanes(v):
    def body(v_ref, o_ref):
        o_ref[...] = jnp.broadcast_to(jnp.sum(v_ref[...], axis=-1, keepdims=True), (1, LANES))

    return pl.pallas_call(body, name="sum_lanes", out_shape=jax.ShapeDtypeStruct((1, LANES), F32))(v)


def ada_forward(c_all, ada_w, ada_b_cols):
    nb, n = c_all.shape[0], ada_w.shape[1]

    def body(c_ref, w_ref, b_ref, o_ref):
        cv = c_ref[...]
        s = (cv * jax.nn.sigmoid(cv)).astype(BF16)
        o_ref[...] = _dot(s, w_ref[...].astype(BF16)) + b_ref[...]

    return pl.pallas_call(body, name="ada_fwd", out_shape=jax.ShapeDtypeStruct((nb, n), F32),
                          compiler_params=_params())(c_all, ada_w, ada_b_cols)


def ada_backward(c_all16, dmod16):
    d, n = c_all16.shape[1], dmod16.shape[1]

    def body(c_ref, g_ref, o_ref):
        cv = c_ref[...]
        s = (cv * jax.nn.sigmoid(cv)).astype(BF16)
        o_ref[...] = _dot(s, g_ref[...].astype(BF16), TN)

    return pl.pallas_call(body, name="ada_bwd", out_shape=jax.ShapeDtypeStruct((d, n), F32),
                          compiler_params=_params())(c_all16, dmod16)


def ffn_forward(x, gn, sc, sh, gate, ws, name, rider=None, loss_head=None):
    t, d = x.shape
    f = ws[0].shape[0]
    tm, tf = _tile(t, FFN_FWD_TILE[0], 16), _tile(f, FFN_FWD_TILE[1])
    nf = f // tf
    n_in = 5 if loss_head is None else 7

    def body(*refs):
        x_ref, gn_ref, sc_ref, sh_ref, gate_ref = refs[:5]
        w1_ref, w3_ref, w2_ref, xo_ref, h_ref, a_ref, b_ref, y_ref = refs[n_in:n_in + 8]
        hs, acc = refs[-2:]
        i, j = pl.program_id(0), pl.program_id(1)

        if loss_head is not None:
            @pl.when(jnp.logical_and(i == 0, j == 0))
            def _():
                refs[n_in + 9][...] = jnp.zeros_like(refs[n_in + 9])

        @pl.when(j == 0)
        def _():
            xhat, _ = _rms(x_ref[...])
            h = (xhat * gn_ref[...] * (1.0 + sc_ref[...]) + sh_ref[...]).astype(BF16)
            hs[...] = h
            h_ref[...] = h
            acc[...] = jnp.zeros_like(acc)

        h = hs[...]
        a = _dot(h, w1_ref[...], NT)
        b = _dot(h, w3_ref[...], NT)
        a_ref[...] = a.astype(BF16)
        b_ref[...] = b.astype(BF16)
        u = (a * _sigmoid(a) * b).astype(BF16)
        acc[...] += _dot(u, w2_ref[...])

        @pl.when(j == nf - 1)
        def _():
            y = acc[...]
            y_ref[...] = y.astype(BF16)
            x_out = x_ref[...] + 0.5 * gate_ref[...] * y
            if loss_head is None:
                xo_ref[...] = x_out
            else:
                dx, d_g, loss = _loss_head(x_out, refs[5][...], refs[6][...])
                xo_ref[...] = dx
                refs[n_in + 8][...] = (0.5 * gate_ref[...] * dx).astype(BF16)
                _add_rows(refs[n_in + 9], [d_g, loss])

    row = pl.BlockSpec((tm, d), lambda i, j: (i, 0))
    vec = pl.BlockSpec((1, d), lambda i, j: (0, 0))
    wide = pl.BlockSpec((tm, tf), lambda i, j: (i, j))
    head = loss_head is not None
    return _call_with_rider(
        body, rider, name=name, grid=(t // tm, nf),
        in_specs=[row, vec, vec, vec, vec] + ([row, vec] if head else [])
        + [pl.BlockSpec((tf, d), lambda i, j: (j, 0))] * 3,
        out_specs=[row, row, wide, wide, row] + ([row, pl.BlockSpec((8, d), lambda i, j: (0, 0))] if head else []),
        out_shape=[jax.ShapeDtypeStruct((t, d), F32), jax.ShapeDtypeStruct((t, d), BF16),
                   jax.ShapeDtypeStruct((t, f), BF16), jax.ShapeDtypeStruct((t, f), BF16),
                   jax.ShapeDtypeStruct((t, d), BF16)]
        + ([jax.ShapeDtypeStruct((t, d), BF16), jax.ShapeDtypeStruct((8, d), F32)] if head else []),
        scratch_shapes=[pltpu.VMEM((tm, d), BF16), pltpu.VMEM((tm, d), F32)],
        operands=(x, gn, sc, sh, gate) + (tuple(loss_head) if head else ()) + tuple(ws))


def _loss_head(x, target, g):
    d = x.shape[-1]
    xhat, r = _rms(x)
    err = xhat * g - target
    dyf = err * (1.0 / d)
    dxh = dyf * g
    dx = r * (dxh - xhat * jnp.mean(dxh * xhat, axis=-1, keepdims=True))
    return dx, jnp.sum(dyf * xhat, axis=0, keepdims=True), jnp.sum(err * err, axis=0, keepdims=True) * (0.5 / d)


def ffn_backward_gate(dy, a, b, w2, name, rider=None):
    t, d = dy.shape
    f = w2.shape[0]
    tm, tf = _tile(t, FFN_BWD_TILE[0], 16), _tile(f, FFN_BWD_TILE[1])
    nf = f // tf

    def gate_body(dy_ref, a_ref, b_ref, w2_ref, da_ref, db_ref, gw2_ref):
        dy_v = dy_ref[...]
        du = _dot(dy_v, w2_ref[...], NT)
        av = a_ref[...].astype(F32)
        bv = b_ref[...].astype(F32)
        s = _sigmoid(av)
        sa = av * s
        da_ref[...] = (du * bv * (s + sa * (1.0 - s))).astype(BF16)
        db_ref[...] = (du * sa).astype(BF16)
        part = _dot((sa * bv).astype(BF16), dy_v, TN)

        @pl.when(pl.program_id(1) == 0)
        def _():
            gw2_ref[...] = part

        @pl.when(pl.program_id(1) > 0)
        def _():
            gw2_ref[...] += part

    hidden = jax.ShapeDtypeStruct((t, f), BF16)
    wide_t = pl.BlockSpec((tm, tf), lambda j, i: (i, j))
    return _call_with_rider(
        gate_body, rider, name=name, grid=(nf, t // tm),
        in_specs=[pl.BlockSpec((tm, d), lambda j, i: (i, 0)), wide_t, wide_t,
                  pl.BlockSpec((tf, d), lambda j, i: (j, 0))],
        out_specs=[wide_t, wide_t, pl.BlockSpec((tf, d), lambda j, i: (j, 0))],
        out_shape=[hidden, hidden, jax.ShapeDtypeStruct((f, d), F32)],
        scratch_shapes=[], operands=(dy, a, b, w2))


def ffn_backward_norm(da, db, dxo, x, y, gn, sc, w1t, w3t, name, rider=None):
    t, d = x.shape
    f = w1t.shape[0]
    tm, tf = _tile(t, FFN_BWD_TILE[0], 16), _tile(f, FFN_BWD_TILE[1])
    nf = f // tf
    row = pl.BlockSpec((tm, d), lambda i, j: (i, 0))
    vec = pl.BlockSpec((1, d), lambda i, j: (0, 0))
    wide = pl.BlockSpec((tm, tf), lambda i, j: (i, j))

    def norm_body(da_ref, db_ref, w1_ref, w3_ref, dxo_ref, x_ref, y_ref, gn_ref, sc_ref, dx_ref, sums_ref, acc):
        i, j = pl.program_id(0), pl.program_id(1)

        @pl.when(jnp.logical_and(i == 0, j == 0))
        def _():
            sums_ref[...] = jnp.zeros_like(sums_ref)

        part = _dot(da_ref[...], w1_ref[...]) + _dot(db_ref[...], w3_ref[...])

        @pl.when(j == 0)
        def _():
            acc[...] = part

        @pl.when(jnp.logical_and(j > 0, j < nf - 1))
        def _():
            acc[...] += part

        @pl.when(j == nf - 1)
        def _():
            dh = part if nf == 1 else acc[...] + part
            dxo_v = dxo_ref[...]
            dx, d_sh, d_sc, d_gn = _norm_mod_bwd(dh, x_ref[...], gn_ref[...], sc_ref[...])
            dx_ref[...] = dxo_v + dx
            d_gate = jnp.sum(dxo_v * (0.5 * y_ref[...].astype(F32)), axis=0, keepdims=True)
            _add_rows(sums_ref, [d_sh, d_sc, d_gate, d_gn])

    w_spec = pl.BlockSpec((tf, d), lambda i, j: (j, 0))
    return _call_with_rider(
        norm_body, rider, name=name, grid=(t // tm, nf),
        in_specs=[wide, wide, w_spec, w_spec, row, row, row, vec, vec],
        out_specs=[row, pl.BlockSpec((8, d), lambda i, j: (0, 0))],
        out_shape=[jax.ShapeDtypeStruct((t, d), F32), jax.ShapeDtypeStruct((8, d), F32)],
        scratch_shapes=[pltpu.VMEM((tm, d), F32)],
        operands=(da, db, w1t, w3t, dxo, x, y, gn, sc))


def matmul_tn(a, b, name, rider=None):
    t, m = a.shape
    n = b.shape[1]
    tm, tn, tk = _tile(m, GRAD_TILE), _tile(n, GRAD_TILE), _tile(t, GRAD_DEPTH, 16)
    nk = t // tk

    def body(a_ref, b_ref, o_ref, acc):
        k = pl.program_id(2)

        @pl.when(k == 0)
        def _():
            acc[...] = jnp.zeros_like(acc)

        acc[...] += _dot(a_ref[...], b_ref[...], TN)

        @pl.when(k == nk - 1)
        def _():
            o_ref[...] = acc[...]

    out = _call_with_rider(
        body, rider, name=name, grid=(m // tm, n // tn, nk),
        in_specs=[pl.BlockSpec((tk, tm), lambda i, j, k: (k, i)), pl.BlockSpec((tk, tn), lambda i, j, k: (k, j))],
        out_specs=[pl.BlockSpec((tm, tn), lambda i, j, k: (i, j))],
        out_shape=[jax.ShapeDtypeStruct((m, n), F32)],
        scratch_shapes=[pltpu.VMEM((tm, tn), F32)], operands=(a, b))
    return out[0] if rider is None else out


def mix_in_forward(x, gn, sc, sh, w_in):
    t, d = x.shape
    tm = _tile(t, ROW_TILE, 16)

    def body(x_ref, gn_ref, sc_ref, sh_ref, w_ref, h_ref, zc_ref, zm_ref):
        xhat, _ = _rms(x_ref[...])
        h = (xhat * gn_ref[...] * (1.0 + sc_ref[...]) + sh_ref[...]).astype(BF16)
        h_ref[...] = h
        z = _dot(h, w_ref[...], NT)
        zc_ref[...] = z[:, :ZC_COLS].astype(BF16)
        zm_ref[...] = z[:, ZC_COLS:].astype(BF16)

    row = pl.BlockSpec((tm, d), lambda i: (i, 0))
    vec = pl.BlockSpec((1, d), lambda i: (0, 0))
    return pl.pallas_call(
        body, name="mix_in_fwd", grid=(t // tm,),
        in_specs=[row, vec, vec, vec, _row(w_in)],
        out_specs=[row, pl.BlockSpec((tm, ZC_COLS), lambda i: (i, 0)), pl.BlockSpec((tm, ZM_COLS), lambda i: (i, 0))],
        out_shape=[jax.ShapeDtypeStruct((t, d), BF16), jax.ShapeDtypeStruct((t, ZC_COLS), BF16),
                   jax.ShapeDtypeStruct((t, ZM_COLS), BF16)],
        compiler_params=_params(("arbitrary",)),
    )(x, gn, sc, sh, w_in)


def _rope_tables(pos, inv_freq):
    ang = pos * inv_freq
    lane = lax.broadcasted_iota(jnp.int32, ang.shape, 1)
    cos, sin = jnp.cos(ang), jnp.sin(ang)
    half = QK_ROPE // 2
    return cos, jnp.where(lane < half, -sin, 0.0), jnp.where(jnp.logical_and(lane >= half, lane < QK_ROPE), sin, 0.0)


def _rope(v, tables):
    cos, sin_a, sin_b = tables
    return v * cos + pltpu.roll(v, LANES - QK_ROPE // 2, 1) * sin_a + pltpu.roll(v, QK_ROPE // 2, 1) * sin_b


def _rope_transposed(dv, tables):
    cos, sin_a, sin_b = tables
    return dv * cos + pltpu.roll(dv * sin_a, QK_ROPE // 2, 1) + pltpu.roll(dv * sin_b, LANES - QK_ROPE // 2, 1)


def mla_project(zm, pos, inv_freq, qg, kvg, w_uq, w_ukv):
    t = zm.shape[0]
    tm = _tile(t, ROW_TILE, 16)

    def body(zm_ref, pos_ref, if_ref, qg_ref, kvg_ref, wq_ref, wkv_ref, qn_ref, kvn_ref, q_ref, k_ref, v_ref):
        zv = zm_ref[...].astype(F32)
        qn = (_rms(zv[:, :Q_LORA])[0] * qg_ref[...]).astype(BF16)
        kvn = (_rms(zv[:, Q_LORA:Q_LORA + KV_LORA])[0] * kvg_ref[...]).astype(BF16)
        qn_ref[...] = qn
        kvn_ref[...] = kvn
        qf = _dot(qn, wq_ref[...], NT) * QK_FOLD
        kvf = _dot(kvn, wkv_ref[...], NT)
        tables = _rope_tables(pos_ref[...], if_ref[...])
        kr = _rope(zv[:, Q_LORA + KV_LORA:], tables).astype(BF16)
        for h in range(MLA_HEADS):
            lo = h * HEAD_PAD
            q_ref[:, lo:lo + QK_NOPE] = qf[:, lo:lo + QK_NOPE].astype(BF16)
            q_ref[:, lo + QK_NOPE:lo + HEAD_PAD] = _rope(qf[:, lo + QK_NOPE:lo + HEAD_PAD], tables).astype(BF16)
            k_ref[:, lo:lo + QK_NOPE] = kvf[:, h * QK_NOPE:(h + 1) * QK_NOPE].astype(BF16)
            k_ref[:, lo + QK_NOPE:lo + HEAD_PAD] = kr
        v_ref[...] = kvf[:, MLA_HEADS * QK_NOPE:].astype(BF16)

    def rows(n):
        return pl.BlockSpec((tm, n), lambda i: (i, 0))

    return pl.pallas_call(
        body, name="mla_project", grid=(t // tm,),
        in_specs=[rows(ZM_COLS), rows(1), _row(inv_freq), _row(qg), _row(kvg), _row(w_uq), _row(w_ukv)],
        out_specs=[rows(Q_LORA), rows(KV_LORA), rows(QK_COLS), rows(QK_COLS), rows(MLA_WIDTH)],
        out_shape=[jax.ShapeDtypeStruct((t, Q_LORA), BF16), jax.ShapeDtypeStruct((t, KV_LORA), BF16),
                   jax.ShapeDtypeStruct((t, QK_COLS), BF16), jax.ShapeDtypeStruct((t, QK_COLS), BF16),
                   jax.ShapeDtypeStruct((t, MLA_WIDTH), BF16)],
        compiler_params=_params(("arbitrary",)),
    )(zm, pos, inv_freq, qg, kvg, w_uq, w_ukv)


def _chunk_mask(shape, q_axis):
    qi = lax.broadcasted_iota(jnp.int32, shape, q_axis) // CHUNK
    ki = lax.broadcasted_iota(jnp.int32, shape, 1 - q_axis) // CHUNK
    return ki <= qi


def attention_forward(q, k, v, rider=None):
    t = q.shape[0]
    tq = _tile(t, ATTN_TILE, CHUNK)

    def body(q_ref, k_ref, v_ref, o_ref, lse_ref):
        i = pl.program_id(1)
        qv = q_ref[...]

        def step(kb, carry, masked, tiles=1):
            m, l, acc = carry
            keys = pl.ds(pl.multiple_of(kb * tq, tq), tiles * tq)
            s = _dot(qv, k_ref[keys, :], NT)
            if masked:
                s = jnp.where(_chunk_mask(s.shape, 0), s, NEG_INF)
            m_new = jnp.maximum(m, jnp.max(s, axis=-1, keepdims=True))
            alpha = jnp.exp2(m - m_new)
            p = jnp.exp2(s - m_new)
            l = alpha * l + jnp.sum(p, axis=-1, keepdims=True)
            acc = alpha * acc + _dot(p.astype(BF16), v_ref[keys, :])
            return m_new, l, acc

        init = (jnp.full((tq, 1), NEG_INF, F32), jnp.zeros((tq, 1), F32), jnp.zeros((tq, V_HEAD), F32))
        carry = lax.fori_loop(0, i // 2, lambda pb, cr: step(2 * pb, cr, False, 2), init)
        carry = lax.fori_loop(0, i % 2, lambda _, cr: step(i - 1, cr, False), carry)
        m, l, acc = step(i, carry, True)
        o_ref[...] = (acc / l).astype(BF16)
        lse_ref[0] = m + jnp.log2(l)

    return _call_with_rider(
        body, rider, name="attn_fwd", grid=(MLA_HEADS, t // tq),
        in_specs=[pl.BlockSpec((tq, HEAD_PAD), lambda h, i: (i, h)),
                  pl.BlockSpec((t, HEAD_PAD), lambda h, i: (0, h)),
                  pl.BlockSpec((t, V_HEAD), lambda h, i: (0, h))],
        out_specs=[pl.BlockSpec((tq, V_HEAD), lambda h, i: (i, h)),
                   pl.BlockSpec((1, tq, 1), lambda h, i: (h, i, 0))],
        out_shape=[jax.ShapeDtypeStruct((t, MLA_WIDTH), BF16), jax.ShapeDtypeStruct((MLA_HEADS, t, 1), F32)],
        scratch_shapes=[], operands=(q, k, v))


def attention_backward(q, k, v, do, lse, delta, rider=None):
    t = q.shape[0]
    tq = _tile(t, ATTN_TILE, CHUNK)
    nq = t // tq

    def body(q_ref, k_ref, v_ref, do_ref, lse_ref, delta_ref, dq_ref, dk_ref, dv_ref, dq_acc):
        kb = pl.program_id(1)

        @pl.when(kb == 0)
        def _():
            dq_acc[...] = jnp.zeros_like(dq_acc)

        kv, vv = k_ref[...], v_ref[...]

        def step(qb, carry, masked):
            dk, dv = carry
            rows = pl.ds(pl.multiple_of(qb * tq, tq), tq)
            qv, dov = q_ref[rows, :], do_ref[rows, :]
            s = _dot(kv, qv, NT)
            if masked:
                s = jnp.where(_chunk_mask(s.shape, 1), s, NEG_INF)
            p = jnp.exp2(s - lse_ref[0, qb])
            dv = dv + _dot(p.astype(BF16), dov)
            dp = _dot(vv, dov, NT)
            ds = (p * (dp - delta_ref[0, qb]) * LN_2).astype(BF16)
            dk = dk + _dot(ds, qv)
            dq_acc[rows, :] += _dot(ds, kv, TN)
            return dk, dv

        carry = step(kb, (jnp.zeros((tq, HEAD_PAD), F32), jnp.zeros((tq, V_HEAD), F32)), True)
        odd = (nq - 1 - kb) % 2
        carry = lax.fori_loop(0, odd, lambda _, cr: step(kb + 1, cr, False), carry)
        first = kb + 1 + odd
        dk, dv = lax.fori_loop(0, (nq - first) // 2,
                               lambda pb, cr: step(first + 2 * pb + 1, step(first + 2 * pb, cr, False), False), carry)
        dk_ref[...] = dk.astype(BF16)
        dv_ref[...] = dv.astype(BF16)

        @pl.when(kb == nq - 1)
        def _():
            dq_ref[...] = dq_acc[...].astype(BF16)

    stat = pl.BlockSpec((1, nq, 1, tq), lambda h, j: (h, 0, 0, 0))
    return _call_with_rider(
        body, rider, name="attn_bwd", grid=(MLA_HEADS, nq),
        in_specs=[pl.BlockSpec((t, HEAD_PAD), lambda h, j: (0, h)),
                  pl.BlockSpec((tq, HEAD_PAD), lambda h, j: (j, h)),
                  pl.BlockSpec((tq, V_HEAD), lambda h, j: (j, h)),
                  pl.BlockSpec((t, V_HEAD), lambda h, j: (0, h)), stat, stat],
        out_specs=[pl.BlockSpec((t, HEAD_PAD), lambda h, j: (0, h)),
                   pl.BlockSpec((tq, HEAD_PAD), lambda h, j: (j, h)),
                   pl.BlockSpec((tq, V_HEAD), lambda h, j: (j, h))],
        out_shape=[jax.ShapeDtypeStruct((t, QK_COLS), BF16), jax.ShapeDtypeStruct((t, QK_COLS), BF16),
                   jax.ShapeDtypeStruct((t, MLA_WIDTH), BF16)],
        scratch_shapes=[pltpu.VMEM((t, HEAD_PAD), F32)], operands=(q, k, v, do, lse, delta))


HALO = 16


def _halo_spec(tm, n, step, last):
    return pl.BlockSpec((HALO, n), lambda i: (jnp.clip(i * (tm // HALO) + step, 0, last), 0))


def _shift_rows(v, prev, n):
    out = pltpu.roll(v, n, 0)
    row = lax.broadcasted_iota(jnp.int32, v.shape, 0)
    for r in range(n):
        out = jnp.where(row == r, prev[HALO - n + r:HALO - n + r + 1, :], out)
    return out


def _advance_rows(v, nxt, n):
    rows = v.shape[0]
    out = pltpu.roll(v, rows - n, 0)
    row = lax.broadcasted_iota(jnp.int32, v.shape, 0)
    for r in range(n):
        out = jnp.where(row == rows - n + r, nxt[r:r + 1, :], out)
    return out


def _conv_taps(zc, zc_prev, first):
    w = CONV_WIDTH
    u = zc[:, w:2 * w] * zc[:, 2 * w:]
    up = jnp.where(first, 0.0, zc_prev[:, w:2 * w] * zc_prev[:, 2 * w:])
    return u, _shift_rows(u, up, 1), _shift_rows(u, up, 2)


def mix_out_forward(zc, o, conv_w, og, gmat_a, gmat_b, w_out, x, gate):
    t, d = x.shape
    tm = _tile(t, ROW_TILE, 16)
    w = CONV_WIDTH

    def body(zc_ref, zp_ref, o_ref, cw_ref, og_ref, ga_ref, gb_ref, w_ref, x_ref, gate_ref,
             xo_ref, yn_ref, y_ref, ya_ref):
        zc_v = zc_ref[...].astype(F32)
        u, u1, u2 = _conv_taps(zc_v, zp_ref[...].astype(F32), pl.program_id(0) == 0)
        cw = cw_ref[...]
        ya = zc_v[:, :w] * (cw[0:1] * u2 + cw[1:2] * u1 + cw[2:3] * u)
        ya_ref[...] = ya.astype(BF16)
        ov = o_ref[...].astype(F32)
        ogv = og_ref[...]
        yn_ref[:, :w] = (ya * lax.rsqrt(_group_mean(ya * ya, ga_ref[...]) + EPS) * ogv[:, :w]).astype(BF16)
        yn_ref[:, w:] = (ov * lax.rsqrt(_group_mean(ov * ov, gb_ref[...]) + EPS) * ogv[:, w:]).astype(BF16)
        y = _dot(yn_ref[...], w_ref[...])
        y_ref[...] = y.astype(BF16)
        xo_ref[...] = x_ref[...] + gate_ref[...] * y

    def rows(n):
        return pl.BlockSpec((tm, n), lambda i: (i, 0))

    return pl.pallas_call(
        body, name="mix_out_fwd", grid=(t // tm,),
        in_specs=[rows(ZC_COLS), _halo_spec(tm, ZC_COLS, -1, t // HALO - 1), rows(MLA_WIDTH), _row(conv_w), _row(og),
                  _row(gmat_a), _row(gmat_b), _row(w_out), rows(d), _row(gate)],
        out_specs=[rows(d), rows(MIX_WIDTH), rows(d), rows(w)],
        out_shape=[jax.ShapeDtypeStruct((t, d), F32), jax.ShapeDtypeStruct((t, MIX_WIDTH), BF16),
                   jax.ShapeDtypeStruct((t, d), BF16), jax.ShapeDtypeStruct((t, w), BF16)],
        compiler_params=_params(("arbitrary",)),
    )(zc, zc, o, conv_w, og, gmat_a, gmat_b, w_out, x, gate)


def _group_norm_bwd(dyn, y, og, gmat):
    rs = lax.rsqrt(_group_mean(y * y, gmat) + EPS)
    yhat = y * rs
    d_og = jnp.sum(dyn * yhat, axis=0, keepdims=True)
    dyh = dyn * og
    return rs * (dyh - yhat * _group_mean(dyh * yhat, gmat)), d_og


def mix_out_backward(dxo, y, gate, ya, o, yn, og, gmat_a, gmat_b, w_out, rider=None):
    t, d = dxo.shape
    tm = _tile(t, ROW_TILE, 16)
    w = CONV_WIDTH

    def body(dxo_ref, y_ref, gate_ref, ya_ref, o_ref, yn_ref, og_ref, ga_ref, gb_ref, w_ref,
             dya_ref, do_ref, delta_ref, sd_ref, so_ref, gw_ref):
        @pl.when(pl.program_id(0) == 0)
        def _():
            sd_ref[...] = jnp.zeros_like(sd_ref)
            so_ref[...] = jnp.zeros_like(so_ref)
            gw_ref[...] = jnp.zeros_like(gw_ref)

        dxo_v = dxo_ref[...]
        dy = (gate_ref[...] * dxo_v).astype(BF16)
        gw_ref[...] += _dot(yn_ref[...], dy, TN)
        sd_ref[0:1, :] += jnp.sum(dxo_v * y_ref[...].astype(F32), axis=0, keepdims=True)
        dyn = _dot(dy, w_ref[...], NT)
        ogv = og_ref[...]
        ov = o_ref[...].astype(F32)
        dya, d_og_a = _group_norm_bwd(dyn[:, :w], ya_ref[...].astype(F32), ogv[:, :w], ga_ref[...])
        dov, d_og_b = _group_norm_bwd(dyn[:, w:], ov, ogv[:, w:], gb_ref[...])
        dya_ref[...] = dya.astype(BF16)
        do_ref[...] = dov.astype(BF16)
        so_ref[0:1, :w] += d_og_a
        so_ref[0:1, w:] += d_og_b
        prod = dov * ov
        for h in range(MLA_HEADS):
            delta_ref[h] = jnp.sum(prod[:, h * V_HEAD:(h + 1) * V_HEAD], axis=-1, keepdims=True)

    def rows(n):
        return pl.BlockSpec((tm, n), lambda i: (i, 0))

    return _call_with_rider(
        body, rider, name="mix_out_bwd", grid=(t // tm,),
        in_specs=[rows(d), rows(d), _row(gate), rows(w), rows(MLA_WIDTH), rows(MIX_WIDTH), _row(og), _row(gmat_a),
                  _row(gmat_b), _row(w_out)],
        out_specs=[rows(w), rows(MLA_WIDTH), pl.BlockSpec((MLA_HEADS, tm, 1), lambda i: (0, i, 0)),
                   pl.BlockSpec((8, d), lambda i: (0, 0)), pl.BlockSpec((8, MIX_WIDTH), lambda i: (0, 0)),
                   pl.BlockSpec((MIX_WIDTH, d), lambda i: (0, 0))],
        out_shape=[jax.ShapeDtypeStruct((t, w), BF16),
                   jax.ShapeDtypeStruct((t, MLA_WIDTH), BF16), jax.ShapeDtypeStruct((MLA_HEADS, t, 1), F32),
                   jax.ShapeDtypeStruct((8, d), F32), jax.ShapeDtypeStruct((8, MIX_WIDTH), F32),
                   jax.ShapeDtypeStruct((MIX_WIDTH, d), F32)],
        scratch_shapes=[], operands=(dxo, y, gate, ya, o, yn, og, gmat_a, gmat_b, w_out))


def conv_backward(zc, dya, conv_w, h):
    t, d = h.shape
    tm = _tile(t, ROW_TILE, 16)
    nt = t // tm
    w = CONV_WIDTH

    def body(zc_ref, zp_ref, zn_ref, dya_ref, dn_ref, cw_ref, h_ref, dzc_ref, sums_ref, gw_ref):
        i = pl.program_id(0)

        @pl.when(i == 0)
        def _():
            sums_ref[...] = jnp.zeros_like(sums_ref)
            gw_ref[...] = jnp.zeros_like(gw_ref)

        zc_v = zc_ref[...].astype(F32)
        u, u1, u2 = _conv_taps(zc_v, zp_ref[...].astype(F32), i == 0)
        cw = cw_ref[...]
        dya_v = dya_ref[...].astype(F32)
        dyc = dya_v * zc_v[:, :w]
        dyc_next = jnp.where(i == nt - 1, 0.0, dn_ref[...].astype(F32) * zn_ref[:, :w].astype(F32))
        du = cw[2:3] * dyc + cw[1:2] * _advance_rows(dyc, dyc_next, 1) + cw[0:1] * _advance_rows(dyc, dyc_next, 2)
        dzc_ref[:, :w] = (dya_v * (cw[0:1] * u2 + cw[1:2] * u1 + cw[2:3] * u)).astype(BF16)
        dzc_ref[:, w:2 * w] = (du * zc_v[:, 2 * w:]).astype(BF16)
        dzc_ref[:, 2 * w:] = (du * zc_v[:, w:2 * w]).astype(BF16)
        _add_rows(sums_ref, [jnp.sum(dyc * tap, axis=0, keepdims=True) for tap in (u2, u1, u)])
        gw_ref[...] += _dot(dzc_ref[...], h_ref[...], TN)

    def rows(n):
        return pl.BlockSpec((tm, n), lambda i: (i, 0))

    def halo(n, step):
        return _halo_spec(tm, n, step, t // HALO - 1)

    return pl.pallas_call(
        body, name="conv_bwd", grid=(nt,),
        in_specs=[rows(ZC_COLS), halo(ZC_COLS, -1), halo(ZC_COLS, tm // HALO), rows(w), halo(w, tm // HALO),
                  _row(conv_w), rows(d)],
        out_specs=[rows(ZC_COLS), pl.BlockSpec((8, w), lambda i: (0, 0)), pl.BlockSpec((ZC_COLS, d), lambda i: (0, 0))],
        out_shape=[jax.ShapeDtypeStruct((t, ZC_COLS), BF16), jax.ShapeDtypeStruct((8, w), F32),
                   jax.ShapeDtypeStruct((ZC_COLS, d), F32)],
        compiler_params=_params(("arbitrary",)),
    )(zc, zc, zc, dya, dya, conv_w, h)


def _rms_bwd(dy, x, g):
    xhat, r = _rms(x)
    d_g = jnp.sum(dy * xhat, axis=0, keepdims=True)
    dxh = dy * g
    return r * (dxh - xhat * jnp.mean(dxh * xhat, axis=-1, keepdims=True)), d_g


def mla_project_backward(dq, dk, dv, zm, qn, kvn, h, pos, inv_freq, qg, kvg, w_uq, w_ukv):
    t, d = h.shape
    tm = _tile(t, ROW_TILE, 16)

    def body(dq_ref, dk_ref, dv_ref, zm_ref, qn_ref, kvn_ref, h_ref, pos_ref, if_ref, qg_ref, kvg_ref, wq_ref,
             wkv_ref, dzm_ref, sums_ref, guq_ref, gukv_ref, gin_ref, dql_ref, dkvl_ref):
        @pl.when(pl.program_id(0) == 0)
        def _():
            sums_ref[...] = jnp.zeros_like(sums_ref)
            guq_ref[...] = jnp.zeros_like(guq_ref)
            gukv_ref[...] = jnp.zeros_like(gukv_ref)
            gin_ref[...] = jnp.zeros_like(gin_ref)

        tables = _rope_tables(pos_ref[...], if_ref[...])
        dkr = jnp.zeros((tm, LANES), F32)
        for h in range(MLA_HEADS):
            lo = h * HEAD_PAD
            dql_ref[:, lo:lo + QK_NOPE] = (dq_ref[:, lo:lo + QK_NOPE].astype(F32) * QK_FOLD).astype(BF16)
            dql_ref[:, lo + QK_NOPE:lo + HEAD_PAD] = _rope_transposed(
                dq_ref[:, lo + QK_NOPE:lo + HEAD_PAD].astype(F32) * QK_FOLD, tables).astype(BF16)
            dkvl_ref[:, h * QK_NOPE:(h + 1) * QK_NOPE] = dk_ref[:, lo:lo + QK_NOPE]
            dkr = dkr + dk_ref[:, lo + QK_NOPE:lo + HEAD_PAD].astype(F32)
        dkvl_ref[:, MLA_HEADS * QK_NOPE:] = dv_ref[...]
        zv = zm_ref[...].astype(F32)
        dqn = _dot(dql_ref[...], wq_ref[...])
        dkvn = _dot(dkvl_ref[...], wkv_ref[...])
        dcq, d_qg = _rms_bwd(dqn, zv[:, :Q_LORA], qg_ref[...])
        dckv, d_kvg = _rms_bwd(dkvn, zv[:, Q_LORA:Q_LORA + KV_LORA], kvg_ref[...])
        dzm_ref[:, :Q_LORA] = dcq.astype(BF16)
        dzm_ref[:, Q_LORA:Q_LORA + KV_LORA] = dckv.astype(BF16)
        dzm_ref[:, Q_LORA + KV_LORA:] = _rope_transposed(dkr, tables).astype(BF16)
        sums_ref[0:1, :Q_LORA] += d_qg
        sums_ref[0:1, Q_LORA:Q_LORA + KV_LORA] += d_kvg
        guq_ref[...] += _dot(dql_ref[...], qn_ref[...], TN)
        gukv_ref[...] += _dot(dkvl_ref[...], kvn_ref[...], TN)
        gin_ref[...] += _dot(dzm_ref[...], h_ref[...], TN)

    def rows(n):
        return pl.BlockSpec((tm, n), lambda i: (i, 0))

    def whole(r, n):
        return pl.BlockSpec((r, n), lambda i: (0, 0))

    return pl.pallas_call(
        body, name="mla_project_bwd", grid=(t // tm,),
        in_specs=[rows(QK_COLS), rows(QK_COLS), rows(MLA_WIDTH), rows(ZM_COLS), rows(Q_LORA), rows(KV_LORA), rows(d),
                  rows(1), _row(inv_freq), _row(qg), _row(kvg), _row(w_uq), _row(w_ukv)],
        out_specs=[rows(ZM_COLS), whole(8, ZM_COLS), whole(QK_COLS, Q_LORA), whole(QK_COLS, KV_LORA),
                   whole(ZM_COLS, d)],
        out_shape=[jax.ShapeDtypeStruct((t, ZM_COLS), BF16), jax.ShapeDtypeStruct((8, ZM_COLS), F32),
                   jax.ShapeDtypeStruct((QK_COLS, Q_LORA), F32), jax.ShapeDtypeStruct((QK_COLS, KV_LORA), F32),
                   jax.ShapeDtypeStruct((ZM_COLS, d), F32)],
        scratch_shapes=[pltpu.VMEM((tm, QK_COLS), BF16), pltpu.VMEM((tm, QK_COLS), BF16)],
        compiler_params=_params(("arbitrary",)),
    )(dq, dk, dv, zm, qn, kvn, h, pos, inv_freq, qg, kvg, w_uq, w_ukv)


def mix_in_backward(dzc, dzm, w_in, x, dxo, gn, sc, gate, rider=None):
    t, d = x.shape
    tm = _tile(t, ROW_TILE, 16)

    def body(dzc_ref, dzm_ref, w_ref, x_ref, dxo_ref, gn_ref, sc_ref, gate_ref, dx_ref, dy_ref, sums_ref):
        @pl.when(pl.program_id(0) == 0)
        def _():
            sums_ref[...] = jnp.zeros_like(sums_ref)

        dh = _dot(dzc_ref[...], w_ref[:ZC_COLS, :]) + _dot(dzm_ref[...], w_ref[ZC_COLS:, :])
        dx, d_sh, d_sc, d_gn = _norm_mod_bwd(dh, x_ref[...], gn_ref[...], sc_ref[...])
        dx = dxo_ref[...] + dx
        dx_ref[...] = dx
        dy_ref[...] = (0.5 * gate_ref[...] * dx).astype(BF16)
        _add_rows(sums_ref, [d_sh, d_sc, d_gn])

    def rows(n):
        return pl.BlockSpec((tm, n), lambda i: (i, 0))

    return _call_with_rider(
        body, rider, name="mix_in_bwd", grid=(t // tm,),
        in_specs=[rows(ZC_COLS), rows(ZM_COLS), _row(w_in), rows(d), rows(d), _row(gn), _row(sc), _row(gate)],
        out_specs=[rows(d), rows(d), pl.BlockSpec((8, d), lambda i: (0, 0))],
        out_shape=[jax.ShapeDtypeStruct((t, d), F32), jax.ShapeDtypeStruct((t, d), BF16),
                   jax.ShapeDtypeStruct((8, d), F32)],
        scratch_shapes=[], operands=(dzc, dzm, w_in, x, dxo, gn, sc, gate))


def _adamw_step(w, g, m, v):
    m_new = ADAM_B1 * m + (1.0 - ADAM_B1) * g
    v_new = ADAM_B2 * v + (1.0 - ADAM_B2) * (g * g)
    m_hat = m_new / (1.0 - ADAM_B1 ** ADAM_STEP)
    v_hat = v_new / (1.0 - ADAM_B2 ** ADAM_STEP)
    return -ADAM_LR * (m_hat / (jnp.sqrt(v_hat) + ADAM_EPS) + ADAM_WD * w), m_new, v_new


def adamw(w, g, m, v, name):
    r, n = w.shape
    tr = _tile(r, max(8, ADAM_TILE_ELEMS // n), 8)

    def body(w_ref, g_ref, m_ref, v_ref, d_ref, mo_ref, vo_ref):
        d_ref[...], mo_ref[...], vo_ref[...] = _adamw_step(w_ref[...], g_ref[...], m_ref[...], v_ref[...])

    blk = pl.BlockSpec((tr, n), lambda i: (i, 0))
    shape = jax.ShapeDtypeStruct((r, n), F32)
    return pl.pallas_call(
        body, name=name, grid=(r // tr,), in_specs=[blk] * 4, out_specs=[blk] * 3, out_shape=[shape] * 3,
        compiler_params=_params(("arbitrary",)),
    )(w, g, m, v)


def adamw_received(w, own, got, m, v, name):
    r, n = w.shape
    tr = _tile(r, SUM_ROWS, 16)

    def body(w_ref, own_ref, got_ref, m_ref, v_ref, g_ref, d_ref, mo_ref, vo_ref):
        g = own_ref[...]
        for j in range(3):
            g = g + got_ref[j].astype(F32)
        g_ref[...] = g
        d_ref[...], mo_ref[...], vo_ref[...] = _adamw_step(w_ref[...], g, m_ref[...], v_ref[...])

    blk = pl.BlockSpec((tr, n), lambda i: (i, 0))
    shape = jax.ShapeDtypeStruct((r, n), F32)
    return pl.pallas_call(
        body, name=name, grid=(r // tr,),
        in_specs=[blk, blk, pl.BlockSpec((3, tr, n), lambda i: (0, i, 0)), blk, blk],
        out_specs=[blk] * 4, out_shape=[shape] * 4, compiler_params=_params(("arbitrary",)),
    )(w, own, got, m, v)


def _pad_to(v, n):
    return jnp.pad(v, (0, n - v.shape[0]))


def _pad_heads(w, axis_len):
    n = w.shape[1]
    return jnp.pad(w.reshape(MLA_HEADS, axis_len, n), ((0, 0), (0, HEAD_PAD - axis_len), (0, 0))).reshape(-1, n)


def _swap_head_parts(w, inner, outer):
    n = w.shape[1]
    return w.reshape(outer, inner, QK_NOPE, n).transpose(1, 0, 2, 3).reshape(-1, n)


def kernel(x, c, positions, ada_w, ada_b, norm_ffn1_g, ffn1_w1, ffn1_w3, ffn1_w2, norm_mix_g, w_in, conv_w, q_norm_g, w_uq, kv_norm_g, w_ukv, out_norm_g, w_out, norm_ffn2_g, ffn2_w1, ffn2_w3, ffn2_w2, final_norm_g, loss_target, m_ada_w, m_ada_b, m_norm_ffn1_g, m_ffn1_w1, m_ffn1_w3, m_ffn1_w2, m_norm_mix_g, m_w_in, m_conv_w, m_q_norm_g, m_w_uq, m_kv_norm_g, m_w_ukv, m_out_norm_g, m_w_out, m_norm_ffn2_g, m_ffn2_w1, m_ffn2_w3, m_ffn2_w2, m_final_norm_g, v_ada_w, v_ada_b, v_norm_ffn1_g, v_ffn1_w1, v_ffn1_w3, v_ffn1_w2, v_norm_mix_g, v_w_in, v_conv_w, v_q_norm_g, v_w_uq, v_kv_norm_g, v_w_ukv, v_out_norm_g, v_w_out, v_norm_ffn2_g, v_ffn2_w1, v_ffn2_w3, v_ffn2_w2, v_final_norm_g):
    t, d = x.shape[1], x.shape[2]
    f = ffn1_w2.shape[1] * N_DEV
    me = 4 * lax.axis_index("x") + 2 * lax.axis_index("y") + lax.axis_index("c")
    my_c = lax.axis_index("c")
    my_chip = 2 * lax.axis_index("x") + lax.axis_index("y")
    xs = x[0]
    n_ada = ada_w.shape[2]
    cw_n = conv_w.shape[2]

    c_rows = jnp.broadcast_to(c, (8, d))
    conv_rows = jnp.pad(conv_w[0], ((0, 8 - CONV_K), (0, LANES - cw_n)))
    ffn1_blocks = [ffn1_w1[0].T.astype(BF16), ffn1_w3[0].T.astype(BF16), ffn1_w2[0].astype(BF16)]
    ffn2_blocks = [ffn2_w1[0].T.astype(BF16), ffn2_w3[0].T.astype(BF16), ffn2_w2[0].astype(BF16)]
    c_all, conv_all, *ffn1_all = all_gather_relayed([c_rows, conv_rows] + ffn1_blocks, [0] * 5, "gather_first")
    c_all = c_all[:, 0, :]
    conv_full8 = conv_all[:, :, :cw_n].transpose(1, 0, 2).reshape(8, CONV_WIDTH)
    ffn1_ws = [w.reshape(f, d) for w in ffn1_all]
    gather_mix = riding_gather(
        [w_in[0].T.astype(BF16), w_uq[0].T.astype(BF16), w_ukv[0].T.astype(BF16), w_out[0].astype(BF16)], [0, 0, 0, 0])

    ada_b_cols = lax.dynamic_slice_in_dim(ada_b, me * n_ada, n_ada, axis=1)
    mod_cols = ada_forward(c_all, ada_w[0], ada_b_cols)
    mod_all, = all_gather([mod_cols], [0], "gather_mod")
    mod = lax.dynamic_index_in_dim(mod_all, me, axis=1, keepdims=False).reshape(N_MOD, 1, d)
    sh1, sc1, g1, sh2, sc2, g2, sh3, sc3, g3 = [mod[i] for i in range(N_MOD)]

    gf = final_norm_g.reshape(1, d)
    x1, h1, a1, b1, y1, *gathered = ffn_forward(xs, norm_ffn1_g, sc1, sh1, g1, ffn1_ws, "ffn1_fwd", gather_mix)
    w_in_p = jnp.pad(gathered[0].reshape(IN_COLS, d), ((0, ZC_COLS + ZM_COLS - IN_COLS), (0, 0)))
    w_uq_p = _pad_heads(gathered[1].reshape(-1, Q_LORA), QK_NOPE + QK_ROPE)
    w_ukv_p = _swap_head_parts(gathered[2].reshape(-1, KV_LORA), 2, MLA_HEADS)
    w_out_f = gathered[3].reshape(MIX_WIDTH, d)
    h2, zc, zm = mix_in_forward(x1, norm_mix_g, sc2, sh2, w_in_p)
    pos = positions[0].astype(F32).reshape(t, 1)
    inv_freq = ROPE_THETA ** (-jnp.arange(0, QK_ROPE, 2, dtype=F32) / QK_ROPE)
    inv_freq = jnp.concatenate([inv_freq, inv_freq, jnp.zeros((LANES - QK_ROPE,), F32)]).reshape(1, LANES)
    qn, kvn, q, k, v = mla_project(zm, pos, inv_freq, q_norm_g, kv_norm_g, w_uq_p, w_ukv_p)
    o, lse, *ffn2_all = attention_forward(q, k, v, riding_gather(ffn2_blocks, [0] * 3))
    ffn2_ws = [w.reshape(f, d) for w in ffn2_all]
    lane = jnp.arange(CONV_WIDTH)
    gmat_a = (lane[:, None] // (CONV_WIDTH // CONV_GROUPS) == lane[None, :] // (CONV_WIDTH // CONV_GROUPS))
    gmat_a = (gmat_a / (CONV_WIDTH // CONV_GROUPS)).astype(BF16)
    gmat_b = ((lane[:, None] // V_HEAD == lane[None, :] // V_HEAD) / V_HEAD).astype(BF16)
    x2, yn, y2, ya = mix_out_forward(zc, o, conv_full8, out_norm_g, gmat_a, gmat_b, w_out_f, x1, g2)
    dx3, h3, a3, b3, y3, dy3, sums_f = ffn_forward(x2, norm_ffn2_g, sc3, sh3, g3, ffn2_ws, "ffn2_fwd",
                                                   loss_head=(loss_target[0], gf))

    chip_idx = jnp.bitwise_xor(my_chip, jnp.array([0, 2, 1, 3], jnp.int32)).astype(jnp.int32)
    src_idx = (2 * chip_idx + my_c).astype(jnp.int32)

    def row_blocks(named):
        return [g.reshape(N_DEV, g.shape[0] // N_DEV, g.shape[1]) for _, g in named]

    def chip_sums(named, g8, got):
        return [add_sibling(g, r, src_idx, chip_idx, "rs_add_" + n) for g, r, (n, _) in zip(g8, got, named)]

    da3, db3, g_w2b = ffn_backward_gate(dy3, a3, b3, ffn2_ws[2], "ffn2_bwd_gate")
    dx2, sums_3 = ffn_backward_norm(da3, db3, dx3, x2, y3, norm_ffn2_g, sc3, ffn2_ws[0], ffn2_ws[1], "ffn2_bwd_norm")
    ffn2_named = [("ffn2_w1", matmul_tn(da3, h3, "ffn2_gw1")), ("ffn2_w3", matmul_tn(db3, h3, "ffn2_gw3")),
                  ("ffn2_w2", g_w2b)]
    ffn2_g8 = row_blocks(ffn2_named)
    dya, do, delta, sums_2d, sums_2o, g_w_out, *ffn2_sib = mix_out_backward(
        dx2, y2, g2, ya, o, yn, out_norm_g, gmat_a, gmat_b, w_out_f, riding_sibling(ffn2_g8))
    ffn2_sums = chip_sums(ffn2_named, ffn2_g8, ffn2_sib)
    nq = t // _tile(t, ATTN_TILE, CHUNK)
    stat_shape = (MLA_HEADS, nq, 1, t // nq)
    dq, dk, dv, *ffn2_got = attention_backward(q, k, v, do, lse.reshape(stat_shape), delta.reshape(stat_shape),
                                               riding_exchange([s[1] for s in ffn2_sums]))
    dzc, sums_c, g_w_in_conv = conv_backward(zc, dya, conv_full8, h2)
    dzm, sums_m, g_w_uq_p, g_w_ukv_p, g_w_in_mla = mla_project_backward(
        dq, dk, dv, zm, qn, kvn, h2, pos, inv_freq, q_norm_g, kv_norm_g, w_uq_p, w_ukv_p)
    g_w_in = jnp.concatenate([g_w_in_conv, g_w_in_mla])[:IN_COLS]
    g_w_uq = g_w_uq_p.reshape(MLA_HEADS, HEAD_PAD, Q_LORA)[:, :QK_NOPE + QK_ROPE].reshape(-1, Q_LORA)
    g_w_ukv = _swap_head_parts(g_w_ukv_p, MLA_HEADS, 2)
    mix_named = [("w_in", g_w_in), ("w_uq", g_w_uq), ("w_ukv", g_w_ukv), ("w_out", g_w_out)]
    mix_g8 = row_blocks(mix_named)
    dx1, dy1, sums_1m, *mix_sib = mix_in_backward(dzc, dzm, w_in_p, x1, dx2, norm_mix_g, sc2, g1, riding_sibling(mix_g8))
    mix_sums = chip_sums(mix_named, mix_g8, mix_sib)
    da1, db1, g_w2a, *mix_got = ffn_backward_gate(dy1, a1, b1, ffn1_ws[2], "ffn1_bwd_gate",
                                                  riding_exchange([s[1] for s in mix_sums]))
    ffn1_pair = [("ffn1_w2", g_w2a), ("ffn1_w1", matmul_tn(da1, h1, "ffn1_gw1"))]
    pair_g8 = row_blocks(ffn1_pair)
    g_w3a, *pair_sib = matmul_tn(db1, h1, "ffn1_gw3", riding_sibling(pair_g8))
    ffn1_last = [("ffn1_w3", g_w3a)]
    last_g8 = row_blocks(ffn1_last)
    ffn1_named = ffn1_pair + ffn1_last
    ffn1_sums = chip_sums(ffn1_pair, pair_g8, pair_sib) + chip_sums(
        ffn1_last, last_g8, exchange_sibling(last_g8, "rs_sibling_ffn1_w3"))
    dx0, sums_1, *ffn1_got = ffn_backward_norm(da1, db1, dx1, xs, y1, norm_ffn1_g, sc1, ffn1_ws[0], ffn1_ws[1], "ffn1_bwd_norm",
                                               riding_exchange([s[1] for s in ffn1_sums]))
    reduced = {}
    for named, group_sums, group_got in ((ffn2_named, ffn2_sums, ffn2_got), (mix_named, mix_sums, mix_got),
                                         (ffn1_named, ffn1_sums, ffn1_got)):
        for (n, _), (own, _), got in zip(named, group_sums, group_got):
            reduced[n] = (own, got)

    dmod = jnp.concatenate([sums_1[0], sums_1[1], sums_1[2], sums_1m[0], sums_1m[1], sums_2d[0],
                            sums_3[0], sums_3[1], sums_3[2]])
    pieces = [dmod, sums_1[3], sums_1m[2], sums_m[0, :Q_LORA], sums_m[0, Q_LORA:Q_LORA + KV_LORA], sums_2o[0],
              sums_3[3], sums_f[0], sums_f[1], sums_c[:CONV_K].reshape(-1)]
    plens = [p.shape[0] for p in pieces]
    poffs = [sum(plens[:i]) for i in range(len(plens))]
    vec_len = -(-sum(plens) // 1024) * 1024
    vec = _pad_to(jnp.concatenate(pieces), vec_len).reshape(-1, LANES)
    vec_all, = all_gather([vec], [0], "gather_sums")
    tot = sum_devices(vec_all).reshape(-1)
    g_ada_b, g_n1, g_nmix, g_qg, g_kvg, g_og, g_n3, g_gf, loss_lanes, g_conv_full = [
        tot[o:o + n] for o, n in zip(poffs, plens)]
    loss = sum_lanes(loss_lanes.reshape(1, d))[0, 0]
    g_conv = lax.dynamic_slice_in_dim(g_conv_full.reshape(CONV_K, CONV_WIDTH), me * cw_n, cw_n, axis=1)
    dmod_all = vec_all.reshape(N_DEV, vec_len)[:, :N_MOD * d]
    dmod_cols = lax.dynamic_slice_in_dim(dmod_all, me * n_ada, n_ada, axis=1)
    g_ada_w = ada_backward(jnp.pad(c_all, ((0, 8), (0, 0))), jnp.pad(dmod_cols, ((0, 8), (0, 0))))

    def update(name, w, g, m, v, received=None):
        k, n = w.shape[-2:]
        if g.shape == (k, n):
            flat, back = (lambda a: a.reshape(k, n)), (lambda a: a.reshape(w.shape))
        else:
            flat, back = (lambda a: a.reshape(k, n).T), (lambda a: a.T.reshape(w.shape))
        if received is None:
            out = (g,) + tuple(adamw(flat(w), g, flat(m), flat(v), "adamw_" + name))
        else:
            out = adamw_received(flat(w), g, received, flat(m), flat(v), "adamw_" + name)
        return tuple(back(a) for a in out)

    res = {}
    res["ada_w"] = update("ada_w", ada_w, g_ada_w, m_ada_w, v_ada_w)
    big = [("ffn1_w1", ffn1_w1, m_ffn1_w1, v_ffn1_w1), ("ffn1_w3", ffn1_w3, m_ffn1_w3, v_ffn1_w3),
           ("ffn2_w1", ffn2_w1, m_ffn2_w1, v_ffn2_w1), ("ffn2_w3", ffn2_w3, m_ffn2_w3, v_ffn2_w3),
           ("w_in", w_in, m_w_in, v_w_in), ("w_uq", w_uq, m_w_uq, v_w_uq), ("w_ukv", w_ukv, m_w_ukv, v_w_ukv),
           ("ffn1_w2", ffn1_w2, m_ffn1_w2, v_ffn1_w2), ("ffn2_w2", ffn2_w2, m_ffn2_w2, v_ffn2_w2),
           ("w_out", w_out, m_w_out, v_w_out)]
    for name, w, m, v in big:
        res[name] = update(name, w, reduced[name][0], m, v, reduced[name][1])
    smalls = [("ada_b", ada_b, g_ada_b, m_ada_b, v_ada_b),
              ("norm_ffn1_g", norm_ffn1_g, g_n1, m_norm_ffn1_g, v_norm_ffn1_g),
              ("norm_mix_g", norm_mix_g, g_nmix, m_norm_mix_g, v_norm_mix_g),
              ("conv_w", conv_w, g_conv, m_conv_w, v_conv_w),
              ("q_norm_g", q_norm_g, g_qg, m_q_norm_g, v_q_norm_g),
              ("kv_norm_g", kv_norm_g, g_kvg, m_kv_norm_g, v_kv_norm_g),
              ("out_norm_g", out_norm_g, g_og, m_out_norm_g, v_out_norm_g),
              ("norm_ffn2_g", norm_ffn2_g, g_n3, m_norm_ffn2_g, v_norm_ffn2_g),
              ("final_norm_g", final_norm_g, g_gf, m_final_norm_g, v_final_norm_g)]
    slens = [w.size for _, w, _, _, _ in smalls]
    soffs = [sum(slens[:i]) for i in range(len(slens))]
    s_len = -(-sum(slens) // 1024) * 1024

    def pack_small(i):
        return _pad_to(jnp.concatenate([s[i].reshape(-1) for s in smalls]), s_len).reshape(8, -1)

    s_out = adamw(pack_small(1), pack_small(2), pack_small(3), pack_small(4), "adamw_small")
    for (name, w, g, _, _), o, n in zip(smalls, soffs, slens):
        res[name] = (g.reshape(w.shape),) + tuple(a.reshape(-1)[o:o + n].reshape(w.shape) for a in s_out)

    order = ["ada_w", "ada_b", "norm_ffn1_g", "ffn1_w1", "ffn1_w3", "ffn1_w2", "norm_mix_g", "w_in", "conv_w",
             "q_norm_g", "w_uq", "kv_norm_g", "w_ukv", "out_norm_g", "w_out", "norm_ffn2_g", "ffn2_w1", "ffn2_w3",
             "ffn2_w2", "final_norm_g"]
    return (loss, dx0.reshape(x.shape), *[res[n][0] for n in order], *[res[n][1] for n in order],
            *[res[n][2] for n in order], *[res[n][3] for n in order])
```

```python
import functools

import jax
import jax.numpy as jnp
from jax import lax
from jax.experimental import pallas as pl
from jax.experimental.pallas import tpu as pltpu

F32 = jnp.float32
BF16 = jnp.bfloat16
MESH_ID = pl.DeviceIdType.MESH
N_DEV = 8

EPS = 1e-6
CHUNK = 64
N_MOD = 9
CONV_WIDTH = 512
CONV_GROUPS = 8
CONV_K = 3
MLA_HEADS = 4
QK_NOPE = 128
QK_ROPE = 64
V_HEAD = 128
Q_LORA = 384
KV_LORA = 256
ROPE_THETA = 10000.0
MLA_WIDTH = MLA_HEADS * V_HEAD
MIX_WIDTH = CONV_WIDTH + MLA_WIDTH
IN_COLS = 3 * CONV_WIDTH + Q_LORA + KV_LORA + QK_ROPE
ZC_COLS = 3 * CONV_WIDTH
ZM_COLS = Q_LORA + KV_LORA + 128
HEAD_PAD = 256
QK_COLS = MLA_HEADS * HEAD_PAD
ATTN_SCALE = (QK_NOPE + QK_ROPE) ** -0.5
LOG2_E = 1.4426950408889634
LN_2 = 0.6931471805599453
QK_FOLD = ATTN_SCALE * LOG2_E
NEG_INF = -1e30

ADAM_LR = 0.001
ADAM_B1 = 0.9
ADAM_B2 = 0.999
ADAM_EPS = 1e-08
ADAM_WD = 0.01
ADAM_STEP = 10

LANES = 128
VMEM_LIMIT = 56 * 1024 * 1024
ROW_TILE = 1024
FFN_FWD_TILE = (1024, 256)
FFN_BWD_TILE = (512, 1408)
FFN_NORM_TILE = (256, 2816)
GRAD_TILE = 1408
GRAD_DEPTH = 2048
SUM_ROWS = 256
ADAM_TILE_ELEMS = 1 << 19
ATTN_TILE = 1024

NN = (((1,), (0,)), ((), ()))
NT = (((1,), (1,)), ((), ()))
TN = (((0,), (0,)), ((), ()))


def _dot(a, b, dims=NN):
    return lax.dot_general(a, b, dims, preferred_element_type=F32)


def _tile(n, cap, mult=LANES):
    best = None
    for t in range(mult, min(n, cap) + 1, mult):
        if n % t == 0:
            best = t
    return n if best is None else best


def _params(sem=None):
    return pltpu.CompilerParams(dimension_semantics=sem, vmem_limit_bytes=VMEM_LIMIT)


def _row(v):
    return pl.BlockSpec(v.shape, lambda *_: (0,) * v.ndim)


def _sigmoid(x):
    return 0.5 * jnp.tanh(0.5 * x) + 0.5


def _rms(x):
    r = lax.rsqrt(jnp.mean(x * x, axis=-1, keepdims=True) + EPS)
    return x * r, r


def _norm_mod_bwd(dh, x, gn, sc):
    xhat, r = _rms(x)
    d_sh = jnp.sum(dh, axis=0, keepdims=True)
    d_sc = jnp.sum(dh * (xhat * gn), axis=0, keepdims=True)
    dxn = dh * (1.0 + sc)
    d_gn = jnp.sum(dxn * xhat, axis=0, keepdims=True)
    dxh = dxn * gn
    dx = r * (dxh - xhat * jnp.mean(dxh * xhat, axis=-1, keepdims=True))
    return dx, d_sh, d_sc, d_gn


def _group_mean(v, gmat):
    return _dot(v.astype(BF16), gmat)


def _add_rows(ref, rows):
    for r, v in enumerate(rows):
        ref[r:r + 1, :] += v


def _window(ref, axis, j):
    return ref.at[(slice(None),) * axis + (j,)]


def _any_specs(n):
    return [pl.BlockSpec(memory_space=pl.ANY)] * n


def all_gather(blocks, axes, name):
    n_arr = len(blocks)

    def body(*refs):
        start, forward, finish = _gather_steps(refs[:n_arr], refs[n_arr:2 * n_arr], axes, *refs[2 * n_arr:])
        start()
        for j in range(3):
            forward(j)
        finish()

    return pl.pallas_call(
        body, name=name, out_shape=_gathered_shapes(blocks, axes),
        in_specs=_any_specs(n_arr), out_specs=_any_specs(n_arr), scratch_shapes=_gather_sems(n_arr),
    )(*blocks)


def all_gather_relayed(blocks, axes, name):
    n_arr = len(blocks)
    arrays = range(n_arr)

    def body(*refs):
        ins, outs = refs[:n_arr], refs[n_arr:2 * n_arr]
        send_sems, recv_sems, local_sems = refs[2 * n_arr:]
        x, y, c = lax.axis_index("x"), lax.axis_index("y"), lax.axis_index("c")
        sibling, x_nbr, y_nbr, diagonal = (x, y, 1 - c), (1 - x, y, c), (x, 1 - y, c), (1 - x, 1 - y, c)
        north = c == 1
        relay_slot = jnp.where(north, 1, 2)
        relay_from = tuple(jnp.where(north, a, b) for a, b in zip(x_nbr, y_nbr))
        relay_to = tuple(jnp.where(north, a, b) for a, b in zip(y_nbr, x_nbr))
        other_from = relay_to

        def slot(a, px, py, pc):
            return _window(outs[a], axes[a], 4 * px + 2 * py + pc)

        def copy(a, k, block, to, src=None):
            return pltpu.make_async_remote_copy(
                src_ref=slot(a, *block) if src is None else src, dst_ref=slot(a, *block),
                send_sem=send_sems.at[k, a], recv_sem=recv_sems.at[k, a], device_id=to, device_id_type=MESH_ID)

        mine = [pltpu.make_async_copy(ins[a], slot(a, x, y, c), local_sems.at[a]) for a in arrays]
        for cp in mine:
            cp.start()
        first = [copy(a, k, (x, y, c), to, src=ins[a])
                 for k, to in enumerate((sibling, x_nbr, y_nbr)) for a in arrays]
        for cp in first:
            cp.start()
        later = []
        for a in arrays:
            copy(a, relay_slot, relay_from, (x, y, c)).wait_recv()
            later += [copy(a, 3, relay_from, relay_to), copy(a, 3 + relay_slot, relay_from, sibling)]
            later[-2].start()
            later[-1].start()
        for a in arrays:
            copy(a, 3 - relay_slot, other_from, (x, y, c)).wait_recv()
            later.append(copy(a, 6 - relay_slot, other_from, sibling))
            later[-1].start()
        for a in arrays:
            copy(a, 3, diagonal, (x, y, c)).wait_recv()
            later.append(copy(a, 6, diagonal, sibling))
            later[-1].start()
        for a in arrays:
            for k, block in ((0, sibling), (4, (1 - x, y, 1 - c)), (5, (x, 1 - y, 1 - c)), (6, (1 - x, 1 - y, 1 - c))):
                copy(a, k, block, (x, y, c)).wait_recv()
        for cp in first + later:
            cp.wait_send()
        for cp in mine:
            cp.wait()

    return pl.pallas_call(
        body, name=name, out_shape=_gathered_shapes(blocks, axes),
        in_specs=_any_specs(n_arr), out_specs=_any_specs(n_arr), scratch_shapes=_gather_sems(n_arr),
    )(*blocks)


def _gathered_shapes(blocks, axes):
    return [jax.ShapeDtypeStruct(b.shape[:ax] + (N_DEV,) + b.shape[ax:], b.dtype) for b, ax in zip(blocks, axes)]


def _gather_sems(n_arr):
    return [pltpu.SemaphoreType.DMA((7, n_arr)), pltpu.SemaphoreType.DMA((7, n_arr)), pltpu.SemaphoreType.DMA((n_arr,))]


def _gather_steps(ins, outs, axes, send_sems, recv_sems, local_sems):
    arrays = range(len(ins))
    x, y, c = lax.axis_index("x"), lax.axis_index("y"), lax.axis_index("c")
    me, sibling = (x, y, c), (x, y, 1 - c)
    chips = [(1 - x, y), (x, 1 - y), (1 - x, 1 - y)]

    def slot(a, px, py, pc):
        return _window(outs[a], axes[a], 4 * px + 2 * py + pc)

    def copy(a, k, block, to, src=None):
        return pltpu.make_async_remote_copy(
            src_ref=slot(a, *block) if src is None else src, dst_ref=slot(a, *block),
            send_sem=send_sems.at[k, a], recv_sem=recv_sems.at[k, a], device_id=to, device_id_type=MESH_ID)

    def mine(a):
        return pltpu.make_async_copy(ins[a], slot(a, *me), local_sems.at[a])

    def first():
        return ([copy(a, 0, me, sibling, src=ins[a]) for a in arrays]
                + [copy(a, 1 + j, me, (*chip, c), src=ins[a]) for j, chip in enumerate(chips) for a in arrays])

    def passed(j):
        return [copy(a, 4 + j, (*chips[j], c), sibling) for a in arrays]

    def start():
        for a in arrays:
            mine(a).start()
        for cp in first():
            cp.start()

    def forward(j):
        for a, cp in zip(arrays, passed(j)):
            copy(a, 1 + j, (*chips[j], c), me).wait_recv()
            cp.start()

    def finish():
        for a in arrays:
            copy(a, 0, sibling, me).wait_recv()
        for j, chip in enumerate(chips):
            for a in arrays:
                copy(a, 4 + j, (*chip, 1 - c), me).wait_recv()
        for cp in first() + passed(0) + passed(1) + passed(2):
            cp.wait_send()
        for a in arrays:
            mine(a).wait()

    return start, forward, finish


def exchange_sibling(grads, name):
    n_arr = len(grads)

    def body(*refs):
        start, finish = _sibling_exchange_steps(refs[:n_arr], refs[n_arr:2 * n_arr], *refs[2 * n_arr:])
        start()
        finish()

    return pl.pallas_call(
        body, name=name, out_shape=_sibling_shapes(grads),
        in_specs=_any_specs(n_arr), out_specs=_any_specs(n_arr), scratch_shapes=_exchange_sems(n_arr),
    )(*grads)


def _sibling_shapes(grads):
    return [jax.ShapeDtypeStruct((4,) + g.shape[1:], g.dtype) for g in grads]


def _exchange_sems(n_arr):
    return [pltpu.SemaphoreType.DMA((n_arr,)), pltpu.SemaphoreType.DMA((n_arr,))]


def _sibling_exchange_steps(ins, outs, send_sems, recv_sems):
    x, y, c = lax.axis_index("x"), lax.axis_index("y"), lax.axis_index("c")

    def copy(a, src, dst):
        return pltpu.make_async_remote_copy(
            src_ref=src, dst_ref=dst, send_sem=send_sems.at[a], recv_sem=recv_sems.at[a],
            device_id=(x, y, 1 - c), device_id_type=MESH_ID)

    def start():
        for a in range(len(ins)):
            for k in range(4):
                copy(a, ins[a].at[2 * k + (1 - c)], outs[a].at[k]).start()

    def finish():
        whole = [copy(a, ins[a].at[pl.ds(0, 4)], outs[a]) for a in range(len(ins))]
        for cp in whole:
            cp.wait_recv()
        for cp in whole:
            cp.wait_send()

    return start, finish


def _chip_exchange_steps(ins, outs, send_sems, recv_sems):
    x, y, c = lax.axis_index("x"), lax.axis_index("y"), lax.axis_index("c")
    chips = [(1 - x, y), (x, 1 - y), (1 - x, 1 - y)]

    def copy(a, src, dst, chip):
        return pltpu.make_async_remote_copy(
            src_ref=src, dst_ref=dst, send_sem=send_sems.at[a], recv_sem=recv_sems.at[a],
            device_id=(*chip, c), device_id_type=MESH_ID)

    def start():
        for a in range(len(ins)):
            for j, chip in enumerate(chips):
                copy(a, ins[a].at[j], outs[a].at[j], chip).start()

    def finish():
        whole = [copy(a, ins[a], outs[a], chips[0]) for a in range(len(ins))]
        for cp in whole:
            cp.wait_recv()
        for cp in whole:
            cp.wait_send()

    return start, finish


def riding_gather(blocks, axes):
    def phases(ins, outs, *sems):
        start, forward, finish = _gather_steps(ins, outs, axes, *sems)
        return [start] + [functools.partial(forward, j) for j in range(3)] + [finish]

    return dict(operands=blocks, out_shape=_gathered_shapes(blocks, axes), sems=_gather_sems(len(blocks)),
                phases=phases, when=("first", "late0", "late1", "late2", "last"))


def riding_exchange(parts):
    def phases(ins, outs, *sems):
        return list(_chip_exchange_steps(ins, outs, *sems))

    return dict(operands=parts, out_shape=[jax.ShapeDtypeStruct(p.shape, p.dtype) for p in parts],
                sems=_exchange_sems(len(parts)), phases=phases, when=("first", "last"))


def riding_sibling(grads):
    def phases(ins, outs, *sems):
        return list(_sibling_exchange_steps(ins, outs, *sems))

    return dict(operands=grads, out_shape=_sibling_shapes(grads), sems=_exchange_sems(len(grads)),
                phases=phases, when=("first", "last"))


def _call_with_rider(body, rider, *, name, grid, in_specs, out_specs, out_shape, scratch_shapes, operands):
    params = _params(("arbitrary",) * len(grid))
    if rider is None:
        return pl.pallas_call(body, name=name, grid=grid, in_specs=in_specs, out_specs=out_specs,
                              out_shape=out_shape, scratch_shapes=scratch_shapes, compiler_params=params)(*operands)
    n_in, n_out, n_scr, k = len(in_specs), len(out_specs), len(scratch_shapes), len(rider["operands"])
    at = {"first": (0,) * len(grid), "last": tuple(g - 1 for g in grid)}
    if "late0" in rider["when"]:
        rows, cols = grid
        assert cols >= 3
        at.update({"late%d" % j: (max(rows - 2, 0), j) for j in range(3)})

    def wrapped(*refs):
        ins, c_in = refs[:n_in], refs[n_in:n_in + k]
        outs, c_out = refs[n_in + k:n_in + k + n_out], refs[n_in + k + n_out:n_in + 2 * k + n_out]
        scratch, sems = refs[n_in + 2 * k + n_out:n_in + 2 * k + n_out + n_scr], refs[n_in + 2 * k + n_out + n_scr:]
        pos = [pl.program_id(axis) for axis in range(len(grid))]

        def here(key):
            return functools.reduce(jnp.logical_and, [p == v for p, v in zip(pos, at[key])])

        phases = rider["phases"](c_in, c_out, *sems)
        for fn, key in zip(phases, rider["when"]):
            if key != "last":
                pl.when(here(key))(fn)
        body(*ins, *outs, *scratch)
        pl.when(here("last"))(phases[-1])

    return pl.pallas_call(
        wrapped, name=name, grid=grid,
        in_specs=list(in_specs) + _any_specs(k), out_specs=list(out_specs) + _any_specs(k),
        out_shape=list(out_shape) + rider["out_shape"], scratch_shapes=list(scratch_shapes) + rider["sems"],
        compiler_params=params)(*operands, *rider["operands"])


def add_sibling(g8, got, src_idx, chip_idx, name):
    _, r, n = g8.shape
    tr = _tile(r, SUM_ROWS, 16)

    def body(si_ref, ci_ref, g0_ref, g1_ref, g2_ref, g3_ref, got_ref, own_ref, send_ref):
        own_ref[...] = g0_ref[0] + got_ref[ci_ref[0]]
        for j, g_ref in enumerate((g1_ref, g2_ref, g3_ref)):
            send_ref[j] = (g_ref[0] + got_ref[ci_ref[j + 1]]).astype(BF16)

    def mine(j):
        return pl.BlockSpec((1, tr, n), lambda i, si, ci: (si[j], i, 0))

    return pl.pallas_call(
        body, name=name,
        out_shape=[jax.ShapeDtypeStruct((r, n), F32), jax.ShapeDtypeStruct((3, r, n), BF16)],
        grid_spec=pltpu.PrefetchScalarGridSpec(
            num_scalar_prefetch=2, grid=(r // tr,),
            in_specs=[mine(0), mine(1), mine(2), mine(3), pl.BlockSpec((4, tr, n), lambda i, si, ci: (0, i, 0))],
            out_specs=[pl.BlockSpec((tr, n), lambda i, si, ci: (i, 0)),
                       pl.BlockSpec((3, tr, n), lambda i, si, ci: (0, i, 0))]),
        compiler_params=_params(("arbitrary",)),
    )(src_idx, chip_idx, g8, g8, g8, g8, got)


def sum_devices(g):
    def body(g_ref, o_ref):
        acc = g_ref[0]
        for j in range(1, N_DEV):
            acc = acc + g_ref[j]
        o_ref[...] = acc

    return pl.pallas_call(body, name="sum_devices", out_shape=jax.ShapeDtypeStruct(g.shape[1:], F32))(g)


def sum_lanes(v):
    def body(v_ref, o_ref):
        o_ref[...] = jnp.broadcast_to(jnp.sum(v_ref[...], axis=-1, keepdims=True), (1, LANES))

    return pl.pallas_call(body, name="sum_lanes", out_shape=jax.ShapeDtypeStruct((1, LANES), F32))(v)


def ada_forward(c_all, ada_w, ada_b_cols):
    nb, n = c_all.shape[0], ada_w.shape[1]

    def body(c_ref, w_ref, b_ref, o_ref):
        cv = c_ref[...]
        s = (cv * jax.nn.sigmoid(cv)).astype(BF16)
        o_ref[...] = _dot(s, w_ref[...].astype(BF16)) + b_ref[...]

    return pl.pallas_call(body, name="ada_fwd", out_shape=jax.ShapeDtypeStruct((nb, n), F32),
                          compiler_params=_params())(c_all, ada_w, ada_b_cols)


def ada_backward(c_all16, dmod16):
    d, n = c_all16.shape[1], dmod16.shape[1]

    def body(c_ref, g_ref, o_ref):
        cv = c_ref[...]
        s = (cv * jax.nn.sigmoid(cv)).astype(BF16)
        o_ref[...] = _dot(s, g_ref[...].astype(BF16), TN)

    return pl.pallas_call(body, name="ada_bwd", out_shape=jax.ShapeDtypeStruct((d, n), F32),
                          compiler_params=_params())(c_all16, dmod16)


def ffn_forward(x, gn, sc, sh, gate, ws, name, rider=None, loss_head=None):
    t, d = x.shape
    f = ws[0].shape[0]
    tm, tf = _tile(t, FFN_FWD_TILE[0], 16), _tile(f, FFN_FWD_TILE[1])
    nf = f // tf
    n_in = 5 if loss_head is None else 7

    def body(*refs):
        x_ref, gn_ref, sc_ref, sh_ref, gate_ref = refs[:5]
        w1_ref, w3_ref, w2_ref, xo_ref, h_ref, a_ref, b_ref, y_ref = refs[n_in:n_in + 8]
        hs, acc = refs[-2:]
        i, j = pl.program_id(0), pl.program_id(1)

        if loss_head is not None:
            @pl.when(jnp.logical_and(i == 0, j == 0))
            def _():
                refs[n_in + 9][...] = jnp.zeros_like(refs[n_in + 9])

        @pl.when(j == 0)
        def _():
            xhat, _ = _rms(x_ref[...])
            h = (xhat * gn_ref[...] * (1.0 + sc_ref[...]) + sh_ref[...]).astype(BF16)
            hs[...] = h
            h_ref[...] = h
            acc[...] = jnp.zeros_like(acc)

        h = hs[...]
        a = _dot(h, w1_ref[...], NT)
        b = _dot(h, w3_ref[...], NT)
        a_ref[...] = a.astype(BF16)
        b_ref[...] = b.astype(BF16)
        u = (a * _sigmoid(a) * b).astype(BF16)
        acc[...] += _dot(u, w2_ref[...])

        @pl.when(j == nf - 1)
        def _():
            y = acc[...]
            y_ref[...] = y.astype(BF16)
            x_out = x_ref[...] + 0.5 * gate_ref[...] * y
            if loss_head is None:
                xo_ref[...] = x_out
            else:
                dx, d_g, loss = _loss_head(x_out, refs[5][...], refs[6][...])
                xo_ref[...] = dx
                refs[n_in + 8][...] = (0.5 * gate_ref[...] * dx).astype(BF16)
                _add_rows(refs[n_in + 9], [d_g, loss])

    row = pl.BlockSpec((tm, d), lambda i, j: (i, 0))
    vec = pl.BlockSpec((1, d), lambda i, j: (0, 0))
    wide = pl.BlockSpec((tm, tf), lambda i, j: (i, j))
    head = loss_head is not None
    return _call_with_rider(
        body, rider, name=name, grid=(t // tm, nf),
        in_specs=[row, vec, vec, vec, vec] + ([row, vec] if head else [])
        + [pl.BlockSpec((tf, d), lambda i, j: (j, 0))] * 3,
        out_specs=[row, row, wide, wide, row] + ([row, pl.BlockSpec((8, d), lambda i, j: (0, 0))] if head else []),
        out_shape=[jax.ShapeDtypeStruct((t, d), F32), jax.ShapeDtypeStruct((t, d), BF16),
                   jax.ShapeDtypeStruct((t, f), BF16), jax.ShapeDtypeStruct((t, f), BF16),
                   jax.ShapeDtypeStruct((t, d), BF16)]
        + ([jax.ShapeDtypeStruct((t, d), BF16), jax.ShapeDtypeStruct((8, d), F32)] if head else []),
        scratch_shapes=[pltpu.VMEM((tm, d), BF16), pltpu.VMEM((tm, d), F32)],
        operands=(x, gn, sc, sh, gate) + (tuple(loss_head) if head else ()) + tuple(ws))


def _loss_head(x, target, g):
    d = x.shape[-1]
    xhat, r = _rms(x)
    err = xhat * g - target
    dyf = err * (1.0 / d)
    dxh = dyf * g
    dx = r * (dxh - xhat * jnp.mean(dxh * xhat, axis=-1, keepdims=True))
    return dx, jnp.sum(dyf * xhat, axis=0, keepdims=True), jnp.sum(err * err, axis=0, keepdims=True) * (0.5 / d)


def ffn_backward_gate(dy, a, b, w2, name, rider=None):
    t, d = dy.shape
    f = w2.shape[0]
    tm, tf = _tile(t, FFN_BWD_TILE[0], 16), _tile(f, FFN_BWD_TILE[1])
    nf = f // tf

    def gate_body(dy_ref, a_ref, b_ref, w2_ref, da_ref, db_ref, gw2_ref):
        dy_v = dy_ref[...]
        du = _dot(dy_v, w2_ref[...], NT)
        av = a_ref[...].astype(F32)
        bv = b_ref[...].astype(F32)
        s = _sigmoid(av)
        sa = av * s
        da_ref[...] = (du * bv * (s + sa * (1.0 - s))).astype(BF16)
        db_ref[...] = (du * sa).astype(BF16)
        part = _dot((sa * bv).astype(BF16), dy_v, TN)

        @pl.when(pl.program_id(1) == 0)
        def _():
            gw2_ref[...] = part

        @pl.when(pl.program_id(1) > 0)
        def _():
            gw2_ref[...] += part

    hidden = jax.ShapeDtypeStruct((t, f), BF16)
    wide_t = pl.BlockSpec((tm, tf), lambda j, i: (i, j))
    return _call_with_rider(
        gate_body, rider, name=name, grid=(nf, t // tm),
        in_specs=[pl.BlockSpec((tm, d), lambda j, i: (i, 0)), wide_t, wide_t,
                  pl.BlockSpec((tf, d), lambda j, i: (j, 0))],
        out_specs=[wide_t, wide_t, pl.BlockSpec((tf, d), lambda j, i: (j, 0))],
        out_shape=[hidden, hidden, jax.ShapeDtypeStruct((f, d), F32)],
        scratch_shapes=[], operands=(dy, a, b, w2))


def ffn_backward_norm(da, db, dxo, x, y, gn, sc, w1t, w3t, name, rider=None):
    t, d = x.shape
    f = w1t.shape[0]
    tm, tf = _tile(t, FFN_NORM_TILE[0], 16), _tile(f, FFN_NORM_TILE[1])
    nf = f // tf
    row = pl.BlockSpec((tm, d), lambda i, j: (i, 0))
    vec = pl.BlockSpec((1, d), lambda i, j: (0, 0))
    wide = pl.BlockSpec((tm, tf), lambda i, j: (i, j))

    def norm_body(da_ref, db_ref, w1_ref, w3_ref, dxo_ref, x_ref, y_ref, gn_ref, sc_ref, dx_ref, sums_ref, acc):
        i, j = pl.program_id(0), pl.program_id(1)

        @pl.when(jnp.logical_and(i == 0, j == 0))
        def _():
            sums_ref[...] = jnp.zeros_like(sums_ref)

        part = _dot(da_ref[...], w1_ref[...]) + _dot(db_ref[...], w3_ref[...])

        @pl.when(j == 0)
        def _():
            acc[...] = part

        @pl.when(jnp.logical_and(j > 0, j < nf - 1))
        def _():
            acc[...] += part

        @pl.when(j == nf - 1)
        def _():
            dh = part if nf == 1 else acc[...] + part
            dxo_v = dxo_ref[...]
            dx, d_sh, d_sc, d_gn = _norm_mod_bwd(dh, x_ref[...], gn_ref[...], sc_ref[...])
            dx_ref[...] = dxo_v + dx
            d_gate = jnp.sum(dxo_v * (0.5 * y_ref[...].astype(F32)), axis=0, keepdims=True)
            _add_rows(sums_ref, [d_sh, d_sc, d_gate, d_gn])

    w_spec = pl.BlockSpec((tf, d), lambda i, j: (j, 0))
    return _call_with_rider(
        norm_body, rider, name=name, grid=(t // tm, nf),
        in_specs=[wide, wide, w_spec, w_spec, row, row, row, vec, vec],
        out_specs=[row, pl.BlockSpec((8, d), lambda i, j: (0, 0))],
        out_shape=[jax.ShapeDtypeStruct((t, d), F32), jax.ShapeDtypeStruct((8, d), F32)],
        scratch_shapes=[pltpu.VMEM((tm, d), F32)],
        operands=(da, db, w1t, w3t, dxo, x, y, gn, sc))


def matmul_tn(a, b, name, rider=None):
    t, m = a.shape
    n = b.shape[1]
    tm, tn, tk = _tile(m, GRAD_TILE), _tile(n, GRAD_TILE), _tile(t, GRAD_DEPTH, 16)
    nk = t // tk

    def body(a_ref, b_ref, o_ref, acc):
        k = pl.program_id(2)

        @pl.when(k == 0)
        def _():
            acc[...] = jnp.zeros_like(acc)

        acc[...] += _dot(a_ref[...], b_ref[...], TN)

        @pl.when(k == nk - 1)
        def _():
            o_ref[...] = acc[...]

    out = _call_with_rider(
        body, rider, name=name, grid=(m // tm, n // tn, nk),
        in_specs=[pl.BlockSpec((tk, tm), lambda i, j, k: (k, i)), pl.BlockSpec((tk, tn), lambda i, j, k: (k, j))],
        out_specs=[pl.BlockSpec((tm, tn), lambda i, j, k: (i, j))],
        out_shape=[jax.ShapeDtypeStruct((m, n), F32)],
        scratch_shapes=[pltpu.VMEM((tm, tn), F32)], operands=(a, b))
    return out[0] if rider is None else out


def mix_in_forward(x, gn, sc, sh, w_in):
    t, d = x.shape
    tm = _tile(t, ROW_TILE, 16)

    def body(x_ref, gn_ref, sc_ref, sh_ref, w_ref, h_ref, zc_ref, zm_ref):
        xhat, _ = _rms(x_ref[...])
        h = (xhat * gn_ref[...] * (1.0 + sc_ref[...]) + sh_ref[...]).astype(BF16)
        h_ref[...] = h
        z = _dot(h, w_ref[...], NT)
        zc_ref[...] = z[:, :ZC_COLS].astype(BF16)
        zm_ref[...] = z[:, ZC_COLS:].astype(BF16)

    row = pl.BlockSpec((tm, d), lambda i: (i, 0))
    vec = pl.BlockSpec((1, d), lambda i: (0, 0))
    return pl.pallas_call(
        body, name="mix_in_fwd", grid=(t // tm,),
        in_specs=[row, vec, vec, vec, _row(w_in)],
        out_specs=[row, pl.BlockSpec((tm, ZC_COLS), lambda i: (i, 0)), pl.BlockSpec((tm, ZM_COLS), lambda i: (i, 0))],
        out_shape=[jax.ShapeDtypeStruct((t, d), BF16), jax.ShapeDtypeStruct((t, ZC_COLS), BF16),
                   jax.ShapeDtypeStruct((t, ZM_COLS), BF16)],
        compiler_params=_params(("arbitrary",)),
    )(x, gn, sc, sh, w_in)


def _rope_tables(pos, inv_freq):
    ang = pos * inv_freq
    lane = lax.broadcasted_iota(jnp.int32, ang.shape, 1)
    cos, sin = jnp.cos(ang), jnp.sin(ang)
    half = QK_ROPE // 2
    return cos, jnp.where(lane < half, -sin, 0.0), jnp.where(jnp.logical_and(lane >= half, lane < QK_ROPE), sin, 0.0)


def _rope(v, tables):
    cos, sin_a, sin_b = tables
    return v * cos + pltpu.roll(v, LANES - QK_ROPE // 2, 1) * sin_a + pltpu.roll(v, QK_ROPE // 2, 1) * sin_b


def _rope_transposed(dv, tables):
    cos, sin_a, sin_b = tables
    return dv * cos + pltpu.roll(dv * sin_a, QK_ROPE // 2, 1) + pltpu.roll(dv * sin_b, LANES - QK_ROPE // 2, 1)


def mla_project(zm, pos, inv_freq, qg, kvg, w_uq, w_ukv):
    t = zm.shape[0]
    tm = _tile(t, ROW_TILE, 16)

    def body(zm_ref, pos_ref, if_ref, qg_ref, kvg_ref, wq_ref, wkv_ref, qn_ref, kvn_ref, q_ref, k_ref, v_ref):
        zv = zm_ref[...].astype(F32)
        qn = (_rms(zv[:, :Q_LORA])[0] * qg_ref[...]).astype(BF16)
        kvn = (_rms(zv[:, Q_LORA:Q_LORA + KV_LORA])[0] * kvg_ref[...]).astype(BF16)
        qn_ref[...] = qn
        kvn_ref[...] = kvn
        qf = _dot(qn, wq_ref[...], NT) * QK_FOLD
        kvf = _dot(kvn, wkv_ref[...], NT)
        tables = _rope_tables(pos_ref[...], if_ref[...])
        kr = _rope(zv[:, Q_LORA + KV_LORA:], tables).astype(BF16)
        for h in range(MLA_HEADS):
            lo = h * HEAD_PAD
            q_ref[:, lo:lo + QK_NOPE] = qf[:, lo:lo + QK_NOPE].astype(BF16)
            q_ref[:, lo + QK_NOPE:lo + HEAD_PAD] = _rope(qf[:, lo + QK_NOPE:lo + HEAD_PAD], tables).astype(BF16)
            k_ref[:, lo:lo + QK_NOPE] = kvf[:, h * QK_NOPE:(h + 1) * QK_NOPE].astype(BF16)
            k_ref[:, lo + QK_NOPE:lo + HEAD_PAD] = kr
        v_ref[...] = kvf[:, MLA_HEADS * QK_NOPE:].astype(BF16)

    def rows(n):
        return pl.BlockSpec((tm, n), lambda i: (i, 0))

    return pl.pallas_call(
        body, name="mla_project", grid=(t // tm,),
        in_specs=[rows(ZM_COLS), rows(1), _row(inv_freq), _row(qg), _row(kvg), _row(w_uq), _row(w_ukv)],
        out_specs=[rows(Q_LORA), rows(KV_LORA), rows(QK_COLS), rows(QK_COLS), rows(MLA_WIDTH)],
        out_shape=[jax.ShapeDtypeStruct((t, Q_LORA), BF16), jax.ShapeDtypeStruct((t, KV_LORA), BF16),
                   jax.ShapeDtypeStruct((t, QK_COLS), BF16), jax.ShapeDtypeStruct((t, QK_COLS), BF16),
                   jax.ShapeDtypeStruct((t, MLA_WIDTH), BF16)],
        compiler_params=_params(("arbitrary",)),
    )(zm, pos, inv_freq, qg, kvg, w_uq, w_ukv)


def _chunk_mask(shape, q_axis):
    qi = lax.broadcasted_iota(jnp.int32, shape, q_axis) // CHUNK
    ki = lax.broadcasted_iota(jnp.int32, shape, 1 - q_axis) // CHUNK
    return ki <= qi


def attention_forward(q, k, v, rider=None):
    t = q.shape[0]
    tq = _tile(t, ATTN_TILE, CHUNK)

    def body(q_ref, k_ref, v_ref, o_ref, lse_ref):
        i = pl.program_id(1)
        qv = q_ref[...]

        def step(kb, carry, masked, tiles=1):
            m, l, acc = carry
            keys = pl.ds(pl.multiple_of(kb * tq, tq), tiles * tq)
            s = _dot(qv, k_ref[keys, :], NT)
            if masked:
                s = jnp.where(_chunk_mask(s.shape, 0), s, NEG_INF)
            m_new = jnp.maximum(m, jnp.max(s, axis=-1, keepdims=True))
            alpha = jnp.exp2(m - m_new)
            p = jnp.exp2(s - m_new)
            l = alpha * l + jnp.sum(p, axis=-1, keepdims=True)
            acc = alpha * acc + _dot(p.astype(BF16), v_ref[keys, :])
            return m_new, l, acc

        init = (jnp.full((tq, 1), NEG_INF, F32), jnp.zeros((tq, 1), F32), jnp.zeros((tq, V_HEAD), F32))
        carry = lax.fori_loop(0, i // 2, lambda pb, cr: step(2 * pb, cr, False, 2), init)
        carry = lax.fori_loop(0, i % 2, lambda _, cr: step(i - 1, cr, False), carry)
        m, l, acc = step(i, carry, True)
        o_ref[...] = (acc / l).astype(BF16)
        lse_ref[0] = m + jnp.log2(l)

    return _call_with_rider(
        body, rider, name="attn_fwd", grid=(MLA_HEADS, t // tq),
        in_specs=[pl.BlockSpec((tq, HEAD_PAD), lambda h, i: (i, h)),
                  pl.BlockSpec((t, HEAD_PAD), lambda h, i: (0, h)),
                  pl.BlockSpec((t, V_HEAD), lambda h, i: (0, h))],
        out_specs=[pl.BlockSpec((tq, V_HEAD), lambda h, i: (i, h)),
                   pl.BlockSpec((1, tq, 1), lambda h, i: (h, i, 0))],
        out_shape=[jax.ShapeDtypeStruct((t, MLA_WIDTH), BF16), jax.ShapeDtypeStruct((MLA_HEADS, t, 1), F32)],
        scratch_shapes=[], operands=(q, k, v))


def attention_backward(q, k, v, do, lse, delta, rider=None):
    t = q.shape[0]
    tq = _tile(t, ATTN_TILE, CHUNK)
    nq = t // tq

    def body(q_ref, k_ref, v_ref, do_ref, lse_ref, delta_ref, dq_ref, dk_ref, dv_ref, dq_acc):
        kb = pl.program_id(1)

        @pl.when(kb == 0)
        def _():
            dq_acc[...] = jnp.zeros_like(dq_acc)

        kv, vv = k_ref[...], v_ref[...]

        def step(qb, carry, masked):
            dk, dv = carry
            rows = pl.ds(pl.multiple_of(qb * tq, tq), tq)
            qv, dov = q_ref[rows, :], do_ref[rows, :]
            s = _dot(kv, qv, NT)
            if masked:
                s = jnp.where(_chunk_mask(s.shape, 1), s, NEG_INF)
            p = jnp.exp2(s - lse_ref[0, qb])
            dv = dv + _dot(p.astype(BF16), dov)
            dp = _dot(vv, dov, NT)
            ds = (p * (dp - delta_ref[0, qb]) * LN_2).astype(BF16)
            dk = dk + _dot(ds, qv)
            dq_acc[rows, :] += _dot(ds, kv, TN)
            return dk, dv

        carry = step(kb, (jnp.zeros((tq, HEAD_PAD), F32), jnp.zeros((tq, V_HEAD), F32)), True)
        odd = (nq - 1 - kb) % 2
        carry = lax.fori_loop(0, odd, lambda _, cr: step(kb + 1, cr, False), carry)
        first = kb + 1 + odd
        dk, dv = lax.fori_loop(0, (nq - first) // 2,
                               lambda pb, cr: step(first + 2 * pb + 1, step(first + 2 * pb, cr, False), False), carry)
        dk_ref[...] = dk.astype(BF16)
        dv_ref[...] = dv.astype(BF16)

        @pl.when(kb == nq - 1)
        def _():
            dq_ref[...] = dq_acc[...].astype(BF16)

    stat = pl.BlockSpec((1, nq, 1, tq), lambda h, j: (h, 0, 0, 0))
    return _call_with_rider(
        body, rider, name="attn_bwd", grid=(MLA_HEADS, nq),
        in_specs=[pl.BlockSpec((t, HEAD_PAD), lambda h, j: (0, h)),
                  pl.BlockSpec((tq, HEAD_PAD), lambda h, j: (j, h)),
                  pl.BlockSpec((tq, V_HEAD), lambda h, j: (j, h)),
                  pl.BlockSpec((t, V_HEAD), lambda h, j: (0, h)), stat, stat],
        out_specs=[pl.BlockSpec((t, HEAD_PAD), lambda h, j: (0, h)),
                   pl.BlockSpec((tq, HEAD_PAD), lambda h, j: (j, h)),
                   pl.BlockSpec((tq, V_HEAD), lambda h, j: (j, h))],
        out_shape=[jax.ShapeDtypeStruct((t, QK_COLS), BF16), jax.ShapeDtypeStruct((t, QK_COLS), BF16),
                   jax.ShapeDtypeStruct((t, MLA_WIDTH), BF16)],
        scratch_shapes=[pltpu.VMEM((t, HEAD_PAD), F32)], operands=(q, k, v, do, lse, delta))


HALO = 16


def _halo_spec(tm, n, step, last):
    return pl.BlockSpec((HALO, n), lambda i: (jnp.clip(i * (tm // HALO) + step, 0, last), 0))


def _shift_rows(v, prev, n):
    out = pltpu.roll(v, n, 0)
    row = lax.broadcasted_iota(jnp.int32, v.shape, 0)
    for r in range(n):
        out = jnp.where(row == r, prev[HALO - n + r:HALO - n + r + 1, :], out)
    return out


def _advance_rows(v, nxt, n):
    rows = v.shape[0]
    out = pltpu.roll(v, rows - n, 0)
    row = lax.broadcasted_iota(jnp.int32, v.shape, 0)
    for r in range(n):
        out = jnp.where(row == rows - n + r, nxt[r:r + 1, :], out)
    return out


def _conv_taps(zc, zc_prev, first):
    w = CONV_WIDTH
    u = zc[:, w:2 * w] * zc[:, 2 * w:]
    up = jnp.where(first, 0.0, zc_prev[:, w:2 * w] * zc_prev[:, 2 * w:])
    return u, _shift_rows(u, up, 1), _shift_rows(u, up, 2)


def mix_out_forward(zc, o, conv_w, og, gmat_a, gmat_b, w_out, x, gate):
    t, d = x.shape
    tm = _tile(t, ROW_TILE, 16)
    w = CONV_WIDTH

    def body(zc_ref, zp_ref, o_ref, cw_ref, og_ref, ga_ref, gb_ref, w_ref, x_ref, gate_ref,
             xo_ref, yn_ref, y_ref, ya_ref):
        zc_v = zc_ref[...].astype(F32)
        u, u1, u2 = _conv_taps(zc_v, zp_ref[...].astype(F32), pl.program_id(0) == 0)
        cw = cw_ref[...]
        ya = zc_v[:, :w] * (cw[0:1] * u2 + cw[1:2] * u1 + cw[2:3] * u)
        ya_ref[...] = ya.astype(BF16)
        ov = o_ref[...].astype(F32)
        ogv = og_ref[...]
        yn_ref[:, :w] = (ya * lax.rsqrt(_group_mean(ya * ya, ga_ref[...]) + EPS) * ogv[:, :w]).astype(BF16)
        yn_ref[:, w:] = (ov * lax.rsqrt(_group_mean(ov * ov, gb_ref[...]) + EPS) * ogv[:, w:]).astype(BF16)
        y = _dot(yn_ref[...], w_ref[...])
        y_ref[...] = y.astype(BF16)
        xo_ref[...] = x_ref[...] + gate_ref[...] * y

    def rows(n):
        return pl.BlockSpec((tm, n), lambda i: (i, 0))

    return pl.pallas_call(
        body, name="mix_out_fwd", grid=(t // tm,),
        in_specs=[rows(ZC_COLS), _halo_spec(tm, ZC_COLS, -1, t // HALO - 1), rows(MLA_WIDTH), _row(conv_w), _row(og),
                  _row(gmat_a), _row(gmat_b), _row(w_out), rows(d), _row(gate)],
        out_specs=[rows(d), rows(MIX_WIDTH), rows(d), rows(w)],
        out_shape=[jax.ShapeDtypeStruct((t, d), F32), jax.ShapeDtypeStruct((t, MIX_WIDTH), BF16),
                   jax.ShapeDtypeStruct((t, d), BF16), jax.ShapeDtypeStruct((t, w), BF16)],
        compiler_params=_params(("arbitrary",)),
    )(zc, zc, o, conv_w, og, gmat_a, gmat_b, w_out, x, gate)


def _group_norm_bwd(dyn, y, og, gmat):
    rs = lax.rsqrt(_group_mean(y * y, gmat) + EPS)
    yhat = y * rs
    d_og = jnp.sum(dyn * yhat, axis=0, keepdims=True)
    dyh = dyn * og
    return rs * (dyh - yhat * _group_mean(dyh * yhat, gmat)), d_og


def mix_out_backward(dxo, y, gate, ya, o, yn, og, gmat_a, gmat_b, w_out, rider=None):
    t, d = dxo.shape
    tm = _tile(t, ROW_TILE, 16)
    w = CONV_WIDTH

    def body(dxo_ref, y_ref, gate_ref, ya_ref, o_ref, yn_ref, og_ref, ga_ref, gb_ref, w_ref,
             dya_ref, do_ref, delta_ref, sd_ref, so_ref, gw_ref):
        @pl.when(pl.program_id(0) == 0)
        def _():
            sd_ref[...] = jnp.zeros_like(sd_ref)
            so_ref[...] = jnp.zeros_like(so_ref)
            gw_ref[...] = jnp.zeros_like(gw_ref)

        dxo_v = dxo_ref[...]
        dy = (gate_ref[...] * dxo_v).astype(BF16)
        gw_ref[...] += _dot(yn_ref[...], dy, TN)
        sd_ref[0:1, :] += jnp.sum(dxo_v * y_ref[...].astype(F32), axis=0, keepdims=True)
        dyn = _dot(dy, w_ref[...], NT)
        ogv = og_ref[...]
        ov = o_ref[...].astype(F32)
        dya, d_og_a = _group_norm_bwd(dyn[:, :w], ya_ref[...].astype(F32), ogv[:, :w], ga_ref[...])
        dov, d_og_b = _group_norm_bwd(dyn[:, w:], ov, ogv[:, w:], gb_ref[...])
        dya_ref[...] = dya.astype(BF16)
        do_ref[...] = dov.astype(BF16)
        so_ref[0:1, :w] += d_og_a
        so_ref[0:1, w:] += d_og_b
        prod = dov * ov
        for h in range(MLA_HEADS):
            delta_ref[h] = jnp.sum(prod[:, h * V_HEAD:(h + 1) * V_HEAD], axis=-1, keepdims=True)

    def rows(n):
        return pl.BlockSpec((tm, n), lambda i: (i, 0))

    return _call_with_rider(
        body, rider, name="mix_out_bwd", grid=(t // tm,),
        in_specs=[rows(d), rows(d), _row(gate), rows(w), rows(MLA_WIDTH), rows(MIX_WIDTH), _row(og), _row(gmat_a),
                  _row(gmat_b), _row(w_out)],
        out_specs=[rows(w), rows(MLA_WIDTH), pl.BlockSpec((MLA_HEADS, tm, 1), lambda i: (0, i, 0)),
                   pl.BlockSpec((8, d), lambda i: (0, 0)), pl.BlockSpec((8, MIX_WIDTH), lambda i: (0, 0)),
                   pl.BlockSpec((MIX_WIDTH, d), lambda i: (0, 0))],
        out_shape=[jax.ShapeDtypeStruct((t, w), BF16),
                   jax.ShapeDtypeStruct((t, MLA_WIDTH), BF16), jax.ShapeDtypeStruct((MLA_HEADS, t, 1), F32),
                   jax.ShapeDtypeStruct((8, d), F32), jax.ShapeDtypeStruct((8, MIX_WIDTH), F32),
                   jax.ShapeDtypeStruct((MIX_WIDTH, d), F32)],
        scratch_shapes=[], operands=(dxo, y, gate, ya, o, yn, og, gmat_a, gmat_b, w_out))


def conv_backward(zc, dya, conv_w, h):
    t, d = h.shape
    tm = _tile(t, ROW_TILE, 16)
    nt = t // tm
    w = CONV_WIDTH

    def body(zc_ref, zp_ref, zn_ref, dya_ref, dn_ref, cw_ref, h_ref, dzc_ref, sums_ref, gw_ref):
        i = pl.program_id(0)

        @pl.when(i == 0)
        def _():
            sums_ref[...] = jnp.zeros_like(sums_ref)
            gw_ref[...] = jnp.zeros_like(gw_ref)

        zc_v = zc_ref[...].astype(F32)
        u, u1, u2 = _conv_taps(zc_v, zp_ref[...].astype(F32), i == 0)
        cw = cw_ref[...]
        dya_v = dya_ref[...].astype(F32)
        dyc = dya_v * zc_v[:, :w]
        dyc_next = jnp.where(i == nt - 1, 0.0, dn_ref[...].astype(F32) * zn_ref[:, :w].astype(F32))
        du = cw[2:3] * dyc + cw[1:2] * _advance_rows(dyc, dyc_next, 1) + cw[0:1] * _advance_rows(dyc, dyc_next, 2)
        dzc_ref[:, :w] = (dya_v * (cw[0:1] * u2 + cw[1:2] * u1 + cw[2:3] * u)).astype(BF16)
        dzc_ref[:, w:2 * w] = (du * zc_v[:, 2 * w:]).astype(BF16)
        dzc_ref[:, 2 * w:] = (du * zc_v[:, w:2 * w]).astype(BF16)
        _add_rows(sums_ref, [jnp.sum(dyc * tap, axis=0, keepdims=True) for tap in (u2, u1, u)])
        gw_ref[...] += _dot(dzc_ref[...], h_ref[...], TN)

    def rows(n):
        return pl.BlockSpec((tm, n), lambda i: (i, 0))

    def halo(n, step):
        return _halo_spec(tm, n, step, t // HALO - 1)

    return pl.pallas_call(
        body, name="conv_bwd", grid=(nt,),
        in_specs=[rows(ZC_COLS), halo(ZC_COLS, -1), halo(ZC_COLS, tm // HALO), rows(w), halo(w, tm // HALO),
                  _row(conv_w), rows(d)],
        out_specs=[rows(ZC_COLS), pl.BlockSpec((8, w), lambda i: (0, 0)), pl.BlockSpec((ZC_COLS, d), lambda i: (0, 0))],
        out_shape=[jax.ShapeDtypeStruct((t, ZC_COLS), BF16), jax.ShapeDtypeStruct((8, w), F32),
                   jax.ShapeDtypeStruct((ZC_COLS, d), F32)],
        compiler_params=_params(("arbitrary",)),
    )(zc, zc, zc, dya, dya, conv_w, h)


def _rms_bwd(dy, x, g):
    xhat, r = _rms(x)
    d_g = jnp.sum(dy * xhat, axis=0, keepdims=True)
    dxh = dy * g
    return r * (dxh - xhat * jnp.mean(dxh * xhat, axis=-1, keepdims=True)), d_g


def mla_project_backward(dq, dk, dv, zm, qn, kvn, h, pos, inv_freq, qg, kvg, w_uq, w_ukv):
    t, d = h.shape
    tm = _tile(t, ROW_TILE, 16)

    def body(dq_ref, dk_ref, dv_ref, zm_ref, qn_ref, kvn_ref, h_ref, pos_ref, if_ref, qg_ref, kvg_ref, wq_ref,
             wkv_ref, dzm_ref, sums_ref, guq_ref, gukv_ref, gin_ref, dql_ref, dkvl_ref):
        @pl.when(pl.program_id(0) == 0)
        def _():
            sums_ref[...] = jnp.zeros_like(sums_ref)
            guq_ref[...] = jnp.zeros_like(guq_ref)
            gukv_ref[...] = jnp.zeros_like(gukv_ref)
            gin_ref[...] = jnp.zeros_like(gin_ref)

        tables = _rope_tables(pos_ref[...], if_ref[...])
        dkr = jnp.zeros((tm, LANES), F32)
        for h in range(MLA_HEADS):
            lo = h * HEAD_PAD
            dql_ref[:, lo:lo + QK_NOPE] = (dq_ref[:, lo:lo + QK_NOPE].astype(F32) * QK_FOLD).astype(BF16)
            dql_ref[:, lo + QK_NOPE:lo + HEAD_PAD] = _rope_transposed(
                dq_ref[:, lo + QK_NOPE:lo + HEAD_PAD].astype(F32) * QK_FOLD, tables).astype(BF16)
            dkvl_ref[:, h * QK_NOPE:(h + 1) * QK_NOPE] = dk_ref[:, lo:lo + QK_NOPE]
            dkr = dkr + dk_ref[:, lo + QK_NOPE:lo + HEAD_PAD].astype(F32)
        dkvl_ref[:, MLA_HEADS * QK_NOPE:] = dv_ref[...]
        zv = zm_ref[...].astype(F32)
        dqn = _dot(dql_ref[...], wq_ref[...])
        dkvn = _dot(dkvl_ref[...], wkv_ref[...])
        dcq, d_qg = _rms_bwd(dqn, zv[:, :Q_LORA], qg_ref[...])
        dckv, d_kvg = _rms_bwd(dkvn, zv[:, Q_LORA:Q_LORA + KV_LORA], kvg_ref[...])
        dzm_ref[:, :Q_LORA] = dcq.astype(BF16)
        dzm_ref[:, Q_LORA:Q_LORA + KV_LORA] = dckv.astype(BF16)
        dzm_ref[:, Q_LORA + KV_LORA:] = _rope_transposed(dkr, tables).astype(BF16)
        sums_ref[0:1, :Q_LORA] += d_qg
        sums_ref[0:1, Q_LORA:Q_LORA + KV_LORA] += d_kvg
        guq_ref[...] += _dot(dql_ref[...], qn_ref[...], TN)
        gukv_ref[...] += _dot(dkvl_ref[...], kvn_ref[...], TN)
        gin_ref[...] += _dot(dzm_ref[...], h_ref[...], TN)

    def rows(n):
        return pl.BlockSpec((tm, n), lambda i: (i, 0))

    def whole(r, n):
        return pl.BlockSpec((r, n), lambda i: (0, 0))

    return pl.pallas_call(
        body, name="mla_project_bwd", grid=(t // tm,),
        in_specs=[rows(QK_COLS), rows(QK_COLS), rows(MLA_WIDTH), rows(ZM_COLS), rows(Q_LORA), rows(KV_LORA), rows(d),
                  rows(1), _row(inv_freq), _row(qg), _row(kvg), _row(w_uq), _row(w_ukv)],
        out_specs=[rows(ZM_COLS), whole(8, ZM_COLS), whole(QK_COLS, Q_LORA), whole(QK_COLS, KV_LORA),
                   whole(ZM_COLS, d)],
        out_shape=[jax.ShapeDtypeStruct((t, ZM_COLS), BF16), jax.ShapeDtypeStruct((8, ZM_COLS), F32),
                   jax.ShapeDtypeStruct((QK_COLS, Q_LORA), F32), jax.ShapeDtypeStruct((QK_COLS, KV_LORA), F32),
                   jax.ShapeDtypeStruct((ZM_COLS, d), F32)],
        scratch_shapes=[pltpu.VMEM((tm, QK_COLS), BF16), pltpu.VMEM((tm, QK_COLS), BF16)],
        compiler_params=_params(("arbitrary",)),
    )(dq, dk, dv, zm, qn, kvn, h, pos, inv_freq, qg, kvg, w_uq, w_ukv)


def mix_in_backward(dzc, dzm, w_in, x, dxo, gn, sc, gate, rider=None):
    t, d = x.shape
    tm = _tile(t, ROW_TILE, 16)

    def body(dzc_ref, dzm_ref, w_ref, x_ref, dxo_ref, gn_ref, sc_ref, gate_ref, dx_ref, dy_ref, sums_ref):
        @pl.when(pl.program_id(0) == 0)
        def _():
            sums_ref[...] = jnp.zeros_like(sums_ref)

        dh = _dot(dzc_ref[...], w_ref[:ZC_COLS, :]) + _dot(dzm_ref[...], w_ref[ZC_COLS:, :])
        dx, d_sh, d_sc, d_gn = _norm_mod_bwd(dh, x_ref[...], gn_ref[...], sc_ref[...])
        dx = dxo_ref[...] + dx
        dx_ref[...] = dx
        dy_ref[...] = (0.5 * gate_ref[...] * dx).astype(BF16)
        _add_rows(sums_ref, [d_sh, d_sc, d_gn])

    def rows(n):
        return pl.BlockSpec((tm, n), lambda i: (i, 0))

    return _call_with_rider(
        body, rider, name="mix_in_bwd", grid=(t // tm,),
        in_specs=[rows(ZC_COLS), rows(ZM_COLS), _row(w_in), rows(d), rows(d), _row(gn), _row(sc), _row(gate)],
        out_specs=[rows(d), rows(d), pl.BlockSpec((8, d), lambda i: (0, 0))],
        out_shape=[jax.ShapeDtypeStruct((t, d), F32), jax.ShapeDtypeStruct((t, d), BF16),
                   jax.ShapeDtypeStruct((8, d), F32)],
        scratch_shapes=[], operands=(dzc, dzm, w_in, x, dxo, gn, sc, gate))


def _adamw_step(w, g, m, v):
    m_new = ADAM_B1 * m + (1.0 - ADAM_B1) * g
    v_new = ADAM_B2 * v + (1.0 - ADAM_B2) * (g * g)
    m_hat = m_new / (1.0 - ADAM_B1 ** ADAM_STEP)
    v_hat = v_new / (1.0 - ADAM_B2 ** ADAM_STEP)
    return -ADAM_LR * (m_hat / (jnp.sqrt(v_hat) + ADAM_EPS) + ADAM_WD * w), m_new, v_new


def adamw(w, g, m, v, name):
    r, n = w.shape
    tr = _tile(r, max(8, ADAM_TILE_ELEMS // n), 8)

    def body(w_ref, g_ref, m_ref, v_ref, d_ref, mo_ref, vo_ref):
        d_ref[...], mo_ref[...], vo_ref[...] = _adamw_step(w_ref[...], g_ref[...], m_ref[...], v_ref[...])

    blk = pl.BlockSpec((tr, n), lambda i: (i, 0))
    shape = jax.ShapeDtypeStruct((r, n), F32)
    return pl.pallas_call(
        body, name=name, grid=(r // tr,), in_specs=[blk] * 4, out_specs=[blk] * 3, out_shape=[shape] * 3,
        compiler_params=_params(("arbitrary",)),
    )(w, g, m, v)


def adamw_received(w, own, got, m, v, name):
    r, n = w.shape
    tr = _tile(r, SUM_ROWS, 16)

    def body(w_ref, own_ref, got_ref, m_ref, v_ref, g_ref, d_ref, mo_ref, vo_ref):
        g = own_ref[...]
        for j in range(3):
            g = g + got_ref[j].astype(F32)
        g_ref[...] = g
        d_ref[...], mo_ref[...], vo_ref[...] = _adamw_step(w_ref[...], g, m_ref[...], v_ref[...])

    blk = pl.BlockSpec((tr, n), lambda i: (i, 0))
    shape = jax.ShapeDtypeStruct((r, n), F32)
    return pl.pallas_call(
        body, name=name, grid=(r // tr,),
        in_specs=[blk, blk, pl.BlockSpec((3, tr, n), lambda i: (0, i, 0)), blk, blk],
        out_specs=[blk] * 4, out_shape=[shape] * 4, compiler_params=_params(("arbitrary",)),
    )(w, own, got, m, v)


def _pad_to(v, n):
    return jnp.pad(v, (0, n - v.shape[0]))


def _pad_heads(w, axis_len):
    n = w.shape[1]
    return jnp.pad(w.reshape(MLA_HEADS, axis_len, n), ((0, 0), (0, HEAD_PAD - axis_len), (0, 0))).reshape(-1, n)


def _swap_head_parts(w, inner, outer):
    n = w.shape[1]
    return w.reshape(outer, inner, QK_NOPE, n).transpose(1, 0, 2, 3).reshape(-1, n)


def kernel(x, c, positions, ada_w, ada_b, norm_ffn1_g, ffn1_w1, ffn1_w3, ffn1_w2, norm_mix_g, w_in, conv_w, q_norm_g, w_uq, kv_norm_g, w_ukv, out_norm_g, w_out, norm_ffn2_g, ffn2_w1, ffn2_w3, ffn2_w2, final_norm_g, loss_target, m_ada_w, m_ada_b, m_norm_ffn1_g, m_ffn1_w1, m_ffn1_w3, m_ffn1_w2, m_norm_mix_g, m_w_in, m_conv_w, m_q_norm_g, m_w_uq, m_kv_norm_g, m_w_ukv, m_out_norm_g, m_w_out, m_norm_ffn2_g, m_ffn2_w1, m_ffn2_w3, m_ffn2_w2, m_final_norm_g, v_ada_w, v_ada_b, v_norm_ffn1_g, v_ffn1_w1, v_ffn1_w3, v_ffn1_w2, v_norm_mix_g, v_w_in, v_conv_w, v_q_norm_g, v_w_uq, v_kv_norm_g, v_w_ukv, v_out_norm_g, v_w_out, v_norm_ffn2_g, v_ffn2_w1, v_ffn2_w3, v_ffn2_w2, v_final_norm_g):
    t, d = x.shape[1], x.shape[2]
    f = ffn1_w2.shape[1] * N_DEV
    me = 4 * lax.axis_index("x") + 2 * lax.axis_index("y") + lax.axis_index("c")
    my_c = lax.axis_index("c")
    my_chip = 2 * lax.axis_index("x") + lax.axis_index("y")
    xs = x[0]
    n_ada = ada_w.shape[2]
    cw_n = conv_w.shape[2]

    c_rows = jnp.broadcast_to(c, (8, d))
    conv_rows = jnp.pad(conv_w[0], ((0, 8 - CONV_K), (0, LANES - cw_n)))
    ffn1_blocks = [ffn1_w1[0].T.astype(BF16), ffn1_w3[0].T.astype(BF16), ffn1_w2[0].astype(BF16)]
    ffn2_blocks = [ffn2_w1[0].T.astype(BF16), ffn2_w3[0].T.astype(BF16), ffn2_w2[0].astype(BF16)]
    c_all, conv_all, *ffn1_all = all_gather_relayed([c_rows, conv_rows] + ffn1_blocks, [0] * 5, "gather_first")
    c_all = c_all[:, 0, :]
    conv_full8 = conv_all[:, :, :cw_n].transpose(1, 0, 2).reshape(8, CONV_WIDTH)
    ffn1_ws = [w.reshape(f, d) for w in ffn1_all]
    gather_mix = riding_gather(
        [w_in[0].T.astype(BF16), w_uq[0].T.astype(BF16), w_ukv[0].T.astype(BF16), w_out[0].astype(BF16)], [0, 0, 0, 0])

    ada_b_cols = lax.dynamic_slice_in_dim(ada_b, me * n_ada, n_ada, axis=1)
    mod_cols = ada_forward(c_all, ada_w[0], ada_b_cols)
    mod_all, = all_gather([mod_cols], [0], "gather_mod")
    mod = lax.dynamic_index_in_dim(mod_all, me, axis=1, keepdims=False).reshape(N_MOD, 1, d)
    sh1, sc1, g1, sh2, sc2, g2, sh3, sc3, g3 = [mod[i] for i in range(N_MOD)]

    gf = final_norm_g.reshape(1, d)
    x1, h1, a1, b1, y1, *gathered = ffn_forward(xs, norm_ffn1_g, sc1, sh1, g1, ffn1_ws, "ffn1_fwd", gather_mix)
    w_in_p = jnp.pad(gathered[0].reshape(IN_COLS, d), ((0, ZC_COLS + ZM_COLS - IN_COLS), (0, 0)))
    w_uq_p = _pad_heads(gathered[1].reshape(-1, Q_LORA), QK_NOPE + QK_ROPE)
    w_ukv_p = _swap_head_parts(gathered[2].reshape(-1, KV_LORA), 2, MLA_HEADS)
    w_out_f = gathered[3].reshape(MIX_WIDTH, d)
    h2, zc, zm = mix_in_forward(x1, norm_mix_g, sc2, sh2, w_in_p)
    pos = positions[0].astype(F32).reshape(t, 1)
    inv_freq = ROPE_THETA ** (-jnp.arange(0, QK_ROPE, 2, dtype=F32) / QK_ROPE)
    inv_freq = jnp.concatenate([inv_freq, inv_freq, jnp.zeros((LANES - QK_ROPE,), F32)]).reshape(1, LANES)
    qn, kvn, q, k, v = mla_project(zm, pos, inv_freq, q_norm_g, kv_norm_g, w_uq_p, w_ukv_p)
    o, lse, *ffn2_all = attention_forward(q, k, v, riding_gather(ffn2_blocks, [0] * 3))
    ffn2_ws = [w.reshape(f, d) for w in ffn2_all]
    lane = jnp.arange(CONV_WIDTH)
    gmat_a = (lane[:, None] // (CONV_WIDTH // CONV_GROUPS) == lane[None, :] // (CONV_WIDTH // CONV_GROUPS))
    gmat_a = (gmat_a / (CONV_WIDTH // CONV_GROUPS)).astype(BF16)
    gmat_b = ((lane[:, None] // V_HEAD == lane[None, :] // V_HEAD) / V_HEAD).astype(BF16)
    x2, yn, y2, ya = mix_out_forward(zc, o, conv_full8, out_norm_g, gmat_a, gmat_b, w_out_f, x1, g2)
    dx3, h3, a3, b3, y3, dy3, sums_f = ffn_forward(x2, norm_ffn2_g, sc3, sh3, g3, ffn2_ws, "ffn2_fwd",
                                                   loss_head=(loss_target[0], gf))

    chip_idx = jnp.bitwise_xor(my_chip, jnp.array([0, 2, 1, 3], jnp.int32)).astype(jnp.int32)
    src_idx = (2 * chip_idx + my_c).astype(jnp.int32)

    def row_blocks(named):
        return [g.reshape(N_DEV, g.shape[0] // N_DEV, g.shape[1]) for _, g in named]

    def chip_sums(named, g8, got):
        return [add_sibling(g, r, src_idx, chip_idx, "rs_add_" + n) for g, r, (n, _) in zip(g8, got, named)]

    da3, db3, g_w2b = ffn_backward_gate(dy3, a3, b3, ffn2_ws[2], "ffn2_bwd_gate")
    dx2, sums_3 = ffn_backward_norm(da3, db3, dx3, x2, y3, norm_ffn2_g, sc3, ffn2_ws[0], ffn2_ws[1], "ffn2_bwd_norm")
    ffn2_named = [("ffn2_w1", matmul_tn(da3, h3, "ffn2_gw1")), ("ffn2_w3", matmul_tn(db3, h3, "ffn2_gw3")),
                  ("ffn2_w2", g_w2b)]
    ffn2_g8 = row_blocks(ffn2_named)
    dya, do, delta, sums_2d, sums_2o, g_w_out, *ffn2_sib = mix_out_backward(
        dx2, y2, g2, ya, o, yn, out_norm_g, gmat_a, gmat_b, w_out_f, riding_sibling(ffn2_g8))
    ffn2_sums = chip_sums(ffn2_named, ffn2_g8, ffn2_sib)
    nq = t // _tile(t, ATTN_TILE, CHUNK)
    stat_shape = (MLA_HEADS, nq, 1, t // nq)
    dq, dk, dv, *ffn2_got = attention_backward(q, k, v, do, lse.reshape(stat_shape), delta.reshape(stat_shape),
                                               riding_exchange([s[1] for s in ffn2_sums]))
    dzc, sums_c, g_w_in_conv = conv_backward(zc, dya, conv_full8, h2)
    dzm, sums_m, g_w_uq_p, g_w_ukv_p, g_w_in_mla = mla_project_backward(
        dq, dk, dv, zm, qn, kvn, h2, pos, inv_freq, q_norm_g, kv_norm_g, w_uq_p, w_ukv_p)
    g_w_in = jnp.concatenate([g_w_in_conv, g_w_in_mla])[:IN_COLS]
    g_w_uq = g_w_uq_p.reshape(MLA_HEADS, HEAD_PAD, Q_LORA)[:, :QK_NOPE + QK_ROPE].reshape(-1, Q_LORA)
    g_w_ukv = _swap_head_parts(g_w_ukv_p, MLA_HEADS, 2)
    mix_named = [("w_in", g_w_in), ("w_uq", g_w_uq), ("w_ukv", g_w_ukv), ("w_out", g_w_out)]
    mix_g8 = row_blocks(mix_named)
    dx1, dy1, sums_1m, *mix_sib = mix_in_backward(dzc, dzm, w_in_p, x1, dx2, norm_mix_g, sc2, g1, riding_sibling(mix_g8))
    mix_sums = chip_sums(mix_named, mix_g8, mix_sib)
    da1, db1, g_w2a, *mix_got = ffn_backward_gate(dy1, a1, b1, ffn1_ws[2], "ffn1_bwd_gate",
                                                  riding_exchange([s[1] for s in mix_sums]))
    ffn1_pair = [("ffn1_w2", g_w2a), ("ffn1_w1", matmul_tn(da1, h1, "ffn1_gw1"))]
    pair_g8 = row_blocks(ffn1_pair)
    g_w3a, *pair_sib = matmul_tn(db1, h1, "ffn1_gw3", riding_sibling(pair_g8))
    ffn1_last = [("ffn1_w3", g_w3a)]
    last_g8 = row_blocks(ffn1_last)
    ffn1_named = ffn1_pair + ffn1_last
    ffn1_sums = chip_sums(ffn1_pair, pair_g8, pair_sib) + chip_sums(
        ffn1_last, last_g8, exchange_sibling(last_g8, "rs_sibling_ffn1_w3"))
    dx0, sums_1, *ffn1_got = ffn_backward_norm(da1, db1, dx1, xs, y1, norm_ffn1_g, sc1, ffn1_ws[0], ffn1_ws[1], "ffn1_bwd_norm",
                                               riding_exchange([s[1] for s in ffn1_sums]))
    reduced = {}
    for named, group_sums, group_got in ((ffn2_named, ffn2_sums, ffn2_got), (mix_named, mix_sums, mix_got),
                                         (ffn1_named, ffn1_sums, ffn1_got)):
        for (n, _), (own, _), got in zip(named, group_sums, group_got):
            reduced[n] = (own, got)

    dmod = jnp.concatenate([sums_1[0], sums_1[1], sums_1[2], sums_1m[0], sums_1m[1], sums_2d[0],
                            sums_3[0], sums_3[1], sums_3[2]])
    pieces = [dmod, sums_1[3], sums_1m[2], sums_m[0, :Q_LORA], sums_m[0, Q_LORA:Q_LORA + KV_LORA], sums_2o[0],
              sums_3[3], sums_f[0], sums_f[1], sums_c[:CONV_K].reshape(-1)]
    plens = [p.shape[0] for p in pieces]
    poffs = [sum(plens[:i]) for i in range(len(plens))]
    vec_len = -(-sum(plens) // 1024) * 1024
    vec = _pad_to(jnp.concatenate(pieces), vec_len).reshape(-1, LANES)
    vec_all, = all_gather([vec], [0], "gather_sums")
    tot = sum_devices(vec_all).reshape(-1)
    g_ada_b, g_n1, g_nmix, g_qg, g_kvg, g_og, g_n3, g_gf, loss_lanes, g_conv_full = [
        tot[o:o + n] for o, n in zip(poffs, plens)]
    loss = sum_lanes(loss_lanes.reshape(1, d))[0, 0]
    g_conv = lax.dynamic_slice_in_dim(g_conv_full.reshape(CONV_K, CONV_WIDTH), me * cw_n, cw_n, axis=1)
    dmod_all = vec_all.reshape(N_DEV, vec_len)[:, :N_MOD * d]
    dmod_cols = lax.dynamic_slice_in_dim(dmod_all, me * n_ada, n_ada, axis=1)
    g_ada_w = ada_backward(jnp.pad(c_all, ((0, 8), (0, 0))), jnp.pad(dmod_cols, ((0, 8), (0, 0))))

    def update(name, w, g, m, v, received=None):
        k, n = w.shape[-2:]
        if g.shape == (k, n):
            flat, back = (lambda a: a.reshape(k, n)), (lambda a: a.reshape(w.shape))
        else:
            flat, back = (lambda a: a.reshape(k, n).T), (lambda a: a.T.reshape(w.shape))
        if received is None:
            out = (g,) + tuple(adamw(flat(w), g, flat(m), flat(v), "adamw_" + name))
        else:
            out = adamw_received(flat(w), g, received, flat(m), flat(v), "adamw_" + name)
        return tuple(back(a) for a in out)

    res = {}
    res["ada_w"] = update("ada_w", ada_w, g_ada_w, m_ada_w, v_ada_w)
    big = [("ffn1_w1", ffn1_w1, m_ffn1_w1, v_ffn1_w1), ("ffn1_w3", ffn1_w3, m_ffn1_w3, v_ffn1_w3),
           ("ffn2_w1", ffn2_w1, m_ffn2_w1, v_ffn2_w1), ("ffn2_w3", ffn2_w3, m_ffn2_w3, v_ffn2_w3),
           ("w_in", w_in, m_w_in, v_w_in), ("w_uq", w_uq, m_w_uq, v_w_uq), ("w_ukv", w_ukv, m_w_ukv, v_w_ukv),
           ("ffn1_w2", ffn1_w2, m_ffn1_w2, v_ffn1_w2), ("ffn2_w2", ffn2_w2, m_ffn2_w2, v_ffn2_w2),
           ("w_out", w_out, m_w_out, v_w_out)]
    for name, w, m, v in big:
        res[name] = update(name, w, reduced[name][0], m, v, reduced[name][1])
    smalls = [("ada_b", ada_b, g_ada_b, m_ada_b, v_ada_b),
              ("norm_ffn1_g", norm_ffn1_g, g_n1, m_norm_ffn1_g, v_norm_ffn1_g),
              ("norm_mix_g", norm_mix_g, g_nmix, m_norm_mix_g, v_norm_mix_g),
              ("conv_w", conv_w, g_conv, m_conv_w, v_conv_w),
              ("q_norm_g", q_norm_g, g_qg, m_q_norm_g, v_q_norm_g),
              ("kv_norm_g", kv_norm_g, g_kvg, m_kv_norm_g, v_kv_norm_g),
              ("out_norm_g", out_norm_g, g_og, m_out_norm_g, v_out_norm_g),
              ("norm_ffn2_g", norm_ffn2_g, g_n3, m_norm_ffn2_g, v_norm_ffn2_g),
              ("final_norm_g", final_norm_g, g_gf, m_final_norm_g, v_final_norm_g)]
    slens = [w.size for _, w, _, _, _ in smalls]
    soffs = [sum(slens[:i]) for i in range(len(slens))]
    s_len = -(-sum(slens) // 1024) * 1024

    def pack_small(i):
        return _pad_to(jnp.concatenate([s[i].reshape(-1) for s in smalls]), s_len).reshape(8, -1)

    s_out = adamw(pack_small(1), pack_small(2), pack_small(3), pack_small(4), "adamw_small")
    for (name, w, g, _, _), o, n in zip(smalls, soffs, slens):
        res[name] = (g.reshape(w.shape),) + tuple(a.reshape(-1)[o:o + n].reshape(w.shape) for a in s_out)

    order = ["ada_w", "ada_b", "norm_ffn1_g", "ffn1_w1", "ffn1_w3", "ffn1_w2", "norm_mix_g", "w_in", "conv_w",
             "q_norm_g", "w_uq", "kv_norm_g", "w_ukv", "out_norm_g", "w_out", "norm_ffn2_g", "ffn2_w1", "ffn2_w3",
             "ffn2_w2", "final_norm_g"]
    return (loss, dx0.reshape(x.shape), *[res[n][0] for n in order], *[res[n][1] for n in order],
            *[res[n][2] for n in order], *[res[n][3] for n in order])
```

```python
import functools

import jax
import jax.numpy as jnp
from jax import lax
from jax.experimental import pallas as pl
from jax.experimental.pallas import tpu as pltpu

F32 = jnp.float32
BF16 = jnp.bfloat16
MESH_ID = pl.DeviceIdType.MESH
N_DEV = 8

EPS = 1e-6
CHUNK = 64
N_MOD = 9
CONV_WIDTH = 512
CONV_GROUPS = 8
CONV_K = 3
MLA_HEADS = 4
QK_NOPE = 128
QK_ROPE = 64
V_HEAD = 128
Q_LORA = 384
KV_LORA = 256
ROPE_THETA = 10000.0
MLA_WIDTH = MLA_HEADS * V_HEAD
MIX_WIDTH = CONV_WIDTH + MLA_WIDTH
IN_COLS = 3 * CONV_WIDTH + Q_LORA + KV_LORA + QK_ROPE
ZC_COLS = 3 * CONV_WIDTH
ZM_COLS = Q_LORA + KV_LORA + 128
HEAD_PAD = 256
QK_COLS = MLA_HEADS * HEAD_PAD
ATTN_SCALE = (QK_NOPE + QK_ROPE) ** -0.5
LOG2_E = 1.4426950408889634
LN_2 = 0.6931471805599453
QK_FOLD = ATTN_SCALE * LOG2_E
NEG_INF = -1e30

ADAM_LR = 0.001
ADAM_B1 = 0.9
ADAM_B2 = 0.999
ADAM_EPS = 1e-08
ADAM_WD = 0.01
ADAM_STEP = 10

LANES = 128
VMEM_LIMIT = 56 * 1024 * 1024
ROW_TILE = 1024
FFN_FWD_TILE = (256, 2816)
FFN_BWD_TILE = (512, 1408)
FFN_NORM_TILE = (256, 2816)
GRAD_TILE = 1408
GRAD_DEPTH = 2048
SUM_ROWS = 256
ADAM_TILE_ELEMS = 1 << 19
ATTN_TILE = 1024

NN = (((1,), (0,)), ((), ()))
NT = (((1,), (1,)), ((), ()))
TN = (((0,), (0,)), ((), ()))


def _dot(a, b, dims=NN):
    return lax.dot_general(a, b, dims, preferred_element_type=F32)


def _tile(n, cap, mult=LANES):
    best = None
    for t in range(mult, min(n, cap) + 1, mult):
        if n % t == 0:
            best = t
    return n if best is None else best


def _params(sem=None):
    return pltpu.CompilerParams(dimension_semantics=sem, vmem_limit_bytes=VMEM_LIMIT)


def _row(v):
    return pl.BlockSpec(v.shape, lambda *_: (0,) * v.ndim)


def _sigmoid(x):
    return 0.5 * jnp.tanh(0.5 * x) + 0.5


def _rms(x):
    r = lax.rsqrt(jnp.mean(x * x, axis=-1, keepdims=True) + EPS)
    return x * r, r


def _norm_mod_bwd(dh, x, gn, sc):
    xhat, r = _rms(x)
    d_sh = jnp.sum(dh, axis=0, keepdims=True)
    d_sc = jnp.sum(dh * (xhat * gn), axis=0, keepdims=True)
    dxn = dh * (1.0 + sc)
    d_gn = jnp.sum(dxn * xhat, axis=0, keepdims=True)
    dxh = dxn * gn
    dx = r * (dxh - xhat * jnp.mean(dxh * xhat, axis=-1, keepdims=True))
    return dx, d_sh, d_sc, d_gn


def _group_mean(v, gmat):
    return _dot(v.astype(BF16), gmat)


def _add_rows(ref, rows):
    for r, v in enumerate(rows):
        ref[r:r + 1, :] += v


def _window(ref, axis, j):
    return ref.at[(slice(None),) * axis + (j,)]


def _any_specs(n):
    return [pl.BlockSpec(memory_space=pl.ANY)] * n


def all_gather(blocks, axes, name):
    n_arr = len(blocks)

    def body(*refs):
        start, forward, finish = _gather_steps(refs[:n_arr], refs[n_arr:2 * n_arr], axes, *refs[2 * n_arr:])
        start()
        for j in range(3):
            forward(j)
        finish()

    return pl.pallas_call(
        body, name=name, out_shape=_gathered_shapes(blocks, axes),
        in_specs=_any_specs(n_arr), out_specs=_any_specs(n_arr), scratch_shapes=_gather_sems(n_arr),
    )(*blocks)


def all_gather_relayed(blocks, axes, name):
    n_arr = len(blocks)
    arrays = range(n_arr)

    def body(*refs):
        ins, outs = refs[:n_arr], refs[n_arr:2 * n_arr]
        send_sems, recv_sems, local_sems = refs[2 * n_arr:]
        x, y, c = lax.axis_index("x"), lax.axis_index("y"), lax.axis_index("c")
        sibling, x_nbr, y_nbr, diagonal = (x, y, 1 - c), (1 - x, y, c), (x, 1 - y, c), (1 - x, 1 - y, c)
        north = c == 1
        relay_slot = jnp.where(north, 1, 2)
        relay_from = tuple(jnp.where(north, a, b) for a, b in zip(x_nbr, y_nbr))
        relay_to = tuple(jnp.where(north, a, b) for a, b in zip(y_nbr, x_nbr))
        other_from = relay_to

        def slot(a, px, py, pc):
            return _window(outs[a], axes[a], 4 * px + 2 * py + pc)

        def copy(a, k, block, to, src=None):
            return pltpu.make_async_remote_copy(
                src_ref=slot(a, *block) if src is None else src, dst_ref=slot(a, *block),
                send_sem=send_sems.at[k, a], recv_sem=recv_sems.at[k, a], device_id=to, device_id_type=MESH_ID)

        mine = [pltpu.make_async_copy(ins[a], slot(a, x, y, c), local_sems.at[a]) for a in arrays]
        for cp in mine:
            cp.start()
        first = [copy(a, k, (x, y, c), to, src=ins[a])
                 for k, to in enumerate((sibling, x_nbr, y_nbr)) for a in arrays]
        for cp in first:
            cp.start()
        later = []
        for a in arrays:
            copy(a, relay_slot, relay_from, (x, y, c)).wait_recv()
            later += [copy(a, 3, relay_from, relay_to), copy(a, 3 + relay_slot, relay_from, sibling)]
            later[-2].start()
            later[-1].start()
        for a in arrays:
            copy(a, 3 - relay_slot, other_from, (x, y, c)).wait_recv()
            later.append(copy(a, 6 - relay_slot, other_from, sibling))
            later[-1].start()
        for a in arrays:
            copy(a, 3, diagonal, (x, y, c)).wait_recv()
            later.append(copy(a, 6, diagonal, sibling))
            later[-1].start()
        for a in arrays:
            for k, block in ((0, sibling), (4, (1 - x, y, 1 - c)), (5, (x, 1 - y, 1 - c)), (6, (1 - x, 1 - y, 1 - c))):
                copy(a, k, block, (x, y, c)).wait_recv()
        for cp in first + later:
            cp.wait_send()
        for cp in mine:
            cp.wait()

    return pl.pallas_call(
        body, name=name, out_shape=_gathered_shapes(blocks, axes),
        in_specs=_any_specs(n_arr), out_specs=_any_specs(n_arr), scratch_shapes=_gather_sems(n_arr),
    )(*blocks)


def _gathered_shapes(blocks, axes):
    return [jax.ShapeDtypeStruct(b.shape[:ax] + (N_DEV,) + b.shape[ax:], b.dtype) for b, ax in zip(blocks, axes)]


def _gather_sems(n_arr):
    return [pltpu.SemaphoreType.DMA((7, n_arr)), pltpu.SemaphoreType.DMA((7, n_arr)), pltpu.SemaphoreType.DMA((n_arr,))]


def _gather_steps(ins, outs, axes, send_sems, recv_sems, local_sems):
    arrays = range(len(ins))
    x, y, c = lax.axis_index("x"), lax.axis_index("y"), lax.axis_index("c")
    me, sibling = (x, y, c), (x, y, 1 - c)
    chips = [(1 - x, y), (x, 1 - y), (1 - x, 1 - y)]

    def slot(a, px, py, pc):
        return _window(outs[a], axes[a], 4 * px + 2 * py + pc)

    def copy(a, k, block, to, src=None):
        return pltpu.make_async_remote_copy(
            src_ref=slot(a, *block) if src is None else src, dst_ref=slot(a, *block),
            send_sem=send_sems.at[k, a], recv_sem=recv_sems.at[k, a], device_id=to, device_id_type=MESH_ID)

    def mine(a):
        return pltpu.make_async_copy(ins[a], slot(a, *me), local_sems.at[a])

    def first():
        return ([copy(a, 0, me, sibling, src=ins[a]) for a in arrays]
                + [copy(a, 1 + j, me, (*chip, c), src=ins[a]) for j, chip in enumerate(chips) for a in arrays])

    def passed(j):
        return [copy(a, 4 + j, (*chips[j], c), sibling) for a in arrays]

    def start():
        for a in arrays:
            mine(a).start()
        for cp in first():
            cp.start()

    def forward(j):
        for a, cp in zip(arrays, passed(j)):
            copy(a, 1 + j, (*chips[j], c), me).wait_recv()
            cp.start()

    def finish():
        for a in arrays:
            copy(a, 0, sibling, me).wait_recv()
        for j, chip in enumerate(chips):
            for a in arrays:
                copy(a, 4 + j, (*chip, 1 - c), me).wait_recv()
        for cp in first() + passed(0) + passed(1) + passed(2):
            cp.wait_send()
        for a in arrays:
            mine(a).wait()

    return start, forward, finish


def exchange_sibling(grads, name):
    n_arr = len(grads)

    def body(*refs):
        start, finish = _sibling_exchange_steps(refs[:n_arr], refs[n_arr:2 * n_arr], *refs[2 * n_arr:])
        start()
        finish()

    return pl.pallas_call(
        body, name=name, out_shape=_sibling_shapes(grads),
        in_specs=_any_specs(n_arr), out_specs=_any_specs(n_arr), scratch_shapes=_exchange_sems(n_arr),
    )(*grads)


def _sibling_shapes(grads):
    return [jax.ShapeDtypeStruct((4,) + g.shape[1:], g.dtype) for g in grads]


def _exchange_sems(n_arr):
    return [pltpu.SemaphoreType.DMA((n_arr,)), pltpu.SemaphoreType.DMA((n_arr,))]


def _sibling_exchange_steps(ins, outs, send_sems, recv_sems):
    x, y, c = lax.axis_index("x"), lax.axis_index("y"), lax.axis_index("c")

    def copy(a, src, dst):
        return pltpu.make_async_remote_copy(
            src_ref=src, dst_ref=dst, send_sem=send_sems.at[a], recv_sem=recv_sems.at[a],
            device_id=(x, y, 1 - c), device_id_type=MESH_ID)

    def start():
        for a in range(len(ins)):
            for k in range(4):
                copy(a, ins[a].at[2 * k + (1 - c)], outs[a].at[k]).start()

    def finish():
        whole = [copy(a, ins[a].at[pl.ds(0, 4)], outs[a]) for a in range(len(ins))]
        for cp in whole:
            cp.wait_recv()
        for cp in whole:
            cp.wait_send()

    return start, finish


def _chip_exchange_steps(ins, outs, send_sems, recv_sems):
    x, y, c = lax.axis_index("x"), lax.axis_index("y"), lax.axis_index("c")
    chips = [(1 - x, y), (x, 1 - y), (1 - x, 1 - y)]

    def copy(a, src, dst, chip):
        return pltpu.make_async_remote_copy(
            src_ref=src, dst_ref=dst, send_sem=send_sems.at[a], recv_sem=recv_sems.at[a],
            device_id=(*chip, c), device_id_type=MESH_ID)

    def start():
        for a in range(len(ins)):
            for j, chip in enumerate(chips):
                copy(a, ins[a].at[j], outs[a].at[j], chip).start()

    def finish():
        whole = [copy(a, ins[a], outs[a], chips[0]) for a in range(len(ins))]
        for cp in whole:
            cp.wait_recv()
        for cp in whole:
            cp.wait_send()

    return start, finish


def riding_gather(blocks, axes):
    def phases(ins, outs, *sems):
        start, forward, finish = _gather_steps(ins, outs, axes, *sems)
        return [start] + [functools.partial(forward, j) for j in range(3)] + [finish]

    return dict(operands=blocks, out_shape=_gathered_shapes(blocks, axes), sems=_gather_sems(len(blocks)),
                phases=phases, when=("first", "late0", "late1", "late2", "last"))


def riding_exchange(parts):
    def phases(ins, outs, *sems):
        return list(_chip_exchange_steps(ins, outs, *sems))

    return dict(operands=parts, out_shape=[jax.ShapeDtypeStruct(p.shape, p.dtype) for p in parts],
                sems=_exchange_sems(len(parts)), phases=phases, when=("first", "last"))


def riding_sibling(grads):
    def phases(ins, outs, *sems):
        return list(_sibling_exchange_steps(ins, outs, *sems))

    return dict(operands=grads, out_shape=_sibling_shapes(grads), sems=_exchange_sems(len(grads)),
                phases=phases, when=("first", "last"))


def _call_with_rider(body, rider, *, name, grid, in_specs, out_specs, out_shape, scratch_shapes, operands):
    params = _params(("arbitrary",) * len(grid))
    if rider is None:
        return pl.pallas_call(body, name=name, grid=grid, in_specs=in_specs, out_specs=out_specs,
                              out_shape=out_shape, scratch_shapes=scratch_shapes, compiler_params=params)(*operands)
    n_in, n_out, n_scr, k = len(in_specs), len(out_specs), len(scratch_shapes), len(rider["operands"])
    at = {"first": (0,) * len(grid), "last": tuple(g - 1 for g in grid)}
    if "late0" in rider["when"]:
        rows, cols = grid
        late = max(rows * cols - cols - 4, 0)
        assert late + 2 < rows * cols - 1
        at.update({"late%d" % j: ((late + j) // cols, (late + j) % cols) for j in range(3)})

    def wrapped(*refs):
        ins, c_in = refs[:n_in], refs[n_in:n_in + k]
        outs, c_out = refs[n_in + k:n_in + k + n_out], refs[n_in + k + n_out:n_in + 2 * k + n_out]
        scratch, sems = refs[n_in + 2 * k + n_out:n_in + 2 * k + n_out + n_scr], refs[n_in + 2 * k + n_out + n_scr:]
        pos = [pl.program_id(axis) for axis in range(len(grid))]

        def here(key):
            return functools.reduce(jnp.logical_and, [p == v for p, v in zip(pos, at[key])])

        phases = rider["phases"](c_in, c_out, *sems)
        for fn, key in zip(phases, rider["when"]):
            if key != "last":
                pl.when(here(key))(fn)
        body(*ins, *outs, *scratch)
        pl.when(here("last"))(phases[-1])

    return pl.pallas_call(
        wrapped, name=name, grid=grid,
        in_specs=list(in_specs) + _any_specs(k), out_specs=list(out_specs) + _any_specs(k),
        out_shape=list(out_shape) + rider["out_shape"], scratch_shapes=list(scratch_shapes) + rider["sems"],
        compiler_params=params)(*operands, *rider["operands"])


def add_sibling(g8, got, src_idx, chip_idx, name):
    _, r, n = g8.shape
    tr = _tile(r, SUM_ROWS, 16)

    def body(si_ref, ci_ref, g0_ref, g1_ref, g2_ref, g3_ref, got_ref, own_ref, send_ref):
        own_ref[...] = g0_ref[0] + got_ref[ci_ref[0]]
        for j, g_ref in enumerate((g1_ref, g2_ref, g3_ref)):
            send_ref[j] = (g_ref[0] + got_ref[ci_ref[j + 1]]).astype(BF16)

    def mine(j):
        return pl.BlockSpec((1, tr, n), lambda i, si, ci: (si[j], i, 0))

    return pl.pallas_call(
        body, name=name,
        out_shape=[jax.ShapeDtypeStruct((r, n), F32), jax.ShapeDtypeStruct((3, r, n), BF16)],
        grid_spec=pltpu.PrefetchScalarGridSpec(
            num_scalar_prefetch=2, grid=(r // tr,),
            in_specs=[mine(0), mine(1), mine(2), mine(3), pl.BlockSpec((4, tr, n), lambda i, si, ci: (0, i, 0))],
            out_specs=[pl.BlockSpec((tr, n), lambda i, si, ci: (i, 0)),
                       pl.BlockSpec((3, tr, n), lambda i, si, ci: (0, i, 0))]),
        compiler_params=_params(("arbitrary",)),
    )(src_idx, chip_idx, g8, g8, g8, g8, got)


def sum_devices(g):
    def body(g_ref, o_ref):
        acc = g_ref[0]
        for j in range(1, N_DEV):
            acc = acc + g_ref[j]
        o_ref[...] = acc

    return pl.pallas_call(body, name="sum_devices", out_shape=jax.ShapeDtypeStruct(g.shape[1:], F32))(g)


def sum_lanes(v):
    def body(v_ref, o_ref):
        o_ref[...] = jnp.broadcast_to(jnp.sum(v_ref[...], axis=-1, keepdims=True), (1, LANES))

    return pl.pallas_call(body, name="sum_lanes", out_shape=jax.ShapeDtypeStruct((1, LANES), F32))(v)


def ada_forward(c_all, ada_w, ada_b_cols):
    nb, n = c_all.shape[0], ada_w.shape[1]

    def body(c_ref, w_ref, b_ref, o_ref):
        cv = c_ref[...]
        s = (cv * jax.nn.sigmoid(cv)).astype(BF16)
        o_ref[...] = _dot(s, w_ref[...].astype(BF16)) + b_ref[...]

    return pl.pallas_call(body, name="ada_fwd", out_shape=jax.ShapeDtypeStruct((nb, n), F32),
                          compiler_params=_params())(c_all, ada_w, ada_b_cols)


def ada_backward(c_all16, dmod16):
    d, n = c_all16.shape[1], dmod16.shape[1]

    def body(c_ref, g_ref, o_ref):
        cv = c_ref[...]
        s = (cv * jax.nn.sigmoid(cv)).astype(BF16)
        o_ref[...] = _dot(s, g_ref[...].astype(BF16), TN)

    return pl.pallas_call(body, name="ada_bwd", out_shape=jax.ShapeDtypeStruct((d, n), F32),
                          compiler_params=_params())(c_all16, dmod16)


def ffn_forward(x, gn, sc, sh, gate, ws, name, rider=None, loss_head=None):
    t, d = x.shape
    f = ws[0].shape[0]
    tm, tf = _tile(t, FFN_FWD_TILE[0], 16), _tile(f, FFN_FWD_TILE[1])
    nf = f // tf
    n_in = 5 if loss_head is None else 7

    def body(*refs):
        x_ref, gn_ref, sc_ref, sh_ref, gate_ref = refs[:5]
        w1_ref, w3_ref, w2_ref, xo_ref, h_ref, a_ref, b_ref, y_ref = refs[n_in:n_in + 8]
        hs, acc = refs[-2:]
        i, j = pl.program_id(0), pl.program_id(1)

        if loss_head is not None:
            @pl.when(jnp.logical_and(i == 0, j == 0))
            def _():
                refs[n_in + 9][...] = jnp.zeros_like(refs[n_in + 9])

        @pl.when(j == 0)
        def _():
            xhat, _ = _rms(x_ref[...])
            h = (xhat * gn_ref[...] * (1.0 + sc_ref[...]) + sh_ref[...]).astype(BF16)
            hs[...] = h
            h_ref[...] = h
            acc[...] = jnp.zeros_like(acc)

        h = hs[...]
        a = _dot(h, w1_ref[...], NT)
        b = _dot(h, w3_ref[...], NT)
        a_ref[...] = a.astype(BF16)
        b_ref[...] = b.astype(BF16)
        u = (a * _sigmoid(a) * b).astype(BF16)
        acc[...] += _dot(u, w2_ref[...])

        @pl.when(j == nf - 1)
        def _():
            y = acc[...]
            y_ref[...] = y.astype(BF16)
            x_out = x_ref[...] + 0.5 * gate_ref[...] * y
            if loss_head is None:
                xo_ref[...] = x_out
            else:
                dx, d_g, loss = _loss_head(x_out, refs[5][...], refs[6][...])
                xo_ref[...] = dx
                refs[n_in + 8][...] = (0.5 * gate_ref[...] * dx).astype(BF16)
                _add_rows(refs[n_in + 9], [d_g, loss])

    row = pl.BlockSpec((tm, d), lambda i, j: (i, 0))
    vec = pl.BlockSpec((1, d), lambda i, j: (0, 0))
    wide = pl.BlockSpec((tm, tf), lambda i, j: (i, j))
    w_spec = pl.BlockSpec((tf, d), lambda i, j: (j, 0), **({"pipeline_mode": pl.Buffered(1)} if nf == 1 else {}))
    head = loss_head is not None
    return _call_with_rider(
        body, rider, name=name, grid=(t // tm, nf),
        in_specs=[row, vec, vec, vec, vec] + ([row, vec] if head else []) + [w_spec] * 3,
        out_specs=[row, row, wide, wide, row] + ([row, pl.BlockSpec((8, d), lambda i, j: (0, 0))] if head else []),
        out_shape=[jax.ShapeDtypeStruct((t, d), F32), jax.ShapeDtypeStruct((t, d), BF16),
                   jax.ShapeDtypeStruct((t, f), BF16), jax.ShapeDtypeStruct((t, f), BF16),
                   jax.ShapeDtypeStruct((t, d), BF16)]
        + ([jax.ShapeDtypeStruct((t, d), BF16), jax.ShapeDtypeStruct((8, d), F32)] if head else []),
        scratch_shapes=[pltpu.VMEM((tm, d), BF16), pltpu.VMEM((tm, d), F32)],
        operands=(x, gn, sc, sh, gate) + (tuple(loss_head) if head else ()) + tuple(ws))


def _loss_head(x, target, g):
    d = x.shape[-1]
    xhat, r = _rms(x)
    err = xhat * g - target
    dyf = err * (1.0 / d)
    dxh = dyf * g
    dx = r * (dxh - xhat * jnp.mean(dxh * xhat, axis=-1, keepdims=True))
    return dx, jnp.sum(dyf * xhat, axis=0, keepdims=True), jnp.sum(err * err, axis=0, keepdims=True) * (0.5 / d)


def ffn_backward_gate(dy, a, b, w2, name, rider=None):
    t, d = dy.shape
    f = w2.shape[0]
    tm, tf = _tile(t, FFN_BWD_TILE[0], 16), _tile(f, FFN_BWD_TILE[1])
    nf = f // tf

    def gate_body(dy_ref, a_ref, b_ref, w2_ref, da_ref, db_ref, gw2_ref):
        dy_v = dy_ref[...]
        du = _dot(dy_v, w2_ref[...], NT)
        av = a_ref[...].astype(F32)
        bv = b_ref[...].astype(F32)
        s = _sigmoid(av)
        sa = av * s
        da_ref[...] = (du * bv * (s + sa * (1.0 - s))).astype(BF16)
        db_ref[...] = (du * sa).astype(BF16)
        part = _dot((sa * bv).astype(BF16), dy_v, TN)

        @pl.when(pl.program_id(1) == 0)
        def _():
            gw2_ref[...] = part

        @pl.when(pl.program_id(1) > 0)
        def _():
            gw2_ref[...] += part

    hidden = jax.ShapeDtypeStruct((t, f), BF16)
    wide_t = pl.BlockSpec((tm, tf), lambda j, i: (i, j))
    return _call_with_rider(
        gate_body, rider, name=name, grid=(nf, t // tm),
        in_specs=[pl.BlockSpec((tm, d), lambda j, i: (i, 0)), wide_t, wide_t,
                  pl.BlockSpec((tf, d), lambda j, i: (j, 0))],
        out_specs=[wide_t, wide_t, pl.BlockSpec((tf, d), lambda j, i: (j, 0))],
        out_shape=[hidden, hidden, jax.ShapeDtypeStruct((f, d), F32)],
        scratch_shapes=[], operands=(dy, a, b, w2))


def ffn_backward_norm(da, db, dxo, x, y, gn, sc, w1t, w3t, name, rider=None):
    t, d = x.shape
    f = w1t.shape[0]
    tm, tf = _tile(t, FFN_NORM_TILE[0], 16), _tile(f, FFN_NORM_TILE[1])
    nf = f // tf
    row = pl.BlockSpec((tm, d), lambda i, j: (i, 0))
    vec = pl.BlockSpec((1, d), lambda i, j: (0, 0))
    wide = pl.BlockSpec((tm, tf), lambda i, j: (i, j))

    def norm_body(da_ref, db_ref, w1_ref, w3_ref, dxo_ref, x_ref, y_ref, gn_ref, sc_ref, dx_ref, sums_ref, acc):
        i, j = pl.program_id(0), pl.program_id(1)

        @pl.when(jnp.logical_and(i == 0, j == 0))
        def _():
            sums_ref[...] = jnp.zeros_like(sums_ref)

        part = _dot(da_ref[...], w1_ref[...]) + _dot(db_ref[...], w3_ref[...])

        @pl.when(j == 0)
        def _():
            acc[...] = part

        @pl.when(jnp.logical_and(j > 0, j < nf - 1))
        def _():
            acc[...] += part

        @pl.when(j == nf - 1)
        def _():
            dh = part if nf == 1 else acc[...] + part
            dxo_v = dxo_ref[...]
            dx, d_sh, d_sc, d_gn = _norm_mod_bwd(dh, x_ref[...], gn_ref[...], sc_ref[...])
            dx_ref[...] = dxo_v + dx
            d_gate = jnp.sum(dxo_v * (0.5 * y_ref[...].astype(F32)), axis=0, keepdims=True)
            _add_rows(sums_ref, [d_sh, d_sc, d_gate, d_gn])

    w_spec = pl.BlockSpec((tf, d), lambda i, j: (j, 0))
    return _call_with_rider(
        norm_body, rider, name=name, grid=(t // tm, nf),
        in_specs=[wide, wide, w_spec, w_spec, row, row, row, vec, vec],
        out_specs=[row, pl.BlockSpec((8, d), lambda i, j: (0, 0))],
        out_shape=[jax.ShapeDtypeStruct((t, d), F32), jax.ShapeDtypeStruct((8, d), F32)],
        scratch_shapes=[pltpu.VMEM((tm, d), F32)],
        operands=(da, db, w1t, w3t, dxo, x, y, gn, sc))


def matmul_tn(a, b, name, rider=None):
    t, m = a.shape
    n = b.shape[1]
    tm, tn, tk = _tile(m, GRAD_TILE), _tile(n, GRAD_TILE), _tile(t, GRAD_DEPTH, 16)
    nk = t // tk

    def body(a_ref, b_ref, o_ref, acc):
        k = pl.program_id(2)

        @pl.when(k == 0)
        def _():
            acc[...] = jnp.zeros_like(acc)

        acc[...] += _dot(a_ref[...], b_ref[...], TN)

        @pl.when(k == nk - 1)
        def _():
            o_ref[...] = acc[...]

    out = _call_with_rider(
        body, rider, name=name, grid=(m // tm, n // tn, nk),
        in_specs=[pl.BlockSpec((tk, tm), lambda i, j, k: (k, i)), pl.BlockSpec((tk, tn), lambda i, j, k: (k, j))],
        out_specs=[pl.BlockSpec((tm, tn), lambda i, j, k: (i, j))],
        out_shape=[jax.ShapeDtypeStruct((m, n), F32)],
        scratch_shapes=[pltpu.VMEM((tm, tn), F32)], operands=(a, b))
    return out[0] if rider is None else out


def mix_in_forward(x, gn, sc, sh, w_in):
    t, d = x.shape
    tm = _tile(t, ROW_TILE, 16)

    def body(x_ref, gn_ref, sc_ref, sh_ref, w_ref, h_ref, zc_ref, zm_ref):
        xhat, _ = _rms(x_ref[...])
        h = (xhat * gn_ref[...] * (1.0 + sc_ref[...]) + sh_ref[...]).astype(BF16)
        h_ref[...] = h
        z = _dot(h, w_ref[...], NT)
        zc_ref[...] = z[:, :ZC_COLS].astype(BF16)
        zm_ref[...] = z[:, ZC_COLS:].astype(BF16)

    row = pl.BlockSpec((tm, d), lambda i: (i, 0))
    vec = pl.BlockSpec((1, d), lambda i: (0, 0))
    return pl.pallas_call(
        body, name="mix_in_fwd", grid=(t // tm,),
        in_specs=[row, vec, vec, vec, _row(w_in)],
        out_specs=[row, pl.BlockSpec((tm, ZC_COLS), lambda i: (i, 0)), pl.BlockSpec((tm, ZM_COLS), lambda i: (i, 0))],
        out_shape=[jax.ShapeDtypeStruct((t, d), BF16), jax.ShapeDtypeStruct((t, ZC_COLS), BF16),
                   jax.ShapeDtypeStruct((t, ZM_COLS), BF16)],
        compiler_params=_params(("arbitrary",)),
    )(x, gn, sc, sh, w_in)


def _rope_tables(pos, inv_freq):
    ang = pos * inv_freq
    lane = lax.broadcasted_iota(jnp.int32, ang.shape, 1)
    cos, sin = jnp.cos(ang), jnp.sin(ang)
    half = QK_ROPE // 2
    return cos, jnp.where(lane < half, -sin, 0.0), jnp.where(jnp.logical_and(lane >= half, lane < QK_ROPE), sin, 0.0)


def _rope(v, tables):
    cos, sin_a, sin_b = tables
    return v * cos + pltpu.roll(v, LANES - QK_ROPE // 2, 1) * sin_a + pltpu.roll(v, QK_ROPE // 2, 1) * sin_b


def _rope_transposed(dv, tables):
    cos, sin_a, sin_b = tables
    return dv * cos + pltpu.roll(dv * sin_a, QK_ROPE // 2, 1) + pltpu.roll(dv * sin_b, LANES - QK_ROPE // 2, 1)


def mla_project(zm, pos, inv_freq, qg, kvg, w_uq, w_ukv):
    t = zm.shape[0]
    tm = _tile(t, ROW_TILE, 16)

    def body(zm_ref, pos_ref, if_ref, qg_ref, kvg_ref, wq_ref, wkv_ref, qn_ref, kvn_ref, q_ref, k_ref, v_ref):
        zv = zm_ref[...].astype(F32)
        qn = (_rms(zv[:, :Q_LORA])[0] * qg_ref[...]).astype(BF16)
        kvn = (_rms(zv[:, Q_LORA:Q_LORA + KV_LORA])[0] * kvg_ref[...]).astype(BF16)
        qn_ref[...] = qn
        kvn_ref[...] = kvn
        qf = _dot(qn, wq_ref[...], NT) * QK_FOLD
        kvf = _dot(kvn, wkv_ref[...], NT)
        tables = _rope_tables(pos_ref[...], if_ref[...])
        kr = _rope(zv[:, Q_LORA + KV_LORA:], tables).astype(BF16)
        for h in range(MLA_HEADS):
            lo = h * HEAD_PAD
            q_ref[:, lo:lo + QK_NOPE] = qf[:, lo:lo + QK_NOPE].astype(BF16)
            q_ref[:, lo + QK_NOPE:lo + HEAD_PAD] = _rope(qf[:, lo + QK_NOPE:lo + HEAD_PAD], tables).astype(BF16)
            k_ref[:, lo:lo + QK_NOPE] = kvf[:, h * QK_NOPE:(h + 1) * QK_NOPE].astype(BF16)
            k_ref[:, lo + QK_NOPE:lo + HEAD_PAD] = kr
        v_ref[...] = kvf[:, MLA_HEADS * QK_NOPE:].astype(BF16)

    def rows(n):
        return pl.BlockSpec((tm, n), lambda i: (i, 0))

    return pl.pallas_call(
        body, name="mla_project", grid=(t // tm,),
        in_specs=[rows(ZM_COLS), rows(1), _row(inv_freq), _row(qg), _row(kvg), _row(w_uq), _row(w_ukv)],
        out_specs=[rows(Q_LORA), rows(KV_LORA), rows(QK_COLS), rows(QK_COLS), rows(MLA_WIDTH)],
        out_shape=[jax.ShapeDtypeStruct((t, Q_LORA), BF16), jax.ShapeDtypeStruct((t, KV_LORA), BF16),
                   jax.ShapeDtypeStruct((t, QK_COLS), BF16), jax.ShapeDtypeStruct((t, QK_COLS), BF16),
                   jax.ShapeDtypeStruct((t, MLA_WIDTH), BF16)],
        compiler_params=_params(("arbitrary",)),
    )(zm, pos, inv_freq, qg, kvg, w_uq, w_ukv)


def _chunk_mask(shape, q_axis):
    qi = lax.broadcasted_iota(jnp.int32, shape, q_axis) // CHUNK
    ki = lax.broadcasted_iota(jnp.int32, shape, 1 - q_axis) // CHUNK
    return ki <= qi


def attention_forward(q, k, v, rider=None):
    t = q.shape[0]
    tq = _tile(t, ATTN_TILE, CHUNK)

    def body(q_ref, k_ref, v_ref, o_ref, lse_ref):
        i = pl.program_id(1)
        qv = q_ref[...]

        def step(kb, carry, masked, tiles=1):
            m, l, acc = carry
            keys = pl.ds(pl.multiple_of(kb * tq, tq), tiles * tq)
            s = _dot(qv, k_ref[keys, :], NT)
            if masked:
                s = jnp.where(_chunk_mask(s.shape, 0), s, NEG_INF)
            m_new = jnp.maximum(m, jnp.max(s, axis=-1, keepdims=True))
            alpha = jnp.exp2(m - m_new)
            p = jnp.exp2(s - m_new)
            l = alpha * l + jnp.sum(p, axis=-1, keepdims=True)
            acc = alpha * acc + _dot(p.astype(BF16), v_ref[keys, :])
            return m_new, l, acc

        init = (jnp.full((tq, 1), NEG_INF, F32), jnp.zeros((tq, 1), F32), jnp.zeros((tq, V_HEAD), F32))
        carry = lax.fori_loop(0, i // 2, lambda pb, cr: step(2 * pb, cr, False, 2), init)
        carry = lax.fori_loop(0, i % 2, lambda _, cr: step(i - 1, cr, False), carry)
        m, l, acc = step(i, carry, True)
        o_ref[...] = (acc / l).astype(BF16)
        lse_ref[0] = m + jnp.log2(l)

    return _call_with_rider(
        body, rider, name="attn_fwd", grid=(MLA_HEADS, t // tq),
        in_specs=[pl.BlockSpec((tq, HEAD_PAD), lambda h, i: (i, h)),
                  pl.BlockSpec((t, HEAD_PAD), lambda h, i: (0, h)),
                  pl.BlockSpec((t, V_HEAD), lambda h, i: (0, h))],
        out_specs=[pl.BlockSpec((tq, V_HEAD), lambda h, i: (i, h)),
                   pl.BlockSpec((1, tq, 1), lambda h, i: (h, i, 0))],
        out_shape=[jax.ShapeDtypeStruct((t, MLA_WIDTH), BF16), jax.ShapeDtypeStruct((MLA_HEADS, t, 1), F32)],
        scratch_shapes=[], operands=(q, k, v))


def attention_backward(q, k, v, do, lse, delta, rider=None):
    t = q.shape[0]
    tq = _tile(t, ATTN_TILE, CHUNK)
    nq = t // tq

    def body(q_ref, k_ref, v_ref, do_ref, lse_ref, delta_ref, dq_ref, dk_ref, dv_ref, dq_acc):
        kb = pl.program_id(1)

        @pl.when(kb == 0)
        def _():
            dq_acc[...] = jnp.zeros_like(dq_acc)

        kv, vv = k_ref[...], v_ref[...]

        def step(qb, carry, masked):
            dk, dv = carry
            rows = pl.ds(pl.multiple_of(qb * tq, tq), tq)
            qv, dov = q_ref[rows, :], do_ref[rows, :]
            s = _dot(kv, qv, NT)
            if masked:
                s = jnp.where(_chunk_mask(s.shape, 1), s, NEG_INF)
            p = jnp.exp2(s - lse_ref[0, qb])
            dv = dv + _dot(p.astype(BF16), dov)
            dp = _dot(vv, dov, NT)
            ds = (p * (dp - delta_ref[0, qb]) * LN_2).astype(BF16)
            dk = dk + _dot(ds, qv)
            dq_acc[rows, :] += _dot(ds, kv, TN)
            return dk, dv

        carry = step(kb, (jnp.zeros((tq, HEAD_PAD), F32), jnp.zeros((tq, V_HEAD), F32)), True)
        odd = (nq - 1 - kb) % 2
        carry = lax.fori_loop(0, odd, lambda _, cr: step(kb + 1, cr, False), carry)
        first = kb + 1 + odd
        dk, dv = lax.fori_loop(0, (nq - first) // 2,
                               lambda pb, cr: step(first + 2 * pb + 1, step(first + 2 * pb, cr, False), False), carry)
        dk_ref[...] = dk.astype(BF16)
        dv_ref[...] = dv.astype(BF16)

        @pl.when(kb == nq - 1)
        def _():
            dq_ref[...] = dq_acc[...].astype(BF16)

    stat = pl.BlockSpec((1, nq, 1, tq), lambda h, j: (h, 0, 0, 0))
    return _call_with_rider(
        body, rider, name="attn_bwd", grid=(MLA_HEADS, nq),
        in_specs=[pl.BlockSpec((t, HEAD_PAD), lambda h, j: (0, h)),
                  pl.BlockSpec((tq, HEAD_PAD), lambda h, j: (j, h)),
                  pl.BlockSpec((tq, V_HEAD), lambda h, j: (j, h)),
                  pl.BlockSpec((t, V_HEAD), lambda h, j: (0, h)), stat, stat],
        out_specs=[pl.BlockSpec((t, HEAD_PAD), lambda h, j: (0, h)),
                   pl.BlockSpec((tq, HEAD_PAD), lambda h, j: (j, h)),
                   pl.BlockSpec((tq, V_HEAD), lambda h, j: (j, h))],
        out_shape=[jax.ShapeDtypeStruct((t, QK_COLS), BF16), jax.ShapeDtypeStruct((t, QK_COLS), BF16),
                   jax.ShapeDtypeStruct((t, MLA_WIDTH), BF16)],
        scratch_shapes=[pltpu.VMEM((t, HEAD_PAD), F32)], operands=(q, k, v, do, lse, delta))


HALO = 16


def _halo_spec(tm, n, step, last):
    return pl.BlockSpec((HALO, n), lambda i: (jnp.clip(i * (tm // HALO) + step, 0, last), 0))


def _shift_rows(v, prev, n):
    out = pltpu.roll(v, n, 0)
    row = lax.broadcasted_iota(jnp.int32, v.shape, 0)
    for r in range(n):
        out = jnp.where(row == r, prev[HALO - n + r:HALO - n + r + 1, :], out)
    return out


def _advance_rows(v, nxt, n):
    rows = v.shape[0]
    out = pltpu.roll(v, rows - n, 0)
    row = lax.broadcasted_iota(jnp.int32, v.shape, 0)
    for r in range(n):
        out = jnp.where(row == rows - n + r, nxt[r:r + 1, :], out)
    return out


def _conv_taps(zc, zc_prev, first):
    w = CONV_WIDTH
    u = zc[:, w:2 * w] * zc[:, 2 * w:]
    up = jnp.where(first, 0.0, zc_prev[:, w:2 * w] * zc_prev[:, 2 * w:])
    return u, _shift_rows(u, up, 1), _shift_rows(u, up, 2)


def mix_out_forward(zc, o, conv_w, og, gmat_a, gmat_b, w_out, x, gate):
    t, d = x.shape
    tm = _tile(t, ROW_TILE, 16)
    w = CONV_WIDTH

    def body(zc_ref, zp_ref, o_ref, cw_ref, og_ref, ga_ref, gb_ref, w_ref, x_ref, gate_ref,
             xo_ref, yn_ref, y_ref, ya_ref):
        zc_v = zc_ref[...].astype(F32)
        u, u1, u2 = _conv_taps(zc_v, zp_ref[...].astype(F32), pl.program_id(0) == 0)
        cw = cw_ref[...]
        ya = zc_v[:, :w] * (cw[0:1] * u2 + cw[1:2] * u1 + cw[2:3] * u)
        ya_ref[...] = ya.astype(BF16)
        ov = o_ref[...].astype(F32)
        ogv = og_ref[...]
        yn_ref[:, :w] = (ya * lax.rsqrt(_group_mean(ya * ya, ga_ref[...]) + EPS) * ogv[:, :w]).astype(BF16)
        yn_ref[:, w:] = (ov * lax.rsqrt(_group_mean(ov * ov, gb_ref[...]) + EPS) * ogv[:, w:]).astype(BF16)
        y = _dot(yn_ref[...], w_ref[...])
        y_ref[...] = y.astype(BF16)
        xo_ref[...] = x_ref[...] + gate_ref[...] * y

    def rows(n):
        return pl.BlockSpec((tm, n), lambda i: (i, 0))

    return pl.pallas_call(
        body, name="mix_out_fwd", grid=(t // tm,),
        in_specs=[rows(ZC_COLS), _halo_spec(tm, ZC_COLS, -1, t // HALO - 1), rows(MLA_WIDTH), _row(conv_w), _row(og),
                  _row(gmat_a), _row(gmat_b), _row(w_out), rows(d), _row(gate)],
        out_specs=[rows(d), rows(MIX_WIDTH), rows(d), rows(w)],
        out_shape=[jax.ShapeDtypeStruct((t, d), F32), jax.ShapeDtypeStruct((t, MIX_WIDTH), BF16),
                   jax.ShapeDtypeStruct((t, d), BF16), jax.ShapeDtypeStruct((t, w), BF16)],
        compiler_params=_params(("arbitrary",)),
    )(zc, zc, o, conv_w, og, gmat_a, gmat_b, w_out, x, gate)


def _group_norm_bwd(dyn, y, og, gmat):
    rs = lax.rsqrt(_group_mean(y * y, gmat) + EPS)
    yhat = y * rs
    d_og = jnp.sum(dyn * yhat, axis=0, keepdims=True)
    dyh = dyn * og
    return rs * (dyh - yhat * _group_mean(dyh * yhat, gmat)), d_og


def mix_out_backward(dxo, y, gate, ya, o, yn, og, gmat_a, gmat_b, w_out, rider=None):
    t, d = dxo.shape
    tm = _tile(t, ROW_TILE, 16)
    w = CONV_WIDTH

    def body(dxo_ref, y_ref, gate_ref, ya_ref, o_ref, yn_ref, og_ref, ga_ref, gb_ref, w_ref,
             dya_ref, do_ref, delta_ref, sd_ref, so_ref, gw_ref):
        @pl.when(pl.program_id(0) == 0)
        def _():
            sd_ref[...] = jnp.zeros_like(sd_ref)
            so_ref[...] = jnp.zeros_like(so_ref)
            gw_ref[...] = jnp.zeros_like(gw_ref)

        dxo_v = dxo_ref[...]
        dy = (gate_ref[...] * dxo_v).astype(BF16)
        gw_ref[...] += _dot(yn_ref[...], dy, TN)
        sd_ref[0:1, :] += jnp.sum(dxo_v * y_ref[...].astype(F32), axis=0, keepdims=True)
        dyn = _dot(dy, w_ref[...], NT)
        ogv = og_ref[...]
        ov = o_ref[...].astype(F32)
        dya, d_og_a = _group_norm_bwd(dyn[:, :w], ya_ref[...].astype(F32), ogv[:, :w], ga_ref[...])
        dov, d_og_b = _group_norm_bwd(dyn[:, w:], ov, ogv[:, w:], gb_ref[...])
        dya_ref[...] = dya.astype(BF16)
        do_ref[...] = dov.astype(BF16)
        so_ref[0:1, :w] += d_og_a
        so_ref[0:1, w:] += d_og_b
        prod = dov * ov
        for h in range(MLA_HEADS):
            delta_ref[h] = jnp.sum(prod[:, h * V_HEAD:(h + 1) * V_HEAD], axis=-1, keepdims=True)

    def rows(n):
        return pl.BlockSpec((tm, n), lambda i: (i, 0))

    return _call_with_rider(
        body, rider, name="mix_out_bwd", grid=(t // tm,),
        in_specs=[rows(d), rows(d), _row(gate), rows(w), rows(MLA_WIDTH), rows(MIX_WIDTH), _row(og), _row(gmat_a),
                  _row(gmat_b), _row(w_out)],
        out_specs=[rows(w), rows(MLA_WIDTH), pl.BlockSpec((MLA_HEADS, tm, 1), lambda i: (0, i, 0)),
                   pl.BlockSpec((8, d), lambda i: (0, 0)), pl.BlockSpec((8, MIX_WIDTH), lambda i: (0, 0)),
                   pl.BlockSpec((MIX_WIDTH, d), lambda i: (0, 0))],
        out_shape=[jax.ShapeDtypeStruct((t, w), BF16),
                   jax.ShapeDtypeStruct((t, MLA_WIDTH), BF16), jax.ShapeDtypeStruct((MLA_HEADS, t, 1), F32),
                   jax.ShapeDtypeStruct((8, d), F32), jax.ShapeDtypeStruct((8, MIX_WIDTH), F32),
                   jax.ShapeDtypeStruct((MIX_WIDTH, d), F32)],
        scratch_shapes=[], operands=(dxo, y, gate, ya, o, yn, og, gmat_a, gmat_b, w_out))


def conv_backward(zc, dya, conv_w, h):
    t, d = h.shape
    tm = _tile(t, ROW_TILE, 16)
    nt = t // tm
    w = CONV_WIDTH

    def body(zc_ref, zp_ref, zn_ref, dya_ref, dn_ref, cw_ref, h_ref, dzc_ref, sums_ref, gw_ref):
        i = pl.program_id(0)

        @pl.when(i == 0)
        def _():
            sums_ref[...] = jnp.zeros_like(sums_ref)
            gw_ref[...] = jnp.zeros_like(gw_ref)

        zc_v = zc_ref[...].astype(F32)
        u, u1, u2 = _conv_taps(zc_v, zp_ref[...].astype(F32), i == 0)
        cw = cw_ref[...]
        dya_v = dya_ref[...].astype(F32)
        dyc = dya_v * zc_v[:, :w]
        dyc_next = jnp.where(i == nt - 1, 0.0, dn_ref[...].astype(F32) * zn_ref[:, :w].astype(F32))
        du = cw[2:3] * dyc + cw[1:2] * _advance_rows(dyc, dyc_next, 1) + cw[0:1] * _advance_rows(dyc, dyc_next, 2)
        dzc_ref[:, :w] = (dya_v * (cw[0:1] * u2 + cw[1:2] * u1 + cw[2:3] * u)).astype(BF16)
        dzc_ref[:, w:2 * w] = (du * zc_v[:, 2 * w:]).astype(BF16)
        dzc_ref[:, 2 * w:] = (du * zc_v[:, w:2 * w]).astype(BF16)
        _add_rows(sums_ref, [jnp.sum(dyc * tap, axis=0, keepdims=True) for tap in (u2, u1, u)])
        gw_ref[...] += _dot(dzc_ref[...], h_ref[...], TN)

    def rows(n):
        return pl.BlockSpec((tm, n), lambda i: (i, 0))

    def halo(n, step):
        return _halo_spec(tm, n, step, t // HALO - 1)

    return pl.pallas_call(
        body, name="conv_bwd", grid=(nt,),
        in_specs=[rows(ZC_COLS), halo(ZC_COLS, -1), halo(ZC_COLS, tm // HALO), rows(w), halo(w, tm // HALO),
                  _row(conv_w), rows(d)],
        out_specs=[rows(ZC_COLS), pl.BlockSpec((8, w), lambda i: (0, 0)), pl.BlockSpec((ZC_COLS, d), lambda i: (0, 0))],
        out_shape=[jax.ShapeDtypeStruct((t, ZC_COLS), BF16), jax.ShapeDtypeStruct((8, w), F32),
                   jax.ShapeDtypeStruct((ZC_COLS, d), F32)],
        compiler_params=_params(("arbitrary",)),
    )(zc, zc, zc, dya, dya, conv_w, h)


def _rms_bwd(dy, x, g):
    xhat, r = _rms(x)
    d_g = jnp.sum(dy * xhat, axis=0, keepdims=True)
    dxh = dy * g
    return r * (dxh - xhat * jnp.mean(dxh * xhat, axis=-1, keepdims=True)), d_g


def mla_project_backward(dq, dk, dv, zm, qn, kvn, h, pos, inv_freq, qg, kvg, w_uq, w_ukv):
    t, d = h.shape
    tm = _tile(t, ROW_TILE, 16)

    def body(dq_ref, dk_ref, dv_ref, zm_ref, qn_ref, kvn_ref, h_ref, pos_ref, if_ref, qg_ref, kvg_ref, wq_ref,
             wkv_ref, dzm_ref, sums_ref, guq_ref, gukv_ref, gin_ref, dql_ref, dkvl_ref):
        @pl.when(pl.program_id(0) == 0)
        def _():
            sums_ref[...] = jnp.zeros_like(sums_ref)
            guq_ref[...] = jnp.zeros_like(guq_ref)
            gukv_ref[...] = jnp.zeros_like(gukv_ref)
            gin_ref[...] = jnp.zeros_like(gin_ref)

        tables = _rope_tables(pos_ref[...], if_ref[...])
        dkr = jnp.zeros((tm, LANES), F32)
        for h in range(MLA_HEADS):
            lo = h * HEAD_PAD
            dql_ref[:, lo:lo + QK_NOPE] = (dq_ref[:, lo:lo + QK_NOPE].astype(F32) * QK_FOLD).astype(BF16)
            dql_ref[:, lo + QK_NOPE:lo + HEAD_PAD] = _rope_transposed(
                dq_ref[:, lo + QK_NOPE:lo + HEAD_PAD].astype(F32) * QK_FOLD, tables).astype(BF16)
            dkvl_ref[:, h * QK_NOPE:(h + 1) * QK_NOPE] = dk_ref[:, lo:lo + QK_NOPE]
            dkr = dkr + dk_ref[:, lo + QK_NOPE:lo + HEAD_PAD].astype(F32)
        dkvl_ref[:, MLA_HEADS * QK_NOPE:] = dv_ref[...]
        zv = zm_ref[...].astype(F32)
        dqn = _dot(dql_ref[...], wq_ref[...])
        dkvn = _dot(dkvl_ref[...], wkv_ref[...])
        dcq, d_qg = _rms_bwd(dqn, zv[:, :Q_LORA], qg_ref[...])
        dckv, d_kvg = _rms_bwd(dkvn, zv[:, Q_LORA:Q_LORA + KV_LORA], kvg_ref[...])
        dzm_ref[:, :Q_LORA] = dcq.astype(BF16)
        dzm_ref[:, Q_LORA:Q_LORA + KV_LORA] = dckv.astype(BF16)
        dzm_ref[:, Q_LORA + KV_LORA:] = _rope_transposed(dkr, tables).astype(BF16)
        sums_ref[0:1, :Q_LORA] += d_qg
        sums_ref[0:1, Q_LORA:Q_LORA + KV_LORA] += d_kvg
        guq_ref[...] += _dot(dql_ref[...], qn_ref[...], TN)
        gukv_ref[...] += _dot(dkvl_ref[...], kvn_ref[...], TN)
        gin_ref[...] += _dot(dzm_ref[...], h_ref[...], TN)

    def rows(n):
        return pl.BlockSpec((tm, n), lambda i: (i, 0))

    def whole(r, n):
        return pl.BlockSpec((r, n), lambda i: (0, 0))

    return pl.pallas_call(
        body, name="mla_project_bwd", grid=(t // tm,),
        in_specs=[rows(QK_COLS), rows(QK_COLS), rows(MLA_WIDTH), rows(ZM_COLS), rows(Q_LORA), rows(KV_LORA), rows(d),
                  rows(1), _row(inv_freq), _row(qg), _row(kvg), _row(w_uq), _row(w_ukv)],
        out_specs=[rows(ZM_COLS), whole(8, ZM_COLS), whole(QK_COLS, Q_LORA), whole(QK_COLS, KV_LORA),
                   whole(ZM_COLS, d)],
        out_shape=[jax.ShapeDtypeStruct((t, ZM_COLS), BF16), jax.ShapeDtypeStruct((8, ZM_COLS), F32),
                   jax.ShapeDtypeStruct((QK_COLS, Q_LORA), F32), jax.ShapeDtypeStruct((QK_COLS, KV_LORA), F32),
                   jax.ShapeDtypeStruct((ZM_COLS, d), F32)],
        scratch_shapes=[pltpu.VMEM((tm, QK_COLS), BF16), pltpu.VMEM((tm, QK_COLS), BF16)],
        compiler_params=_params(("arbitrary",)),
    )(dq, dk, dv, zm, qn, kvn, h, pos, inv_freq, qg, kvg, w_uq, w_ukv)


def mix_in_backward(dzc, dzm, w_in, x, dxo, gn, sc, gate, rider=None):
    t, d = x.shape
    tm = _tile(t, ROW_TILE, 16)

    def body(dzc_ref, dzm_ref, w_ref, x_ref, dxo_ref, gn_ref, sc_ref, gate_ref, dx_ref, dy_ref, sums_ref):
        @pl.when(pl.program_id(0) == 0)
        def _():
            sums_ref[...] = jnp.zeros_like(sums_ref)

        dh = _dot(dzc_ref[...], w_ref[:ZC_COLS, :]) + _dot(dzm_ref[...], w_ref[ZC_COLS:, :])
        dx, d_sh, d_sc, d_gn = _norm_mod_bwd(dh, x_ref[...], gn_ref[...], sc_ref[...])
        dx = dxo_ref[...] + dx
        dx_ref[...] = dx
        dy_ref[...] = (0.5 * gate_ref[...] * dx).astype(BF16)
        _add_rows(sums_ref, [d_sh, d_sc, d_gn])

    def rows(n):
        return pl.BlockSpec((tm, n), lambda i: (i, 0))

    return _call_with_rider(
        body, rider, name="mix_in_bwd", grid=(t // tm,),
        in_specs=[rows(ZC_COLS), rows(ZM_COLS), _row(w_in), rows(d), rows(d), _row(gn), _row(sc), _row(gate)],
        out_specs=[rows(d), rows(d), pl.BlockSpec((8, d), lambda i: (0, 0))],
        out_shape=[jax.ShapeDtypeStruct((t, d), F32), jax.ShapeDtypeStruct((t, d), BF16),
                   jax.ShapeDtypeStruct((8, d), F32)],
        scratch_shapes=[], operands=(dzc, dzm, w_in, x, dxo, gn, sc, gate))


def _adamw_step(w, g, m, v):
    m_new = ADAM_B1 * m + (1.0 - ADAM_B1) * g
    v_new = ADAM_B2 * v + (1.0 - ADAM_B2) * (g * g)
    m_hat = m_new / (1.0 - ADAM_B1 ** ADAM_STEP)
    v_hat = v_new / (1.0 - ADAM_B2 ** ADAM_STEP)
    return -ADAM_LR * (m_hat / (jnp.sqrt(v_hat) + ADAM_EPS) + ADAM_WD * w), m_new, v_new


def adamw(w, g, m, v, name):
    r, n = w.shape
    tr = _tile(r, max(8, ADAM_TILE_ELEMS // n), 8)

    def body(w_ref, g_ref, m_ref, v_ref, d_ref, mo_ref, vo_ref):
        d_ref[...], mo_ref[...], vo_ref[...] = _adamw_step(w_ref[...], g_ref[...], m_ref[...], v_ref[...])

    blk = pl.BlockSpec((tr, n), lambda i: (i, 0))
    shape = jax.ShapeDtypeStruct((r, n), F32)
    return pl.pallas_call(
        body, name=name, grid=(r // tr,), in_specs=[blk] * 4, out_specs=[blk] * 3, out_shape=[shape] * 3,
        compiler_params=_params(("arbitrary",)),
    )(w, g, m, v)


def adamw_received(w, own, got, m, v, name):
    r, n = w.shape
    tr = _tile(r, SUM_ROWS, 16)

    def body(w_ref, own_ref, got_ref, m_ref, v_ref, g_ref, d_ref, mo_ref, vo_ref):
        g = own_ref[...]
        for j in range(3):
            g = g + got_ref[j].astype(F32)
        g_ref[...] = g
        d_ref[...], mo_ref[...], vo_ref[...] = _adamw_step(w_ref[...], g, m_ref[...], v_ref[...])

    blk = pl.BlockSpec((tr, n), lambda i: (i, 0))
    shape = jax.ShapeDtypeStruct((r, n), F32)
    return pl.pallas_call(
        body, name=name, grid=(r // tr,),
        in_specs=[blk, blk, pl.BlockSpec((3, tr, n), lambda i: (0, i, 0)), blk, blk],
        out_specs=[blk] * 4, out_shape=[shape] * 4, compiler_params=_params(("arbitrary",)),
    )(w, own, got, m, v)


def _pad_to(v, n):
    return jnp.pad(v, (0, n - v.shape[0]))


def _pad_heads(w, axis_len):
    n = w.shape[1]
    return jnp.pad(w.reshape(MLA_HEADS, axis_len, n), ((0, 0), (0, HEAD_PAD - axis_len), (0, 0))).reshape(-1, n)


def _swap_head_parts(w, inner, outer):
    n = w.shape[1]
    return w.reshape(outer, inner, QK_NOPE, n).transpose(1, 0, 2, 3).reshape(-1, n)


def kernel(x, c, positions, ada_w, ada_b, norm_ffn1_g, ffn1_w1, ffn1_w3, ffn1_w2, norm_mix_g, w_in, conv_w, q_norm_g, w_uq, kv_norm_g, w_ukv, out_norm_g, w_out, norm_ffn2_g, ffn2_w1, ffn2_w3, ffn2_w2, final_norm_g, loss_target, m_ada_w, m_ada_b, m_norm_ffn1_g, m_ffn1_w1, m_ffn1_w3, m_ffn1_w2, m_norm_mix_g, m_w_in, m_conv_w, m_q_norm_g, m_w_uq, m_kv_norm_g, m_w_ukv, m_out_norm_g, m_w_out, m_norm_ffn2_g, m_ffn2_w1, m_ffn2_w3, m_ffn2_w2, m_final_norm_g, v_ada_w, v_ada_b, v_norm_ffn1_g, v_ffn1_w1, v_ffn1_w3, v_ffn1_w2, v_norm_mix_g, v_w_in, v_conv_w, v_q_norm_g, v_w_uq, v_kv_norm_g, v_w_ukv, v_out_norm_g, v_w_out, v_norm_ffn2_g, v_ffn2_w1, v_ffn2_w3, v_ffn2_w2, v_final_norm_g):
    t, d = x.shape[1], x.shape[2]
    f = ffn1_w2.shape[1] * N_DEV
    me = 4 * lax.axis_index("x") + 2 * lax.axis_index("y") + lax.axis_index("c")
    my_c = lax.axis_index("c")
    my_chip = 2 * lax.axis_index("x") + lax.axis_index("y")
    xs = x[0]
    n_ada = ada_w.shape[2]
    cw_n = conv_w.shape[2]

    c_rows = jnp.broadcast_to(c, (8, d))
    conv_rows = jnp.pad(conv_w[0], ((0, 8 - CONV_K), (0, LANES - cw_n)))
    ffn1_blocks = [ffn1_w1[0].T.astype(BF16), ffn1_w3[0].T.astype(BF16), ffn1_w2[0].astype(BF16)]
    ffn2_blocks = [ffn2_w1[0].T.astype(BF16), ffn2_w3[0].T.astype(BF16), ffn2_w2[0].astype(BF16)]
    c_all, conv_all, *ffn1_all = all_gather_relayed([c_rows, conv_rows] + ffn1_blocks, [0] * 5, "gather_first")
    c_all = c_all[:, 0, :]
    conv_full8 = conv_all[:, :, :cw_n].transpose(1, 0, 2).reshape(8, CONV_WIDTH)
    ffn1_ws = [w.reshape(f, d) for w in ffn1_all]
    gather_mix = riding_gather(
        [w_in[0].T.astype(BF16), w_uq[0].T.astype(BF16), w_ukv[0].T.astype(BF16), w_out[0].astype(BF16)], [0, 0, 0, 0])

    ada_b_cols = lax.dynamic_slice_in_dim(ada_b, me * n_ada, n_ada, axis=1)
    mod_cols = ada_forward(c_all, ada_w[0], ada_b_cols)
    mod_all, = all_gather([mod_cols], [0], "gather_mod")
    mod = lax.dynamic_index_in_dim(mod_all, me, axis=1, keepdims=False).reshape(N_MOD, 1, d)
    sh1, sc1, g1, sh2, sc2, g2, sh3, sc3, g3 = [mod[i] for i in range(N_MOD)]

    gf = final_norm_g.reshape(1, d)
    x1, h1, a1, b1, y1, *gathered = ffn_forward(xs, norm_ffn1_g, sc1, sh1, g1, ffn1_ws, "ffn1_fwd", gather_mix)
    w_in_p = jnp.pad(gathered[0].reshape(IN_COLS, d), ((0, ZC_COLS + ZM_COLS - IN_COLS), (0, 0)))
    w_uq_p = _pad_heads(gathered[1].reshape(-1, Q_LORA), QK_NOPE + QK_ROPE)
    w_ukv_p = _swap_head_parts(gathered[2].reshape(-1, KV_LORA), 2, MLA_HEADS)
    w_out_f = gathered[3].reshape(MIX_WIDTH, d)
    h2, zc, zm = mix_in_forward(x1, norm_mix_g, sc2, sh2, w_in_p)
    pos = positions[0].astype(F32).reshape(t, 1)
    inv_freq = ROPE_THETA ** (-jnp.arange(0, QK_ROPE, 2, dtype=F32) / QK_ROPE)
    inv_freq = jnp.concatenate([inv_freq, inv_freq, jnp.zeros((LANES - QK_ROPE,), F32)]).reshape(1, LANES)
    qn, kvn, q, k, v = mla_project(zm, pos, inv_freq, q_norm_g, kv_norm_g, w_uq_p, w_ukv_p)
    o, lse, *ffn2_all = attention_forward(q, k, v, riding_gather(ffn2_blocks, [0] * 3))
    ffn2_ws = [w.reshape(f, d) for w in ffn2_all]
    lane = jnp.arange(CONV_WIDTH)
    gmat_a = (lane[:, None] // (CONV_WIDTH // CONV_GROUPS) == lane[None, :] // (CONV_WIDTH // CONV_GROUPS))
    gmat_a = (gmat_a / (CONV_WIDTH // CONV_GROUPS)).astype(BF16)
    gmat_b = ((lane[:, None] // V_HEAD == lane[None, :] // V_HEAD) / V_HEAD).astype(BF16)
    x2, yn, y2, ya = mix_out_forward(zc, o, conv_full8, out_norm_g, gmat_a, gmat_b, w_out_f, x1, g2)
    dx3, h3, a3, b3, y3, dy3, sums_f = ffn_forward(x2, norm_ffn2_g, sc3, sh3, g3, ffn2_ws, "ffn2_fwd",
                                                   loss_head=(loss_target[0], gf))

    chip_idx = jnp.bitwise_xor(my_chip, jnp.array([0, 2, 1, 3], jnp.int32)).astype(jnp.int32)
    src_idx = (2 * chip_idx + my_c).astype(jnp.int32)

    def row_blocks(named):
        return [g.reshape(N_DEV, g.shape[0] // N_DEV, g.shape[1]) for _, g in named]

    def chip_sums(named, g8, got):
        return [add_sibling(g, r, src_idx, chip_idx, "rs_add_" + n) for g, r, (n, _) in zip(g8, got, named)]

    da3, db3, g_w2b = ffn_backward_gate(dy3, a3, b3, ffn2_ws[2], "ffn2_bwd_gate")
    dx2, sums_3 = ffn_backward_norm(da3, db3, dx3, x2, y3, norm_ffn2_g, sc3, ffn2_ws[0], ffn2_ws[1], "ffn2_bwd_norm")
    ffn2_named = [("ffn2_w1", matmul_tn(da3, h3, "ffn2_gw1")), ("ffn2_w3", matmul_tn(db3, h3, "ffn2_gw3")),
                  ("ffn2_w2", g_w2b)]
    ffn2_g8 = row_blocks(ffn2_named)
    dya, do, delta, sums_2d, sums_2o, g_w_out, *ffn2_sib = mix_out_backward(
        dx2, y2, g2, ya, o, yn, out_norm_g, gmat_a, gmat_b, w_out_f, riding_sibling(ffn2_g8))
    ffn2_sums = chip_sums(ffn2_named, ffn2_g8, ffn2_sib)
    nq = t // _tile(t, ATTN_TILE, CHUNK)
    stat_shape = (MLA_HEADS, nq, 1, t // nq)
    dq, dk, dv, *ffn2_got = attention_backward(q, k, v, do, lse.reshape(stat_shape), delta.reshape(stat_shape),
                                               riding_exchange([s[1] for s in ffn2_sums]))
    dzc, sums_c, g_w_in_conv = conv_backward(zc, dya, conv_full8, h2)
    dzm, sums_m, g_w_uq_p, g_w_ukv_p, g_w_in_mla = mla_project_backward(
        dq, dk, dv, zm, qn, kvn, h2, pos, inv_freq, q_norm_g, kv_norm_g, w_uq_p, w_ukv_p)
    g_w_in = jnp.concatenate([g_w_in_conv, g_w_in_mla])[:IN_COLS]
    g_w_uq = g_w_uq_p.reshape(MLA_HEADS, HEAD_PAD, Q_LORA)[:, :QK_NOPE + QK_ROPE].reshape(-1, Q_LORA)
    g_w_ukv = _swap_head_parts(g_w_ukv_p, MLA_HEADS, 2)
    mix_named = [("w_in", g_w_in), ("w_uq", g_w_uq), ("w_ukv", g_w_ukv), ("w_out", g_w_out)]
    mix_g8 = row_blocks(mix_named)
    dx1, dy1, sums_1m, *mix_sib = mix_in_backward(dzc, dzm, w_in_p, x1, dx2, norm_mix_g, sc2, g1, riding_sibling(mix_g8))
    mix_sums = chip_sums(mix_named, mix_g8, mix_sib)
    da1, db1, g_w2a, *mix_got = ffn_backward_gate(dy1, a1, b1, ffn1_ws[2], "ffn1_bwd_gate",
                                                  riding_exchange([s[1] for s in mix_sums]))
    ffn1_pair = [("ffn1_w2", g_w2a), ("ffn1_w1", matmul_tn(da1, h1, "ffn1_gw1"))]
    pair_g8 = row_blocks(ffn1_pair)
    g_w3a, *pair_sib = matmul_tn(db1, h1, "ffn1_gw3", riding_sibling(pair_g8))
    ffn1_last = [("ffn1_w3", g_w3a)]
    last_g8 = row_blocks(ffn1_last)
    ffn1_named = ffn1_pair + ffn1_last
    ffn1_sums = chip_sums(ffn1_pair, pair_g8, pair_sib) + chip_sums(
        ffn1_last, last_g8, exchange_sibling(last_g8, "rs_sibling_ffn1_w3"))
    dx0, sums_1, *ffn1_got = ffn_backward_norm(da1, db1, dx1, xs, y1, norm_ffn1_g, sc1, ffn1_ws[0], ffn1_ws[1], "ffn1_bwd_norm",
                                               riding_exchange([s[1] for s in ffn1_sums]))
    reduced = {}
    for named, group_sums, group_got in ((ffn2_named, ffn2_sums, ffn2_got), (mix_named, mix_sums, mix_got),
                                         (ffn1_named, ffn1_sums, ffn1_got)):
        for (n, _), (own, _), got in zip(named, group_sums, group_got):
            reduced[n] = (own, got)

    dmod = jnp.concatenate([sums_1[0], sums_1[1], sums_1[2], sums_1m[0], sums_1m[1], sums_2d[0],
                            sums_3[0], sums_3[1], sums_3[2]])
    pieces = [dmod, sums_1[3], sums_1m[2], sums_m[0, :Q_LORA], sums_m[0, Q_LORA:Q_LORA + KV_LORA], sums_2o[0],
              sums_3[3], sums_f[0], sums_f[1], sums_c[:CONV_K].reshape(-1)]
    plens = [p.shape[0] for p in pieces]
    poffs = [sum(plens[:i]) for i in range(len(plens))]
    vec_len = -(-sum(plens) // 1024) * 1024
    vec = _pad_to(jnp.concatenate(pieces), vec_len).reshape(-1, LANES)
    vec_all, = all_gather([vec], [0], "gather_sums")
    tot = sum_devices(vec_all).reshape(-1)
    g_ada_b, g_n1, g_nmix, g_qg, g_kvg, g_og, g_n3, g_gf, loss_lanes, g_conv_full = [
        tot[o:o + n] for o, n in zip(poffs, plens)]
    loss = sum_lanes(loss_lanes.reshape(1, d))[0, 0]
    g_conv = lax.dynamic_slice_in_dim(g_conv_full.reshape(CONV_K, CONV_WIDTH), me * cw_n, cw_n, axis=1)
    dmod_all = vec_all.reshape(N_DEV, vec_len)[:, :N_MOD * d]
    dmod_cols = lax.dynamic_slice_in_dim(dmod_all, me * n_ada, n_ada, axis=1)
    g_ada_w = ada_backward(jnp.pad(c_all, ((0, 8), (0, 0))), jnp.pad(dmod_cols, ((0, 8), (0, 0))))

    def update(name, w, g, m, v, received=None):
        k, n = w.shape[-2:]
        if g.shape == (k, n):
            flat, back = (lambda a: a.reshape(k, n)), (lambda a: a.reshape(w.shape))
        else:
            flat, back = (lambda a: a.reshape(k, n).T), (lambda a: a.T.reshape(w.shape))
        if received is None:
            out = (g,) + tuple(adamw(flat(w), g, flat(m), flat(v), "adamw_" + name))
        else:
            out = adamw_received(flat(w), g, received, flat(m), flat(v), "adamw_" + name)
        return tuple(back(a) for a in out)

    res = {}
    res["ada_w"] = update("ada_w", ada_w, g_ada_w, m_ada_w, v_ada_w)
    big = [("ffn1_w1", ffn1_w1, m_ffn1_w1, v_ffn1_w1), ("ffn1_w3", ffn1_w3, m_ffn1_w3, v_ffn1_w3),
           ("ffn2_w1", ffn2_w1, m_ffn2_w1, v_ffn2_w1), ("ffn2_w3", ffn2_w3, m_ffn2_w3, v_ffn2_w3),
           ("w_in", w_in, m_w_in, v_w_in), ("w_uq", w_uq, m_w_uq, v_w_uq), ("w_ukv", w_ukv, m_w_ukv, v_w_ukv),
           ("ffn1_w2", ffn1_w2, m_ffn1_w2, v_ffn1_w2), ("ffn2_w2", ffn2_w2, m_ffn2_w2, v_ffn2_w2),
           ("w_out", w_out, m_w_out, v_w_out)]
    for name, w, m, v in big:
        res[name] = update(name, w, reduced[name][0], m, v, reduced[name][1])
    smalls = [("ada_b", ada_b, g_ada_b, m_ada_b, v_ada_b),
              ("norm_ffn1_g", norm_ffn1_g, g_n1, m_norm_ffn1_g, v_norm_ffn1_g),
              ("norm_mix_g", norm_mix_g, g_nmix, m_norm_mix_g, v_norm_mix_g),
              ("conv_w", conv_w, g_conv, m_conv_w, v_conv_w),
              ("q_norm_g", q_norm_g, g_qg, m_q_norm_g, v_q_norm_g),
              ("kv_norm_g", kv_norm_g, g_kvg, m_kv_norm_g, v_kv_norm_g),
              ("out_norm_g", out_norm_g, g_og, m_out_norm_g, v_out_norm_g),
              ("norm_ffn2_g", norm_ffn2_g, g_n3, m_norm_ffn2_g, v_norm_ffn2_g),
              ("final_norm_g", final_norm_g, g_gf, m_final_norm_g, v_final_norm_g)]
    slens = [w.size for _, w, _, _, _ in smalls]
    soffs = [sum(slens[:i]) for i in range(len(slens))]
    s_len = -(-sum(slens) // 1024) * 1024

    def pack_small(i):
        return _pad_to(jnp.concatenate([s[i].reshape(-1) for s in smalls]), s_len).reshape(8, -1)

    s_out = adamw(pack_small(1), pack_small(2), pack_small(3), pack_small(4), "adamw_small")
    for (name, w, g, _, _), o, n in zip(smalls, soffs, slens):
        res[name] = (g.reshape(w.shape),) + tuple(a.reshape(-1)[o:o + n].reshape(w.shape) for a in s_out)

    order = ["ada_w", "ada_b", "norm_ffn1_g", "ffn1_w1", "ffn1_w3", "ffn1_w2", "norm_mix_g", "w_in", "conv_w",
             "q_norm_g", "w_uq", "kv_norm_g", "w_ukv", "out_norm_g", "w_out", "norm_ffn2_g", "ffn2_w1", "ffn2_w3",
             "ffn2_w2", "final_norm_g"]
    return (loss, dx0.reshape(x.shape), *[res[n][0] for n in order], *[res[n][1] for n in order],
            *[res[n][2] for n in order], *[res[n][3] for n in order])
```

```python
import functools

import jax
import jax.numpy as jnp
from jax import lax
from jax.experimental import pallas as pl
from jax.experimental.pallas import tpu as pltpu

F32 = jnp.float32
BF16 = jnp.bfloat16
MESH_ID = pl.DeviceIdType.MESH
N_DEV = 8

EPS = 1e-6
CHUNK = 64
N_MOD = 9
CONV_WIDTH = 512
CONV_GROUPS = 8
CONV_K = 3
MLA_HEADS = 4
QK_NOPE = 128
QK_ROPE = 64
V_HEAD = 128
Q_LORA = 384
KV_LORA = 256
ROPE_THETA = 10000.0
MLA_WIDTH = MLA_HEADS * V_HEAD
MIX_WIDTH = CONV_WIDTH + MLA_WIDTH
IN_COLS = 3 * CONV_WIDTH + Q_LORA + KV_LORA + QK_ROPE
ZC_COLS = 3 * CONV_WIDTH
ZM_COLS = Q_LORA + KV_LORA + 128
HEAD_PAD = 256
QK_COLS = MLA_HEADS * HEAD_PAD
ATTN_SCALE = (QK_NOPE + QK_ROPE) ** -0.5
LOG2_E = 1.4426950408889634
LN_2 = 0.6931471805599453
QK_FOLD = ATTN_SCALE * LOG2_E
NEG_INF = -1e30

ADAM_LR = 0.001
ADAM_B1 = 0.9
ADAM_B2 = 0.999
ADAM_EPS = 1e-08
ADAM_WD = 0.01
ADAM_STEP = 10

LANES = 128
VMEM_LIMIT = 56 * 1024 * 1024
ROW_TILE = 1024
FFN_FWD_TILE = (256, 2816)
FFN_BWD_TILE = (512, 1408)
FFN_NORM_TILE = (256, 2816)
GRAD_TILE = 1408
GRAD_DEPTH = 2048
SUM_ROWS = 256
ADAM_TILE_ELEMS = 1 << 19
ATTN_TILE = 1024

NN = (((1,), (0,)), ((), ()))
NT = (((1,), (1,)), ((), ()))
TN = (((0,), (0,)), ((), ()))


def _dot(a, b, dims=NN):
    return lax.dot_general(a, b, dims, preferred_element_type=F32)


def _tile(n, cap, mult=LANES):
    best = None
    for t in range(mult, min(n, cap) + 1, mult):
        if n % t == 0:
            best = t
    return n if best is None else best


def _params(sem=None):
    return pltpu.CompilerParams(dimension_semantics=sem, vmem_limit_bytes=VMEM_LIMIT)


def _row(v):
    return pl.BlockSpec(v.shape, lambda *_: (0,) * v.ndim)


def _sigmoid(x):
    return 0.5 * jnp.tanh(0.5 * x) + 0.5


def _rms(x):
    r = lax.rsqrt(jnp.mean(x * x, axis=-1, keepdims=True) + EPS)
    return x * r, r


def _norm_mod_bwd(dh, x, gn, sc):
    xhat, r = _rms(x)
    d_sh = jnp.sum(dh, axis=0, keepdims=True)
    d_sc = jnp.sum(dh * (xhat * gn), axis=0, keepdims=True)
    dxn = dh * (1.0 + sc)
    d_gn = jnp.sum(dxn * xhat, axis=0, keepdims=True)
    dxh = dxn * gn
    dx = r * (dxh - xhat * jnp.mean(dxh * xhat, axis=-1, keepdims=True))
    return dx, d_sh, d_sc, d_gn


def _group_mean(v, gmat):
    return _dot(v.astype(BF16), gmat)


def _add_rows(ref, rows):
    for r, v in enumerate(rows):
        ref[r:r + 1, :] += v


def _window(ref, axis, j):
    return ref.at[(slice(None),) * axis + (j,)]


def _any_specs(n):
    return [pl.BlockSpec(memory_space=pl.ANY)] * n


def all_gather(blocks, axes, name):
    n_arr = len(blocks)

    def body(*refs):
        start, forward, finish = _gather_steps(refs[:n_arr], refs[n_arr:2 * n_arr], axes, *refs[2 * n_arr:])
        start()
        for j in range(3):
            forward(j)
        finish()

    return pl.pallas_call(
        body, name=name, out_shape=_gathered_shapes(blocks, axes),
        in_specs=_any_specs(n_arr), out_specs=_any_specs(n_arr), scratch_shapes=_gather_sems(n_arr),
    )(*blocks)


def all_gather_relayed(blocks, axes, name):
    n_arr = len(blocks)
    arrays = range(n_arr)

    def body(*refs):
        ins, outs = refs[:n_arr], refs[n_arr:2 * n_arr]
        send_sems, recv_sems, local_sems = refs[2 * n_arr:]
        x, y, c = lax.axis_index("x"), lax.axis_index("y"), lax.axis_index("c")
        sibling, x_nbr, y_nbr, diagonal = (x, y, 1 - c), (1 - x, y, c), (x, 1 - y, c), (1 - x, 1 - y, c)
        north = c == 1
        relay_slot = jnp.where(north, 1, 2)
        relay_from = tuple(jnp.where(north, a, b) for a, b in zip(x_nbr, y_nbr))
        relay_to = tuple(jnp.where(north, a, b) for a, b in zip(y_nbr, x_nbr))
        other_from = relay_to

        def slot(a, px, py, pc):
            return _window(outs[a], axes[a], 4 * px + 2 * py + pc)

        def copy(a, k, block, to, src=None):
            return pltpu.make_async_remote_copy(
                src_ref=slot(a, *block) if src is None else src, dst_ref=slot(a, *block),
                send_sem=send_sems.at[k, a], recv_sem=recv_sems.at[k, a], device_id=to, device_id_type=MESH_ID)

        mine = [pltpu.make_async_copy(ins[a], slot(a, x, y, c), local_sems.at[a]) for a in arrays]
        for cp in mine:
            cp.start()
        first = [copy(a, k, (x, y, c), to, src=ins[a])
                 for k, to in enumerate((sibling, x_nbr, y_nbr)) for a in arrays]
        for cp in first:
            cp.start()
        later = []
        for a in arrays:
            copy(a, relay_slot, relay_from, (x, y, c)).wait_recv()
            later += [copy(a, 3, relay_from, relay_to), copy(a, 3 + relay_slot, relay_from, sibling)]
            later[-2].start()
            later[-1].start()
        for a in arrays:
            copy(a, 3 - relay_slot, other_from, (x, y, c)).wait_recv()
            later.append(copy(a, 6 - relay_slot, other_from, sibling))
            later[-1].start()
        for a in arrays:
            copy(a, 3, diagonal, (x, y, c)).wait_recv()
            later.append(copy(a, 6, diagonal, sibling))
            later[-1].start()
        for a in arrays:
            for k, block in ((0, sibling), (4, (1 - x, y, 1 - c)), (5, (x, 1 - y, 1 - c)), (6, (1 - x, 1 - y, 1 - c))):
                copy(a, k, block, (x, y, c)).wait_recv()
        for cp in first + later:
            cp.wait_send()
        for cp in mine:
            cp.wait()

    return pl.pallas_call(
        body, name=name, out_shape=_gathered_shapes(blocks, axes),
        in_specs=_any_specs(n_arr), out_specs=_any_specs(n_arr), scratch_shapes=_gather_sems(n_arr),
    )(*blocks)


def _gathered_shapes(blocks, axes):
    return [jax.ShapeDtypeStruct(b.shape[:ax] + (N_DEV,) + b.shape[ax:], b.dtype) for b, ax in zip(blocks, axes)]


def _gather_sems(n_arr):
    return [pltpu.SemaphoreType.DMA((7, n_arr)), pltpu.SemaphoreType.DMA((7, n_arr)), pltpu.SemaphoreType.DMA((n_arr,))]


def _gather_steps(ins, outs, axes, send_sems, recv_sems, local_sems):
    arrays = range(len(ins))
    x, y, c = lax.axis_index("x"), lax.axis_index("y"), lax.axis_index("c")
    me, sibling = (x, y, c), (x, y, 1 - c)
    chips = [(1 - x, y), (x, 1 - y), (1 - x, 1 - y)]

    def slot(a, px, py, pc):
        return _window(outs[a], axes[a], 4 * px + 2 * py + pc)

    def copy(a, k, block, to, src=None):
        return pltpu.make_async_remote_copy(
            src_ref=slot(a, *block) if src is None else src, dst_ref=slot(a, *block),
            send_sem=send_sems.at[k, a], recv_sem=recv_sems.at[k, a], device_id=to, device_id_type=MESH_ID)

    def mine(a):
        return pltpu.make_async_copy(ins[a], slot(a, *me), local_sems.at[a])

    def first():
        return ([copy(a, 0, me, sibling, src=ins[a]) for a in arrays]
                + [copy(a, 1 + j, me, (*chip, c), src=ins[a]) for j, chip in enumerate(chips) for a in arrays])

    def passed(j):
        return [copy(a, 4 + j, (*chips[j], c), sibling) for a in arrays]

    def start():
        for a in arrays:
            mine(a).start()
        for cp in first():
            cp.start()

    def forward(j):
        for a, cp in zip(arrays, passed(j)):
            copy(a, 1 + j, (*chips[j], c), me).wait_recv()
            cp.start()

    def finish():
        for a in arrays:
            copy(a, 0, sibling, me).wait_recv()
        for j, chip in enumerate(chips):
            for a in arrays:
                copy(a, 4 + j, (*chip, 1 - c), me).wait_recv()
        for cp in first() + passed(0) + passed(1) + passed(2):
            cp.wait_send()
        for a in arrays:
            mine(a).wait()

    return start, forward, finish


def exchange_sibling(grads, name):
    n_arr = len(grads)

    def body(*refs):
        start, finish = _sibling_exchange_steps(refs[:n_arr], refs[n_arr:2 * n_arr], *refs[2 * n_arr:])
        start()
        finish()

    return pl.pallas_call(
        body, name=name, out_shape=_sibling_shapes(grads),
        in_specs=_any_specs(n_arr), out_specs=_any_specs(n_arr), scratch_shapes=_exchange_sems(n_arr),
    )(*grads)


def _sibling_shapes(grads):
    return [jax.ShapeDtypeStruct((4,) + g.shape[1:], g.dtype) for g in grads]


def _exchange_sems(n_arr):
    return [pltpu.SemaphoreType.DMA((n_arr,)), pltpu.SemaphoreType.DMA((n_arr,))]


def _sibling_exchange_steps(ins, outs, send_sems, recv_sems):
    x, y, c = lax.axis_index("x"), lax.axis_index("y"), lax.axis_index("c")

    def copy(a, src, dst):
        return pltpu.make_async_remote_copy(
            src_ref=src, dst_ref=dst, send_sem=send_sems.at[a], recv_sem=recv_sems.at[a],
            device_id=(x, y, 1 - c), device_id_type=MESH_ID)

    def start():
        for a in range(len(ins)):
            for k in range(4):
                copy(a, ins[a].at[2 * k + (1 - c)], outs[a].at[k]).start()

    def finish():
        whole = [copy(a, ins[a].at[pl.ds(0, 4)], outs[a]) for a in range(len(ins))]
        for cp in whole:
            cp.wait_recv()
        for cp in whole:
            cp.wait_send()

    return start, finish


def _chip_exchange_steps(ins, outs, send_sems, recv_sems):
    x, y, c = lax.axis_index("x"), lax.axis_index("y"), lax.axis_index("c")
    chips = [(1 - x, y), (x, 1 - y), (1 - x, 1 - y)]

    def copy(a, src, dst, chip):
        return pltpu.make_async_remote_copy(
            src_ref=src, dst_ref=dst, send_sem=send_sems.at[a], recv_sem=recv_sems.at[a],
            device_id=(*chip, c), device_id_type=MESH_ID)

    def start():
        for a in range(len(ins)):
            for j, chip in enumerate(chips):
                copy(a, ins[a].at[j], outs[a].at[j], chip).start()

    def finish():
        whole = [copy(a, ins[a], outs[a], chips[0]) for a in range(len(ins))]
        for cp in whole:
            cp.wait_recv()
        for cp in whole:
            cp.wait_send()

    return start, finish


def riding_gather(blocks, axes):
    def phases(ins, outs, *sems):
        start, forward, finish = _gather_steps(ins, outs, axes, *sems)
        return [start] + [functools.partial(forward, j) for j in range(3)] + [finish]

    return dict(operands=blocks, out_shape=_gathered_shapes(blocks, axes), sems=_gather_sems(len(blocks)),
                phases=phases, when=("first", "late0", "late1", "late2", "last"))


def riding_exchange(parts):
    def phases(ins, outs, *sems):
        return list(_chip_exchange_steps(ins, outs, *sems))

    return dict(operands=parts, out_shape=[jax.ShapeDtypeStruct(p.shape, p.dtype) for p in parts],
                sems=_exchange_sems(len(parts)), phases=phases, when=("first", "last"))


def riding_sibling(grads):
    def phases(ins, outs, *sems):
        return list(_sibling_exchange_steps(ins, outs, *sems))

    return dict(operands=grads, out_shape=_sibling_shapes(grads), sems=_exchange_sems(len(grads)),
                phases=phases, when=("first", "last"))


def _call_with_rider(body, rider, *, name, grid, in_specs, out_specs, out_shape, scratch_shapes, operands):
    params = _params(("arbitrary",) * len(grid))
    if rider is None:
        return pl.pallas_call(body, name=name, grid=grid, in_specs=in_specs, out_specs=out_specs,
                              out_shape=out_shape, scratch_shapes=scratch_shapes, compiler_params=params)(*operands)
    n_in, n_out, n_scr, k = len(in_specs), len(out_specs), len(scratch_shapes), len(rider["operands"])
    at = {"first": (0,) * len(grid), "last": tuple(g - 1 for g in grid)}
    if "late0" in rider["when"]:
        rows, cols = grid
        late = max(rows * cols - cols - 4, 0)
        assert late + 2 < rows * cols - 1
        at.update({"late%d" % j: ((late + j) // cols, (late + j) % cols) for j in range(3)})

    def wrapped(*refs):
        ins, c_in = refs[:n_in], refs[n_in:n_in + k]
        outs, c_out = refs[n_in + k:n_in + k + n_out], refs[n_in + k + n_out:n_in + 2 * k + n_out]
        scratch, sems = refs[n_in + 2 * k + n_out:n_in + 2 * k + n_out + n_scr], refs[n_in + 2 * k + n_out + n_scr:]
        pos = [pl.program_id(axis) for axis in range(len(grid))]

        def here(key):
            return functools.reduce(jnp.logical_and, [p == v for p, v in zip(pos, at[key])])

        phases = rider["phases"](c_in, c_out, *sems)
        for fn, key in zip(phases, rider["when"]):
            if key != "last":
                pl.when(here(key))(fn)
        body(*ins, *outs, *scratch)
        pl.when(here("last"))(phases[-1])

    return pl.pallas_call(
        wrapped, name=name, grid=grid,
        in_specs=list(in_specs) + _any_specs(k), out_specs=list(out_specs) + _any_specs(k),
        out_shape=list(out_shape) + rider["out_shape"], scratch_shapes=list(scratch_shapes) + rider["sems"],
        compiler_params=params)(*operands, *rider["operands"])


def add_sibling(g8, got, src_idx, chip_idx, name):
    _, r, n = g8.shape
    tr = _tile(r, SUM_ROWS, 16)

    def body(si_ref, ci_ref, g0_ref, g1_ref, g2_ref, g3_ref, got_ref, own_ref, send_ref):
        own_ref[...] = g0_ref[0] + got_ref[ci_ref[0]]
        for j, g_ref in enumerate((g1_ref, g2_ref, g3_ref)):
            send_ref[j] = (g_ref[0] + got_ref[ci_ref[j + 1]]).astype(BF16)

    def mine(j):
        return pl.BlockSpec((1, tr, n), lambda i, si, ci: (si[j], i, 0))

    return pl.pallas_call(
        body, name=name,
        out_shape=[jax.ShapeDtypeStruct((r, n), F32), jax.ShapeDtypeStruct((3, r, n), BF16)],
        grid_spec=pltpu.PrefetchScalarGridSpec(
            num_scalar_prefetch=2, grid=(r // tr,),
            in_specs=[mine(0), mine(1), mine(2), mine(3), pl.BlockSpec((4, tr, n), lambda i, si, ci: (0, i, 0))],
            out_specs=[pl.BlockSpec((tr, n), lambda i, si, ci: (i, 0)),
                       pl.BlockSpec((3, tr, n), lambda i, si, ci: (0, i, 0))]),
        compiler_params=_params(("arbitrary",)),
    )(src_idx, chip_idx, g8, g8, g8, g8, got)


def sum_devices(g):
    def body(g_ref, o_ref):
        acc = g_ref[0]
        for j in range(1, N_DEV):
            acc = acc + g_ref[j]
        o_ref[...] = acc

    return pl.pallas_call(body, name="sum_devices", out_shape=jax.ShapeDtypeStruct(g.shape[1:], F32))(g)


def sum_lanes(v):
    def body(v_ref, o_ref):
        o_ref[...] = jnp.broadcast_to(jnp.sum(v_ref[...], axis=-1, keepdims=True), (1, LANES))

    return pl.pallas_call(body, name="sum_lanes", out_shape=jax.ShapeDtypeStruct((1, LANES), F32))(v)


def ada_forward(c_all, ada_w, ada_b_cols):
    nb, n = c_all.shape[0], ada_w.shape[1]

    def body(c_ref, w_ref, b_ref, o_ref):
        cv = c_ref[...]
        s = (cv * jax.nn.sigmoid(cv)).astype(BF16)
        o_ref[...] = _dot(s, w_ref[...].astype(BF16)) + b_ref[...]

    return pl.pallas_call(body, name="ada_fwd", out_shape=jax.ShapeDtypeStruct((nb, n), F32),
                          compiler_params=_params())(c_all, ada_w, ada_b_cols)


def ada_backward(c_all16, dmod16):
    d, n = c_all16.shape[1], dmod16.shape[1]

    def body(c_ref, g_ref, o_ref):
        cv = c_ref[...]
        s = (cv * jax.nn.sigmoid(cv)).astype(BF16)
        o_ref[...] = _dot(s, g_ref[...].astype(BF16), TN)

    return pl.pallas_call(body, name="ada_bwd", out_shape=jax.ShapeDtypeStruct((d, n), F32),
                          compiler_params=_params())(c_all16, dmod16)


def ffn_forward(x, gn, sc, sh, gate, ws, name, rider=None, loss_head=None):
    t, d = x.shape
    f = ws[0].shape[0]
    tm, tf = _tile(t, FFN_FWD_TILE[0], 16), _tile(f, FFN_FWD_TILE[1])
    nf = f // tf
    n_in = 5 if loss_head is None else 7

    def body(*refs):
        x_ref, gn_ref, sc_ref, sh_ref, gate_ref = refs[:5]
        w1_ref, w3_ref, w2_ref, xo_ref, h_ref, a_ref, b_ref, y_ref = refs[n_in:n_in + 8]
        hs, acc = refs[-2:]
        i, j = pl.program_id(0), pl.program_id(1)

        if loss_head is not None:
            @pl.when(jnp.logical_and(i == 0, j == 0))
            def _():
                refs[n_in + 9][...] = jnp.zeros_like(refs[n_in + 9])

        @pl.when(j == 0)
        def _():
            xhat, _ = _rms(x_ref[...])
            h = (xhat * gn_ref[...] * (1.0 + sc_ref[...]) + sh_ref[...]).astype(BF16)
            hs[...] = h
            h_ref[...] = h
            acc[...] = jnp.zeros_like(acc)

        h = hs[...]
        a = _dot(h, w1_ref[...], NT)
        b = _dot(h, w3_ref[...], NT)
        a_ref[...] = a.astype(BF16)
        b_ref[...] = b.astype(BF16)
        u = (a * _sigmoid(a) * b).astype(BF16)
        acc[...] += _dot(u, w2_ref[...])

        @pl.when(j == nf - 1)
        def _():
            y = acc[...]
            y_ref[...] = y.astype(BF16)
            x_out = x_ref[...] + 0.5 * gate_ref[...] * y
            if loss_head is None:
                xo_ref[...] = x_out
            else:
                dx, d_g, loss = _loss_head(x_out, refs[5][...], refs[6][...])
                xo_ref[...] = dx
                refs[n_in + 8][...] = (0.5 * gate_ref[...] * dx).astype(BF16)
                _add_rows(refs[n_in + 9], [d_g, loss])

    row = pl.BlockSpec((tm, d), lambda i, j: (i, 0))
    vec = pl.BlockSpec((1, d), lambda i, j: (0, 0))
    wide = pl.BlockSpec((tm, tf), lambda i, j: (i, j))
    w_spec = pl.BlockSpec((tf, d), lambda i, j: (j, 0), **({"pipeline_mode": pl.Buffered(1)} if nf == 1 else {}))
    head = loss_head is not None
    return _call_with_rider(
        body, rider, name=name, grid=(t // tm, nf),
        in_specs=[row, vec, vec, vec, vec] + ([row, vec] if head else []) + [w_spec] * 3,
        out_specs=[row, row, wide, wide, row] + ([row, pl.BlockSpec((8, d), lambda i, j: (0, 0))] if head else []),
        out_shape=[jax.ShapeDtypeStruct((t, d), F32), jax.ShapeDtypeStruct((t, d), BF16),
                   jax.ShapeDtypeStruct((t, f), BF16), jax.ShapeDtypeStruct((t, f), BF16),
                   jax.ShapeDtypeStruct((t, d), BF16)]
        + ([jax.ShapeDtypeStruct((t, d), BF16), jax.ShapeDtypeStruct((8, d), F32)] if head else []),
        scratch_shapes=[pltpu.VMEM((tm, d), BF16), pltpu.VMEM((tm, d), F32)],
        operands=(x, gn, sc, sh, gate) + (tuple(loss_head) if head else ()) + tuple(ws))


def _loss_head(x, target, g):
    d = x.shape[-1]
    xhat, r = _rms(x)
    err = xhat * g - target
    dyf = err * (1.0 / d)
    dxh = dyf * g
    dx = r * (dxh - xhat * jnp.mean(dxh * xhat, axis=-1, keepdims=True))
    return dx, jnp.sum(dyf * xhat, axis=0, keepdims=True), jnp.sum(err * err, axis=0, keepdims=True) * (0.5 / d)


def ffn_backward_gate(dy, a, b, w2, name, rider=None):
    t, d = dy.shape
    f = w2.shape[0]
    tm, tf = _tile(t, FFN_BWD_TILE[0], 16), _tile(f, FFN_BWD_TILE[1])
    nf = f // tf

    def gate_body(dy_ref, a_ref, b_ref, w2_ref, da_ref, db_ref, gw2_ref):
        dy_v = dy_ref[...]
        du = _dot(dy_v, w2_ref[...], NT)
        av = a_ref[...].astype(F32)
        bv = b_ref[...].astype(F32)
        s = _sigmoid(av)
        sa = av * s
        da_ref[...] = (du * bv * (s + sa * (1.0 - s))).astype(BF16)
        db_ref[...] = (du * sa).astype(BF16)
        part = _dot((sa * bv).astype(BF16), dy_v, TN)

        @pl.when(pl.program_id(1) == 0)
        def _():
            gw2_ref[...] = part

        @pl.when(pl.program_id(1) > 0)
        def _():
            gw2_ref[...] += part

    hidden = jax.ShapeDtypeStruct((t, f), BF16)
    wide_t = pl.BlockSpec((tm, tf), lambda j, i: (i, j))
    return _call_with_rider(
        gate_body, rider, name=name, grid=(nf, t // tm),
        in_specs=[pl.BlockSpec((tm, d), lambda j, i: (i, 0)), wide_t, wide_t,
                  pl.BlockSpec((tf, d), lambda j, i: (j, 0))],
        out_specs=[wide_t, wide_t, pl.BlockSpec((tf, d), lambda j, i: (j, 0))],
        out_shape=[hidden, hidden, jax.ShapeDtypeStruct((f, d), F32)],
        scratch_shapes=[], operands=(dy, a, b, w2))


def ffn_backward_norm(da, db, dxo, x, y, gn, sc, w1t, w3t, name, rider=None):
    t, d = x.shape
    f = w1t.shape[0]
    tm, tf = _tile(t, FFN_NORM_TILE[0], 16), _tile(f, FFN_NORM_TILE[1])
    nf = f // tf
    row = pl.BlockSpec((tm, d), lambda i, j: (i, 0))
    vec = pl.BlockSpec((1, d), lambda i, j: (0, 0))
    wide = pl.BlockSpec((tm, tf), lambda i, j: (i, j))

    def norm_body(da_ref, db_ref, w1_ref, w3_ref, dxo_ref, x_ref, y_ref, gn_ref, sc_ref, dx_ref, sums_ref, acc):
        i, j = pl.program_id(0), pl.program_id(1)

        @pl.when(jnp.logical_and(i == 0, j == 0))
        def _():
            sums_ref[...] = jnp.zeros_like(sums_ref)

        part = _dot(da_ref[...], w1_ref[...]) + _dot(db_ref[...], w3_ref[...])

        @pl.when(j == 0)
        def _():
            acc[...] = part

        @pl.when(jnp.logical_and(j > 0, j < nf - 1))
        def _():
            acc[...] += part

        @pl.when(j == nf - 1)
        def _():
            dh = part if nf == 1 else acc[...] + part
            dxo_v = dxo_ref[...]
            dx, d_sh, d_sc, d_gn = _norm_mod_bwd(dh, x_ref[...], gn_ref[...], sc_ref[...])
            dx_ref[...] = dxo_v + dx
            d_gate = jnp.sum(dxo_v * (0.5 * y_ref[...].astype(F32)), axis=0, keepdims=True)
            _add_rows(sums_ref, [d_sh, d_sc, d_gate, d_gn])

    w_spec = pl.BlockSpec((tf, d), lambda i, j: (j, 0))
    return _call_with_rider(
        norm_body, rider, name=name, grid=(t // tm, nf),
        in_specs=[wide, wide, w_spec, w_spec, row, row, row, vec, vec],
        out_specs=[row, pl.BlockSpec((8, d), lambda i, j: (0, 0))],
        out_shape=[jax.ShapeDtypeStruct((t, d), F32), jax.ShapeDtypeStruct((8, d), F32)],
        scratch_shapes=[pltpu.VMEM((tm, d), F32)],
        operands=(da, db, w1t, w3t, dxo, x, y, gn, sc))


def matmul_tn(a, b, name, rider=None):
    t, m = a.shape
    n = b.shape[1]
    tm, tn, tk = _tile(m, GRAD_TILE), _tile(n, GRAD_TILE), _tile(t, GRAD_DEPTH, 16)
    nk = t // tk

    def body(a_ref, b_ref, o_ref, acc):
        k = pl.program_id(2)

        @pl.when(k == 0)
        def _():
            acc[...] = jnp.zeros_like(acc)

        acc[...] += _dot(a_ref[...], b_ref[...], TN)

        @pl.when(k == nk - 1)
        def _():
            o_ref[...] = acc[...]

    out = _call_with_rider(
        body, rider, name=name, grid=(m // tm, n // tn, nk),
        in_specs=[pl.BlockSpec((tk, tm), lambda i, j, k: (k, i)), pl.BlockSpec((tk, tn), lambda i, j, k: (k, j))],
        out_specs=[pl.BlockSpec((tm, tn), lambda i, j, k: (i, j))],
        out_shape=[jax.ShapeDtypeStruct((m, n), F32)],
        scratch_shapes=[pltpu.VMEM((tm, tn), F32)], operands=(a, b))
    return out[0] if rider is None else out


def mix_in_forward(x, gn, sc, sh, w_in):
    t, d = x.shape
    tm = _tile(t, ROW_TILE, 16)

    def body(x_ref, gn_ref, sc_ref, sh_ref, w_ref, h_ref, zc_ref, zm_ref):
        xhat, _ = _rms(x_ref[...])
        h = (xhat * gn_ref[...] * (1.0 + sc_ref[...]) + sh_ref[...]).astype(BF16)
        h_ref[...] = h
        z = _dot(h, w_ref[...], NT)
        zc_ref[...] = z[:, :ZC_COLS].astype(BF16)
        zm_ref[...] = z[:, ZC_COLS:].astype(BF16)

    row = pl.BlockSpec((tm, d), lambda i: (i, 0))
    vec = pl.BlockSpec((1, d), lambda i: (0, 0))
    return pl.pallas_call(
        body, name="mix_in_fwd", grid=(t // tm,),
        in_specs=[row, vec, vec, vec, _row(w_in)],
        out_specs=[row, pl.BlockSpec((tm, ZC_COLS), lambda i: (i, 0)), pl.BlockSpec((tm, ZM_COLS), lambda i: (i, 0))],
        out_shape=[jax.ShapeDtypeStruct((t, d), BF16), jax.ShapeDtypeStruct((t, ZC_COLS), BF16),
                   jax.ShapeDtypeStruct((t, ZM_COLS), BF16)],
        compiler_params=_params(("arbitrary",)),
    )(x, gn, sc, sh, w_in)


def _rope_tables(pos, inv_freq):
    ang = pos * inv_freq
    lane = lax.broadcasted_iota(jnp.int32, ang.shape, 1)
    cos, sin = jnp.cos(ang), jnp.sin(ang)
    half = QK_ROPE // 2
    return cos, jnp.where(lane < half, -sin, 0.0), jnp.where(jnp.logical_and(lane >= half, lane < QK_ROPE), sin, 0.0)


def _rope(v, tables):
    cos, sin_a, sin_b = tables
    return v * cos + pltpu.roll(v, LANES - QK_ROPE // 2, 1) * sin_a + pltpu.roll(v, QK_ROPE // 2, 1) * sin_b


def _rope_transposed(dv, tables):
    cos, sin_a, sin_b = tables
    return dv * cos + pltpu.roll(dv * sin_a, QK_ROPE // 2, 1) + pltpu.roll(dv * sin_b, LANES - QK_ROPE // 2, 1)


def mla_project(zm, pos, inv_freq, qg, kvg, w_uq, w_ukv):
    t = zm.shape[0]
    tm = _tile(t, ROW_TILE, 16)

    def body(zm_ref, pos_ref, if_ref, qg_ref, kvg_ref, wq_ref, wkv_ref, qn_ref, kvn_ref, q_ref, k_ref, v_ref):
        zv = zm_ref[...].astype(F32)
        qn = (_rms(zv[:, :Q_LORA])[0] * qg_ref[...]).astype(BF16)
        kvn = (_rms(zv[:, Q_LORA:Q_LORA + KV_LORA])[0] * kvg_ref[...]).astype(BF16)
        qn_ref[...] = qn
        kvn_ref[...] = kvn
        qf = _dot(qn, wq_ref[...], NT) * QK_FOLD
        kvf = _dot(kvn, wkv_ref[...], NT)
        tables = _rope_tables(pos_ref[...], if_ref[...])
        kr = _rope(zv[:, Q_LORA + KV_LORA:], tables).astype(BF16)
        for h in range(MLA_HEADS):
            lo = h * HEAD_PAD
            q_ref[:, lo:lo + QK_NOPE] = qf[:, lo:lo + QK_NOPE].astype(BF16)
            q_ref[:, lo + QK_NOPE:lo + HEAD_PAD] = _rope(qf[:, lo + QK_NOPE:lo + HEAD_PAD], tables).astype(BF16)
            k_ref[:, lo:lo + QK_NOPE] = kvf[:, h * QK_NOPE:(h + 1) * QK_NOPE].astype(BF16)
            k_ref[:, lo + QK_NOPE:lo + HEAD_PAD] = kr
        v_ref[...] = kvf[:, MLA_HEADS * QK_NOPE:].astype(BF16)

    def rows(n):
        return pl.BlockSpec((tm, n), lambda i: (i, 0))

    return pl.pallas_call(
        body, name="mla_project", grid=(t // tm,),
        in_specs=[rows(ZM_COLS), rows(1), _row(inv_freq), _row(qg), _row(kvg), _row(w_uq), _row(w_ukv)],
        out_specs=[rows(Q_LORA), rows(KV_LORA), rows(QK_COLS), rows(QK_COLS), rows(MLA_WIDTH)],
        out_shape=[jax.ShapeDtypeStruct((t, Q_LORA), BF16), jax.ShapeDtypeStruct((t, KV_LORA), BF16),
                   jax.ShapeDtypeStruct((t, QK_COLS), BF16), jax.ShapeDtypeStruct((t, QK_COLS), BF16),
                   jax.ShapeDtypeStruct((t, MLA_WIDTH), BF16)],
        compiler_params=_params(("arbitrary",)),
    )(zm, pos, inv_freq, qg, kvg, w_uq, w_ukv)


def chunk_bias(tile):
    chunk = jnp.arange(tile) // CHUNK
    return jnp.where(chunk[None, :] <= chunk[:, None], 0.0, NEG_INF).astype(F32)


def attention_forward(q, k, v, bias, rider=None):
    t = q.shape[0]
    tq = _tile(t, ATTN_TILE, CHUNK)

    def body(q_ref, k_ref, v_ref, bias_ref, o_ref, lse_ref):
        i = pl.program_id(1)
        qv = q_ref[...]

        def step(kb, carry, masked, tiles=1):
            m, l, acc = carry
            keys = pl.ds(pl.multiple_of(kb * tq, tq), tiles * tq)
            s = _dot(qv, k_ref[keys, :], NT)
            if masked:
                s = s + bias_ref[...]
            m_new = jnp.maximum(m, jnp.max(s, axis=-1, keepdims=True))
            alpha = jnp.exp2(m - m_new)
            p = jnp.exp2(s - m_new)
            l = alpha * l + jnp.sum(p, axis=-1, keepdims=True)
            acc = alpha * acc + _dot(p.astype(BF16), v_ref[keys, :])
            return m_new, l, acc

        init = (jnp.full((tq, 1), NEG_INF, F32), jnp.zeros((tq, 1), F32), jnp.zeros((tq, V_HEAD), F32))
        carry = lax.fori_loop(0, i // 2, lambda pb, cr: step(2 * pb, cr, False, 2), init)
        carry = lax.fori_loop(0, i % 2, lambda _, cr: step(i - 1, cr, False), carry)
        m, l, acc = step(i, carry, True)
        o_ref[...] = (acc / l).astype(BF16)
        lse_ref[0] = m + jnp.log2(l)

    return _call_with_rider(
        body, rider, name="attn_fwd", grid=(MLA_HEADS, t // tq),
        in_specs=[pl.BlockSpec((tq, HEAD_PAD), lambda h, i: (i, h)),
                  pl.BlockSpec((t, HEAD_PAD), lambda h, i: (0, h)),
                  pl.BlockSpec((t, V_HEAD), lambda h, i: (0, h)),
                  pl.BlockSpec((tq, tq), lambda h, i: (0, 0), pipeline_mode=pl.Buffered(1))],
        out_specs=[pl.BlockSpec((tq, V_HEAD), lambda h, i: (i, h)),
                   pl.BlockSpec((1, tq, 1), lambda h, i: (h, i, 0))],
        out_shape=[jax.ShapeDtypeStruct((t, MLA_WIDTH), BF16), jax.ShapeDtypeStruct((MLA_HEADS, t, 1), F32)],
        scratch_shapes=[], operands=(q, k, v, bias))


def attention_backward(q, k, v, do, lse, delta, bias_t, rider=None):
    t = q.shape[0]
    tq = _tile(t, ATTN_TILE, CHUNK)
    nq = t // tq

    def body(q_ref, k_ref, v_ref, do_ref, lse_ref, delta_ref, bias_ref, dq_ref, dk_ref, dv_ref, dq_acc):
        kb = pl.program_id(1)

        @pl.when(kb == 0)
        def _():
            dq_acc[...] = jnp.zeros_like(dq_acc)

        kv, vv = k_ref[...], v_ref[...]

        def step(qb, carry, masked):
            dk, dv = carry
            rows = pl.ds(pl.multiple_of(qb * tq, tq), tq)
            qv, dov = q_ref[rows, :], do_ref[rows, :]
            s = _dot(kv, qv, NT)
            if masked:
                s = s + bias_ref[...]
            p = jnp.exp2(s - lse_ref[0, qb])
            dv = dv + _dot(p.astype(BF16), dov)
            dp = _dot(vv, dov, NT)
            ds = (p * (dp - delta_ref[0, qb]) * LN_2).astype(BF16)
            dk = dk + _dot(ds, qv)
            dq_acc[rows, :] += _dot(ds, kv, TN)
            return dk, dv

        carry = step(kb, (jnp.zeros((tq, HEAD_PAD), F32), jnp.zeros((tq, V_HEAD), F32)), True)
        odd = (nq - 1 - kb) % 2
        carry = lax.fori_loop(0, odd, lambda _, cr: step(kb + 1, cr, False), carry)
        first = kb + 1 + odd
        dk, dv = lax.fori_loop(0, (nq - first) // 2,
                               lambda pb, cr: step(first + 2 * pb + 1, step(first + 2 * pb, cr, False), False), carry)
        dk_ref[...] = dk.astype(BF16)
        dv_ref[...] = dv.astype(BF16)

        @pl.when(kb == nq - 1)
        def _():
            dq_ref[...] = dq_acc[...].astype(BF16)

    stat = pl.BlockSpec((1, nq, 1, tq), lambda h, j: (h, 0, 0, 0))
    return _call_with_rider(
        body, rider, name="attn_bwd", grid=(MLA_HEADS, nq),
        in_specs=[pl.BlockSpec((t, HEAD_PAD), lambda h, j: (0, h)),
                  pl.BlockSpec((tq, HEAD_PAD), lambda h, j: (j, h)),
                  pl.BlockSpec((tq, V_HEAD), lambda h, j: (j, h)),
                  pl.BlockSpec((t, V_HEAD), lambda h, j: (0, h)), stat, stat,
                  pl.BlockSpec((tq, tq), lambda h, j: (0, 0), pipeline_mode=pl.Buffered(1))],
        out_specs=[pl.BlockSpec((t, HEAD_PAD), lambda h, j: (0, h)),
                   pl.BlockSpec((tq, HEAD_PAD), lambda h, j: (j, h)),
                   pl.BlockSpec((tq, V_HEAD), lambda h, j: (j, h))],
        out_shape=[jax.ShapeDtypeStruct((t, QK_COLS), BF16), jax.ShapeDtypeStruct((t, QK_COLS), BF16),
                   jax.ShapeDtypeStruct((t, MLA_WIDTH), BF16)],
        scratch_shapes=[pltpu.VMEM((t, HEAD_PAD), F32)], operands=(q, k, v, do, lse, delta, bias_t))


HALO = 16


def _halo_spec(tm, n, step, last):
    return pl.BlockSpec((HALO, n), lambda i: (jnp.clip(i * (tm // HALO) + step, 0, last), 0))


def _shift_rows(v, prev, n):
    out = pltpu.roll(v, n, 0)
    row = lax.broadcasted_iota(jnp.int32, v.shape, 0)
    for r in range(n):
        out = jnp.where(row == r, prev[HALO - n + r:HALO - n + r + 1, :], out)
    return out


def _advance_rows(v, nxt, n):
    rows = v.shape[0]
    out = pltpu.roll(v, rows - n, 0)
    row = lax.broadcasted_iota(jnp.int32, v.shape, 0)
    for r in range(n):
        out = jnp.where(row == rows - n + r, nxt[r:r + 1, :], out)
    return out


def _conv_taps(zc, zc_prev, first):
    w = CONV_WIDTH
    u = zc[:, w:2 * w] * zc[:, 2 * w:]
    up = jnp.where(first, 0.0, zc_prev[:, w:2 * w] * zc_prev[:, 2 * w:])
    return u, _shift_rows(u, up, 1), _shift_rows(u, up, 2)


def mix_out_forward(zc, o, conv_w, og, gmat_a, gmat_b, w_out, x, gate):
    t, d = x.shape
    tm = _tile(t, ROW_TILE, 16)
    w = CONV_WIDTH

    def body(zc_ref, zp_ref, o_ref, cw_ref, og_ref, ga_ref, gb_ref, w_ref, x_ref, gate_ref,
             xo_ref, yn_ref, y_ref, ya_ref):
        zc_v = zc_ref[...].astype(F32)
        u, u1, u2 = _conv_taps(zc_v, zp_ref[...].astype(F32), pl.program_id(0) == 0)
        cw = cw_ref[...]
        ya = zc_v[:, :w] * (cw[0:1] * u2 + cw[1:2] * u1 + cw[2:3] * u)
        ya_ref[...] = ya.astype(BF16)
        ov = o_ref[...].astype(F32)
        ogv = og_ref[...]
        yn_ref[:, :w] = (ya * lax.rsqrt(_group_mean(ya * ya, ga_ref[...]) + EPS) * ogv[:, :w]).astype(BF16)
        yn_ref[:, w:] = (ov * lax.rsqrt(_group_mean(ov * ov, gb_ref[...]) + EPS) * ogv[:, w:]).astype(BF16)
        y = _dot(yn_ref[...], w_ref[...])
        y_ref[...] = y.astype(BF16)
        xo_ref[...] = x_ref[...] + gate_ref[...] * y

    def rows(n):
        return pl.BlockSpec((tm, n), lambda i: (i, 0))

    return pl.pallas_call(
        body, name="mix_out_fwd", grid=(t // tm,),
        in_specs=[rows(ZC_COLS), _halo_spec(tm, ZC_COLS, -1, t // HALO - 1), rows(MLA_WIDTH), _row(conv_w), _row(og),
                  _row(gmat_a), _row(gmat_b), _row(w_out), rows(d), _row(gate)],
        out_specs=[rows(d), rows(MIX_WIDTH), rows(d), rows(w)],
        out_shape=[jax.ShapeDtypeStruct((t, d), F32), jax.ShapeDtypeStruct((t, MIX_WIDTH), BF16),
                   jax.ShapeDtypeStruct((t, d), BF16), jax.ShapeDtypeStruct((t, w), BF16)],
        compiler_params=_params(("arbitrary",)),
    )(zc, zc, o, conv_w, og, gmat_a, gmat_b, w_out, x, gate)


def _group_norm_bwd(dyn, y, og, gmat):
    rs = lax.rsqrt(_group_mean(y * y, gmat) + EPS)
    yhat = y * rs
    d_og = jnp.sum(dyn * yhat, axis=0, keepdims=True)
    dyh = dyn * og
    return rs * (dyh - yhat * _group_mean(dyh * yhat, gmat)), d_og


def mix_out_backward(dxo, y, gate, ya, o, yn, og, gmat_a, gmat_b, w_out, rider=None):
    t, d = dxo.shape
    tm = _tile(t, ROW_TILE, 16)
    w = CONV_WIDTH

    def body(dxo_ref, y_ref, gate_ref, ya_ref, o_ref, yn_ref, og_ref, ga_ref, gb_ref, w_ref,
             dya_ref, do_ref, delta_ref, sd_ref, so_ref, gw_ref):
        @pl.when(pl.program_id(0) == 0)
        def _():
            sd_ref[...] = jnp.zeros_like(sd_ref)
            so_ref[...] = jnp.zeros_like(so_ref)
            gw_ref[...] = jnp.zeros_like(gw_ref)

        dxo_v = dxo_ref[...]
        dy = (gate_ref[...] * dxo_v).astype(BF16)
        gw_ref[...] += _dot(yn_ref[...], dy, TN)
        sd_ref[0:1, :] += jnp.sum(dxo_v * y_ref[...].astype(F32), axis=0, keepdims=True)
        dyn = _dot(dy, w_ref[...], NT)
        ogv = og_ref[...]
        ov = o_ref[...].astype(F32)
        dya, d_og_a = _group_norm_bwd(dyn[:, :w], ya_ref[...].astype(F32), ogv[:, :w], ga_ref[...])
        dov, d_og_b = _group_norm_bwd(dyn[:, w:], ov, ogv[:, w:], gb_ref[...])
        dya_ref[...] = dya.astype(BF16)
        do_ref[...] = dov.astype(BF16)
        so_ref[0:1, :w] += d_og_a
        so_ref[0:1, w:] += d_og_b
        prod = dov * ov
        for h in range(MLA_HEADS):
            delta_ref[h] = jnp.sum(prod[:, h * V_HEAD:(h + 1) * V_HEAD], axis=-1, keepdims=True)

    def rows(n):
        return pl.BlockSpec((tm, n), lambda i: (i, 0))

    return _call_with_rider(
        body, rider, name="mix_out_bwd", grid=(t // tm,),
        in_specs=[rows(d), rows(d), _row(gate), rows(w), rows(MLA_WIDTH), rows(MIX_WIDTH), _row(og), _row(gmat_a),
                  _row(gmat_b), _row(w_out)],
        out_specs=[rows(w), rows(MLA_WIDTH), pl.BlockSpec((MLA_HEADS, tm, 1), lambda i: (0, i, 0)),
                   pl.BlockSpec((8, d), lambda i: (0, 0)), pl.BlockSpec((8, MIX_WIDTH), lambda i: (0, 0)),
                   pl.BlockSpec((MIX_WIDTH, d), lambda i: (0, 0))],
        out_shape=[jax.ShapeDtypeStruct((t, w), BF16),
                   jax.ShapeDtypeStruct((t, MLA_WIDTH), BF16), jax.ShapeDtypeStruct((MLA_HEADS, t, 1), F32),
                   jax.ShapeDtypeStruct((8, d), F32), jax.ShapeDtypeStruct((8, MIX_WIDTH), F32),
                   jax.ShapeDtypeStruct((MIX_WIDTH, d), F32)],
        scratch_shapes=[], operands=(dxo, y, gate, ya, o, yn, og, gmat_a, gmat_b, w_out))


def conv_backward(zc, dya, conv_w, h):
    t, d = h.shape
    tm = _tile(t, ROW_TILE, 16)
    nt = t // tm
    w = CONV_WIDTH

    def body(zc_ref, zp_ref, zn_ref, dya_ref, dn_ref, cw_ref, h_ref, dzc_ref, sums_ref, gw_ref):
        i = pl.program_id(0)

        @pl.when(i == 0)
        def _():
            sums_ref[...] = jnp.zeros_like(sums_ref)
            gw_ref[...] = jnp.zeros_like(gw_ref)

        zc_v = zc_ref[...].astype(F32)
        u, u1, u2 = _conv_taps(zc_v, zp_ref[...].astype(F32), i == 0)
        cw = cw_ref[...]
        dya_v = dya_ref[...].astype(F32)
        dyc = dya_v * zc_v[:, :w]
        dyc_next = jnp.where(i == nt - 1, 0.0, dn_ref[...].astype(F32) * zn_ref[:, :w].astype(F32))
        du = cw[2:3] * dyc + cw[1:2] * _advance_rows(dyc, dyc_next, 1) + cw[0:1] * _advance_rows(dyc, dyc_next, 2)
        dzc_ref[:, :w] = (dya_v * (cw[0:1] * u2 + cw[1:2] * u1 + cw[2:3] * u)).astype(BF16)
        dzc_ref[:, w:2 * w] = (du * zc_v[:, 2 * w:]).astype(BF16)
        dzc_ref[:, 2 * w:] = (du * zc_v[:, w:2 * w]).astype(BF16)
        _add_rows(sums_ref, [jnp.sum(dyc * tap, axis=0, keepdims=True) for tap in (u2, u1, u)])
        gw_ref[...] += _dot(dzc_ref[...], h_ref[...], TN)

    def rows(n):
        return pl.BlockSpec((tm, n), lambda i: (i, 0))

    def halo(n, step):
        return _halo_spec(tm, n, step, t // HALO - 1)

    return pl.pallas_call(
        body, name="conv_bwd", grid=(nt,),
        in_specs=[rows(ZC_COLS), halo(ZC_COLS, -1), halo(ZC_COLS, tm // HALO), rows(w), halo(w, tm // HALO),
                  _row(conv_w), rows(d)],
        out_specs=[rows(ZC_COLS), pl.BlockSpec((8, w), lambda i: (0, 0)), pl.BlockSpec((ZC_COLS, d), lambda i: (0, 0))],
        out_shape=[jax.ShapeDtypeStruct((t, ZC_COLS), BF16), jax.ShapeDtypeStruct((8, w), F32),
                   jax.ShapeDtypeStruct((ZC_COLS, d), F32)],
        compiler_params=_params(("arbitrary",)),
    )(zc, zc, zc, dya, dya, conv_w, h)


def _rms_bwd(dy, x, g):
    xhat, r = _rms(x)
    d_g = jnp.sum(dy * xhat, axis=0, keepdims=True)
    dxh = dy * g
    return r * (dxh - xhat * jnp.mean(dxh * xhat, axis=-1, keepdims=True)), d_g


def mla_project_backward(dq, dk, dv, zm, qn, kvn, h, pos, inv_freq, qg, kvg, w_uq, w_ukv):
    t, d = h.shape
    tm = _tile(t, ROW_TILE, 16)

    def body(dq_ref, dk_ref, dv_ref, zm_ref, qn_ref, kvn_ref, h_ref, pos_ref, if_ref, qg_ref, kvg_ref, wq_ref,
             wkv_ref, dzm_ref, sums_ref, guq_ref, gukv_ref, gin_ref, dql_ref, dkvl_ref):
        @pl.when(pl.program_id(0) == 0)
        def _():
            sums_ref[...] = jnp.zeros_like(sums_ref)
            guq_ref[...] = jnp.zeros_like(guq_ref)
            gukv_ref[...] = jnp.zeros_like(gukv_ref)
            gin_ref[...] = jnp.zeros_like(gin_ref)

        tables = _rope_tables(pos_ref[...], if_ref[...])
        dkr = jnp.zeros((tm, LANES), F32)
        for h in range(MLA_HEADS):
            lo = h * HEAD_PAD
            dql_ref[:, lo:lo + QK_NOPE] = (dq_ref[:, lo:lo + QK_NOPE].astype(F32) * QK_FOLD).astype(BF16)
            dql_ref[:, lo + QK_NOPE:lo + HEAD_PAD] = _rope_transposed(
                dq_ref[:, lo + QK_NOPE:lo + HEAD_PAD].astype(F32) * QK_FOLD, tables).astype(BF16)
            dkvl_ref[:, h * QK_NOPE:(h + 1) * QK_NOPE] = dk_ref[:, lo:lo + QK_NOPE]
            dkr = dkr + dk_ref[:, lo + QK_NOPE:lo + HEAD_PAD].astype(F32)
        dkvl_ref[:, MLA_HEADS * QK_NOPE:] = dv_ref[...]
        zv = zm_ref[...].astype(F32)
        dqn = _dot(dql_ref[...], wq_ref[...])
        dkvn = _dot(dkvl_ref[...], wkv_ref[...])
        dcq, d_qg = _rms_bwd(dqn, zv[:, :Q_LORA], qg_ref[...])
        dckv, d_kvg = _rms_bwd(dkvn, zv[:, Q_LORA:Q_LORA + KV_LORA], kvg_ref[...])
        dzm_ref[:, :Q_LORA] = dcq.astype(BF16)
        dzm_ref[:, Q_LORA:Q_LORA + KV_LORA] = dckv.astype(BF16)
        dzm_ref[:, Q_LORA + KV_LORA:] = _rope_transposed(dkr, tables).astype(BF16)
        sums_ref[0:1, :Q_LORA] += d_qg
        sums_ref[0:1, Q_LORA:Q_LORA + KV_LORA] += d_kvg
        guq_ref[...] += _dot(dql_ref[...], qn_ref[...], TN)
        gukv_ref[...] += _dot(dkvl_ref[...], kvn_ref[...], TN)
        gin_ref[...] += _dot(dzm_ref[...], h_ref[...], TN)

    def rows(n):
        return pl.BlockSpec((tm, n), lambda i: (i, 0))

    def whole(r, n):
        return pl.BlockSpec((r, n), lambda i: (0, 0))

    return pl.pallas_call(
        body, name="mla_project_bwd", grid=(t // tm,),
        in_specs=[rows(QK_COLS), rows(QK_COLS), rows(MLA_WIDTH), rows(ZM_COLS), rows(Q_LORA), rows(KV_LORA), rows(d),
                  rows(1), _row(inv_freq), _row(qg), _row(kvg), _row(w_uq), _row(w_ukv)],
        out_specs=[rows(ZM_COLS), whole(8, ZM_COLS), whole(QK_COLS, Q_LORA), whole(QK_COLS, KV_LORA),
                   whole(ZM_COLS, d)],
        out_shape=[jax.ShapeDtypeStruct((t, ZM_COLS), BF16), jax.ShapeDtypeStruct((8, ZM_COLS), F32),
                   jax.ShapeDtypeStruct((QK_COLS, Q_LORA), F32), jax.ShapeDtypeStruct((QK_COLS, KV_LORA), F32),
                   jax.ShapeDtypeStruct((ZM_COLS, d), F32)],
        scratch_shapes=[pltpu.VMEM((tm, QK_COLS), BF16), pltpu.VMEM((tm, QK_COLS), BF16)],
        compiler_params=_params(("arbitrary",)),
    )(dq, dk, dv, zm, qn, kvn, h, pos, inv_freq, qg, kvg, w_uq, w_ukv)


def mix_in_backward(dzc, dzm, w_in, x, dxo, gn, sc, gate, rider=None):
    t, d = x.shape
    tm = _tile(t, ROW_TILE, 16)

    def body(dzc_ref, dzm_ref, w_ref, x_ref, dxo_ref, gn_ref, sc_ref, gate_ref, dx_ref, dy_ref, sums_ref):
        @pl.when(pl.program_id(0) == 0)
        def _():
            sums_ref[...] = jnp.zeros_like(sums_ref)

        dh = _dot(dzc_ref[...], w_ref[:ZC_COLS, :]) + _dot(dzm_ref[...], w_ref[ZC_COLS:, :])
        dx, d_sh, d_sc, d_gn = _norm_mod_bwd(dh, x_ref[...], gn_ref[...], sc_ref[...])
        dx = dxo_ref[...] + dx
        dx_ref[...] = dx
        dy_ref[...] = (0.5 * gate_ref[...] * dx).astype(BF16)
        _add_rows(sums_ref, [d_sh, d_sc, d_gn])

    def rows(n):
        return pl.BlockSpec((tm, n), lambda i: (i, 0))

    return _call_with_rider(
        body, rider, name="mix_in_bwd", grid=(t // tm,),
        in_specs=[rows(ZC_COLS), rows(ZM_COLS), _row(w_in), rows(d), rows(d), _row(gn), _row(sc), _row(gate)],
        out_specs=[rows(d), rows(d), pl.BlockSpec((8, d), lambda i: (0, 0))],
        out_shape=[jax.ShapeDtypeStruct((t, d), F32), jax.ShapeDtypeStruct((t, d), BF16),
                   jax.ShapeDtypeStruct((8, d), F32)],
        scratch_shapes=[], operands=(dzc, dzm, w_in, x, dxo, gn, sc, gate))


def _adamw_step(w, g, m, v):
    m_new = ADAM_B1 * m + (1.0 - ADAM_B1) * g
    v_new = ADAM_B2 * v + (1.0 - ADAM_B2) * (g * g)
    m_hat = m_new / (1.0 - ADAM_B1 ** ADAM_STEP)
    v_hat = v_new / (1.0 - ADAM_B2 ** ADAM_STEP)
    return -ADAM_LR * (m_hat / (jnp.sqrt(v_hat) + ADAM_EPS) + ADAM_WD * w), m_new, v_new


def adamw(w, g, m, v, name):
    r, n = w.shape
    tr = _tile(r, max(8, ADAM_TILE_ELEMS // n), 8)

    def body(w_ref, g_ref, m_ref, v_ref, d_ref, mo_ref, vo_ref):
        d_ref[...], mo_ref[...], vo_ref[...] = _adamw_step(w_ref[...], g_ref[...], m_ref[...], v_ref[...])

    blk = pl.BlockSpec((tr, n), lambda i: (i, 0))
    shape = jax.ShapeDtypeStruct((r, n), F32)
    return pl.pallas_call(
        body, name=name, grid=(r // tr,), in_specs=[blk] * 4, out_specs=[blk] * 3, out_shape=[shape] * 3,
        compiler_params=_params(("arbitrary",)),
    )(w, g, m, v)


def adamw_received(w, own, got, m, v, name):
    r, n = w.shape
    tr = _tile(r, SUM_ROWS, 16)

    def body(w_ref, own_ref, got_ref, m_ref, v_ref, g_ref, d_ref, mo_ref, vo_ref):
        g = own_ref[...]
        for j in range(3):
            g = g + got_ref[j].astype(F32)
        g_ref[...] = g
        d_ref[...], mo_ref[...], vo_ref[...] = _adamw_step(w_ref[...], g, m_ref[...], v_ref[...])

    blk = pl.BlockSpec((tr, n), lambda i: (i, 0))
    shape = jax.ShapeDtypeStruct((r, n), F32)
    return pl.pallas_call(
        body, name=name, grid=(r // tr,),
        in_specs=[blk, blk, pl.BlockSpec((3, tr, n), lambda i: (0, i, 0)), blk, blk],
        out_specs=[blk] * 4, out_shape=[shape] * 4, compiler_params=_params(("arbitrary",)),
    )(w, own, got, m, v)


def _pad_to(v, n):
    return jnp.pad(v, (0, n - v.shape[0]))


def _pad_heads(w, axis_len):
    n = w.shape[1]
    return jnp.pad(w.reshape(MLA_HEADS, axis_len, n), ((0, 0), (0, HEAD_PAD - axis_len), (0, 0))).reshape(-1, n)


def _swap_head_parts(w, inner, outer):
    n = w.shape[1]
    return w.reshape(outer, inner, QK_NOPE, n).transpose(1, 0, 2, 3).reshape(-1, n)


def kernel(x, c, positions, ada_w, ada_b, norm_ffn1_g, ffn1_w1, ffn1_w3, ffn1_w2, norm_mix_g, w_in, conv_w, q_norm_g, w_uq, kv_norm_g, w_ukv, out_norm_g, w_out, norm_ffn2_g, ffn2_w1, ffn2_w3, ffn2_w2, final_norm_g, loss_target, m_ada_w, m_ada_b, m_norm_ffn1_g, m_ffn1_w1, m_ffn1_w3, m_ffn1_w2, m_norm_mix_g, m_w_in, m_conv_w, m_q_norm_g, m_w_uq, m_kv_norm_g, m_w_ukv, m_out_norm_g, m_w_out, m_norm_ffn2_g, m_ffn2_w1, m_ffn2_w3, m_ffn2_w2, m_final_norm_g, v_ada_w, v_ada_b, v_norm_ffn1_g, v_ffn1_w1, v_ffn1_w3, v_ffn1_w2, v_norm_mix_g, v_w_in, v_conv_w, v_q_norm_g, v_w_uq, v_kv_norm_g, v_w_ukv, v_out_norm_g, v_w_out, v_norm_ffn2_g, v_ffn2_w1, v_ffn2_w3, v_ffn2_w2, v_final_norm_g):
    t, d = x.shape[1], x.shape[2]
    f = ffn1_w2.shape[1] * N_DEV
    me = 4 * lax.axis_index("x") + 2 * lax.axis_index("y") + lax.axis_index("c")
    my_c = lax.axis_index("c")
    my_chip = 2 * lax.axis_index("x") + lax.axis_index("y")
    xs = x[0]
    n_ada = ada_w.shape[2]
    cw_n = conv_w.shape[2]

    c_rows = jnp.broadcast_to(c, (8, d))
    conv_rows = jnp.pad(conv_w[0], ((0, 8 - CONV_K), (0, LANES - cw_n)))
    ffn1_blocks = [ffn1_w1[0].T.astype(BF16), ffn1_w3[0].T.astype(BF16), ffn1_w2[0].astype(BF16)]
    ffn2_blocks = [ffn2_w1[0].T.astype(BF16), ffn2_w3[0].T.astype(BF16), ffn2_w2[0].astype(BF16)]
    c_all, conv_all, *ffn1_all = all_gather_relayed([c_rows, conv_rows] + ffn1_blocks, [0] * 5, "gather_first")
    c_all = c_all[:, 0, :]
    conv_full8 = conv_all[:, :, :cw_n].transpose(1, 0, 2).reshape(8, CONV_WIDTH)
    ffn1_ws = [w.reshape(f, d) for w in ffn1_all]
    gather_mix = riding_gather(
        [w_in[0].T.astype(BF16), w_uq[0].T.astype(BF16), w_ukv[0].T.astype(BF16), w_out[0].astype(BF16)], [0, 0, 0, 0])

    ada_b_cols = lax.dynamic_slice_in_dim(ada_b, me * n_ada, n_ada, axis=1)
    mod_cols = ada_forward(c_all, ada_w[0], ada_b_cols)
    mod_all, = all_gather([mod_cols], [0], "gather_mod")
    mod = lax.dynamic_index_in_dim(mod_all, me, axis=1, keepdims=False).reshape(N_MOD, 1, d)
    sh1, sc1, g1, sh2, sc2, g2, sh3, sc3, g3 = [mod[i] for i in range(N_MOD)]

    gf = final_norm_g.reshape(1, d)
    x1, h1, a1, b1, y1, *gathered = ffn_forward(xs, norm_ffn1_g, sc1, sh1, g1, ffn1_ws, "ffn1_fwd", gather_mix)
    w_in_p = jnp.pad(gathered[0].reshape(IN_COLS, d), ((0, ZC_COLS + ZM_COLS - IN_COLS), (0, 0)))
    w_uq_p = _pad_heads(gathered[1].reshape(-1, Q_LORA), QK_NOPE + QK_ROPE)
    w_ukv_p = _swap_head_parts(gathered[2].reshape(-1, KV_LORA), 2, MLA_HEADS)
    w_out_f = gathered[3].reshape(MIX_WIDTH, d)
    h2, zc, zm = mix_in_forward(x1, norm_mix_g, sc2, sh2, w_in_p)
    pos = positions[0].astype(F32).reshape(t, 1)
    inv_freq = ROPE_THETA ** (-jnp.arange(0, QK_ROPE, 2, dtype=F32) / QK_ROPE)
    inv_freq = jnp.concatenate([inv_freq, inv_freq, jnp.zeros((LANES - QK_ROPE,), F32)]).reshape(1, LANES)
    qn, kvn, q, k, v = mla_project(zm, pos, inv_freq, q_norm_g, kv_norm_g, w_uq_p, w_ukv_p)
    bias = chunk_bias(_tile(t, ATTN_TILE, CHUNK))
    o, lse, *ffn2_all = attention_forward(q, k, v, bias, riding_gather(ffn2_blocks, [0] * 3))
    ffn2_ws = [w.reshape(f, d) for w in ffn2_all]
    lane = jnp.arange(CONV_WIDTH)
    gmat_a = (lane[:, None] // (CONV_WIDTH // CONV_GROUPS) == lane[None, :] // (CONV_WIDTH // CONV_GROUPS))
    gmat_a = (gmat_a / (CONV_WIDTH // CONV_GROUPS)).astype(BF16)
    gmat_b = ((lane[:, None] // V_HEAD == lane[None, :] // V_HEAD) / V_HEAD).astype(BF16)
    x2, yn, y2, ya = mix_out_forward(zc, o, conv_full8, out_norm_g, gmat_a, gmat_b, w_out_f, x1, g2)
    dx3, h3, a3, b3, y3, dy3, sums_f = ffn_forward(x2, norm_ffn2_g, sc3, sh3, g3, ffn2_ws, "ffn2_fwd",
                                                   loss_head=(loss_target[0], gf))

    chip_idx = jnp.bitwise_xor(my_chip, jnp.array([0, 2, 1, 3], jnp.int32)).astype(jnp.int32)
    src_idx = (2 * chip_idx + my_c).astype(jnp.int32)

    def row_blocks(named):
        return [g.reshape(N_DEV, g.shape[0] // N_DEV, g.shape[1]) for _, g in named]

    def chip_sums(named, g8, got):
        return [add_sibling(g, r, src_idx, chip_idx, "rs_add_" + n) for g, r, (n, _) in zip(g8, got, named)]

    da3, db3, g_w2b = ffn_backward_gate(dy3, a3, b3, ffn2_ws[2], "ffn2_bwd_gate")
    dx2, sums_3 = ffn_backward_norm(da3, db3, dx3, x2, y3, norm_ffn2_g, sc3, ffn2_ws[0], ffn2_ws[1], "ffn2_bwd_norm")
    ffn2_named = [("ffn2_w1", matmul_tn(da3, h3, "ffn2_gw1")), ("ffn2_w3", matmul_tn(db3, h3, "ffn2_gw3")),
                  ("ffn2_w2", g_w2b)]
    ffn2_g8 = row_blocks(ffn2_named)
    dya, do, delta, sums_2d, sums_2o, g_w_out, *ffn2_sib = mix_out_backward(
        dx2, y2, g2, ya, o, yn, out_norm_g, gmat_a, gmat_b, w_out_f, riding_sibling(ffn2_g8))
    ffn2_sums = chip_sums(ffn2_named, ffn2_g8, ffn2_sib)
    nq = t // _tile(t, ATTN_TILE, CHUNK)
    stat_shape = (MLA_HEADS, nq, 1, t // nq)
    dq, dk, dv, *ffn2_got = attention_backward(q, k, v, do, lse.reshape(stat_shape), delta.reshape(stat_shape),
                                               bias.T, riding_exchange([s[1] for s in ffn2_sums]))
    dzc, sums_c, g_w_in_conv = conv_backward(zc, dya, conv_full8, h2)
    dzm, sums_m, g_w_uq_p, g_w_ukv_p, g_w_in_mla = mla_project_backward(
        dq, dk, dv, zm, qn, kvn, h2, pos, inv_freq, q_norm_g, kv_norm_g, w_uq_p, w_ukv_p)
    g_w_in = jnp.concatenate([g_w_in_conv, g_w_in_mla])[:IN_COLS]
    g_w_uq = g_w_uq_p.reshape(MLA_HEADS, HEAD_PAD, Q_LORA)[:, :QK_NOPE + QK_ROPE].reshape(-1, Q_LORA)
    g_w_ukv = _swap_head_parts(g_w_ukv_p, MLA_HEADS, 2)
    mix_named = [("w_in", g_w_in), ("w_uq", g_w_uq), ("w_ukv", g_w_ukv), ("w_out", g_w_out)]
    mix_g8 = row_blocks(mix_named)
    dx1, dy1, sums_1m, *mix_sib = mix_in_backward(dzc, dzm, w_in_p, x1, dx2, norm_mix_g, sc2, g1, riding_sibling(mix_g8))
    mix_sums = chip_sums(mix_named, mix_g8, mix_sib)
    da1, db1, g_w2a, *mix_got = ffn_backward_gate(dy1, a1, b1, ffn1_ws[2], "ffn1_bwd_gate",
                                                  riding_exchange([s[1] for s in mix_sums]))
    ffn1_pair = [("ffn1_w2", g_w2a), ("ffn1_w1", matmul_tn(da1, h1, "ffn1_gw1"))]
    pair_g8 = row_blocks(ffn1_pair)
    g_w3a, *pair_sib = matmul_tn(db1, h1, "ffn1_gw3", riding_sibling(pair_g8))
    ffn1_last = [("ffn1_w3", g_w3a)]
    last_g8 = row_blocks(ffn1_last)
    ffn1_named = ffn1_pair + ffn1_last
    ffn1_sums = chip_sums(ffn1_pair, pair_g8, pair_sib) + chip_sums(
        ffn1_last, last_g8, exchange_sibling(last_g8, "rs_sibling_ffn1_w3"))
    dx0, sums_1, *ffn1_got = ffn_backward_norm(da1, db1, dx1, xs, y1, norm_ffn1_g, sc1, ffn1_ws[0], ffn1_ws[1], "ffn1_bwd_norm",
                                               riding_exchange([s[1] for s in ffn1_sums]))
    reduced = {}
    for named, group_sums, group_got in ((ffn2_named, ffn2_sums, ffn2_got), (mix_named, mix_sums, mix_got),
                                         (ffn1_named, ffn1_sums, ffn1_got)):
        for (n, _), (own, _), got in zip(named, group_sums, group_got):
            reduced[n] = (own, got)

    dmod = jnp.concatenate([sums_1[0], sums_1[1], sums_1[2], sums_1m[0], sums_1m[1], sums_2d[0],
                            sums_3[0], sums_3[1], sums_3[2]])
    pieces = [dmod, sums_1[3], sums_1m[2], sums_m[0, :Q_LORA], sums_m[0, Q_LORA:Q_LORA + KV_LORA], sums_2o[0],
              sums_3[3], sums_f[0], sums_f[1], sums_c[:CONV_K].reshape(-1)]
    plens = [p.shape[0] for p in pieces]
    poffs = [sum(plens[:i]) for i in range(len(plens))]
    vec_len = -(-sum(plens) // 1024) * 1024
    vec = _pad_to(jnp.concatenate(pieces), vec_len).reshape(-1, LANES)
    vec_all, = all_gather([vec], [0], "gather_sums")
    tot = sum_devices(vec_all).reshape(-1)
    g_ada_b, g_n1, g_nmix, g_qg, g_kvg, g_og, g_n3, g_gf, loss_lanes, g_conv_full = [
        tot[o:o + n] for o, n in zip(poffs, plens)]
    loss = sum_lanes(loss_lanes.reshape(1, d))[0, 0]
    g_conv = lax.dynamic_slice_in_dim(g_conv_full.reshape(CONV_K, CONV_WIDTH), me * cw_n, cw_n, axis=1)
    dmod_all = vec_all.reshape(N_DEV, vec_len)[:, :N_MOD * d]
    dmod_cols = lax.dynamic_slice_in_dim(dmod_all, me * n_ada, n_ada, axis=1)
    g_ada_w = ada_backward(jnp.pad(c_all, ((0, 8), (0, 0))), jnp.pad(dmod_cols, ((0, 8), (0, 0))))

    def update(name, w, g, m, v, received=None):
        k, n = w.shape[-2:]
        if g.shape == (k, n):
            flat, back = (lambda a: a.reshape(k, n)), (lambda a: a.reshape(w.shape))
        else:
            flat, back = (lambda a: a.reshape(k, n).T), (lambda a: a.T.reshape(w.shape))
        if received is None:
            out = (g,) + tuple(adamw(flat(w), g, flat(m), flat(v), "adamw_" + name))
        else:
            out = adamw_received(flat(w), g, received, flat(m), flat(v), "adamw_" + name)
        return tuple(back(a) for a in out)

    res = {}
    res["ada_w"] = update("ada_w", ada_w, g_ada_w, m_ada_w, v_ada_w)
    big = [("ffn1_w1", ffn1_w1, m_ffn1_w1, v_ffn1_w1), ("ffn1_w3", ffn1_w3, m_ffn1_w3, v_ffn1_w3),
           ("ffn2_w1", ffn2_w1, m_ffn2_w1, v_ffn2_w1), ("ffn2_w3", ffn2_w3, m_ffn2_w3, v_ffn2_w3),
           ("w_in", w_in, m_w_in, v_w_in), ("w_uq", w_uq, m_w_uq, v_w_uq), ("w_ukv", w_ukv, m_w_ukv, v_w_ukv),
           ("ffn1_w2", ffn1_w2, m_ffn1_w2, v_ffn1_w2), ("ffn2_w2", ffn2_w2, m_ffn2_w2, v_ffn2_w2),
           ("w_out", w_out, m_w_out, v_w_out)]
    for name, w, m, v in big:
        res[name] = update(name, w, reduced[name][0], m, v, reduced[name][1])
    smalls = [("ada_b", ada_b, g_ada_b, m_ada_b, v_ada_b),
              ("norm_ffn1_g", norm_ffn1_g, g_n1, m_norm_ffn1_g, v_norm_ffn1_g),
              ("norm_mix_g", norm_mix_g, g_nmix, m_norm_mix_g, v_norm_mix_g),
              ("conv_w", conv_w, g_conv, m_conv_w, v_conv_w),
              ("q_norm_g", q_norm_g, g_qg, m_q_norm_g, v_q_norm_g),
              ("kv_norm_g", kv_norm_g, g_kvg, m_kv_norm_g, v_kv_norm_g),
              ("out_norm_g", out_norm_g, g_og, m_out_norm_g, v_out_norm_g),
              ("norm_ffn2_g", norm_ffn2_g, g_n3, m_norm_ffn2_g, v_norm_ffn2_g),
              ("final_norm_g", final_norm_g, g_gf, m_final_norm_g, v_final_norm_g)]
    slens = [w.size for _, w, _, _, _ in smalls]
    soffs = [sum(slens[:i]) for i in range(len(slens))]
    s_len = -(-sum(slens) // 1024) * 1024

    def pack_small(i):
        return _pad_to(jnp.concatenate([s[i].reshape(-1) for s in smalls]), s_len).reshape(8, -1)

    s_out = adamw(pack_small(1), pack_small(2), pack_small(3), pack_small(4), "adamw_small")
    for (name, w, g, _, _), o, n in zip(smalls, soffs, slens):
        res[name] = (g.reshape(w.shape),) + tuple(a.reshape(-1)[o:o + n].reshape(w.shape) for a in s_out)

    order = ["ada_w", "ada_b", "norm_ffn1_g", "ffn1_w1", "ffn1_w3", "ffn1_w2", "norm_mix_g", "w_in", "conv_w",
             "q_norm_g", "w_uq", "kv_norm_g", "w_ukv", "out_norm_g", "w_out", "norm_ffn2_g", "ffn2_w1", "ffn2_w3",
             "ffn2_w2", "final_norm_g"]
    return (loss, dx0.reshape(x.shape), *[res[n][0] for n in order], *[res[n][1] for n in order],
            *[res[n][2] for n in order], *[res[n][3] for n in order])
```
